```python
import jax, jax.numpy as jnp
from jax import lax
import numpy as np

D_MODEL = 1024
BATCH = 16
SEQ = 2048
DEPTH = 1

HEAD_DIM = 64
N_HEADS_DIL = 8
N_HEADS_SB = 8
DIL_WIDTH = N_HEADS_DIL * HEAD_DIM
SB_WIDTH = N_HEADS_SB * HEAD_DIM
DIL_PATTERNS = ((128, 1), (512, 4), (2048, 16))
BLOCK = 128
ROPE_THETA = 500000.0
ROPE_DIM = HEAD_DIM // 4
N_MEM = 256
N_HEADS_MEM = 4
MEM_HEAD_DIM = 128
MEM_WIDTH = N_HEADS_MEM * MEM_HEAD_DIM
D_FF = -(-(8 * D_MODEL) // (3 * 256)) * 256
IN_COLS = 3 * DIL_WIDTH + 3 * SB_WIDTH + 2 * D_MODEL
RMS_EPS = 1e-6
MAX_POS_OFFSET = 1024

kernel_name = "hybrid_dilated_stickbreak_gated_block"


def _rmsnorm(x, g):
    xf = x.astype(jnp.float32)
    y = xf * lax.rsqrt(jnp.mean(xf * xf, axis=-1, keepdims=True) + RMS_EPS)
    return (y * g.astype(jnp.float32)).astype(x.dtype)


def _partial_rope(x, positions):
    half = ROPE_DIM // 2
    inv_freq = ROPE_THETA ** (-jnp.arange(half, dtype=jnp.float32) / half)
    ang = positions.astype(jnp.float32)[:, None, :, None] * inv_freq
    cos, sin = jnp.cos(ang), jnp.sin(ang)
    xf = x.astype(jnp.float32)
    x1, x2 = xf[..., :half], xf[..., half:ROPE_DIM]
    out = jnp.concatenate([x1 * cos - x2 * sin, x2 * cos + x1 * sin, xf[..., ROPE_DIM:]], axis=-1)
    return out.astype(x.dtype)


def _dilated_pattern(q, k, v, window, dilation):
    B, H, S, hd = q.shape
    L = S // dilation
    n_back = window // dilation
    C = BLOCK
    nb = -(-L // C)
    Lp = nb * C

    def to_stream(t):
        t = t.reshape(B, H, L, dilation, hd).transpose(0, 1, 3, 2, 4)
        t = jnp.pad(t, ((0, 0), (0, 0), (0, 0), (0, Lp - L), (0, 0)))
        return t.reshape(B, H, dilation, nb, C, hd)

    def with_prev(t):
        prev = jnp.pad(t, ((0, 0), (0, 0), (0, 0), (1, 0), (0, 0), (0, 0)))[:, :, :, :-1]
        return jnp.concatenate([prev, t], axis=4)

    qs = to_stream(q)
    kb = with_prev(to_stream(k))
    vb = with_prev(to_stream(v)).astype(jnp.float32)
    s = jnp.einsum('bhrnqd,bhrnkd->bhrnqk', qs, kb,
                   preferred_element_type=jnp.float32) * (hd ** -0.5)
    i = jnp.arange(C)[:, None]
    j = jnp.arange(2 * C)[None, :]
    dist = C + i - j
    band = (dist >= 0) & (dist <= n_back)
    in_cur = j >= C
    blk = jnp.arange(nb)[:, None, None]
    valid = band[None] & ((blk > 0) | in_cur[None])
    s = jnp.where(valid, s, -jnp.inf)
    m = jnp.max(s, axis=-1, keepdims=True)
    p = jnp.exp(s - m)
    den = jnp.sum(p, axis=-1)
    o = jnp.einsum('bhrnqk,bhrnkd->bhrnqd', p, vb) / den[..., None]
    lse = m[..., 0] + jnp.log(den)

    def from_stream(t):
        rest = t.shape[5:]
        t = t.reshape((B, H, dilation, Lp) + rest)[:, :, :, :L]
        t = jnp.moveaxis(t, 2, 3)
        return t.reshape((B, H, S) + rest)

    return from_stream(o), from_stream(lse)


def _dilated_attention(q, k, v):
    outs, lses = [], []
    for window, dilation in DIL_PATTERNS:
        o, lse = _dilated_pattern(q, k, v, window, dilation)
        outs.append(o)
        lses.append(lse)
    alpha = jax.nn.softmax(jnp.stack(lses, axis=0), axis=0)
    return jnp.einsum('gbhs,gbhsd->bhsd', alpha, jnp.stack(outs, axis=0))


def _stick_breaking(q, k, v):
    B, H, S, hd = q.shape
    nb = S // BLOCK
    qb = q.reshape(B, H, nb, BLOCK, hd).transpose(2, 0, 1, 3, 4)
    vf = v.astype(jnp.float32)
    kpos = jnp.arange(S)

    def block(args):
        qi, bi = args
        z = jnp.einsum('bhqd,bhkd->bhqk', qi, k,
                       preferred_element_type=jnp.float32) * (hd ** -0.5)
        qpos = bi * BLOCK + jnp.arange(BLOCK)
        strict = kpos[None, :] < qpos[:, None]
        log1m = jnp.where(strict, jax.nn.log_sigmoid(-z), 0.0)
        between = lax.cumsum(log1m, axis=3, reverse=True) - log1m
        a = jnp.where(strict, jnp.exp(jax.nn.log_sigmoid(z) + between), 0.0)
        return jnp.einsum('bhqk,bhkd->bhqd', a, vf)

    o = lax.map(block, (qb, jnp.arange(nb)))
    return o.transpose(1, 2, 0, 3, 4).reshape(B, H, S, hd)


def _memory_attention(hn, mem_n, w_q, w_kv, w_o):
    B, S, _ = hn.shape
    q = (hn @ w_q).reshape(B, S, N_HEADS_MEM, MEM_HEAD_DIM)
    kv = mem_n @ w_kv
    k = kv[..., :MEM_WIDTH].reshape(B, N_MEM, N_HEADS_MEM, MEM_HEAD_DIM)
    v = kv[..., MEM_WIDTH:].reshape(B, N_MEM, N_HEADS_MEM, MEM_HEAD_DIM)
    s = jnp.einsum('bshd,bmhd->bhsm', q, k,
                   preferred_element_type=jnp.float32) * (MEM_HEAD_DIM ** -0.5)
    p = jax.nn.softmax(s, axis=-1)
    o = jnp.einsum('bhsm,bmhd->bshd', p, v.astype(jnp.float32))
    return o.reshape(B, S, MEM_WIDTH).astype(hn.dtype) @ w_o


def _swiglu(n, w_gate, w_up, w_down):
    return (jax.nn.silu(n @ w_gate) * (n @ w_up)) @ w_down


def _fwd_setup_inputs(seed: int = 0) -> dict:
    key = jax.random.key(seed)
    ks = jax.random.split(key, 20)

    def w(k, shape):
        return jax.random.normal(k, shape, jnp.float32) * (shape[-2] ** -0.5)

    def gain(k, shape):
        return 1.0 + 0.01 * jax.random.normal(k, shape, jnp.float32)

    x = jax.random.normal(ks[0], (BATCH, SEQ, D_MODEL), jnp.float32)
    mem = jax.random.normal(ks[1], (BATCH, N_MEM, D_MODEL), jnp.float32)
    offs = jax.random.randint(ks[2], (BATCH, 1), 0, MAX_POS_OFFSET, dtype=jnp.int32)
    positions = offs + jnp.arange(SEQ, dtype=jnp.int32)[None, :]
    return {
        "x": x,
        "mem": mem,
        "positions": positions,
        "g_mix": gain(ks[3], (DEPTH, D_MODEL)),
        "w_in": w(ks[4], (DEPTH, D_MODEL, IN_COLS)),
        "w_up_a": w(ks[5], (DEPTH, DIL_WIDTH, D_MODEL)),
        "w_up_b": w(ks[6], (DEPTH, SB_WIDTH, D_MODEL)),
        "w_out": w(ks[7], (DEPTH, D_MODEL, D_MODEL)),
        "g_mem_q": gain(ks[8], (DEPTH, D_MODEL)),
        "g_mem_kv": gain(ks[9], (DEPTH, D_MODEL)),
        "w_q_mem": w(ks[10], (DEPTH, D_MODEL, MEM_WIDTH)),
        "w_kv_mem": w(ks[11], (DEPTH, D_MODEL, 2 * MEM_WIDTH)),
        "w_o_mem": w(ks[12], (DEPTH, MEM_WIDTH, D_MODEL)),
        "g_ffn": gain(ks[13], (DEPTH, D_MODEL)),
        "w_ffn_gate": w(ks[14], (DEPTH, D_MODEL, D_FF)),
        "w_ffn_up": w(ks[15], (DEPTH, D_MODEL, D_FF)),
        "w_ffn_down": w(ks[16], (DEPTH, D_FF, D_MODEL)),
        "g_final": gain(ks[17], (D_MODEL,)),
    }


def _fwd_reference(x, mem, positions, g_mix, w_in, w_up_a, w_up_b, w_out, g_mem_q, g_mem_kv,
              w_q_mem, w_kv_mem, w_o_mem, g_ffn, w_ffn_gate, w_ffn_up, w_ffn_down, g_final):
    B, S, _ = x.shape
    split_at = list(np.cumsum([DIL_WIDTH, DIL_WIDTH, DIL_WIDTH,
                               SB_WIDTH, SB_WIDTH, SB_WIDTH, D_MODEL]))

    def heads(t, nh):
        return t.reshape(B, S, nh, HEAD_DIM).transpose(0, 2, 1, 3)

    def merge(t):
        return t.transpose(0, 2, 1, 3).reshape(B, S, -1).astype(x.dtype)

    h = x
    for l in range(DEPTH):
        n = _rmsnorm(h, g_mix[l])
        proj = n @ w_in[l]
        qa, ka, va, qb, kb, vb, gate_a, gate_b = jnp.split(proj, split_at, axis=-1)
        qa = _partial_rope(heads(qa, N_HEADS_DIL), positions)
        ka = _partial_rope(heads(ka, N_HEADS_DIL), positions)
        o_a = merge(_dilated_attention(qa, ka, heads(va, N_HEADS_DIL)))
        o_b = merge(_stick_breaking(heads(qb, N_HEADS_SB), heads(kb, N_HEADS_SB),
                                    heads(vb, N_HEADS_SB)))
        mixed = (jax.nn.sigmoid(gate_a) * (o_a @ w_up_a[l])
                 + jax.nn.sigmoid(gate_b) * (o_b @ w_up_b[l]))
        h = h + mixed @ w_out[l]
        h = h + _memory_attention(_rmsnorm(h, g_mem_q[l]), _rmsnorm(mem, g_mem_kv[l]),
                                  w_q_mem[l], w_kv_mem[l], w_o_mem[l])
        h = h + _swiglu(_rmsnorm(h, g_ffn[l]), w_ffn_gate[l], w_ffn_up[l], w_ffn_down[l])
    return _rmsnorm(h, g_final)


import jax as _jax
import jax.numpy as _jnp

TWIN_FORMAT = 'train_step'
FWD_PARAMS = ['x', 'mem', 'positions', 'g_mix', 'w_in', 'w_up_a', 'w_up_b', 'w_out', 'g_mem_q', 'g_mem_kv', 'w_q_mem', 'w_kv_mem', 'w_o_mem', 'g_ffn', 'w_ffn_gate', 'w_ffn_up', 'w_ffn_down', 'g_final']
TWIN_WEIGHTS = ['g_mix', 'w_in', 'w_up_a', 'w_up_b', 'w_out', 'g_mem_q', 'g_mem_kv', 'w_q_mem', 'w_kv_mem', 'w_o_mem', 'g_ffn', 'w_ffn_gate', 'w_ffn_up', 'w_ffn_down', 'g_final']
TWIN_DIFF_INPUT = 'x'
TWIN_INPUTS = ['x', 'mem', 'positions', 'g_mix', 'w_in', 'w_up_a', 'w_up_b', 'w_out', 'g_mem_q', 'g_mem_kv', 'w_q_mem', 'w_kv_mem', 'w_o_mem', 'g_ffn', 'w_ffn_gate', 'w_ffn_up', 'w_ffn_down', 'g_final', 'loss_target', 'm_g_mix', 'm_w_in', 'm_w_up_a', 'm_w_up_b', 'm_w_out', 'm_g_mem_q', 'm_g_mem_kv', 'm_w_q_mem', 'm_w_kv_mem', 'm_w_o_mem', 'm_g_ffn', 'm_w_ffn_gate', 'm_w_ffn_up', 'm_w_ffn_down', 'm_g_final', 'v_g_mix', 'v_w_in', 'v_w_up_a', 'v_w_up_b', 'v_w_out', 'v_g_mem_q', 'v_g_mem_kv', 'v_w_q_mem', 'v_w_kv_mem', 'v_w_o_mem', 'v_g_ffn', 'v_w_ffn_gate', 'v_w_ffn_up', 'v_w_ffn_down', 'v_g_final']
TWIN_OUTPUTS = ['loss', 'grad_x', 'grad_g_mix', 'grad_w_in', 'grad_w_up_a', 'grad_w_up_b', 'grad_w_out', 'grad_g_mem_q', 'grad_g_mem_kv', 'grad_w_q_mem', 'grad_w_kv_mem', 'grad_w_o_mem', 'grad_g_ffn', 'grad_w_ffn_gate', 'grad_w_ffn_up', 'grad_w_ffn_down', 'grad_g_final', 'delta_g_mix', 'delta_w_in', 'delta_w_up_a', 'delta_w_up_b', 'delta_w_out', 'delta_g_mem_q', 'delta_g_mem_kv', 'delta_w_q_mem', 'delta_w_kv_mem', 'delta_w_o_mem', 'delta_g_ffn', 'delta_w_ffn_gate', 'delta_w_ffn_up', 'delta_w_ffn_down', 'delta_g_final', 'new_m_g_mix', 'new_m_w_in', 'new_m_w_up_a', 'new_m_w_up_b', 'new_m_w_out', 'new_m_g_mem_q', 'new_m_g_mem_kv', 'new_m_w_q_mem', 'new_m_w_kv_mem', 'new_m_w_o_mem', 'new_m_g_ffn', 'new_m_w_ffn_gate', 'new_m_w_ffn_up', 'new_m_w_ffn_down', 'new_m_g_final', 'new_v_g_mix', 'new_v_w_in', 'new_v_w_up_a', 'new_v_w_up_b', 'new_v_w_out', 'new_v_g_mem_q', 'new_v_g_mem_kv', 'new_v_w_q_mem', 'new_v_w_kv_mem', 'new_v_w_o_mem', 'new_v_g_ffn', 'new_v_w_ffn_gate', 'new_v_w_ffn_up', 'new_v_w_ffn_down', 'new_v_g_final']
TWIN_LEAF_KINDS = {'loss': 'loss', 'grad_x': 'grad_x', 'grad_g_mix': 'grad_w', 'grad_w_in': 'grad_w', 'grad_w_up_a': 'grad_w', 'grad_w_up_b': 'grad_w', 'grad_w_out': 'grad_w', 'grad_g_mem_q': 'grad_w', 'grad_g_mem_kv': 'grad_w', 'grad_w_q_mem': 'grad_w', 'grad_w_kv_mem': 'grad_w', 'grad_w_o_mem': 'grad_w', 'grad_g_ffn': 'grad_w', 'grad_w_ffn_gate': 'grad_w', 'grad_w_ffn_up': 'grad_w', 'grad_w_ffn_down': 'grad_w', 'grad_g_final': 'grad_w', 'delta_g_mix': 'delta_w', 'delta_w_in': 'delta_w', 'delta_w_up_a': 'delta_w', 'delta_w_up_b': 'delta_w', 'delta_w_out': 'delta_w', 'delta_g_mem_q': 'delta_w', 'delta_g_mem_kv': 'delta_w', 'delta_w_q_mem': 'delta_w', 'delta_w_kv_mem': 'delta_w', 'delta_w_o_mem': 'delta_w', 'delta_g_ffn': 'delta_w', 'delta_w_ffn_gate': 'delta_w', 'delta_w_ffn_up': 'delta_w', 'delta_w_ffn_down': 'delta_w', 'delta_g_final': 'delta_w', 'new_m_g_mix': 'new_m', 'new_m_w_in': 'new_m', 'new_m_w_up_a': 'new_m', 'new_m_w_up_b': 'new_m', 'new_m_w_out': 'new_m', 'new_m_g_mem_q': 'new_m', 'new_m_g_mem_kv': 'new_m', 'new_m_w_q_mem': 'new_m', 'new_m_w_kv_mem': 'new_m', 'new_m_w_o_mem': 'new_m', 'new_m_g_ffn': 'new_m', 'new_m_w_ffn_gate': 'new_m', 'new_m_w_ffn_up': 'new_m', 'new_m_w_ffn_down': 'new_m', 'new_m_g_final': 'new_m', 'new_v_g_mix': 'new_v', 'new_v_w_in': 'new_v', 'new_v_w_up_a': 'new_v', 'new_v_w_up_b': 'new_v', 'new_v_w_out': 'new_v', 'new_v_g_mem_q': 'new_v', 'new_v_g_mem_kv': 'new_v', 'new_v_w_q_mem': 'new_v', 'new_v_w_kv_mem': 'new_v', 'new_v_w_o_mem': 'new_v', 'new_v_g_ffn': 'new_v', 'new_v_w_ffn_gate': 'new_v', 'new_v_w_ffn_up': 'new_v', 'new_v_w_ffn_down': 'new_v', 'new_v_g_final': 'new_v'}


def _forward(args):
    return _fwd_reference(*[args[k] for k in FWD_PARAMS])


def _output_shape():
    out = _jax.eval_shape(lambda: _forward(_fwd_setup_inputs(0)))
    return out.shape, out.dtype

N_MICROBATCH = 1
ADAM_LR = 0.001
ADAM_B1 = 0.9
ADAM_B2 = 0.999
ADAM_EPS = 1e-08
ADAM_WD = 0.01
ADAM_STEP = 10
PER_EXAMPLE_BATCH_AXIS = {'x': 0, 'mem': 0, 'positions': 0, 'loss_target': 0}
SHARED_INPUTS = []
_WEIGHT_DTYPES = {'g_mix': _jnp.float32, 'w_in': _jnp.float32, 'w_up_a': _jnp.float32, 'w_up_b': _jnp.float32, 'w_out': _jnp.float32, 'g_mem_q': _jnp.float32, 'g_mem_kv': _jnp.float32, 'w_q_mem': _jnp.float32, 'w_kv_mem': _jnp.float32, 'w_o_mem': _jnp.float32, 'g_ffn': _jnp.float32, 'w_ffn_gate': _jnp.float32, 'w_ffn_up': _jnp.float32, 'w_ffn_down': _jnp.float32, 'g_final': _jnp.float32}
MOMENT_SCALE = {'g_mix': 8.988610e-02, 'w_in': 3.996910e-02, 'w_up_a': 2.311503e-02, 'w_up_b': 6.581687e-02, 'w_out': 6.836285e-02, 'g_mem_q': 1.988819e-02, 'g_mem_kv': 2.869595e-02, 'w_q_mem': 2.737226e-02, 'w_kv_mem': 2.769239e-02, 'w_o_mem': 1.970274e-02, 'g_ffn': 1.214906e-01, 'w_ffn_gate': 5.395738e-02, 'w_ffn_up': 5.217183e-02, 'w_ffn_down': 8.664970e-02, 'g_final': 3.199419e+01}


def _to_microbatches(a, axis):
    t = _jnp.moveaxis(a, axis, 0)
    t = t.reshape((N_MICROBATCH, t.shape[0] // N_MICROBATCH) + t.shape[1:])
    return _jnp.moveaxis(t, 1, axis + 1)


def setup_inputs(seed: int = 0) -> dict:
    inp = _fwd_setup_inputs(seed)
    key = _jax.random.fold_in(_jax.random.key(seed), 7919)
    shape, _ = _output_shape()
    out = dict(inp)
    out["loss_target"] = _jax.random.normal(_jax.random.fold_in(key, 0), shape, _jnp.float32)
    for i, name in enumerate(TWIN_WEIGHTS):
        w = inp[name].astype(_jnp.float32)
        if MOMENT_SCALE is None:
            s = _jnp.sqrt(_jnp.mean(_jnp.square(w)) + 1e-30)
        else:
            s = MOMENT_SCALE[name]
        km, kv = _jax.random.split(_jax.random.fold_in(key, i + 1))
        out[name] = w
        out["m_" + name] = s * _jax.random.normal(km, w.shape, _jnp.float32)
        out["v_" + name] = (s * s) * _jax.random.uniform(kv, w.shape, _jnp.float32, 0.5, 1.5)
    if N_MICROBATCH > 1:
        for name, axis in PER_EXAMPLE_BATCH_AXIS.items():
            out[name] = _to_microbatches(out[name], axis)
    return {'x': out['x'], 'mem': out['mem'], 'positions': out['positions'], 'g_mix': out['g_mix'], 'w_in': out['w_in'], 'w_up_a': out['w_up_a'], 'w_up_b': out['w_up_b'], 'w_out': out['w_out'], 'g_mem_q': out['g_mem_q'], 'g_mem_kv': out['g_mem_kv'], 'w_q_mem': out['w_q_mem'], 'w_kv_mem': out['w_kv_mem'], 'w_o_mem': out['w_o_mem'], 'g_ffn': out['g_ffn'], 'w_ffn_gate': out['w_ffn_gate'], 'w_ffn_up': out['w_ffn_up'], 'w_ffn_down': out['w_ffn_down'], 'g_final': out['g_final'], 'loss_target': out['loss_target'], 'm_g_mix': out['m_g_mix'], 'm_w_in': out['m_w_in'], 'm_w_up_a': out['m_w_up_a'], 'm_w_up_b': out['m_w_up_b'], 'm_w_out': out['m_w_out'], 'm_g_mem_q': out['m_g_mem_q'], 'm_g_mem_kv': out['m_g_mem_kv'], 'm_w_q_mem': out['m_w_q_mem'], 'm_w_kv_mem': out['m_w_kv_mem'], 'm_w_o_mem': out['m_w_o_mem'], 'm_g_ffn': out['m_g_ffn'], 'm_w_ffn_gate': out['m_w_ffn_gate'], 'm_w_ffn_up': out['m_w_ffn_up'], 'm_w_ffn_down': out['m_w_ffn_down'], 'm_g_final': out['m_g_final'], 'v_g_mix': out['v_g_mix'], 'v_w_in': out['v_w_in'], 'v_w_up_a': out['v_w_up_a'], 'v_w_up_b': out['v_w_up_b'], 'v_w_out': out['v_w_out'], 'v_g_mem_q': out['v_g_mem_q'], 'v_g_mem_kv': out['v_g_mem_kv'], 'v_w_q_mem': out['v_w_q_mem'], 'v_w_kv_mem': out['v_w_kv_mem'], 'v_w_o_mem': out['v_w_o_mem'], 'v_g_ffn': out['v_g_ffn'], 'v_w_ffn_gate': out['v_w_ffn_gate'], 'v_w_ffn_up': out['v_w_ffn_up'], 'v_w_ffn_down': out['v_w_ffn_down'], 'v_g_final': out['v_g_final']}


def _loss(weights, diff, rest, loss_target):
    with _jax.named_scope("forward"):
        args = {**rest, TWIN_DIFF_INPUT: diff, **{k: w.astype(_WEIGHT_DTYPES[k]) for k, w in weights.items()}}
        y = _forward(args)
    with _jax.named_scope("loss_head"):
        err = _jnp.square(y.astype(_jnp.float32) - loss_target)
        return 0.5 * _jnp.sum(_jnp.mean(err, axis=-1)) if err.ndim else 0.5 * err


def _adamw(w, g, m, v):
    m = ADAM_B1 * m + (1.0 - ADAM_B1) * g
    v = ADAM_B2 * v + (1.0 - ADAM_B2) * _jnp.square(g)
    m_hat = m / (1.0 - ADAM_B1 ** ADAM_STEP)
    v_hat = v / (1.0 - ADAM_B2 ** ADAM_STEP)
    delta = -ADAM_LR * (m_hat / (_jnp.sqrt(v_hat) + ADAM_EPS) + ADAM_WD * w)
    return delta, m, v


def reference(x, mem, positions, g_mix, w_in, w_up_a, w_up_b, w_out, g_mem_q, g_mem_kv, w_q_mem, w_kv_mem, w_o_mem, g_ffn, w_ffn_gate, w_ffn_up, w_ffn_down, g_final, loss_target, m_g_mix, m_w_in, m_w_up_a, m_w_up_b, m_w_out, m_g_mem_q, m_g_mem_kv, m_w_q_mem, m_w_kv_mem, m_w_o_mem, m_g_ffn, m_w_ffn_gate, m_w_ffn_up, m_w_ffn_down, m_g_final, v_g_mix, v_w_in, v_w_up_a, v_w_up_b, v_w_out, v_g_mem_q, v_g_mem_kv, v_w_q_mem, v_w_kv_mem, v_w_o_mem, v_g_ffn, v_w_ffn_gate, v_w_ffn_up, v_w_ffn_down, v_g_final):
    given = dict(x=x, mem=mem, positions=positions, g_mix=g_mix, w_in=w_in, w_up_a=w_up_a, w_up_b=w_up_b, w_out=w_out, g_mem_q=g_mem_q, g_mem_kv=g_mem_kv, w_q_mem=w_q_mem, w_kv_mem=w_kv_mem, w_o_mem=w_o_mem, g_ffn=g_ffn, w_ffn_gate=w_ffn_gate, w_ffn_up=w_ffn_up, w_ffn_down=w_ffn_down, g_final=g_final, loss_target=loss_target, m_g_mix=m_g_mix, m_w_in=m_w_in, m_w_up_a=m_w_up_a, m_w_up_b=m_w_up_b, m_w_out=m_w_out, m_g_mem_q=m_g_mem_q, m_g_mem_kv=m_g_mem_kv, m_w_q_mem=m_w_q_mem, m_w_kv_mem=m_w_kv_mem, m_w_o_mem=m_w_o_mem, m_g_ffn=m_g_ffn, m_w_ffn_gate=m_w_ffn_gate, m_w_ffn_up=m_w_ffn_up, m_w_ffn_down=m_w_ffn_down, m_g_final=m_g_final, v_g_mix=v_g_mix, v_w_in=v_w_in, v_w_up_a=v_w_up_a, v_w_up_b=v_w_up_b, v_w_out=v_w_out, v_g_mem_q=v_g_mem_q, v_g_mem_kv=v_g_mem_kv, v_w_q_mem=v_w_q_mem, v_w_kv_mem=v_w_kv_mem, v_w_o_mem=v_w_o_mem, v_g_ffn=v_g_ffn, v_w_ffn_gate=v_w_ffn_gate, v_w_ffn_up=v_w_ffn_up, v_w_ffn_down=v_w_ffn_down, v_g_final=v_g_final)
    weights = {n: given[n] for n in TWIN_WEIGHTS}
    shared = {n: given[n] for n in SHARED_INPUTS}
    per_example = {n: given[n] for n in ['x', 'mem', 'positions']}
    grad_fn = _jax.value_and_grad(_loss, argnums=(0, 1))

    def one_microbatch(ex, loss_target):
        ex = dict(ex)
        diff = ex.pop(TWIN_DIFF_INPUT)
        return grad_fn(weights, diff, {**shared, **ex}, loss_target)

    if N_MICROBATCH == 1:
        loss, (grad_w, grad_x) = one_microbatch(per_example, given["loss_target"])
    else:
        def body(carry, xs):
            loss_sum, grad_sum = carry
            l_k, (gw_k, gx_k) = one_microbatch(xs[0], xs[1])
            with _jax.named_scope("update"):
                return (loss_sum + l_k, _jax.tree.map(_jnp.add, grad_sum, gw_k)), gx_k

        init = (_jnp.zeros((), _jnp.float32), _jax.tree.map(_jnp.zeros_like, weights))
        (loss, grad_w), grad_x = _jax.lax.scan(body, init, (per_example, given["loss_target"]))
    with _jax.named_scope("update"):
        delta_w, new_m, new_v = {}, {}, {}
        for n in TWIN_WEIGHTS:
            delta_w[n], new_m[n], new_v[n] = _adamw(weights[n], grad_w[n], given["m_" + n], given["v_" + n])
    return (loss, grad_x, *[grad_w[n] for n in TWIN_WEIGHTS], *[delta_w[n] for n in TWIN_WEIGHTS],
            *[new_m[n] for n in TWIN_WEIGHTS], *[new_v[n] for n in TWIN_WEIGHTS])
```

```python
import functools

import jax
import jax.numpy as jnp
from jax import lax
from jax.experimental import pallas as pl
from jax.experimental.pallas import tpu as pltpu

F32 = jnp.float32
BF16 = jnp.bfloat16
MESH = pl.DeviceIdType.MESH

D_MODEL = 1024
HEAD_DIM = 64
N_HEADS = 8
ATT_WIDTH = N_HEADS * HEAD_DIM
DIL_PATTERNS = ((128, 1), (512, 4), (2048, 16))
BLOCK = 128
ROPE_THETA = 500000.0
ROPE_DIM = HEAD_DIM // 4
N_HEADS_MEM = 4
MEM_HEAD_DIM = 128
MEM_WIDTH = N_HEADS_MEM * MEM_HEAD_DIM
D_FF = 2816
IN_COLS = 6 * ATT_WIDTH + 2 * D_MODEL
RMS_EPS = 1e-6
ADAM_LR = 0.001
ADAM_B1 = 0.9
ADAM_B2 = 0.999
ADAM_EPS = 1e-08
ADAM_WD = 0.01
ADAM_STEP = 10

N_CHIPS = 4
LANES = 128
FLAT_COLS = 1024
VMEM_LIMIT = 56 * 1024 * 1024


def _tile(dim, cap, unit=LANES):
    if dim <= cap:
        return dim
    best = None
    for t in range(unit, cap + 1, unit):
        if dim % t == 0:
            best = t
    assert best is not None, (dim, cap)
    return best


def _params(sem):
    return pltpu.CompilerParams(dimension_semantics=sem, vmem_limit_bytes=VMEM_LIMIT)


def _mm(a, b, *, name, ta=False, tb=False, add=None, out_dtype=F32,
        tm_cap=512, tn_cap=512, tk_cap=1024):
    if ta:
        k_dim, m_dim = a.shape
    else:
        m_dim, k_dim = a.shape
    if tb:
        n_dim, kb = b.shape
    else:
        kb, n_dim = b.shape
    assert kb == k_dim, (a.shape, b.shape, ta, tb)
    tm, tn, tk = _tile(m_dim, tm_cap), _tile(n_dim, tn_cap), _tile(k_dim, tk_cap)
    nk = k_dim // tk
    dims = (((0 if ta else 1,), (1 if tb else 0,)), ((), ()))
    has_add = add is not None

    def body(*refs):
        if has_add:
            a_ref, b_ref, add_ref, o_ref, acc_ref = refs
        else:
            a_ref, b_ref, o_ref, acc_ref = refs
        k = pl.program_id(2)

        @pl.when(k == 0)
        def _():
            acc_ref[...] = jnp.zeros_like(acc_ref)

        acc_ref[...] += lax.dot_general(a_ref[...].astype(BF16), b_ref[...].astype(BF16), dims,
                                        preferred_element_type=F32)

        @pl.when(k == nk - 1)
        def _():
            r = acc_ref[...]
            if has_add:
                r = add_ref[...] + r
            o_ref[...] = r.astype(out_dtype)

    a_spec = pl.BlockSpec((tk, tm), lambda i, j, k: (k, i)) if ta else pl.BlockSpec((tm, tk), lambda i, j, k: (i, k))
    b_spec = pl.BlockSpec((tn, tk), lambda i, j, k: (j, k)) if tb else pl.BlockSpec((tk, tn), lambda i, j, k: (k, j))
    o_spec = pl.BlockSpec((tm, tn), lambda i, j, k: (i, j))
    in_specs = [a_spec, b_spec] + ([o_spec] if has_add else [])
    args = (a, b) + ((add,) if has_add else ())
    return pl.pallas_call(
        body, name=name, grid=(m_dim // tm, n_dim // tn, nk),
        in_specs=in_specs, out_specs=o_spec,
        out_shape=jax.ShapeDtypeStruct((m_dim, n_dim), out_dtype),
        scratch_shapes=[pltpu.VMEM((tm, tn), F32)],
        compiler_params=_params(("parallel", "parallel", "arbitrary")),
    )(*args)


def _rms_fwd(x, g, *, name, tt=512):
    t_dim, d = x.shape
    tt = _tile(t_dim, tt, 8)

    def body(x_ref, g_ref, o_ref):
        xv = x_ref[...]
        r = lax.rsqrt(jnp.mean(xv * xv, axis=-1, keepdims=True) + RMS_EPS)
        o_ref[...] = ((xv * r) * g_ref[...]).astype(o_ref.dtype)

    return pl.pallas_call(
        body, name=name, grid=(t_dim // tt,),
        in_specs=[pl.BlockSpec((tt, d), lambda i: (i, 0)), pl.BlockSpec((1, d), lambda i: (0, 0))],
        out_specs=pl.BlockSpec((tt, d), lambda i: (i, 0)),
        out_shape=jax.ShapeDtypeStruct((t_dim, d), BF16),
        compiler_params=_params(("parallel",)),
    )(x, g)


def _rms_bwd(x, g, dy, add, *, name, tt=512):
    t_dim, d = x.shape
    tt = _tile(t_dim, tt, 8)
    has_add = add is not None

    def body(*refs):
        if has_add:
            x_ref, g_ref, dy_ref, add_ref, dx_ref, dg_ref = refs
        else:
            x_ref, g_ref, dy_ref, dx_ref, dg_ref = refs
        xv = x_ref[...]
        dyv = dy_ref[...].astype(F32)
        r = lax.rsqrt(jnp.mean(xv * xv, axis=-1, keepdims=True) + RMS_EPS)
        xh = xv * r
        u = dyv * g_ref[...]
        dx = r * (u - xh * jnp.mean(u * xh, axis=-1, keepdims=True))
        if has_add:
            dx = add_ref[...] + dx
        dx_ref[...] = dx

        @pl.when(pl.program_id(0) == 0)
        def _():
            dg_ref[...] = jnp.zeros_like(dg_ref)

        dg_ref[...] += jnp.sum(dyv * xh, axis=0, keepdims=True)

    row = pl.BlockSpec((tt, d), lambda i: (i, 0))
    vec = pl.BlockSpec((1, d), lambda i: (0, 0))
    in_specs = [row, vec, row] + ([row] if has_add else [])
    args = (x, g, dy) + ((add,) if has_add else ())
    return pl.pallas_call(
        body, name=name, grid=(t_dim // tt,),
        in_specs=in_specs, out_specs=[row, vec],
        out_shape=[jax.ShapeDtypeStruct((t_dim, d), F32), jax.ShapeDtypeStruct((1, d), F32)],
        compiler_params=_params(("arbitrary",)),
    )(*args)


def _final(h, g, target, *, tt=512):
    t_dim, d = h.shape
    n_steps = t_dim // tt

    def body(h_ref, g_ref, t_ref, loss_ref, dh_ref, dg_ref, sq_ref):
        i = pl.program_id(0)
        xv = h_ref[...]
        gv = g_ref[...]
        r = lax.rsqrt(jnp.mean(xv * xv, axis=-1, keepdims=True) + RMS_EPS)
        xh = xv * r
        err = xh * gv - t_ref[...]
        dyv = err * (1.0 / d)
        u = dyv * gv
        dh_ref[...] = r * (u - xh * jnp.mean(u * xh, axis=-1, keepdims=True))

        @pl.when(i == 0)
        def _():
            dg_ref[...] = jnp.zeros_like(dg_ref)
            sq_ref[...] = jnp.zeros_like(sq_ref)

        dg_ref[...] += jnp.sum(dyv * xh, axis=0, keepdims=True)
        sq_ref[...] += jnp.sum(err * err, axis=0, keepdims=True)

        @pl.when(i == n_steps - 1)
        def _():
            total = jnp.sum(sq_ref[...], axis=-1, keepdims=True) * (0.5 / d)
            loss_ref[...] = jnp.broadcast_to(total, loss_ref.shape)

    row = pl.BlockSpec((tt, d), lambda i: (i, 0))
    vec = pl.BlockSpec((1, d), lambda i: (0, 0))
    return pl.pallas_call(
        body, name="final_loss", grid=(n_steps,),
        in_specs=[row, vec, row],
        out_specs=[pl.BlockSpec((1, LANES), lambda i: (0, 0)), row, vec],
        out_shape=[jax.ShapeDtypeStruct((1, LANES), F32), jax.ShapeDtypeStruct((t_dim, d), F32),
                   jax.ShapeDtypeStruct((1, d), F32)],
        scratch_shapes=[pltpu.VMEM((1, d), F32)],
        compiler_params=_params(("arbitrary",)),
    )(h, g, target)


def _rope(src, pos, inv_lane, sel_a, sel_b, *, name, sign, col_block, tt=512):
    t_dim = src.shape[0]
    w = 2 * ATT_WIDTH
    half = ROPE_DIM // 2

    def body(s_ref, p_ref, f_ref, a_ref, b_ref, o_ref):
        xv = s_ref[...].astype(F32)
        ang = p_ref[...] * f_ref[...]
        cs = jnp.cos(ang)
        sn = jnp.sin(ang) * sign
        sa = a_ref[...]
        sb = b_ref[...]
        up = pltpu.roll(xv, w - half, 1)
        dn = pltpu.roll(xv, half, 1)
        rot = xv * cs + (dn * sb - up * sa) * sn
        o_ref[...] = jnp.where((sa + sb) > 0.0, rot, xv)

    vec = pl.BlockSpec((1, w), lambda i: (0, 0))
    return pl.pallas_call(
        body, name=name, grid=(t_dim // tt,),
        in_specs=[pl.BlockSpec((tt, w), lambda i: (i, col_block)), pl.BlockSpec((tt, 1), lambda i: (i, 0)),
                  vec, vec, vec],
        out_specs=pl.BlockSpec((tt, w), lambda i: (i, 0)),
        out_shape=jax.ShapeDtypeStruct((t_dim, w), F32),
        compiler_params=_params(("parallel",)),
    )(src, pos, inv_lane, sel_a, sel_b)


def _tri_masks():
    r = lax.broadcasted_iota(jnp.int32, (BLOCK, BLOCK), 0)
    c = lax.broadcasted_iota(jnp.int32, (BLOCK, BLOCK), 1)
    return c <= r, r <= c


def _band_fwd(q, k, v, *, name):
    g_dim, d, l_dim, hd = q.shape
    nb = l_dim // BLOCK

    def body(q_ref, k_ref, v_ref, o_ref, l_ref):
        cur_ok, prev_ok = _tri_masks()

        def one(idx, _):
            s = idx // nb
            n = idx % nb
            rows = pl.ds(pl.multiple_of(n * BLOCK, BLOCK), BLOCK)
            prow = pl.ds(pl.multiple_of(jnp.maximum(n - 1, 0) * BLOCK, BLOCK), BLOCK)
            qv = (q_ref[0, s, rows, :] * (HEAD_DIM ** -0.5)).astype(BF16)
            kc = k_ref[0, s, rows, :].astype(BF16)
            vc = v_ref[0, s, rows, :].astype(BF16)
            sc = lax.dot_general(qv, kc, (((1,), (1,)), ((), ())), preferred_element_type=F32)
            sc = jnp.where(cur_ok, sc, -jnp.inf)
            if nb > 1:
                kp = k_ref[0, s, prow, :].astype(BF16)
                vp = v_ref[0, s, prow, :].astype(BF16)
                sp = lax.dot_general(qv, kp, (((1,), (1,)), ((), ())), preferred_element_type=F32)
                sp = jnp.where(prev_ok & (n > 0), sp, -jnp.inf)
                m = jnp.maximum(jnp.max(sc, axis=-1, keepdims=True), jnp.max(sp, axis=-1, keepdims=True))
                pc = jnp.exp(sc - m)
                pp = jnp.exp(sp - m)
                den = jnp.sum(pc, axis=-1, keepdims=True) + jnp.sum(pp, axis=-1, keepdims=True)
                acc = (jnp.dot(pc.astype(BF16), vc, preferred_element_type=F32)
                       + jnp.dot(pp.astype(BF16), vp, preferred_element_type=F32))
            else:
                m = jnp.max(sc, axis=-1, keepdims=True)
                pc = jnp.exp(sc - m)
                den = jnp.sum(pc, axis=-1, keepdims=True)
                acc = jnp.dot(pc.astype(BF16), vc, preferred_element_type=F32)
            o_ref[0, s, rows, :] = acc / den
            l_ref[0, s, rows, :] = m + jnp.log(den)
            return 0

        lax.fori_loop(0, d * nb, one, 0)

    big = pl.BlockSpec((1, d, l_dim, hd), lambda i: (i, 0, 0, 0))
    col = pl.BlockSpec((1, d, l_dim, 1), lambda i: (i, 0, 0, 0))
    return pl.pallas_call(
        body, name=name, grid=(g_dim,),
        in_specs=[big, big, big], out_specs=[big, col],
        out_shape=[jax.ShapeDtypeStruct(q.shape, F32), jax.ShapeDtypeStruct((g_dim, d, l_dim, 1), F32)],
        compiler_params=_params(("parallel",)),
    )(q, k, v)


def _band_bwd(q, k, v, do, lse, delta, *, name):
    g_dim, d, l_dim, hd = q.shape
    nb = l_dim // BLOCK
    scale = HEAD_DIM ** -0.5
    nt = (((1,), (1,)), ((), ()))
    tn = (((0,), (0,)), ((), ()))

    def body(q_ref, k_ref, v_ref, do_ref, l_ref, dl_ref, dq_ref, dk_ref, dv_ref):
        cur_ok, prev_ok = _tri_masks()
        dk_ref[...] = jnp.zeros_like(dk_ref)
        dv_ref[...] = jnp.zeros_like(dv_ref)

        def one(idx, _):
            s = idx // nb
            n = idx % nb
            rows = pl.ds(pl.multiple_of(n * BLOCK, BLOCK), BLOCK)
            prow = pl.ds(pl.multiple_of(jnp.maximum(n - 1, 0) * BLOCK, BLOCK), BLOCK)
            qf = q_ref[0, s, rows, :]
            qv = (qf * scale).astype(BF16)
            qb = qf.astype(BF16)
            dov = do_ref[0, s, rows, :].astype(BF16)
            lv = l_ref[0, s, rows, :]
            dlv = dl_ref[0, s, rows, :]

            def side(krows, ok):
                kv = k_ref[0, s, krows, :].astype(BF16)
                vv = v_ref[0, s, krows, :].astype(BF16)
                sc = lax.dot_general(qv, kv, nt, preferred_element_type=F32)
                p = jnp.where(ok, jnp.exp(sc - lv), 0.0)
                dp = lax.dot_general(dov, vv, nt, preferred_element_type=F32)
                ds = (p * (dp - dlv)).astype(BF16)
                dv_ref[0, s, krows, :] += lax.dot_general(p.astype(BF16), dov, tn, preferred_element_type=F32)
                dk_ref[0, s, krows, :] += lax.dot_general(ds, qb, tn, preferred_element_type=F32) * scale
                return jnp.dot(ds, kv, preferred_element_type=F32)

            dq = side(rows, cur_ok)
            if nb > 1:
                dq = dq + side(prow, prev_ok & (n > 0))
            dq_ref[0, s, rows, :] = dq * scale
            return 0

        lax.fori_loop(0, d * nb, one, 0)

    big = pl.BlockSpec((1, d, l_dim, hd), lambda i: (i, 0, 0, 0))
    col = pl.BlockSpec((1, d, l_dim, 1), lambda i: (i, 0, 0, 0))
    shp = jax.ShapeDtypeStruct(q.shape, F32)
    return pl.pallas_call(
        body, name=name, grid=(g_dim,),
        in_specs=[big, big, big, big, col, col], out_specs=[big, big, big],
        out_shape=[shp, shp, shp],
        compiler_params=_params(("parallel",)),
    )(q, k, v, do, lse, delta)


def _combine(o_list, l_list, *, tt=512):
    g_dim, s_dim, hd = o_list[0].shape

    def body(o1, o2, o3, l1, l2, l3, o_ref, l_ref):
        a, b, c = l1[...], l2[...], l3[...]
        m = jnp.maximum(jnp.maximum(a, b), c)
        ea, eb, ec = jnp.exp(a - m), jnp.exp(b - m), jnp.exp(c - m)
        den = ea + eb + ec
        o_ref[...] = (ea * o1[...] + eb * o2[...] + ec * o3[...]) / den
        l_ref[...] = m + jnp.log(den)

    big = pl.BlockSpec((1, tt, hd), lambda i, j: (i, j, 0))
    col = pl.BlockSpec((1, tt, 1), lambda i, j: (i, j, 0))
    return pl.pallas_call(
        body, name="dil_combine", grid=(g_dim, s_dim // tt),
        in_specs=[big, big, big, col, col, col], out_specs=[big, col],
        out_shape=[jax.ShapeDtypeStruct((g_dim, s_dim, hd), F32), jax.ShapeDtypeStruct((g_dim, s_dim, 1), F32)],
        compiler_params=_params(("parallel", "parallel")),
    )(*o_list, *l_list)


def _rowdot(a, b, *, tt=512):
    g_dim, s_dim, hd = a.shape

    def body(a_ref, b_ref, o_ref):
        o_ref[...] = jnp.sum(a_ref[...] * b_ref[...], axis=-1, keepdims=True)

    big = pl.BlockSpec((1, tt, hd), lambda i, j: (i, j, 0))
    return pl.pallas_call(
        body, name="dil_delta", grid=(g_dim, s_dim // tt),
        in_specs=[big, big], out_specs=pl.BlockSpec((1, tt, 1), lambda i, j: (i, j, 0)),
        out_shape=jax.ShapeDtypeStruct((g_dim, s_dim, 1), F32),
        compiler_params=_params(("parallel", "parallel")),
    )(a, b)


def _split_dot(x, tri):
    hi = x.astype(BF16)
    lo = (x - hi.astype(F32)).astype(BF16)
    return jnp.dot(hi, tri, preferred_element_type=F32) + jnp.dot(lo, tri, preferred_element_type=F32)


def _log_sigmoid(z):
    return jnp.minimum(z, 0.0) - jnp.log(1.0 + jnp.exp(-jnp.abs(z)))


def _sb_tile(qv, kv, on_diag):
    r = lax.broadcasted_iota(jnp.int32, (BLOCK, BLOCK), 0)
    c = lax.broadcasted_iota(jnp.int32, (BLOCK, BLOCK), 1)
    valid = jnp.logical_or(jnp.logical_not(on_diag), c < r)
    z = lax.dot_general(qv, kv, (((1,), (1,)), ((), ())), preferred_element_type=F32)
    ls = _log_sigmoid(z)
    l1m = jnp.where(valid, ls - z, 0.0)
    return ls, l1m, valid


def _sb_fwd(q, k, v):
    g_dim, s_dim, hd = q.shape
    nq = s_dim // BLOCK

    def body(q_ref, k_ref, v_ref, o_ref):
        r = lax.broadcasted_iota(jnp.int32, (BLOCK, BLOCK), 0)
        c = lax.broadcasted_iota(jnp.int32, (BLOCK, BLOCK), 1)
        after = (r > c).astype(BF16)

        def qloop(qi, _):
            rows = pl.ds(pl.multiple_of(qi * BLOCK, BLOCK), BLOCK)
            qv = (q_ref[0, rows, :] * (HEAD_DIM ** -0.5)).astype(BF16)

            def kloop(i, carry):
                acc, run = carry
                kb = qi - i
                krows = pl.ds(pl.multiple_of(kb * BLOCK, BLOCK), BLOCK)
                kv = k_ref[0, krows, :].astype(BF16)
                vv = v_ref[0, krows, :].astype(BF16)
                ls, l1m, valid = _sb_tile(qv, kv, i == 0)
                between = _split_dot(l1m, after) + run
                a = jnp.where(valid, jnp.exp(ls + between), 0.0)
                acc = acc + jnp.dot(a.astype(BF16), vv, preferred_element_type=F32)
                run = run + jnp.sum(l1m, axis=-1, keepdims=True)
                return acc, run

            acc, _ = lax.fori_loop(0, qi + 1, kloop,
                                   (jnp.zeros((BLOCK, hd), F32), jnp.zeros((BLOCK, 1), F32)))
            o_ref[0, rows, :] = acc
            return 0

        lax.fori_loop(0, nq, qloop, 0)

    big = pl.BlockSpec((1, s_dim, hd), lambda i: (i, 0, 0))
    return pl.pallas_call(
        body, name="sb_fwd", grid=(g_dim,),
        in_specs=[big, big, big], out_specs=big,
        out_shape=jax.ShapeDtypeStruct(q.shape, F32),
        compiler_params=_params(("parallel",)),
    )(q, k, v)


def _sb_bwd(q, k, v, do):
    g_dim, s_dim, hd = q.shape
    nq = s_dim // BLOCK
    scale = HEAD_DIM ** -0.5
    nt = (((1,), (1,)), ((), ()))
    tn = (((0,), (0,)), ((), ()))

    def body(q_ref, k_ref, v_ref, do_ref, dq_ref, dk_ref, dv_ref, e_ref, sg_ref):
        r = lax.broadcasted_iota(jnp.int32, (BLOCK, BLOCK), 0)
        c = lax.broadcasted_iota(jnp.int32, (BLOCK, BLOCK), 1)
        after = (r > c).astype(BF16)
        before = (r < c).astype(BF16)
        dk_ref[...] = jnp.zeros_like(dk_ref)
        dv_ref[...] = jnp.zeros_like(dv_ref)

        def qloop(qi, _):
            rows = pl.ds(pl.multiple_of(qi * BLOCK, BLOCK), BLOCK)
            qf = q_ref[0, rows, :]
            qv = (qf * scale).astype(BF16)
            qb = qf.astype(BF16)
            dov = do_ref[0, rows, :].astype(BF16)

            def pass1(i, run):
                kb = qi - i
                krows = pl.ds(pl.multiple_of(kb * BLOCK, BLOCK), BLOCK)
                kv = k_ref[0, krows, :].astype(BF16)
                vv = v_ref[0, krows, :].astype(BF16)
                ls, l1m, valid = _sb_tile(qv, kv, i == 0)
                between = _split_dot(l1m, after) + run
                a = jnp.where(valid, jnp.exp(ls + between), 0.0)
                da = lax.dot_general(dov, vv, nt, preferred_element_type=F32)
                dv_ref[0, krows, :] += lax.dot_general(a.astype(BF16), dov, tn, preferred_element_type=F32)
                e_ref[kb] = a * da
                sg_ref[kb] = jnp.exp(ls)
                return run + jnp.sum(l1m, axis=-1, keepdims=True)

            lax.fori_loop(0, qi + 1, pass1, jnp.zeros((BLOCK, 1), F32))

            def pass2(kb, carry):
                dq, run = carry
                krows = pl.ds(pl.multiple_of(kb * BLOCK, BLOCK), BLOCK)
                kv = k_ref[0, krows, :].astype(BF16)
                ev = e_ref[kb]
                sg = sg_ref[kb]
                prefix = _split_dot(ev, before) + run
                valid = jnp.logical_or(kb < qi, c < r)
                dz = jnp.where(valid, ev * (1.0 - sg) - prefix * sg, 0.0).astype(BF16)
                dk_ref[0, krows, :] += lax.dot_general(dz, qb, tn, preferred_element_type=F32) * scale
                dq = dq + jnp.dot(dz, kv, preferred_element_type=F32)
                return dq, run + jnp.sum(ev, axis=-1, keepdims=True)

            dq, _ = lax.fori_loop(0, qi + 1, pass2,
                                  (jnp.zeros((BLOCK, hd), F32), jnp.zeros((BLOCK, 1), F32)))
            dq_ref[0, rows, :] = dq * scale
            return 0

        lax.fori_loop(0, nq, qloop, 0)

    big = pl.BlockSpec((1, s_dim, hd), lambda i: (i, 0, 0))
    shp = jax.ShapeDtypeStruct(q.shape, F32)
    return pl.pallas_call(
        body, name="sb_bwd", grid=(g_dim,),
        in_specs=[big, big, big, big], out_specs=[big, big, big],
        out_shape=[shp, shp, shp],
        scratch_shapes=[pltpu.VMEM((nq, BLOCK, BLOCK), F32), pltpu.VMEM((nq, BLOCK, BLOCK), F32)],
        compiler_params=_params(("parallel",)),
    )(q, k, v, do)


def _sigmoid(x):
    return 1.0 / (1.0 + jnp.exp(-x))


def _gate_fwd(proj, ua, ub, *, tt=512):
    t_dim, d = ua.shape

    def body(ga_ref, gb_ref, ua_ref, ub_ref, o_ref):
        o_ref[...] = (_sigmoid(ga_ref[...]) * ua_ref[...] + _sigmoid(gb_ref[...]) * ub_ref[...]).astype(BF16)

    row = pl.BlockSpec((tt, d), lambda i: (i, 0))
    return pl.pallas_call(
        body, name="gate_fwd", grid=(t_dim // tt,),
        in_specs=[pl.BlockSpec((tt, d), lambda i: (i, 3)), pl.BlockSpec((tt, d), lambda i: (i, 4)), row, row],
        out_specs=row, out_shape=jax.ShapeDtypeStruct((t_dim, d), BF16),
        compiler_params=_params(("parallel",)),
    )(proj, proj, ua, ub)


def _gate_bwd(proj, ua, ub, dmix, *, tt=512):
    t_dim, d = ua.shape

    def body(ga_ref, gb_ref, ua_ref, ub_ref, dm_ref, dua_ref, dub_ref, dg_ref):
        dm = dm_ref[...]
        sa = _sigmoid(ga_ref[...])
        sb = _sigmoid(gb_ref[...])
        dua_ref[...] = (dm * sa).astype(BF16)
        dub_ref[...] = (dm * sb).astype(BF16)
        dg_ref[:, :d] = (dm * ua_ref[...] * (sa * (1.0 - sa))).astype(BF16)
        dg_ref[:, d:] = (dm * ub_ref[...] * (sb * (1.0 - sb))).astype(BF16)

    row = pl.BlockSpec((tt, d), lambda i: (i, 0))
    wide = pl.BlockSpec((tt, 2 * d), lambda i: (i, 0))
    return pl.pallas_call(
        body, name="gate_bwd", grid=(t_dim // tt,),
        in_specs=[pl.BlockSpec((tt, d), lambda i: (i, 3)), pl.BlockSpec((tt, d), lambda i: (i, 4)), row, row, row],
        out_specs=[row, row, wide],
        out_shape=[jax.ShapeDtypeStruct((t_dim, d), BF16), jax.ShapeDtypeStruct((t_dim, d), BF16),
                   jax.ShapeDtypeStruct((t_dim, 2 * d), BF16)],
        compiler_params=_params(("parallel",)),
    )(proj, proj, ua, ub, dmix)


def _swiglu_fwd(gu, *, tt=256):
    t_dim = gu.shape[0]

    def body(g_ref, u_ref, o_ref):
        gv = g_ref[...]
        o_ref[...] = (gv * _sigmoid(gv) * u_ref[...]).astype(BF16)

    return pl.pallas_call(
        body, name="swiglu_fwd", grid=(t_dim // tt,),
        in_specs=[pl.BlockSpec((tt, D_FF), lambda i: (i, 0)), pl.BlockSpec((tt, D_FF), lambda i: (i, 1))],
        out_specs=pl.BlockSpec((tt, D_FF), lambda i: (i, 0)),
        out_shape=jax.ShapeDtypeStruct((t_dim, D_FF), BF16),
        compiler_params=_params(("parallel",)),
    )(gu, gu)


def _swiglu_bwd(gu, dact, *, tt=256):
    t_dim = gu.shape[0]

    def body(g_ref, u_ref, da_ref, o_ref):
        gv = g_ref[...]
        da = da_ref[...]
        sg = _sigmoid(gv)
        o_ref[:, :D_FF] = (da * u_ref[...] * (sg + gv * sg * (1.0 - sg))).astype(BF16)
        o_ref[:, D_FF:] = (da * (gv * sg)).astype(BF16)

    return pl.pallas_call(
        body, name="swiglu_bwd", grid=(t_dim // tt,),
        in_specs=[pl.BlockSpec((tt, D_FF), lambda i: (i, 0)), pl.BlockSpec((tt, D_FF), lambda i: (i, 1)),
                  pl.BlockSpec((tt, D_FF), lambda i: (i, 0))],
        out_specs=pl.BlockSpec((tt, 2 * D_FF), lambda i: (i, 0)),
        out_shape=jax.ShapeDtypeStruct((t_dim, 2 * D_FF), BF16),
        compiler_params=_params(("parallel",)),
    )(gu, gu, dact)


def _mem_fwd(qm, kvm, *, tt=512):
    b_dim, s_dim, _ = qm.shape
    n_mem = kvm.shape[1]
    scale = MEM_HEAD_DIM ** -0.5

    def body(q_ref, k_ref, v_ref, o_ref):
        sc = lax.dot_general(q_ref[0], k_ref[0], (((1,), (1,)), ((), ())), preferred_element_type=F32) * scale
        p = jnp.exp(sc - jnp.max(sc, axis=-1, keepdims=True))
        p = p / jnp.sum(p, axis=-1, keepdims=True)
        o_ref[0] = jnp.dot(p.astype(BF16), v_ref[0], preferred_element_type=F32).astype(BF16)

    qs = pl.BlockSpec((1, tt, MEM_HEAD_DIM), lambda b, h, i: (b, i, h))
    return pl.pallas_call(
        body, name="mem_fwd", grid=(b_dim, N_HEADS_MEM, s_dim // tt),
        in_specs=[qs, pl.BlockSpec((1, n_mem, MEM_HEAD_DIM), lambda b, h, i: (b, 0, h)),
                  pl.BlockSpec((1, n_mem, MEM_HEAD_DIM), lambda b, h, i: (b, 0, N_HEADS_MEM + h))],
        out_specs=qs, out_shape=jax.ShapeDtypeStruct(qm.shape, BF16),
        compiler_params=_params(("parallel", "parallel", "parallel")),
    )(qm, kvm, kvm)


def _mem_bwd(qm, kvm, dom, *, tt=512):
    b_dim, s_dim, _ = qm.shape
    n_mem = kvm.shape[1]
    scale = MEM_HEAD_DIM ** -0.5
    nt = (((1,), (1,)), ((), ()))
    tn = (((0,), (0,)), ((), ()))

    def body(q_ref, k_ref, v_ref, do_ref, dq_ref, dk_ref, dv_ref):
        qv, kv, vv, dov = q_ref[0], k_ref[0], v_ref[0], do_ref[0]
        sc = lax.dot_general(qv, kv, nt, preferred_element_type=F32) * scale
        p = jnp.exp(sc - jnp.max(sc, axis=-1, keepdims=True))
        p = p / jnp.sum(p, axis=-1, keepdims=True)
        dp = lax.dot_general(dov, vv, nt, preferred_element_type=F32)
        ds = (p * (dp - jnp.sum(p * dp, axis=-1, keepdims=True)) * scale).astype(BF16)
        dq_ref[0] = jnp.dot(ds, kv, preferred_element_type=F32).astype(BF16)

        @pl.when(pl.program_id(2) == 0)
        def _():
            dk_ref[...] = jnp.zeros_like(dk_ref)
            dv_ref[...] = jnp.zeros_like(dv_ref)

        dk_ref[0] += lax.dot_general(ds, qv, tn, preferred_element_type=F32)
        dv_ref[0] += lax.dot_general(p.astype(BF16), dov, tn, preferred_element_type=F32)

    qs = pl.BlockSpec((1, tt, MEM_HEAD_DIM), lambda b, h, i: (b, i, h))
    ks = pl.BlockSpec((1, n_mem, MEM_HEAD_DIM), lambda b, h, i: (b, 0, h))
    vs = pl.BlockSpec((1, n_mem, MEM_HEAD_DIM), lambda b, h, i: (b, 0, N_HEADS_MEM + h))
    return pl.pallas_call(
        body, name="mem_bwd", grid=(b_dim, N_HEADS_MEM, s_dim // tt),
        in_specs=[qs, ks, vs, qs], out_specs=[qs, ks, ks],
        out_shape=[jax.ShapeDtypeStruct(qm.shape, BF16), jax.ShapeDtypeStruct((b_dim, n_mem, MEM_WIDTH), F32),
                   jax.ShapeDtypeStruct((b_dim, n_mem, MEM_WIDTH), F32)],
        compiler_params=_params(("parallel", "parallel", "arbitrary")),
    )(qm, kvm, kvm, dom)


def _adamw(w, g, m, v, *, name):
    rows, cols = w.shape
    tr = _tile(rows, 256, 8)

    def body(w_ref, g_ref, m_ref, v_ref, d_ref, nm_ref, nv_ref):
        gv = g_ref[...]
        nm = ADAM_B1 * m_ref[...] + (1.0 - ADAM_B1) * gv
        nv = ADAM_B2 * v_ref[...] + (1.0 - ADAM_B2) * (gv * gv)
        m_hat = nm / (1.0 - ADAM_B1 ** ADAM_STEP)
        v_hat = nv / (1.0 - ADAM_B2 ** ADAM_STEP)
        d_ref[...] = -ADAM_LR * (m_hat / (jnp.sqrt(v_hat) + ADAM_EPS) + ADAM_WD * w_ref[...])
        nm_ref[...] = nm
        nv_ref[...] = nv

    spec = pl.BlockSpec((tr, cols), lambda i: (i, 0))
    shp = jax.ShapeDtypeStruct((rows, cols), F32)
    return pl.pallas_call(
        body, name=name, grid=(rows // tr,),
        in_specs=[spec] * 4, out_specs=[spec] * 3, out_shape=[shp] * 3,
        compiler_params=_params(("parallel",)),
    )(w, g, m, v)


def _add(a, b, *, name, tr=256):
    rows, cols = a.shape
    tr = _tile(rows, tr, 8)

    def body(a_ref, b_ref, o_ref):
        o_ref[...] = a_ref[...] + b_ref[...]

    spec = pl.BlockSpec((tr, cols), lambda i: (i, 0))
    return pl.pallas_call(
        body, name=name, grid=(rows // tr,), in_specs=[spec, spec], out_specs=spec,
        out_shape=jax.ShapeDtypeStruct((rows, cols), a.dtype),
        compiler_params=_params(("parallel",)),
    )(a, b)


def _sum4(parts, *, name, tr=256):
    _, rows, cols = parts.shape
    tr = _tile(rows, tr, 8)

    def body(p_ref, o_ref):
        o_ref[...] = ((p_ref[0] + p_ref[1]) + p_ref[2]) + p_ref[3]

    return pl.pallas_call(
        body, name=name, grid=(rows // tr,),
        in_specs=[pl.BlockSpec((N_CHIPS, tr, cols), lambda i: (0, i, 0))],
        out_specs=pl.BlockSpec((tr, cols), lambda i: (i, 0)),
        out_shape=jax.ShapeDtypeStruct((rows, cols), parts.dtype),
        compiler_params=_params(("parallel",)),
    )(parts)


def _place():
    return lax.axis_index("x"), lax.axis_index("y"), lax.axis_index("c")


ANY = pl.BlockSpec(memory_space=pl.ANY)


def _gather_weights(flat):
    rows2, cols = flat.shape
    half = rows2 // 2

    def body(x_ref, out_ref, send_sems, recv_sems, local_sem):
        x, y, c = _place()
        me, sibling = (x, y, c), (x, y, 1 - c)
        chips = [(1 - x, y), (x, 1 - y), (1 - x, 1 - y)]
        mine_src = x_ref.at[pl.ds(c * half, half), :]

        def rows(px, py, pc):
            return out_ref.at[pl.ds((4 * px + 2 * py + pc) * half, half), :]

        def copy(k, block, to, src=None):
            return pltpu.make_async_remote_copy(
                src_ref=rows(*block) if src is None else src, dst_ref=rows(*block),
                send_sem=send_sems.at[k], recv_sem=recv_sems.at[k], device_id=to, device_id_type=MESH)

        mine = pltpu.make_async_copy(mine_src, rows(*me), local_sem)
        mine.start()
        first = [copy(0, me, sibling, src=mine_src)]
        first += [copy(1 + j, me, (*chip, c), src=mine_src) for j, chip in enumerate(chips)]
        for cp in first:
            cp.start()
        passed = [copy(4 + j, (*chip, c), sibling) for j, chip in enumerate(chips)]
        for j, chip in enumerate(chips):
            copy(1 + j, (*chip, c), me).wait_recv()
            passed[j].start()
        copy(0, sibling, me).wait_recv()
        for j, chip in enumerate(chips):
            copy(4 + j, (*chip, 1 - c), me).wait_recv()
        for cp in first + passed:
            cp.wait_send()
        mine.wait()

    return pl.pallas_call(
        body, name="gather_weights",
        out_shape=jax.ShapeDtypeStruct((8 * half, cols), flat.dtype),
        in_specs=[ANY], out_specs=ANY,
        scratch_shapes=[pltpu.SemaphoreType.DMA((7,)), pltpu.SemaphoreType.DMA((7,)), pltpu.SemaphoreType.DMA],
    )(flat)


def _pair_exchange(g):
    n, rows2, cols = g.shape
    half = rows2 // 2

    def body(g_ref, out_ref, send_sem, recv_sem):
        x, y, c = _place()
        cp = pltpu.make_async_remote_copy(
            src_ref=g_ref.at[:, pl.ds((1 - c) * half, half), :], dst_ref=out_ref,
            send_sem=send_sem, recv_sem=recv_sem, device_id=(x, y, 1 - c), device_id_type=MESH)
        cp.start()
        cp.wait()

    return pl.pallas_call(
        body, name="grad_pair_exchange",
        out_shape=jax.ShapeDtypeStruct((n, half, cols), g.dtype),
        in_specs=[ANY], out_specs=ANY,
        scratch_shapes=[pltpu.SemaphoreType.DMA, pltpu.SemaphoreType.DMA],
    )(g)


def _chip_exchange(a):
    n, half, cols = a.shape

    def body(a_ref, out_ref, send_sems, recv_sems, local_sem):
        x, y, c = _place()
        my_chip = 2 * x + y
        others = [(1 - x, y), (x, 1 - y), (1 - x, 1 - y)]
        mine = pltpu.make_async_copy(a_ref.at[my_chip], out_ref.at[my_chip], local_sem)
        mine.start()
        sends = []
        for j, (px, py) in enumerate(others):
            sends.append(pltpu.make_async_remote_copy(
                src_ref=a_ref.at[2 * px + py], dst_ref=out_ref.at[my_chip],
                send_sem=send_sems.at[j], recv_sem=recv_sems.at[j], device_id=(px, py, c), device_id_type=MESH))
        for cp in sends:
            cp.start()
        for j, (px, py) in enumerate(others):
            pltpu.make_async_remote_copy(
                src_ref=a_ref.at[my_chip], dst_ref=out_ref.at[2 * px + py],
                send_sem=send_sems.at[j], recv_sem=recv_sems.at[j], device_id=(px, py, c),
                device_id_type=MESH).wait_recv()
        for cp in sends:
            cp.wait_send()
        mine.wait()

    return pl.pallas_call(
        body, name="grad_chip_exchange",
        out_shape=jax.ShapeDtypeStruct((n, half, cols), a.dtype),
        in_specs=[ANY], out_specs=ANY,
        scratch_shapes=[pltpu.SemaphoreType.DMA((3,)), pltpu.SemaphoreType.DMA((3,)), pltpu.SemaphoreType.DMA],
    )(a)


def _share(h, small):
    half, cols = h.shape
    srows = small.shape[0]

    def body(h_ref, s_ref, out_ref, all_ref, send_sems, recv_sems, local_sems):
        x, y, c = _place()
        me = 4 * x + 2 * y + c
        keep = pltpu.make_async_copy(h_ref, out_ref.at[pl.ds(c * half, half), :], local_sems.at[0])
        keep.start()
        keep_small = pltpu.make_async_copy(s_ref, all_ref.at[me], local_sems.at[1])
        keep_small.start()
        swap = pltpu.make_async_remote_copy(
            src_ref=h_ref, dst_ref=out_ref.at[pl.ds(c * half, half), :],
            send_sem=send_sems.at[0], recv_sem=recv_sems.at[0], device_id=(x, y, 1 - c), device_id_type=MESH)
        swap.start()
        sends = []
        for kk in range(1, 8):
            peer = (x ^ (kk >> 2), y ^ ((kk >> 1) & 1), c ^ (kk & 1))
            sends.append(pltpu.make_async_remote_copy(
                src_ref=s_ref, dst_ref=all_ref.at[me],
                send_sem=send_sems.at[kk], recv_sem=recv_sems.at[kk], device_id=peer, device_id_type=MESH))
        for cp in sends:
            cp.start()
        pltpu.make_async_remote_copy(
            src_ref=h_ref, dst_ref=out_ref.at[pl.ds((1 - c) * half, half), :],
            send_sem=send_sems.at[0], recv_sem=recv_sems.at[0], device_id=(x, y, 1 - c),
            device_id_type=MESH).wait_recv()
        for kk in range(1, 8):
            px, py, pc = x ^ (kk >> 2), y ^ ((kk >> 1) & 1), c ^ (kk & 1)
            pltpu.make_async_remote_copy(
                src_ref=s_ref, dst_ref=all_ref.at[4 * px + 2 * py + pc],
                send_sem=send_sems.at[kk], recv_sem=recv_sems.at[kk], device_id=(px, py, pc),
                device_id_type=MESH).wait_recv()
        swap.wait_send()
        for cp in sends:
            cp.wait_send()
        keep.wait()
        keep_small.wait()

    return pl.pallas_call(
        body, name="grad_share",
        out_shape=[jax.ShapeDtypeStruct((2 * half, cols), h.dtype), jax.ShapeDtypeStruct((8, srows, cols), small.dtype)],
        in_specs=[ANY, ANY], out_specs=[ANY, ANY],
        scratch_shapes=[pltpu.SemaphoreType.DMA((8,)), pltpu.SemaphoreType.DMA((8,)), pltpu.SemaphoreType.DMA((2,))],
    )(h, small)


def _sum8(parts):
    n, rows, cols = parts.shape

    def body(p_ref, o_ref):
        acc = p_ref[0]
        for i in range(1, n):
            acc = acc + p_ref[i]
        o_ref[...] = acc

    return pl.pallas_call(
        body, name="small_sum", grid=(1,),
        in_specs=[pl.BlockSpec((n, rows, cols), lambda i: (0, 0, 0))],
        out_specs=pl.BlockSpec((rows, cols), lambda i: (0, 0)),
        out_shape=jax.ShapeDtypeStruct((rows, cols), parts.dtype),
        compiler_params=_params(("arbitrary",)),
    )(parts)


SHARDED = (("w_in", D_MODEL, IN_COLS, 1), ("w_up_a", ATT_WIDTH, D_MODEL, 1), ("w_up_b", ATT_WIDTH, D_MODEL, 1),
           ("w_out", D_MODEL, D_MODEL, 0), ("w_q_mem", D_MODEL, MEM_WIDTH, 0), ("w_kv_mem", D_MODEL, 2 * MEM_WIDTH, 0),
           ("w_o_mem", MEM_WIDTH, D_MODEL, 1), ("w_ffn_gate", D_MODEL, D_FF, 1), ("w_ffn_up", D_MODEL, D_FF, 1),
           ("w_ffn_down", D_FF, D_MODEL, 0))
SHARD_ELEMS = sum(r * c for _, r, c, _ in SHARDED) // N_CHIPS
SHARD_ROWS = SHARD_ELEMS // FLAT_COLS
assert SHARD_ROWS * FLAT_COLS == SHARD_ELEMS and SHARD_ROWS % 32 == 0
GAINS = ("g_mix", "g_mem_q", "g_mem_kv", "g_ffn", "g_final")


def _flatten_shards(shards, dtype):
    return jnp.concatenate([shards[n].astype(dtype).reshape(-1) for n, _, _, _ in SHARDED]).reshape(SHARD_ROWS, FLAT_COLS)


def _unflatten_full(flat4):
    flat4 = flat4.reshape(N_CHIPS, SHARD_ELEMS)
    out, off = {}, 0
    for n, r, c, axis in SHARDED:
        size = r * c // N_CHIPS
        piece = flat4[:, off:off + size]
        off += size
        if axis == 0:
            out[n] = piece.reshape(r, c)
        else:
            out[n] = piece.reshape(N_CHIPS, r, c // N_CHIPS).transpose(1, 0, 2).reshape(r, c)
    return out


def _flatten_full(full, dtype):
    pieces = []
    for n, r, c, axis in SHARDED:
        gfull = full[n].astype(dtype)
        if axis == 0:
            pieces.append(gfull.reshape(N_CHIPS, r * c // N_CHIPS))
        else:
            pieces.append(gfull.reshape(r, N_CHIPS, c // N_CHIPS).transpose(1, 0, 2).reshape(N_CHIPS, r * c // N_CHIPS))
    return jnp.concatenate(pieces, axis=1).reshape(N_CHIPS, SHARD_ROWS, FLAT_COLS)


def _unflatten_shard(flat):
    flat = flat.reshape(-1)
    out, off = {}, 0
    for n, r, c, axis in SHARDED:
        size = r * c // N_CHIPS
        shape = (r // N_CHIPS, c) if axis == 0 else (r, c // N_CHIPS)
        out[n] = flat[off:off + size].reshape(shape)
        off += size
    return out


def _heads(t, b_dim, s_dim):
    return t.reshape(b_dim, s_dim, N_HEADS, HEAD_DIM).transpose(0, 2, 1, 3).reshape(b_dim * N_HEADS, s_dim, HEAD_DIM)


def _merge(t, b_dim, s_dim):
    return t.reshape(b_dim, N_HEADS, s_dim, HEAD_DIM).transpose(0, 2, 1, 3).reshape(b_dim * s_dim, ATT_WIDTH)


def _to_stream(t, dil):
    g_dim, s_dim, c = t.shape
    return t.reshape(g_dim, s_dim // dil, dil, c).transpose(0, 2, 1, 3)


def _from_stream(t):
    g_dim, dil, l_dim, c = t.shape
    return t.transpose(0, 2, 1, 3).reshape(g_dim, dil * l_dim, c)


def kernel(x, mem, positions, g_mix, w_in, w_up_a, w_up_b, w_out, g_mem_q, g_mem_kv, w_q_mem, w_kv_mem, w_o_mem, g_ffn, w_ffn_gate, w_ffn_up, w_ffn_down, g_final, loss_target, m_g_mix, m_w_in, m_w_up_a, m_w_up_b, m_w_out, m_g_mem_q, m_g_mem_kv, m_w_q_mem, m_w_kv_mem, m_w_o_mem, m_g_ffn, m_w_ffn_gate, m_w_ffn_up, m_w_ffn_down, m_g_final, v_g_mix, v_w_in, v_w_up_a, v_w_up_b, v_w_out, v_g_mem_q, v_g_mem_kv, v_w_q_mem, v_w_kv_mem, v_w_o_mem, v_g_ffn, v_w_ffn_gate, v_w_ffn_up, v_w_ffn_down, v_g_final):
    given = dict(locals())
    d = x.shape[-1]
    shards = {n: given[n][0] for n, _, _, _ in SHARDED}

    gathered = _gather_weights(_flatten_shards(shards, BF16))
    wf = _unflatten_full(gathered.reshape(N_CHIPS, SHARD_ROWS, FLAT_COLS))

    loss_row, grad_x, grads, gain_grads = _local_step(x, mem, positions, loss_target, g_mix, g_mem_q, g_mem_kv,
                                                      g_ffn, g_final, wf)
    return _reduce_and_update(given, shards, loss_row, grad_x, grads, gain_grads)


def _local_step(x, mem, positions, loss_target, g_mix, g_mem_q, g_mem_kv, g_ffn, g_final, wf):
    b_dim, s_dim, d = x.shape
    t_dim = b_dim * s_dim
    n_mem = mem.shape[1]
    w_gu = jnp.concatenate([wf["w_ffn_gate"], wf["w_ffn_up"]], axis=1)

    xb = x.reshape(t_dim, d)
    tgt = loss_target.reshape(t_dim, d)
    memf = mem.reshape(b_dim * n_mem, d)
    gfin = g_final.reshape(1, d)
    pos = positions.reshape(t_dim, 1).astype(F32)

    lane = jnp.arange(2 * ATT_WIDTH) % HEAD_DIM
    half = ROPE_DIM // 2
    inv_freq = ROPE_THETA ** (-jnp.arange(half, dtype=F32) / half)
    inv_lane = jnp.where(lane < ROPE_DIM, inv_freq[lane % half], 0.0).reshape(1, -1).astype(F32)
    sel_a = (lane < half).astype(F32).reshape(1, -1)
    sel_b = ((lane >= half) & (lane < ROPE_DIM)).astype(F32).reshape(1, -1)

    n1 = _rms_fwd(xb, g_mix, name="rms_mix")
    proj = _mm(n1, wf["w_in"], name="mm_in")
    qk_a = _rope(proj, pos, inv_lane, sel_a, sel_b, name="rope_fwd", sign=1.0, col_block=0)
    q_a = _heads(qk_a[:, :ATT_WIDTH], b_dim, s_dim)
    k_a = _heads(qk_a[:, ATT_WIDTH:], b_dim, s_dim)
    v_a = _heads(proj[:, 2 * ATT_WIDTH:3 * ATT_WIDTH], b_dim, s_dim)
    q_b = _heads(proj[:, 3 * ATT_WIDTH:4 * ATT_WIDTH], b_dim, s_dim)
    k_b = _heads(proj[:, 4 * ATT_WIDTH:5 * ATT_WIDTH], b_dim, s_dim)
    v_b = _heads(proj[:, 5 * ATT_WIDTH:6 * ATT_WIDTH], b_dim, s_dim)

    streams, outs, lses = [], [], []
    for i, (_, dil) in enumerate(DIL_PATTERNS):
        qs, ks, vs = _to_stream(q_a, dil), _to_stream(k_a, dil), _to_stream(v_a, dil)
        streams.append((qs, ks, vs))
        o_s, l_s = _band_fwd(qs, ks, vs, name="dil_fwd_%d" % dil)
        outs.append(_from_stream(o_s))
        lses.append(_from_stream(l_s))
    o_a, lse_a = _combine(outs, lses)
    o_b = _sb_fwd(q_b, k_b, v_b)
    oa = _merge(o_a, b_dim, s_dim).astype(BF16)
    ob = _merge(o_b, b_dim, s_dim).astype(BF16)
    ua = _mm(oa, wf["w_up_a"], name="mm_up_a")
    ub = _mm(ob, wf["w_up_b"], name="mm_up_b")
    mixed = _gate_fwd(proj, ua, ub)
    h1 = _mm(mixed, wf["w_out"], name="mm_out", add=xb)

    hn = _rms_fwd(h1, g_mem_q, name="rms_mem_q")
    memn = _rms_fwd(memf, g_mem_kv, name="rms_mem_kv")
    qm = _mm(hn, wf["w_q_mem"], name="mm_q_mem", out_dtype=BF16)
    kvm = _mm(memn, wf["w_kv_mem"], name="mm_kv_mem", out_dtype=BF16)
    om = _mem_fwd(qm.reshape(b_dim, s_dim, MEM_WIDTH), kvm.reshape(b_dim, n_mem, 2 * MEM_WIDTH))
    om = om.reshape(t_dim, MEM_WIDTH)
    h2 = _mm(om, wf["w_o_mem"], name="mm_o_mem", add=h1)

    n3 = _rms_fwd(h2, g_ffn, name="rms_ffn")
    gu = _mm(n3, w_gu, name="mm_gate_up", tn_cap=256)
    act = _swiglu_fwd(gu)
    h3 = _mm(act, wf["w_ffn_down"], name="mm_down", add=h2, tk_cap=1408)
    loss_row, dh3, dg_final = _final(h3, gfin, tgt)

    grads = {}
    dact = _mm(dh3, wf["w_ffn_down"], name="mm_down_dx", tb=True, tn_cap=256)
    grads["w_ffn_down"] = _mm(act, dh3, name="mm_down_dw", ta=True, tm_cap=256)
    dgu = _swiglu_bwd(gu, dact)
    dw_gu = _mm(n3, dgu, name="mm_gate_up_dw", ta=True, tn_cap=256)
    grads["w_ffn_gate"], grads["w_ffn_up"] = dw_gu[:, :D_FF], dw_gu[:, D_FF:]
    dn3 = _mm(dgu, w_gu, name="mm_gate_up_dx", tb=True, tk_cap=1408)
    dh2, dg_ffn = _rms_bwd(h2, g_ffn, dn3, dh3, name="rms_ffn_bwd")

    dom = _mm(dh2, wf["w_o_mem"], name="mm_o_mem_dx", tb=True, out_dtype=BF16)
    grads["w_o_mem"] = _mm(om, dh2, name="mm_o_mem_dw", ta=True)
    dqm, dkm, dvm = _mem_bwd(qm.reshape(b_dim, s_dim, MEM_WIDTH), kvm.reshape(b_dim, n_mem, 2 * MEM_WIDTH),
                             dom.reshape(b_dim, s_dim, MEM_WIDTH))
    dqm = dqm.reshape(t_dim, MEM_WIDTH)
    dkvm = jnp.concatenate([dkm, dvm], axis=-1).reshape(b_dim * n_mem, 2 * MEM_WIDTH)
    dkvm = dkvm.astype(BF16)
    grads["w_q_mem"] = _mm(hn, dqm, name="mm_q_mem_dw", ta=True)
    dhn = _mm(dqm, wf["w_q_mem"], name="mm_q_mem_dx", tb=True)
    grads["w_kv_mem"] = _mm(memn, dkvm, name="mm_kv_mem_dw", ta=True)
    dmemn = _mm(dkvm, wf["w_kv_mem"], name="mm_kv_mem_dx", tb=True)
    _, dg_mem_kv = _rms_bwd(memf, g_mem_kv, dmemn, None, name="rms_mem_kv_bwd")
    dh1, dg_mem_q = _rms_bwd(h1, g_mem_q, dhn, dh2, name="rms_mem_q_bwd")

    dmix = _mm(dh1, wf["w_out"], name="mm_out_dx", tb=True)
    grads["w_out"] = _mm(mixed, dh1, name="mm_out_dw", ta=True)
    dua, dub, dgates = _gate_bwd(proj, ua, ub, dmix)
    doa = _mm(dua, wf["w_up_a"], name="mm_up_a_dx", tb=True)
    grads["w_up_a"] = _mm(oa, dua, name="mm_up_a_dw", ta=True)
    dob = _mm(dub, wf["w_up_b"], name="mm_up_b_dx", tb=True)
    grads["w_up_b"] = _mm(ob, dub, name="mm_up_b_dw", ta=True)

    dq_b, dk_b, dv_b = _sb_bwd(q_b, k_b, v_b, _heads(dob, b_dim, s_dim))
    do_a = _heads(doa, b_dim, s_dim)
    delta_a = _rowdot(do_a, o_a)
    dq_a = dk_a = dv_a = None
    for i, (_, dil) in enumerate(DIL_PATTERNS):
        qs, ks, vs = streams[i]
        dqs, dks, dvs = _band_bwd(qs, ks, vs, _to_stream(do_a, dil), _to_stream(lse_a, dil),
                                  _to_stream(delta_a, dil), name="dil_bwd_%d" % dil)
        dqs, dks, dvs = _from_stream(dqs), _from_stream(dks), _from_stream(dvs)
        dq_a = dqs if dq_a is None else dq_a + dqs
        dk_a = dks if dk_a is None else dk_a + dks
        dv_a = dvs if dv_a is None else dv_a + dvs
    dqk_rot = jnp.concatenate([_merge(dq_a, b_dim, s_dim), _merge(dk_a, b_dim, s_dim)], axis=1)
    dqk_a = _rope(dqk_rot, pos, inv_lane, sel_a, sel_b, name="rope_bwd", sign=-1.0, col_block=0)
    dproj = jnp.concatenate([dqk_a.astype(BF16), _merge(dv_a, b_dim, s_dim).astype(BF16),
                             _merge(dq_b, b_dim, s_dim).astype(BF16), _merge(dk_b, b_dim, s_dim).astype(BF16),
                             _merge(dv_b, b_dim, s_dim).astype(BF16), dgates], axis=1)
    grads["w_in"] = _mm(n1, dproj, name="mm_in_dw", ta=True)
    dn1 = _mm(dproj, wf["w_in"], name="mm_in_dx", tb=True)
    dx, dg_mix = _rms_bwd(xb, g_mix, dn1, dh1, name="rms_mix_bwd")
    grad_x = dx.reshape(b_dim, s_dim, d)
    return loss_row, grad_x, grads, (dg_mix, dg_mem_q, dg_mem_kv, dg_ffn, dg_final)


def _reduce_and_update(given, shards, loss_row, grad_x, grads, gain_grads):
    d = D_MODEL
    dg_mix, dg_mem_q, dg_mem_kv, dg_ffn, dg_final = gain_grads
    g4 = _flatten_full(grads, F32)
    half = SHARD_ROWS // 2
    c = lax.axis_index("c")
    theirs = _pair_exchange(g4)
    mine = lax.dynamic_slice_in_dim(g4, c * half, half, axis=1)
    pair = _add(mine.reshape(N_CHIPS * half, FLAT_COLS), theirs.reshape(N_CHIPS * half, FLAT_COLS), name="grad_pair_sum")
    parts = _chip_exchange(pair.reshape(N_CHIPS, half, FLAT_COLS))
    my_half = _sum4(parts, name="grad_chip_sum")
    small = jnp.concatenate([dg_mix, dg_mem_q, dg_mem_kv, dg_ffn, dg_final,
                             jnp.pad(loss_row, ((0, 0), (0, FLAT_COLS - LANES))), jnp.zeros((2, FLAT_COLS), F32)], axis=0)
    shard_flat, small_all = _share(my_half, small)
    small_sum = _sum8(small_all)
    loss = small_sum[5, 0]
    gsh = _unflatten_shard(shard_flat)

    out_g, out_d, out_m, out_v = {}, {}, {}, {}
    for n, _, _, _ in SHARDED:
        w2, g2 = shards[n], gsh[n]
        dl, nm, nv = _adamw(w2, g2, given["m_" + n][0], given["v_" + n][0], name="adamw_" + n)
        out_g[n], out_d[n], out_m[n], out_v[n] = g2[None], dl[None], nm[None], nv[None]
    gain_w = jnp.concatenate([given[n].reshape(1, d) for n in GAINS], axis=0)
    gain_m = jnp.concatenate([given["m_" + n].reshape(1, d) for n in GAINS], axis=0)
    gain_v = jnp.concatenate([given["v_" + n].reshape(1, d) for n in GAINS], axis=0)
    gain_g = small_sum[:len(GAINS)]
    gd, gm, gv = _adamw(gain_w, gain_g, gain_m, gain_v, name="adamw_gains")
    for i, n in enumerate(GAINS):
        shape = given[n].shape
        out_g[n], out_d[n] = gain_g[i].reshape(shape), gd[i].reshape(shape)
        out_m[n], out_v[n] = gm[i].reshape(shape), gv[i].reshape(shape)

    order = ["g_mix", "w_in", "w_up_a", "w_up_b", "w_out", "g_mem_q", "g_mem_kv", "w_q_mem", "w_kv_mem", "w_o_mem",
             "g_ffn", "w_ffn_gate", "w_ffn_up", "w_ffn_down", "g_final"]
    return (loss, grad_x, *[out_g[n] for n in order], *[out_d[n] for n in order],
            *[out_m[n] for n in order], *[out_v[n] for n in order])
```

```python
import jax
import jax.numpy as jnp
from jax import lax
from jax.experimental import pallas as pl
from jax.experimental.pallas import tpu as pltpu

F32 = jnp.float32
BF16 = jnp.bfloat16
MESH = pl.DeviceIdType.MESH

D_MODEL = 1024
HEAD_DIM = 64
N_HEADS = 8
ATT_WIDTH = N_HEADS * HEAD_DIM
DIL_PATTERNS = ((128, 1), (512, 4), (2048, 16))
BLOCK = 128
SB_ROWS = 256
ROPE_THETA = 500000.0
ROPE_DIM = HEAD_DIM // 4
N_HEADS_MEM = 4
MEM_HEAD_DIM = 128
MEM_WIDTH = N_HEADS_MEM * MEM_HEAD_DIM
D_FF = 2816
IN_COLS = 6 * ATT_WIDTH + 2 * D_MODEL
RMS_EPS = 1e-6
ADAM_LR = 0.001
ADAM_B1 = 0.9
ADAM_B2 = 0.999
ADAM_EPS = 1e-08
ADAM_WD = 0.01
ADAM_STEP = 10

N_CHIPS = 4
LANES = 128
FLAT_COLS = 1024
VMEM_LIMIT = 56 * 1024 * 1024

PAIRS = ATT_WIDTH // LANES
COL_QA, COL_KA, COL_VA, COL_QB, COL_KB, COL_VB = (i * PAIRS for i in range(6))

NT = (((1,), (1,)), ((), ()))
TN = (((0,), (0,)), ((), ()))


def _tile(dim, cap, unit=LANES):
    if dim <= cap:
        return dim
    best = None
    for t in range(unit, cap + 1, unit):
        if dim % t == 0:
            best = t
    assert best is not None, (dim, cap)
    return best


def _params(sem):
    return pltpu.CompilerParams(dimension_semantics=sem, vmem_limit_bytes=VMEM_LIMIT)


def _mm(a, b, *, name, ta=False, tb=False, add=None, out_dtype=F32,
        tm_cap=1024, tn_cap=1024, tk_cap=1024):
    if ta:
        k_dim, m_dim = a.shape
    else:
        m_dim, k_dim = a.shape
    if tb:
        n_dim, kb = b.shape
    else:
        kb, n_dim = b.shape
    assert kb == k_dim, (a.shape, b.shape, ta, tb)
    tm, tn, tk = _tile(m_dim, tm_cap), _tile(n_dim, tn_cap), _tile(k_dim, tk_cap)
    nk = k_dim // tk
    dims = (((0 if ta else 1,), (1 if tb else 0,)), ((), ()))
    has_add = add is not None

    def body(*refs):
        if has_add:
            a_ref, b_ref, add_ref, o_ref = refs[:4]
        else:
            a_ref, b_ref, o_ref = refs[:3]
        part = lax.dot_general(a_ref[...].astype(BF16), b_ref[...].astype(BF16), dims, preferred_element_type=F32)

        def finish(r):
            if has_add:
                r = add_ref[...] + r
            o_ref[...] = r.astype(out_dtype)

        if nk == 1:
            finish(part)
            return
        acc_ref = refs[-1]
        k = pl.program_id(2)

        @pl.when(k == 0)
        def _():
            acc_ref[...] = part

        @pl.when(k > 0)
        def _():
            acc_ref[...] += part

        @pl.when(k == nk - 1)
        def _():
            finish(acc_ref[...])

    a_spec = pl.BlockSpec((tk, tm), lambda i, j, k: (k, i)) if ta else pl.BlockSpec((tm, tk), lambda i, j, k: (i, k))
    b_spec = pl.BlockSpec((tn, tk), lambda i, j, k: (j, k)) if tb else pl.BlockSpec((tk, tn), lambda i, j, k: (k, j))
    o_spec = pl.BlockSpec((tm, tn), lambda i, j, k: (i, j))
    in_specs = [a_spec, b_spec] + ([o_spec] if has_add else [])
    args = (a, b) + ((add,) if has_add else ())
    return pl.pallas_call(
        body, name=name, grid=(m_dim // tm, n_dim // tn, nk),
        in_specs=in_specs, out_specs=o_spec,
        out_shape=jax.ShapeDtypeStruct((m_dim, n_dim), out_dtype),
        scratch_shapes=[pltpu.VMEM((tm, tn), F32)] if nk > 1 else [],
        compiler_params=_params(("parallel", "parallel", "arbitrary")),
    )(*args)


def _rms_fwd(x, g, *, name, tt=512):
    t_dim, d = x.shape
    tt = _tile(t_dim, tt, 8)

    def body(x_ref, g_ref, o_ref):
        xv = x_ref[...]
        r = lax.rsqrt(jnp.mean(xv * xv, axis=-1, keepdims=True) + RMS_EPS)
        o_ref[...] = ((xv * r) * g_ref[...]).astype(o_ref.dtype)

    return pl.pallas_call(
        body, name=name, grid=(t_dim // tt,),
        in_specs=[pl.BlockSpec((tt, d), lambda i: (i, 0)), pl.BlockSpec((1, d), lambda i: (0, 0))],
        out_specs=pl.BlockSpec((tt, d), lambda i: (i, 0)),
        out_shape=jax.ShapeDtypeStruct((t_dim, d), BF16),
        compiler_params=_params(("parallel",)),
    )(x, g)


def _rms_bwd(x, g, dy, add, *, name, tt=512):
    t_dim, d = x.shape
    tt = _tile(t_dim, tt, 8)
    has_add = add is not None

    def body(*refs):
        if has_add:
            x_ref, g_ref, dy_ref, add_ref, dx_ref, dg_ref = refs
        else:
            x_ref, g_ref, dy_ref, dx_ref, dg_ref = refs
        xv = x_ref[...]
        dyv = dy_ref[...].astype(F32)
        r = lax.rsqrt(jnp.mean(xv * xv, axis=-1, keepdims=True) + RMS_EPS)
        xh = xv * r
        u = dyv * g_ref[...]
        dx = r * (u - xh * jnp.mean(u * xh, axis=-1, keepdims=True))
        if has_add:
            dx = add_ref[...] + dx
        dx_ref[...] = dx

        @pl.when(pl.program_id(0) == 0)
        def _():
            dg_ref[...] = jnp.zeros_like(dg_ref)

        dg_ref[...] += jnp.sum(dyv * xh, axis=0, keepdims=True)

    row = pl.BlockSpec((tt, d), lambda i: (i, 0))
    vec = pl.BlockSpec((1, d), lambda i: (0, 0))
    in_specs = [row, vec, row] + ([row] if has_add else [])
    args = (x, g, dy) + ((add,) if has_add else ())
    return pl.pallas_call(
        body, name=name, grid=(t_dim // tt,),
        in_specs=in_specs, out_specs=[row, vec],
        out_shape=[jax.ShapeDtypeStruct((t_dim, d), F32), jax.ShapeDtypeStruct((1, d), F32)],
        compiler_params=_params(("arbitrary",)),
    )(*args)


def _final(h, g, target, *, tt=512):
    t_dim, d = h.shape
    n_steps = t_dim // tt

    def body(h_ref, g_ref, t_ref, loss_ref, dh_ref, dg_ref, sq_ref):
        i = pl.program_id(0)
        xv = h_ref[...]
        gv = g_ref[...]
        r = lax.rsqrt(jnp.mean(xv * xv, axis=-1, keepdims=True) + RMS_EPS)
        xh = xv * r
        err = xh * gv - t_ref[...]
        dyv = err * (1.0 / d)
        u = dyv * gv
        dh_ref[...] = r * (u - xh * jnp.mean(u * xh, axis=-1, keepdims=True))

        @pl.when(i == 0)
        def _():
            dg_ref[...] = jnp.zeros_like(dg_ref)
            sq_ref[...] = jnp.zeros_like(sq_ref)

        dg_ref[...] += jnp.sum(dyv * xh, axis=0, keepdims=True)
        sq_ref[...] += jnp.sum(err * err, axis=0, keepdims=True)

        @pl.when(i == n_steps - 1)
        def _():
            total = jnp.sum(sq_ref[...], axis=-1, keepdims=True) * (0.5 / d)
            loss_ref[...] = jnp.broadcast_to(total, loss_ref.shape)

    row = pl.BlockSpec((tt, d), lambda i: (i, 0))
    vec = pl.BlockSpec((1, d), lambda i: (0, 0))
    return pl.pallas_call(
        body, name="final_loss", grid=(n_steps,),
        in_specs=[row, vec, row],
        out_specs=[pl.BlockSpec((1, LANES), lambda i: (0, 0)), row, vec],
        out_shape=[jax.ShapeDtypeStruct((1, LANES), F32), jax.ShapeDtypeStruct((t_dim, d), F32),
                   jax.ShapeDtypeStruct((1, d), F32)],
        scratch_shapes=[pltpu.VMEM((1, d), F32)],
        compiler_params=_params(("arbitrary",)),
    )(h, g, target)


def _rope_table(pos, inv_lane, sel_a, sel_b, *, tt=512):
    t_dim = pos.shape[0]

    def body(p_ref, f_ref, a_ref, b_ref, c_ref, s_ref):
        ang = p_ref[...] * f_ref[...]
        on = (a_ref[...] + b_ref[...]) > 0.0
        c_ref[...] = jnp.where(on, jnp.cos(ang), 1.0)
        s_ref[...] = jnp.where(on, jnp.sin(ang), 0.0)

    vec = pl.BlockSpec((1, LANES), lambda i: (0, 0))
    row = pl.BlockSpec((tt, LANES), lambda i: (i, 0))
    shp = jax.ShapeDtypeStruct((t_dim, LANES), F32)
    return pl.pallas_call(
        body, name="rope_table", grid=(t_dim // tt,),
        in_specs=[pl.BlockSpec((tt, 1), lambda i: (i, 0)), vec, vec, vec],
        out_specs=[row, row], out_shape=[shp, shp],
        compiler_params=_params(("parallel",)),
    )(pos, inv_lane, sel_a, sel_b)


def _rotate(xv, cs, sn, sa, sb):
    half = ROPE_DIM // 2
    up = pltpu.roll(xv, LANES - half, 1)
    dn = pltpu.roll(xv, half, 1)
    return xv * cs + (dn * sb - up * sa) * sn


def _head_masks():
    h1 = lax.broadcasted_iota(jnp.int32, (1, LANES), 1) < HEAD_DIM
    return h1, jnp.logical_not(h1)


def _split_heads(xv, h1, h2):
    return jnp.where(h1, xv, 0.0).astype(BF16), jnp.where(h2, xv, 0.0).astype(BF16)


def _tri_masks():
    r = lax.broadcasted_iota(jnp.int32, (BLOCK, BLOCK), 0)
    c = lax.broadcasted_iota(jnp.int32, (BLOCK, BLOCK), 1)
    return c <= r, r <= c


def _stream_rows(start, dil):
    if dil == 1:
        return pl.ds(pl.multiple_of(start, BLOCK), BLOCK)
    return pl.ds(start, BLOCK, stride=dil)


def _dil_specs(b_dim, s_dim):
    def col(c0):
        return pl.BlockSpec((None, s_dim, LANES),lambda b, h: (b, 0, c0 + h))
    tab = pl.BlockSpec((None, s_dim, LANES),lambda b, h: (b, 0, 0))
    vec = pl.BlockSpec((1, LANES), lambda b, h: (0, 0))
    return col, tab, vec


def _dil_fwd(proj3, cs3, sn3, sel_a, sel_b):
    b_dim, s_dim, _ = proj3.shape
    scale = HEAD_DIM ** -0.5
    n_pat = len(DIL_PATTERNS)

    def body(q_ref, k_ref, v_ref, cs_ref, sn_ref, sa_ref, sb_ref, o16_ref, o32_ref, l_ref, qr, kr, *per_pattern):
        og, lg = per_pattern[:n_pat], per_pattern[n_pat:]
        h1, h2 = _head_masks()
        cur_ok, prev_ok = _tri_masks()
        sa, sb = sa_ref[...], sb_ref[...]

        def prep(j, _):
            rows = pl.ds(pl.multiple_of(j * BLOCK, BLOCK), BLOCK)
            cs, sn = cs_ref[rows, :], sn_ref[rows, :]
            qr[rows, :] = _rotate(q_ref[rows, :], cs, sn, sa, sb) * scale
            kr[rows, :] = _rotate(k_ref[rows, :], cs, sn, sa, sb)
            return 0

        lax.fori_loop(0, s_dim // BLOCK, prep, 0)

        for g, (_, dil) in enumerate(DIL_PATTERNS):
            nb = s_dim // dil // BLOCK

            def one(idx, _, g=g, dil=dil, nb=nb):
                r = idx // nb
                n = idx % nb
                rows = _stream_rows(r + dil * BLOCK * n, dil)
                prow = _stream_rows(r + dil * BLOCK * jnp.maximum(n - 1, 0), dil)
                q1, q2 = _split_heads(qr[rows, :], h1, h2)
                kc = kr[rows, :].astype(BF16)
                vc1, vc2 = _split_heads(v_ref[rows, :], h1, h2)
                if nb > 1:
                    kp = kr[prow, :].astype(BF16)
                    vp1, vp2 = _split_heads(v_ref[prow, :], h1, h2)
                    p_ok = jnp.logical_and(prev_ok, n > 0)

                def head(qh, vch, vph):
                    sc = jnp.where(cur_ok, lax.dot_general(qh, kc, NT, preferred_element_type=F32), -jnp.inf)
                    m = jnp.max(sc, axis=-1, keepdims=True)
                    if nb > 1:
                        sp = jnp.where(p_ok, lax.dot_general(qh, kp, NT, preferred_element_type=F32), -jnp.inf)
                        m = jnp.maximum(m, jnp.max(sp, axis=-1, keepdims=True))
                    pc = jnp.exp(sc - m)
                    den = jnp.sum(pc, axis=-1, keepdims=True)
                    acc = jnp.dot(pc.astype(BF16), vch, preferred_element_type=F32)
                    if nb > 1:
                        pp = jnp.exp(sp - m)
                        den = den + jnp.sum(pp, axis=-1, keepdims=True)
                        acc = acc + jnp.dot(pp.astype(BF16), vph, preferred_element_type=F32)
                    return acc / den, m + jnp.log(den)

                o1, l1 = head(q1, vc1, vp1 if nb > 1 else None)
                o2, l2 = head(q2, vc2, vp2 if nb > 1 else None)
                og[g][rows, :] = o1 + o2
                lg[g][rows, :] = jnp.where(h1, l1, l2)
                return 0

            lax.fori_loop(0, dil * nb, one, 0)

        def comb(j, _):
            rows = pl.ds(pl.multiple_of(j * BLOCK, BLOCK), BLOCK)
            ls = [lg[g][rows, :] for g in range(n_pat)]
            m = jnp.maximum(jnp.maximum(ls[0], ls[1]), ls[2])
            es = [jnp.exp(l - m) for l in ls]
            den = es[0] + es[1] + es[2]
            o = (es[0] * og[0][rows, :] + es[1] * og[1][rows, :] + es[2] * og[2][rows, :]) / den
            o16_ref[rows, :] = o.astype(BF16)
            o32_ref[rows, :] = o
            l_ref[rows, :] = m + jnp.log(den)
            return 0

        lax.fori_loop(0, s_dim // BLOCK, comb, 0)

    col, tab, vec = _dil_specs(b_dim, s_dim)
    out = pl.BlockSpec((None, s_dim, LANES),lambda b, h: (b, 0, h))
    shp = (b_dim, s_dim, ATT_WIDTH)
    return pl.pallas_call(
        body, name="dil_fwd", grid=(b_dim, PAIRS),
        in_specs=[col(COL_QA), col(COL_KA), col(COL_VA), tab, tab, vec, vec],
        out_specs=[out, out, out],
        out_shape=[jax.ShapeDtypeStruct(shp, BF16), jax.ShapeDtypeStruct(shp, F32), jax.ShapeDtypeStruct(shp, F32)],
        scratch_shapes=[pltpu.VMEM((s_dim, LANES), F32)] * (2 + 2 * n_pat),
        compiler_params=_params(("parallel", "parallel")),
    )(proj3, proj3, proj3, cs3, sn3, sel_a, sel_b)


def _dil_bwd(proj3, cs3, sn3, sel_a, sel_b, do3, o3, lse3):
    b_dim, s_dim, _ = proj3.shape
    scale = HEAD_DIM ** -0.5

    def body(q_ref, k_ref, v_ref, cs_ref, sn_ref, sa_ref, sb_ref, do_ref, o_ref, l_ref,
             dq_ref, dk_ref, dv_ref, qr, kr, dqa, dka, dva):
        h1, h2 = _head_masks()
        cur_ok, prev_ok = _tri_masks()
        sa, sb = sa_ref[...], sb_ref[...]

        def prep(j, _):
            rows = pl.ds(pl.multiple_of(j * BLOCK, BLOCK), BLOCK)
            cs, sn = cs_ref[rows, :], sn_ref[rows, :]
            qr[rows, :] = _rotate(q_ref[rows, :], cs, sn, sa, sb) * scale
            kr[rows, :] = _rotate(k_ref[rows, :], cs, sn, sa, sb)
            zero = jnp.zeros((BLOCK, LANES), F32)
            dqa[rows, :] = zero
            dka[rows, :] = zero
            dva[rows, :] = zero
            return 0

        lax.fori_loop(0, s_dim // BLOCK, prep, 0)

        for _, dil in DIL_PATTERNS:
            nb = s_dim // dil // BLOCK

            def one(idx, _, dil=dil, nb=nb):
                r = idx // nb
                n = idx % nb
                rows = _stream_rows(r + dil * BLOCK * n, dil)
                prow = _stream_rows(r + dil * BLOCK * jnp.maximum(n - 1, 0), dil)
                q1, q2 = _split_heads(qr[rows, :], h1, h2)
                dof = do_ref[rows, :]
                do1, do2 = _split_heads(dof, h1, h2)
                prod = dof * o_ref[rows, :]
                delta1 = jnp.sum(jnp.where(h1, prod, 0.0), axis=-1, keepdims=True)
                delta2 = jnp.sum(jnp.where(h2, prod, 0.0), axis=-1, keepdims=True)
                lt = l_ref[rows, :]
                lse1 = jnp.max(jnp.where(h1, lt, -jnp.inf), axis=-1, keepdims=True)
                lse2 = jnp.max(jnp.where(h2, lt, -jnp.inf), axis=-1, keepdims=True)

                def side(krows, ok):
                    kf = kr[krows, :]
                    k16 = kf.astype(BF16)
                    k1, k2 = _split_heads(kf, h1, h2)
                    v16 = v_ref[krows, :].astype(BF16)

                    def head(qh, doh, lse, delta):
                        sc = lax.dot_general(qh, k16, NT, preferred_element_type=F32)
                        p = jnp.where(ok, jnp.exp(sc - lse), 0.0)
                        dp = lax.dot_general(doh, v16, NT, preferred_element_type=F32)
                        return p.astype(BF16), (p * (dp - delta)).astype(BF16)

                    p1, ds1 = head(q1, do1, lse1, delta1)
                    p2, ds2 = head(q2, do2, lse2, delta2)
                    dva[krows, :] += (lax.dot_general(p1, do1, TN, preferred_element_type=F32)
                                      + lax.dot_general(p2, do2, TN, preferred_element_type=F32))
                    dka[krows, :] += (lax.dot_general(ds1, q1, TN, preferred_element_type=F32)
                                      + lax.dot_general(ds2, q2, TN, preferred_element_type=F32))
                    return (jnp.dot(ds1, k1, preferred_element_type=F32)
                            + jnp.dot(ds2, k2, preferred_element_type=F32))

                dq = side(rows, cur_ok)
                if nb > 1:
                    dq = dq + side(prow, jnp.logical_and(prev_ok, n > 0))
                dqa[rows, :] += dq * scale
                return 0

            lax.fori_loop(0, dil * nb, one, 0)

        def finish(j, _):
            rows = pl.ds(pl.multiple_of(j * BLOCK, BLOCK), BLOCK)
            cs, sn = cs_ref[rows, :], -sn_ref[rows, :]
            dq_ref[rows, :] = _rotate(dqa[rows, :], cs, sn, sa, sb).astype(BF16)
            dk_ref[rows, :] = _rotate(dka[rows, :], cs, sn, sa, sb).astype(BF16)
            dv_ref[rows, :] = dva[rows, :].astype(BF16)
            return 0

        lax.fori_loop(0, s_dim // BLOCK, finish, 0)

    col, tab, vec = _dil_specs(b_dim, s_dim)
    out = pl.BlockSpec((None, s_dim, LANES),lambda b, h: (b, 0, h))
    shp = jax.ShapeDtypeStruct((b_dim, s_dim, ATT_WIDTH), BF16)
    acc = pltpu.VMEM((s_dim, LANES), F32)
    return pl.pallas_call(
        body, name="dil_bwd", grid=(b_dim, PAIRS),
        in_specs=[col(COL_QA), col(COL_KA), col(COL_VA), tab, tab, vec, vec, out, out, out],
        out_specs=[out, out, out], out_shape=[shp, shp, shp],
        scratch_shapes=[acc, acc, acc, acc, acc],
        compiler_params=_params(("parallel", "parallel")),
    )(proj3, proj3, proj3, cs3, sn3, sel_a, sel_b, do3, o3, lse3)


def _split_dot(x, tri):
    hi = x.astype(BF16)
    lo = (x - hi.astype(F32)).astype(BF16)
    return jnp.dot(hi, tri, preferred_element_type=F32) + jnp.dot(lo, tri, preferred_element_type=F32)


def _log_sigmoid(z):
    return jnp.minimum(z, 0.0) - jnp.log(1.0 + jnp.exp(-jnp.abs(z)))


def _sb_scores(qh, k16, valid):
    z = lax.dot_general(qh, k16, NT, preferred_element_type=F32)
    ls = _log_sigmoid(z)
    return ls, jnp.where(valid, ls - z, 0.0)


def _sb_consts():
    r = lax.broadcasted_iota(jnp.int32, (BLOCK, BLOCK), 0)
    c = lax.broadcasted_iota(jnp.int32, (BLOCK, BLOCK), 1)
    after = (r > c).astype(BF16)
    before = (r < c).astype(BF16)
    qrow = lax.broadcasted_iota(jnp.int32, (SB_ROWS, BLOCK), 0)
    kcol = lax.broadcasted_iota(jnp.int32, (SB_ROWS, BLOCK), 1)
    return after, before, qrow, kcol


def _sb_fwd(proj3):
    b_dim, s_dim, _ = proj3.shape
    scale = HEAD_DIM ** -0.5
    per = SB_ROWS // BLOCK

    def body(q_ref, k_ref, v_ref, o_ref):
        h1, h2 = _head_masks()
        after, _, qrow, kcol = _sb_consts()

        def qloop(qi, _):
            rows = pl.ds(pl.multiple_of(qi * SB_ROWS, SB_ROWS), SB_ROWS)
            q1, q2 = _split_heads(q_ref[rows, :] * scale, h1, h2)
            qpos = qi * SB_ROWS + qrow
            nkb = (qi + 1) * per

            def kloop(i, carry):
                acc, run1, run2 = carry
                kb = nkb - 1 - i
                krows = pl.ds(pl.multiple_of(kb * BLOCK, BLOCK), BLOCK)
                k16 = k_ref[krows, :].astype(BF16)
                v1, v2 = _split_heads(v_ref[krows, :], h1, h2)
                valid = (kb * BLOCK + kcol) < qpos

                def head(qh, vh, run):
                    ls, l1m = _sb_scores(qh, k16, valid)
                    a = jnp.where(valid, jnp.exp(ls + _split_dot(l1m, after) + run), 0.0)
                    return (jnp.dot(a.astype(BF16), vh, preferred_element_type=F32),
                            run + jnp.sum(l1m, axis=-1, keepdims=True))

                o1, run1 = head(q1, v1, run1)
                o2, run2 = head(q2, v2, run2)
                return acc + o1 + o2, run1, run2

            zcol = jnp.zeros((SB_ROWS, 1), F32)
            acc, _, _ = lax.fori_loop(0, nkb, kloop, (jnp.zeros((SB_ROWS, LANES), F32), zcol, zcol))
            o_ref[rows, :] = acc.astype(BF16)
            return 0

        lax.fori_loop(0, s_dim // SB_ROWS, qloop, 0)

    def col(c0):
        return pl.BlockSpec((None, s_dim, LANES),lambda b, h: (b, 0, c0 + h))

    return pl.pallas_call(
        body, name="sb_fwd", grid=(b_dim, PAIRS),
        in_specs=[col(COL_QB), col(COL_KB), col(COL_VB)], out_specs=col(0),
        out_shape=jax.ShapeDtypeStruct((b_dim, s_dim, ATT_WIDTH), BF16),
        compiler_params=_params(("parallel", "parallel")),
    )(proj3, proj3, proj3)


def _sb_bwd(proj3, do3):
    b_dim, s_dim, _ = proj3.shape
    scale = HEAD_DIM ** -0.5
    per = SB_ROWS // BLOCK
    nkb_max = s_dim // BLOCK

    def body(q_ref, k_ref, v_ref, do_ref, dq_ref, dk_ref, dv_ref, dka, dva, e_ref, sg_ref):
        h1, h2 = _head_masks()
        after, before, qrow, kcol = _sb_consts()
        dka[...] = jnp.zeros_like(dka)
        dva[...] = jnp.zeros_like(dva)

        def qloop(qi, _):
            rows = pl.ds(pl.multiple_of(qi * SB_ROWS, SB_ROWS), SB_ROWS)
            q1, q2 = _split_heads(q_ref[rows, :] * scale, h1, h2)
            do1, do2 = _split_heads(do_ref[rows, :].astype(F32), h1, h2)
            qpos = qi * SB_ROWS + qrow
            nkb = (qi + 1) * per

            def pass1(i, carry):
                run1, run2 = carry
                kb = nkb - 1 - i
                krows = pl.ds(pl.multiple_of(kb * BLOCK, BLOCK), BLOCK)
                k16 = k_ref[krows, :].astype(BF16)
                v16 = v_ref[krows, :].astype(BF16)
                valid = (kb * BLOCK + kcol) < qpos

                def head(h, qh, doh, run):
                    ls, l1m = _sb_scores(qh, k16, valid)
                    a = jnp.where(valid, jnp.exp(ls + _split_dot(l1m, after) + run), 0.0)
                    da = lax.dot_general(doh, v16, NT, preferred_element_type=F32)
                    e_ref[h, kb] = a * da
                    sg_ref[h, kb] = jnp.exp(ls)
                    return a.astype(BF16), run + jnp.sum(l1m, axis=-1, keepdims=True)

                a1, run1 = head(0, q1, do1, run1)
                a2, run2 = head(1, q2, do2, run2)
                dva[krows, :] += (lax.dot_general(a1, do1, TN, preferred_element_type=F32)
                                  + lax.dot_general(a2, do2, TN, preferred_element_type=F32))
                return run1, run2

            zcol = jnp.zeros((SB_ROWS, 1), F32)
            lax.fori_loop(0, nkb, pass1, (zcol, zcol))

            def pass2(kb, carry):
                dq, pre1, pre2 = carry
                krows = pl.ds(pl.multiple_of(kb * BLOCK, BLOCK), BLOCK)
                k1, k2 = _split_heads(k_ref[krows, :], h1, h2)
                valid = (kb * BLOCK + kcol) < qpos

                def head(h, pre):
                    ev = e_ref[h, kb]
                    sg = sg_ref[h, kb]
                    prefix = _split_dot(ev, before) + pre
                    dz = jnp.where(valid, ev * (1.0 - sg) - prefix * sg, 0.0).astype(BF16)
                    return dz, pre + jnp.sum(ev, axis=-1, keepdims=True)

                dz1, pre1 = head(0, pre1)
                dz2, pre2 = head(1, pre2)
                dka[krows, :] += (lax.dot_general(dz1, q1, TN, preferred_element_type=F32)
                                  + lax.dot_general(dz2, q2, TN, preferred_element_type=F32))
                dq = dq + jnp.dot(dz1, k1, preferred_element_type=F32) + jnp.dot(dz2, k2, preferred_element_type=F32)
                return dq, pre1, pre2

            dq, _, _ = lax.fori_loop(0, nkb, pass2, (jnp.zeros((SB_ROWS, LANES), F32), zcol, zcol))
            dq_ref[rows, :] = (dq * scale).astype(BF16)
            return 0

        lax.fori_loop(0, s_dim // SB_ROWS, qloop, 0)
        dk_ref[...] = dka[...].astype(BF16)
        dv_ref[...] = dva[...].astype(BF16)

    def col(c0):
        return pl.BlockSpec((None, s_dim, LANES),lambda b, h: (b, 0, c0 + h))

    shp = jax.ShapeDtypeStruct((b_dim, s_dim, ATT_WIDTH), BF16)
    acc = pltpu.VMEM((s_dim, LANES), F32)
    strip = pltpu.VMEM((2, nkb_max, SB_ROWS, BLOCK), F32)
    return pl.pallas_call(
        body, name="sb_bwd", grid=(b_dim, PAIRS),
        in_specs=[col(COL_QB), col(COL_KB), col(COL_VB), col(0)], out_specs=[col(0), col(0), col(0)],
        out_shape=[shp, shp, shp],
        scratch_shapes=[acc, acc, strip, strip],
        compiler_params=_params(("parallel", "parallel")),
    )(proj3, proj3, proj3, do3)


def _sigmoid(x):
    return 1.0 / (1.0 + jnp.exp(-x))


def _gate_fwd(proj, ua, ub, *, tt=512):
    t_dim, d = ua.shape

    def body(ga_ref, gb_ref, ua_ref, ub_ref, o_ref):
        o_ref[...] = (_sigmoid(ga_ref[...]) * ua_ref[...] + _sigmoid(gb_ref[...]) * ub_ref[...]).astype(BF16)

    row = pl.BlockSpec((tt, d), lambda i: (i, 0))
    return pl.pallas_call(
        body, name="gate_fwd", grid=(t_dim // tt,),
        in_specs=[pl.BlockSpec((tt, d), lambda i: (i, 3)), pl.BlockSpec((tt, d), lambda i: (i, 4)), row, row],
        out_specs=row, out_shape=jax.ShapeDtypeStruct((t_dim, d), BF16),
        compiler_params=_params(("parallel",)),
    )(proj, proj, ua, ub)


def _gate_bwd(proj, ua, ub, dmix, *, tt=512):
    t_dim, d = ua.shape

    def body(ga_ref, gb_ref, ua_ref, ub_ref, dm_ref, dua_ref, dub_ref, dg_ref):
        dm = dm_ref[...]
        sa = _sigmoid(ga_ref[...])
        sb = _sigmoid(gb_ref[...])
        dua_ref[...] = (dm * sa).astype(BF16)
        dub_ref[...] = (dm * sb).astype(BF16)
        dg_ref[:, :d] = (dm * ua_ref[...] * (sa * (1.0 - sa))).astype(BF16)
        dg_ref[:, d:] = (dm * ub_ref[...] * (sb * (1.0 - sb))).astype(BF16)

    row = pl.BlockSpec((tt, d), lambda i: (i, 0))
    wide = pl.BlockSpec((tt, 2 * d), lambda i: (i, 0))
    return pl.pallas_call(
        body, name="gate_bwd", grid=(t_dim // tt,),
        in_specs=[pl.BlockSpec((tt, d), lambda i: (i, 3)), pl.BlockSpec((tt, d), lambda i: (i, 4)), row, row, row],
        out_specs=[row, row, wide],
        out_shape=[jax.ShapeDtypeStruct((t_dim, d), BF16), jax.ShapeDtypeStruct((t_dim, d), BF16),
                   jax.ShapeDtypeStruct((t_dim, 2 * d), BF16)],
        compiler_params=_params(("parallel",)),
    )(proj, proj, ua, ub, dmix)


def _swiglu_fwd(gu, *, tt=256):
    t_dim = gu.shape[0]

    def body(g_ref, u_ref, o_ref):
        gv = g_ref[...]
        o_ref[...] = (gv * _sigmoid(gv) * u_ref[...]).astype(BF16)

    return pl.pallas_call(
        body, name="swiglu_fwd", grid=(t_dim // tt,),
        in_specs=[pl.BlockSpec((tt, D_FF), lambda i: (i, 0)), pl.BlockSpec((tt, D_FF), lambda i: (i, 1))],
        out_specs=pl.BlockSpec((tt, D_FF), lambda i: (i, 0)),
        out_shape=jax.ShapeDtypeStruct((t_dim, D_FF), BF16),
        compiler_params=_params(("parallel",)),
    )(gu, gu)


def _swiglu_bwd(gu, dact, *, tt=256):
    t_dim = gu.shape[0]

    def body(g_ref, u_ref, da_ref, o_ref):
        gv = g_ref[...]
        da = da_ref[...]
        sg = _sigmoid(gv)
        o_ref[:, :D_FF] = (da * u_ref[...] * (sg + gv * sg * (1.0 - sg))).astype(BF16)
        o_ref[:, D_FF:] = (da * (gv * sg)).astype(BF16)

    return pl.pallas_call(
        body, name="swiglu_bwd", grid=(t_dim // tt,),
        in_specs=[pl.BlockSpec((tt, D_FF), lambda i: (i, 0)), pl.BlockSpec((tt, D_FF), lambda i: (i, 1)),
                  pl.BlockSpec((tt, D_FF), lambda i: (i, 0))],
        out_specs=pl.BlockSpec((tt, 2 * D_FF), lambda i: (i, 0)),
        out_shape=jax.ShapeDtypeStruct((t_dim, 2 * D_FF), BF16),
        compiler_params=_params(("parallel",)),
    )(gu, gu, dact)


def _mem_fwd(qm, kvm, *, tt=512):
    b_dim, s_dim, _ = qm.shape
    n_mem = kvm.shape[1]
    scale = MEM_HEAD_DIM ** -0.5

    def body(q_ref, k_ref, v_ref, o_ref):
        sc = lax.dot_general(q_ref[0], k_ref[0], NT, preferred_element_type=F32) * scale
        p = jnp.exp(sc - jnp.max(sc, axis=-1, keepdims=True))
        p = p / jnp.sum(p, axis=-1, keepdims=True)
        o_ref[0] = jnp.dot(p.astype(BF16), v_ref[0], preferred_element_type=F32).astype(BF16)

    qs = pl.BlockSpec((1, tt, MEM_HEAD_DIM), lambda b, h, i: (b, i, h))
    return pl.pallas_call(
        body, name="mem_fwd", grid=(b_dim, N_HEADS_MEM, s_dim // tt),
        in_specs=[qs, pl.BlockSpec((1, n_mem, MEM_HEAD_DIM), lambda b, h, i: (b, 0, h)),
                  pl.BlockSpec((1, n_mem, MEM_HEAD_DIM), lambda b, h, i: (b, 0, N_HEADS_MEM + h))],
        out_specs=qs, out_shape=jax.ShapeDtypeStruct(qm.shape, BF16),
        compiler_params=_params(("parallel", "parallel", "parallel")),
    )(qm, kvm, kvm)


def _mem_bwd(qm, kvm, dom, *, tt=512):
    b_dim, s_dim, _ = qm.shape
    n_mem = kvm.shape[1]
    scale = MEM_HEAD_DIM ** -0.5

    def body(q_ref, k_ref, v_ref, do_ref, dq_ref, dk_ref, dv_ref):
        qv, kv, vv, dov = q_ref[0], k_ref[0], v_ref[0], do_ref[0]
        sc = lax.dot_general(qv, kv, NT, preferred_element_type=F32) * scale
        p = jnp.exp(sc - jnp.max(sc, axis=-1, keepdims=True))
        p = p / jnp.sum(p, axis=-1, keepdims=True)
        dp = lax.dot_general(dov, vv, NT, preferred_element_type=F32)
        ds = (p * (dp - jnp.sum(p * dp, axis=-1, keepdims=True)) * scale).astype(BF16)
        dq_ref[0] = jnp.dot(ds, kv, preferred_element_type=F32).astype(BF16)

        @pl.when(pl.program_id(2) == 0)
        def _():
            dk_ref[...] = jnp.zeros_like(dk_ref)
            dv_ref[...] = jnp.zeros_like(dv_ref)

        dk_ref[0] += lax.dot_general(ds, qv, TN, preferred_element_type=F32)
        dv_ref[0] += lax.dot_general(p.astype(BF16), dov, TN, preferred_element_type=F32)

    qs = pl.BlockSpec((1, tt, MEM_HEAD_DIM), lambda b, h, i: (b, i, h))
    ks = pl.BlockSpec((1, n_mem, MEM_HEAD_DIM), lambda b, h, i: (b, 0, h))
    vs = pl.BlockSpec((1, n_mem, MEM_HEAD_DIM), lambda b, h, i: (b, 0, N_HEADS_MEM + h))
    return pl.pallas_call(
        body, name="mem_bwd", grid=(b_dim, N_HEADS_MEM, s_dim // tt),
        in_specs=[qs, ks, vs, qs], out_specs=[qs, ks, ks],
        out_shape=[jax.ShapeDtypeStruct(qm.shape, BF16), jax.ShapeDtypeStruct((b_dim, n_mem, MEM_WIDTH), F32),
                   jax.ShapeDtypeStruct((b_dim, n_mem, MEM_WIDTH), F32)],
        compiler_params=_params(("parallel", "parallel", "arbitrary")),
    )(qm, kvm, kvm, dom)


def _adamw(w, g, m, v, *, name):
    rows, cols = w.shape
    tr = _tile(rows, 256, 8)

    def body(w_ref, g_ref, m_ref, v_ref, d_ref, nm_ref, nv_ref):
        gv = g_ref[...]
        nm = ADAM_B1 * m_ref[...] + (1.0 - ADAM_B1) * gv
        nv = ADAM_B2 * v_ref[...] + (1.0 - ADAM_B2) * (gv * gv)
        m_hat = nm / (1.0 - ADAM_B1 ** ADAM_STEP)
        v_hat = nv / (1.0 - ADAM_B2 ** ADAM_STEP)
        d_ref[...] = -ADAM_LR * (m_hat / (jnp.sqrt(v_hat) + ADAM_EPS) + ADAM_WD * w_ref[...])
        nm_ref[...] = nm
        nv_ref[...] = nv

    spec = pl.BlockSpec((tr, cols), lambda i: (i, 0))
    shp = jax.ShapeDtypeStruct((rows, cols), F32)
    return pl.pallas_call(
        body, name=name, grid=(rows // tr,),
        in_specs=[spec] * 4, out_specs=[spec] * 3, out_shape=[shp] * 3,
        compiler_params=_params(("parallel",)),
    )(w, g, m, v)


def _add(a, b, *, name, tr=256):
    rows, cols = a.shape
    tr = _tile(rows, tr, 8)

    def body(a_ref, b_ref, o_ref):
        o_ref[...] = a_ref[...] + b_ref[...]

    spec = pl.BlockSpec((tr, cols), lambda i: (i, 0))
    return pl.pallas_call(
        body, name=name, grid=(rows // tr,), in_specs=[spec, spec], out_specs=spec,
        out_shape=jax.ShapeDtypeStruct((rows, cols), a.dtype),
        compiler_params=_params(("parallel",)),
    )(a, b)


def _sum4(parts, *, name, tr=256):
    _, rows, cols = parts.shape
    tr = _tile(rows, tr, 8)

    def body(p_ref, o_ref):
        o_ref[...] = ((p_ref[0] + p_ref[1]) + p_ref[2]) + p_ref[3]

    return pl.pallas_call(
        body, name=name, grid=(rows // tr,),
        in_specs=[pl.BlockSpec((N_CHIPS, tr, cols), lambda i: (0, i, 0))],
        out_specs=pl.BlockSpec((tr, cols), lambda i: (i, 0)),
        out_shape=jax.ShapeDtypeStruct((rows, cols), parts.dtype),
        compiler_params=_params(("parallel",)),
    )(parts)


def _sum8(parts):
    n, rows, cols = parts.shape

    def body(p_ref, o_ref):
        acc = p_ref[0]
        for i in range(1, n):
            acc = acc + p_ref[i]
        o_ref[...] = acc

    return pl.pallas_call(
        body, name="small_sum", grid=(1,),
        in_specs=[pl.BlockSpec((n, rows, cols), lambda i: (0, 0, 0))],
        out_specs=pl.BlockSpec((rows, cols), lambda i: (0, 0)),
        out_shape=jax.ShapeDtypeStruct((rows, cols), parts.dtype),
        compiler_params=_params(("arbitrary",)),
    )(parts)


def _place():
    return lax.axis_index("x"), lax.axis_index("y"), lax.axis_index("c")


ANY = pl.BlockSpec(memory_space=pl.ANY)


def _gather_weights(flat):
    rows2, cols = flat.shape
    half = rows2 // 2

    def body(x_ref, out_ref, send_sems, recv_sems, local_sem):
        x, y, c = _place()
        me, sibling = (x, y, c), (x, y, 1 - c)
        chips = [(1 - x, y), (x, 1 - y), (1 - x, 1 - y)]
        mine_src = x_ref.at[pl.ds(c * half, half), :]

        def rows(px, py, pc):
            return out_ref.at[pl.ds((4 * px + 2 * py + pc) * half, half), :]

        def copy(k, block, to, src=None):
            return pltpu.make_async_remote_copy(
                src_ref=rows(*block) if src is None else src, dst_ref=rows(*block),
                send_sem=send_sems.at[k], recv_sem=recv_sems.at[k], device_id=to, device_id_type=MESH)

        mine = pltpu.make_async_copy(mine_src, rows(*me), local_sem)
        mine.start()
        first = [copy(0, me, sibling, src=mine_src)]
        first += [copy(1 + j, me, (*chip, c), src=mine_src) for j, chip in enumerate(chips)]
        for cp in first:
            cp.start()
        passed = [copy(4 + j, (*chip, c), sibling) for j, chip in enumerate(chips)]
        for j, chip in enumerate(chips):
            copy(1 + j, (*chip, c), me).wait_recv()
            passed[j].start()
        copy(0, sibling, me).wait_recv()
        for j, chip in enumerate(chips):
            copy(4 + j, (*chip, 1 - c), me).wait_recv()
        for cp in first + passed:
            cp.wait_send()
        mine.wait()

    return pl.pallas_call(
        body, name="gather_weights",
        out_shape=jax.ShapeDtypeStruct((8 * half, cols), flat.dtype),
        in_specs=[ANY], out_specs=ANY,
        scratch_shapes=[pltpu.SemaphoreType.DMA((7,)), pltpu.SemaphoreType.DMA((7,)), pltpu.SemaphoreType.DMA],
    )(flat)


def _pair_exchange(g):
    n, rows2, cols = g.shape
    half = rows2 // 2

    def body(g_ref, out_ref, send_sem, recv_sem):
        x, y, c = _place()
        cp = pltpu.make_async_remote_copy(
            src_ref=g_ref.at[:, pl.ds((1 - c) * half, half), :], dst_ref=out_ref,
            send_sem=send_sem, recv_sem=recv_sem, device_id=(x, y, 1 - c), device_id_type=MESH)
        cp.start()
        cp.wait()

    return pl.pallas_call(
        body, name="grad_pair_exchange",
        out_shape=jax.ShapeDtypeStruct((n, half, cols), g.dtype),
        in_specs=[ANY], out_specs=ANY,
        scratch_shapes=[pltpu.SemaphoreType.DMA, pltpu.SemaphoreType.DMA],
    )(g)


def _chip_exchange(a):
    n, half, cols = a.shape

    def body(a_ref, out_ref, send_sems, recv_sems, local_sem):
        x, y, c = _place()
        my_chip = 2 * x + y
        others = [(1 - x, y), (x, 1 - y), (1 - x, 1 - y)]
        mine = pltpu.make_async_copy(a_ref.at[my_chip], out_ref.at[my_chip], local_sem)
        mine.start()
        sends = []
        for j, (px, py) in enumerate(others):
            sends.append(pltpu.make_async_remote_copy(
                src_ref=a_ref.at[2 * px + py], dst_ref=out_ref.at[my_chip],
                send_sem=send_sems.at[j], recv_sem=recv_sems.at[j], device_id=(px, py, c), device_id_type=MESH))
        for cp in sends:
            cp.start()
        for j, (px, py) in enumerate(others):
            pltpu.make_async_remote_copy(
                src_ref=a_ref.at[my_chip], dst_ref=out_ref.at[2 * px + py],
                send_sem=send_sems.at[j], recv_sem=recv_sems.at[j], device_id=(px, py, c),
                device_id_type=MESH).wait_recv()
        for cp in sends:
            cp.wait_send()
        mine.wait()

    return pl.pallas_call(
        body, name="grad_chip_exchange",
        out_shape=jax.ShapeDtypeStruct((n, half, cols), a.dtype),
        in_specs=[ANY], out_specs=ANY,
        scratch_shapes=[pltpu.SemaphoreType.DMA((3,)), pltpu.SemaphoreType.DMA((3,)), pltpu.SemaphoreType.DMA],
    )(a)


def _share(h, small):
    half, cols = h.shape
    srows = small.shape[0]

    def body(h_ref, s_ref, out_ref, all_ref, send_sems, recv_sems, local_sems):
        x, y, c = _place()
        me = 4 * x + 2 * y + c
        keep = pltpu.make_async_copy(h_ref, out_ref.at[pl.ds(c * half, half), :], local_sems.at[0])
        keep.start()
        keep_small = pltpu.make_async_copy(s_ref, all_ref.at[me], local_sems.at[1])
        keep_small.start()
        swap = pltpu.make_async_remote_copy(
            src_ref=h_ref, dst_ref=out_ref.at[pl.ds(c * half, half), :],
            send_sem=send_sems.at[0], recv_sem=recv_sems.at[0], device_id=(x, y, 1 - c), device_id_type=MESH)
        swap.start()
        sends = []
        for kk in range(1, 8):
            peer = (x ^ (kk >> 2), y ^ ((kk >> 1) & 1), c ^ (kk & 1))
            sends.append(pltpu.make_async_remote_copy(
                src_ref=s_ref, dst_ref=all_ref.at[me],
                send_sem=send_sems.at[kk], recv_sem=recv_sems.at[kk], device_id=peer, device_id_type=MESH))
        for cp in sends:
            cp.start()
        pltpu.make_async_remote_copy(
            src_ref=h_ref, dst_ref=out_ref.at[pl.ds((1 - c) * half, half), :],
            send_sem=send_sems.at[0], recv_sem=recv_sems.at[0], device_id=(x, y, 1 - c),
            device_id_type=MESH).wait_recv()
        for kk in range(1, 8):
            px, py, pc = x ^ (kk >> 2), y ^ ((kk >> 1) & 1), c ^ (kk & 1)
            pltpu.make_async_remote_copy(
                src_ref=s_ref, dst_ref=all_ref.at[4 * px + 2 * py + pc],
                send_sem=send_sems.at[kk], recv_sem=recv_sems.at[kk], device_id=(px, py, pc),
                device_id_type=MESH).wait_recv()
        swap.wait_send()
        for cp in sends:
            cp.wait_send()
        keep.wait()
        keep_small.wait()

    return pl.pallas_call(
        body, name="grad_share",
        out_shape=[jax.ShapeDtypeStruct((2 * half, cols), h.dtype), jax.ShapeDtypeStruct((8, srows, cols), small.dtype)],
        in_specs=[ANY, ANY], out_specs=[ANY, ANY],
        scratch_shapes=[pltpu.SemaphoreType.DMA((8,)), pltpu.SemaphoreType.DMA((8,)), pltpu.SemaphoreType.DMA((2,))],
    )(h, small)


SHARDED = (("w_in", D_MODEL, IN_COLS, 1), ("w_up_a", ATT_WIDTH, D_MODEL, 1), ("w_up_b", ATT_WIDTH, D_MODEL, 1),
           ("w_out", D_MODEL, D_MODEL, 0), ("w_q_mem", D_MODEL, MEM_WIDTH, 0), ("w_kv_mem", D_MODEL, 2 * MEM_WIDTH, 0),
           ("w_o_mem", MEM_WIDTH, D_MODEL, 1), ("w_ffn_gate", D_MODEL, D_FF, 1), ("w_ffn_up", D_MODEL, D_FF, 1),
           ("w_ffn_down", D_FF, D_MODEL, 0))
SHARD_ELEMS = sum(r * c for _, r, c, _ in SHARDED) // N_CHIPS
SHARD_ROWS = SHARD_ELEMS // FLAT_COLS
assert SHARD_ROWS * FLAT_COLS == SHARD_ELEMS and SHARD_ROWS % 32 == 0
GAINS = ("g_mix", "g_mem_q", "g_mem_kv", "g_ffn", "g_final")


def _flatten_shards(shards, dtype):
    return jnp.concatenate([shards[n].astype(dtype).reshape(-1) for n, _, _, _ in SHARDED]).reshape(SHARD_ROWS, FLAT_COLS)


def _unflatten_full(flat4):
    flat4 = flat4.reshape(N_CHIPS, SHARD_ELEMS)
    out, off = {}, 0
    for n, r, c, axis in SHARDED:
        size = r * c // N_CHIPS
        piece = flat4[:, off:off + size]
        off += size
        if axis == 0:
            out[n] = piece.reshape(r, c)
        else:
            out[n] = piece.reshape(N_CHIPS, r, c // N_CHIPS).transpose(1, 0, 2).reshape(r, c)
    return out


def _flatten_full(full, dtype):
    pieces = []
    for n, r, c, axis in SHARDED:
        gfull = full[n].astype(dtype)
        if axis == 0:
            pieces.append(gfull.reshape(N_CHIPS, r * c // N_CHIPS))
        else:
            pieces.append(gfull.reshape(r, N_CHIPS, c // N_CHIPS).transpose(1, 0, 2).reshape(N_CHIPS, r * c // N_CHIPS))
    return jnp.concatenate(pieces, axis=1).reshape(N_CHIPS, SHARD_ROWS, FLAT_COLS)


def _unflatten_shard(flat):
    flat = flat.reshape(-1)
    out, off = {}, 0
    for n, r, c, axis in SHARDED:
        size = r * c // N_CHIPS
        shape = (r // N_CHIPS, c) if axis == 0 else (r, c // N_CHIPS)
        out[n] = flat[off:off + size].reshape(shape)
        off += size
    return out


def kernel(x, mem, positions, g_mix, w_in, w_up_a, w_up_b, w_out, g_mem_q, g_mem_kv, w_q_mem, w_kv_mem, w_o_mem, g_ffn, w_ffn_gate, w_ffn_up, w_ffn_down, g_final, loss_target, m_g_mix, m_w_in, m_w_up_a, m_w_up_b, m_w_out, m_g_mem_q, m_g_mem_kv, m_w_q_mem, m_w_kv_mem, m_w_o_mem, m_g_ffn, m_w_ffn_gate, m_w_ffn_up, m_w_ffn_down, m_g_final, v_g_mix, v_w_in, v_w_up_a, v_w_up_b, v_w_out, v_g_mem_q, v_g_mem_kv, v_w_q_mem, v_w_kv_mem, v_w_o_mem, v_g_ffn, v_w_ffn_gate, v_w_ffn_up, v_w_ffn_down, v_g_final):
    given = dict(locals())
    shards = {n: given[n][0] for n, _, _, _ in SHARDED}

    gathered = _gather_weights(_flatten_shards(shards, BF16))
    wf = _unflatten_full(gathered.reshape(N_CHIPS, SHARD_ROWS, FLAT_COLS))

    loss_row, grad_x, grads, gain_grads = _local_step(x, mem, positions, loss_target, g_mix, g_mem_q, g_mem_kv,
                                                      g_ffn, g_final, wf)
    return _reduce_and_update(given, shards, loss_row, grad_x, grads, gain_grads)


def _local_step(x, mem, positions, loss_target, g_mix, g_mem_q, g_mem_kv, g_ffn, g_final, wf):
    b_dim, s_dim, d = x.shape
    t_dim = b_dim * s_dim
    n_mem = mem.shape[1]
    w_gu = jnp.concatenate([wf["w_ffn_gate"], wf["w_ffn_up"]], axis=1)

    xb = x.reshape(t_dim, d)
    tgt = loss_target.reshape(t_dim, d)
    memf = mem.reshape(b_dim * n_mem, d)
    gfin = g_final.reshape(1, d)
    pos = positions.reshape(t_dim, 1).astype(F32)

    lane = jnp.arange(LANES) % HEAD_DIM
    half = ROPE_DIM // 2
    inv_freq = ROPE_THETA ** (-jnp.arange(half, dtype=F32) / half)
    inv_lane = jnp.where(lane < ROPE_DIM, inv_freq[lane % half], 0.0).reshape(1, -1).astype(F32)
    sel_a = (lane < half).astype(F32).reshape(1, -1)
    sel_b = ((lane >= half) & (lane < ROPE_DIM)).astype(F32).reshape(1, -1)

    def rows3(t):
        return t.reshape(b_dim, s_dim, t.shape[-1])

    def rows2(t):
        return t.reshape(t_dim, t.shape[-1])

    n1 = _rms_fwd(xb, g_mix, name="rms_mix")
    proj = _mm(n1, wf["w_in"], name="mm_in")
    proj3 = rows3(proj)
    cs, sn = _rope_table(pos, inv_lane, sel_a, sel_b)
    cs3, sn3 = rows3(cs), rows3(sn)
    oa16, oa32, lse_a = _dil_fwd(proj3, cs3, sn3, sel_a, sel_b)
    ob16 = _sb_fwd(proj3)
    oa, ob = rows2(oa16), rows2(ob16)
    ua = _mm(oa, wf["w_up_a"], name="mm_up_a")
    ub = _mm(ob, wf["w_up_b"], name="mm_up_b")
    mixed = _gate_fwd(proj, ua, ub)
    h1 = _mm(mixed, wf["w_out"], name="mm_out", add=xb)

    hn = _rms_fwd(h1, g_mem_q, name="rms_mem_q")
    memn = _rms_fwd(memf, g_mem_kv, name="rms_mem_kv")
    qm = _mm(hn, wf["w_q_mem"], name="mm_q_mem", out_dtype=BF16)
    kvm = _mm(memn, wf["w_kv_mem"], name="mm_kv_mem", out_dtype=BF16)
    qm3, kvm3 = rows3(qm), kvm.reshape(b_dim, n_mem, 2 * MEM_WIDTH)
    om = rows2(_mem_fwd(qm3, kvm3))
    h2 = _mm(om, wf["w_o_mem"], name="mm_o_mem", add=h1)

    n3 = _rms_fwd(h2, g_ffn, name="rms_ffn")
    gu = _mm(n3, w_gu, name="mm_gate_up")
    act = _swiglu_fwd(gu)
    h3 = _mm(act, wf["w_ffn_down"], name="mm_down", add=h2)
    loss_row, dh3, dg_final = _final(h3, gfin, tgt)

    grads = {}
    dact = _mm(dh3, wf["w_ffn_down"], name="mm_down_dx", tb=True)
    grads["w_ffn_down"] = _mm(act, dh3, name="mm_down_dw", ta=True)
    dgu = _swiglu_bwd(gu, dact)
    dw_gu = _mm(n3, dgu, name="mm_gate_up_dw", ta=True)
    grads["w_ffn_gate"], grads["w_ffn_up"] = dw_gu[:, :D_FF], dw_gu[:, D_FF:]
    dn3 = _mm(dgu, w_gu, name="mm_gate_up_dx", tb=True)
    dh2, dg_ffn = _rms_bwd(h2, g_ffn, dn3, dh3, name="rms_ffn_bwd")

    dom = _mm(dh2, wf["w_o_mem"], name="mm_o_mem_dx", tb=True, out_dtype=BF16)
    grads["w_o_mem"] = _mm(om, dh2, name="mm_o_mem_dw", ta=True)
    dqm, dkm, dvm = _mem_bwd(qm3, kvm3, rows3(dom))
    dqm = rows2(dqm)
    dkvm = jnp.concatenate([dkm, dvm], axis=-1).reshape(b_dim * n_mem, 2 * MEM_WIDTH).astype(BF16)
    grads["w_q_mem"] = _mm(hn, dqm, name="mm_q_mem_dw", ta=True)
    dhn = _mm(dqm, wf["w_q_mem"], name="mm_q_mem_dx", tb=True)
    grads["w_kv_mem"] = _mm(memn, dkvm, name="mm_kv_mem_dw", ta=True)
    dmemn = _mm(dkvm, wf["w_kv_mem"], name="mm_kv_mem_dx", tb=True)
    _, dg_mem_kv = _rms_bwd(memf, g_mem_kv, dmemn, None, name="rms_mem_kv_bwd")
    dh1, dg_mem_q = _rms_bwd(h1, g_mem_q, dhn, dh2, name="rms_mem_q_bwd")

    dmix = _mm(dh1, wf["w_out"], name="mm_out_dx", tb=True)
    grads["w_out"] = _mm(mixed, dh1, name="mm_out_dw", ta=True)
    dua, dub, dgates = _gate_bwd(proj, ua, ub, dmix)
    doa = _mm(dua, wf["w_up_a"], name="mm_up_a_dx", tb=True)
    grads["w_up_a"] = _mm(oa, dua, name="mm_up_a_dw", ta=True)
    dob = _mm(dub, wf["w_up_b"], name="mm_up_b_dx", tb=True, out_dtype=BF16)
    grads["w_up_b"] = _mm(ob, dub, name="mm_up_b_dw", ta=True)

    dq_b, dk_b, dv_b = _sb_bwd(proj3, rows3(dob))
    dq_a, dk_a, dv_a = _dil_bwd(proj3, cs3, sn3, sel_a, sel_b, rows3(doa), oa32, lse_a)
    dproj = jnp.concatenate([rows2(t) for t in (dq_a, dk_a, dv_a, dq_b, dk_b, dv_b)] + [dgates], axis=1)
    grads["w_in"] = _mm(n1, dproj, name="mm_in_dw", ta=True)
    dn1 = _mm(dproj, wf["w_in"], name="mm_in_dx", tb=True)
    dx, dg_mix = _rms_bwd(xb, g_mix, dn1, dh1, name="rms_mix_bwd")
    grad_x = dx.reshape(b_dim, s_dim, d)
    return loss_row, grad_x, grads, (dg_mix, dg_mem_q, dg_mem_kv, dg_ffn, dg_final)


def _reduce_and_update(given, shards, loss_row, grad_x, grads, gain_grads):
    d = D_MODEL
    dg_mix, dg_mem_q, dg_mem_kv, dg_ffn, dg_final = gain_grads
    g4 = _flatten_full(grads, F32)
    half = SHARD_ROWS // 2
    c = lax.axis_index("c")
    theirs = _pair_exchange(g4)
    mine = lax.dynamic_slice_in_dim(g4, c * half, half, axis=1)
    pair = _add(mine.reshape(N_CHIPS * half, FLAT_COLS), theirs.reshape(N_CHIPS * half, FLAT_COLS), name="grad_pair_sum")
    parts = _chip_exchange(pair.reshape(N_CHIPS, half, FLAT_COLS))
    my_half = _sum4(parts, name="grad_chip_sum")
    small = jnp.concatenate([dg_mix, dg_mem_q, dg_mem_kv, dg_ffn, dg_final,
                             jnp.pad(loss_row, ((0, 0), (0, FLAT_COLS - LANES))), jnp.zeros((2, FLAT_COLS), F32)], axis=0)
    shard_flat, small_all = _share(my_half, small)
    small_sum = _sum8(small_all)
    loss = small_sum[5, 0]
    gsh = _unflatten_shard(shard_flat)

    out_g, out_d, out_m, out_v = {}, {}, {}, {}
    for n, _, _, _ in SHARDED:
        w2, g2 = shards[n], gsh[n]
        dl, nm, nv = _adamw(w2, g2, given["m_" + n][0], given["v_" + n][0], name="adamw_" + n)
        out_g[n], out_d[n], out_m[n], out_v[n] = g2[None], dl[None], nm[None], nv[None]
    gain_w = jnp.concatenate([given[n].reshape(1, d) for n in GAINS], axis=0)
    gain_m = jnp.concatenate([given["m_" + n].reshape(1, d) for n in GAINS], axis=0)
    gain_v = jnp.concatenate([given["v_" + n].reshape(1, d) for n in GAINS], axis=0)
    gain_g = small_sum[:len(GAINS)]
    gd, gm, gv = _adamw(gain_w, gain_g, gain_m, gain_v, name="adamw_gains")
    for i, n in enumerate(GAINS):
        shape = given[n].shape
        out_g[n], out_d[n] = gain_g[i].reshape(shape), gd[i].reshape(shape)
        out_m[n], out_v[n] = gm[i].reshape(shape), gv[i].reshape(shape)

    order = ["g_mix", "w_in", "w_up_a", "w_up_b", "w_out", "g_mem_q", "g_mem_kv", "w_q_mem", "w_kv_mem", "w_o_mem",
             "g_ffn", "w_ffn_gate", "w_ffn_up", "w_ffn_down", "g_final"]
    return (loss, grad_x, *[out_g[n] for n in order], *[out_d[n] for n in order],
            *[out_m[n] for n in order], *[out_v[n] for n in order])
```

```python
import jax
import jax.numpy as jnp
from jax import lax
from jax.experimental import pallas as pl
from jax.experimental.pallas import tpu as pltpu

F32 = jnp.float32
BF16 = jnp.bfloat16
MESH = pl.DeviceIdType.MESH

D_MODEL = 1024
HEAD_DIM = 64
N_HEADS = 8
ATT_WIDTH = N_HEADS * HEAD_DIM
DIL_PATTERNS = ((128, 1), (512, 4), (2048, 16))
BLOCK = 128
SB_ROWS = 256
ROPE_THETA = 500000.0
ROPE_DIM = HEAD_DIM // 4
N_HEADS_MEM = 4
MEM_HEAD_DIM = 128
MEM_WIDTH = N_HEADS_MEM * MEM_HEAD_DIM
D_FF = 2816
IN_COLS = 6 * ATT_WIDTH + 2 * D_MODEL
RMS_EPS = 1e-6
ADAM_LR = 0.001
ADAM_B1 = 0.9
ADAM_B2 = 0.999
ADAM_EPS = 1e-08
ADAM_WD = 0.01
ADAM_STEP = 10

N_CHIPS = 4
LANES = 128
FLAT_COLS = 1024
VMEM_LIMIT = 56 * 1024 * 1024

PAIRS = ATT_WIDTH // LANES
COL_QA, COL_KA, COL_VA, COL_QB, COL_KB, COL_VB = (i * PAIRS for i in range(6))

NT = (((1,), (1,)), ((), ()))
TN = (((0,), (0,)), ((), ()))


def _tile(dim, cap, unit=LANES):
    if dim <= cap:
        return dim
    best = None
    for t in range(unit, cap + 1, unit):
        if dim % t == 0:
            best = t
    assert best is not None, (dim, cap)
    return best


def _params(sem):
    return pltpu.CompilerParams(dimension_semantics=sem, vmem_limit_bytes=VMEM_LIMIT)


def _mm(a, b, *, name, ta=False, tb=False, add=None, out_dtype=F32,
        tm_cap=1024, tn_cap=1024, tk_cap=1024):
    if ta:
        k_dim, m_dim = a.shape
    else:
        m_dim, k_dim = a.shape
    if tb:
        n_dim, kb = b.shape
    else:
        kb, n_dim = b.shape
    assert kb == k_dim, (a.shape, b.shape, ta, tb)
    tm, tn, tk = _tile(m_dim, tm_cap), _tile(n_dim, tn_cap), _tile(k_dim, tk_cap)
    nk = k_dim // tk
    dims = (((0 if ta else 1,), (1 if tb else 0,)), ((), ()))
    has_add = add is not None

    def body(*refs):
        if has_add:
            a_ref, b_ref, add_ref, o_ref = refs[:4]
        else:
            a_ref, b_ref, o_ref = refs[:3]
        part = lax.dot_general(a_ref[...].astype(BF16), b_ref[...].astype(BF16), dims, preferred_element_type=F32)

        def finish(r):
            if has_add:
                r = add_ref[...] + r
            o_ref[...] = r.astype(out_dtype)

        if nk == 1:
            finish(part)
            return
        acc_ref = refs[-1]
        k = pl.program_id(2)

        @pl.when(k == 0)
        def _():
            acc_ref[...] = part

        @pl.when(k > 0)
        def _():
            acc_ref[...] += part

        @pl.when(k == nk - 1)
        def _():
            finish(acc_ref[...])

    a_spec = pl.BlockSpec((tk, tm), lambda i, j, k: (k, i)) if ta else pl.BlockSpec((tm, tk), lambda i, j, k: (i, k))
    b_spec = pl.BlockSpec((tn, tk), lambda i, j, k: (j, k)) if tb else pl.BlockSpec((tk, tn), lambda i, j, k: (k, j))
    o_spec = pl.BlockSpec((tm, tn), lambda i, j, k: (i, j))
    in_specs = [a_spec, b_spec] + ([o_spec] if has_add else [])
    args = (a, b) + ((add,) if has_add else ())
    return pl.pallas_call(
        body, name=name, grid=(m_dim // tm, n_dim // tn, nk),
        in_specs=in_specs, out_specs=o_spec,
        out_shape=jax.ShapeDtypeStruct((m_dim, n_dim), out_dtype),
        scratch_shapes=[pltpu.VMEM((tm, tn), F32)] if nk > 1 else [],
        compiler_params=_params(("parallel", "parallel", "arbitrary")),
    )(*args)


def _rms_fwd(x, g, *, name, tt=512):
    t_dim, d = x.shape
    tt = _tile(t_dim, tt, 8)

    def body(x_ref, g_ref, o_ref):
        xv = x_ref[...]
        r = lax.rsqrt(jnp.mean(xv * xv, axis=-1, keepdims=True) + RMS_EPS)
        o_ref[...] = ((xv * r) * g_ref[...]).astype(o_ref.dtype)

    return pl.pallas_call(
        body, name=name, grid=(t_dim // tt,),
        in_specs=[pl.BlockSpec((tt, d), lambda i: (i, 0)), pl.BlockSpec((1, d), lambda i: (0, 0))],
        out_specs=pl.BlockSpec((tt, d), lambda i: (i, 0)),
        out_shape=jax.ShapeDtypeStruct((t_dim, d), BF16),
        compiler_params=_params(("parallel",)),
    )(x, g)


def _rms_bwd(x, g, dy, add, *, name, tt=512):
    t_dim, d = x.shape
    tt = _tile(t_dim, tt, 8)
    has_add = add is not None

    def body(*refs):
        if has_add:
            x_ref, g_ref, dy_ref, add_ref, dx_ref, dg_ref = refs
        else:
            x_ref, g_ref, dy_ref, dx_ref, dg_ref = refs
        xv = x_ref[...]
        dyv = dy_ref[...].astype(F32)
        r = lax.rsqrt(jnp.mean(xv * xv, axis=-1, keepdims=True) + RMS_EPS)
        xh = xv * r
        u = dyv * g_ref[...]
        dx = r * (u - xh * jnp.mean(u * xh, axis=-1, keepdims=True))
        if has_add:
            dx = add_ref[...] + dx
        dx_ref[...] = dx

        @pl.when(pl.program_id(0) == 0)
        def _():
            dg_ref[...] = jnp.zeros_like(dg_ref)

        dg_ref[...] += jnp.sum(dyv * xh, axis=0, keepdims=True)

    row = pl.BlockSpec((tt, d), lambda i: (i, 0))
    vec = pl.BlockSpec((1, d), lambda i: (0, 0))
    in_specs = [row, vec, row] + ([row] if has_add else [])
    args = (x, g, dy) + ((add,) if has_add else ())
    return pl.pallas_call(
        body, name=name, grid=(t_dim // tt,),
        in_specs=in_specs, out_specs=[row, vec],
        out_shape=[jax.ShapeDtypeStruct((t_dim, d), F32), jax.ShapeDtypeStruct((1, d), F32)],
        compiler_params=_params(("arbitrary",)),
    )(*args)


def _final(h, g, target, *, tt=512):
    t_dim, d = h.shape
    n_steps = t_dim // tt

    def body(h_ref, g_ref, t_ref, loss_ref, dh_ref, dg_ref, sq_ref):
        i = pl.program_id(0)
        xv = h_ref[...]
        gv = g_ref[...]
        r = lax.rsqrt(jnp.mean(xv * xv, axis=-1, keepdims=True) + RMS_EPS)
        xh = xv * r
        err = xh * gv - t_ref[...]
        dyv = err * (1.0 / d)
        u = dyv * gv
        dh_ref[...] = r * (u - xh * jnp.mean(u * xh, axis=-1, keepdims=True))

        @pl.when(i == 0)
        def _():
            dg_ref[...] = jnp.zeros_like(dg_ref)
            sq_ref[...] = jnp.zeros_like(sq_ref)

        dg_ref[...] += jnp.sum(dyv * xh, axis=0, keepdims=True)
        sq_ref[...] += jnp.sum(err * err, axis=0, keepdims=True)

        @pl.when(i == n_steps - 1)
        def _():
            total = jnp.sum(sq_ref[...], axis=-1, keepdims=True) * (0.5 / d)
            loss_ref[...] = jnp.broadcast_to(total, loss_ref.shape)

    row = pl.BlockSpec((tt, d), lambda i: (i, 0))
    vec = pl.BlockSpec((1, d), lambda i: (0, 0))
    return pl.pallas_call(
        body, name="final_loss", grid=(n_steps,),
        in_specs=[row, vec, row],
        out_specs=[pl.BlockSpec((1, LANES), lambda i: (0, 0)), row, vec],
        out_shape=[jax.ShapeDtypeStruct((1, LANES), F32), jax.ShapeDtypeStruct((t_dim, d), F32),
                   jax.ShapeDtypeStruct((1, d), F32)],
        scratch_shapes=[pltpu.VMEM((1, d), F32)],
        compiler_params=_params(("arbitrary",)),
    )(h, g, target)


def _rope_table(pos, inv_lane, sel_a, sel_b, *, tt=512):
    t_dim = pos.shape[0]

    def body(p_ref, f_ref, a_ref, b_ref, c_ref, s_ref):
        ang = p_ref[...] * f_ref[...]
        on = (a_ref[...] + b_ref[...]) > 0.0
        c_ref[...] = jnp.where(on, jnp.cos(ang), 1.0)
        s_ref[...] = jnp.where(on, jnp.sin(ang), 0.0)

    vec = pl.BlockSpec((1, LANES), lambda i: (0, 0))
    row = pl.BlockSpec((tt, LANES), lambda i: (i, 0))
    shp = jax.ShapeDtypeStruct((t_dim, LANES), F32)
    return pl.pallas_call(
        body, name="rope_table", grid=(t_dim // tt,),
        in_specs=[pl.BlockSpec((tt, 1), lambda i: (i, 0)), vec, vec, vec],
        out_specs=[row, row], out_shape=[shp, shp],
        compiler_params=_params(("parallel",)),
    )(pos, inv_lane, sel_a, sel_b)


def _rotate(xv, cs, sn, sa, sb):
    half = ROPE_DIM // 2
    up = pltpu.roll(xv, LANES - half, 1)
    dn = pltpu.roll(xv, half, 1)
    return xv * cs + (dn * sb - up * sa) * sn


def _head_masks():
    h1 = lax.broadcasted_iota(jnp.int32, (1, LANES), 1) < HEAD_DIM
    return h1, jnp.logical_not(h1)


def _split_heads(xv, h1, h2):
    return jnp.where(h1, xv, 0.0).astype(BF16), jnp.where(h2, xv, 0.0).astype(BF16)


def _tri_masks():
    r = lax.broadcasted_iota(jnp.int32, (BLOCK, BLOCK), 0)
    c = lax.broadcasted_iota(jnp.int32, (BLOCK, BLOCK), 1)
    return c <= r, r <= c


def _stream_rows(start, dil):
    if dil == 1:
        return pl.ds(pl.multiple_of(start, BLOCK), BLOCK)
    return pl.ds(start, BLOCK, stride=dil)


def _dil_specs(b_dim, s_dim):
    def col(c0):
        return pl.BlockSpec((None, s_dim, LANES),lambda b, h: (b, 0, c0 + h))
    tab = pl.BlockSpec((None, s_dim, LANES),lambda b, h: (b, 0, 0))
    vec = pl.BlockSpec((1, LANES), lambda b, h: (0, 0))
    return col, tab, vec


def _dil_fwd(proj3, cs3, sn3, sel_a, sel_b):
    b_dim, s_dim, _ = proj3.shape
    scale = HEAD_DIM ** -0.5
    n_pat = len(DIL_PATTERNS)

    def body(q_ref, k_ref, v_ref, cs_ref, sn_ref, sa_ref, sb_ref, o16_ref, o32_ref, l_ref, qr, kr, *per_pattern):
        og, lg = per_pattern[:n_pat], per_pattern[n_pat:]
        h1, h2 = _head_masks()
        cur_ok, prev_ok = _tri_masks()
        sa, sb = sa_ref[...], sb_ref[...]

        def prep(j, _):
            rows = pl.ds(pl.multiple_of(j * BLOCK, BLOCK), BLOCK)
            cs, sn = cs_ref[rows, :], sn_ref[rows, :]
            qr[rows, :] = _rotate(q_ref[rows, :], cs, sn, sa, sb) * scale
            kr[rows, :] = _rotate(k_ref[rows, :], cs, sn, sa, sb)
            return 0

        lax.fori_loop(0, s_dim // BLOCK, prep, 0)

        for g, (_, dil) in enumerate(DIL_PATTERNS):
            nb = s_dim // dil // BLOCK

            def one(idx, _, g=g, dil=dil, nb=nb):
                r = idx // nb
                n = idx % nb
                rows = _stream_rows(r + dil * BLOCK * n, dil)
                prow = _stream_rows(r + dil * BLOCK * jnp.maximum(n - 1, 0), dil)
                q1, q2 = _split_heads(qr[rows, :], h1, h2)
                kc = kr[rows, :].astype(BF16)
                vc1, vc2 = _split_heads(v_ref[rows, :], h1, h2)
                if nb > 1:
                    kp = kr[prow, :].astype(BF16)
                    vp1, vp2 = _split_heads(v_ref[prow, :], h1, h2)
                    p_ok = jnp.logical_and(prev_ok, n > 0)

                def head(qh, vch, vph):
                    sc = jnp.where(cur_ok, lax.dot_general(qh, kc, NT, preferred_element_type=F32), -jnp.inf)
                    m = jnp.max(sc, axis=-1, keepdims=True)
                    if nb > 1:
                        sp = jnp.where(p_ok, lax.dot_general(qh, kp, NT, preferred_element_type=F32), -jnp.inf)
                        m = jnp.maximum(m, jnp.max(sp, axis=-1, keepdims=True))
                    pc = jnp.exp(sc - m)
                    den = jnp.sum(pc, axis=-1, keepdims=True)
                    acc = jnp.dot(pc.astype(BF16), vch, preferred_element_type=F32)
                    if nb > 1:
                        pp = jnp.exp(sp - m)
                        den = den + jnp.sum(pp, axis=-1, keepdims=True)
                        acc = acc + jnp.dot(pp.astype(BF16), vph, preferred_element_type=F32)
                    return acc / den, m + jnp.log(den)

                o1, l1 = head(q1, vc1, vp1 if nb > 1 else None)
                o2, l2 = head(q2, vc2, vp2 if nb > 1 else None)
                og[g][rows, :] = o1 + o2
                lg[g][rows, :] = jnp.where(h1, l1, l2)
                return 0

            lax.fori_loop(0, dil * nb, one, 0, unroll=2)

        def comb(j, _):
            rows = pl.ds(pl.multiple_of(j * BLOCK, BLOCK), BLOCK)
            ls = [lg[g][rows, :] for g in range(n_pat)]
            m = jnp.maximum(jnp.maximum(ls[0], ls[1]), ls[2])
            es = [jnp.exp(l - m) for l in ls]
            den = es[0] + es[1] + es[2]
            o = (es[0] * og[0][rows, :] + es[1] * og[1][rows, :] + es[2] * og[2][rows, :]) / den
            o16_ref[rows, :] = o.astype(BF16)
            o32_ref[rows, :] = o
            l_ref[rows, :] = m + jnp.log(den)
            return 0

        lax.fori_loop(0, s_dim // BLOCK, comb, 0)

    col, tab, vec = _dil_specs(b_dim, s_dim)
    out = pl.BlockSpec((None, s_dim, LANES),lambda b, h: (b, 0, h))
    shp = (b_dim, s_dim, ATT_WIDTH)
    return pl.pallas_call(
        body, name="dil_fwd", grid=(b_dim, PAIRS),
        in_specs=[col(COL_QA), col(COL_KA), col(COL_VA), tab, tab, vec, vec],
        out_specs=[out, out, out],
        out_shape=[jax.ShapeDtypeStruct(shp, BF16), jax.ShapeDtypeStruct(shp, F32), jax.ShapeDtypeStruct(shp, F32)],
        scratch_shapes=[pltpu.VMEM((s_dim, LANES), F32)] * (2 + 2 * n_pat),
        compiler_params=_params(("parallel", "parallel")),
    )(proj3, proj3, proj3, cs3, sn3, sel_a, sel_b)


def _dil_bwd(proj3, cs3, sn3, sel_a, sel_b, do3, o3, lse3):
    b_dim, s_dim, _ = proj3.shape
    scale = HEAD_DIM ** -0.5

    def body(q_ref, k_ref, v_ref, cs_ref, sn_ref, sa_ref, sb_ref, do_ref, o_ref, l_ref,
             dq_ref, dk_ref, dv_ref, qr, kr, dqa, dka, dva):
        h1, h2 = _head_masks()
        cur_ok, prev_ok = _tri_masks()
        sa, sb = sa_ref[...], sb_ref[...]

        def prep(j, _):
            rows = pl.ds(pl.multiple_of(j * BLOCK, BLOCK), BLOCK)
            cs, sn = cs_ref[rows, :], sn_ref[rows, :]
            qr[rows, :] = _rotate(q_ref[rows, :], cs, sn, sa, sb) * scale
            kr[rows, :] = _rotate(k_ref[rows, :], cs, sn, sa, sb)
            zero = jnp.zeros((BLOCK, LANES), F32)
            dqa[rows, :] = zero
            dka[rows, :] = zero
            dva[rows, :] = zero
            return 0

        lax.fori_loop(0, s_dim // BLOCK, prep, 0)

        for _, dil in DIL_PATTERNS:
            nb = s_dim // dil // BLOCK

            def one(idx, _, dil=dil, nb=nb):
                r = idx // nb
                n = idx % nb
                rows = _stream_rows(r + dil * BLOCK * n, dil)
                prow = _stream_rows(r + dil * BLOCK * jnp.maximum(n - 1, 0), dil)
                q1, q2 = _split_heads(qr[rows, :], h1, h2)
                dof = do_ref[rows, :]
                do1, do2 = _split_heads(dof, h1, h2)
                prod = dof * o_ref[rows, :]
                delta1 = jnp.sum(jnp.where(h1, prod, 0.0), axis=-1, keepdims=True)
                delta2 = jnp.sum(jnp.where(h2, prod, 0.0), axis=-1, keepdims=True)
                lt = l_ref[rows, :]
                lse1 = jnp.max(jnp.where(h1, lt, -jnp.inf), axis=-1, keepdims=True)
                lse2 = jnp.max(jnp.where(h2, lt, -jnp.inf), axis=-1, keepdims=True)

                def side(krows, ok):
                    kf = kr[krows, :]
                    k16 = kf.astype(BF16)
                    k1, k2 = _split_heads(kf, h1, h2)
                    v16 = v_ref[krows, :].astype(BF16)

                    def head(qh, doh, lse, delta):
                        sc = lax.dot_general(qh, k16, NT, preferred_element_type=F32)
                        p = jnp.where(ok, jnp.exp(sc - lse), 0.0)
                        dp = lax.dot_general(doh, v16, NT, preferred_element_type=F32)
                        return p.astype(BF16), (p * (dp - delta)).astype(BF16)

                    p1, ds1 = head(q1, do1, lse1, delta1)
                    p2, ds2 = head(q2, do2, lse2, delta2)
                    dva[krows, :] += (lax.dot_general(p1, do1, TN, preferred_element_type=F32)
                                      + lax.dot_general(p2, do2, TN, preferred_element_type=F32))
                    dka[krows, :] += (lax.dot_general(ds1, q1, TN, preferred_element_type=F32)
                                      + lax.dot_general(ds2, q2, TN, preferred_element_type=F32))
                    return (jnp.dot(ds1, k1, preferred_element_type=F32)
                            + jnp.dot(ds2, k2, preferred_element_type=F32))

                dq = side(rows, cur_ok)
                if nb > 1:
                    dq = dq + side(prow, jnp.logical_and(prev_ok, n > 0))
                dqa[rows, :] += dq * scale
                return 0

            lax.fori_loop(0, dil * nb, one, 0, unroll=2)

        def finish(j, _):
            rows = pl.ds(pl.multiple_of(j * BLOCK, BLOCK), BLOCK)
            cs, sn = cs_ref[rows, :], -sn_ref[rows, :]
            dq_ref[rows, :] = _rotate(dqa[rows, :], cs, sn, sa, sb).astype(BF16)
            dk_ref[rows, :] = _rotate(dka[rows, :], cs, sn, sa, sb).astype(BF16)
            dv_ref[rows, :] = dva[rows, :].astype(BF16)
            return 0

        lax.fori_loop(0, s_dim // BLOCK, finish, 0)

    col, tab, vec = _dil_specs(b_dim, s_dim)
    out = pl.BlockSpec((None, s_dim, LANES),lambda b, h: (b, 0, h))
    shp = jax.ShapeDtypeStruct((b_dim, s_dim, ATT_WIDTH), BF16)
    acc = pltpu.VMEM((s_dim, LANES), F32)
    return pl.pallas_call(
        body, name="dil_bwd", grid=(b_dim, PAIRS),
        in_specs=[col(COL_QA), col(COL_KA), col(COL_VA), tab, tab, vec, vec, out, out, out],
        out_specs=[out, out, out], out_shape=[shp, shp, shp],
        scratch_shapes=[acc, acc, acc, acc, acc],
        compiler_params=_params(("parallel", "parallel")),
    )(proj3, proj3, proj3, cs3, sn3, sel_a, sel_b, do3, o3, lse3)


def _split_dot(x, tri):
    hi = x.astype(BF16)
    lo = (x - hi.astype(F32)).astype(BF16)
    return jnp.dot(hi, tri, preferred_element_type=F32) + jnp.dot(lo, tri, preferred_element_type=F32)


def _log_sigmoid(z):
    return jnp.minimum(z, 0.0) - jnp.log(1.0 + jnp.exp(-jnp.abs(z)))


def _sb_scores(qh, k16, valid):
    z = lax.dot_general(qh, k16, NT, preferred_element_type=F32)
    ls = _log_sigmoid(z)
    return ls, jnp.where(valid, ls - z, 0.0)


def _sb_consts():
    r = lax.broadcasted_iota(jnp.int32, (BLOCK, BLOCK), 0)
    c = lax.broadcasted_iota(jnp.int32, (BLOCK, BLOCK), 1)
    after = (r > c).astype(BF16)
    before = (r < c).astype(BF16)
    qrow = lax.broadcasted_iota(jnp.int32, (SB_ROWS, BLOCK), 0)
    kcol = lax.broadcasted_iota(jnp.int32, (SB_ROWS, BLOCK), 1)
    return after, before, qrow, kcol


def _sb_fwd(proj3):
    b_dim, s_dim, _ = proj3.shape
    scale = HEAD_DIM ** -0.5
    per = SB_ROWS // BLOCK

    def body(q_ref, k_ref, v_ref, o_ref):
        h1, h2 = _head_masks()
        after, _, qrow, kcol = _sb_consts()

        def qloop(qi, _):
            rows = pl.ds(pl.multiple_of(qi * SB_ROWS, SB_ROWS), SB_ROWS)
            q1, q2 = _split_heads(q_ref[rows, :] * scale, h1, h2)
            qpos = qi * SB_ROWS + qrow
            nkb = (qi + 1) * per

            def kloop(i, carry):
                acc, run1, run2 = carry
                kb = nkb - 1 - i
                krows = pl.ds(pl.multiple_of(kb * BLOCK, BLOCK), BLOCK)
                k16 = k_ref[krows, :].astype(BF16)
                v1, v2 = _split_heads(v_ref[krows, :], h1, h2)
                valid = (kb * BLOCK + kcol) < qpos

                def head(qh, vh, run):
                    ls, l1m = _sb_scores(qh, k16, valid)
                    a = jnp.where(valid, jnp.exp(ls + _split_dot(l1m, after) + run), 0.0)
                    return (jnp.dot(a.astype(BF16), vh, preferred_element_type=F32),
                            run + jnp.sum(l1m, axis=-1, keepdims=True))

                o1, run1 = head(q1, v1, run1)
                o2, run2 = head(q2, v2, run2)
                return acc + o1 + o2, run1, run2

            zcol = jnp.zeros((SB_ROWS, 1), F32)
            acc, _, _ = lax.fori_loop(0, nkb // 2, lambda i, c: kloop(2 * i + 1, kloop(2 * i, c)),
                                      (jnp.zeros((SB_ROWS, LANES), F32), zcol, zcol))
            o_ref[rows, :] = acc.astype(BF16)
            return 0

        lax.fori_loop(0, s_dim // SB_ROWS, qloop, 0)

    def col(c0):
        return pl.BlockSpec((None, s_dim, LANES),lambda b, h: (b, 0, c0 + h))

    return pl.pallas_call(
        body, name="sb_fwd", grid=(b_dim, PAIRS),
        in_specs=[col(COL_QB), col(COL_KB), col(COL_VB)], out_specs=col(0),
        out_shape=jax.ShapeDtypeStruct((b_dim, s_dim, ATT_WIDTH), BF16),
        compiler_params=_params(("parallel", "parallel")),
    )(proj3, proj3, proj3)


def _sb_bwd(proj3, do3):
    b_dim, s_dim, _ = proj3.shape
    scale = HEAD_DIM ** -0.5
    per = SB_ROWS // BLOCK
    nkb_max = s_dim // BLOCK

    def body(q_ref, k_ref, v_ref, do_ref, dq_ref, dk_ref, dv_ref, dka, dva, e_ref, sg_ref):
        h1, h2 = _head_masks()
        after, before, qrow, kcol = _sb_consts()
        dka[...] = jnp.zeros_like(dka)
        dva[...] = jnp.zeros_like(dva)

        def qloop(qi, _):
            rows = pl.ds(pl.multiple_of(qi * SB_ROWS, SB_ROWS), SB_ROWS)
            q1, q2 = _split_heads(q_ref[rows, :] * scale, h1, h2)
            do1, do2 = _split_heads(do_ref[rows, :].astype(F32), h1, h2)
            qpos = qi * SB_ROWS + qrow
            nkb = (qi + 1) * per

            def pass1(i, carry):
                run1, run2 = carry
                kb = nkb - 1 - i
                krows = pl.ds(pl.multiple_of(kb * BLOCK, BLOCK), BLOCK)
                k16 = k_ref[krows, :].astype(BF16)
                v16 = v_ref[krows, :].astype(BF16)
                valid = (kb * BLOCK + kcol) < qpos

                def head(h, qh, doh, run):
                    ls, l1m = _sb_scores(qh, k16, valid)
                    a = jnp.where(valid, jnp.exp(ls + _split_dot(l1m, after) + run), 0.0)
                    da = lax.dot_general(doh, v16, NT, preferred_element_type=F32)
                    e_ref[h, kb] = a * da
                    sg_ref[h, kb] = jnp.exp(ls)
                    return a.astype(BF16), run + jnp.sum(l1m, axis=-1, keepdims=True)

                a1, run1 = head(0, q1, do1, run1)
                a2, run2 = head(1, q2, do2, run2)
                dva[krows, :] += (lax.dot_general(a1, do1, TN, preferred_element_type=F32)
                                  + lax.dot_general(a2, do2, TN, preferred_element_type=F32))
                return run1, run2

            zcol = jnp.zeros((SB_ROWS, 1), F32)
            lax.fori_loop(0, nkb // 2, lambda i, c: pass1(2 * i + 1, pass1(2 * i, c)), (zcol, zcol))

            def pass2(kb, carry):
                dq, pre1, pre2 = carry
                krows = pl.ds(pl.multiple_of(kb * BLOCK, BLOCK), BLOCK)
                k1, k2 = _split_heads(k_ref[krows, :], h1, h2)
                valid = (kb * BLOCK + kcol) < qpos

                def head(h, pre):
                    ev = e_ref[h, kb]
                    sg = sg_ref[h, kb]
                    prefix = _split_dot(ev, before) + pre
                    dz = jnp.where(valid, ev * (1.0 - sg) - prefix * sg, 0.0).astype(BF16)
                    return dz, pre + jnp.sum(ev, axis=-1, keepdims=True)

                dz1, pre1 = head(0, pre1)
                dz2, pre2 = head(1, pre2)
                dka[krows, :] += (lax.dot_general(dz1, q1, TN, preferred_element_type=F32)
                                  + lax.dot_general(dz2, q2, TN, preferred_element_type=F32))
                dq = dq + jnp.dot(dz1, k1, preferred_element_type=F32) + jnp.dot(dz2, k2, preferred_element_type=F32)
                return dq, pre1, pre2

            dq, _, _ = lax.fori_loop(0, nkb // 2, lambda i, c: pass2(2 * i + 1, pass2(2 * i, c)),
                                     (jnp.zeros((SB_ROWS, LANES), F32), zcol, zcol))
            dq_ref[rows, :] = (dq * scale).astype(BF16)
            return 0

        lax.fori_loop(0, s_dim // SB_ROWS, qloop, 0)
        dk_ref[...] = dka[...].astype(BF16)
        dv_ref[...] = dva[...].astype(BF16)

    def col(c0):
        return pl.BlockSpec((None, s_dim, LANES),lambda b, h: (b, 0, c0 + h))

    shp = jax.ShapeDtypeStruct((b_dim, s_dim, ATT_WIDTH), BF16)
    acc = pltpu.VMEM((s_dim, LANES), F32)
    strip = pltpu.VMEM((2, nkb_max, SB_ROWS, BLOCK), F32)
    return pl.pallas_call(
        body, name="sb_bwd", grid=(b_dim, PAIRS),
        in_specs=[col(COL_QB), col(COL_KB), col(COL_VB), col(0)], out_specs=[col(0), col(0), col(0)],
        out_shape=[shp, shp, shp],
        scratch_shapes=[acc, acc, strip, strip],
        compiler_params=_params(("parallel", "parallel")),
    )(proj3, proj3, proj3, do3)


def _sigmoid(x):
    return 1.0 / (1.0 + jnp.exp(-x))


def _gate_fwd(proj, ua, ub, *, tt=512):
    t_dim, d = ua.shape

    def body(ga_ref, gb_ref, ua_ref, ub_ref, o_ref):
        o_ref[...] = (_sigmoid(ga_ref[...]) * ua_ref[...] + _sigmoid(gb_ref[...]) * ub_ref[...]).astype(BF16)

    row = pl.BlockSpec((tt, d), lambda i: (i, 0))
    return pl.pallas_call(
        body, name="gate_fwd", grid=(t_dim // tt,),
        in_specs=[pl.BlockSpec((tt, d), lambda i: (i, 3)), pl.BlockSpec((tt, d), lambda i: (i, 4)), row, row],
        out_specs=row, out_shape=jax.ShapeDtypeStruct((t_dim, d), BF16),
        compiler_params=_params(("parallel",)),
    )(proj, proj, ua, ub)


def _gate_bwd(proj, ua, ub, dmix, *, tt=512):
    t_dim, d = ua.shape

    def body(ga_ref, gb_ref, ua_ref, ub_ref, dm_ref, dua_ref, dub_ref, dg_ref):
        dm = dm_ref[...]
        sa = _sigmoid(ga_ref[...])
        sb = _sigmoid(gb_ref[...])
        dua_ref[...] = (dm * sa).astype(BF16)
        dub_ref[...] = (dm * sb).astype(BF16)
        dg_ref[:, :d] = (dm * ua_ref[...] * (sa * (1.0 - sa))).astype(BF16)
        dg_ref[:, d:] = (dm * ub_ref[...] * (sb * (1.0 - sb))).astype(BF16)

    row = pl.BlockSpec((tt, d), lambda i: (i, 0))
    wide = pl.BlockSpec((tt, 2 * d), lambda i: (i, 0))
    return pl.pallas_call(
        body, name="gate_bwd", grid=(t_dim // tt,),
        in_specs=[pl.BlockSpec((tt, d), lambda i: (i, 3)), pl.BlockSpec((tt, d), lambda i: (i, 4)), row, row, row],
        out_specs=[row, row, wide],
        out_shape=[jax.ShapeDtypeStruct((t_dim, d), BF16), jax.ShapeDtypeStruct((t_dim, d), BF16),
                   jax.ShapeDtypeStruct((t_dim, 2 * d), BF16)],
        compiler_params=_params(("parallel",)),
    )(proj, proj, ua, ub, dmix)


def _swiglu_fwd(gu, *, tt=256):
    t_dim = gu.shape[0]

    def body(g_ref, u_ref, o_ref):
        gv = g_ref[...]
        o_ref[...] = (gv * _sigmoid(gv) * u_ref[...]).astype(BF16)

    return pl.pallas_call(
        body, name="swiglu_fwd", grid=(t_dim // tt,),
        in_specs=[pl.BlockSpec((tt, D_FF), lambda i: (i, 0)), pl.BlockSpec((tt, D_FF), lambda i: (i, 1))],
        out_specs=pl.BlockSpec((tt, D_FF), lambda i: (i, 0)),
        out_shape=jax.ShapeDtypeStruct((t_dim, D_FF), BF16),
        compiler_params=_params(("parallel",)),
    )(gu, gu)


def _swiglu_bwd(gu, dact, *, tt=256):
    t_dim = gu.shape[0]

    def body(g_ref, u_ref, da_ref, o_ref):
        gv = g_ref[...]
        da = da_ref[...]
        sg = _sigmoid(gv)
        o_ref[:, :D_FF] = (da * u_ref[...] * (sg + gv * sg * (1.0 - sg))).astype(BF16)
        o_ref[:, D_FF:] = (da * (gv * sg)).astype(BF16)

    return pl.pallas_call(
        body, name="swiglu_bwd", grid=(t_dim // tt,),
        in_specs=[pl.BlockSpec((tt, D_FF), lambda i: (i, 0)), pl.BlockSpec((tt, D_FF), lambda i: (i, 1)),
                  pl.BlockSpec((tt, D_FF), lambda i: (i, 0))],
        out_specs=pl.BlockSpec((tt, 2 * D_FF), lambda i: (i, 0)),
        out_shape=jax.ShapeDtypeStruct((t_dim, 2 * D_FF), BF16),
        compiler_params=_params(("parallel",)),
    )(gu, gu, dact)


def _mem_fwd(qm, kvm, *, tt=512):
    b_dim, s_dim, _ = qm.shape
    n_mem = kvm.shape[1]
    scale = MEM_HEAD_DIM ** -0.5

    def body(q_ref, k_ref, v_ref, o_ref):
        sc = lax.dot_general(q_ref[0], k_ref[0], NT, preferred_element_type=F32) * scale
        p = jnp.exp(sc - jnp.max(sc, axis=-1, keepdims=True))
        p = p / jnp.sum(p, axis=-1, keepdims=True)
        o_ref[0] = jnp.dot(p.astype(BF16), v_ref[0], preferred_element_type=F32).astype(BF16)

    qs = pl.BlockSpec((1, tt, MEM_HEAD_DIM), lambda b, h, i: (b, i, h))
    return pl.pallas_call(
        body, name="mem_fwd", grid=(b_dim, N_HEADS_MEM, s_dim // tt),
        in_specs=[qs, pl.BlockSpec((1, n_mem, MEM_HEAD_DIM), lambda b, h, i: (b, 0, h)),
                  pl.BlockSpec((1, n_mem, MEM_HEAD_DIM), lambda b, h, i: (b, 0, N_HEADS_MEM + h))],
        out_specs=qs, out_shape=jax.ShapeDtypeStruct(qm.shape, BF16),
        compiler_params=_params(("parallel", "parallel", "parallel")),
    )(qm, kvm, kvm)


def _mem_bwd(qm, kvm, dom, *, tt=512):
    b_dim, s_dim, _ = qm.shape
    n_mem = kvm.shape[1]
    scale = MEM_HEAD_DIM ** -0.5

    def body(q_ref, k_ref, v_ref, do_ref, dq_ref, dk_ref, dv_ref):
        qv, kv, vv, dov = q_ref[0], k_ref[0], v_ref[0], do_ref[0]
        sc = lax.dot_general(qv, kv, NT, preferred_element_type=F32) * scale
        p = jnp.exp(sc - jnp.max(sc, axis=-1, keepdims=True))
        p = p / jnp.sum(p, axis=-1, keepdims=True)
        dp = lax.dot_general(dov, vv, NT, preferred_element_type=F32)
        ds = (p * (dp - jnp.sum(p * dp, axis=-1, keepdims=True)) * scale).astype(BF16)
        dq_ref[0] = jnp.dot(ds, kv, preferred_element_type=F32).astype(BF16)

        @pl.when(pl.program_id(2) == 0)
        def _():
            dk_ref[...] = jnp.zeros_like(dk_ref)
            dv_ref[...] = jnp.zeros_like(dv_ref)

        dk_ref[0] += lax.dot_general(ds, qv, TN, preferred_element_type=F32)
        dv_ref[0] += lax.dot_general(p.astype(BF16), dov, TN, preferred_element_type=F32)

    qs = pl.BlockSpec((1, tt, MEM_HEAD_DIM), lambda b, h, i: (b, i, h))
    ks = pl.BlockSpec((1, n_mem, MEM_HEAD_DIM), lambda b, h, i: (b, 0, h))
    vs = pl.BlockSpec((1, n_mem, MEM_HEAD_DIM), lambda b, h, i: (b, 0, N_HEADS_MEM + h))
    return pl.pallas_call(
        body, name="mem_bwd", grid=(b_dim, N_HEADS_MEM, s_dim // tt),
        in_specs=[qs, ks, vs, qs], out_specs=[qs, ks, ks],
        out_shape=[jax.ShapeDtypeStruct(qm.shape, BF16), jax.ShapeDtypeStruct((b_dim, n_mem, MEM_WIDTH), F32),
                   jax.ShapeDtypeStruct((b_dim, n_mem, MEM_WIDTH), F32)],
        compiler_params=_params(("parallel", "parallel", "arbitrary")),
    )(qm, kvm, kvm, dom)


def _adamw(w, g, m, v, *, name):
    rows, cols = w.shape
    tr = _tile(rows, 256, 8)

    def body(w_ref, g_ref, m_ref, v_ref, d_ref, nm_ref, nv_ref):
        gv = g_ref[...]
        nm = ADAM_B1 * m_ref[...] + (1.0 - ADAM_B1) * gv
        nv = ADAM_B2 * v_ref[...] + (1.0 - ADAM_B2) * (gv * gv)
        m_hat = nm / (1.0 - ADAM_B1 ** ADAM_STEP)
        v_hat = nv / (1.0 - ADAM_B2 ** ADAM_STEP)
        d_ref[...] = -ADAM_LR * (m_hat / (jnp.sqrt(v_hat) + ADAM_EPS) + ADAM_WD * w_ref[...])
        nm_ref[...] = nm
        nv_ref[...] = nv

    spec = pl.BlockSpec((tr, cols), lambda i: (i, 0))
    shp = jax.ShapeDtypeStruct((rows, cols), F32)
    return pl.pallas_call(
        body, name=name, grid=(rows // tr,),
        in_specs=[spec] * 4, out_specs=[spec] * 3, out_shape=[shp] * 3,
        compiler_params=_params(("parallel",)),
    )(w, g, m, v)


def _add(a, b, *, name, out_dtype, tr=256):
    rows, cols = a.shape
    tr = _tile(rows, tr, 16)

    def body(a_ref, b_ref, o_ref):
        o_ref[...] = (a_ref[...] + b_ref[...]).astype(out_dtype)

    spec = pl.BlockSpec((tr, cols), lambda i: (i, 0))
    return pl.pallas_call(
        body, name=name, grid=(rows // tr,), in_specs=[spec, spec], out_specs=spec,
        out_shape=jax.ShapeDtypeStruct((rows, cols), out_dtype),
        compiler_params=_params(("parallel",)),
    )(a, b)


def _sum4(parts, *, name, tr=256):
    _, rows, cols = parts.shape
    tr = _tile(rows, tr, 16)

    def body(p_ref, o_ref):
        p = [p_ref[i].astype(F32) for i in range(N_CHIPS)]
        o_ref[...] = ((p[0] + p[1]) + p[2]) + p[3]

    return pl.pallas_call(
        body, name=name, grid=(rows // tr,),
        in_specs=[pl.BlockSpec((N_CHIPS, tr, cols), lambda i: (0, i, 0))],
        out_specs=pl.BlockSpec((tr, cols), lambda i: (i, 0)),
        out_shape=jax.ShapeDtypeStruct((rows, cols), F32),
        compiler_params=_params(("parallel",)),
    )(parts)


def _sum8(parts):
    n, rows, cols = parts.shape

    def body(p_ref, o_ref):
        acc = p_ref[0]
        for i in range(1, n):
            acc = acc + p_ref[i]
        o_ref[...] = acc

    return pl.pallas_call(
        body, name="small_sum", grid=(1,),
        in_specs=[pl.BlockSpec((n, rows, cols), lambda i: (0, 0, 0))],
        out_specs=pl.BlockSpec((rows, cols), lambda i: (0, 0)),
        out_shape=jax.ShapeDtypeStruct((rows, cols), parts.dtype),
        compiler_params=_params(("arbitrary",)),
    )(parts)


def _place():
    return lax.axis_index("x"), lax.axis_index("y"), lax.axis_index("c")


ANY = pl.BlockSpec(memory_space=pl.ANY)


def _gather_weights(flat):
    rows2, cols = flat.shape
    half = rows2 // 2

    def body(x_ref, out_ref, send_sems, recv_sems, local_sem):
        x, y, c = _place()
        me, sibling = (x, y, c), (x, y, 1 - c)
        chips = [(1 - x, y), (x, 1 - y), (1 - x, 1 - y)]
        mine_src = x_ref.at[pl.ds(c * half, half), :]

        def rows(px, py, pc):
            return out_ref.at[pl.ds((4 * px + 2 * py + pc) * half, half), :]

        def copy(k, block, to, src=None):
            return pltpu.make_async_remote_copy(
                src_ref=rows(*block) if src is None else src, dst_ref=rows(*block),
                send_sem=send_sems.at[k], recv_sem=recv_sems.at[k], device_id=to, device_id_type=MESH)

        mine = pltpu.make_async_copy(mine_src, rows(*me), local_sem)
        mine.start()
        first = [copy(0, me, sibling, src=mine_src)]
        first += [copy(1 + j, me, (*chip, c), src=mine_src) for j, chip in enumerate(chips)]
        for cp in first:
            cp.start()
        passed = [copy(4 + j, (*chip, c), sibling) for j, chip in enumerate(chips)]
        for j, chip in enumerate(chips):
            copy(1 + j, (*chip, c), me).wait_recv()
            passed[j].start()
        copy(0, sibling, me).wait_recv()
        for j, chip in enumerate(chips):
            copy(4 + j, (*chip, 1 - c), me).wait_recv()
        for cp in first + passed:
            cp.wait_send()
        mine.wait()

    return pl.pallas_call(
        body, name="gather_weights",
        out_shape=jax.ShapeDtypeStruct((8 * half, cols), flat.dtype),
        in_specs=[ANY], out_specs=ANY,
        scratch_shapes=[pltpu.SemaphoreType.DMA((7,)), pltpu.SemaphoreType.DMA((7,)), pltpu.SemaphoreType.DMA],
    )(flat)


def _pair_exchange(g):
    n, rows2, cols = g.shape
    half = rows2 // 2

    def body(g_ref, out_ref, send_sem, recv_sem):
        x, y, c = _place()
        cp = pltpu.make_async_remote_copy(
            src_ref=g_ref.at[:, pl.ds((1 - c) * half, half), :], dst_ref=out_ref,
            send_sem=send_sem, recv_sem=recv_sem, device_id=(x, y, 1 - c), device_id_type=MESH)
        cp.start()
        cp.wait()

    return pl.pallas_call(
        body, name="grad_pair_exchange",
        out_shape=jax.ShapeDtypeStruct((n, half, cols), g.dtype),
        in_specs=[ANY], out_specs=ANY,
        scratch_shapes=[pltpu.SemaphoreType.DMA, pltpu.SemaphoreType.DMA],
    )(g)


def _chip_exchange(a):
    n, half, cols = a.shape

    def body(a_ref, out_ref, send_sems, recv_sems, local_sem):
        x, y, c = _place()
        my_chip = 2 * x + y
        others = [(1 - x, y), (x, 1 - y), (1 - x, 1 - y)]
        mine = pltpu.make_async_copy(a_ref.at[my_chip], out_ref.at[my_chip], local_sem)
        mine.start()
        sends = []
        for j, (px, py) in enumerate(others):
            sends.append(pltpu.make_async_remote_copy(
                src_ref=a_ref.at[2 * px + py], dst_ref=out_ref.at[my_chip],
                send_sem=send_sems.at[j], recv_sem=recv_sems.at[j], device_id=(px, py, c), device_id_type=MESH))
        for cp in sends:
            cp.start()
        for j, (px, py) in enumerate(others):
            pltpu.make_async_remote_copy(
                src_ref=a_ref.at[my_chip], dst_ref=out_ref.at[2 * px + py],
                send_sem=send_sems.at[j], recv_sem=recv_sems.at[j], device_id=(px, py, c),
                device_id_type=MESH).wait_recv()
        for cp in sends:
            cp.wait_send()
        mine.wait()

    return pl.pallas_call(
        body, name="grad_chip_exchange",
        out_shape=jax.ShapeDtypeStruct((n, half, cols), a.dtype),
        in_specs=[ANY], out_specs=ANY,
        scratch_shapes=[pltpu.SemaphoreType.DMA((3,)), pltpu.SemaphoreType.DMA((3,)), pltpu.SemaphoreType.DMA],
    )(a)


def _swap_halves(h):
    half, cols = h.shape

    def body(h_ref, out_ref, send_sem, recv_sem, local_sem):
        x, y, c = _place()
        keep = pltpu.make_async_copy(h_ref, out_ref.at[pl.ds(c * half, half), :], local_sem)
        keep.start()
        swap = pltpu.make_async_remote_copy(
            src_ref=h_ref, dst_ref=out_ref.at[pl.ds(c * half, half), :],
            send_sem=send_sem, recv_sem=recv_sem, device_id=(x, y, 1 - c), device_id_type=MESH)
        swap.start()
        pltpu.make_async_remote_copy(
            src_ref=h_ref, dst_ref=out_ref.at[pl.ds((1 - c) * half, half), :],
            send_sem=send_sem, recv_sem=recv_sem, device_id=(x, y, 1 - c), device_id_type=MESH).wait_recv()
        swap.wait_send()
        keep.wait()

    return pl.pallas_call(
        body, name="grad_swap_halves",
        out_shape=jax.ShapeDtypeStruct((2 * half, cols), h.dtype),
        in_specs=[ANY], out_specs=ANY,
        scratch_shapes=[pltpu.SemaphoreType.DMA, pltpu.SemaphoreType.DMA, pltpu.SemaphoreType.DMA],
    )(h)


def _gather_small(small):
    srows, cols = small.shape

    def body(s_ref, all_ref, send_sems, recv_sems, local_sem):
        x, y, c = _place()
        me = 4 * x + 2 * y + c
        keep_small = pltpu.make_async_copy(s_ref, all_ref.at[me], local_sem)
        keep_small.start()
        sends = []
        for kk in range(1, 8):
            peer = (x ^ (kk >> 2), y ^ ((kk >> 1) & 1), c ^ (kk & 1))
            sends.append(pltpu.make_async_remote_copy(
                src_ref=s_ref, dst_ref=all_ref.at[me],
                send_sem=send_sems.at[kk], recv_sem=recv_sems.at[kk], device_id=peer, device_id_type=MESH))
        for cp in sends:
            cp.start()
        for kk in range(1, 8):
            px, py, pc = x ^ (kk >> 2), y ^ ((kk >> 1) & 1), c ^ (kk & 1)
            pltpu.make_async_remote_copy(
                src_ref=s_ref, dst_ref=all_ref.at[4 * px + 2 * py + pc],
                send_sem=send_sems.at[kk], recv_sem=recv_sems.at[kk], device_id=(px, py, pc),
                device_id_type=MESH).wait_recv()
        for cp in sends:
            cp.wait_send()
        keep_small.wait()

    return pl.pallas_call(
        body, name="gather_small",
        out_shape=jax.ShapeDtypeStruct((8, srows, cols), small.dtype),
        in_specs=[ANY], out_specs=ANY,
        scratch_shapes=[pltpu.SemaphoreType.DMA((8,)), pltpu.SemaphoreType.DMA((8,)), pltpu.SemaphoreType.DMA],
    )(small)


SHARDED = (("w_in", D_MODEL, IN_COLS, 1), ("w_up_a", ATT_WIDTH, D_MODEL, 1), ("w_up_b", ATT_WIDTH, D_MODEL, 1),
           ("w_out", D_MODEL, D_MODEL, 0), ("w_q_mem", D_MODEL, MEM_WIDTH, 0), ("w_kv_mem", D_MODEL, 2 * MEM_WIDTH, 0),
           ("w_o_mem", MEM_WIDTH, D_MODEL, 1), ("w_ffn_gate", D_MODEL, D_FF, 1), ("w_ffn_up", D_MODEL, D_FF, 1),
           ("w_ffn_down", D_FF, D_MODEL, 0))
SHARD_ELEMS = sum(r * c for _, r, c, _ in SHARDED) // N_CHIPS
SHARD_ROWS = SHARD_ELEMS // FLAT_COLS
assert SHARD_ROWS * FLAT_COLS == SHARD_ELEMS and SHARD_ROWS % 32 == 0
GAINS = ("g_mix", "g_mem_q", "g_mem_kv", "g_ffn", "g_final")


def _flatten_shards(shards, dtype):
    return jnp.concatenate([shards[n].astype(dtype).reshape(-1) for n, _, _, _ in SHARDED]).reshape(SHARD_ROWS, FLAT_COLS)


def _unflatten_full(flat4):
    flat4 = flat4.reshape(N_CHIPS, SHARD_ELEMS)
    out, off = {}, 0
    for n, r, c, axis in SHARDED:
        size = r * c // N_CHIPS
        piece = flat4[:, off:off + size]
        off += size
        if axis == 0:
            out[n] = piece.reshape(r, c)
        else:
            out[n] = piece.reshape(N_CHIPS, r, c // N_CHIPS).transpose(1, 0, 2).reshape(r, c)
    return out


def _flatten_full(full, dtype):
    pieces = []
    for n, r, c, axis in SHARDED:
        gfull = full[n].astype(dtype)
        if axis == 0:
            pieces.append(gfull.reshape(N_CHIPS, r * c // N_CHIPS))
        else:
            pieces.append(gfull.reshape(r, N_CHIPS, c // N_CHIPS).transpose(1, 0, 2).reshape(N_CHIPS, r * c // N_CHIPS))
    return jnp.concatenate(pieces, axis=1).reshape(N_CHIPS, SHARD_ROWS, FLAT_COLS)


def _unflatten_shard(flat):
    flat = flat.reshape(-1)
    out, off = {}, 0
    for n, r, c, axis in SHARDED:
        size = r * c // N_CHIPS
        shape = (r // N_CHIPS, c) if axis == 0 else (r, c // N_CHIPS)
        out[n] = flat[off:off + size].reshape(shape)
        off += size
    return out


def kernel(x, mem, positions, g_mix, w_in, w_up_a, w_up_b, w_out, g_mem_q, g_mem_kv, w_q_mem, w_kv_mem, w_o_mem, g_ffn, w_ffn_gate, w_ffn_up, w_ffn_down, g_final, loss_target, m_g_mix, m_w_in, m_w_up_a, m_w_up_b, m_w_out, m_g_mem_q, m_g_mem_kv, m_w_q_mem, m_w_kv_mem, m_w_o_mem, m_g_ffn, m_w_ffn_gate, m_w_ffn_up, m_w_ffn_down, m_g_final, v_g_mix, v_w_in, v_w_up_a, v_w_up_b, v_w_out, v_g_mem_q, v_g_mem_kv, v_w_q_mem, v_w_kv_mem, v_w_o_mem, v_g_ffn, v_w_ffn_gate, v_w_ffn_up, v_w_ffn_down, v_g_final):
    given = dict(locals())
    shards = {n: given[n][0] for n, _, _, _ in SHARDED}

    gathered = _gather_weights(_flatten_shards(shards, BF16))
    wf = _unflatten_full(gathered.reshape(N_CHIPS, SHARD_ROWS, FLAT_COLS))

    loss_row, grad_x, grads, gain_grads = _local_step(x, mem, positions, loss_target, g_mix, g_mem_q, g_mem_kv,
                                                      g_ffn, g_final, wf)
    return _reduce_and_update(given, shards, loss_row, grad_x, grads, gain_grads)


def _local_step(x, mem, positions, loss_target, g_mix, g_mem_q, g_mem_kv, g_ffn, g_final, wf):
    b_dim, s_dim, d = x.shape
    t_dim = b_dim * s_dim
    n_mem = mem.shape[1]
    w_gu = jnp.concatenate([wf["w_ffn_gate"], wf["w_ffn_up"]], axis=1)

    xb = x.reshape(t_dim, d)
    tgt = loss_target.reshape(t_dim, d)
    memf = mem.reshape(b_dim * n_mem, d)
    gfin = g_final.reshape(1, d)
    pos = positions.reshape(t_dim, 1).astype(F32)

    lane = jnp.arange(LANES) % HEAD_DIM
    half = ROPE_DIM // 2
    inv_freq = ROPE_THETA ** (-jnp.arange(half, dtype=F32) / half)
    inv_lane = jnp.where(lane < ROPE_DIM, inv_freq[lane % half], 0.0).reshape(1, -1).astype(F32)
    sel_a = (lane < half).astype(F32).reshape(1, -1)
    sel_b = ((lane >= half) & (lane < ROPE_DIM)).astype(F32).reshape(1, -1)

    def rows3(t):
        return t.reshape(b_dim, s_dim, t.shape[-1])

    def rows2(t):
        return t.reshape(t_dim, t.shape[-1])

    n1 = _rms_fwd(xb, g_mix, name="rms_mix")
    proj = _mm(n1, wf["w_in"], name="mm_in")
    proj3 = rows3(proj)
    cs, sn = _rope_table(pos, inv_lane, sel_a, sel_b)
    cs3, sn3 = rows3(cs), rows3(sn)
    oa16, oa32, lse_a = _dil_fwd(proj3, cs3, sn3, sel_a, sel_b)
    ob16 = _sb_fwd(proj3)
    oa, ob = rows2(oa16), rows2(ob16)
    ua = _mm(oa, wf["w_up_a"], name="mm_up_a")
    ub = _mm(ob, wf["w_up_b"], name="mm_up_b")
    mixed = _gate_fwd(proj, ua, ub)
    h1 = _mm(mixed, wf["w_out"], name="mm_out", add=xb)

    hn = _rms_fwd(h1, g_mem_q, name="rms_mem_q")
    memn = _rms_fwd(memf, g_mem_kv, name="rms_mem_kv")
    qm = _mm(hn, wf["w_q_mem"], name="mm_q_mem", out_dtype=BF16)
    kvm = _mm(memn, wf["w_kv_mem"], name="mm_kv_mem", out_dtype=BF16)
    qm3, kvm3 = rows3(qm), kvm.reshape(b_dim, n_mem, 2 * MEM_WIDTH)
    om = rows2(_mem_fwd(qm3, kvm3))
    h2 = _mm(om, wf["w_o_mem"], name="mm_o_mem", add=h1)

    n3 = _rms_fwd(h2, g_ffn, name="rms_ffn")
    gu = _mm(n3, w_gu, name="mm_gate_up")
    act = _swiglu_fwd(gu)
    h3 = _mm(act, wf["w_ffn_down"], name="mm_down", add=h2)
    loss_row, dh3, dg_final = _final(h3, gfin, tgt)

    grads = {}
    dact = _mm(dh3, wf["w_ffn_down"], name="mm_down_dx", tb=True)
    grads["w_ffn_down"] = _mm(act, dh3, name="mm_down_dw", ta=True)
    dgu = _swiglu_bwd(gu, dact)
    dw_gu = _mm(n3, dgu, name="mm_gate_up_dw", ta=True)
    grads["w_ffn_gate"], grads["w_ffn_up"] = dw_gu[:, :D_FF], dw_gu[:, D_FF:]
    dn3 = _mm(dgu, w_gu, name="mm_gate_up_dx", tb=True)
    dh2, dg_ffn = _rms_bwd(h2, g_ffn, dn3, dh3, name="rms_ffn_bwd")

    dom = _mm(dh2, wf["w_o_mem"], name="mm_o_mem_dx", tb=True, out_dtype=BF16)
    grads["w_o_mem"] = _mm(om, dh2, name="mm_o_mem_dw", ta=True)
    dqm, dkm, dvm = _mem_bwd(qm3, kvm3, rows3(dom))
    dqm = rows2(dqm)
    dkvm = jnp.concatenate([dkm, dvm], axis=-1).reshape(b_dim * n_mem, 2 * MEM_WIDTH).astype(BF16)
    grads["w_q_mem"] = _mm(hn, dqm, name="mm_q_mem_dw", ta=True)
    dhn = _mm(dqm, wf["w_q_mem"], name="mm_q_mem_dx", tb=True)
    grads["w_kv_mem"] = _mm(memn, dkvm, name="mm_kv_mem_dw", ta=True)
    dmemn = _mm(dkvm, wf["w_kv_mem"], name="mm_kv_mem_dx", tb=True)
    _, dg_mem_kv = _rms_bwd(memf, g_mem_kv, dmemn, None, name="rms_mem_kv_bwd")
    dh1, dg_mem_q = _rms_bwd(h1, g_mem_q, dhn, dh2, name="rms_mem_q_bwd")

    dmix = _mm(dh1, wf["w_out"], name="mm_out_dx", tb=True)
    grads["w_out"] = _mm(mixed, dh1, name="mm_out_dw", ta=True)
    dua, dub, dgates = _gate_bwd(proj, ua, ub, dmix)
    doa = _mm(dua, wf["w_up_a"], name="mm_up_a_dx", tb=True)
    grads["w_up_a"] = _mm(oa, dua, name="mm_up_a_dw", ta=True)
    dob = _mm(dub, wf["w_up_b"], name="mm_up_b_dx", tb=True, out_dtype=BF16)
    grads["w_up_b"] = _mm(ob, dub, name="mm_up_b_dw", ta=True)

    dq_b, dk_b, dv_b = _sb_bwd(proj3, rows3(dob))
    dq_a, dk_a, dv_a = _dil_bwd(proj3, cs3, sn3, sel_a, sel_b, rows3(doa), oa32, lse_a)
    dproj = jnp.concatenate([rows2(t) for t in (dq_a, dk_a, dv_a, dq_b, dk_b, dv_b)] + [dgates], axis=1)
    grads["w_in"] = _mm(n1, dproj, name="mm_in_dw", ta=True)
    dn1 = _mm(dproj, wf["w_in"], name="mm_in_dx", tb=True)
    dx, dg_mix = _rms_bwd(xb, g_mix, dn1, dh1, name="rms_mix_bwd")
    grad_x = dx.reshape(b_dim, s_dim, d)
    return loss_row, grad_x, grads, (dg_mix, dg_mem_q, dg_mem_kv, dg_ffn, dg_final)


def _reduce_and_update(given, shards, loss_row, grad_x, grads, gain_grads):
    d = D_MODEL
    dg_mix, dg_mem_q, dg_mem_kv, dg_ffn, dg_final = gain_grads
    g4 = _flatten_full(grads, F32)
    half = SHARD_ROWS // 2
    c = lax.axis_index("c")
    theirs = _pair_exchange(g4)
    mine = lax.dynamic_slice_in_dim(g4, c * half, half, axis=1)
    pair = _add(mine.reshape(N_CHIPS * half, FLAT_COLS), theirs.reshape(N_CHIPS * half, FLAT_COLS),
                name="grad_pair_sum", out_dtype=BF16)
    parts = _chip_exchange(pair.reshape(N_CHIPS, half, FLAT_COLS))
    my_half = _sum4(parts, name="grad_chip_sum")
    small = jnp.concatenate([dg_mix, dg_mem_q, dg_mem_kv, dg_ffn, dg_final,
                             jnp.pad(loss_row, ((0, 0), (0, FLAT_COLS - LANES))), jnp.zeros((2, FLAT_COLS), F32)], axis=0)
    small_all = _gather_small(small)
    shard_flat = _swap_halves(my_half)
    small_sum = _sum8(small_all)
    loss = small_sum[5, 0]
    gsh = _unflatten_shard(shard_flat)

    out_g, out_d, out_m, out_v = {}, {}, {}, {}
    for n, _, _, _ in SHARDED:
        w2, g2 = shards[n], gsh[n]
        dl, nm, nv = _adamw(w2, g2, given["m_" + n][0], given["v_" + n][0], name="adamw_" + n)
        out_g[n], out_d[n], out_m[n], out_v[n] = g2[None], dl[None], nm[None], nv[None]
    gain_w = jnp.concatenate([given[n].reshape(1, d) for n in GAINS], axis=0)
    gain_m = jnp.concatenate([given["m_" + n].reshape(1, d) for n in GAINS], axis=0)
    gain_v = jnp.concatenate([given["v_" + n].reshape(1, d) for n in GAINS], axis=0)
    gain_g = small_sum[:len(GAINS)]
    gd, gm, gv = _adamw(gain_w, gain_g, gain_m, gain_v, name="adamw_gains")
    for i, n in enumerate(GAINS):
        shape = given[n].shape
        out_g[n], out_d[n] = gain_g[i].reshape(shape), gd[i].reshape(shape)
        out_m[n], out_v[n] = gm[i].reshape(shape), gv[i].reshape(shape)

    order = ["g_mix", "w_in", "w_up_a", "w_up_b", "w_out", "g_mem_q", "g_mem_kv", "w_q_mem", "w_kv_mem", "w_o_mem",
             "g_ffn", "w_ffn_gate", "w_ffn_up", "w_ffn_down", "g_final"]
    return (loss, grad_x, *[out_g[n] for n in order], *[out_d[n] for n in order],
            *[out_m[n] for n in order], *[out_v[n] for n in order])
```

```python
import jax
import jax.numpy as jnp
from jax import lax
from jax.experimental import pallas as pl
from jax.experimental.pallas import tpu as pltpu

F32 = jnp.float32
BF16 = jnp.bfloat16
MESH = pl.DeviceIdType.MESH

D_MODEL = 1024
HEAD_DIM = 64
N_HEADS = 8
ATT_WIDTH = N_HEADS * HEAD_DIM
DIL_PATTERNS = ((128, 1), (512, 4), (2048, 16))
BLOCK = 128
SB_ROWS = 512
ROPE_THETA = 500000.0
ROPE_DIM = HEAD_DIM // 4
N_HEADS_MEM = 4
MEM_HEAD_DIM = 128
MEM_WIDTH = N_HEADS_MEM * MEM_HEAD_DIM
D_FF = 2816
IN_COLS = 6 * ATT_WIDTH + 2 * D_MODEL
RMS_EPS = 1e-6
ADAM_LR = 0.001
ADAM_B1 = 0.9
ADAM_B2 = 0.999
ADAM_EPS = 1e-08
ADAM_WD = 0.01
ADAM_STEP = 10

N_CHIPS = 4
LANES = 128
FLAT_COLS = 1024
VMEM_LIMIT = 56 * 1024 * 1024

PAIRS = ATT_WIDTH // LANES
COL_QA, COL_KA, COL_VA, COL_QB, COL_KB, COL_VB = (i * PAIRS for i in range(6))

NT = (((1,), (1,)), ((), ()))
TN = (((0,), (0,)), ((), ()))


def _tile(dim, cap, unit=LANES):
    if dim <= cap:
        return dim
    best = None
    for t in range(unit, cap + 1, unit):
        if dim % t == 0:
            best = t
    assert best is not None, (dim, cap)
    return best


def _params(sem):
    return pltpu.CompilerParams(dimension_semantics=sem, vmem_limit_bytes=VMEM_LIMIT)


def _mm(a, b, *, name, ta=False, tb=False, add=None, out_dtype=F32,
        tm_cap=1408, tn_cap=1408, tk_cap=1408):
    if ta:
        k_dim, m_dim = a.shape
    else:
        m_dim, k_dim = a.shape
    if tb:
        n_dim, kb = b.shape
    else:
        kb, n_dim = b.shape
    assert kb == k_dim, (a.shape, b.shape, ta, tb)
    tm, tn, tk = _tile(m_dim, tm_cap), _tile(n_dim, tn_cap), _tile(k_dim, tk_cap)
    nk = k_dim // tk
    dims = (((0 if ta else 1,), (1 if tb else 0,)), ((), ()))
    has_add = add is not None

    def body(*refs):
        if has_add:
            a_ref, b_ref, add_ref, o_ref = refs[:4]
        else:
            a_ref, b_ref, o_ref = refs[:3]
        part = lax.dot_general(a_ref[...].astype(BF16), b_ref[...].astype(BF16), dims, preferred_element_type=F32)

        def finish(r):
            if has_add:
                r = add_ref[...] + r
            o_ref[...] = r.astype(out_dtype)

        if nk == 1:
            finish(part)
            return
        acc_ref = refs[-1]
        k = pl.program_id(2)

        @pl.when(k == 0)
        def _():
            acc_ref[...] = part

        @pl.when(k > 0)
        def _():
            acc_ref[...] += part

        @pl.when(k == nk - 1)
        def _():
            finish(acc_ref[...])

    a_spec = pl.BlockSpec((tk, tm), lambda i, j, k: (k, i)) if ta else pl.BlockSpec((tm, tk), lambda i, j, k: (i, k))
    b_spec = pl.BlockSpec((tn, tk), lambda i, j, k: (j, k)) if tb else pl.BlockSpec((tk, tn), lambda i, j, k: (k, j))
    o_spec = pl.BlockSpec((tm, tn), lambda i, j, k: (i, j))
    in_specs = [a_spec, b_spec] + ([o_spec] if has_add else [])
    args = (a, b) + ((add,) if has_add else ())
    return pl.pallas_call(
        body, name=name, grid=(m_dim // tm, n_dim // tn, nk),
        in_specs=in_specs, out_specs=o_spec,
        out_shape=jax.ShapeDtypeStruct((m_dim, n_dim), out_dtype),
        scratch_shapes=[pltpu.VMEM((tm, tn), F32)] if nk > 1 else [],
        compiler_params=_params(("parallel", "parallel", "arbitrary")),
    )(*args)


def _rms_fwd(x, g, *, name, tt=512):
    t_dim, d = x.shape
    tt = _tile(t_dim, tt, 8)

    def body(x_ref, g_ref, o_ref):
        xv = x_ref[...]
        r = lax.rsqrt(jnp.mean(xv * xv, axis=-1, keepdims=True) + RMS_EPS)
        o_ref[...] = ((xv * r) * g_ref[...]).astype(o_ref.dtype)

    return pl.pallas_call(
        body, name=name, grid=(t_dim // tt,),
        in_specs=[pl.BlockSpec((tt, d), lambda i: (i, 0)), pl.BlockSpec((1, d), lambda i: (0, 0))],
        out_specs=pl.BlockSpec((tt, d), lambda i: (i, 0)),
        out_shape=jax.ShapeDtypeStruct((t_dim, d), BF16),
        compiler_params=_params(("parallel",)),
    )(x, g)


def _rms_bwd(x, g, dy, add, *, name, tt=512):
    t_dim, d = x.shape
    tt = _tile(t_dim, tt, 8)
    has_add = add is not None

    def body(*refs):
        if has_add:
            x_ref, g_ref, dy_ref, add_ref, dx_ref, dg_ref = refs
        else:
            x_ref, g_ref, dy_ref, dx_ref, dg_ref = refs
        xv = x_ref[...]
        dyv = dy_ref[...].astype(F32)
        r = lax.rsqrt(jnp.mean(xv * xv, axis=-1, keepdims=True) + RMS_EPS)
        xh = xv * r
        u = dyv * g_ref[...]
        dx = r * (u - xh * jnp.mean(u * xh, axis=-1, keepdims=True))
        if has_add:
            dx = add_ref[...] + dx
        dx_ref[...] = dx

        @pl.when(pl.program_id(0) == 0)
        def _():
            dg_ref[...] = jnp.zeros_like(dg_ref)

        dg_ref[...] += jnp.sum(dyv * xh, axis=0, keepdims=True)

    row = pl.BlockSpec((tt, d), lambda i: (i, 0))
    vec = pl.BlockSpec((1, d), lambda i: (0, 0))
    in_specs = [row, vec, row] + ([row] if has_add else [])
    args = (x, g, dy) + ((add,) if has_add else ())
    return pl.pallas_call(
        body, name=name, grid=(t_dim // tt,),
        in_specs=in_specs, out_specs=[row, vec],
        out_shape=[jax.ShapeDtypeStruct((t_dim, d), F32), jax.ShapeDtypeStruct((1, d), F32)],
        compiler_params=_params(("arbitrary",)),
    )(*args)


def _final(h, g, target, *, tt=512):
    t_dim, d = h.shape
    n_steps = t_dim // tt

    def body(h_ref, g_ref, t_ref, loss_ref, dh_ref, dg_ref, sq_ref):
        i = pl.program_id(0)
        xv = h_ref[...]
        gv = g_ref[...]
        r = lax.rsqrt(jnp.mean(xv * xv, axis=-1, keepdims=True) + RMS_EPS)
        xh = xv * r
        err = xh * gv - t_ref[...]
        dyv = err * (1.0 / d)
        u = dyv * gv
        dh_ref[...] = r * (u - xh * jnp.mean(u * xh, axis=-1, keepdims=True))

        @pl.when(i == 0)
        def _():
            dg_ref[...] = jnp.zeros_like(dg_ref)
            sq_ref[...] = jnp.zeros_like(sq_ref)

        dg_ref[...] += jnp.sum(dyv * xh, axis=0, keepdims=True)
        sq_ref[...] += jnp.sum(err * err, axis=0, keepdims=True)

        @pl.when(i == n_steps - 1)
        def _():
            total = jnp.sum(sq_ref[...], axis=-1, keepdims=True) * (0.5 / d)
            loss_ref[...] = jnp.broadcast_to(total, loss_ref.shape)

    row = pl.BlockSpec((tt, d), lambda i: (i, 0))
    vec = pl.BlockSpec((1, d), lambda i: (0, 0))
    return pl.pallas_call(
        body, name="final_loss", grid=(n_steps,),
        in_specs=[row, vec, row],
        out_specs=[pl.BlockSpec((1, LANES), lambda i: (0, 0)), row, vec],
        out_shape=[jax.ShapeDtypeStruct((1, LANES), F32), jax.ShapeDtypeStruct((t_dim, d), F32),
                   jax.ShapeDtypeStruct((1, d), F32)],
        scratch_shapes=[pltpu.VMEM((1, d), F32)],
        compiler_params=_params(("arbitrary",)),
    )(h, g, target)


def _rope_table(pos, inv_lane, sel_a, sel_b, *, tt=512):
    t_dim = pos.shape[0]

    def body(p_ref, f_ref, a_ref, b_ref, c_ref, s_ref):
        ang = p_ref[...] * f_ref[...]
        on = (a_ref[...] + b_ref[...]) > 0.0
        c_ref[...] = jnp.where(on, jnp.cos(ang), 1.0)
        s_ref[...] = jnp.where(on, jnp.sin(ang), 0.0)

    vec = pl.BlockSpec((1, LANES), lambda i: (0, 0))
    row = pl.BlockSpec((tt, LANES), lambda i: (i, 0))
    shp = jax.ShapeDtypeStruct((t_dim, LANES), F32)
    return pl.pallas_call(
        body, name="rope_table", grid=(t_dim // tt,),
        in_specs=[pl.BlockSpec((tt, 1), lambda i: (i, 0)), vec, vec, vec],
        out_specs=[row, row], out_shape=[shp, shp],
        compiler_params=_params(("parallel",)),
    )(pos, inv_lane, sel_a, sel_b)


def _rotate(xv, cs, sn, sa, sb):
    half = ROPE_DIM // 2
    up = pltpu.roll(xv, LANES - half, 1)
    dn = pltpu.roll(xv, half, 1)
    return xv * cs + (dn * sb - up * sa) * sn


def _head_masks():
    h1 = lax.broadcasted_iota(jnp.int32, (1, LANES), 1) < HEAD_DIM
    return h1, jnp.logical_not(h1)


def _split_heads(xv, h1, h2):
    return jnp.where(h1, xv, 0.0).astype(BF16), jnp.where(h2, xv, 0.0).astype(BF16)


def _tri_masks():
    r = lax.broadcasted_iota(jnp.int32, (BLOCK, BLOCK), 0)
    c = lax.broadcasted_iota(jnp.int32, (BLOCK, BLOCK), 1)
    return c <= r, r <= c


def _stream_rows(start, dil):
    if dil == 1:
        return pl.ds(pl.multiple_of(start, BLOCK), BLOCK)
    return pl.ds(start, BLOCK, stride=dil)


def _dil_specs(b_dim, s_dim):
    def col(c0):
        return pl.BlockSpec((None, s_dim, LANES),lambda b, h: (b, 0, c0 + h))
    tab = pl.BlockSpec((None, s_dim, LANES),lambda b, h: (b, 0, 0))
    vec = pl.BlockSpec((1, LANES), lambda b, h: (0, 0))
    return col, tab, vec


def _dil_fwd(proj3, cs3, sn3, sel_a, sel_b):
    b_dim, s_dim, _ = proj3.shape
    scale = HEAD_DIM ** -0.5
    n_pat = len(DIL_PATTERNS)

    def body(q_ref, k_ref, v_ref, cs_ref, sn_ref, sa_ref, sb_ref, o16_ref, o32_ref, l_ref, qr, kr, *per_pattern):
        og, lg = per_pattern[:n_pat], per_pattern[n_pat:]
        h1, h2 = _head_masks()
        cur_ok, prev_ok = _tri_masks()
        sa, sb = sa_ref[...], sb_ref[...]

        def prep(j, _):
            rows = pl.ds(pl.multiple_of(j * BLOCK, BLOCK), BLOCK)
            cs, sn = cs_ref[rows, :], sn_ref[rows, :]
            qr[rows, :] = _rotate(q_ref[rows, :], cs, sn, sa, sb) * scale
            kr[rows, :] = _rotate(k_ref[rows, :], cs, sn, sa, sb)
            return 0

        lax.fori_loop(0, s_dim // BLOCK, prep, 0)

        for g, (_, dil) in enumerate(DIL_PATTERNS):
            nb = s_dim // dil // BLOCK

            def one(idx, _, g=g, dil=dil, nb=nb):
                r = idx // nb
                n = idx % nb
                rows = _stream_rows(r + dil * BLOCK * n, dil)
                prow = _stream_rows(r + dil * BLOCK * jnp.maximum(n - 1, 0), dil)
                q1, q2 = _split_heads(qr[rows, :], h1, h2)
                kc = kr[rows, :].astype(BF16)
                vc1, vc2 = _split_heads(v_ref[rows, :], h1, h2)
                if nb > 1:
                    kp = kr[prow, :].astype(BF16)
                    vp1, vp2 = _split_heads(v_ref[prow, :], h1, h2)
                    p_ok = jnp.logical_and(prev_ok, n > 0)

                def head(qh, vch, vph):
                    sc = jnp.where(cur_ok, lax.dot_general(qh, kc, NT, preferred_element_type=F32), -jnp.inf)
                    m = jnp.max(sc, axis=-1, keepdims=True)
                    if nb > 1:
                        sp = jnp.where(p_ok, lax.dot_general(qh, kp, NT, preferred_element_type=F32), -jnp.inf)
                        m = jnp.maximum(m, jnp.max(sp, axis=-1, keepdims=True))
                    pc = jnp.exp(sc - m)
                    den = jnp.sum(pc, axis=-1, keepdims=True)
                    acc = jnp.dot(pc.astype(BF16), vch, preferred_element_type=F32)
                    if nb > 1:
                        pp = jnp.exp(sp - m)
                        den = den + jnp.sum(pp, axis=-1, keepdims=True)
                        acc = acc + jnp.dot(pp.astype(BF16), vph, preferred_element_type=F32)
                    return acc / den, m + jnp.log(den)

                o1, l1 = head(q1, vc1, vp1 if nb > 1 else None)
                o2, l2 = head(q2, vc2, vp2 if nb > 1 else None)
                og[g][rows, :] = o1 + o2
                lg[g][rows, :] = jnp.where(h1, l1, l2)
                return 0

            lax.fori_loop(0, dil * nb, one, 0, unroll=2)

        def comb(j, _):
            rows = pl.ds(pl.multiple_of(j * BLOCK, BLOCK), BLOCK)
            ls = [lg[g][rows, :] for g in range(n_pat)]
            m = jnp.maximum(jnp.maximum(ls[0], ls[1]), ls[2])
            es = [jnp.exp(l - m) for l in ls]
            den = es[0] + es[1] + es[2]
            o = (es[0] * og[0][rows, :] + es[1] * og[1][rows, :] + es[2] * og[2][rows, :]) / den
            o16_ref[rows, :] = o.astype(BF16)
            o32_ref[rows, :] = o
            l_ref[rows, :] = m + jnp.log(den)
            return 0

        lax.fori_loop(0, s_dim // BLOCK, comb, 0)

    col, tab, vec = _dil_specs(b_dim, s_dim)
    out = pl.BlockSpec((None, s_dim, LANES),lambda b, h: (b, 0, h))
    shp = (b_dim, s_dim, ATT_WIDTH)
    return pl.pallas_call(
        body, name="dil_fwd", grid=(b_dim, PAIRS),
        in_specs=[col(COL_QA), col(COL_KA), col(COL_VA), tab, tab, vec, vec],
        out_specs=[out, out, out],
        out_shape=[jax.ShapeDtypeStruct(shp, BF16), jax.ShapeDtypeStruct(shp, F32), jax.ShapeDtypeStruct(shp, F32)],
        scratch_shapes=[pltpu.VMEM((s_dim, LANES), F32)] * (2 + 2 * n_pat),
        compiler_params=_params(("parallel", "parallel")),
    )(proj3, proj3, proj3, cs3, sn3, sel_a, sel_b)


def _dil_bwd(proj3, cs3, sn3, sel_a, sel_b, do3, o3, lse3):
    b_dim, s_dim, _ = proj3.shape
    scale = HEAD_DIM ** -0.5

    def body(q_ref, k_ref, v_ref, cs_ref, sn_ref, sa_ref, sb_ref, do_ref, o_ref, l_ref,
             dq_ref, dk_ref, dv_ref, qr, kr, dqa, dka, dva):
        h1, h2 = _head_masks()
        cur_ok, prev_ok = _tri_masks()
        sa, sb = sa_ref[...], sb_ref[...]

        def prep(j, _):
            rows = pl.ds(pl.multiple_of(j * BLOCK, BLOCK), BLOCK)
            cs, sn = cs_ref[rows, :], sn_ref[rows, :]
            qr[rows, :] = _rotate(q_ref[rows, :], cs, sn, sa, sb) * scale
            kr[rows, :] = _rotate(k_ref[rows, :], cs, sn, sa, sb)
            zero = jnp.zeros((BLOCK, LANES), F32)
            dqa[rows, :] = zero
            dka[rows, :] = zero
            dva[rows, :] = zero
            return 0

        lax.fori_loop(0, s_dim // BLOCK, prep, 0)

        for _, dil in DIL_PATTERNS:
            nb = s_dim // dil // BLOCK

            def one(idx, _, dil=dil, nb=nb):
                r = idx // nb
                n = idx % nb
                rows = _stream_rows(r + dil * BLOCK * n, dil)
                prow = _stream_rows(r + dil * BLOCK * jnp.maximum(n - 1, 0), dil)
                q1, q2 = _split_heads(qr[rows, :], h1, h2)
                dof = do_ref[rows, :]
                do1, do2 = _split_heads(dof, h1, h2)
                prod = dof * o_ref[rows, :]
                delta1 = jnp.sum(jnp.where(h1, prod, 0.0), axis=-1, keepdims=True)
                delta2 = jnp.sum(jnp.where(h2, prod, 0.0), axis=-1, keepdims=True)
                lt = l_ref[rows, :]
                lse1 = jnp.max(jnp.where(h1, lt, -jnp.inf), axis=-1, keepdims=True)
                lse2 = jnp.max(jnp.where(h2, lt, -jnp.inf), axis=-1, keepdims=True)

                def side(krows, ok):
                    kf = kr[krows, :]
                    k16 = kf.astype(BF16)
                    k1, k2 = _split_heads(kf, h1, h2)
                    v16 = v_ref[krows, :].astype(BF16)

                    def head(qh, doh, lse, delta):
                        sc = lax.dot_general(qh, k16, NT, preferred_element_type=F32)
                        p = jnp.where(ok, jnp.exp(sc - lse), 0.0)
                        dp = lax.dot_general(doh, v16, NT, preferred_element_type=F32)
                        return p.astype(BF16), (p * (dp - delta)).astype(BF16)

                    p1, ds1 = head(q1, do1, lse1, delta1)
                    p2, ds2 = head(q2, do2, lse2, delta2)
                    dva[krows, :] += (lax.dot_general(p1, do1, TN, preferred_element_type=F32)
                                      + lax.dot_general(p2, do2, TN, preferred_element_type=F32))
                    dka[krows, :] += (lax.dot_general(ds1, q1, TN, preferred_element_type=F32)
                                      + lax.dot_general(ds2, q2, TN, preferred_element_type=F32))
                    return (jnp.dot(ds1, k1, preferred_element_type=F32)
                            + jnp.dot(ds2, k2, preferred_element_type=F32))

                dq = side(rows, cur_ok)
                if nb > 1:
                    dq = dq + side(prow, jnp.logical_and(prev_ok, n > 0))
                dqa[rows, :] += dq * scale
                return 0

            lax.fori_loop(0, dil * nb, one, 0, unroll=2)

        def finish(j, _):
            rows = pl.ds(pl.multiple_of(j * BLOCK, BLOCK), BLOCK)
            cs, sn = cs_ref[rows, :], -sn_ref[rows, :]
            dq_ref[rows, :] = _rotate(dqa[rows, :], cs, sn, sa, sb).astype(BF16)
            dk_ref[rows, :] = _rotate(dka[rows, :], cs, sn, sa, sb).astype(BF16)
            dv_ref[rows, :] = dva[rows, :].astype(BF16)
            return 0

        lax.fori_loop(0, s_dim // BLOCK, finish, 0)

    col, tab, vec = _dil_specs(b_dim, s_dim)
    out = pl.BlockSpec((None, s_dim, LANES),lambda b, h: (b, 0, h))
    shp = jax.ShapeDtypeStruct((b_dim, s_dim, ATT_WIDTH), BF16)
    acc = pltpu.VMEM((s_dim, LANES), F32)
    return pl.pallas_call(
        body, name="dil_bwd", grid=(b_dim, PAIRS),
        in_specs=[col(COL_QA), col(COL_KA), col(COL_VA), tab, tab, vec, vec, out, out, out],
        out_specs=[out, out, out], out_shape=[shp, shp, shp],
        scratch_shapes=[acc, acc, acc, acc, acc],
        compiler_params=_params(("parallel", "parallel")),
    )(proj3, proj3, proj3, cs3, sn3, sel_a, sel_b, do3, o3, lse3)


def _split_dot(x, tri):
    hi = x.astype(BF16)
    lo = (x - hi.astype(F32)).astype(BF16)
    return jnp.dot(hi, tri, preferred_element_type=F32) + jnp.dot(lo, tri, preferred_element_type=F32)


def _log_sigmoid(z):
    return jnp.minimum(z, 0.0) - jnp.log(1.0 + jnp.exp(-jnp.abs(z)))


def _sb_scores(qh, k16, valid):
    z = lax.dot_general(qh, k16, NT, preferred_element_type=F32)
    ls = _log_sigmoid(z)
    return ls, jnp.where(valid, ls - z, 0.0)


def _sb_consts():
    r = lax.broadcasted_iota(jnp.int32, (BLOCK, BLOCK), 0)
    c = lax.broadcasted_iota(jnp.int32, (BLOCK, BLOCK), 1)
    after = (r > c).astype(BF16)
    before = (r < c).astype(BF16)
    qrow = lax.broadcasted_iota(jnp.int32, (SB_ROWS, BLOCK), 0)
    kcol = lax.broadcasted_iota(jnp.int32, (SB_ROWS, BLOCK), 1)
    return after, before, qrow, kcol


def _sb_fwd(proj3):
    b_dim, s_dim, _ = proj3.shape
    scale = HEAD_DIM ** -0.5
    per = SB_ROWS // BLOCK

    def body(q_ref, k_ref, v_ref, o_ref):
        h1, h2 = _head_masks()
        after, _, qrow, kcol = _sb_consts()

        def qloop(qi, _):
            rows = pl.ds(pl.multiple_of(qi * SB_ROWS, SB_ROWS), SB_ROWS)
            q1, q2 = _split_heads(q_ref[rows, :] * scale, h1, h2)
            qpos = qi * SB_ROWS + qrow
            nkb = (qi + 1) * per

            def kloop(i, carry):
                acc, run1, run2 = carry
                kb = nkb - 1 - i
                krows = pl.ds(pl.multiple_of(kb * BLOCK, BLOCK), BLOCK)
                k16 = k_ref[krows, :].astype(BF16)
                v1, v2 = _split_heads(v_ref[krows, :], h1, h2)
                valid = (kb * BLOCK + kcol) < qpos

                def head(qh, vh, run):
                    ls, l1m = _sb_scores(qh, k16, valid)
                    a = jnp.where(valid, jnp.exp(ls + _split_dot(l1m, after) + run), 0.0)
                    return (jnp.dot(a.astype(BF16), vh, preferred_element_type=F32),
                            run + jnp.sum(l1m, axis=-1, keepdims=True))

                o1, run1 = head(q1, v1, run1)
                o2, run2 = head(q2, v2, run2)
                return acc + o1 + o2, run1, run2

            zcol = jnp.zeros((SB_ROWS, 1), F32)
            acc, _, _ = lax.fori_loop(0, nkb // 2, lambda i, c: kloop(2 * i + 1, kloop(2 * i, c)),
                                      (jnp.zeros((SB_ROWS, LANES), F32), zcol, zcol))
            o_ref[rows, :] = acc.astype(BF16)
            return 0

        lax.fori_loop(0, s_dim // SB_ROWS, qloop, 0)

    def col(c0):
        return pl.BlockSpec((None, s_dim, LANES),lambda b, h: (b, 0, c0 + h))

    return pl.pallas_call(
        body, name="sb_fwd", grid=(b_dim, PAIRS),
        in_specs=[col(COL_QB), col(COL_KB), col(COL_VB)], out_specs=col(0),
        out_shape=jax.ShapeDtypeStruct((b_dim, s_dim, ATT_WIDTH), BF16),
        compiler_params=_params(("parallel", "parallel")),
    )(proj3, proj3, proj3)


def _sb_bwd(proj3, do3):
    b_dim, s_dim, _ = proj3.shape
    scale = HEAD_DIM ** -0.5
    per = SB_ROWS // BLOCK
    nkb_max = s_dim // BLOCK

    def body(q_ref, k_ref, v_ref, do_ref, dq_ref, dk_ref, dv_ref, dka, dva, e_ref, sg_ref):
        h1, h2 = _head_masks()
        after, before, qrow, kcol = _sb_consts()
        dka[...] = jnp.zeros_like(dka)
        dva[...] = jnp.zeros_like(dva)

        def qloop(qi, _):
            rows = pl.ds(pl.multiple_of(qi * SB_ROWS, SB_ROWS), SB_ROWS)
            q1, q2 = _split_heads(q_ref[rows, :] * scale, h1, h2)
            do1, do2 = _split_heads(do_ref[rows, :].astype(F32), h1, h2)
            qpos = qi * SB_ROWS + qrow
            nkb = (qi + 1) * per

            def pass1(i, carry):
                run1, run2 = carry
                kb = nkb - 1 - i
                krows = pl.ds(pl.multiple_of(kb * BLOCK, BLOCK), BLOCK)
                k16 = k_ref[krows, :].astype(BF16)
                v16 = v_ref[krows, :].astype(BF16)
                valid = (kb * BLOCK + kcol) < qpos

                def head(h, qh, doh, run):
                    ls, l1m = _sb_scores(qh, k16, valid)
                    a = jnp.where(valid, jnp.exp(ls + _split_dot(l1m, after) + run), 0.0)
                    da = lax.dot_general(doh, v16, NT, preferred_element_type=F32)
                    e_ref[h, kb] = a * da
                    sg_ref[h, kb] = jnp.exp(ls)
                    return a.astype(BF16), run + jnp.sum(l1m, axis=-1, keepdims=True)

                a1, run1 = head(0, q1, do1, run1)
                a2, run2 = head(1, q2, do2, run2)
                dva[krows, :] += (lax.dot_general(a1, do1, TN, preferred_element_type=F32)
                                  + lax.dot_general(a2, do2, TN, preferred_element_type=F32))
                return run1, run2

            zcol = jnp.zeros((SB_ROWS, 1), F32)
            lax.fori_loop(0, nkb // 2, lambda i, c: pass1(2 * i + 1, pass1(2 * i, c)), (zcol, zcol))

            def pass2(kb, carry):
                dq, pre1, pre2 = carry
                krows = pl.ds(pl.multiple_of(kb * BLOCK, BLOCK), BLOCK)
                k1, k2 = _split_heads(k_ref[krows, :], h1, h2)
                valid = (kb * BLOCK + kcol) < qpos

                def head(h, pre):
                    ev = e_ref[h, kb]
                    sg = sg_ref[h, kb]
                    prefix = _split_dot(ev, before) + pre
                    dz = jnp.where(valid, ev * (1.0 - sg) - prefix * sg, 0.0).astype(BF16)
                    return dz, pre + jnp.sum(ev, axis=-1, keepdims=True)

                dz1, pre1 = head(0, pre1)
                dz2, pre2 = head(1, pre2)
                dka[krows, :] += (lax.dot_general(dz1, q1, TN, preferred_element_type=F32)
                                  + lax.dot_general(dz2, q2, TN, preferred_element_type=F32))
                dq = dq + jnp.dot(dz1, k1, preferred_element_type=F32) + jnp.dot(dz2, k2, preferred_element_type=F32)
                return dq, pre1, pre2

            dq, _, _ = lax.fori_loop(0, nkb // 2, lambda i, c: pass2(2 * i + 1, pass2(2 * i, c)),
                                     (jnp.zeros((SB_ROWS, LANES), F32), zcol, zcol))
            dq_ref[rows, :] = (dq * scale).astype(BF16)
            return 0

        lax.fori_loop(0, s_dim // SB_ROWS, qloop, 0)
        dk_ref[...] = dka[...].astype(BF16)
        dv_ref[...] = dva[...].astype(BF16)

    def col(c0):
        return pl.BlockSpec((None, s_dim, LANES),lambda b, h: (b, 0, c0 + h))

    shp = jax.ShapeDtypeStruct((b_dim, s_dim, ATT_WIDTH), BF16)
    acc = pltpu.VMEM((s_dim, LANES), F32)
    strip = pltpu.VMEM((2, nkb_max, SB_ROWS, BLOCK), F32)
    return pl.pallas_call(
        body, name="sb_bwd", grid=(b_dim, PAIRS),
        in_specs=[col(COL_QB), col(COL_KB), col(COL_VB), col(0)], out_specs=[col(0), col(0), col(0)],
        out_shape=[shp, shp, shp],
        scratch_shapes=[acc, acc, strip, strip],
        compiler_params=_params(("parallel", "parallel")),
    )(proj3, proj3, proj3, do3)


def _sigmoid(x):
    return 1.0 / (1.0 + jnp.exp(-x))


def _gate_fwd(proj, ua, ub, *, tt=512):
    t_dim, d = ua.shape

    def body(ga_ref, gb_ref, ua_ref, ub_ref, o_ref):
        o_ref[...] = (_sigmoid(ga_ref[...]) * ua_ref[...] + _sigmoid(gb_ref[...]) * ub_ref[...]).astype(BF16)

    row = pl.BlockSpec((tt, d), lambda i: (i, 0))
    return pl.pallas_call(
        body, name="gate_fwd", grid=(t_dim // tt,),
        in_specs=[pl.BlockSpec((tt, d), lambda i: (i, 3)), pl.BlockSpec((tt, d), lambda i: (i, 4)), row, row],
        out_specs=row, out_shape=jax.ShapeDtypeStruct((t_dim, d), BF16),
        compiler_params=_params(("parallel",)),
    )(proj, proj, ua, ub)


def _gate_bwd(proj, ua, ub, dmix, *, tt=512):
    t_dim, d = ua.shape

    def body(ga_ref, gb_ref, ua_ref, ub_ref, dm_ref, dua_ref, dub_ref, dg_ref):
        dm = dm_ref[...]
        sa = _sigmoid(ga_ref[...])
        sb = _sigmoid(gb_ref[...])
        dua_ref[...] = (dm * sa).astype(BF16)
        dub_ref[...] = (dm * sb).astype(BF16)
        dg_ref[:, :d] = (dm * ua_ref[...] * (sa * (1.0 - sa))).astype(BF16)
        dg_ref[:, d:] = (dm * ub_ref[...] * (sb * (1.0 - sb))).astype(BF16)

    row = pl.BlockSpec((tt, d), lambda i: (i, 0))
    wide = pl.BlockSpec((tt, 2 * d), lambda i: (i, 0))
    return pl.pallas_call(
        body, name="gate_bwd", grid=(t_dim // tt,),
        in_specs=[pl.BlockSpec((tt, d), lambda i: (i, 3)), pl.BlockSpec((tt, d), lambda i: (i, 4)), row, row, row],
        out_specs=[row, row, wide],
        out_shape=[jax.ShapeDtypeStruct((t_dim, d), BF16), jax.ShapeDtypeStruct((t_dim, d), BF16),
                   jax.ShapeDtypeStruct((t_dim, 2 * d), BF16)],
        compiler_params=_params(("parallel",)),
    )(proj, proj, ua, ub, dmix)


def _swiglu_fwd(gu, *, tt=256):
    t_dim = gu.shape[0]

    def body(g_ref, u_ref, o_ref):
        gv = g_ref[...]
        o_ref[...] = (gv * _sigmoid(gv) * u_ref[...]).astype(BF16)

    return pl.pallas_call(
        body, name="swiglu_fwd", grid=(t_dim // tt,),
        in_specs=[pl.BlockSpec((tt, D_FF), lambda i: (i, 0)), pl.BlockSpec((tt, D_FF), lambda i: (i, 1))],
        out_specs=pl.BlockSpec((tt, D_FF), lambda i: (i, 0)),
        out_shape=jax.ShapeDtypeStruct((t_dim, D_FF), BF16),
        compiler_params=_params(("parallel",)),
    )(gu, gu)


def _swiglu_bwd(gu, dact, *, tt=256):
    t_dim = gu.shape[0]

    def body(g_ref, u_ref, da_ref, o_ref):
        gv = g_ref[...]
        da = da_ref[...]
        sg = _sigmoid(gv)
        o_ref[:, :D_FF] = (da * u_ref[...] * (sg + gv * sg * (1.0 - sg))).astype(BF16)
        o_ref[:, D_FF:] = (da * (gv * sg)).astype(BF16)

    return pl.pallas_call(
        body, name="swiglu_bwd", grid=(t_dim // tt,),
        in_specs=[pl.BlockSpec((tt, D_FF), lambda i: (i, 0)), pl.BlockSpec((tt, D_FF), lambda i: (i, 1)),
                  pl.BlockSpec((tt, D_FF), lambda i: (i, 0))],
        out_specs=pl.BlockSpec((tt, 2 * D_FF), lambda i: (i, 0)),
        out_shape=jax.ShapeDtypeStruct((t_dim, 2 * D_FF), BF16),
        compiler_params=_params(("parallel",)),
    )(gu, gu, dact)


def _mem_fwd(qm, kvm, *, tt=512):
    b_dim, s_dim, _ = qm.shape
    n_mem = kvm.shape[1]
    scale = MEM_HEAD_DIM ** -0.5

    def body(q_ref, k_ref, v_ref, o_ref):
        sc = lax.dot_general(q_ref[0], k_ref[0], NT, preferred_element_type=F32) * scale
        p = jnp.exp(sc - jnp.max(sc, axis=-1, keepdims=True))
        p = p / jnp.sum(p, axis=-1, keepdims=True)
        o_ref[0] = jnp.dot(p.astype(BF16), v_ref[0], preferred_element_type=F32).astype(BF16)

    qs = pl.BlockSpec((1, tt, MEM_HEAD_DIM), lambda b, h, i: (b, i, h))
    return pl.pallas_call(
        body, name="mem_fwd", grid=(b_dim, N_HEADS_MEM, s_dim // tt),
        in_specs=[qs, pl.BlockSpec((1, n_mem, MEM_HEAD_DIM), lambda b, h, i: (b, 0, h)),
                  pl.BlockSpec((1, n_mem, MEM_HEAD_DIM), lambda b, h, i: (b, 0, N_HEADS_MEM + h))],
        out_specs=qs, out_shape=jax.ShapeDtypeStruct(qm.shape, BF16),
        compiler_params=_params(("parallel", "parallel", "parallel")),
    )(qm, kvm, kvm)


def _mem_bwd(qm, kvm, dom, *, tt=512):
    b_dim, s_dim, _ = qm.shape
    n_mem = kvm.shape[1]
    scale = MEM_HEAD_DIM ** -0.5

    def body(q_ref, k_ref, v_ref, do_ref, dq_ref, dk_ref, dv_ref):
        qv, kv, vv, dov = q_ref[0], k_ref[0], v_ref[0], do_ref[0]
        sc = lax.dot_general(qv, kv, NT, preferred_element_type=F32) * scale
        p = jnp.exp(sc - jnp.max(sc, axis=-1, keepdims=True))
        p = p / jnp.sum(p, axis=-1, keepdims=True)
        dp = lax.dot_general(dov, vv, NT, preferred_element_type=F32)
        ds = (p * (dp - jnp.sum(p * dp, axis=-1, keepdims=True)) * scale).astype(BF16)
        dq_ref[0] = jnp.dot(ds, kv, preferred_element_type=F32).astype(BF16)

        @pl.when(pl.program_id(2) == 0)
        def _():
            dk_ref[...] = jnp.zeros_like(dk_ref)
            dv_ref[...] = jnp.zeros_like(dv_ref)

        dk_ref[0] += lax.dot_general(ds, qv, TN, preferred_element_type=F32)
        dv_ref[0] += lax.dot_general(p.astype(BF16), dov, TN, preferred_element_type=F32)

    qs = pl.BlockSpec((1, tt, MEM_HEAD_DIM), lambda b, h, i: (b, i, h))
    ks = pl.BlockSpec((1, n_mem, MEM_HEAD_DIM), lambda b, h, i: (b, 0, h))
    vs = pl.BlockSpec((1, n_mem, MEM_HEAD_DIM), lambda b, h, i: (b, 0, N_HEADS_MEM + h))
    return pl.pallas_call(
        body, name="mem_bwd", grid=(b_dim, N_HEADS_MEM, s_dim // tt),
        in_specs=[qs, ks, vs, qs], out_specs=[qs, ks, ks],
        out_shape=[jax.ShapeDtypeStruct(qm.shape, BF16), jax.ShapeDtypeStruct((b_dim, n_mem, MEM_WIDTH), F32),
                   jax.ShapeDtypeStruct((b_dim, n_mem, MEM_WIDTH), F32)],
        compiler_params=_params(("parallel", "parallel", "arbitrary")),
    )(qm, kvm, kvm, dom)


def _adamw(w, g, m, v, *, name):
    rows, cols = w.shape
    tr = _tile(rows, 256, 8)

    def body(w_ref, g_ref, m_ref, v_ref, d_ref, nm_ref, nv_ref):
        gv = g_ref[...]
        nm = ADAM_B1 * m_ref[...] + (1.0 - ADAM_B1) * gv
        nv = ADAM_B2 * v_ref[...] + (1.0 - ADAM_B2) * (gv * gv)
        m_hat = nm / (1.0 - ADAM_B1 ** ADAM_STEP)
        v_hat = nv / (1.0 - ADAM_B2 ** ADAM_STEP)
        d_ref[...] = -ADAM_LR * (m_hat / (jnp.sqrt(v_hat) + ADAM_EPS) + ADAM_WD * w_ref[...])
        nm_ref[...] = nm
        nv_ref[...] = nv

    spec = pl.BlockSpec((tr, cols), lambda i: (i, 0))
    shp = jax.ShapeDtypeStruct((rows, cols), F32)
    return pl.pallas_call(
        body, name=name, grid=(rows // tr,),
        in_specs=[spec] * 4, out_specs=[spec] * 3, out_shape=[shp] * 3,
        compiler_params=_params(("parallel",)),
    )(w, g, m, v)


def _add(a, b, *, name, out_dtype, tr=256):
    rows, cols = a.shape
    tr = _tile(rows, tr, 16)

    def body(a_ref, b_ref, o_ref):
        o_ref[...] = (a_ref[...] + b_ref[...]).astype(out_dtype)

    spec = pl.BlockSpec((tr, cols), lambda i: (i, 0))
    return pl.pallas_call(
        body, name=name, grid=(rows // tr,), in_specs=[spec, spec], out_specs=spec,
        out_shape=jax.ShapeDtypeStruct((rows, cols), out_dtype),
        compiler_params=_params(("parallel",)),
    )(a, b)


def _sum4(parts, *, name, tr=256):
    _, rows, cols = parts.shape
    tr = _tile(rows, tr, 16)

    def body(p_ref, o_ref):
        p = [p_ref[i].astype(F32) for i in range(N_CHIPS)]
        o_ref[...] = ((p[0] + p[1]) + p[2]) + p[3]

    return pl.pallas_call(
        body, name=name, grid=(rows // tr,),
        in_specs=[pl.BlockSpec((N_CHIPS, tr, cols), lambda i: (0, i, 0))],
        out_specs=pl.BlockSpec((tr, cols), lambda i: (i, 0)),
        out_shape=jax.ShapeDtypeStruct((rows, cols), F32),
        compiler_params=_params(("parallel",)),
    )(parts)


def _sum8(parts):
    n, rows, cols = parts.shape

    def body(p_ref, o_ref):
        acc = p_ref[0]
        for i in range(1, n):
            acc = acc + p_ref[i]
        o_ref[...] = acc

    return pl.pallas_call(
        body, name="small_sum", grid=(1,),
        in_specs=[pl.BlockSpec((n, rows, cols), lambda i: (0, 0, 0))],
        out_specs=pl.BlockSpec((rows, cols), lambda i: (0, 0)),
        out_shape=jax.ShapeDtypeStruct((rows, cols), parts.dtype),
        compiler_params=_params(("arbitrary",)),
    )(parts)


def _place():
    return lax.axis_index("x"), lax.axis_index("y"), lax.axis_index("c")


ANY = pl.BlockSpec(memory_space=pl.ANY)


def _gather_weights(flat):
    rows2, cols = flat.shape
    half = rows2 // 2

    def body(x_ref, out_ref, send_sems, recv_sems, local_sem):
        x, y, c = _place()
        me, sibling = (x, y, c), (x, y, 1 - c)
        chips = [(1 - x, y), (x, 1 - y), (1 - x, 1 - y)]
        mine_src = x_ref.at[pl.ds(c * half, half), :]

        def rows(px, py, pc):
            return out_ref.at[pl.ds((4 * px + 2 * py + pc) * half, half), :]

        def copy(k, block, to, src=None):
            return pltpu.make_async_remote_copy(
                src_ref=rows(*block) if src is None else src, dst_ref=rows(*block),
                send_sem=send_sems.at[k], recv_sem=recv_sems.at[k], device_id=to, device_id_type=MESH)

        mine = pltpu.make_async_copy(mine_src, rows(*me), local_sem)
        mine.start()
        first = [copy(0, me, sibling, src=mine_src)]
        first += [copy(1 + j, me, (*chip, c), src=mine_src) for j, chip in enumerate(chips)]
        for cp in first:
            cp.start()
        passed = [copy(4 + j, (*chip, c), sibling) for j, chip in enumerate(chips)]
        for j, chip in enumerate(chips):
            copy(1 + j, (*chip, c), me).wait_recv()
            passed[j].start()
        copy(0, sibling, me).wait_recv()
        for j, chip in enumerate(chips):
            copy(4 + j, (*chip, 1 - c), me).wait_recv()
        for cp in first + passed:
            cp.wait_send()
        mine.wait()

    return pl.pallas_call(
        body, name="gather_weights",
        out_shape=jax.ShapeDtypeStruct((8 * half, cols), flat.dtype),
        in_specs=[ANY], out_specs=ANY,
        scratch_shapes=[pltpu.SemaphoreType.DMA((7,)), pltpu.SemaphoreType.DMA((7,)), pltpu.SemaphoreType.DMA],
    )(flat)


def _pair_exchange(g):
    n, rows2, cols = g.shape
    half = rows2 // 2

    def body(g_ref, out_ref, send_sem, recv_sem):
        x, y, c = _place()
        cp = pltpu.make_async_remote_copy(
            src_ref=g_ref.at[:, pl.ds((1 - c) * half, half), :], dst_ref=out_ref,
            send_sem=send_sem, recv_sem=recv_sem, device_id=(x, y, 1 - c), device_id_type=MESH)
        cp.start()
        cp.wait()

    return pl.pallas_call(
        body, name="grad_pair_exchange",
        out_shape=jax.ShapeDtypeStruct((n, half, cols), g.dtype),
        in_specs=[ANY], out_specs=ANY,
        scratch_shapes=[pltpu.SemaphoreType.DMA, pltpu.SemaphoreType.DMA],
    )(g)


def _chip_exchange(a):
    n, half, cols = a.shape

    def body(a_ref, out_ref, send_sems, recv_sems, local_sem):
        x, y, c = _place()
        my_chip = 2 * x + y
        others = [(1 - x, y), (x, 1 - y), (1 - x, 1 - y)]
        mine = pltpu.make_async_copy(a_ref.at[my_chip], out_ref.at[my_chip], local_sem)
        mine.start()
        sends = []
        for j, (px, py) in enumerate(others):
            sends.append(pltpu.make_async_remote_copy(
                src_ref=a_ref.at[2 * px + py], dst_ref=out_ref.at[my_chip],
                send_sem=send_sems.at[j], recv_sem=recv_sems.at[j], device_id=(px, py, c), device_id_type=MESH))
        for cp in sends:
            cp.start()
        for j, (px, py) in enumerate(others):
            pltpu.make_async_remote_copy(
                src_ref=a_ref.at[my_chip], dst_ref=out_ref.at[2 * px + py],
                send_sem=send_sems.at[j], recv_sem=recv_sems.at[j], device_id=(px, py, c),
                device_id_type=MESH).wait_recv()
        for cp in sends:
            cp.wait_send()
        mine.wait()

    return pl.pallas_call(
        body, name="grad_chip_exchange",
        out_shape=jax.ShapeDtypeStruct((n, half, cols), a.dtype),
        in_specs=[ANY], out_specs=ANY,
        scratch_shapes=[pltpu.SemaphoreType.DMA((3,)), pltpu.SemaphoreType.DMA((3,)), pltpu.SemaphoreType.DMA],
    )(a)


def _swap_halves(h):
    def body(h_ref, out_ref, send_sem, recv_sem):
        x, y, c = _place()
        swap = pltpu.make_async_remote_copy(
            src_ref=h_ref, dst_ref=out_ref, send_sem=send_sem, recv_sem=recv_sem,
            device_id=(x, y, 1 - c), device_id_type=MESH)
        swap.start()
        swap.wait()

    return pl.pallas_call(
        body, name="grad_swap_halves",
        out_shape=jax.ShapeDtypeStruct(h.shape, h.dtype),
        in_specs=[ANY], out_specs=ANY,
        scratch_shapes=[pltpu.SemaphoreType.DMA, pltpu.SemaphoreType.DMA],
    )(h)


def _gather_small(small):
    srows, cols = small.shape

    def body(s_ref, all_ref, send_sems, recv_sems, local_sem):
        x, y, c = _place()
        me = 4 * x + 2 * y + c
        keep_small = pltpu.make_async_copy(s_ref, all_ref.at[me], local_sem)
        keep_small.start()
        sends = []
        for kk in range(1, 8):
            peer = (x ^ (kk >> 2), y ^ ((kk >> 1) & 1), c ^ (kk & 1))
            sends.append(pltpu.make_async_remote_copy(
                src_ref=s_ref, dst_ref=all_ref.at[me],
                send_sem=send_sems.at[kk], recv_sem=recv_sems.at[kk], device_id=peer, device_id_type=MESH))
        for cp in sends:
            cp.start()
        for kk in range(1, 8):
            px, py, pc = x ^ (kk >> 2), y ^ ((kk >> 1) & 1), c ^ (kk & 1)
            pltpu.make_async_remote_copy(
                src_ref=s_ref, dst_ref=all_ref.at[4 * px + 2 * py + pc],
                send_sem=send_sems.at[kk], recv_sem=recv_sems.at[kk], device_id=(px, py, pc),
                device_id_type=MESH).wait_recv()
        for cp in sends:
            cp.wait_send()
        keep_small.wait()

    return pl.pallas_call(
        body, name="gather_small",
        out_shape=jax.ShapeDtypeStruct((8, srows, cols), small.dtype),
        in_specs=[ANY], out_specs=ANY,
        scratch_shapes=[pltpu.SemaphoreType.DMA((8,)), pltpu.SemaphoreType.DMA((8,)), pltpu.SemaphoreType.DMA],
    )(small)


SHARDED = (("w_in", D_MODEL, IN_COLS, 1), ("w_up_a", ATT_WIDTH, D_MODEL, 1), ("w_up_b", ATT_WIDTH, D_MODEL, 1),
           ("w_out", D_MODEL, D_MODEL, 0), ("w_q_mem", D_MODEL, MEM_WIDTH, 0), ("w_kv_mem", D_MODEL, 2 * MEM_WIDTH, 0),
           ("w_o_mem", MEM_WIDTH, D_MODEL, 1), ("w_ffn_gate", D_MODEL, D_FF, 1), ("w_ffn_up", D_MODEL, D_FF, 1),
           ("w_ffn_down", D_FF, D_MODEL, 0))
SHARD_ELEMS = sum(r * c for _, r, c, _ in SHARDED) // N_CHIPS
SHARD_ROWS = SHARD_ELEMS // FLAT_COLS
assert SHARD_ROWS * FLAT_COLS == SHARD_ELEMS and SHARD_ROWS % 32 == 0
GAINS = ("g_mix", "g_mem_q", "g_mem_kv", "g_ffn", "g_final")


def _flatten_shards(shards, dtype):
    return jnp.concatenate([shards[n].astype(dtype).reshape(-1) for n, _, _, _ in SHARDED]).reshape(SHARD_ROWS, FLAT_COLS)


def _unflatten_full(flat4):
    flat4 = flat4.reshape(N_CHIPS, SHARD_ELEMS)
    out, off = {}, 0
    for n, r, c, axis in SHARDED:
        size = r * c // N_CHIPS
        piece = flat4[:, off:off + size]
        off += size
        if axis == 0:
            out[n] = piece.reshape(r, c)
        else:
            out[n] = piece.reshape(N_CHIPS, r, c // N_CHIPS).transpose(1, 0, 2).reshape(r, c)
    return out


def _flatten_full(full, dtype):
    pieces = []
    for n, r, c, axis in SHARDED:
        gfull = full[n].astype(dtype)
        if axis == 0:
            pieces.append(gfull.reshape(N_CHIPS, r * c // N_CHIPS))
        else:
            pieces.append(gfull.reshape(r, N_CHIPS, c // N_CHIPS).transpose(1, 0, 2).reshape(N_CHIPS, r * c // N_CHIPS))
    return jnp.concatenate(pieces, axis=1).reshape(N_CHIPS, SHARD_ROWS, FLAT_COLS)


def _unflatten_shard(flat):
    flat = flat.reshape(-1)
    out, off = {}, 0
    for n, r, c, axis in SHARDED:
        size = r * c // N_CHIPS
        shape = (r // N_CHIPS, c) if axis == 0 else (r, c // N_CHIPS)
        out[n] = flat[off:off + size].reshape(shape)
        off += size
    return out


def kernel(x, mem, positions, g_mix, w_in, w_up_a, w_up_b, w_out, g_mem_q, g_mem_kv, w_q_mem, w_kv_mem, w_o_mem, g_ffn, w_ffn_gate, w_ffn_up, w_ffn_down, g_final, loss_target, m_g_mix, m_w_in, m_w_up_a, m_w_up_b, m_w_out, m_g_mem_q, m_g_mem_kv, m_w_q_mem, m_w_kv_mem, m_w_o_mem, m_g_ffn, m_w_ffn_gate, m_w_ffn_up, m_w_ffn_down, m_g_final, v_g_mix, v_w_in, v_w_up_a, v_w_up_b, v_w_out, v_g_mem_q, v_g_mem_kv, v_w_q_mem, v_w_kv_mem, v_w_o_mem, v_g_ffn, v_w_ffn_gate, v_w_ffn_up, v_w_ffn_down, v_g_final):
    given = dict(locals())
    shards = {n: given[n][0] for n, _, _, _ in SHARDED}

    gathered = _gather_weights(_flatten_shards(shards, BF16))
    wf = _unflatten_full(gathered.reshape(N_CHIPS, SHARD_ROWS, FLAT_COLS))

    loss_row, grad_x, grads, gain_grads = _local_step(x, mem, positions, loss_target, g_mix, g_mem_q, g_mem_kv,
                                                      g_ffn, g_final, wf)
    return _reduce_and_update(given, shards, loss_row, grad_x, grads, gain_grads)


def _local_step(x, mem, positions, loss_target, g_mix, g_mem_q, g_mem_kv, g_ffn, g_final, wf):
    b_dim, s_dim, d = x.shape
    t_dim = b_dim * s_dim
    n_mem = mem.shape[1]
    w_gu = jnp.concatenate([wf["w_ffn_gate"], wf["w_ffn_up"]], axis=1)

    xb = x.reshape(t_dim, d)
    tgt = loss_target.reshape(t_dim, d)
    memf = mem.reshape(b_dim * n_mem, d)
    gfin = g_final.reshape(1, d)
    pos = positions.reshape(t_dim, 1).astype(F32)

    lane = jnp.arange(LANES) % HEAD_DIM
    half = ROPE_DIM // 2
    inv_freq = ROPE_THETA ** (-jnp.arange(half, dtype=F32) / half)
    inv_lane = jnp.where(lane < ROPE_DIM, inv_freq[lane % half], 0.0).reshape(1, -1).astype(F32)
    sel_a = (lane < half).astype(F32).reshape(1, -1)
    sel_b = ((lane >= half) & (lane < ROPE_DIM)).astype(F32).reshape(1, -1)

    def rows3(t):
        return t.reshape(b_dim, s_dim, t.shape[-1])

    def rows2(t):
        return t.reshape(t_dim, t.shape[-1])

    n1 = _rms_fwd(xb, g_mix, name="rms_mix")
    proj = _mm(n1, wf["w_in"], name="mm_in")
    proj3 = rows3(proj)
    cs, sn = _rope_table(pos, inv_lane, sel_a, sel_b)
    cs3, sn3 = rows3(cs), rows3(sn)
    oa16, oa32, lse_a = _dil_fwd(proj3, cs3, sn3, sel_a, sel_b)
    ob16 = _sb_fwd(proj3)
    oa, ob = rows2(oa16), rows2(ob16)
    ua = _mm(oa, wf["w_up_a"], name="mm_up_a")
    ub = _mm(ob, wf["w_up_b"], name="mm_up_b")
    mixed = _gate_fwd(proj, ua, ub)
    h1 = _mm(mixed, wf["w_out"], name="mm_out", add=xb)

    hn = _rms_fwd(h1, g_mem_q, name="rms_mem_q")
    memn = _rms_fwd(memf, g_mem_kv, name="rms_mem_kv")
    qm = _mm(hn, wf["w_q_mem"], name="mm_q_mem", out_dtype=BF16)
    kvm = _mm(memn, wf["w_kv_mem"], name="mm_kv_mem", out_dtype=BF16)
    qm3, kvm3 = rows3(qm), kvm.reshape(b_dim, n_mem, 2 * MEM_WIDTH)
    om = rows2(_mem_fwd(qm3, kvm3))
    h2 = _mm(om, wf["w_o_mem"], name="mm_o_mem", add=h1)

    n3 = _rms_fwd(h2, g_ffn, name="rms_ffn")
    gu = _mm(n3, w_gu, name="mm_gate_up")
    act = _swiglu_fwd(gu)
    h3 = _mm(act, wf["w_ffn_down"], name="mm_down", add=h2)
    loss_row, dh3, dg_final = _final(h3, gfin, tgt)

    grads = {}
    dact = _mm(dh3, wf["w_ffn_down"], name="mm_down_dx", tb=True)
    grads["w_ffn_down"] = _mm(act, dh3, name="mm_down_dw", ta=True)
    dgu = _swiglu_bwd(gu, dact)
    dw_gu = _mm(n3, dgu, name="mm_gate_up_dw", ta=True)
    grads["w_ffn_gate"], grads["w_ffn_up"] = dw_gu[:, :D_FF], dw_gu[:, D_FF:]
    dn3 = _mm(dgu, w_gu, name="mm_gate_up_dx", tb=True)
    dh2, dg_ffn = _rms_bwd(h2, g_ffn, dn3, dh3, name="rms_ffn_bwd")

    dom = _mm(dh2, wf["w_o_mem"], name="mm_o_mem_dx", tb=True, out_dtype=BF16)
    grads["w_o_mem"] = _mm(om, dh2, name="mm_o_mem_dw", ta=True)
    dqm, dkm, dvm = _mem_bwd(qm3, kvm3, rows3(dom))
    dqm = rows2(dqm)
    dkvm = jnp.concatenate([dkm, dvm], axis=-1).reshape(b_dim * n_mem, 2 * MEM_WIDTH).astype(BF16)
    grads["w_q_mem"] = _mm(hn, dqm, name="mm_q_mem_dw", ta=True)
    dhn = _mm(dqm, wf["w_q_mem"], name="mm_q_mem_dx", tb=True)
    grads["w_kv_mem"] = _mm(memn, dkvm, name="mm_kv_mem_dw", ta=True)
    dmemn = _mm(dkvm, wf["w_kv_mem"], name="mm_kv_mem_dx", tb=True)
    _, dg_mem_kv = _rms_bwd(memf, g_mem_kv, dmemn, None, name="rms_mem_kv_bwd")
    dh1, dg_mem_q = _rms_bwd(h1, g_mem_q, dhn, dh2, name="rms_mem_q_bwd")

    dmix = _mm(dh1, wf["w_out"], name="mm_out_dx", tb=True)
    grads["w_out"] = _mm(mixed, dh1, name="mm_out_dw", ta=True)
    dua, dub, dgates = _gate_bwd(proj, ua, ub, dmix)
    doa = _mm(dua, wf["w_up_a"], name="mm_up_a_dx", tb=True)
    grads["w_up_a"] = _mm(oa, dua, name="mm_up_a_dw", ta=True)
    dob = _mm(dub, wf["w_up_b"], name="mm_up_b_dx", tb=True, out_dtype=BF16)
    grads["w_up_b"] = _mm(ob, dub, name="mm_up_b_dw", ta=True)

    dq_b, dk_b, dv_b = _sb_bwd(proj3, rows3(dob))
    dq_a, dk_a, dv_a = _dil_bwd(proj3, cs3, sn3, sel_a, sel_b, rows3(doa), oa32, lse_a)
    dproj = jnp.concatenate([rows2(t) for t in (dq_a, dk_a, dv_a, dq_b, dk_b, dv_b)] + [dgates], axis=1)
    grads["w_in"] = _mm(n1, dproj, name="mm_in_dw", ta=True)
    dn1 = _mm(dproj, wf["w_in"], name="mm_in_dx", tb=True)
    dx, dg_mix = _rms_bwd(xb, g_mix, dn1, dh1, name="rms_mix_bwd")
    grad_x = dx.reshape(b_dim, s_dim, d)
    return loss_row, grad_x, grads, (dg_mix, dg_mem_q, dg_mem_kv, dg_ffn, dg_final)


def _reduce_and_update(given, shards, loss_row, grad_x, grads, gain_grads):
    d = D_MODEL
    dg_mix, dg_mem_q, dg_mem_kv, dg_ffn, dg_final = gain_grads
    g4 = _flatten_full(grads, F32)
    half = SHARD_ROWS // 2
    c = lax.axis_index("c")
    theirs = _pair_exchange(g4)
    mine = lax.dynamic_slice_in_dim(g4, c * half, half, axis=1)
    pair = _add(mine.reshape(N_CHIPS * half, FLAT_COLS), theirs.reshape(N_CHIPS * half, FLAT_COLS),
                name="grad_pair_sum", out_dtype=BF16)
    parts = _chip_exchange(pair.reshape(N_CHIPS, half, FLAT_COLS))
    my_half = _sum4(parts, name="grad_chip_sum")
    small = jnp.concatenate([dg_mix, dg_mem_q, dg_mem_kv, dg_ffn, dg_final,
                             jnp.pad(loss_row, ((0, 0), (0, FLAT_COLS - LANES))), jnp.zeros((2, FLAT_COLS), F32)], axis=0)
    small_all = _gather_small(small)
    their_half = _swap_halves(my_half)
    shard_flat = jnp.concatenate([jnp.where(c == 0, my_half, their_half), jnp.where(c == 0, their_half, my_half)], axis=0)
    small_sum = _sum8(small_all)
    loss = small_sum[5, 0]
    gsh = _unflatten_shard(shard_flat)

    out_g, out_d, out_m, out_v = {}, {}, {}, {}
    for n, _, _, _ in SHARDED:
        w2, g2 = shards[n], gsh[n]
        dl, nm, nv = _adamw(w2, g2, given["m_" + n][0], given["v_" + n][0], name="adamw_" + n)
        out_g[n], out_d[n], out_m[n], out_v[n] = g2[None], dl[None], nm[None], nv[None]
    gain_w = jnp.concatenate([given[n].reshape(1, d) for n in GAINS], axis=0)
    gain_m = jnp.concatenate([given["m_" + n].reshape(1, d) for n in GAINS], axis=0)
    gain_v = jnp.concatenate([given["v_" + n].reshape(1, d) for n in GAINS], axis=0)
    gain_g = small_sum[:len(GAINS)]
    gd, gm, gv = _adamw(gain_w, gain_g, gain_m, gain_v, name="adamw_gains")
    for i, n in enumerate(GAINS):
        shape = given[n].shape
        out_g[n], out_d[n] = gain_g[i].reshape(shape), gd[i].reshape(shape)
        out_m[n], out_v[n] = gm[i].reshape(shape), gv[i].reshape(shape)

    order = ["g_mix", "w_in", "w_up_a", "w_up_b", "w_out", "g_mem_q", "g_mem_kv", "w_q_mem", "w_kv_mem", "w_o_mem",
             "g_ffn", "w_ffn_gate", "w_ffn_up", "w_ffn_down", "g_final"]
    return (loss, grad_x, *[out_g[n] for n in order], *[out_d[n] for n in order],
            *[out_m[n] for n in order], *[out_v[n] for n in order])
```

```python
import jax
import jax.numpy as jnp
from jax import lax
from jax.experimental import pallas as pl
from jax.experimental.pallas import tpu as pltpu

F32 = jnp.float32
BF16 = jnp.bfloat16
MESH = pl.DeviceIdType.MESH

D_MODEL = 1024
HEAD_DIM = 64
N_HEADS = 8
ATT_WIDTH = N_HEADS * HEAD_DIM
DIL_PATTERNS = ((128, 1), (512, 4), (2048, 16))
BLOCK = 128
SB_ROWS = 512
ROPE_THETA = 500000.0
ROPE_DIM = HEAD_DIM // 4
N_HEADS_MEM = 4
MEM_HEAD_DIM = 128
MEM_WIDTH = N_HEADS_MEM * MEM_HEAD_DIM
D_FF = 2816
IN_COLS = 6 * ATT_WIDTH + 2 * D_MODEL
RMS_EPS = 1e-6
ADAM_LR = 0.001
ADAM_B1 = 0.9
ADAM_B2 = 0.999
ADAM_EPS = 1e-08
ADAM_WD = 0.01
ADAM_STEP = 10

N_CHIPS = 4
LANES = 128
FLAT_COLS = 1024
VMEM_LIMIT = 56 * 1024 * 1024

PAIRS = ATT_WIDTH // LANES
COL_QA, COL_KA, COL_VA, COL_QB, COL_KB, COL_VB = (i * PAIRS for i in range(6))

MM_CAP = 1408
NN = (((1,), (0,)), ((), ()))
NT = (((1,), (1,)), ((), ()))
TN = (((0,), (0,)), ((), ()))


def _tile(dim, cap, unit=LANES):
    if dim <= cap:
        return dim
    best = None
    for t in range(unit, cap + 1, unit):
        if dim % t == 0:
            best = t
    assert best is not None, (dim, cap)
    return best


def _params(sem):
    return pltpu.CompilerParams(dimension_semantics=sem, vmem_limit_bytes=VMEM_LIMIT)


def _mm(a, b, *, name, ta=False, tb=False, add=None, out_dtype=F32,
        tm_cap=MM_CAP, tn_cap=MM_CAP, tk_cap=MM_CAP):
    if ta:
        k_dim, m_dim = a.shape
    else:
        m_dim, k_dim = a.shape
    if tb:
        n_dim, kb = b.shape
    else:
        kb, n_dim = b.shape
    assert kb == k_dim, (a.shape, b.shape, ta, tb)
    tm, tn, tk = _tile(m_dim, tm_cap), _tile(n_dim, tn_cap), _tile(k_dim, tk_cap)
    nk = k_dim // tk
    dims = (((0 if ta else 1,), (1 if tb else 0,)), ((), ()))
    has_add = add is not None

    def body(*refs):
        if has_add:
            a_ref, b_ref, add_ref, o_ref = refs[:4]
        else:
            a_ref, b_ref, o_ref = refs[:3]
        part = lax.dot_general(a_ref[...].astype(BF16), b_ref[...].astype(BF16), dims, preferred_element_type=F32)

        def finish(r):
            if has_add:
                r = add_ref[...] + r
            o_ref[...] = r.astype(out_dtype)

        if nk == 1:
            finish(part)
            return
        acc_ref = refs[-1]
        k = pl.program_id(2)

        @pl.when(k == 0)
        def _():
            acc_ref[...] = part

        @pl.when(k > 0)
        def _():
            acc_ref[...] += part

        @pl.when(k == nk - 1)
        def _():
            finish(acc_ref[...])

    a_spec = pl.BlockSpec((tk, tm), lambda i, j, k: (k, i)) if ta else pl.BlockSpec((tm, tk), lambda i, j, k: (i, k))
    b_spec = pl.BlockSpec((tn, tk), lambda i, j, k: (j, k)) if tb else pl.BlockSpec((tk, tn), lambda i, j, k: (k, j))
    o_spec = pl.BlockSpec((tm, tn), lambda i, j, k: (i, j))
    in_specs = [a_spec, b_spec] + ([o_spec] if has_add else [])
    args = (a, b) + ((add,) if has_add else ())
    return pl.pallas_call(
        body, name=name, grid=(m_dim // tm, n_dim // tn, nk),
        in_specs=in_specs, out_specs=o_spec,
        out_shape=jax.ShapeDtypeStruct((m_dim, n_dim), out_dtype),
        scratch_shapes=[pltpu.VMEM((tm, tn), F32)] if nk > 1 else [],
        compiler_params=_params(("parallel", "parallel", "arbitrary")),
    )(*args)


def _mm_core(name, a, b, a_spec, b_spec, o_spec, out_shape, grid, dims, *, add=None, out_dtype=F32):
    nk = grid[2]
    has_add = add is not None
    acc_shape = tuple(d for d in o_spec.block_shape if d is not None)

    def body(*refs):
        if has_add:
            a_ref, b_ref, add_ref, o_ref = refs[:4]
        else:
            a_ref, b_ref, o_ref = refs[:3]
        part = lax.dot_general(a_ref[...].astype(BF16), b_ref[...].astype(BF16), dims, preferred_element_type=F32)

        def finish(r):
            if has_add:
                r = add_ref[...] + r
            o_ref[...] = r.astype(out_dtype)

        if nk == 1:
            finish(part)
            return
        acc_ref = refs[-1]
        k = pl.program_id(2)

        @pl.when(k == 0)
        def _():
            acc_ref[...] = part

        @pl.when(k > 0)
        def _():
            acc_ref[...] += part

        @pl.when(k == nk - 1)
        def _():
            finish(acc_ref[...])

    in_specs = [a_spec, b_spec] + ([o_spec] if has_add else [])
    args = (a, b) + ((add,) if has_add else ())
    return pl.pallas_call(
        body, name=name, grid=grid, in_specs=in_specs, out_specs=o_spec,
        out_shape=jax.ShapeDtypeStruct(out_shape, out_dtype),
        scratch_shapes=[pltpu.VMEM(acc_shape, F32)] if nk > 1 else [],
        compiler_params=_params(("parallel", "parallel", "arbitrary")),
    )(*args)


def _mm_cs(a, w3, *, name, add=None, out_dtype=F32):
    m_dim, k_dim = a.shape
    _, _, n4 = w3.shape
    tm, tn, tk = _tile(m_dim, MM_CAP), _tile(n4, MM_CAP), _tile(k_dim, MM_CAP)
    npb = n4 // tn
    return _mm_core(name, a, w3,
                    pl.BlockSpec((tm, tk), lambda i, j, k: (i, k)),
                    pl.BlockSpec((None, tk, tn), lambda i, j, k: (j // npb, k, j % npb)),
                    pl.BlockSpec((tm, tn), lambda i, j, k: (i, j)),
                    (m_dim, N_CHIPS * n4), (m_dim // tm, N_CHIPS * npb, k_dim // tk), NN, add=add, out_dtype=out_dtype)


def _mm_cs_dx(dy, w3, *, name, out_dtype=F32):
    m_dim, _ = dy.shape
    _, k_dim, n4 = w3.shape
    tm, tkw, tn = _tile(m_dim, MM_CAP), _tile(k_dim, MM_CAP), _tile(n4, MM_CAP)
    npb = n4 // tn
    return _mm_core(name, dy, w3,
                    pl.BlockSpec((tm, tn), lambda i, j, k: (i, k)),
                    pl.BlockSpec((None, tkw, tn), lambda i, j, k: (k // npb, j, k % npb)),
                    pl.BlockSpec((tm, tkw), lambda i, j, k: (i, j)),
                    (m_dim, k_dim), (m_dim // tm, k_dim // tkw, N_CHIPS * npb), NT, out_dtype=out_dtype)


def _mm_cs_dw(a, dy, *, name):
    m_dim, k_dim = a.shape
    n4 = dy.shape[1] // N_CHIPS
    tmk, tn, tk = _tile(k_dim, MM_CAP), _tile(n4, MM_CAP), _tile(m_dim, MM_CAP)
    npb = n4 // tn
    return _mm_core(name, a, dy,
                    pl.BlockSpec((tk, tmk), lambda i, j, k: (k, i)),
                    pl.BlockSpec((tk, tn), lambda i, j, k: (k, j)),
                    pl.BlockSpec((None, tmk, tn), lambda i, j, k: (j // npb, i, j % npb)),
                    (N_CHIPS, k_dim, n4), (k_dim // tmk, N_CHIPS * npb, m_dim // tk), TN)


def _mm_ffn_up(n, w3, *, name):
    t_dim, d = n.shape
    _, _, f4 = w3.shape
    tm = _tile(t_dim, MM_CAP)
    return _mm_core(name, n, w3,
                    pl.BlockSpec((tm, d), lambda i, j, k: (i, 0)),
                    pl.BlockSpec((None, d, f4), lambda i, j, k: (j, 0, 0)),
                    pl.BlockSpec((None, tm, f4), lambda i, j, k: (j, i, 0)),
                    (N_CHIPS, t_dim, f4), (t_dim // tm, N_CHIPS, 1), NN)


def _mm_ffn_up_dw(n, d3, *, name):
    t_dim, d = n.shape
    _, _, f4 = d3.shape
    tk = _tile(t_dim, MM_CAP)
    return _mm_core(name, n, d3,
                    pl.BlockSpec((tk, d), lambda i, j, k: (k, 0)),
                    pl.BlockSpec((None, tk, f4), lambda i, j, k: (j, k, 0)),
                    pl.BlockSpec((None, d, f4), lambda i, j, k: (j, 0, 0)),
                    (N_CHIPS, d, f4), (1, N_CHIPS, t_dim // tk), TN)


def _mm_ffn_up_dx(d3, w3, *, name, add=None):
    _, t_dim, f4 = d3.shape
    _, d, _ = w3.shape
    tm = _tile(t_dim, MM_CAP)
    return _mm_core(name, d3, w3,
                    pl.BlockSpec((None, tm, f4), lambda i, j, k: (k, i, 0)),
                    pl.BlockSpec((None, d, f4), lambda i, j, k: (k, 0, 0)),
                    pl.BlockSpec((tm, d), lambda i, j, k: (i, 0)),
                    (t_dim, d), (t_dim // tm, 1, N_CHIPS), NT, add=add)


def _mm_ffn_down(act3, wd3, *, name, add):
    _, t_dim, f4 = act3.shape
    _, _, d = wd3.shape
    tm = _tile(t_dim, MM_CAP)
    return _mm_core(name, act3, wd3,
                    pl.BlockSpec((None, tm, f4), lambda i, j, k: (k, i, 0)),
                    pl.BlockSpec((None, f4, d), lambda i, j, k: (k, 0, 0)),
                    pl.BlockSpec((tm, d), lambda i, j, k: (i, 0)),
                    (t_dim, d), (t_dim // tm, 1, N_CHIPS), NN, add=add)


def _mm_ffn_down_dx(dh, wd3, *, name):
    t_dim, d = dh.shape
    _, f4, _ = wd3.shape
    tm = _tile(t_dim, MM_CAP)
    return _mm_core(name, dh, wd3,
                    pl.BlockSpec((tm, d), lambda i, j, k: (i, 0)),
                    pl.BlockSpec((None, f4, d), lambda i, j, k: (j, 0, 0)),
                    pl.BlockSpec((None, tm, f4), lambda i, j, k: (j, i, 0)),
                    (N_CHIPS, t_dim, f4), (t_dim // tm, N_CHIPS, 1), NT)


def _mm_ffn_down_dw(act3, dh, *, name):
    _, t_dim, f4 = act3.shape
    d = dh.shape[1]
    tk = _tile(t_dim, MM_CAP)
    return _mm_core(name, act3, dh,
                    pl.BlockSpec((None, tk, f4), lambda i, j, k: (i, k, 0)),
                    pl.BlockSpec((tk, d), lambda i, j, k: (k, 0)),
                    pl.BlockSpec((None, f4, d), lambda i, j, k: (i, 0, 0)),
                    (N_CHIPS, f4, d), (N_CHIPS, 1, t_dim // tk), TN)


def _rms_fwd(x, g, *, name, tt=512):
    t_dim, d = x.shape
    tt = _tile(t_dim, tt, 8)

    def body(x_ref, g_ref, o_ref):
        xv = x_ref[...]
        r = lax.rsqrt(jnp.mean(xv * xv, axis=-1, keepdims=True) + RMS_EPS)
        o_ref[...] = ((xv * r) * g_ref[...]).astype(o_ref.dtype)

    return pl.pallas_call(
        body, name=name, grid=(t_dim // tt,),
        in_specs=[pl.BlockSpec((tt, d), lambda i: (i, 0)), pl.BlockSpec((1, d), lambda i: (0, 0))],
        out_specs=pl.BlockSpec((tt, d), lambda i: (i, 0)),
        out_shape=jax.ShapeDtypeStruct((t_dim, d), BF16),
        compiler_params=_params(("parallel",)),
    )(x, g)


def _rms_bwd(x, g, dy, add, *, name, tt=512):
    t_dim, d = x.shape
    tt = _tile(t_dim, tt, 8)
    has_add = add is not None

    def body(*refs):
        if has_add:
            x_ref, g_ref, dy_ref, add_ref, dx_ref, dg_ref = refs
        else:
            x_ref, g_ref, dy_ref, dx_ref, dg_ref = refs
        xv = x_ref[...]
        dyv = dy_ref[...].astype(F32)
        r = lax.rsqrt(jnp.mean(xv * xv, axis=-1, keepdims=True) + RMS_EPS)
        xh = xv * r
        u = dyv * g_ref[...]
        dx = r * (u - xh * jnp.mean(u * xh, axis=-1, keepdims=True))
        if has_add:
            dx = add_ref[...] + dx
        dx_ref[...] = dx

        @pl.when(pl.program_id(0) == 0)
        def _():
            dg_ref[...] = jnp.zeros_like(dg_ref)

        dg_ref[...] += jnp.sum(dyv * xh, axis=0, keepdims=True)

    row = pl.BlockSpec((tt, d), lambda i: (i, 0))
    vec = pl.BlockSpec((1, d), lambda i: (0, 0))
    in_specs = [row, vec, row] + ([row] if has_add else [])
    args = (x, g, dy) + ((add,) if has_add else ())
    return pl.pallas_call(
        body, name=name, grid=(t_dim // tt,),
        in_specs=in_specs, out_specs=[row, vec],
        out_shape=[jax.ShapeDtypeStruct((t_dim, d), F32), jax.ShapeDtypeStruct((1, d), F32)],
        compiler_params=_params(("arbitrary",)),
    )(*args)


def _final(h, g, target, *, tt=512):
    t_dim, d = h.shape
    n_steps = t_dim // tt

    def body(h_ref, g_ref, t_ref, loss_ref, dh_ref, dg_ref, sq_ref):
        i = pl.program_id(0)
        xv = h_ref[...]
        gv = g_ref[...]
        r = lax.rsqrt(jnp.mean(xv * xv, axis=-1, keepdims=True) + RMS_EPS)
        xh = xv * r
        err = xh * gv - t_ref[...]
        dyv = err * (1.0 / d)
        u = dyv * gv
        dh_ref[...] = r * (u - xh * jnp.mean(u * xh, axis=-1, keepdims=True))

        @pl.when(i == 0)
        def _():
            dg_ref[...] = jnp.zeros_like(dg_ref)
            sq_ref[...] = jnp.zeros_like(sq_ref)

        dg_ref[...] += jnp.sum(dyv * xh, axis=0, keepdims=True)
        sq_ref[...] += jnp.sum(err * err, axis=0, keepdims=True)

        @pl.when(i == n_steps - 1)
        def _():
            total = jnp.sum(sq_ref[...], axis=-1, keepdims=True) * (0.5 / d)
            loss_ref[...] = jnp.broadcast_to(total, loss_ref.shape)

    row = pl.BlockSpec((tt, d), lambda i: (i, 0))
    vec = pl.BlockSpec((1, d), lambda i: (0, 0))
    return pl.pallas_call(
        body, name="final_loss", grid=(n_steps,),
        in_specs=[row, vec, row],
        out_specs=[pl.BlockSpec((1, LANES), lambda i: (0, 0)), row, vec],
        out_shape=[jax.ShapeDtypeStruct((1, LANES), F32), jax.ShapeDtypeStruct((t_dim, d), F32),
                   jax.ShapeDtypeStruct((1, d), F32)],
        scratch_shapes=[pltpu.VMEM((1, d), F32)],
        compiler_params=_params(("arbitrary",)),
    )(h, g, target)


def _rope_table(pos, inv_lane, sel_a, sel_b, *, tt=512):
    t_dim = pos.shape[0]

    def body(p_ref, f_ref, a_ref, b_ref, c_ref, s_ref):
        ang = p_ref[...] * f_ref[...]
        on = (a_ref[...] + b_ref[...]) > 0.0
        c_ref[...] = jnp.where(on, jnp.cos(ang), 1.0)
        s_ref[...] = jnp.where(on, jnp.sin(ang), 0.0)

    vec = pl.BlockSpec((1, LANES), lambda i: (0, 0))
    row = pl.BlockSpec((tt, LANES), lambda i: (i, 0))
    shp = jax.ShapeDtypeStruct((t_dim, LANES), F32)
    return pl.pallas_call(
        body, name="rope_table", grid=(t_dim // tt,),
        in_specs=[pl.BlockSpec((tt, 1), lambda i: (i, 0)), vec, vec, vec],
        out_specs=[row, row], out_shape=[shp, shp],
        compiler_params=_params(("parallel",)),
    )(pos, inv_lane, sel_a, sel_b)


def _rotate(xv, cs, sn, sa, sb):
    half = ROPE_DIM // 2
    up = pltpu.roll(xv, LANES - half, 1)
    dn = pltpu.roll(xv, half, 1)
    return xv * cs + (dn * sb - up * sa) * sn


def _head_masks():
    h1 = lax.broadcasted_iota(jnp.int32, (1, LANES), 1) < HEAD_DIM
    return h1, jnp.logical_not(h1)


def _split_heads(xv, h1, h2):
    return jnp.where(h1, xv, 0.0).astype(BF16), jnp.where(h2, xv, 0.0).astype(BF16)


def _tri_masks():
    r = lax.broadcasted_iota(jnp.int32, (BLOCK, BLOCK), 0)
    c = lax.broadcasted_iota(jnp.int32, (BLOCK, BLOCK), 1)
    return c <= r, r <= c


def _stream_rows(start, dil):
    if dil == 1:
        return pl.ds(pl.multiple_of(start, BLOCK), BLOCK)
    return pl.ds(start, BLOCK, stride=dil)


def _dil_specs(b_dim, s_dim):
    def col(c0):
        return pl.BlockSpec((None, s_dim, LANES),lambda b, h: (b, 0, c0 + h))
    tab = pl.BlockSpec((None, s_dim, LANES),lambda b, h: (b, 0, 0))
    vec = pl.BlockSpec((1, LANES), lambda b, h: (0, 0))
    return col, tab, vec


def _dil_fwd(proj3, cs3, sn3, sel_a, sel_b):
    b_dim, s_dim, _ = proj3.shape
    scale = HEAD_DIM ** -0.5
    n_pat = len(DIL_PATTERNS)

    def body(q_ref, k_ref, v_ref, cs_ref, sn_ref, sa_ref, sb_ref, o16_ref, o32_ref, l_ref, qr, kr, *per_pattern):
        og, lg = per_pattern[:n_pat], per_pattern[n_pat:]
        h1, h2 = _head_masks()
        cur_ok, prev_ok = _tri_masks()
        sa, sb = sa_ref[...], sb_ref[...]

        def prep(j, _):
            rows = pl.ds(pl.multiple_of(j * BLOCK, BLOCK), BLOCK)
            cs, sn = cs_ref[rows, :], sn_ref[rows, :]
            qr[rows, :] = _rotate(q_ref[rows, :], cs, sn, sa, sb) * scale
            kr[rows, :] = _rotate(k_ref[rows, :], cs, sn, sa, sb)
            return 0

        lax.fori_loop(0, s_dim // BLOCK, prep, 0)

        for g, (_, dil) in enumerate(DIL_PATTERNS):
            nb = s_dim // dil // BLOCK

            def one(idx, _, g=g, dil=dil, nb=nb):
                r = idx // nb
                n = idx % nb
                rows = _stream_rows(r + dil * BLOCK * n, dil)
                prow = _stream_rows(r + dil * BLOCK * jnp.maximum(n - 1, 0), dil)
                q1, q2 = _split_heads(qr[rows, :], h1, h2)
                kc = kr[rows, :].astype(BF16)
                vc1, vc2 = _split_heads(v_ref[rows, :], h1, h2)
                if nb > 1:
                    kp = kr[prow, :].astype(BF16)
                    vp1, vp2 = _split_heads(v_ref[prow, :], h1, h2)
                    p_ok = jnp.logical_and(prev_ok, n > 0)

                def head(qh, vch, vph):
                    sc = jnp.where(cur_ok, lax.dot_general(qh, kc, NT, preferred_element_type=F32), -jnp.inf)
                    m = jnp.max(sc, axis=-1, keepdims=True)
                    if nb > 1:
                        sp = jnp.where(p_ok, lax.dot_general(qh, kp, NT, preferred_element_type=F32), -jnp.inf)
                        m = jnp.maximum(m, jnp.max(sp, axis=-1, keepdims=True))
                    pc = jnp.exp(sc - m)
                    den = jnp.sum(pc, axis=-1, keepdims=True)
                    acc = jnp.dot(pc.astype(BF16), vch, preferred_element_type=F32)
                    if nb > 1:
                        pp = jnp.exp(sp - m)
                        den = den + jnp.sum(pp, axis=-1, keepdims=True)
                        acc = acc + jnp.dot(pp.astype(BF16), vph, preferred_element_type=F32)
                    return acc / den, m + jnp.log(den)

                o1, l1 = head(q1, vc1, vp1 if nb > 1 else None)
                o2, l2 = head(q2, vc2, vp2 if nb > 1 else None)
                og[g][rows, :] = o1 + o2
                lg[g][rows, :] = jnp.where(h1, l1, l2)
                return 0

            lax.fori_loop(0, dil * nb, one, 0, unroll=2)

        def comb(j, _):
            rows = pl.ds(pl.multiple_of(j * BLOCK, BLOCK), BLOCK)
            ls = [lg[g][rows, :] for g in range(n_pat)]
            m = jnp.maximum(jnp.maximum(ls[0], ls[1]), ls[2])
            es = [jnp.exp(l - m) for l in ls]
            den = es[0] + es[1] + es[2]
            o = (es[0] * og[0][rows, :] + es[1] * og[1][rows, :] + es[2] * og[2][rows, :]) / den
            o16_ref[rows, :] = o.astype(BF16)
            o32_ref[rows, :] = o
            l_ref[rows, :] = m + jnp.log(den)
            return 0

        lax.fori_loop(0, s_dim // BLOCK, comb, 0)

    col, tab, vec = _dil_specs(b_dim, s_dim)
    out = pl.BlockSpec((None, s_dim, LANES),lambda b, h: (b, 0, h))
    shp = (b_dim, s_dim, ATT_WIDTH)
    return pl.pallas_call(
        body, name="dil_fwd", grid=(b_dim, PAIRS),
        in_specs=[col(COL_QA), col(COL_KA), col(COL_VA), tab, tab, vec, vec],
        out_specs=[out, out, out],
        out_shape=[jax.ShapeDtypeStruct(shp, BF16), jax.ShapeDtypeStruct(shp, F32), jax.ShapeDtypeStruct(shp, F32)],
        scratch_shapes=[pltpu.VMEM((s_dim, LANES), F32)] * (2 + 2 * n_pat),
        compiler_params=_params(("parallel", "parallel")),
    )(proj3, proj3, proj3, cs3, sn3, sel_a, sel_b)


def _dil_bwd(proj3, cs3, sn3, sel_a, sel_b, do3, o3, lse3):
    b_dim, s_dim, _ = proj3.shape
    scale = HEAD_DIM ** -0.5

    def body(q_ref, k_ref, v_ref, cs_ref, sn_ref, sa_ref, sb_ref, do_ref, o_ref, l_ref,
             dq_ref, dk_ref, dv_ref, qr, kr, dqa, dka, dva):
        h1, h2 = _head_masks()
        cur_ok, prev_ok = _tri_masks()
        sa, sb = sa_ref[...], sb_ref[...]

        def prep(j, _):
            rows = pl.ds(pl.multiple_of(j * BLOCK, BLOCK), BLOCK)
            cs, sn = cs_ref[rows, :], sn_ref[rows, :]
            qr[rows, :] = _rotate(q_ref[rows, :], cs, sn, sa, sb) * scale
            kr[rows, :] = _rotate(k_ref[rows, :], cs, sn, sa, sb)
            zero = jnp.zeros((BLOCK, LANES), F32)
            dqa[rows, :] = zero
            dka[rows, :] = zero
            dva[rows, :] = zero
            return 0

        lax.fori_loop(0, s_dim // BLOCK, prep, 0)

        for _, dil in DIL_PATTERNS:
            nb = s_dim // dil // BLOCK

            def one(idx, _, dil=dil, nb=nb):
                r = idx // nb
                n = idx % nb
                rows = _stream_rows(r + dil * BLOCK * n, dil)
                prow = _stream_rows(r + dil * BLOCK * jnp.maximum(n - 1, 0), dil)
                q1, q2 = _split_heads(qr[rows, :], h1, h2)
                dof = do_ref[rows, :]
                do1, do2 = _split_heads(dof, h1, h2)
                prod = dof * o_ref[rows, :]
                delta1 = jnp.sum(jnp.where(h1, prod, 0.0), axis=-1, keepdims=True)
                delta2 = jnp.sum(jnp.where(h2, prod, 0.0), axis=-1, keepdims=True)
                lt = l_ref[rows, :]
                lse1 = jnp.max(jnp.where(h1, lt, -jnp.inf), axis=-1, keepdims=True)
                lse2 = jnp.max(jnp.where(h2, lt, -jnp.inf), axis=-1, keepdims=True)

                def side(krows, ok):
                    kf = kr[krows, :]
                    k16 = kf.astype(BF16)
                    k1, k2 = _split_heads(kf, h1, h2)
                    v16 = v_ref[krows, :].astype(BF16)

                    def head(qh, doh, lse, delta):
                        sc = lax.dot_general(qh, k16, NT, preferred_element_type=F32)
                        p = jnp.where(ok, jnp.exp(sc - lse), 0.0)
                        dp = lax.dot_general(doh, v16, NT, preferred_element_type=F32)
                        return p.astype(BF16), (p * (dp - delta)).astype(BF16)

                    p1, ds1 = head(q1, do1, lse1, delta1)
                    p2, ds2 = head(q2, do2, lse2, delta2)
                    dva[krows, :] += (lax.dot_general(p1, do1, TN, preferred_element_type=F32)
                                      + lax.dot_general(p2, do2, TN, preferred_element_type=F32))
                    dka[krows, :] += (lax.dot_general(ds1, q1, TN, preferred_element_type=F32)
                                      + lax.dot_general(ds2, q2, TN, preferred_element_type=F32))
                    return (jnp.dot(ds1, k1, preferred_element_type=F32)
                            + jnp.dot(ds2, k2, preferred_element_type=F32))

                dq = side(rows, cur_ok)
                if nb > 1:
                    dq = dq + side(prow, jnp.logical_and(prev_ok, n > 0))
                dqa[rows, :] += dq * scale
                return 0

            lax.fori_loop(0, dil * nb, one, 0, unroll=2)

        def finish(j, _):
            rows = pl.ds(pl.multiple_of(j * BLOCK, BLOCK), BLOCK)
            cs, sn = cs_ref[rows, :], -sn_ref[rows, :]
            dq_ref[rows, :] = _rotate(dqa[rows, :], cs, sn, sa, sb).astype(BF16)
            dk_ref[rows, :] = _rotate(dka[rows, :], cs, sn, sa, sb).astype(BF16)
            dv_ref[rows, :] = dva[rows, :].astype(BF16)
            return 0

        lax.fori_loop(0, s_dim // BLOCK, finish, 0)

    col, tab, vec = _dil_specs(b_dim, s_dim)
    out = pl.BlockSpec((None, s_dim, LANES),lambda b, h: (b, 0, h))
    shp = jax.ShapeDtypeStruct((b_dim, s_dim, ATT_WIDTH), BF16)
    acc = pltpu.VMEM((s_dim, LANES), F32)
    return pl.pallas_call(
        body, name="dil_bwd", grid=(b_dim, PAIRS),
        in_specs=[col(COL_QA), col(COL_KA), col(COL_VA), tab, tab, vec, vec, out, out, out],
        out_specs=[out, out, out], out_shape=[shp, shp, shp],
        scratch_shapes=[acc, acc, acc, acc, acc],
        compiler_params=_params(("parallel", "parallel")),
    )(proj3, proj3, proj3, cs3, sn3, sel_a, sel_b, do3, o3, lse3)


def _split_dot(x, tri):
    hi = x.astype(BF16)
    lo = (x - hi.astype(F32)).astype(BF16)
    return jnp.dot(hi, tri, preferred_element_type=F32) + jnp.dot(lo, tri, preferred_element_type=F32)


def _log_sigmoid(z):
    return jnp.minimum(z, 0.0) - jnp.log(1.0 + jnp.exp(-jnp.abs(z)))


def _sb_scores(qh, k16, valid):
    z = lax.dot_general(qh, k16, NT, preferred_element_type=F32)
    ls = _log_sigmoid(z)
    return ls, jnp.where(valid, ls - z, 0.0)


def _sb_consts():
    r = lax.broadcasted_iota(jnp.int32, (BLOCK, BLOCK), 0)
    c = lax.broadcasted_iota(jnp.int32, (BLOCK, BLOCK), 1)
    after = (r > c).astype(BF16)
    before = (r < c).astype(BF16)
    qrow = lax.broadcasted_iota(jnp.int32, (SB_ROWS, BLOCK), 0)
    kcol = lax.broadcasted_iota(jnp.int32, (SB_ROWS, BLOCK), 1)
    return after, before, qrow, kcol


def _sb_fwd(proj3):
    b_dim, s_dim, _ = proj3.shape
    scale = HEAD_DIM ** -0.5
    per = SB_ROWS // BLOCK

    def body(q_ref, k_ref, v_ref, o_ref):
        h1, h2 = _head_masks()
        after, _, qrow, kcol = _sb_consts()

        def qloop(qi, _):
            rows = pl.ds(pl.multiple_of(qi * SB_ROWS, SB_ROWS), SB_ROWS)
            q1, q2 = _split_heads(q_ref[rows, :] * scale, h1, h2)
            qpos = qi * SB_ROWS + qrow
            nkb = (qi + 1) * per

            def kloop(i, carry):
                acc, run1, run2 = carry
                kb = nkb - 1 - i
                krows = pl.ds(pl.multiple_of(kb * BLOCK, BLOCK), BLOCK)
                k16 = k_ref[krows, :].astype(BF16)
                v1, v2 = _split_heads(v_ref[krows, :], h1, h2)
                valid = (kb * BLOCK + kcol) < qpos

                def head(qh, vh, run):
                    ls, l1m = _sb_scores(qh, k16, valid)
                    a = jnp.where(valid, jnp.exp(ls + _split_dot(l1m, after) + run), 0.0)
                    return (jnp.dot(a.astype(BF16), vh, preferred_element_type=F32),
                            run + jnp.sum(l1m, axis=-1, keepdims=True))

                o1, run1 = head(q1, v1, run1)
                o2, run2 = head(q2, v2, run2)
                return acc + o1 + o2, run1, run2

            zcol = jnp.zeros((SB_ROWS, 1), F32)
            acc, _, _ = lax.fori_loop(0, nkb // 2, lambda i, c: kloop(2 * i + 1, kloop(2 * i, c)),
                                      (jnp.zeros((SB_ROWS, LANES), F32), zcol, zcol))
            o_ref[rows, :] = acc.astype(BF16)
            return 0

        lax.fori_loop(0, s_dim // SB_ROWS, qloop, 0)

    def col(c0):
        return pl.BlockSpec((None, s_dim, LANES),lambda b, h: (b, 0, c0 + h))

    return pl.pallas_call(
        body, name="sb_fwd", grid=(b_dim, PAIRS),
        in_specs=[col(COL_QB), col(COL_KB), col(COL_VB)], out_specs=col(0),
        out_shape=jax.ShapeDtypeStruct((b_dim, s_dim, ATT_WIDTH), BF16),
        compiler_params=_params(("parallel", "parallel")),
    )(proj3, proj3, proj3)


def _sb_bwd(proj3, do3):
    b_dim, s_dim, _ = proj3.shape
    scale = HEAD_DIM ** -0.5
    per = SB_ROWS // BLOCK
    nkb_max = s_dim // BLOCK

    def body(q_ref, k_ref, v_ref, do_ref, dq_ref, dk_ref, dv_ref, dka, dva, e_ref, sg_ref):
        h1, h2 = _head_masks()
        after, before, qrow, kcol = _sb_consts()
        dka[...] = jnp.zeros_like(dka)
        dva[...] = jnp.zeros_like(dva)

        def qloop(qi, _):
            rows = pl.ds(pl.multiple_of(qi * SB_ROWS, SB_ROWS), SB_ROWS)
            q1, q2 = _split_heads(q_ref[rows, :] * scale, h1, h2)
            do1, do2 = _split_heads(do_ref[rows, :].astype(F32), h1, h2)
            qpos = qi * SB_ROWS + qrow
            nkb = (qi + 1) * per

            def pass1(i, carry):
                run1, run2 = carry
                kb = nkb - 1 - i
                krows = pl.ds(pl.multiple_of(kb * BLOCK, BLOCK), BLOCK)
                k16 = k_ref[krows, :].astype(BF16)
                v16 = v_ref[krows, :].astype(BF16)
                valid = (kb * BLOCK + kcol) < qpos

                def head(h, qh, doh, run):
                    ls, l1m = _sb_scores(qh, k16, valid)
                    a = jnp.where(valid, jnp.exp(ls + _split_dot(l1m, after) + run), 0.0)
                    da = lax.dot_general(doh, v16, NT, preferred_element_type=F32)
                    e_ref[h, kb] = a * da
                    sg_ref[h, kb] = jnp.exp(ls)
                    return a.astype(BF16), run + jnp.sum(l1m, axis=-1, keepdims=True)

                a1, run1 = head(0, q1, do1, run1)
                a2, run2 = head(1, q2, do2, run2)
                dva[krows, :] += (lax.dot_general(a1, do1, TN, preferred_element_type=F32)
                                  + lax.dot_general(a2, do2, TN, preferred_element_type=F32))
                return run1, run2

            zcol = jnp.zeros((SB_ROWS, 1), F32)
            lax.fori_loop(0, nkb // 2, lambda i, c: pass1(2 * i + 1, pass1(2 * i, c)), (zcol, zcol))

            def pass2(kb, carry):
                dq, pre1, pre2 = carry
                krows = pl.ds(pl.multiple_of(kb * BLOCK, BLOCK), BLOCK)
                k1, k2 = _split_heads(k_ref[krows, :], h1, h2)
                valid = (kb * BLOCK + kcol) < qpos

                def head(h, pre):
                    ev = e_ref[h, kb]
                    sg = sg_ref[h, kb]
                    prefix = _split_dot(ev, before) + pre
                    dz = jnp.where(valid, ev * (1.0 - sg) - prefix * sg, 0.0).astype(BF16)
                    return dz, pre + jnp.sum(ev, axis=-1, keepdims=True)

                dz1, pre1 = head(0, pre1)
                dz2, pre2 = head(1, pre2)
                dka[krows, :] += (lax.dot_general(dz1, q1, TN, preferred_element_type=F32)
                                  + lax.dot_general(dz2, q2, TN, preferred_element_type=F32))
                dq = dq + jnp.dot(dz1, k1, preferred_element_type=F32) + jnp.dot(dz2, k2, preferred_element_type=F32)
                return dq, pre1, pre2

            dq, _, _ = lax.fori_loop(0, nkb // 2, lambda i, c: pass2(2 * i + 1, pass2(2 * i, c)),
                                     (jnp.zeros((SB_ROWS, LANES), F32), zcol, zcol))
            dq_ref[rows, :] = (dq * scale).astype(BF16)
            return 0

        lax.fori_loop(0, s_dim // SB_ROWS, qloop, 0)
        dk_ref[...] = dka[...].astype(BF16)
        dv_ref[...] = dva[...].astype(BF16)

    def col(c0):
        return pl.BlockSpec((None, s_dim, LANES),lambda b, h: (b, 0, c0 + h))

    shp = jax.ShapeDtypeStruct((b_dim, s_dim, ATT_WIDTH), BF16)
    acc = pltpu.VMEM((s_dim, LANES), F32)
    strip = pltpu.VMEM((2, nkb_max, SB_ROWS, BLOCK), F32)
    return pl.pallas_call(
        body, name="sb_bwd", grid=(b_dim, PAIRS),
        in_specs=[col(COL_QB), col(COL_KB), col(COL_VB), col(0)], out_specs=[col(0), col(0), col(0)],
        out_shape=[shp, shp, shp],
        scratch_shapes=[acc, acc, strip, strip],
        compiler_params=_params(("parallel", "parallel")),
    )(proj3, proj3, proj3, do3)


def _sigmoid(x):
    return 1.0 / (1.0 + jnp.exp(-x))


def _gate_fwd(proj, ua, ub, *, tt=512):
    t_dim, d = ua.shape

    def body(ga_ref, gb_ref, ua_ref, ub_ref, o_ref):
        o_ref[...] = (_sigmoid(ga_ref[...]) * ua_ref[...] + _sigmoid(gb_ref[...]) * ub_ref[...]).astype(BF16)

    row = pl.BlockSpec((tt, d), lambda i: (i, 0))
    return pl.pallas_call(
        body, name="gate_fwd", grid=(t_dim // tt,),
        in_specs=[pl.BlockSpec((tt, d), lambda i: (i, 3)), pl.BlockSpec((tt, d), lambda i: (i, 4)), row, row],
        out_specs=row, out_shape=jax.ShapeDtypeStruct((t_dim, d), BF16),
        compiler_params=_params(("parallel",)),
    )(proj, proj, ua, ub)


def _gate_bwd(proj, ua, ub, dmix, *, tt=512):
    t_dim, d = ua.shape

    def body(ga_ref, gb_ref, ua_ref, ub_ref, dm_ref, dua_ref, dub_ref, dg_ref):
        dm = dm_ref[...]
        sa = _sigmoid(ga_ref[...])
        sb = _sigmoid(gb_ref[...])
        dua_ref[...] = (dm * sa).astype(BF16)
        dub_ref[...] = (dm * sb).astype(BF16)
        dg_ref[:, :d] = (dm * ua_ref[...] * (sa * (1.0 - sa))).astype(BF16)
        dg_ref[:, d:] = (dm * ub_ref[...] * (sb * (1.0 - sb))).astype(BF16)

    row = pl.BlockSpec((tt, d), lambda i: (i, 0))
    wide = pl.BlockSpec((tt, 2 * d), lambda i: (i, 0))
    return pl.pallas_call(
        body, name="gate_bwd", grid=(t_dim // tt,),
        in_specs=[pl.BlockSpec((tt, d), lambda i: (i, 3)), pl.BlockSpec((tt, d), lambda i: (i, 4)), row, row, row],
        out_specs=[row, row, wide],
        out_shape=[jax.ShapeDtypeStruct((t_dim, d), BF16), jax.ShapeDtypeStruct((t_dim, d), BF16),
                   jax.ShapeDtypeStruct((t_dim, 2 * d), BF16)],
        compiler_params=_params(("parallel",)),
    )(proj, proj, ua, ub, dmix)


def _swiglu_fwd(g3, u3, *, tt=1024):
    n, t_dim, f4 = g3.shape

    def body(g_ref, u_ref, o_ref):
        gv = g_ref[...]
        o_ref[...] = (gv * _sigmoid(gv) * u_ref[...]).astype(BF16)

    spec = pl.BlockSpec((None, tt, f4), lambda s, i: (s, i, 0))
    return pl.pallas_call(
        body, name="swiglu_fwd", grid=(n, t_dim // tt), in_specs=[spec, spec], out_specs=spec,
        out_shape=jax.ShapeDtypeStruct(g3.shape, BF16),
        compiler_params=_params(("parallel", "parallel")),
    )(g3, u3)


def _swiglu_bwd(g3, u3, dact3, *, tt=1024):
    n, t_dim, f4 = g3.shape

    def body(g_ref, u_ref, da_ref, dg_ref, du_ref):
        gv = g_ref[...]
        da = da_ref[...]
        sg = _sigmoid(gv)
        dg_ref[...] = (da * u_ref[...] * (sg + gv * sg * (1.0 - sg))).astype(BF16)
        du_ref[...] = (da * (gv * sg)).astype(BF16)

    spec = pl.BlockSpec((None, tt, f4), lambda s, i: (s, i, 0))
    shp = jax.ShapeDtypeStruct(g3.shape, BF16)
    return pl.pallas_call(
        body, name="swiglu_bwd", grid=(n, t_dim // tt), in_specs=[spec, spec, spec], out_specs=[spec, spec],
        out_shape=[shp, shp],
        compiler_params=_params(("parallel", "parallel")),
    )(g3, u3, dact3)


def _mem_fwd(qm, kvm, *, tt=512):
    b_dim, s_dim, _ = qm.shape
    n_mem = kvm.shape[1]
    scale = MEM_HEAD_DIM ** -0.5

    def body(q_ref, k_ref, v_ref, o_ref):
        sc = lax.dot_general(q_ref[0], k_ref[0], NT, preferred_element_type=F32) * scale
        p = jnp.exp(sc - jnp.max(sc, axis=-1, keepdims=True))
        p = p / jnp.sum(p, axis=-1, keepdims=True)
        o_ref[0] = jnp.dot(p.astype(BF16), v_ref[0], preferred_element_type=F32).astype(BF16)

    qs = pl.BlockSpec((1, tt, MEM_HEAD_DIM), lambda b, h, i: (b, i, h))
    return pl.pallas_call(
        body, name="mem_fwd", grid=(b_dim, N_HEADS_MEM, s_dim // tt),
        in_specs=[qs, pl.BlockSpec((1, n_mem, MEM_HEAD_DIM), lambda b, h, i: (b, 0, h)),
                  pl.BlockSpec((1, n_mem, MEM_HEAD_DIM), lambda b, h, i: (b, 0, N_HEADS_MEM + h))],
        out_specs=qs, out_shape=jax.ShapeDtypeStruct(qm.shape, BF16),
        compiler_params=_params(("parallel", "parallel", "parallel")),
    )(qm, kvm, kvm)


def _mem_bwd(qm, kvm, dom, *, tt=512):
    b_dim, s_dim, _ = qm.shape
    n_mem = kvm.shape[1]
    scale = MEM_HEAD_DIM ** -0.5

    def body(q_ref, k_ref, v_ref, do_ref, dq_ref, dk_ref, dv_ref):
        qv, kv, vv, dov = q_ref[0], k_ref[0], v_ref[0], do_ref[0]
        sc = lax.dot_general(qv, kv, NT, preferred_element_type=F32) * scale
        p = jnp.exp(sc - jnp.max(sc, axis=-1, keepdims=True))
        p = p / jnp.sum(p, axis=-1, keepdims=True)
        dp = lax.dot_general(dov, vv, NT, preferred_element_type=F32)
        ds = (p * (dp - jnp.sum(p * dp, axis=-1, keepdims=True)) * scale).astype(BF16)
        dq_ref[0] = jnp.dot(ds, kv, preferred_element_type=F32).astype(BF16)

        @pl.when(pl.program_id(2) == 0)
        def _():
            dk_ref[...] = jnp.zeros_like(dk_ref)
            dv_ref[...] = jnp.zeros_like(dv_ref)

        dk_ref[0] += lax.dot_general(ds, qv, TN, preferred_element_type=F32)
        dv_ref[0] += lax.dot_general(p.astype(BF16), dov, TN, preferred_element_type=F32)

    qs = pl.BlockSpec((1, tt, MEM_HEAD_DIM), lambda b, h, i: (b, i, h))
    ks = pl.BlockSpec((1, n_mem, MEM_HEAD_DIM), lambda b, h, i: (b, 0, h))
    vs = pl.BlockSpec((1, n_mem, MEM_HEAD_DIM), lambda b, h, i: (b, 0, N_HEADS_MEM + h))
    return pl.pallas_call(
        body, name="mem_bwd", grid=(b_dim, N_HEADS_MEM, s_dim // tt),
        in_specs=[qs, ks, vs, qs], out_specs=[qs, ks, ks],
        out_shape=[jax.ShapeDtypeStruct(qm.shape, BF16), jax.ShapeDtypeStruct((b_dim, n_mem, MEM_WIDTH), F32),
                   jax.ShapeDtypeStruct((b_dim, n_mem, MEM_WIDTH), F32)],
        compiler_params=_params(("parallel", "parallel", "arbitrary")),
    )(qm, kvm, kvm, dom)


def _adamw_math(wv, gv, mv, vv):
    nm = ADAM_B1 * mv + (1.0 - ADAM_B1) * gv
    nv = ADAM_B2 * vv + (1.0 - ADAM_B2) * (gv * gv)
    m_hat = nm / (1.0 - ADAM_B1 ** ADAM_STEP)
    v_hat = nv / (1.0 - ADAM_B2 ** ADAM_STEP)
    return -ADAM_LR * (m_hat / (jnp.sqrt(v_hat) + ADAM_EPS) + ADAM_WD * wv), nm, nv


def _adamw(w, g, m, v, *, name):
    rows, cols = w.shape
    tr = _tile(rows, 256, 8)

    def body(w_ref, g_ref, m_ref, v_ref, d_ref, nm_ref, nv_ref):
        d_ref[...], nm_ref[...], nv_ref[...] = _adamw_math(w_ref[...], g_ref[...], m_ref[...], v_ref[...])

    spec = pl.BlockSpec((tr, cols), lambda i: (i, 0))
    shp = jax.ShapeDtypeStruct((rows, cols), F32)
    return pl.pallas_call(
        body, name=name, grid=(rows // tr,),
        in_specs=[spec] * 4, out_specs=[spec] * 3, out_shape=[shp] * 3,
        compiler_params=_params(("parallel",)),
    )(w, g, m, v)


def _prefetch_spec(grid, in_specs, out_specs):
    return pltpu.PrefetchScalarGridSpec(num_scalar_prefetch=1, grid=grid, in_specs=in_specs, out_specs=out_specs)


def _adamw_halves(w, mine, theirs, m, v, c_idx, *, name):
    rows, cols = w.shape
    half = rows // 2
    tr = _tile(half, 256, 8)
    nh = half // tr

    def body(c_ref, w_ref, mine_ref, theirs_ref, m_ref, v_ref, g_ref, d_ref, nm_ref, nv_ref):
        gv = jnp.where(pl.program_id(0) == c_ref[0], mine_ref[...], theirs_ref[...])
        g_ref[...] = gv
        d_ref[...], nm_ref[...], nv_ref[...] = _adamw_math(w_ref[...], gv, m_ref[...], v_ref[...])

    full = pl.BlockSpec((tr, cols), lambda h, i, c_ref: (h * nh + i, 0))
    part = pl.BlockSpec((tr, cols), lambda h, i, c_ref: (i, 0))
    shp = jax.ShapeDtypeStruct((rows, cols), F32)
    return pl.pallas_call(
        body, name=name, grid_spec=_prefetch_spec((2, nh), [full, part, part, full, full], [full] * 4),
        out_shape=[shp] * 4,
        compiler_params=_params(("parallel", "parallel")),
    )(c_idx, w, mine, theirs, m, v)


def _pair_sum(g3, theirs, c_idx, *, name):
    n, rows, cols = g3.shape
    half = rows // 2
    tr = _tile(half, 256, 16)

    def body(c_ref, g_ref, t_ref, o_ref):
        o_ref[...] = (g_ref[...] + t_ref[...]).astype(BF16)

    part = pl.BlockSpec((None, tr, cols), lambda s, i, c_ref: (s, i, 0))
    return pl.pallas_call(
        body, name=name,
        grid_spec=_prefetch_spec((n, half // tr),
                                 [pl.BlockSpec((None, None, tr, cols), lambda s, i, c_ref: (s, c_ref[0], i, 0)), part],
                                 part),
        out_shape=jax.ShapeDtypeStruct((n, half, cols), BF16),
        compiler_params=_params(("parallel", "parallel")),
    )(c_idx, g3.reshape(n, 2, half, cols), theirs)


def _chip_sum(pair, recv, s_idx, *, name):
    _, half, cols = pair.shape
    tr = _tile(half, 256, 16)

    def body(s_ref, p_ref, r_ref, o_ref):
        o_ref[...] = ((p_ref[...].astype(F32) + r_ref[0].astype(F32)) + r_ref[1].astype(F32)) + r_ref[2].astype(F32)

    return pl.pallas_call(
        body, name=name,
        grid_spec=_prefetch_spec((half // tr,),
                                 [pl.BlockSpec((None, tr, cols), lambda i, s_ref: (s_ref[0], i, 0)),
                                  pl.BlockSpec((N_CHIPS - 1, tr, cols), lambda i, s_ref: (0, i, 0))],
                                 pl.BlockSpec((tr, cols), lambda i, s_ref: (i, 0))),
        out_shape=jax.ShapeDtypeStruct((half, cols), F32),
        compiler_params=_params(("parallel",)),
    )(s_idx, pair, recv)


def _sum8(parts):
    n, rows, cols = parts.shape

    def body(p_ref, o_ref):
        acc = p_ref[0]
        for i in range(1, n):
            acc = acc + p_ref[i]
        o_ref[...] = acc

    return pl.pallas_call(
        body, name="small_sum", grid=(1,),
        in_specs=[pl.BlockSpec((n, rows, cols), lambda i: (0, 0, 0))],
        out_specs=pl.BlockSpec((rows, cols), lambda i: (0, 0)),
        out_shape=jax.ShapeDtypeStruct((rows, cols), parts.dtype),
        compiler_params=_params(("arbitrary",)),
    )(parts)


def _place():
    return lax.axis_index("x"), lax.axis_index("y"), lax.axis_index("c")


ANY = pl.BlockSpec(memory_space=pl.ANY)


def _gather_weights(shards):
    n = len(shards)
    halves = [s.shape[0] // 2 for s in shards]

    def body(*refs):
        ins, outs, send_sems, recv_sems = refs[:n], refs[n:2 * n], refs[2 * n], refs[2 * n + 1]
        x, y, c = _place()
        my_chip = 2 * x + y
        me, sibling = (x, y, c), (x, y, 1 - c)
        chips = [(1 - x, y), (x, 1 - y), (1 - x, 1 - y)]

        def half_of(w, chip, pc):
            return outs[w].at[chip, pl.ds(pc * halves[w], halves[w]), :]

        def copy(w, k, src, dst, to):
            return pltpu.make_async_remote_copy(
                src_ref=src, dst_ref=dst, send_sem=send_sems.at[7 * w + k], recv_sem=recv_sems.at[7 * w + k],
                device_id=to, device_id_type=MESH)

        first = []
        for w in range(n):
            first.append(copy(w, 0, ins[w], outs[w].at[my_chip], sibling))
            mine = ins[w].at[pl.ds(c * halves[w], halves[w]), :]
            for j, (px, py) in enumerate(chips):
                first.append(copy(w, 1 + j, mine, half_of(w, my_chip, c), (px, py, c)))
        for cp in first:
            cp.start()
        passed = []
        for w in range(n):
            for j, (px, py) in enumerate(chips):
                landed = half_of(w, 2 * px + py, c)
                copy(w, 1 + j, landed, landed, me).wait_recv()
                passed.append(copy(w, 4 + j, landed, landed, sibling))
                passed[-1].start()
        for w in range(n):
            copy(w, 0, ins[w], outs[w].at[my_chip], me).wait_recv()
            for j, (px, py) in enumerate(chips):
                landed = half_of(w, 2 * px + py, 1 - c)
                copy(w, 4 + j, landed, landed, me).wait_recv()
        for cp in first + passed:
            cp.wait_send()

    return pl.pallas_call(
        body, name="gather_weights",
        out_shape=[jax.ShapeDtypeStruct((N_CHIPS,) + s.shape, s.dtype) for s in shards],
        in_specs=[ANY] * n, out_specs=[ANY] * n,
        scratch_shapes=[pltpu.SemaphoreType.DMA((7 * n,)), pltpu.SemaphoreType.DMA((7 * n,))],
    )(*shards)


def _pair_exchange(grads):
    n = len(grads)
    halves = [g.shape[1] // 2 for g in grads]

    def body(*refs):
        ins, outs, send_sems, recv_sems = refs[:n], refs[n:2 * n], refs[2 * n], refs[2 * n + 1]
        x, y, c = _place()
        copies = [pltpu.make_async_remote_copy(
            src_ref=ins[w].at[:, pl.ds((1 - c) * halves[w], halves[w]), :], dst_ref=outs[w],
            send_sem=send_sems.at[w], recv_sem=recv_sems.at[w], device_id=(x, y, 1 - c), device_id_type=MESH)
            for w in range(n)]
        for cp in copies:
            cp.start()
        for cp in copies:
            cp.wait()

    return pl.pallas_call(
        body, name="grad_pair_exchange",
        out_shape=[jax.ShapeDtypeStruct((g.shape[0], h, g.shape[2]), g.dtype) for g, h in zip(grads, halves)],
        in_specs=[ANY] * n, out_specs=[ANY] * n,
        scratch_shapes=[pltpu.SemaphoreType.DMA((n,)), pltpu.SemaphoreType.DMA((n,))],
    )(*grads)


def _chip_exchange(pairs):
    n = len(pairs)

    def body(*refs):
        ins, outs, send_sems, recv_sems = refs[:n], refs[n:2 * n], refs[2 * n], refs[2 * n + 1]
        x, y, c = _place()
        others = [(1 - x, y), (x, 1 - y), (1 - x, 1 - y)]
        copies = [pltpu.make_async_remote_copy(
            src_ref=ins[w].at[2 * px + py], dst_ref=outs[w].at[j],
            send_sem=send_sems.at[3 * w + j], recv_sem=recv_sems.at[3 * w + j],
            device_id=(px, py, c), device_id_type=MESH)
            for w in range(n) for j, (px, py) in enumerate(others)]
        for cp in copies:
            cp.start()
        for cp in copies:
            cp.wait()

    return pl.pallas_call(
        body, name="grad_chip_exchange",
        out_shape=[jax.ShapeDtypeStruct((N_CHIPS - 1,) + p.shape[1:], p.dtype) for p in pairs],
        in_specs=[ANY] * n, out_specs=[ANY] * n,
        scratch_shapes=[pltpu.SemaphoreType.DMA((3 * n,)), pltpu.SemaphoreType.DMA((3 * n,))],
    )(*pairs)


def _swap_halves(mine):
    n = len(mine)

    def body(*refs):
        ins, outs, send_sems, recv_sems = refs[:n], refs[n:2 * n], refs[2 * n], refs[2 * n + 1]
        x, y, c = _place()
        copies = [pltpu.make_async_remote_copy(
            src_ref=ins[w], dst_ref=outs[w], send_sem=send_sems.at[w], recv_sem=recv_sems.at[w],
            device_id=(x, y, 1 - c), device_id_type=MESH) for w in range(n)]
        for cp in copies:
            cp.start()
        for cp in copies:
            cp.wait()

    return pl.pallas_call(
        body, name="grad_swap_halves",
        out_shape=[jax.ShapeDtypeStruct(h.shape, h.dtype) for h in mine],
        in_specs=[ANY] * n, out_specs=[ANY] * n,
        scratch_shapes=[pltpu.SemaphoreType.DMA((n,)), pltpu.SemaphoreType.DMA((n,))],
    )(*mine)


def _gather_small(small):
    srows, cols = small.shape

    def body(s_ref, all_ref, send_sems, recv_sems, local_sem):
        x, y, c = _place()
        me = 4 * x + 2 * y + c
        keep_small = pltpu.make_async_copy(s_ref, all_ref.at[me], local_sem)
        keep_small.start()
        sends = []
        for kk in range(1, 8):
            peer = (x ^ (kk >> 2), y ^ ((kk >> 1) & 1), c ^ (kk & 1))
            sends.append(pltpu.make_async_remote_copy(
                src_ref=s_ref, dst_ref=all_ref.at[me],
                send_sem=send_sems.at[kk], recv_sem=recv_sems.at[kk], device_id=peer, device_id_type=MESH))
        for cp in sends:
            cp.start()
        for kk in range(1, 8):
            px, py, pc = x ^ (kk >> 2), y ^ ((kk >> 1) & 1), c ^ (kk & 1)
            pltpu.make_async_remote_copy(
                src_ref=s_ref, dst_ref=all_ref.at[4 * px + 2 * py + pc],
                send_sem=send_sems.at[kk], recv_sem=recv_sems.at[kk], device_id=(px, py, pc),
                device_id_type=MESH).wait_recv()
        for cp in sends:
            cp.wait_send()
        keep_small.wait()

    return pl.pallas_call(
        body, name="gather_small",
        out_shape=jax.ShapeDtypeStruct((8, srows, cols), small.dtype),
        in_specs=[ANY], out_specs=ANY,
        scratch_shapes=[pltpu.SemaphoreType.DMA((8,)), pltpu.SemaphoreType.DMA((8,)), pltpu.SemaphoreType.DMA],
    )(small)


SHARDED = (("w_in", D_MODEL, IN_COLS, 1), ("w_up_a", ATT_WIDTH, D_MODEL, 1), ("w_up_b", ATT_WIDTH, D_MODEL, 1),
           ("w_out", D_MODEL, D_MODEL, 0), ("w_q_mem", D_MODEL, MEM_WIDTH, 0), ("w_kv_mem", D_MODEL, 2 * MEM_WIDTH, 0),
           ("w_o_mem", MEM_WIDTH, D_MODEL, 1), ("w_ffn_gate", D_MODEL, D_FF, 1), ("w_ffn_up", D_MODEL, D_FF, 1),
           ("w_ffn_down", D_FF, D_MODEL, 0))
NAMES = tuple(n for n, _, _, _ in SHARDED)
GAINS = ("g_mix", "g_mem_q", "g_mem_kv", "g_ffn", "g_final")


def _natural(w3):
    n, r, c = w3.shape
    return w3.reshape(n * r, c)


def _shard_major(g, axis):
    if axis == 1:
        return g
    r, c = g.shape
    return g.reshape(N_CHIPS, r // N_CHIPS, c)


def kernel(x, mem, positions, g_mix, w_in, w_up_a, w_up_b, w_out, g_mem_q, g_mem_kv, w_q_mem, w_kv_mem, w_o_mem, g_ffn, w_ffn_gate, w_ffn_up, w_ffn_down, g_final, loss_target, m_g_mix, m_w_in, m_w_up_a, m_w_up_b, m_w_out, m_g_mem_q, m_g_mem_kv, m_w_q_mem, m_w_kv_mem, m_w_o_mem, m_g_ffn, m_w_ffn_gate, m_w_ffn_up, m_w_ffn_down, m_g_final, v_g_mix, v_w_in, v_w_up_a, v_w_up_b, v_w_out, v_g_mem_q, v_g_mem_kv, v_w_q_mem, v_w_kv_mem, v_w_o_mem, v_g_ffn, v_w_ffn_gate, v_w_ffn_up, v_w_ffn_down, v_g_final):
    given = dict(locals())
    shards = {n: given[n][0] for n, _, _, _ in SHARDED}

    wf = dict(zip(NAMES, _gather_weights([shards[n].astype(BF16) for n in NAMES])))

    loss_row, grad_x, grads, gain_grads = _local_step(x, mem, positions, loss_target, g_mix, g_mem_q, g_mem_kv,
                                                      g_ffn, g_final, wf)
    return _reduce_and_update(given, shards, loss_row, grad_x, grads, gain_grads)


def _local_step(x, mem, positions, loss_target, g_mix, g_mem_q, g_mem_kv, g_ffn, g_final, wf):
    b_dim, s_dim, d = x.shape
    t_dim = b_dim * s_dim
    n_mem = mem.shape[1]
    w_out, w_q, w_kv = _natural(wf["w_out"]), _natural(wf["w_q_mem"]), _natural(wf["w_kv_mem"])

    xb = x.reshape(t_dim, d)
    tgt = loss_target.reshape(t_dim, d)
    memf = mem.reshape(b_dim * n_mem, d)
    gfin = g_final.reshape(1, d)
    pos = positions.reshape(t_dim, 1).astype(F32)

    lane = jnp.arange(LANES) % HEAD_DIM
    half = ROPE_DIM // 2
    inv_freq = ROPE_THETA ** (-jnp.arange(half, dtype=F32) / half)
    inv_lane = jnp.where(lane < ROPE_DIM, inv_freq[lane % half], 0.0).reshape(1, -1).astype(F32)
    sel_a = (lane < half).astype(F32).reshape(1, -1)
    sel_b = ((lane >= half) & (lane < ROPE_DIM)).astype(F32).reshape(1, -1)

    def rows3(t):
        return t.reshape(b_dim, s_dim, t.shape[-1])

    def rows2(t):
        return t.reshape(t_dim, t.shape[-1])

    n1 = _rms_fwd(xb, g_mix, name="rms_mix")
    proj = _mm_cs(n1, wf["w_in"], name="mm_in")
    proj3 = rows3(proj)
    cs, sn = _rope_table(pos, inv_lane, sel_a, sel_b)
    cs3, sn3 = rows3(cs), rows3(sn)
    oa16, oa32, lse_a = _dil_fwd(proj3, cs3, sn3, sel_a, sel_b)
    ob16 = _sb_fwd(proj3)
    oa, ob = rows2(oa16), rows2(ob16)
    ua = _mm_cs(oa, wf["w_up_a"], name="mm_up_a")
    ub = _mm_cs(ob, wf["w_up_b"], name="mm_up_b")
    mixed = _gate_fwd(proj, ua, ub)
    h1 = _mm(mixed, w_out, name="mm_out", add=xb)

    hn = _rms_fwd(h1, g_mem_q, name="rms_mem_q")
    memn = _rms_fwd(memf, g_mem_kv, name="rms_mem_kv")
    qm = _mm(hn, w_q, name="mm_q_mem", out_dtype=BF16)
    kvm = _mm(memn, w_kv, name="mm_kv_mem", out_dtype=BF16)
    qm3, kvm3 = rows3(qm), kvm.reshape(b_dim, n_mem, 2 * MEM_WIDTH)
    om = rows2(_mem_fwd(qm3, kvm3))
    h2 = _mm_cs(om, wf["w_o_mem"], name="mm_o_mem", add=h1)

    n3 = _rms_fwd(h2, g_ffn, name="rms_ffn")
    gate3 = _mm_ffn_up(n3, wf["w_ffn_gate"], name="mm_gate")
    up3 = _mm_ffn_up(n3, wf["w_ffn_up"], name="mm_up")
    act3 = _swiglu_fwd(gate3, up3)
    h3 = _mm_ffn_down(act3, wf["w_ffn_down"], name="mm_down", add=h2)
    loss_row, dh3, dg_final = _final(h3, gfin, tgt)

    grads = {}
    dact3 = _mm_ffn_down_dx(dh3, wf["w_ffn_down"], name="mm_down_dx")
    grads["w_ffn_down"] = _mm_ffn_down_dw(act3, dh3, name="mm_down_dw")
    dgate3, dup3 = _swiglu_bwd(gate3, up3, dact3)
    grads["w_ffn_gate"] = _mm_ffn_up_dw(n3, dgate3, name="mm_gate_dw")
    grads["w_ffn_up"] = _mm_ffn_up_dw(n3, dup3, name="mm_up_dw")
    dn3 = _mm_ffn_up_dx(dgate3, wf["w_ffn_gate"], name="mm_gate_dx")
    dn3 = _mm_ffn_up_dx(dup3, wf["w_ffn_up"], name="mm_up_dx", add=dn3)
    dh2, dg_ffn = _rms_bwd(h2, g_ffn, dn3, dh3, name="rms_ffn_bwd")

    dom = _mm_cs_dx(dh2, wf["w_o_mem"], name="mm_o_mem_dx", out_dtype=BF16)
    grads["w_o_mem"] = _mm_cs_dw(om, dh2, name="mm_o_mem_dw")
    dqm, dkm, dvm = _mem_bwd(qm3, kvm3, rows3(dom))
    dqm = rows2(dqm)
    dkvm = jnp.concatenate([dkm, dvm], axis=-1).reshape(b_dim * n_mem, 2 * MEM_WIDTH).astype(BF16)
    grads["w_q_mem"] = _shard_major(_mm(hn, dqm, name="mm_q_mem_dw", ta=True), 0)
    dhn = _mm(dqm, w_q, name="mm_q_mem_dx", tb=True)
    grads["w_kv_mem"] = _shard_major(_mm(memn, dkvm, name="mm_kv_mem_dw", ta=True), 0)
    dmemn = _mm(dkvm, w_kv, name="mm_kv_mem_dx", tb=True)
    _, dg_mem_kv = _rms_bwd(memf, g_mem_kv, dmemn, None, name="rms_mem_kv_bwd")
    dh1, dg_mem_q = _rms_bwd(h1, g_mem_q, dhn, dh2, name="rms_mem_q_bwd")

    dmix = _mm(dh1, w_out, name="mm_out_dx", tb=True)
    grads["w_out"] = _shard_major(_mm(mixed, dh1, name="mm_out_dw", ta=True), 0)
    dua, dub, dgates = _gate_bwd(proj, ua, ub, dmix)
    doa = _mm_cs_dx(dua, wf["w_up_a"], name="mm_up_a_dx")
    grads["w_up_a"] = _mm_cs_dw(oa, dua, name="mm_up_a_dw")
    dob = _mm_cs_dx(dub, wf["w_up_b"], name="mm_up_b_dx", out_dtype=BF16)
    grads["w_up_b"] = _mm_cs_dw(ob, dub, name="mm_up_b_dw")

    dq_b, dk_b, dv_b = _sb_bwd(proj3, rows3(dob))
    dq_a, dk_a, dv_a = _dil_bwd(proj3, cs3, sn3, sel_a, sel_b, rows3(doa), oa32, lse_a)
    dproj = jnp.concatenate([rows2(t) for t in (dq_a, dk_a, dv_a, dq_b, dk_b, dv_b)] + [dgates], axis=1)
    grads["w_in"] = _mm_cs_dw(n1, dproj, name="mm_in_dw")
    dn1 = _mm_cs_dx(dproj, wf["w_in"], name="mm_in_dx")
    dx, dg_mix = _rms_bwd(xb, g_mix, dn1, dh1, name="rms_mix_bwd")
    grad_x = dx.reshape(b_dim, s_dim, d)
    return loss_row, grad_x, grads, (dg_mix, dg_mem_q, dg_mem_kv, dg_ffn, dg_final)


def _reduce_and_update(given, shards, loss_row, grad_x, grads, gain_grads):
    d = D_MODEL
    dg_mix, dg_mem_q, dg_mem_kv, dg_ffn, dg_final = gain_grads
    c_idx = lax.axis_index("c").astype(jnp.int32).reshape(1)
    s_idx = (2 * lax.axis_index("x") + lax.axis_index("y")).astype(jnp.int32).reshape(1)
    glist = [grads[n] for n in NAMES]
    theirs = _pair_exchange(glist)
    pairs = [_pair_sum(g, t, c_idx, name="pair_sum_" + n) for n, g, t in zip(NAMES, glist, theirs)]
    recv = _chip_exchange(pairs)
    mine = [_chip_sum(p, r, s_idx, name="chip_sum_" + n) for n, p, r in zip(NAMES, pairs, recv)]
    small = jnp.concatenate([dg_mix, dg_mem_q, dg_mem_kv, dg_ffn, dg_final,
                             jnp.pad(loss_row, ((0, 0), (0, FLAT_COLS - LANES))), jnp.zeros((2, FLAT_COLS), F32)], axis=0)
    small_all = _gather_small(small)
    others = _swap_halves(mine)
    small_sum = _sum8(small_all)
    loss = small_sum[5, 0]

    out_g, out_d, out_m, out_v = {}, {}, {}, {}
    for n, mine_n, other_n in zip(NAMES, mine, others):
        g2, dl, nm, nv = _adamw_halves(shards[n], mine_n, other_n, given["m_" + n][0], given["v_" + n][0], c_idx,
                                       name="adamw_" + n)
        out_g[n], out_d[n], out_m[n], out_v[n] = g2[None], dl[None], nm[None], nv[None]
    gain_w = jnp.concatenate([given[n].reshape(1, d) for n in GAINS], axis=0)
    gain_m = jnp.concatenate([given["m_" + n].reshape(1, d) for n in GAINS], axis=0)
    gain_v = jnp.concatenate([given["v_" + n].reshape(1, d) for n in GAINS], axis=0)
    gain_g = small_sum[:len(GAINS)]
    gd, gm, gv = _adamw(gain_w, gain_g, gain_m, gain_v, name="adamw_gains")
    for i, n in enumerate(GAINS):
        shape = given[n].shape
        out_g[n], out_d[n] = gain_g[i].reshape(shape), gd[i].reshape(shape)
        out_m[n], out_v[n] = gm[i].reshape(shape), gv[i].reshape(shape)

    order = ["g_mix", "w_in", "w_up_a", "w_up_b", "w_out", "g_mem_q", "g_mem_kv", "w_q_mem", "w_kv_mem", "w_o_mem",
             "g_ffn", "w_ffn_gate", "w_ffn_up", "w_ffn_down", "g_final"]
    return (loss, grad_x, *[out_g[n] for n in order], *[out_d[n] for n in order],
            *[out_m[n] for n in order], *[out_v[n] for n in order])
```

```python
import jax
import jax.numpy as jnp
from jax import lax
from jax.experimental import pallas as pl
from jax.experimental.pallas import tpu as pltpu

F32 = jnp.float32
BF16 = jnp.bfloat16
MESH = pl.DeviceIdType.MESH

D_MODEL = 1024
HEAD_DIM = 64
N_HEADS = 8
ATT_WIDTH = N_HEADS * HEAD_DIM
DIL_PATTERNS = ((128, 1), (512, 4), (2048, 16))
BLOCK = 128
SB_ROWS = 512
ROPE_THETA = 500000.0
ROPE_DIM = HEAD_DIM // 4
N_HEADS_MEM = 4
MEM_HEAD_DIM = 128
MEM_WIDTH = N_HEADS_MEM * MEM_HEAD_DIM
D_FF = 2816
IN_COLS = 6 * ATT_WIDTH + 2 * D_MODEL
RMS_EPS = 1e-6
ADAM_LR = 0.001
ADAM_B1 = 0.9
ADAM_B2 = 0.999
ADAM_EPS = 1e-08
ADAM_WD = 0.01
ADAM_STEP = 10

N_CHIPS = 4
LANES = 128
FLAT_COLS = 1024
VMEM_LIMIT = 56 * 1024 * 1024

PAIRS = ATT_WIDTH // LANES
COL_QA, COL_KA, COL_VA, COL_QB, COL_KB, COL_VB = (i * PAIRS for i in range(6))

MM_CAP = 1408
NN = (((1,), (0,)), ((), ()))
NT = (((1,), (1,)), ((), ()))
TN = (((0,), (0,)), ((), ()))


def _tile(dim, cap, unit=LANES):
    if dim <= cap:
        return dim
    best = None
    for t in range(unit, cap + 1, unit):
        if dim % t == 0:
            best = t
    assert best is not None, (dim, cap)
    return best


def _params(sem):
    return pltpu.CompilerParams(dimension_semantics=sem, vmem_limit_bytes=VMEM_LIMIT)


def _mm(a, b, *, name, ta=False, tb=False, add=None, out_dtype=F32,
        tm_cap=MM_CAP, tn_cap=MM_CAP, tk_cap=MM_CAP):
    if ta:
        k_dim, m_dim = a.shape
    else:
        m_dim, k_dim = a.shape
    if tb:
        n_dim, kb = b.shape
    else:
        kb, n_dim = b.shape
    assert kb == k_dim, (a.shape, b.shape, ta, tb)
    tm, tn, tk = _tile(m_dim, tm_cap), _tile(n_dim, tn_cap), _tile(k_dim, tk_cap)
    nk = k_dim // tk
    dims = (((0 if ta else 1,), (1 if tb else 0,)), ((), ()))
    has_add = add is not None

    def body(*refs):
        if has_add:
            a_ref, b_ref, add_ref, o_ref = refs[:4]
        else:
            a_ref, b_ref, o_ref = refs[:3]
        part = lax.dot_general(a_ref[...].astype(BF16), b_ref[...].astype(BF16), dims, preferred_element_type=F32)

        def finish(r):
            if has_add:
                r = add_ref[...] + r
            o_ref[...] = r.astype(out_dtype)

        if nk == 1:
            finish(part)
            return
        acc_ref = refs[-1]
        k = pl.program_id(2)

        @pl.when(k == 0)
        def _():
            acc_ref[...] = part

        @pl.when(k > 0)
        def _():
            acc_ref[...] += part

        @pl.when(k == nk - 1)
        def _():
            finish(acc_ref[...])

    a_spec = pl.BlockSpec((tk, tm), lambda i, j, k: (k, i)) if ta else pl.BlockSpec((tm, tk), lambda i, j, k: (i, k))
    b_spec = pl.BlockSpec((tn, tk), lambda i, j, k: (j, k)) if tb else pl.BlockSpec((tk, tn), lambda i, j, k: (k, j))
    o_spec = pl.BlockSpec((tm, tn), lambda i, j, k: (i, j))
    in_specs = [a_spec, b_spec] + ([o_spec] if has_add else [])
    args = (a, b) + ((add,) if has_add else ())
    return pl.pallas_call(
        body, name=name, grid=(m_dim // tm, n_dim // tn, nk),
        in_specs=in_specs, out_specs=o_spec,
        out_shape=jax.ShapeDtypeStruct((m_dim, n_dim), out_dtype),
        scratch_shapes=[pltpu.VMEM((tm, tn), F32)] if nk > 1 else [],
        compiler_params=_params(("parallel", "parallel", "arbitrary")),
    )(*args)


def _mm_core(name, a, b, a_spec, b_spec, o_spec, out_shape, grid, dims, *, add=None, out_dtype=F32):
    nk = grid[2]
    has_add = add is not None
    acc_shape = tuple(d for d in o_spec.block_shape if d is not None)

    def body(*refs):
        if has_add:
            a_ref, b_ref, add_ref, o_ref = refs[:4]
        else:
            a_ref, b_ref, o_ref = refs[:3]
        part = lax.dot_general(a_ref[...].astype(BF16), b_ref[...].astype(BF16), dims, preferred_element_type=F32)

        def finish(r):
            if has_add:
                r = add_ref[...] + r
            o_ref[...] = r.astype(out_dtype)

        if nk == 1:
            finish(part)
            return
        acc_ref = refs[-1]
        k = pl.program_id(2)

        @pl.when(k == 0)
        def _():
            acc_ref[...] = part

        @pl.when(k > 0)
        def _():
            acc_ref[...] += part

        @pl.when(k == nk - 1)
        def _():
            finish(acc_ref[...])

    in_specs = [a_spec, b_spec] + ([o_spec] if has_add else [])
    args = (a, b) + ((add,) if has_add else ())
    return pl.pallas_call(
        body, name=name, grid=grid, in_specs=in_specs, out_specs=o_spec,
        out_shape=jax.ShapeDtypeStruct(out_shape, out_dtype),
        scratch_shapes=[pltpu.VMEM(acc_shape, F32)] if nk > 1 else [],
        compiler_params=_params(("parallel", "parallel", "arbitrary")),
    )(*args)


def _mm_cs(a, w3, *, name, add=None, out_dtype=F32):
    m_dim, k_dim = a.shape
    _, _, n4 = w3.shape
    tm, tn, tk = _tile(m_dim, MM_CAP), _tile(n4, MM_CAP), _tile(k_dim, MM_CAP)
    npb = n4 // tn
    return _mm_core(name, a, w3,
                    pl.BlockSpec((tm, tk), lambda i, j, k: (i, k)),
                    pl.BlockSpec((None, tk, tn), lambda i, j, k: (j // npb, k, j % npb)),
                    pl.BlockSpec((tm, tn), lambda i, j, k: (i, j)),
                    (m_dim, N_CHIPS * n4), (m_dim // tm, N_CHIPS * npb, k_dim // tk), NN, add=add, out_dtype=out_dtype)


def _mm_cs_dx(dy, w3, *, name, out_dtype=F32):
    m_dim, _ = dy.shape
    _, k_dim, n4 = w3.shape
    tm, tkw, tn = _tile(m_dim, MM_CAP), _tile(k_dim, MM_CAP), _tile(n4, MM_CAP)
    npb = n4 // tn
    return _mm_core(name, dy, w3,
                    pl.BlockSpec((tm, tn), lambda i, j, k: (i, k)),
                    pl.BlockSpec((None, tkw, tn), lambda i, j, k: (k // npb, j, k % npb)),
                    pl.BlockSpec((tm, tkw), lambda i, j, k: (i, j)),
                    (m_dim, k_dim), (m_dim // tm, k_dim // tkw, N_CHIPS * npb), NT, out_dtype=out_dtype)


def _mm_cs_dw(a, dy, *, name):
    m_dim, k_dim = a.shape
    n4 = dy.shape[1] // N_CHIPS
    tmk, tn, tk = _tile(k_dim, MM_CAP), _tile(n4, MM_CAP), _tile(m_dim, MM_CAP)
    npb = n4 // tn
    return _mm_core(name, a, dy,
                    pl.BlockSpec((tk, tmk), lambda i, j, k: (k, i)),
                    pl.BlockSpec((tk, tn), lambda i, j, k: (k, j)),
                    pl.BlockSpec((None, tmk, tn), lambda i, j, k: (j // npb, i, j % npb)),
                    (N_CHIPS, k_dim, n4), (k_dim // tmk, N_CHIPS * npb, m_dim // tk), TN)


def _mm_ffn_up(n, w3, *, name):
    t_dim, d = n.shape
    _, _, f4 = w3.shape
    tm = _tile(t_dim, MM_CAP)
    return _mm_core(name, n, w3,
                    pl.BlockSpec((tm, d), lambda i, j, k: (i, 0)),
                    pl.BlockSpec((None, d, f4), lambda i, j, k: (j, 0, 0)),
                    pl.BlockSpec((None, tm, f4), lambda i, j, k: (j, i, 0)),
                    (N_CHIPS, t_dim, f4), (t_dim // tm, N_CHIPS, 1), NN)


def _mm_ffn_up_dw(n, d3, *, name):
    t_dim, d = n.shape
    _, _, f4 = d3.shape
    tk = _tile(t_dim, MM_CAP)
    return _mm_core(name, n, d3,
                    pl.BlockSpec((tk, d), lambda i, j, k: (k, 0)),
                    pl.BlockSpec((None, tk, f4), lambda i, j, k: (j, k, 0)),
                    pl.BlockSpec((None, d, f4), lambda i, j, k: (j, 0, 0)),
                    (N_CHIPS, d, f4), (1, N_CHIPS, t_dim // tk), TN)


def _mm_ffn_up_dx(d3, w3, *, name, add=None):
    _, t_dim, f4 = d3.shape
    _, d, _ = w3.shape
    tm = _tile(t_dim, MM_CAP)
    return _mm_core(name, d3, w3,
                    pl.BlockSpec((None, tm, f4), lambda i, j, k: (k, i, 0)),
                    pl.BlockSpec((None, d, f4), lambda i, j, k: (k, 0, 0)),
                    pl.BlockSpec((tm, d), lambda i, j, k: (i, 0)),
                    (t_dim, d), (t_dim // tm, 1, N_CHIPS), NT, add=add)


def _mm_ffn_down(act3, wd3, *, name, add):
    _, t_dim, f4 = act3.shape
    _, _, d = wd3.shape
    tm = _tile(t_dim, MM_CAP)
    return _mm_core(name, act3, wd3,
                    pl.BlockSpec((None, tm, f4), lambda i, j, k: (k, i, 0)),
                    pl.BlockSpec((None, f4, d), lambda i, j, k: (k, 0, 0)),
                    pl.BlockSpec((tm, d), lambda i, j, k: (i, 0)),
                    (t_dim, d), (t_dim // tm, 1, N_CHIPS), NN, add=add)


def _mm_ffn_down_dx(dh, wd3, *, name):
    t_dim, d = dh.shape
    _, f4, _ = wd3.shape
    tm = _tile(t_dim, MM_CAP)
    return _mm_core(name, dh, wd3,
                    pl.BlockSpec((tm, d), lambda i, j, k: (i, 0)),
                    pl.BlockSpec((None, f4, d), lambda i, j, k: (j, 0, 0)),
                    pl.BlockSpec((None, tm, f4), lambda i, j, k: (j, i, 0)),
                    (N_CHIPS, t_dim, f4), (t_dim // tm, N_CHIPS, 1), NT)


def _mm_ffn_down_dw(act3, dh, *, name):
    _, t_dim, f4 = act3.shape
    d = dh.shape[1]
    tk = _tile(t_dim, MM_CAP)
    return _mm_core(name, act3, dh,
                    pl.BlockSpec((None, tk, f4), lambda i, j, k: (i, k, 0)),
                    pl.BlockSpec((tk, d), lambda i, j, k: (k, 0)),
                    pl.BlockSpec((None, f4, d), lambda i, j, k: (i, 0, 0)),
                    (N_CHIPS, f4, d), (N_CHIPS, 1, t_dim // tk), TN)


def _rms_fwd(x, g, *, name, tt=512):
    t_dim, d = x.shape
    tt = _tile(t_dim, tt, 8)

    def body(x_ref, g_ref, o_ref):
        xv = x_ref[...]
        r = lax.rsqrt(jnp.mean(xv * xv, axis=-1, keepdims=True) + RMS_EPS)
        o_ref[...] = ((xv * r) * g_ref[...]).astype(o_ref.dtype)

    return pl.pallas_call(
        body, name=name, grid=(t_dim // tt,),
        in_specs=[pl.BlockSpec((tt, d), lambda i: (i, 0)), pl.BlockSpec((1, d), lambda i: (0, 0))],
        out_specs=pl.BlockSpec((tt, d), lambda i: (i, 0)),
        out_shape=jax.ShapeDtypeStruct((t_dim, d), BF16),
        compiler_params=_params(("parallel",)),
    )(x, g)


def _rms_bwd(x, g, dy, add, *, name, tt=512):
    t_dim, d = x.shape
    tt = _tile(t_dim, tt, 8)
    has_add = add is not None

    def body(*refs):
        if has_add:
            x_ref, g_ref, dy_ref, add_ref, dx_ref, dg_ref = refs
        else:
            x_ref, g_ref, dy_ref, dx_ref, dg_ref = refs
        xv = x_ref[...]
        dyv = dy_ref[...].astype(F32)
        r = lax.rsqrt(jnp.mean(xv * xv, axis=-1, keepdims=True) + RMS_EPS)
        xh = xv * r
        u = dyv * g_ref[...]
        dx = r * (u - xh * jnp.mean(u * xh, axis=-1, keepdims=True))
        if has_add:
            dx = add_ref[...] + dx
        dx_ref[...] = dx

        @pl.when(pl.program_id(0) == 0)
        def _():
            dg_ref[...] = jnp.zeros_like(dg_ref)

        dg_ref[...] += jnp.sum(dyv * xh, axis=0, keepdims=True)

    row = pl.BlockSpec((tt, d), lambda i: (i, 0))
    vec = pl.BlockSpec((1, d), lambda i: (0, 0))
    in_specs = [row, vec, row] + ([row] if has_add else [])
    args = (x, g, dy) + ((add,) if has_add else ())
    return pl.pallas_call(
        body, name=name, grid=(t_dim // tt,),
        in_specs=in_specs, out_specs=[row, vec],
        out_shape=[jax.ShapeDtypeStruct((t_dim, d), F32), jax.ShapeDtypeStruct((1, d), F32)],
        compiler_params=_params(("arbitrary",)),
    )(*args)


def _final(h, g, target, *, tt=512):
    t_dim, d = h.shape
    n_steps = t_dim // tt

    def body(h_ref, g_ref, t_ref, loss_ref, dh_ref, dg_ref, sq_ref):
        i = pl.program_id(0)
        xv = h_ref[...]
        gv = g_ref[...]
        r = lax.rsqrt(jnp.mean(xv * xv, axis=-1, keepdims=True) + RMS_EPS)
        xh = xv * r
        err = xh * gv - t_ref[...]
        dyv = err * (1.0 / d)
        u = dyv * gv
        dh_ref[...] = r * (u - xh * jnp.mean(u * xh, axis=-1, keepdims=True))

        @pl.when(i == 0)
        def _():
            dg_ref[...] = jnp.zeros_like(dg_ref)
            sq_ref[...] = jnp.zeros_like(sq_ref)

        dg_ref[...] += jnp.sum(dyv * xh, axis=0, keepdims=True)
        sq_ref[...] += jnp.sum(err * err, axis=0, keepdims=True)

        @pl.when(i == n_steps - 1)
        def _():
            total = jnp.sum(sq_ref[...], axis=-1, keepdims=True) * (0.5 / d)
            loss_ref[...] = jnp.broadcast_to(total, loss_ref.shape)

    row = pl.BlockSpec((tt, d), lambda i: (i, 0))
    vec = pl.BlockSpec((1, d), lambda i: (0, 0))
    return pl.pallas_call(
        body, name="final_loss", grid=(n_steps,),
        in_specs=[row, vec, row],
        out_specs=[pl.BlockSpec((1, LANES), lambda i: (0, 0)), row, vec],
        out_shape=[jax.ShapeDtypeStruct((1, LANES), F32), jax.ShapeDtypeStruct((t_dim, d), F32),
                   jax.ShapeDtypeStruct((1, d), F32)],
        scratch_shapes=[pltpu.VMEM((1, d), F32)],
        compiler_params=_params(("arbitrary",)),
    )(h, g, target)


def _rope_table(pos, inv_lane, sel_a, sel_b, *, tt=512):
    t_dim = pos.shape[0]

    def body(p_ref, f_ref, a_ref, b_ref, c_ref, s_ref):
        ang = p_ref[...] * f_ref[...]
        on = (a_ref[...] + b_ref[...]) > 0.0
        c_ref[...] = jnp.where(on, jnp.cos(ang), 1.0)
        s_ref[...] = jnp.where(on, jnp.sin(ang), 0.0)

    vec = pl.BlockSpec((1, LANES), lambda i: (0, 0))
    row = pl.BlockSpec((tt, LANES), lambda i: (i, 0))
    shp = jax.ShapeDtypeStruct((t_dim, LANES), F32)
    return pl.pallas_call(
        body, name="rope_table", grid=(t_dim // tt,),
        in_specs=[pl.BlockSpec((tt, 1), lambda i: (i, 0)), vec, vec, vec],
        out_specs=[row, row], out_shape=[shp, shp],
        compiler_params=_params(("parallel",)),
    )(pos, inv_lane, sel_a, sel_b)


def _rotate(xv, cs, sn, sa, sb):
    half = ROPE_DIM // 2
    up = pltpu.roll(xv, LANES - half, 1)
    dn = pltpu.roll(xv, half, 1)
    return xv * cs + (dn * sb - up * sa) * sn


def _head_masks():
    h1 = lax.broadcasted_iota(jnp.int32, (1, LANES), 1) < HEAD_DIM
    return h1, jnp.logical_not(h1)


def _split_heads(xv, h1, h2):
    return jnp.where(h1, xv, 0.0).astype(BF16), jnp.where(h2, xv, 0.0).astype(BF16)


def _tri_masks():
    r = lax.broadcasted_iota(jnp.int32, (BLOCK, BLOCK), 0)
    c = lax.broadcasted_iota(jnp.int32, (BLOCK, BLOCK), 1)
    return c <= r, r <= c


def _stream_rows(start, dil):
    if dil == 1:
        return pl.ds(pl.multiple_of(start, BLOCK), BLOCK)
    return pl.ds(start, BLOCK, stride=dil)


def _dil_specs(b_dim, s_dim):
    def col(c0):
        return pl.BlockSpec((None, s_dim, LANES),lambda b, h: (b, 0, c0 + h))
    tab = pl.BlockSpec((None, s_dim, LANES),lambda b, h: (b, 0, 0))
    vec = pl.BlockSpec((1, LANES), lambda b, h: (0, 0))
    return col, tab, vec


def _dil_fwd(proj3, cs3, sn3, sel_a, sel_b, shards=()):
    b_dim, s_dim, _ = proj3.shape
    scale = HEAD_DIM ** -0.5
    n_pat = len(DIL_PATTERNS)
    n_w = len(shards)
    n_steps = b_dim * PAIRS

    def body(*refs):
        q_ref, k_ref, v_ref, cs_ref, sn_ref, sa_ref, sb_ref = refs[:7]
        o16_ref, o32_ref, l_ref = refs[7 + n_w:10 + n_w]
        qr, kr = refs[10 + 2 * n_w:12 + 2 * n_w]
        per_pattern = refs[12 + 2 * n_w:12 + 2 * n_w + 2 * n_pat]
        og, lg = per_pattern[:n_pat], per_pattern[n_pat:]
        step = pl.program_id(0) * PAIRS + pl.program_id(1)
        if n_w:
            g_start, g_forward, g_finish = _gather_steps(refs[7:7 + n_w], refs[10 + n_w:10 + 2 * n_w], *refs[-2:])
            pl.when(step == 0)(g_start)
        h1, h2 = _head_masks()
        cur_ok, prev_ok = _tri_masks()
        sa, sb = sa_ref[...], sb_ref[...]

        def prep(j, _):
            rows = pl.ds(pl.multiple_of(j * BLOCK, BLOCK), BLOCK)
            cs, sn = cs_ref[rows, :], sn_ref[rows, :]
            qr[rows, :] = _rotate(q_ref[rows, :], cs, sn, sa, sb) * scale
            kr[rows, :] = _rotate(k_ref[rows, :], cs, sn, sa, sb)
            return 0

        lax.fori_loop(0, s_dim // BLOCK, prep, 0)

        for g, (_, dil) in enumerate(DIL_PATTERNS):
            nb = s_dim // dil // BLOCK

            def one(idx, _, g=g, dil=dil, nb=nb):
                r = idx // nb
                n = idx % nb
                rows = _stream_rows(r + dil * BLOCK * n, dil)
                prow = _stream_rows(r + dil * BLOCK * jnp.maximum(n - 1, 0), dil)
                q1, q2 = _split_heads(qr[rows, :], h1, h2)
                kc = kr[rows, :].astype(BF16)
                vc1, vc2 = _split_heads(v_ref[rows, :], h1, h2)
                if nb > 1:
                    kp = kr[prow, :].astype(BF16)
                    vp1, vp2 = _split_heads(v_ref[prow, :], h1, h2)
                    p_ok = jnp.logical_and(prev_ok, n > 0)

                def head(qh, vch, vph):
                    sc = jnp.where(cur_ok, lax.dot_general(qh, kc, NT, preferred_element_type=F32), -jnp.inf)
                    m = jnp.max(sc, axis=-1, keepdims=True)
                    if nb > 1:
                        sp = jnp.where(p_ok, lax.dot_general(qh, kp, NT, preferred_element_type=F32), -jnp.inf)
                        m = jnp.maximum(m, jnp.max(sp, axis=-1, keepdims=True))
                    pc = jnp.exp(sc - m)
                    den = jnp.sum(pc, axis=-1, keepdims=True)
                    acc = jnp.dot(pc.astype(BF16), vch, preferred_element_type=F32)
                    if nb > 1:
                        pp = jnp.exp(sp - m)
                        den = den + jnp.sum(pp, axis=-1, keepdims=True)
                        acc = acc + jnp.dot(pp.astype(BF16), vph, preferred_element_type=F32)
                    return acc / den, m + jnp.log(den)

                o1, l1 = head(q1, vc1, vp1 if nb > 1 else None)
                o2, l2 = head(q2, vc2, vp2 if nb > 1 else None)
                og[g][rows, :] = o1 + o2
                lg[g][rows, :] = jnp.where(h1, l1, l2)
                return 0

            lax.fori_loop(0, dil * nb, one, 0, unroll=4)

        def comb(j, _):
            rows = pl.ds(pl.multiple_of(j * BLOCK, BLOCK), BLOCK)
            ls = [lg[g][rows, :] for g in range(n_pat)]
            m = jnp.maximum(jnp.maximum(ls[0], ls[1]), ls[2])
            es = [jnp.exp(l - m) for l in ls]
            den = es[0] + es[1] + es[2]
            o = (es[0] * og[0][rows, :] + es[1] * og[1][rows, :] + es[2] * og[2][rows, :]) / den
            o16_ref[rows, :] = o.astype(BF16)
            o32_ref[rows, :] = o
            l_ref[rows, :] = m + jnp.log(den)
            return 0

        lax.fori_loop(0, s_dim // BLOCK, comb, 0)
        if n_w:
            pl.when(step == n_steps - 2)(g_forward)
            pl.when(step == n_steps - 1)(g_finish)

    col, tab, vec = _dil_specs(b_dim, s_dim)
    out = pl.BlockSpec((None, s_dim, LANES),lambda b, h: (b, 0, h))
    shp = (b_dim, s_dim, ATT_WIDTH)
    res = pl.pallas_call(
        body, name="dil_fwd", grid=(b_dim, PAIRS),
        in_specs=[col(COL_QA), col(COL_KA), col(COL_VA), tab, tab, vec, vec] + [ANY] * n_w,
        out_specs=[out, out, out] + [ANY] * n_w,
        out_shape=[jax.ShapeDtypeStruct(shp, BF16), jax.ShapeDtypeStruct(shp, F32), jax.ShapeDtypeStruct(shp, F32)]
        + _gathered_shapes(shards),
        scratch_shapes=[pltpu.VMEM((s_dim, LANES), F32)] * (2 + 2 * n_pat) + (_gather_sems(n_w) if n_w else []),
        compiler_params=_params(("arbitrary", "arbitrary")),
    )(proj3, proj3, proj3, cs3, sn3, sel_a, sel_b, *shards)
    return res[:3], res[3:]


def _dil_bwd(proj3, cs3, sn3, sel_a, sel_b, do3, o3, lse3, pairs=()):
    b_dim, s_dim, _ = proj3.shape
    scale = HEAD_DIM ** -0.5
    n_w = len(pairs)

    def body(*refs):
        q_ref, k_ref, v_ref, cs_ref, sn_ref, sa_ref, sb_ref, do_ref, o_ref, l_ref = refs[:10]
        dq_ref, dk_ref, dv_ref = refs[10 + n_w:13 + n_w]
        qr, kr, dqa, dka, dva = refs[13 + 2 * n_w:18 + 2 * n_w]
        step = pl.program_id(0) * PAIRS + pl.program_id(1)
        if n_w:
            x_start, x_finish = _chip_steps(refs[10:10 + n_w], refs[13 + n_w:13 + 2 * n_w], *refs[-2:])
            pl.when(step == 0)(x_start)
        h1, h2 = _head_masks()
        cur_ok, prev_ok = _tri_masks()
        sa, sb = sa_ref[...], sb_ref[...]

        def prep(j, _):
            rows = pl.ds(pl.multiple_of(j * BLOCK, BLOCK), BLOCK)
            cs, sn = cs_ref[rows, :], sn_ref[rows, :]
            qr[rows, :] = _rotate(q_ref[rows, :], cs, sn, sa, sb) * scale
            kr[rows, :] = _rotate(k_ref[rows, :], cs, sn, sa, sb)
            zero = jnp.zeros((BLOCK, LANES), F32)
            dqa[rows, :] = zero
            dka[rows, :] = zero
            dva[rows, :] = zero
            return 0

        lax.fori_loop(0, s_dim // BLOCK, prep, 0)

        for _, dil in DIL_PATTERNS:
            nb = s_dim // dil // BLOCK

            def one(idx, _, dil=dil, nb=nb):
                r = idx // nb
                n = idx % nb
                rows = _stream_rows(r + dil * BLOCK * n, dil)
                prow = _stream_rows(r + dil * BLOCK * jnp.maximum(n - 1, 0), dil)
                q1, q2 = _split_heads(qr[rows, :], h1, h2)
                dof = do_ref[rows, :]
                do1, do2 = _split_heads(dof, h1, h2)
                prod = dof * o_ref[rows, :]
                delta1 = jnp.sum(jnp.where(h1, prod, 0.0), axis=-1, keepdims=True)
                delta2 = jnp.sum(jnp.where(h2, prod, 0.0), axis=-1, keepdims=True)
                lt = l_ref[rows, :]
                lse1 = jnp.max(jnp.where(h1, lt, -jnp.inf), axis=-1, keepdims=True)
                lse2 = jnp.max(jnp.where(h2, lt, -jnp.inf), axis=-1, keepdims=True)

                def side(krows, ok):
                    kf = kr[krows, :]
                    k16 = kf.astype(BF16)
                    k1, k2 = _split_heads(kf, h1, h2)
                    v16 = v_ref[krows, :].astype(BF16)

                    def head(qh, doh, lse, delta):
                        sc = lax.dot_general(qh, k16, NT, preferred_element_type=F32)
                        p = jnp.where(ok, jnp.exp(sc - lse), 0.0)
                        dp = lax.dot_general(doh, v16, NT, preferred_element_type=F32)
                        return p.astype(BF16), (p * (dp - delta)).astype(BF16)

                    p1, ds1 = head(q1, do1, lse1, delta1)
                    p2, ds2 = head(q2, do2, lse2, delta2)
                    dva[krows, :] += (lax.dot_general(p1, do1, TN, preferred_element_type=F32)
                                      + lax.dot_general(p2, do2, TN, preferred_element_type=F32))
                    dka[krows, :] += (lax.dot_general(ds1, q1, TN, preferred_element_type=F32)
                                      + lax.dot_general(ds2, q2, TN, preferred_element_type=F32))
                    return (jnp.dot(ds1, k1, preferred_element_type=F32)
                            + jnp.dot(ds2, k2, preferred_element_type=F32))

                dq = side(rows, cur_ok)
                if nb > 1:
                    dq = dq + side(prow, jnp.logical_and(prev_ok, n > 0))
                dqa[rows, :] += dq * scale
                return 0

            lax.fori_loop(0, dil * nb, one, 0, unroll=4)

        def finish(j, _):
            rows = pl.ds(pl.multiple_of(j * BLOCK, BLOCK), BLOCK)
            cs, sn = cs_ref[rows, :], -sn_ref[rows, :]
            dq_ref[rows, :] = _rotate(dqa[rows, :], cs, sn, sa, sb).astype(BF16)
            dk_ref[rows, :] = _rotate(dka[rows, :], cs, sn, sa, sb).astype(BF16)
            dv_ref[rows, :] = dva[rows, :].astype(BF16)
            return 0

        lax.fori_loop(0, s_dim // BLOCK, finish, 0)
        if n_w:
            pl.when(step == b_dim * PAIRS - 1)(x_finish)

    col, tab, vec = _dil_specs(b_dim, s_dim)
    out = pl.BlockSpec((None, s_dim, LANES),lambda b, h: (b, 0, h))
    shp = jax.ShapeDtypeStruct((b_dim, s_dim, ATT_WIDTH), BF16)
    acc = pltpu.VMEM((s_dim, LANES), F32)
    res = pl.pallas_call(
        body, name="dil_bwd", grid=(b_dim, PAIRS),
        in_specs=[col(COL_QA), col(COL_KA), col(COL_VA), tab, tab, vec, vec, out, out, out] + [ANY] * n_w,
        out_specs=[out, out, out] + [ANY] * n_w, out_shape=[shp, shp, shp] + _chip_shapes(pairs),
        scratch_shapes=[acc, acc, acc, acc, acc] + (_exchange_sems(3 * n_w) if n_w else []),
        compiler_params=_params(("arbitrary", "arbitrary")),
    )(proj3, proj3, proj3, cs3, sn3, sel_a, sel_b, do3, o3, lse3, *pairs)
    return res[:3], res[3:]


def _split_dot(x, tri):
    hi = x.astype(BF16)
    lo = (x - hi.astype(F32)).astype(BF16)
    return jnp.dot(hi, tri, preferred_element_type=F32) + jnp.dot(lo, tri, preferred_element_type=F32)


def _log_sigmoid(z):
    return jnp.minimum(z, 0.0) - jnp.log(1.0 + jnp.exp(-jnp.abs(z)))


def _sb_scores(qh, k16, valid):
    z = lax.dot_general(qh, k16, NT, preferred_element_type=F32)
    ls = _log_sigmoid(z)
    return ls, jnp.where(valid, ls - z, 0.0)


def _sb_consts():
    r = lax.broadcasted_iota(jnp.int32, (BLOCK, BLOCK), 0)
    c = lax.broadcasted_iota(jnp.int32, (BLOCK, BLOCK), 1)
    after = (r > c).astype(BF16)
    before = (r < c).astype(BF16)
    qrow = lax.broadcasted_iota(jnp.int32, (SB_ROWS, BLOCK), 0)
    kcol = lax.broadcasted_iota(jnp.int32, (SB_ROWS, BLOCK), 1)
    return after, before, qrow, kcol


def _sb_fwd(proj3):
    b_dim, s_dim, _ = proj3.shape
    scale = HEAD_DIM ** -0.5
    per = SB_ROWS // BLOCK

    def body(q_ref, k_ref, v_ref, o_ref):
        h1, h2 = _head_masks()
        after, _, qrow, kcol = _sb_consts()

        def qloop(qi, _):
            rows = pl.ds(pl.multiple_of(qi * SB_ROWS, SB_ROWS), SB_ROWS)
            q1, q2 = _split_heads(q_ref[rows, :] * scale, h1, h2)
            qpos = qi * SB_ROWS + qrow
            nkb = (qi + 1) * per

            def kloop(i, carry):
                acc, run1, run2 = carry
                kb = nkb - 1 - i
                krows = pl.ds(pl.multiple_of(kb * BLOCK, BLOCK), BLOCK)
                k16 = k_ref[krows, :].astype(BF16)
                v1, v2 = _split_heads(v_ref[krows, :], h1, h2)
                valid = (kb * BLOCK + kcol) < qpos

                def head(qh, vh, run):
                    ls, l1m = _sb_scores(qh, k16, valid)
                    a = jnp.where(valid, jnp.exp(ls + _split_dot(l1m, after) + run), 0.0)
                    return (jnp.dot(a.astype(BF16), vh, preferred_element_type=F32),
                            run + jnp.sum(l1m, axis=-1, keepdims=True))

                o1, run1 = head(q1, v1, run1)
                o2, run2 = head(q2, v2, run2)
                return acc + o1 + o2, run1, run2

            zcol = jnp.zeros((SB_ROWS, 1), F32)
            acc, _, _ = lax.fori_loop(0, nkb // 2, lambda i, c: kloop(2 * i + 1, kloop(2 * i, c)),
                                      (jnp.zeros((SB_ROWS, LANES), F32), zcol, zcol))
            o_ref[rows, :] = acc.astype(BF16)
            return 0

        lax.fori_loop(0, s_dim // SB_ROWS, qloop, 0)

    def col(c0):
        return pl.BlockSpec((None, s_dim, LANES),lambda b, h: (b, 0, c0 + h))

    return pl.pallas_call(
        body, name="sb_fwd", grid=(b_dim, PAIRS),
        in_specs=[col(COL_QB), col(COL_KB), col(COL_VB)], out_specs=col(0),
        out_shape=jax.ShapeDtypeStruct((b_dim, s_dim, ATT_WIDTH), BF16),
        compiler_params=_params(("parallel", "parallel")),
    )(proj3, proj3, proj3)


def _sb_bwd(proj3, do3, grads=()):
    b_dim, s_dim, _ = proj3.shape
    scale = HEAD_DIM ** -0.5
    per = SB_ROWS // BLOCK
    nkb_max = s_dim // BLOCK
    n_w = len(grads)

    def body(*refs):
        q_ref, k_ref, v_ref, do_ref = refs[:4]
        dq_ref, dk_ref, dv_ref = refs[4 + n_w:7 + n_w]
        dka, dva, e_ref, sg_ref = refs[7 + 2 * n_w:11 + 2 * n_w]
        step = pl.program_id(0) * PAIRS + pl.program_id(1)
        if n_w:
            x_start, x_finish = _pair_steps(refs[4:4 + n_w], refs[7 + n_w:7 + 2 * n_w], *refs[-2:])
            pl.when(step == 0)(x_start)
        h1, h2 = _head_masks()
        after, before, qrow, kcol = _sb_consts()
        dka[...] = jnp.zeros_like(dka)
        dva[...] = jnp.zeros_like(dva)

        def qloop(qi, _):
            rows = pl.ds(pl.multiple_of(qi * SB_ROWS, SB_ROWS), SB_ROWS)
            q1, q2 = _split_heads(q_ref[rows, :] * scale, h1, h2)
            do1, do2 = _split_heads(do_ref[rows, :].astype(F32), h1, h2)
            qpos = qi * SB_ROWS + qrow
            nkb = (qi + 1) * per

            def pass1(i, carry):
                run1, run2 = carry
                kb = nkb - 1 - i
                krows = pl.ds(pl.multiple_of(kb * BLOCK, BLOCK), BLOCK)
                k16 = k_ref[krows, :].astype(BF16)
                v16 = v_ref[krows, :].astype(BF16)
                valid = (kb * BLOCK + kcol) < qpos

                def head(h, qh, doh, run):
                    ls, l1m = _sb_scores(qh, k16, valid)
                    a = jnp.where(valid, jnp.exp(ls + _split_dot(l1m, after) + run), 0.0)
                    da = lax.dot_general(doh, v16, NT, preferred_element_type=F32)
                    e_ref[h, kb] = a * da
                    sg_ref[h, kb] = jnp.exp(ls)
                    return a.astype(BF16), run + jnp.sum(l1m, axis=-1, keepdims=True)

                a1, run1 = head(0, q1, do1, run1)
                a2, run2 = head(1, q2, do2, run2)
                dva[krows, :] += (lax.dot_general(a1, do1, TN, preferred_element_type=F32)
                                  + lax.dot_general(a2, do2, TN, preferred_element_type=F32))
                return run1, run2

            zcol = jnp.zeros((SB_ROWS, 1), F32)
            lax.fori_loop(0, nkb // 2, lambda i, c: pass1(2 * i + 1, pass1(2 * i, c)), (zcol, zcol))

            def pass2(kb, carry):
                dq, pre1, pre2 = carry
                krows = pl.ds(pl.multiple_of(kb * BLOCK, BLOCK), BLOCK)
                k1, k2 = _split_heads(k_ref[krows, :], h1, h2)
                valid = (kb * BLOCK + kcol) < qpos

                def head(h, pre):
                    ev = e_ref[h, kb]
                    sg = sg_ref[h, kb]
                    prefix = _split_dot(ev, before) + pre
                    dz = jnp.where(valid, ev * (1.0 - sg) - prefix * sg, 0.0).astype(BF16)
                    return dz, pre + jnp.sum(ev, axis=-1, keepdims=True)

                dz1, pre1 = head(0, pre1)
                dz2, pre2 = head(1, pre2)
                dka[krows, :] += (lax.dot_general(dz1, q1, TN, preferred_element_type=F32)
                                  + lax.dot_general(dz2, q2, TN, preferred_element_type=F32))
                dq = dq + jnp.dot(dz1, k1, preferred_element_type=F32) + jnp.dot(dz2, k2, preferred_element_type=F32)
                return dq, pre1, pre2

            dq, _, _ = lax.fori_loop(0, nkb // 2, lambda i, c: pass2(2 * i + 1, pass2(2 * i, c)),
                                     (jnp.zeros((SB_ROWS, LANES), F32), zcol, zcol))
            dq_ref[rows, :] = (dq * scale).astype(BF16)
            return 0

        lax.fori_loop(0, s_dim // SB_ROWS, qloop, 0)
        dk_ref[...] = dka[...].astype(BF16)
        dv_ref[...] = dva[...].astype(BF16)
        if n_w:
            pl.when(step == b_dim * PAIRS - 1)(x_finish)

    def col(c0):
        return pl.BlockSpec((None, s_dim, LANES),lambda b, h: (b, 0, c0 + h))

    shp = jax.ShapeDtypeStruct((b_dim, s_dim, ATT_WIDTH), BF16)
    acc = pltpu.VMEM((s_dim, LANES), F32)
    strip = pltpu.VMEM((2, nkb_max, SB_ROWS, BLOCK), F32)
    res = pl.pallas_call(
        body, name="sb_bwd", grid=(b_dim, PAIRS),
        in_specs=[col(COL_QB), col(COL_KB), col(COL_VB), col(0)] + [ANY] * n_w,
        out_specs=[col(0), col(0), col(0)] + [ANY] * n_w,
        out_shape=[shp, shp, shp] + _pair_shapes(grads),
        scratch_shapes=[acc, acc, strip, strip] + (_exchange_sems(n_w) if n_w else []),
        compiler_params=_params(("arbitrary", "arbitrary")),
    )(proj3, proj3, proj3, do3, *grads)
    return res[:3], res[3:]


def _sigmoid(x):
    return 1.0 / (1.0 + jnp.exp(-x))


def _gate_fwd(proj, ua, ub, *, tt=512):
    t_dim, d = ua.shape

    def body(ga_ref, gb_ref, ua_ref, ub_ref, o_ref):
        o_ref[...] = (_sigmoid(ga_ref[...]) * ua_ref[...] + _sigmoid(gb_ref[...]) * ub_ref[...]).astype(BF16)

    row = pl.BlockSpec((tt, d), lambda i: (i, 0))
    return pl.pallas_call(
        body, name="gate_fwd", grid=(t_dim // tt,),
        in_specs=[pl.BlockSpec((tt, d), lambda i: (i, 3)), pl.BlockSpec((tt, d), lambda i: (i, 4)), row, row],
        out_specs=row, out_shape=jax.ShapeDtypeStruct((t_dim, d), BF16),
        compiler_params=_params(("parallel",)),
    )(proj, proj, ua, ub)


def _gate_bwd(proj, ua, ub, dmix, *, tt=512):
    t_dim, d = ua.shape

    def body(ga_ref, gb_ref, ua_ref, ub_ref, dm_ref, dua_ref, dub_ref, dg_ref):
        dm = dm_ref[...]
        sa = _sigmoid(ga_ref[...])
        sb = _sigmoid(gb_ref[...])
        dua_ref[...] = (dm * sa).astype(BF16)
        dub_ref[...] = (dm * sb).astype(BF16)
        dg_ref[:, :d] = (dm * ua_ref[...] * (sa * (1.0 - sa))).astype(BF16)
        dg_ref[:, d:] = (dm * ub_ref[...] * (sb * (1.0 - sb))).astype(BF16)

    row = pl.BlockSpec((tt, d), lambda i: (i, 0))
    wide = pl.BlockSpec((tt, 2 * d), lambda i: (i, 0))
    return pl.pallas_call(
        body, name="gate_bwd", grid=(t_dim // tt,),
        in_specs=[pl.BlockSpec((tt, d), lambda i: (i, 3)), pl.BlockSpec((tt, d), lambda i: (i, 4)), row, row, row],
        out_specs=[row, row, wide],
        out_shape=[jax.ShapeDtypeStruct((t_dim, d), BF16), jax.ShapeDtypeStruct((t_dim, d), BF16),
                   jax.ShapeDtypeStruct((t_dim, 2 * d), BF16)],
        compiler_params=_params(("parallel",)),
    )(proj, proj, ua, ub, dmix)


def _swiglu_fwd(g3, u3, *, tt=1024):
    n, t_dim, f4 = g3.shape

    def body(g_ref, u_ref, o_ref):
        gv = g_ref[...]
        o_ref[...] = (gv * _sigmoid(gv) * u_ref[...]).astype(BF16)

    spec = pl.BlockSpec((None, tt, f4), lambda s, i: (s, i, 0))
    return pl.pallas_call(
        body, name="swiglu_fwd", grid=(n, t_dim // tt), in_specs=[spec, spec], out_specs=spec,
        out_shape=jax.ShapeDtypeStruct(g3.shape, BF16),
        compiler_params=_params(("parallel", "parallel")),
    )(g3, u3)


def _swiglu_bwd(g3, u3, dact3, *, tt=1024):
    n, t_dim, f4 = g3.shape

    def body(g_ref, u_ref, da_ref, dg_ref, du_ref):
        gv = g_ref[...]
        da = da_ref[...]
        sg = _sigmoid(gv)
        dg_ref[...] = (da * u_ref[...] * (sg + gv * sg * (1.0 - sg))).astype(BF16)
        du_ref[...] = (da * (gv * sg)).astype(BF16)

    spec = pl.BlockSpec((None, tt, f4), lambda s, i: (s, i, 0))
    shp = jax.ShapeDtypeStruct(g3.shape, BF16)
    return pl.pallas_call(
        body, name="swiglu_bwd", grid=(n, t_dim // tt), in_specs=[spec, spec, spec], out_specs=[spec, spec],
        out_shape=[shp, shp],
        compiler_params=_params(("parallel", "parallel")),
    )(g3, u3, dact3)


def _mem_fwd(qm, kvm, *, tt=512):
    b_dim, s_dim, _ = qm.shape
    n_mem = kvm.shape[1]
    scale = MEM_HEAD_DIM ** -0.5

    def body(q_ref, k_ref, v_ref, o_ref):
        sc = lax.dot_general(q_ref[0], k_ref[0], NT, preferred_element_type=F32) * scale
        p = jnp.exp(sc - jnp.max(sc, axis=-1, keepdims=True))
        p = p / jnp.sum(p, axis=-1, keepdims=True)
        o_ref[0] = jnp.dot(p.astype(BF16), v_ref[0], preferred_element_type=F32).astype(BF16)

    qs = pl.BlockSpec((1, tt, MEM_HEAD_DIM), lambda b, h, i: (b, i, h))
    return pl.pallas_call(
        body, name="mem_fwd", grid=(b_dim, N_HEADS_MEM, s_dim // tt),
        in_specs=[qs, pl.BlockSpec((1, n_mem, MEM_HEAD_DIM), lambda b, h, i: (b, 0, h)),
                  pl.BlockSpec((1, n_mem, MEM_HEAD_DIM), lambda b, h, i: (b, 0, N_HEADS_MEM + h))],
        out_specs=qs, out_shape=jax.ShapeDtypeStruct(qm.shape, BF16),
        compiler_params=_params(("parallel", "parallel", "parallel")),
    )(qm, kvm, kvm)


def _mem_bwd(qm, kvm, dom, *, tt=512):
    b_dim, s_dim, _ = qm.shape
    n_mem = kvm.shape[1]
    scale = MEM_HEAD_DIM ** -0.5

    def body(q_ref, k_ref, v_ref, do_ref, dq_ref, dk_ref, dv_ref):
        qv, kv, vv, dov = q_ref[0], k_ref[0], v_ref[0], do_ref[0]
        sc = lax.dot_general(qv, kv, NT, preferred_element_type=F32) * scale
        p = jnp.exp(sc - jnp.max(sc, axis=-1, keepdims=True))
        p = p / jnp.sum(p, axis=-1, keepdims=True)
        dp = lax.dot_general(dov, vv, NT, preferred_element_type=F32)
        ds = (p * (dp - jnp.sum(p * dp, axis=-1, keepdims=True)) * scale).astype(BF16)
        dq_ref[0] = jnp.dot(ds, kv, preferred_element_type=F32).astype(BF16)

        @pl.when(pl.program_id(2) == 0)
        def _():
            dk_ref[...] = jnp.zeros_like(dk_ref)
            dv_ref[...] = jnp.zeros_like(dv_ref)

        dk_ref[0] += lax.dot_general(ds, qv, TN, preferred_element_type=F32)
        dv_ref[0] += lax.dot_general(p.astype(BF16), dov, TN, preferred_element_type=F32)

    qs = pl.BlockSpec((1, tt, MEM_HEAD_DIM), lambda b, h, i: (b, i, h))
    ks = pl.BlockSpec((1, n_mem, MEM_HEAD_DIM), lambda b, h, i: (b, 0, h))
    vs = pl.BlockSpec((1, n_mem, MEM_HEAD_DIM), lambda b, h, i: (b, 0, N_HEADS_MEM + h))
    return pl.pallas_call(
        body, name="mem_bwd", grid=(b_dim, N_HEADS_MEM, s_dim // tt),
        in_specs=[qs, ks, vs, qs], out_specs=[qs, ks, ks],
        out_shape=[jax.ShapeDtypeStruct(qm.shape, BF16), jax.ShapeDtypeStruct((b_dim, n_mem, MEM_WIDTH), F32),
                   jax.ShapeDtypeStruct((b_dim, n_mem, MEM_WIDTH), F32)],
        compiler_params=_params(("parallel", "parallel", "arbitrary")),
    )(qm, kvm, kvm, dom)


def _adamw_math(wv, gv, mv, vv):
    nm = ADAM_B1 * mv + (1.0 - ADAM_B1) * gv
    nv = ADAM_B2 * vv + (1.0 - ADAM_B2) * (gv * gv)
    m_hat = nm / (1.0 - ADAM_B1 ** ADAM_STEP)
    v_hat = nv / (1.0 - ADAM_B2 ** ADAM_STEP)
    return -ADAM_LR * (m_hat / (jnp.sqrt(v_hat) + ADAM_EPS) + ADAM_WD * wv), nm, nv


def _adamw(w, g, m, v, *, name):
    rows, cols = w.shape
    tr = _tile(rows, 256, 8)

    def body(w_ref, g_ref, m_ref, v_ref, d_ref, nm_ref, nv_ref):
        d_ref[...], nm_ref[...], nv_ref[...] = _adamw_math(w_ref[...], g_ref[...], m_ref[...], v_ref[...])

    spec = pl.BlockSpec((tr, cols), lambda i: (i, 0))
    shp = jax.ShapeDtypeStruct((rows, cols), F32)
    return pl.pallas_call(
        body, name=name, grid=(rows // tr,),
        in_specs=[spec] * 4, out_specs=[spec] * 3, out_shape=[shp] * 3,
        compiler_params=_params(("parallel",)),
    )(w, g, m, v)


def _prefetch_spec(grid, in_specs, out_specs):
    return pltpu.PrefetchScalarGridSpec(num_scalar_prefetch=1, grid=grid, in_specs=in_specs, out_specs=out_specs)


def _adamw_halves(w, mine, theirs, m, v, c_idx, *, name):
    rows, cols = w.shape
    half = rows // 2
    tr = _tile(half, 256, 8)
    nh = half // tr

    def body(c_ref, w_ref, mine_ref, theirs_ref, m_ref, v_ref, g_ref, d_ref, nm_ref, nv_ref):
        gv = jnp.where(pl.program_id(0) == c_ref[0], mine_ref[...], theirs_ref[...])
        g_ref[...] = gv
        d_ref[...], nm_ref[...], nv_ref[...] = _adamw_math(w_ref[...], gv, m_ref[...], v_ref[...])

    full = pl.BlockSpec((tr, cols), lambda h, i, c_ref: (h * nh + i, 0))
    part = pl.BlockSpec((tr, cols), lambda h, i, c_ref: (i, 0))
    shp = jax.ShapeDtypeStruct((rows, cols), F32)
    return pl.pallas_call(
        body, name=name, grid_spec=_prefetch_spec((2, nh), [full, part, part, full, full], [full] * 4),
        out_shape=[shp] * 4,
        compiler_params=_params(("parallel", "parallel")),
    )(c_idx, w, mine, theirs, m, v)


def _pair_sum(g3, theirs, c_idx, *, name):
    n, rows, cols = g3.shape
    half = rows // 2
    tr = _tile(half, 256, 16)

    def body(c_ref, g_ref, t_ref, o_ref):
        o_ref[...] = (g_ref[...] + t_ref[...]).astype(BF16)

    part = pl.BlockSpec((None, tr, cols), lambda s, i, c_ref: (s, i, 0))
    return pl.pallas_call(
        body, name=name,
        grid_spec=_prefetch_spec((n, half // tr),
                                 [pl.BlockSpec((None, None, tr, cols), lambda s, i, c_ref: (s, c_ref[0], i, 0)), part],
                                 part),
        out_shape=jax.ShapeDtypeStruct((n, half, cols), BF16),
        compiler_params=_params(("parallel", "parallel")),
    )(c_idx, g3.reshape(n, 2, half, cols), theirs)


def _chip_sum(pair, recv, s_idx, *, name):
    _, half, cols = pair.shape
    tr = _tile(half, 256, 16)

    def body(s_ref, p_ref, r_ref, o_ref):
        o_ref[...] = ((p_ref[...].astype(F32) + r_ref[0].astype(F32)) + r_ref[1].astype(F32)) + r_ref[2].astype(F32)

    return pl.pallas_call(
        body, name=name,
        grid_spec=_prefetch_spec((half // tr,),
                                 [pl.BlockSpec((None, tr, cols), lambda i, s_ref: (s_ref[0], i, 0)),
                                  pl.BlockSpec((N_CHIPS - 1, tr, cols), lambda i, s_ref: (0, i, 0))],
                                 pl.BlockSpec((tr, cols), lambda i, s_ref: (i, 0))),
        out_shape=jax.ShapeDtypeStruct((half, cols), F32),
        compiler_params=_params(("parallel",)),
    )(s_idx, pair, recv)


def _sum8(parts):
    n, rows, cols = parts.shape

    def body(p_ref, o_ref):
        acc = p_ref[0]
        for i in range(1, n):
            acc = acc + p_ref[i]
        o_ref[...] = acc

    return pl.pallas_call(
        body, name="small_sum", grid=(1,),
        in_specs=[pl.BlockSpec((n, rows, cols), lambda i: (0, 0, 0))],
        out_specs=pl.BlockSpec((rows, cols), lambda i: (0, 0)),
        out_shape=jax.ShapeDtypeStruct((rows, cols), parts.dtype),
        compiler_params=_params(("arbitrary",)),
    )(parts)


def _place():
    return lax.axis_index("x"), lax.axis_index("y"), lax.axis_index("c")


ANY = pl.BlockSpec(memory_space=pl.ANY)


def _gather_weights(shards):
    n = len(shards)

    def body(*refs):
        start, forward, finish = _gather_steps(refs[:n], refs[n:2 * n], refs[2 * n], refs[2 * n + 1])
        start()
        forward()
        finish()

    return pl.pallas_call(
        body, name="gather_weights", out_shape=_gathered_shapes(shards),
        in_specs=[ANY] * n, out_specs=[ANY] * n, scratch_shapes=_gather_sems(n),
    )(*shards)


def _gathered_shapes(shards):
    return [jax.ShapeDtypeStruct((N_CHIPS,) + s.shape, s.dtype) for s in shards]


def _gather_sems(n):
    return [pltpu.SemaphoreType.DMA((7 * n,)), pltpu.SemaphoreType.DMA((7 * n,))]


def _gather_steps(ins, outs, send_sems, recv_sems):
    n = len(ins)
    halves = [r.shape[0] // 2 for r in ins]
    x, y, c = _place()
    my_chip = 2 * x + y
    me, sibling = (x, y, c), (x, y, 1 - c)
    chips = [(1 - x, y), (x, 1 - y), (1 - x, 1 - y)]

    def half_of(w, chip, pc):
        return outs[w].at[chip, pl.ds(pc * halves[w], halves[w]), :]

    def copy(w, k, src, dst, to):
        return pltpu.make_async_remote_copy(
            src_ref=src, dst_ref=dst, send_sem=send_sems.at[7 * w + k], recv_sem=recv_sems.at[7 * w + k],
            device_id=to, device_id_type=MESH)

    def firsts():
        cps = []
        for w in range(n):
            cps.append(copy(w, 0, ins[w], outs[w].at[my_chip], sibling))
            mine = ins[w].at[pl.ds(c * halves[w], halves[w]), :]
            for j, (px, py) in enumerate(chips):
                cps.append(copy(w, 1 + j, mine, half_of(w, my_chip, c), (px, py, c)))
        return cps

    def passes():
        return [copy(w, 4 + j, half_of(w, 2 * px + py, c), half_of(w, 2 * px + py, c), sibling)
                for w in range(n) for j, (px, py) in enumerate(chips)]

    def start():
        for cp in firsts():
            cp.start()

    def forward():
        fws = passes()
        for w in range(n):
            for j, (px, py) in enumerate(chips):
                landed = half_of(w, 2 * px + py, c)
                copy(w, 1 + j, landed, landed, me).wait_recv()
                fws[3 * w + j].start()

    def finish():
        for w in range(n):
            copy(w, 0, ins[w], outs[w].at[my_chip], me).wait_recv()
            for j, (px, py) in enumerate(chips):
                landed = half_of(w, 2 * px + py, 1 - c)
                copy(w, 4 + j, landed, landed, me).wait_recv()
        for cp in firsts() + passes():
            cp.wait_send()

    return start, forward, finish


def _pair_exchange(grads):
    n = len(grads)

    def body(*refs):
        start, finish = _pair_steps(refs[:n], refs[n:2 * n], refs[2 * n], refs[2 * n + 1])
        start()
        finish()

    return pl.pallas_call(
        body, name="grad_pair_exchange", out_shape=_pair_shapes(grads),
        in_specs=[ANY] * n, out_specs=[ANY] * n, scratch_shapes=_exchange_sems(n),
    )(*grads)


def _pair_shapes(grads):
    return [jax.ShapeDtypeStruct((g.shape[0], g.shape[1] // 2, g.shape[2]), g.dtype) for g in grads]


def _exchange_sems(n):
    return [pltpu.SemaphoreType.DMA((n,)), pltpu.SemaphoreType.DMA((n,))]


def _exchange_steps(copies):
    def start():
        for cp in copies():
            cp.start()

    def finish():
        for cp in copies():
            cp.wait()

    return start, finish


def _pair_steps(ins, outs, send_sems, recv_sems):
    x, y, c = _place()

    def copies():
        return [pltpu.make_async_remote_copy(
            src_ref=ins[w].at[:, pl.ds((1 - c) * (ins[w].shape[1] // 2), ins[w].shape[1] // 2), :], dst_ref=outs[w],
            send_sem=send_sems.at[w], recv_sem=recv_sems.at[w], device_id=(x, y, 1 - c), device_id_type=MESH)
            for w in range(len(ins))]

    return _exchange_steps(copies)


def _chip_exchange(pairs):
    n = len(pairs)

    def body(*refs):
        start, finish = _chip_steps(refs[:n], refs[n:2 * n], refs[2 * n], refs[2 * n + 1])
        start()
        finish()

    return pl.pallas_call(
        body, name="grad_chip_exchange", out_shape=_chip_shapes(pairs),
        in_specs=[ANY] * n, out_specs=[ANY] * n, scratch_shapes=_exchange_sems(3 * n),
    )(*pairs)


def _chip_shapes(pairs):
    return [jax.ShapeDtypeStruct((N_CHIPS - 1,) + p.shape[1:], p.dtype) for p in pairs]


def _chip_steps(ins, outs, send_sems, recv_sems):
    x, y, c = _place()
    others = [(1 - x, y), (x, 1 - y), (1 - x, 1 - y)]

    def copies():
        return [pltpu.make_async_remote_copy(
            src_ref=ins[w].at[2 * px + py], dst_ref=outs[w].at[j],
            send_sem=send_sems.at[3 * w + j], recv_sem=recv_sems.at[3 * w + j],
            device_id=(px, py, c), device_id_type=MESH)
            for w in range(len(ins)) for j, (px, py) in enumerate(others)]

    return _exchange_steps(copies)


def _swap_halves(mine):
    n = len(mine)

    def body(*refs):
        ins, outs, send_sems, recv_sems = refs[:n], refs[n:2 * n], refs[2 * n], refs[2 * n + 1]
        x, y, c = _place()
        copies = [pltpu.make_async_remote_copy(
            src_ref=ins[w], dst_ref=outs[w], send_sem=send_sems.at[w], recv_sem=recv_sems.at[w],
            device_id=(x, y, 1 - c), device_id_type=MESH) for w in range(n)]
        for cp in copies:
            cp.start()
        for cp in copies:
            cp.wait()

    return pl.pallas_call(
        body, name="grad_swap_halves",
        out_shape=[jax.ShapeDtypeStruct(h.shape, h.dtype) for h in mine],
        in_specs=[ANY] * n, out_specs=[ANY] * n,
        scratch_shapes=[pltpu.SemaphoreType.DMA((n,)), pltpu.SemaphoreType.DMA((n,))],
    )(*mine)


def _gather_small(small):
    srows, cols = small.shape

    def body(s_ref, all_ref, send_sems, recv_sems, local_sem):
        x, y, c = _place()
        me = 4 * x + 2 * y + c
        keep_small = pltpu.make_async_copy(s_ref, all_ref.at[me], local_sem)
        keep_small.start()
        sends = []
        for kk in range(1, 8):
            peer = (x ^ (kk >> 2), y ^ ((kk >> 1) & 1), c ^ (kk & 1))
            sends.append(pltpu.make_async_remote_copy(
                src_ref=s_ref, dst_ref=all_ref.at[me],
                send_sem=send_sems.at[kk], recv_sem=recv_sems.at[kk], device_id=peer, device_id_type=MESH))
        for cp in sends:
            cp.start()
        for kk in range(1, 8):
            px, py, pc = x ^ (kk >> 2), y ^ ((kk >> 1) & 1), c ^ (kk & 1)
            pltpu.make_async_remote_copy(
                src_ref=s_ref, dst_ref=all_ref.at[4 * px + 2 * py + pc],
                send_sem=send_sems.at[kk], recv_sem=recv_sems.at[kk], device_id=(px, py, pc),
                device_id_type=MESH).wait_recv()
        for cp in sends:
            cp.wait_send()
        keep_small.wait()

    return pl.pallas_call(
        body, name="gather_small",
        out_shape=jax.ShapeDtypeStruct((8, srows, cols), small.dtype),
        in_specs=[ANY], out_specs=ANY,
        scratch_shapes=[pltpu.SemaphoreType.DMA((8,)), pltpu.SemaphoreType.DMA((8,)), pltpu.SemaphoreType.DMA],
    )(small)


SHARDED = (("w_in", D_MODEL, IN_COLS, 1), ("w_up_a", ATT_WIDTH, D_MODEL, 1), ("w_up_b", ATT_WIDTH, D_MODEL, 1),
           ("w_out", D_MODEL, D_MODEL, 0), ("w_q_mem", D_MODEL, MEM_WIDTH, 0), ("w_kv_mem", D_MODEL, 2 * MEM_WIDTH, 0),
           ("w_o_mem", MEM_WIDTH, D_MODEL, 1), ("w_ffn_gate", D_MODEL, D_FF, 1), ("w_ffn_up", D_MODEL, D_FF, 1),
           ("w_ffn_down", D_FF, D_MODEL, 0))
NAMES = tuple(n for n, _, _, _ in SHARDED)
EARLY, LATE = NAMES[:1], NAMES[1:]
GAINS = ("g_mix", "g_mem_q", "g_mem_kv", "g_ffn", "g_final")


def _natural(w3):
    n, r, c = w3.shape
    return w3.reshape(n * r, c)


def _shard_major(g, axis):
    if axis == 1:
        return g
    r, c = g.shape
    return g.reshape(N_CHIPS, r // N_CHIPS, c)


def kernel(x, mem, positions, g_mix, w_in, w_up_a, w_up_b, w_out, g_mem_q, g_mem_kv, w_q_mem, w_kv_mem, w_o_mem, g_ffn, w_ffn_gate, w_ffn_up, w_ffn_down, g_final, loss_target, m_g_mix, m_w_in, m_w_up_a, m_w_up_b, m_w_out, m_g_mem_q, m_g_mem_kv, m_w_q_mem, m_w_kv_mem, m_w_o_mem, m_g_ffn, m_w_ffn_gate, m_w_ffn_up, m_w_ffn_down, m_g_final, v_g_mix, v_w_in, v_w_up_a, v_w_up_b, v_w_out, v_g_mem_q, v_g_mem_kv, v_w_q_mem, v_w_kv_mem, v_w_o_mem, v_g_ffn, v_w_ffn_gate, v_w_ffn_up, v_w_ffn_down, v_g_final):
    given = dict(locals())
    shards = {n: given[n][0] for n, _, _, _ in SHARDED}

    wf = dict(zip(EARLY, _gather_weights([shards[n].astype(BF16) for n in EARLY])))
    late_shards = [shards[n].astype(BF16) for n in LATE]
    c_idx = lax.axis_index("c").astype(jnp.int32).reshape(1)
    s_idx = (2 * lax.axis_index("x") + lax.axis_index("y")).astype(jnp.int32).reshape(1)

    loss_row, grad_x, mine, gain_grads = _local_step(x, mem, positions, loss_target, g_mix, g_mem_q, g_mem_kv,
                                                     g_ffn, g_final, wf, late_shards, (c_idx, s_idx))
    return _reduce_and_update(given, shards, loss_row, grad_x, mine, gain_grads, c_idx)


def _reduce_halves(glist, names, c_idx, s_idx, pair_exchange, chip_exchange):
    theirs = pair_exchange(glist)
    pairs = [_pair_sum(g, t, c_idx, name="pair_sum_" + n) for n, g, t in zip(names, glist, theirs)]
    recv = chip_exchange(pairs)
    return [_chip_sum(p, r, s_idx, name="chip_sum_" + n) for n, p, r in zip(names, pairs, recv)]


def _local_step(x, mem, positions, loss_target, g_mix, g_mem_q, g_mem_kv, g_ffn, g_final, wf,
                late_shards=None, place=None):
    b_dim, s_dim, d = x.shape
    t_dim = b_dim * s_dim
    n_mem = mem.shape[1]
    wf = dict(wf)

    xb = x.reshape(t_dim, d)
    tgt = loss_target.reshape(t_dim, d)
    memf = mem.reshape(b_dim * n_mem, d)
    gfin = g_final.reshape(1, d)
    pos = positions.reshape(t_dim, 1).astype(F32)

    lane = jnp.arange(LANES) % HEAD_DIM
    half = ROPE_DIM // 2
    inv_freq = ROPE_THETA ** (-jnp.arange(half, dtype=F32) / half)
    inv_lane = jnp.where(lane < ROPE_DIM, inv_freq[lane % half], 0.0).reshape(1, -1).astype(F32)
    sel_a = (lane < half).astype(F32).reshape(1, -1)
    sel_b = ((lane >= half) & (lane < ROPE_DIM)).astype(F32).reshape(1, -1)

    def rows3(t):
        return t.reshape(b_dim, s_dim, t.shape[-1])

    def rows2(t):
        return t.reshape(t_dim, t.shape[-1])

    n1 = _rms_fwd(xb, g_mix, name="rms_mix")
    proj = _mm_cs(n1, wf["w_in"], name="mm_in")
    proj3 = rows3(proj)
    cs, sn = _rope_table(pos, inv_lane, sel_a, sel_b)
    cs3, sn3 = rows3(cs), rows3(sn)
    (oa16, oa32, lse_a), gathered = _dil_fwd(proj3, cs3, sn3, sel_a, sel_b, late_shards or ())
    wf.update(zip(LATE, gathered))
    w_out, w_q, w_kv = _natural(wf["w_out"]), _natural(wf["w_q_mem"]), _natural(wf["w_kv_mem"])
    ob16 = _sb_fwd(proj3)
    oa, ob = rows2(oa16), rows2(ob16)
    ua = _mm_cs(oa, wf["w_up_a"], name="mm_up_a")
    ub = _mm_cs(ob, wf["w_up_b"], name="mm_up_b")
    mixed = _gate_fwd(proj, ua, ub)
    h1 = _mm(mixed, w_out, name="mm_out", add=xb)

    hn = _rms_fwd(h1, g_mem_q, name="rms_mem_q")
    memn = _rms_fwd(memf, g_mem_kv, name="rms_mem_kv")
    qm = _mm(hn, w_q, name="mm_q_mem", out_dtype=BF16)
    kvm = _mm(memn, w_kv, name="mm_kv_mem", out_dtype=BF16)
    qm3, kvm3 = rows3(qm), kvm.reshape(b_dim, n_mem, 2 * MEM_WIDTH)
    om = rows2(_mem_fwd(qm3, kvm3))
    h2 = _mm_cs(om, wf["w_o_mem"], name="mm_o_mem", add=h1)

    n3 = _rms_fwd(h2, g_ffn, name="rms_ffn")
    gate3 = _mm_ffn_up(n3, wf["w_ffn_gate"], name="mm_gate")
    up3 = _mm_ffn_up(n3, wf["w_ffn_up"], name="mm_up")
    act3 = _swiglu_fwd(gate3, up3)
    h3 = _mm_ffn_down(act3, wf["w_ffn_down"], name="mm_down", add=h2)
    loss_row, dh3, dg_final = _final(h3, gfin, tgt)

    grads = {}
    dact3 = _mm_ffn_down_dx(dh3, wf["w_ffn_down"], name="mm_down_dx")
    grads["w_ffn_down"] = _mm_ffn_down_dw(act3, dh3, name="mm_down_dw")
    dgate3, dup3 = _swiglu_bwd(gate3, up3, dact3)
    grads["w_ffn_gate"] = _mm_ffn_up_dw(n3, dgate3, name="mm_gate_dw")
    grads["w_ffn_up"] = _mm_ffn_up_dw(n3, dup3, name="mm_up_dw")
    dn3 = _mm_ffn_up_dx(dgate3, wf["w_ffn_gate"], name="mm_gate_dx")
    dn3 = _mm_ffn_up_dx(dup3, wf["w_ffn_up"], name="mm_up_dx", add=dn3)
    dh2, dg_ffn = _rms_bwd(h2, g_ffn, dn3, dh3, name="rms_ffn_bwd")

    dom = _mm_cs_dx(dh2, wf["w_o_mem"], name="mm_o_mem_dx", out_dtype=BF16)
    grads["w_o_mem"] = _mm_cs_dw(om, dh2, name="mm_o_mem_dw")
    dqm, dkm, dvm = _mem_bwd(qm3, kvm3, rows3(dom))
    dqm = rows2(dqm)
    dkvm = jnp.concatenate([dkm, dvm], axis=-1).reshape(b_dim * n_mem, 2 * MEM_WIDTH).astype(BF16)
    grads["w_q_mem"] = _shard_major(_mm(hn, dqm, name="mm_q_mem_dw", ta=True), 0)
    dhn = _mm(dqm, w_q, name="mm_q_mem_dx", tb=True)
    grads["w_kv_mem"] = _shard_major(_mm(memn, dkvm, name="mm_kv_mem_dw", ta=True), 0)
    dmemn = _mm(dkvm, w_kv, name="mm_kv_mem_dx", tb=True)
    _, dg_mem_kv = _rms_bwd(memf, g_mem_kv, dmemn, None, name="rms_mem_kv_bwd")
    dh1, dg_mem_q = _rms_bwd(h1, g_mem_q, dhn, dh2, name="rms_mem_q_bwd")

    dmix = _mm(dh1, w_out, name="mm_out_dx", tb=True)
    grads["w_out"] = _shard_major(_mm(mixed, dh1, name="mm_out_dw", ta=True), 0)
    dua, dub, dgates = _gate_bwd(proj, ua, ub, dmix)
    doa = _mm_cs_dx(dua, wf["w_up_a"], name="mm_up_a_dx")
    grads["w_up_a"] = _mm_cs_dw(oa, dua, name="mm_up_a_dw")
    dob = _mm_cs_dx(dub, wf["w_up_b"], name="mm_up_b_dx", out_dtype=BF16)
    grads["w_up_b"] = _mm_cs_dw(ob, dub, name="mm_up_b_dw")

    att = {}

    def sb_with_pairs(glist):
        att["b"], theirs = _sb_bwd(proj3, rows3(dob), glist)
        return theirs

    def dil_with_chips(pairs):
        att["a"], recv = _dil_bwd(proj3, cs3, sn3, sel_a, sel_b, rows3(doa), oa32, lse_a, pairs)
        return recv

    if place is None:
        sb_with_pairs(())
        dil_with_chips(())
    else:
        mine_late = _reduce_halves([grads[n] for n in LATE], LATE, *place, sb_with_pairs, dil_with_chips)
    dproj = jnp.concatenate([rows2(t) for t in att["a"] + att["b"]] + [dgates], axis=1)
    grads["w_in"] = _mm_cs_dw(n1, dproj, name="mm_in_dw")
    dn1 = _mm_cs_dx(dproj, wf["w_in"], name="mm_in_dx")
    dx, dg_mix = _rms_bwd(xb, g_mix, dn1, dh1, name="rms_mix_bwd")
    grad_x = dx.reshape(b_dim, s_dim, d)
    gains = (dg_mix, dg_mem_q, dg_mem_kv, dg_ffn, dg_final)
    if place is None:
        return loss_row, grad_x, grads, gains
    mine_early = _reduce_halves([grads[n] for n in EARLY], EARLY, *place, _pair_exchange, _chip_exchange)
    return loss_row, grad_x, mine_early + mine_late, gains


def _reduce_and_update(given, shards, loss_row, grad_x, mine, gain_grads, c_idx):
    d = D_MODEL
    dg_mix, dg_mem_q, dg_mem_kv, dg_ffn, dg_final = gain_grads
    small = jnp.concatenate([dg_mix, dg_mem_q, dg_mem_kv, dg_ffn, dg_final,
                             jnp.pad(loss_row, ((0, 0), (0, FLAT_COLS - LANES))), jnp.zeros((2, FLAT_COLS), F32)], axis=0)
    small_all = _gather_small(small)
    others = _swap_halves(mine)
    small_sum = _sum8(small_all)
    loss = small_sum[5, 0]

    out_g, out_d, out_m, out_v = {}, {}, {}, {}
    for n, mine_n, other_n in zip(NAMES, mine, others):
        g2, dl, nm, nv = _adamw_halves(shards[n], mine_n, other_n, given["m_" + n][0], given["v_" + n][0], c_idx,
                                       name="adamw_" + n)
        out_g[n], out_d[n], out_m[n], out_v[n] = g2[None], dl[None], nm[None], nv[None]
    gain_w = jnp.concatenate([given[n].reshape(1, d) for n in GAINS], axis=0)
    gain_m = jnp.concatenate([given["m_" + n].reshape(1, d) for n in GAINS], axis=0)
    gain_v = jnp.concatenate([given["v_" + n].reshape(1, d) for n in GAINS], axis=0)
    gain_g = small_sum[:len(GAINS)]
    gd, gm, gv = _adamw(gain_w, gain_g, gain_m, gain_v, name="adamw_gains")
    for i, n in enumerate(GAINS):
        shape = given[n].shape
        out_g[n], out_d[n] = gain_g[i].reshape(shape), gd[i].reshape(shape)
        out_m[n], out_v[n] = gm[i].reshape(shape), gv[i].reshape(shape)

    order = ["g_mix", "w_in", "w_up_a", "w_up_b", "w_out", "g_mem_q", "g_mem_kv", "w_q_mem", "w_kv_mem", "w_o_mem",
             "g_ffn", "w_ffn_gate", "w_ffn_up", "w_ffn_down", "g_final"]
    return (loss, grad_x, *[out_g[n] for n in order], *[out_d[n] for n in order],
            *[out_m[n] for n in order], *[out_v[n] for n in order])
```

```python
import jax
import jax.numpy as jnp
from jax import lax
from jax.experimental import pallas as pl
from jax.experimental.pallas import tpu as pltpu

F32 = jnp.float32
BF16 = jnp.bfloat16
MESH = pl.DeviceIdType.MESH

D_MODEL = 1024
HEAD_DIM = 64
N_HEADS = 8
ATT_WIDTH = N_HEADS * HEAD_DIM
DIL_PATTERNS = ((128, 1), (512, 4), (2048, 16))
BLOCK = 128
SB_ROWS = 512
ROPE_THETA = 500000.0
ROPE_DIM = HEAD_DIM // 4
N_HEADS_MEM = 4
MEM_HEAD_DIM = 128
MEM_WIDTH = N_HEADS_MEM * MEM_HEAD_DIM
D_FF = 2816
IN_COLS = 6 * ATT_WIDTH + 2 * D_MODEL
RMS_EPS = 1e-6
ADAM_LR = 0.001
ADAM_B1 = 0.9
ADAM_B2 = 0.999
ADAM_EPS = 1e-08
ADAM_WD = 0.01
ADAM_STEP = 10

N_CHIPS = 4
LANES = 128
FLAT_COLS = 1024
VMEM_LIMIT = 56 * 1024 * 1024

PAIRS = ATT_WIDTH // LANES
COL_QA, COL_KA, COL_VA, COL_QB, COL_KB, COL_VB = (i * PAIRS for i in range(6))

MM_CAP = 1408
NN = (((1,), (0,)), ((), ()))
NT = (((1,), (1,)), ((), ()))
TN = (((0,), (0,)), ((), ()))
BNN = (((2,), (1,)), ((0,), (0,)))
BNT = (((2,), (2,)), ((0,), (0,)))
BTN = (((1,), (1,)), ((0,), (0,)))
DIL_BATCH = 4


def _tile(dim, cap, unit=LANES):
    if dim <= cap:
        return dim
    best = None
    for t in range(unit, cap + 1, unit):
        if dim % t == 0:
            best = t
    assert best is not None, (dim, cap)
    return best


def _params(sem):
    return pltpu.CompilerParams(dimension_semantics=sem, vmem_limit_bytes=VMEM_LIMIT)


def _mm(a, b, *, name, ta=False, tb=False, add=None, out_dtype=F32,
        tm_cap=MM_CAP, tn_cap=MM_CAP, tk_cap=MM_CAP):
    if ta:
        k_dim, m_dim = a.shape
    else:
        m_dim, k_dim = a.shape
    if tb:
        n_dim, kb = b.shape
    else:
        kb, n_dim = b.shape
    assert kb == k_dim, (a.shape, b.shape, ta, tb)
    tm, tn, tk = _tile(m_dim, tm_cap), _tile(n_dim, tn_cap), _tile(k_dim, tk_cap)
    nk = k_dim // tk
    dims = (((0 if ta else 1,), (1 if tb else 0,)), ((), ()))
    has_add = add is not None

    def body(*refs):
        if has_add:
            a_ref, b_ref, add_ref, o_ref = refs[:4]
        else:
            a_ref, b_ref, o_ref = refs[:3]
        part = lax.dot_general(a_ref[...].astype(BF16), b_ref[...].astype(BF16), dims, preferred_element_type=F32)

        def finish(r):
            if has_add:
                r = add_ref[...] + r
            o_ref[...] = r.astype(out_dtype)

        if nk == 1:
            finish(part)
            return
        acc_ref = refs[-1]
        k = pl.program_id(2)

        @pl.when(k == 0)
        def _():
            acc_ref[...] = part

        @pl.when(k > 0)
        def _():
            acc_ref[...] += part

        @pl.when(k == nk - 1)
        def _():
            finish(acc_ref[...])

    a_spec = pl.BlockSpec((tk, tm), lambda i, j, k: (k, i)) if ta else pl.BlockSpec((tm, tk), lambda i, j, k: (i, k))
    b_spec = pl.BlockSpec((tn, tk), lambda i, j, k: (j, k)) if tb else pl.BlockSpec((tk, tn), lambda i, j, k: (k, j))
    o_spec = pl.BlockSpec((tm, tn), lambda i, j, k: (i, j))
    in_specs = [a_spec, b_spec] + ([o_spec] if has_add else [])
    args = (a, b) + ((add,) if has_add else ())
    return pl.pallas_call(
        body, name=name, grid=(m_dim // tm, n_dim // tn, nk),
        in_specs=in_specs, out_specs=o_spec,
        out_shape=jax.ShapeDtypeStruct((m_dim, n_dim), out_dtype),
        scratch_shapes=[pltpu.VMEM((tm, tn), F32)] if nk > 1 else [],
        compiler_params=_params(("parallel", "parallel", "arbitrary")),
    )(*args)


def _mm_core(name, a, b, a_spec, b_spec, o_spec, out_shape, grid, dims, *, add=None, out_dtype=F32):
    nk = grid[2]
    has_add = add is not None
    acc_shape = tuple(d for d in o_spec.block_shape if d is not None)

    def body(*refs):
        if has_add:
            a_ref, b_ref, add_ref, o_ref = refs[:4]
        else:
            a_ref, b_ref, o_ref = refs[:3]
        part = lax.dot_general(a_ref[...].astype(BF16), b_ref[...].astype(BF16), dims, preferred_element_type=F32)

        def finish(r):
            if has_add:
                r = add_ref[...] + r
            o_ref[...] = r.astype(out_dtype)

        if nk == 1:
            finish(part)
            return
        acc_ref = refs[-1]
        k = pl.program_id(2)

        @pl.when(k == 0)
        def _():
            acc_ref[...] = part

        @pl.when(k > 0)
        def _():
            acc_ref[...] += part

        @pl.when(k == nk - 1)
        def _():
            finish(acc_ref[...])

    in_specs = [a_spec, b_spec] + ([o_spec] if has_add else [])
    args = (a, b) + ((add,) if has_add else ())
    return pl.pallas_call(
        body, name=name, grid=grid, in_specs=in_specs, out_specs=o_spec,
        out_shape=jax.ShapeDtypeStruct(out_shape, out_dtype),
        scratch_shapes=[pltpu.VMEM(acc_shape, F32)] if nk > 1 else [],
        compiler_params=_params(("parallel", "parallel", "arbitrary")),
    )(*args)


def _mm_cs(a, w3, *, name, add=None, out_dtype=F32):
    m_dim, k_dim = a.shape
    _, _, n4 = w3.shape
    tm, tn, tk = _tile(m_dim, MM_CAP), _tile(n4, MM_CAP), _tile(k_dim, MM_CAP)
    npb = n4 // tn
    return _mm_core(name, a, w3,
                    pl.BlockSpec((tm, tk), lambda i, j, k: (i, k)),
                    pl.BlockSpec((None, tk, tn), lambda i, j, k: (j // npb, k, j % npb)),
                    pl.BlockSpec((tm, tn), lambda i, j, k: (i, j)),
                    (m_dim, N_CHIPS * n4), (m_dim // tm, N_CHIPS * npb, k_dim // tk), NN, add=add, out_dtype=out_dtype)


def _mm_cs_dx(dy, w3, *, name, out_dtype=F32):
    m_dim, _ = dy.shape
    _, k_dim, n4 = w3.shape
    tm, tkw, tn = _tile(m_dim, MM_CAP), _tile(k_dim, MM_CAP), _tile(n4, MM_CAP)
    npb = n4 // tn
    return _mm_core(name, dy, w3,
                    pl.BlockSpec((tm, tn), lambda i, j, k: (i, k)),
                    pl.BlockSpec((None, tkw, tn), lambda i, j, k: (k // npb, j, k % npb)),
                    pl.BlockSpec((tm, tkw), lambda i, j, k: (i, j)),
                    (m_dim, k_dim), (m_dim // tm, k_dim // tkw, N_CHIPS * npb), NT, out_dtype=out_dtype)


def _mm_cs_dw(a, dy, *, name):
    m_dim, k_dim = a.shape
    n4 = dy.shape[1] // N_CHIPS
    tmk, tn, tk = _tile(k_dim, MM_CAP), _tile(n4, MM_CAP), _tile(m_dim, MM_CAP)
    npb = n4 // tn
    return _mm_core(name, a, dy,
                    pl.BlockSpec((tk, tmk), lambda i, j, k: (k, i)),
                    pl.BlockSpec((tk, tn), lambda i, j, k: (k, j)),
                    pl.BlockSpec((None, tmk, tn), lambda i, j, k: (j // npb, i, j % npb)),
                    (N_CHIPS, k_dim, n4), (k_dim // tmk, N_CHIPS * npb, m_dim // tk), TN)


def _mm_ffn_up(n, w3, *, name):
    t_dim, d = n.shape
    _, _, f4 = w3.shape
    tm = _tile(t_dim, MM_CAP)
    return _mm_core(name, n, w3,
                    pl.BlockSpec((tm, d), lambda i, j, k: (i, 0)),
                    pl.BlockSpec((None, d, f4), lambda i, j, k: (j, 0, 0)),
                    pl.BlockSpec((None, tm, f4), lambda i, j, k: (j, i, 0)),
                    (N_CHIPS, t_dim, f4), (t_dim // tm, N_CHIPS, 1), NN)


def _mm_ffn_up_dw(n, d3, *, name):
    t_dim, d = n.shape
    _, _, f4 = d3.shape
    tk = _tile(t_dim, MM_CAP)
    return _mm_core(name, n, d3,
                    pl.BlockSpec((tk, d), lambda i, j, k: (k, 0)),
                    pl.BlockSpec((None, tk, f4), lambda i, j, k: (j, k, 0)),
                    pl.BlockSpec((None, d, f4), lambda i, j, k: (j, 0, 0)),
                    (N_CHIPS, d, f4), (1, N_CHIPS, t_dim // tk), TN)


def _mm_ffn_up_dx(d3, w3, *, name, add=None):
    _, t_dim, f4 = d3.shape
    _, d, _ = w3.shape
    tm = _tile(t_dim, MM_CAP)
    return _mm_core(name, d3, w3,
                    pl.BlockSpec((None, tm, f4), lambda i, j, k: (k, i, 0)),
                    pl.BlockSpec((None, d, f4), lambda i, j, k: (k, 0, 0)),
                    pl.BlockSpec((tm, d), lambda i, j, k: (i, 0)),
                    (t_dim, d), (t_dim // tm, 1, N_CHIPS), NT, add=add)


def _mm_ffn_down(act3, wd3, *, name, add):
    _, t_dim, f4 = act3.shape
    _, _, d = wd3.shape
    tm = _tile(t_dim, MM_CAP)
    return _mm_core(name, act3, wd3,
                    pl.BlockSpec((None, tm, f4), lambda i, j, k: (k, i, 0)),
                    pl.BlockSpec((None, f4, d), lambda i, j, k: (k, 0, 0)),
                    pl.BlockSpec((tm, d), lambda i, j, k: (i, 0)),
                    (t_dim, d), (t_dim // tm, 1, N_CHIPS), NN, add=add)


def _mm_ffn_down_dx(dh, wd3, *, name):
    t_dim, d = dh.shape
    _, f4, _ = wd3.shape
    tm = _tile(t_dim, MM_CAP)
    return _mm_core(name, dh, wd3,
                    pl.BlockSpec((tm, d), lambda i, j, k: (i, 0)),
                    pl.BlockSpec((None, f4, d), lambda i, j, k: (j, 0, 0)),
                    pl.BlockSpec((None, tm, f4), lambda i, j, k: (j, i, 0)),
                    (N_CHIPS, t_dim, f4), (t_dim // tm, N_CHIPS, 1), NT)


def _mm_ffn_down_dw(act3, dh, *, name):
    _, t_dim, f4 = act3.shape
    d = dh.shape[1]
    tk = _tile(t_dim, MM_CAP)
    return _mm_core(name, act3, dh,
                    pl.BlockSpec((None, tk, f4), lambda i, j, k: (i, k, 0)),
                    pl.BlockSpec((tk, d), lambda i, j, k: (k, 0)),
                    pl.BlockSpec((None, f4, d), lambda i, j, k: (i, 0, 0)),
                    (N_CHIPS, f4, d), (N_CHIPS, 1, t_dim // tk), TN)


def _rms_fwd(x, g, *, name, tt=512):
    t_dim, d = x.shape
    tt = _tile(t_dim, tt, 8)

    def body(x_ref, g_ref, o_ref):
        xv = x_ref[...]
        r = lax.rsqrt(jnp.mean(xv * xv, axis=-1, keepdims=True) + RMS_EPS)
        o_ref[...] = ((xv * r) * g_ref[...]).astype(o_ref.dtype)

    return pl.pallas_call(
        body, name=name, grid=(t_dim // tt,),
        in_specs=[pl.BlockSpec((tt, d), lambda i: (i, 0)), pl.BlockSpec((1, d), lambda i: (0, 0))],
        out_specs=pl.BlockSpec((tt, d), lambda i: (i, 0)),
        out_shape=jax.ShapeDtypeStruct((t_dim, d), BF16),
        compiler_params=_params(("parallel",)),
    )(x, g)


def _rms_bwd(x, g, dy, add, *, name, tt=512):
    t_dim, d = x.shape
    tt = _tile(t_dim, tt, 8)
    has_add = add is not None

    def body(*refs):
        if has_add:
            x_ref, g_ref, dy_ref, add_ref, dx_ref, dg_ref = refs
        else:
            x_ref, g_ref, dy_ref, dx_ref, dg_ref = refs
        xv = x_ref[...]
        dyv = dy_ref[...].astype(F32)
        r = lax.rsqrt(jnp.mean(xv * xv, axis=-1, keepdims=True) + RMS_EPS)
        xh = xv * r
        u = dyv * g_ref[...]
        dx = r * (u - xh * jnp.mean(u * xh, axis=-1, keepdims=True))
        if has_add:
            dx = add_ref[...] + dx
        dx_ref[...] = dx

        @pl.when(pl.program_id(0) == 0)
        def _():
            dg_ref[...] = jnp.zeros_like(dg_ref)

        dg_ref[...] += jnp.sum(dyv * xh, axis=0, keepdims=True)

    row = pl.BlockSpec((tt, d), lambda i: (i, 0))
    vec = pl.BlockSpec((1, d), lambda i: (0, 0))
    in_specs = [row, vec, row] + ([row] if has_add else [])
    args = (x, g, dy) + ((add,) if has_add else ())
    return pl.pallas_call(
        body, name=name, grid=(t_dim // tt,),
        in_specs=in_specs, out_specs=[row, vec],
        out_shape=[jax.ShapeDtypeStruct((t_dim, d), F32), jax.ShapeDtypeStruct((1, d), F32)],
        compiler_params=_params(("arbitrary",)),
    )(*args)


def _final(h, g, target, *, tt=512):
    t_dim, d = h.shape
    n_steps = t_dim // tt

    def body(h_ref, g_ref, t_ref, loss_ref, dh_ref, dg_ref, sq_ref):
        i = pl.program_id(0)
        xv = h_ref[...]
        gv = g_ref[...]
        r = lax.rsqrt(jnp.mean(xv * xv, axis=-1, keepdims=True) + RMS_EPS)
        xh = xv * r
        err = xh * gv - t_ref[...]
        dyv = err * (1.0 / d)
        u = dyv * gv
        dh_ref[...] = r * (u - xh * jnp.mean(u * xh, axis=-1, keepdims=True))

        @pl.when(i == 0)
        def _():
            dg_ref[...] = jnp.zeros_like(dg_ref)
            sq_ref[...] = jnp.zeros_like(sq_ref)

        dg_ref[...] += jnp.sum(dyv * xh, axis=0, keepdims=True)
        sq_ref[...] += jnp.sum(err * err, axis=0, keepdims=True)

        @pl.when(i == n_steps - 1)
        def _():
            total = jnp.sum(sq_ref[...], axis=-1, keepdims=True) * (0.5 / d)
            loss_ref[...] = jnp.broadcast_to(total, loss_ref.shape)

    row = pl.BlockSpec((tt, d), lambda i: (i, 0))
    vec = pl.BlockSpec((1, d), lambda i: (0, 0))
    return pl.pallas_call(
        body, name="final_loss", grid=(n_steps,),
        in_specs=[row, vec, row],
        out_specs=[pl.BlockSpec((1, LANES), lambda i: (0, 0)), row, vec],
        out_shape=[jax.ShapeDtypeStruct((1, LANES), F32), jax.ShapeDtypeStruct((t_dim, d), F32),
                   jax.ShapeDtypeStruct((1, d), F32)],
        scratch_shapes=[pltpu.VMEM((1, d), F32)],
        compiler_params=_params(("arbitrary",)),
    )(h, g, target)


def _rope_table(pos, inv_lane, sel_a, sel_b, *, tt=512):
    t_dim = pos.shape[0]

    def body(p_ref, f_ref, a_ref, b_ref, c_ref, s_ref):
        ang = p_ref[...] * f_ref[...]
        on = (a_ref[...] + b_ref[...]) > 0.0
        c_ref[...] = jnp.where(on, jnp.cos(ang), 1.0)
        s_ref[...] = jnp.where(on, jnp.sin(ang), 0.0)

    vec = pl.BlockSpec((1, LANES), lambda i: (0, 0))
    row = pl.BlockSpec((tt, LANES), lambda i: (i, 0))
    shp = jax.ShapeDtypeStruct((t_dim, LANES), F32)
    return pl.pallas_call(
        body, name="rope_table", grid=(t_dim // tt,),
        in_specs=[pl.BlockSpec((tt, 1), lambda i: (i, 0)), vec, vec, vec],
        out_specs=[row, row], out_shape=[shp, shp],
        compiler_params=_params(("parallel",)),
    )(pos, inv_lane, sel_a, sel_b)


def _rotate(xv, cs, sn, sa, sb):
    half = ROPE_DIM // 2
    up = pltpu.roll(xv, LANES - half, 1)
    dn = pltpu.roll(xv, half, 1)
    return xv * cs + (dn * sb - up * sa) * sn


def _head_masks():
    h1 = lax.broadcasted_iota(jnp.int32, (1, LANES), 1) < HEAD_DIM
    return h1, jnp.logical_not(h1)


def _split_heads(xv, h1, h2):
    return jnp.where(h1, xv, 0.0).astype(BF16), jnp.where(h2, xv, 0.0).astype(BF16)


def _tri_masks():
    r = lax.broadcasted_iota(jnp.int32, (BLOCK, BLOCK), 0)
    c = lax.broadcasted_iota(jnp.int32, (BLOCK, BLOCK), 1)
    return c <= r, r <= c


def _stream_rows(start, dil):
    if dil == 1:
        return pl.ds(pl.multiple_of(start, BLOCK), BLOCK)
    return pl.ds(start, BLOCK, stride=dil)


def _dil_tile(idx, dil, nb):
    r = idx // nb
    n = idx % nb
    return (_stream_rows(r + dil * BLOCK * n, dil), _stream_rows(r + dil * BLOCK * jnp.maximum(n - 1, 0), dil),
            n > 0)


def _dil_specs(b_dim, s_dim):
    def col(c0):
        return pl.BlockSpec((None, s_dim, LANES),lambda b, h: (b, 0, c0 + h))
    tab = pl.BlockSpec((None, s_dim, LANES),lambda b, h: (b, 0, 0))
    vec = pl.BlockSpec((1, LANES), lambda b, h: (0, 0))
    return col, tab, vec


def _dil_fwd(proj3, cs3, sn3, sel_a, sel_b, shards=()):
    b_dim, s_dim, _ = proj3.shape
    scale = HEAD_DIM ** -0.5
    n_pat = len(DIL_PATTERNS)
    n_w = len(shards)
    n_steps = b_dim * PAIRS

    def body(*refs):
        q_ref, k_ref, v_ref, cs_ref, sn_ref, sa_ref, sb_ref = refs[:7]
        o16_ref, o32_ref, l_ref = refs[7 + n_w:10 + n_w]
        qr, kr = refs[10 + 2 * n_w:12 + 2 * n_w]
        per_pattern = refs[12 + 2 * n_w:12 + 2 * n_w + 2 * n_pat]
        og, lg = per_pattern[:n_pat], per_pattern[n_pat:]
        step = pl.program_id(0) * PAIRS + pl.program_id(1)
        if n_w:
            g_start, g_forward, g_finish = _gather_steps(refs[7:7 + n_w], refs[10 + n_w:10 + 2 * n_w], *refs[-2:])
            pl.when(step == 0)(g_start)
        h1, h2 = _head_masks()
        cur_ok, prev_ok = _tri_masks()
        sa, sb = sa_ref[...], sb_ref[...]

        def prep(j, _):
            rows = pl.ds(pl.multiple_of(j * BLOCK, BLOCK), BLOCK)
            cs, sn = cs_ref[rows, :], sn_ref[rows, :]
            qr[rows, :] = _rotate(q_ref[rows, :], cs, sn, sa, sb) * scale
            kr[rows, :] = _rotate(k_ref[rows, :], cs, sn, sa, sb)
            return 0

        lax.fori_loop(0, s_dim // BLOCK, prep, 0)

        for g, (_, dil) in enumerate(DIL_PATTERNS):
            nb = s_dim // dil // BLOCK

            def some(bi, _, g=g, dil=dil, nb=nb):
                tiles = [_dil_tile(bi * DIL_BATCH + t, dil, nb) for t in range(DIL_BATCH)]
                rows = [t[0] for t in tiles]
                q1, q2 = _split_heads(jnp.stack([qr[rw, :] for rw in rows]), h1, h2)
                kc = jnp.stack([kr[rw, :] for rw in rows]).astype(BF16)
                vc1, vc2 = _split_heads(jnp.stack([v_ref[rw, :] for rw in rows]), h1, h2)
                if nb > 1:
                    kp = jnp.stack([kr[t[1], :] for t in tiles]).astype(BF16)
                    vp1, vp2 = _split_heads(jnp.stack([v_ref[t[1], :] for t in tiles]), h1, h2)
                    p_ok = jnp.stack([jnp.logical_and(prev_ok, t[2]) for t in tiles])

                def head(qh, vch, vph):
                    sc = jnp.where(cur_ok, lax.dot_general(qh, kc, BNT, preferred_element_type=F32), -jnp.inf)
                    m = jnp.max(sc, axis=-1, keepdims=True)
                    if nb > 1:
                        sp = jnp.where(p_ok, lax.dot_general(qh, kp, BNT, preferred_element_type=F32), -jnp.inf)
                        m = jnp.maximum(m, jnp.max(sp, axis=-1, keepdims=True))
                    pc = jnp.exp(sc - m)
                    den = jnp.sum(pc, axis=-1, keepdims=True)
                    acc = lax.dot_general(pc.astype(BF16), vch, BNN, preferred_element_type=F32)
                    if nb > 1:
                        pp = jnp.exp(sp - m)
                        den = den + jnp.sum(pp, axis=-1, keepdims=True)
                        acc = acc + lax.dot_general(pp.astype(BF16), vph, BNN, preferred_element_type=F32)
                    return acc / den, m + jnp.log(den)

                o1, l1 = head(q1, vc1, vp1 if nb > 1 else None)
                o2, l2 = head(q2, vc2, vp2 if nb > 1 else None)
                o, l = o1 + o2, jnp.where(h1, l1, l2)
                for t, rw in enumerate(rows):
                    og[g][rw, :] = o[t]
                    lg[g][rw, :] = l[t]
                return 0

            lax.fori_loop(0, dil * nb // DIL_BATCH, some, 0)

        def comb(j, _):
            rows = pl.ds(pl.multiple_of(j * BLOCK, BLOCK), BLOCK)
            ls = [lg[g][rows, :] for g in range(n_pat)]
            m = jnp.maximum(jnp.maximum(ls[0], ls[1]), ls[2])
            es = [jnp.exp(l - m) for l in ls]
            den = es[0] + es[1] + es[2]
            o = (es[0] * og[0][rows, :] + es[1] * og[1][rows, :] + es[2] * og[2][rows, :]) / den
            o16_ref[rows, :] = o.astype(BF16)
            o32_ref[rows, :] = o
            l_ref[rows, :] = m + jnp.log(den)
            return 0

        lax.fori_loop(0, s_dim // BLOCK, comb, 0)
        if n_w:
            pl.when(step == n_steps - 2)(g_forward)
            pl.when(step == n_steps - 1)(g_finish)

    col, tab, vec = _dil_specs(b_dim, s_dim)
    out = pl.BlockSpec((None, s_dim, LANES),lambda b, h: (b, 0, h))
    shp = (b_dim, s_dim, ATT_WIDTH)
    res = pl.pallas_call(
        body, name="dil_fwd", grid=(b_dim, PAIRS),
        in_specs=[col(COL_QA), col(COL_KA), col(COL_VA), tab, tab, vec, vec] + [ANY] * n_w,
        out_specs=[out, out, out] + [ANY] * n_w,
        out_shape=[jax.ShapeDtypeStruct(shp, BF16), jax.ShapeDtypeStruct(shp, F32), jax.ShapeDtypeStruct(shp, F32)]
        + _gathered_shapes(shards),
        scratch_shapes=[pltpu.VMEM((s_dim, LANES), F32)] * (2 + 2 * n_pat) + (_gather_sems(n_w) if n_w else []),
        compiler_params=_params(("arbitrary", "arbitrary")),
    )(proj3, proj3, proj3, cs3, sn3, sel_a, sel_b, *shards)
    return res[:3], res[3:]


def _dil_bwd(proj3, cs3, sn3, sel_a, sel_b, do3, o3, lse3, pairs=()):
    b_dim, s_dim, _ = proj3.shape
    scale = HEAD_DIM ** -0.5
    n_w = len(pairs)

    def body(*refs):
        q_ref, k_ref, v_ref, cs_ref, sn_ref, sa_ref, sb_ref, do_ref, o_ref, l_ref = refs[:10]
        dq_ref, dk_ref, dv_ref = refs[10 + n_w:13 + n_w]
        qr, kr, dqa, dka, dva = refs[13 + 2 * n_w:18 + 2 * n_w]
        step = pl.program_id(0) * PAIRS + pl.program_id(1)
        if n_w:
            x_start, x_finish = _chip_steps(refs[10:10 + n_w], refs[13 + n_w:13 + 2 * n_w], *refs[-2:])
            pl.when(step == 0)(x_start)
        h1, h2 = _head_masks()
        cur_ok, prev_ok = _tri_masks()
        sa, sb = sa_ref[...], sb_ref[...]

        def prep(j, _):
            rows = pl.ds(pl.multiple_of(j * BLOCK, BLOCK), BLOCK)
            cs, sn = cs_ref[rows, :], sn_ref[rows, :]
            qr[rows, :] = _rotate(q_ref[rows, :], cs, sn, sa, sb) * scale
            kr[rows, :] = _rotate(k_ref[rows, :], cs, sn, sa, sb)
            zero = jnp.zeros((BLOCK, LANES), F32)
            dqa[rows, :] = zero
            dka[rows, :] = zero
            dva[rows, :] = zero
            return 0

        lax.fori_loop(0, s_dim // BLOCK, prep, 0)

        for _, dil in DIL_PATTERNS:
            nb = s_dim // dil // BLOCK

            def some(bi, _, dil=dil, nb=nb):
                tiles = [_dil_tile(bi * DIL_BATCH + t, dil, nb) for t in range(DIL_BATCH)]
                rows = [t[0] for t in tiles]
                q1, q2 = _split_heads(jnp.stack([qr[rw, :] for rw in rows]), h1, h2)
                dof = jnp.stack([do_ref[rw, :] for rw in rows])
                do1, do2 = _split_heads(dof, h1, h2)
                prod = dof * jnp.stack([o_ref[rw, :] for rw in rows])
                delta1 = jnp.sum(jnp.where(h1, prod, 0.0), axis=-1, keepdims=True)
                delta2 = jnp.sum(jnp.where(h2, prod, 0.0), axis=-1, keepdims=True)
                lt = jnp.stack([l_ref[rw, :] for rw in rows])
                lse1 = jnp.max(jnp.where(h1, lt, -jnp.inf), axis=-1, keepdims=True)
                lse2 = jnp.max(jnp.where(h2, lt, -jnp.inf), axis=-1, keepdims=True)

                def side(krows, ok):
                    kf = jnp.stack([kr[kw, :] for kw in krows])
                    k16 = kf.astype(BF16)
                    k1, k2 = _split_heads(kf, h1, h2)
                    v16 = jnp.stack([v_ref[kw, :] for kw in krows]).astype(BF16)

                    def head(qh, doh, lse, delta):
                        sc = lax.dot_general(qh, k16, BNT, preferred_element_type=F32)
                        p = jnp.where(ok, jnp.exp(sc - lse), 0.0)
                        dp = lax.dot_general(doh, v16, BNT, preferred_element_type=F32)
                        return p.astype(BF16), (p * (dp - delta)).astype(BF16)

                    p1, ds1 = head(q1, do1, lse1, delta1)
                    p2, ds2 = head(q2, do2, lse2, delta2)
                    dv = (lax.dot_general(p1, do1, BTN, preferred_element_type=F32)
                          + lax.dot_general(p2, do2, BTN, preferred_element_type=F32))
                    dk = (lax.dot_general(ds1, q1, BTN, preferred_element_type=F32)
                          + lax.dot_general(ds2, q2, BTN, preferred_element_type=F32))
                    for t, kw in enumerate(krows):
                        dva[kw, :] += dv[t]
                        dka[kw, :] += dk[t]
                    return (lax.dot_general(ds1, k1, BNN, preferred_element_type=F32)
                            + lax.dot_general(ds2, k2, BNN, preferred_element_type=F32))

                dq = side(rows, cur_ok)
                if nb > 1:
                    dq = dq + side([t[1] for t in tiles], jnp.stack([jnp.logical_and(prev_ok, t[2]) for t in tiles]))
                for t, rw in enumerate(rows):
                    dqa[rw, :] += dq[t] * scale
                return 0

            lax.fori_loop(0, dil * nb // DIL_BATCH, some, 0)

        def finish(j, _):
            rows = pl.ds(pl.multiple_of(j * BLOCK, BLOCK), BLOCK)
            cs, sn = cs_ref[rows, :], -sn_ref[rows, :]
            dq_ref[rows, :] = _rotate(dqa[rows, :], cs, sn, sa, sb).astype(BF16)
            dk_ref[rows, :] = _rotate(dka[rows, :], cs, sn, sa, sb).astype(BF16)
            dv_ref[rows, :] = dva[rows, :].astype(BF16)
            return 0

        lax.fori_loop(0, s_dim // BLOCK, finish, 0)
        if n_w:
            pl.when(step == b_dim * PAIRS - 1)(x_finish)

    col, tab, vec = _dil_specs(b_dim, s_dim)
    out = pl.BlockSpec((None, s_dim, LANES),lambda b, h: (b, 0, h))
    shp = jax.ShapeDtypeStruct((b_dim, s_dim, ATT_WIDTH), BF16)
    acc = pltpu.VMEM((s_dim, LANES), F32)
    res = pl.pallas_call(
        body, name="dil_bwd", grid=(b_dim, PAIRS),
        in_specs=[col(COL_QA), col(COL_KA), col(COL_VA), tab, tab, vec, vec, out, out, out] + [ANY] * n_w,
        out_specs=[out, out, out] + [ANY] * n_w, out_shape=[shp, shp, shp] + _chip_shapes(pairs),
        scratch_shapes=[acc, acc, acc, acc, acc] + (_exchange_sems(3 * n_w) if n_w else []),
        compiler_params=_params(("arbitrary", "arbitrary")),
    )(proj3, proj3, proj3, cs3, sn3, sel_a, sel_b, do3, o3, lse3, *pairs)
    return res[:3], res[3:]


def _split_dot(x, tri):
    hi = x.astype(BF16)
    lo = (x - hi.astype(F32)).astype(BF16)
    return jnp.dot(hi, tri, preferred_element_type=F32) + jnp.dot(lo, tri, preferred_element_type=F32)


def _log_sigmoid(z):
    return jnp.minimum(z, 0.0) - jnp.log(1.0 + jnp.exp(-jnp.abs(z)))


def _sb_scores(qh, k16, valid):
    z = lax.dot_general(qh, k16, NT, preferred_element_type=F32)
    ls = _log_sigmoid(z)
    l1m = ls - z
    return ls, (l1m if valid is None else jnp.where(valid, l1m, 0.0))


def _sb_consts():
    r = lax.broadcasted_iota(jnp.int32, (BLOCK, BLOCK), 0)
    c = lax.broadcasted_iota(jnp.int32, (BLOCK, BLOCK), 1)
    after = (r > c).astype(BF16)
    before = (r < c).astype(BF16)
    qrow = lax.broadcasted_iota(jnp.int32, (SB_ROWS, BLOCK), 0)
    kcol = lax.broadcasted_iota(jnp.int32, (SB_ROWS, BLOCK), 1)
    return after, before, qrow, kcol


def _pairs_loop(n_blocks, step, carry):
    return lax.fori_loop(0, n_blocks // 2, lambda i, c: step(2 * i + 1, step(2 * i, c)), carry)


def _sb_fwd(proj3):
    b_dim, s_dim, _ = proj3.shape
    scale = HEAD_DIM ** -0.5
    per = SB_ROWS // BLOCK

    def body(q_ref, k_ref, v_ref, o_ref):
        h1, h2 = _head_masks()
        after, _, qrow, kcol = _sb_consts()

        def qloop(qi, _):
            rows = pl.ds(pl.multiple_of(qi * SB_ROWS, SB_ROWS), SB_ROWS)
            q1, q2 = _split_heads(q_ref[rows, :] * scale, h1, h2)
            first = qi * per

            def block(kb, carry, masked):
                acc, run1, run2 = carry
                krows = pl.ds(pl.multiple_of(kb * BLOCK, BLOCK), BLOCK)
                k16 = k_ref[krows, :].astype(BF16)
                v1, v2 = _split_heads(v_ref[krows, :], h1, h2)
                valid = ((kb - first) * BLOCK + kcol < qrow) if masked else None

                def head(qh, vh, run):
                    ls, l1m = _sb_scores(qh, k16, valid)
                    a = jnp.exp(ls + _split_dot(l1m, after) + run)
                    if masked:
                        a = jnp.where(valid, a, 0.0)
                    return (jnp.dot(a.astype(BF16), vh, preferred_element_type=F32),
                            run + jnp.sum(l1m, axis=-1, keepdims=True))

                o1, run1 = head(q1, v1, run1)
                o2, run2 = head(q2, v2, run2)
                return acc + o1 + o2, run1, run2

            zcol = jnp.zeros((SB_ROWS, 1), F32)
            carry = (jnp.zeros((SB_ROWS, LANES), F32), zcol, zcol)
            carry = _pairs_loop(per, lambda i, c: block(first + per - 1 - i, c, True), carry)
            acc, _, _ = _pairs_loop(first, lambda i, c: block(first - 1 - i, c, False), carry)
            o_ref[rows, :] = acc.astype(BF16)
            return 0

        lax.fori_loop(0, s_dim // SB_ROWS, qloop, 0)

    def col(c0):
        return pl.BlockSpec((None, s_dim, LANES),lambda b, h: (b, 0, c0 + h))

    return pl.pallas_call(
        body, name="sb_fwd", grid=(b_dim, PAIRS),
        in_specs=[col(COL_QB), col(COL_KB), col(COL_VB)], out_specs=col(0),
        out_shape=jax.ShapeDtypeStruct((b_dim, s_dim, ATT_WIDTH), BF16),
        compiler_params=_params(("parallel", "parallel")),
    )(proj3, proj3, proj3)


def _sb_bwd(proj3, do3, grads=()):
    b_dim, s_dim, _ = proj3.shape
    scale = HEAD_DIM ** -0.5
    per = SB_ROWS // BLOCK
    nkb_max = s_dim // BLOCK
    n_w = len(grads)

    def body(*refs):
        q_ref, k_ref, v_ref, do_ref = refs[:4]
        dq_ref, dk_ref, dv_ref = refs[4 + n_w:7 + n_w]
        dka, dva, e_ref, sg_ref = refs[7 + 2 * n_w:11 + 2 * n_w]
        step = pl.program_id(0) * PAIRS + pl.program_id(1)
        if n_w:
            x_start, x_finish = _pair_steps(refs[4:4 + n_w], refs[7 + n_w:7 + 2 * n_w], *refs[-2:])
            pl.when(step == 0)(x_start)
        h1, h2 = _head_masks()
        after, before, qrow, kcol = _sb_consts()
        dka[...] = jnp.zeros_like(dka)
        dva[...] = jnp.zeros_like(dva)

        def qloop(qi, _):
            rows = pl.ds(pl.multiple_of(qi * SB_ROWS, SB_ROWS), SB_ROWS)
            q1, q2 = _split_heads(q_ref[rows, :] * scale, h1, h2)
            do1, do2 = _split_heads(do_ref[rows, :].astype(F32), h1, h2)
            first = qi * per

            def pass1(kb, carry, masked):
                run1, run2 = carry
                krows = pl.ds(pl.multiple_of(kb * BLOCK, BLOCK), BLOCK)
                k16 = k_ref[krows, :].astype(BF16)
                v16 = v_ref[krows, :].astype(BF16)
                valid = ((kb - first) * BLOCK + kcol < qrow) if masked else None

                def head(h, qh, doh, run):
                    ls, l1m = _sb_scores(qh, k16, valid)
                    a = jnp.exp(ls + _split_dot(l1m, after) + run)
                    if masked:
                        a = jnp.where(valid, a, 0.0)
                    da = lax.dot_general(doh, v16, NT, preferred_element_type=F32)
                    e_ref[h, kb] = a * da
                    sg_ref[h, kb] = jnp.exp(ls)
                    return a.astype(BF16), run + jnp.sum(l1m, axis=-1, keepdims=True)

                a1, run1 = head(0, q1, do1, run1)
                a2, run2 = head(1, q2, do2, run2)
                dva[krows, :] += (lax.dot_general(a1, do1, TN, preferred_element_type=F32)
                                  + lax.dot_general(a2, do2, TN, preferred_element_type=F32))
                return run1, run2

            zcol = jnp.zeros((SB_ROWS, 1), F32)
            carry = _pairs_loop(per, lambda i, c: pass1(first + per - 1 - i, c, True), (zcol, zcol))
            _pairs_loop(first, lambda i, c: pass1(first - 1 - i, c, False), carry)

            def pass2(kb, carry, masked):
                dq, pre1, pre2 = carry
                krows = pl.ds(pl.multiple_of(kb * BLOCK, BLOCK), BLOCK)
                k1, k2 = _split_heads(k_ref[krows, :], h1, h2)

                def head(h, pre):
                    ev = e_ref[h, kb]
                    sg = sg_ref[h, kb]
                    dz = ev * (1.0 - sg) - (_split_dot(ev, before) + pre) * sg
                    if masked:
                        dz = jnp.where((kb - first) * BLOCK + kcol < qrow, dz, 0.0)
                    return dz.astype(BF16), pre + jnp.sum(ev, axis=-1, keepdims=True)

                dz1, pre1 = head(0, pre1)
                dz2, pre2 = head(1, pre2)
                dka[krows, :] += (lax.dot_general(dz1, q1, TN, preferred_element_type=F32)
                                  + lax.dot_general(dz2, q2, TN, preferred_element_type=F32))
                dq = dq + jnp.dot(dz1, k1, preferred_element_type=F32) + jnp.dot(dz2, k2, preferred_element_type=F32)
                return dq, pre1, pre2

            carry = _pairs_loop(first, lambda i, c: pass2(i, c, False), (jnp.zeros((SB_ROWS, LANES), F32), zcol, zcol))
            dq, _, _ = _pairs_loop(per, lambda i, c: pass2(first + i, c, True), carry)
            dq_ref[rows, :] = (dq * scale).astype(BF16)
            return 0

        lax.fori_loop(0, s_dim // SB_ROWS, qloop, 0)
        dk_ref[...] = dka[...].astype(BF16)
        dv_ref[...] = dva[...].astype(BF16)
        if n_w:
            pl.when(step == b_dim * PAIRS - 1)(x_finish)

    def col(c0):
        return pl.BlockSpec((None, s_dim, LANES),lambda b, h: (b, 0, c0 + h))

    shp = jax.ShapeDtypeStruct((b_dim, s_dim, ATT_WIDTH), BF16)
    acc = pltpu.VMEM((s_dim, LANES), F32)
    strip = pltpu.VMEM((2, nkb_max, SB_ROWS, BLOCK), F32)
    res = pl.pallas_call(
        body, name="sb_bwd", grid=(b_dim, PAIRS),
        in_specs=[col(COL_QB), col(COL_KB), col(COL_VB), col(0)] + [ANY] * n_w,
        out_specs=[col(0), col(0), col(0)] + [ANY] * n_w,
        out_shape=[shp, shp, shp] + _pair_shapes(grads),
        scratch_shapes=[acc, acc, strip, strip] + (_exchange_sems(n_w) if n_w else []),
        compiler_params=_params(("arbitrary", "arbitrary")),
    )(proj3, proj3, proj3, do3, *grads)
    return res[:3], res[3:]


def _sigmoid(x):
    return 1.0 / (1.0 + jnp.exp(-x))


def _gate_fwd(proj, ua, ub, *, tt=512):
    t_dim, d = ua.shape

    def body(ga_ref, gb_ref, ua_ref, ub_ref, o_ref):
        o_ref[...] = (_sigmoid(ga_ref[...]) * ua_ref[...] + _sigmoid(gb_ref[...]) * ub_ref[...]).astype(BF16)

    row = pl.BlockSpec((tt, d), lambda i: (i, 0))
    return pl.pallas_call(
        body, name="gate_fwd", grid=(t_dim // tt,),
        in_specs=[pl.BlockSpec((tt, d), lambda i: (i, 3)), pl.BlockSpec((tt, d), lambda i: (i, 4)), row, row],
        out_specs=row, out_shape=jax.ShapeDtypeStruct((t_dim, d), BF16),
        compiler_params=_params(("parallel",)),
    )(proj, proj, ua, ub)


def _gate_bwd(proj, ua, ub, dmix, *, tt=512):
    t_dim, d = ua.shape

    def body(ga_ref, gb_ref, ua_ref, ub_ref, dm_ref, dua_ref, dub_ref, dg_ref):
        dm = dm_ref[...]
        sa = _sigmoid(ga_ref[...])
        sb = _sigmoid(gb_ref[...])
        dua_ref[...] = (dm * sa).astype(BF16)
        dub_ref[...] = (dm * sb).astype(BF16)
        dg_ref[:, :d] = (dm * ua_ref[...] * (sa * (1.0 - sa))).astype(BF16)
        dg_ref[:, d:] = (dm * ub_ref[...] * (sb * (1.0 - sb))).astype(BF16)

    row = pl.BlockSpec((tt, d), lambda i: (i, 0))
    wide = pl.BlockSpec((tt, 2 * d), lambda i: (i, 0))
    return pl.pallas_call(
        body, name="gate_bwd", grid=(t_dim // tt,),
        in_specs=[pl.BlockSpec((tt, d), lambda i: (i, 3)), pl.BlockSpec((tt, d), lambda i: (i, 4)), row, row, row],
        out_specs=[row, row, wide],
        out_shape=[jax.ShapeDtypeStruct((t_dim, d), BF16), jax.ShapeDtypeStruct((t_dim, d), BF16),
                   jax.ShapeDtypeStruct((t_dim, 2 * d), BF16)],
        compiler_params=_params(("parallel",)),
    )(proj, proj, ua, ub, dmix)


def _swiglu_fwd(g3, u3, *, tt=1024):
    n, t_dim, f4 = g3.shape

    def body(g_ref, u_ref, o_ref):
        gv = g_ref[...]
        o_ref[...] = (gv * _sigmoid(gv) * u_ref[...]).astype(BF16)

    spec = pl.BlockSpec((None, tt, f4), lambda s, i: (s, i, 0))
    return pl.pallas_call(
        body, name="swiglu_fwd", grid=(n, t_dim // tt), in_specs=[spec, spec], out_specs=spec,
        out_shape=jax.ShapeDtypeStruct(g3.shape, BF16),
        compiler_params=_params(("parallel", "parallel")),
    )(g3, u3)


def _swiglu_bwd(g3, u3, dact3, *, tt=1024):
    n, t_dim, f4 = g3.shape

    def body(g_ref, u_ref, da_ref, dg_ref, du_ref):
        gv = g_ref[...]
        da = da_ref[...]
        sg = _sigmoid(gv)
        dg_ref[...] = (da * u_ref[...] * (sg + gv * sg * (1.0 - sg))).astype(BF16)
        du_ref[...] = (da * (gv * sg)).astype(BF16)

    spec = pl.BlockSpec((None, tt, f4), lambda s, i: (s, i, 0))
    shp = jax.ShapeDtypeStruct(g3.shape, BF16)
    return pl.pallas_call(
        body, name="swiglu_bwd", grid=(n, t_dim // tt), in_specs=[spec, spec, spec], out_specs=[spec, spec],
        out_shape=[shp, shp],
        compiler_params=_params(("parallel", "parallel")),
    )(g3, u3, dact3)


def _mem_fwd(qm, kvm, *, tt=512):
    b_dim, s_dim, _ = qm.shape
    n_mem = kvm.shape[1]
    scale = MEM_HEAD_DIM ** -0.5

    def body(q_ref, k_ref, v_ref, o_ref):
        sc = lax.dot_general(q_ref[0], k_ref[0], NT, preferred_element_type=F32) * scale
        p = jnp.exp(sc - jnp.max(sc, axis=-1, keepdims=True))
        p = p / jnp.sum(p, axis=-1, keepdims=True)
        o_ref[0] = jnp.dot(p.astype(BF16), v_ref[0], preferred_element_type=F32).astype(BF16)

    qs = pl.BlockSpec((1, tt, MEM_HEAD_DIM), lambda b, h, i: (b, i, h))
    return pl.pallas_call(
        body, name="mem_fwd", grid=(b_dim, N_HEADS_MEM, s_dim // tt),
        in_specs=[qs, pl.BlockSpec((1, n_mem, MEM_HEAD_DIM), lambda b, h, i: (b, 0, h)),
                  pl.BlockSpec((1, n_mem, MEM_HEAD_DIM), lambda b, h, i: (b, 0, N_HEADS_MEM + h))],
        out_specs=qs, out_shape=jax.ShapeDtypeStruct(qm.shape, BF16),
        compiler_params=_params(("parallel", "parallel", "parallel")),
    )(qm, kvm, kvm)


def _mem_bwd(qm, kvm, dom, *, tt=512):
    b_dim, s_dim, _ = qm.shape
    n_mem = kvm.shape[1]
    scale = MEM_HEAD_DIM ** -0.5

    def body(q_ref, k_ref, v_ref, do_ref, dq_ref, dk_ref, dv_ref):
        qv, kv, vv, dov = q_ref[0], k_ref[0], v_ref[0], do_ref[0]
        sc = lax.dot_general(qv, kv, NT, preferred_element_type=F32) * scale
        p = jnp.exp(sc - jnp.max(sc, axis=-1, keepdims=True))
        p = p / jnp.sum(p, axis=-1, keepdims=True)
        dp = lax.dot_general(dov, vv, NT, preferred_element_type=F32)
        ds = (p * (dp - jnp.sum(p * dp, axis=-1, keepdims=True)) * scale).astype(BF16)
        dq_ref[0] = jnp.dot(ds, kv, preferred_element_type=F32).astype(BF16)

        @pl.when(pl.program_id(2) == 0)
        def _():
            dk_ref[...] = jnp.zeros_like(dk_ref)
            dv_ref[...] = jnp.zeros_like(dv_ref)

        dk_ref[0] += lax.dot_general(ds, qv, TN, preferred_element_type=F32)
        dv_ref[0] += lax.dot_general(p.astype(BF16), dov, TN, preferred_element_type=F32)

    qs = pl.BlockSpec((1, tt, MEM_HEAD_DIM), lambda b, h, i: (b, i, h))
    ks = pl.BlockSpec((1, n_mem, MEM_HEAD_DIM), lambda b, h, i: (b, 0, h))
    vs = pl.BlockSpec((1, n_mem, MEM_HEAD_DIM), lambda b, h, i: (b, 0, N_HEADS_MEM + h))
    return pl.pallas_call(
        body, name="mem_bwd", grid=(b_dim, N_HEADS_MEM, s_dim // tt),
        in_specs=[qs, ks, vs, qs], out_specs=[qs, ks, ks],
        out_shape=[jax.ShapeDtypeStruct(qm.shape, BF16), jax.ShapeDtypeStruct((b_dim, n_mem, MEM_WIDTH), F32),
                   jax.ShapeDtypeStruct((b_dim, n_mem, MEM_WIDTH), F32)],
        compiler_params=_params(("parallel", "parallel", "arbitrary")),
    )(qm, kvm, kvm, dom)


def _adamw_math(wv, gv, mv, vv):
    nm = ADAM_B1 * mv + (1.0 - ADAM_B1) * gv
    nv = ADAM_B2 * vv + (1.0 - ADAM_B2) * (gv * gv)
    m_hat = nm / (1.0 - ADAM_B1 ** ADAM_STEP)
    v_hat = nv / (1.0 - ADAM_B2 ** ADAM_STEP)
    return -ADAM_LR * (m_hat / (jnp.sqrt(v_hat) + ADAM_EPS) + ADAM_WD * wv), nm, nv


def _adamw(w, g, m, v, *, name):
    rows, cols = w.shape
    tr = _tile(rows, 256, 8)

    def body(w_ref, g_ref, m_ref, v_ref, d_ref, nm_ref, nv_ref):
        d_ref[...], nm_ref[...], nv_ref[...] = _adamw_math(w_ref[...], g_ref[...], m_ref[...], v_ref[...])

    spec = pl.BlockSpec((tr, cols), lambda i: (i, 0))
    shp = jax.ShapeDtypeStruct((rows, cols), F32)
    return pl.pallas_call(
        body, name=name, grid=(rows // tr,),
        in_specs=[spec] * 4, out_specs=[spec] * 3, out_shape=[shp] * 3,
        compiler_params=_params(("parallel",)),
    )(w, g, m, v)


def _prefetch_spec(grid, in_specs, out_specs):
    return pltpu.PrefetchScalarGridSpec(num_scalar_prefetch=1, grid=grid, in_specs=in_specs, out_specs=out_specs)


def _adamw_halves(w, mine, theirs, m, v, c_idx, *, name):
    rows, cols = w.shape
    half = rows // 2
    tr = _tile(half, 256, 8)
    nh = half // tr

    def body(c_ref, w_ref, mine_ref, theirs_ref, m_ref, v_ref, g_ref, d_ref, nm_ref, nv_ref):
        gv = jnp.where(pl.program_id(0) == c_ref[0], mine_ref[...], theirs_ref[...])
        g_ref[...] = gv
        d_ref[...], nm_ref[...], nv_ref[...] = _adamw_math(w_ref[...], gv, m_ref[...], v_ref[...])

    full = pl.BlockSpec((tr, cols), lambda h, i, c_ref: (h * nh + i, 0))
    part = pl.BlockSpec((tr, cols), lambda h, i, c_ref: (i, 0))
    shp = jax.ShapeDtypeStruct((rows, cols), F32)
    return pl.pallas_call(
        body, name=name, grid_spec=_prefetch_spec((2, nh), [full, part, part, full, full], [full] * 4),
        out_shape=[shp] * 4,
        compiler_params=_params(("parallel", "parallel")),
    )(c_idx, w, mine, theirs, m, v)


def _pair_sum(g3, theirs, c_idx, *, name):
    n, rows, cols = g3.shape
    half = rows // 2
    tr = _tile(half, 256, 16)

    def body(c_ref, g_ref, t_ref, o_ref):
        o_ref[...] = (g_ref[...] + t_ref[...]).astype(BF16)

    part = pl.BlockSpec((None, tr, cols), lambda s, i, c_ref: (s, i, 0))
    return pl.pallas_call(
        body, name=name,
        grid_spec=_prefetch_spec((n, half // tr),
                                 [pl.BlockSpec((None, None, tr, cols), lambda s, i, c_ref: (s, c_ref[0], i, 0)), part],
                                 part),
        out_shape=jax.ShapeDtypeStruct((n, half, cols), BF16),
        compiler_params=_params(("parallel", "parallel")),
    )(c_idx, g3.reshape(n, 2, half, cols), theirs)


def _chip_sum(pair, recv, s_idx, *, name):
    _, half, cols = pair.shape
    tr = _tile(half, 256, 16)

    def body(s_ref, p_ref, r_ref, o_ref):
        o_ref[...] = ((p_ref[...].astype(F32) + r_ref[0].astype(F32)) + r_ref[1].astype(F32)) + r_ref[2].astype(F32)

    return pl.pallas_call(
        body, name=name,
        grid_spec=_prefetch_spec((half // tr,),
                                 [pl.BlockSpec((None, tr, cols), lambda i, s_ref: (s_ref[0], i, 0)),
                                  pl.BlockSpec((N_CHIPS - 1, tr, cols), lambda i, s_ref: (0, i, 0))],
                                 pl.BlockSpec((tr, cols), lambda i, s_ref: (i, 0))),
        out_shape=jax.ShapeDtypeStruct((half, cols), F32),
        compiler_params=_params(("parallel",)),
    )(s_idx, pair, recv)


def _sum8(parts):
    n, rows, cols = parts.shape

    def body(p_ref, o_ref):
        acc = p_ref[0]
        for i in range(1, n):
            acc = acc + p_ref[i]
        o_ref[...] = acc

    return pl.pallas_call(
        body, name="small_sum", grid=(1,),
        in_specs=[pl.BlockSpec((n, rows, cols), lambda i: (0, 0, 0))],
        out_specs=pl.BlockSpec((rows, cols), lambda i: (0, 0)),
        out_shape=jax.ShapeDtypeStruct((rows, cols), parts.dtype),
        compiler_params=_params(("arbitrary",)),
    )(parts)


def _place():
    return lax.axis_index("x"), lax.axis_index("y"), lax.axis_index("c")


ANY = pl.BlockSpec(memory_space=pl.ANY)


def _gather_weights(shards):
    n = len(shards)

    def body(*refs):
        start, forward, finish = _gather_steps(refs[:n], refs[n:2 * n], refs[2 * n], refs[2 * n + 1])
        start()
        forward()
        finish()

    return pl.pallas_call(
        body, name="gather_weights", out_shape=_gathered_shapes(shards),
        in_specs=[ANY] * n, out_specs=[ANY] * n, scratch_shapes=_gather_sems(n),
    )(*shards)


def _gathered_shapes(shards):
    return [jax.ShapeDtypeStruct((N_CHIPS,) + s.shape, s.dtype) for s in shards]


def _gather_sems(n):
    return [pltpu.SemaphoreType.DMA((7 * n,)), pltpu.SemaphoreType.DMA((7 * n,))]


def _gather_steps(ins, outs, send_sems, recv_sems):
    n = len(ins)
    halves = [r.shape[0] // 2 for r in ins]
    x, y, c = _place()
    my_chip = 2 * x + y
    me, sibling = (x, y, c), (x, y, 1 - c)
    chips = [(1 - x, y), (x, 1 - y), (1 - x, 1 - y)]

    def half_of(w, chip, pc):
        return outs[w].at[chip, pl.ds(pc * halves[w], halves[w]), :]

    def copy(w, k, src, dst, to):
        return pltpu.make_async_remote_copy(
            src_ref=src, dst_ref=dst, send_sem=send_sems.at[7 * w + k], recv_sem=recv_sems.at[7 * w + k],
            device_id=to, device_id_type=MESH)

    def firsts():
        cps = []
        for w in range(n):
            cps.append(copy(w, 0, ins[w], outs[w].at[my_chip], sibling))
            mine = ins[w].at[pl.ds(c * halves[w], halves[w]), :]
            for j, (px, py) in enumerate(chips):
                cps.append(copy(w, 1 + j, mine, half_of(w, my_chip, c), (px, py, c)))
        return cps

    def passes():
        return [copy(w, 4 + j, half_of(w, 2 * px + py, c), half_of(w, 2 * px + py, c), sibling)
                for w in range(n) for j, (px, py) in enumerate(chips)]

    def start():
        for cp in firsts():
            cp.start()

    def forward():
        fws = passes()
        for w in range(n):
            for j, (px, py) in enumerate(chips):
                landed = half_of(w, 2 * px + py, c)
                copy(w, 1 + j, landed, landed, me).wait_recv()
                fws[3 * w + j].start()

    def finish():
        for w in range(n):
            copy(w, 0, ins[w], outs[w].at[my_chip], me).wait_recv()
            for j, (px, py) in enumerate(chips):
                landed = half_of(w, 2 * px + py, 1 - c)
                copy(w, 4 + j, landed, landed, me).wait_recv()
        for cp in firsts() + passes():
            cp.wait_send()

    return start, forward, finish


def _pair_exchange(grads):
    n = len(grads)

    def body(*refs):
        start, finish = _pair_steps(refs[:n], refs[n:2 * n], refs[2 * n], refs[2 * n + 1])
        start()
        finish()

    return pl.pallas_call(
        body, name="grad_pair_exchange", out_shape=_pair_shapes(grads),
        in_specs=[ANY] * n, out_specs=[ANY] * n, scratch_shapes=_exchange_sems(n),
    )(*grads)


def _pair_shapes(grads):
    return [jax.ShapeDtypeStruct((g.shape[0], g.shape[1] // 2, g.shape[2]), g.dtype) for g in grads]


def _exchange_sems(n):
    return [pltpu.SemaphoreType.DMA((n,)), pltpu.SemaphoreType.DMA((n,))]


def _exchange_steps(copies):
    def start():
        for cp in copies():
            cp.start()

    def finish():
        for cp in copies():
            cp.wait()

    return start, finish


def _pair_steps(ins, outs, send_sems, recv_sems):
    x, y, c = _place()

    def copies():
        return [pltpu.make_async_remote_copy(
            src_ref=ins[w].at[:, pl.ds((1 - c) * (ins[w].shape[1] // 2), ins[w].shape[1] // 2), :], dst_ref=outs[w],
            send_sem=send_sems.at[w], recv_sem=recv_sems.at[w], device_id=(x, y, 1 - c), device_id_type=MESH)
            for w in range(len(ins))]

    return _exchange_steps(copies)


def _chip_exchange(pairs):
    n = len(pairs)

    def body(*refs):
        start, finish = _chip_steps(refs[:n], refs[n:2 * n], refs[2 * n], refs[2 * n + 1])
        start()
        finish()

    return pl.pallas_call(
        body, name="grad_chip_exchange", out_shape=_chip_shapes(pairs),
        in_specs=[ANY] * n, out_specs=[ANY] * n, scratch_shapes=_exchange_sems(3 * n),
    )(*pairs)


def _chip_shapes(pairs):
    return [jax.ShapeDtypeStruct((N_CHIPS - 1,) + p.shape[1:], p.dtype) for p in pairs]


def _chip_steps(ins, outs, send_sems, recv_sems):
    x, y, c = _place()
    others = [(1 - x, y), (x, 1 - y), (1 - x, 1 - y)]

    def copies():
        return [pltpu.make_async_remote_copy(
            src_ref=ins[w].at[2 * px + py], dst_ref=outs[w].at[j],
            send_sem=send_sems.at[3 * w + j], recv_sem=recv_sems.at[3 * w + j],
            device_id=(px, py, c), device_id_type=MESH)
            for w in range(len(ins)) for j, (px, py) in enumerate(others)]

    return _exchange_steps(copies)


def _swap_halves(mine):
    n = len(mine)

    def body(*refs):
        ins, outs, send_sems, recv_sems = refs[:n], refs[n:2 * n], refs[2 * n], refs[2 * n + 1]
        x, y, c = _place()
        copies = [pltpu.make_async_remote_copy(
            src_ref=ins[w], dst_ref=outs[w], send_sem=send_sems.at[w], recv_sem=recv_sems.at[w],
            device_id=(x, y, 1 - c), device_id_type=MESH) for w in range(n)]
        for cp in copies:
            cp.start()
        for cp in copies:
            cp.wait()

    return pl.pallas_call(
        body, name="grad_swap_halves",
        out_shape=[jax.ShapeDtypeStruct(h.shape, h.dtype) for h in mine],
        in_specs=[ANY] * n, out_specs=[ANY] * n,
        scratch_shapes=[pltpu.SemaphoreType.DMA((n,)), pltpu.SemaphoreType.DMA((n,))],
    )(*mine)


def _gather_small(small):
    srows, cols = small.shape

    def body(s_ref, all_ref, send_sems, recv_sems, local_sem):
        x, y, c = _place()
        me = 4 * x + 2 * y + c
        keep_small = pltpu.make_async_copy(s_ref, all_ref.at[me], local_sem)
        keep_small.start()
        sends = []
        for kk in range(1, 8):
            peer = (x ^ (kk >> 2), y ^ ((kk >> 1) & 1), c ^ (kk & 1))
            sends.append(pltpu.make_async_remote_copy(
                src_ref=s_ref, dst_ref=all_ref.at[me],
                send_sem=send_sems.at[kk], recv_sem=recv_sems.at[kk], device_id=peer, device_id_type=MESH))
        for cp in sends:
            cp.start()
        for kk in range(1, 8):
            px, py, pc = x ^ (kk >> 2), y ^ ((kk >> 1) & 1), c ^ (kk & 1)
            pltpu.make_async_remote_copy(
                src_ref=s_ref, dst_ref=all_ref.at[4 * px + 2 * py + pc],
                send_sem=send_sems.at[kk], recv_sem=recv_sems.at[kk], device_id=(px, py, pc),
                device_id_type=MESH).wait_recv()
        for cp in sends:
            cp.wait_send()
        keep_small.wait()

    return pl.pallas_call(
        body, name="gather_small",
        out_shape=jax.ShapeDtypeStruct((8, srows, cols), small.dtype),
        in_specs=[ANY], out_specs=ANY,
        scratch_shapes=[pltpu.SemaphoreType.DMA((8,)), pltpu.SemaphoreType.DMA((8,)), pltpu.SemaphoreType.DMA],
    )(small)


SHARDED = (("w_in", D_MODEL, IN_COLS, 1), ("w_up_a", ATT_WIDTH, D_MODEL, 1), ("w_up_b", ATT_WIDTH, D_MODEL, 1),
           ("w_out", D_MODEL, D_MODEL, 0), ("w_q_mem", D_MODEL, MEM_WIDTH, 0), ("w_kv_mem", D_MODEL, 2 * MEM_WIDTH, 0),
           ("w_o_mem", MEM_WIDTH, D_MODEL, 1), ("w_ffn_gate", D_MODEL, D_FF, 1), ("w_ffn_up", D_MODEL, D_FF, 1),
           ("w_ffn_down", D_FF, D_MODEL, 0))
NAMES = tuple(n for n, _, _, _ in SHARDED)
EARLY, LATE = NAMES[:1], NAMES[1:]
GAINS = ("g_mix", "g_mem_q", "g_mem_kv", "g_ffn", "g_final")


def _natural(w3):
    n, r, c = w3.shape
    return w3.reshape(n * r, c)


def _shard_major(g, axis):
    if axis == 1:
        return g
    r, c = g.shape
    return g.reshape(N_CHIPS, r // N_CHIPS, c)


def kernel(x, mem, positions, g_mix, w_in, w_up_a, w_up_b, w_out, g_mem_q, g_mem_kv, w_q_mem, w_kv_mem, w_o_mem, g_ffn, w_ffn_gate, w_ffn_up, w_ffn_down, g_final, loss_target, m_g_mix, m_w_in, m_w_up_a, m_w_up_b, m_w_out, m_g_mem_q, m_g_mem_kv, m_w_q_mem, m_w_kv_mem, m_w_o_mem, m_g_ffn, m_w_ffn_gate, m_w_ffn_up, m_w_ffn_down, m_g_final, v_g_mix, v_w_in, v_w_up_a, v_w_up_b, v_w_out, v_g_mem_q, v_g_mem_kv, v_w_q_mem, v_w_kv_mem, v_w_o_mem, v_g_ffn, v_w_ffn_gate, v_w_ffn_up, v_w_ffn_down, v_g_final):
    given = dict(locals())
    shards = {n: given[n][0] for n, _, _, _ in SHARDED}

    wf = dict(zip(EARLY, _gather_weights([shards[n].astype(BF16) for n in EARLY])))
    late_shards = [shards[n].astype(BF16) for n in LATE]
    c_idx = lax.axis_index("c").astype(jnp.int32).reshape(1)
    s_idx = (2 * lax.axis_index("x") + lax.axis_index("y")).astype(jnp.int32).reshape(1)

    loss_row, grad_x, mine, gain_grads = _local_step(x, mem, positions, loss_target, g_mix, g_mem_q, g_mem_kv,
                                                     g_ffn, g_final, wf, late_shards, (c_idx, s_idx))
    return _reduce_and_update(given, shards, loss_row, grad_x, mine, gain_grads, c_idx)


def _reduce_halves(glist, names, c_idx, s_idx, pair_exchange, chip_exchange):
    theirs = pair_exchange(glist)
    pairs = [_pair_sum(g, t, c_idx, name="pair_sum_" + n) for n, g, t in zip(names, glist, theirs)]
    recv = chip_exchange(pairs)
    return [_chip_sum(p, r, s_idx, name="chip_sum_" + n) for n, p, r in zip(names, pairs, recv)]


def _local_step(x, mem, positions, loss_target, g_mix, g_mem_q, g_mem_kv, g_ffn, g_final, wf,
                late_shards=None, place=None):
    b_dim, s_dim, d = x.shape
    t_dim = b_dim * s_dim
    n_mem = mem.shape[1]
    wf = dict(wf)

    xb = x.reshape(t_dim, d)
    tgt = loss_target.reshape(t_dim, d)
    memf = mem.reshape(b_dim * n_mem, d)
    gfin = g_final.reshape(1, d)
    pos = positions.reshape(t_dim, 1).astype(F32)

    lane = jnp.arange(LANES) % HEAD_DIM
    half = ROPE_DIM // 2
    inv_freq = ROPE_THETA ** (-jnp.arange(half, dtype=F32) / half)
    inv_lane = jnp.where(lane < ROPE_DIM, inv_freq[lane % half], 0.0).reshape(1, -1).astype(F32)
    sel_a = (lane < half).astype(F32).reshape(1, -1)
    sel_b = ((lane >= half) & (lane < ROPE_DIM)).astype(F32).reshape(1, -1)

    def rows3(t):
        return t.reshape(b_dim, s_dim, t.shape[-1])

    def rows2(t):
        return t.reshape(t_dim, t.shape[-1])

    n1 = _rms_fwd(xb, g_mix, name="rms_mix")
    proj = _mm_cs(n1, wf["w_in"], name="mm_in")
    proj3 = rows3(proj)
    cs, sn = _rope_table(pos, inv_lane, sel_a, sel_b)
    cs3, sn3 = rows3(cs), rows3(sn)
    (oa16, oa32, lse_a), gathered = _dil_fwd(proj3, cs3, sn3, sel_a, sel_b, late_shards or ())
    wf.update(zip(LATE, gathered))
    w_out, w_q, w_kv = _natural(wf["w_out"]), _natural(wf["w_q_mem"]), _natural(wf["w_kv_mem"])
    ob16 = _sb_fwd(proj3)
    oa, ob = rows2(oa16), rows2(ob16)
    ua = _mm_cs(oa, wf["w_up_a"], name="mm_up_a")
    ub = _mm_cs(ob, wf["w_up_b"], name="mm_up_b")
    mixed = _gate_fwd(proj, ua, ub)
    h1 = _mm(mixed, w_out, name="mm_out", add=xb)

    hn = _rms_fwd(h1, g_mem_q, name="rms_mem_q")
    memn = _rms_fwd(memf, g_mem_kv, name="rms_mem_kv")
    qm = _mm(hn, w_q, name="mm_q_mem", out_dtype=BF16)
    kvm = _mm(memn, w_kv, name="mm_kv_mem", out_dtype=BF16)
    qm3, kvm3 = rows3(qm), kvm.reshape(b_dim, n_mem, 2 * MEM_WIDTH)
    om = rows2(_mem_fwd(qm3, kvm3))
    h2 = _mm_cs(om, wf["w_o_mem"], name="mm_o_mem", add=h1)

    n3 = _rms_fwd(h2, g_ffn, name="rms_ffn")
    gate3 = _mm_ffn_up(n3, wf["w_ffn_gate"], name="mm_gate")
    up3 = _mm_ffn_up(n3, wf["w_ffn_up"], name="mm_up")
    act3 = _swiglu_fwd(gate3, up3)
    h3 = _mm_ffn_down(act3, wf["w_ffn_down"], name="mm_down", add=h2)
    loss_row, dh3, dg_final = _final(h3, gfin, tgt)

    grads = {}
    dact3 = _mm_ffn_down_dx(dh3, wf["w_ffn_down"], name="mm_down_dx")
    grads["w_ffn_down"] = _mm_ffn_down_dw(act3, dh3, name="mm_down_dw")
    dgate3, dup3 = _swiglu_bwd(gate3, up3, dact3)
    grads["w_ffn_gate"] = _mm_ffn_up_dw(n3, dgate3, name="mm_gate_dw")
    grads["w_ffn_up"] = _mm_ffn_up_dw(n3, dup3, name="mm_up_dw")
    dn3 = _mm_ffn_up_dx(dgate3, wf["w_ffn_gate"], name="mm_gate_dx")
    dn3 = _mm_ffn_up_dx(dup3, wf["w_ffn_up"], name="mm_up_dx", add=dn3)
    dh2, dg_ffn = _rms_bwd(h2, g_ffn, dn3, dh3, name="rms_ffn_bwd")

    dom = _mm_cs_dx(dh2, wf["w_o_mem"], name="mm_o_mem_dx", out_dtype=BF16)
    grads["w_o_mem"] = _mm_cs_dw(om, dh2, name="mm_o_mem_dw")
    dqm, dkm, dvm = _mem_bwd(qm3, kvm3, rows3(dom))
    dqm = rows2(dqm)
    dkvm = jnp.concatenate([dkm, dvm], axis=-1).reshape(b_dim * n_mem, 2 * MEM_WIDTH).astype(BF16)
    grads["w_q_mem"] = _shard_major(_mm(hn, dqm, name="mm_q_mem_dw", ta=True), 0)
    dhn = _mm(dqm, w_q, name="mm_q_mem_dx", tb=True)
    grads["w_kv_mem"] = _shard_major(_mm(memn, dkvm, name="mm_kv_mem_dw", ta=True), 0)
    dmemn = _mm(dkvm, w_kv, name="mm_kv_mem_dx", tb=True)
    _, dg_mem_kv = _rms_bwd(memf, g_mem_kv, dmemn, None, name="rms_mem_kv_bwd")
    dh1, dg_mem_q = _rms_bwd(h1, g_mem_q, dhn, dh2, name="rms_mem_q_bwd")

    dmix = _mm(dh1, w_out, name="mm_out_dx", tb=True)
    grads["w_out"] = _shard_major(_mm(mixed, dh1, name="mm_out_dw", ta=True), 0)
    dua, dub, dgates = _gate_bwd(proj, ua, ub, dmix)
    doa = _mm_cs_dx(dua, wf["w_up_a"], name="mm_up_a_dx")
    grads["w_up_a"] = _mm_cs_dw(oa, dua, name="mm_up_a_dw")
    dob = _mm_cs_dx(dub, wf["w_up_b"], name="mm_up_b_dx", out_dtype=BF16)
    grads["w_up_b"] = _mm_cs_dw(ob, dub, name="mm_up_b_dw")

    att = {}

    def sb_with_pairs(glist):
        att["b"], theirs = _sb_bwd(proj3, rows3(dob), glist)
        return theirs

    def dil_with_chips(pairs):
        att["a"], recv = _dil_bwd(proj3, cs3, sn3, sel_a, sel_b, rows3(doa), oa32, lse_a, pairs)
        return recv

    if place is None:
        sb_with_pairs(())
        dil_with_chips(())
    else:
        mine_late = _reduce_halves([grads[n] for n in LATE], LATE, *place, sb_with_pairs, dil_with_chips)
    dproj = jnp.concatenate([rows2(t) for t in att["a"] + att["b"]] + [dgates], axis=1)
    grads["w_in"] = _mm_cs_dw(n1, dproj, name="mm_in_dw")
    dn1 = _mm_cs_dx(dproj, wf["w_in"], name="mm_in_dx")
    dx, dg_mix = _rms_bwd(xb, g_mix, dn1, dh1, name="rms_mix_bwd")
    grad_x = dx.reshape(b_dim, s_dim, d)
    gains = (dg_mix, dg_mem_q, dg_mem_kv, dg_ffn, dg_final)
    if place is None:
        return loss_row, grad_x, grads, gains
    mine_early = _reduce_halves([grads[n] for n in EARLY], EARLY, *place, _pair_exchange, _chip_exchange)
    return loss_row, grad_x, mine_early + mine_late, gains


def _reduce_and_update(given, shards, loss_row, grad_x, mine, gain_grads, c_idx):
    d = D_MODEL
    dg_mix, dg_mem_q, dg_mem_kv, dg_ffn, dg_final = gain_grads
    small = jnp.concatenate([dg_mix, dg_mem_q, dg_mem_kv, dg_ffn, dg_final,
                             jnp.pad(loss_row, ((0, 0), (0, FLAT_COLS - LANES))), jnp.zeros((2, FLAT_COLS), F32)], axis=0)
    small_all = _gather_small(small)
    others = _swap_halves(mine)
    small_sum = _sum8(small_all)
    loss = small_sum[5, 0]

    out_g, out_d, out_m, out_v = {}, {}, {}, {}
    for n, mine_n, other_n in zip(NAMES, mine, others):
        g2, dl, nm, nv = _adamw_halves(shards[n], mine_n, other_n, given["m_" + n][0], given["v_" + n][0], c_idx,
                                       name="adamw_" + n)
        out_g[n], out_d[n], out_m[n], out_v[n] = g2[None], dl[None], nm[None], nv[None]
    gain_w = jnp.concatenate([given[n].reshape(1, d) for n in GAINS], axis=0)
    gain_m = jnp.concatenate([given["m_" + n].reshape(1, d) for n in GAINS], axis=0)
    gain_v = jnp.concatenate([given["v_" + n].reshape(1, d) for n in GAINS], axis=0)
    gain_g = small_sum[:len(GAINS)]
    gd, gm, gv = _adamw(gain_w, gain_g, gain_m, gain_v, name="adamw_gains")
    for i, n in enumerate(GAINS):
        shape = given[n].shape
        out_g[n], out_d[n] = gain_g[i].reshape(shape), gd[i].reshape(shape)
        out_m[n], out_v[n] = gm[i].reshape(shape), gv[i].reshape(shape)

    order = ["g_mix", "w_in", "w_up_a", "w_up_b", "w_out", "g_mem_q", "g_mem_kv", "w_q_mem", "w_kv_mem", "w_o_mem",
             "g_ffn", "w_ffn_gate", "w_ffn_up", "w_ffn_down", "g_final"]
    return (loss, grad_x, *[out_g[n] for n in order], *[out_d[n] for n in order],
            *[out_m[n] for n in order], *[out_v[n] for n in order])
```

```python
import jax
import jax.numpy as jnp
from jax import lax
from jax.experimental import pallas as pl
from jax.experimental.pallas import tpu as pltpu

F32 = jnp.float32
BF16 = jnp.bfloat16
MESH = pl.DeviceIdType.MESH

D_MODEL = 1024
HEAD_DIM = 64
N_HEADS = 8
ATT_WIDTH = N_HEADS * HEAD_DIM
DIL_PATTERNS = ((128, 1), (512, 4), (2048, 16))
BLOCK = 128
SB_ROWS = 512
ROPE_THETA = 500000.0
ROPE_DIM = HEAD_DIM // 4
N_HEADS_MEM = 4
MEM_HEAD_DIM = 128
MEM_WIDTH = N_HEADS_MEM * MEM_HEAD_DIM
D_FF = 2816
IN_COLS = 6 * ATT_WIDTH + 2 * D_MODEL
RMS_EPS = 1e-6
ADAM_LR = 0.001
ADAM_B1 = 0.9
ADAM_B2 = 0.999
ADAM_EPS = 1e-08
ADAM_WD = 0.01
ADAM_STEP = 10

N_CHIPS = 4
LANES = 128
FLAT_COLS = 1024
VMEM_LIMIT = 56 * 1024 * 1024

PAIRS = ATT_WIDTH // LANES
COL_QA, COL_KA, COL_VA, COL_QB, COL_KB, COL_VB = (i * PAIRS for i in range(6))

MM_CAP = 1408
NN = (((1,), (0,)), ((), ()))
NT = (((1,), (1,)), ((), ()))
TN = (((0,), (0,)), ((), ()))
BNN = (((2,), (1,)), ((0,), (0,)))
BNT = (((2,), (2,)), ((0,), (0,)))
BTN = (((1,), (1,)), ((0,), (0,)))
DIL_BATCH = 4


def _tile(dim, cap, unit=LANES):
    if dim <= cap:
        return dim
    best = None
    for t in range(unit, cap + 1, unit):
        if dim % t == 0:
            best = t
    assert best is not None, (dim, cap)
    return best


def _params(sem):
    return pltpu.CompilerParams(dimension_semantics=sem, vmem_limit_bytes=VMEM_LIMIT)


def _mm(a, b, *, name, ta=False, tb=False, add=None, out_dtype=F32,
        tm_cap=MM_CAP, tn_cap=MM_CAP, tk_cap=MM_CAP):
    if ta:
        k_dim, m_dim = a.shape
    else:
        m_dim, k_dim = a.shape
    if tb:
        n_dim, kb = b.shape
    else:
        kb, n_dim = b.shape
    assert kb == k_dim, (a.shape, b.shape, ta, tb)
    tm, tn, tk = _tile(m_dim, tm_cap), _tile(n_dim, tn_cap), _tile(k_dim, tk_cap)
    nk = k_dim // tk
    dims = (((0 if ta else 1,), (1 if tb else 0,)), ((), ()))
    has_add = add is not None

    def body(*refs):
        if has_add:
            a_ref, b_ref, add_ref, o_ref = refs[:4]
        else:
            a_ref, b_ref, o_ref = refs[:3]
        part = lax.dot_general(a_ref[...].astype(BF16), b_ref[...].astype(BF16), dims, preferred_element_type=F32)

        def finish(r):
            if has_add:
                r = add_ref[...] + r
            o_ref[...] = r.astype(out_dtype)

        if nk == 1:
            finish(part)
            return
        acc_ref = refs[-1]
        k = pl.program_id(2)

        @pl.when(k == 0)
        def _():
            acc_ref[...] = part

        @pl.when(k > 0)
        def _():
            acc_ref[...] += part

        @pl.when(k == nk - 1)
        def _():
            finish(acc_ref[...])

    a_spec = pl.BlockSpec((tk, tm), lambda i, j, k: (k, i)) if ta else pl.BlockSpec((tm, tk), lambda i, j, k: (i, k))
    b_spec = pl.BlockSpec((tn, tk), lambda i, j, k: (j, k)) if tb else pl.BlockSpec((tk, tn), lambda i, j, k: (k, j))
    o_spec = pl.BlockSpec((tm, tn), lambda i, j, k: (i, j))
    in_specs = [a_spec, b_spec] + ([o_spec] if has_add else [])
    args = (a, b) + ((add,) if has_add else ())
    return pl.pallas_call(
        body, name=name, grid=(m_dim // tm, n_dim // tn, nk),
        in_specs=in_specs, out_specs=o_spec,
        out_shape=jax.ShapeDtypeStruct((m_dim, n_dim), out_dtype),
        scratch_shapes=[pltpu.VMEM((tm, tn), F32)] if nk > 1 else [],
        compiler_params=_params(("parallel", "parallel", "arbitrary")),
    )(*args)


def _mm_core(name, a, b, a_spec, b_spec, o_spec, out_shape, grid, dims, *, add=None, out_dtype=F32):
    nk = grid[2]
    has_add = add is not None
    acc_shape = tuple(d for d in o_spec.block_shape if d is not None)

    def body(*refs):
        if has_add:
            a_ref, b_ref, add_ref, o_ref = refs[:4]
        else:
            a_ref, b_ref, o_ref = refs[:3]
        part = lax.dot_general(a_ref[...].astype(BF16), b_ref[...].astype(BF16), dims, preferred_element_type=F32)

        def finish(r):
            if has_add:
                r = add_ref[...] + r
            o_ref[...] = r.astype(out_dtype)

        if nk == 1:
            finish(part)
            return
        acc_ref = refs[-1]
        k = pl.program_id(2)

        @pl.when(k == 0)
        def _():
            acc_ref[...] = part

        @pl.when(k > 0)
        def _():
            acc_ref[...] += part

        @pl.when(k == nk - 1)
        def _():
            finish(acc_ref[...])

    in_specs = [a_spec, b_spec] + ([o_spec] if has_add else [])
    args = (a, b) + ((add,) if has_add else ())
    return pl.pallas_call(
        body, name=name, grid=grid, in_specs=in_specs, out_specs=o_spec,
        out_shape=jax.ShapeDtypeStruct(out_shape, out_dtype),
        scratch_shapes=[pltpu.VMEM(acc_shape, F32)] if nk > 1 else [],
        compiler_params=_params(("parallel", "parallel", "arbitrary")),
    )(*args)


def _mm_cs(a, w3, *, name, add=None, out_dtype=F32):
    m_dim, k_dim = a.shape
    _, _, n4 = w3.shape
    tm, tn, tk = _tile(m_dim, MM_CAP), _tile(n4, MM_CAP), _tile(k_dim, MM_CAP)
    npb = n4 // tn
    return _mm_core(name, a, w3,
                    pl.BlockSpec((tm, tk), lambda i, j, k: (i, k)),
                    pl.BlockSpec((None, tk, tn), lambda i, j, k: (j // npb, k, j % npb)),
                    pl.BlockSpec((tm, tn), lambda i, j, k: (i, j)),
                    (m_dim, N_CHIPS * n4), (m_dim // tm, N_CHIPS * npb, k_dim // tk), NN, add=add, out_dtype=out_dtype)


def _mm_cs_dx(dy, w3, *, name, out_dtype=F32):
    m_dim, _ = dy.shape
    _, k_dim, n4 = w3.shape
    tm, tkw, tn = _tile(m_dim, MM_CAP), _tile(k_dim, MM_CAP), _tile(n4, MM_CAP)
    npb = n4 // tn
    return _mm_core(name, dy, w3,
                    pl.BlockSpec((tm, tn), lambda i, j, k: (i, k)),
                    pl.BlockSpec((None, tkw, tn), lambda i, j, k: (k // npb, j, k % npb)),
                    pl.BlockSpec((tm, tkw), lambda i, j, k: (i, j)),
                    (m_dim, k_dim), (m_dim // tm, k_dim // tkw, N_CHIPS * npb), NT, out_dtype=out_dtype)


def _mm_cs_dw(a, dy, *, name):
    m_dim, k_dim = a.shape
    n4 = dy.shape[1] // N_CHIPS
    tmk, tn, tk = _tile(k_dim, MM_CAP), _tile(n4, MM_CAP), _tile(m_dim, MM_CAP)
    npb = n4 // tn
    return _mm_core(name, a, dy,
                    pl.BlockSpec((tk, tmk), lambda i, j, k: (k, i)),
                    pl.BlockSpec((tk, tn), lambda i, j, k: (k, j)),
                    pl.BlockSpec((None, tmk, tn), lambda i, j, k: (j // npb, i, j % npb)),
                    (N_CHIPS, k_dim, n4), (k_dim // tmk, N_CHIPS * npb, m_dim // tk), TN)


def _mm_ffn_up(n, w3, *, name):
    t_dim, d = n.shape
    _, _, f4 = w3.shape
    tm = _tile(t_dim, MM_CAP)
    return _mm_core(name, n, w3,
                    pl.BlockSpec((tm, d), lambda i, j, k: (i, 0)),
                    pl.BlockSpec((None, d, f4), lambda i, j, k: (j, 0, 0)),
                    pl.BlockSpec((None, tm, f4), lambda i, j, k: (j, i, 0)),
                    (N_CHIPS, t_dim, f4), (t_dim // tm, N_CHIPS, 1), NN)


def _mm_ffn_up_dw(n, d3, *, name):
    t_dim, d = n.shape
    _, _, f4 = d3.shape
    tk = _tile(t_dim, MM_CAP)
    return _mm_core(name, n, d3,
                    pl.BlockSpec((tk, d), lambda i, j, k: (k, 0)),
                    pl.BlockSpec((None, tk, f4), lambda i, j, k: (j, k, 0)),
                    pl.BlockSpec((None, d, f4), lambda i, j, k: (j, 0, 0)),
                    (N_CHIPS, d, f4), (1, N_CHIPS, t_dim // tk), TN)


def _mm_ffn_up_dx(d3, w3, *, name, add=None):
    _, t_dim, f4 = d3.shape
    _, d, _ = w3.shape
    tm = _tile(t_dim, MM_CAP)
    return _mm_core(name, d3, w3,
                    pl.BlockSpec((None, tm, f4), lambda i, j, k: (k, i, 0)),
                    pl.BlockSpec((None, d, f4), lambda i, j, k: (k, 0, 0)),
                    pl.BlockSpec((tm, d), lambda i, j, k: (i, 0)),
                    (t_dim, d), (t_dim // tm, 1, N_CHIPS), NT, add=add)


def _mm_ffn_down(act3, wd3, *, name, add):
    _, t_dim, f4 = act3.shape
    _, _, d = wd3.shape
    tm = _tile(t_dim, MM_CAP)
    return _mm_core(name, act3, wd3,
                    pl.BlockSpec((None, tm, f4), lambda i, j, k: (k, i, 0)),
                    pl.BlockSpec((None, f4, d), lambda i, j, k: (k, 0, 0)),
                    pl.BlockSpec((tm, d), lambda i, j, k: (i, 0)),
                    (t_dim, d), (t_dim // tm, 1, N_CHIPS), NN, add=add)


def _mm_ffn_down_dx(dh, wd3, *, name):
    t_dim, d = dh.shape
    _, f4, _ = wd3.shape
    tm = _tile(t_dim, MM_CAP)
    return _mm_core(name, dh, wd3,
                    pl.BlockSpec((tm, d), lambda i, j, k: (i, 0)),
                    pl.BlockSpec((None, f4, d), lambda i, j, k: (j, 0, 0)),
                    pl.BlockSpec((None, tm, f4), lambda i, j, k: (j, i, 0)),
                    (N_CHIPS, t_dim, f4), (t_dim // tm, N_CHIPS, 1), NT)


def _mm_ffn_down_dw(act3, dh, *, name):
    _, t_dim, f4 = act3.shape
    d = dh.shape[1]
    tk = _tile(t_dim, MM_CAP)
    return _mm_core(name, act3, dh,
                    pl.BlockSpec((None, tk, f4), lambda i, j, k: (i, k, 0)),
                    pl.BlockSpec((tk, d), lambda i, j, k: (k, 0)),
                    pl.BlockSpec((None, f4, d), lambda i, j, k: (i, 0, 0)),
                    (N_CHIPS, f4, d), (N_CHIPS, 1, t_dim // tk), TN)


def _rms_fwd(x, g, *, name, tt=512):
    t_dim, d = x.shape
    tt = _tile(t_dim, tt, 8)

    def body(x_ref, g_ref, o_ref):
        xv = x_ref[...]
        r = lax.rsqrt(jnp.mean(xv * xv, axis=-1, keepdims=True) + RMS_EPS)
        o_ref[...] = ((xv * r) * g_ref[...]).astype(o_ref.dtype)

    return pl.pallas_call(
        body, name=name, grid=(t_dim // tt,),
        in_specs=[pl.BlockSpec((tt, d), lambda i: (i, 0)), pl.BlockSpec((1, d), lambda i: (0, 0))],
        out_specs=pl.BlockSpec((tt, d), lambda i: (i, 0)),
        out_shape=jax.ShapeDtypeStruct((t_dim, d), BF16),
        compiler_params=_params(("parallel",)),
    )(x, g)


def _rms_bwd(x, g, dy, add, *, name, tt=512):
    t_dim, d = x.shape
    tt = _tile(t_dim, tt, 8)
    has_add = add is not None

    def body(*refs):
        if has_add:
            x_ref, g_ref, dy_ref, add_ref, dx_ref, dg_ref = refs
        else:
            x_ref, g_ref, dy_ref, dx_ref, dg_ref = refs
        xv = x_ref[...]
        dyv = dy_ref[...].astype(F32)
        r = lax.rsqrt(jnp.mean(xv * xv, axis=-1, keepdims=True) + RMS_EPS)
        xh = xv * r
        u = dyv * g_ref[...]
        dx = r * (u - xh * jnp.mean(u * xh, axis=-1, keepdims=True))
        if has_add:
            dx = add_ref[...] + dx
        dx_ref[...] = dx

        @pl.when(pl.program_id(0) == 0)
        def _():
            dg_ref[...] = jnp.zeros_like(dg_ref)

        dg_ref[...] += jnp.sum(dyv * xh, axis=0, keepdims=True)

    row = pl.BlockSpec((tt, d), lambda i: (i, 0))
    vec = pl.BlockSpec((1, d), lambda i: (0, 0))
    in_specs = [row, vec, row] + ([row] if has_add else [])
    args = (x, g, dy) + ((add,) if has_add else ())
    return pl.pallas_call(
        body, name=name, grid=(t_dim // tt,),
        in_specs=in_specs, out_specs=[row, vec],
        out_shape=[jax.ShapeDtypeStruct((t_dim, d), F32), jax.ShapeDtypeStruct((1, d), F32)],
        compiler_params=_params(("arbitrary",)),
    )(*args)


def _final(h, g, target, *, tt=512):
    t_dim, d = h.shape
    n_steps = t_dim // tt

    def body(h_ref, g_ref, t_ref, loss_ref, dh_ref, dg_ref, sq_ref):
        i = pl.program_id(0)
        xv = h_ref[...]
        gv = g_ref[...]
        r = lax.rsqrt(jnp.mean(xv * xv, axis=-1, keepdims=True) + RMS_EPS)
        xh = xv * r
        err = xh * gv - t_ref[...]
        dyv = err * (1.0 / d)
        u = dyv * gv
        dh_ref[...] = r * (u - xh * jnp.mean(u * xh, axis=-1, keepdims=True))

        @pl.when(i == 0)
        def _():
            dg_ref[...] = jnp.zeros_like(dg_ref)
            sq_ref[...] = jnp.zeros_like(sq_ref)

        dg_ref[...] += jnp.sum(dyv * xh, axis=0, keepdims=True)
        sq_ref[...] += jnp.sum(err * err, axis=0, keepdims=True)

        @pl.when(i == n_steps - 1)
        def _():
            total = jnp.sum(sq_ref[...], axis=-1, keepdims=True) * (0.5 / d)
            loss_ref[...] = jnp.broadcast_to(total, loss_ref.shape)

    row = pl.BlockSpec((tt, d), lambda i: (i, 0))
    vec = pl.BlockSpec((1, d), lambda i: (0, 0))
    return pl.pallas_call(
        body, name="final_loss", grid=(n_steps,),
        in_specs=[row, vec, row],
        out_specs=[pl.BlockSpec((1, LANES), lambda i: (0, 0)), row, vec],
        out_shape=[jax.ShapeDtypeStruct((1, LANES), F32), jax.ShapeDtypeStruct((t_dim, d), F32),
                   jax.ShapeDtypeStruct((1, d), F32)],
        scratch_shapes=[pltpu.VMEM((1, d), F32)],
        compiler_params=_params(("arbitrary",)),
    )(h, g, target)


def _rope_table(pos, inv_lane, sel_a, sel_b, *, tt=512):
    t_dim = pos.shape[0]

    def body(p_ref, f_ref, a_ref, b_ref, c_ref, s_ref):
        ang = p_ref[...] * f_ref[...]
        on = (a_ref[...] + b_ref[...]) > 0.0
        c_ref[...] = jnp.where(on, jnp.cos(ang), 1.0)
        s_ref[...] = jnp.where(on, jnp.sin(ang), 0.0)

    vec = pl.BlockSpec((1, LANES), lambda i: (0, 0))
    row = pl.BlockSpec((tt, LANES), lambda i: (i, 0))
    shp = jax.ShapeDtypeStruct((t_dim, LANES), F32)
    return pl.pallas_call(
        body, name="rope_table", grid=(t_dim // tt,),
        in_specs=[pl.BlockSpec((tt, 1), lambda i: (i, 0)), vec, vec, vec],
        out_specs=[row, row], out_shape=[shp, shp],
        compiler_params=_params(("parallel",)),
    )(pos, inv_lane, sel_a, sel_b)


def _rotate(xv, cs, sn, sa, sb):
    half = ROPE_DIM // 2
    up = pltpu.roll(xv, LANES - half, 1)
    dn = pltpu.roll(xv, half, 1)
    return xv * cs + (dn * sb - up * sa) * sn


def _head_masks():
    h1 = lax.broadcasted_iota(jnp.int32, (1, LANES), 1) < HEAD_DIM
    return h1, jnp.logical_not(h1)


def _split_heads(xv, h1, h2):
    return jnp.where(h1, xv, 0.0).astype(BF16), jnp.where(h2, xv, 0.0).astype(BF16)


def _tri_masks():
    r = lax.broadcasted_iota(jnp.int32, (BLOCK, BLOCK), 0)
    c = lax.broadcasted_iota(jnp.int32, (BLOCK, BLOCK), 1)
    return c <= r, r <= c


def _stream_rows(start, dil):
    if dil == 1:
        return pl.ds(pl.multiple_of(start, BLOCK), BLOCK)
    return pl.ds(start, BLOCK, stride=dil)


def _dil_tile(idx, dil, nb):
    r = idx // nb
    n = idx % nb
    return (_stream_rows(r + dil * BLOCK * n, dil), _stream_rows(r + dil * BLOCK * jnp.maximum(n - 1, 0), dil),
            n > 0)


def _dil_specs(b_dim, s_dim):
    def col(c0):
        return pl.BlockSpec((None, s_dim, LANES),lambda b, h: (b, 0, c0 + h))
    tab = pl.BlockSpec((None, s_dim, LANES),lambda b, h: (b, 0, 0))
    vec = pl.BlockSpec((1, LANES), lambda b, h: (0, 0))
    return col, tab, vec


def _dil_fwd(proj3, cs3, sn3, sel_a, sel_b, rider=None):
    b_dim, s_dim, _ = proj3.shape
    scale = HEAD_DIM ** -0.5
    n_pat = len(DIL_PATTERNS)
    extra, extra_shapes, extra_sems = _rider_parts(rider)
    n_w = len(extra)
    n_steps = b_dim * PAIRS

    def body(*refs):
        q_ref, k_ref, v_ref, cs_ref, sn_ref, sa_ref, sb_ref = refs[:7]
        o16_ref, o32_ref, l_ref = refs[7 + n_w:10 + n_w]
        qr, kr = refs[10 + 2 * n_w:12 + 2 * n_w]
        per_pattern = refs[12 + 2 * n_w:12 + 2 * n_w + 2 * n_pat]
        og, lg = per_pattern[:n_pat], per_pattern[n_pat:]
        step = pl.program_id(0) * PAIRS + pl.program_id(1)
        begin, end = _rider_hooks(rider, refs[7:7 + n_w], refs[10 + n_w:10 + 2 * n_w], refs[-2:], step, n_steps)
        begin()
        h1, h2 = _head_masks()
        cur_ok, prev_ok = _tri_masks()
        sa, sb = sa_ref[...], sb_ref[...]

        def prep(j, _):
            rows = pl.ds(pl.multiple_of(j * BLOCK, BLOCK), BLOCK)
            cs, sn = cs_ref[rows, :], sn_ref[rows, :]
            qr[rows, :] = _rotate(q_ref[rows, :], cs, sn, sa, sb) * scale
            kr[rows, :] = _rotate(k_ref[rows, :], cs, sn, sa, sb)
            return 0

        lax.fori_loop(0, s_dim // BLOCK, prep, 0)

        for g, (_, dil) in enumerate(DIL_PATTERNS):
            nb = s_dim // dil // BLOCK

            def some(bi, _, g=g, dil=dil, nb=nb):
                tiles = [_dil_tile(bi * DIL_BATCH + t, dil, nb) for t in range(DIL_BATCH)]
                rows = [t[0] for t in tiles]
                q1, q2 = _split_heads(jnp.stack([qr[rw, :] for rw in rows]), h1, h2)
                kc = jnp.stack([kr[rw, :] for rw in rows]).astype(BF16)
                vc1, vc2 = _split_heads(jnp.stack([v_ref[rw, :] for rw in rows]), h1, h2)
                if nb > 1:
                    kp = jnp.stack([kr[t[1], :] for t in tiles]).astype(BF16)
                    vp1, vp2 = _split_heads(jnp.stack([v_ref[t[1], :] for t in tiles]), h1, h2)
                    p_ok = jnp.stack([jnp.logical_and(prev_ok, t[2]) for t in tiles])

                def head(qh, vch, vph):
                    sc = jnp.where(cur_ok, lax.dot_general(qh, kc, BNT, preferred_element_type=F32), -jnp.inf)
                    m = jnp.max(sc, axis=-1, keepdims=True)
                    if nb > 1:
                        sp = jnp.where(p_ok, lax.dot_general(qh, kp, BNT, preferred_element_type=F32), -jnp.inf)
                        m = jnp.maximum(m, jnp.max(sp, axis=-1, keepdims=True))
                    pc = jnp.exp(sc - m)
                    den = jnp.sum(pc, axis=-1, keepdims=True)
                    acc = lax.dot_general(pc.astype(BF16), vch, BNN, preferred_element_type=F32)
                    if nb > 1:
                        pp = jnp.exp(sp - m)
                        den = den + jnp.sum(pp, axis=-1, keepdims=True)
                        acc = acc + lax.dot_general(pp.astype(BF16), vph, BNN, preferred_element_type=F32)
                    return acc / den, m + jnp.log(den)

                o1, l1 = head(q1, vc1, vp1 if nb > 1 else None)
                o2, l2 = head(q2, vc2, vp2 if nb > 1 else None)
                o, l = o1 + o2, jnp.where(h1, l1, l2)
                for t, rw in enumerate(rows):
                    og[g][rw, :] = o[t]
                    lg[g][rw, :] = l[t]
                return 0

            lax.fori_loop(0, dil * nb // DIL_BATCH, some, 0)

        def comb(j, _):
            rows = pl.ds(pl.multiple_of(j * BLOCK, BLOCK), BLOCK)
            ls = [lg[g][rows, :] for g in range(n_pat)]
            m = jnp.maximum(jnp.maximum(ls[0], ls[1]), ls[2])
            es = [jnp.exp(l - m) for l in ls]
            den = es[0] + es[1] + es[2]
            o = (es[0] * og[0][rows, :] + es[1] * og[1][rows, :] + es[2] * og[2][rows, :]) / den
            o16_ref[rows, :] = o.astype(BF16)
            o32_ref[rows, :] = o
            l_ref[rows, :] = m + jnp.log(den)
            return 0

        lax.fori_loop(0, s_dim // BLOCK, comb, 0)
        end()

    col, tab, vec = _dil_specs(b_dim, s_dim)
    out = pl.BlockSpec((None, s_dim, LANES),lambda b, h: (b, 0, h))
    shp = (b_dim, s_dim, ATT_WIDTH)
    res = pl.pallas_call(
        body, name="dil_fwd", grid=(b_dim, PAIRS),
        in_specs=[col(COL_QA), col(COL_KA), col(COL_VA), tab, tab, vec, vec] + [ANY] * n_w,
        out_specs=[out, out, out] + [ANY] * n_w,
        out_shape=[jax.ShapeDtypeStruct(shp, BF16), jax.ShapeDtypeStruct(shp, F32), jax.ShapeDtypeStruct(shp, F32)]
        + extra_shapes,
        scratch_shapes=[pltpu.VMEM((s_dim, LANES), F32)] * (2 + 2 * n_pat) + extra_sems,
        compiler_params=_params(("arbitrary", "arbitrary")),
    )(proj3, proj3, proj3, cs3, sn3, sel_a, sel_b, *extra)
    return res[:3], res[3:]


def _dil_bwd(proj3, cs3, sn3, sel_a, sel_b, do3, o3, lse3, rider=None):
    b_dim, s_dim, _ = proj3.shape
    scale = HEAD_DIM ** -0.5
    extra, extra_shapes, extra_sems = _rider_parts(rider)
    n_w = len(extra)

    def body(*refs):
        q_ref, k_ref, v_ref, cs_ref, sn_ref, sa_ref, sb_ref, do_ref, o_ref, l_ref = refs[:10]
        dq_ref, dk_ref, dv_ref = refs[10 + n_w:13 + n_w]
        qr, kr, dqa, dka, dva = refs[13 + 2 * n_w:18 + 2 * n_w]
        step = pl.program_id(0) * PAIRS + pl.program_id(1)
        begin, end = _rider_hooks(rider, refs[10:10 + n_w], refs[13 + n_w:13 + 2 * n_w], refs[-2:], step,
                                  b_dim * PAIRS)
        begin()
        h1, h2 = _head_masks()
        cur_ok, prev_ok = _tri_masks()
        sa, sb = sa_ref[...], sb_ref[...]

        def prep(j, _):
            rows = pl.ds(pl.multiple_of(j * BLOCK, BLOCK), BLOCK)
            cs, sn = cs_ref[rows, :], sn_ref[rows, :]
            qr[rows, :] = _rotate(q_ref[rows, :], cs, sn, sa, sb) * scale
            kr[rows, :] = _rotate(k_ref[rows, :], cs, sn, sa, sb)
            zero = jnp.zeros((BLOCK, LANES), F32)
            dqa[rows, :] = zero
            dka[rows, :] = zero
            dva[rows, :] = zero
            return 0

        lax.fori_loop(0, s_dim // BLOCK, prep, 0)

        for _, dil in DIL_PATTERNS:
            nb = s_dim // dil // BLOCK

            def some(bi, _, dil=dil, nb=nb):
                tiles = [_dil_tile(bi * DIL_BATCH + t, dil, nb) for t in range(DIL_BATCH)]
                rows = [t[0] for t in tiles]
                q1, q2 = _split_heads(jnp.stack([qr[rw, :] for rw in rows]), h1, h2)
                dof = jnp.stack([do_ref[rw, :] for rw in rows])
                do1, do2 = _split_heads(dof, h1, h2)
                prod = dof * jnp.stack([o_ref[rw, :] for rw in rows])
                delta1 = jnp.sum(jnp.where(h1, prod, 0.0), axis=-1, keepdims=True)
                delta2 = jnp.sum(jnp.where(h2, prod, 0.0), axis=-1, keepdims=True)
                lt = jnp.stack([l_ref[rw, :] for rw in rows])
                lse1 = jnp.max(jnp.where(h1, lt, -jnp.inf), axis=-1, keepdims=True)
                lse2 = jnp.max(jnp.where(h2, lt, -jnp.inf), axis=-1, keepdims=True)

                def side(krows, ok):
                    kf = jnp.stack([kr[kw, :] for kw in krows])
                    k16 = kf.astype(BF16)
                    k1, k2 = _split_heads(kf, h1, h2)
                    v16 = jnp.stack([v_ref[kw, :] for kw in krows]).astype(BF16)

                    def head(qh, doh, lse, delta):
                        sc = lax.dot_general(qh, k16, BNT, preferred_element_type=F32)
                        p = jnp.where(ok, jnp.exp(sc - lse), 0.0)
                        dp = lax.dot_general(doh, v16, BNT, preferred_element_type=F32)
                        return p.astype(BF16), (p * (dp - delta)).astype(BF16)

                    p1, ds1 = head(q1, do1, lse1, delta1)
                    p2, ds2 = head(q2, do2, lse2, delta2)
                    dv = (lax.dot_general(p1, do1, BTN, preferred_element_type=F32)
                          + lax.dot_general(p2, do2, BTN, preferred_element_type=F32))
                    dk = (lax.dot_general(ds1, q1, BTN, preferred_element_type=F32)
                          + lax.dot_general(ds2, q2, BTN, preferred_element_type=F32))
                    for t, kw in enumerate(krows):
                        dva[kw, :] += dv[t]
                        dka[kw, :] += dk[t]
                    return (lax.dot_general(ds1, k1, BNN, preferred_element_type=F32)
                            + lax.dot_general(ds2, k2, BNN, preferred_element_type=F32))

                dq = side(rows, cur_ok)
                if nb > 1:
                    dq = dq + side([t[1] for t in tiles], jnp.stack([jnp.logical_and(prev_ok, t[2]) for t in tiles]))
                for t, rw in enumerate(rows):
                    dqa[rw, :] += dq[t] * scale
                return 0

            lax.fori_loop(0, dil * nb // DIL_BATCH, some, 0)

        def finish(j, _):
            rows = pl.ds(pl.multiple_of(j * BLOCK, BLOCK), BLOCK)
            cs, sn = cs_ref[rows, :], -sn_ref[rows, :]
            dq_ref[rows, :] = _rotate(dqa[rows, :], cs, sn, sa, sb).astype(BF16)
            dk_ref[rows, :] = _rotate(dka[rows, :], cs, sn, sa, sb).astype(BF16)
            dv_ref[rows, :] = dva[rows, :].astype(BF16)
            return 0

        lax.fori_loop(0, s_dim // BLOCK, finish, 0)
        end()

    col, tab, vec = _dil_specs(b_dim, s_dim)
    out = pl.BlockSpec((None, s_dim, LANES),lambda b, h: (b, 0, h))
    shp = jax.ShapeDtypeStruct((b_dim, s_dim, ATT_WIDTH), BF16)
    acc = pltpu.VMEM((s_dim, LANES), F32)
    res = pl.pallas_call(
        body, name="dil_bwd", grid=(b_dim, PAIRS),
        in_specs=[col(COL_QA), col(COL_KA), col(COL_VA), tab, tab, vec, vec, out, out, out] + [ANY] * n_w,
        out_specs=[out, out, out] + [ANY] * n_w, out_shape=[shp, shp, shp] + extra_shapes,
        scratch_shapes=[acc, acc, acc, acc, acc] + extra_sems,
        compiler_params=_params(("arbitrary", "arbitrary")),
    )(proj3, proj3, proj3, cs3, sn3, sel_a, sel_b, do3, o3, lse3, *extra)
    return res[:3], res[3:]


def _split_dot(x, tri):
    hi = x.astype(BF16)
    lo = (x - hi.astype(F32)).astype(BF16)
    return jnp.dot(hi, tri, preferred_element_type=F32) + jnp.dot(lo, tri, preferred_element_type=F32)


def _log_sigmoid(z):
    return jnp.minimum(z, 0.0) - jnp.log(1.0 + jnp.exp(-jnp.abs(z)))


def _sb_scores(qh, k16, valid):
    z = lax.dot_general(qh, k16, NT, preferred_element_type=F32)
    ls = _log_sigmoid(z)
    l1m = ls - z
    return ls, (l1m if valid is None else jnp.where(valid, l1m, 0.0))


def _sb_consts():
    r = lax.broadcasted_iota(jnp.int32, (BLOCK, BLOCK), 0)
    c = lax.broadcasted_iota(jnp.int32, (BLOCK, BLOCK), 1)
    after = (r > c).astype(BF16)
    before = (r < c).astype(BF16)
    qrow = lax.broadcasted_iota(jnp.int32, (SB_ROWS, BLOCK), 0)
    kcol = lax.broadcasted_iota(jnp.int32, (SB_ROWS, BLOCK), 1)
    return after, before, qrow, kcol


def _pairs_loop(n_blocks, step, carry):
    return lax.fori_loop(0, n_blocks // 2, lambda i, c: step(2 * i + 1, step(2 * i, c)), carry)


def _sb_fwd(proj3, rider=None):
    b_dim, s_dim, _ = proj3.shape
    scale = HEAD_DIM ** -0.5
    per = SB_ROWS // BLOCK
    extra, extra_shapes, extra_sems = _rider_parts(rider)
    n_w = len(extra)

    def body(*refs):
        q_ref, k_ref, v_ref = refs[:3]
        o_ref = refs[3 + n_w]
        step = pl.program_id(0) * PAIRS + pl.program_id(1)
        begin, end = _rider_hooks(rider, refs[3:3 + n_w], refs[4 + n_w:4 + 2 * n_w], refs[-2:], step, b_dim * PAIRS)
        begin()
        h1, h2 = _head_masks()
        after, _, qrow, kcol = _sb_consts()

        def qloop(qi, _):
            rows = pl.ds(pl.multiple_of(qi * SB_ROWS, SB_ROWS), SB_ROWS)
            q1, q2 = _split_heads(q_ref[rows, :] * scale, h1, h2)
            first = qi * per

            def block(kb, carry, masked):
                acc, run1, run2 = carry
                krows = pl.ds(pl.multiple_of(kb * BLOCK, BLOCK), BLOCK)
                k16 = k_ref[krows, :].astype(BF16)
                v1, v2 = _split_heads(v_ref[krows, :], h1, h2)
                valid = ((kb - first) * BLOCK + kcol < qrow) if masked else None

                def head(qh, vh, run):
                    ls, l1m = _sb_scores(qh, k16, valid)
                    a = jnp.exp(ls + _split_dot(l1m, after) + run)
                    if masked:
                        a = jnp.where(valid, a, 0.0)
                    return (jnp.dot(a.astype(BF16), vh, preferred_element_type=F32),
                            run + jnp.sum(l1m, axis=-1, keepdims=True))

                o1, run1 = head(q1, v1, run1)
                o2, run2 = head(q2, v2, run2)
                return acc + o1 + o2, run1, run2

            zcol = jnp.zeros((SB_ROWS, 1), F32)
            carry = (jnp.zeros((SB_ROWS, LANES), F32), zcol, zcol)
            carry = _pairs_loop(per, lambda i, c: block(first + per - 1 - i, c, True), carry)
            acc, _, _ = _pairs_loop(first, lambda i, c: block(first - 1 - i, c, False), carry)
            o_ref[rows, :] = acc.astype(BF16)
            return 0

        lax.fori_loop(0, s_dim // SB_ROWS, qloop, 0)
        end()

    def col(c0):
        return pl.BlockSpec((None, s_dim, LANES),lambda b, h: (b, 0, c0 + h))

    res = pl.pallas_call(
        body, name="sb_fwd", grid=(b_dim, PAIRS),
        in_specs=[col(COL_QB), col(COL_KB), col(COL_VB)] + [ANY] * n_w, out_specs=[col(0)] + [ANY] * n_w,
        out_shape=[jax.ShapeDtypeStruct((b_dim, s_dim, ATT_WIDTH), BF16)] + extra_shapes,
        scratch_shapes=extra_sems,
        compiler_params=_params(("arbitrary", "arbitrary")),
    )(proj3, proj3, proj3, *extra)
    return res[0], res[1:]


def _sb_bwd(proj3, do3, rider=None):
    b_dim, s_dim, _ = proj3.shape
    scale = HEAD_DIM ** -0.5
    per = SB_ROWS // BLOCK
    nkb_max = s_dim // BLOCK
    extra, extra_shapes, extra_sems = _rider_parts(rider)
    n_w = len(extra)

    def body(*refs):
        q_ref, k_ref, v_ref, do_ref = refs[:4]
        dq_ref, dk_ref, dv_ref = refs[4 + n_w:7 + n_w]
        dka, dva, e_ref, sg_ref = refs[7 + 2 * n_w:11 + 2 * n_w]
        step = pl.program_id(0) * PAIRS + pl.program_id(1)
        begin, end = _rider_hooks(rider, refs[4:4 + n_w], refs[7 + n_w:7 + 2 * n_w], refs[-2:], step, b_dim * PAIRS)
        begin()
        h1, h2 = _head_masks()
        after, before, qrow, kcol = _sb_consts()
        dka[...] = jnp.zeros_like(dka)
        dva[...] = jnp.zeros_like(dva)

        def qloop(qi, _):
            rows = pl.ds(pl.multiple_of(qi * SB_ROWS, SB_ROWS), SB_ROWS)
            q1, q2 = _split_heads(q_ref[rows, :] * scale, h1, h2)
            do1, do2 = _split_heads(do_ref[rows, :].astype(F32), h1, h2)
            first = qi * per

            def pass1(kb, carry, masked):
                run1, run2 = carry
                krows = pl.ds(pl.multiple_of(kb * BLOCK, BLOCK), BLOCK)
                k16 = k_ref[krows, :].astype(BF16)
                v16 = v_ref[krows, :].astype(BF16)
                valid = ((kb - first) * BLOCK + kcol < qrow) if masked else None

                def head(h, qh, doh, run):
                    ls, l1m = _sb_scores(qh, k16, valid)
                    a = jnp.exp(ls + _split_dot(l1m, after) + run)
                    if masked:
                        a = jnp.where(valid, a, 0.0)
                    da = lax.dot_general(doh, v16, NT, preferred_element_type=F32)
                    e_ref[h, kb] = a * da
                    sg_ref[h, kb] = jnp.exp(ls)
                    return a.astype(BF16), run + jnp.sum(l1m, axis=-1, keepdims=True)

                a1, run1 = head(0, q1, do1, run1)
                a2, run2 = head(1, q2, do2, run2)
                dva[krows, :] += (lax.dot_general(a1, do1, TN, preferred_element_type=F32)
                                  + lax.dot_general(a2, do2, TN, preferred_element_type=F32))
                return run1, run2

            zcol = jnp.zeros((SB_ROWS, 1), F32)
            carry = _pairs_loop(per, lambda i, c: pass1(first + per - 1 - i, c, True), (zcol, zcol))
            _pairs_loop(first, lambda i, c: pass1(first - 1 - i, c, False), carry)

            def pass2(kb, carry, masked):
                dq, pre1, pre2 = carry
                krows = pl.ds(pl.multiple_of(kb * BLOCK, BLOCK), BLOCK)
                k1, k2 = _split_heads(k_ref[krows, :], h1, h2)

                def head(h, pre):
                    ev = e_ref[h, kb]
                    sg = sg_ref[h, kb]
                    dz = ev * (1.0 - sg) - (_split_dot(ev, before) + pre) * sg
                    if masked:
                        dz = jnp.where((kb - first) * BLOCK + kcol < qrow, dz, 0.0)
                    return dz.astype(BF16), pre + jnp.sum(ev, axis=-1, keepdims=True)

                dz1, pre1 = head(0, pre1)
                dz2, pre2 = head(1, pre2)
                dka[krows, :] += (lax.dot_general(dz1, q1, TN, preferred_element_type=F32)
                                  + lax.dot_general(dz2, q2, TN, preferred_element_type=F32))
                dq = dq + jnp.dot(dz1, k1, preferred_element_type=F32) + jnp.dot(dz2, k2, preferred_element_type=F32)
                return dq, pre1, pre2

            carry = _pairs_loop(first, lambda i, c: pass2(i, c, False), (jnp.zeros((SB_ROWS, LANES), F32), zcol, zcol))
            dq, _, _ = _pairs_loop(per, lambda i, c: pass2(first + i, c, True), carry)
            dq_ref[rows, :] = (dq * scale).astype(BF16)
            return 0

        lax.fori_loop(0, s_dim // SB_ROWS, qloop, 0)
        dk_ref[...] = dka[...].astype(BF16)
        dv_ref[...] = dva[...].astype(BF16)
        end()

    def col(c0):
        return pl.BlockSpec((None, s_dim, LANES),lambda b, h: (b, 0, c0 + h))

    shp = jax.ShapeDtypeStruct((b_dim, s_dim, ATT_WIDTH), BF16)
    acc = pltpu.VMEM((s_dim, LANES), F32)
    strip = pltpu.VMEM((2, nkb_max, SB_ROWS, BLOCK), F32)
    res = pl.pallas_call(
        body, name="sb_bwd", grid=(b_dim, PAIRS),
        in_specs=[col(COL_QB), col(COL_KB), col(COL_VB), col(0)] + [ANY] * n_w,
        out_specs=[col(0), col(0), col(0)] + [ANY] * n_w,
        out_shape=[shp, shp, shp] + extra_shapes,
        scratch_shapes=[acc, acc, strip, strip] + extra_sems,
        compiler_params=_params(("arbitrary", "arbitrary")),
    )(proj3, proj3, proj3, do3, *extra)
    return res[:3], res[3:]


def _sigmoid(x):
    return 1.0 / (1.0 + jnp.exp(-x))


def _gate_fwd(proj, ua, ub, *, tt=512):
    t_dim, d = ua.shape

    def body(ga_ref, gb_ref, ua_ref, ub_ref, o_ref):
        o_ref[...] = (_sigmoid(ga_ref[...]) * ua_ref[...] + _sigmoid(gb_ref[...]) * ub_ref[...]).astype(BF16)

    row = pl.BlockSpec((tt, d), lambda i: (i, 0))
    return pl.pallas_call(
        body, name="gate_fwd", grid=(t_dim // tt,),
        in_specs=[pl.BlockSpec((tt, d), lambda i: (i, 3)), pl.BlockSpec((tt, d), lambda i: (i, 4)), row, row],
        out_specs=row, out_shape=jax.ShapeDtypeStruct((t_dim, d), BF16),
        compiler_params=_params(("parallel",)),
    )(proj, proj, ua, ub)


def _gate_bwd(proj, ua, ub, dmix, *, tt=512):
    t_dim, d = ua.shape

    def body(ga_ref, gb_ref, ua_ref, ub_ref, dm_ref, dua_ref, dub_ref, dg_ref):
        dm = dm_ref[...]
        sa = _sigmoid(ga_ref[...])
        sb = _sigmoid(gb_ref[...])
        dua_ref[...] = (dm * sa).astype(BF16)
        dub_ref[...] = (dm * sb).astype(BF16)
        dg_ref[:, :d] = (dm * ua_ref[...] * (sa * (1.0 - sa))).astype(BF16)
        dg_ref[:, d:] = (dm * ub_ref[...] * (sb * (1.0 - sb))).astype(BF16)

    row = pl.BlockSpec((tt, d), lambda i: (i, 0))
    wide = pl.BlockSpec((tt, 2 * d), lambda i: (i, 0))
    return pl.pallas_call(
        body, name="gate_bwd", grid=(t_dim // tt,),
        in_specs=[pl.BlockSpec((tt, d), lambda i: (i, 3)), pl.BlockSpec((tt, d), lambda i: (i, 4)), row, row, row],
        out_specs=[row, row, wide],
        out_shape=[jax.ShapeDtypeStruct((t_dim, d), BF16), jax.ShapeDtypeStruct((t_dim, d), BF16),
                   jax.ShapeDtypeStruct((t_dim, 2 * d), BF16)],
        compiler_params=_params(("parallel",)),
    )(proj, proj, ua, ub, dmix)


def _swiglu_fwd(g3, u3, *, tt=1024):
    n, t_dim, f4 = g3.shape

    def body(g_ref, u_ref, o_ref):
        gv = g_ref[...]
        o_ref[...] = (gv * _sigmoid(gv) * u_ref[...]).astype(BF16)

    spec = pl.BlockSpec((None, tt, f4), lambda s, i: (s, i, 0))
    return pl.pallas_call(
        body, name="swiglu_fwd", grid=(n, t_dim // tt), in_specs=[spec, spec], out_specs=spec,
        out_shape=jax.ShapeDtypeStruct(g3.shape, BF16),
        compiler_params=_params(("parallel", "parallel")),
    )(g3, u3)


def _swiglu_bwd(g3, u3, dact3, *, tt=1024):
    n, t_dim, f4 = g3.shape

    def body(g_ref, u_ref, da_ref, dg_ref, du_ref):
        gv = g_ref[...]
        da = da_ref[...]
        sg = _sigmoid(gv)
        dg_ref[...] = (da * u_ref[...] * (sg + gv * sg * (1.0 - sg))).astype(BF16)
        du_ref[...] = (da * (gv * sg)).astype(BF16)

    spec = pl.BlockSpec((None, tt, f4), lambda s, i: (s, i, 0))
    shp = jax.ShapeDtypeStruct(g3.shape, BF16)
    return pl.pallas_call(
        body, name="swiglu_bwd", grid=(n, t_dim // tt), in_specs=[spec, spec, spec], out_specs=[spec, spec],
        out_shape=[shp, shp],
        compiler_params=_params(("parallel", "parallel")),
    )(g3, u3, dact3)


def _mem_fwd(qm, kvm, *, tt=512):
    b_dim, s_dim, _ = qm.shape
    n_mem = kvm.shape[1]
    scale = MEM_HEAD_DIM ** -0.5

    def body(q_ref, k_ref, v_ref, o_ref):
        sc = lax.dot_general(q_ref[0], k_ref[0], NT, preferred_element_type=F32) * scale
        p = jnp.exp(sc - jnp.max(sc, axis=-1, keepdims=True))
        p = p / jnp.sum(p, axis=-1, keepdims=True)
        o_ref[0] = jnp.dot(p.astype(BF16), v_ref[0], preferred_element_type=F32).astype(BF16)

    qs = pl.BlockSpec((1, tt, MEM_HEAD_DIM), lambda b, h, i: (b, i, h))
    return pl.pallas_call(
        body, name="mem_fwd", grid=(b_dim, N_HEADS_MEM, s_dim // tt),
        in_specs=[qs, pl.BlockSpec((1, n_mem, MEM_HEAD_DIM), lambda b, h, i: (b, 0, h)),
                  pl.BlockSpec((1, n_mem, MEM_HEAD_DIM), lambda b, h, i: (b, 0, N_HEADS_MEM + h))],
        out_specs=qs, out_shape=jax.ShapeDtypeStruct(qm.shape, BF16),
        compiler_params=_params(("parallel", "parallel", "parallel")),
    )(qm, kvm, kvm)


def _mem_bwd(qm, kvm, dom, *, tt=512):
    b_dim, s_dim, _ = qm.shape
    n_mem = kvm.shape[1]
    scale = MEM_HEAD_DIM ** -0.5

    def body(q_ref, k_ref, v_ref, do_ref, dq_ref, dk_ref, dv_ref):
        qv, kv, vv, dov = q_ref[0], k_ref[0], v_ref[0], do_ref[0]
        sc = lax.dot_general(qv, kv, NT, preferred_element_type=F32) * scale
        p = jnp.exp(sc - jnp.max(sc, axis=-1, keepdims=True))
        p = p / jnp.sum(p, axis=-1, keepdims=True)
        dp = lax.dot_general(dov, vv, NT, preferred_element_type=F32)
        ds = (p * (dp - jnp.sum(p * dp, axis=-1, keepdims=True)) * scale).astype(BF16)
        dq_ref[0] = jnp.dot(ds, kv, preferred_element_type=F32).astype(BF16)

        @pl.when(pl.program_id(2) == 0)
        def _():
            dk_ref[...] = jnp.zeros_like(dk_ref)
            dv_ref[...] = jnp.zeros_like(dv_ref)

        dk_ref[0] += lax.dot_general(ds, qv, TN, preferred_element_type=F32)
        dv_ref[0] += lax.dot_general(p.astype(BF16), dov, TN, preferred_element_type=F32)

    qs = pl.BlockSpec((1, tt, MEM_HEAD_DIM), lambda b, h, i: (b, i, h))
    ks = pl.BlockSpec((1, n_mem, MEM_HEAD_DIM), lambda b, h, i: (b, 0, h))
    vs = pl.BlockSpec((1, n_mem, MEM_HEAD_DIM), lambda b, h, i: (b, 0, N_HEADS_MEM + h))
    return pl.pallas_call(
        body, name="mem_bwd", grid=(b_dim, N_HEADS_MEM, s_dim // tt),
        in_specs=[qs, ks, vs, qs], out_specs=[qs, ks, ks],
        out_shape=[jax.ShapeDtypeStruct(qm.shape, BF16), jax.ShapeDtypeStruct((b_dim, n_mem, MEM_WIDTH), F32),
                   jax.ShapeDtypeStruct((b_dim, n_mem, MEM_WIDTH), F32)],
        compiler_params=_params(("parallel", "parallel", "arbitrary")),
    )(qm, kvm, kvm, dom)


def _adamw_math(wv, gv, mv, vv):
    nm = ADAM_B1 * mv + (1.0 - ADAM_B1) * gv
    nv = ADAM_B2 * vv + (1.0 - ADAM_B2) * (gv * gv)
    m_hat = nm / (1.0 - ADAM_B1 ** ADAM_STEP)
    v_hat = nv / (1.0 - ADAM_B2 ** ADAM_STEP)
    return -ADAM_LR * (m_hat / (jnp.sqrt(v_hat) + ADAM_EPS) + ADAM_WD * wv), nm, nv


def _adamw(w, g, m, v, *, name):
    rows, cols = w.shape
    tr = _tile(rows, 256, 8)

    def body(w_ref, g_ref, m_ref, v_ref, d_ref, nm_ref, nv_ref):
        d_ref[...], nm_ref[...], nv_ref[...] = _adamw_math(w_ref[...], g_ref[...], m_ref[...], v_ref[...])

    spec = pl.BlockSpec((tr, cols), lambda i: (i, 0))
    shp = jax.ShapeDtypeStruct((rows, cols), F32)
    return pl.pallas_call(
        body, name=name, grid=(rows // tr,),
        in_specs=[spec] * 4, out_specs=[spec] * 3, out_shape=[shp] * 3,
        compiler_params=_params(("parallel",)),
    )(w, g, m, v)


def _prefetch_spec(grid, in_specs, out_specs):
    return pltpu.PrefetchScalarGridSpec(num_scalar_prefetch=1, grid=grid, in_specs=in_specs, out_specs=out_specs)


def _adamw_halves(w, mine, theirs, m, v, c_idx, *, name):
    rows, cols = w.shape
    half = rows // 2
    tr = _tile(half, 256, 8)
    nh = half // tr

    def body(c_ref, w_ref, mine_ref, theirs_ref, m_ref, v_ref, g_ref, d_ref, nm_ref, nv_ref):
        gv = jnp.where(pl.program_id(0) == c_ref[0], mine_ref[...], theirs_ref[...])
        g_ref[...] = gv
        d_ref[...], nm_ref[...], nv_ref[...] = _adamw_math(w_ref[...], gv, m_ref[...], v_ref[...])

    full = pl.BlockSpec((tr, cols), lambda h, i, c_ref: (h * nh + i, 0))
    part = pl.BlockSpec((tr, cols), lambda h, i, c_ref: (i, 0))
    shp = jax.ShapeDtypeStruct((rows, cols), F32)
    return pl.pallas_call(
        body, name=name, grid_spec=_prefetch_spec((2, nh), [full, part, part, full, full], [full] * 4),
        out_shape=[shp] * 4,
        compiler_params=_params(("parallel", "parallel")),
    )(c_idx, w, mine, theirs, m, v)


def _pair_sum(g3, theirs, c_idx, *, name):
    n, rows, cols = g3.shape
    half = rows // 2
    tr = _tile(half, 256, 16)

    def body(c_ref, g_ref, t_ref, o_ref):
        o_ref[...] = (g_ref[...] + t_ref[...]).astype(BF16)

    part = pl.BlockSpec((None, tr, cols), lambda s, i, c_ref: (s, i, 0))
    return pl.pallas_call(
        body, name=name,
        grid_spec=_prefetch_spec((n, half // tr),
                                 [pl.BlockSpec((None, None, tr, cols), lambda s, i, c_ref: (s, c_ref[0], i, 0)), part],
                                 part),
        out_shape=jax.ShapeDtypeStruct((n, half, cols), BF16),
        compiler_params=_params(("parallel", "parallel")),
    )(c_idx, g3.reshape(n, 2, half, cols), theirs)


def _chip_sum(pair, recv, s_idx, *, name):
    _, half, cols = pair.shape
    tr = _tile(half, 256, 16)

    def body(s_ref, p_ref, r_ref, o_ref):
        o_ref[...] = ((p_ref[...].astype(F32) + r_ref[0].astype(F32)) + r_ref[1].astype(F32)) + r_ref[2].astype(F32)

    return pl.pallas_call(
        body, name=name,
        grid_spec=_prefetch_spec((half // tr,),
                                 [pl.BlockSpec((None, tr, cols), lambda i, s_ref: (s_ref[0], i, 0)),
                                  pl.BlockSpec((N_CHIPS - 1, tr, cols), lambda i, s_ref: (0, i, 0))],
                                 pl.BlockSpec((tr, cols), lambda i, s_ref: (i, 0))),
        out_shape=jax.ShapeDtypeStruct((half, cols), F32),
        compiler_params=_params(("parallel",)),
    )(s_idx, pair, recv)


def _sum8(parts):
    n, rows, cols = parts.shape

    def body(p_ref, o_ref):
        acc = p_ref[0]
        for i in range(1, n):
            acc = acc + p_ref[i]
        o_ref[...] = acc

    return pl.pallas_call(
        body, name="small_sum", grid=(1,),
        in_specs=[pl.BlockSpec((n, rows, cols), lambda i: (0, 0, 0))],
        out_specs=pl.BlockSpec((rows, cols), lambda i: (0, 0)),
        out_shape=jax.ShapeDtypeStruct((rows, cols), parts.dtype),
        compiler_params=_params(("arbitrary",)),
    )(parts)


def _place():
    return lax.axis_index("x"), lax.axis_index("y"), lax.axis_index("c")


ANY = pl.BlockSpec(memory_space=pl.ANY)


def _rider_parts(rider):
    if rider is None:
        return (), [], []
    kind, arrays = rider
    n = len(arrays)
    shapes = {"gather": _gathered_shapes, "pair": _pair_shapes, "chip": _chip_shapes}[kind](arrays)
    sems = _gather_sems(n) if kind == "gather" else _exchange_sems(n if kind == "pair" else 3 * n)
    return tuple(arrays), shapes, sems


def _rider_hooks(rider, ins, outs, sems, step, n_steps):
    if rider is None:
        return (lambda: None), (lambda: None)
    if rider[0] == "gather":
        start, forward, finish = _gather_steps(ins, outs, *sems)
    else:
        start, finish = {"pair": _pair_steps, "chip": _chip_steps}[rider[0]](ins, outs, *sems)
        forward = None

    def begin():
        pl.when(step == 0)(start)

    def end():
        if forward is not None:
            pl.when(step == n_steps - 2)(forward)
        pl.when(step == n_steps - 1)(finish)

    return begin, end


def _gather_weights(shards):
    n = len(shards)

    def body(*refs):
        start, forward, finish = _gather_steps(refs[:n], refs[n:2 * n], refs[2 * n], refs[2 * n + 1])
        start()
        forward()
        finish()

    return pl.pallas_call(
        body, name="gather_weights", out_shape=_gathered_shapes(shards),
        in_specs=[ANY] * n, out_specs=[ANY] * n, scratch_shapes=_gather_sems(n),
    )(*shards)


def _gathered_shapes(shards):
    return [jax.ShapeDtypeStruct((N_CHIPS,) + s.shape, s.dtype) for s in shards]


def _gather_sems(n):
    return [pltpu.SemaphoreType.DMA((7 * n,)), pltpu.SemaphoreType.DMA((7 * n,))]


def _gather_steps(ins, outs, send_sems, recv_sems):
    n = len(ins)
    halves = [r.shape[0] // 2 for r in ins]
    x, y, c = _place()
    my_chip = 2 * x + y
    me, sibling = (x, y, c), (x, y, 1 - c)
    chips = [(1 - x, y), (x, 1 - y), (1 - x, 1 - y)]

    def half_of(w, chip, pc):
        return outs[w].at[chip, pl.ds(pc * halves[w], halves[w]), :]

    def copy(w, k, src, dst, to):
        return pltpu.make_async_remote_copy(
            src_ref=src, dst_ref=dst, send_sem=send_sems.at[7 * w + k], recv_sem=recv_sems.at[7 * w + k],
            device_id=to, device_id_type=MESH)

    def firsts():
        cps = []
        for w in range(n):
            cps.append(copy(w, 0, ins[w], outs[w].at[my_chip], sibling))
            mine = ins[w].at[pl.ds(c * halves[w], halves[w]), :]
            for j, (px, py) in enumerate(chips):
                cps.append(copy(w, 1 + j, mine, half_of(w, my_chip, c), (px, py, c)))
        return cps

    def passes():
        return [copy(w, 4 + j, half_of(w, 2 * px + py, c), half_of(w, 2 * px + py, c), sibling)
                for w in range(n) for j, (px, py) in enumerate(chips)]

    def start():
        for cp in firsts():
            cp.start()

    def forward():
        fws = passes()
        for w in range(n):
            for j, (px, py) in enumerate(chips):
                landed = half_of(w, 2 * px + py, c)
                copy(w, 1 + j, landed, landed, me).wait_recv()
                fws[3 * w + j].start()

    def finish():
        for w in range(n):
            copy(w, 0, ins[w], outs[w].at[my_chip], me).wait_recv()
            for j, (px, py) in enumerate(chips):
                landed = half_of(w, 2 * px + py, 1 - c)
                copy(w, 4 + j, landed, landed, me).wait_recv()
        for cp in firsts() + passes():
            cp.wait_send()

    return start, forward, finish


def _pair_exchange(grads):
    n = len(grads)

    def body(*refs):
        start, finish = _pair_steps(refs[:n], refs[n:2 * n], refs[2 * n], refs[2 * n + 1])
        start()
        finish()

    return pl.pallas_call(
        body, name="grad_pair_exchange", out_shape=_pair_shapes(grads),
        in_specs=[ANY] * n, out_specs=[ANY] * n, scratch_shapes=_exchange_sems(n),
    )(*grads)


def _pair_shapes(grads):
    return [jax.ShapeDtypeStruct((g.shape[0], g.shape[1] // 2, g.shape[2]), g.dtype) for g in grads]


def _exchange_sems(n):
    return [pltpu.SemaphoreType.DMA((n,)), pltpu.SemaphoreType.DMA((n,))]


def _exchange_steps(copies):
    def start():
        for cp in copies():
            cp.start()

    def finish():
        for cp in copies():
            cp.wait()

    return start, finish


def _pair_steps(ins, outs, send_sems, recv_sems):
    x, y, c = _place()

    def copies():
        return [pltpu.make_async_remote_copy(
            src_ref=ins[w].at[:, pl.ds((1 - c) * (ins[w].shape[1] // 2), ins[w].shape[1] // 2), :], dst_ref=outs[w],
            send_sem=send_sems.at[w], recv_sem=recv_sems.at[w], device_id=(x, y, 1 - c), device_id_type=MESH)
            for w in range(len(ins))]

    return _exchange_steps(copies)


def _chip_exchange(pairs):
    n = len(pairs)

    def body(*refs):
        start, finish = _chip_steps(refs[:n], refs[n:2 * n], refs[2 * n], refs[2 * n + 1])
        start()
        finish()

    return pl.pallas_call(
        body, name="grad_chip_exchange", out_shape=_chip_shapes(pairs),
        in_specs=[ANY] * n, out_specs=[ANY] * n, scratch_shapes=_exchange_sems(3 * n),
    )(*pairs)


def _chip_shapes(pairs):
    return [jax.ShapeDtypeStruct((N_CHIPS - 1,) + p.shape[1:], p.dtype) for p in pairs]


def _chip_steps(ins, outs, send_sems, recv_sems):
    x, y, c = _place()
    others = [(1 - x, y), (x, 1 - y), (1 - x, 1 - y)]

    def copies():
        return [pltpu.make_async_remote_copy(
            src_ref=ins[w].at[2 * px + py], dst_ref=outs[w].at[j],
            send_sem=send_sems.at[3 * w + j], recv_sem=recv_sems.at[3 * w + j],
            device_id=(px, py, c), device_id_type=MESH)
            for w in range(len(ins)) for j, (px, py) in enumerate(others)]

    return _exchange_steps(copies)


def _swap_halves(mine):
    n = len(mine)

    def body(*refs):
        ins, outs, send_sems, recv_sems = refs[:n], refs[n:2 * n], refs[2 * n], refs[2 * n + 1]
        x, y, c = _place()
        copies = [pltpu.make_async_remote_copy(
            src_ref=ins[w], dst_ref=outs[w], send_sem=send_sems.at[w], recv_sem=recv_sems.at[w],
            device_id=(x, y, 1 - c), device_id_type=MESH) for w in range(n)]
        for cp in copies:
            cp.start()
        for cp in copies:
            cp.wait()

    return pl.pallas_call(
        body, name="grad_swap_halves",
        out_shape=[jax.ShapeDtypeStruct(h.shape, h.dtype) for h in mine],
        in_specs=[ANY] * n, out_specs=[ANY] * n,
        scratch_shapes=[pltpu.SemaphoreType.DMA((n,)), pltpu.SemaphoreType.DMA((n,))],
    )(*mine)


def _gather_small(small):
    srows, cols = small.shape

    def body(s_ref, all_ref, send_sems, recv_sems, local_sem):
        x, y, c = _place()
        me = 4 * x + 2 * y + c
        keep_small = pltpu.make_async_copy(s_ref, all_ref.at[me], local_sem)
        keep_small.start()
        sends = []
        for kk in range(1, 8):
            peer = (x ^ (kk >> 2), y ^ ((kk >> 1) & 1), c ^ (kk & 1))
            sends.append(pltpu.make_async_remote_copy(
                src_ref=s_ref, dst_ref=all_ref.at[me],
                send_sem=send_sems.at[kk], recv_sem=recv_sems.at[kk], device_id=peer, device_id_type=MESH))
        for cp in sends:
            cp.start()
        for kk in range(1, 8):
            px, py, pc = x ^ (kk >> 2), y ^ ((kk >> 1) & 1), c ^ (kk & 1)
            pltpu.make_async_remote_copy(
                src_ref=s_ref, dst_ref=all_ref.at[4 * px + 2 * py + pc],
                send_sem=send_sems.at[kk], recv_sem=recv_sems.at[kk], device_id=(px, py, pc),
                device_id_type=MESH).wait_recv()
        for cp in sends:
            cp.wait_send()
        keep_small.wait()

    return pl.pallas_call(
        body, name="gather_small",
        out_shape=jax.ShapeDtypeStruct((8, srows, cols), small.dtype),
        in_specs=[ANY], out_specs=ANY,
        scratch_shapes=[pltpu.SemaphoreType.DMA((8,)), pltpu.SemaphoreType.DMA((8,)), pltpu.SemaphoreType.DMA],
    )(small)


SHARDED = (("w_in", D_MODEL, IN_COLS, 1), ("w_up_a", ATT_WIDTH, D_MODEL, 1), ("w_up_b", ATT_WIDTH, D_MODEL, 1),
           ("w_out", D_MODEL, D_MODEL, 0), ("w_q_mem", D_MODEL, MEM_WIDTH, 0), ("w_kv_mem", D_MODEL, 2 * MEM_WIDTH, 0),
           ("w_o_mem", MEM_WIDTH, D_MODEL, 1), ("w_ffn_gate", D_MODEL, D_FF, 1), ("w_ffn_up", D_MODEL, D_FF, 1),
           ("w_ffn_down", D_FF, D_MODEL, 0))
NAMES = tuple(n for n, _, _, _ in SHARDED)
EARLY, LATE = NAMES[:1], NAMES[1:]
GAINS = ("g_mix", "g_mem_q", "g_mem_kv", "g_ffn", "g_final")


def _natural(w3):
    n, r, c = w3.shape
    return w3.reshape(n * r, c)


def _shard_major(g, axis):
    if axis == 1:
        return g
    r, c = g.shape
    return g.reshape(N_CHIPS, r // N_CHIPS, c)


def kernel(x, mem, positions, g_mix, w_in, w_up_a, w_up_b, w_out, g_mem_q, g_mem_kv, w_q_mem, w_kv_mem, w_o_mem, g_ffn, w_ffn_gate, w_ffn_up, w_ffn_down, g_final, loss_target, m_g_mix, m_w_in, m_w_up_a, m_w_up_b, m_w_out, m_g_mem_q, m_g_mem_kv, m_w_q_mem, m_w_kv_mem, m_w_o_mem, m_g_ffn, m_w_ffn_gate, m_w_ffn_up, m_w_ffn_down, m_g_final, v_g_mix, v_w_in, v_w_up_a, v_w_up_b, v_w_out, v_g_mem_q, v_g_mem_kv, v_w_q_mem, v_w_kv_mem, v_w_o_mem, v_g_ffn, v_w_ffn_gate, v_w_ffn_up, v_w_ffn_down, v_g_final):
    given = dict(locals())
    shards = {n: given[n][0] for n, _, _, _ in SHARDED}

    wf = dict(zip(EARLY, _gather_weights([shards[n].astype(BF16) for n in EARLY])))
    late_shards = [shards[n].astype(BF16) for n in LATE]
    c_idx = lax.axis_index("c").astype(jnp.int32).reshape(1)
    s_idx = (2 * lax.axis_index("x") + lax.axis_index("y")).astype(jnp.int32).reshape(1)

    loss_row, grad_x, mine, gain_grads = _local_step(x, mem, positions, loss_target, g_mix, g_mem_q, g_mem_kv,
                                                     g_ffn, g_final, wf, late_shards, (c_idx, s_idx))
    return _reduce_and_update(given, shards, loss_row, grad_x, mine, gain_grads, c_idx)


def _reduce_halves(glist, names, c_idx, s_idx, pair_exchange, chip_exchange):
    theirs = pair_exchange(glist)
    pairs = [_pair_sum(g, t, c_idx, name="pair_sum_" + n) for n, g, t in zip(names, glist, theirs)]
    recv = chip_exchange(pairs)
    return [_chip_sum(p, r, s_idx, name="chip_sum_" + n) for n, p, r in zip(names, pairs, recv)]


def _local_step(x, mem, positions, loss_target, g_mix, g_mem_q, g_mem_kv, g_ffn, g_final, wf,
                late_shards=None, place=None):
    b_dim, s_dim, d = x.shape
    t_dim = b_dim * s_dim
    n_mem = mem.shape[1]
    wf = dict(wf)

    xb = x.reshape(t_dim, d)
    tgt = loss_target.reshape(t_dim, d)
    memf = mem.reshape(b_dim * n_mem, d)
    gfin = g_final.reshape(1, d)
    pos = positions.reshape(t_dim, 1).astype(F32)

    lane = jnp.arange(LANES) % HEAD_DIM
    half = ROPE_DIM // 2
    inv_freq = ROPE_THETA ** (-jnp.arange(half, dtype=F32) / half)
    inv_lane = jnp.where(lane < ROPE_DIM, inv_freq[lane % half], 0.0).reshape(1, -1).astype(F32)
    sel_a = (lane < half).astype(F32).reshape(1, -1)
    sel_b = ((lane >= half) & (lane < ROPE_DIM)).astype(F32).reshape(1, -1)

    def rows3(t):
        return t.reshape(b_dim, s_dim, t.shape[-1])

    def rows2(t):
        return t.reshape(t_dim, t.shape[-1])

    n1 = _rms_fwd(xb, g_mix, name="rms_mix")
    proj = _mm_cs(n1, wf["w_in"], name="mm_in")
    proj3 = rows3(proj)
    cs, sn = _rope_table(pos, inv_lane, sel_a, sel_b)
    cs3, sn3 = rows3(cs), rows3(sn)
    (oa16, oa32, lse_a), _ = _dil_fwd(proj3, cs3, sn3, sel_a, sel_b)
    ob16, gathered = _sb_fwd(proj3, ("gather", late_shards) if late_shards else None)
    wf.update(zip(LATE, gathered))
    w_out, w_q, w_kv = _natural(wf["w_out"]), _natural(wf["w_q_mem"]), _natural(wf["w_kv_mem"])
    oa, ob = rows2(oa16), rows2(ob16)
    ua = _mm_cs(oa, wf["w_up_a"], name="mm_up_a")
    ub = _mm_cs(ob, wf["w_up_b"], name="mm_up_b")
    mixed = _gate_fwd(proj, ua, ub)
    h1 = _mm(mixed, w_out, name="mm_out", add=xb)

    hn = _rms_fwd(h1, g_mem_q, name="rms_mem_q")
    memn = _rms_fwd(memf, g_mem_kv, name="rms_mem_kv")
    qm = _mm(hn, w_q, name="mm_q_mem", out_dtype=BF16)
    kvm = _mm(memn, w_kv, name="mm_kv_mem", out_dtype=BF16)
    qm3, kvm3 = rows3(qm), kvm.reshape(b_dim, n_mem, 2 * MEM_WIDTH)
    om = rows2(_mem_fwd(qm3, kvm3))
    h2 = _mm_cs(om, wf["w_o_mem"], name="mm_o_mem", add=h1)

    n3 = _rms_fwd(h2, g_ffn, name="rms_ffn")
    gate3 = _mm_ffn_up(n3, wf["w_ffn_gate"], name="mm_gate")
    up3 = _mm_ffn_up(n3, wf["w_ffn_up"], name="mm_up")
    act3 = _swiglu_fwd(gate3, up3)
    h3 = _mm_ffn_down(act3, wf["w_ffn_down"], name="mm_down", add=h2)
    loss_row, dh3, dg_final = _final(h3, gfin, tgt)

    grads = {}
    dact3 = _mm_ffn_down_dx(dh3, wf["w_ffn_down"], name="mm_down_dx")
    grads["w_ffn_down"] = _mm_ffn_down_dw(act3, dh3, name="mm_down_dw")
    dgate3, dup3 = _swiglu_bwd(gate3, up3, dact3)
    grads["w_ffn_gate"] = _mm_ffn_up_dw(n3, dgate3, name="mm_gate_dw")
    grads["w_ffn_up"] = _mm_ffn_up_dw(n3, dup3, name="mm_up_dw")
    dn3 = _mm_ffn_up_dx(dgate3, wf["w_ffn_gate"], name="mm_gate_dx")
    dn3 = _mm_ffn_up_dx(dup3, wf["w_ffn_up"], name="mm_up_dx", add=dn3)
    dh2, dg_ffn = _rms_bwd(h2, g_ffn, dn3, dh3, name="rms_ffn_bwd")

    dom = _mm_cs_dx(dh2, wf["w_o_mem"], name="mm_o_mem_dx", out_dtype=BF16)
    grads["w_o_mem"] = _mm_cs_dw(om, dh2, name="mm_o_mem_dw")
    dqm, dkm, dvm = _mem_bwd(qm3, kvm3, rows3(dom))
    dqm = rows2(dqm)
    dkvm = jnp.concatenate([dkm, dvm], axis=-1).reshape(b_dim * n_mem, 2 * MEM_WIDTH).astype(BF16)
    grads["w_q_mem"] = _shard_major(_mm(hn, dqm, name="mm_q_mem_dw", ta=True), 0)
    dhn = _mm(dqm, w_q, name="mm_q_mem_dx", tb=True)
    grads["w_kv_mem"] = _shard_major(_mm(memn, dkvm, name="mm_kv_mem_dw", ta=True), 0)
    dmemn = _mm(dkvm, w_kv, name="mm_kv_mem_dx", tb=True)
    _, dg_mem_kv = _rms_bwd(memf, g_mem_kv, dmemn, None, name="rms_mem_kv_bwd")
    dh1, dg_mem_q = _rms_bwd(h1, g_mem_q, dhn, dh2, name="rms_mem_q_bwd")

    dmix = _mm(dh1, w_out, name="mm_out_dx", tb=True)
    grads["w_out"] = _shard_major(_mm(mixed, dh1, name="mm_out_dw", ta=True), 0)
    dua, dub, dgates = _gate_bwd(proj, ua, ub, dmix)
    doa = _mm_cs_dx(dua, wf["w_up_a"], name="mm_up_a_dx")
    grads["w_up_a"] = _mm_cs_dw(oa, dua, name="mm_up_a_dw")
    dob = _mm_cs_dx(dub, wf["w_up_b"], name="mm_up_b_dx", out_dtype=BF16)
    grads["w_up_b"] = _mm_cs_dw(ob, dub, name="mm_up_b_dw")

    att = {}

    def dil_with_pairs(glist):
        att["a"], theirs = _dil_bwd(proj3, cs3, sn3, sel_a, sel_b, rows3(doa), oa32, lse_a,
                                    ("pair", glist) if glist else None)
        return theirs

    def sb_with_chips(pairs):
        att["b"], recv = _sb_bwd(proj3, rows3(dob), ("chip", pairs) if pairs else None)
        return recv

    if place is None:
        dil_with_pairs(())
        sb_with_chips(())
    else:
        mine_late = _reduce_halves([grads[n] for n in LATE], LATE, *place, dil_with_pairs, sb_with_chips)
    dproj = jnp.concatenate([rows2(t) for t in att["a"] + att["b"]] + [dgates], axis=1)
    grads["w_in"] = _mm_cs_dw(n1, dproj, name="mm_in_dw")
    dn1 = _mm_cs_dx(dproj, wf["w_in"], name="mm_in_dx")
    dx, dg_mix = _rms_bwd(xb, g_mix, dn1, dh1, name="rms_mix_bwd")
    grad_x = dx.reshape(b_dim, s_dim, d)
    gains = (dg_mix, dg_mem_q, dg_mem_kv, dg_ffn, dg_final)
    if place is None:
        return loss_row, grad_x, grads, gains
    mine_early = _reduce_halves([grads[n] for n in EARLY], EARLY, *place, _pair_exchange, _chip_exchange)
    return loss_row, grad_x, mine_early + mine_late, gains


def _reduce_and_update(given, shards, loss_row, grad_x, mine, gain_grads, c_idx):
    d = D_MODEL
    dg_mix, dg_mem_q, dg_mem_kv, dg_ffn, dg_final = gain_grads
    small = jnp.concatenate([dg_mix, dg_mem_q, dg_mem_kv, dg_ffn, dg_final,
                             jnp.pad(loss_row, ((0, 0), (0, FLAT_COLS - LANES))), jnp.zeros((2, FLAT_COLS), F32)], axis=0)
    small_all = _gather_small(small)
    others = _swap_halves(mine)
    small_sum = _sum8(small_all)
    loss = small_sum[5, 0]

    out_g, out_d, out_m, out_v = {}, {}, {}, {}
    for n, mine_n, other_n in zip(NAMES, mine, others):
        g2, dl, nm, nv = _adamw_halves(shards[n], mine_n, other_n, given["m_" + n][0], given["v_" + n][0], c_idx,
                                       name="adamw_" + n)
        out_g[n], out_d[n], out_m[n], out_v[n] = g2[None], dl[None], nm[None], nv[None]
    gain_w = jnp.concatenate([given[n].reshape(1, d) for n in GAINS], axis=0)
    gain_m = jnp.concatenate([given["m_" + n].reshape(1, d) for n in GAINS], axis=0)
    gain_v = jnp.concatenate([given["v_" + n].reshape(1, d) for n in GAINS], axis=0)
    gain_g = small_sum[:len(GAINS)]
    gd, gm, gv = _adamw(gain_w, gain_g, gain_m, gain_v, name="adamw_gains")
    for i, n in enumerate(GAINS):
        shape = given[n].shape
        out_g[n], out_d[n] = gain_g[i].reshape(shape), gd[i].reshape(shape)
        out_m[n], out_v[n] = gm[i].reshape(shape), gv[i].reshape(shape)

    order = ["g_mix", "w_in", "w_up_a", "w_up_b", "w_out", "g_mem_q", "g_mem_kv", "w_q_mem", "w_kv_mem", "w_o_mem",
             "g_ffn", "w_ffn_gate", "w_ffn_up", "w_ffn_down", "g_final"]
    return (loss, grad_x, *[out_g[n] for n in order], *[out_d[n] for n in order],
            *[out_m[n] for n in order], *[out_v[n] for n in order])
```

```python
import jax
import jax.numpy as jnp
from jax import lax
from jax.experimental import pallas as pl
from jax.experimental.pallas import tpu as pltpu

F32 = jnp.float32
BF16 = jnp.bfloat16
MESH = pl.DeviceIdType.MESH

D_MODEL = 1024
HEAD_DIM = 64
N_HEADS = 8
ATT_WIDTH = N_HEADS * HEAD_DIM
DIL_PATTERNS = ((128, 1), (512, 4), (2048, 16))
BLOCK = 128
SB_ROWS = 512
SB_STEP = 4
ROPE_THETA = 500000.0
ROPE_DIM = HEAD_DIM // 4
N_HEADS_MEM = 4
MEM_HEAD_DIM = 128
MEM_WIDTH = N_HEADS_MEM * MEM_HEAD_DIM
D_FF = 2816
IN_COLS = 6 * ATT_WIDTH + 2 * D_MODEL
RMS_EPS = 1e-6
ADAM_LR = 0.001
ADAM_B1 = 0.9
ADAM_B2 = 0.999
ADAM_EPS = 1e-08
ADAM_WD = 0.01
ADAM_STEP = 10

N_CHIPS = 4
LANES = 128
FLAT_COLS = 1024
VMEM_LIMIT = 56 * 1024 * 1024

PAIRS = ATT_WIDTH // LANES
COL_QA, COL_KA, COL_VA, COL_QB, COL_KB, COL_VB = (i * PAIRS for i in range(6))

MM_CAP = 1408
TOK_CAP = 2048
NN = (((1,), (0,)), ((), ()))
NT = (((1,), (1,)), ((), ()))
TN = (((0,), (0,)), ((), ()))
BNN = (((2,), (1,)), ((0,), (0,)))
BNT = (((2,), (2,)), ((0,), (0,)))
BTN = (((1,), (1,)), ((0,), (0,)))
DIL_BATCH = 4


def _tile(dim, cap, unit=LANES):
    if dim <= cap:
        return dim
    best = None
    for t in range(unit, cap + 1, unit):
        if dim % t == 0:
            best = t
    assert best is not None, (dim, cap)
    return best


def _params(sem):
    return pltpu.CompilerParams(dimension_semantics=sem, vmem_limit_bytes=VMEM_LIMIT)


def _mm(a, b, *, name, ta=False, tb=False, add=None, out_dtype=F32,
        tm_cap=MM_CAP, tn_cap=MM_CAP, tk_cap=MM_CAP):
    if ta:
        k_dim, m_dim = a.shape
    else:
        m_dim, k_dim = a.shape
    if tb:
        n_dim, kb = b.shape
    else:
        kb, n_dim = b.shape
    assert kb == k_dim, (a.shape, b.shape, ta, tb)
    tm, tn, tk = _tile(m_dim, tm_cap), _tile(n_dim, tn_cap), _tile(k_dim, tk_cap)
    nk = k_dim // tk
    dims = (((0 if ta else 1,), (1 if tb else 0,)), ((), ()))
    has_add = add is not None

    def body(*refs):
        if has_add:
            a_ref, b_ref, add_ref, o_ref = refs[:4]
        else:
            a_ref, b_ref, o_ref = refs[:3]
        part = lax.dot_general(a_ref[...].astype(BF16), b_ref[...].astype(BF16), dims, preferred_element_type=F32)

        def finish(r):
            if has_add:
                r = add_ref[...] + r
            o_ref[...] = r.astype(out_dtype)

        if nk == 1:
            finish(part)
            return
        acc_ref = refs[-1]
        k = pl.program_id(2)

        @pl.when(k == 0)
        def _():
            acc_ref[...] = part

        @pl.when(k > 0)
        def _():
            acc_ref[...] += part

        @pl.when(k == nk - 1)
        def _():
            finish(acc_ref[...])

    a_spec = pl.BlockSpec((tk, tm), lambda i, j, k: (k, i)) if ta else pl.BlockSpec((tm, tk), lambda i, j, k: (i, k))
    b_spec = pl.BlockSpec((tn, tk), lambda i, j, k: (j, k)) if tb else pl.BlockSpec((tk, tn), lambda i, j, k: (k, j))
    o_spec = pl.BlockSpec((tm, tn), lambda i, j, k: (i, j))
    in_specs = [a_spec, b_spec] + ([o_spec] if has_add else [])
    args = (a, b) + ((add,) if has_add else ())
    return pl.pallas_call(
        body, name=name, grid=(m_dim // tm, n_dim // tn, nk),
        in_specs=in_specs, out_specs=o_spec,
        out_shape=jax.ShapeDtypeStruct((m_dim, n_dim), out_dtype),
        scratch_shapes=[pltpu.VMEM((tm, tn), F32)] if nk > 1 else [],
        compiler_params=_params(("parallel", "parallel", "arbitrary")),
    )(*args)


def _mm_core(name, a, b, a_spec, b_spec, o_spec, out_shape, grid, dims, *, add=None, out_dtype=F32):
    nk = grid[2]
    has_add = add is not None
    acc_shape = tuple(d for d in o_spec.block_shape if d is not None)

    def body(*refs):
        if has_add:
            a_ref, b_ref, add_ref, o_ref = refs[:4]
        else:
            a_ref, b_ref, o_ref = refs[:3]
        part = lax.dot_general(a_ref[...].astype(BF16), b_ref[...].astype(BF16), dims, preferred_element_type=F32)

        def finish(r):
            if has_add:
                r = add_ref[...] + r
            o_ref[...] = r.astype(out_dtype)

        if nk == 1:
            finish(part)
            return
        acc_ref = refs[-1]
        k = pl.program_id(2)

        @pl.when(k == 0)
        def _():
            acc_ref[...] = part

        @pl.when(k > 0)
        def _():
            acc_ref[...] += part

        @pl.when(k == nk - 1)
        def _():
            finish(acc_ref[...])

    in_specs = [a_spec, b_spec] + ([o_spec] if has_add else [])
    args = (a, b) + ((add,) if has_add else ())
    return pl.pallas_call(
        body, name=name, grid=grid, in_specs=in_specs, out_specs=o_spec,
        out_shape=jax.ShapeDtypeStruct(out_shape, out_dtype),
        scratch_shapes=[pltpu.VMEM(acc_shape, F32)] if nk > 1 else [],
        compiler_params=_params(("parallel", "parallel", "arbitrary")),
    )(*args)


def _mm_cs(a, w3, *, name, add=None, out_dtype=F32):
    m_dim, k_dim = a.shape
    _, _, n4 = w3.shape
    tm, tn, tk = _tile(m_dim, MM_CAP if add is not None else TOK_CAP), _tile(n4, MM_CAP), _tile(k_dim, MM_CAP)
    npb = n4 // tn
    return _mm_core(name, a, w3,
                    pl.BlockSpec((tm, tk), lambda i, j, k: (i, k)),
                    pl.BlockSpec((None, tk, tn), lambda i, j, k: (j // npb, k, j % npb)),
                    pl.BlockSpec((tm, tn), lambda i, j, k: (i, j)),
                    (m_dim, N_CHIPS * n4), (m_dim // tm, N_CHIPS * npb, k_dim // tk), NN, add=add, out_dtype=out_dtype)


def _mm_cs_dx(dy, w3, *, name, out_dtype=F32):
    m_dim, _ = dy.shape
    _, k_dim, n4 = w3.shape
    tm, tkw, tn = _tile(m_dim, MM_CAP), _tile(k_dim, MM_CAP), _tile(n4, MM_CAP)
    npb = n4 // tn
    return _mm_core(name, dy, w3,
                    pl.BlockSpec((tm, tn), lambda i, j, k: (i, k)),
                    pl.BlockSpec((None, tkw, tn), lambda i, j, k: (k // npb, j, k % npb)),
                    pl.BlockSpec((tm, tkw), lambda i, j, k: (i, j)),
                    (m_dim, k_dim), (m_dim // tm, k_dim // tkw, N_CHIPS * npb), NT, out_dtype=out_dtype)


def _mm_cs_dw(a, dy, *, name):
    m_dim, k_dim = a.shape
    n4 = dy.shape[1] // N_CHIPS
    tmk, tn, tk = _tile(k_dim, MM_CAP), _tile(n4, MM_CAP), _tile(m_dim, MM_CAP)
    npb = n4 // tn
    return _mm_core(name, a, dy,
                    pl.BlockSpec((tk, tmk), lambda i, j, k: (k, i)),
                    pl.BlockSpec((tk, tn), lambda i, j, k: (k, j)),
                    pl.BlockSpec((None, tmk, tn), lambda i, j, k: (j // npb, i, j % npb)),
                    (N_CHIPS, k_dim, n4), (k_dim // tmk, N_CHIPS * npb, m_dim // tk), TN)


def _mm_ffn_up(n, w3, *, name):
    t_dim, d = n.shape
    _, _, f4 = w3.shape
    tm = _tile(t_dim, TOK_CAP)
    return _mm_core(name, n, w3,
                    pl.BlockSpec((tm, d), lambda i, j, k: (i, 0)),
                    pl.BlockSpec((None, d, f4), lambda i, j, k: (j, 0, 0)),
                    pl.BlockSpec((None, tm, f4), lambda i, j, k: (j, i, 0)),
                    (N_CHIPS, t_dim, f4), (t_dim // tm, N_CHIPS, 1), NN)


def _mm_ffn_up_dw(n, d3, *, name):
    t_dim, d = n.shape
    _, _, f4 = d3.shape
    tk = _tile(t_dim, MM_CAP)
    return _mm_core(name, n, d3,
                    pl.BlockSpec((tk, d), lambda i, j, k: (k, 0)),
                    pl.BlockSpec((None, tk, f4), lambda i, j, k: (j, k, 0)),
                    pl.BlockSpec((None, d, f4), lambda i, j, k: (j, 0, 0)),
                    (N_CHIPS, d, f4), (1, N_CHIPS, t_dim // tk), TN)


def _mm_ffn_up_dx(d3, w3, *, name, add=None):
    _, t_dim, f4 = d3.shape
    _, d, _ = w3.shape
    tm = _tile(t_dim, MM_CAP)
    return _mm_core(name, d3, w3,
                    pl.BlockSpec((None, tm, f4), lambda i, j, k: (k, i, 0)),
                    pl.BlockSpec((None, d, f4), lambda i, j, k: (k, 0, 0)),
                    pl.BlockSpec((tm, d), lambda i, j, k: (i, 0)),
                    (t_dim, d), (t_dim // tm, 1, N_CHIPS), NT, add=add)


def _mm_ffn_down(act3, wd3, *, name, add):
    _, t_dim, f4 = act3.shape
    _, _, d = wd3.shape
    tm = _tile(t_dim, MM_CAP)
    return _mm_core(name, act3, wd3,
                    pl.BlockSpec((None, tm, f4), lambda i, j, k: (k, i, 0)),
                    pl.BlockSpec((None, f4, d), lambda i, j, k: (k, 0, 0)),
                    pl.BlockSpec((tm, d), lambda i, j, k: (i, 0)),
                    (t_dim, d), (t_dim // tm, 1, N_CHIPS), NN, add=add)


def _mm_ffn_down_dx(dh, wd3, *, name):
    t_dim, d = dh.shape
    _, f4, _ = wd3.shape
    tm = _tile(t_dim, TOK_CAP)
    return _mm_core(name, dh, wd3,
                    pl.BlockSpec((tm, d), lambda i, j, k: (i, 0)),
                    pl.BlockSpec((None, f4, d), lambda i, j, k: (j, 0, 0)),
                    pl.BlockSpec((None, tm, f4), lambda i, j, k: (j, i, 0)),
                    (N_CHIPS, t_dim, f4), (t_dim // tm, N_CHIPS, 1), NT)


def _mm_ffn_down_dw(act3, dh, *, name):
    _, t_dim, f4 = act3.shape
    d = dh.shape[1]
    tk = _tile(t_dim, MM_CAP)
    return _mm_core(name, act3, dh,
                    pl.BlockSpec((None, tk, f4), lambda i, j, k: (i, k, 0)),
                    pl.BlockSpec((tk, d), lambda i, j, k: (k, 0)),
                    pl.BlockSpec((None, f4, d), lambda i, j, k: (i, 0, 0)),
                    (N_CHIPS, f4, d), (N_CHIPS, 1, t_dim // tk), TN)


def _rms_fwd(x, g, *, name, tt=512):
    t_dim, d = x.shape
    tt = _tile(t_dim, tt, 8)

    def body(x_ref, g_ref, o_ref):
        xv = x_ref[...]
        r = lax.rsqrt(jnp.mean(xv * xv, axis=-1, keepdims=True) + RMS_EPS)
        o_ref[...] = ((xv * r) * g_ref[...]).astype(o_ref.dtype)

    return pl.pallas_call(
        body, name=name, grid=(t_dim // tt,),
        in_specs=[pl.BlockSpec((tt, d), lambda i: (i, 0)), pl.BlockSpec((1, d), lambda i: (0, 0))],
        out_specs=pl.BlockSpec((tt, d), lambda i: (i, 0)),
        out_shape=jax.ShapeDtypeStruct((t_dim, d), BF16),
        compiler_params=_params(("parallel",)),
    )(x, g)


def _rms_bwd(x, g, dy, add, *, name, tt=512):
    t_dim, d = x.shape
    tt = _tile(t_dim, tt, 8)
    has_add = add is not None

    def body(*refs):
        if has_add:
            x_ref, g_ref, dy_ref, add_ref, dx_ref, dg_ref = refs
        else:
            x_ref, g_ref, dy_ref, dx_ref, dg_ref = refs
        xv = x_ref[...]
        dyv = dy_ref[...].astype(F32)
        r = lax.rsqrt(jnp.mean(xv * xv, axis=-1, keepdims=True) + RMS_EPS)
        xh = xv * r
        u = dyv * g_ref[...]
        dx = r * (u - xh * jnp.mean(u * xh, axis=-1, keepdims=True))
        if has_add:
            dx = add_ref[...] + dx
        dx_ref[...] = dx

        @pl.when(pl.program_id(0) == 0)
        def _():
            dg_ref[...] = jnp.zeros_like(dg_ref)

        dg_ref[...] += jnp.sum(dyv * xh, axis=0, keepdims=True)

    row = pl.BlockSpec((tt, d), lambda i: (i, 0))
    vec = pl.BlockSpec((1, d), lambda i: (0, 0))
    in_specs = [row, vec, row] + ([row] if has_add else [])
    args = (x, g, dy) + ((add,) if has_add else ())
    return pl.pallas_call(
        body, name=name, grid=(t_dim // tt,),
        in_specs=in_specs, out_specs=[row, vec],
        out_shape=[jax.ShapeDtypeStruct((t_dim, d), F32), jax.ShapeDtypeStruct((1, d), F32)],
        compiler_params=_params(("arbitrary",)),
    )(*args)


def _final(h, g, target, *, tt=512):
    t_dim, d = h.shape
    n_steps = t_dim // tt

    def body(h_ref, g_ref, t_ref, loss_ref, dh_ref, dg_ref, sq_ref):
        i = pl.program_id(0)
        xv = h_ref[...]
        gv = g_ref[...]
        r = lax.rsqrt(jnp.mean(xv * xv, axis=-1, keepdims=True) + RMS_EPS)
        xh = xv * r
        err = xh * gv - t_ref[...]
        dyv = err * (1.0 / d)
        u = dyv * gv
        dh_ref[...] = r * (u - xh * jnp.mean(u * xh, axis=-1, keepdims=True))

        @pl.when(i == 0)
        def _():
            dg_ref[...] = jnp.zeros_like(dg_ref)
            sq_ref[...] = jnp.zeros_like(sq_ref)

        dg_ref[...] += jnp.sum(dyv * xh, axis=0, keepdims=True)
        sq_ref[...] += jnp.sum(err * err, axis=0, keepdims=True)

        @pl.when(i == n_steps - 1)
        def _():
            total = jnp.sum(sq_ref[...], axis=-1, keepdims=True) * (0.5 / d)
            loss_ref[...] = jnp.broadcast_to(total, loss_ref.shape)

    row = pl.BlockSpec((tt, d), lambda i: (i, 0))
    vec = pl.BlockSpec((1, d), lambda i: (0, 0))
    return pl.pallas_call(
        body, name="final_loss", grid=(n_steps,),
        in_specs=[row, vec, row],
        out_specs=[pl.BlockSpec((1, LANES), lambda i: (0, 0)), row, vec],
        out_shape=[jax.ShapeDtypeStruct((1, LANES), F32), jax.ShapeDtypeStruct((t_dim, d), F32),
                   jax.ShapeDtypeStruct((1, d), F32)],
        scratch_shapes=[pltpu.VMEM((1, d), F32)],
        compiler_params=_params(("arbitrary",)),
    )(h, g, target)


def _rope_table(pos, inv_lane, sel_a, sel_b, *, tt=512):
    t_dim = pos.shape[0]

    def body(p_ref, f_ref, a_ref, b_ref, c_ref, s_ref):
        ang = p_ref[...] * f_ref[...]
        on = (a_ref[...] + b_ref[...]) > 0.0
        c_ref[...] = jnp.where(on, jnp.cos(ang), 1.0)
        s_ref[...] = jnp.where(on, jnp.sin(ang), 0.0)

    vec = pl.BlockSpec((1, LANES), lambda i: (0, 0))
    row = pl.BlockSpec((tt, LANES), lambda i: (i, 0))
    shp = jax.ShapeDtypeStruct((t_dim, LANES), F32)
    return pl.pallas_call(
        body, name="rope_table", grid=(t_dim // tt,),
        in_specs=[pl.BlockSpec((tt, 1), lambda i: (i, 0)), vec, vec, vec],
        out_specs=[row, row], out_shape=[shp, shp],
        compiler_params=_params(("parallel",)),
    )(pos, inv_lane, sel_a, sel_b)


def _rotate(xv, cs, sn, sa, sb):
    half = ROPE_DIM // 2
    up = pltpu.roll(xv, LANES - half, 1)
    dn = pltpu.roll(xv, half, 1)
    return xv * cs + (dn * sb - up * sa) * sn


def _head_masks():
    h1 = lax.broadcasted_iota(jnp.int32, (1, LANES), 1) < HEAD_DIM
    return h1, jnp.logical_not(h1)


def _split_heads(xv, h1, h2):
    return jnp.where(h1, xv, 0.0).astype(BF16), jnp.where(h2, xv, 0.0).astype(BF16)


def _tri_masks():
    r = lax.broadcasted_iota(jnp.int32, (BLOCK, BLOCK), 0)
    c = lax.broadcasted_iota(jnp.int32, (BLOCK, BLOCK), 1)
    return c <= r, r <= c


def _stream_rows(start, dil):
    if dil == 1:
        return pl.ds(pl.multiple_of(start, BLOCK), BLOCK)
    return pl.ds(start, BLOCK, stride=dil)


def _dil_tile(idx, dil, nb):
    r = idx // nb
    n = idx % nb
    return (_stream_rows(r + dil * BLOCK * n, dil), _stream_rows(r + dil * BLOCK * jnp.maximum(n - 1, 0), dil),
            n > 0)


def _dil_specs(b_dim, s_dim):
    def col(c0):
        return pl.BlockSpec((None, s_dim, LANES),lambda b, h: (b, 0, c0 + h))
    tab = pl.BlockSpec((None, s_dim, LANES),lambda b, h: (b, 0, 0))
    vec = pl.BlockSpec((1, LANES), lambda b, h: (0, 0))
    return col, tab, vec


def _dil_fwd(proj3, cs3, sn3, sel_a, sel_b, rider=None):
    b_dim, s_dim, _ = proj3.shape
    scale = HEAD_DIM ** -0.5
    n_pat = len(DIL_PATTERNS)
    extra, extra_shapes, extra_sems = _rider_parts(rider)
    n_w = len(extra)
    n_steps = b_dim * PAIRS

    def body(*refs):
        q_ref, k_ref, v_ref, cs_ref, sn_ref, sa_ref, sb_ref = refs[:7]
        o16_ref, o32_ref, l_ref = refs[7 + n_w:10 + n_w]
        qr, kr = refs[10 + 2 * n_w:12 + 2 * n_w]
        per_pattern = refs[12 + 2 * n_w:12 + 2 * n_w + 2 * n_pat]
        og, lg = per_pattern[:n_pat], per_pattern[n_pat:]
        step = pl.program_id(0) * PAIRS + pl.program_id(1)
        begin, end = _rider_hooks(rider, refs[7:7 + n_w], refs[10 + n_w:10 + 2 * n_w], refs[-2:], step, n_steps)
        begin()
        h1, h2 = _head_masks()
        cur_ok, prev_ok = _tri_masks()
        sa, sb = sa_ref[...], sb_ref[...]

        def prep(j, _):
            rows = pl.ds(pl.multiple_of(j * BLOCK, BLOCK), BLOCK)
            cs, sn = cs_ref[rows, :], sn_ref[rows, :]
            qr[rows, :] = _rotate(q_ref[rows, :], cs, sn, sa, sb) * scale
            kr[rows, :] = _rotate(k_ref[rows, :], cs, sn, sa, sb)
            return 0

        lax.fori_loop(0, s_dim // BLOCK, prep, 0)

        for g, (_, dil) in enumerate(DIL_PATTERNS):
            nb = s_dim // dil // BLOCK

            def some(bi, _, g=g, dil=dil, nb=nb):
                tiles = [_dil_tile(bi * DIL_BATCH + t, dil, nb) for t in range(DIL_BATCH)]
                rows = [t[0] for t in tiles]
                q1, q2 = _split_heads(jnp.stack([qr[rw, :] for rw in rows]), h1, h2)
                kc = jnp.stack([kr[rw, :] for rw in rows]).astype(BF16)
                vc1, vc2 = _split_heads(jnp.stack([v_ref[rw, :] for rw in rows]), h1, h2)
                if nb > 1:
                    kp = jnp.stack([kr[t[1], :] for t in tiles]).astype(BF16)
                    vp1, vp2 = _split_heads(jnp.stack([v_ref[t[1], :] for t in tiles]), h1, h2)
                    p_ok = jnp.stack([jnp.logical_and(prev_ok, t[2]) for t in tiles])

                def head(qh, vch, vph):
                    sc = jnp.where(cur_ok, lax.dot_general(qh, kc, BNT, preferred_element_type=F32), -jnp.inf)
                    m = jnp.max(sc, axis=-1, keepdims=True)
                    if nb > 1:
                        sp = jnp.where(p_ok, lax.dot_general(qh, kp, BNT, preferred_element_type=F32), -jnp.inf)
                        m = jnp.maximum(m, jnp.max(sp, axis=-1, keepdims=True))
                    pc = jnp.exp(sc - m)
                    den = jnp.sum(pc, axis=-1, keepdims=True)
                    acc = lax.dot_general(pc.astype(BF16), vch, BNN, preferred_element_type=F32)
                    if nb > 1:
                        pp = jnp.exp(sp - m)
                        den = den + jnp.sum(pp, axis=-1, keepdims=True)
                        acc = acc + lax.dot_general(pp.astype(BF16), vph, BNN, preferred_element_type=F32)
                    return acc / den, m + jnp.log(den)

                o1, l1 = head(q1, vc1, vp1 if nb > 1 else None)
                o2, l2 = head(q2, vc2, vp2 if nb > 1 else None)
                o, l = o1 + o2, jnp.where(h1, l1, l2)
                for t, rw in enumerate(rows):
                    og[g][rw, :] = o[t]
                    lg[g][rw, :] = l[t]
                return 0

            lax.fori_loop(0, dil * nb // DIL_BATCH, some, 0)

        def comb(j, _):
            rows = pl.ds(pl.multiple_of(j * BLOCK, BLOCK), BLOCK)
            ls = [lg[g][rows, :] for g in range(n_pat)]
            m = jnp.maximum(jnp.maximum(ls[0], ls[1]), ls[2])
            es = [jnp.exp(l - m) for l in ls]
            den = es[0] + es[1] + es[2]
            o = (es[0] * og[0][rows, :] + es[1] * og[1][rows, :] + es[2] * og[2][rows, :]) / den
            o16_ref[rows, :] = o.astype(BF16)
            o32_ref[rows, :] = o
            l_ref[rows, :] = m + jnp.log(den)
            return 0

        lax.fori_loop(0, s_dim // BLOCK, comb, 0)
        end()

    col, tab, vec = _dil_specs(b_dim, s_dim)
    out = pl.BlockSpec((None, s_dim, LANES),lambda b, h: (b, 0, h))
    shp = (b_dim, s_dim, ATT_WIDTH)
    res = pl.pallas_call(
        body, name="dil_fwd", grid=(b_dim, PAIRS),
        in_specs=[col(COL_QA), col(COL_KA), col(COL_VA), tab, tab, vec, vec] + [ANY] * n_w,
        out_specs=[out, out, out] + [ANY] * n_w,
        out_shape=[jax.ShapeDtypeStruct(shp, BF16), jax.ShapeDtypeStruct(shp, F32), jax.ShapeDtypeStruct(shp, F32)]
        + extra_shapes,
        scratch_shapes=[pltpu.VMEM((s_dim, LANES), F32)] * (2 + 2 * n_pat) + extra_sems,
        compiler_params=_params(("arbitrary", "arbitrary")),
    )(proj3, proj3, proj3, cs3, sn3, sel_a, sel_b, *extra)
    return res[:3], res[3:]


def _dil_bwd(proj3, cs3, sn3, sel_a, sel_b, do3, o3, lse3, rider=None):
    b_dim, s_dim, _ = proj3.shape
    scale = HEAD_DIM ** -0.5
    extra, extra_shapes, extra_sems = _rider_parts(rider)
    n_w = len(extra)

    def body(*refs):
        q_ref, k_ref, v_ref, cs_ref, sn_ref, sa_ref, sb_ref, do_ref, o_ref, l_ref = refs[:10]
        dq_ref, dk_ref, dv_ref = refs[10 + n_w:13 + n_w]
        qr, kr, dqa, dka, dva = refs[13 + 2 * n_w:18 + 2 * n_w]
        step = pl.program_id(0) * PAIRS + pl.program_id(1)
        begin, end = _rider_hooks(rider, refs[10:10 + n_w], refs[13 + n_w:13 + 2 * n_w], refs[-2:], step,
                                  b_dim * PAIRS)
        begin()
        h1, h2 = _head_masks()
        cur_ok, prev_ok = _tri_masks()
        sa, sb = sa_ref[...], sb_ref[...]

        def prep(j, _):
            rows = pl.ds(pl.multiple_of(j * BLOCK, BLOCK), BLOCK)
            cs, sn = cs_ref[rows, :], sn_ref[rows, :]
            qr[rows, :] = _rotate(q_ref[rows, :], cs, sn, sa, sb) * scale
            kr[rows, :] = _rotate(k_ref[rows, :], cs, sn, sa, sb)
            zero = jnp.zeros((BLOCK, LANES), F32)
            dqa[rows, :] = zero
            dka[rows, :] = zero
            dva[rows, :] = zero
            return 0

        lax.fori_loop(0, s_dim // BLOCK, prep, 0)

        for _, dil in DIL_PATTERNS:
            nb = s_dim // dil // BLOCK

            def some(bi, _, dil=dil, nb=nb):
                tiles = [_dil_tile(bi * DIL_BATCH + t, dil, nb) for t in range(DIL_BATCH)]
                rows = [t[0] for t in tiles]
                q1, q2 = _split_heads(jnp.stack([qr[rw, :] for rw in rows]), h1, h2)
                dof = jnp.stack([do_ref[rw, :] for rw in rows])
                do1, do2 = _split_heads(dof, h1, h2)
                prod = dof * jnp.stack([o_ref[rw, :] for rw in rows])
                delta1 = jnp.sum(jnp.where(h1, prod, 0.0), axis=-1, keepdims=True)
                delta2 = jnp.sum(jnp.where(h2, prod, 0.0), axis=-1, keepdims=True)
                lt = jnp.stack([l_ref[rw, :] for rw in rows])
                lse1 = jnp.max(jnp.where(h1, lt, -jnp.inf), axis=-1, keepdims=True)
                lse2 = jnp.max(jnp.where(h2, lt, -jnp.inf), axis=-1, keepdims=True)

                def side(krows, ok):
                    kf = jnp.stack([kr[kw, :] for kw in krows])
                    k16 = kf.astype(BF16)
                    k1, k2 = _split_heads(kf, h1, h2)
                    v16 = jnp.stack([v_ref[kw, :] for kw in krows]).astype(BF16)

                    def head(qh, doh, lse, delta):
                        sc = lax.dot_general(qh, k16, BNT, preferred_element_type=F32)
                        p = jnp.where(ok, jnp.exp(sc - lse), 0.0)
                        dp = lax.dot_general(doh, v16, BNT, preferred_element_type=F32)
                        return p.astype(BF16), (p * (dp - delta)).astype(BF16)

                    p1, ds1 = head(q1, do1, lse1, delta1)
                    p2, ds2 = head(q2, do2, lse2, delta2)
                    dv = (lax.dot_general(p1, do1, BTN, preferred_element_type=F32)
                          + lax.dot_general(p2, do2, BTN, preferred_element_type=F32))
                    dk = (lax.dot_general(ds1, q1, BTN, preferred_element_type=F32)
                          + lax.dot_general(ds2, q2, BTN, preferred_element_type=F32))
                    for t, kw in enumerate(krows):
                        dva[kw, :] += dv[t]
                        dka[kw, :] += dk[t]
                    return (lax.dot_general(ds1, k1, BNN, preferred_element_type=F32)
                            + lax.dot_general(ds2, k2, BNN, preferred_element_type=F32))

                dq = side(rows, cur_ok)
                if nb > 1:
                    dq = dq + side([t[1] for t in tiles], jnp.stack([jnp.logical_and(prev_ok, t[2]) for t in tiles]))
                for t, rw in enumerate(rows):
                    dqa[rw, :] += dq[t] * scale
                return 0

            lax.fori_loop(0, dil * nb // DIL_BATCH, some, 0)

        def finish(j, _):
            rows = pl.ds(pl.multiple_of(j * BLOCK, BLOCK), BLOCK)
            cs, sn = cs_ref[rows, :], -sn_ref[rows, :]
            dq_ref[rows, :] = _rotate(dqa[rows, :], cs, sn, sa, sb).astype(BF16)
            dk_ref[rows, :] = _rotate(dka[rows, :], cs, sn, sa, sb).astype(BF16)
            dv_ref[rows, :] = dva[rows, :].astype(BF16)
            return 0

        lax.fori_loop(0, s_dim // BLOCK, finish, 0)
        end()

    col, tab, vec = _dil_specs(b_dim, s_dim)
    out = pl.BlockSpec((None, s_dim, LANES),lambda b, h: (b, 0, h))
    shp = jax.ShapeDtypeStruct((b_dim, s_dim, ATT_WIDTH), BF16)
    acc = pltpu.VMEM((s_dim, LANES), F32)
    res = pl.pallas_call(
        body, name="dil_bwd", grid=(b_dim, PAIRS),
        in_specs=[col(COL_QA), col(COL_KA), col(COL_VA), tab, tab, vec, vec, out, out, out] + [ANY] * n_w,
        out_specs=[out, out, out] + [ANY] * n_w, out_shape=[shp, shp, shp] + extra_shapes,
        scratch_shapes=[acc, acc, acc, acc, acc] + extra_sems,
        compiler_params=_params(("arbitrary", "arbitrary")),
    )(proj3, proj3, proj3, cs3, sn3, sel_a, sel_b, do3, o3, lse3, *extra)
    return res[:3], res[3:]


def _split_dot(x, tri):
    hi = x.astype(BF16)
    lo = (x - hi.astype(F32)).astype(BF16)
    return jnp.dot(hi, tri, preferred_element_type=F32) + jnp.dot(lo, tri, preferred_element_type=F32)


def _log_sigmoid(z):
    return jnp.minimum(z, 0.0) - jnp.log(1.0 + jnp.exp(-jnp.abs(z)))


def _sb_scores(qh, k16, valid):
    z = lax.dot_general(qh, k16, NT, preferred_element_type=F32)
    ls = _log_sigmoid(z)
    l1m = ls - z
    return ls, (l1m if valid is None else jnp.where(valid, l1m, 0.0))


def _sb_consts():
    r = lax.broadcasted_iota(jnp.int32, (BLOCK, BLOCK), 0)
    c = lax.broadcasted_iota(jnp.int32, (BLOCK, BLOCK), 1)
    after = (r > c).astype(BF16)
    before = (r < c).astype(BF16)
    qrow = lax.broadcasted_iota(jnp.int32, (SB_ROWS, BLOCK), 0)
    kcol = lax.broadcasted_iota(jnp.int32, (SB_ROWS, BLOCK), 1)
    return after, before, qrow, kcol


def _pairs_loop(n_blocks, step, carry):
    def several(i, c):
        for j in range(SB_STEP):
            c = step(SB_STEP * i + j, c)
        return c

    if isinstance(n_blocks, int) and n_blocks <= SB_STEP:
        for i in range(n_blocks):
            carry = step(i, carry)
        return carry
    n_main = n_blocks // SB_STEP
    carry = lax.fori_loop(0, n_main, several, carry)
    return lax.fori_loop(n_main * SB_STEP, n_blocks, step, carry)


def _sb_fwd(proj3, rider=None):
    b_dim, s_dim, _ = proj3.shape
    scale = HEAD_DIM ** -0.5
    per = SB_ROWS // BLOCK
    extra, extra_shapes, extra_sems = _rider_parts(rider)
    n_w = len(extra)

    def body(*refs):
        q_ref, k_ref, v_ref = refs[:3]
        o_ref = refs[3 + n_w]
        step = pl.program_id(0) * PAIRS + pl.program_id(1)
        begin, end = _rider_hooks(rider, refs[3:3 + n_w], refs[4 + n_w:4 + 2 * n_w], refs[-2:], step, b_dim * PAIRS)
        begin()
        h1, h2 = _head_masks()
        after, _, qrow, kcol = _sb_consts()

        def qloop(qi, _):
            rows = pl.ds(pl.multiple_of(qi * SB_ROWS, SB_ROWS), SB_ROWS)
            q1, q2 = _split_heads(q_ref[rows, :] * scale, h1, h2)
            first = qi * per

            def block(kb, carry, masked):
                acc, run1, run2 = carry
                krows = pl.ds(pl.multiple_of(kb * BLOCK, BLOCK), BLOCK)
                k16 = k_ref[krows, :].astype(BF16)
                v1, v2 = _split_heads(v_ref[krows, :], h1, h2)
                valid = ((kb - first) * BLOCK + kcol < qrow) if masked else None

                def head(qh, vh, run):
                    ls, l1m = _sb_scores(qh, k16, valid)
                    a = jnp.exp(ls + _split_dot(l1m, after) + run)
                    if masked:
                        a = jnp.where(valid, a, 0.0)
                    return (jnp.dot(a.astype(BF16), vh, preferred_element_type=F32),
                            run + jnp.sum(l1m, axis=-1, keepdims=True))

                o1, run1 = head(q1, v1, run1)
                o2, run2 = head(q2, v2, run2)
                return acc + o1 + o2, run1, run2

            zcol = jnp.zeros((SB_ROWS, 1), F32)
            carry = (jnp.zeros((SB_ROWS, LANES), F32), zcol, zcol)
            carry = _pairs_loop(per, lambda i, c: block(first + per - 1 - i, c, True), carry)
            acc, _, _ = _pairs_loop(first, lambda i, c: block(first - 1 - i, c, False), carry)
            o_ref[rows, :] = acc.astype(BF16)
            return 0

        lax.fori_loop(0, s_dim // SB_ROWS, qloop, 0)
        end()

    def col(c0):
        return pl.BlockSpec((None, s_dim, LANES),lambda b, h: (b, 0, c0 + h))

    res = pl.pallas_call(
        body, name="sb_fwd", grid=(b_dim, PAIRS),
        in_specs=[col(COL_QB), col(COL_KB), col(COL_VB)] + [ANY] * n_w, out_specs=[col(0)] + [ANY] * n_w,
        out_shape=[jax.ShapeDtypeStruct((b_dim, s_dim, ATT_WIDTH), BF16)] + extra_shapes,
        scratch_shapes=extra_sems,
        compiler_params=_params(("arbitrary", "arbitrary")),
    )(proj3, proj3, proj3, *extra)
    return res[0], res[1:]


def _sb_bwd(proj3, do3, rider=None):
    b_dim, s_dim, _ = proj3.shape
    scale = HEAD_DIM ** -0.5
    per = SB_ROWS // BLOCK
    nkb_max = s_dim // BLOCK
    extra, extra_shapes, extra_sems = _rider_parts(rider)
    n_w = len(extra)

    def body(*refs):
        q_ref, k_ref, v_ref, do_ref = refs[:4]
        dq_ref, dk_ref, dv_ref = refs[4 + n_w:7 + n_w]
        dka, dva, e_ref, sg_ref = refs[7 + 2 * n_w:11 + 2 * n_w]
        step = pl.program_id(0) * PAIRS + pl.program_id(1)
        begin, end = _rider_hooks(rider, refs[4:4 + n_w], refs[7 + n_w:7 + 2 * n_w], refs[-2:], step, b_dim * PAIRS)
        begin()
        h1, h2 = _head_masks()
        after, before, qrow, kcol = _sb_consts()
        dka[...] = jnp.zeros_like(dka)
        dva[...] = jnp.zeros_like(dva)

        def qloop(qi, _):
            rows = pl.ds(pl.multiple_of(qi * SB_ROWS, SB_ROWS), SB_ROWS)
            q1, q2 = _split_heads(q_ref[rows, :] * scale, h1, h2)
            do1, do2 = _split_heads(do_ref[rows, :].astype(F32), h1, h2)
            first = qi * per

            def pass1(kb, carry, masked):
                run1, run2 = carry
                krows = pl.ds(pl.multiple_of(kb * BLOCK, BLOCK), BLOCK)
                k16 = k_ref[krows, :].astype(BF16)
                v16 = v_ref[krows, :].astype(BF16)
                valid = ((kb - first) * BLOCK + kcol < qrow) if masked else None

                def head(h, qh, doh, run):
                    ls, l1m = _sb_scores(qh, k16, valid)
                    a = jnp.exp(ls + _split_dot(l1m, after) + run)
                    if masked:
                        a = jnp.where(valid, a, 0.0)
                    da = lax.dot_general(doh, v16, NT, preferred_element_type=F32)
                    e_ref[h, kb] = a * da
                    sg_ref[h, kb] = jnp.exp(ls)
                    return a.astype(BF16), run + jnp.sum(l1m, axis=-1, keepdims=True)

                a1, run1 = head(0, q1, do1, run1)
                a2, run2 = head(1, q2, do2, run2)
                dva[krows, :] += (lax.dot_general(a1, do1, TN, preferred_element_type=F32)
                                  + lax.dot_general(a2, do2, TN, preferred_element_type=F32))
                return run1, run2

            zcol = jnp.zeros((SB_ROWS, 1), F32)
            carry = _pairs_loop(per, lambda i, c: pass1(first + per - 1 - i, c, True), (zcol, zcol))
            _pairs_loop(first, lambda i, c: pass1(first - 1 - i, c, False), carry)

            def pass2(kb, carry, masked):
                dq, pre1, pre2 = carry
                krows = pl.ds(pl.multiple_of(kb * BLOCK, BLOCK), BLOCK)
                k1, k2 = _split_heads(k_ref[krows, :], h1, h2)

                def head(h, pre):
                    ev = e_ref[h, kb]
                    sg = sg_ref[h, kb]
                    dz = ev * (1.0 - sg) - (_split_dot(ev, before) + pre) * sg
                    if masked:
                        dz = jnp.where((kb - first) * BLOCK + kcol < qrow, dz, 0.0)
                    return dz.astype(BF16), pre + jnp.sum(ev, axis=-1, keepdims=True)

                dz1, pre1 = head(0, pre1)
                dz2, pre2 = head(1, pre2)
                dka[krows, :] += (lax.dot_general(dz1, q1, TN, preferred_element_type=F32)
                                  + lax.dot_general(dz2, q2, TN, preferred_element_type=F32))
                dq = dq + jnp.dot(dz1, k1, preferred_element_type=F32) + jnp.dot(dz2, k2, preferred_element_type=F32)
                return dq, pre1, pre2

            carry = _pairs_loop(first, lambda i, c: pass2(i, c, False), (jnp.zeros((SB_ROWS, LANES), F32), zcol, zcol))
            dq, _, _ = _pairs_loop(per, lambda i, c: pass2(first + i, c, True), carry)
            dq_ref[rows, :] = (dq * scale).astype(BF16)
            return 0

        lax.fori_loop(0, s_dim // SB_ROWS, qloop, 0)
        dk_ref[...] = dka[...].astype(BF16)
        dv_ref[...] = dva[...].astype(BF16)
        end()

    def col(c0):
        return pl.BlockSpec((None, s_dim, LANES),lambda b, h: (b, 0, c0 + h))

    shp = jax.ShapeDtypeStruct((b_dim, s_dim, ATT_WIDTH), BF16)
    acc = pltpu.VMEM((s_dim, LANES), F32)
    strip = pltpu.VMEM((2, nkb_max, SB_ROWS, BLOCK), F32)
    res = pl.pallas_call(
        body, name="sb_bwd", grid=(b_dim, PAIRS),
        in_specs=[col(COL_QB), col(COL_KB), col(COL_VB), col(0)] + [ANY] * n_w,
        out_specs=[col(0), col(0), col(0)] + [ANY] * n_w,
        out_shape=[shp, shp, shp] + extra_shapes,
        scratch_shapes=[acc, acc, strip, strip] + extra_sems,
        compiler_params=_params(("arbitrary", "arbitrary")),
    )(proj3, proj3, proj3, do3, *extra)
    return res[:3], res[3:]


def _sigmoid(x):
    return 1.0 / (1.0 + jnp.exp(-x))


def _gate_fwd(proj, ua, ub, *, tt=512):
    t_dim, d = ua.shape

    def body(ga_ref, gb_ref, ua_ref, ub_ref, o_ref):
        o_ref[...] = (_sigmoid(ga_ref[...]) * ua_ref[...] + _sigmoid(gb_ref[...]) * ub_ref[...]).astype(BF16)

    row = pl.BlockSpec((tt, d), lambda i: (i, 0))
    return pl.pallas_call(
        body, name="gate_fwd", grid=(t_dim // tt,),
        in_specs=[pl.BlockSpec((tt, d), lambda i: (i, 3)), pl.BlockSpec((tt, d), lambda i: (i, 4)), row, row],
        out_specs=row, out_shape=jax.ShapeDtypeStruct((t_dim, d), BF16),
        compiler_params=_params(("parallel",)),
    )(proj, proj, ua, ub)


def _gate_bwd(proj, ua, ub, dmix, *, tt=512):
    t_dim, d = ua.shape

    def body(ga_ref, gb_ref, ua_ref, ub_ref, dm_ref, dua_ref, dub_ref, dg_ref):
        dm = dm_ref[...]
        sa = _sigmoid(ga_ref[...])
        sb = _sigmoid(gb_ref[...])
        dua_ref[...] = (dm * sa).astype(BF16)
        dub_ref[...] = (dm * sb).astype(BF16)
        dg_ref[:, :d] = (dm * ua_ref[...] * (sa * (1.0 - sa))).astype(BF16)
        dg_ref[:, d:] = (dm * ub_ref[...] * (sb * (1.0 - sb))).astype(BF16)

    row = pl.BlockSpec((tt, d), lambda i: (i, 0))
    wide = pl.BlockSpec((tt, 2 * d), lambda i: (i, 0))
    return pl.pallas_call(
        body, name="gate_bwd", grid=(t_dim // tt,),
        in_specs=[pl.BlockSpec((tt, d), lambda i: (i, 3)), pl.BlockSpec((tt, d), lambda i: (i, 4)), row, row, row],
        out_specs=[row, row, wide],
        out_shape=[jax.ShapeDtypeStruct((t_dim, d), BF16), jax.ShapeDtypeStruct((t_dim, d), BF16),
                   jax.ShapeDtypeStruct((t_dim, 2 * d), BF16)],
        compiler_params=_params(("parallel",)),
    )(proj, proj, ua, ub, dmix)


def _swiglu_fwd(g3, u3, *, tt=1024):
    n, t_dim, f4 = g3.shape

    def body(g_ref, u_ref, o_ref):
        gv = g_ref[...]
        o_ref[...] = (gv * _sigmoid(gv) * u_ref[...]).astype(BF16)

    spec = pl.BlockSpec((None, tt, f4), lambda s, i: (s, i, 0))
    return pl.pallas_call(
        body, name="swiglu_fwd", grid=(n, t_dim // tt), in_specs=[spec, spec], out_specs=spec,
        out_shape=jax.ShapeDtypeStruct(g3.shape, BF16),
        compiler_params=_params(("parallel", "parallel")),
    )(g3, u3)


def _swiglu_bwd(g3, u3, dact3, *, tt=1024):
    n, t_dim, f4 = g3.shape

    def body(g_ref, u_ref, da_ref, dg_ref, du_ref):
        gv = g_ref[...]
        da = da_ref[...]
        sg = _sigmoid(gv)
        dg_ref[...] = (da * u_ref[...] * (sg + gv * sg * (1.0 - sg))).astype(BF16)
        du_ref[...] = (da * (gv * sg)).astype(BF16)

    spec = pl.BlockSpec((None, tt, f4), lambda s, i: (s, i, 0))
    shp = jax.ShapeDtypeStruct(g3.shape, BF16)
    return pl.pallas_call(
        body, name="swiglu_bwd", grid=(n, t_dim // tt), in_specs=[spec, spec, spec], out_specs=[spec, spec],
        out_shape=[shp, shp],
        compiler_params=_params(("parallel", "parallel")),
    )(g3, u3, dact3)


def _mem_fwd(qm, kvm, *, tt=512):
    b_dim, s_dim, _ = qm.shape
    n_mem = kvm.shape[1]
    scale = MEM_HEAD_DIM ** -0.5

    def body(q_ref, k_ref, v_ref, o_ref):
        sc = lax.dot_general(q_ref[0], k_ref[0], NT, preferred_element_type=F32) * scale
        p = jnp.exp(sc - jnp.max(sc, axis=-1, keepdims=True))
        p = p / jnp.sum(p, axis=-1, keepdims=True)
        o_ref[0] = jnp.dot(p.astype(BF16), v_ref[0], preferred_element_type=F32).astype(BF16)

    qs = pl.BlockSpec((1, tt, MEM_HEAD_DIM), lambda b, h, i: (b, i, h))
    return pl.pallas_call(
        body, name="mem_fwd", grid=(b_dim, N_HEADS_MEM, s_dim // tt),
        in_specs=[qs, pl.BlockSpec((1, n_mem, MEM_HEAD_DIM), lambda b, h, i: (b, 0, h)),
                  pl.BlockSpec((1, n_mem, MEM_HEAD_DIM), lambda b, h, i: (b, 0, N_HEADS_MEM + h))],
        out_specs=qs, out_shape=jax.ShapeDtypeStruct(qm.shape, BF16),
        compiler_params=_params(("parallel", "parallel", "parallel")),
    )(qm, kvm, kvm)


def _mem_bwd(qm, kvm, dom, *, tt=512):
    b_dim, s_dim, _ = qm.shape
    n_mem = kvm.shape[1]
    scale = MEM_HEAD_DIM ** -0.5

    def body(q_ref, k_ref, v_ref, do_ref, dq_ref, dk_ref, dv_ref):
        qv, kv, vv, dov = q_ref[0], k_ref[0], v_ref[0], do_ref[0]
        sc = lax.dot_general(qv, kv, NT, preferred_element_type=F32) * scale
        p = jnp.exp(sc - jnp.max(sc, axis=-1, keepdims=True))
        p = p / jnp.sum(p, axis=-1, keepdims=True)
        dp = lax.dot_general(dov, vv, NT, preferred_element_type=F32)
        ds = (p * (dp - jnp.sum(p * dp, axis=-1, keepdims=True)) * scale).astype(BF16)
        dq_ref[0] = jnp.dot(ds, kv, preferred_element_type=F32).astype(BF16)

        @pl.when(pl.program_id(2) == 0)
        def _():
            dk_ref[...] = jnp.zeros_like(dk_ref)
            dv_ref[...] = jnp.zeros_like(dv_ref)

        dk_ref[0] += lax.dot_general(ds, qv, TN, preferred_element_type=F32)
        dv_ref[0] += lax.dot_general(p.astype(BF16), dov, TN, preferred_element_type=F32)

    qs = pl.BlockSpec((1, tt, MEM_HEAD_DIM), lambda b, h, i: (b, i, h))
    ks = pl.BlockSpec((1, n_mem, MEM_HEAD_DIM), lambda b, h, i: (b, 0, h))
    vs = pl.BlockSpec((1, n_mem, MEM_HEAD_DIM), lambda b, h, i: (b, 0, N_HEADS_MEM + h))
    return pl.pallas_call(
        body, name="mem_bwd", grid=(b_dim, N_HEADS_MEM, s_dim // tt),
        in_specs=[qs, ks, vs, qs], out_specs=[qs, ks, ks],
        out_shape=[jax.ShapeDtypeStruct(qm.shape, BF16), jax.ShapeDtypeStruct((b_dim, n_mem, MEM_WIDTH), F32),
                   jax.ShapeDtypeStruct((b_dim, n_mem, MEM_WIDTH), F32)],
        compiler_params=_params(("parallel", "parallel", "arbitrary")),
    )(qm, kvm, kvm, dom)


def _adamw_math(wv, gv, mv, vv):
    nm = ADAM_B1 * mv + (1.0 - ADAM_B1) * gv
    nv = ADAM_B2 * vv + (1.0 - ADAM_B2) * (gv * gv)
    m_hat = nm / (1.0 - ADAM_B1 ** ADAM_STEP)
    v_hat = nv / (1.0 - ADAM_B2 ** ADAM_STEP)
    return -ADAM_LR * (m_hat / (jnp.sqrt(v_hat) + ADAM_EPS) + ADAM_WD * wv), nm, nv


def _adamw(w, g, m, v, *, name):
    rows, cols = w.shape
    tr = _tile(rows, 256, 8)

    def body(w_ref, g_ref, m_ref, v_ref, d_ref, nm_ref, nv_ref):
        d_ref[...], nm_ref[...], nv_ref[...] = _adamw_math(w_ref[...], g_ref[...], m_ref[...], v_ref[...])

    spec = pl.BlockSpec((tr, cols), lambda i: (i, 0))
    shp = jax.ShapeDtypeStruct((rows, cols), F32)
    return pl.pallas_call(
        body, name=name, grid=(rows // tr,),
        in_specs=[spec] * 4, out_specs=[spec] * 3, out_shape=[shp] * 3,
        compiler_params=_params(("parallel",)),
    )(w, g, m, v)


def _prefetch_spec(grid, in_specs, out_specs):
    return pltpu.PrefetchScalarGridSpec(num_scalar_prefetch=1, grid=grid, in_specs=in_specs, out_specs=out_specs)


def _adamw_halves(w, mine, theirs, m, v, c_idx, *, name):
    rows, cols = w.shape
    half = rows // 2
    tr = _tile(half, 256, 8)
    nh = half // tr

    def body(c_ref, w_ref, mine_ref, theirs_ref, m_ref, v_ref, g_ref, d_ref, nm_ref, nv_ref):
        gv = jnp.where(pl.program_id(0) == c_ref[0], mine_ref[...], theirs_ref[...])
        g_ref[...] = gv
        d_ref[...], nm_ref[...], nv_ref[...] = _adamw_math(w_ref[...], gv, m_ref[...], v_ref[...])

    full = pl.BlockSpec((tr, cols), lambda h, i, c_ref: (h * nh + i, 0))
    part = pl.BlockSpec((tr, cols), lambda h, i, c_ref: (i, 0))
    shp = jax.ShapeDtypeStruct((rows, cols), F32)
    return pl.pallas_call(
        body, name=name, grid_spec=_prefetch_spec((2, nh), [full, part, part, full, full], [full] * 4),
        out_shape=[shp] * 4,
        compiler_params=_params(("parallel", "parallel")),
    )(c_idx, w, mine, theirs, m, v)


def _pair_sum(g3, theirs, c_idx, *, name):
    n, rows, cols = g3.shape
    half = rows // 2
    tr = _tile(half, 256, 16)

    def body(c_ref, g_ref, t_ref, o_ref):
        o_ref[...] = (g_ref[...] + t_ref[...]).astype(BF16)

    part = pl.BlockSpec((None, tr, cols), lambda s, i, c_ref: (s, i, 0))
    return pl.pallas_call(
        body, name=name,
        grid_spec=_prefetch_spec((n, half // tr),
                                 [pl.BlockSpec((None, None, tr, cols), lambda s, i, c_ref: (s, c_ref[0], i, 0)), part],
                                 part),
        out_shape=jax.ShapeDtypeStruct((n, half, cols), BF16),
        compiler_params=_params(("parallel", "parallel")),
    )(c_idx, g3.reshape(n, 2, half, cols), theirs)


def _chip_sum(pair, recv, s_idx, *, name):
    _, half, cols = pair.shape
    tr = _tile(half, 256, 16)

    def body(s_ref, p_ref, r_ref, o_ref):
        o_ref[...] = ((p_ref[...].astype(F32) + r_ref[0].astype(F32)) + r_ref[1].astype(F32)) + r_ref[2].astype(F32)

    return pl.pallas_call(
        body, name=name,
        grid_spec=_prefetch_spec((half // tr,),
                                 [pl.BlockSpec((None, tr, cols), lambda i, s_ref: (s_ref[0], i, 0)),
                                  pl.BlockSpec((N_CHIPS - 1, tr, cols), lambda i, s_ref: (0, i, 0))],
                                 pl.BlockSpec((tr, cols), lambda i, s_ref: (i, 0))),
        out_shape=jax.ShapeDtypeStruct((half, cols), F32),
        compiler_params=_params(("parallel",)),
    )(s_idx, pair, recv)


def _sum8(parts):
    n, rows, cols = parts.shape

    def body(p_ref, o_ref):
        acc = p_ref[0]
        for i in range(1, n):
            acc = acc + p_ref[i]
        o_ref[...] = acc

    return pl.pallas_call(
        body, name="small_sum", grid=(1,),
        in_specs=[pl.BlockSpec((n, rows, cols), lambda i: (0, 0, 0))],
        out_specs=pl.BlockSpec((rows, cols), lambda i: (0, 0)),
        out_shape=jax.ShapeDtypeStruct((rows, cols), parts.dtype),
        compiler_params=_params(("arbitrary",)),
    )(parts)


def _place():
    return lax.axis_index("x"), lax.axis_index("y"), lax.axis_index("c")


ANY = pl.BlockSpec(memory_space=pl.ANY)


def _rider_parts(rider):
    if rider is None:
        return (), [], []
    kind, arrays = rider
    n = len(arrays)
    shapes = {"gather": _gathered_shapes, "pair": _pair_shapes, "chip": _chip_shapes}[kind](arrays)
    sems = _gather_sems(n) if kind == "gather" else _exchange_sems(n if kind == "pair" else 3 * n)
    return tuple(arrays), shapes, sems


def _rider_hooks(rider, ins, outs, sems, step, n_steps):
    if rider is None:
        return (lambda: None), (lambda: None)
    if rider[0] == "gather":
        start, forward, finish = _gather_steps(ins, outs, *sems)
    else:
        start, finish = {"pair": _pair_steps, "chip": _chip_steps}[rider[0]](ins, outs, *sems)
        forward = None

    def begin():
        pl.when(step == 0)(start)

    def end():
        if forward is not None:
            pl.when(step == n_steps - 2)(forward)
        pl.when(step == n_steps - 1)(finish)

    return begin, end


def _gather_weights(shards):
    n = len(shards)

    def body(*refs):
        start, forward, finish = _gather_steps(refs[:n], refs[n:2 * n], refs[2 * n], refs[2 * n + 1])
        start()
        forward()
        finish()

    return pl.pallas_call(
        body, name="gather_weights", out_shape=_gathered_shapes(shards),
        in_specs=[ANY] * n, out_specs=[ANY] * n, scratch_shapes=_gather_sems(n),
    )(*shards)


def _gathered_shapes(shards):
    return [jax.ShapeDtypeStruct((N_CHIPS,) + s.shape, s.dtype) for s in shards]


def _gather_sems(n):
    return [pltpu.SemaphoreType.DMA((7 * n,)), pltpu.SemaphoreType.DMA((7 * n,))]


def _gather_steps(ins, outs, send_sems, recv_sems):
    n = len(ins)
    halves = [r.shape[0] // 2 for r in ins]
    x, y, c = _place()
    my_chip = 2 * x + y
    me, sibling = (x, y, c), (x, y, 1 - c)
    chips = [(1 - x, y), (x, 1 - y), (1 - x, 1 - y)]

    def half_of(w, chip, pc):
        return outs[w].at[chip, pl.ds(pc * halves[w], halves[w]), :]

    def copy(w, k, src, dst, to):
        return pltpu.make_async_remote_copy(
            src_ref=src, dst_ref=dst, send_sem=send_sems.at[7 * w + k], recv_sem=recv_sems.at[7 * w + k],
            device_id=to, device_id_type=MESH)

    def firsts():
        cps = []
        for w in range(n):
            cps.append(copy(w, 0, ins[w], outs[w].at[my_chip], sibling))
            mine = ins[w].at[pl.ds(c * halves[w], halves[w]), :]
            for j, (px, py) in enumerate(chips):
                cps.append(copy(w, 1 + j, mine, half_of(w, my_chip, c), (px, py, c)))
        return cps

    def passes():
        return [copy(w, 4 + j, half_of(w, 2 * px + py, c), half_of(w, 2 * px + py, c), sibling)
                for w in range(n) for j, (px, py) in enumerate(chips)]

    def start():
        for cp in firsts():
            cp.start()

    def forward():
        fws = passes()
        for w in range(n):
            for j, (px, py) in enumerate(chips):
                landed = half_of(w, 2 * px + py, c)
                copy(w, 1 + j, landed, landed, me).wait_recv()
                fws[3 * w + j].start()

    def finish():
        for w in range(n):
            copy(w, 0, ins[w], outs[w].at[my_chip], me).wait_recv()
            for j, (px, py) in enumerate(chips):
                landed = half_of(w, 2 * px + py, 1 - c)
                copy(w, 4 + j, landed, landed, me).wait_recv()
        for cp in firsts() + passes():
            cp.wait_send()

    return start, forward, finish


def _pair_exchange(grads):
    n = len(grads)

    def body(*refs):
        start, finish = _pair_steps(refs[:n], refs[n:2 * n], refs[2 * n], refs[2 * n + 1])
        start()
        finish()

    return pl.pallas_call(
        body, name="grad_pair_exchange", out_shape=_pair_shapes(grads),
        in_specs=[ANY] * n, out_specs=[ANY] * n, scratch_shapes=_exchange_sems(n),
    )(*grads)


def _pair_shapes(grads):
    return [jax.ShapeDtypeStruct((g.shape[0], g.shape[1] // 2, g.shape[2]), g.dtype) for g in grads]


def _exchange_sems(n):
    return [pltpu.SemaphoreType.DMA((n,)), pltpu.SemaphoreType.DMA((n,))]


def _exchange_steps(copies):
    def start():
        for cp in copies():
            cp.start()

    def finish():
        for cp in copies():
            cp.wait()

    return start, finish


def _pair_steps(ins, outs, send_sems, recv_sems):
    x, y, c = _place()

    def copies():
        return [pltpu.make_async_remote_copy(
            src_ref=ins[w].at[:, pl.ds((1 - c) * (ins[w].shape[1] // 2), ins[w].shape[1] // 2), :], dst_ref=outs[w],
            send_sem=send_sems.at[w], recv_sem=recv_sems.at[w], device_id=(x, y, 1 - c), device_id_type=MESH)
            for w in range(len(ins))]

    return _exchange_steps(copies)


def _chip_exchange(pairs):
    n = len(pairs)

    def body(*refs):
        start, finish = _chip_steps(refs[:n], refs[n:2 * n], refs[2 * n], refs[2 * n + 1])
        start()
        finish()

    return pl.pallas_call(
        body, name="grad_chip_exchange", out_shape=_chip_shapes(pairs),
        in_specs=[ANY] * n, out_specs=[ANY] * n, scratch_shapes=_exchange_sems(3 * n),
    )(*pairs)


def _chip_shapes(pairs):
    return [jax.ShapeDtypeStruct((N_CHIPS - 1,) + p.shape[1:], p.dtype) for p in pairs]


def _chip_steps(ins, outs, send_sems, recv_sems):
    x, y, c = _place()
    others = [(1 - x, y), (x, 1 - y), (1 - x, 1 - y)]

    def copies():
        return [pltpu.make_async_remote_copy(
            src_ref=ins[w].at[2 * px + py], dst_ref=outs[w].at[j],
            send_sem=send_sems.at[3 * w + j], recv_sem=recv_sems.at[3 * w + j],
            device_id=(px, py, c), device_id_type=MESH)
            for w in range(len(ins)) for j, (px, py) in enumerate(others)]

    return _exchange_steps(copies)


def _swap_halves(mine):
    n = len(mine)

    def body(*refs):
        ins, outs, send_sems, recv_sems = refs[:n], refs[n:2 * n], refs[2 * n], refs[2 * n + 1]
        x, y, c = _place()
        copies = [pltpu.make_async_remote_copy(
            src_ref=ins[w], dst_ref=outs[w], send_sem=send_sems.at[w], recv_sem=recv_sems.at[w],
            device_id=(x, y, 1 - c), device_id_type=MESH) for w in range(n)]
        for cp in copies:
            cp.start()
        for cp in copies:
            cp.wait()

    return pl.pallas_call(
        body, name="grad_swap_halves",
        out_shape=[jax.ShapeDtypeStruct(h.shape, h.dtype) for h in mine],
        in_specs=[ANY] * n, out_specs=[ANY] * n,
        scratch_shapes=[pltpu.SemaphoreType.DMA((n,)), pltpu.SemaphoreType.DMA((n,))],
    )(*mine)


def _gather_small(small):
    srows, cols = small.shape

    def body(s_ref, all_ref, send_sems, recv_sems, local_sem):
        x, y, c = _place()
        me = 4 * x + 2 * y + c
        keep_small = pltpu.make_async_copy(s_ref, all_ref.at[me], local_sem)
        keep_small.start()
        sends = []
        for kk in range(1, 8):
            peer = (x ^ (kk >> 2), y ^ ((kk >> 1) & 1), c ^ (kk & 1))
            sends.append(pltpu.make_async_remote_copy(
                src_ref=s_ref, dst_ref=all_ref.at[me],
                send_sem=send_sems.at[kk], recv_sem=recv_sems.at[kk], device_id=peer, device_id_type=MESH))
        for cp in sends:
            cp.start()
        for kk in range(1, 8):
            px, py, pc = x ^ (kk >> 2), y ^ ((kk >> 1) & 1), c ^ (kk & 1)
            pltpu.make_async_remote_copy(
                src_ref=s_ref, dst_ref=all_ref.at[4 * px + 2 * py + pc],
                send_sem=send_sems.at[kk], recv_sem=recv_sems.at[kk], device_id=(px, py, pc),
                device_id_type=MESH).wait_recv()
        for cp in sends:
            cp.wait_send()
        keep_small.wait()

    return pl.pallas_call(
        body, name="gather_small",
        out_shape=jax.ShapeDtypeStruct((8, srows, cols), small.dtype),
        in_specs=[ANY], out_specs=ANY,
        scratch_shapes=[pltpu.SemaphoreType.DMA((8,)), pltpu.SemaphoreType.DMA((8,)), pltpu.SemaphoreType.DMA],
    )(small)


SHARDED = (("w_in", D_MODEL, IN_COLS, 1), ("w_up_a", ATT_WIDTH, D_MODEL, 1), ("w_up_b", ATT_WIDTH, D_MODEL, 1),
           ("w_out", D_MODEL, D_MODEL, 0), ("w_q_mem", D_MODEL, MEM_WIDTH, 0), ("w_kv_mem", D_MODEL, 2 * MEM_WIDTH, 0),
           ("w_o_mem", MEM_WIDTH, D_MODEL, 1), ("w_ffn_gate", D_MODEL, D_FF, 1), ("w_ffn_up", D_MODEL, D_FF, 1),
           ("w_ffn_down", D_FF, D_MODEL, 0))
NAMES = tuple(n for n, _, _, _ in SHARDED)
EARLY, LATE = NAMES[:1], NAMES[1:]
GAINS = ("g_mix", "g_mem_q", "g_mem_kv", "g_ffn", "g_final")


def _natural(w3):
    n, r, c = w3.shape
    return w3.reshape(n * r, c)


def _shard_major(g, axis):
    if axis == 1:
        return g
    r, c = g.shape
    return g.reshape(N_CHIPS, r // N_CHIPS, c)


def kernel(x, mem, positions, g_mix, w_in, w_up_a, w_up_b, w_out, g_mem_q, g_mem_kv, w_q_mem, w_kv_mem, w_o_mem, g_ffn, w_ffn_gate, w_ffn_up, w_ffn_down, g_final, loss_target, m_g_mix, m_w_in, m_w_up_a, m_w_up_b, m_w_out, m_g_mem_q, m_g_mem_kv, m_w_q_mem, m_w_kv_mem, m_w_o_mem, m_g_ffn, m_w_ffn_gate, m_w_ffn_up, m_w_ffn_down, m_g_final, v_g_mix, v_w_in, v_w_up_a, v_w_up_b, v_w_out, v_g_mem_q, v_g_mem_kv, v_w_q_mem, v_w_kv_mem, v_w_o_mem, v_g_ffn, v_w_ffn_gate, v_w_ffn_up, v_w_ffn_down, v_g_final):
    given = dict(locals())
    shards = {n: given[n][0] for n, _, _, _ in SHARDED}

    wf = dict(zip(EARLY, _gather_weights([shards[n].astype(BF16) for n in EARLY])))
    late_shards = [shards[n].astype(BF16) for n in LATE]
    c_idx = lax.axis_index("c").astype(jnp.int32).reshape(1)
    s_idx = (2 * lax.axis_index("x") + lax.axis_index("y")).astype(jnp.int32).reshape(1)

    loss_row, grad_x, mine, gain_grads = _local_step(x, mem, positions, loss_target, g_mix, g_mem_q, g_mem_kv,
                                                     g_ffn, g_final, wf, late_shards, (c_idx, s_idx))
    return _reduce_and_update(given, shards, loss_row, grad_x, mine, gain_grads, c_idx)


def _reduce_halves(glist, names, c_idx, s_idx, pair_exchange, chip_exchange):
    theirs = pair_exchange(glist)
    pairs = [_pair_sum(g, t, c_idx, name="pair_sum_" + n) for n, g, t in zip(names, glist, theirs)]
    recv = chip_exchange(pairs)
    return [_chip_sum(p, r, s_idx, name="chip_sum_" + n) for n, p, r in zip(names, pairs, recv)]


def _local_step(x, mem, positions, loss_target, g_mix, g_mem_q, g_mem_kv, g_ffn, g_final, wf,
                late_shards=None, place=None):
    b_dim, s_dim, d = x.shape
    t_dim = b_dim * s_dim
    n_mem = mem.shape[1]
    wf = dict(wf)

    xb = x.reshape(t_dim, d)
    tgt = loss_target.reshape(t_dim, d)
    memf = mem.reshape(b_dim * n_mem, d)
    gfin = g_final.reshape(1, d)
    pos = positions.reshape(t_dim, 1).astype(F32)

    lane = jnp.arange(LANES) % HEAD_DIM
    half = ROPE_DIM // 2
    inv_freq = ROPE_THETA ** (-jnp.arange(half, dtype=F32) / half)
    inv_lane = jnp.where(lane < ROPE_DIM, inv_freq[lane % half], 0.0).reshape(1, -1).astype(F32)
    sel_a = (lane < half).astype(F32).reshape(1, -1)
    sel_b = ((lane >= half) & (lane < ROPE_DIM)).astype(F32).reshape(1, -1)

    def rows3(t):
        return t.reshape(b_dim, s_dim, t.shape[-1])

    def rows2(t):
        return t.reshape(t_dim, t.shape[-1])

    n1 = _rms_fwd(xb, g_mix, name="rms_mix")
    proj = _mm_cs(n1, wf["w_in"], name="mm_in")
    proj3 = rows3(proj)
    cs, sn = _rope_table(pos, inv_lane, sel_a, sel_b)
    cs3, sn3 = rows3(cs), rows3(sn)
    (oa16, oa32, lse_a), _ = _dil_fwd(proj3, cs3, sn3, sel_a, sel_b)
    ob16, gathered = _sb_fwd(proj3, ("gather", late_shards) if late_shards else None)
    wf.update(zip(LATE, gathered))
    w_out, w_q, w_kv = _natural(wf["w_out"]), _natural(wf["w_q_mem"]), _natural(wf["w_kv_mem"])
    oa, ob = rows2(oa16), rows2(ob16)
    ua = _mm_cs(oa, wf["w_up_a"], name="mm_up_a")
    ub = _mm_cs(ob, wf["w_up_b"], name="mm_up_b")
    mixed = _gate_fwd(proj, ua, ub)
    h1 = _mm(mixed, w_out, name="mm_out", add=xb)

    hn = _rms_fwd(h1, g_mem_q, name="rms_mem_q")
    memn = _rms_fwd(memf, g_mem_kv, name="rms_mem_kv")
    qm = _mm(hn, w_q, name="mm_q_mem", out_dtype=BF16)
    kvm = _mm(memn, w_kv, name="mm_kv_mem", out_dtype=BF16)
    qm3, kvm3 = rows3(qm), kvm.reshape(b_dim, n_mem, 2 * MEM_WIDTH)
    om = rows2(_mem_fwd(qm3, kvm3))
    h2 = _mm_cs(om, wf["w_o_mem"], name="mm_o_mem", add=h1)

    n3 = _rms_fwd(h2, g_ffn, name="rms_ffn")
    gate3 = _mm_ffn_up(n3, wf["w_ffn_gate"], name="mm_gate")
    up3 = _mm_ffn_up(n3, wf["w_ffn_up"], name="mm_up")
    act3 = _swiglu_fwd(gate3, up3)
    h3 = _mm_ffn_down(act3, wf["w_ffn_down"], name="mm_down", add=h2)
    loss_row, dh3, dg_final = _final(h3, gfin, tgt)

    grads = {}
    dact3 = _mm_ffn_down_dx(dh3, wf["w_ffn_down"], name="mm_down_dx")
    grads["w_ffn_down"] = _mm_ffn_down_dw(act3, dh3, name="mm_down_dw")
    dgate3, dup3 = _swiglu_bwd(gate3, up3, dact3)
    grads["w_ffn_gate"] = _mm_ffn_up_dw(n3, dgate3, name="mm_gate_dw")
    grads["w_ffn_up"] = _mm_ffn_up_dw(n3, dup3, name="mm_up_dw")
    dn3 = _mm_ffn_up_dx(dgate3, wf["w_ffn_gate"], name="mm_gate_dx")
    dn3 = _mm_ffn_up_dx(dup3, wf["w_ffn_up"], name="mm_up_dx", add=dn3)
    dh2, dg_ffn = _rms_bwd(h2, g_ffn, dn3, dh3, name="rms_ffn_bwd")

    dom = _mm_cs_dx(dh2, wf["w_o_mem"], name="mm_o_mem_dx", out_dtype=BF16)
    grads["w_o_mem"] = _mm_cs_dw(om, dh2, name="mm_o_mem_dw")
    dqm, dkm, dvm = _mem_bwd(qm3, kvm3, rows3(dom))
    dqm = rows2(dqm)
    dkvm = jnp.concatenate([dkm, dvm], axis=-1).reshape(b_dim * n_mem, 2 * MEM_WIDTH).astype(BF16)
    grads["w_q_mem"] = _shard_major(_mm(hn, dqm, name="mm_q_mem_dw", ta=True), 0)
    dhn = _mm(dqm, w_q, name="mm_q_mem_dx", tb=True)
    grads["w_kv_mem"] = _shard_major(_mm(memn, dkvm, name="mm_kv_mem_dw", ta=True), 0)
    dmemn = _mm(dkvm, w_kv, name="mm_kv_mem_dx", tb=True)
    _, dg_mem_kv = _rms_bwd(memf, g_mem_kv, dmemn, None, name="rms_mem_kv_bwd")
    dh1, dg_mem_q = _rms_bwd(h1, g_mem_q, dhn, dh2, name="rms_mem_q_bwd")

    dmix = _mm(dh1, w_out, name="mm_out_dx", tb=True)
    grads["w_out"] = _shard_major(_mm(mixed, dh1, name="mm_out_dw", ta=True), 0)
    dua, dub, dgates = _gate_bwd(proj, ua, ub, dmix)
    doa = _mm_cs_dx(dua, wf["w_up_a"], name="mm_up_a_dx")
    grads["w_up_a"] = _mm_cs_dw(oa, dua, name="mm_up_a_dw")
    dob = _mm_cs_dx(dub, wf["w_up_b"], name="mm_up_b_dx", out_dtype=BF16)
    grads["w_up_b"] = _mm_cs_dw(ob, dub, name="mm_up_b_dw")

    att = {}

    def dil_with_pairs(glist):
        att["a"], theirs = _dil_bwd(proj3, cs3, sn3, sel_a, sel_b, rows3(doa), oa32, lse_a,
                                    ("pair", glist) if glist else None)
        return theirs

    def sb_with_chips(pairs):
        att["b"], recv = _sb_bwd(proj3, rows3(dob), ("chip", pairs) if pairs else None)
        return recv

    if place is None:
        dil_with_pairs(())
        sb_with_chips(())
    else:
        mine_late = _reduce_halves([grads[n] for n in LATE], LATE, *place, dil_with_pairs, sb_with_chips)
    dproj = jnp.concatenate([rows2(t) for t in att["a"] + att["b"]] + [dgates], axis=1)
    grads["w_in"] = _mm_cs_dw(n1, dproj, name="mm_in_dw")
    dn1 = _mm_cs_dx(dproj, wf["w_in"], name="mm_in_dx")
    dx, dg_mix = _rms_bwd(xb, g_mix, dn1, dh1, name="rms_mix_bwd")
    grad_x = dx.reshape(b_dim, s_dim, d)
    gains = (dg_mix, dg_mem_q, dg_mem_kv, dg_ffn, dg_final)
    if place is None:
        return loss_row, grad_x, grads, gains
    mine_early = _reduce_halves([grads[n] for n in EARLY], EARLY, *place, _pair_exchange, _chip_exchange)
    return loss_row, grad_x, mine_early + mine_late, gains


def _reduce_and_update(given, shards, loss_row, grad_x, mine, gain_grads, c_idx):
    d = D_MODEL
    dg_mix, dg_mem_q, dg_mem_kv, dg_ffn, dg_final = gain_grads
    small = jnp.concatenate([dg_mix, dg_mem_q, dg_mem_kv, dg_ffn, dg_final,
                             jnp.pad(loss_row, ((0, 0), (0, FLAT_COLS - LANES))), jnp.zeros((2, FLAT_COLS), F32)], axis=0)
    small_all = _gather_small(small)
    others = _swap_halves(mine)
    small_sum = _sum8(small_all)
    loss = small_sum[5, 0]

    out_g, out_d, out_m, out_v = {}, {}, {}, {}
    for n, mine_n, other_n in zip(NAMES, mine, others):
        g2, dl, nm, nv = _adamw_halves(shards[n], mine_n, other_n, given["m_" + n][0], given["v_" + n][0], c_idx,
                                       name="adamw_" + n)
        out_g[n], out_d[n], out_m[n], out_v[n] = g2[None], dl[None], nm[None], nv[None]
    gain_w = jnp.concatenate([given[n].reshape(1, d) for n in GAINS], axis=0)
    gain_m = jnp.concatenate([given["m_" + n].reshape(1, d) for n in GAINS], axis=0)
    gain_v = jnp.concatenate([given["v_" + n].reshape(1, d) for n in GAINS], axis=0)
    gain_g = small_sum[:len(GAINS)]
    gd, gm, gv = _adamw(gain_w, gain_g, gain_m, gain_v, name="adamw_gains")
    for i, n in enumerate(GAINS):
        shape = given[n].shape
        out_g[n], out_d[n] = gain_g[i].reshape(shape), gd[i].reshape(shape)
        out_m[n], out_v[n] = gm[i].reshape(shape), gv[i].reshape(shape)

    order = ["g_mix", "w_in", "w_up_a", "w_up_b", "w_out", "g_mem_q", "g_mem_kv", "w_q_mem", "w_kv_mem", "w_o_mem",
             "g_ffn", "w_ffn_gate", "w_ffn_up", "w_ffn_down", "g_final"]
    return (loss, grad_x, *[out_g[n] for n in order], *[out_d[n] for n in order],
            *[out_m[n] for n in order], *[out_v[n] for n in order])
```

```python
import jax
import jax.numpy as jnp
from jax import lax
from jax.experimental import pallas as pl
from jax.experimental.pallas import tpu as pltpu

F32 = jnp.float32
BF16 = jnp.bfloat16
MESH = pl.DeviceIdType.MESH

D_MODEL = 1024
HEAD_DIM = 64
N_HEADS = 8
ATT_WIDTH = N_HEADS * HEAD_DIM
DIL_PATTERNS = ((128, 1), (512, 4), (2048, 16))
BLOCK = 128
SB_ROWS = 512
SB_STEP = 4
ROPE_THETA = 500000.0
ROPE_DIM = HEAD_DIM // 4
N_HEADS_MEM = 4
MEM_HEAD_DIM = 128
MEM_WIDTH = N_HEADS_MEM * MEM_HEAD_DIM
D_FF = 2816
IN_COLS = 6 * ATT_WIDTH + 2 * D_MODEL
RMS_EPS = 1e-6
ADAM_LR = 0.001
ADAM_B1 = 0.9
ADAM_B2 = 0.999
ADAM_EPS = 1e-08
ADAM_WD = 0.01
ADAM_STEP = 10

N_CHIPS = 4
LANES = 128
FLAT_COLS = 1024
VMEM_LIMIT = 56 * 1024 * 1024

PAIRS = ATT_WIDTH // LANES
COL_QA, COL_KA, COL_VA, COL_QB, COL_KB, COL_VB = (i * PAIRS for i in range(6))

MM_CAP = 1408
TOK_CAP = 2048
NN = (((1,), (0,)), ((), ()))
NT = (((1,), (1,)), ((), ()))
TN = (((0,), (0,)), ((), ()))
BNN = (((2,), (1,)), ((0,), (0,)))
BNT = (((2,), (2,)), ((0,), (0,)))
BTN = (((1,), (1,)), ((0,), (0,)))
DIL_BATCH = 4


def _tile(dim, cap, unit=LANES):
    if dim <= cap:
        return dim
    best = None
    for t in range(unit, cap + 1, unit):
        if dim % t == 0:
            best = t
    assert best is not None, (dim, cap)
    return best


def _params(sem):
    return pltpu.CompilerParams(dimension_semantics=sem, vmem_limit_bytes=VMEM_LIMIT)


def _mm(a, b, *, name, ta=False, tb=False, add=None, out_dtype=F32,
        tm_cap=MM_CAP, tn_cap=MM_CAP, tk_cap=MM_CAP):
    if ta:
        k_dim, m_dim = a.shape
    else:
        m_dim, k_dim = a.shape
    if tb:
        n_dim, kb = b.shape
    else:
        kb, n_dim = b.shape
    assert kb == k_dim, (a.shape, b.shape, ta, tb)
    tm, tn, tk = _tile(m_dim, tm_cap), _tile(n_dim, tn_cap), _tile(k_dim, tk_cap)
    nk = k_dim // tk
    dims = (((0 if ta else 1,), (1 if tb else 0,)), ((), ()))
    has_add = add is not None

    def body(*refs):
        if has_add:
            a_ref, b_ref, add_ref, o_ref = refs[:4]
        else:
            a_ref, b_ref, o_ref = refs[:3]
        part = lax.dot_general(a_ref[...].astype(BF16), b_ref[...].astype(BF16), dims, preferred_element_type=F32)

        def finish(r):
            if has_add:
                r = add_ref[...] + r
            o_ref[...] = r.astype(out_dtype)

        if nk == 1:
            finish(part)
            return
        acc_ref = refs[-1]
        k = pl.program_id(2)

        @pl.when(k == 0)
        def _():
            acc_ref[...] = part

        @pl.when(k > 0)
        def _():
            acc_ref[...] += part

        @pl.when(k == nk - 1)
        def _():
            finish(acc_ref[...])

    a_spec = pl.BlockSpec((tk, tm), lambda i, j, k: (k, i)) if ta else pl.BlockSpec((tm, tk), lambda i, j, k: (i, k))
    b_spec = pl.BlockSpec((tn, tk), lambda i, j, k: (j, k)) if tb else pl.BlockSpec((tk, tn), lambda i, j, k: (k, j))
    o_spec = pl.BlockSpec((tm, tn), lambda i, j, k: (i, j))
    in_specs = [a_spec, b_spec] + ([o_spec] if has_add else [])
    args = (a, b) + ((add,) if has_add else ())
    return pl.pallas_call(
        body, name=name, grid=(m_dim // tm, n_dim // tn, nk),
        in_specs=in_specs, out_specs=o_spec,
        out_shape=jax.ShapeDtypeStruct((m_dim, n_dim), out_dtype),
        scratch_shapes=[pltpu.VMEM((tm, tn), F32)] if nk > 1 else [],
        compiler_params=_params(("parallel", "parallel", "arbitrary")),
    )(*args)


def _mm_core(name, a, b, a_spec, b_spec, o_spec, out_shape, grid, dims, *, add=None, out_dtype=F32):
    nk = grid[2]
    has_add = add is not None
    acc_shape = tuple(d for d in o_spec.block_shape if d is not None)

    def body(*refs):
        if has_add:
            a_ref, b_ref, add_ref, o_ref = refs[:4]
        else:
            a_ref, b_ref, o_ref = refs[:3]
        part = lax.dot_general(a_ref[...].astype(BF16), b_ref[...].astype(BF16), dims, preferred_element_type=F32)

        def finish(r):
            if has_add:
                r = add_ref[...] + r
            o_ref[...] = r.astype(out_dtype)

        if nk == 1:
            finish(part)
            return
        acc_ref = refs[-1]
        k = pl.program_id(2)

        @pl.when(k == 0)
        def _():
            acc_ref[...] = part

        @pl.when(k > 0)
        def _():
            acc_ref[...] += part

        @pl.when(k == nk - 1)
        def _():
            finish(acc_ref[...])

    in_specs = [a_spec, b_spec] + ([o_spec] if has_add else [])
    args = (a, b) + ((add,) if has_add else ())
    return pl.pallas_call(
        body, name=name, grid=grid, in_specs=in_specs, out_specs=o_spec,
        out_shape=jax.ShapeDtypeStruct(out_shape, out_dtype),
        scratch_shapes=[pltpu.VMEM(acc_shape, F32)] if nk > 1 else [],
        compiler_params=_params(("parallel", "parallel", "arbitrary")),
    )(*args)


def _mm_cs(a, w3, *, name, add=None, out_dtype=F32):
    m_dim, k_dim = a.shape
    _, _, n4 = w3.shape
    tm, tn, tk = _tile(m_dim, MM_CAP if add is not None else TOK_CAP), _tile(n4, MM_CAP), _tile(k_dim, MM_CAP)
    npb = n4 // tn
    return _mm_core(name, a, w3,
                    pl.BlockSpec((tm, tk), lambda i, j, k: (i, k)),
                    pl.BlockSpec((None, tk, tn), lambda i, j, k: (j // npb, k, j % npb)),
                    pl.BlockSpec((tm, tn), lambda i, j, k: (i, j)),
                    (m_dim, N_CHIPS * n4), (m_dim // tm, N_CHIPS * npb, k_dim // tk), NN, add=add, out_dtype=out_dtype)


def _mm_cs_dx(dy, w3, *, name, out_dtype=F32):
    m_dim, _ = dy.shape
    _, k_dim, n4 = w3.shape
    tm, tkw, tn = _tile(m_dim, MM_CAP), _tile(k_dim, MM_CAP), _tile(n4, MM_CAP)
    npb = n4 // tn
    return _mm_core(name, dy, w3,
                    pl.BlockSpec((tm, tn), lambda i, j, k: (i, k)),
                    pl.BlockSpec((None, tkw, tn), lambda i, j, k: (k // npb, j, k % npb)),
                    pl.BlockSpec((tm, tkw), lambda i, j, k: (i, j)),
                    (m_dim, k_dim), (m_dim // tm, k_dim // tkw, N_CHIPS * npb), NT, out_dtype=out_dtype)


def _mm_cs_dw(a, dy, *, name):
    m_dim, k_dim = a.shape
    n4 = dy.shape[1] // N_CHIPS
    tmk, tn, tk = _tile(k_dim, MM_CAP), _tile(n4, MM_CAP), _tile(m_dim, TOK_CAP)
    npb = n4 // tn
    return _mm_core(name, a, dy,
                    pl.BlockSpec((tk, tmk), lambda i, j, k: (k, i)),
                    pl.BlockSpec((tk, tn), lambda i, j, k: (k, j)),
                    pl.BlockSpec((None, tmk, tn), lambda i, j, k: (j // npb, i, j % npb)),
                    (N_CHIPS, k_dim, n4), (k_dim // tmk, N_CHIPS * npb, m_dim // tk), TN)


def _mm_ffn_up_dw(n, d3, *, name):
    t_dim, d = n.shape
    _, _, f4 = d3.shape
    tk = _tile(t_dim, TOK_CAP)
    return _mm_core(name, n, d3,
                    pl.BlockSpec((tk, d), lambda i, j, k: (k, 0)),
                    pl.BlockSpec((None, tk, f4), lambda i, j, k: (j, k, 0)),
                    pl.BlockSpec((None, d, f4), lambda i, j, k: (j, 0, 0)),
                    (N_CHIPS, d, f4), (1, N_CHIPS, t_dim // tk), TN)


def _mm_ffn_up_dx(d3, w3, *, name, add=None):
    _, t_dim, f4 = d3.shape
    _, d, _ = w3.shape
    tm = _tile(t_dim, MM_CAP)
    return _mm_core(name, d3, w3,
                    pl.BlockSpec((None, tm, f4), lambda i, j, k: (k, i, 0)),
                    pl.BlockSpec((None, d, f4), lambda i, j, k: (k, 0, 0)),
                    pl.BlockSpec((tm, d), lambda i, j, k: (i, 0)),
                    (t_dim, d), (t_dim // tm, 1, N_CHIPS), NT, add=add)


def _mm_ffn_down(act3, wd3, *, name, add):
    _, t_dim, f4 = act3.shape
    _, _, d = wd3.shape
    tm = _tile(t_dim, MM_CAP)
    return _mm_core(name, act3, wd3,
                    pl.BlockSpec((None, tm, f4), lambda i, j, k: (k, i, 0)),
                    pl.BlockSpec((None, f4, d), lambda i, j, k: (k, 0, 0)),
                    pl.BlockSpec((tm, d), lambda i, j, k: (i, 0)),
                    (t_dim, d), (t_dim // tm, 1, N_CHIPS), NN, add=add)


def _mm_ffn_down_dw(act3, dh, *, name):
    _, t_dim, f4 = act3.shape
    d = dh.shape[1]
    tk = _tile(t_dim, TOK_CAP)
    return _mm_core(name, act3, dh,
                    pl.BlockSpec((None, tk, f4), lambda i, j, k: (i, k, 0)),
                    pl.BlockSpec((tk, d), lambda i, j, k: (k, 0)),
                    pl.BlockSpec((None, f4, d), lambda i, j, k: (i, 0, 0)),
                    (N_CHIPS, f4, d), (N_CHIPS, 1, t_dim // tk), TN)


def _rms_fwd(x, g, *, name, tt=512):
    t_dim, d = x.shape
    tt = _tile(t_dim, tt, 8)

    def body(x_ref, g_ref, o_ref):
        xv = x_ref[...]
        r = lax.rsqrt(jnp.mean(xv * xv, axis=-1, keepdims=True) + RMS_EPS)
        o_ref[...] = ((xv * r) * g_ref[...]).astype(o_ref.dtype)

    return pl.pallas_call(
        body, name=name, grid=(t_dim // tt,),
        in_specs=[pl.BlockSpec((tt, d), lambda i: (i, 0)), pl.BlockSpec((1, d), lambda i: (0, 0))],
        out_specs=pl.BlockSpec((tt, d), lambda i: (i, 0)),
        out_shape=jax.ShapeDtypeStruct((t_dim, d), BF16),
        compiler_params=_params(("parallel",)),
    )(x, g)


def _rms_bwd(x, g, dy, add, *, name, tt=512):
    t_dim, d = x.shape
    tt = _tile(t_dim, tt, 8)
    has_add = add is not None

    def body(*refs):
        if has_add:
            x_ref, g_ref, dy_ref, add_ref, dx_ref, dg_ref = refs
        else:
            x_ref, g_ref, dy_ref, dx_ref, dg_ref = refs
        xv = x_ref[...]
        dyv = dy_ref[...].astype(F32)
        r = lax.rsqrt(jnp.mean(xv * xv, axis=-1, keepdims=True) + RMS_EPS)
        xh = xv * r
        u = dyv * g_ref[...]
        dx = r * (u - xh * jnp.mean(u * xh, axis=-1, keepdims=True))
        if has_add:
            dx = add_ref[...] + dx
        dx_ref[...] = dx

        @pl.when(pl.program_id(0) == 0)
        def _():
            dg_ref[...] = jnp.zeros_like(dg_ref)

        dg_ref[...] += jnp.sum(dyv * xh, axis=0, keepdims=True)

    row = pl.BlockSpec((tt, d), lambda i: (i, 0))
    vec = pl.BlockSpec((1, d), lambda i: (0, 0))
    in_specs = [row, vec, row] + ([row] if has_add else [])
    args = (x, g, dy) + ((add,) if has_add else ())
    return pl.pallas_call(
        body, name=name, grid=(t_dim // tt,),
        in_specs=in_specs, out_specs=[row, vec],
        out_shape=[jax.ShapeDtypeStruct((t_dim, d), F32), jax.ShapeDtypeStruct((1, d), F32)],
        compiler_params=_params(("arbitrary",)),
    )(*args)


def _final(h, g, target, *, tt=512):
    t_dim, d = h.shape
    n_steps = t_dim // tt

    def body(h_ref, g_ref, t_ref, loss_ref, dh_ref, dg_ref, sq_ref):
        i = pl.program_id(0)
        xv = h_ref[...]
        gv = g_ref[...]
        r = lax.rsqrt(jnp.mean(xv * xv, axis=-1, keepdims=True) + RMS_EPS)
        xh = xv * r
        err = xh * gv - t_ref[...]
        dyv = err * (1.0 / d)
        u = dyv * gv
        dh_ref[...] = r * (u - xh * jnp.mean(u * xh, axis=-1, keepdims=True))

        @pl.when(i == 0)
        def _():
            dg_ref[...] = jnp.zeros_like(dg_ref)
            sq_ref[...] = jnp.zeros_like(sq_ref)

        dg_ref[...] += jnp.sum(dyv * xh, axis=0, keepdims=True)
        sq_ref[...] += jnp.sum(err * err, axis=0, keepdims=True)

        @pl.when(i == n_steps - 1)
        def _():
            total = jnp.sum(sq_ref[...], axis=-1, keepdims=True) * (0.5 / d)
            loss_ref[...] = jnp.broadcast_to(total, loss_ref.shape)

    row = pl.BlockSpec((tt, d), lambda i: (i, 0))
    vec = pl.BlockSpec((1, d), lambda i: (0, 0))
    return pl.pallas_call(
        body, name="final_loss", grid=(n_steps,),
        in_specs=[row, vec, row],
        out_specs=[pl.BlockSpec((1, LANES), lambda i: (0, 0)), row, vec],
        out_shape=[jax.ShapeDtypeStruct((1, LANES), F32), jax.ShapeDtypeStruct((t_dim, d), F32),
                   jax.ShapeDtypeStruct((1, d), F32)],
        scratch_shapes=[pltpu.VMEM((1, d), F32)],
        compiler_params=_params(("arbitrary",)),
    )(h, g, target)


def _rope_table(pos, inv_lane, sel_a, sel_b, *, tt=512):
    t_dim = pos.shape[0]

    def body(p_ref, f_ref, a_ref, b_ref, c_ref, s_ref):
        ang = p_ref[...] * f_ref[...]
        on = (a_ref[...] + b_ref[...]) > 0.0
        c_ref[...] = jnp.where(on, jnp.cos(ang), 1.0)
        s_ref[...] = jnp.where(on, jnp.sin(ang), 0.0)

    vec = pl.BlockSpec((1, LANES), lambda i: (0, 0))
    row = pl.BlockSpec((tt, LANES), lambda i: (i, 0))
    shp = jax.ShapeDtypeStruct((t_dim, LANES), F32)
    return pl.pallas_call(
        body, name="rope_table", grid=(t_dim // tt,),
        in_specs=[pl.BlockSpec((tt, 1), lambda i: (i, 0)), vec, vec, vec],
        out_specs=[row, row], out_shape=[shp, shp],
        compiler_params=_params(("parallel",)),
    )(pos, inv_lane, sel_a, sel_b)


def _rotate(xv, cs, sn, sa, sb):
    half = ROPE_DIM // 2
    up = pltpu.roll(xv, LANES - half, 1)
    dn = pltpu.roll(xv, half, 1)
    return xv * cs + (dn * sb - up * sa) * sn


def _head_masks():
    h1 = lax.broadcasted_iota(jnp.int32, (1, LANES), 1) < HEAD_DIM
    return h1, jnp.logical_not(h1)


def _split_heads(xv, h1, h2):
    return jnp.where(h1, xv, 0.0).astype(BF16), jnp.where(h2, xv, 0.0).astype(BF16)


def _tri_masks():
    r = lax.broadcasted_iota(jnp.int32, (BLOCK, BLOCK), 0)
    c = lax.broadcasted_iota(jnp.int32, (BLOCK, BLOCK), 1)
    return c <= r, r <= c


def _stream_rows(start, dil):
    if dil == 1:
        return pl.ds(pl.multiple_of(start, BLOCK), BLOCK)
    return pl.ds(start, BLOCK, stride=dil)


def _dil_tile(idx, dil, nb):
    r = idx // nb
    n = idx % nb
    return (_stream_rows(r + dil * BLOCK * n, dil), _stream_rows(r + dil * BLOCK * jnp.maximum(n - 1, 0), dil),
            n > 0)


def _dil_specs(b_dim, s_dim):
    def col(c0):
        return pl.BlockSpec((None, s_dim, LANES),lambda b, h: (b, 0, c0 + h))
    tab = pl.BlockSpec((None, s_dim, LANES),lambda b, h: (b, 0, 0))
    vec = pl.BlockSpec((1, LANES), lambda b, h: (0, 0))
    return col, tab, vec


def _dil_fwd(proj3, cs3, sn3, sel_a, sel_b, rider=None):
    b_dim, s_dim, _ = proj3.shape
    scale = HEAD_DIM ** -0.5
    n_pat = len(DIL_PATTERNS)
    extra, extra_shapes, extra_sems = _rider_parts(rider)
    n_w = len(extra)
    n_steps = b_dim * PAIRS

    def body(*refs):
        q_ref, k_ref, v_ref, cs_ref, sn_ref, sa_ref, sb_ref = refs[:7]
        o16_ref, o32_ref, l_ref = refs[7 + n_w:10 + n_w]
        qr, kr = refs[10 + 2 * n_w:12 + 2 * n_w]
        per_pattern = refs[12 + 2 * n_w:12 + 2 * n_w + 2 * n_pat]
        og, lg = per_pattern[:n_pat], per_pattern[n_pat:]
        step = pl.program_id(0) * PAIRS + pl.program_id(1)
        begin, end = _rider_hooks(rider, refs[7:7 + n_w], refs[10 + n_w:10 + 2 * n_w], refs[-2:], step, n_steps)
        begin()
        h1, h2 = _head_masks()
        cur_ok, prev_ok = _tri_masks()
        sa, sb = sa_ref[...], sb_ref[...]

        def prep(j, _):
            rows = pl.ds(pl.multiple_of(j * BLOCK, BLOCK), BLOCK)
            cs, sn = cs_ref[rows, :], sn_ref[rows, :]
            qr[rows, :] = _rotate(q_ref[rows, :], cs, sn, sa, sb) * scale
            kr[rows, :] = _rotate(k_ref[rows, :], cs, sn, sa, sb)
            return 0

        lax.fori_loop(0, s_dim // BLOCK, prep, 0)

        for g, (_, dil) in enumerate(DIL_PATTERNS):
            nb = s_dim // dil // BLOCK

            def some(bi, _, g=g, dil=dil, nb=nb):
                tiles = [_dil_tile(bi * DIL_BATCH + t, dil, nb) for t in range(DIL_BATCH)]
                rows = [t[0] for t in tiles]
                q1, q2 = _split_heads(jnp.stack([qr[rw, :] for rw in rows]), h1, h2)
                kc = jnp.stack([kr[rw, :] for rw in rows]).astype(BF16)
                vc1, vc2 = _split_heads(jnp.stack([v_ref[rw, :] for rw in rows]), h1, h2)
                if nb > 1:
                    kp = jnp.stack([kr[t[1], :] for t in tiles]).astype(BF16)
                    vp1, vp2 = _split_heads(jnp.stack([v_ref[t[1], :] for t in tiles]), h1, h2)
                    p_ok = jnp.stack([jnp.logical_and(prev_ok, t[2]) for t in tiles])

                def head(qh, vch, vph):
                    sc = jnp.where(cur_ok, lax.dot_general(qh, kc, BNT, preferred_element_type=F32), -jnp.inf)
                    m = jnp.max(sc, axis=-1, keepdims=True)
                    if nb > 1:
                        sp = jnp.where(p_ok, lax.dot_general(qh, kp, BNT, preferred_element_type=F32), -jnp.inf)
                        m = jnp.maximum(m, jnp.max(sp, axis=-1, keepdims=True))
                    pc = jnp.exp(sc - m)
                    den = jnp.sum(pc, axis=-1, keepdims=True)
                    acc = lax.dot_general(pc.astype(BF16), vch, BNN, preferred_element_type=F32)
                    if nb > 1:
                        pp = jnp.exp(sp - m)
                        den = den + jnp.sum(pp, axis=-1, keepdims=True)
                        acc = acc + lax.dot_general(pp.astype(BF16), vph, BNN, preferred_element_type=F32)
                    return acc / den, m + jnp.log(den)

                o1, l1 = head(q1, vc1, vp1 if nb > 1 else None)
                o2, l2 = head(q2, vc2, vp2 if nb > 1 else None)
                o, l = o1 + o2, jnp.where(h1, l1, l2)
                for t, rw in enumerate(rows):
                    og[g][rw, :] = o[t]
                    lg[g][rw, :] = l[t]
                return 0

            lax.fori_loop(0, dil * nb // DIL_BATCH, some, 0)

        def comb(j, _):
            rows = pl.ds(pl.multiple_of(j * BLOCK, BLOCK), BLOCK)
            ls = [lg[g][rows, :] for g in range(n_pat)]
            m = jnp.maximum(jnp.maximum(ls[0], ls[1]), ls[2])
            es = [jnp.exp(l - m) for l in ls]
            den = es[0] + es[1] + es[2]
            o = (es[0] * og[0][rows, :] + es[1] * og[1][rows, :] + es[2] * og[2][rows, :]) / den
            o16_ref[rows, :] = o.astype(BF16)
            o32_ref[rows, :] = o
            l_ref[rows, :] = m + jnp.log(den)
            return 0

        lax.fori_loop(0, s_dim // BLOCK, comb, 0)
        end()

    col, tab, vec = _dil_specs(b_dim, s_dim)
    out = pl.BlockSpec((None, s_dim, LANES),lambda b, h: (b, 0, h))
    shp = (b_dim, s_dim, ATT_WIDTH)
    res = pl.pallas_call(
        body, name="dil_fwd", grid=(b_dim, PAIRS),
        in_specs=[col(COL_QA), col(COL_KA), col(COL_VA), tab, tab, vec, vec] + [ANY] * n_w,
        out_specs=[out, out, out] + [ANY] * n_w,
        out_shape=[jax.ShapeDtypeStruct(shp, BF16), jax.ShapeDtypeStruct(shp, F32), jax.ShapeDtypeStruct(shp, F32)]
        + extra_shapes,
        scratch_shapes=[pltpu.VMEM((s_dim, LANES), F32)] * (2 + 2 * n_pat) + extra_sems,
        compiler_params=_params(("arbitrary", "arbitrary")),
    )(proj3, proj3, proj3, cs3, sn3, sel_a, sel_b, *extra)
    return res[:3], res[3:]


def _dil_bwd(proj3, cs3, sn3, sel_a, sel_b, do3, o3, lse3, rider=None):
    b_dim, s_dim, _ = proj3.shape
    scale = HEAD_DIM ** -0.5
    extra, extra_shapes, extra_sems = _rider_parts(rider)
    n_w = len(extra)

    def body(*refs):
        q_ref, k_ref, v_ref, cs_ref, sn_ref, sa_ref, sb_ref, do_ref, o_ref, l_ref = refs[:10]
        dq_ref, dk_ref, dv_ref = refs[10 + n_w:13 + n_w]
        qr, kr, dqa, dka, dva = refs[13 + 2 * n_w:18 + 2 * n_w]
        step = pl.program_id(0) * PAIRS + pl.program_id(1)
        begin, end = _rider_hooks(rider, refs[10:10 + n_w], refs[13 + n_w:13 + 2 * n_w], refs[-2:], step,
                                  b_dim * PAIRS)
        begin()
        h1, h2 = _head_masks()
        cur_ok, prev_ok = _tri_masks()
        sa, sb = sa_ref[...], sb_ref[...]

        def prep(j, _):
            rows = pl.ds(pl.multiple_of(j * BLOCK, BLOCK), BLOCK)
            cs, sn = cs_ref[rows, :], sn_ref[rows, :]
            qr[rows, :] = _rotate(q_ref[rows, :], cs, sn, sa, sb) * scale
            kr[rows, :] = _rotate(k_ref[rows, :], cs, sn, sa, sb)
            zero = jnp.zeros((BLOCK, LANES), F32)
            dqa[rows, :] = zero
            dka[rows, :] = zero
            dva[rows, :] = zero
            return 0

        lax.fori_loop(0, s_dim // BLOCK, prep, 0)

        for _, dil in DIL_PATTERNS:
            nb = s_dim // dil // BLOCK

            def some(bi, _, dil=dil, nb=nb):
                tiles = [_dil_tile(bi * DIL_BATCH + t, dil, nb) for t in range(DIL_BATCH)]
                rows = [t[0] for t in tiles]
                q1, q2 = _split_heads(jnp.stack([qr[rw, :] for rw in rows]), h1, h2)
                dof = jnp.stack([do_ref[rw, :] for rw in rows])
                do1, do2 = _split_heads(dof, h1, h2)
                prod = dof * jnp.stack([o_ref[rw, :] for rw in rows])
                delta1 = jnp.sum(jnp.where(h1, prod, 0.0), axis=-1, keepdims=True)
                delta2 = jnp.sum(jnp.where(h2, prod, 0.0), axis=-1, keepdims=True)
                lt = jnp.stack([l_ref[rw, :] for rw in rows])
                lse1 = jnp.max(jnp.where(h1, lt, -jnp.inf), axis=-1, keepdims=True)
                lse2 = jnp.max(jnp.where(h2, lt, -jnp.inf), axis=-1, keepdims=True)

                def side(krows, ok):
                    kf = jnp.stack([kr[kw, :] for kw in krows])
                    k16 = kf.astype(BF16)
                    k1, k2 = _split_heads(kf, h1, h2)
                    v16 = jnp.stack([v_ref[kw, :] for kw in krows]).astype(BF16)

                    def head(qh, doh, lse, delta):
                        sc = lax.dot_general(qh, k16, BNT, preferred_element_type=F32)
                        p = jnp.where(ok, jnp.exp(sc - lse), 0.0)
                        dp = lax.dot_general(doh, v16, BNT, preferred_element_type=F32)
                        return p.astype(BF16), (p * (dp - delta)).astype(BF16)

                    p1, ds1 = head(q1, do1, lse1, delta1)
                    p2, ds2 = head(q2, do2, lse2, delta2)
                    dv = (lax.dot_general(p1, do1, BTN, preferred_element_type=F32)
                          + lax.dot_general(p2, do2, BTN, preferred_element_type=F32))
                    dk = (lax.dot_general(ds1, q1, BTN, preferred_element_type=F32)
                          + lax.dot_general(ds2, q2, BTN, preferred_element_type=F32))
                    for t, kw in enumerate(krows):
                        dva[kw, :] += dv[t]
                        dka[kw, :] += dk[t]
                    return (lax.dot_general(ds1, k1, BNN, preferred_element_type=F32)
                            + lax.dot_general(ds2, k2, BNN, preferred_element_type=F32))

                dq = side(rows, cur_ok)
                if nb > 1:
                    dq = dq + side([t[1] for t in tiles], jnp.stack([jnp.logical_and(prev_ok, t[2]) for t in tiles]))
                for t, rw in enumerate(rows):
                    dqa[rw, :] += dq[t] * scale
                return 0

            lax.fori_loop(0, dil * nb // DIL_BATCH, some, 0)

        def finish(j, _):
            rows = pl.ds(pl.multiple_of(j * BLOCK, BLOCK), BLOCK)
            cs, sn = cs_ref[rows, :], -sn_ref[rows, :]
            dq_ref[rows, :] = _rotate(dqa[rows, :], cs, sn, sa, sb).astype(BF16)
            dk_ref[rows, :] = _rotate(dka[rows, :], cs, sn, sa, sb).astype(BF16)
            dv_ref[rows, :] = dva[rows, :].astype(BF16)
            return 0

        lax.fori_loop(0, s_dim // BLOCK, finish, 0)
        end()

    col, tab, vec = _dil_specs(b_dim, s_dim)
    out = pl.BlockSpec((None, s_dim, LANES),lambda b, h: (b, 0, h))
    shp = jax.ShapeDtypeStruct((b_dim, s_dim, ATT_WIDTH), BF16)
    acc = pltpu.VMEM((s_dim, LANES), F32)
    res = pl.pallas_call(
        body, name="dil_bwd", grid=(b_dim, PAIRS),
        in_specs=[col(COL_QA), col(COL_KA), col(COL_VA), tab, tab, vec, vec, out, out, out] + [ANY] * n_w,
        out_specs=[out, out, out] + [ANY] * n_w, out_shape=[shp, shp, shp] + extra_shapes,
        scratch_shapes=[acc, acc, acc, acc, acc] + extra_sems,
        compiler_params=_params(("arbitrary", "arbitrary")),
    )(proj3, proj3, proj3, cs3, sn3, sel_a, sel_b, do3, o3, lse3, *extra)
    return res[:3], res[3:]


def _split_dot(x, tri):
    hi = x.astype(BF16)
    lo = (x - hi.astype(F32)).astype(BF16)
    return jnp.dot(hi, tri, preferred_element_type=F32) + jnp.dot(lo, tri, preferred_element_type=F32)


def _log_sigmoid(z):
    return jnp.minimum(z, 0.0) - jnp.log(1.0 + jnp.exp(-jnp.abs(z)))


def _sb_scores(qh, k16, valid):
    z = lax.dot_general(qh, k16, NT, preferred_element_type=F32)
    ls = _log_sigmoid(z)
    l1m = ls - z
    return ls, (l1m if valid is None else jnp.where(valid, l1m, 0.0))


def _sb_consts():
    r = lax.broadcasted_iota(jnp.int32, (BLOCK, BLOCK), 0)
    c = lax.broadcasted_iota(jnp.int32, (BLOCK, BLOCK), 1)
    after = (r > c).astype(BF16)
    before = (r < c).astype(BF16)
    qrow = lax.broadcasted_iota(jnp.int32, (SB_ROWS, BLOCK), 0)
    kcol = lax.broadcasted_iota(jnp.int32, (SB_ROWS, BLOCK), 1)
    return after, before, qrow, kcol


def _below(whole, lo, delta):
    if lo == 0:
        return whole + delta
    return whole + jnp.concatenate([jnp.zeros((lo,) + delta.shape[1:], delta.dtype), delta], axis=0)


def _pairs_loop(n_blocks, step, carry):
    def several(i, c):
        for j in range(SB_STEP):
            c = step(SB_STEP * i + j, c)
        return c

    return lax.fori_loop(0, n_blocks // SB_STEP, several, carry)


def _sb_fwd(proj3, rider=None):
    b_dim, s_dim, _ = proj3.shape
    scale = HEAD_DIM ** -0.5
    per = SB_ROWS // BLOCK
    extra, extra_shapes, extra_sems = _rider_parts(rider)
    n_w = len(extra)

    def body(*refs):
        q_ref, k_ref, v_ref = refs[:3]
        o_ref = refs[3 + n_w]
        step = pl.program_id(0) * PAIRS + pl.program_id(1)
        begin, end = _rider_hooks(rider, refs[3:3 + n_w], refs[4 + n_w:4 + 2 * n_w], refs[-2:], step, b_dim * PAIRS)
        begin()
        h1, h2 = _head_masks()
        after, _, qrow, kcol = _sb_consts()

        def qloop(qi, _):
            rows = pl.ds(pl.multiple_of(qi * SB_ROWS, SB_ROWS), SB_ROWS)
            q1, q2 = _split_heads(q_ref[rows, :] * scale, h1, h2)
            first = qi * per

            def block(kb, carry, lo):
                acc, run1, run2 = carry
                krows = pl.ds(pl.multiple_of(kb * BLOCK, BLOCK), BLOCK)
                k16 = k_ref[krows, :].astype(BF16)
                v1, v2 = _split_heads(v_ref[krows, :], h1, h2)
                valid = None if lo is None else kcol[:SB_ROWS - lo] < qrow[:SB_ROWS - lo]
                lo = lo or 0

                def head(qh, vh, run):
                    ls, l1m = _sb_scores(qh[lo:], k16, valid)
                    a = jnp.exp(ls + _split_dot(l1m, after) + run[lo:])
                    if valid is not None:
                        a = jnp.where(valid, a, 0.0)
                    return (jnp.dot(a.astype(BF16), vh, preferred_element_type=F32),
                            _below(run, lo, jnp.sum(l1m, axis=-1, keepdims=True)))

                o1, run1 = head(q1, v1, run1)
                o2, run2 = head(q2, v2, run2)
                return _below(acc, lo, o1 + o2), run1, run2

            zcol = jnp.zeros((SB_ROWS, 1), F32)
            carry = (jnp.zeros((SB_ROWS, LANES), F32), zcol, zcol)
            for kl in reversed(range(per)):
                carry = block(first + kl, carry, kl * BLOCK)
            acc, _, _ = _pairs_loop(first, lambda i, c: block(first - 1 - i, c, None), carry)
            o_ref[rows, :] = acc.astype(BF16)
            return 0

        lax.fori_loop(0, s_dim // SB_ROWS, qloop, 0)
        end()

    def col(c0):
        return pl.BlockSpec((None, s_dim, LANES),lambda b, h: (b, 0, c0 + h))

    res = pl.pallas_call(
        body, name="sb_fwd", grid=(b_dim, PAIRS),
        in_specs=[col(COL_QB), col(COL_KB), col(COL_VB)] + [ANY] * n_w, out_specs=[col(0)] + [ANY] * n_w,
        out_shape=[jax.ShapeDtypeStruct((b_dim, s_dim, ATT_WIDTH), BF16)] + extra_shapes,
        scratch_shapes=extra_sems,
        compiler_params=_params(("arbitrary", "arbitrary")),
    )(proj3, proj3, proj3, *extra)
    return res[0], res[1:]


def _sb_bwd(proj3, do3, rider=None):
    b_dim, s_dim, _ = proj3.shape
    scale = HEAD_DIM ** -0.5
    per = SB_ROWS // BLOCK
    nkb_max = s_dim // BLOCK
    extra, extra_shapes, extra_sems = _rider_parts(rider)
    n_w = len(extra)

    def body(*refs):
        q_ref, k_ref, v_ref, do_ref = refs[:4]
        dq_ref, dk_ref, dv_ref = refs[4 + n_w:7 + n_w]
        dka, dva, e_ref, sg_ref = refs[7 + 2 * n_w:11 + 2 * n_w]
        step = pl.program_id(0) * PAIRS + pl.program_id(1)
        begin, end = _rider_hooks(rider, refs[4:4 + n_w], refs[7 + n_w:7 + 2 * n_w], refs[-2:], step, b_dim * PAIRS)
        begin()
        h1, h2 = _head_masks()
        after, before, qrow, kcol = _sb_consts()
        dka[...] = jnp.zeros_like(dka)
        dva[...] = jnp.zeros_like(dva)

        def qloop(qi, _):
            rows = pl.ds(pl.multiple_of(qi * SB_ROWS, SB_ROWS), SB_ROWS)
            q1, q2 = _split_heads(q_ref[rows, :] * scale, h1, h2)
            do1, do2 = _split_heads(do_ref[rows, :].astype(F32), h1, h2)
            first = qi * per

            def pass1(kb, carry, lo):
                run1, run2 = carry
                krows = pl.ds(pl.multiple_of(kb * BLOCK, BLOCK), BLOCK)
                k16 = k_ref[krows, :].astype(BF16)
                v16 = v_ref[krows, :].astype(BF16)
                valid = None if lo is None else kcol[:SB_ROWS - lo] < qrow[:SB_ROWS - lo]
                lo = lo or 0
                part = pl.ds(lo, SB_ROWS - lo)

                def head(h, qh, doh, run):
                    ls, l1m = _sb_scores(qh[lo:], k16, valid)
                    a = jnp.exp(ls + _split_dot(l1m, after) + run[lo:])
                    if valid is not None:
                        a = jnp.where(valid, a, 0.0)
                    da = lax.dot_general(doh[lo:], v16, NT, preferred_element_type=F32)
                    e_ref[h, kb, part, :] = a * da
                    sg_ref[h, kb, part, :] = jnp.exp(ls)
                    return a.astype(BF16), _below(run, lo, jnp.sum(l1m, axis=-1, keepdims=True))

                a1, run1 = head(0, q1, do1, run1)
                a2, run2 = head(1, q2, do2, run2)
                dva[krows, :] += (lax.dot_general(a1, do1[lo:], TN, preferred_element_type=F32)
                                  + lax.dot_general(a2, do2[lo:], TN, preferred_element_type=F32))
                return run1, run2

            zcol = jnp.zeros((SB_ROWS, 1), F32)
            carry = (zcol, zcol)
            for kl in reversed(range(per)):
                carry = pass1(first + kl, carry, kl * BLOCK)
            _pairs_loop(first, lambda i, c: pass1(first - 1 - i, c, None), carry)

            def pass2(kb, carry, lo):
                dq, pre1, pre2 = carry
                krows = pl.ds(pl.multiple_of(kb * BLOCK, BLOCK), BLOCK)
                k1, k2 = _split_heads(k_ref[krows, :], h1, h2)
                valid = None if lo is None else kcol[:SB_ROWS - lo] < qrow[:SB_ROWS - lo]
                lo = lo or 0
                part = pl.ds(lo, SB_ROWS - lo)

                def head(h, pre):
                    ev = e_ref[h, kb, part, :]
                    sg = sg_ref[h, kb, part, :]
                    dz = ev * (1.0 - sg) - (_split_dot(ev, before) + pre[lo:]) * sg
                    if valid is not None:
                        dz = jnp.where(valid, dz, 0.0)
                    return dz.astype(BF16), _below(pre, lo, jnp.sum(ev, axis=-1, keepdims=True))

                dz1, pre1 = head(0, pre1)
                dz2, pre2 = head(1, pre2)
                dka[krows, :] += (lax.dot_general(dz1, q1[lo:], TN, preferred_element_type=F32)
                                  + lax.dot_general(dz2, q2[lo:], TN, preferred_element_type=F32))
                dq = _below(dq, lo, jnp.dot(dz1, k1, preferred_element_type=F32)
                            + jnp.dot(dz2, k2, preferred_element_type=F32))
                return dq, pre1, pre2

            carry = _pairs_loop(first, lambda i, c: pass2(i, c, None), (jnp.zeros((SB_ROWS, LANES), F32), zcol, zcol))
            for kl in range(per):
                carry = pass2(first + kl, carry, kl * BLOCK)
            dq = carry[0]
            dq_ref[rows, :] = (dq * scale).astype(BF16)
            return 0

        lax.fori_loop(0, s_dim // SB_ROWS, qloop, 0)
        dk_ref[...] = dka[...].astype(BF16)
        dv_ref[...] = dva[...].astype(BF16)
        end()

    def col(c0):
        return pl.BlockSpec((None, s_dim, LANES),lambda b, h: (b, 0, c0 + h))

    shp = jax.ShapeDtypeStruct((b_dim, s_dim, ATT_WIDTH), BF16)
    acc = pltpu.VMEM((s_dim, LANES), F32)
    strip = pltpu.VMEM((2, nkb_max, SB_ROWS, BLOCK), F32)
    res = pl.pallas_call(
        body, name="sb_bwd", grid=(b_dim, PAIRS),
        in_specs=[col(COL_QB), col(COL_KB), col(COL_VB), col(0)] + [ANY] * n_w,
        out_specs=[col(0), col(0), col(0)] + [ANY] * n_w,
        out_shape=[shp, shp, shp] + extra_shapes,
        scratch_shapes=[acc, acc, strip, strip] + extra_sems,
        compiler_params=_params(("arbitrary", "arbitrary")),
    )(proj3, proj3, proj3, do3, *extra)
    return res[:3], res[3:]


def _sigmoid(x):
    return 1.0 / (1.0 + jnp.exp(-x))


def _gate_fwd(proj, ua, ub, *, tt=512):
    t_dim, d = ua.shape

    def body(ga_ref, gb_ref, ua_ref, ub_ref, o_ref):
        o_ref[...] = (_sigmoid(ga_ref[...]) * ua_ref[...] + _sigmoid(gb_ref[...]) * ub_ref[...]).astype(BF16)

    row = pl.BlockSpec((tt, d), lambda i: (i, 0))
    return pl.pallas_call(
        body, name="gate_fwd", grid=(t_dim // tt,),
        in_specs=[pl.BlockSpec((tt, d), lambda i: (i, 3)), pl.BlockSpec((tt, d), lambda i: (i, 4)), row, row],
        out_specs=row, out_shape=jax.ShapeDtypeStruct((t_dim, d), BF16),
        compiler_params=_params(("parallel",)),
    )(proj, proj, ua, ub)


def _gate_bwd(proj, ua, ub, dmix, *, tt=512):
    t_dim, d = ua.shape

    def body(ga_ref, gb_ref, ua_ref, ub_ref, dm_ref, dua_ref, dub_ref, dg_ref):
        dm = dm_ref[...]
        sa = _sigmoid(ga_ref[...])
        sb = _sigmoid(gb_ref[...])
        dua_ref[...] = (dm * sa).astype(BF16)
        dub_ref[...] = (dm * sb).astype(BF16)
        dg_ref[:, :d] = (dm * ua_ref[...] * (sa * (1.0 - sa))).astype(BF16)
        dg_ref[:, d:] = (dm * ub_ref[...] * (sb * (1.0 - sb))).astype(BF16)

    row = pl.BlockSpec((tt, d), lambda i: (i, 0))
    wide = pl.BlockSpec((tt, 2 * d), lambda i: (i, 0))
    return pl.pallas_call(
        body, name="gate_bwd", grid=(t_dim // tt,),
        in_specs=[pl.BlockSpec((tt, d), lambda i: (i, 3)), pl.BlockSpec((tt, d), lambda i: (i, 4)), row, row, row],
        out_specs=[row, row, wide],
        out_shape=[jax.ShapeDtypeStruct((t_dim, d), BF16), jax.ShapeDtypeStruct((t_dim, d), BF16),
                   jax.ShapeDtypeStruct((t_dim, 2 * d), BF16)],
        compiler_params=_params(("parallel",)),
    )(proj, proj, ua, ub, dmix)


def _ffn_up_swiglu(n, wg3, wu3, *, tt=1024):
    t_dim, d = n.shape
    n_s, _, f4 = wg3.shape

    def body(n_ref, wg_ref, wu_ref, g_ref, u_ref, a_ref):
        nv = n_ref[...]
        gv = jnp.dot(nv, wg_ref[...], preferred_element_type=F32)
        uv = jnp.dot(nv, wu_ref[...], preferred_element_type=F32)
        g_ref[...] = gv
        u_ref[...] = uv
        a_ref[...] = (gv * _sigmoid(gv) * uv).astype(BF16)

    wspec = pl.BlockSpec((None, d, f4), lambda i, s: (s, 0, 0))
    ospec = pl.BlockSpec((None, tt, f4), lambda i, s: (s, i, 0))
    shp = (n_s, t_dim, f4)
    return pl.pallas_call(
        body, name="ffn_up_swiglu", grid=(t_dim // tt, n_s),
        in_specs=[pl.BlockSpec((tt, d), lambda i, s: (i, 0)), wspec, wspec], out_specs=[ospec, ospec, ospec],
        out_shape=[jax.ShapeDtypeStruct(shp, F32), jax.ShapeDtypeStruct(shp, F32), jax.ShapeDtypeStruct(shp, BF16)],
        compiler_params=_params(("parallel", "parallel")),
    )(n, wg3, wu3)


def _ffn_down_dx_swiglu(dh, wd3, g3, u3, *, tt=1024):
    t_dim, d = dh.shape
    n_s, f4, _ = wd3.shape

    def body(dh_ref, w_ref, g_ref, u_ref, dg_ref, du_ref):
        da = lax.dot_general(dh_ref[...].astype(BF16), w_ref[...], NT, preferred_element_type=F32)
        gv = g_ref[...]
        sg = _sigmoid(gv)
        dg_ref[...] = (da * u_ref[...] * (sg + gv * sg * (1.0 - sg))).astype(BF16)
        du_ref[...] = (da * (gv * sg)).astype(BF16)

    spec = pl.BlockSpec((None, tt, f4), lambda i, s: (s, i, 0))
    shp = jax.ShapeDtypeStruct((n_s, t_dim, f4), BF16)
    return pl.pallas_call(
        body, name="ffn_down_dx_swiglu", grid=(t_dim // tt, n_s),
        in_specs=[pl.BlockSpec((tt, d), lambda i, s: (i, 0)), pl.BlockSpec((None, f4, d), lambda i, s: (s, 0, 0)),
                  spec, spec],
        out_specs=[spec, spec], out_shape=[shp, shp],
        compiler_params=_params(("parallel", "parallel")),
    )(dh, wd3, g3, u3)


def _mem_fwd(qm, kvm, *, tt=512):
    b_dim, s_dim, _ = qm.shape
    n_mem = kvm.shape[1]
    scale = MEM_HEAD_DIM ** -0.5

    def body(q_ref, k_ref, v_ref, o_ref):
        sc = lax.dot_general(q_ref[0], k_ref[0], NT, preferred_element_type=F32) * scale
        p = jnp.exp(sc - jnp.max(sc, axis=-1, keepdims=True))
        p = p / jnp.sum(p, axis=-1, keepdims=True)
        o_ref[0] = jnp.dot(p.astype(BF16), v_ref[0], preferred_element_type=F32).astype(BF16)

    qs = pl.BlockSpec((1, tt, MEM_HEAD_DIM), lambda b, h, i: (b, i, h))
    return pl.pallas_call(
        body, name="mem_fwd", grid=(b_dim, N_HEADS_MEM, s_dim // tt),
        in_specs=[qs, pl.BlockSpec((1, n_mem, MEM_HEAD_DIM), lambda b, h, i: (b, 0, h)),
                  pl.BlockSpec((1, n_mem, MEM_HEAD_DIM), lambda b, h, i: (b, 0, N_HEADS_MEM + h))],
        out_specs=qs, out_shape=jax.ShapeDtypeStruct(qm.shape, BF16),
        compiler_params=_params(("parallel", "parallel", "parallel")),
    )(qm, kvm, kvm)


def _mem_bwd(qm, kvm, dom, *, tt=512):
    b_dim, s_dim, _ = qm.shape
    n_mem = kvm.shape[1]
    scale = MEM_HEAD_DIM ** -0.5

    def body(q_ref, k_ref, v_ref, do_ref, dq_ref, dk_ref, dv_ref):
        qv, kv, vv, dov = q_ref[0], k_ref[0], v_ref[0], do_ref[0]
        sc = lax.dot_general(qv, kv, NT, preferred_element_type=F32) * scale
        p = jnp.exp(sc - jnp.max(sc, axis=-1, keepdims=True))
        p = p / jnp.sum(p, axis=-1, keepdims=True)
        dp = lax.dot_general(dov, vv, NT, preferred_element_type=F32)
        ds = (p * (dp - jnp.sum(p * dp, axis=-1, keepdims=True)) * scale).astype(BF16)
        dq_ref[0] = jnp.dot(ds, kv, preferred_element_type=F32).astype(BF16)

        @pl.when(pl.program_id(2) == 0)
        def _():
            dk_ref[...] = jnp.zeros_like(dk_ref)
            dv_ref[...] = jnp.zeros_like(dv_ref)

        dk_ref[0] += lax.dot_general(ds, qv, TN, preferred_element_type=F32)
        dv_ref[0] += lax.dot_general(p.astype(BF16), dov, TN, preferred_element_type=F32)

    qs = pl.BlockSpec((1, tt, MEM_HEAD_DIM), lambda b, h, i: (b, i, h))
    ks = pl.BlockSpec((1, n_mem, MEM_HEAD_DIM), lambda b, h, i: (b, 0, h))
    vs = pl.BlockSpec((1, n_mem, MEM_HEAD_DIM), lambda b, h, i: (b, 0, N_HEADS_MEM + h))
    return pl.pallas_call(
        body, name="mem_bwd", grid=(b_dim, N_HEADS_MEM, s_dim // tt),
        in_specs=[qs, ks, vs, qs], out_specs=[qs, ks, ks],
        out_shape=[jax.ShapeDtypeStruct(qm.shape, BF16), jax.ShapeDtypeStruct((b_dim, n_mem, MEM_WIDTH), F32),
                   jax.ShapeDtypeStruct((b_dim, n_mem, MEM_WIDTH), F32)],
        compiler_params=_params(("parallel", "parallel", "arbitrary")),
    )(qm, kvm, kvm, dom)


def _adamw_math(wv, gv, mv, vv):
    nm = ADAM_B1 * mv + (1.0 - ADAM_B1) * gv
    nv = ADAM_B2 * vv + (1.0 - ADAM_B2) * (gv * gv)
    m_hat = nm / (1.0 - ADAM_B1 ** ADAM_STEP)
    v_hat = nv / (1.0 - ADAM_B2 ** ADAM_STEP)
    return -ADAM_LR * (m_hat / (jnp.sqrt(v_hat) + ADAM_EPS) + ADAM_WD * wv), nm, nv


def _adamw(w, g, m, v, *, name):
    rows, cols = w.shape
    tr = _tile(rows, 256, 8)

    def body(w_ref, g_ref, m_ref, v_ref, d_ref, nm_ref, nv_ref):
        d_ref[...], nm_ref[...], nv_ref[...] = _adamw_math(w_ref[...], g_ref[...], m_ref[...], v_ref[...])

    spec = pl.BlockSpec((tr, cols), lambda i: (i, 0))
    shp = jax.ShapeDtypeStruct((rows, cols), F32)
    return pl.pallas_call(
        body, name=name, grid=(rows // tr,),
        in_specs=[spec] * 4, out_specs=[spec] * 3, out_shape=[shp] * 3,
        compiler_params=_params(("parallel",)),
    )(w, g, m, v)


def _prefetch_spec(grid, in_specs, out_specs):
    return pltpu.PrefetchScalarGridSpec(num_scalar_prefetch=1, grid=grid, in_specs=in_specs, out_specs=out_specs)


def _adamw_halves(w, mine, theirs, m, v, c_idx, *, name):
    rows, cols = w.shape
    half = rows // 2
    tr = _tile(half, 256, 8)
    nh = half // tr

    def body(c_ref, w_ref, mine_ref, theirs_ref, m_ref, v_ref, g_ref, d_ref, nm_ref, nv_ref):
        gv = jnp.where(pl.program_id(0) == c_ref[0], mine_ref[...], theirs_ref[...])
        g_ref[...] = gv
        d_ref[...], nm_ref[...], nv_ref[...] = _adamw_math(w_ref[...], gv, m_ref[...], v_ref[...])

    full = pl.BlockSpec((tr, cols), lambda h, i, c_ref: (h * nh + i, 0))
    part = pl.BlockSpec((tr, cols), lambda h, i, c_ref: (i, 0))
    shp = jax.ShapeDtypeStruct((rows, cols), F32)
    return pl.pallas_call(
        body, name=name, grid_spec=_prefetch_spec((2, nh), [full, part, part, full, full], [full] * 4),
        out_shape=[shp] * 4,
        compiler_params=_params(("parallel", "parallel")),
    )(c_idx, w, mine, theirs, m, v)


def _pair_sum(g3, theirs, c_idx, *, name):
    n, rows, cols = g3.shape
    half = rows // 2
    tr = _tile(half, 256, 16)

    def body(c_ref, g_ref, t_ref, o_ref):
        o_ref[...] = (g_ref[...] + t_ref[...]).astype(BF16)

    part = pl.BlockSpec((None, tr, cols), lambda s, i, c_ref: (s, i, 0))
    return pl.pallas_call(
        body, name=name,
        grid_spec=_prefetch_spec((n, half // tr),
                                 [pl.BlockSpec((None, None, tr, cols), lambda s, i, c_ref: (s, c_ref[0], i, 0)), part],
                                 part),
        out_shape=jax.ShapeDtypeStruct((n, half, cols), BF16),
        compiler_params=_params(("parallel", "parallel")),
    )(c_idx, g3.reshape(n, 2, half, cols), theirs)


def _chip_sum(pair, recv, s_idx, *, name):
    _, half, cols = pair.shape
    tr = _tile(half, 256, 16)

    def body(s_ref, p_ref, r_ref, o_ref):
        o_ref[...] = ((p_ref[...].astype(F32) + r_ref[0].astype(F32)) + r_ref[1].astype(F32)) + r_ref[2].astype(F32)

    return pl.pallas_call(
        body, name=name,
        grid_spec=_prefetch_spec((half // tr,),
                                 [pl.BlockSpec((None, tr, cols), lambda i, s_ref: (s_ref[0], i, 0)),
                                  pl.BlockSpec((N_CHIPS - 1, tr, cols), lambda i, s_ref: (0, i, 0))],
                                 pl.BlockSpec((tr, cols), lambda i, s_ref: (i, 0))),
        out_shape=jax.ShapeDtypeStruct((half, cols), F32),
        compiler_params=_params(("parallel",)),
    )(s_idx, pair, recv)


def _sum8(parts):
    n, rows, cols = parts.shape

    def body(p_ref, o_ref):
        acc = p_ref[0]
        for i in range(1, n):
            acc = acc + p_ref[i]
        o_ref[...] = acc

    return pl.pallas_call(
        body, name="small_sum", grid=(1,),
        in_specs=[pl.BlockSpec((n, rows, cols), lambda i: (0, 0, 0))],
        out_specs=pl.BlockSpec((rows, cols), lambda i: (0, 0)),
        out_shape=jax.ShapeDtypeStruct((rows, cols), parts.dtype),
        compiler_params=_params(("arbitrary",)),
    )(parts)


def _place():
    return lax.axis_index("x"), lax.axis_index("y"), lax.axis_index("c")


ANY = pl.BlockSpec(memory_space=pl.ANY)


def _rider_parts(rider):
    if rider is None:
        return (), [], []
    kind, arrays = rider
    n = len(arrays)
    shapes = {"gather": _gathered_shapes, "pair": _pair_shapes, "chip": _chip_shapes}[kind](arrays)
    sems = _gather_sems(n) if kind == "gather" else _exchange_sems(n if kind == "pair" else 3 * n)
    return tuple(arrays), shapes, sems


def _rider_hooks(rider, ins, outs, sems, step, n_steps):
    if rider is None:
        return (lambda: None), (lambda: None)
    if rider[0] == "gather":
        start, forward, finish = _gather_steps(ins, outs, *sems)
    else:
        start, finish = {"pair": _pair_steps, "chip": _chip_steps}[rider[0]](ins, outs, *sems)
        forward = None

    def begin():
        pl.when(step == 0)(start)

    def end():
        if forward is not None:
            pl.when(step == n_steps - 2)(forward)
        pl.when(step == n_steps - 1)(finish)

    return begin, end


def _gather_weights(shards):
    n = len(shards)

    def body(*refs):
        start, forward, finish = _gather_steps(refs[:n], refs[n:2 * n], refs[2 * n], refs[2 * n + 1])
        start()
        forward()
        finish()

    return pl.pallas_call(
        body, name="gather_weights", out_shape=_gathered_shapes(shards),
        in_specs=[ANY] * n, out_specs=[ANY] * n, scratch_shapes=_gather_sems(n),
    )(*shards)


def _gathered_shapes(shards):
    return [jax.ShapeDtypeStruct((N_CHIPS,) + s.shape, s.dtype) for s in shards]


def _gather_sems(n):
    return [pltpu.SemaphoreType.DMA((7 * n,)), pltpu.SemaphoreType.DMA((7 * n,))]


def _gather_steps(ins, outs, send_sems, recv_sems):
    n = len(ins)
    halves = [r.shape[0] // 2 for r in ins]
    x, y, c = _place()
    my_chip = 2 * x + y
    me, sibling = (x, y, c), (x, y, 1 - c)
    chips = [(1 - x, y), (x, 1 - y), (1 - x, 1 - y)]

    def half_of(w, chip, pc):
        return outs[w].at[chip, pl.ds(pc * halves[w], halves[w]), :]

    def copy(w, k, src, dst, to):
        return pltpu.make_async_remote_copy(
            src_ref=src, dst_ref=dst, send_sem=send_sems.at[7 * w + k], recv_sem=recv_sems.at[7 * w + k],
            device_id=to, device_id_type=MESH)

    def firsts():
        cps = []
        for w in range(n):
            cps.append(copy(w, 0, ins[w], outs[w].at[my_chip], sibling))
            mine = ins[w].at[pl.ds(c * halves[w], halves[w]), :]
            for j, (px, py) in enumerate(chips):
                cps.append(copy(w, 1 + j, mine, half_of(w, my_chip, c), (px, py, c)))
        return cps

    def passes():
        return [copy(w, 4 + j, half_of(w, 2 * px + py, c), half_of(w, 2 * px + py, c), sibling)
                for w in range(n) for j, (px, py) in enumerate(chips)]

    def start():
        for cp in firsts():
            cp.start()

    def forward():
        fws = passes()
        for w in range(n):
            for j, (px, py) in enumerate(chips):
                landed = half_of(w, 2 * px + py, c)
                copy(w, 1 + j, landed, landed, me).wait_recv()
                fws[3 * w + j].start()

    def finish():
        for w in range(n):
            copy(w, 0, ins[w], outs[w].at[my_chip], me).wait_recv()
            for j, (px, py) in enumerate(chips):
                landed = half_of(w, 2 * px + py, 1 - c)
                copy(w, 4 + j, landed, landed, me).wait_recv()
        for cp in firsts() + passes():
            cp.wait_send()

    return start, forward, finish


def _pair_exchange(grads):
    n = len(grads)

    def body(*refs):
        start, finish = _pair_steps(refs[:n], refs[n:2 * n], refs[2 * n], refs[2 * n + 1])
        start()
        finish()

    return pl.pallas_call(
        body, name="grad_pair_exchange", out_shape=_pair_shapes(grads),
        in_specs=[ANY] * n, out_specs=[ANY] * n, scratch_shapes=_exchange_sems(n),
    )(*grads)


def _pair_shapes(grads):
    return [jax.ShapeDtypeStruct((g.shape[0], g.shape[1] // 2, g.shape[2]), g.dtype) for g in grads]


def _exchange_sems(n):
    return [pltpu.SemaphoreType.DMA((n,)), pltpu.SemaphoreType.DMA((n,))]


def _exchange_steps(copies):
    def start():
        for cp in copies():
            cp.start()

    def finish():
        for cp in copies():
            cp.wait()

    return start, finish


def _pair_steps(ins, outs, send_sems, recv_sems):
    x, y, c = _place()

    def copies():
        return [pltpu.make_async_remote_copy(
            src_ref=ins[w].at[:, pl.ds((1 - c) * (ins[w].shape[1] // 2), ins[w].shape[1] // 2), :], dst_ref=outs[w],
            send_sem=send_sems.at[w], recv_sem=recv_sems.at[w], device_id=(x, y, 1 - c), device_id_type=MESH)
            for w in range(len(ins))]

    return _exchange_steps(copies)


def _chip_exchange(pairs):
    n = len(pairs)

    def body(*refs):
        start, finish = _chip_steps(refs[:n], refs[n:2 * n], refs[2 * n], refs[2 * n + 1])
        start()
        finish()

    return pl.pallas_call(
        body, name="grad_chip_exchange", out_shape=_chip_shapes(pairs),
        in_specs=[ANY] * n, out_specs=[ANY] * n, scratch_shapes=_exchange_sems(3 * n),
    )(*pairs)


def _chip_shapes(pairs):
    return [jax.ShapeDtypeStruct((N_CHIPS - 1,) + p.shape[1:], p.dtype) for p in pairs]


def _chip_steps(ins, outs, send_sems, recv_sems):
    x, y, c = _place()
    others = [(1 - x, y), (x, 1 - y), (1 - x, 1 - y)]

    def copies():
        return [pltpu.make_async_remote_copy(
            src_ref=ins[w].at[2 * px + py], dst_ref=outs[w].at[j],
            send_sem=send_sems.at[3 * w + j], recv_sem=recv_sems.at[3 * w + j],
            device_id=(px, py, c), device_id_type=MESH)
            for w in range(len(ins)) for j, (px, py) in enumerate(others)]

    return _exchange_steps(copies)


def _swap_halves(mine):
    n = len(mine)

    def body(*refs):
        ins, outs, send_sems, recv_sems = refs[:n], refs[n:2 * n], refs[2 * n], refs[2 * n + 1]
        x, y, c = _place()
        copies = [pltpu.make_async_remote_copy(
            src_ref=ins[w], dst_ref=outs[w], send_sem=send_sems.at[w], recv_sem=recv_sems.at[w],
            device_id=(x, y, 1 - c), device_id_type=MESH) for w in range(n)]
        for cp in copies:
            cp.start()
        for cp in copies:
            cp.wait()

    return pl.pallas_call(
        body, name="grad_swap_halves",
        out_shape=[jax.ShapeDtypeStruct(h.shape, h.dtype) for h in mine],
        in_specs=[ANY] * n, out_specs=[ANY] * n,
        scratch_shapes=[pltpu.SemaphoreType.DMA((n,)), pltpu.SemaphoreType.DMA((n,))],
    )(*mine)


def _gather_small(small):
    srows, cols = small.shape

    def body(s_ref, all_ref, send_sems, recv_sems, local_sem):
        x, y, c = _place()
        me = 4 * x + 2 * y + c
        keep_small = pltpu.make_async_copy(s_ref, all_ref.at[me], local_sem)
        keep_small.start()
        sends = []
        for kk in range(1, 8):
            peer = (x ^ (kk >> 2), y ^ ((kk >> 1) & 1), c ^ (kk & 1))
            sends.append(pltpu.make_async_remote_copy(
                src_ref=s_ref, dst_ref=all_ref.at[me],
                send_sem=send_sems.at[kk], recv_sem=recv_sems.at[kk], device_id=peer, device_id_type=MESH))
        for cp in sends:
            cp.start()
        for kk in range(1, 8):
            px, py, pc = x ^ (kk >> 2), y ^ ((kk >> 1) & 1), c ^ (kk & 1)
            pltpu.make_async_remote_copy(
                src_ref=s_ref, dst_ref=all_ref.at[4 * px + 2 * py + pc],
                send_sem=send_sems.at[kk], recv_sem=recv_sems.at[kk], device_id=(px, py, pc),
                device_id_type=MESH).wait_recv()
        for cp in sends:
            cp.wait_send()
        keep_small.wait()

    return pl.pallas_call(
        body, name="gather_small",
        out_shape=jax.ShapeDtypeStruct((8, srows, cols), small.dtype),
        in_specs=[ANY], out_specs=ANY,
        scratch_shapes=[pltpu.SemaphoreType.DMA((8,)), pltpu.SemaphoreType.DMA((8,)), pltpu.SemaphoreType.DMA],
    )(small)


SHARDED = (("w_in", D_MODEL, IN_COLS, 1), ("w_up_a", ATT_WIDTH, D_MODEL, 1), ("w_up_b", ATT_WIDTH, D_MODEL, 1),
           ("w_out", D_MODEL, D_MODEL, 0), ("w_q_mem", D_MODEL, MEM_WIDTH, 0), ("w_kv_mem", D_MODEL, 2 * MEM_WIDTH, 0),
           ("w_o_mem", MEM_WIDTH, D_MODEL, 1), ("w_ffn_gate", D_MODEL, D_FF, 1), ("w_ffn_up", D_MODEL, D_FF, 1),
           ("w_ffn_down", D_FF, D_MODEL, 0))
NAMES = tuple(n for n, _, _, _ in SHARDED)
EARLY, LATE = NAMES[:1], NAMES[1:]
GAINS = ("g_mix", "g_mem_q", "g_mem_kv", "g_ffn", "g_final")


def _natural(w3):
    n, r, c = w3.shape
    return w3.reshape(n * r, c)


def _shard_major(g, axis):
    if axis == 1:
        return g
    r, c = g.shape
    return g.reshape(N_CHIPS, r // N_CHIPS, c)


def kernel(x, mem, positions, g_mix, w_in, w_up_a, w_up_b, w_out, g_mem_q, g_mem_kv, w_q_mem, w_kv_mem, w_o_mem, g_ffn, w_ffn_gate, w_ffn_up, w_ffn_down, g_final, loss_target, m_g_mix, m_w_in, m_w_up_a, m_w_up_b, m_w_out, m_g_mem_q, m_g_mem_kv, m_w_q_mem, m_w_kv_mem, m_w_o_mem, m_g_ffn, m_w_ffn_gate, m_w_ffn_up, m_w_ffn_down, m_g_final, v_g_mix, v_w_in, v_w_up_a, v_w_up_b, v_w_out, v_g_mem_q, v_g_mem_kv, v_w_q_mem, v_w_kv_mem, v_w_o_mem, v_g_ffn, v_w_ffn_gate, v_w_ffn_up, v_w_ffn_down, v_g_final):
    given = dict(locals())
    shards = {n: given[n][0] for n, _, _, _ in SHARDED}

    wf = dict(zip(EARLY, _gather_weights([shards[n].astype(BF16) for n in EARLY])))
    late_shards = [shards[n].astype(BF16) for n in LATE]
    c_idx = lax.axis_index("c").astype(jnp.int32).reshape(1)
    s_idx = (2 * lax.axis_index("x") + lax.axis_index("y")).astype(jnp.int32).reshape(1)

    loss_row, grad_x, mine, gain_grads = _local_step(x, mem, positions, loss_target, g_mix, g_mem_q, g_mem_kv,
                                                     g_ffn, g_final, wf, late_shards, (c_idx, s_idx))
    return _reduce_and_update(given, shards, loss_row, grad_x, mine, gain_grads, c_idx)


def _reduce_halves(glist, names, c_idx, s_idx, pair_exchange, chip_exchange):
    theirs = pair_exchange(glist)
    pairs = [_pair_sum(g, t, c_idx, name="pair_sum_" + n) for n, g, t in zip(names, glist, theirs)]
    recv = chip_exchange(pairs)
    return [_chip_sum(p, r, s_idx, name="chip_sum_" + n) for n, p, r in zip(names, pairs, recv)]


def _local_step(x, mem, positions, loss_target, g_mix, g_mem_q, g_mem_kv, g_ffn, g_final, wf,
                late_shards=None, place=None):
    b_dim, s_dim, d = x.shape
    t_dim = b_dim * s_dim
    n_mem = mem.shape[1]
    wf = dict(wf)

    xb = x.reshape(t_dim, d)
    tgt = loss_target.reshape(t_dim, d)
    memf = mem.reshape(b_dim * n_mem, d)
    gfin = g_final.reshape(1, d)
    pos = positions.reshape(t_dim, 1).astype(F32)

    lane = jnp.arange(LANES) % HEAD_DIM
    half = ROPE_DIM // 2
    inv_freq = ROPE_THETA ** (-jnp.arange(half, dtype=F32) / half)
    inv_lane = jnp.where(lane < ROPE_DIM, inv_freq[lane % half], 0.0).reshape(1, -1).astype(F32)
    sel_a = (lane < half).astype(F32).reshape(1, -1)
    sel_b = ((lane >= half) & (lane < ROPE_DIM)).astype(F32).reshape(1, -1)

    def rows3(t):
        return t.reshape(b_dim, s_dim, t.shape[-1])

    def rows2(t):
        return t.reshape(t_dim, t.shape[-1])

    n1 = _rms_fwd(xb, g_mix, name="rms_mix")
    proj = _mm_cs(n1, wf["w_in"], name="mm_in")
    proj3 = rows3(proj)
    cs, sn = _rope_table(pos, inv_lane, sel_a, sel_b)
    cs3, sn3 = rows3(cs), rows3(sn)
    (oa16, oa32, lse_a), _ = _dil_fwd(proj3, cs3, sn3, sel_a, sel_b)
    ob16, gathered = _sb_fwd(proj3, ("gather", late_shards) if late_shards else None)
    wf.update(zip(LATE, gathered))
    w_out, w_q, w_kv = _natural(wf["w_out"]), _natural(wf["w_q_mem"]), _natural(wf["w_kv_mem"])
    oa, ob = rows2(oa16), rows2(ob16)
    ua = _mm_cs(oa, wf["w_up_a"], name="mm_up_a")
    ub = _mm_cs(ob, wf["w_up_b"], name="mm_up_b")
    mixed = _gate_fwd(proj, ua, ub)
    h1 = _mm(mixed, w_out, name="mm_out", add=xb)

    hn = _rms_fwd(h1, g_mem_q, name="rms_mem_q")
    memn = _rms_fwd(memf, g_mem_kv, name="rms_mem_kv")
    qm = _mm(hn, w_q, name="mm_q_mem", out_dtype=BF16)
    kvm = _mm(memn, w_kv, name="mm_kv_mem", out_dtype=BF16)
    qm3, kvm3 = rows3(qm), kvm.reshape(b_dim, n_mem, 2 * MEM_WIDTH)
    om = rows2(_mem_fwd(qm3, kvm3))
    h2 = _mm_cs(om, wf["w_o_mem"], name="mm_o_mem", add=h1)

    n3 = _rms_fwd(h2, g_ffn, name="rms_ffn")
    gate3, up3, act3 = _ffn_up_swiglu(n3, wf["w_ffn_gate"], wf["w_ffn_up"])
    h3 = _mm_ffn_down(act3, wf["w_ffn_down"], name="mm_down", add=h2)
    loss_row, dh3, dg_final = _final(h3, gfin, tgt)

    grads = {}
    grads["w_ffn_down"] = _mm_ffn_down_dw(act3, dh3, name="mm_down_dw")
    dgate3, dup3 = _ffn_down_dx_swiglu(dh3, wf["w_ffn_down"], gate3, up3)
    grads["w_ffn_gate"] = _mm_ffn_up_dw(n3, dgate3, name="mm_gate_dw")
    grads["w_ffn_up"] = _mm_ffn_up_dw(n3, dup3, name="mm_up_dw")
    dn3 = _mm_ffn_up_dx(dgate3, wf["w_ffn_gate"], name="mm_gate_dx")
    dn3 = _mm_ffn_up_dx(dup3, wf["w_ffn_up"], name="mm_up_dx", add=dn3)
    dh2, dg_ffn = _rms_bwd(h2, g_ffn, dn3, dh3, name="rms_ffn_bwd")

    dom = _mm_cs_dx(dh2, wf["w_o_mem"], name="mm_o_mem_dx", out_dtype=BF16)
    grads["w_o_mem"] = _mm_cs_dw(om, dh2, name="mm_o_mem_dw")
    dqm, dkm, dvm = _mem_bwd(qm3, kvm3, rows3(dom))
    dqm = rows2(dqm)
    dkvm = jnp.concatenate([dkm, dvm], axis=-1).reshape(b_dim * n_mem, 2 * MEM_WIDTH).astype(BF16)
    grads["w_q_mem"] = _shard_major(_mm(hn, dqm, name="mm_q_mem_dw", ta=True), 0)
    dhn = _mm(dqm, w_q, name="mm_q_mem_dx", tb=True)
    grads["w_kv_mem"] = _shard_major(_mm(memn, dkvm, name="mm_kv_mem_dw", ta=True), 0)
    dmemn = _mm(dkvm, w_kv, name="mm_kv_mem_dx", tb=True)
    _, dg_mem_kv = _rms_bwd(memf, g_mem_kv, dmemn, None, name="rms_mem_kv_bwd")
    dh1, dg_mem_q = _rms_bwd(h1, g_mem_q, dhn, dh2, name="rms_mem_q_bwd")

    dmix = _mm(dh1, w_out, name="mm_out_dx", tb=True)
    grads["w_out"] = _shard_major(_mm(mixed, dh1, name="mm_out_dw", ta=True), 0)
    dua, dub, dgates = _gate_bwd(proj, ua, ub, dmix)
    doa = _mm_cs_dx(dua, wf["w_up_a"], name="mm_up_a_dx")
    grads["w_up_a"] = _mm_cs_dw(oa, dua, name="mm_up_a_dw")
    dob = _mm_cs_dx(dub, wf["w_up_b"], name="mm_up_b_dx", out_dtype=BF16)
    grads["w_up_b"] = _mm_cs_dw(ob, dub, name="mm_up_b_dw")

    att = {}

    def dil_with_pairs(glist):
        att["a"], theirs = _dil_bwd(proj3, cs3, sn3, sel_a, sel_b, rows3(doa), oa32, lse_a,
                                    ("pair", glist) if glist else None)
        return theirs

    def sb_with_chips(pairs):
        att["b"], recv = _sb_bwd(proj3, rows3(dob), ("chip", pairs) if pairs else None)
        return recv

    if place is None:
        dil_with_pairs(())
        sb_with_chips(())
    else:
        mine_late = _reduce_halves([grads[n] for n in LATE], LATE, *place, dil_with_pairs, sb_with_chips)
    dproj = jnp.concatenate([rows2(t) for t in att["a"] + att["b"]] + [dgates], axis=1)
    grads["w_in"] = _mm_cs_dw(n1, dproj, name="mm_in_dw")
    dn1 = _mm_cs_dx(dproj, wf["w_in"], name="mm_in_dx")
    dx, dg_mix = _rms_bwd(xb, g_mix, dn1, dh1, name="rms_mix_bwd")
    grad_x = dx.reshape(b_dim, s_dim, d)
    gains = (dg_mix, dg_mem_q, dg_mem_kv, dg_ffn, dg_final)
    if place is None:
        return loss_row, grad_x, grads, gains
    mine_early = _reduce_halves([grads[n] for n in EARLY], EARLY, *place, _pair_exchange, _chip_exchange)
    return loss_row, grad_x, mine_early + mine_late, gains


def _reduce_and_update(given, shards, loss_row, grad_x, mine, gain_grads, c_idx):
    d = D_MODEL
    dg_mix, dg_mem_q, dg_mem_kv, dg_ffn, dg_final = gain_grads
    small = jnp.concatenate([dg_mix, dg_mem_q, dg_mem_kv, dg_ffn, dg_final,
                             jnp.pad(loss_row, ((0, 0), (0, FLAT_COLS - LANES))), jnp.zeros((2, FLAT_COLS), F32)], axis=0)
    small_all = _gather_small(small)
    others = _swap_halves(mine)
    small_sum = _sum8(small_all)
    loss = small_sum[5, 0]

    out_g, out_d, out_m, out_v = {}, {}, {}, {}
    for n, mine_n, other_n in zip(NAMES, mine, others):
        g2, dl, nm, nv = _adamw_halves(shards[n], mine_n, other_n, given["m_" + n][0], given["v_" + n][0], c_idx,
                                       name="adamw_" + n)
        out_g[n], out_d[n], out_m[n], out_v[n] = g2[None], dl[None], nm[None], nv[None]
    gain_w = jnp.concatenate([given[n].reshape(1, d) for n in GAINS], axis=0)
    gain_m = jnp.concatenate([given["m_" + n].reshape(1, d) for n in GAINS], axis=0)
    gain_v = jnp.concatenate([given["v_" + n].reshape(1, d) for n in GAINS], axis=0)
    gain_g = small_sum[:len(GAINS)]
    gd, gm, gv = _adamw(gain_w, gain_g, gain_m, gain_v, name="adamw_gains")
    for i, n in enumerate(GAINS):
        shape = given[n].shape
        out_g[n], out_d[n] = gain_g[i].reshape(shape), gd[i].reshape(shape)
        out_m[n], out_v[n] = gm[i].reshape(shape), gv[i].reshape(shape)

    order = ["g_mix", "w_in", "w_up_a", "w_up_b", "w_out", "g_mem_q", "g_mem_kv", "w_q_mem", "w_kv_mem", "w_o_mem",
             "g_ffn", "w_ffn_gate", "w_ffn_up", "w_ffn_down", "g_final"]
    return (loss, grad_x, *[out_g[n] for n in order], *[out_d[n] for n in order],
            *[out_m[n] for n in order], *[out_v[n] for n in order])
```

```python
import jax
import jax.numpy as jnp
from jax import lax
from jax.experimental import pallas as pl
from jax.experimental.pallas import tpu as pltpu

F32 = jnp.float32
BF16 = jnp.bfloat16
MESH = pl.DeviceIdType.MESH

D_MODEL = 1024
HEAD_DIM = 64
N_HEADS = 8
ATT_WIDTH = N_HEADS * HEAD_DIM
DIL_PATTERNS = ((128, 1), (512, 4), (2048, 16))
BLOCK = 128
SB_ROWS = 512
SB_STEP = 4
ROPE_THETA = 500000.0
ROPE_DIM = HEAD_DIM // 4
N_HEADS_MEM = 4
MEM_HEAD_DIM = 128
MEM_WIDTH = N_HEADS_MEM * MEM_HEAD_DIM
D_FF = 2816
IN_COLS = 6 * ATT_WIDTH + 2 * D_MODEL
RMS_EPS = 1e-6
ADAM_LR = 0.001
ADAM_B1 = 0.9
ADAM_B2 = 0.999
ADAM_EPS = 1e-08
ADAM_WD = 0.01
ADAM_STEP = 10

N_CHIPS = 4
LANES = 128
FLAT_COLS = 1024
VMEM_LIMIT = 56 * 1024 * 1024

PAIRS = ATT_WIDTH // LANES
COL_QA, COL_KA, COL_VA, COL_QB, COL_KB, COL_VB = (i * PAIRS for i in range(6))

MM_CAP = 1408
TOK_CAP = 2048
NN = (((1,), (0,)), ((), ()))
NT = (((1,), (1,)), ((), ()))
TN = (((0,), (0,)), ((), ()))
BNN = (((2,), (1,)), ((0,), (0,)))
BNT = (((2,), (2,)), ((0,), (0,)))
BTN = (((1,), (1,)), ((0,), (0,)))
DIL_BATCH = 8


def _tile(dim, cap, unit=LANES):
    if dim <= cap:
        return dim
    best = None
    for t in range(unit, cap + 1, unit):
        if dim % t == 0:
            best = t
    assert best is not None, (dim, cap)
    return best


def _row_cap(cols):
    return max(256, (1 << 18) // cols)


def _params(sem):
    return pltpu.CompilerParams(dimension_semantics=sem, vmem_limit_bytes=VMEM_LIMIT)


def _mm(a, b, *, name, ta=False, tb=False, add=None, out_dtype=F32,
        tm_cap=MM_CAP, tn_cap=MM_CAP, tk_cap=MM_CAP):
    if ta:
        k_dim, m_dim = a.shape
    else:
        m_dim, k_dim = a.shape
    if tb:
        n_dim, kb = b.shape
    else:
        kb, n_dim = b.shape
    assert kb == k_dim, (a.shape, b.shape, ta, tb)
    tm, tn, tk = _tile(m_dim, tm_cap), _tile(n_dim, tn_cap), _tile(k_dim, tk_cap)
    nk = k_dim // tk
    dims = (((0 if ta else 1,), (1 if tb else 0,)), ((), ()))
    has_add = add is not None

    def body(*refs):
        if has_add:
            a_ref, b_ref, add_ref, o_ref = refs[:4]
        else:
            a_ref, b_ref, o_ref = refs[:3]
        part = lax.dot_general(a_ref[...].astype(BF16), b_ref[...].astype(BF16), dims, preferred_element_type=F32)

        def finish(r):
            if has_add:
                r = add_ref[...] + r
            o_ref[...] = r.astype(out_dtype)

        if nk == 1:
            finish(part)
            return
        acc_ref = refs[-1]
        k = pl.program_id(2)

        @pl.when(k == 0)
        def _():
            acc_ref[...] = part

        @pl.when(k > 0)
        def _():
            acc_ref[...] += part

        @pl.when(k == nk - 1)
        def _():
            finish(acc_ref[...])

    a_spec = pl.BlockSpec((tk, tm), lambda i, j, k: (k, i)) if ta else pl.BlockSpec((tm, tk), lambda i, j, k: (i, k))
    b_spec = pl.BlockSpec((tn, tk), lambda i, j, k: (j, k)) if tb else pl.BlockSpec((tk, tn), lambda i, j, k: (k, j))
    o_spec = pl.BlockSpec((tm, tn), lambda i, j, k: (i, j))
    in_specs = [a_spec, b_spec] + ([o_spec] if has_add else [])
    args = (a, b) + ((add,) if has_add else ())
    return pl.pallas_call(
        body, name=name, grid=(m_dim // tm, n_dim // tn, nk),
        in_specs=in_specs, out_specs=o_spec,
        out_shape=jax.ShapeDtypeStruct((m_dim, n_dim), out_dtype),
        scratch_shapes=[pltpu.VMEM((tm, tn), F32)] if nk > 1 else [],
        compiler_params=_params(("parallel", "parallel", "arbitrary")),
    )(*args)


def _mm_core(name, a, b, a_spec, b_spec, o_spec, out_shape, grid, dims, *, add=None, out_dtype=F32, rider=None):
    nk = grid[2]
    has_add = add is not None
    n_in = 3 if has_add else 2
    acc_shape = tuple(d for d in o_spec.block_shape if d is not None)
    extra, extra_shapes, extra_sems = _rider_parts(rider)
    n_w = len(extra)

    def body(*refs):
        a_ref, b_ref = refs[:2]
        o_ref = refs[n_in + n_w]
        step = (pl.program_id(0) * grid[1] + pl.program_id(1)) * nk + pl.program_id(2)
        begin, end = _rider_hooks(rider, refs[n_in:n_in + n_w], refs[n_in + n_w + 1:n_in + 2 * n_w + 1], refs[-2:],
                                  step, grid[0] * grid[1] * nk)
        begin()
        part = lax.dot_general(a_ref[...].astype(BF16), b_ref[...].astype(BF16), dims, preferred_element_type=F32)

        def finish(r):
            if has_add:
                r = refs[2][...] + r
            o_ref[...] = r.astype(out_dtype)

        if nk == 1:
            finish(part)
        else:
            acc_ref = refs[n_in + 2 * n_w + 1]
            k = pl.program_id(2)

            @pl.when(k == 0)
            def _():
                acc_ref[...] = part

            @pl.when(k > 0)
            def _():
                acc_ref[...] += part

            @pl.when(k == nk - 1)
            def _():
                finish(acc_ref[...])
        end()

    in_specs = [a_spec, b_spec] + ([o_spec] if has_add else []) + [ANY] * n_w
    args = (a, b) + ((add,) if has_add else ()) + extra
    res = pl.pallas_call(
        body, name=name, grid=grid, in_specs=in_specs, out_specs=[o_spec] + [ANY] * n_w,
        out_shape=[jax.ShapeDtypeStruct(out_shape, out_dtype)] + extra_shapes,
        scratch_shapes=([pltpu.VMEM(acc_shape, F32)] if nk > 1 else []) + extra_sems,
        compiler_params=_params(("arbitrary",) * 3 if n_w else ("parallel", "parallel", "arbitrary")),
    )(*args)
    return (res[0], res[1:]) if n_w else res[0]


def _mm_cs(a, w3, *, name, add=None, out_dtype=F32):
    m_dim, k_dim = a.shape
    _, _, n4 = w3.shape
    tm, tn, tk = _tile(m_dim, MM_CAP if add is not None else TOK_CAP), _tile(n4, MM_CAP), _tile(k_dim, MM_CAP)
    npb = n4 // tn
    return _mm_core(name, a, w3,
                    pl.BlockSpec((tm, tk), lambda i, j, k: (i, k)),
                    pl.BlockSpec((None, tk, tn), lambda i, j, k: (j // npb, k, j % npb)),
                    pl.BlockSpec((tm, tn), lambda i, j, k: (i, j)),
                    (m_dim, N_CHIPS * n4), (m_dim // tm, N_CHIPS * npb, k_dim // tk), NN, add=add, out_dtype=out_dtype)


def _mm_cs_dx(dy, w3, *, name, out_dtype=F32, rider=None):
    m_dim, _ = dy.shape
    _, k_dim, n4 = w3.shape
    tm, tkw, tn = _tile(m_dim, MM_CAP), _tile(k_dim, MM_CAP), _tile(n4, MM_CAP)
    npb = n4 // tn
    return _mm_core(name, dy, w3,
                    pl.BlockSpec((tm, tn), lambda i, j, k: (i, k)),
                    pl.BlockSpec((None, tkw, tn), lambda i, j, k: (k // npb, j, k % npb)),
                    pl.BlockSpec((tm, tkw), lambda i, j, k: (i, j)),
                    (m_dim, k_dim), (m_dim // tm, k_dim // tkw, N_CHIPS * npb), NT, out_dtype=out_dtype, rider=rider)


def _mm_cs_dw(a, dy, *, name):
    m_dim, k_dim = a.shape
    n4 = dy.shape[1] // N_CHIPS
    tmk, tn, tk = _tile(k_dim, MM_CAP), _tile(n4, MM_CAP), _tile(m_dim, TOK_CAP)
    npb = n4 // tn
    return _mm_core(name, a, dy,
                    pl.BlockSpec((tk, tmk), lambda i, j, k: (k, i)),
                    pl.BlockSpec((tk, tn), lambda i, j, k: (k, j)),
                    pl.BlockSpec((None, tmk, tn), lambda i, j, k: (j // npb, i, j % npb)),
                    (N_CHIPS, k_dim, n4), (k_dim // tmk, N_CHIPS * npb, m_dim // tk), TN)


def _mm_ffn_up_dw(n, d3, *, name):
    t_dim, d = n.shape
    _, _, f4 = d3.shape
    tk = _tile(t_dim, TOK_CAP)
    return _mm_core(name, n, d3,
                    pl.BlockSpec((tk, d), lambda i, j, k: (k, 0)),
                    pl.BlockSpec((None, tk, f4), lambda i, j, k: (j, k, 0)),
                    pl.BlockSpec((None, d, f4), lambda i, j, k: (j, 0, 0)),
                    (N_CHIPS, d, f4), (1, N_CHIPS, t_dim // tk), TN)


def _mm_ffn_up_dx(d3, w3, *, name, add=None):
    _, t_dim, f4 = d3.shape
    _, d, _ = w3.shape
    tm = _tile(t_dim, MM_CAP)
    return _mm_core(name, d3, w3,
                    pl.BlockSpec((None, tm, f4), lambda i, j, k: (k, i, 0)),
                    pl.BlockSpec((None, d, f4), lambda i, j, k: (k, 0, 0)),
                    pl.BlockSpec((tm, d), lambda i, j, k: (i, 0)),
                    (t_dim, d), (t_dim // tm, 1, N_CHIPS), NT, add=add)


def _mm_ffn_down(act3, wd3, *, name, add):
    _, t_dim, f4 = act3.shape
    _, _, d = wd3.shape
    tm = _tile(t_dim, MM_CAP)
    return _mm_core(name, act3, wd3,
                    pl.BlockSpec((None, tm, f4), lambda i, j, k: (k, i, 0)),
                    pl.BlockSpec((None, f4, d), lambda i, j, k: (k, 0, 0)),
                    pl.BlockSpec((tm, d), lambda i, j, k: (i, 0)),
                    (t_dim, d), (t_dim // tm, 1, N_CHIPS), NN, add=add)


def _mm_ffn_down_dw(act3, dh, *, name):
    _, t_dim, f4 = act3.shape
    d = dh.shape[1]
    tk = _tile(t_dim, TOK_CAP)
    return _mm_core(name, act3, dh,
                    pl.BlockSpec((None, tk, f4), lambda i, j, k: (i, k, 0)),
                    pl.BlockSpec((tk, d), lambda i, j, k: (k, 0)),
                    pl.BlockSpec((None, f4, d), lambda i, j, k: (i, 0, 0)),
                    (N_CHIPS, f4, d), (N_CHIPS, 1, t_dim // tk), TN)


def _rms_fwd(x, g, *, name, tt=512):
    t_dim, d = x.shape
    tt = _tile(t_dim, tt, 8)

    def body(x_ref, g_ref, o_ref):
        xv = x_ref[...]
        r = lax.rsqrt(jnp.mean(xv * xv, axis=-1, keepdims=True) + RMS_EPS)
        o_ref[...] = ((xv * r) * g_ref[...]).astype(o_ref.dtype)

    return pl.pallas_call(
        body, name=name, grid=(t_dim // tt,),
        in_specs=[pl.BlockSpec((tt, d), lambda i: (i, 0)), pl.BlockSpec((1, d), lambda i: (0, 0))],
        out_specs=pl.BlockSpec((tt, d), lambda i: (i, 0)),
        out_shape=jax.ShapeDtypeStruct((t_dim, d), BF16),
        compiler_params=_params(("parallel",)),
    )(x, g)


def _rms_bwd(x, g, dy, add, *, name, tt=512, rider=None):
    t_dim, d = x.shape
    tt = _tile(t_dim, tt, 8)
    has_add = add is not None
    n_in = 4 if has_add else 3
    extra, extra_shapes, extra_sems = _rider_parts(rider)
    n_w = len(extra)

    def body(*refs):
        x_ref, g_ref, dy_ref = refs[:3]
        add_ref = refs[3] if has_add else None
        dx_ref, dg_ref = refs[n_in + n_w:n_in + n_w + 2]
        begin, end = _rider_hooks(rider, refs[n_in:n_in + n_w], refs[n_in + n_w + 2:n_in + 2 * n_w + 2], refs[-2:],
                                  pl.program_id(0), t_dim // tt)
        begin()
        xv = x_ref[...]
        dyv = dy_ref[...].astype(F32)
        r = lax.rsqrt(jnp.mean(xv * xv, axis=-1, keepdims=True) + RMS_EPS)
        xh = xv * r
        u = dyv * g_ref[...]
        dx = r * (u - xh * jnp.mean(u * xh, axis=-1, keepdims=True))
        if has_add:
            dx = add_ref[...] + dx
        dx_ref[...] = dx

        @pl.when(pl.program_id(0) == 0)
        def _():
            dg_ref[...] = jnp.zeros_like(dg_ref)

        dg_ref[...] += jnp.sum(dyv * xh, axis=0, keepdims=True)
        end()

    row = pl.BlockSpec((tt, d), lambda i: (i, 0))
    vec = pl.BlockSpec((1, d), lambda i: (0, 0))
    in_specs = [row, vec, row] + ([row] if has_add else []) + [ANY] * n_w
    args = (x, g, dy) + ((add,) if has_add else ()) + extra
    res = pl.pallas_call(
        body, name=name, grid=(t_dim // tt,),
        in_specs=in_specs, out_specs=[row, vec] + [ANY] * n_w,
        out_shape=[jax.ShapeDtypeStruct((t_dim, d), F32), jax.ShapeDtypeStruct((1, d), F32)] + extra_shapes,
        scratch_shapes=extra_sems,
        compiler_params=_params(("arbitrary",)),
    )(*args)
    return (res[0], res[1], res[2:]) if n_w else (res[0], res[1])


def _final(h, g, target, *, tt=512):
    t_dim, d = h.shape
    n_steps = t_dim // tt

    def body(h_ref, g_ref, t_ref, loss_ref, dh_ref, dg_ref, sq_ref):
        i = pl.program_id(0)
        xv = h_ref[...]
        gv = g_ref[...]
        r = lax.rsqrt(jnp.mean(xv * xv, axis=-1, keepdims=True) + RMS_EPS)
        xh = xv * r
        err = xh * gv - t_ref[...]
        dyv = err * (1.0 / d)
        u = dyv * gv
        dh_ref[...] = r * (u - xh * jnp.mean(u * xh, axis=-1, keepdims=True))

        @pl.when(i == 0)
        def _():
            dg_ref[...] = jnp.zeros_like(dg_ref)
            sq_ref[...] = jnp.zeros_like(sq_ref)

        dg_ref[...] += jnp.sum(dyv * xh, axis=0, keepdims=True)
        sq_ref[...] += jnp.sum(err * err, axis=0, keepdims=True)

        @pl.when(i == n_steps - 1)
        def _():
            total = jnp.sum(sq_ref[...], axis=-1, keepdims=True) * (0.5 / d)
            loss_ref[...] = jnp.broadcast_to(total, loss_ref.shape)

    row = pl.BlockSpec((tt, d), lambda i: (i, 0))
    vec = pl.BlockSpec((1, d), lambda i: (0, 0))
    return pl.pallas_call(
        body, name="final_loss", grid=(n_steps,),
        in_specs=[row, vec, row],
        out_specs=[pl.BlockSpec((1, LANES), lambda i: (0, 0)), row, vec],
        out_shape=[jax.ShapeDtypeStruct((1, LANES), F32), jax.ShapeDtypeStruct((t_dim, d), F32),
                   jax.ShapeDtypeStruct((1, d), F32)],
        scratch_shapes=[pltpu.VMEM((1, d), F32)],
        compiler_params=_params(("arbitrary",)),
    )(h, g, target)


def _rope_table(pos, inv_lane, sel_a, sel_b, *, tt=512):
    t_dim = pos.shape[0]

    def body(p_ref, f_ref, a_ref, b_ref, c_ref, s_ref):
        ang = p_ref[...] * f_ref[...]
        on = (a_ref[...] + b_ref[...]) > 0.0
        c_ref[...] = jnp.where(on, jnp.cos(ang), 1.0)
        s_ref[...] = jnp.where(on, jnp.sin(ang), 0.0)

    vec = pl.BlockSpec((1, LANES), lambda i: (0, 0))
    row = pl.BlockSpec((tt, LANES), lambda i: (i, 0))
    shp = jax.ShapeDtypeStruct((t_dim, LANES), F32)
    return pl.pallas_call(
        body, name="rope_table", grid=(t_dim // tt,),
        in_specs=[pl.BlockSpec((tt, 1), lambda i: (i, 0)), vec, vec, vec],
        out_specs=[row, row], out_shape=[shp, shp],
        compiler_params=_params(("parallel",)),
    )(pos, inv_lane, sel_a, sel_b)


def _rotate(xv, cs, sn, sa, sb):
    half = ROPE_DIM // 2
    up = pltpu.roll(xv, LANES - half, 1)
    dn = pltpu.roll(xv, half, 1)
    return xv * cs + (dn * sb - up * sa) * sn


def _head_masks():
    h1 = lax.broadcasted_iota(jnp.int32, (1, LANES), 1) < HEAD_DIM
    return h1, jnp.logical_not(h1)


def _split_heads(xv, h1, h2):
    return jnp.where(h1, xv, 0.0).astype(BF16), jnp.where(h2, xv, 0.0).astype(BF16)


def _tri_masks():
    r = lax.broadcasted_iota(jnp.int32, (BLOCK, BLOCK), 0)
    c = lax.broadcasted_iota(jnp.int32, (BLOCK, BLOCK), 1)
    return c <= r, r <= c


def _stream_rows(start, dil):
    if dil == 1:
        return pl.ds(pl.multiple_of(start, BLOCK), BLOCK)
    return pl.ds(start, BLOCK, stride=dil)


def _dil_tile(idx, dil, nb):
    r = idx // nb
    n = idx % nb
    return (_stream_rows(r + dil * BLOCK * n, dil), _stream_rows(r + dil * BLOCK * jnp.maximum(n - 1, 0), dil),
            n > 0)


def _dil_specs(b_dim, s_dim):
    def col(c0):
        return pl.BlockSpec((None, s_dim, LANES),lambda b, h: (b, 0, c0 + h))
    tab = pl.BlockSpec((None, s_dim, LANES),lambda b, h: (b, 0, 0))
    vec = pl.BlockSpec((1, LANES), lambda b, h: (0, 0))
    return col, tab, vec


def _dil_fwd(proj3, cs3, sn3, sel_a, sel_b, rider=None):
    b_dim, s_dim, _ = proj3.shape
    scale = HEAD_DIM ** -0.5
    n_pat = len(DIL_PATTERNS)
    extra, extra_shapes, extra_sems = _rider_parts(rider)
    n_w = len(extra)
    n_steps = b_dim * PAIRS

    def body(*refs):
        q_ref, k_ref, v_ref, cs_ref, sn_ref, sa_ref, sb_ref = refs[:7]
        o16_ref, o32_ref, l_ref = refs[7 + n_w:10 + n_w]
        qr, kr = refs[10 + 2 * n_w:12 + 2 * n_w]
        per_pattern = refs[12 + 2 * n_w:12 + 2 * n_w + 2 * n_pat]
        og, lg = per_pattern[:n_pat], per_pattern[n_pat:]
        step = pl.program_id(0) * PAIRS + pl.program_id(1)
        begin, end = _rider_hooks(rider, refs[7:7 + n_w], refs[10 + n_w:10 + 2 * n_w], refs[-2:], step, n_steps)
        begin()
        h1, h2 = _head_masks()
        cur_ok, prev_ok = _tri_masks()
        sa, sb = sa_ref[...], sb_ref[...]

        def prep(j, _):
            rows = pl.ds(pl.multiple_of(j * BLOCK, BLOCK), BLOCK)
            cs, sn = cs_ref[rows, :], sn_ref[rows, :]
            qr[rows, :] = _rotate(q_ref[rows, :], cs, sn, sa, sb) * scale
            kr[rows, :] = _rotate(k_ref[rows, :], cs, sn, sa, sb)
            return 0

        lax.fori_loop(0, s_dim // BLOCK, prep, 0)

        for g, (_, dil) in enumerate(DIL_PATTERNS):
            nb = s_dim // dil // BLOCK

            def some(bi, _, g=g, dil=dil, nb=nb):
                tiles = [_dil_tile(bi * DIL_BATCH + t, dil, nb) for t in range(DIL_BATCH)]
                rows = [t[0] for t in tiles]
                q1, q2 = _split_heads(jnp.stack([qr[rw, :] for rw in rows]), h1, h2)
                kc = jnp.stack([kr[rw, :] for rw in rows]).astype(BF16)
                vc1, vc2 = _split_heads(jnp.stack([v_ref[rw, :] for rw in rows]), h1, h2)
                if nb > 1:
                    kp = jnp.stack([kr[t[1], :] for t in tiles]).astype(BF16)
                    vp1, vp2 = _split_heads(jnp.stack([v_ref[t[1], :] for t in tiles]), h1, h2)
                    p_ok = jnp.stack([jnp.logical_and(prev_ok, t[2]) for t in tiles])

                def head(qh, vch, vph):
                    sc = jnp.where(cur_ok, lax.dot_general(qh, kc, BNT, preferred_element_type=F32), -jnp.inf)
                    m = jnp.max(sc, axis=-1, keepdims=True)
                    if nb > 1:
                        sp = jnp.where(p_ok, lax.dot_general(qh, kp, BNT, preferred_element_type=F32), -jnp.inf)
                        m = jnp.maximum(m, jnp.max(sp, axis=-1, keepdims=True))
                    pc = jnp.exp(sc - m)
                    den = jnp.sum(pc, axis=-1, keepdims=True)
                    acc = lax.dot_general(pc.astype(BF16), vch, BNN, preferred_element_type=F32)
                    if nb > 1:
                        pp = jnp.exp(sp - m)
                        den = den + jnp.sum(pp, axis=-1, keepdims=True)
                        acc = acc + lax.dot_general(pp.astype(BF16), vph, BNN, preferred_element_type=F32)
                    return acc / den, m + jnp.log(den)

                o1, l1 = head(q1, vc1, vp1 if nb > 1 else None)
                o2, l2 = head(q2, vc2, vp2 if nb > 1 else None)
                o, l = o1 + o2, jnp.where(h1, l1, l2)
                for t, rw in enumerate(rows):
                    og[g][rw, :] = o[t]
                    lg[g][rw, :] = l[t]
                return 0

            lax.fori_loop(0, dil * nb // DIL_BATCH, some, 0)

        def comb(j, _):
            rows = pl.ds(pl.multiple_of(j * BLOCK, BLOCK), BLOCK)
            ls = [lg[g][rows, :] for g in range(n_pat)]
            m = jnp.maximum(jnp.maximum(ls[0], ls[1]), ls[2])
            es = [jnp.exp(l - m) for l in ls]
            den = es[0] + es[1] + es[2]
            o = (es[0] * og[0][rows, :] + es[1] * og[1][rows, :] + es[2] * og[2][rows, :]) / den
            o16_ref[rows, :] = o.astype(BF16)
            o32_ref[rows, :] = o
            l_ref[rows, :] = m + jnp.log(den)
            return 0

        lax.fori_loop(0, s_dim // BLOCK, comb, 0)
        end()

    col, tab, vec = _dil_specs(b_dim, s_dim)
    out = pl.BlockSpec((None, s_dim, LANES),lambda b, h: (b, 0, h))
    shp = (b_dim, s_dim, ATT_WIDTH)
    res = pl.pallas_call(
        body, name="dil_fwd", grid=(b_dim, PAIRS),
        in_specs=[col(COL_QA), col(COL_KA), col(COL_VA), tab, tab, vec, vec] + [ANY] * n_w,
        out_specs=[out, out, out] + [ANY] * n_w,
        out_shape=[jax.ShapeDtypeStruct(shp, BF16), jax.ShapeDtypeStruct(shp, F32), jax.ShapeDtypeStruct(shp, F32)]
        + extra_shapes,
        scratch_shapes=[pltpu.VMEM((s_dim, LANES), F32)] * (2 + 2 * n_pat) + extra_sems,
        compiler_params=_params(("arbitrary", "arbitrary")),
    )(proj3, proj3, proj3, cs3, sn3, sel_a, sel_b, *extra)
    return res[:3], res[3:]


def _dil_bwd(proj3, cs3, sn3, sel_a, sel_b, do3, o3, lse3, rider=None):
    b_dim, s_dim, _ = proj3.shape
    scale = HEAD_DIM ** -0.5
    extra, extra_shapes, extra_sems = _rider_parts(rider)
    n_w = len(extra)

    def body(*refs):
        q_ref, k_ref, v_ref, cs_ref, sn_ref, sa_ref, sb_ref, do_ref, o_ref, l_ref = refs[:10]
        dq_ref, dk_ref, dv_ref = refs[10 + n_w:13 + n_w]
        qr, kr, dqa, dka, dva = refs[13 + 2 * n_w:18 + 2 * n_w]
        step = pl.program_id(0) * PAIRS + pl.program_id(1)
        begin, end = _rider_hooks(rider, refs[10:10 + n_w], refs[13 + n_w:13 + 2 * n_w], refs[-2:], step,
                                  b_dim * PAIRS)
        begin()
        h1, h2 = _head_masks()
        cur_ok, prev_ok = _tri_masks()
        sa, sb = sa_ref[...], sb_ref[...]

        def prep(j, _):
            rows = pl.ds(pl.multiple_of(j * BLOCK, BLOCK), BLOCK)
            cs, sn = cs_ref[rows, :], sn_ref[rows, :]
            qr[rows, :] = _rotate(q_ref[rows, :], cs, sn, sa, sb) * scale
            kr[rows, :] = _rotate(k_ref[rows, :], cs, sn, sa, sb)
            zero = jnp.zeros((BLOCK, LANES), F32)
            dqa[rows, :] = zero
            dka[rows, :] = zero
            dva[rows, :] = zero
            return 0

        lax.fori_loop(0, s_dim // BLOCK, prep, 0)

        for _, dil in DIL_PATTERNS:
            nb = s_dim // dil // BLOCK

            def some(bi, _, dil=dil, nb=nb):
                tiles = [_dil_tile(bi * DIL_BATCH + t, dil, nb) for t in range(DIL_BATCH)]
                rows = [t[0] for t in tiles]
                q1, q2 = _split_heads(jnp.stack([qr[rw, :] for rw in rows]), h1, h2)
                dof = jnp.stack([do_ref[rw, :] for rw in rows])
                do1, do2 = _split_heads(dof, h1, h2)
                prod = dof * jnp.stack([o_ref[rw, :] for rw in rows])
                delta1 = jnp.sum(jnp.where(h1, prod, 0.0), axis=-1, keepdims=True)
                delta2 = jnp.sum(jnp.where(h2, prod, 0.0), axis=-1, keepdims=True)
                lt = jnp.stack([l_ref[rw, :] for rw in rows])
                lse1 = jnp.max(jnp.where(h1, lt, -jnp.inf), axis=-1, keepdims=True)
                lse2 = jnp.max(jnp.where(h2, lt, -jnp.inf), axis=-1, keepdims=True)

                def side(krows, ok):
                    kf = jnp.stack([kr[kw, :] for kw in krows])
                    k16 = kf.astype(BF16)
                    k1, k2 = _split_heads(kf, h1, h2)
                    v16 = jnp.stack([v_ref[kw, :] for kw in krows]).astype(BF16)

                    def head(qh, doh, lse, delta):
                        sc = lax.dot_general(qh, k16, BNT, preferred_element_type=F32)
                        p = jnp.where(ok, jnp.exp(sc - lse), 0.0)
                        dp = lax.dot_general(doh, v16, BNT, preferred_element_type=F32)
                        return p.astype(BF16), (p * (dp - delta)).astype(BF16)

                    p1, ds1 = head(q1, do1, lse1, delta1)
                    p2, ds2 = head(q2, do2, lse2, delta2)
                    dv = (lax.dot_general(p1, do1, BTN, preferred_element_type=F32)
                          + lax.dot_general(p2, do2, BTN, preferred_element_type=F32))
                    dk = (lax.dot_general(ds1, q1, BTN, preferred_element_type=F32)
                          + lax.dot_general(ds2, q2, BTN, preferred_element_type=F32))
                    for t, kw in enumerate(krows):
                        dva[kw, :] += dv[t]
                        dka[kw, :] += dk[t]
                    return (lax.dot_general(ds1, k1, BNN, preferred_element_type=F32)
                            + lax.dot_general(ds2, k2, BNN, preferred_element_type=F32))

                dq = side(rows, cur_ok)
                if nb > 1:
                    dq = dq + side([t[1] for t in tiles], jnp.stack([jnp.logical_and(prev_ok, t[2]) for t in tiles]))
                for t, rw in enumerate(rows):
                    dqa[rw, :] += dq[t] * scale
                return 0

            lax.fori_loop(0, dil * nb // DIL_BATCH, some, 0)

        def finish(j, _):
            rows = pl.ds(pl.multiple_of(j * BLOCK, BLOCK), BLOCK)
            cs, sn = cs_ref[rows, :], -sn_ref[rows, :]
            dq_ref[rows, :] = _rotate(dqa[rows, :], cs, sn, sa, sb).astype(BF16)
            dk_ref[rows, :] = _rotate(dka[rows, :], cs, sn, sa, sb).astype(BF16)
            dv_ref[rows, :] = dva[rows, :].astype(BF16)
            return 0

        lax.fori_loop(0, s_dim // BLOCK, finish, 0)
        end()

    col, tab, vec = _dil_specs(b_dim, s_dim)
    out = pl.BlockSpec((None, s_dim, LANES),lambda b, h: (b, 0, h))
    shp = jax.ShapeDtypeStruct((b_dim, s_dim, ATT_WIDTH), BF16)
    acc = pltpu.VMEM((s_dim, LANES), F32)
    res = pl.pallas_call(
        body, name="dil_bwd", grid=(b_dim, PAIRS),
        in_specs=[col(COL_QA), col(COL_KA), col(COL_VA), tab, tab, vec, vec, out, out, out] + [ANY] * n_w,
        out_specs=[out, out, out] + [ANY] * n_w, out_shape=[shp, shp, shp] + extra_shapes,
        scratch_shapes=[acc, acc, acc, acc, acc] + extra_sems,
        compiler_params=_params(("arbitrary", "arbitrary")),
    )(proj3, proj3, proj3, cs3, sn3, sel_a, sel_b, do3, o3, lse3, *extra)
    return res[:3], res[3:]


def _split_dot(x, tri):
    hi = x.astype(BF16)
    lo = (x - hi.astype(F32)).astype(BF16)
    return jnp.dot(hi, tri, preferred_element_type=F32) + jnp.dot(lo, tri, preferred_element_type=F32)


def _log_sigmoid(z):
    return jnp.minimum(z, 0.0) - jnp.log(1.0 + jnp.exp(-jnp.abs(z)))


def _sb_scores(qh, k16, valid):
    z = lax.dot_general(qh, k16, NT, preferred_element_type=F32)
    ls = _log_sigmoid(z)
    l1m = ls - z
    return ls, (l1m if valid is None else jnp.where(valid, l1m, 0.0))


def _sb_consts():
    r = lax.broadcasted_iota(jnp.int32, (BLOCK, BLOCK), 0)
    c = lax.broadcasted_iota(jnp.int32, (BLOCK, BLOCK), 1)
    after = (r > c).astype(BF16)
    before = (r < c).astype(BF16)
    qrow = lax.broadcasted_iota(jnp.int32, (SB_ROWS, BLOCK), 0)
    kcol = lax.broadcasted_iota(jnp.int32, (SB_ROWS, BLOCK), 1)
    return after, before, qrow, kcol


def _below(whole, lo, delta):
    if lo == 0:
        return whole + delta
    return whole + jnp.concatenate([jnp.zeros((lo,) + delta.shape[1:], delta.dtype), delta], axis=0)


def _pairs_loop(n_blocks, step, carry):
    def several(i, c):
        for j in range(SB_STEP):
            c = step(SB_STEP * i + j, c)
        return c

    return lax.fori_loop(0, n_blocks // SB_STEP, several, carry)


def _sb_fwd(proj3, rider=None):
    b_dim, s_dim, _ = proj3.shape
    scale = HEAD_DIM ** -0.5
    per = SB_ROWS // BLOCK
    extra, extra_shapes, extra_sems = _rider_parts(rider)
    n_w = len(extra)

    def body(*refs):
        q_ref, k_ref, v_ref = refs[:3]
        o_ref = refs[3 + n_w]
        step = pl.program_id(0) * PAIRS + pl.program_id(1)
        begin, end = _rider_hooks(rider, refs[3:3 + n_w], refs[4 + n_w:4 + 2 * n_w], refs[-2:], step, b_dim * PAIRS)
        begin()
        h1, h2 = _head_masks()
        after, _, qrow, kcol = _sb_consts()

        def qloop(qi, _):
            rows = pl.ds(pl.multiple_of(qi * SB_ROWS, SB_ROWS), SB_ROWS)
            q1, q2 = _split_heads(q_ref[rows, :] * scale, h1, h2)
            first = qi * per

            def block(kb, carry, lo):
                acc, run1, run2 = carry
                krows = pl.ds(pl.multiple_of(kb * BLOCK, BLOCK), BLOCK)
                k16 = k_ref[krows, :].astype(BF16)
                v1, v2 = _split_heads(v_ref[krows, :], h1, h2)
                valid = None if lo is None else kcol[:SB_ROWS - lo] < qrow[:SB_ROWS - lo]
                lo = lo or 0

                def head(qh, vh, run):
                    ls, l1m = _sb_scores(qh[lo:], k16, valid)
                    a = jnp.exp(ls + _split_dot(l1m, after) + run[lo:])
                    if valid is not None:
                        a = jnp.where(valid, a, 0.0)
                    return (jnp.dot(a.astype(BF16), vh, preferred_element_type=F32),
                            _below(run, lo, jnp.sum(l1m, axis=-1, keepdims=True)))

                o1, run1 = head(q1, v1, run1)
                o2, run2 = head(q2, v2, run2)
                return _below(acc, lo, o1 + o2), run1, run2

            zcol = jnp.zeros((SB_ROWS, 1), F32)
            carry = (jnp.zeros((SB_ROWS, LANES), F32), zcol, zcol)
            for kl in reversed(range(per)):
                carry = block(first + kl, carry, kl * BLOCK)
            acc, _, _ = _pairs_loop(first, lambda i, c: block(first - 1 - i, c, None), carry)
            o_ref[rows, :] = acc.astype(BF16)
            return 0

        lax.fori_loop(0, s_dim // SB_ROWS, qloop, 0)
        end()

    def col(c0):
        return pl.BlockSpec((None, s_dim, LANES),lambda b, h: (b, 0, c0 + h))

    res = pl.pallas_call(
        body, name="sb_fwd", grid=(b_dim, PAIRS),
        in_specs=[col(COL_QB), col(COL_KB), col(COL_VB)] + [ANY] * n_w, out_specs=[col(0)] + [ANY] * n_w,
        out_shape=[jax.ShapeDtypeStruct((b_dim, s_dim, ATT_WIDTH), BF16)] + extra_shapes,
        scratch_shapes=extra_sems,
        compiler_params=_params(("arbitrary", "arbitrary")),
    )(proj3, proj3, proj3, *extra)
    return res[0], res[1:]


def _sb_bwd(proj3, do3, rider=None):
    b_dim, s_dim, _ = proj3.shape
    scale = HEAD_DIM ** -0.5
    per = SB_ROWS // BLOCK
    nkb_max = s_dim // BLOCK
    extra, extra_shapes, extra_sems = _rider_parts(rider)
    n_w = len(extra)

    def body(*refs):
        q_ref, k_ref, v_ref, do_ref = refs[:4]
        dq_ref, dk_ref, dv_ref = refs[4 + n_w:7 + n_w]
        dka, dva, e_ref, sg_ref = refs[7 + 2 * n_w:11 + 2 * n_w]
        step = pl.program_id(0) * PAIRS + pl.program_id(1)
        begin, end = _rider_hooks(rider, refs[4:4 + n_w], refs[7 + n_w:7 + 2 * n_w], refs[-2:], step, b_dim * PAIRS)
        begin()
        h1, h2 = _head_masks()
        after, before, qrow, kcol = _sb_consts()
        dka[...] = jnp.zeros_like(dka)
        dva[...] = jnp.zeros_like(dva)

        def qloop(qi, _):
            rows = pl.ds(pl.multiple_of(qi * SB_ROWS, SB_ROWS), SB_ROWS)
            q1, q2 = _split_heads(q_ref[rows, :] * scale, h1, h2)
            do1, do2 = _split_heads(do_ref[rows, :].astype(F32), h1, h2)
            first = qi * per

            def pass1(kb, carry, lo):
                run1, run2 = carry
                krows = pl.ds(pl.multiple_of(kb * BLOCK, BLOCK), BLOCK)
                k16 = k_ref[krows, :].astype(BF16)
                v16 = v_ref[krows, :].astype(BF16)
                valid = None if lo is None else kcol[:SB_ROWS - lo] < qrow[:SB_ROWS - lo]
                lo = lo or 0
                part = pl.ds(lo, SB_ROWS - lo)

                def head(h, qh, doh, run):
                    ls, l1m = _sb_scores(qh[lo:], k16, valid)
                    a = jnp.exp(ls + _split_dot(l1m, after) + run[lo:])
                    if valid is not None:
                        a = jnp.where(valid, a, 0.0)
                    da = lax.dot_general(doh[lo:], v16, NT, preferred_element_type=F32)
                    e_ref[h, kb, part, :] = a * da
                    sg_ref[h, kb, part, :] = jnp.exp(ls)
                    return a.astype(BF16), _below(run, lo, jnp.sum(l1m, axis=-1, keepdims=True))

                a1, run1 = head(0, q1, do1, run1)
                a2, run2 = head(1, q2, do2, run2)
                dva[krows, :] += (lax.dot_general(a1, do1[lo:], TN, preferred_element_type=F32)
                                  + lax.dot_general(a2, do2[lo:], TN, preferred_element_type=F32))
                return run1, run2

            zcol = jnp.zeros((SB_ROWS, 1), F32)
            carry = (zcol, zcol)
            for kl in reversed(range(per)):
                carry = pass1(first + kl, carry, kl * BLOCK)
            _pairs_loop(first, lambda i, c: pass1(first - 1 - i, c, None), carry)

            def pass2(kb, carry, lo):
                dq, pre1, pre2 = carry
                krows = pl.ds(pl.multiple_of(kb * BLOCK, BLOCK), BLOCK)
                k1, k2 = _split_heads(k_ref[krows, :], h1, h2)
                valid = None if lo is None else kcol[:SB_ROWS - lo] < qrow[:SB_ROWS - lo]
                lo = lo or 0
                part = pl.ds(lo, SB_ROWS - lo)

                def head(h, pre):
                    ev = e_ref[h, kb, part, :]
                    sg = sg_ref[h, kb, part, :]
                    dz = ev * (1.0 - sg) - (_split_dot(ev, before) + pre[lo:]) * sg
                    if valid is not None:
                        dz = jnp.where(valid, dz, 0.0)
                    return dz.astype(BF16), _below(pre, lo, jnp.sum(ev, axis=-1, keepdims=True))

                dz1, pre1 = head(0, pre1)
                dz2, pre2 = head(1, pre2)
                dka[krows, :] += (lax.dot_general(dz1, q1[lo:], TN, preferred_element_type=F32)
                                  + lax.dot_general(dz2, q2[lo:], TN, preferred_element_type=F32))
                dq = _below(dq, lo, jnp.dot(dz1, k1, preferred_element_type=F32)
                            + jnp.dot(dz2, k2, preferred_element_type=F32))
                return dq, pre1, pre2

            carry = _pairs_loop(first, lambda i, c: pass2(i, c, None), (jnp.zeros((SB_ROWS, LANES), F32), zcol, zcol))
            for kl in range(per):
                carry = pass2(first + kl, carry, kl * BLOCK)
            dq = carry[0]
            dq_ref[rows, :] = (dq * scale).astype(BF16)
            return 0

        lax.fori_loop(0, s_dim // SB_ROWS, qloop, 0)
        dk_ref[...] = dka[...].astype(BF16)
        dv_ref[...] = dva[...].astype(BF16)
        end()

    def col(c0):
        return pl.BlockSpec((None, s_dim, LANES),lambda b, h: (b, 0, c0 + h))

    shp = jax.ShapeDtypeStruct((b_dim, s_dim, ATT_WIDTH), BF16)
    acc = pltpu.VMEM((s_dim, LANES), F32)
    strip = pltpu.VMEM((2, nkb_max, SB_ROWS, BLOCK), F32)
    res = pl.pallas_call(
        body, name="sb_bwd", grid=(b_dim, PAIRS),
        in_specs=[col(COL_QB), col(COL_KB), col(COL_VB), col(0)] + [ANY] * n_w,
        out_specs=[col(0), col(0), col(0)] + [ANY] * n_w,
        out_shape=[shp, shp, shp] + extra_shapes,
        scratch_shapes=[acc, acc, strip, strip] + extra_sems,
        compiler_params=_params(("arbitrary", "arbitrary")),
    )(proj3, proj3, proj3, do3, *extra)
    return res[:3], res[3:]


def _sigmoid(x):
    return 1.0 / (1.0 + jnp.exp(-x))


def _gate_fwd(proj, ua, ub, *, tt=512):
    t_dim, d = ua.shape

    def body(ga_ref, gb_ref, ua_ref, ub_ref, o_ref):
        o_ref[...] = (_sigmoid(ga_ref[...]) * ua_ref[...] + _sigmoid(gb_ref[...]) * ub_ref[...]).astype(BF16)

    row = pl.BlockSpec((tt, d), lambda i: (i, 0))
    return pl.pallas_call(
        body, name="gate_fwd", grid=(t_dim // tt,),
        in_specs=[pl.BlockSpec((tt, d), lambda i: (i, 3)), pl.BlockSpec((tt, d), lambda i: (i, 4)), row, row],
        out_specs=row, out_shape=jax.ShapeDtypeStruct((t_dim, d), BF16),
        compiler_params=_params(("parallel",)),
    )(proj, proj, ua, ub)


def _gate_bwd(proj, ua, ub, dmix, *, tt=512):
    t_dim, d = ua.shape

    def body(ga_ref, gb_ref, ua_ref, ub_ref, dm_ref, dua_ref, dub_ref, dg_ref):
        dm = dm_ref[...]
        sa = _sigmoid(ga_ref[...])
        sb = _sigmoid(gb_ref[...])
        dua_ref[...] = (dm * sa).astype(BF16)
        dub_ref[...] = (dm * sb).astype(BF16)
        dg_ref[:, :d] = (dm * ua_ref[...] * (sa * (1.0 - sa))).astype(BF16)
        dg_ref[:, d:] = (dm * ub_ref[...] * (sb * (1.0 - sb))).astype(BF16)

    row = pl.BlockSpec((tt, d), lambda i: (i, 0))
    wide = pl.BlockSpec((tt, 2 * d), lambda i: (i, 0))
    return pl.pallas_call(
        body, name="gate_bwd", grid=(t_dim // tt,),
        in_specs=[pl.BlockSpec((tt, d), lambda i: (i, 3)), pl.BlockSpec((tt, d), lambda i: (i, 4)), row, row, row],
        out_specs=[row, row, wide],
        out_shape=[jax.ShapeDtypeStruct((t_dim, d), BF16), jax.ShapeDtypeStruct((t_dim, d), BF16),
                   jax.ShapeDtypeStruct((t_dim, 2 * d), BF16)],
        compiler_params=_params(("parallel",)),
    )(proj, proj, ua, ub, dmix)


def _ffn_up_swiglu(n, wg3, wu3, *, tt=1024):
    t_dim, d = n.shape
    n_s, _, f4 = wg3.shape

    def body(n_ref, wg_ref, wu_ref, g_ref, u_ref, a_ref):
        nv = n_ref[...]
        gv = jnp.dot(nv, wg_ref[...], preferred_element_type=F32)
        uv = jnp.dot(nv, wu_ref[...], preferred_element_type=F32)
        g_ref[...] = gv
        u_ref[...] = uv
        a_ref[...] = (gv * _sigmoid(gv) * uv).astype(BF16)

    wspec = pl.BlockSpec((None, d, f4), lambda i, s: (s, 0, 0))
    ospec = pl.BlockSpec((None, tt, f4), lambda i, s: (s, i, 0))
    shp = (n_s, t_dim, f4)
    return pl.pallas_call(
        body, name="ffn_up_swiglu", grid=(t_dim // tt, n_s),
        in_specs=[pl.BlockSpec((tt, d), lambda i, s: (i, 0)), wspec, wspec], out_specs=[ospec, ospec, ospec],
        out_shape=[jax.ShapeDtypeStruct(shp, F32), jax.ShapeDtypeStruct(shp, F32), jax.ShapeDtypeStruct(shp, BF16)],
        compiler_params=_params(("parallel", "parallel")),
    )(n, wg3, wu3)


def _ffn_down_dx_swiglu(dh, wd3, g3, u3, *, tt=1024):
    t_dim, d = dh.shape
    n_s, f4, _ = wd3.shape

    def body(dh_ref, w_ref, g_ref, u_ref, dg_ref, du_ref):
        da = lax.dot_general(dh_ref[...].astype(BF16), w_ref[...], NT, preferred_element_type=F32)
        gv = g_ref[...]
        sg = _sigmoid(gv)
        dg_ref[...] = (da * u_ref[...] * (sg + gv * sg * (1.0 - sg))).astype(BF16)
        du_ref[...] = (da * (gv * sg)).astype(BF16)

    spec = pl.BlockSpec((None, tt, f4), lambda i, s: (s, i, 0))
    shp = jax.ShapeDtypeStruct((n_s, t_dim, f4), BF16)
    return pl.pallas_call(
        body, name="ffn_down_dx_swiglu", grid=(t_dim // tt, n_s),
        in_specs=[pl.BlockSpec((tt, d), lambda i, s: (i, 0)), pl.BlockSpec((None, f4, d), lambda i, s: (s, 0, 0)),
                  spec, spec],
        out_specs=[spec, spec], out_shape=[shp, shp],
        compiler_params=_params(("parallel", "parallel")),
    )(dh, wd3, g3, u3)


def _mem_fwd(qm, kvm, *, tt=512):
    b_dim, s_dim, _ = qm.shape
    n_mem = kvm.shape[1]
    scale = MEM_HEAD_DIM ** -0.5

    def body(q_ref, k_ref, v_ref, o_ref):
        sc = lax.dot_general(q_ref[0], k_ref[0], NT, preferred_element_type=F32) * scale
        p = jnp.exp(sc - jnp.max(sc, axis=-1, keepdims=True))
        p = p / jnp.sum(p, axis=-1, keepdims=True)
        o_ref[0] = jnp.dot(p.astype(BF16), v_ref[0], preferred_element_type=F32).astype(BF16)

    qs = pl.BlockSpec((1, tt, MEM_HEAD_DIM), lambda b, h, i: (b, i, h))
    return pl.pallas_call(
        body, name="mem_fwd", grid=(b_dim, N_HEADS_MEM, s_dim // tt),
        in_specs=[qs, pl.BlockSpec((1, n_mem, MEM_HEAD_DIM), lambda b, h, i: (b, 0, h)),
                  pl.BlockSpec((1, n_mem, MEM_HEAD_DIM), lambda b, h, i: (b, 0, N_HEADS_MEM + h))],
        out_specs=qs, out_shape=jax.ShapeDtypeStruct(qm.shape, BF16),
        compiler_params=_params(("parallel", "parallel", "parallel")),
    )(qm, kvm, kvm)


def _mem_bwd(qm, kvm, dom, *, tt=512):
    b_dim, s_dim, _ = qm.shape
    n_mem = kvm.shape[1]
    scale = MEM_HEAD_DIM ** -0.5

    def body(q_ref, k_ref, v_ref, do_ref, dq_ref, dk_ref, dv_ref):
        qv, kv, vv, dov = q_ref[0], k_ref[0], v_ref[0], do_ref[0]
        sc = lax.dot_general(qv, kv, NT, preferred_element_type=F32) * scale
        p = jnp.exp(sc - jnp.max(sc, axis=-1, keepdims=True))
        p = p / jnp.sum(p, axis=-1, keepdims=True)
        dp = lax.dot_general(dov, vv, NT, preferred_element_type=F32)
        ds = (p * (dp - jnp.sum(p * dp, axis=-1, keepdims=True)) * scale).astype(BF16)
        dq_ref[0] = jnp.dot(ds, kv, preferred_element_type=F32).astype(BF16)

        @pl.when(pl.program_id(2) == 0)
        def _():
            dk_ref[...] = jnp.zeros_like(dk_ref)
            dv_ref[...] = jnp.zeros_like(dv_ref)

        dk_ref[0] += lax.dot_general(ds, qv, TN, preferred_element_type=F32)
        dv_ref[0] += lax.dot_general(p.astype(BF16), dov, TN, preferred_element_type=F32)

    qs = pl.BlockSpec((1, tt, MEM_HEAD_DIM), lambda b, h, i: (b, i, h))
    ks = pl.BlockSpec((1, n_mem, MEM_HEAD_DIM), lambda b, h, i: (b, 0, h))
    vs = pl.BlockSpec((1, n_mem, MEM_HEAD_DIM), lambda b, h, i: (b, 0, N_HEADS_MEM + h))
    return pl.pallas_call(
        body, name="mem_bwd", grid=(b_dim, N_HEADS_MEM, s_dim // tt),
        in_specs=[qs, ks, vs, qs], out_specs=[qs, ks, ks],
        out_shape=[jax.ShapeDtypeStruct(qm.shape, BF16), jax.ShapeDtypeStruct((b_dim, n_mem, MEM_WIDTH), F32),
                   jax.ShapeDtypeStruct((b_dim, n_mem, MEM_WIDTH), F32)],
        compiler_params=_params(("parallel", "parallel", "arbitrary")),
    )(qm, kvm, kvm, dom)


def _adamw_math(wv, gv, mv, vv):
    nm = ADAM_B1 * mv + (1.0 - ADAM_B1) * gv
    nv = ADAM_B2 * vv + (1.0 - ADAM_B2) * (gv * gv)
    m_hat = nm / (1.0 - ADAM_B1 ** ADAM_STEP)
    v_hat = nv / (1.0 - ADAM_B2 ** ADAM_STEP)
    return -ADAM_LR * (m_hat / (jnp.sqrt(v_hat) + ADAM_EPS) + ADAM_WD * wv), nm, nv


def _adamw(w, g, m, v, *, name):
    rows, cols = w.shape
    tr = _tile(rows, 256, 8)

    def body(w_ref, g_ref, m_ref, v_ref, d_ref, nm_ref, nv_ref):
        d_ref[...], nm_ref[...], nv_ref[...] = _adamw_math(w_ref[...], g_ref[...], m_ref[...], v_ref[...])

    spec = pl.BlockSpec((tr, cols), lambda i: (i, 0))
    shp = jax.ShapeDtypeStruct((rows, cols), F32)
    return pl.pallas_call(
        body, name=name, grid=(rows // tr,),
        in_specs=[spec] * 4, out_specs=[spec] * 3, out_shape=[shp] * 3,
        compiler_params=_params(("parallel",)),
    )(w, g, m, v)


def _prefetch_spec(grid, in_specs, out_specs):
    return pltpu.PrefetchScalarGridSpec(num_scalar_prefetch=1, grid=grid, in_specs=in_specs, out_specs=out_specs)


def _adamw_halves(w, mine, theirs, m, v, c_idx, *, name):
    rows, cols = w.shape
    half = rows // 2
    tr = _tile(half, _row_cap(cols), 8)
    nh = half // tr

    def body(c_ref, w_ref, mine_ref, theirs_ref, m_ref, v_ref, g_ref, d_ref, nm_ref, nv_ref):
        gv = jnp.where(pl.program_id(0) == c_ref[0], mine_ref[...], theirs_ref[...])
        g_ref[...] = gv
        d_ref[...], nm_ref[...], nv_ref[...] = _adamw_math(w_ref[...], gv, m_ref[...], v_ref[...])

    full = pl.BlockSpec((tr, cols), lambda h, i, c_ref: (h * nh + i, 0))
    part = pl.BlockSpec((tr, cols), lambda h, i, c_ref: (i, 0))
    shp = jax.ShapeDtypeStruct((rows, cols), F32)
    return pl.pallas_call(
        body, name=name, grid_spec=_prefetch_spec((2, nh), [full, part, part, full, full], [full] * 4),
        out_shape=[shp] * 4,
        compiler_params=_params(("parallel", "parallel")),
    )(c_idx, w, mine, theirs, m, v)


def _pair_sum(g3, theirs, c_idx, *, name):
    n, rows, cols = g3.shape
    half = rows // 2
    tr = _tile(half, _row_cap(cols), 16)

    def body(c_ref, g_ref, t_ref, o_ref):
        o_ref[...] = (g_ref[...] + t_ref[...]).astype(BF16)

    part = pl.BlockSpec((None, tr, cols), lambda s, i, c_ref: (s, i, 0))
    return pl.pallas_call(
        body, name=name,
        grid_spec=_prefetch_spec((n, half // tr),
                                 [pl.BlockSpec((None, None, tr, cols), lambda s, i, c_ref: (s, c_ref[0], i, 0)), part],
                                 part),
        out_shape=jax.ShapeDtypeStruct((n, half, cols), BF16),
        compiler_params=_params(("parallel", "parallel")),
    )(c_idx, g3.reshape(n, 2, half, cols), theirs)


def _chip_sum(pair, recv, s_idx, *, name):
    _, half, cols = pair.shape
    tr = _tile(half, _row_cap(cols), 16)

    def body(s_ref, p_ref, r_ref, o_ref):
        o_ref[...] = ((p_ref[...].astype(F32) + r_ref[0].astype(F32)) + r_ref[1].astype(F32)) + r_ref[2].astype(F32)

    return pl.pallas_call(
        body, name=name,
        grid_spec=_prefetch_spec((half // tr,),
                                 [pl.BlockSpec((None, tr, cols), lambda i, s_ref: (s_ref[0], i, 0)),
                                  pl.BlockSpec((N_CHIPS - 1, tr, cols), lambda i, s_ref: (0, i, 0))],
                                 pl.BlockSpec((tr, cols), lambda i, s_ref: (i, 0))),
        out_shape=jax.ShapeDtypeStruct((half, cols), F32),
        compiler_params=_params(("parallel",)),
    )(s_idx, pair, recv)


def _sum8(parts):
    n, rows, cols = parts.shape

    def body(p_ref, o_ref):
        acc = p_ref[0]
        for i in range(1, n):
            acc = acc + p_ref[i]
        o_ref[...] = acc

    return pl.pallas_call(
        body, name="small_sum", grid=(1,),
        in_specs=[pl.BlockSpec((n, rows, cols), lambda i: (0, 0, 0))],
        out_specs=pl.BlockSpec((rows, cols), lambda i: (0, 0)),
        out_shape=jax.ShapeDtypeStruct((rows, cols), parts.dtype),
        compiler_params=_params(("arbitrary",)),
    )(parts)


def _place():
    return lax.axis_index("x"), lax.axis_index("y"), lax.axis_index("c")


ANY = pl.BlockSpec(memory_space=pl.ANY)


def _rider_parts(rider):
    if rider is None:
        return (), [], []
    kind, arrays = rider
    n = len(arrays)
    shapes = {"gather": _gathered_shapes, "pair": _pair_shapes, "chip": _chip_shapes}[kind](arrays)
    sems = _gather_sems(n) if kind == "gather" else _exchange_sems(n if kind == "pair" else 3 * n)
    return tuple(arrays), shapes, sems


def _rider_hooks(rider, ins, outs, sems, step, n_steps):
    if rider is None:
        return (lambda: None), (lambda: None)
    if rider[0] == "gather":
        start, forward, finish = _gather_steps(ins, outs, *sems)
    else:
        start, finish = {"pair": _pair_steps, "chip": _chip_steps}[rider[0]](ins, outs, *sems)
        forward = None

    def begin():
        pl.when(step == 0)(start)

    def end():
        if forward is not None:
            pl.when(step == n_steps - 2)(forward)
        pl.when(step == n_steps - 1)(finish)

    return begin, end


def _gather_weights(shards):
    n = len(shards)

    def body(*refs):
        start, forward, finish = _gather_steps(refs[:n], refs[n:2 * n], refs[2 * n], refs[2 * n + 1])
        start()
        forward()
        finish()

    return pl.pallas_call(
        body, name="gather_weights", out_shape=_gathered_shapes(shards),
        in_specs=[ANY] * n, out_specs=[ANY] * n, scratch_shapes=_gather_sems(n),
    )(*shards)


def _gathered_shapes(shards):
    return [jax.ShapeDtypeStruct((N_CHIPS,) + s.shape, s.dtype) for s in shards]


def _gather_sems(n):
    return [pltpu.SemaphoreType.DMA((7 * n,)), pltpu.SemaphoreType.DMA((7 * n,))]


def _gather_steps(ins, outs, send_sems, recv_sems):
    n = len(ins)
    halves = [r.shape[0] // 2 for r in ins]
    x, y, c = _place()
    my_chip = 2 * x + y
    me, sibling = (x, y, c), (x, y, 1 - c)
    chips = [(1 - x, y), (x, 1 - y), (1 - x, 1 - y)]

    def half_of(w, chip, pc):
        return outs[w].at[chip, pl.ds(pc * halves[w], halves[w]), :]

    def copy(w, k, src, dst, to):
        return pltpu.make_async_remote_copy(
            src_ref=src, dst_ref=dst, send_sem=send_sems.at[7 * w + k], recv_sem=recv_sems.at[7 * w + k],
            device_id=to, device_id_type=MESH)

    def firsts():
        cps = []
        for w in range(n):
            cps.append(copy(w, 0, ins[w], outs[w].at[my_chip], sibling))
            mine = ins[w].at[pl.ds(c * halves[w], halves[w]), :]
            for j, (px, py) in enumerate(chips):
                cps.append(copy(w, 1 + j, mine, half_of(w, my_chip, c), (px, py, c)))
        return cps

    def passes():
        return [copy(w, 4 + j, half_of(w, 2 * px + py, c), half_of(w, 2 * px + py, c), sibling)
                for w in range(n) for j, (px, py) in enumerate(chips)]

    def start():
        for cp in firsts():
            cp.start()

    def forward():
        fws = passes()
        for w in range(n):
            for j, (px, py) in enumerate(chips):
                landed = half_of(w, 2 * px + py, c)
                copy(w, 1 + j, landed, landed, me).wait_recv()
                fws[3 * w + j].start()

    def finish():
        for w in range(n):
            copy(w, 0, ins[w], outs[w].at[my_chip], me).wait_recv()
            for j, (px, py) in enumerate(chips):
                landed = half_of(w, 2 * px + py, 1 - c)
                copy(w, 4 + j, landed, landed, me).wait_recv()
        for cp in firsts() + passes():
            cp.wait_send()

    return start, forward, finish


def _pair_shapes(grads):
    return [jax.ShapeDtypeStruct((g.shape[0], g.shape[1] // 2, g.shape[2]), g.dtype) for g in grads]


def _exchange_sems(n):
    return [pltpu.SemaphoreType.DMA((n,)), pltpu.SemaphoreType.DMA((n,))]


def _exchange_steps(copies):
    def start():
        for cp in copies():
            cp.start()

    def finish():
        for cp in copies():
            cp.wait()

    return start, finish


def _pair_steps(ins, outs, send_sems, recv_sems):
    x, y, c = _place()

    def copies():
        return [pltpu.make_async_remote_copy(
            src_ref=ins[w].at[:, pl.ds((1 - c) * (ins[w].shape[1] // 2), ins[w].shape[1] // 2), :], dst_ref=outs[w],
            send_sem=send_sems.at[w], recv_sem=recv_sems.at[w], device_id=(x, y, 1 - c), device_id_type=MESH)
            for w in range(len(ins))]

    return _exchange_steps(copies)


def _chip_shapes(pairs):
    return [jax.ShapeDtypeStruct((N_CHIPS - 1,) + p.shape[1:], p.dtype) for p in pairs]


def _chip_steps(ins, outs, send_sems, recv_sems):
    x, y, c = _place()
    others = [(1 - x, y), (x, 1 - y), (1 - x, 1 - y)]

    def copies():
        return [pltpu.make_async_remote_copy(
            src_ref=ins[w].at[2 * px + py], dst_ref=outs[w].at[j],
            send_sem=send_sems.at[3 * w + j], recv_sem=recv_sems.at[3 * w + j],
            device_id=(px, py, c), device_id_type=MESH)
            for w in range(len(ins)) for j, (px, py) in enumerate(others)]

    return _exchange_steps(copies)


def _swap_halves(mine):
    n = len(mine)

    def body(*refs):
        ins, outs, send_sems, recv_sems = refs[:n], refs[n:2 * n], refs[2 * n], refs[2 * n + 1]
        x, y, c = _place()
        copies = [pltpu.make_async_remote_copy(
            src_ref=ins[w], dst_ref=outs[w], send_sem=send_sems.at[w], recv_sem=recv_sems.at[w],
            device_id=(x, y, 1 - c), device_id_type=MESH) for w in range(n)]
        for cp in copies:
            cp.start()
        for cp in copies:
            cp.wait()

    return pl.pallas_call(
        body, name="grad_swap_halves",
        out_shape=[jax.ShapeDtypeStruct(h.shape, h.dtype) for h in mine],
        in_specs=[ANY] * n, out_specs=[ANY] * n,
        scratch_shapes=[pltpu.SemaphoreType.DMA((n,)), pltpu.SemaphoreType.DMA((n,))],
    )(*mine)


def _gather_small(small):
    srows, cols = small.shape

    def body(s_ref, all_ref, send_sems, recv_sems, local_sem):
        x, y, c = _place()
        me = 4 * x + 2 * y + c
        keep_small = pltpu.make_async_copy(s_ref, all_ref.at[me], local_sem)
        keep_small.start()
        sends = []
        for kk in range(1, 8):
            peer = (x ^ (kk >> 2), y ^ ((kk >> 1) & 1), c ^ (kk & 1))
            sends.append(pltpu.make_async_remote_copy(
                src_ref=s_ref, dst_ref=all_ref.at[me],
                send_sem=send_sems.at[kk], recv_sem=recv_sems.at[kk], device_id=peer, device_id_type=MESH))
        for cp in sends:
            cp.start()
        for kk in range(1, 8):
            px, py, pc = x ^ (kk >> 2), y ^ ((kk >> 1) & 1), c ^ (kk & 1)
            pltpu.make_async_remote_copy(
                src_ref=s_ref, dst_ref=all_ref.at[4 * px + 2 * py + pc],
                send_sem=send_sems.at[kk], recv_sem=recv_sems.at[kk], device_id=(px, py, pc),
                device_id_type=MESH).wait_recv()
        for cp in sends:
            cp.wait_send()
        keep_small.wait()

    return pl.pallas_call(
        body, name="gather_small",
        out_shape=jax.ShapeDtypeStruct((8, srows, cols), small.dtype),
        in_specs=[ANY], out_specs=ANY,
        scratch_shapes=[pltpu.SemaphoreType.DMA((8,)), pltpu.SemaphoreType.DMA((8,)), pltpu.SemaphoreType.DMA],
    )(small)


SHARDED = (("w_in", D_MODEL, IN_COLS, 1), ("w_up_a", ATT_WIDTH, D_MODEL, 1), ("w_up_b", ATT_WIDTH, D_MODEL, 1),
           ("w_out", D_MODEL, D_MODEL, 0), ("w_q_mem", D_MODEL, MEM_WIDTH, 0), ("w_kv_mem", D_MODEL, 2 * MEM_WIDTH, 0),
           ("w_o_mem", MEM_WIDTH, D_MODEL, 1), ("w_ffn_gate", D_MODEL, D_FF, 1), ("w_ffn_up", D_MODEL, D_FF, 1),
           ("w_ffn_down", D_FF, D_MODEL, 0))
NAMES = tuple(n for n, _, _, _ in SHARDED)
EARLY, LATE = NAMES[:1], NAMES[1:]
GAINS = ("g_mix", "g_mem_q", "g_mem_kv", "g_ffn", "g_final")


def _natural(w3):
    n, r, c = w3.shape
    return w3.reshape(n * r, c)


def _shard_major(g, axis):
    if axis == 1:
        return g
    r, c = g.shape
    return g.reshape(N_CHIPS, r // N_CHIPS, c)


def kernel(x, mem, positions, g_mix, w_in, w_up_a, w_up_b, w_out, g_mem_q, g_mem_kv, w_q_mem, w_kv_mem, w_o_mem, g_ffn, w_ffn_gate, w_ffn_up, w_ffn_down, g_final, loss_target, m_g_mix, m_w_in, m_w_up_a, m_w_up_b, m_w_out, m_g_mem_q, m_g_mem_kv, m_w_q_mem, m_w_kv_mem, m_w_o_mem, m_g_ffn, m_w_ffn_gate, m_w_ffn_up, m_w_ffn_down, m_g_final, v_g_mix, v_w_in, v_w_up_a, v_w_up_b, v_w_out, v_g_mem_q, v_g_mem_kv, v_w_q_mem, v_w_kv_mem, v_w_o_mem, v_g_ffn, v_w_ffn_gate, v_w_ffn_up, v_w_ffn_down, v_g_final):
    given = dict(locals())
    shards = {n: given[n][0] for n, _, _, _ in SHARDED}

    wf = dict(zip(EARLY, _gather_weights([shards[n].astype(BF16) for n in EARLY])))
    late_shards = [shards[n].astype(BF16) for n in LATE]
    c_idx = lax.axis_index("c").astype(jnp.int32).reshape(1)
    s_idx = (2 * lax.axis_index("x") + lax.axis_index("y")).astype(jnp.int32).reshape(1)

    loss_row, grad_x, mine, gain_grads = _local_step(x, mem, positions, loss_target, g_mix, g_mem_q, g_mem_kv,
                                                     g_ffn, g_final, wf, late_shards, (c_idx, s_idx))
    return _reduce_and_update(given, shards, loss_row, grad_x, mine, gain_grads, c_idx)


def _reduce_halves(glist, names, c_idx, s_idx, pair_exchange, chip_exchange):
    theirs = pair_exchange(glist)
    pairs = [_pair_sum(g, t, c_idx, name="pair_sum_" + n) for n, g, t in zip(names, glist, theirs)]
    recv = chip_exchange(pairs)
    return [_chip_sum(p, r, s_idx, name="chip_sum_" + n) for n, p, r in zip(names, pairs, recv)]


def _local_step(x, mem, positions, loss_target, g_mix, g_mem_q, g_mem_kv, g_ffn, g_final, wf,
                late_shards=None, place=None):
    b_dim, s_dim, d = x.shape
    t_dim = b_dim * s_dim
    n_mem = mem.shape[1]
    wf = dict(wf)

    xb = x.reshape(t_dim, d)
    tgt = loss_target.reshape(t_dim, d)
    memf = mem.reshape(b_dim * n_mem, d)
    gfin = g_final.reshape(1, d)
    pos = positions.reshape(t_dim, 1).astype(F32)

    lane = jnp.arange(LANES) % HEAD_DIM
    half = ROPE_DIM // 2
    inv_freq = ROPE_THETA ** (-jnp.arange(half, dtype=F32) / half)
    inv_lane = jnp.where(lane < ROPE_DIM, inv_freq[lane % half], 0.0).reshape(1, -1).astype(F32)
    sel_a = (lane < half).astype(F32).reshape(1, -1)
    sel_b = ((lane >= half) & (lane < ROPE_DIM)).astype(F32).reshape(1, -1)

    def rows3(t):
        return t.reshape(b_dim, s_dim, t.shape[-1])

    def rows2(t):
        return t.reshape(t_dim, t.shape[-1])

    n1 = _rms_fwd(xb, g_mix, name="rms_mix")
    proj = _mm_cs(n1, wf["w_in"], name="mm_in")
    proj3 = rows3(proj)
    cs, sn = _rope_table(pos, inv_lane, sel_a, sel_b)
    cs3, sn3 = rows3(cs), rows3(sn)
    (oa16, oa32, lse_a), _ = _dil_fwd(proj3, cs3, sn3, sel_a, sel_b)
    ob16, gathered = _sb_fwd(proj3, ("gather", late_shards) if late_shards else None)
    wf.update(zip(LATE, gathered))
    w_out, w_q, w_kv = _natural(wf["w_out"]), _natural(wf["w_q_mem"]), _natural(wf["w_kv_mem"])
    oa, ob = rows2(oa16), rows2(ob16)
    ua = _mm_cs(oa, wf["w_up_a"], name="mm_up_a")
    ub = _mm_cs(ob, wf["w_up_b"], name="mm_up_b")
    mixed = _gate_fwd(proj, ua, ub)
    h1 = _mm(mixed, w_out, name="mm_out", add=xb)

    hn = _rms_fwd(h1, g_mem_q, name="rms_mem_q")
    memn = _rms_fwd(memf, g_mem_kv, name="rms_mem_kv")
    qm = _mm(hn, w_q, name="mm_q_mem", out_dtype=BF16)
    kvm = _mm(memn, w_kv, name="mm_kv_mem", out_dtype=BF16)
    qm3, kvm3 = rows3(qm), kvm.reshape(b_dim, n_mem, 2 * MEM_WIDTH)
    om = rows2(_mem_fwd(qm3, kvm3))
    h2 = _mm_cs(om, wf["w_o_mem"], name="mm_o_mem", add=h1)

    n3 = _rms_fwd(h2, g_ffn, name="rms_ffn")
    gate3, up3, act3 = _ffn_up_swiglu(n3, wf["w_ffn_gate"], wf["w_ffn_up"])
    h3 = _mm_ffn_down(act3, wf["w_ffn_down"], name="mm_down", add=h2)
    loss_row, dh3, dg_final = _final(h3, gfin, tgt)

    grads = {}
    grads["w_ffn_down"] = _mm_ffn_down_dw(act3, dh3, name="mm_down_dw")
    dgate3, dup3 = _ffn_down_dx_swiglu(dh3, wf["w_ffn_down"], gate3, up3)
    grads["w_ffn_gate"] = _mm_ffn_up_dw(n3, dgate3, name="mm_gate_dw")
    grads["w_ffn_up"] = _mm_ffn_up_dw(n3, dup3, name="mm_up_dw")
    dn3 = _mm_ffn_up_dx(dgate3, wf["w_ffn_gate"], name="mm_gate_dx")
    dn3 = _mm_ffn_up_dx(dup3, wf["w_ffn_up"], name="mm_up_dx", add=dn3)
    dh2, dg_ffn = _rms_bwd(h2, g_ffn, dn3, dh3, name="rms_ffn_bwd")

    dom = _mm_cs_dx(dh2, wf["w_o_mem"], name="mm_o_mem_dx", out_dtype=BF16)
    grads["w_o_mem"] = _mm_cs_dw(om, dh2, name="mm_o_mem_dw")
    dqm, dkm, dvm = _mem_bwd(qm3, kvm3, rows3(dom))
    dqm = rows2(dqm)
    dkvm = jnp.concatenate([dkm, dvm], axis=-1).reshape(b_dim * n_mem, 2 * MEM_WIDTH).astype(BF16)
    grads["w_q_mem"] = _shard_major(_mm(hn, dqm, name="mm_q_mem_dw", ta=True), 0)
    dhn = _mm(dqm, w_q, name="mm_q_mem_dx", tb=True)
    grads["w_kv_mem"] = _shard_major(_mm(memn, dkvm, name="mm_kv_mem_dw", ta=True), 0)
    dmemn = _mm(dkvm, w_kv, name="mm_kv_mem_dx", tb=True)
    _, dg_mem_kv = _rms_bwd(memf, g_mem_kv, dmemn, None, name="rms_mem_kv_bwd")
    dh1, dg_mem_q = _rms_bwd(h1, g_mem_q, dhn, dh2, name="rms_mem_q_bwd")

    dmix = _mm(dh1, w_out, name="mm_out_dx", tb=True)
    grads["w_out"] = _shard_major(_mm(mixed, dh1, name="mm_out_dw", ta=True), 0)
    dua, dub, dgates = _gate_bwd(proj, ua, ub, dmix)
    doa = _mm_cs_dx(dua, wf["w_up_a"], name="mm_up_a_dx")
    grads["w_up_a"] = _mm_cs_dw(oa, dua, name="mm_up_a_dw")
    dob = _mm_cs_dx(dub, wf["w_up_b"], name="mm_up_b_dx", out_dtype=BF16)
    grads["w_up_b"] = _mm_cs_dw(ob, dub, name="mm_up_b_dw")

    att = {}

    def dil_with_pairs(glist):
        att["a"], theirs = _dil_bwd(proj3, cs3, sn3, sel_a, sel_b, rows3(doa), oa32, lse_a,
                                    ("pair", glist) if glist else None)
        return theirs

    def sb_with_chips(pairs):
        att["b"], recv = _sb_bwd(proj3, rows3(dob), ("chip", pairs) if pairs else None)
        return recv

    if place is None:
        dil_with_pairs(())
        sb_with_chips(())
    else:
        mine_late = _reduce_halves([grads[n] for n in LATE], LATE, *place, dil_with_pairs, sb_with_chips)
    dproj = jnp.concatenate([rows2(t) for t in att["a"] + att["b"]] + [dgates], axis=1)
    grads["w_in"] = _mm_cs_dw(n1, dproj, name="mm_in_dw")
    if place is None:
        dn1 = _mm_cs_dx(dproj, wf["w_in"], name="mm_in_dx")
        dx, dg_mix = _rms_bwd(xb, g_mix, dn1, dh1, name="rms_mix_bwd")
    else:
        tail = {}

        def dx_with_pairs(glist):
            tail["dn1"], theirs = _mm_cs_dx(dproj, wf["w_in"], name="mm_in_dx", rider=("pair", glist))
            return theirs

        def rms_with_chips(pairs):
            tail["dx"], tail["dg"], recv = _rms_bwd(xb, g_mix, tail["dn1"], dh1, name="rms_mix_bwd",
                                                    rider=("chip", pairs))
            return recv

        mine_early = _reduce_halves([grads[n] for n in EARLY], EARLY, *place, dx_with_pairs, rms_with_chips)
        dx, dg_mix = tail["dx"], tail["dg"]
    grad_x = dx.reshape(b_dim, s_dim, d)
    gains = (dg_mix, dg_mem_q, dg_mem_kv, dg_ffn, dg_final)
    if place is None:
        return loss_row, grad_x, grads, gains
    return loss_row, grad_x, mine_early + mine_late, gains


def _reduce_and_update(given, shards, loss_row, grad_x, mine, gain_grads, c_idx):
    d = D_MODEL
    dg_mix, dg_mem_q, dg_mem_kv, dg_ffn, dg_final = gain_grads
    small = jnp.concatenate([dg_mix, dg_mem_q, dg_mem_kv, dg_ffn, dg_final,
                             jnp.pad(loss_row, ((0, 0), (0, FLAT_COLS - LANES))), jnp.zeros((2, FLAT_COLS), F32)], axis=0)
    small_all = _gather_small(small)
    others = _swap_halves(mine)
    small_sum = _sum8(small_all)
    loss = small_sum[5, 0]

    out_g, out_d, out_m, out_v = {}, {}, {}, {}
    for n, mine_n, other_n in zip(NAMES, mine, others):
        g2, dl, nm, nv = _adamw_halves(shards[n], mine_n, other_n, given["m_" + n][0], given["v_" + n][0], c_idx,
                                       name="adamw_" + n)
        out_g[n], out_d[n], out_m[n], out_v[n] = g2[None], dl[None], nm[None], nv[None]
    gain_w = jnp.concatenate([given[n].reshape(1, d) for n in GAINS], axis=0)
    gain_m = jnp.concatenate([given["m_" + n].reshape(1, d) for n in GAINS], axis=0)
    gain_v = jnp.concatenate([given["v_" + n].reshape(1, d) for n in GAINS], axis=0)
    gain_g = small_sum[:len(GAINS)]
    gd, gm, gv = _adamw(gain_w, gain_g, gain_m, gain_v, name="adamw_gains")
    for i, n in enumerate(GAINS):
        shape = given[n].shape
        out_g[n], out_d[n] = gain_g[i].reshape(shape), gd[i].reshape(shape)
        out_m[n], out_v[n] = gm[i].reshape(shape), gv[i].reshape(shape)

    order = ["g_mix", "w_in", "w_up_a", "w_up_b", "w_out", "g_mem_q", "g_mem_kv", "w_q_mem", "w_kv_mem", "w_o_mem",
             "g_ffn", "w_ffn_gate", "w_ffn_up", "w_ffn_down", "g_final"]
    return (loss, grad_x, *[out_g[n] for n in order], *[out_d[n] for n in order],
            *[out_m[n] for n in order], *[out_v[n] for n in order])
```

```python
import jax
import jax.numpy as jnp
from jax import lax
from jax.experimental import pallas as pl
from jax.experimental.pallas import tpu as pltpu

F32 = jnp.float32
BF16 = jnp.bfloat16
MESH = pl.DeviceIdType.MESH

D_MODEL = 1024
HEAD_DIM = 64
N_HEADS = 8
ATT_WIDTH = N_HEADS * HEAD_DIM
DIL_PATTERNS = ((128, 1), (512, 4), (2048, 16))
BLOCK = 128
SB_ROWS = 1024
SB_STEP = 4
ROPE_THETA = 500000.0
ROPE_DIM = HEAD_DIM // 4
N_HEADS_MEM = 4
MEM_HEAD_DIM = 128
MEM_WIDTH = N_HEADS_MEM * MEM_HEAD_DIM
D_FF = 2816
IN_COLS = 6 * ATT_WIDTH + 2 * D_MODEL
RMS_EPS = 1e-6
ADAM_LR = 0.001
ADAM_B1 = 0.9
ADAM_B2 = 0.999
ADAM_EPS = 1e-08
ADAM_WD = 0.01
ADAM_STEP = 10

N_CHIPS = 4
LANES = 128
FLAT_COLS = 1024
VMEM_LIMIT = 56 * 1024 * 1024

PAIRS = ATT_WIDTH // LANES
COL_QA, COL_KA, COL_VA, COL_QB, COL_KB, COL_VB = (i * PAIRS for i in range(6))

MM_CAP = 1408
TOK_CAP = 2048
NN = (((1,), (0,)), ((), ()))
NT = (((1,), (1,)), ((), ()))
TN = (((0,), (0,)), ((), ()))
BNN = (((2,), (1,)), ((0,), (0,)))
BNT = (((2,), (2,)), ((0,), (0,)))
BTN = (((1,), (1,)), ((0,), (0,)))
DIL_BATCH = 8


def _tile(dim, cap, unit=LANES):
    if dim <= cap:
        return dim
    best = None
    for t in range(unit, cap + 1, unit):
        if dim % t == 0:
            best = t
    assert best is not None, (dim, cap)
    return best


def _row_cap(cols):
    return max(256, (1 << 18) // cols)


def _params(sem):
    return pltpu.CompilerParams(dimension_semantics=sem, vmem_limit_bytes=VMEM_LIMIT)


def _mm(a, b, *, name, ta=False, tb=False, add=None, out_dtype=F32,
        tm_cap=MM_CAP, tn_cap=MM_CAP, tk_cap=MM_CAP):
    if ta:
        k_dim, m_dim = a.shape
    else:
        m_dim, k_dim = a.shape
    if tb:
        n_dim, kb = b.shape
    else:
        kb, n_dim = b.shape
    assert kb == k_dim, (a.shape, b.shape, ta, tb)
    tm, tn, tk = _tile(m_dim, tm_cap), _tile(n_dim, tn_cap), _tile(k_dim, tk_cap)
    nk = k_dim // tk
    dims = (((0 if ta else 1,), (1 if tb else 0,)), ((), ()))
    has_add = add is not None

    def body(*refs):
        if has_add:
            a_ref, b_ref, add_ref, o_ref = refs[:4]
        else:
            a_ref, b_ref, o_ref = refs[:3]
        part = lax.dot_general(a_ref[...].astype(BF16), b_ref[...].astype(BF16), dims, preferred_element_type=F32)

        def finish(r):
            if has_add:
                r = add_ref[...] + r
            o_ref[...] = r.astype(out_dtype)

        if nk == 1:
            finish(part)
            return
        acc_ref = refs[-1]
        k = pl.program_id(2)

        @pl.when(k == 0)
        def _():
            acc_ref[...] = part

        @pl.when(k > 0)
        def _():
            acc_ref[...] += part

        @pl.when(k == nk - 1)
        def _():
            finish(acc_ref[...])

    a_spec = pl.BlockSpec((tk, tm), lambda i, j, k: (k, i)) if ta else pl.BlockSpec((tm, tk), lambda i, j, k: (i, k))
    b_spec = pl.BlockSpec((tn, tk), lambda i, j, k: (j, k)) if tb else pl.BlockSpec((tk, tn), lambda i, j, k: (k, j))
    o_spec = pl.BlockSpec((tm, tn), lambda i, j, k: (i, j))
    in_specs = [a_spec, b_spec] + ([o_spec] if has_add else [])
    args = (a, b) + ((add,) if has_add else ())
    return pl.pallas_call(
        body, name=name, grid=(m_dim // tm, n_dim // tn, nk),
        in_specs=in_specs, out_specs=o_spec,
        out_shape=jax.ShapeDtypeStruct((m_dim, n_dim), out_dtype),
        scratch_shapes=[pltpu.VMEM((tm, tn), F32)] if nk > 1 else [],
        compiler_params=_params(("parallel", "parallel", "arbitrary")),
    )(*args)


def _mm_core(name, a, b, a_spec, b_spec, o_spec, out_shape, grid, dims, *, add=None, out_dtype=F32, rider=None):
    nk = grid[2]
    has_add = add is not None
    n_in = 3 if has_add else 2
    acc_shape = tuple(d for d in o_spec.block_shape if d is not None)
    extra, extra_shapes, extra_sems = _rider_parts(rider)
    n_w = len(extra)

    def body(*refs):
        a_ref, b_ref = refs[:2]
        o_ref = refs[n_in + n_w]
        step = (pl.program_id(0) * grid[1] + pl.program_id(1)) * nk + pl.program_id(2)
        begin, end = _rider_hooks(rider, refs[n_in:n_in + n_w], refs[n_in + n_w + 1:n_in + 2 * n_w + 1], refs[-2:],
                                  step, grid[0] * grid[1] * nk)
        begin()
        part = lax.dot_general(a_ref[...].astype(BF16), b_ref[...].astype(BF16), dims, preferred_element_type=F32)

        def finish(r):
            if has_add:
                r = refs[2][...] + r
            o_ref[...] = r.astype(out_dtype)

        if nk == 1:
            finish(part)
        else:
            acc_ref = refs[n_in + 2 * n_w + 1]
            k = pl.program_id(2)

            @pl.when(k == 0)
            def _():
                acc_ref[...] = part

            @pl.when(k > 0)
            def _():
                acc_ref[...] += part

            @pl.when(k == nk - 1)
            def _():
                finish(acc_ref[...])
        end()

    in_specs = [a_spec, b_spec] + ([o_spec] if has_add else []) + [ANY] * n_w
    args = (a, b) + ((add,) if has_add else ()) + extra
    res = pl.pallas_call(
        body, name=name, grid=grid, in_specs=in_specs, out_specs=[o_spec] + [ANY] * n_w,
        out_shape=[jax.ShapeDtypeStruct(out_shape, out_dtype)] + extra_shapes,
        scratch_shapes=([pltpu.VMEM(acc_shape, F32)] if nk > 1 else []) + extra_sems,
        compiler_params=_params(("arbitrary",) * 3 if n_w else ("parallel", "parallel", "arbitrary")),
    )(*args)
    return (res[0], res[1:]) if n_w else res[0]


def _mm_cs(a, w3, *, name, add=None, out_dtype=F32):
    m_dim, k_dim = a.shape
    _, _, n4 = w3.shape
    tm, tn, tk = _tile(m_dim, MM_CAP if add is not None else TOK_CAP), _tile(n4, MM_CAP), _tile(k_dim, MM_CAP)
    npb = n4 // tn
    return _mm_core(name, a, w3,
                    pl.BlockSpec((tm, tk), lambda i, j, k: (i, k)),
                    pl.BlockSpec((None, tk, tn), lambda i, j, k: (j // npb, k, j % npb)),
                    pl.BlockSpec((tm, tn), lambda i, j, k: (i, j)),
                    (m_dim, N_CHIPS * n4), (m_dim // tm, N_CHIPS * npb, k_dim // tk), NN, add=add, out_dtype=out_dtype)


def _mm_cs_dx(dy, w3, *, name, out_dtype=F32, rider=None):
    m_dim, _ = dy.shape
    _, k_dim, n4 = w3.shape
    tm, tkw, tn = _tile(m_dim, MM_CAP), _tile(k_dim, MM_CAP), _tile(n4, MM_CAP)
    npb = n4 // tn
    return _mm_core(name, dy, w3,
                    pl.BlockSpec((tm, tn), lambda i, j, k: (i, k)),
                    pl.BlockSpec((None, tkw, tn), lambda i, j, k: (k // npb, j, k % npb)),
                    pl.BlockSpec((tm, tkw), lambda i, j, k: (i, j)),
                    (m_dim, k_dim), (m_dim // tm, k_dim // tkw, N_CHIPS * npb), NT, out_dtype=out_dtype, rider=rider)


def _mm_cs_dw(a, dy, *, name):
    m_dim, k_dim = a.shape
    n4 = dy.shape[1] // N_CHIPS
    tmk, tn, tk = _tile(k_dim, MM_CAP), _tile(n4, MM_CAP), _tile(m_dim, TOK_CAP)
    npb = n4 // tn
    return _mm_core(name, a, dy,
                    pl.BlockSpec((tk, tmk), lambda i, j, k: (k, i)),
                    pl.BlockSpec((tk, tn), lambda i, j, k: (k, j)),
                    pl.BlockSpec((None, tmk, tn), lambda i, j, k: (j // npb, i, j % npb)),
                    (N_CHIPS, k_dim, n4), (k_dim // tmk, N_CHIPS * npb, m_dim // tk), TN)


def _mm_ffn_up_dw(n, d3, *, name):
    t_dim, d = n.shape
    _, _, f4 = d3.shape
    tk = _tile(t_dim, TOK_CAP)
    return _mm_core(name, n, d3,
                    pl.BlockSpec((tk, d), lambda i, j, k: (k, 0)),
                    pl.BlockSpec((None, tk, f4), lambda i, j, k: (j, k, 0)),
                    pl.BlockSpec((None, d, f4), lambda i, j, k: (j, 0, 0)),
                    (N_CHIPS, d, f4), (1, N_CHIPS, t_dim // tk), TN)


def _mm_ffn_up_dx(d3, w3, *, name, add=None):
    _, t_dim, f4 = d3.shape
    _, d, _ = w3.shape
    tm = _tile(t_dim, MM_CAP)
    return _mm_core(name, d3, w3,
                    pl.BlockSpec((None, tm, f4), lambda i, j, k: (k, i, 0)),
                    pl.BlockSpec((None, d, f4), lambda i, j, k: (k, 0, 0)),
                    pl.BlockSpec((tm, d), lambda i, j, k: (i, 0)),
                    (t_dim, d), (t_dim // tm, 1, N_CHIPS), NT, add=add)


def _mm_ffn_down(act3, wd3, *, name, add):
    _, t_dim, f4 = act3.shape
    _, _, d = wd3.shape
    tm = _tile(t_dim, MM_CAP)
    return _mm_core(name, act3, wd3,
                    pl.BlockSpec((None, tm, f4), lambda i, j, k: (k, i, 0)),
                    pl.BlockSpec((None, f4, d), lambda i, j, k: (k, 0, 0)),
                    pl.BlockSpec((tm, d), lambda i, j, k: (i, 0)),
                    (t_dim, d), (t_dim // tm, 1, N_CHIPS), NN, add=add)


def _mm_ffn_down_dw(act3, dh, *, name):
    _, t_dim, f4 = act3.shape
    d = dh.shape[1]
    tk = _tile(t_dim, TOK_CAP)
    return _mm_core(name, act3, dh,
                    pl.BlockSpec((None, tk, f4), lambda i, j, k: (i, k, 0)),
                    pl.BlockSpec((tk, d), lambda i, j, k: (k, 0)),
                    pl.BlockSpec((None, f4, d), lambda i, j, k: (i, 0, 0)),
                    (N_CHIPS, f4, d), (N_CHIPS, 1, t_dim // tk), TN)


def _rms_fwd(x, g, *, name, tt=512):
    t_dim, d = x.shape
    tt = _tile(t_dim, tt, 8)

    def body(x_ref, g_ref, o_ref):
        xv = x_ref[...]
        r = lax.rsqrt(jnp.mean(xv * xv, axis=-1, keepdims=True) + RMS_EPS)
        o_ref[...] = ((xv * r) * g_ref[...]).astype(o_ref.dtype)

    return pl.pallas_call(
        body, name=name, grid=(t_dim // tt,),
        in_specs=[pl.BlockSpec((tt, d), lambda i: (i, 0)), pl.BlockSpec((1, d), lambda i: (0, 0))],
        out_specs=pl.BlockSpec((tt, d), lambda i: (i, 0)),
        out_shape=jax.ShapeDtypeStruct((t_dim, d), BF16),
        compiler_params=_params(("parallel",)),
    )(x, g)


def _rms_bwd(x, g, dy, add, *, name, tt=512, rider=None):
    t_dim, d = x.shape
    tt = _tile(t_dim, tt, 8)
    has_add = add is not None
    n_in = 4 if has_add else 3
    extra, extra_shapes, extra_sems = _rider_parts(rider)
    n_w = len(extra)

    def body(*refs):
        x_ref, g_ref, dy_ref = refs[:3]
        add_ref = refs[3] if has_add else None
        dx_ref, dg_ref = refs[n_in + n_w:n_in + n_w + 2]
        begin, end = _rider_hooks(rider, refs[n_in:n_in + n_w], refs[n_in + n_w + 2:n_in + 2 * n_w + 2], refs[-2:],
                                  pl.program_id(0), t_dim // tt)
        begin()
        xv = x_ref[...]
        dyv = dy_ref[...].astype(F32)
        r = lax.rsqrt(jnp.mean(xv * xv, axis=-1, keepdims=True) + RMS_EPS)
        xh = xv * r
        u = dyv * g_ref[...]
        dx = r * (u - xh * jnp.mean(u * xh, axis=-1, keepdims=True))
        if has_add:
            dx = add_ref[...] + dx
        dx_ref[...] = dx

        @pl.when(pl.program_id(0) == 0)
        def _():
            dg_ref[...] = jnp.zeros_like(dg_ref)

        dg_ref[...] += jnp.sum(dyv * xh, axis=0, keepdims=True)
        end()

    row = pl.BlockSpec((tt, d), lambda i: (i, 0))
    vec = pl.BlockSpec((1, d), lambda i: (0, 0))
    in_specs = [row, vec, row] + ([row] if has_add else []) + [ANY] * n_w
    args = (x, g, dy) + ((add,) if has_add else ()) + extra
    res = pl.pallas_call(
        body, name=name, grid=(t_dim // tt,),
        in_specs=in_specs, out_specs=[row, vec] + [ANY] * n_w,
        out_shape=[jax.ShapeDtypeStruct((t_dim, d), F32), jax.ShapeDtypeStruct((1, d), F32)] + extra_shapes,
        scratch_shapes=extra_sems,
        compiler_params=_params(("arbitrary",)),
    )(*args)
    return (res[0], res[1], res[2:]) if n_w else (res[0], res[1])


def _final(h, g, target, *, tt=512):
    t_dim, d = h.shape
    n_steps = t_dim // tt

    def body(h_ref, g_ref, t_ref, loss_ref, dh_ref, dg_ref, sq_ref):
        i = pl.program_id(0)
        xv = h_ref[...]
        gv = g_ref[...]
        r = lax.rsqrt(jnp.mean(xv * xv, axis=-1, keepdims=True) + RMS_EPS)
        xh = xv * r
        err = xh * gv - t_ref[...]
        dyv = err * (1.0 / d)
        u = dyv * gv
        dh_ref[...] = r * (u - xh * jnp.mean(u * xh, axis=-1, keepdims=True))

        @pl.when(i == 0)
        def _():
            dg_ref[...] = jnp.zeros_like(dg_ref)
            sq_ref[...] = jnp.zeros_like(sq_ref)

        dg_ref[...] += jnp.sum(dyv * xh, axis=0, keepdims=True)
        sq_ref[...] += jnp.sum(err * err, axis=0, keepdims=True)

        @pl.when(i == n_steps - 1)
        def _():
            total = jnp.sum(sq_ref[...], axis=-1, keepdims=True) * (0.5 / d)
            loss_ref[...] = jnp.broadcast_to(total, loss_ref.shape)

    row = pl.BlockSpec((tt, d), lambda i: (i, 0))
    vec = pl.BlockSpec((1, d), lambda i: (0, 0))
    return pl.pallas_call(
        body, name="final_loss", grid=(n_steps,),
        in_specs=[row, vec, row],
        out_specs=[pl.BlockSpec((1, LANES), lambda i: (0, 0)), row, vec],
        out_shape=[jax.ShapeDtypeStruct((1, LANES), F32), jax.ShapeDtypeStruct((t_dim, d), F32),
                   jax.ShapeDtypeStruct((1, d), F32)],
        scratch_shapes=[pltpu.VMEM((1, d), F32)],
        compiler_params=_params(("arbitrary",)),
    )(h, g, target)


def _rope_table(pos, inv_lane, sel_a, sel_b, *, tt=512):
    t_dim = pos.shape[0]

    def body(p_ref, f_ref, a_ref, b_ref, c_ref, s_ref):
        ang = p_ref[...] * f_ref[...]
        on = (a_ref[...] + b_ref[...]) > 0.0
        c_ref[...] = jnp.where(on, jnp.cos(ang), 1.0)
        s_ref[...] = jnp.where(on, jnp.sin(ang), 0.0)

    vec = pl.BlockSpec((1, LANES), lambda i: (0, 0))
    row = pl.BlockSpec((tt, LANES), lambda i: (i, 0))
    shp = jax.ShapeDtypeStruct((t_dim, LANES), F32)
    return pl.pallas_call(
        body, name="rope_table", grid=(t_dim // tt,),
        in_specs=[pl.BlockSpec((tt, 1), lambda i: (i, 0)), vec, vec, vec],
        out_specs=[row, row], out_shape=[shp, shp],
        compiler_params=_params(("parallel",)),
    )(pos, inv_lane, sel_a, sel_b)


def _rotate(xv, cs, sn, sa, sb):
    half = ROPE_DIM // 2
    up = pltpu.roll(xv, LANES - half, 1)
    dn = pltpu.roll(xv, half, 1)
    return xv * cs + (dn * sb - up * sa) * sn


def _head_masks():
    h1 = lax.broadcasted_iota(jnp.int32, (1, LANES), 1) < HEAD_DIM
    return h1, jnp.logical_not(h1)


def _split_heads(xv, h1, h2):
    return jnp.where(h1, xv, 0.0).astype(BF16), jnp.where(h2, xv, 0.0).astype(BF16)


def _tri_masks():
    r = lax.broadcasted_iota(jnp.int32, (BLOCK, BLOCK), 0)
    c = lax.broadcasted_iota(jnp.int32, (BLOCK, BLOCK), 1)
    return c <= r, r <= c


def _stream_rows(start, dil):
    if dil == 1:
        return pl.ds(pl.multiple_of(start, BLOCK), BLOCK)
    return pl.ds(start, BLOCK, stride=dil)


def _dil_tile(idx, dil, nb):
    r = idx // nb
    n = idx % nb
    return (_stream_rows(r + dil * BLOCK * n, dil), _stream_rows(r + dil * BLOCK * jnp.maximum(n - 1, 0), dil),
            n > 0)


def _dil_specs(b_dim, s_dim):
    def col(c0):
        return pl.BlockSpec((None, s_dim, LANES),lambda b, h: (b, 0, c0 + h))
    tab = pl.BlockSpec((None, s_dim, LANES),lambda b, h: (b, 0, 0))
    vec = pl.BlockSpec((1, LANES), lambda b, h: (0, 0))
    return col, tab, vec


def _dil_fwd(proj3, cs3, sn3, sel_a, sel_b, rider=None):
    b_dim, s_dim, _ = proj3.shape
    scale = HEAD_DIM ** -0.5
    n_pat = len(DIL_PATTERNS)
    extra, extra_shapes, extra_sems = _rider_parts(rider)
    n_w = len(extra)
    n_steps = b_dim * PAIRS

    def body(*refs):
        q_ref, k_ref, v_ref, cs_ref, sn_ref, sa_ref, sb_ref = refs[:7]
        o16_ref, o32_ref, l_ref = refs[7 + n_w:10 + n_w]
        qr, kr = refs[10 + 2 * n_w:12 + 2 * n_w]
        per_pattern = refs[12 + 2 * n_w:12 + 2 * n_w + 2 * n_pat]
        og, lg = per_pattern[:n_pat], per_pattern[n_pat:]
        step = pl.program_id(0) * PAIRS + pl.program_id(1)
        begin, end = _rider_hooks(rider, refs[7:7 + n_w], refs[10 + n_w:10 + 2 * n_w], refs[-2:], step, n_steps)
        begin()
        h1, h2 = _head_masks()
        cur_ok, prev_ok = _tri_masks()
        sa, sb = sa_ref[...], sb_ref[...]

        def prep(j, _):
            rows = pl.ds(pl.multiple_of(j * BLOCK, BLOCK), BLOCK)
            cs, sn = cs_ref[rows, :], sn_ref[rows, :]
            qr[rows, :] = _rotate(q_ref[rows, :], cs, sn, sa, sb) * scale
            kr[rows, :] = _rotate(k_ref[rows, :], cs, sn, sa, sb)
            return 0

        lax.fori_loop(0, s_dim // BLOCK, prep, 0)

        for g, (_, dil) in enumerate(DIL_PATTERNS):
            nb = s_dim // dil // BLOCK

            def some(bi, _, g=g, dil=dil, nb=nb):
                tiles = [_dil_tile(bi * DIL_BATCH + t, dil, nb) for t in range(DIL_BATCH)]
                rows = [t[0] for t in tiles]
                q1, q2 = _split_heads(jnp.stack([qr[rw, :] for rw in rows]), h1, h2)
                kc = jnp.stack([kr[rw, :] for rw in rows]).astype(BF16)
                vc1, vc2 = _split_heads(jnp.stack([v_ref[rw, :] for rw in rows]), h1, h2)
                if nb > 1:
                    kp = jnp.stack([kr[t[1], :] for t in tiles]).astype(BF16)
                    vp1, vp2 = _split_heads(jnp.stack([v_ref[t[1], :] for t in tiles]), h1, h2)
                    p_ok = jnp.stack([jnp.logical_and(prev_ok, t[2]) for t in tiles])

                def head(qh, vch, vph):
                    sc = jnp.where(cur_ok, lax.dot_general(qh, kc, BNT, preferred_element_type=F32), -jnp.inf)
                    m = jnp.max(sc, axis=-1, keepdims=True)
                    if nb > 1:
                        sp = jnp.where(p_ok, lax.dot_general(qh, kp, BNT, preferred_element_type=F32), -jnp.inf)
                        m = jnp.maximum(m, jnp.max(sp, axis=-1, keepdims=True))
                    pc = jnp.exp(sc - m)
                    den = jnp.sum(pc, axis=-1, keepdims=True)
                    acc = lax.dot_general(pc.astype(BF16), vch, BNN, preferred_element_type=F32)
                    if nb > 1:
                        pp = jnp.exp(sp - m)
                        den = den + jnp.sum(pp, axis=-1, keepdims=True)
                        acc = acc + lax.dot_general(pp.astype(BF16), vph, BNN, preferred_element_type=F32)
                    return acc / den, m + jnp.log(den)

                o1, l1 = head(q1, vc1, vp1 if nb > 1 else None)
                o2, l2 = head(q2, vc2, vp2 if nb > 1 else None)
                o, l = o1 + o2, jnp.where(h1, l1, l2)
                for t, rw in enumerate(rows):
                    og[g][rw, :] = o[t]
                    lg[g][rw, :] = l[t]
                return 0

            lax.fori_loop(0, dil * nb // DIL_BATCH, some, 0)

        def comb(j, _):
            rows = pl.ds(pl.multiple_of(j * BLOCK, BLOCK), BLOCK)
            ls = [lg[g][rows, :] for g in range(n_pat)]
            m = jnp.maximum(jnp.maximum(ls[0], ls[1]), ls[2])
            es = [jnp.exp(l - m) for l in ls]
            den = es[0] + es[1] + es[2]
            o = (es[0] * og[0][rows, :] + es[1] * og[1][rows, :] + es[2] * og[2][rows, :]) / den
            o16_ref[rows, :] = o.astype(BF16)
            o32_ref[rows, :] = o
            l_ref[rows, :] = m + jnp.log(den)
            return 0

        lax.fori_loop(0, s_dim // BLOCK, comb, 0)
        end()

    col, tab, vec = _dil_specs(b_dim, s_dim)
    out = pl.BlockSpec((None, s_dim, LANES),lambda b, h: (b, 0, h))
    shp = (b_dim, s_dim, ATT_WIDTH)
    res = pl.pallas_call(
        body, name="dil_fwd", grid=(b_dim, PAIRS),
        in_specs=[col(COL_QA), col(COL_KA), col(COL_VA), tab, tab, vec, vec] + [ANY] * n_w,
        out_specs=[out, out, out] + [ANY] * n_w,
        out_shape=[jax.ShapeDtypeStruct(shp, BF16), jax.ShapeDtypeStruct(shp, F32), jax.ShapeDtypeStruct(shp, F32)]
        + extra_shapes,
        scratch_shapes=[pltpu.VMEM((s_dim, LANES), F32)] * (2 + 2 * n_pat) + extra_sems,
        compiler_params=_params(("arbitrary", "arbitrary")),
    )(proj3, proj3, proj3, cs3, sn3, sel_a, sel_b, *extra)
    return res[:3], res[3:]


def _dil_bwd(proj3, cs3, sn3, sel_a, sel_b, do3, o3, lse3, rider=None):
    b_dim, s_dim, _ = proj3.shape
    scale = HEAD_DIM ** -0.5
    extra, extra_shapes, extra_sems = _rider_parts(rider)
    n_w = len(extra)

    def body(*refs):
        q_ref, k_ref, v_ref, cs_ref, sn_ref, sa_ref, sb_ref, do_ref, o_ref, l_ref = refs[:10]
        dq_ref, dk_ref, dv_ref = refs[10 + n_w:13 + n_w]
        qr, kr, dqa, dka, dva = refs[13 + 2 * n_w:18 + 2 * n_w]
        step = pl.program_id(0) * PAIRS + pl.program_id(1)
        begin, end = _rider_hooks(rider, refs[10:10 + n_w], refs[13 + n_w:13 + 2 * n_w], refs[-2:], step,
                                  b_dim * PAIRS)
        begin()
        h1, h2 = _head_masks()
        cur_ok, prev_ok = _tri_masks()
        sa, sb = sa_ref[...], sb_ref[...]

        def prep(j, _):
            rows = pl.ds(pl.multiple_of(j * BLOCK, BLOCK), BLOCK)
            cs, sn = cs_ref[rows, :], sn_ref[rows, :]
            qr[rows, :] = _rotate(q_ref[rows, :], cs, sn, sa, sb) * scale
            kr[rows, :] = _rotate(k_ref[rows, :], cs, sn, sa, sb)
            zero = jnp.zeros((BLOCK, LANES), F32)
            dqa[rows, :] = zero
            dka[rows, :] = zero
            dva[rows, :] = zero
            return 0

        lax.fori_loop(0, s_dim // BLOCK, prep, 0)

        for _, dil in DIL_PATTERNS:
            nb = s_dim // dil // BLOCK

            def some(bi, _, dil=dil, nb=nb):
                tiles = [_dil_tile(bi * DIL_BATCH + t, dil, nb) for t in range(DIL_BATCH)]
                rows = [t[0] for t in tiles]
                q1, q2 = _split_heads(jnp.stack([qr[rw, :] for rw in rows]), h1, h2)
                dof = jnp.stack([do_ref[rw, :] for rw in rows])
                do1, do2 = _split_heads(dof, h1, h2)
                prod = dof * jnp.stack([o_ref[rw, :] for rw in rows])
                delta1 = jnp.sum(jnp.where(h1, prod, 0.0), axis=-1, keepdims=True)
                delta2 = jnp.sum(jnp.where(h2, prod, 0.0), axis=-1, keepdims=True)
                lt = jnp.stack([l_ref[rw, :] for rw in rows])
                lse1 = jnp.max(jnp.where(h1, lt, -jnp.inf), axis=-1, keepdims=True)
                lse2 = jnp.max(jnp.where(h2, lt, -jnp.inf), axis=-1, keepdims=True)

                def side(krows, ok):
                    kf = jnp.stack([kr[kw, :] for kw in krows])
                    k16 = kf.astype(BF16)
                    k1, k2 = _split_heads(kf, h1, h2)
                    v16 = jnp.stack([v_ref[kw, :] for kw in krows]).astype(BF16)

                    def head(qh, doh, lse, delta):
                        sc = lax.dot_general(qh, k16, BNT, preferred_element_type=F32)
                        p = jnp.where(ok, jnp.exp(sc - lse), 0.0)
                        dp = lax.dot_general(doh, v16, BNT, preferred_element_type=F32)
                        return p.astype(BF16), (p * (dp - delta)).astype(BF16)

                    p1, ds1 = head(q1, do1, lse1, delta1)
                    p2, ds2 = head(q2, do2, lse2, delta2)
                    dv = (lax.dot_general(p1, do1, BTN, preferred_element_type=F32)
                          + lax.dot_general(p2, do2, BTN, preferred_element_type=F32))
                    dk = (lax.dot_general(ds1, q1, BTN, preferred_element_type=F32)
                          + lax.dot_general(ds2, q2, BTN, preferred_element_type=F32))
                    for t, kw in enumerate(krows):
                        dva[kw, :] += dv[t]
                        dka[kw, :] += dk[t]
                    return (lax.dot_general(ds1, k1, BNN, preferred_element_type=F32)
                            + lax.dot_general(ds2, k2, BNN, preferred_element_type=F32))

                dq = side(rows, cur_ok)
                if nb > 1:
                    dq = dq + side([t[1] for t in tiles], jnp.stack([jnp.logical_and(prev_ok, t[2]) for t in tiles]))
                for t, rw in enumerate(rows):
                    dqa[rw, :] += dq[t] * scale
                return 0

            lax.fori_loop(0, dil * nb // DIL_BATCH, some, 0)

        def finish(j, _):
            rows = pl.ds(pl.multiple_of(j * BLOCK, BLOCK), BLOCK)
            cs, sn = cs_ref[rows, :], -sn_ref[rows, :]
            dq_ref[rows, :] = _rotate(dqa[rows, :], cs, sn, sa, sb).astype(BF16)
            dk_ref[rows, :] = _rotate(dka[rows, :], cs, sn, sa, sb).astype(BF16)
            dv_ref[rows, :] = dva[rows, :].astype(BF16)
            return 0

        lax.fori_loop(0, s_dim // BLOCK, finish, 0)
        end()

    col, tab, vec = _dil_specs(b_dim, s_dim)
    out = pl.BlockSpec((None, s_dim, LANES),lambda b, h: (b, 0, h))
    shp = jax.ShapeDtypeStruct((b_dim, s_dim, ATT_WIDTH), BF16)
    acc = pltpu.VMEM((s_dim, LANES), F32)
    res = pl.pallas_call(
        body, name="dil_bwd", grid=(b_dim, PAIRS),
        in_specs=[col(COL_QA), col(COL_KA), col(COL_VA), tab, tab, vec, vec, out, out, out] + [ANY] * n_w,
        out_specs=[out, out, out] + [ANY] * n_w, out_shape=[shp, shp, shp] + extra_shapes,
        scratch_shapes=[acc, acc, acc, acc, acc] + extra_sems,
        compiler_params=_params(("arbitrary", "arbitrary")),
    )(proj3, proj3, proj3, cs3, sn3, sel_a, sel_b, do3, o3, lse3, *extra)
    return res[:3], res[3:]


def _split_dot(x, tri):
    hi = x.astype(BF16)
    lo = (x - hi.astype(F32)).astype(BF16)
    return jnp.dot(hi, tri, preferred_element_type=F32) + jnp.dot(lo, tri, preferred_element_type=F32)


def _log_sigmoid(z):
    return jnp.minimum(z, 0.0) - jnp.log(1.0 + jnp.exp(-jnp.abs(z)))


def _sb_scores(qh, k16, valid):
    z = lax.dot_general(qh, k16, NT, preferred_element_type=F32)
    ls = _log_sigmoid(z)
    l1m = ls - z
    return ls, (l1m if valid is None else jnp.where(valid, l1m, 0.0))


def _sb_consts():
    r = lax.broadcasted_iota(jnp.int32, (BLOCK, BLOCK), 0)
    c = lax.broadcasted_iota(jnp.int32, (BLOCK, BLOCK), 1)
    after = (r > c).astype(BF16)
    before = (r < c).astype(BF16)
    qrow = lax.broadcasted_iota(jnp.int32, (SB_ROWS, BLOCK), 0)
    kcol = lax.broadcasted_iota(jnp.int32, (SB_ROWS, BLOCK), 1)
    return after, before, qrow, kcol


def _below(whole, lo, delta):
    if lo == 0:
        return whole + delta
    return whole + jnp.concatenate([jnp.zeros((lo,) + delta.shape[1:], delta.dtype), delta], axis=0)


def _pairs_loop(n_blocks, step, carry):
    def several(i, c):
        for j in range(SB_STEP):
            c = step(SB_STEP * i + j, c)
        return c

    return lax.fori_loop(0, n_blocks // SB_STEP, several, carry)


def _sb_fwd(proj3, rider=None):
    b_dim, s_dim, _ = proj3.shape
    scale = HEAD_DIM ** -0.5
    per = SB_ROWS // BLOCK
    extra, extra_shapes, extra_sems = _rider_parts(rider)
    n_w = len(extra)

    def body(*refs):
        q_ref, k_ref, v_ref = refs[:3]
        o_ref = refs[3 + n_w]
        step = pl.program_id(0) * PAIRS + pl.program_id(1)
        begin, end = _rider_hooks(rider, refs[3:3 + n_w], refs[4 + n_w:4 + 2 * n_w], refs[-2:], step, b_dim * PAIRS)
        begin()
        h1, h2 = _head_masks()
        after, _, qrow, kcol = _sb_consts()

        def qloop(qi, _):
            rows = pl.ds(pl.multiple_of(qi * SB_ROWS, SB_ROWS), SB_ROWS)
            q1, q2 = _split_heads(q_ref[rows, :] * scale, h1, h2)
            first = qi * per

            def block(kb, carry, lo):
                acc, run1, run2 = carry
                krows = pl.ds(pl.multiple_of(kb * BLOCK, BLOCK), BLOCK)
                k16 = k_ref[krows, :].astype(BF16)
                v1, v2 = _split_heads(v_ref[krows, :], h1, h2)
                valid = None if lo is None else kcol[:SB_ROWS - lo] < qrow[:SB_ROWS - lo]
                lo = lo or 0

                def head(qh, vh, run):
                    ls, l1m = _sb_scores(qh[lo:], k16, valid)
                    a = jnp.exp(ls + _split_dot(l1m, after) + run[lo:])
                    if valid is not None:
                        a = jnp.where(valid, a, 0.0)
                    return (jnp.dot(a.astype(BF16), vh, preferred_element_type=F32),
                            _below(run, lo, jnp.sum(l1m, axis=-1, keepdims=True)))

                o1, run1 = head(q1, v1, run1)
                o2, run2 = head(q2, v2, run2)
                return _below(acc, lo, o1 + o2), run1, run2

            zcol = jnp.zeros((SB_ROWS, 1), F32)
            carry = (jnp.zeros((SB_ROWS, LANES), F32), zcol, zcol)
            for kl in reversed(range(per)):
                carry = block(first + kl, carry, kl * BLOCK)
            acc, _, _ = _pairs_loop(first, lambda i, c: block(first - 1 - i, c, None), carry)
            o_ref[rows, :] = acc.astype(BF16)
            return 0

        lax.fori_loop(0, s_dim // SB_ROWS, qloop, 0)
        end()

    def col(c0):
        return pl.BlockSpec((None, s_dim, LANES),lambda b, h: (b, 0, c0 + h))

    res = pl.pallas_call(
        body, name="sb_fwd", grid=(b_dim, PAIRS),
        in_specs=[col(COL_QB), col(COL_KB), col(COL_VB)] + [ANY] * n_w, out_specs=[col(0)] + [ANY] * n_w,
        out_shape=[jax.ShapeDtypeStruct((b_dim, s_dim, ATT_WIDTH), BF16)] + extra_shapes,
        scratch_shapes=extra_sems,
        compiler_params=_params(("arbitrary", "arbitrary")),
    )(proj3, proj3, proj3, *extra)
    return res[0], res[1:]


def _sb_bwd(proj3, do3, rider=None):
    b_dim, s_dim, _ = proj3.shape
    scale = HEAD_DIM ** -0.5
    per = SB_ROWS // BLOCK
    nkb_max = s_dim // BLOCK
    extra, extra_shapes, extra_sems = _rider_parts(rider)
    n_w = len(extra)

    def body(*refs):
        q_ref, k_ref, v_ref, do_ref = refs[:4]
        dq_ref, dk_ref, dv_ref = refs[4 + n_w:7 + n_w]
        dka, dva, e_ref, sg_ref = refs[7 + 2 * n_w:11 + 2 * n_w]
        step = pl.program_id(0) * PAIRS + pl.program_id(1)
        begin, end = _rider_hooks(rider, refs[4:4 + n_w], refs[7 + n_w:7 + 2 * n_w], refs[-2:], step, b_dim * PAIRS)
        begin()
        h1, h2 = _head_masks()
        after, before, qrow, kcol = _sb_consts()
        dka[...] = jnp.zeros_like(dka)
        dva[...] = jnp.zeros_like(dva)

        def qloop(qi, _):
            rows = pl.ds(pl.multiple_of(qi * SB_ROWS, SB_ROWS), SB_ROWS)
            q1, q2 = _split_heads(q_ref[rows, :] * scale, h1, h2)
            do1, do2 = _split_heads(do_ref[rows, :].astype(F32), h1, h2)
            first = qi * per

            def pass1(kb, carry, lo):
                run1, run2 = carry
                krows = pl.ds(pl.multiple_of(kb * BLOCK, BLOCK), BLOCK)
                k16 = k_ref[krows, :].astype(BF16)
                v16 = v_ref[krows, :].astype(BF16)
                valid = None if lo is None else kcol[:SB_ROWS - lo] < qrow[:SB_ROWS - lo]
                lo = lo or 0
                part = pl.ds(lo, SB_ROWS - lo)

                def head(h, qh, doh, run):
                    ls, l1m = _sb_scores(qh[lo:], k16, valid)
                    a = jnp.exp(ls + _split_dot(l1m, after) + run[lo:])
                    if valid is not None:
                        a = jnp.where(valid, a, 0.0)
                    da = lax.dot_general(doh[lo:], v16, NT, preferred_element_type=F32)
                    e_ref[h, kb, part, :] = a * da
                    sg_ref[h, kb, part, :] = jnp.exp(ls)
                    return a.astype(BF16), _below(run, lo, jnp.sum(l1m, axis=-1, keepdims=True))

                a1, run1 = head(0, q1, do1, run1)
                a2, run2 = head(1, q2, do2, run2)
                dva[krows, :] += (lax.dot_general(a1, do1[lo:], TN, preferred_element_type=F32)
                                  + lax.dot_general(a2, do2[lo:], TN, preferred_element_type=F32))
                return run1, run2

            zcol = jnp.zeros((SB_ROWS, 1), F32)
            carry = (zcol, zcol)
            for kl in reversed(range(per)):
                carry = pass1(first + kl, carry, kl * BLOCK)
            _pairs_loop(first, lambda i, c: pass1(first - 1 - i, c, None), carry)

            def pass2(kb, carry, lo):
                dq, pre1, pre2 = carry
                krows = pl.ds(pl.multiple_of(kb * BLOCK, BLOCK), BLOCK)
                k1, k2 = _split_heads(k_ref[krows, :], h1, h2)
                valid = None if lo is None else kcol[:SB_ROWS - lo] < qrow[:SB_ROWS - lo]
                lo = lo or 0
                part = pl.ds(lo, SB_ROWS - lo)

                def head(h, pre):
                    ev = e_ref[h, kb, part, :]
                    sg = sg_ref[h, kb, part, :]
                    dz = ev * (1.0 - sg) - (_split_dot(ev, before) + pre[lo:]) * sg
                    if valid is not None:
                        dz = jnp.where(valid, dz, 0.0)
                    return dz.astype(BF16), _below(pre, lo, jnp.sum(ev, axis=-1, keepdims=True))

                dz1, pre1 = head(0, pre1)
                dz2, pre2 = head(1, pre2)
                dka[krows, :] += (lax.dot_general(dz1, q1[lo:], TN, preferred_element_type=F32)
                                  + lax.dot_general(dz2, q2[lo:], TN, preferred_element_type=F32))
                dq = _below(dq, lo, jnp.dot(dz1, k1, preferred_element_type=F32)
                            + jnp.dot(dz2, k2, preferred_element_type=F32))
                return dq, pre1, pre2

            carry = _pairs_loop(first, lambda i, c: pass2(i, c, None), (jnp.zeros((SB_ROWS, LANES), F32), zcol, zcol))
            for kl in range(per):
                carry = pass2(first + kl, carry, kl * BLOCK)
            dq = carry[0]
            dq_ref[rows, :] = (dq * scale).astype(BF16)
            return 0

        lax.fori_loop(0, s_dim // SB_ROWS, qloop, 0)
        dk_ref[...] = dka[...].astype(BF16)
        dv_ref[...] = dva[...].astype(BF16)
        end()

    def col(c0):
        return pl.BlockSpec((None, s_dim, LANES),lambda b, h: (b, 0, c0 + h))

    shp = jax.ShapeDtypeStruct((b_dim, s_dim, ATT_WIDTH), BF16)
    acc = pltpu.VMEM((s_dim, LANES), F32)
    strip = pltpu.VMEM((2, nkb_max, SB_ROWS, BLOCK), F32)
    res = pl.pallas_call(
        body, name="sb_bwd", grid=(b_dim, PAIRS),
        in_specs=[col(COL_QB), col(COL_KB), col(COL_VB), col(0)] + [ANY] * n_w,
        out_specs=[col(0), col(0), col(0)] + [ANY] * n_w,
        out_shape=[shp, shp, shp] + extra_shapes,
        scratch_shapes=[acc, acc, strip, strip] + extra_sems,
        compiler_params=_params(("arbitrary", "arbitrary")),
    )(proj3, proj3, proj3, do3, *extra)
    return res[:3], res[3:]


def _sigmoid(x):
    return 1.0 / (1.0 + jnp.exp(-x))


def _gate_fwd(proj, ua, ub, *, tt=512):
    t_dim, d = ua.shape

    def body(ga_ref, gb_ref, ua_ref, ub_ref, o_ref):
        o_ref[...] = (_sigmoid(ga_ref[...]) * ua_ref[...] + _sigmoid(gb_ref[...]) * ub_ref[...]).astype(BF16)

    row = pl.BlockSpec((tt, d), lambda i: (i, 0))
    return pl.pallas_call(
        body, name="gate_fwd", grid=(t_dim // tt,),
        in_specs=[pl.BlockSpec((tt, d), lambda i: (i, 3)), pl.BlockSpec((tt, d), lambda i: (i, 4)), row, row],
        out_specs=row, out_shape=jax.ShapeDtypeStruct((t_dim, d), BF16),
        compiler_params=_params(("parallel",)),
    )(proj, proj, ua, ub)


def _gate_bwd(proj, ua, ub, dmix, *, tt=512):
    t_dim, d = ua.shape

    def body(ga_ref, gb_ref, ua_ref, ub_ref, dm_ref, dua_ref, dub_ref, dg_ref):
        dm = dm_ref[...]
        sa = _sigmoid(ga_ref[...])
        sb = _sigmoid(gb_ref[...])
        dua_ref[...] = (dm * sa).astype(BF16)
        dub_ref[...] = (dm * sb).astype(BF16)
        dg_ref[:, :d] = (dm * ua_ref[...] * (sa * (1.0 - sa))).astype(BF16)
        dg_ref[:, d:] = (dm * ub_ref[...] * (sb * (1.0 - sb))).astype(BF16)

    row = pl.BlockSpec((tt, d), lambda i: (i, 0))
    wide = pl.BlockSpec((tt, 2 * d), lambda i: (i, 0))
    return pl.pallas_call(
        body, name="gate_bwd", grid=(t_dim // tt,),
        in_specs=[pl.BlockSpec((tt, d), lambda i: (i, 3)), pl.BlockSpec((tt, d), lambda i: (i, 4)), row, row, row],
        out_specs=[row, row, wide],
        out_shape=[jax.ShapeDtypeStruct((t_dim, d), BF16), jax.ShapeDtypeStruct((t_dim, d), BF16),
                   jax.ShapeDtypeStruct((t_dim, 2 * d), BF16)],
        compiler_params=_params(("parallel",)),
    )(proj, proj, ua, ub, dmix)


def _ffn_up_swiglu(n, wg3, wu3, *, tt=1024):
    t_dim, d = n.shape
    n_s, _, f4 = wg3.shape

    def body(n_ref, wg_ref, wu_ref, g_ref, u_ref, a_ref):
        nv = n_ref[...]
        gv = jnp.dot(nv, wg_ref[...], preferred_element_type=F32)
        uv = jnp.dot(nv, wu_ref[...], preferred_element_type=F32)
        g_ref[...] = gv
        u_ref[...] = uv
        a_ref[...] = (gv * _sigmoid(gv) * uv).astype(BF16)

    wspec = pl.BlockSpec((None, d, f4), lambda i, s: (s, 0, 0))
    ospec = pl.BlockSpec((None, tt, f4), lambda i, s: (s, i, 0))
    shp = (n_s, t_dim, f4)
    return pl.pallas_call(
        body, name="ffn_up_swiglu", grid=(t_dim // tt, n_s),
        in_specs=[pl.BlockSpec((tt, d), lambda i, s: (i, 0)), wspec, wspec], out_specs=[ospec, ospec, ospec],
        out_shape=[jax.ShapeDtypeStruct(shp, F32), jax.ShapeDtypeStruct(shp, F32), jax.ShapeDtypeStruct(shp, BF16)],
        compiler_params=_params(("parallel", "parallel")),
    )(n, wg3, wu3)


def _ffn_down_dx_swiglu(dh, wd3, g3, u3, *, tt=1024):
    t_dim, d = dh.shape
    n_s, f4, _ = wd3.shape

    def body(dh_ref, w_ref, g_ref, u_ref, dg_ref, du_ref):
        da = lax.dot_general(dh_ref[...].astype(BF16), w_ref[...], NT, preferred_element_type=F32)
        gv = g_ref[...]
        sg = _sigmoid(gv)
        dg_ref[...] = (da * u_ref[...] * (sg + gv * sg * (1.0 - sg))).astype(BF16)
        du_ref[...] = (da * (gv * sg)).astype(BF16)

    spec = pl.BlockSpec((None, tt, f4), lambda i, s: (s, i, 0))
    shp = jax.ShapeDtypeStruct((n_s, t_dim, f4), BF16)
    return pl.pallas_call(
        body, name="ffn_down_dx_swiglu", grid=(t_dim // tt, n_s),
        in_specs=[pl.BlockSpec((tt, d), lambda i, s: (i, 0)), pl.BlockSpec((None, f4, d), lambda i, s: (s, 0, 0)),
                  spec, spec],
        out_specs=[spec, spec], out_shape=[shp, shp],
        compiler_params=_params(("parallel", "parallel")),
    )(dh, wd3, g3, u3)


def _mem_fwd(qm, kvm, *, tt=512):
    b_dim, s_dim, _ = qm.shape
    n_mem = kvm.shape[1]
    scale = MEM_HEAD_DIM ** -0.5

    def body(q_ref, k_ref, v_ref, o_ref):
        sc = lax.dot_general(q_ref[0], k_ref[0], NT, preferred_element_type=F32) * scale
        p = jnp.exp(sc - jnp.max(sc, axis=-1, keepdims=True))
        p = p / jnp.sum(p, axis=-1, keepdims=True)
        o_ref[0] = jnp.dot(p.astype(BF16), v_ref[0], preferred_element_type=F32).astype(BF16)

    qs = pl.BlockSpec((1, tt, MEM_HEAD_DIM), lambda b, h, i: (b, i, h))
    return pl.pallas_call(
        body, name="mem_fwd", grid=(b_dim, N_HEADS_MEM, s_dim // tt),
        in_specs=[qs, pl.BlockSpec((1, n_mem, MEM_HEAD_DIM), lambda b, h, i: (b, 0, h)),
                  pl.BlockSpec((1, n_mem, MEM_HEAD_DIM), lambda b, h, i: (b, 0, N_HEADS_MEM + h))],
        out_specs=qs, out_shape=jax.ShapeDtypeStruct(qm.shape, BF16),
        compiler_params=_params(("parallel", "parallel", "parallel")),
    )(qm, kvm, kvm)


def _mem_bwd(qm, kvm, dom, *, tt=512):
    b_dim, s_dim, _ = qm.shape
    n_mem = kvm.shape[1]
    scale = MEM_HEAD_DIM ** -0.5

    def body(q_ref, k_ref, v_ref, do_ref, dq_ref, dk_ref, dv_ref):
        qv, kv, vv, dov = q_ref[0], k_ref[0], v_ref[0], do_ref[0]
        sc = lax.dot_general(qv, kv, NT, preferred_element_type=F32) * scale
        p = jnp.exp(sc - jnp.max(sc, axis=-1, keepdims=True))
        p = p / jnp.sum(p, axis=-1, keepdims=True)
        dp = lax.dot_general(dov, vv, NT, preferred_element_type=F32)
        ds = (p * (dp - jnp.sum(p * dp, axis=-1, keepdims=True)) * scale).astype(BF16)
        dq_ref[0] = jnp.dot(ds, kv, preferred_element_type=F32).astype(BF16)

        @pl.when(pl.program_id(2) == 0)
        def _():
            dk_ref[...] = jnp.zeros_like(dk_ref)
            dv_ref[...] = jnp.zeros_like(dv_ref)

        dk_ref[0] += lax.dot_general(ds, qv, TN, preferred_element_type=F32)
        dv_ref[0] += lax.dot_general(p.astype(BF16), dov, TN, preferred_element_type=F32)

    qs = pl.BlockSpec((1, tt, MEM_HEAD_DIM), lambda b, h, i: (b, i, h))
    ks = pl.BlockSpec((1, n_mem, MEM_HEAD_DIM), lambda b, h, i: (b, 0, h))
    vs = pl.BlockSpec((1, n_mem, MEM_HEAD_DIM), lambda b, h, i: (b, 0, N_HEADS_MEM + h))
    return pl.pallas_call(
        body, name="mem_bwd", grid=(b_dim, N_HEADS_MEM, s_dim // tt),
        in_specs=[qs, ks, vs, qs], out_specs=[qs, ks, ks],
        out_shape=[jax.ShapeDtypeStruct(qm.shape, BF16), jax.ShapeDtypeStruct((b_dim, n_mem, MEM_WIDTH), F32),
                   jax.ShapeDtypeStruct((b_dim, n_mem, MEM_WIDTH), F32)],
        compiler_params=_params(("parallel", "parallel", "arbitrary")),
    )(qm, kvm, kvm, dom)


def _adamw_math(wv, gv, mv, vv):
    nm = ADAM_B1 * mv + (1.0 - ADAM_B1) * gv
    nv = ADAM_B2 * vv + (1.0 - ADAM_B2) * (gv * gv)
    m_hat = nm / (1.0 - ADAM_B1 ** ADAM_STEP)
    v_hat = nv / (1.0 - ADAM_B2 ** ADAM_STEP)
    return -ADAM_LR * (m_hat / (jnp.sqrt(v_hat) + ADAM_EPS) + ADAM_WD * wv), nm, nv


def _adamw(w, g, m, v, *, name):
    rows, cols = w.shape
    tr = _tile(rows, 256, 8)

    def body(w_ref, g_ref, m_ref, v_ref, d_ref, nm_ref, nv_ref):
        d_ref[...], nm_ref[...], nv_ref[...] = _adamw_math(w_ref[...], g_ref[...], m_ref[...], v_ref[...])

    spec = pl.BlockSpec((tr, cols), lambda i: (i, 0))
    shp = jax.ShapeDtypeStruct((rows, cols), F32)
    return pl.pallas_call(
        body, name=name, grid=(rows // tr,),
        in_specs=[spec] * 4, out_specs=[spec] * 3, out_shape=[shp] * 3,
        compiler_params=_params(("parallel",)),
    )(w, g, m, v)


def _prefetch_spec(grid, in_specs, out_specs):
    return pltpu.PrefetchScalarGridSpec(num_scalar_prefetch=1, grid=grid, in_specs=in_specs, out_specs=out_specs)


def _adamw_halves(w, mine, theirs, m, v, c_idx, *, name):
    rows, cols = w.shape
    half = rows // 2
    tr = _tile(half, _row_cap(cols), 8)
    nh = half // tr

    def body(c_ref, w_ref, mine_ref, theirs_ref, m_ref, v_ref, g_ref, d_ref, nm_ref, nv_ref):
        gv = jnp.where(pl.program_id(0) == c_ref[0], mine_ref[...], theirs_ref[...])
        g_ref[...] = gv
        d_ref[...], nm_ref[...], nv_ref[...] = _adamw_math(w_ref[...], gv, m_ref[...], v_ref[...])

    full = pl.BlockSpec((tr, cols), lambda h, i, c_ref: (h * nh + i, 0))
    part = pl.BlockSpec((tr, cols), lambda h, i, c_ref: (i, 0))
    shp = jax.ShapeDtypeStruct((rows, cols), F32)
    return pl.pallas_call(
        body, name=name, grid_spec=_prefetch_spec((2, nh), [full, part, part, full, full], [full] * 4),
        out_shape=[shp] * 4,
        compiler_params=_params(("parallel", "parallel")),
    )(c_idx, w, mine, theirs, m, v)


def _pair_sum(g3, theirs, c_idx, *, name):
    n, rows, cols = g3.shape
    half = rows // 2
    tr = _tile(half, _row_cap(cols), 16)

    def body(c_ref, g_ref, t_ref, o_ref):
        o_ref[...] = (g_ref[...] + t_ref[...]).astype(BF16)

    part = pl.BlockSpec((None, tr, cols), lambda s, i, c_ref: (s, i, 0))
    return pl.pallas_call(
        body, name=name,
        grid_spec=_prefetch_spec((n, half // tr),
                                 [pl.BlockSpec((None, None, tr, cols), lambda s, i, c_ref: (s, c_ref[0], i, 0)), part],
                                 part),
        out_shape=jax.ShapeDtypeStruct((n, half, cols), BF16),
        compiler_params=_params(("parallel", "parallel")),
    )(c_idx, g3.reshape(n, 2, half, cols), theirs)


def _chip_sum(pair, recv, s_idx, *, name):
    _, half, cols = pair.shape
    tr = _tile(half, _row_cap(cols), 16)

    def body(s_ref, p_ref, r_ref, o_ref):
        o_ref[...] = ((p_ref[...].astype(F32) + r_ref[0].astype(F32)) + r_ref[1].astype(F32)) + r_ref[2].astype(F32)

    return pl.pallas_call(
        body, name=name,
        grid_spec=_prefetch_spec((half // tr,),
                                 [pl.BlockSpec((None, tr, cols), lambda i, s_ref: (s_ref[0], i, 0)),
                                  pl.BlockSpec((N_CHIPS - 1, tr, cols), lambda i, s_ref: (0, i, 0))],
                                 pl.BlockSpec((tr, cols), lambda i, s_ref: (i, 0))),
        out_shape=jax.ShapeDtypeStruct((half, cols), F32),
        compiler_params=_params(("parallel",)),
    )(s_idx, pair, recv)


def _sum8(parts):
    n, rows, cols = parts.shape

    def body(p_ref, o_ref):
        acc = p_ref[0]
        for i in range(1, n):
            acc = acc + p_ref[i]
        o_ref[...] = acc

    return pl.pallas_call(
        body, name="small_sum", grid=(1,),
        in_specs=[pl.BlockSpec((n, rows, cols), lambda i: (0, 0, 0))],
        out_specs=pl.BlockSpec((rows, cols), lambda i: (0, 0)),
        out_shape=jax.ShapeDtypeStruct((rows, cols), parts.dtype),
        compiler_params=_params(("arbitrary",)),
    )(parts)


def _place():
    return lax.axis_index("x"), lax.axis_index("y"), lax.axis_index("c")


ANY = pl.BlockSpec(memory_space=pl.ANY)


def _rider_parts(rider):
    if rider is None:
        return (), [], []
    kind, arrays = rider
    n = len(arrays)
    shapes = {"gather": _gathered_shapes, "pair": _pair_shapes, "chip": _chip_shapes}[kind](arrays)
    sems = _gather_sems(n) if kind == "gather" else _exchange_sems(n if kind == "pair" else 3 * n)
    return tuple(arrays), shapes, sems


def _rider_hooks(rider, ins, outs, sems, step, n_steps):
    if rider is None:
        return (lambda: None), (lambda: None)
    if rider[0] == "gather":
        start, forward, finish = _gather_steps(ins, outs, *sems)
    else:
        start, finish = {"pair": _pair_steps, "chip": _chip_steps}[rider[0]](ins, outs, *sems)
        forward = None

    def begin():
        pl.when(step == 0)(start)

    def end():
        if forward is not None:
            pl.when(step == n_steps - 2)(forward)
        pl.when(step == n_steps - 1)(finish)

    return begin, end


def _gather_weights(shards):
    n = len(shards)

    def body(*refs):
        start, forward, finish = _gather_steps(refs[:n], refs[n:2 * n], refs[2 * n], refs[2 * n + 1])
        start()
        forward()
        finish()

    return pl.pallas_call(
        body, name="gather_weights", out_shape=_gathered_shapes(shards),
        in_specs=[ANY] * n, out_specs=[ANY] * n, scratch_shapes=_gather_sems(n),
    )(*shards)


def _gathered_shapes(shards):
    return [jax.ShapeDtypeStruct((N_CHIPS,) + s.shape, s.dtype) for s in shards]


def _gather_sems(n):
    return [pltpu.SemaphoreType.DMA((7 * n,)), pltpu.SemaphoreType.DMA((7 * n,))]


def _gather_steps(ins, outs, send_sems, recv_sems):
    n = len(ins)
    halves = [r.shape[0] // 2 for r in ins]
    x, y, c = _place()
    my_chip = 2 * x + y
    me, sibling = (x, y, c), (x, y, 1 - c)
    chips = [(1 - x, y), (x, 1 - y), (1 - x, 1 - y)]

    def half_of(w, chip, pc):
        return outs[w].at[chip, pl.ds(pc * halves[w], halves[w]), :]

    def copy(w, k, src, dst, to):
        return pltpu.make_async_remote_copy(
            src_ref=src, dst_ref=dst, send_sem=send_sems.at[7 * w + k], recv_sem=recv_sems.at[7 * w + k],
            device_id=to, device_id_type=MESH)

    def firsts():
        cps = []
        for w in range(n):
            cps.append(copy(w, 0, ins[w], outs[w].at[my_chip], sibling))
            mine = ins[w].at[pl.ds(c * halves[w], halves[w]), :]
            for j, (px, py) in enumerate(chips):
                cps.append(copy(w, 1 + j, mine, half_of(w, my_chip, c), (px, py, c)))
        return cps

    def passes():
        return [copy(w, 4 + j, half_of(w, 2 * px + py, c), half_of(w, 2 * px + py, c), sibling)
                for w in range(n) for j, (px, py) in enumerate(chips)]

    def start():
        for cp in firsts():
            cp.start()

    def forward():
        fws = passes()
        for w in range(n):
            for j, (px, py) in enumerate(chips):
                landed = half_of(w, 2 * px + py, c)
                copy(w, 1 + j, landed, landed, me).wait_recv()
                fws[3 * w + j].start()

    def finish():
        for w in range(n):
            copy(w, 0, ins[w], outs[w].at[my_chip], me).wait_recv()
            for j, (px, py) in enumerate(chips):
                landed = half_of(w, 2 * px + py, 1 - c)
                copy(w, 4 + j, landed, landed, me).wait_recv()
        for cp in firsts() + passes():
            cp.wait_send()

    return start, forward, finish


def _pair_shapes(grads):
    return [jax.ShapeDtypeStruct((g.shape[0], g.shape[1] // 2, g.shape[2]), g.dtype) for g in grads]


def _exchange_sems(n):
    return [pltpu.SemaphoreType.DMA((n,)), pltpu.SemaphoreType.DMA((n,))]


def _exchange_steps(copies):
    def start():
        for cp in copies():
            cp.start()

    def finish():
        for cp in copies():
            cp.wait()

    return start, finish


def _pair_steps(ins, outs, send_sems, recv_sems):
    x, y, c = _place()

    def copies():
        return [pltpu.make_async_remote_copy(
            src_ref=ins[w].at[:, pl.ds((1 - c) * (ins[w].shape[1] // 2), ins[w].shape[1] // 2), :], dst_ref=outs[w],
            send_sem=send_sems.at[w], recv_sem=recv_sems.at[w], device_id=(x, y, 1 - c), device_id_type=MESH)
            for w in range(len(ins))]

    return _exchange_steps(copies)


def _chip_shapes(pairs):
    return [jax.ShapeDtypeStruct((N_CHIPS - 1,) + p.shape[1:], p.dtype) for p in pairs]


def _chip_steps(ins, outs, send_sems, recv_sems):
    x, y, c = _place()
    others = [(1 - x, y), (x, 1 - y), (1 - x, 1 - y)]

    def copies():
        return [pltpu.make_async_remote_copy(
            src_ref=ins[w].at[2 * px + py], dst_ref=outs[w].at[j],
            send_sem=send_sems.at[3 * w + j], recv_sem=recv_sems.at[3 * w + j],
            device_id=(px, py, c), device_id_type=MESH)
            for w in range(len(ins)) for j, (px, py) in enumerate(others)]

    return _exchange_steps(copies)


def _swap_halves(mine):
    n = len(mine)

    def body(*refs):
        ins, outs, send_sems, recv_sems = refs[:n], refs[n:2 * n], refs[2 * n], refs[2 * n + 1]
        x, y, c = _place()
        copies = [pltpu.make_async_remote_copy(
            src_ref=ins[w], dst_ref=outs[w], send_sem=send_sems.at[w], recv_sem=recv_sems.at[w],
            device_id=(x, y, 1 - c), device_id_type=MESH) for w in range(n)]
        for cp in copies:
            cp.start()
        for cp in copies:
            cp.wait()

    return pl.pallas_call(
        body, name="grad_swap_halves",
        out_shape=[jax.ShapeDtypeStruct(h.shape, h.dtype) for h in mine],
        in_specs=[ANY] * n, out_specs=[ANY] * n,
        scratch_shapes=[pltpu.SemaphoreType.DMA((n,)), pltpu.SemaphoreType.DMA((n,))],
    )(*mine)


def _gather_small(small):
    srows, cols = small.shape

    def body(s_ref, all_ref, send_sems, recv_sems, local_sem):
        x, y, c = _place()
        me = 4 * x + 2 * y + c
        keep_small = pltpu.make_async_copy(s_ref, all_ref.at[me], local_sem)
        keep_small.start()
        sends = []
        for kk in range(1, 8):
            peer = (x ^ (kk >> 2), y ^ ((kk >> 1) & 1), c ^ (kk & 1))
            sends.append(pltpu.make_async_remote_copy(
                src_ref=s_ref, dst_ref=all_ref.at[me],
                send_sem=send_sems.at[kk], recv_sem=recv_sems.at[kk], device_id=peer, device_id_type=MESH))
        for cp in sends:
            cp.start()
        for kk in range(1, 8):
            px, py, pc = x ^ (kk >> 2), y ^ ((kk >> 1) & 1), c ^ (kk & 1)
            pltpu.make_async_remote_copy(
                src_ref=s_ref, dst_ref=all_ref.at[4 * px + 2 * py + pc],
                send_sem=send_sems.at[kk], recv_sem=recv_sems.at[kk], device_id=(px, py, pc),
                device_id_type=MESH).wait_recv()
        for cp in sends:
            cp.wait_send()
        keep_small.wait()

    return pl.pallas_call(
        body, name="gather_small",
        out_shape=jax.ShapeDtypeStruct((8, srows, cols), small.dtype),
        in_specs=[ANY], out_specs=ANY,
        scratch_shapes=[pltpu.SemaphoreType.DMA((8,)), pltpu.SemaphoreType.DMA((8,)), pltpu.SemaphoreType.DMA],
    )(small)


SHARDED = (("w_in", D_MODEL, IN_COLS, 1), ("w_up_a", ATT_WIDTH, D_MODEL, 1), ("w_up_b", ATT_WIDTH, D_MODEL, 1),
           ("w_out", D_MODEL, D_MODEL, 0), ("w_q_mem", D_MODEL, MEM_WIDTH, 0), ("w_kv_mem", D_MODEL, 2 * MEM_WIDTH, 0),
           ("w_o_mem", MEM_WIDTH, D_MODEL, 1), ("w_ffn_gate", D_MODEL, D_FF, 1), ("w_ffn_up", D_MODEL, D_FF, 1),
           ("w_ffn_down", D_FF, D_MODEL, 0))
NAMES = tuple(n for n, _, _, _ in SHARDED)
EARLY, LATE = NAMES[:1], NAMES[1:]
GAINS = ("g_mix", "g_mem_q", "g_mem_kv", "g_ffn", "g_final")


def _natural(w3):
    n, r, c = w3.shape
    return w3.reshape(n * r, c)


def _shard_major(g, axis):
    if axis == 1:
        return g
    r, c = g.shape
    return g.reshape(N_CHIPS, r // N_CHIPS, c)


def kernel(x, mem, positions, g_mix, w_in, w_up_a, w_up_b, w_out, g_mem_q, g_mem_kv, w_q_mem, w_kv_mem, w_o_mem, g_ffn, w_ffn_gate, w_ffn_up, w_ffn_down, g_final, loss_target, m_g_mix, m_w_in, m_w_up_a, m_w_up_b, m_w_out, m_g_mem_q, m_g_mem_kv, m_w_q_mem, m_w_kv_mem, m_w_o_mem, m_g_ffn, m_w_ffn_gate, m_w_ffn_up, m_w_ffn_down, m_g_final, v_g_mix, v_w_in, v_w_up_a, v_w_up_b, v_w_out, v_g_mem_q, v_g_mem_kv, v_w_q_mem, v_w_kv_mem, v_w_o_mem, v_g_ffn, v_w_ffn_gate, v_w_ffn_up, v_w_ffn_down, v_g_final):
    given = dict(locals())
    shards = {n: given[n][0] for n, _, _, _ in SHARDED}

    wf = dict(zip(EARLY, _gather_weights([shards[n].astype(BF16) for n in EARLY])))
    late_shards = [shards[n].astype(BF16) for n in LATE]
    c_idx = lax.axis_index("c").astype(jnp.int32).reshape(1)
    s_idx = (2 * lax.axis_index("x") + lax.axis_index("y")).astype(jnp.int32).reshape(1)

    loss_row, grad_x, mine, gain_grads = _local_step(x, mem, positions, loss_target, g_mix, g_mem_q, g_mem_kv,
                                                     g_ffn, g_final, wf, late_shards, (c_idx, s_idx))
    return _reduce_and_update(given, shards, loss_row, grad_x, mine, gain_grads, c_idx)


def _reduce_halves(glist, names, c_idx, s_idx, pair_exchange, chip_exchange):
    theirs = pair_exchange(glist)
    pairs = [_pair_sum(g, t, c_idx, name="pair_sum_" + n) for n, g, t in zip(names, glist, theirs)]
    recv = chip_exchange(pairs)
    return [_chip_sum(p, r, s_idx, name="chip_sum_" + n) for n, p, r in zip(names, pairs, recv)]


def _local_step(x, mem, positions, loss_target, g_mix, g_mem_q, g_mem_kv, g_ffn, g_final, wf,
                late_shards=None, place=None):
    b_dim, s_dim, d = x.shape
    t_dim = b_dim * s_dim
    n_mem = mem.shape[1]
    wf = dict(wf)

    xb = x.reshape(t_dim, d)
    tgt = loss_target.reshape(t_dim, d)
    memf = mem.reshape(b_dim * n_mem, d)
    gfin = g_final.reshape(1, d)
    pos = positions.reshape(t_dim, 1).astype(F32)

    lane = jnp.arange(LANES) % HEAD_DIM
    half = ROPE_DIM // 2
    inv_freq = ROPE_THETA ** (-jnp.arange(half, dtype=F32) / half)
    inv_lane = jnp.where(lane < ROPE_DIM, inv_freq[lane % half], 0.0).reshape(1, -1).astype(F32)
    sel_a = (lane < half).astype(F32).reshape(1, -1)
    sel_b = ((lane >= half) & (lane < ROPE_DIM)).astype(F32).reshape(1, -1)

    def rows3(t):
        return t.reshape(b_dim, s_dim, t.shape[-1])

    def rows2(t):
        return t.reshape(t_dim, t.shape[-1])

    n1 = _rms_fwd(xb, g_mix, name="rms_mix")
    proj = _mm_cs(n1, wf["w_in"], name="mm_in")
    proj3 = rows3(proj)
    cs, sn = _rope_table(pos, inv_lane, sel_a, sel_b)
    cs3, sn3 = rows3(cs), rows3(sn)
    (oa16, oa32, lse_a), _ = _dil_fwd(proj3, cs3, sn3, sel_a, sel_b)
    ob16, gathered = _sb_fwd(proj3, ("gather", late_shards) if late_shards else None)
    wf.update(zip(LATE, gathered))
    w_out, w_q, w_kv = _natural(wf["w_out"]), _natural(wf["w_q_mem"]), _natural(wf["w_kv_mem"])
    oa, ob = rows2(oa16), rows2(ob16)
    ua = _mm_cs(oa, wf["w_up_a"], name="mm_up_a")
    ub = _mm_cs(ob, wf["w_up_b"], name="mm_up_b")
    mixed = _gate_fwd(proj, ua, ub)
    h1 = _mm(mixed, w_out, name="mm_out", add=xb)

    hn = _rms_fwd(h1, g_mem_q, name="rms_mem_q")
    memn = _rms_fwd(memf, g_mem_kv, name="rms_mem_kv")
    qm = _mm(hn, w_q, name="mm_q_mem", out_dtype=BF16)
    kvm = _mm(memn, w_kv, name="mm_kv_mem", out_dtype=BF16)
    qm3, kvm3 = rows3(qm), kvm.reshape(b_dim, n_mem, 2 * MEM_WIDTH)
    om = rows2(_mem_fwd(qm3, kvm3))
    h2 = _mm_cs(om, wf["w_o_mem"], name="mm_o_mem", add=h1)

    n3 = _rms_fwd(h2, g_ffn, name="rms_ffn")
    gate3, up3, act3 = _ffn_up_swiglu(n3, wf["w_ffn_gate"], wf["w_ffn_up"])
    h3 = _mm_ffn_down(act3, wf["w_ffn_down"], name="mm_down", add=h2)
    loss_row, dh3, dg_final = _final(h3, gfin, tgt)

    grads = {}
    grads["w_ffn_down"] = _mm_ffn_down_dw(act3, dh3, name="mm_down_dw")
    dgate3, dup3 = _ffn_down_dx_swiglu(dh3, wf["w_ffn_down"], gate3, up3)
    grads["w_ffn_gate"] = _mm_ffn_up_dw(n3, dgate3, name="mm_gate_dw")
    grads["w_ffn_up"] = _mm_ffn_up_dw(n3, dup3, name="mm_up_dw")
    dn3 = _mm_ffn_up_dx(dgate3, wf["w_ffn_gate"], name="mm_gate_dx")
    dn3 = _mm_ffn_up_dx(dup3, wf["w_ffn_up"], name="mm_up_dx", add=dn3)
    dh2, dg_ffn = _rms_bwd(h2, g_ffn, dn3, dh3, name="rms_ffn_bwd")

    dom = _mm_cs_dx(dh2, wf["w_o_mem"], name="mm_o_mem_dx", out_dtype=BF16)
    grads["w_o_mem"] = _mm_cs_dw(om, dh2, name="mm_o_mem_dw")
    dqm, dkm, dvm = _mem_bwd(qm3, kvm3, rows3(dom))
    dqm = rows2(dqm)
    dkvm = jnp.concatenate([dkm, dvm], axis=-1).reshape(b_dim * n_mem, 2 * MEM_WIDTH).astype(BF16)
    grads["w_q_mem"] = _shard_major(_mm(hn, dqm, name="mm_q_mem_dw", ta=True), 0)
    dhn = _mm(dqm, w_q, name="mm_q_mem_dx", tb=True)
    grads["w_kv_mem"] = _shard_major(_mm(memn, dkvm, name="mm_kv_mem_dw", ta=True), 0)
    dmemn = _mm(dkvm, w_kv, name="mm_kv_mem_dx", tb=True)
    _, dg_mem_kv = _rms_bwd(memf, g_mem_kv, dmemn, None, name="rms_mem_kv_bwd")
    dh1, dg_mem_q = _rms_bwd(h1, g_mem_q, dhn, dh2, name="rms_mem_q_bwd")

    dmix = _mm(dh1, w_out, name="mm_out_dx", tb=True)
    grads["w_out"] = _shard_major(_mm(mixed, dh1, name="mm_out_dw", ta=True), 0)
    dua, dub, dgates = _gate_bwd(proj, ua, ub, dmix)
    doa = _mm_cs_dx(dua, wf["w_up_a"], name="mm_up_a_dx")
    grads["w_up_a"] = _mm_cs_dw(oa, dua, name="mm_up_a_dw")
    dob = _mm_cs_dx(dub, wf["w_up_b"], name="mm_up_b_dx", out_dtype=BF16)
    grads["w_up_b"] = _mm_cs_dw(ob, dub, name="mm_up_b_dw")

    att = {}

    def dil_with_pairs(glist):
        att["a"], theirs = _dil_bwd(proj3, cs3, sn3, sel_a, sel_b, rows3(doa), oa32, lse_a,
                                    ("pair", glist) if glist else None)
        return theirs

    def sb_with_chips(pairs):
        att["b"], recv = _sb_bwd(proj3, rows3(dob), ("chip", pairs) if pairs else None)
        return recv

    if place is None:
        dil_with_pairs(())
        sb_with_chips(())
    else:
        mine_late = _reduce_halves([grads[n] for n in LATE], LATE, *place, dil_with_pairs, sb_with_chips)
    dproj = jnp.concatenate([rows2(t) for t in att["a"] + att["b"]] + [dgates], axis=1)
    grads["w_in"] = _mm_cs_dw(n1, dproj, name="mm_in_dw")
    if place is None:
        dn1 = _mm_cs_dx(dproj, wf["w_in"], name="mm_in_dx")
        dx, dg_mix = _rms_bwd(xb, g_mix, dn1, dh1, name="rms_mix_bwd")
    else:
        tail = {}

        def dx_with_pairs(glist):
            tail["dn1"], theirs = _mm_cs_dx(dproj, wf["w_in"], name="mm_in_dx", rider=("pair", glist))
            return theirs

        def rms_with_chips(pairs):
            tail["dx"], tail["dg"], recv = _rms_bwd(xb, g_mix, tail["dn1"], dh1, name="rms_mix_bwd",
                                                    rider=("chip", pairs))
            return recv

        mine_early = _reduce_halves([grads[n] for n in EARLY], EARLY, *place, dx_with_pairs, rms_with_chips)
        dx, dg_mix = tail["dx"], tail["dg"]
    grad_x = dx.reshape(b_dim, s_dim, d)
    gains = (dg_mix, dg_mem_q, dg_mem_kv, dg_ffn, dg_final)
    if place is None:
        return loss_row, grad_x, grads, gains
    return loss_row, grad_x, mine_early + mine_late, gains


def _reduce_and_update(given, shards, loss_row, grad_x, mine, gain_grads, c_idx):
    d = D_MODEL
    dg_mix, dg_mem_q, dg_mem_kv, dg_ffn, dg_final = gain_grads
    small = jnp.concatenate([dg_mix, dg_mem_q, dg_mem_kv, dg_ffn, dg_final,
                             jnp.pad(loss_row, ((0, 0), (0, FLAT_COLS - LANES))), jnp.zeros((2, FLAT_COLS), F32)], axis=0)
    small_all = _gather_small(small)
    others = _swap_halves(mine)
    small_sum = _sum8(small_all)
    loss = small_sum[5, 0]

    out_g, out_d, out_m, out_v = {}, {}, {}, {}
    for n, mine_n, other_n in zip(NAMES, mine, others):
        g2, dl, nm, nv = _adamw_halves(shards[n], mine_n, other_n, given["m_" + n][0], given["v_" + n][0], c_idx,
                                       name="adamw_" + n)
        out_g[n], out_d[n], out_m[n], out_v[n] = g2[None], dl[None], nm[None], nv[None]
    gain_w = jnp.concatenate([given[n].reshape(1, d) for n in GAINS], axis=0)
    gain_m = jnp.concatenate([given["m_" + n].reshape(1, d) for n in GAINS], axis=0)
    gain_v = jnp.concatenate([given["v_" + n].reshape(1, d) for n in GAINS], axis=0)
    gain_g = small_sum[:len(GAINS)]
    gd, gm, gv = _adamw(gain_w, gain_g, gain_m, gain_v, name="adamw_gains")
    for i, n in enumerate(GAINS):
        shape = given[n].shape
        out_g[n], out_d[n] = gain_g[i].reshape(shape), gd[i].reshape(shape)
        out_m[n], out_v[n] = gm[i].reshape(shape), gv[i].reshape(shape)

    order = ["g_mix", "w_in", "w_up_a", "w_up_b", "w_out", "g_mem_q", "g_mem_kv", "w_q_mem", "w_kv_mem", "w_o_mem",
             "g_ffn", "w_ffn_gate", "w_ffn_up", "w_ffn_down", "g_final"]
    return (loss, grad_x, *[out_g[n] for n in order], *[out_d[n] for n in order],
            *[out_m[n] for n in order], *[out_v[n] for n in order])
```

```python
import jax
import jax.numpy as jnp
from jax import lax
from jax.experimental import pallas as pl
from jax.experimental.pallas import tpu as pltpu

F32 = jnp.float32
BF16 = jnp.bfloat16
MESH = pl.DeviceIdType.MESH

D_MODEL = 1024
HEAD_DIM = 64
N_HEADS = 8
ATT_WIDTH = N_HEADS * HEAD_DIM
DIL_PATTERNS = ((128, 1), (512, 4), (2048, 16))
BLOCK = 128
SB_ROWS = 1024
SB_STEP = 4
ROPE_THETA = 500000.0
ROPE_DIM = HEAD_DIM // 4
N_HEADS_MEM = 4
MEM_HEAD_DIM = 128
MEM_WIDTH = N_HEADS_MEM * MEM_HEAD_DIM
D_FF = 2816
IN_COLS = 6 * ATT_WIDTH + 2 * D_MODEL
RMS_EPS = 1e-6
ADAM_LR = 0.001
ADAM_B1 = 0.9
ADAM_B2 = 0.999
ADAM_EPS = 1e-08
ADAM_WD = 0.01
ADAM_STEP = 10

N_CHIPS = 4
LANES = 128
FLAT_COLS = 1024
VMEM_LIMIT = 56 * 1024 * 1024

PAIRS = ATT_WIDTH // LANES
COL_QA, COL_KA, COL_VA, COL_QB, COL_KB, COL_VB = (i * PAIRS for i in range(6))

MM_CAP = 1408
TOK_CAP = 2048
NN = (((1,), (0,)), ((), ()))
NT = (((1,), (1,)), ((), ()))
TN = (((0,), (0,)), ((), ()))
BNN = (((2,), (1,)), ((0,), (0,)))
BNT = (((2,), (2,)), ((0,), (0,)))
BTN = (((1,), (1,)), ((0,), (0,)))
DIL_BATCH = 8


def _tile(dim, cap, unit=LANES):
    if dim <= cap:
        return dim
    best = None
    for t in range(unit, cap + 1, unit):
        if dim % t == 0:
            best = t
    assert best is not None, (dim, cap)
    return best


def _row_cap(cols):
    return max(256, (1 << 18) // cols)


def _params(sem):
    return pltpu.CompilerParams(dimension_semantics=sem, vmem_limit_bytes=VMEM_LIMIT)


def _mm(a, b, *, name, ta=False, tb=False, add=None, out_dtype=F32,
        tm_cap=MM_CAP, tn_cap=MM_CAP, tk_cap=MM_CAP):
    if ta:
        k_dim, m_dim = a.shape
    else:
        m_dim, k_dim = a.shape
    if tb:
        n_dim, kb = b.shape
    else:
        kb, n_dim = b.shape
    assert kb == k_dim, (a.shape, b.shape, ta, tb)
    tm, tn, tk = _tile(m_dim, tm_cap), _tile(n_dim, tn_cap), _tile(k_dim, tk_cap)
    nk = k_dim // tk
    dims = (((0 if ta else 1,), (1 if tb else 0,)), ((), ()))
    has_add = add is not None

    def body(*refs):
        if has_add:
            a_ref, b_ref, add_ref, o_ref = refs[:4]
        else:
            a_ref, b_ref, o_ref = refs[:3]
        part = lax.dot_general(a_ref[...].astype(BF16), b_ref[...].astype(BF16), dims, preferred_element_type=F32)

        def finish(r):
            if has_add:
                r = add_ref[...] + r
            o_ref[...] = r.astype(out_dtype)

        if nk == 1:
            finish(part)
            return
        acc_ref = refs[-1]
        k = pl.program_id(2)

        @pl.when(k == 0)
        def _():
            acc_ref[...] = part

        @pl.when(k > 0)
        def _():
            acc_ref[...] += part

        @pl.when(k == nk - 1)
        def _():
            finish(acc_ref[...])

    a_spec = pl.BlockSpec((tk, tm), lambda i, j, k: (k, i)) if ta else pl.BlockSpec((tm, tk), lambda i, j, k: (i, k))
    b_spec = pl.BlockSpec((tn, tk), lambda i, j, k: (j, k)) if tb else pl.BlockSpec((tk, tn), lambda i, j, k: (k, j))
    o_spec = pl.BlockSpec((tm, tn), lambda i, j, k: (i, j))
    in_specs = [a_spec, b_spec] + ([o_spec] if has_add else [])
    args = (a, b) + ((add,) if has_add else ())
    return pl.pallas_call(
        body, name=name, grid=(m_dim // tm, n_dim // tn, nk),
        in_specs=in_specs, out_specs=o_spec,
        out_shape=jax.ShapeDtypeStruct((m_dim, n_dim), out_dtype),
        scratch_shapes=[pltpu.VMEM((tm, tn), F32)] if nk > 1 else [],
        compiler_params=_params(("parallel", "parallel", "arbitrary")),
    )(*args)


def _mm_core(name, a, b, a_spec, b_spec, o_spec, out_shape, grid, dims, *, add=None, out_dtype=F32, rider=None):
    nk = grid[2]
    has_add = add is not None
    n_in = 3 if has_add else 2
    acc_shape = tuple(d for d in o_spec.block_shape if d is not None)
    extra, extra_shapes, extra_sems = _rider_parts(rider)
    n_w = len(extra)

    def body(*refs):
        a_ref, b_ref = refs[:2]
        o_ref = refs[n_in + n_w]
        step = (pl.program_id(0) * grid[1] + pl.program_id(1)) * nk + pl.program_id(2)
        begin, end = _rider_hooks(rider, refs[n_in:n_in + n_w], refs[n_in + n_w + 1:n_in + 2 * n_w + 1], refs[-2:],
                                  step, grid[0] * grid[1] * nk)
        begin()
        part = lax.dot_general(a_ref[...].astype(BF16), b_ref[...].astype(BF16), dims, preferred_element_type=F32)

        def finish(r):
            if has_add:
                r = refs[2][...] + r
            o_ref[...] = r.astype(out_dtype)

        if nk == 1:
            finish(part)
        else:
            acc_ref = refs[n_in + 2 * n_w + 1]
            k = pl.program_id(2)

            @pl.when(k == 0)
            def _():
                acc_ref[...] = part

            @pl.when(k > 0)
            def _():
                acc_ref[...] += part

            @pl.when(k == nk - 1)
            def _():
                finish(acc_ref[...])
        end()

    in_specs = [a_spec, b_spec] + ([o_spec] if has_add else []) + [ANY] * n_w
    args = (a, b) + ((add,) if has_add else ()) + extra
    res = pl.pallas_call(
        body, name=name, grid=grid, in_specs=in_specs, out_specs=[o_spec] + [ANY] * n_w,
        out_shape=[jax.ShapeDtypeStruct(out_shape, out_dtype)] + extra_shapes,
        scratch_shapes=([pltpu.VMEM(acc_shape, F32)] if nk > 1 else []) + extra_sems,
        compiler_params=_params(("arbitrary",) * 3 if n_w else ("parallel", "parallel", "arbitrary")),
    )(*args)
    return (res[0], res[1:]) if n_w else res[0]


def _mm_cs(a, w3, *, name, add=None, out_dtype=F32):
    m_dim, k_dim = a.shape
    _, _, n4 = w3.shape
    tm, tn, tk = _tile(m_dim, MM_CAP if add is not None else TOK_CAP), _tile(n4, MM_CAP), _tile(k_dim, MM_CAP)
    npb = n4 // tn
    return _mm_core(name, a, w3,
                    pl.BlockSpec((tm, tk), lambda i, j, k: (i, k)),
                    pl.BlockSpec((None, tk, tn), lambda i, j, k: (j // npb, k, j % npb)),
                    pl.BlockSpec((tm, tn), lambda i, j, k: (i, j)),
                    (m_dim, N_CHIPS * n4), (m_dim // tm, N_CHIPS * npb, k_dim // tk), NN, add=add, out_dtype=out_dtype)


def _mm_cs_dx(dy, w3, *, name, out_dtype=F32, rider=None):
    m_dim, _ = dy.shape
    _, k_dim, n4 = w3.shape
    tm, tkw, tn = _tile(m_dim, MM_CAP), _tile(k_dim, MM_CAP), _tile(n4, MM_CAP)
    npb = n4 // tn
    return _mm_core(name, dy, w3,
                    pl.BlockSpec((tm, tn), lambda i, j, k: (i, k)),
                    pl.BlockSpec((None, tkw, tn), lambda i, j, k: (k // npb, j, k % npb)),
                    pl.BlockSpec((tm, tkw), lambda i, j, k: (i, j)),
                    (m_dim, k_dim), (m_dim // tm, k_dim // tkw, N_CHIPS * npb), NT, out_dtype=out_dtype, rider=rider)


def _mm_cs_dw(a, dy, *, name):
    m_dim, k_dim = a.shape
    n4 = dy.shape[1] // N_CHIPS
    tmk, tn, tk = _tile(k_dim, MM_CAP), _tile(n4, MM_CAP), _tile(m_dim, TOK_CAP)
    npb = n4 // tn
    return _mm_core(name, a, dy,
                    pl.BlockSpec((tk, tmk), lambda i, j, k: (k, i)),
                    pl.BlockSpec((tk, tn), lambda i, j, k: (k, j)),
                    pl.BlockSpec((None, tmk, tn), lambda i, j, k: (j // npb, i, j % npb)),
                    (N_CHIPS, k_dim, n4), (k_dim // tmk, N_CHIPS * npb, m_dim // tk), TN)


def _mm_ffn_up_dw(n, d3, *, name):
    t_dim, d = n.shape
    _, _, f4 = d3.shape
    tk = _tile(t_dim, TOK_CAP)
    return _mm_core(name, n, d3,
                    pl.BlockSpec((tk, d), lambda i, j, k: (k, 0)),
                    pl.BlockSpec((None, tk, f4), lambda i, j, k: (j, k, 0)),
                    pl.BlockSpec((None, d, f4), lambda i, j, k: (j, 0, 0)),
                    (N_CHIPS, d, f4), (1, N_CHIPS, t_dim // tk), TN)


def _mm_ffn_up_dx(d3, w3, *, name, add=None):
    _, t_dim, f4 = d3.shape
    _, d, _ = w3.shape
    tm = _tile(t_dim, MM_CAP)
    return _mm_core(name, d3, w3,
                    pl.BlockSpec((None, tm, f4), lambda i, j, k: (k, i, 0)),
                    pl.BlockSpec((None, d, f4), lambda i, j, k: (k, 0, 0)),
                    pl.BlockSpec((tm, d), lambda i, j, k: (i, 0)),
                    (t_dim, d), (t_dim // tm, 1, N_CHIPS), NT, add=add)


def _mm_ffn_down(act3, wd3, *, name, add):
    _, t_dim, f4 = act3.shape
    _, _, d = wd3.shape
    tm = _tile(t_dim, MM_CAP)
    return _mm_core(name, act3, wd3,
                    pl.BlockSpec((None, tm, f4), lambda i, j, k: (k, i, 0)),
                    pl.BlockSpec((None, f4, d), lambda i, j, k: (k, 0, 0)),
                    pl.BlockSpec((tm, d), lambda i, j, k: (i, 0)),
                    (t_dim, d), (t_dim // tm, 1, N_CHIPS), NN, add=add)


def _mm_ffn_down_dw(act3, dh, *, name):
    _, t_dim, f4 = act3.shape
    d = dh.shape[1]
    tk = _tile(t_dim, TOK_CAP)
    return _mm_core(name, act3, dh,
                    pl.BlockSpec((None, tk, f4), lambda i, j, k: (i, k, 0)),
                    pl.BlockSpec((tk, d), lambda i, j, k: (k, 0)),
                    pl.BlockSpec((None, f4, d), lambda i, j, k: (i, 0, 0)),
                    (N_CHIPS, f4, d), (N_CHIPS, 1, t_dim // tk), TN)


def _rms_fwd(x, g, *, name, tt=512):
    t_dim, d = x.shape
    tt = _tile(t_dim, tt, 8)

    def body(x_ref, g_ref, o_ref):
        xv = x_ref[...]
        r = lax.rsqrt(jnp.mean(xv * xv, axis=-1, keepdims=True) + RMS_EPS)
        o_ref[...] = ((xv * r) * g_ref[...]).astype(o_ref.dtype)

    return pl.pallas_call(
        body, name=name, grid=(t_dim // tt,),
        in_specs=[pl.BlockSpec((tt, d), lambda i: (i, 0)), pl.BlockSpec((1, d), lambda i: (0, 0))],
        out_specs=pl.BlockSpec((tt, d), lambda i: (i, 0)),
        out_shape=jax.ShapeDtypeStruct((t_dim, d), BF16),
        compiler_params=_params(("parallel",)),
    )(x, g)


def _rms_bwd(x, g, dy, add, *, name, tt=512, rider=None):
    t_dim, d = x.shape
    tt = _tile(t_dim, tt, 8)
    has_add = add is not None
    n_in = 4 if has_add else 3
    extra, extra_shapes, extra_sems = _rider_parts(rider)
    n_w = len(extra)

    def body(*refs):
        x_ref, g_ref, dy_ref = refs[:3]
        add_ref = refs[3] if has_add else None
        dx_ref, dg_ref = refs[n_in + n_w:n_in + n_w + 2]
        begin, end = _rider_hooks(rider, refs[n_in:n_in + n_w], refs[n_in + n_w + 2:n_in + 2 * n_w + 2], refs[-2:],
                                  pl.program_id(0), t_dim // tt)
        begin()
        xv = x_ref[...]
        dyv = dy_ref[...].astype(F32)
        r = lax.rsqrt(jnp.mean(xv * xv, axis=-1, keepdims=True) + RMS_EPS)
        xh = xv * r
        u = dyv * g_ref[...]
        dx = r * (u - xh * jnp.mean(u * xh, axis=-1, keepdims=True))
        if has_add:
            dx = add_ref[...] + dx
        dx_ref[...] = dx

        @pl.when(pl.program_id(0) == 0)
        def _():
            dg_ref[...] = jnp.zeros_like(dg_ref)

        dg_ref[...] += jnp.sum(dyv * xh, axis=0, keepdims=True)
        end()

    row = pl.BlockSpec((tt, d), lambda i: (i, 0))
    vec = pl.BlockSpec((1, d), lambda i: (0, 0))
    in_specs = [row, vec, row] + ([row] if has_add else []) + [ANY] * n_w
    args = (x, g, dy) + ((add,) if has_add else ()) + extra
    res = pl.pallas_call(
        body, name=name, grid=(t_dim // tt,),
        in_specs=in_specs, out_specs=[row, vec] + [ANY] * n_w,
        out_shape=[jax.ShapeDtypeStruct((t_dim, d), F32), jax.ShapeDtypeStruct((1, d), F32)] + extra_shapes,
        scratch_shapes=extra_sems,
        compiler_params=_params(("arbitrary",)),
    )(*args)
    return (res[0], res[1], res[2:]) if n_w else (res[0], res[1])


def _final(h, g, target, *, tt=512):
    t_dim, d = h.shape
    n_steps = t_dim // tt

    def body(h_ref, g_ref, t_ref, loss_ref, dh_ref, dg_ref, sq_ref):
        i = pl.program_id(0)
        xv = h_ref[...]
        gv = g_ref[...]
        r = lax.rsqrt(jnp.mean(xv * xv, axis=-1, keepdims=True) + RMS_EPS)
        xh = xv * r
        err = xh * gv - t_ref[...]
        dyv = err * (1.0 / d)
        u = dyv * gv
        dh_ref[...] = r * (u - xh * jnp.mean(u * xh, axis=-1, keepdims=True))

        @pl.when(i == 0)
        def _():
            dg_ref[...] = jnp.zeros_like(dg_ref)
            sq_ref[...] = jnp.zeros_like(sq_ref)

        dg_ref[...] += jnp.sum(dyv * xh, axis=0, keepdims=True)
        sq_ref[...] += jnp.sum(err * err, axis=0, keepdims=True)

        @pl.when(i == n_steps - 1)
        def _():
            total = jnp.sum(sq_ref[...], axis=-1, keepdims=True) * (0.5 / d)
            loss_ref[...] = jnp.broadcast_to(total, loss_ref.shape)

    row = pl.BlockSpec((tt, d), lambda i: (i, 0))
    vec = pl.BlockSpec((1, d), lambda i: (0, 0))
    return pl.pallas_call(
        body, name="final_loss", grid=(n_steps,),
        in_specs=[row, vec, row],
        out_specs=[pl.BlockSpec((1, LANES), lambda i: (0, 0)), row, vec],
        out_shape=[jax.ShapeDtypeStruct((1, LANES), F32), jax.ShapeDtypeStruct((t_dim, d), F32),
                   jax.ShapeDtypeStruct((1, d), F32)],
        scratch_shapes=[pltpu.VMEM((1, d), F32)],
        compiler_params=_params(("arbitrary",)),
    )(h, g, target)


def _rope_table(pos, inv_lane, sel_a, sel_b, *, tt=512):
    t_dim = pos.shape[0]

    def body(p_ref, f_ref, a_ref, b_ref, c_ref, s_ref):
        ang = p_ref[...] * f_ref[...]
        on = (a_ref[...] + b_ref[...]) > 0.0
        c_ref[...] = jnp.where(on, jnp.cos(ang), 1.0)
        s_ref[...] = jnp.where(on, jnp.sin(ang), 0.0)

    vec = pl.BlockSpec((1, LANES), lambda i: (0, 0))
    row = pl.BlockSpec((tt, LANES), lambda i: (i, 0))
    shp = jax.ShapeDtypeStruct((t_dim, LANES), F32)
    return pl.pallas_call(
        body, name="rope_table", grid=(t_dim // tt,),
        in_specs=[pl.BlockSpec((tt, 1), lambda i: (i, 0)), vec, vec, vec],
        out_specs=[row, row], out_shape=[shp, shp],
        compiler_params=_params(("parallel",)),
    )(pos, inv_lane, sel_a, sel_b)


def _rotate(xv, cs, sn, sa, sb):
    half = ROPE_DIM // 2
    up = pltpu.roll(xv, LANES - half, 1)
    dn = pltpu.roll(xv, half, 1)
    return xv * cs + (dn * sb - up * sa) * sn


def _head_masks():
    h1 = lax.broadcasted_iota(jnp.int32, (1, LANES), 1) < HEAD_DIM
    return h1, jnp.logical_not(h1)


def _split_heads(xv, h1, h2):
    return jnp.where(h1, xv, 0.0).astype(BF16), jnp.where(h2, xv, 0.0).astype(BF16)


def _tri_masks():
    r = lax.broadcasted_iota(jnp.int32, (BLOCK, BLOCK), 0)
    c = lax.broadcasted_iota(jnp.int32, (BLOCK, BLOCK), 1)
    return c <= r, r <= c


def _stream_rows(start, dil):
    if dil == 1:
        return pl.ds(pl.multiple_of(start, BLOCK), BLOCK)
    return pl.ds(start, BLOCK, stride=dil)


def _dil_tile(idx, dil, nb):
    r = idx // nb
    n = idx % nb
    return (_stream_rows(r + dil * BLOCK * n, dil), _stream_rows(r + dil * BLOCK * jnp.maximum(n - 1, 0), dil),
            n > 0)


def _dil_specs(b_dim, s_dim):
    def col(c0):
        return pl.BlockSpec((None, s_dim, LANES),lambda b, h: (b, 0, c0 + h))
    tab = pl.BlockSpec((None, s_dim, LANES),lambda b, h: (b, 0, 0))
    vec = pl.BlockSpec((1, LANES), lambda b, h: (0, 0))
    return col, tab, vec


def _dil_fwd(proj3, cs3, sn3, sel_a, sel_b, rider=None):
    b_dim, s_dim, _ = proj3.shape
    scale = HEAD_DIM ** -0.5
    n_pat = len(DIL_PATTERNS)
    extra, extra_shapes, extra_sems = _rider_parts(rider)
    n_w = len(extra)
    n_steps = b_dim * PAIRS

    def body(*refs):
        q_ref, k_ref, v_ref, cs_ref, sn_ref, sa_ref, sb_ref = refs[:7]
        o16_ref, o32_ref, l_ref = refs[7 + n_w:10 + n_w]
        qr, kr = refs[10 + 2 * n_w:12 + 2 * n_w]
        per_pattern = refs[12 + 2 * n_w:12 + 2 * n_w + 2 * n_pat]
        og, lg = per_pattern[:n_pat], per_pattern[n_pat:]
        step = pl.program_id(0) * PAIRS + pl.program_id(1)
        begin, end = _rider_hooks(rider, refs[7:7 + n_w], refs[10 + n_w:10 + 2 * n_w], refs[-2:], step, n_steps)
        begin()
        h1, h2 = _head_masks()
        cur_ok, prev_ok = _tri_masks()
        sa, sb = sa_ref[...], sb_ref[...]

        def prep(j, _):
            rows = pl.ds(pl.multiple_of(j * BLOCK, BLOCK), BLOCK)
            cs, sn = cs_ref[rows, :], sn_ref[rows, :]
            qr[rows, :] = _rotate(q_ref[rows, :], cs, sn, sa, sb) * scale
            kr[rows, :] = _rotate(k_ref[rows, :], cs, sn, sa, sb)
            return 0

        lax.fori_loop(0, s_dim // BLOCK, prep, 0)

        for g, (_, dil) in enumerate(DIL_PATTERNS):
            nb = s_dim // dil // BLOCK

            def some(bi, _, g=g, dil=dil, nb=nb):
                tiles = [_dil_tile(bi * DIL_BATCH + t, dil, nb) for t in range(DIL_BATCH)]
                rows = [t[0] for t in tiles]
                q1, q2 = _split_heads(jnp.stack([qr[rw, :] for rw in rows]), h1, h2)
                kc = jnp.stack([kr[rw, :] for rw in rows]).astype(BF16)
                vc1, vc2 = _split_heads(jnp.stack([v_ref[rw, :] for rw in rows]), h1, h2)
                if nb > 1:
                    kp = jnp.stack([kr[t[1], :] for t in tiles]).astype(BF16)
                    vp1, vp2 = _split_heads(jnp.stack([v_ref[t[1], :] for t in tiles]), h1, h2)
                    p_ok = jnp.stack([jnp.logical_and(prev_ok, t[2]) for t in tiles])

                def head(qh, vch, vph):
                    sc = jnp.where(cur_ok, lax.dot_general(qh, kc, BNT, preferred_element_type=F32), -jnp.inf)
                    m = jnp.max(sc, axis=-1, keepdims=True)
                    if nb > 1:
                        sp = jnp.where(p_ok, lax.dot_general(qh, kp, BNT, preferred_element_type=F32), -jnp.inf)
                        m = jnp.maximum(m, jnp.max(sp, axis=-1, keepdims=True))
                    pc = jnp.exp(sc - m)
                    den = jnp.sum(pc, axis=-1, keepdims=True)
                    acc = lax.dot_general(pc.astype(BF16), vch, BNN, preferred_element_type=F32)
                    if nb > 1:
                        pp = jnp.exp(sp - m)
                        den = den + jnp.sum(pp, axis=-1, keepdims=True)
                        acc = acc + lax.dot_general(pp.astype(BF16), vph, BNN, preferred_element_type=F32)
                    return acc / den, m + jnp.log(den)

                o1, l1 = head(q1, vc1, vp1 if nb > 1 else None)
                o2, l2 = head(q2, vc2, vp2 if nb > 1 else None)
                o, l = o1 + o2, jnp.where(h1, l1, l2)
                for t, rw in enumerate(rows):
                    og[g][rw, :] = o[t]
                    lg[g][rw, :] = l[t]
                return 0

            lax.fori_loop(0, dil * nb // DIL_BATCH, some, 0)

        def comb(j, _):
            rows = pl.ds(pl.multiple_of(j * BLOCK, BLOCK), BLOCK)
            ls = [lg[g][rows, :] for g in range(n_pat)]
            m = jnp.maximum(jnp.maximum(ls[0], ls[1]), ls[2])
            es = [jnp.exp(l - m) for l in ls]
            den = es[0] + es[1] + es[2]
            o = (es[0] * og[0][rows, :] + es[1] * og[1][rows, :] + es[2] * og[2][rows, :]) / den
            o16_ref[rows, :] = o.astype(BF16)
            o32_ref[rows, :] = o
            l_ref[rows, :] = m + jnp.log(den)
            return 0

        lax.fori_loop(0, s_dim // BLOCK, comb, 0)
        end()

    col, tab, vec = _dil_specs(b_dim, s_dim)
    out = pl.BlockSpec((None, s_dim, LANES),lambda b, h: (b, 0, h))
    shp = (b_dim, s_dim, ATT_WIDTH)
    res = pl.pallas_call(
        body, name="dil_fwd", grid=(b_dim, PAIRS),
        in_specs=[col(COL_QA), col(COL_KA), col(COL_VA), tab, tab, vec, vec] + [ANY] * n_w,
        out_specs=[out, out, out] + [ANY] * n_w,
        out_shape=[jax.ShapeDtypeStruct(shp, BF16), jax.ShapeDtypeStruct(shp, F32), jax.ShapeDtypeStruct(shp, F32)]
        + extra_shapes,
        scratch_shapes=[pltpu.VMEM((s_dim, LANES), F32)] * (2 + 2 * n_pat) + extra_sems,
        compiler_params=_params(("arbitrary", "arbitrary")),
    )(proj3, proj3, proj3, cs3, sn3, sel_a, sel_b, *extra)
    return res[:3], res[3:]


def _dil_bwd(proj3, cs3, sn3, sel_a, sel_b, do3, o3, lse3, rider=None):
    b_dim, s_dim, _ = proj3.shape
    scale = HEAD_DIM ** -0.5
    extra, extra_shapes, extra_sems = _rider_parts(rider)
    n_w = len(extra)

    def body(*refs):
        q_ref, k_ref, v_ref, cs_ref, sn_ref, sa_ref, sb_ref, do_ref, o_ref, l_ref = refs[:10]
        dq_ref, dk_ref, dv_ref = refs[10 + n_w:13 + n_w]
        qr, kr, dqa, dka, dva = refs[13 + 2 * n_w:18 + 2 * n_w]
        step = pl.program_id(0) * PAIRS + pl.program_id(1)
        begin, end = _rider_hooks(rider, refs[10:10 + n_w], refs[13 + n_w:13 + 2 * n_w], refs[-2:], step,
                                  b_dim * PAIRS)
        begin()
        h1, h2 = _head_masks()
        cur_ok, prev_ok = _tri_masks()
        sa, sb = sa_ref[...], sb_ref[...]

        def prep(j, _):
            rows = pl.ds(pl.multiple_of(j * BLOCK, BLOCK), BLOCK)
            cs, sn = cs_ref[rows, :], sn_ref[rows, :]
            qr[rows, :] = _rotate(q_ref[rows, :], cs, sn, sa, sb) * scale
            kr[rows, :] = _rotate(k_ref[rows, :], cs, sn, sa, sb)
            zero = jnp.zeros((BLOCK, LANES), F32)
            dqa[rows, :] = zero
            dka[rows, :] = zero
            dva[rows, :] = zero
            return 0

        lax.fori_loop(0, s_dim // BLOCK, prep, 0)

        for _, dil in DIL_PATTERNS:
            nb = s_dim // dil // BLOCK

            def some(bi, _, dil=dil, nb=nb):
                tiles = [_dil_tile(bi * DIL_BATCH + t, dil, nb) for t in range(DIL_BATCH)]
                rows = [t[0] for t in tiles]
                q1, q2 = _split_heads(jnp.stack([qr[rw, :] for rw in rows]), h1, h2)
                dof = jnp.stack([do_ref[rw, :] for rw in rows])
                do1, do2 = _split_heads(dof, h1, h2)
                prod = dof * jnp.stack([o_ref[rw, :] for rw in rows])
                delta1 = jnp.sum(jnp.where(h1, prod, 0.0), axis=-1, keepdims=True)
                delta2 = jnp.sum(jnp.where(h2, prod, 0.0), axis=-1, keepdims=True)
                lt = jnp.stack([l_ref[rw, :] for rw in rows])
                lse1 = jnp.max(jnp.where(h1, lt, -jnp.inf), axis=-1, keepdims=True)
                lse2 = jnp.max(jnp.where(h2, lt, -jnp.inf), axis=-1, keepdims=True)

                def side(krows, ok):
                    kf = jnp.stack([kr[kw, :] for kw in krows])
                    k16 = kf.astype(BF16)
                    k1, k2 = _split_heads(kf, h1, h2)
                    v16 = jnp.stack([v_ref[kw, :] for kw in krows]).astype(BF16)

                    def head(qh, doh, lse, delta):
                        sc = lax.dot_general(qh, k16, BNT, preferred_element_type=F32)
                        p = jnp.where(ok, jnp.exp(sc - lse), 0.0)
                        dp = lax.dot_general(doh, v16, BNT, preferred_element_type=F32)
                        return p.astype(BF16), (p * (dp - delta)).astype(BF16)

                    p1, ds1 = head(q1, do1, lse1, delta1)
                    p2, ds2 = head(q2, do2, lse2, delta2)
                    dv = (lax.dot_general(p1, do1, BTN, preferred_element_type=F32)
                          + lax.dot_general(p2, do2, BTN, preferred_element_type=F32))
                    dk = (lax.dot_general(ds1, q1, BTN, preferred_element_type=F32)
                          + lax.dot_general(ds2, q2, BTN, preferred_element_type=F32))
                    for t, kw in enumerate(krows):
                        dva[kw, :] += dv[t]
                        dka[kw, :] += dk[t]
                    return (lax.dot_general(ds1, k1, BNN, preferred_element_type=F32)
                            + lax.dot_general(ds2, k2, BNN, preferred_element_type=F32))

                dq = side(rows, cur_ok)
                if nb > 1:
                    dq = dq + side([t[1] for t in tiles], jnp.stack([jnp.logical_and(prev_ok, t[2]) for t in tiles]))
                for t, rw in enumerate(rows):
                    dqa[rw, :] += dq[t] * scale
                return 0

            lax.fori_loop(0, dil * nb // DIL_BATCH, some, 0)

        def finish(j, _):
            rows = pl.ds(pl.multiple_of(j * BLOCK, BLOCK), BLOCK)
            cs, sn = cs_ref[rows, :], -sn_ref[rows, :]
            dq_ref[rows, :] = _rotate(dqa[rows, :], cs, sn, sa, sb).astype(BF16)
            dk_ref[rows, :] = _rotate(dka[rows, :], cs, sn, sa, sb).astype(BF16)
            dv_ref[rows, :] = dva[rows, :].astype(BF16)
            return 0

        lax.fori_loop(0, s_dim // BLOCK, finish, 0)
        end()

    col, tab, vec = _dil_specs(b_dim, s_dim)
    out = pl.BlockSpec((None, s_dim, LANES),lambda b, h: (b, 0, h))
    shp = jax.ShapeDtypeStruct((b_dim, s_dim, ATT_WIDTH), BF16)
    acc = pltpu.VMEM((s_dim, LANES), F32)
    res = pl.pallas_call(
        body, name="dil_bwd", grid=(b_dim, PAIRS),
        in_specs=[col(COL_QA), col(COL_KA), col(COL_VA), tab, tab, vec, vec, out, out, out] + [ANY] * n_w,
        out_specs=[out, out, out] + [ANY] * n_w, out_shape=[shp, shp, shp] + extra_shapes,
        scratch_shapes=[acc, acc, acc, acc, acc] + extra_sems,
        compiler_params=_params(("arbitrary", "arbitrary")),
    )(proj3, proj3, proj3, cs3, sn3, sel_a, sel_b, do3, o3, lse3, *extra)
    return res[:3], res[3:]


def _split_dot(x, tri):
    hi = x.astype(BF16)
    lo = (x - hi.astype(F32)).astype(BF16)
    return jnp.dot(hi, tri, preferred_element_type=F32) + jnp.dot(lo, tri, preferred_element_type=F32)


def _log_sigmoid(z):
    return jnp.minimum(z, 0.0) - jnp.log(1.0 + jnp.exp(-jnp.abs(z)))


def _sb_scores(qh, k16, valid):
    z = lax.dot_general(qh, k16, NT, preferred_element_type=F32)
    ls = _log_sigmoid(z)
    l1m = ls - z
    return ls, (l1m if valid is None else jnp.where(valid, l1m, 0.0))


def _sb_consts():
    r = lax.broadcasted_iota(jnp.int32, (BLOCK, BLOCK), 0)
    c = lax.broadcasted_iota(jnp.int32, (BLOCK, BLOCK), 1)
    after = (r > c).astype(BF16)
    before = (r < c).astype(BF16)
    qrow = lax.broadcasted_iota(jnp.int32, (SB_ROWS, BLOCK), 0)
    kcol = lax.broadcasted_iota(jnp.int32, (SB_ROWS, BLOCK), 1)
    return after, before, qrow, kcol


def _below(whole, lo, delta):
    if lo == 0:
        return whole + delta
    return whole + jnp.concatenate([jnp.zeros((lo,) + delta.shape[1:], delta.dtype), delta], axis=0)


def _pairs_loop(n_blocks, step, carry):
    def several(i, c):
        for j in range(SB_STEP):
            c = step(SB_STEP * i + j, c)
        return c

    return lax.fori_loop(0, n_blocks // SB_STEP, several, carry)


def _sb_fwd(proj3, rider=None):
    b_dim, s_dim, _ = proj3.shape
    scale = HEAD_DIM ** -0.5
    per = SB_ROWS // BLOCK
    extra, extra_shapes, extra_sems = _rider_parts(rider)
    n_w = len(extra)

    def body(*refs):
        q_ref, k_ref, v_ref = refs[:3]
        o_ref = refs[3 + n_w]
        step = pl.program_id(0) * PAIRS + pl.program_id(1)
        begin, end = _rider_hooks(rider, refs[3:3 + n_w], refs[4 + n_w:4 + 2 * n_w], refs[-2:], step, b_dim * PAIRS)
        begin()
        h1, h2 = _head_masks()
        after, _, qrow, kcol = _sb_consts()

        def qloop(qi, _):
            rows = pl.ds(pl.multiple_of(qi * SB_ROWS, SB_ROWS), SB_ROWS)
            q1, q2 = _split_heads(q_ref[rows, :] * scale, h1, h2)
            first = qi * per

            def block(kb, carry, lo):
                acc, run1, run2 = carry
                krows = pl.ds(pl.multiple_of(kb * BLOCK, BLOCK), BLOCK)
                k16 = k_ref[krows, :].astype(BF16)
                v1, v2 = _split_heads(v_ref[krows, :], h1, h2)
                valid = None if lo is None else kcol[:SB_ROWS - lo] < qrow[:SB_ROWS - lo]
                lo = lo or 0

                def head(qh, vh, run):
                    ls, l1m = _sb_scores(qh[lo:], k16, valid)
                    a = jnp.exp(ls + _split_dot(l1m, after) + run[lo:])
                    if valid is not None:
                        a = jnp.where(valid, a, 0.0)
                    return (jnp.dot(a.astype(BF16), vh, preferred_element_type=F32),
                            _below(run, lo, jnp.sum(l1m, axis=-1, keepdims=True)))

                o1, run1 = head(q1, v1, run1)
                o2, run2 = head(q2, v2, run2)
                return _below(acc, lo, o1 + o2), run1, run2

            zcol = jnp.zeros((SB_ROWS, 1), F32)
            carry = (jnp.zeros((SB_ROWS, LANES), F32), zcol, zcol)
            for kl in reversed(range(per)):
                carry = block(first + kl, carry, kl * BLOCK)
            acc, _, _ = _pairs_loop(first, lambda i, c: block(first - 1 - i, c, None), carry)
            o_ref[rows, :] = acc.astype(BF16)
            return 0

        lax.fori_loop(0, s_dim // SB_ROWS, qloop, 0)
        end()

    def col(c0):
        return pl.BlockSpec((None, s_dim, LANES),lambda b, h: (b, 0, c0 + h))

    res = pl.pallas_call(
        body, name="sb_fwd", grid=(b_dim, PAIRS),
        in_specs=[col(COL_QB), col(COL_KB), col(COL_VB)] + [ANY] * n_w, out_specs=[col(0)] + [ANY] * n_w,
        out_shape=[jax.ShapeDtypeStruct((b_dim, s_dim, ATT_WIDTH), BF16)] + extra_shapes,
        scratch_shapes=extra_sems,
        compiler_params=_params(("arbitrary", "arbitrary")),
    )(proj3, proj3, proj3, *extra)
    return res[0], res[1:]


def _sb_bwd(proj3, do3, rider=None):
    b_dim, s_dim, _ = proj3.shape
    scale = HEAD_DIM ** -0.5
    per = SB_ROWS // BLOCK
    nkb_max = s_dim // BLOCK
    extra, extra_shapes, extra_sems = _rider_parts(rider)
    n_w = len(extra)

    def body(*refs):
        q_ref, k_ref, v_ref, do_ref = refs[:4]
        dq_ref, dk_ref, dv_ref = refs[4 + n_w:7 + n_w]
        dka, dva, e_ref, sg_ref = refs[7 + 2 * n_w:11 + 2 * n_w]
        step = pl.program_id(0) * PAIRS + pl.program_id(1)
        begin, end = _rider_hooks(rider, refs[4:4 + n_w], refs[7 + n_w:7 + 2 * n_w], refs[-2:], step, b_dim * PAIRS)
        begin()
        h1, h2 = _head_masks()
        after, before, qrow, kcol = _sb_consts()
        dka[...] = jnp.zeros_like(dka)
        dva[...] = jnp.zeros_like(dva)

        def qloop(qi, _):
            rows = pl.ds(pl.multiple_of(qi * SB_ROWS, SB_ROWS), SB_ROWS)
            q1, q2 = _split_heads(q_ref[rows, :] * scale, h1, h2)
            do1, do2 = _split_heads(do_ref[rows, :].astype(F32), h1, h2)
            first = qi * per

            def pass1(kb, carry, lo):
                run1, run2 = carry
                krows = pl.ds(pl.multiple_of(kb * BLOCK, BLOCK), BLOCK)
                k16 = k_ref[krows, :].astype(BF16)
                v16 = v_ref[krows, :].astype(BF16)
                valid = None if lo is None else kcol[:SB_ROWS - lo] < qrow[:SB_ROWS - lo]
                lo = lo or 0
                part = pl.ds(lo, SB_ROWS - lo)

                def head(h, qh, doh, run):
                    ls, l1m = _sb_scores(qh[lo:], k16, valid)
                    a = jnp.exp(ls + _split_dot(l1m, after) + run[lo:])
                    if valid is not None:
                        a = jnp.where(valid, a, 0.0)
                    da = lax.dot_general(doh[lo:], v16, NT, preferred_element_type=F32)
                    e_ref[h, kb, part, :] = a * da
                    sg_ref[h, kb, part, :] = jnp.exp(ls)
                    return a.astype(BF16), _below(run, lo, jnp.sum(l1m, axis=-1, keepdims=True))

                a1, run1 = head(0, q1, do1, run1)
                a2, run2 = head(1, q2, do2, run2)
                dva[krows, :] += (lax.dot_general(a1, do1[lo:], TN, preferred_element_type=F32)
                                  + lax.dot_general(a2, do2[lo:], TN, preferred_element_type=F32))
                return run1, run2

            zcol = jnp.zeros((SB_ROWS, 1), F32)
            carry = (zcol, zcol)
            for kl in reversed(range(per)):
                carry = pass1(first + kl, carry, kl * BLOCK)
            _pairs_loop(first, lambda i, c: pass1(first - 1 - i, c, None), carry)

            def pass2(kb, carry, lo):
                dq, pre1, pre2 = carry
                krows = pl.ds(pl.multiple_of(kb * BLOCK, BLOCK), BLOCK)
                k1, k2 = _split_heads(k_ref[krows, :], h1, h2)
                valid = None if lo is None else kcol[:SB_ROWS - lo] < qrow[:SB_ROWS - lo]
                lo = lo or 0
                part = pl.ds(lo, SB_ROWS - lo)

                def head(h, pre):
                    ev = e_ref[h, kb, part, :]
                    sg = sg_ref[h, kb, part, :]
                    dz = ev * (1.0 - sg) - (_split_dot(ev, before) + pre[lo:]) * sg
                    if valid is not None:
                        dz = jnp.where(valid, dz, 0.0)
                    return dz.astype(BF16), _below(pre, lo, jnp.sum(ev, axis=-1, keepdims=True))

                dz1, pre1 = head(0, pre1)
                dz2, pre2 = head(1, pre2)
                dka[krows, :] += (lax.dot_general(dz1, q1[lo:], TN, preferred_element_type=F32)
                                  + lax.dot_general(dz2, q2[lo:], TN, preferred_element_type=F32))
                dq = _below(dq, lo, jnp.dot(dz1, k1, preferred_element_type=F32)
                            + jnp.dot(dz2, k2, preferred_element_type=F32))
                return dq, pre1, pre2

            carry = _pairs_loop(first, lambda i, c: pass2(i, c, None), (jnp.zeros((SB_ROWS, LANES), F32), zcol, zcol))
            for kl in range(per):
                carry = pass2(first + kl, carry, kl * BLOCK)
            dq = carry[0]
            dq_ref[rows, :] = (dq * scale).astype(BF16)
            return 0

        lax.fori_loop(0, s_dim // SB_ROWS, qloop, 0)
        dk_ref[...] = dka[...].astype(BF16)
        dv_ref[...] = dva[...].astype(BF16)
        end()

    def col(c0):
        return pl.BlockSpec((None, s_dim, LANES),lambda b, h: (b, 0, c0 + h))

    shp = jax.ShapeDtypeStruct((b_dim, s_dim, ATT_WIDTH), BF16)
    acc = pltpu.VMEM((s_dim, LANES), F32)
    strip = pltpu.VMEM((2, nkb_max, SB_ROWS, BLOCK), F32)
    res = pl.pallas_call(
        body, name="sb_bwd", grid=(b_dim, PAIRS),
        in_specs=[col(COL_QB), col(COL_KB), col(COL_VB), col(0)] + [ANY] * n_w,
        out_specs=[col(0), col(0), col(0)] + [ANY] * n_w,
        out_shape=[shp, shp, shp] + extra_shapes,
        scratch_shapes=[acc, acc, strip, strip] + extra_sems,
        compiler_params=_params(("arbitrary", "arbitrary")),
    )(proj3, proj3, proj3, do3, *extra)
    return res[:3], res[3:]


def _sigmoid(x):
    return 1.0 / (1.0 + jnp.exp(-x))


def _gate_out_norm(proj, ua, ub, w_out, x, g, *, tt=512):
    t_dim, d = ua.shape

    def body(ga_ref, gb_ref, ua_ref, ub_ref, w_ref, x_ref, g_ref, m_ref, h_ref, n_ref):
        mixed = (_sigmoid(ga_ref[...]) * ua_ref[...] + _sigmoid(gb_ref[...]) * ub_ref[...]).astype(BF16)
        m_ref[...] = mixed
        hv = x_ref[...] + jnp.dot(mixed, w_ref[...], preferred_element_type=F32)
        h_ref[...] = hv
        r = lax.rsqrt(jnp.mean(hv * hv, axis=-1, keepdims=True) + RMS_EPS)
        n_ref[...] = ((hv * r) * g_ref[...]).astype(BF16)

    row = pl.BlockSpec((tt, d), lambda i: (i, 0))
    return pl.pallas_call(
        body, name="gate_out_norm", grid=(t_dim // tt,),
        in_specs=[pl.BlockSpec((tt, d), lambda i: (i, 3)), pl.BlockSpec((tt, d), lambda i: (i, 4)), row, row,
                  pl.BlockSpec((d, d), lambda i: (0, 0)), row, pl.BlockSpec((1, d), lambda i: (0, 0))],
        out_specs=[row, row, row],
        out_shape=[jax.ShapeDtypeStruct((t_dim, d), BF16), jax.ShapeDtypeStruct((t_dim, d), F32),
                   jax.ShapeDtypeStruct((t_dim, d), BF16)],
        compiler_params=_params(("parallel",)),
    )(proj, proj, ua, ub, w_out, x, g)


def _out_dx_gate_bwd(dh, w_out, proj, ua, ub, *, tt=512):
    t_dim, d = ua.shape

    def body(dh_ref, w_ref, ga_ref, gb_ref, ua_ref, ub_ref, dua_ref, dub_ref, dg_ref):
        dm = lax.dot_general(dh_ref[...].astype(BF16), w_ref[...], NT, preferred_element_type=F32)
        sa = _sigmoid(ga_ref[...])
        sb = _sigmoid(gb_ref[...])
        dua_ref[...] = (dm * sa).astype(BF16)
        dub_ref[...] = (dm * sb).astype(BF16)
        dg_ref[:, :d] = (dm * ua_ref[...] * (sa * (1.0 - sa))).astype(BF16)
        dg_ref[:, d:] = (dm * ub_ref[...] * (sb * (1.0 - sb))).astype(BF16)

    row = pl.BlockSpec((tt, d), lambda i: (i, 0))
    wide = pl.BlockSpec((tt, 2 * d), lambda i: (i, 0))
    return pl.pallas_call(
        body, name="out_dx_gate_bwd", grid=(t_dim // tt,),
        in_specs=[row, pl.BlockSpec((d, d), lambda i: (0, 0)),
                  pl.BlockSpec((tt, d), lambda i: (i, 3)), pl.BlockSpec((tt, d), lambda i: (i, 4)), row, row],
        out_specs=[row, row, wide],
        out_shape=[jax.ShapeDtypeStruct((t_dim, d), BF16), jax.ShapeDtypeStruct((t_dim, d), BF16),
                   jax.ShapeDtypeStruct((t_dim, 2 * d), BF16)],
        compiler_params=_params(("parallel",)),
    )(dh, w_out, proj, proj, ua, ub)


def _ffn_up_swiglu(n, wg3, wu3, *, tt=1024):
    t_dim, d = n.shape
    n_s, _, f4 = wg3.shape

    def body(n_ref, wg_ref, wu_ref, g_ref, u_ref, a_ref):
        nv = n_ref[...]
        gv = jnp.dot(nv, wg_ref[...], preferred_element_type=F32)
        uv = jnp.dot(nv, wu_ref[...], preferred_element_type=F32)
        g_ref[...] = gv
        u_ref[...] = uv
        a_ref[...] = (gv * _sigmoid(gv) * uv).astype(BF16)

    wspec = pl.BlockSpec((None, d, f4), lambda i, s: (s, 0, 0))
    ospec = pl.BlockSpec((None, tt, f4), lambda i, s: (s, i, 0))
    shp = (n_s, t_dim, f4)
    return pl.pallas_call(
        body, name="ffn_up_swiglu", grid=(t_dim // tt, n_s),
        in_specs=[pl.BlockSpec((tt, d), lambda i, s: (i, 0)), wspec, wspec], out_specs=[ospec, ospec, ospec],
        out_shape=[jax.ShapeDtypeStruct(shp, F32), jax.ShapeDtypeStruct(shp, F32), jax.ShapeDtypeStruct(shp, BF16)],
        compiler_params=_params(("parallel", "parallel")),
    )(n, wg3, wu3)


def _ffn_down_dx_swiglu(dh, wd3, g3, u3, *, tt=1024):
    t_dim, d = dh.shape
    n_s, f4, _ = wd3.shape

    def body(dh_ref, w_ref, g_ref, u_ref, dg_ref, du_ref):
        da = lax.dot_general(dh_ref[...].astype(BF16), w_ref[...], NT, preferred_element_type=F32)
        gv = g_ref[...]
        sg = _sigmoid(gv)
        dg_ref[...] = (da * u_ref[...] * (sg + gv * sg * (1.0 - sg))).astype(BF16)
        du_ref[...] = (da * (gv * sg)).astype(BF16)

    spec = pl.BlockSpec((None, tt, f4), lambda i, s: (s, i, 0))
    shp = jax.ShapeDtypeStruct((n_s, t_dim, f4), BF16)
    return pl.pallas_call(
        body, name="ffn_down_dx_swiglu", grid=(t_dim // tt, n_s),
        in_specs=[pl.BlockSpec((tt, d), lambda i, s: (i, 0)), pl.BlockSpec((None, f4, d), lambda i, s: (s, 0, 0)),
                  spec, spec],
        out_specs=[spec, spec], out_shape=[shp, shp],
        compiler_params=_params(("parallel", "parallel")),
    )(dh, wd3, g3, u3)


def _mem_fwd(qm, kvm, *, tt=512):
    b_dim, s_dim, _ = qm.shape
    n_mem = kvm.shape[1]
    scale = MEM_HEAD_DIM ** -0.5

    def body(q_ref, k_ref, v_ref, o_ref):
        sc = lax.dot_general(q_ref[0], k_ref[0], NT, preferred_element_type=F32) * scale
        p = jnp.exp(sc - jnp.max(sc, axis=-1, keepdims=True))
        p = p / jnp.sum(p, axis=-1, keepdims=True)
        o_ref[0] = jnp.dot(p.astype(BF16), v_ref[0], preferred_element_type=F32).astype(BF16)

    qs = pl.BlockSpec((1, tt, MEM_HEAD_DIM), lambda b, h, i: (b, i, h))
    return pl.pallas_call(
        body, name="mem_fwd", grid=(b_dim, N_HEADS_MEM, s_dim // tt),
        in_specs=[qs, pl.BlockSpec((1, n_mem, MEM_HEAD_DIM), lambda b, h, i: (b, 0, h)),
                  pl.BlockSpec((1, n_mem, MEM_HEAD_DIM), lambda b, h, i: (b, 0, N_HEADS_MEM + h))],
        out_specs=qs, out_shape=jax.ShapeDtypeStruct(qm.shape, BF16),
        compiler_params=_params(("parallel", "parallel", "parallel")),
    )(qm, kvm, kvm)


def _mem_bwd(qm, kvm, dom, *, tt=512):
    b_dim, s_dim, _ = qm.shape
    n_mem = kvm.shape[1]
    scale = MEM_HEAD_DIM ** -0.5

    def body(q_ref, k_ref, v_ref, do_ref, dq_ref, dk_ref, dv_ref):
        qv, kv, vv, dov = q_ref[0], k_ref[0], v_ref[0], do_ref[0]
        sc = lax.dot_general(qv, kv, NT, preferred_element_type=F32) * scale
        p = jnp.exp(sc - jnp.max(sc, axis=-1, keepdims=True))
        p = p / jnp.sum(p, axis=-1, keepdims=True)
        dp = lax.dot_general(dov, vv, NT, preferred_element_type=F32)
        ds = (p * (dp - jnp.sum(p * dp, axis=-1, keepdims=True)) * scale).astype(BF16)
        dq_ref[0] = jnp.dot(ds, kv, preferred_element_type=F32).astype(BF16)

        @pl.when(pl.program_id(2) == 0)
        def _():
            dk_ref[...] = jnp.zeros_like(dk_ref)
            dv_ref[...] = jnp.zeros_like(dv_ref)

        dk_ref[0] += lax.dot_general(ds, qv, TN, preferred_element_type=F32)
        dv_ref[0] += lax.dot_general(p.astype(BF16), dov, TN, preferred_element_type=F32)

    qs = pl.BlockSpec((1, tt, MEM_HEAD_DIM), lambda b, h, i: (b, i, h))
    ks = pl.BlockSpec((1, n_mem, MEM_HEAD_DIM), lambda b, h, i: (b, 0, h))
    vs = pl.BlockSpec((1, n_mem, MEM_HEAD_DIM), lambda b, h, i: (b, 0, N_HEADS_MEM + h))
    return pl.pallas_call(
        body, name="mem_bwd", grid=(b_dim, N_HEADS_MEM, s_dim // tt),
        in_specs=[qs, ks, vs, qs], out_specs=[qs, ks, ks],
        out_shape=[jax.ShapeDtypeStruct(qm.shape, BF16), jax.ShapeDtypeStruct((b_dim, n_mem, MEM_WIDTH), F32),
                   jax.ShapeDtypeStruct((b_dim, n_mem, MEM_WIDTH), F32)],
        compiler_params=_params(("parallel", "parallel", "arbitrary")),
    )(qm, kvm, kvm, dom)


def _adamw_math(wv, gv, mv, vv):
    nm = ADAM_B1 * mv + (1.0 - ADAM_B1) * gv
    nv = ADAM_B2 * vv + (1.0 - ADAM_B2) * (gv * gv)
    m_hat = nm / (1.0 - ADAM_B1 ** ADAM_STEP)
    v_hat = nv / (1.0 - ADAM_B2 ** ADAM_STEP)
    return -ADAM_LR * (m_hat / (jnp.sqrt(v_hat) + ADAM_EPS) + ADAM_WD * wv), nm, nv


def _adamw(w, g, m, v, *, name):
    rows, cols = w.shape
    tr = _tile(rows, 256, 8)

    def body(w_ref, g_ref, m_ref, v_ref, d_ref, nm_ref, nv_ref):
        d_ref[...], nm_ref[...], nv_ref[...] = _adamw_math(w_ref[...], g_ref[...], m_ref[...], v_ref[...])

    spec = pl.BlockSpec((tr, cols), lambda i: (i, 0))
    shp = jax.ShapeDtypeStruct((rows, cols), F32)
    return pl.pallas_call(
        body, name=name, grid=(rows // tr,),
        in_specs=[spec] * 4, out_specs=[spec] * 3, out_shape=[shp] * 3,
        compiler_params=_params(("parallel",)),
    )(w, g, m, v)


def _prefetch_spec(grid, in_specs, out_specs):
    return pltpu.PrefetchScalarGridSpec(num_scalar_prefetch=1, grid=grid, in_specs=in_specs, out_specs=out_specs)


def _adamw_halves(w, mine, theirs, m, v, c_idx, *, name):
    rows, cols = w.shape
    half = rows // 2
    tr = _tile(half, _row_cap(cols), 8)
    nh = half // tr

    def body(c_ref, w_ref, mine_ref, theirs_ref, m_ref, v_ref, g_ref, d_ref, nm_ref, nv_ref):
        gv = jnp.where(pl.program_id(0) == c_ref[0], mine_ref[...], theirs_ref[...])
        g_ref[...] = gv
        d_ref[...], nm_ref[...], nv_ref[...] = _adamw_math(w_ref[...], gv, m_ref[...], v_ref[...])

    full = pl.BlockSpec((tr, cols), lambda h, i, c_ref: (h * nh + i, 0))
    part = pl.BlockSpec((tr, cols), lambda h, i, c_ref: (i, 0))
    shp = jax.ShapeDtypeStruct((rows, cols), F32)
    return pl.pallas_call(
        body, name=name, grid_spec=_prefetch_spec((2, nh), [full, part, part, full, full], [full] * 4),
        out_shape=[shp] * 4,
        compiler_params=_params(("parallel", "parallel")),
    )(c_idx, w, mine, theirs, m, v)


def _pair_sum(g3, theirs, c_idx, *, name):
    n, rows, cols = g3.shape
    half = rows // 2
    tr = _tile(half, _row_cap(cols), 16)

    def body(c_ref, g_ref, t_ref, o_ref):
        o_ref[...] = (g_ref[...] + t_ref[...]).astype(BF16)

    part = pl.BlockSpec((None, tr, cols), lambda s, i, c_ref: (s, i, 0))
    return pl.pallas_call(
        body, name=name,
        grid_spec=_prefetch_spec((n, half // tr),
                                 [pl.BlockSpec((None, None, tr, cols), lambda s, i, c_ref: (s, c_ref[0], i, 0)), part],
                                 part),
        out_shape=jax.ShapeDtypeStruct((n, half, cols), BF16),
        compiler_params=_params(("parallel", "parallel")),
    )(c_idx, g3.reshape(n, 2, half, cols), theirs)


def _chip_sum(pair, recv, s_idx, *, name):
    _, half, cols = pair.shape
    tr = _tile(half, _row_cap(cols), 16)

    def body(s_ref, p_ref, r_ref, o_ref):
        o_ref[...] = ((p_ref[...].astype(F32) + r_ref[0].astype(F32)) + r_ref[1].astype(F32)) + r_ref[2].astype(F32)

    return pl.pallas_call(
        body, name=name,
        grid_spec=_prefetch_spec((half // tr,),
                                 [pl.BlockSpec((None, tr, cols), lambda i, s_ref: (s_ref[0], i, 0)),
                                  pl.BlockSpec((N_CHIPS - 1, tr, cols), lambda i, s_ref: (0, i, 0))],
                                 pl.BlockSpec((tr, cols), lambda i, s_ref: (i, 0))),
        out_shape=jax.ShapeDtypeStruct((half, cols), F32),
        compiler_params=_params(("parallel",)),
    )(s_idx, pair, recv)


def _sum8(parts):
    n, rows, cols = parts.shape

    def body(p_ref, o_ref):
        acc = p_ref[0]
        for i in range(1, n):
            acc = acc + p_ref[i]
        o_ref[...] = acc

    return pl.pallas_call(
        body, name="small_sum", grid=(1,),
        in_specs=[pl.BlockSpec((n, rows, cols), lambda i: (0, 0, 0))],
        out_specs=pl.BlockSpec((rows, cols), lambda i: (0, 0)),
        out_shape=jax.ShapeDtypeStruct((rows, cols), parts.dtype),
        compiler_params=_params(("arbitrary",)),
    )(parts)


def _place():
    return lax.axis_index("x"), lax.axis_index("y"), lax.axis_index("c")


ANY = pl.BlockSpec(memory_space=pl.ANY)


def _rider_parts(rider):
    if rider is None:
        return (), [], []
    kind, arrays = rider
    n = len(arrays)
    shapes = {"gather": _gathered_shapes, "pair": _pair_shapes, "chip": _chip_shapes}[kind](arrays)
    sems = _gather_sems(n) if kind == "gather" else _exchange_sems(n if kind == "pair" else 3 * n)
    return tuple(arrays), shapes, sems


def _rider_hooks(rider, ins, outs, sems, step, n_steps):
    if rider is None:
        return (lambda: None), (lambda: None)
    if rider[0] == "gather":
        start, forward, finish = _gather_steps(ins, outs, *sems)
    else:
        start, finish = {"pair": _pair_steps, "chip": _chip_steps}[rider[0]](ins, outs, *sems)
        forward = None

    def begin():
        pl.when(step == 0)(start)

    def end():
        if forward is not None:
            pl.when(step == n_steps - 2)(forward)
        pl.when(step == n_steps - 1)(finish)

    return begin, end


def _gather_weights(shards):
    n = len(shards)

    def body(*refs):
        start, forward, finish = _gather_steps(refs[:n], refs[n:2 * n], refs[2 * n], refs[2 * n + 1])
        start()
        forward()
        finish()

    return pl.pallas_call(
        body, name="gather_weights", out_shape=_gathered_shapes(shards),
        in_specs=[ANY] * n, out_specs=[ANY] * n, scratch_shapes=_gather_sems(n),
    )(*shards)


def _gathered_shapes(shards):
    return [jax.ShapeDtypeStruct((N_CHIPS,) + s.shape, s.dtype) for s in shards]


def _gather_sems(n):
    return [pltpu.SemaphoreType.DMA((7 * n,)), pltpu.SemaphoreType.DMA((7 * n,))]


def _gather_steps(ins, outs, send_sems, recv_sems):
    n = len(ins)
    halves = [r.shape[0] // 2 for r in ins]
    x, y, c = _place()
    my_chip = 2 * x + y
    me, sibling = (x, y, c), (x, y, 1 - c)
    chips = [(1 - x, y), (x, 1 - y), (1 - x, 1 - y)]

    def half_of(w, chip, pc):
        return outs[w].at[chip, pl.ds(pc * halves[w], halves[w]), :]

    def copy(w, k, src, dst, to):
        return pltpu.make_async_remote_copy(
            src_ref=src, dst_ref=dst, send_sem=send_sems.at[7 * w + k], recv_sem=recv_sems.at[7 * w + k],
            device_id=to, device_id_type=MESH)

    def firsts():
        cps = []
        for w in range(n):
            cps.append(copy(w, 0, ins[w], outs[w].at[my_chip], sibling))
            mine = ins[w].at[pl.ds(c * halves[w], halves[w]), :]
            for j, (px, py) in enumerate(chips):
                cps.append(copy(w, 1 + j, mine, half_of(w, my_chip, c), (px, py, c)))
        return cps

    def passes():
        return [copy(w, 4 + j, half_of(w, 2 * px + py, c), half_of(w, 2 * px + py, c), sibling)
                for w in range(n) for j, (px, py) in enumerate(chips)]

    def start():
        for cp in firsts():
            cp.start()

    def forward():
        fws = passes()
        for w in range(n):
            for j, (px, py) in enumerate(chips):
                landed = half_of(w, 2 * px + py, c)
                copy(w, 1 + j, landed, landed, me).wait_recv()
                fws[3 * w + j].start()

    def finish():
        for w in range(n):
            copy(w, 0, ins[w], outs[w].at[my_chip], me).wait_recv()
            for j, (px, py) in enumerate(chips):
                landed = half_of(w, 2 * px + py, 1 - c)
                copy(w, 4 + j, landed, landed, me).wait_recv()
        for cp in firsts() + passes():
            cp.wait_send()

    return start, forward, finish


def _pair_shapes(grads):
    return [jax.ShapeDtypeStruct((g.shape[0], g.shape[1] // 2, g.shape[2]), g.dtype) for g in grads]


def _exchange_sems(n):
    return [pltpu.SemaphoreType.DMA((n,)), pltpu.SemaphoreType.DMA((n,))]


def _exchange_steps(copies):
    def start():
        for cp in copies():
            cp.start()

    def finish():
        for cp in copies():
            cp.wait()

    return start, finish


def _pair_steps(ins, outs, send_sems, recv_sems):
    x, y, c = _place()

    def copies():
        return [pltpu.make_async_remote_copy(
            src_ref=ins[w].at[:, pl.ds((1 - c) * (ins[w].shape[1] // 2), ins[w].shape[1] // 2), :], dst_ref=outs[w],
            send_sem=send_sems.at[w], recv_sem=recv_sems.at[w], device_id=(x, y, 1 - c), device_id_type=MESH)
            for w in range(len(ins))]

    return _exchange_steps(copies)


def _chip_shapes(pairs):
    return [jax.ShapeDtypeStruct((N_CHIPS - 1,) + p.shape[1:], p.dtype) for p in pairs]


def _chip_steps(ins, outs, send_sems, recv_sems):
    x, y, c = _place()
    others = [(1 - x, y), (x, 1 - y), (1 - x, 1 - y)]

    def copies():
        return [pltpu.make_async_remote_copy(
            src_ref=ins[w].at[2 * px + py], dst_ref=outs[w].at[j],
            send_sem=send_sems.at[3 * w + j], recv_sem=recv_sems.at[3 * w + j],
            device_id=(px, py, c), device_id_type=MESH)
            for w in range(len(ins)) for j, (px, py) in enumerate(others)]

    return _exchange_steps(copies)


def _swap_halves(mine):
    n = len(mine)

    def body(*refs):
        ins, outs, send_sems, recv_sems = refs[:n], refs[n:2 * n], refs[2 * n], refs[2 * n + 1]
        x, y, c = _place()
        copies = [pltpu.make_async_remote_copy(
            src_ref=ins[w], dst_ref=outs[w], send_sem=send_sems.at[w], recv_sem=recv_sems.at[w],
            device_id=(x, y, 1 - c), device_id_type=MESH) for w in range(n)]
        for cp in copies:
            cp.start()
        for cp in copies:
            cp.wait()

    return pl.pallas_call(
        body, name="grad_swap_halves",
        out_shape=[jax.ShapeDtypeStruct(h.shape, h.dtype) for h in mine],
        in_specs=[ANY] * n, out_specs=[ANY] * n,
        scratch_shapes=[pltpu.SemaphoreType.DMA((n,)), pltpu.SemaphoreType.DMA((n,))],
    )(*mine)


def _gather_small(small):
    srows, cols = small.shape

    def body(s_ref, all_ref, send_sems, recv_sems, local_sem):
        x, y, c = _place()
        me = 4 * x + 2 * y + c
        keep_small = pltpu.make_async_copy(s_ref, all_ref.at[me], local_sem)
        keep_small.start()
        sends = []
        for kk in range(1, 8):
            peer = (x ^ (kk >> 2), y ^ ((kk >> 1) & 1), c ^ (kk & 1))
            sends.append(pltpu.make_async_remote_copy(
                src_ref=s_ref, dst_ref=all_ref.at[me],
                send_sem=send_sems.at[kk], recv_sem=recv_sems.at[kk], device_id=peer, device_id_type=MESH))
        for cp in sends:
            cp.start()
        for kk in range(1, 8):
            px, py, pc = x ^ (kk >> 2), y ^ ((kk >> 1) & 1), c ^ (kk & 1)
            pltpu.make_async_remote_copy(
                src_ref=s_ref, dst_ref=all_ref.at[4 * px + 2 * py + pc],
                send_sem=send_sems.at[kk], recv_sem=recv_sems.at[kk], device_id=(px, py, pc),
                device_id_type=MESH).wait_recv()
        for cp in sends:
            cp.wait_send()
        keep_small.wait()

    return pl.pallas_call(
        body, name="gather_small",
        out_shape=jax.ShapeDtypeStruct((8, srows, cols), small.dtype),
        in_specs=[ANY], out_specs=ANY,
        scratch_shapes=[pltpu.SemaphoreType.DMA((8,)), pltpu.SemaphoreType.DMA((8,)), pltpu.SemaphoreType.DMA],
    )(small)


SHARDED = (("w_in", D_MODEL, IN_COLS, 1), ("w_up_a", ATT_WIDTH, D_MODEL, 1), ("w_up_b", ATT_WIDTH, D_MODEL, 1),
           ("w_out", D_MODEL, D_MODEL, 0), ("w_q_mem", D_MODEL, MEM_WIDTH, 0), ("w_kv_mem", D_MODEL, 2 * MEM_WIDTH, 0),
           ("w_o_mem", MEM_WIDTH, D_MODEL, 1), ("w_ffn_gate", D_MODEL, D_FF, 1), ("w_ffn_up", D_MODEL, D_FF, 1),
           ("w_ffn_down", D_FF, D_MODEL, 0))
NAMES = tuple(n for n, _, _, _ in SHARDED)
EARLY, LATE = NAMES[:1], NAMES[1:]
GAINS = ("g_mix", "g_mem_q", "g_mem_kv", "g_ffn", "g_final")


def _natural(w3):
    n, r, c = w3.shape
    return w3.reshape(n * r, c)


def _shard_major(g, axis):
    if axis == 1:
        return g
    r, c = g.shape
    return g.reshape(N_CHIPS, r // N_CHIPS, c)


def kernel(x, mem, positions, g_mix, w_in, w_up_a, w_up_b, w_out, g_mem_q, g_mem_kv, w_q_mem, w_kv_mem, w_o_mem, g_ffn, w_ffn_gate, w_ffn_up, w_ffn_down, g_final, loss_target, m_g_mix, m_w_in, m_w_up_a, m_w_up_b, m_w_out, m_g_mem_q, m_g_mem_kv, m_w_q_mem, m_w_kv_mem, m_w_o_mem, m_g_ffn, m_w_ffn_gate, m_w_ffn_up, m_w_ffn_down, m_g_final, v_g_mix, v_w_in, v_w_up_a, v_w_up_b, v_w_out, v_g_mem_q, v_g_mem_kv, v_w_q_mem, v_w_kv_mem, v_w_o_mem, v_g_ffn, v_w_ffn_gate, v_w_ffn_up, v_w_ffn_down, v_g_final):
    given = dict(locals())
    shards = {n: given[n][0] for n, _, _, _ in SHARDED}

    wf = dict(zip(EARLY, _gather_weights([shards[n].astype(BF16) for n in EARLY])))
    late_shards = [shards[n].astype(BF16) for n in LATE]
    c_idx = lax.axis_index("c").astype(jnp.int32).reshape(1)
    s_idx = (2 * lax.axis_index("x") + lax.axis_index("y")).astype(jnp.int32).reshape(1)

    loss_row, grad_x, mine, gain_grads = _local_step(x, mem, positions, loss_target, g_mix, g_mem_q, g_mem_kv,
                                                     g_ffn, g_final, wf, late_shards, (c_idx, s_idx))
    return _reduce_and_update(given, shards, loss_row, grad_x, mine, gain_grads, c_idx)


def _reduce_halves(glist, names, c_idx, s_idx, pair_exchange, chip_exchange):
    theirs = pair_exchange(glist)
    pairs = [_pair_sum(g, t, c_idx, name="pair_sum_" + n) for n, g, t in zip(names, glist, theirs)]
    recv = chip_exchange(pairs)
    return [_chip_sum(p, r, s_idx, name="chip_sum_" + n) for n, p, r in zip(names, pairs, recv)]


def _local_step(x, mem, positions, loss_target, g_mix, g_mem_q, g_mem_kv, g_ffn, g_final, wf,
                late_shards=None, place=None):
    b_dim, s_dim, d = x.shape
    t_dim = b_dim * s_dim
    n_mem = mem.shape[1]
    wf = dict(wf)

    xb = x.reshape(t_dim, d)
    tgt = loss_target.reshape(t_dim, d)
    memf = mem.reshape(b_dim * n_mem, d)
    gfin = g_final.reshape(1, d)
    pos = positions.reshape(t_dim, 1).astype(F32)

    lane = jnp.arange(LANES) % HEAD_DIM
    half = ROPE_DIM // 2
    inv_freq = ROPE_THETA ** (-jnp.arange(half, dtype=F32) / half)
    inv_lane = jnp.where(lane < ROPE_DIM, inv_freq[lane % half], 0.0).reshape(1, -1).astype(F32)
    sel_a = (lane < half).astype(F32).reshape(1, -1)
    sel_b = ((lane >= half) & (lane < ROPE_DIM)).astype(F32).reshape(1, -1)

    def rows3(t):
        return t.reshape(b_dim, s_dim, t.shape[-1])

    def rows2(t):
        return t.reshape(t_dim, t.shape[-1])

    n1 = _rms_fwd(xb, g_mix, name="rms_mix")
    proj = _mm_cs(n1, wf["w_in"], name="mm_in")
    proj3 = rows3(proj)
    cs, sn = _rope_table(pos, inv_lane, sel_a, sel_b)
    cs3, sn3 = rows3(cs), rows3(sn)
    (oa16, oa32, lse_a), _ = _dil_fwd(proj3, cs3, sn3, sel_a, sel_b)
    ob16, gathered = _sb_fwd(proj3, ("gather", late_shards) if late_shards else None)
    wf.update(zip(LATE, gathered))
    w_out, w_q, w_kv = _natural(wf["w_out"]), _natural(wf["w_q_mem"]), _natural(wf["w_kv_mem"])
    oa, ob = rows2(oa16), rows2(ob16)
    ua = _mm_cs(oa, wf["w_up_a"], name="mm_up_a")
    ub = _mm_cs(ob, wf["w_up_b"], name="mm_up_b")
    mixed, h1, hn = _gate_out_norm(proj, ua, ub, w_out, xb, g_mem_q)

    memn = _rms_fwd(memf, g_mem_kv, name="rms_mem_kv")
    qm = _mm(hn, w_q, name="mm_q_mem", out_dtype=BF16)
    kvm = _mm(memn, w_kv, name="mm_kv_mem", out_dtype=BF16)
    qm3, kvm3 = rows3(qm), kvm.reshape(b_dim, n_mem, 2 * MEM_WIDTH)
    om = rows2(_mem_fwd(qm3, kvm3))
    h2 = _mm_cs(om, wf["w_o_mem"], name="mm_o_mem", add=h1)

    n3 = _rms_fwd(h2, g_ffn, name="rms_ffn")
    gate3, up3, act3 = _ffn_up_swiglu(n3, wf["w_ffn_gate"], wf["w_ffn_up"])
    h3 = _mm_ffn_down(act3, wf["w_ffn_down"], name="mm_down", add=h2)
    loss_row, dh3, dg_final = _final(h3, gfin, tgt)

    grads = {}
    grads["w_ffn_down"] = _mm_ffn_down_dw(act3, dh3, name="mm_down_dw")
    dgate3, dup3 = _ffn_down_dx_swiglu(dh3, wf["w_ffn_down"], gate3, up3)
    grads["w_ffn_gate"] = _mm_ffn_up_dw(n3, dgate3, name="mm_gate_dw")
    grads["w_ffn_up"] = _mm_ffn_up_dw(n3, dup3, name="mm_up_dw")
    dn3 = _mm_ffn_up_dx(dgate3, wf["w_ffn_gate"], name="mm_gate_dx")
    dn3 = _mm_ffn_up_dx(dup3, wf["w_ffn_up"], name="mm_up_dx", add=dn3)
    dh2, dg_ffn = _rms_bwd(h2, g_ffn, dn3, dh3, name="rms_ffn_bwd")

    dom = _mm_cs_dx(dh2, wf["w_o_mem"], name="mm_o_mem_dx", out_dtype=BF16)
    grads["w_o_mem"] = _mm_cs_dw(om, dh2, name="mm_o_mem_dw")
    dqm, dkm, dvm = _mem_bwd(qm3, kvm3, rows3(dom))
    dqm = rows2(dqm)
    dkvm = jnp.concatenate([dkm, dvm], axis=-1).reshape(b_dim * n_mem, 2 * MEM_WIDTH).astype(BF16)
    grads["w_q_mem"] = _shard_major(_mm(hn, dqm, name="mm_q_mem_dw", ta=True), 0)
    dhn = _mm(dqm, w_q, name="mm_q_mem_dx", tb=True)
    grads["w_kv_mem"] = _shard_major(_mm(memn, dkvm, name="mm_kv_mem_dw", ta=True), 0)
    dmemn = _mm(dkvm, w_kv, name="mm_kv_mem_dx", tb=True)
    _, dg_mem_kv = _rms_bwd(memf, g_mem_kv, dmemn, None, name="rms_mem_kv_bwd")
    dh1, dg_mem_q = _rms_bwd(h1, g_mem_q, dhn, dh2, name="rms_mem_q_bwd")

    grads["w_out"] = _shard_major(_mm(mixed, dh1, name="mm_out_dw", ta=True), 0)
    dua, dub, dgates = _out_dx_gate_bwd(dh1, w_out, proj, ua, ub)
    doa = _mm_cs_dx(dua, wf["w_up_a"], name="mm_up_a_dx")
    grads["w_up_a"] = _mm_cs_dw(oa, dua, name="mm_up_a_dw")
    dob = _mm_cs_dx(dub, wf["w_up_b"], name="mm_up_b_dx", out_dtype=BF16)
    grads["w_up_b"] = _mm_cs_dw(ob, dub, name="mm_up_b_dw")

    att = {}

    def dil_with_pairs(glist):
        att["a"], theirs = _dil_bwd(proj3, cs3, sn3, sel_a, sel_b, rows3(doa), oa32, lse_a,
                                    ("pair", glist) if glist else None)
        return theirs

    def sb_with_chips(pairs):
        att["b"], recv = _sb_bwd(proj3, rows3(dob), ("chip", pairs) if pairs else None)
        return recv

    if place is None:
        dil_with_pairs(())
        sb_with_chips(())
    else:
        mine_late = _reduce_halves([grads[n] for n in LATE], LATE, *place, dil_with_pairs, sb_with_chips)
    dproj = jnp.concatenate([rows2(t) for t in att["a"] + att["b"]] + [dgates], axis=1)
    grads["w_in"] = _mm_cs_dw(n1, dproj, name="mm_in_dw")
    if place is None:
        dn1 = _mm_cs_dx(dproj, wf["w_in"], name="mm_in_dx")
        dx, dg_mix = _rms_bwd(xb, g_mix, dn1, dh1, name="rms_mix_bwd")
    else:
        tail = {}

        def dx_with_pairs(glist):
            tail["dn1"], theirs = _mm_cs_dx(dproj, wf["w_in"], name="mm_in_dx", rider=("pair", glist))
            return theirs

        def rms_with_chips(pairs):
            tail["dx"], tail["dg"], recv = _rms_bwd(xb, g_mix, tail["dn1"], dh1, name="rms_mix_bwd",
                                                    rider=("chip", pairs))
            return recv

        mine_early = _reduce_halves([grads[n] for n in EARLY], EARLY, *place, dx_with_pairs, rms_with_chips)
        dx, dg_mix = tail["dx"], tail["dg"]
    grad_x = dx.reshape(b_dim, s_dim, d)
    gains = (dg_mix, dg_mem_q, dg_mem_kv, dg_ffn, dg_final)
    if place is None:
        return loss_row, grad_x, grads, gains
    return loss_row, grad_x, mine_early + mine_late, gains


def _reduce_and_update(given, shards, loss_row, grad_x, mine, gain_grads, c_idx):
    d = D_MODEL
    dg_mix, dg_mem_q, dg_mem_kv, dg_ffn, dg_final = gain_grads
    small = jnp.concatenate([dg_mix, dg_mem_q, dg_mem_kv, dg_ffn, dg_final,
                             jnp.pad(loss_row, ((0, 0), (0, FLAT_COLS - LANES))), jnp.zeros((2, FLAT_COLS), F32)], axis=0)
    small_all = _gather_small(small)
    others = _swap_halves(mine)
    small_sum = _sum8(small_all)
    loss = small_sum[5, 0]

    out_g, out_d, out_m, out_v = {}, {}, {}, {}
    for n, mine_n, other_n in zip(NAMES, mine, others):
        g2, dl, nm, nv = _adamw_halves(shards[n], mine_n, other_n, given["m_" + n][0], given["v_" + n][0], c_idx,
                                       name="adamw_" + n)
        out_g[n], out_d[n], out_m[n], out_v[n] = g2[None], dl[None], nm[None], nv[None]
    gain_w = jnp.concatenate([given[n].reshape(1, d) for n in GAINS], axis=0)
    gain_m = jnp.concatenate([given["m_" + n].reshape(1, d) for n in GAINS], axis=0)
    gain_v = jnp.concatenate([given["v_" + n].reshape(1, d) for n in GAINS], axis=0)
    gain_g = small_sum[:len(GAINS)]
    gd, gm, gv = _adamw(gain_w, gain_g, gain_m, gain_v, name="adamw_gains")
    for i, n in enumerate(GAINS):
        shape = given[n].shape
        out_g[n], out_d[n] = gain_g[i].reshape(shape), gd[i].reshape(shape)
        out_m[n], out_v[n] = gm[i].reshape(shape), gv[i].reshape(shape)

    order = ["g_mix", "w_in", "w_up_a", "w_up_b", "w_out", "g_mem_q", "g_mem_kv", "w_q_mem", "w_kv_mem", "w_o_mem",
             "g_ffn", "w_ffn_gate", "w_ffn_up", "w_ffn_down", "g_final"]
    return (loss, grad_x, *[out_g[n] for n in order], *[out_d[n] for n in order],
            *[out_m[n] for n in order], *[out_v[n] for n in order])
```

```python
import jax
import jax.numpy as jnp
from jax import lax
from jax.experimental import pallas as pl
from jax.experimental.pallas import tpu as pltpu

F32 = jnp.float32
BF16 = jnp.bfloat16
MESH = pl.DeviceIdType.MESH

D_MODEL = 1024
HEAD_DIM = 64
N_HEADS = 8
ATT_WIDTH = N_HEADS * HEAD_DIM
DIL_PATTERNS = ((128, 1), (512, 4), (2048, 16))
BLOCK = 128
SB_ROWS = 1024
SB_STEP = 4
ROPE_THETA = 500000.0
ROPE_DIM = HEAD_DIM // 4
N_HEADS_MEM = 4
MEM_HEAD_DIM = 128
MEM_WIDTH = N_HEADS_MEM * MEM_HEAD_DIM
D_FF = 2816
IN_COLS = 6 * ATT_WIDTH + 2 * D_MODEL
RMS_EPS = 1e-6
ADAM_LR = 0.001
ADAM_B1 = 0.9
ADAM_B2 = 0.999
ADAM_EPS = 1e-08
ADAM_WD = 0.01
ADAM_STEP = 10

N_CHIPS = 4
LANES = 128
FLAT_COLS = 1024
VMEM_LIMIT = 56 * 1024 * 1024

PAIRS = ATT_WIDTH // LANES
COL_QA, COL_KA, COL_VA, COL_QB, COL_KB, COL_VB = (i * PAIRS for i in range(6))

MM_CAP = 1408
TOK_CAP = 2048
NN = (((1,), (0,)), ((), ()))
NT = (((1,), (1,)), ((), ()))
TN = (((0,), (0,)), ((), ()))
BNN = (((2,), (1,)), ((0,), (0,)))
BNT = (((2,), (2,)), ((0,), (0,)))
BTN = (((1,), (1,)), ((0,), (0,)))
DIL_BATCH = 8


def _tile(dim, cap, unit=LANES):
    if dim <= cap:
        return dim
    best = None
    for t in range(unit, cap + 1, unit):
        if dim % t == 0:
            best = t
    assert best is not None, (dim, cap)
    return best


def _row_cap(cols):
    return max(256, (1 << 18) // cols)


def _params(sem):
    return pltpu.CompilerParams(dimension_semantics=sem, vmem_limit_bytes=VMEM_LIMIT)


def _mm(a, b, *, name, ta=False, tb=False, add=None, out_dtype=F32,
        tm_cap=MM_CAP, tn_cap=MM_CAP, tk_cap=MM_CAP):
    if ta:
        k_dim, m_dim = a.shape
    else:
        m_dim, k_dim = a.shape
    if tb:
        n_dim, kb = b.shape
    else:
        kb, n_dim = b.shape
    assert kb == k_dim, (a.shape, b.shape, ta, tb)
    tm, tn, tk = _tile(m_dim, tm_cap), _tile(n_dim, tn_cap), _tile(k_dim, tk_cap)
    nk = k_dim // tk
    dims = (((0 if ta else 1,), (1 if tb else 0,)), ((), ()))
    has_add = add is not None

    def body(*refs):
        if has_add:
            a_ref, b_ref, add_ref, o_ref = refs[:4]
        else:
            a_ref, b_ref, o_ref = refs[:3]
        part = lax.dot_general(a_ref[...].astype(BF16), b_ref[...].astype(BF16), dims, preferred_element_type=F32)

        def finish(r):
            if has_add:
                r = add_ref[...] + r
            o_ref[...] = r.astype(out_dtype)

        if nk == 1:
            finish(part)
            return
        acc_ref = refs[-1]
        k = pl.program_id(2)

        @pl.when(k == 0)
        def _():
            acc_ref[...] = part

        @pl.when(k > 0)
        def _():
            acc_ref[...] += part

        @pl.when(k == nk - 1)
        def _():
            finish(acc_ref[...])

    a_spec = pl.BlockSpec((tk, tm), lambda i, j, k: (k, i)) if ta else pl.BlockSpec((tm, tk), lambda i, j, k: (i, k))
    b_spec = pl.BlockSpec((tn, tk), lambda i, j, k: (j, k)) if tb else pl.BlockSpec((tk, tn), lambda i, j, k: (k, j))
    o_spec = pl.BlockSpec((tm, tn), lambda i, j, k: (i, j))
    in_specs = [a_spec, b_spec] + ([o_spec] if has_add else [])
    args = (a, b) + ((add,) if has_add else ())
    return pl.pallas_call(
        body, name=name, grid=(m_dim // tm, n_dim // tn, nk),
        in_specs=in_specs, out_specs=o_spec,
        out_shape=jax.ShapeDtypeStruct((m_dim, n_dim), out_dtype),
        scratch_shapes=[pltpu.VMEM((tm, tn), F32)] if nk > 1 else [],
        compiler_params=_params(("parallel", "parallel", "arbitrary")),
    )(*args)


def _mm_core(name, a, b, a_spec, b_spec, o_spec, out_shape, grid, dims, *, add=None, out_dtype=F32, rider=None):
    nk = grid[2]
    has_add = add is not None
    n_in = 3 if has_add else 2
    acc_shape = tuple(d for d in o_spec.block_shape if d is not None)
    extra, extra_shapes, extra_sems = _rider_parts(rider)
    n_w = len(extra)

    def body(*refs):
        a_ref, b_ref = refs[:2]
        o_ref = refs[n_in + n_w]
        step = (pl.program_id(0) * grid[1] + pl.program_id(1)) * nk + pl.program_id(2)
        begin, end = _rider_hooks(rider, refs[n_in:n_in + n_w], refs[n_in + n_w + 1:n_in + 2 * n_w + 1], refs[-2:],
                                  step, grid[0] * grid[1] * nk)
        begin()
        part = lax.dot_general(a_ref[...].astype(BF16), b_ref[...].astype(BF16), dims, preferred_element_type=F32)

        def finish(r):
            if has_add:
                r = refs[2][...] + r
            o_ref[...] = r.astype(out_dtype)

        if nk == 1:
            finish(part)
        else:
            acc_ref = refs[n_in + 2 * n_w + 1]
            k = pl.program_id(2)

            @pl.when(k == 0)
            def _():
                acc_ref[...] = part

            @pl.when(k > 0)
            def _():
                acc_ref[...] += part

            @pl.when(k == nk - 1)
            def _():
                finish(acc_ref[...])
        end()

    in_specs = [a_spec, b_spec] + ([o_spec] if has_add else []) + [ANY] * n_w
    args = (a, b) + ((add,) if has_add else ()) + extra
    res = pl.pallas_call(
        body, name=name, grid=grid, in_specs=in_specs, out_specs=[o_spec] + [ANY] * n_w,
        out_shape=[jax.ShapeDtypeStruct(out_shape, out_dtype)] + extra_shapes,
        scratch_shapes=([pltpu.VMEM(acc_shape, F32)] if nk > 1 else []) + extra_sems,
        compiler_params=_params(("arbitrary",) * 3 if n_w else ("parallel", "parallel", "arbitrary")),
    )(*args)
    return (res[0], res[1:]) if n_w else res[0]


def _mm_cs(a, w3, *, name, add=None, out_dtype=F32):
    m_dim, k_dim = a.shape
    _, _, n4 = w3.shape
    tm, tn, tk = _tile(m_dim, MM_CAP if add is not None else TOK_CAP), _tile(n4, MM_CAP), _tile(k_dim, MM_CAP)
    npb = n4 // tn
    return _mm_core(name, a, w3,
                    pl.BlockSpec((tm, tk), lambda i, j, k: (i, k)),
                    pl.BlockSpec((None, tk, tn), lambda i, j, k: (j // npb, k, j % npb)),
                    pl.BlockSpec((tm, tn), lambda i, j, k: (i, j)),
                    (m_dim, N_CHIPS * n4), (m_dim // tm, N_CHIPS * npb, k_dim // tk), NN, add=add, out_dtype=out_dtype)


def _mm_cs_dx(dy, w3, *, name, out_dtype=F32, rider=None):
    m_dim, _ = dy.shape
    _, k_dim, n4 = w3.shape
    tm, tkw, tn = _tile(m_dim, MM_CAP), _tile(k_dim, MM_CAP), _tile(n4, MM_CAP)
    npb = n4 // tn
    return _mm_core(name, dy, w3,
                    pl.BlockSpec((tm, tn), lambda i, j, k: (i, k)),
                    pl.BlockSpec((None, tkw, tn), lambda i, j, k: (k // npb, j, k % npb)),
                    pl.BlockSpec((tm, tkw), lambda i, j, k: (i, j)),
                    (m_dim, k_dim), (m_dim // tm, k_dim // tkw, N_CHIPS * npb), NT, out_dtype=out_dtype, rider=rider)


def _mm_cs_dw(a, dy, *, name):
    m_dim, k_dim = a.shape
    n4 = dy.shape[1] // N_CHIPS
    tmk, tn, tk = _tile(k_dim, MM_CAP), _tile(n4, MM_CAP), _tile(m_dim, TOK_CAP)
    npb = n4 // tn
    return _mm_core(name, a, dy,
                    pl.BlockSpec((tk, tmk), lambda i, j, k: (k, i)),
                    pl.BlockSpec((tk, tn), lambda i, j, k: (k, j)),
                    pl.BlockSpec((None, tmk, tn), lambda i, j, k: (j // npb, i, j % npb)),
                    (N_CHIPS, k_dim, n4), (k_dim // tmk, N_CHIPS * npb, m_dim // tk), TN)


def _mm_ffn_up_dw(n, d3, *, name):
    t_dim, d = n.shape
    _, _, f4 = d3.shape
    tk = _tile(t_dim, TOK_CAP)
    return _mm_core(name, n, d3,
                    pl.BlockSpec((tk, d), lambda i, j, k: (k, 0)),
                    pl.BlockSpec((None, tk, f4), lambda i, j, k: (j, k, 0)),
                    pl.BlockSpec((None, d, f4), lambda i, j, k: (j, 0, 0)),
                    (N_CHIPS, d, f4), (1, N_CHIPS, t_dim // tk), TN)


def _mm_ffn_up_dx(d3, w3, *, name, add=None):
    _, t_dim, f4 = d3.shape
    _, d, _ = w3.shape
    tm = _tile(t_dim, MM_CAP)
    return _mm_core(name, d3, w3,
                    pl.BlockSpec((None, tm, f4), lambda i, j, k: (k, i, 0)),
                    pl.BlockSpec((None, d, f4), lambda i, j, k: (k, 0, 0)),
                    pl.BlockSpec((tm, d), lambda i, j, k: (i, 0)),
                    (t_dim, d), (t_dim // tm, 1, N_CHIPS), NT, add=add)


def _mm_ffn_down_dw(act3, dh, *, name):
    _, t_dim, f4 = act3.shape
    d = dh.shape[1]
    tk = _tile(t_dim, TOK_CAP)
    return _mm_core(name, act3, dh,
                    pl.BlockSpec((None, tk, f4), lambda i, j, k: (i, k, 0)),
                    pl.BlockSpec((tk, d), lambda i, j, k: (k, 0)),
                    pl.BlockSpec((None, f4, d), lambda i, j, k: (i, 0, 0)),
                    (N_CHIPS, f4, d), (N_CHIPS, 1, t_dim // tk), TN)


def _rms_fwd(x, g, *, name, tt=512):
    t_dim, d = x.shape
    tt = _tile(t_dim, tt, 8)

    def body(x_ref, g_ref, o_ref):
        xv = x_ref[...]
        r = lax.rsqrt(jnp.mean(xv * xv, axis=-1, keepdims=True) + RMS_EPS)
        o_ref[...] = ((xv * r) * g_ref[...]).astype(o_ref.dtype)

    return pl.pallas_call(
        body, name=name, grid=(t_dim // tt,),
        in_specs=[pl.BlockSpec((tt, d), lambda i: (i, 0)), pl.BlockSpec((1, d), lambda i: (0, 0))],
        out_specs=pl.BlockSpec((tt, d), lambda i: (i, 0)),
        out_shape=jax.ShapeDtypeStruct((t_dim, d), BF16),
        compiler_params=_params(("parallel",)),
    )(x, g)


def _rms_bwd(x, g, dy, add, *, name, tt=512, rider=None):
    t_dim, d = x.shape
    tt = _tile(t_dim, tt, 8)
    has_add = add is not None
    n_in = 4 if has_add else 3
    extra, extra_shapes, extra_sems = _rider_parts(rider)
    n_w = len(extra)

    def body(*refs):
        x_ref, g_ref, dy_ref = refs[:3]
        add_ref = refs[3] if has_add else None
        dx_ref, dg_ref = refs[n_in + n_w:n_in + n_w + 2]
        begin, end = _rider_hooks(rider, refs[n_in:n_in + n_w], refs[n_in + n_w + 2:n_in + 2 * n_w + 2], refs[-2:],
                                  pl.program_id(0), t_dim // tt)
        begin()
        xv = x_ref[...]
        dyv = dy_ref[...].astype(F32)
        r = lax.rsqrt(jnp.mean(xv * xv, axis=-1, keepdims=True) + RMS_EPS)
        xh = xv * r
        u = dyv * g_ref[...]
        dx = r * (u - xh * jnp.mean(u * xh, axis=-1, keepdims=True))
        if has_add:
            dx = add_ref[...] + dx
        dx_ref[...] = dx

        @pl.when(pl.program_id(0) == 0)
        def _():
            dg_ref[...] = jnp.zeros_like(dg_ref)

        dg_ref[...] += jnp.sum(dyv * xh, axis=0, keepdims=True)
        end()

    row = pl.BlockSpec((tt, d), lambda i: (i, 0))
    vec = pl.BlockSpec((1, d), lambda i: (0, 0))
    in_specs = [row, vec, row] + ([row] if has_add else []) + [ANY] * n_w
    args = (x, g, dy) + ((add,) if has_add else ()) + extra
    res = pl.pallas_call(
        body, name=name, grid=(t_dim // tt,),
        in_specs=in_specs, out_specs=[row, vec] + [ANY] * n_w,
        out_shape=[jax.ShapeDtypeStruct((t_dim, d), F32), jax.ShapeDtypeStruct((1, d), F32)] + extra_shapes,
        scratch_shapes=extra_sems,
        compiler_params=_params(("arbitrary",)),
    )(*args)
    return (res[0], res[1], res[2:]) if n_w else (res[0], res[1])


def _down_final(act3, wd3, h, g, target, *, tt=1024):
    n_s, t_dim, f4 = act3.shape
    d = h.shape[1]
    n_steps = t_dim // tt

    def body(a_ref, w_ref, h_ref, g_ref, t_ref, loss_ref, dh_ref, dg_ref, acc_ref, sq_ref):
        i, k = pl.program_id(0), pl.program_id(1)
        part = jnp.dot(a_ref[...], w_ref[...], preferred_element_type=F32)

        @pl.when(k == 0)
        def _():
            acc_ref[...] = part

        @pl.when(k > 0)
        def _():
            acc_ref[...] += part

        @pl.when(jnp.logical_and(i == 0, k == 0))
        def _():
            dg_ref[...] = jnp.zeros_like(dg_ref)
            sq_ref[...] = jnp.zeros_like(sq_ref)

        @pl.when(k == n_s - 1)
        def _():
            xv = h_ref[...] + acc_ref[...]
            gv = g_ref[...]
            r = lax.rsqrt(jnp.mean(xv * xv, axis=-1, keepdims=True) + RMS_EPS)
            xh = xv * r
            err = xh * gv - t_ref[...]
            dyv = err * (1.0 / d)
            u = dyv * gv
            dh_ref[...] = r * (u - xh * jnp.mean(u * xh, axis=-1, keepdims=True))
            dg_ref[...] += jnp.sum(dyv * xh, axis=0, keepdims=True)
            sq_ref[...] += jnp.sum(err * err, axis=0, keepdims=True)

        @pl.when(jnp.logical_and(i == n_steps - 1, k == n_s - 1))
        def _():
            total = jnp.sum(sq_ref[...], axis=-1, keepdims=True) * (0.5 / d)
            loss_ref[...] = jnp.broadcast_to(total, loss_ref.shape)

    row = pl.BlockSpec((tt, d), lambda i, k: (i, 0))
    vec = pl.BlockSpec((1, d), lambda i, k: (0, 0))
    return pl.pallas_call(
        body, name="down_final_loss", grid=(n_steps, n_s),
        in_specs=[pl.BlockSpec((None, tt, f4), lambda i, k: (k, i, 0)),
                  pl.BlockSpec((None, f4, d), lambda i, k: (k, 0, 0)), row, vec, row],
        out_specs=[pl.BlockSpec((1, LANES), lambda i, k: (0, 0)), row, vec],
        out_shape=[jax.ShapeDtypeStruct((1, LANES), F32), jax.ShapeDtypeStruct((t_dim, d), F32),
                   jax.ShapeDtypeStruct((1, d), F32)],
        scratch_shapes=[pltpu.VMEM((tt, d), F32), pltpu.VMEM((1, d), F32)],
        compiler_params=_params(("arbitrary", "arbitrary")),
    )(act3, wd3, h, g, target)


def _rope_table(pos, inv_lane, sel_a, sel_b, *, tt=512):
    t_dim = pos.shape[0]

    def body(p_ref, f_ref, a_ref, b_ref, c_ref, s_ref):
        ang = p_ref[...] * f_ref[...]
        on = (a_ref[...] + b_ref[...]) > 0.0
        c_ref[...] = jnp.where(on, jnp.cos(ang), 1.0)
        s_ref[...] = jnp.where(on, jnp.sin(ang), 0.0)

    vec = pl.BlockSpec((1, LANES), lambda i: (0, 0))
    row = pl.BlockSpec((tt, LANES), lambda i: (i, 0))
    shp = jax.ShapeDtypeStruct((t_dim, LANES), F32)
    return pl.pallas_call(
        body, name="rope_table", grid=(t_dim // tt,),
        in_specs=[pl.BlockSpec((tt, 1), lambda i: (i, 0)), vec, vec, vec],
        out_specs=[row, row], out_shape=[shp, shp],
        compiler_params=_params(("parallel",)),
    )(pos, inv_lane, sel_a, sel_b)


def _rotate(xv, cs, sn, sa, sb):
    half = ROPE_DIM // 2
    up = pltpu.roll(xv, LANES - half, 1)
    dn = pltpu.roll(xv, half, 1)
    return xv * cs + (dn * sb - up * sa) * sn


def _head_masks():
    h1 = lax.broadcasted_iota(jnp.int32, (1, LANES), 1) < HEAD_DIM
    return h1, jnp.logical_not(h1)


def _split_heads(xv, h1, h2):
    return jnp.where(h1, xv, 0.0).astype(BF16), jnp.where(h2, xv, 0.0).astype(BF16)


def _tri_masks():
    r = lax.broadcasted_iota(jnp.int32, (BLOCK, BLOCK), 0)
    c = lax.broadcasted_iota(jnp.int32, (BLOCK, BLOCK), 1)
    return c <= r, r <= c


def _stream_rows(start, dil):
    if dil == 1:
        return pl.ds(pl.multiple_of(start, BLOCK), BLOCK)
    return pl.ds(start, BLOCK, stride=dil)


def _dil_tile(idx, dil, nb):
    r = idx // nb
    n = idx % nb
    return (_stream_rows(r + dil * BLOCK * n, dil), _stream_rows(r + dil * BLOCK * jnp.maximum(n - 1, 0), dil),
            n > 0)


def _dil_specs(b_dim, s_dim):
    def col(c0):
        return pl.BlockSpec((None, s_dim, LANES),lambda b, h: (b, 0, c0 + h))
    tab = pl.BlockSpec((None, s_dim, LANES),lambda b, h: (b, 0, 0))
    vec = pl.BlockSpec((1, LANES), lambda b, h: (0, 0))
    return col, tab, vec


def _dil_fwd(proj3, cs3, sn3, sel_a, sel_b, rider=None):
    b_dim, s_dim, _ = proj3.shape
    scale = HEAD_DIM ** -0.5
    n_pat = len(DIL_PATTERNS)
    extra, extra_shapes, extra_sems = _rider_parts(rider)
    n_w = len(extra)
    n_steps = b_dim * PAIRS

    def body(*refs):
        q_ref, k_ref, v_ref, cs_ref, sn_ref, sa_ref, sb_ref = refs[:7]
        o16_ref, o32_ref, l_ref = refs[7 + n_w:10 + n_w]
        qr, kr = refs[10 + 2 * n_w:12 + 2 * n_w]
        per_pattern = refs[12 + 2 * n_w:12 + 2 * n_w + 2 * n_pat]
        og, lg = per_pattern[:n_pat], per_pattern[n_pat:]
        step = pl.program_id(0) * PAIRS + pl.program_id(1)
        begin, end = _rider_hooks(rider, refs[7:7 + n_w], refs[10 + n_w:10 + 2 * n_w], refs[-2:], step, n_steps)
        begin()
        h1, h2 = _head_masks()
        cur_ok, prev_ok = _tri_masks()
        sa, sb = sa_ref[...], sb_ref[...]

        def prep(j, _):
            rows = pl.ds(pl.multiple_of(j * BLOCK, BLOCK), BLOCK)
            cs, sn = cs_ref[rows, :], sn_ref[rows, :]
            qr[rows, :] = _rotate(q_ref[rows, :], cs, sn, sa, sb) * scale
            kr[rows, :] = _rotate(k_ref[rows, :], cs, sn, sa, sb)
            return 0

        lax.fori_loop(0, s_dim // BLOCK, prep, 0)

        for g, (_, dil) in enumerate(DIL_PATTERNS):
            nb = s_dim // dil // BLOCK

            def some(bi, _, g=g, dil=dil, nb=nb):
                tiles = [_dil_tile(bi * DIL_BATCH + t, dil, nb) for t in range(DIL_BATCH)]
                rows = [t[0] for t in tiles]
                q1, q2 = _split_heads(jnp.stack([qr[rw, :] for rw in rows]), h1, h2)
                kc = jnp.stack([kr[rw, :] for rw in rows]).astype(BF16)
                vc1, vc2 = _split_heads(jnp.stack([v_ref[rw, :] for rw in rows]), h1, h2)
                if nb > 1:
                    kp = jnp.stack([kr[t[1], :] for t in tiles]).astype(BF16)
                    vp1, vp2 = _split_heads(jnp.stack([v_ref[t[1], :] for t in tiles]), h1, h2)
                    p_ok = jnp.stack([jnp.logical_and(prev_ok, t[2]) for t in tiles])

                def head(qh, vch, vph):
                    sc = jnp.where(cur_ok, lax.dot_general(qh, kc, BNT, preferred_element_type=F32), -jnp.inf)
                    m = jnp.max(sc, axis=-1, keepdims=True)
                    if nb > 1:
                        sp = jnp.where(p_ok, lax.dot_general(qh, kp, BNT, preferred_element_type=F32), -jnp.inf)
                        m = jnp.maximum(m, jnp.max(sp, axis=-1, keepdims=True))
                    pc = jnp.exp(sc - m)
                    den = jnp.sum(pc, axis=-1, keepdims=True)
                    acc = lax.dot_general(pc.astype(BF16), vch, BNN, preferred_element_type=F32)
                    if nb > 1:
                        pp = jnp.exp(sp - m)
                        den = den + jnp.sum(pp, axis=-1, keepdims=True)
                        acc = acc + lax.dot_general(pp.astype(BF16), vph, BNN, preferred_element_type=F32)
                    return acc / den, m + jnp.log(den)

                o1, l1 = head(q1, vc1, vp1 if nb > 1 else None)
                o2, l2 = head(q2, vc2, vp2 if nb > 1 else None)
                o, l = o1 + o2, jnp.where(h1, l1, l2)
                for t, rw in enumerate(rows):
                    og[g][rw, :] = o[t]
                    lg[g][rw, :] = l[t]
                return 0

            lax.fori_loop(0, dil * nb // DIL_BATCH, some, 0)

        def comb(j, _):
            rows = pl.ds(pl.multiple_of(j * BLOCK, BLOCK), BLOCK)
            ls = [lg[g][rows, :] for g in range(n_pat)]
            m = jnp.maximum(jnp.maximum(ls[0], ls[1]), ls[2])
            es = [jnp.exp(l - m) for l in ls]
            den = es[0] + es[1] + es[2]
            o = (es[0] * og[0][rows, :] + es[1] * og[1][rows, :] + es[2] * og[2][rows, :]) / den
            o16_ref[rows, :] = o.astype(BF16)
            o32_ref[rows, :] = o
            l_ref[rows, :] = m + jnp.log(den)
            return 0

        lax.fori_loop(0, s_dim // BLOCK, comb, 0)
        end()

    col, tab, vec = _dil_specs(b_dim, s_dim)
    out = pl.BlockSpec((None, s_dim, LANES),lambda b, h: (b, 0, h))
    shp = (b_dim, s_dim, ATT_WIDTH)
    res = pl.pallas_call(
        body, name="dil_fwd", grid=(b_dim, PAIRS),
        in_specs=[col(COL_QA), col(COL_KA), col(COL_VA), tab, tab, vec, vec] + [ANY] * n_w,
        out_specs=[out, out, out] + [ANY] * n_w,
        out_shape=[jax.ShapeDtypeStruct(shp, BF16), jax.ShapeDtypeStruct(shp, F32), jax.ShapeDtypeStruct(shp, F32)]
        + extra_shapes,
        scratch_shapes=[pltpu.VMEM((s_dim, LANES), F32)] * (2 + 2 * n_pat) + extra_sems,
        compiler_params=_params(("arbitrary", "arbitrary")),
    )(proj3, proj3, proj3, cs3, sn3, sel_a, sel_b, *extra)
    return res[:3], res[3:]


def _dil_bwd(proj3, cs3, sn3, sel_a, sel_b, do3, o3, lse3, rider=None):
    b_dim, s_dim, _ = proj3.shape
    scale = HEAD_DIM ** -0.5
    extra, extra_shapes, extra_sems = _rider_parts(rider)
    n_w = len(extra)

    def body(*refs):
        q_ref, k_ref, v_ref, cs_ref, sn_ref, sa_ref, sb_ref, do_ref, o_ref, l_ref = refs[:10]
        dq_ref, dk_ref, dv_ref = refs[10 + n_w:13 + n_w]
        qr, kr, dqa, dka, dva = refs[13 + 2 * n_w:18 + 2 * n_w]
        step = pl.program_id(0) * PAIRS + pl.program_id(1)
        begin, end = _rider_hooks(rider, refs[10:10 + n_w], refs[13 + n_w:13 + 2 * n_w], refs[-2:], step,
                                  b_dim * PAIRS)
        begin()
        h1, h2 = _head_masks()
        cur_ok, prev_ok = _tri_masks()
        sa, sb = sa_ref[...], sb_ref[...]

        def prep(j, _):
            rows = pl.ds(pl.multiple_of(j * BLOCK, BLOCK), BLOCK)
            cs, sn = cs_ref[rows, :], sn_ref[rows, :]
            qr[rows, :] = _rotate(q_ref[rows, :], cs, sn, sa, sb) * scale
            kr[rows, :] = _rotate(k_ref[rows, :], cs, sn, sa, sb)
            zero = jnp.zeros((BLOCK, LANES), F32)
            dqa[rows, :] = zero
            dka[rows, :] = zero
            dva[rows, :] = zero
            return 0

        lax.fori_loop(0, s_dim // BLOCK, prep, 0)

        for _, dil in DIL_PATTERNS:
            nb = s_dim // dil // BLOCK

            def some(bi, _, dil=dil, nb=nb):
                tiles = [_dil_tile(bi * DIL_BATCH + t, dil, nb) for t in range(DIL_BATCH)]
                rows = [t[0] for t in tiles]
                q1, q2 = _split_heads(jnp.stack([qr[rw, :] for rw in rows]), h1, h2)
                dof = jnp.stack([do_ref[rw, :] for rw in rows])
                do1, do2 = _split_heads(dof, h1, h2)
                prod = dof * jnp.stack([o_ref[rw, :] for rw in rows])
                delta1 = jnp.sum(jnp.where(h1, prod, 0.0), axis=-1, keepdims=True)
                delta2 = jnp.sum(jnp.where(h2, prod, 0.0), axis=-1, keepdims=True)
                lt = jnp.stack([l_ref[rw, :] for rw in rows])
                lse1 = jnp.max(jnp.where(h1, lt, -jnp.inf), axis=-1, keepdims=True)
                lse2 = jnp.max(jnp.where(h2, lt, -jnp.inf), axis=-1, keepdims=True)

                def side(krows, ok):
                    kf = jnp.stack([kr[kw, :] for kw in krows])
                    k16 = kf.astype(BF16)
                    k1, k2 = _split_heads(kf, h1, h2)
                    v16 = jnp.stack([v_ref[kw, :] for kw in krows]).astype(BF16)

                    def head(qh, doh, lse, delta):
                        sc = lax.dot_general(qh, k16, BNT, preferred_element_type=F32)
                        p = jnp.where(ok, jnp.exp(sc - lse), 0.0)
                        dp = lax.dot_general(doh, v16, BNT, preferred_element_type=F32)
                        return p.astype(BF16), (p * (dp - delta)).astype(BF16)

                    p1, ds1 = head(q1, do1, lse1, delta1)
                    p2, ds2 = head(q2, do2, lse2, delta2)
                    dv = (lax.dot_general(p1, do1, BTN, preferred_element_type=F32)
                          + lax.dot_general(p2, do2, BTN, preferred_element_type=F32))
                    dk = (lax.dot_general(ds1, q1, BTN, preferred_element_type=F32)
                          + lax.dot_general(ds2, q2, BTN, preferred_element_type=F32))
                    for t, kw in enumerate(krows):
                        dva[kw, :] += dv[t]
                        dka[kw, :] += dk[t]
                    return (lax.dot_general(ds1, k1, BNN, preferred_element_type=F32)
                            + lax.dot_general(ds2, k2, BNN, preferred_element_type=F32))

                dq = side(rows, cur_ok)
                if nb > 1:
                    dq = dq + side([t[1] for t in tiles], jnp.stack([jnp.logical_and(prev_ok, t[2]) for t in tiles]))
                for t, rw in enumerate(rows):
                    dqa[rw, :] += dq[t] * scale
                return 0

            lax.fori_loop(0, dil * nb // DIL_BATCH, some, 0)

        def finish(j, _):
            rows = pl.ds(pl.multiple_of(j * BLOCK, BLOCK), BLOCK)
            cs, sn = cs_ref[rows, :], -sn_ref[rows, :]
            dq_ref[rows, :] = _rotate(dqa[rows, :], cs, sn, sa, sb).astype(BF16)
            dk_ref[rows, :] = _rotate(dka[rows, :], cs, sn, sa, sb).astype(BF16)
            dv_ref[rows, :] = dva[rows, :].astype(BF16)
            return 0

        lax.fori_loop(0, s_dim // BLOCK, finish, 0)
        end()

    col, tab, vec = _dil_specs(b_dim, s_dim)
    out = pl.BlockSpec((None, s_dim, LANES),lambda b, h: (b, 0, h))
    shp = jax.ShapeDtypeStruct((b_dim, s_dim, ATT_WIDTH), BF16)
    acc = pltpu.VMEM((s_dim, LANES), F32)
    res = pl.pallas_call(
        body, name="dil_bwd", grid=(b_dim, PAIRS),
        in_specs=[col(COL_QA), col(COL_KA), col(COL_VA), tab, tab, vec, vec, out, out, out] + [ANY] * n_w,
        out_specs=[out, out, out] + [ANY] * n_w, out_shape=[shp, shp, shp] + extra_shapes,
        scratch_shapes=[acc, acc, acc, acc, acc] + extra_sems,
        compiler_params=_params(("arbitrary", "arbitrary")),
    )(proj3, proj3, proj3, cs3, sn3, sel_a, sel_b, do3, o3, lse3, *extra)
    return res[:3], res[3:]


def _split_dot(x, tri):
    hi = x.astype(BF16)
    lo = (x - hi.astype(F32)).astype(BF16)
    return jnp.dot(hi, tri, preferred_element_type=F32) + jnp.dot(lo, tri, preferred_element_type=F32)


def _log_sigmoid(z):
    return jnp.minimum(z, 0.0) - jnp.log(1.0 + jnp.exp(-jnp.abs(z)))


def _sb_scores(qh, k16, valid):
    z = lax.dot_general(qh, k16, NT, preferred_element_type=F32)
    ls = _log_sigmoid(z)
    l1m = ls - z
    return ls, (l1m if valid is None else jnp.where(valid, l1m, 0.0))


def _sb_consts():
    r = lax.broadcasted_iota(jnp.int32, (BLOCK, BLOCK), 0)
    c = lax.broadcasted_iota(jnp.int32, (BLOCK, BLOCK), 1)
    after = (r > c).astype(BF16)
    before = (r < c).astype(BF16)
    qrow = lax.broadcasted_iota(jnp.int32, (SB_ROWS, BLOCK), 0)
    kcol = lax.broadcasted_iota(jnp.int32, (SB_ROWS, BLOCK), 1)
    return after, before, qrow, kcol


def _below(whole, lo, delta):
    if lo == 0:
        return whole + delta
    return whole + jnp.concatenate([jnp.zeros((lo,) + delta.shape[1:], delta.dtype), delta], axis=0)


def _pairs_loop(n_blocks, step, carry):
    def several(i, c):
        for j in range(SB_STEP):
            c = step(SB_STEP * i + j, c)
        return c

    return lax.fori_loop(0, n_blocks // SB_STEP, several, carry)


def _sb_fwd(proj3, rider=None):
    b_dim, s_dim, _ = proj3.shape
    scale = HEAD_DIM ** -0.5
    per = SB_ROWS // BLOCK
    extra, extra_shapes, extra_sems = _rider_parts(rider)
    n_w = len(extra)

    def body(*refs):
        q_ref, k_ref, v_ref = refs[:3]
        o_ref = refs[3 + n_w]
        step = pl.program_id(0) * PAIRS + pl.program_id(1)
        begin, end = _rider_hooks(rider, refs[3:3 + n_w], refs[4 + n_w:4 + 2 * n_w], refs[-2:], step, b_dim * PAIRS)
        begin()
        h1, h2 = _head_masks()
        after, _, qrow, kcol = _sb_consts()

        def qloop(qi, _):
            rows = pl.ds(pl.multiple_of(qi * SB_ROWS, SB_ROWS), SB_ROWS)
            q1, q2 = _split_heads(q_ref[rows, :] * scale, h1, h2)
            first = qi * per

            def block(kb, carry, lo):
                acc, run1, run2 = carry
                krows = pl.ds(pl.multiple_of(kb * BLOCK, BLOCK), BLOCK)
                k16 = k_ref[krows, :].astype(BF16)
                v1, v2 = _split_heads(v_ref[krows, :], h1, h2)
                valid = None if lo is None else kcol[:SB_ROWS - lo] < qrow[:SB_ROWS - lo]
                lo = lo or 0

                def head(qh, vh, run):
                    ls, l1m = _sb_scores(qh[lo:], k16, valid)
                    a = jnp.exp(ls + _split_dot(l1m, after) + run[lo:])
                    if valid is not None:
                        a = jnp.where(valid, a, 0.0)
                    return (jnp.dot(a.astype(BF16), vh, preferred_element_type=F32),
                            _below(run, lo, jnp.sum(l1m, axis=-1, keepdims=True)))

                o1, run1 = head(q1, v1, run1)
                o2, run2 = head(q2, v2, run2)
                return _below(acc, lo, o1 + o2), run1, run2

            zcol = jnp.zeros((SB_ROWS, 1), F32)
            carry = (jnp.zeros((SB_ROWS, LANES), F32), zcol, zcol)
            for kl in reversed(range(per)):
                carry = block(first + kl, carry, kl * BLOCK)
            acc, _, _ = _pairs_loop(first, lambda i, c: block(first - 1 - i, c, None), carry)
            o_ref[rows, :] = acc.astype(BF16)
            return 0

        lax.fori_loop(0, s_dim // SB_ROWS, qloop, 0)
        end()

    def col(c0):
        return pl.BlockSpec((None, s_dim, LANES),lambda b, h: (b, 0, c0 + h))

    res = pl.pallas_call(
        body, name="sb_fwd", grid=(b_dim, PAIRS),
        in_specs=[col(COL_QB), col(COL_KB), col(COL_VB)] + [ANY] * n_w, out_specs=[col(0)] + [ANY] * n_w,
        out_shape=[jax.ShapeDtypeStruct((b_dim, s_dim, ATT_WIDTH), BF16)] + extra_shapes,
        scratch_shapes=extra_sems,
        compiler_params=_params(("arbitrary", "arbitrary")),
    )(proj3, proj3, proj3, *extra)
    return res[0], res[1:]


def _sb_bwd(proj3, do3, rider=None):
    b_dim, s_dim, _ = proj3.shape
    scale = HEAD_DIM ** -0.5
    per = SB_ROWS // BLOCK
    nkb_max = s_dim // BLOCK
    extra, extra_shapes, extra_sems = _rider_parts(rider)
    n_w = len(extra)

    def body(*refs):
        q_ref, k_ref, v_ref, do_ref = refs[:4]
        dq_ref, dk_ref, dv_ref = refs[4 + n_w:7 + n_w]
        dka, dva, e_ref, sg_ref = refs[7 + 2 * n_w:11 + 2 * n_w]
        step = pl.program_id(0) * PAIRS + pl.program_id(1)
        begin, end = _rider_hooks(rider, refs[4:4 + n_w], refs[7 + n_w:7 + 2 * n_w], refs[-2:], step, b_dim * PAIRS)
        begin()
        h1, h2 = _head_masks()
        after, before, qrow, kcol = _sb_consts()
        dka[...] = jnp.zeros_like(dka)
        dva[...] = jnp.zeros_like(dva)

        def qloop(qi, _):
            rows = pl.ds(pl.multiple_of(qi * SB_ROWS, SB_ROWS), SB_ROWS)
            q1, q2 = _split_heads(q_ref[rows, :] * scale, h1, h2)
            do1, do2 = _split_heads(do_ref[rows, :].astype(F32), h1, h2)
            first = qi * per

            def pass1(kb, carry, lo):
                run1, run2 = carry
                krows = pl.ds(pl.multiple_of(kb * BLOCK, BLOCK), BLOCK)
                k16 = k_ref[krows, :].astype(BF16)
                v16 = v_ref[krows, :].astype(BF16)
                valid = None if lo is None else kcol[:SB_ROWS - lo] < qrow[:SB_ROWS - lo]
                lo = lo or 0
                part = pl.ds(lo, SB_ROWS - lo)

                def head(h, qh, doh, run):
                    ls, l1m = _sb_scores(qh[lo:], k16, valid)
                    a = jnp.exp(ls + _split_dot(l1m, after) + run[lo:])
                    if valid is not None:
                        a = jnp.where(valid, a, 0.0)
                    da = lax.dot_general(doh[lo:], v16, NT, preferred_element_type=F32)
                    e_ref[h, kb, part, :] = a * da
                    sg_ref[h, kb, part, :] = jnp.exp(ls)
                    return a.astype(BF16), _below(run, lo, jnp.sum(l1m, axis=-1, keepdims=True))

                a1, run1 = head(0, q1, do1, run1)
                a2, run2 = head(1, q2, do2, run2)
                dva[krows, :] += (lax.dot_general(a1, do1[lo:], TN, preferred_element_type=F32)
                                  + lax.dot_general(a2, do2[lo:], TN, preferred_element_type=F32))
                return run1, run2

            zcol = jnp.zeros((SB_ROWS, 1), F32)
            carry = (zcol, zcol)
            for kl in reversed(range(per)):
                carry = pass1(first + kl, carry, kl * BLOCK)
            _pairs_loop(first, lambda i, c: pass1(first - 1 - i, c, None), carry)

            def pass2(kb, carry, lo):
                dq, pre1, pre2 = carry
                krows = pl.ds(pl.multiple_of(kb * BLOCK, BLOCK), BLOCK)
                k1, k2 = _split_heads(k_ref[krows, :], h1, h2)
                valid = None if lo is None else kcol[:SB_ROWS - lo] < qrow[:SB_ROWS - lo]
                lo = lo or 0
                part = pl.ds(lo, SB_ROWS - lo)

                def head(h, pre):
                    ev = e_ref[h, kb, part, :]
                    sg = sg_ref[h, kb, part, :]
                    dz = ev * (1.0 - sg) - (_split_dot(ev, before) + pre[lo:]) * sg
                    if valid is not None:
                        dz = jnp.where(valid, dz, 0.0)
                    return dz.astype(BF16), _below(pre, lo, jnp.sum(ev, axis=-1, keepdims=True))

                dz1, pre1 = head(0, pre1)
                dz2, pre2 = head(1, pre2)
                dka[krows, :] += (lax.dot_general(dz1, q1[lo:], TN, preferred_element_type=F32)
                                  + lax.dot_general(dz2, q2[lo:], TN, preferred_element_type=F32))
                dq = _below(dq, lo, jnp.dot(dz1, k1, preferred_element_type=F32)
                            + jnp.dot(dz2, k2, preferred_element_type=F32))
                return dq, pre1, pre2

            carry = _pairs_loop(first, lambda i, c: pass2(i, c, None), (jnp.zeros((SB_ROWS, LANES), F32), zcol, zcol))
            for kl in range(per):
                carry = pass2(first + kl, carry, kl * BLOCK)
            dq = carry[0]
            dq_ref[rows, :] = (dq * scale).astype(BF16)
            return 0

        lax.fori_loop(0, s_dim // SB_ROWS, qloop, 0)
        dk_ref[...] = dka[...].astype(BF16)
        dv_ref[...] = dva[...].astype(BF16)
        end()

    def col(c0):
        return pl.BlockSpec((None, s_dim, LANES),lambda b, h: (b, 0, c0 + h))

    shp = jax.ShapeDtypeStruct((b_dim, s_dim, ATT_WIDTH), BF16)
    acc = pltpu.VMEM((s_dim, LANES), F32)
    strip = pltpu.VMEM((2, nkb_max, SB_ROWS, BLOCK), F32)
    res = pl.pallas_call(
        body, name="sb_bwd", grid=(b_dim, PAIRS),
        in_specs=[col(COL_QB), col(COL_KB), col(COL_VB), col(0)] + [ANY] * n_w,
        out_specs=[col(0), col(0), col(0)] + [ANY] * n_w,
        out_shape=[shp, shp, shp] + extra_shapes,
        scratch_shapes=[acc, acc, strip, strip] + extra_sems,
        compiler_params=_params(("arbitrary", "arbitrary")),
    )(proj3, proj3, proj3, do3, *extra)
    return res[:3], res[3:]


def _sigmoid(x):
    return 1.0 / (1.0 + jnp.exp(-x))


def _gate_out_norm(proj, ua, ub, w_out, x, g, *, tt=512):
    t_dim, d = ua.shape

    def body(ga_ref, gb_ref, ua_ref, ub_ref, w_ref, x_ref, g_ref, m_ref, h_ref, n_ref):
        mixed = (_sigmoid(ga_ref[...]) * ua_ref[...] + _sigmoid(gb_ref[...]) * ub_ref[...]).astype(BF16)
        m_ref[...] = mixed
        hv = x_ref[...] + jnp.dot(mixed, w_ref[...], preferred_element_type=F32)
        h_ref[...] = hv
        r = lax.rsqrt(jnp.mean(hv * hv, axis=-1, keepdims=True) + RMS_EPS)
        n_ref[...] = ((hv * r) * g_ref[...]).astype(BF16)

    row = pl.BlockSpec((tt, d), lambda i: (i, 0))
    return pl.pallas_call(
        body, name="gate_out_norm", grid=(t_dim // tt,),
        in_specs=[pl.BlockSpec((tt, d), lambda i: (i, 3)), pl.BlockSpec((tt, d), lambda i: (i, 4)), row, row,
                  pl.BlockSpec((d, d), lambda i: (0, 0)), row, pl.BlockSpec((1, d), lambda i: (0, 0))],
        out_specs=[row, row, row],
        out_shape=[jax.ShapeDtypeStruct((t_dim, d), BF16), jax.ShapeDtypeStruct((t_dim, d), F32),
                   jax.ShapeDtypeStruct((t_dim, d), BF16)],
        compiler_params=_params(("parallel",)),
    )(proj, proj, ua, ub, w_out, x, g)


def _out_dx_gate_bwd(dh, w_out, proj, ua, ub, *, tt=512):
    t_dim, d = ua.shape

    def body(dh_ref, w_ref, ga_ref, gb_ref, ua_ref, ub_ref, dua_ref, dub_ref, dg_ref):
        dm = lax.dot_general(dh_ref[...].astype(BF16), w_ref[...], NT, preferred_element_type=F32)
        sa = _sigmoid(ga_ref[...])
        sb = _sigmoid(gb_ref[...])
        dua_ref[...] = (dm * sa).astype(BF16)
        dub_ref[...] = (dm * sb).astype(BF16)
        dg_ref[:, :d] = (dm * ua_ref[...] * (sa * (1.0 - sa))).astype(BF16)
        dg_ref[:, d:] = (dm * ub_ref[...] * (sb * (1.0 - sb))).astype(BF16)

    row = pl.BlockSpec((tt, d), lambda i: (i, 0))
    wide = pl.BlockSpec((tt, 2 * d), lambda i: (i, 0))
    return pl.pallas_call(
        body, name="out_dx_gate_bwd", grid=(t_dim // tt,),
        in_specs=[row, pl.BlockSpec((d, d), lambda i: (0, 0)),
                  pl.BlockSpec((tt, d), lambda i: (i, 3)), pl.BlockSpec((tt, d), lambda i: (i, 4)), row, row],
        out_specs=[row, row, wide],
        out_shape=[jax.ShapeDtypeStruct((t_dim, d), BF16), jax.ShapeDtypeStruct((t_dim, d), BF16),
                   jax.ShapeDtypeStruct((t_dim, 2 * d), BF16)],
        compiler_params=_params(("parallel",)),
    )(dh, w_out, proj, proj, ua, ub)


def _ffn_up_swiglu(n, wg3, wu3, *, tt=1024):
    t_dim, d = n.shape
    n_s, _, f4 = wg3.shape

    def body(n_ref, wg_ref, wu_ref, g_ref, u_ref, a_ref):
        nv = n_ref[...]
        gv = jnp.dot(nv, wg_ref[...], preferred_element_type=F32)
        uv = jnp.dot(nv, wu_ref[...], preferred_element_type=F32)
        g_ref[...] = gv.astype(BF16)
        u_ref[...] = uv.astype(BF16)
        a_ref[...] = (gv * _sigmoid(gv) * uv).astype(BF16)

    wspec = pl.BlockSpec((None, d, f4), lambda i, s: (s, 0, 0))
    ospec = pl.BlockSpec((None, tt, f4), lambda i, s: (s, i, 0))
    shp = (n_s, t_dim, f4)
    return pl.pallas_call(
        body, name="ffn_up_swiglu", grid=(t_dim // tt, n_s),
        in_specs=[pl.BlockSpec((tt, d), lambda i, s: (i, 0)), wspec, wspec], out_specs=[ospec, ospec, ospec],
        out_shape=[jax.ShapeDtypeStruct(shp, BF16)] * 3,
        compiler_params=_params(("parallel", "parallel")),
    )(n, wg3, wu3)


def _ffn_down_dx_swiglu(dh, wd3, g3, u3, *, tt=1024):
    t_dim, d = dh.shape
    n_s, f4, _ = wd3.shape

    def body(dh_ref, w_ref, g_ref, u_ref, dg_ref, du_ref):
        da = lax.dot_general(dh_ref[...].astype(BF16), w_ref[...], NT, preferred_element_type=F32)
        gv = g_ref[...].astype(F32)
        sg = _sigmoid(gv)
        dg_ref[...] = (da * u_ref[...].astype(F32) * (sg + gv * sg * (1.0 - sg))).astype(BF16)
        du_ref[...] = (da * (gv * sg)).astype(BF16)

    spec = pl.BlockSpec((None, tt, f4), lambda i, s: (s, i, 0))
    shp = jax.ShapeDtypeStruct((n_s, t_dim, f4), BF16)
    return pl.pallas_call(
        body, name="ffn_down_dx_swiglu", grid=(t_dim // tt, n_s),
        in_specs=[pl.BlockSpec((tt, d), lambda i, s: (i, 0)), pl.BlockSpec((None, f4, d), lambda i, s: (s, 0, 0)),
                  spec, spec],
        out_specs=[spec, spec], out_shape=[shp, shp],
        compiler_params=_params(("parallel", "parallel")),
    )(dh, wd3, g3, u3)


def _mem_fwd(qm, kvm, *, tt=512):
    b_dim, s_dim, _ = qm.shape
    n_mem = kvm.shape[1]
    scale = MEM_HEAD_DIM ** -0.5

    def body(q_ref, k_ref, v_ref, o_ref):
        sc = lax.dot_general(q_ref[0], k_ref[0], NT, preferred_element_type=F32) * scale
        p = jnp.exp(sc - jnp.max(sc, axis=-1, keepdims=True))
        p = p / jnp.sum(p, axis=-1, keepdims=True)
        o_ref[0] = jnp.dot(p.astype(BF16), v_ref[0], preferred_element_type=F32).astype(BF16)

    qs = pl.BlockSpec((1, tt, MEM_HEAD_DIM), lambda b, h, i: (b, i, h))
    return pl.pallas_call(
        body, name="mem_fwd", grid=(b_dim, N_HEADS_MEM, s_dim // tt),
        in_specs=[qs, pl.BlockSpec((1, n_mem, MEM_HEAD_DIM), lambda b, h, i: (b, 0, h)),
                  pl.BlockSpec((1, n_mem, MEM_HEAD_DIM), lambda b, h, i: (b, 0, N_HEADS_MEM + h))],
        out_specs=qs, out_shape=jax.ShapeDtypeStruct(qm.shape, BF16),
        compiler_params=_params(("parallel", "parallel", "parallel")),
    )(qm, kvm, kvm)


def _mem_bwd(qm, kvm, dom, *, tt=512):
    b_dim, s_dim, _ = qm.shape
    n_mem = kvm.shape[1]
    scale = MEM_HEAD_DIM ** -0.5

    def body(q_ref, k_ref, v_ref, do_ref, dq_ref, dk_ref, dv_ref):
        qv, kv, vv, dov = q_ref[0], k_ref[0], v_ref[0], do_ref[0]
        sc = lax.dot_general(qv, kv, NT, preferred_element_type=F32) * scale
        p = jnp.exp(sc - jnp.max(sc, axis=-1, keepdims=True))
        p = p / jnp.sum(p, axis=-1, keepdims=True)
        dp = lax.dot_general(dov, vv, NT, preferred_element_type=F32)
        ds = (p * (dp - jnp.sum(p * dp, axis=-1, keepdims=True)) * scale).astype(BF16)
        dq_ref[0] = jnp.dot(ds, kv, preferred_element_type=F32).astype(BF16)

        @pl.when(pl.program_id(2) == 0)
        def _():
            dk_ref[...] = jnp.zeros_like(dk_ref)
            dv_ref[...] = jnp.zeros_like(dv_ref)

        dk_ref[0] += lax.dot_general(ds, qv, TN, preferred_element_type=F32)
        dv_ref[0] += lax.dot_general(p.astype(BF16), dov, TN, preferred_element_type=F32)

    qs = pl.BlockSpec((1, tt, MEM_HEAD_DIM), lambda b, h, i: (b, i, h))
    ks = pl.BlockSpec((1, n_mem, MEM_HEAD_DIM), lambda b, h, i: (b, 0, h))
    vs = pl.BlockSpec((1, n_mem, MEM_HEAD_DIM), lambda b, h, i: (b, 0, N_HEADS_MEM + h))
    return pl.pallas_call(
        body, name="mem_bwd", grid=(b_dim, N_HEADS_MEM, s_dim // tt),
        in_specs=[qs, ks, vs, qs], out_specs=[qs, ks, ks],
        out_shape=[jax.ShapeDtypeStruct(qm.shape, BF16), jax.ShapeDtypeStruct((b_dim, n_mem, MEM_WIDTH), F32),
                   jax.ShapeDtypeStruct((b_dim, n_mem, MEM_WIDTH), F32)],
        compiler_params=_params(("parallel", "parallel", "arbitrary")),
    )(qm, kvm, kvm, dom)


def _adamw_math(wv, gv, mv, vv):
    nm = ADAM_B1 * mv + (1.0 - ADAM_B1) * gv
    nv = ADAM_B2 * vv + (1.0 - ADAM_B2) * (gv * gv)
    m_hat = nm / (1.0 - ADAM_B1 ** ADAM_STEP)
    v_hat = nv / (1.0 - ADAM_B2 ** ADAM_STEP)
    return -ADAM_LR * (m_hat / (jnp.sqrt(v_hat) + ADAM_EPS) + ADAM_WD * wv), nm, nv


def _adamw(w, g, m, v, *, name):
    rows, cols = w.shape
    tr = _tile(rows, 256, 8)

    def body(w_ref, g_ref, m_ref, v_ref, d_ref, nm_ref, nv_ref):
        d_ref[...], nm_ref[...], nv_ref[...] = _adamw_math(w_ref[...], g_ref[...], m_ref[...], v_ref[...])

    spec = pl.BlockSpec((tr, cols), lambda i: (i, 0))
    shp = jax.ShapeDtypeStruct((rows, cols), F32)
    return pl.pallas_call(
        body, name=name, grid=(rows // tr,),
        in_specs=[spec] * 4, out_specs=[spec] * 3, out_shape=[shp] * 3,
        compiler_params=_params(("parallel",)),
    )(w, g, m, v)


def _prefetch_spec(grid, in_specs, out_specs):
    return pltpu.PrefetchScalarGridSpec(num_scalar_prefetch=1, grid=grid, in_specs=in_specs, out_specs=out_specs)


def _adamw_halves(w, mine, theirs, m, v, c_idx, *, name):
    rows, cols = w.shape
    half = rows // 2
    tr = _tile(half, _row_cap(cols), 8)
    nh = half // tr

    def body(c_ref, w_ref, mine_ref, theirs_ref, m_ref, v_ref, g_ref, d_ref, nm_ref, nv_ref):
        gv = jnp.where(pl.program_id(0) == c_ref[0], mine_ref[...], theirs_ref[...])
        g_ref[...] = gv
        d_ref[...], nm_ref[...], nv_ref[...] = _adamw_math(w_ref[...], gv, m_ref[...], v_ref[...])

    full = pl.BlockSpec((tr, cols), lambda h, i, c_ref: (h * nh + i, 0))
    part = pl.BlockSpec((tr, cols), lambda h, i, c_ref: (i, 0))
    shp = jax.ShapeDtypeStruct((rows, cols), F32)
    return pl.pallas_call(
        body, name=name, grid_spec=_prefetch_spec((2, nh), [full, part, part, full, full], [full] * 4),
        out_shape=[shp] * 4,
        compiler_params=_params(("parallel", "parallel")),
    )(c_idx, w, mine, theirs, m, v)


def _pair_sum(g3, theirs, c_idx, *, name):
    n, rows, cols = g3.shape
    half = rows // 2
    tr = _tile(half, _row_cap(cols), 16)

    def body(c_ref, g_ref, t_ref, o_ref):
        o_ref[...] = (g_ref[...] + t_ref[...]).astype(BF16)

    part = pl.BlockSpec((None, tr, cols), lambda s, i, c_ref: (s, i, 0))
    return pl.pallas_call(
        body, name=name,
        grid_spec=_prefetch_spec((n, half // tr),
                                 [pl.BlockSpec((None, None, tr, cols), lambda s, i, c_ref: (s, c_ref[0], i, 0)), part],
                                 part),
        out_shape=jax.ShapeDtypeStruct((n, half, cols), BF16),
        compiler_params=_params(("parallel", "parallel")),
    )(c_idx, g3.reshape(n, 2, half, cols), theirs)


def _chip_sum(pair, recv, s_idx, *, name):
    _, half, cols = pair.shape
    tr = _tile(half, _row_cap(cols), 16)

    def body(s_ref, p_ref, r_ref, o_ref):
        o_ref[...] = ((p_ref[...].astype(F32) + r_ref[0].astype(F32)) + r_ref[1].astype(F32)) + r_ref[2].astype(F32)

    return pl.pallas_call(
        body, name=name,
        grid_spec=_prefetch_spec((half // tr,),
                                 [pl.BlockSpec((None, tr, cols), lambda i, s_ref: (s_ref[0], i, 0)),
                                  pl.BlockSpec((N_CHIPS - 1, tr, cols), lambda i, s_ref: (0, i, 0))],
                                 pl.BlockSpec((tr, cols), lambda i, s_ref: (i, 0))),
        out_shape=jax.ShapeDtypeStruct((half, cols), F32),
        compiler_params=_params(("parallel",)),
    )(s_idx, pair, recv)


def _sum8(parts):
    n, rows, cols = parts.shape

    def body(p_ref, o_ref):
        acc = p_ref[0]
        for i in range(1, n):
            acc = acc + p_ref[i]
        o_ref[...] = acc

    return pl.pallas_call(
        body, name="small_sum", grid=(1,),
        in_specs=[pl.BlockSpec((n, rows, cols), lambda i: (0, 0, 0))],
        out_specs=pl.BlockSpec((rows, cols), lambda i: (0, 0)),
        out_shape=jax.ShapeDtypeStruct((rows, cols), parts.dtype),
        compiler_params=_params(("arbitrary",)),
    )(parts)


def _place():
    return lax.axis_index("x"), lax.axis_index("y"), lax.axis_index("c")


ANY = pl.BlockSpec(memory_space=pl.ANY)


def _rider_parts(rider):
    if rider is None:
        return (), [], []
    kind, arrays = rider
    n = len(arrays)
    shapes = {"gather": _gathered_shapes, "pair": _pair_shapes, "chip": _chip_shapes}[kind](arrays)
    sems = _gather_sems(n) if kind == "gather" else _exchange_sems(n if kind == "pair" else 3 * n)
    return tuple(arrays), shapes, sems


def _rider_hooks(rider, ins, outs, sems, step, n_steps):
    if rider is None:
        return (lambda: None), (lambda: None)
    if rider[0] == "gather":
        start, forward, finish = _gather_steps(ins, outs, *sems)
    else:
        start, finish = {"pair": _pair_steps, "chip": _chip_steps}[rider[0]](ins, outs, *sems)
        forward = None

    def begin():
        pl.when(step == 0)(start)

    def end():
        if forward is not None:
            pl.when(step == n_steps - 2)(forward)
        pl.when(step == n_steps - 1)(finish)

    return begin, end


def _gather_weights(shards):
    n = len(shards)

    def body(*refs):
        start, forward, finish = _gather_steps(refs[:n], refs[n:2 * n], refs[2 * n], refs[2 * n + 1])
        start()
        forward()
        finish()

    return pl.pallas_call(
        body, name="gather_weights", out_shape=_gathered_shapes(shards),
        in_specs=[ANY] * n, out_specs=[ANY] * n, scratch_shapes=_gather_sems(n),
    )(*shards)


def _gathered_shapes(shards):
    return [jax.ShapeDtypeStruct((N_CHIPS,) + s.shape, s.dtype) for s in shards]


def _gather_sems(n):
    return [pltpu.SemaphoreType.DMA((7 * n,)), pltpu.SemaphoreType.DMA((7 * n,))]


def _gather_steps(ins, outs, send_sems, recv_sems):
    n = len(ins)
    halves = [r.shape[0] // 2 for r in ins]
    x, y, c = _place()
    my_chip = 2 * x + y
    me, sibling = (x, y, c), (x, y, 1 - c)
    chips = [(1 - x, y), (x, 1 - y), (1 - x, 1 - y)]

    def half_of(w, chip, pc):
        return outs[w].at[chip, pl.ds(pc * halves[w], halves[w]), :]

    def copy(w, k, src, dst, to):
        return pltpu.make_async_remote_copy(
            src_ref=src, dst_ref=dst, send_sem=send_sems.at[7 * w + k], recv_sem=recv_sems.at[7 * w + k],
            device_id=to, device_id_type=MESH)

    def firsts():
        cps = []
        for w in range(n):
            cps.append(copy(w, 0, ins[w], outs[w].at[my_chip], sibling))
            mine = ins[w].at[pl.ds(c * halves[w], halves[w]), :]
            for j, (px, py) in enumerate(chips):
                cps.append(copy(w, 1 + j, mine, half_of(w, my_chip, c), (px, py, c)))
        return cps

    def passes():
        return [copy(w, 4 + j, half_of(w, 2 * px + py, c), half_of(w, 2 * px + py, c), sibling)
                for w in range(n) for j, (px, py) in enumerate(chips)]

    def start():
        for cp in firsts():
            cp.start()

    def forward():
        fws = passes()
        for w in range(n):
            for j, (px, py) in enumerate(chips):
                landed = half_of(w, 2 * px + py, c)
                copy(w, 1 + j, landed, landed, me).wait_recv()
                fws[3 * w + j].start()

    def finish():
        for w in range(n):
            copy(w, 0, ins[w], outs[w].at[my_chip], me).wait_recv()
            for j, (px, py) in enumerate(chips):
                landed = half_of(w, 2 * px + py, 1 - c)
                copy(w, 4 + j, landed, landed, me).wait_recv()
        for cp in firsts() + passes():
            cp.wait_send()

    return start, forward, finish


def _pair_shapes(grads):
    return [jax.ShapeDtypeStruct((g.shape[0], g.shape[1] // 2, g.shape[2]), g.dtype) for g in grads]


def _exchange_sems(n):
    return [pltpu.SemaphoreType.DMA((n,)), pltpu.SemaphoreType.DMA((n,))]


def _exchange_steps(copies):
    def start():
        for cp in copies():
            cp.start()

    def finish():
        for cp in copies():
            cp.wait()

    return start, finish


def _pair_steps(ins, outs, send_sems, recv_sems):
    x, y, c = _place()

    def copies():
        return [pltpu.make_async_remote_copy(
            src_ref=ins[w].at[:, pl.ds((1 - c) * (ins[w].shape[1] // 2), ins[w].shape[1] // 2), :], dst_ref=outs[w],
            send_sem=send_sems.at[w], recv_sem=recv_sems.at[w], device_id=(x, y, 1 - c), device_id_type=MESH)
            for w in range(len(ins))]

    return _exchange_steps(copies)


def _chip_shapes(pairs):
    return [jax.ShapeDtypeStruct((N_CHIPS - 1,) + p.shape[1:], p.dtype) for p in pairs]


def _chip_steps(ins, outs, send_sems, recv_sems):
    x, y, c = _place()
    others = [(1 - x, y), (x, 1 - y), (1 - x, 1 - y)]

    def copies():
        return [pltpu.make_async_remote_copy(
            src_ref=ins[w].at[2 * px + py], dst_ref=outs[w].at[j],
            send_sem=send_sems.at[3 * w + j], recv_sem=recv_sems.at[3 * w + j],
            device_id=(px, py, c), device_id_type=MESH)
            for w in range(len(ins)) for j, (px, py) in enumerate(others)]

    return _exchange_steps(copies)


def _swap_halves(mine):
    n = len(mine)

    def body(*refs):
        ins, outs, send_sems, recv_sems = refs[:n], refs[n:2 * n], refs[2 * n], refs[2 * n + 1]
        x, y, c = _place()
        copies = [pltpu.make_async_remote_copy(
            src_ref=ins[w], dst_ref=outs[w], send_sem=send_sems.at[w], recv_sem=recv_sems.at[w],
            device_id=(x, y, 1 - c), device_id_type=MESH) for w in range(n)]
        for cp in copies:
            cp.start()
        for cp in copies:
            cp.wait()

    return pl.pallas_call(
        body, name="grad_swap_halves",
        out_shape=[jax.ShapeDtypeStruct(h.shape, h.dtype) for h in mine],
        in_specs=[ANY] * n, out_specs=[ANY] * n,
        scratch_shapes=[pltpu.SemaphoreType.DMA((n,)), pltpu.SemaphoreType.DMA((n,))],
    )(*mine)


def _gather_small(small):
    srows, cols = small.shape

    def body(s_ref, all_ref, send_sems, recv_sems, local_sem):
        x, y, c = _place()
        me = 4 * x + 2 * y + c
        keep_small = pltpu.make_async_copy(s_ref, all_ref.at[me], local_sem)
        keep_small.start()
        sends = []
        for kk in range(1, 8):
            peer = (x ^ (kk >> 2), y ^ ((kk >> 1) & 1), c ^ (kk & 1))
            sends.append(pltpu.make_async_remote_copy(
                src_ref=s_ref, dst_ref=all_ref.at[me],
                send_sem=send_sems.at[kk], recv_sem=recv_sems.at[kk], device_id=peer, device_id_type=MESH))
        for cp in sends:
            cp.start()
        for kk in range(1, 8):
            px, py, pc = x ^ (kk >> 2), y ^ ((kk >> 1) & 1), c ^ (kk & 1)
            pltpu.make_async_remote_copy(
                src_ref=s_ref, dst_ref=all_ref.at[4 * px + 2 * py + pc],
                send_sem=send_sems.at[kk], recv_sem=recv_sems.at[kk], device_id=(px, py, pc),
                device_id_type=MESH).wait_recv()
        for cp in sends:
            cp.wait_send()
        keep_small.wait()

    return pl.pallas_call(
        body, name="gather_small",
        out_shape=jax.ShapeDtypeStruct((8, srows, cols), small.dtype),
        in_specs=[ANY], out_specs=ANY,
        scratch_shapes=[pltpu.SemaphoreType.DMA((8,)), pltpu.SemaphoreType.DMA((8,)), pltpu.SemaphoreType.DMA],
    )(small)


SHARDED = (("w_in", D_MODEL, IN_COLS, 1), ("w_up_a", ATT_WIDTH, D_MODEL, 1), ("w_up_b", ATT_WIDTH, D_MODEL, 1),
           ("w_out", D_MODEL, D_MODEL, 0), ("w_q_mem", D_MODEL, MEM_WIDTH, 0), ("w_kv_mem", D_MODEL, 2 * MEM_WIDTH, 0),
           ("w_o_mem", MEM_WIDTH, D_MODEL, 1), ("w_ffn_gate", D_MODEL, D_FF, 1), ("w_ffn_up", D_MODEL, D_FF, 1),
           ("w_ffn_down", D_FF, D_MODEL, 0))
NAMES = tuple(n for n, _, _, _ in SHARDED)
EARLY, LATE = NAMES[:1], NAMES[1:]
GAINS = ("g_mix", "g_mem_q", "g_mem_kv", "g_ffn", "g_final")


def _natural(w3):
    n, r, c = w3.shape
    return w3.reshape(n * r, c)


def _shard_major(g, axis):
    if axis == 1:
        return g
    r, c = g.shape
    return g.reshape(N_CHIPS, r // N_CHIPS, c)


def kernel(x, mem, positions, g_mix, w_in, w_up_a, w_up_b, w_out, g_mem_q, g_mem_kv, w_q_mem, w_kv_mem, w_o_mem, g_ffn, w_ffn_gate, w_ffn_up, w_ffn_down, g_final, loss_target, m_g_mix, m_w_in, m_w_up_a, m_w_up_b, m_w_out, m_g_mem_q, m_g_mem_kv, m_w_q_mem, m_w_kv_mem, m_w_o_mem, m_g_ffn, m_w_ffn_gate, m_w_ffn_up, m_w_ffn_down, m_g_final, v_g_mix, v_w_in, v_w_up_a, v_w_up_b, v_w_out, v_g_mem_q, v_g_mem_kv, v_w_q_mem, v_w_kv_mem, v_w_o_mem, v_g_ffn, v_w_ffn_gate, v_w_ffn_up, v_w_ffn_down, v_g_final):
    given = dict(locals())
    shards = {n: given[n][0] for n, _, _, _ in SHARDED}

    wf = dict(zip(EARLY, _gather_weights([shards[n].astype(BF16) for n in EARLY])))
    late_shards = [shards[n].astype(BF16) for n in LATE]
    c_idx = lax.axis_index("c").astype(jnp.int32).reshape(1)
    s_idx = (2 * lax.axis_index("x") + lax.axis_index("y")).astype(jnp.int32).reshape(1)

    loss_row, grad_x, mine, gain_grads = _local_step(x, mem, positions, loss_target, g_mix, g_mem_q, g_mem_kv,
                                                     g_ffn, g_final, wf, late_shards, (c_idx, s_idx))
    return _reduce_and_update(given, shards, loss_row, grad_x, mine, gain_grads, c_idx)


def _reduce_halves(glist, names, c_idx, s_idx, pair_exchange, chip_exchange):
    theirs = pair_exchange(glist)
    pairs = [_pair_sum(g, t, c_idx, name="pair_sum_" + n) for n, g, t in zip(names, glist, theirs)]
    recv = chip_exchange(pairs)
    return [_chip_sum(p, r, s_idx, name="chip_sum_" + n) for n, p, r in zip(names, pairs, recv)]


def _local_step(x, mem, positions, loss_target, g_mix, g_mem_q, g_mem_kv, g_ffn, g_final, wf,
                late_shards=None, place=None):
    b_dim, s_dim, d = x.shape
    t_dim = b_dim * s_dim
    n_mem = mem.shape[1]
    wf = dict(wf)

    xb = x.reshape(t_dim, d)
    tgt = loss_target.reshape(t_dim, d)
    memf = mem.reshape(b_dim * n_mem, d)
    gfin = g_final.reshape(1, d)
    pos = positions.reshape(t_dim, 1).astype(F32)

    lane = jnp.arange(LANES) % HEAD_DIM
    half = ROPE_DIM // 2
    inv_freq = ROPE_THETA ** (-jnp.arange(half, dtype=F32) / half)
    inv_lane = jnp.where(lane < ROPE_DIM, inv_freq[lane % half], 0.0).reshape(1, -1).astype(F32)
    sel_a = (lane < half).astype(F32).reshape(1, -1)
    sel_b = ((lane >= half) & (lane < ROPE_DIM)).astype(F32).reshape(1, -1)

    def rows3(t):
        return t.reshape(b_dim, s_dim, t.shape[-1])

    def rows2(t):
        return t.reshape(t_dim, t.shape[-1])

    n1 = _rms_fwd(xb, g_mix, name="rms_mix")
    proj = _mm_cs(n1, wf["w_in"], name="mm_in")
    proj3 = rows3(proj)
    cs, sn = _rope_table(pos, inv_lane, sel_a, sel_b)
    cs3, sn3 = rows3(cs), rows3(sn)
    (oa16, oa32, lse_a), _ = _dil_fwd(proj3, cs3, sn3, sel_a, sel_b)
    ob16, gathered = _sb_fwd(proj3, ("gather", late_shards) if late_shards else None)
    wf.update(zip(LATE, gathered))
    w_out, w_q, w_kv = _natural(wf["w_out"]), _natural(wf["w_q_mem"]), _natural(wf["w_kv_mem"])
    oa, ob = rows2(oa16), rows2(ob16)
    ua = _mm_cs(oa, wf["w_up_a"], name="mm_up_a")
    ub = _mm_cs(ob, wf["w_up_b"], name="mm_up_b")
    mixed, h1, hn = _gate_out_norm(proj, ua, ub, w_out, xb, g_mem_q)

    memn = _rms_fwd(memf, g_mem_kv, name="rms_mem_kv")
    qm = _mm(hn, w_q, name="mm_q_mem", out_dtype=BF16)
    kvm = _mm(memn, w_kv, name="mm_kv_mem", out_dtype=BF16)
    qm3, kvm3 = rows3(qm), kvm.reshape(b_dim, n_mem, 2 * MEM_WIDTH)
    om = rows2(_mem_fwd(qm3, kvm3))
    h2 = _mm_cs(om, wf["w_o_mem"], name="mm_o_mem", add=h1)

    n3 = _rms_fwd(h2, g_ffn, name="rms_ffn")
    gate3, up3, act3 = _ffn_up_swiglu(n3, wf["w_ffn_gate"], wf["w_ffn_up"])
    loss_row, dh3, dg_final = _down_final(act3, wf["w_ffn_down"], h2, gfin, tgt)

    grads = {}
    grads["w_ffn_down"] = _mm_ffn_down_dw(act3, dh3, name="mm_down_dw")
    dgate3, dup3 = _ffn_down_dx_swiglu(dh3, wf["w_ffn_down"], gate3, up3)
    grads["w_ffn_gate"] = _mm_ffn_up_dw(n3, dgate3, name="mm_gate_dw")
    grads["w_ffn_up"] = _mm_ffn_up_dw(n3, dup3, name="mm_up_dw")
    dn3 = _mm_ffn_up_dx(dgate3, wf["w_ffn_gate"], name="mm_gate_dx")
    dn3 = _mm_ffn_up_dx(dup3, wf["w_ffn_up"], name="mm_up_dx", add=dn3)
    dh2, dg_ffn = _rms_bwd(h2, g_ffn, dn3, dh3, name="rms_ffn_bwd")

    dom = _mm_cs_dx(dh2, wf["w_o_mem"], name="mm_o_mem_dx", out_dtype=BF16)
    grads["w_o_mem"] = _mm_cs_dw(om, dh2, name="mm_o_mem_dw")
    dqm, dkm, dvm = _mem_bwd(qm3, kvm3, rows3(dom))
    dqm = rows2(dqm)
    dkvm = jnp.concatenate([dkm, dvm], axis=-1).reshape(b_dim * n_mem, 2 * MEM_WIDTH).astype(BF16)
    grads["w_q_mem"] = _shard_major(_mm(hn, dqm, name="mm_q_mem_dw", ta=True), 0)
    dhn = _mm(dqm, w_q, name="mm_q_mem_dx", tb=True)
    grads["w_kv_mem"] = _shard_major(_mm(memn, dkvm, name="mm_kv_mem_dw", ta=True), 0)
    dmemn = _mm(dkvm, w_kv, name="mm_kv_mem_dx", tb=True)
    _, dg_mem_kv = _rms_bwd(memf, g_mem_kv, dmemn, None, name="rms_mem_kv_bwd")
    dh1, dg_mem_q = _rms_bwd(h1, g_mem_q, dhn, dh2, name="rms_mem_q_bwd")

    grads["w_out"] = _shard_major(_mm(mixed, dh1, name="mm_out_dw", ta=True), 0)
    dua, dub, dgates = _out_dx_gate_bwd(dh1, w_out, proj, ua, ub)
    doa = _mm_cs_dx(dua, wf["w_up_a"], name="mm_up_a_dx")
    grads["w_up_a"] = _mm_cs_dw(oa, dua, name="mm_up_a_dw")
    dob = _mm_cs_dx(dub, wf["w_up_b"], name="mm_up_b_dx", out_dtype=BF16)
    grads["w_up_b"] = _mm_cs_dw(ob, dub, name="mm_up_b_dw")

    att = {}

    def dil_with_pairs(glist):
        att["a"], theirs = _dil_bwd(proj3, cs3, sn3, sel_a, sel_b, rows3(doa), oa32, lse_a,
                                    ("pair", glist) if glist else None)
        return theirs

    def sb_with_chips(pairs):
        att["b"], recv = _sb_bwd(proj3, rows3(dob), ("chip", pairs) if pairs else None)
        return recv

    if place is None:
        dil_with_pairs(())
        sb_with_chips(())
    else:
        mine_late = _reduce_halves([grads[n] for n in LATE], LATE, *place, dil_with_pairs, sb_with_chips)
    dproj = jnp.concatenate([rows2(t) for t in att["a"] + att["b"]] + [dgates], axis=1)
    grads["w_in"] = _mm_cs_dw(n1, dproj, name="mm_in_dw")
    if place is None:
        dn1 = _mm_cs_dx(dproj, wf["w_in"], name="mm_in_dx")
        dx, dg_mix = _rms_bwd(xb, g_mix, dn1, dh1, name="rms_mix_bwd")
    else:
        tail = {}

        def dx_with_pairs(glist):
            tail["dn1"], theirs = _mm_cs_dx(dproj, wf["w_in"], name="mm_in_dx", rider=("pair", glist))
            return theirs

        def rms_with_chips(pairs):
            tail["dx"], tail["dg"], recv = _rms_bwd(xb, g_mix, tail["dn1"], dh1, name="rms_mix_bwd",
                                                    rider=("chip", pairs))
            return recv

        mine_early = _reduce_halves([grads[n] for n in EARLY], EARLY, *place, dx_with_pairs, rms_with_chips)
        dx, dg_mix = tail["dx"], tail["dg"]
    grad_x = dx.reshape(b_dim, s_dim, d)
    gains = (dg_mix, dg_mem_q, dg_mem_kv, dg_ffn, dg_final)
    if place is None:
        return loss_row, grad_x, grads, gains
    return loss_row, grad_x, mine_early + mine_late, gains


def _reduce_and_update(given, shards, loss_row, grad_x, mine, gain_grads, c_idx):
    d = D_MODEL
    dg_mix, dg_mem_q, dg_mem_kv, dg_ffn, dg_final = gain_grads
    small = jnp.concatenate([dg_mix, dg_mem_q, dg_mem_kv, dg_ffn, dg_final,
                             jnp.pad(loss_row, ((0, 0), (0, FLAT_COLS - LANES))), jnp.zeros((2, FLAT_COLS), F32)], axis=0)
    small_all = _gather_small(small)
    others = _swap_halves(mine)
    small_sum = _sum8(small_all)
    loss = small_sum[5, 0]

    out_g, out_d, out_m, out_v = {}, {}, {}, {}
    for n, mine_n, other_n in zip(NAMES, mine, others):
        g2, dl, nm, nv = _adamw_halves(shards[n], mine_n, other_n, given["m_" + n][0], given["v_" + n][0], c_idx,
                                       name="adamw_" + n)
        out_g[n], out_d[n], out_m[n], out_v[n] = g2[None], dl[None], nm[None], nv[None]
    gain_w = jnp.concatenate([given[n].reshape(1, d) for n in GAINS], axis=0)
    gain_m = jnp.concatenate([given["m_" + n].reshape(1, d) for n in GAINS], axis=0)
    gain_v = jnp.concatenate([given["v_" + n].reshape(1, d) for n in GAINS], axis=0)
    gain_g = small_sum[:len(GAINS)]
    gd, gm, gv = _adamw(gain_w, gain_g, gain_m, gain_v, name="adamw_gains")
    for i, n in enumerate(GAINS):
        shape = given[n].shape
        out_g[n], out_d[n] = gain_g[i].reshape(shape), gd[i].reshape(shape)
        out_m[n], out_v[n] = gm[i].reshape(shape), gv[i].reshape(shape)

    order = ["g_mix", "w_in", "w_up_a", "w_up_b", "w_out", "g_mem_q", "g_mem_kv", "w_q_mem", "w_kv_mem", "w_o_mem",
             "g_ffn", "w_ffn_gate", "w_ffn_up", "w_ffn_down", "g_final"]
    return (loss, grad_x, *[out_g[n] for n in order], *[out_d[n] for n in order],
            *[out_m[n] for n in order], *[out_v[n] for n in order])
```

```python
import jax
import jax.numpy as jnp
from jax import lax
from jax.experimental import pallas as pl
from jax.experimental.pallas import tpu as pltpu

F32 = jnp.float32
BF16 = jnp.bfloat16
MESH = pl.DeviceIdType.MESH

D_MODEL = 1024
HEAD_DIM = 64
N_HEADS = 8
ATT_WIDTH = N_HEADS * HEAD_DIM
DIL_PATTERNS = ((128, 1), (512, 4), (2048, 16))
BLOCK = 128
SB_ROWS = 1024
SB_STEP = 4
ROPE_THETA = 500000.0
ROPE_DIM = HEAD_DIM // 4
N_HEADS_MEM = 4
MEM_HEAD_DIM = 128
MEM_WIDTH = N_HEADS_MEM * MEM_HEAD_DIM
D_FF = 2816
IN_COLS = 6 * ATT_WIDTH + 2 * D_MODEL
RMS_EPS = 1e-6
ADAM_LR = 0.001
ADAM_B1 = 0.9
ADAM_B2 = 0.999
ADAM_EPS = 1e-08
ADAM_WD = 0.01
ADAM_STEP = 10

N_CHIPS = 4
LANES = 128
FLAT_COLS = 1024
VMEM_LIMIT = 56 * 1024 * 1024

PAIRS = ATT_WIDTH // LANES
COL_QA, COL_KA, COL_VA, COL_QB, COL_KB, COL_VB = (i * PAIRS for i in range(6))

MM_CAP = 1408
TOK_CAP = 2048
NN = (((1,), (0,)), ((), ()))
NT = (((1,), (1,)), ((), ()))
TN = (((0,), (0,)), ((), ()))
BNN = (((2,), (1,)), ((0,), (0,)))
BNT = (((2,), (2,)), ((0,), (0,)))
BTN = (((1,), (1,)), ((0,), (0,)))
DIL_BATCH = 8


def _tile(dim, cap, unit=LANES):
    if dim <= cap:
        return dim
    best = None
    for t in range(unit, cap + 1, unit):
        if dim % t == 0:
            best = t
    assert best is not None, (dim, cap)
    return best


def _row_cap(cols):
    return max(256, (1 << 18) // cols)


def _params(sem):
    return pltpu.CompilerParams(dimension_semantics=sem, vmem_limit_bytes=VMEM_LIMIT)


def _mm(a, b, *, name, ta=False, tb=False, add=None, out_dtype=F32,
        tm_cap=MM_CAP, tn_cap=MM_CAP, tk_cap=MM_CAP):
    if ta:
        k_dim, m_dim = a.shape
    else:
        m_dim, k_dim = a.shape
    if tb:
        n_dim, kb = b.shape
    else:
        kb, n_dim = b.shape
    assert kb == k_dim, (a.shape, b.shape, ta, tb)
    tm, tn, tk = _tile(m_dim, tm_cap), _tile(n_dim, tn_cap), _tile(k_dim, tk_cap)
    nk = k_dim // tk
    dims = (((0 if ta else 1,), (1 if tb else 0,)), ((), ()))
    has_add = add is not None

    def body(*refs):
        if has_add:
            a_ref, b_ref, add_ref, o_ref = refs[:4]
        else:
            a_ref, b_ref, o_ref = refs[:3]
        part = lax.dot_general(a_ref[...].astype(BF16), b_ref[...].astype(BF16), dims, preferred_element_type=F32)

        def finish(r):
            if has_add:
                r = add_ref[...] + r
            o_ref[...] = r.astype(out_dtype)

        if nk == 1:
            finish(part)
            return
        acc_ref = refs[-1]
        k = pl.program_id(2)

        @pl.when(k == 0)
        def _():
            acc_ref[...] = part

        @pl.when(k > 0)
        def _():
            acc_ref[...] += part

        @pl.when(k == nk - 1)
        def _():
            finish(acc_ref[...])

    a_spec = pl.BlockSpec((tk, tm), lambda i, j, k: (k, i)) if ta else pl.BlockSpec((tm, tk), lambda i, j, k: (i, k))
    b_spec = pl.BlockSpec((tn, tk), lambda i, j, k: (j, k)) if tb else pl.BlockSpec((tk, tn), lambda i, j, k: (k, j))
    o_spec = pl.BlockSpec((tm, tn), lambda i, j, k: (i, j))
    in_specs = [a_spec, b_spec] + ([o_spec] if has_add else [])
    args = (a, b) + ((add,) if has_add else ())
    return pl.pallas_call(
        body, name=name, grid=(m_dim // tm, n_dim // tn, nk),
        in_specs=in_specs, out_specs=o_spec,
        out_shape=jax.ShapeDtypeStruct((m_dim, n_dim), out_dtype),
        scratch_shapes=[pltpu.VMEM((tm, tn), F32)] if nk > 1 else [],
        compiler_params=_params(("parallel", "parallel", "arbitrary")),
    )(*args)


def _mm_core(name, a, b, a_spec, b_spec, o_spec, out_shape, grid, dims, *, add=None, out_dtype=F32, rider=None):
    nk = grid[2]
    has_add = add is not None
    n_in = 3 if has_add else 2
    acc_shape = tuple(d for d in o_spec.block_shape if d is not None)
    extra, extra_shapes, extra_sems = _rider_parts(rider)
    n_w = len(extra)

    def body(*refs):
        a_ref, b_ref = refs[:2]
        o_ref = refs[n_in + n_w]
        step = (pl.program_id(0) * grid[1] + pl.program_id(1)) * nk + pl.program_id(2)
        begin, end = _rider_hooks(rider, refs[n_in:n_in + n_w], refs[n_in + n_w + 1:n_in + 2 * n_w + 1], refs[-2:],
                                  step, grid[0] * grid[1] * nk)
        begin()
        part = lax.dot_general(a_ref[...].astype(BF16), b_ref[...].astype(BF16), dims, preferred_element_type=F32)

        def finish(r):
            if has_add:
                r = refs[2][...] + r
            o_ref[...] = r.astype(out_dtype)

        if nk == 1:
            finish(part)
        else:
            acc_ref = refs[n_in + 2 * n_w + 1]
            k = pl.program_id(2)

            @pl.when(k == 0)
            def _():
                acc_ref[...] = part

            @pl.when(k > 0)
            def _():
                acc_ref[...] += part

            @pl.when(k == nk - 1)
            def _():
                finish(acc_ref[...])
        end()

    in_specs = [a_spec, b_spec] + ([o_spec] if has_add else []) + [ANY] * n_w
    args = (a, b) + ((add,) if has_add else ()) + extra
    res = pl.pallas_call(
        body, name=name, grid=grid, in_specs=in_specs, out_specs=[o_spec] + [ANY] * n_w,
        out_shape=[jax.ShapeDtypeStruct(out_shape, out_dtype)] + extra_shapes,
        scratch_shapes=([pltpu.VMEM(acc_shape, F32)] if nk > 1 else []) + extra_sems,
        compiler_params=_params(("arbitrary",) * 3 if n_w else ("parallel", "parallel", "arbitrary")),
    )(*args)
    return (res[0], res[1:]) if n_w else res[0]


def _mm_cs(a, w3, *, name, add=None, out_dtype=F32):
    m_dim, k_dim = a.shape
    _, _, n4 = w3.shape
    tm, tn, tk = _tile(m_dim, MM_CAP if add is not None else TOK_CAP), _tile(n4, MM_CAP), _tile(k_dim, MM_CAP)
    npb = n4 // tn
    return _mm_core(name, a, w3,
                    pl.BlockSpec((tm, tk), lambda i, j, k: (i, k)),
                    pl.BlockSpec((None, tk, tn), lambda i, j, k: (j // npb, k, j % npb)),
                    pl.BlockSpec((tm, tn), lambda i, j, k: (i, j)),
                    (m_dim, N_CHIPS * n4), (m_dim // tm, N_CHIPS * npb, k_dim // tk), NN, add=add, out_dtype=out_dtype)


def _mm_cs_dx(dy, w3, *, name, out_dtype=F32, rider=None):
    m_dim, _ = dy.shape
    _, k_dim, n4 = w3.shape
    tm, tkw, tn = _tile(m_dim, MM_CAP), _tile(k_dim, MM_CAP), _tile(n4, MM_CAP)
    npb = n4 // tn
    return _mm_core(name, dy, w3,
                    pl.BlockSpec((tm, tn), lambda i, j, k: (i, k)),
                    pl.BlockSpec((None, tkw, tn), lambda i, j, k: (k // npb, j, k % npb)),
                    pl.BlockSpec((tm, tkw), lambda i, j, k: (i, j)),
                    (m_dim, k_dim), (m_dim // tm, k_dim // tkw, N_CHIPS * npb), NT, out_dtype=out_dtype, rider=rider)


def _mm_cs_dw(a, dy, *, name):
    m_dim, k_dim = a.shape
    n4 = dy.shape[1] // N_CHIPS
    tmk, tn, tk = _tile(k_dim, MM_CAP), _tile(n4, MM_CAP), _tile(m_dim, TOK_CAP)
    npb = n4 // tn
    return _mm_core(name, a, dy,
                    pl.BlockSpec((tk, tmk), lambda i, j, k: (k, i)),
                    pl.BlockSpec((tk, tn), lambda i, j, k: (k, j)),
                    pl.BlockSpec((None, tmk, tn), lambda i, j, k: (j // npb, i, j % npb)),
                    (N_CHIPS, k_dim, n4), (k_dim // tmk, N_CHIPS * npb, m_dim // tk), TN)


def _mm_ffn_up_dw(n, d3, *, name):
    t_dim, d = n.shape
    _, _, f4 = d3.shape
    tk = _tile(t_dim, TOK_CAP)
    return _mm_core(name, n, d3,
                    pl.BlockSpec((tk, d), lambda i, j, k: (k, 0)),
                    pl.BlockSpec((None, tk, f4), lambda i, j, k: (j, k, 0)),
                    pl.BlockSpec((None, d, f4), lambda i, j, k: (j, 0, 0)),
                    (N_CHIPS, d, f4), (1, N_CHIPS, t_dim // tk), TN)


def _mm_ffn_up_dx(d3, w3, *, name, add=None):
    _, t_dim, f4 = d3.shape
    _, d, _ = w3.shape
    tm = _tile(t_dim, MM_CAP)
    return _mm_core(name, d3, w3,
                    pl.BlockSpec((None, tm, f4), lambda i, j, k: (k, i, 0)),
                    pl.BlockSpec((None, d, f4), lambda i, j, k: (k, 0, 0)),
                    pl.BlockSpec((tm, d), lambda i, j, k: (i, 0)),
                    (t_dim, d), (t_dim // tm, 1, N_CHIPS), NT, add=add)


def _mm_ffn_down_dw(act3, dh, *, name):
    _, t_dim, f4 = act3.shape
    d = dh.shape[1]
    tk = _tile(t_dim, TOK_CAP)
    return _mm_core(name, act3, dh,
                    pl.BlockSpec((None, tk, f4), lambda i, j, k: (i, k, 0)),
                    pl.BlockSpec((tk, d), lambda i, j, k: (k, 0)),
                    pl.BlockSpec((None, f4, d), lambda i, j, k: (i, 0, 0)),
                    (N_CHIPS, f4, d), (N_CHIPS, 1, t_dim // tk), TN)


def _rms_fwd(x, g, *, name, tt=512, rider=None):
    t_dim, d = x.shape
    tt = _tile(t_dim, tt, 8)
    extra, extra_shapes, extra_sems = _rider_parts(rider)
    n_w = len(extra)

    def body(*refs):
        x_ref, g_ref, o_ref = refs[0], refs[1], refs[2 + n_w]
        begin, end = _rider_hooks(rider, refs[2:2 + n_w], refs[3 + n_w:3 + 2 * n_w], refs[-2:], pl.program_id(0),
                                  t_dim // tt)
        begin()
        xv = x_ref[...]
        r = lax.rsqrt(jnp.mean(xv * xv, axis=-1, keepdims=True) + RMS_EPS)
        o_ref[...] = ((xv * r) * g_ref[...]).astype(o_ref.dtype)
        end()

    res = pl.pallas_call(
        body, name=name, grid=(t_dim // tt,),
        in_specs=[pl.BlockSpec((tt, d), lambda i: (i, 0)), pl.BlockSpec((1, d), lambda i: (0, 0))] + [ANY] * n_w,
        out_specs=[pl.BlockSpec((tt, d), lambda i: (i, 0))] + [ANY] * n_w,
        out_shape=[jax.ShapeDtypeStruct((t_dim, d), BF16)] + extra_shapes,
        scratch_shapes=extra_sems,
        compiler_params=_params(("arbitrary",) if n_w else ("parallel",)),
    )(x, g, *extra)
    return (res[0], res[1:]) if n_w else res[0]


def _rms_bwd(x, g, dy, add, *, name, tt=512, rider=None):
    t_dim, d = x.shape
    tt = _tile(t_dim, tt, 8)
    has_add = add is not None
    n_in = 4 if has_add else 3
    extra, extra_shapes, extra_sems = _rider_parts(rider)
    n_w = len(extra)

    def body(*refs):
        x_ref, g_ref, dy_ref = refs[:3]
        add_ref = refs[3] if has_add else None
        dx_ref, dg_ref = refs[n_in + n_w:n_in + n_w + 2]
        begin, end = _rider_hooks(rider, refs[n_in:n_in + n_w], refs[n_in + n_w + 2:n_in + 2 * n_w + 2], refs[-2:],
                                  pl.program_id(0), t_dim // tt)
        begin()
        xv = x_ref[...]
        dyv = dy_ref[...].astype(F32)
        r = lax.rsqrt(jnp.mean(xv * xv, axis=-1, keepdims=True) + RMS_EPS)
        xh = xv * r
        u = dyv * g_ref[...]
        dx = r * (u - xh * jnp.mean(u * xh, axis=-1, keepdims=True))
        if has_add:
            dx = add_ref[...] + dx
        dx_ref[...] = dx

        @pl.when(pl.program_id(0) == 0)
        def _():
            dg_ref[...] = jnp.zeros_like(dg_ref)

        dg_ref[...] += jnp.sum(dyv * xh, axis=0, keepdims=True)
        end()

    row = pl.BlockSpec((tt, d), lambda i: (i, 0))
    vec = pl.BlockSpec((1, d), lambda i: (0, 0))
    in_specs = [row, vec, row] + ([row] if has_add else []) + [ANY] * n_w
    args = (x, g, dy) + ((add,) if has_add else ()) + extra
    res = pl.pallas_call(
        body, name=name, grid=(t_dim // tt,),
        in_specs=in_specs, out_specs=[row, vec] + [ANY] * n_w,
        out_shape=[jax.ShapeDtypeStruct((t_dim, d), F32), jax.ShapeDtypeStruct((1, d), F32)] + extra_shapes,
        scratch_shapes=extra_sems,
        compiler_params=_params(("arbitrary",)),
    )(*args)
    return (res[0], res[1], res[2:]) if n_w else (res[0], res[1])


def _down_final(act3, wd3, h, g, target, *, tt=1024):
    n_s, t_dim, f4 = act3.shape
    d = h.shape[1]
    n_steps = t_dim // tt

    def body(a_ref, w_ref, h_ref, g_ref, t_ref, loss_ref, dh_ref, dg_ref, acc_ref, sq_ref):
        i, k = pl.program_id(0), pl.program_id(1)
        part = jnp.dot(a_ref[...], w_ref[...], preferred_element_type=F32)

        @pl.when(k == 0)
        def _():
            acc_ref[...] = part

        @pl.when(k > 0)
        def _():
            acc_ref[...] += part

        @pl.when(jnp.logical_and(i == 0, k == 0))
        def _():
            dg_ref[...] = jnp.zeros_like(dg_ref)
            sq_ref[...] = jnp.zeros_like(sq_ref)

        @pl.when(k == n_s - 1)
        def _():
            xv = h_ref[...] + acc_ref[...]
            gv = g_ref[...]
            r = lax.rsqrt(jnp.mean(xv * xv, axis=-1, keepdims=True) + RMS_EPS)
            xh = xv * r
            err = xh * gv - t_ref[...]
            dyv = err * (1.0 / d)
            u = dyv * gv
            dh_ref[...] = r * (u - xh * jnp.mean(u * xh, axis=-1, keepdims=True))
            dg_ref[...] += jnp.sum(dyv * xh, axis=0, keepdims=True)
            sq_ref[...] += jnp.sum(err * err, axis=0, keepdims=True)

        @pl.when(jnp.logical_and(i == n_steps - 1, k == n_s - 1))
        def _():
            total = jnp.sum(sq_ref[...], axis=-1, keepdims=True) * (0.5 / d)
            loss_ref[...] = jnp.broadcast_to(total, loss_ref.shape)

    row = pl.BlockSpec((tt, d), lambda i, k: (i, 0))
    vec = pl.BlockSpec((1, d), lambda i, k: (0, 0))
    return pl.pallas_call(
        body, name="down_final_loss", grid=(n_steps, n_s),
        in_specs=[pl.BlockSpec((None, tt, f4), lambda i, k: (k, i, 0)),
                  pl.BlockSpec((None, f4, d), lambda i, k: (k, 0, 0)), row, vec, row],
        out_specs=[pl.BlockSpec((1, LANES), lambda i, k: (0, 0)), row, vec],
        out_shape=[jax.ShapeDtypeStruct((1, LANES), F32), jax.ShapeDtypeStruct((t_dim, d), F32),
                   jax.ShapeDtypeStruct((1, d), F32)],
        scratch_shapes=[pltpu.VMEM((tt, d), F32), pltpu.VMEM((1, d), F32)],
        compiler_params=_params(("arbitrary", "arbitrary")),
    )(act3, wd3, h, g, target)


def _rope_table(pos, inv_lane, sel_a, sel_b, *, tt=512):
    t_dim = pos.shape[0]

    def body(p_ref, f_ref, a_ref, b_ref, c_ref, s_ref):
        ang = p_ref[...] * f_ref[...]
        on = (a_ref[...] + b_ref[...]) > 0.0
        c_ref[...] = jnp.where(on, jnp.cos(ang), 1.0)
        s_ref[...] = jnp.where(on, jnp.sin(ang), 0.0)

    vec = pl.BlockSpec((1, LANES), lambda i: (0, 0))
    row = pl.BlockSpec((tt, LANES), lambda i: (i, 0))
    shp = jax.ShapeDtypeStruct((t_dim, LANES), F32)
    return pl.pallas_call(
        body, name="rope_table", grid=(t_dim // tt,),
        in_specs=[pl.BlockSpec((tt, 1), lambda i: (i, 0)), vec, vec, vec],
        out_specs=[row, row], out_shape=[shp, shp],
        compiler_params=_params(("parallel",)),
    )(pos, inv_lane, sel_a, sel_b)


def _rotate(xv, cs, sn, sa, sb):
    half = ROPE_DIM // 2
    up = pltpu.roll(xv, LANES - half, 1)
    dn = pltpu.roll(xv, half, 1)
    return xv * cs + (dn * sb - up * sa) * sn


def _head_masks():
    h1 = lax.broadcasted_iota(jnp.int32, (1, LANES), 1) < HEAD_DIM
    return h1, jnp.logical_not(h1)


def _split_heads(xv, h1, h2):
    return jnp.where(h1, xv, 0.0).astype(BF16), jnp.where(h2, xv, 0.0).astype(BF16)


def _tri_masks():
    r = lax.broadcasted_iota(jnp.int32, (BLOCK, BLOCK), 0)
    c = lax.broadcasted_iota(jnp.int32, (BLOCK, BLOCK), 1)
    return c <= r, r <= c


def _stream_rows(start, dil):
    if dil == 1:
        return pl.ds(pl.multiple_of(start, BLOCK), BLOCK)
    return pl.ds(start, BLOCK, stride=dil)


def _dil_tile(idx, dil, nb):
    r = idx // nb
    n = idx % nb
    return (_stream_rows(r + dil * BLOCK * n, dil), _stream_rows(r + dil * BLOCK * jnp.maximum(n - 1, 0), dil),
            n > 0)


def _dil_specs(b_dim, s_dim):
    def col(c0):
        return pl.BlockSpec((None, s_dim, LANES),lambda b, h: (b, 0, c0 + h))
    tab = pl.BlockSpec((None, s_dim, LANES),lambda b, h: (b, 0, 0))
    vec = pl.BlockSpec((1, LANES), lambda b, h: (0, 0))
    return col, tab, vec


def _dil_fwd(proj3, cs3, sn3, sel_a, sel_b, rider=None):
    b_dim, s_dim, _ = proj3.shape
    scale = HEAD_DIM ** -0.5
    n_pat = len(DIL_PATTERNS)
    extra, extra_shapes, extra_sems = _rider_parts(rider)
    n_w = len(extra)
    n_steps = b_dim * PAIRS

    def body(*refs):
        q_ref, k_ref, v_ref, cs_ref, sn_ref, sa_ref, sb_ref = refs[:7]
        o16_ref, o32_ref, l_ref = refs[7 + n_w:10 + n_w]
        qr, kr = refs[10 + 2 * n_w:12 + 2 * n_w]
        per_pattern = refs[12 + 2 * n_w:12 + 2 * n_w + 2 * n_pat]
        og, lg = per_pattern[:n_pat], per_pattern[n_pat:]
        step = pl.program_id(0) * PAIRS + pl.program_id(1)
        begin, end = _rider_hooks(rider, refs[7:7 + n_w], refs[10 + n_w:10 + 2 * n_w], refs[-2:], step, n_steps)
        begin()
        h1, h2 = _head_masks()
        cur_ok, prev_ok = _tri_masks()
        sa, sb = sa_ref[...], sb_ref[...]

        def prep(j, _):
            rows = pl.ds(pl.multiple_of(j * BLOCK, BLOCK), BLOCK)
            cs, sn = cs_ref[rows, :], sn_ref[rows, :]
            qr[rows, :] = _rotate(q_ref[rows, :], cs, sn, sa, sb) * scale
            kr[rows, :] = _rotate(k_ref[rows, :], cs, sn, sa, sb)
            return 0

        lax.fori_loop(0, s_dim // BLOCK, prep, 0)

        for g, (_, dil) in enumerate(DIL_PATTERNS):
            nb = s_dim // dil // BLOCK

            def some(bi, _, g=g, dil=dil, nb=nb):
                tiles = [_dil_tile(bi * DIL_BATCH + t, dil, nb) for t in range(DIL_BATCH)]
                rows = [t[0] for t in tiles]
                q1, q2 = _split_heads(jnp.stack([qr[rw, :] for rw in rows]), h1, h2)
                kc = jnp.stack([kr[rw, :] for rw in rows]).astype(BF16)
                vc1, vc2 = _split_heads(jnp.stack([v_ref[rw, :] for rw in rows]), h1, h2)
                if nb > 1:
                    kp = jnp.stack([kr[t[1], :] for t in tiles]).astype(BF16)
                    vp1, vp2 = _split_heads(jnp.stack([v_ref[t[1], :] for t in tiles]), h1, h2)
                    p_ok = jnp.stack([jnp.logical_and(prev_ok, t[2]) for t in tiles])

                def head(qh, vch, vph):
                    sc = jnp.where(cur_ok, lax.dot_general(qh, kc, BNT, preferred_element_type=F32), -jnp.inf)
                    m = jnp.max(sc, axis=-1, keepdims=True)
                    if nb > 1:
                        sp = jnp.where(p_ok, lax.dot_general(qh, kp, BNT, preferred_element_type=F32), -jnp.inf)
                        m = jnp.maximum(m, jnp.max(sp, axis=-1, keepdims=True))
                    pc = jnp.exp(sc - m)
                    den = jnp.sum(pc, axis=-1, keepdims=True)
                    acc = lax.dot_general(pc.astype(BF16), vch, BNN, preferred_element_type=F32)
                    if nb > 1:
                        pp = jnp.exp(sp - m)
                        den = den + jnp.sum(pp, axis=-1, keepdims=True)
                        acc = acc + lax.dot_general(pp.astype(BF16), vph, BNN, preferred_element_type=F32)
                    return acc / den, m + jnp.log(den)

                o1, l1 = head(q1, vc1, vp1 if nb > 1 else None)
                o2, l2 = head(q2, vc2, vp2 if nb > 1 else None)
                o, l = o1 + o2, jnp.where(h1, l1, l2)
                for t, rw in enumerate(rows):
                    og[g][rw, :] = o[t]
                    lg[g][rw, :] = l[t]
                return 0

            lax.fori_loop(0, dil * nb // DIL_BATCH, some, 0)

        def comb(j, _):
            rows = pl.ds(pl.multiple_of(j * BLOCK, BLOCK), BLOCK)
            ls = [lg[g][rows, :] for g in range(n_pat)]
            m = jnp.maximum(jnp.maximum(ls[0], ls[1]), ls[2])
            es = [jnp.exp(l - m) for l in ls]
            den = es[0] + es[1] + es[2]
            o = (es[0] * og[0][rows, :] + es[1] * og[1][rows, :] + es[2] * og[2][rows, :]) / den
            o16_ref[rows, :] = o.astype(BF16)
            o32_ref[rows, :] = o
            l_ref[rows, :] = m + jnp.log(den)
            return 0

        lax.fori_loop(0, s_dim // BLOCK, comb, 0)
        end()

    col, tab, vec = _dil_specs(b_dim, s_dim)
    out = pl.BlockSpec((None, s_dim, LANES),lambda b, h: (b, 0, h))
    shp = (b_dim, s_dim, ATT_WIDTH)
    res = pl.pallas_call(
        body, name="dil_fwd", grid=(b_dim, PAIRS),
        in_specs=[col(COL_QA), col(COL_KA), col(COL_VA), tab, tab, vec, vec] + [ANY] * n_w,
        out_specs=[out, out, out] + [ANY] * n_w,
        out_shape=[jax.ShapeDtypeStruct(shp, BF16), jax.ShapeDtypeStruct(shp, F32), jax.ShapeDtypeStruct(shp, F32)]
        + extra_shapes,
        scratch_shapes=[pltpu.VMEM((s_dim, LANES), F32)] * (2 + 2 * n_pat) + extra_sems,
        compiler_params=_params(("arbitrary", "arbitrary")),
    )(proj3, proj3, proj3, cs3, sn3, sel_a, sel_b, *extra)
    return res[:3], res[3:]


def _dil_bwd(proj3, cs3, sn3, sel_a, sel_b, do3, o3, lse3, rider=None):
    b_dim, s_dim, _ = proj3.shape
    scale = HEAD_DIM ** -0.5
    extra, extra_shapes, extra_sems = _rider_parts(rider)
    n_w = len(extra)

    def body(*refs):
        q_ref, k_ref, v_ref, cs_ref, sn_ref, sa_ref, sb_ref, do_ref, o_ref, l_ref = refs[:10]
        dq_ref, dk_ref, dv_ref = refs[10 + n_w:13 + n_w]
        qr, kr, dqa, dka, dva = refs[13 + 2 * n_w:18 + 2 * n_w]
        step = pl.program_id(0) * PAIRS + pl.program_id(1)
        begin, end = _rider_hooks(rider, refs[10:10 + n_w], refs[13 + n_w:13 + 2 * n_w], refs[-2:], step,
                                  b_dim * PAIRS)
        begin()
        h1, h2 = _head_masks()
        cur_ok, prev_ok = _tri_masks()
        sa, sb = sa_ref[...], sb_ref[...]

        def prep(j, _):
            rows = pl.ds(pl.multiple_of(j * BLOCK, BLOCK), BLOCK)
            cs, sn = cs_ref[rows, :], sn_ref[rows, :]
            qr[rows, :] = _rotate(q_ref[rows, :], cs, sn, sa, sb) * scale
            kr[rows, :] = _rotate(k_ref[rows, :], cs, sn, sa, sb)
            zero = jnp.zeros((BLOCK, LANES), F32)
            dqa[rows, :] = zero
            dka[rows, :] = zero
            dva[rows, :] = zero
            return 0

        lax.fori_loop(0, s_dim // BLOCK, prep, 0)

        for _, dil in DIL_PATTERNS:
            nb = s_dim // dil // BLOCK

            def some(bi, _, dil=dil, nb=nb):
                tiles = [_dil_tile(bi * DIL_BATCH + t, dil, nb) for t in range(DIL_BATCH)]
                rows = [t[0] for t in tiles]
                q1, q2 = _split_heads(jnp.stack([qr[rw, :] for rw in rows]), h1, h2)
                dof = jnp.stack([do_ref[rw, :] for rw in rows])
                do1, do2 = _split_heads(dof, h1, h2)
                prod = dof * jnp.stack([o_ref[rw, :] for rw in rows])
                delta1 = jnp.sum(jnp.where(h1, prod, 0.0), axis=-1, keepdims=True)
                delta2 = jnp.sum(jnp.where(h2, prod, 0.0), axis=-1, keepdims=True)
                lt = jnp.stack([l_ref[rw, :] for rw in rows])
                lse1 = jnp.max(jnp.where(h1, lt, -jnp.inf), axis=-1, keepdims=True)
                lse2 = jnp.max(jnp.where(h2, lt, -jnp.inf), axis=-1, keepdims=True)

                def side(krows, ok):
                    kf = jnp.stack([kr[kw, :] for kw in krows])
                    k16 = kf.astype(BF16)
                    k1, k2 = _split_heads(kf, h1, h2)
                    v16 = jnp.stack([v_ref[kw, :] for kw in krows]).astype(BF16)

                    def head(qh, doh, lse, delta):
                        sc = lax.dot_general(qh, k16, BNT, preferred_element_type=F32)
                        p = jnp.where(ok, jnp.exp(sc - lse), 0.0)
                        dp = lax.dot_general(doh, v16, BNT, preferred_element_type=F32)
                        return p.astype(BF16), (p * (dp - delta)).astype(BF16)

                    p1, ds1 = head(q1, do1, lse1, delta1)
                    p2, ds2 = head(q2, do2, lse2, delta2)
                    dv = (lax.dot_general(p1, do1, BTN, preferred_element_type=F32)
                          + lax.dot_general(p2, do2, BTN, preferred_element_type=F32))
                    dk = (lax.dot_general(ds1, q1, BTN, preferred_element_type=F32)
                          + lax.dot_general(ds2, q2, BTN, preferred_element_type=F32))
                    for t, kw in enumerate(krows):
                        dva[kw, :] += dv[t]
                        dka[kw, :] += dk[t]
                    return (lax.dot_general(ds1, k1, BNN, preferred_element_type=F32)
                            + lax.dot_general(ds2, k2, BNN, preferred_element_type=F32))

                dq = side(rows, cur_ok)
                if nb > 1:
                    dq = dq + side([t[1] for t in tiles], jnp.stack([jnp.logical_and(prev_ok, t[2]) for t in tiles]))
                for t, rw in enumerate(rows):
                    dqa[rw, :] += dq[t] * scale
                return 0

            lax.fori_loop(0, dil * nb // DIL_BATCH, some, 0)

        def finish(j, _):
            rows = pl.ds(pl.multiple_of(j * BLOCK, BLOCK), BLOCK)
            cs, sn = cs_ref[rows, :], -sn_ref[rows, :]
            dq_ref[rows, :] = _rotate(dqa[rows, :], cs, sn, sa, sb).astype(BF16)
            dk_ref[rows, :] = _rotate(dka[rows, :], cs, sn, sa, sb).astype(BF16)
            dv_ref[rows, :] = dva[rows, :].astype(BF16)
            return 0

        lax.fori_loop(0, s_dim // BLOCK, finish, 0)
        end()

    col, tab, vec = _dil_specs(b_dim, s_dim)
    out = pl.BlockSpec((None, s_dim, LANES),lambda b, h: (b, 0, h))
    shp = jax.ShapeDtypeStruct((b_dim, s_dim, ATT_WIDTH), BF16)
    acc = pltpu.VMEM((s_dim, LANES), F32)
    res = pl.pallas_call(
        body, name="dil_bwd", grid=(b_dim, PAIRS),
        in_specs=[col(COL_QA), col(COL_KA), col(COL_VA), tab, tab, vec, vec, out, out, out] + [ANY] * n_w,
        out_specs=[out, out, out] + [ANY] * n_w, out_shape=[shp, shp, shp] + extra_shapes,
        scratch_shapes=[acc, acc, acc, acc, acc] + extra_sems,
        compiler_params=_params(("arbitrary", "arbitrary")),
    )(proj3, proj3, proj3, cs3, sn3, sel_a, sel_b, do3, o3, lse3, *extra)
    return res[:3], res[3:]


def _split_dot(x, tri):
    hi = x.astype(BF16)
    lo = (x - hi.astype(F32)).astype(BF16)
    return jnp.dot(hi, tri, preferred_element_type=F32) + jnp.dot(lo, tri, preferred_element_type=F32)


def _log_sigmoid(z):
    return jnp.minimum(z, 0.0) - jnp.log(1.0 + jnp.exp(-jnp.abs(z)))


def _sb_scores(qh, k16, valid):
    z = lax.dot_general(qh, k16, NT, preferred_element_type=F32)
    ls = _log_sigmoid(z)
    l1m = ls - z
    return ls, (l1m if valid is None else jnp.where(valid, l1m, 0.0))


def _sb_consts():
    r = lax.broadcasted_iota(jnp.int32, (BLOCK, BLOCK), 0)
    c = lax.broadcasted_iota(jnp.int32, (BLOCK, BLOCK), 1)
    after = (r > c).astype(BF16)
    before = (r < c).astype(BF16)
    qrow = lax.broadcasted_iota(jnp.int32, (SB_ROWS, BLOCK), 0)
    kcol = lax.broadcasted_iota(jnp.int32, (SB_ROWS, BLOCK), 1)
    return after, before, qrow, kcol


def _below(whole, lo, delta):
    if lo == 0:
        return whole + delta
    return whole + jnp.concatenate([jnp.zeros((lo,) + delta.shape[1:], delta.dtype), delta], axis=0)


def _pairs_loop(n_blocks, step, carry):
    def several(i, c):
        for j in range(SB_STEP):
            c = step(SB_STEP * i + j, c)
        return c

    return lax.fori_loop(0, n_blocks // SB_STEP, several, carry)


def _sb_fwd(proj3, rider=None):
    b_dim, s_dim, _ = proj3.shape
    scale = HEAD_DIM ** -0.5
    per = SB_ROWS // BLOCK
    extra, extra_shapes, extra_sems = _rider_parts(rider)
    n_w = len(extra)

    def body(*refs):
        q_ref, k_ref, v_ref = refs[:3]
        o_ref = refs[3 + n_w]
        step = pl.program_id(0) * PAIRS + pl.program_id(1)
        begin, end = _rider_hooks(rider, refs[3:3 + n_w], refs[4 + n_w:4 + 2 * n_w], refs[-2:], step, b_dim * PAIRS)
        begin()
        h1, h2 = _head_masks()
        after, _, qrow, kcol = _sb_consts()

        def qloop(qi, _):
            rows = pl.ds(pl.multiple_of(qi * SB_ROWS, SB_ROWS), SB_ROWS)
            q1, q2 = _split_heads(q_ref[rows, :] * scale, h1, h2)
            first = qi * per

            def block(kb, carry, lo):
                acc, run1, run2 = carry
                krows = pl.ds(pl.multiple_of(kb * BLOCK, BLOCK), BLOCK)
                k16 = k_ref[krows, :].astype(BF16)
                v1, v2 = _split_heads(v_ref[krows, :], h1, h2)
                valid = None if lo is None else kcol[:SB_ROWS - lo] < qrow[:SB_ROWS - lo]
                lo = lo or 0

                def head(qh, vh, run):
                    ls, l1m = _sb_scores(qh[lo:], k16, valid)
                    a = jnp.exp(ls + _split_dot(l1m, after) + run[lo:])
                    if valid is not None:
                        a = jnp.where(valid, a, 0.0)
                    return (jnp.dot(a.astype(BF16), vh, preferred_element_type=F32),
                            _below(run, lo, jnp.sum(l1m, axis=-1, keepdims=True)))

                o1, run1 = head(q1, v1, run1)
                o2, run2 = head(q2, v2, run2)
                return _below(acc, lo, o1 + o2), run1, run2

            zcol = jnp.zeros((SB_ROWS, 1), F32)
            carry = (jnp.zeros((SB_ROWS, LANES), F32), zcol, zcol)
            for kl in reversed(range(per)):
                carry = block(first + kl, carry, kl * BLOCK)
            acc, _, _ = _pairs_loop(first, lambda i, c: block(first - 1 - i, c, None), carry)
            o_ref[rows, :] = acc.astype(BF16)
            return 0

        lax.fori_loop(0, s_dim // SB_ROWS, qloop, 0)
        end()

    def col(c0):
        return pl.BlockSpec((None, s_dim, LANES),lambda b, h: (b, 0, c0 + h))

    res = pl.pallas_call(
        body, name="sb_fwd", grid=(b_dim, PAIRS),
        in_specs=[col(COL_QB), col(COL_KB), col(COL_VB)] + [ANY] * n_w, out_specs=[col(0)] + [ANY] * n_w,
        out_shape=[jax.ShapeDtypeStruct((b_dim, s_dim, ATT_WIDTH), BF16)] + extra_shapes,
        scratch_shapes=extra_sems,
        compiler_params=_params(("arbitrary", "arbitrary")),
    )(proj3, proj3, proj3, *extra)
    return res[0], res[1:]


def _sb_bwd(proj3, do3, rider=None):
    b_dim, s_dim, _ = proj3.shape
    scale = HEAD_DIM ** -0.5
    per = SB_ROWS // BLOCK
    nkb_max = s_dim // BLOCK
    extra, extra_shapes, extra_sems = _rider_parts(rider)
    n_w = len(extra)

    def body(*refs):
        q_ref, k_ref, v_ref, do_ref = refs[:4]
        dq_ref, dk_ref, dv_ref = refs[4 + n_w:7 + n_w]
        dka, dva, e_ref, sg_ref = refs[7 + 2 * n_w:11 + 2 * n_w]
        step = pl.program_id(0) * PAIRS + pl.program_id(1)
        begin, end = _rider_hooks(rider, refs[4:4 + n_w], refs[7 + n_w:7 + 2 * n_w], refs[-2:], step, b_dim * PAIRS)
        begin()
        h1, h2 = _head_masks()
        after, before, qrow, kcol = _sb_consts()
        dka[...] = jnp.zeros_like(dka)
        dva[...] = jnp.zeros_like(dva)

        def qloop(qi, _):
            rows = pl.ds(pl.multiple_of(qi * SB_ROWS, SB_ROWS), SB_ROWS)
            q1, q2 = _split_heads(q_ref[rows, :] * scale, h1, h2)
            do1, do2 = _split_heads(do_ref[rows, :].astype(F32), h1, h2)
            first = qi * per

            def pass1(kb, carry, lo):
                run1, run2 = carry
                krows = pl.ds(pl.multiple_of(kb * BLOCK, BLOCK), BLOCK)
                k16 = k_ref[krows, :].astype(BF16)
                v16 = v_ref[krows, :].astype(BF16)
                valid = None if lo is None else kcol[:SB_ROWS - lo] < qrow[:SB_ROWS - lo]
                lo = lo or 0
                part = pl.ds(lo, SB_ROWS - lo)

                def head(h, qh, doh, run):
                    ls, l1m = _sb_scores(qh[lo:], k16, valid)
                    a = jnp.exp(ls + _split_dot(l1m, after) + run[lo:])
                    if valid is not None:
                        a = jnp.where(valid, a, 0.0)
                    da = lax.dot_general(doh[lo:], v16, NT, preferred_element_type=F32)
                    e_ref[h, kb, part, :] = a * da
                    sg_ref[h, kb, part, :] = jnp.exp(ls)
                    return a.astype(BF16), _below(run, lo, jnp.sum(l1m, axis=-1, keepdims=True))

                a1, run1 = head(0, q1, do1, run1)
                a2, run2 = head(1, q2, do2, run2)
                dva[krows, :] += (lax.dot_general(a1, do1[lo:], TN, preferred_element_type=F32)
                                  + lax.dot_general(a2, do2[lo:], TN, preferred_element_type=F32))
                return run1, run2

            zcol = jnp.zeros((SB_ROWS, 1), F32)
            carry = (zcol, zcol)
            for kl in reversed(range(per)):
                carry = pass1(first + kl, carry, kl * BLOCK)
            _pairs_loop(first, lambda i, c: pass1(first - 1 - i, c, None), carry)

            def pass2(kb, carry, lo):
                dq, pre1, pre2 = carry
                krows = pl.ds(pl.multiple_of(kb * BLOCK, BLOCK), BLOCK)
                k1, k2 = _split_heads(k_ref[krows, :], h1, h2)
                valid = None if lo is None else kcol[:SB_ROWS - lo] < qrow[:SB_ROWS - lo]
                lo = lo or 0
                part = pl.ds(lo, SB_ROWS - lo)

                def head(h, pre):
                    ev = e_ref[h, kb, part, :]
                    sg = sg_ref[h, kb, part, :]
                    dz = ev * (1.0 - sg) - (_split_dot(ev, before) + pre[lo:]) * sg
                    if valid is not None:
                        dz = jnp.where(valid, dz, 0.0)
                    return dz.astype(BF16), _below(pre, lo, jnp.sum(ev, axis=-1, keepdims=True))

                dz1, pre1 = head(0, pre1)
                dz2, pre2 = head(1, pre2)
                dka[krows, :] += (lax.dot_general(dz1, q1[lo:], TN, preferred_element_type=F32)
                                  + lax.dot_general(dz2, q2[lo:], TN, preferred_element_type=F32))
                dq = _below(dq, lo, jnp.dot(dz1, k1, preferred_element_type=F32)
                            + jnp.dot(dz2, k2, preferred_element_type=F32))
                return dq, pre1, pre2

            carry = _pairs_loop(first, lambda i, c: pass2(i, c, None), (jnp.zeros((SB_ROWS, LANES), F32), zcol, zcol))
            for kl in range(per):
                carry = pass2(first + kl, carry, kl * BLOCK)
            dq = carry[0]
            dq_ref[rows, :] = (dq * scale).astype(BF16)
            return 0

        lax.fori_loop(0, s_dim // SB_ROWS, qloop, 0)
        dk_ref[...] = dka[...].astype(BF16)
        dv_ref[...] = dva[...].astype(BF16)
        end()

    def col(c0):
        return pl.BlockSpec((None, s_dim, LANES),lambda b, h: (b, 0, c0 + h))

    shp = jax.ShapeDtypeStruct((b_dim, s_dim, ATT_WIDTH), BF16)
    acc = pltpu.VMEM((s_dim, LANES), F32)
    strip = pltpu.VMEM((2, nkb_max, SB_ROWS, BLOCK), F32)
    res = pl.pallas_call(
        body, name="sb_bwd", grid=(b_dim, PAIRS),
        in_specs=[col(COL_QB), col(COL_KB), col(COL_VB), col(0)] + [ANY] * n_w,
        out_specs=[col(0), col(0), col(0)] + [ANY] * n_w,
        out_shape=[shp, shp, shp] + extra_shapes,
        scratch_shapes=[acc, acc, strip, strip] + extra_sems,
        compiler_params=_params(("arbitrary", "arbitrary")),
    )(proj3, proj3, proj3, do3, *extra)
    return res[:3], res[3:]


def _sigmoid(x):
    return 1.0 / (1.0 + jnp.exp(-x))


def _gate_out_norm(proj, ua, ub, w_out, x, g, *, tt=512):
    t_dim, d = ua.shape

    def body(ga_ref, gb_ref, ua_ref, ub_ref, w_ref, x_ref, g_ref, m_ref, h_ref, n_ref):
        mixed = (_sigmoid(ga_ref[...]) * ua_ref[...] + _sigmoid(gb_ref[...]) * ub_ref[...]).astype(BF16)
        m_ref[...] = mixed
        hv = x_ref[...] + jnp.dot(mixed, w_ref[...], preferred_element_type=F32)
        h_ref[...] = hv
        r = lax.rsqrt(jnp.mean(hv * hv, axis=-1, keepdims=True) + RMS_EPS)
        n_ref[...] = ((hv * r) * g_ref[...]).astype(BF16)

    row = pl.BlockSpec((tt, d), lambda i: (i, 0))
    return pl.pallas_call(
        body, name="gate_out_norm", grid=(t_dim // tt,),
        in_specs=[pl.BlockSpec((tt, d), lambda i: (i, 3)), pl.BlockSpec((tt, d), lambda i: (i, 4)), row, row,
                  pl.BlockSpec((d, d), lambda i: (0, 0)), row, pl.BlockSpec((1, d), lambda i: (0, 0))],
        out_specs=[row, row, row],
        out_shape=[jax.ShapeDtypeStruct((t_dim, d), BF16), jax.ShapeDtypeStruct((t_dim, d), F32),
                   jax.ShapeDtypeStruct((t_dim, d), BF16)],
        compiler_params=_params(("parallel",)),
    )(proj, proj, ua, ub, w_out, x, g)


def _out_dx_gate_bwd(dh, w_out, proj, ua, ub, *, tt=512):
    t_dim, d = ua.shape

    def body(dh_ref, w_ref, ga_ref, gb_ref, ua_ref, ub_ref, dua_ref, dub_ref, dg_ref):
        dm = lax.dot_general(dh_ref[...].astype(BF16), w_ref[...], NT, preferred_element_type=F32)
        sa = _sigmoid(ga_ref[...])
        sb = _sigmoid(gb_ref[...])
        dua_ref[...] = (dm * sa).astype(BF16)
        dub_ref[...] = (dm * sb).astype(BF16)
        dg_ref[:, :d] = (dm * ua_ref[...] * (sa * (1.0 - sa))).astype(BF16)
        dg_ref[:, d:] = (dm * ub_ref[...] * (sb * (1.0 - sb))).astype(BF16)

    row = pl.BlockSpec((tt, d), lambda i: (i, 0))
    wide = pl.BlockSpec((tt, 2 * d), lambda i: (i, 0))
    return pl.pallas_call(
        body, name="out_dx_gate_bwd", grid=(t_dim // tt,),
        in_specs=[row, pl.BlockSpec((d, d), lambda i: (0, 0)),
                  pl.BlockSpec((tt, d), lambda i: (i, 3)), pl.BlockSpec((tt, d), lambda i: (i, 4)), row, row],
        out_specs=[row, row, wide],
        out_shape=[jax.ShapeDtypeStruct((t_dim, d), BF16), jax.ShapeDtypeStruct((t_dim, d), BF16),
                   jax.ShapeDtypeStruct((t_dim, 2 * d), BF16)],
        compiler_params=_params(("parallel",)),
    )(dh, w_out, proj, proj, ua, ub)


def _ffn_up_swiglu(n, wg3, wu3, *, tt=1024):
    t_dim, d = n.shape
    n_s, _, f4 = wg3.shape

    def body(n_ref, wg_ref, wu_ref, g_ref, u_ref, a_ref):
        nv = n_ref[...]
        gv = jnp.dot(nv, wg_ref[...], preferred_element_type=F32)
        uv = jnp.dot(nv, wu_ref[...], preferred_element_type=F32)
        g_ref[...] = gv.astype(BF16)
        u_ref[...] = uv.astype(BF16)
        a_ref[...] = (gv * _sigmoid(gv) * uv).astype(BF16)

    wspec = pl.BlockSpec((None, d, f4), lambda i, s: (s, 0, 0))
    ospec = pl.BlockSpec((None, tt, f4), lambda i, s: (s, i, 0))
    shp = (n_s, t_dim, f4)
    return pl.pallas_call(
        body, name="ffn_up_swiglu", grid=(t_dim // tt, n_s),
        in_specs=[pl.BlockSpec((tt, d), lambda i, s: (i, 0)), wspec, wspec], out_specs=[ospec, ospec, ospec],
        out_shape=[jax.ShapeDtypeStruct(shp, BF16)] * 3,
        compiler_params=_params(("parallel", "parallel")),
    )(n, wg3, wu3)


def _ffn_down_dx_swiglu(dh, wd3, g3, u3, *, tt=1024):
    t_dim, d = dh.shape
    n_s, f4, _ = wd3.shape

    def body(dh_ref, w_ref, g_ref, u_ref, dg_ref, du_ref):
        da = lax.dot_general(dh_ref[...].astype(BF16), w_ref[...], NT, preferred_element_type=F32)
        gv = g_ref[...].astype(F32)
        sg = _sigmoid(gv)
        dg_ref[...] = (da * u_ref[...].astype(F32) * (sg + gv * sg * (1.0 - sg))).astype(BF16)
        du_ref[...] = (da * (gv * sg)).astype(BF16)

    spec = pl.BlockSpec((None, tt, f4), lambda i, s: (s, i, 0))
    shp = jax.ShapeDtypeStruct((n_s, t_dim, f4), BF16)
    return pl.pallas_call(
        body, name="ffn_down_dx_swiglu", grid=(t_dim // tt, n_s),
        in_specs=[pl.BlockSpec((tt, d), lambda i, s: (i, 0)), pl.BlockSpec((None, f4, d), lambda i, s: (s, 0, 0)),
                  spec, spec],
        out_specs=[spec, spec], out_shape=[shp, shp],
        compiler_params=_params(("parallel", "parallel")),
    )(dh, wd3, g3, u3)


def _mem_fwd(qm, kvm, *, tt=1024):
    b_dim, s_dim, _ = qm.shape
    n_mem = kvm.shape[1]
    scale = MEM_HEAD_DIM ** -0.5

    def body(q_ref, k_ref, v_ref, o_ref):
        sc = lax.dot_general(q_ref[0], k_ref[0], NT, preferred_element_type=F32) * scale
        p = jnp.exp(sc - jnp.max(sc, axis=-1, keepdims=True))
        p = p / jnp.sum(p, axis=-1, keepdims=True)
        o_ref[0] = jnp.dot(p.astype(BF16), v_ref[0], preferred_element_type=F32).astype(BF16)

    qs = pl.BlockSpec((1, tt, MEM_HEAD_DIM), lambda b, h, i: (b, i, h))
    return pl.pallas_call(
        body, name="mem_fwd", grid=(b_dim, N_HEADS_MEM, s_dim // tt),
        in_specs=[qs, pl.BlockSpec((1, n_mem, MEM_HEAD_DIM), lambda b, h, i: (b, 0, h)),
                  pl.BlockSpec((1, n_mem, MEM_HEAD_DIM), lambda b, h, i: (b, 0, N_HEADS_MEM + h))],
        out_specs=qs, out_shape=jax.ShapeDtypeStruct(qm.shape, BF16),
        compiler_params=_params(("parallel", "parallel", "parallel")),
    )(qm, kvm, kvm)


def _mem_bwd(qm, kvm, dom, *, tt=1024):
    b_dim, s_dim, _ = qm.shape
    n_mem = kvm.shape[1]
    scale = MEM_HEAD_DIM ** -0.5

    def body(q_ref, k_ref, v_ref, do_ref, dq_ref, dk_ref, dv_ref):
        qv, kv, vv, dov = q_ref[0], k_ref[0], v_ref[0], do_ref[0]
        sc = lax.dot_general(qv, kv, NT, preferred_element_type=F32) * scale
        p = jnp.exp(sc - jnp.max(sc, axis=-1, keepdims=True))
        p = p / jnp.sum(p, axis=-1, keepdims=True)
        dp = lax.dot_general(dov, vv, NT, preferred_element_type=F32)
        ds = (p * (dp - jnp.sum(p * dp, axis=-1, keepdims=True)) * scale).astype(BF16)
        dq_ref[0] = jnp.dot(ds, kv, preferred_element_type=F32).astype(BF16)

        @pl.when(pl.program_id(2) == 0)
        def _():
            dk_ref[...] = jnp.zeros_like(dk_ref)
            dv_ref[...] = jnp.zeros_like(dv_ref)

        dk_ref[0] += lax.dot_general(ds, qv, TN, preferred_element_type=F32)
        dv_ref[0] += lax.dot_general(p.astype(BF16), dov, TN, preferred_element_type=F32)

    qs = pl.BlockSpec((1, tt, MEM_HEAD_DIM), lambda b, h, i: (b, i, h))
    ks = pl.BlockSpec((1, n_mem, MEM_HEAD_DIM), lambda b, h, i: (b, 0, h))
    vs = pl.BlockSpec((1, n_mem, MEM_HEAD_DIM), lambda b, h, i: (b, 0, N_HEADS_MEM + h))
    return pl.pallas_call(
        body, name="mem_bwd", grid=(b_dim, N_HEADS_MEM, s_dim // tt),
        in_specs=[qs, ks, vs, qs], out_specs=[qs, ks, ks],
        out_shape=[jax.ShapeDtypeStruct(qm.shape, BF16), jax.ShapeDtypeStruct((b_dim, n_mem, MEM_WIDTH), F32),
                   jax.ShapeDtypeStruct((b_dim, n_mem, MEM_WIDTH), F32)],
        compiler_params=_params(("parallel", "parallel", "arbitrary")),
    )(qm, kvm, kvm, dom)


def _adamw_math(wv, gv, mv, vv):
    nm = ADAM_B1 * mv + (1.0 - ADAM_B1) * gv
    nv = ADAM_B2 * vv + (1.0 - ADAM_B2) * (gv * gv)
    m_hat = nm / (1.0 - ADAM_B1 ** ADAM_STEP)
    v_hat = nv / (1.0 - ADAM_B2 ** ADAM_STEP)
    return -ADAM_LR * (m_hat / (jnp.sqrt(v_hat) + ADAM_EPS) + ADAM_WD * wv), nm, nv


def _adamw(w, g, m, v, *, name):
    rows, cols = w.shape
    tr = _tile(rows, 256, 8)

    def body(w_ref, g_ref, m_ref, v_ref, d_ref, nm_ref, nv_ref):
        d_ref[...], nm_ref[...], nv_ref[...] = _adamw_math(w_ref[...], g_ref[...], m_ref[...], v_ref[...])

    spec = pl.BlockSpec((tr, cols), lambda i: (i, 0))
    shp = jax.ShapeDtypeStruct((rows, cols), F32)
    return pl.pallas_call(
        body, name=name, grid=(rows // tr,),
        in_specs=[spec] * 4, out_specs=[spec] * 3, out_shape=[shp] * 3,
        compiler_params=_params(("parallel",)),
    )(w, g, m, v)


def _prefetch_spec(grid, in_specs, out_specs):
    return pltpu.PrefetchScalarGridSpec(num_scalar_prefetch=1, grid=grid, in_specs=in_specs, out_specs=out_specs)


def _adamw_halves(w, mine, theirs, m, v, c_idx, *, name):
    rows, cols = w.shape
    half = rows // 2
    tr = _tile(half, _row_cap(cols), 8)
    nh = half // tr

    def body(c_ref, w_ref, mine_ref, theirs_ref, m_ref, v_ref, g_ref, d_ref, nm_ref, nv_ref):
        gv = jnp.where(pl.program_id(0) == c_ref[0], mine_ref[...], theirs_ref[...])
        g_ref[...] = gv
        d_ref[...], nm_ref[...], nv_ref[...] = _adamw_math(w_ref[...], gv, m_ref[...], v_ref[...])

    full = pl.BlockSpec((tr, cols), lambda h, i, c_ref: (h * nh + i, 0))
    part = pl.BlockSpec((tr, cols), lambda h, i, c_ref: (i, 0))
    shp = jax.ShapeDtypeStruct((rows, cols), F32)
    return pl.pallas_call(
        body, name=name, grid_spec=_prefetch_spec((2, nh), [full, part, part, full, full], [full] * 4),
        out_shape=[shp] * 4,
        compiler_params=_params(("parallel", "parallel")),
    )(c_idx, w, mine, theirs, m, v)


def _pair_sum(g3, theirs, c_idx, *, name):
    n, rows, cols = g3.shape
    half = rows // 2
    tr = _tile(half, _row_cap(cols), 16)

    def body(c_ref, g_ref, t_ref, o_ref):
        o_ref[...] = (g_ref[...] + t_ref[...]).astype(BF16)

    part = pl.BlockSpec((None, tr, cols), lambda s, i, c_ref: (s, i, 0))
    return pl.pallas_call(
        body, name=name,
        grid_spec=_prefetch_spec((n, half // tr),
                                 [pl.BlockSpec((None, None, tr, cols), lambda s, i, c_ref: (s, c_ref[0], i, 0)), part],
                                 part),
        out_shape=jax.ShapeDtypeStruct((n, half, cols), BF16),
        compiler_params=_params(("parallel", "parallel")),
    )(c_idx, g3.reshape(n, 2, half, cols), theirs)


def _chip_sum(pair, recv, s_idx, *, name):
    _, half, cols = pair.shape
    tr = _tile(half, _row_cap(cols), 16)

    def body(s_ref, p_ref, r_ref, o_ref):
        o_ref[...] = ((p_ref[...].astype(F32) + r_ref[0].astype(F32)) + r_ref[1].astype(F32)) + r_ref[2].astype(F32)

    return pl.pallas_call(
        body, name=name,
        grid_spec=_prefetch_spec((half // tr,),
                                 [pl.BlockSpec((None, tr, cols), lambda i, s_ref: (s_ref[0], i, 0)),
                                  pl.BlockSpec((N_CHIPS - 1, tr, cols), lambda i, s_ref: (0, i, 0))],
                                 pl.BlockSpec((tr, cols), lambda i, s_ref: (i, 0))),
        out_shape=jax.ShapeDtypeStruct((half, cols), F32),
        compiler_params=_params(("parallel",)),
    )(s_idx, pair, recv)


def _sum8(parts):
    n, rows, cols = parts.shape

    def body(p_ref, o_ref):
        acc = p_ref[0]
        for i in range(1, n):
            acc = acc + p_ref[i]
        o_ref[...] = acc

    return pl.pallas_call(
        body, name="small_sum", grid=(1,),
        in_specs=[pl.BlockSpec((n, rows, cols), lambda i: (0, 0, 0))],
        out_specs=pl.BlockSpec((rows, cols), lambda i: (0, 0)),
        out_shape=jax.ShapeDtypeStruct((rows, cols), parts.dtype),
        compiler_params=_params(("arbitrary",)),
    )(parts)


def _place():
    return lax.axis_index("x"), lax.axis_index("y"), lax.axis_index("c")


ANY = pl.BlockSpec(memory_space=pl.ANY)


def _rider_parts(rider):
    if rider is None:
        return (), [], []
    kind, arrays = rider
    n = len(arrays)
    shapes = {"gather": _gathered_shapes, "pair": _pair_shapes, "chip": _chip_shapes}[kind](arrays)
    sems = _gather_sems(n) if kind == "gather" else _exchange_sems(n if kind == "pair" else 3 * n)
    return tuple(arrays), shapes, sems


def _rider_hooks(rider, ins, outs, sems, step, n_steps):
    if rider is None:
        return (lambda: None), (lambda: None)
    if rider[0] == "gather":
        start, forward, finish = _gather_steps(ins, outs, *sems)
    else:
        start, finish = {"pair": _pair_steps, "chip": _chip_steps}[rider[0]](ins, outs, *sems)
        forward = None

    def begin():
        pl.when(step == 0)(start)

    def end():
        if forward is not None:
            pl.when(step == n_steps - 2)(forward)
        pl.when(step == n_steps - 1)(finish)

    return begin, end


def _gathered_shapes(shards):
    return [jax.ShapeDtypeStruct((N_CHIPS,) + s.shape, s.dtype) for s in shards]


def _gather_sems(n):
    return [pltpu.SemaphoreType.DMA((7 * n,)), pltpu.SemaphoreType.DMA((7 * n,))]


def _gather_steps(ins, outs, send_sems, recv_sems):
    n = len(ins)
    halves = [r.shape[0] // 2 for r in ins]
    x, y, c = _place()
    my_chip = 2 * x + y
    me, sibling = (x, y, c), (x, y, 1 - c)
    chips = [(1 - x, y), (x, 1 - y), (1 - x, 1 - y)]

    def half_of(w, chip, pc):
        return outs[w].at[chip, pl.ds(pc * halves[w], halves[w]), :]

    def copy(w, k, src, dst, to):
        return pltpu.make_async_remote_copy(
            src_ref=src, dst_ref=dst, send_sem=send_sems.at[7 * w + k], recv_sem=recv_sems.at[7 * w + k],
            device_id=to, device_id_type=MESH)

    def firsts():
        cps = []
        for w in range(n):
            cps.append(copy(w, 0, ins[w], outs[w].at[my_chip], sibling))
            mine = ins[w].at[pl.ds(c * halves[w], halves[w]), :]
            for j, (px, py) in enumerate(chips):
                cps.append(copy(w, 1 + j, mine, half_of(w, my_chip, c), (px, py, c)))
        return cps

    def passes():
        return [copy(w, 4 + j, half_of(w, 2 * px + py, c), half_of(w, 2 * px + py, c), sibling)
                for w in range(n) for j, (px, py) in enumerate(chips)]

    def start():
        for cp in firsts():
            cp.start()

    def forward():
        fws = passes()
        for w in range(n):
            for j, (px, py) in enumerate(chips):
                landed = half_of(w, 2 * px + py, c)
                copy(w, 1 + j, landed, landed, me).wait_recv()
                fws[3 * w + j].start()

    def finish():
        for w in range(n):
            copy(w, 0, ins[w], outs[w].at[my_chip], me).wait_recv()
            for j, (px, py) in enumerate(chips):
                landed = half_of(w, 2 * px + py, 1 - c)
                copy(w, 4 + j, landed, landed, me).wait_recv()
        for cp in firsts() + passes():
            cp.wait_send()

    return start, forward, finish


def _pair_shapes(grads):
    return [jax.ShapeDtypeStruct((g.shape[0], g.shape[1] // 2, g.shape[2]), g.dtype) for g in grads]


def _exchange_sems(n):
    return [pltpu.SemaphoreType.DMA((n,)), pltpu.SemaphoreType.DMA((n,))]


def _exchange_steps(copies):
    def start():
        for cp in copies():
            cp.start()

    def finish():
        for cp in copies():
            cp.wait()

    return start, finish


def _pair_steps(ins, outs, send_sems, recv_sems):
    x, y, c = _place()

    def copies():
        return [pltpu.make_async_remote_copy(
            src_ref=ins[w].at[:, pl.ds((1 - c) * (ins[w].shape[1] // 2), ins[w].shape[1] // 2), :], dst_ref=outs[w],
            send_sem=send_sems.at[w], recv_sem=recv_sems.at[w], device_id=(x, y, 1 - c), device_id_type=MESH)
            for w in range(len(ins))]

    return _exchange_steps(copies)


def _chip_shapes(pairs):
    return [jax.ShapeDtypeStruct((N_CHIPS - 1,) + p.shape[1:], p.dtype) for p in pairs]


def _chip_steps(ins, outs, send_sems, recv_sems):
    x, y, c = _place()
    others = [(1 - x, y), (x, 1 - y), (1 - x, 1 - y)]

    def copies():
        return [pltpu.make_async_remote_copy(
            src_ref=ins[w].at[2 * px + py], dst_ref=outs[w].at[j],
            send_sem=send_sems.at[3 * w + j], recv_sem=recv_sems.at[3 * w + j],
            device_id=(px, py, c), device_id_type=MESH)
            for w in range(len(ins)) for j, (px, py) in enumerate(others)]

    return _exchange_steps(copies)


def _swap_halves(mine):
    n = len(mine)

    def body(*refs):
        ins, outs, send_sems, recv_sems = refs[:n], refs[n:2 * n], refs[2 * n], refs[2 * n + 1]
        x, y, c = _place()
        copies = [pltpu.make_async_remote_copy(
            src_ref=ins[w], dst_ref=outs[w], send_sem=send_sems.at[w], recv_sem=recv_sems.at[w],
            device_id=(x, y, 1 - c), device_id_type=MESH) for w in range(n)]
        for cp in copies:
            cp.start()
        for cp in copies:
            cp.wait()

    return pl.pallas_call(
        body, name="grad_swap_halves",
        out_shape=[jax.ShapeDtypeStruct(h.shape, h.dtype) for h in mine],
        in_specs=[ANY] * n, out_specs=[ANY] * n,
        scratch_shapes=[pltpu.SemaphoreType.DMA((n,)), pltpu.SemaphoreType.DMA((n,))],
    )(*mine)


def _gather_small(small):
    srows, cols = small.shape

    def body(s_ref, all_ref, send_sems, recv_sems, local_sem):
        x, y, c = _place()
        me = 4 * x + 2 * y + c
        keep_small = pltpu.make_async_copy(s_ref, all_ref.at[me], local_sem)
        keep_small.start()
        sends = []
        for kk in range(1, 8):
            peer = (x ^ (kk >> 2), y ^ ((kk >> 1) & 1), c ^ (kk & 1))
            sends.append(pltpu.make_async_remote_copy(
                src_ref=s_ref, dst_ref=all_ref.at[me],
                send_sem=send_sems.at[kk], recv_sem=recv_sems.at[kk], device_id=peer, device_id_type=MESH))
        for cp in sends:
            cp.start()
        for kk in range(1, 8):
            px, py, pc = x ^ (kk >> 2), y ^ ((kk >> 1) & 1), c ^ (kk & 1)
            pltpu.make_async_remote_copy(
                src_ref=s_ref, dst_ref=all_ref.at[4 * px + 2 * py + pc],
                send_sem=send_sems.at[kk], recv_sem=recv_sems.at[kk], device_id=(px, py, pc),
                device_id_type=MESH).wait_recv()
        for cp in sends:
            cp.wait_send()
        keep_small.wait()

    return pl.pallas_call(
        body, name="gather_small",
        out_shape=jax.ShapeDtypeStruct((8, srows, cols), small.dtype),
        in_specs=[ANY], out_specs=ANY,
        scratch_shapes=[pltpu.SemaphoreType.DMA((8,)), pltpu.SemaphoreType.DMA((8,)), pltpu.SemaphoreType.DMA],
    )(small)


SHARDED = (("w_in", D_MODEL, IN_COLS, 1), ("w_up_a", ATT_WIDTH, D_MODEL, 1), ("w_up_b", ATT_WIDTH, D_MODEL, 1),
           ("w_out", D_MODEL, D_MODEL, 0), ("w_q_mem", D_MODEL, MEM_WIDTH, 0), ("w_kv_mem", D_MODEL, 2 * MEM_WIDTH, 0),
           ("w_o_mem", MEM_WIDTH, D_MODEL, 1), ("w_ffn_gate", D_MODEL, D_FF, 1), ("w_ffn_up", D_MODEL, D_FF, 1),
           ("w_ffn_down", D_FF, D_MODEL, 0))
NAMES = tuple(n for n, _, _, _ in SHARDED)
EARLY, LATE = NAMES[:1], NAMES[1:]
GAINS = ("g_mix", "g_mem_q", "g_mem_kv", "g_ffn", "g_final")


def _natural(w3):
    n, r, c = w3.shape
    return w3.reshape(n * r, c)


def _shard_major(g, axis):
    if axis == 1:
        return g
    r, c = g.shape
    return g.reshape(N_CHIPS, r // N_CHIPS, c)


def kernel(x, mem, positions, g_mix, w_in, w_up_a, w_up_b, w_out, g_mem_q, g_mem_kv, w_q_mem, w_kv_mem, w_o_mem, g_ffn, w_ffn_gate, w_ffn_up, w_ffn_down, g_final, loss_target, m_g_mix, m_w_in, m_w_up_a, m_w_up_b, m_w_out, m_g_mem_q, m_g_mem_kv, m_w_q_mem, m_w_kv_mem, m_w_o_mem, m_g_ffn, m_w_ffn_gate, m_w_ffn_up, m_w_ffn_down, m_g_final, v_g_mix, v_w_in, v_w_up_a, v_w_up_b, v_w_out, v_g_mem_q, v_g_mem_kv, v_w_q_mem, v_w_kv_mem, v_w_o_mem, v_g_ffn, v_w_ffn_gate, v_w_ffn_up, v_w_ffn_down, v_g_final):
    given = dict(locals())
    shards = {n: given[n][0] for n, _, _, _ in SHARDED}

    early_shards = [shards[n].astype(BF16) for n in EARLY]
    late_shards = [shards[n].astype(BF16) for n in LATE]
    c_idx = lax.axis_index("c").astype(jnp.int32).reshape(1)
    s_idx = (2 * lax.axis_index("x") + lax.axis_index("y")).astype(jnp.int32).reshape(1)

    loss_row, grad_x, mine, gain_grads = _local_step(x, mem, positions, loss_target, g_mix, g_mem_q, g_mem_kv,
                                                     g_ffn, g_final, {}, early_shards, late_shards, (c_idx, s_idx))
    return _reduce_and_update(given, shards, loss_row, grad_x, mine, gain_grads, c_idx)


def _reduce_halves(glist, names, c_idx, s_idx, pair_exchange, chip_exchange):
    theirs = pair_exchange(glist)
    pairs = [_pair_sum(g, t, c_idx, name="pair_sum_" + n) for n, g, t in zip(names, glist, theirs)]
    recv = chip_exchange(pairs)
    return [_chip_sum(p, r, s_idx, name="chip_sum_" + n) for n, p, r in zip(names, pairs, recv)]


def _local_step(x, mem, positions, loss_target, g_mix, g_mem_q, g_mem_kv, g_ffn, g_final, wf,
                early_shards=None, late_shards=None, place=None):
    b_dim, s_dim, d = x.shape
    t_dim = b_dim * s_dim
    n_mem = mem.shape[1]
    wf = dict(wf)

    xb = x.reshape(t_dim, d)
    tgt = loss_target.reshape(t_dim, d)
    memf = mem.reshape(b_dim * n_mem, d)
    gfin = g_final.reshape(1, d)
    pos = positions.reshape(t_dim, 1).astype(F32)

    lane = jnp.arange(LANES) % HEAD_DIM
    half = ROPE_DIM // 2
    inv_freq = ROPE_THETA ** (-jnp.arange(half, dtype=F32) / half)
    inv_lane = jnp.where(lane < ROPE_DIM, inv_freq[lane % half], 0.0).reshape(1, -1).astype(F32)
    sel_a = (lane < half).astype(F32).reshape(1, -1)
    sel_b = ((lane >= half) & (lane < ROPE_DIM)).astype(F32).reshape(1, -1)

    def rows3(t):
        return t.reshape(b_dim, s_dim, t.shape[-1])

    def rows2(t):
        return t.reshape(t_dim, t.shape[-1])

    if early_shards:
        n1, gathered = _rms_fwd(xb, g_mix, name="rms_mix", rider=("gather", early_shards))
        wf.update(zip(EARLY, gathered))
    else:
        n1 = _rms_fwd(xb, g_mix, name="rms_mix")
    proj = _mm_cs(n1, wf["w_in"], name="mm_in")
    proj3 = rows3(proj)
    cs, sn = _rope_table(pos, inv_lane, sel_a, sel_b)
    cs3, sn3 = rows3(cs), rows3(sn)
    (oa16, oa32, lse_a), _ = _dil_fwd(proj3, cs3, sn3, sel_a, sel_b)
    ob16, gathered = _sb_fwd(proj3, ("gather", late_shards) if late_shards else None)
    wf.update(zip(LATE, gathered))
    w_out, w_q, w_kv = _natural(wf["w_out"]), _natural(wf["w_q_mem"]), _natural(wf["w_kv_mem"])
    oa, ob = rows2(oa16), rows2(ob16)
    ua = _mm_cs(oa, wf["w_up_a"], name="mm_up_a")
    ub = _mm_cs(ob, wf["w_up_b"], name="mm_up_b")
    mixed, h1, hn = _gate_out_norm(proj, ua, ub, w_out, xb, g_mem_q)

    memn = _rms_fwd(memf, g_mem_kv, name="rms_mem_kv")
    qm = _mm(hn, w_q, name="mm_q_mem", out_dtype=BF16)
    kvm = _mm(memn, w_kv, name="mm_kv_mem", out_dtype=BF16)
    qm3, kvm3 = rows3(qm), kvm.reshape(b_dim, n_mem, 2 * MEM_WIDTH)
    om = rows2(_mem_fwd(qm3, kvm3))
    h2 = _mm_cs(om, wf["w_o_mem"], name="mm_o_mem", add=h1)

    n3 = _rms_fwd(h2, g_ffn, name="rms_ffn")
    gate3, up3, act3 = _ffn_up_swiglu(n3, wf["w_ffn_gate"], wf["w_ffn_up"])
    loss_row, dh3, dg_final = _down_final(act3, wf["w_ffn_down"], h2, gfin, tgt)

    grads = {}
    grads["w_ffn_down"] = _mm_ffn_down_dw(act3, dh3, name="mm_down_dw")
    dgate3, dup3 = _ffn_down_dx_swiglu(dh3, wf["w_ffn_down"], gate3, up3)
    grads["w_ffn_gate"] = _mm_ffn_up_dw(n3, dgate3, name="mm_gate_dw")
    grads["w_ffn_up"] = _mm_ffn_up_dw(n3, dup3, name="mm_up_dw")
    dn3 = _mm_ffn_up_dx(dgate3, wf["w_ffn_gate"], name="mm_gate_dx")
    dn3 = _mm_ffn_up_dx(dup3, wf["w_ffn_up"], name="mm_up_dx", add=dn3)
    dh2, dg_ffn = _rms_bwd(h2, g_ffn, dn3, dh3, name="rms_ffn_bwd")

    dom = _mm_cs_dx(dh2, wf["w_o_mem"], name="mm_o_mem_dx", out_dtype=BF16)
    grads["w_o_mem"] = _mm_cs_dw(om, dh2, name="mm_o_mem_dw")
    dqm, dkm, dvm = _mem_bwd(qm3, kvm3, rows3(dom))
    dqm = rows2(dqm)
    dkvm = jnp.concatenate([dkm, dvm], axis=-1).reshape(b_dim * n_mem, 2 * MEM_WIDTH).astype(BF16)
    grads["w_q_mem"] = _shard_major(_mm(hn, dqm, name="mm_q_mem_dw", ta=True), 0)
    dhn = _mm(dqm, w_q, name="mm_q_mem_dx", tb=True)
    grads["w_kv_mem"] = _shard_major(_mm(memn, dkvm, name="mm_kv_mem_dw", ta=True), 0)
    dmemn = _mm(dkvm, w_kv, name="mm_kv_mem_dx", tb=True)
    _, dg_mem_kv = _rms_bwd(memf, g_mem_kv, dmemn, None, name="rms_mem_kv_bwd")
    dh1, dg_mem_q = _rms_bwd(h1, g_mem_q, dhn, dh2, name="rms_mem_q_bwd")

    grads["w_out"] = _shard_major(_mm(mixed, dh1, name="mm_out_dw", ta=True), 0)
    dua, dub, dgates = _out_dx_gate_bwd(dh1, w_out, proj, ua, ub)
    doa = _mm_cs_dx(dua, wf["w_up_a"], name="mm_up_a_dx")
    grads["w_up_a"] = _mm_cs_dw(oa, dua, name="mm_up_a_dw")
    dob = _mm_cs_dx(dub, wf["w_up_b"], name="mm_up_b_dx", out_dtype=BF16)
    grads["w_up_b"] = _mm_cs_dw(ob, dub, name="mm_up_b_dw")

    att = {}

    def dil_with_pairs(glist):
        att["a"], theirs = _dil_bwd(proj3, cs3, sn3, sel_a, sel_b, rows3(doa), oa32, lse_a,
                                    ("pair", glist) if glist else None)
        return theirs

    def sb_with_chips(pairs):
        att["b"], recv = _sb_bwd(proj3, rows3(dob), ("chip", pairs) if pairs else None)
        return recv

    if place is None:
        dil_with_pairs(())
        sb_with_chips(())
    else:
        mine_late = _reduce_halves([grads[n] for n in LATE], LATE, *place, dil_with_pairs, sb_with_chips)
    dproj = jnp.concatenate([rows2(t) for t in att["a"] + att["b"]] + [dgates], axis=1)
    grads["w_in"] = _mm_cs_dw(n1, dproj, name="mm_in_dw")
    if place is None:
        dn1 = _mm_cs_dx(dproj, wf["w_in"], name="mm_in_dx")
        dx, dg_mix = _rms_bwd(xb, g_mix, dn1, dh1, name="rms_mix_bwd")
    else:
        tail = {}

        def dx_with_pairs(glist):
            tail["dn1"], theirs = _mm_cs_dx(dproj, wf["w_in"], name="mm_in_dx", rider=("pair", glist))
            return theirs

        def rms_with_chips(pairs):
            tail["dx"], tail["dg"], recv = _rms_bwd(xb, g_mix, tail["dn1"], dh1, name="rms_mix_bwd",
                                                    rider=("chip", pairs))
            return recv

        mine_early = _reduce_halves([grads[n] for n in EARLY], EARLY, *place, dx_with_pairs, rms_with_chips)
        dx, dg_mix = tail["dx"], tail["dg"]
    grad_x = dx.reshape(b_dim, s_dim, d)
    gains = (dg_mix, dg_mem_q, dg_mem_kv, dg_ffn, dg_final)
    if place is None:
        return loss_row, grad_x, grads, gains
    return loss_row, grad_x, mine_early + mine_late, gains


def _reduce_and_update(given, shards, loss_row, grad_x, mine, gain_grads, c_idx):
    d = D_MODEL
    dg_mix, dg_mem_q, dg_mem_kv, dg_ffn, dg_final = gain_grads
    small = jnp.concatenate([dg_mix, dg_mem_q, dg_mem_kv, dg_ffn, dg_final,
                             jnp.pad(loss_row, ((0, 0), (0, FLAT_COLS - LANES))), jnp.zeros((2, FLAT_COLS), F32)], axis=0)
    small_all = _gather_small(small)
    others = _swap_halves(mine)
    small_sum = _sum8(small_all)
    loss = small_sum[5, 0]

    out_g, out_d, out_m, out_v = {}, {}, {}, {}
    for n, mine_n, other_n in zip(NAMES, mine, others):
        g2, dl, nm, nv = _adamw_halves(shards[n], mine_n, other_n, given["m_" + n][0], given["v_" + n][0], c_idx,
                                       name="adamw_" + n)
        out_g[n], out_d[n], out_m[n], out_v[n] = g2[None], dl[None], nm[None], nv[None]
    gain_w = jnp.concatenate([given[n].reshape(1, d) for n in GAINS], axis=0)
    gain_m = jnp.concatenate([given["m_" + n].reshape(1, d) for n in GAINS], axis=0)
    gain_v = jnp.concatenate([given["v_" + n].reshape(1, d) for n in GAINS], axis=0)
    gain_g = small_sum[:len(GAINS)]
    gd, gm, gv = _adamw(gain_w, gain_g, gain_m, gain_v, name="adamw_gains")
    for i, n in enumerate(GAINS):
        shape = given[n].shape
        out_g[n], out_d[n] = gain_g[i].reshape(shape), gd[i].reshape(shape)
        out_m[n], out_v[n] = gm[i].reshape(shape), gv[i].reshape(shape)

    order = ["g_mix", "w_in", "w_up_a", "w_up_b", "w_out", "g_mem_q", "g_mem_kv", "w_q_mem", "w_kv_mem", "w_o_mem",
             "g_ffn", "w_ffn_gate", "w_ffn_up", "w_ffn_down", "g_final"]
    return (loss, grad_x, *[out_g[n] for n in order], *[out_d[n] for n in order],
            *[out_m[n] for n in order], *[out_v[n] for n in order])
```

```python
import jax
import jax.numpy as jnp
from jax import lax
from jax.experimental import pallas as pl
from jax.experimental.pallas import tpu as pltpu

F32 = jnp.float32
BF16 = jnp.bfloat16
MESH = pl.DeviceIdType.MESH

D_MODEL = 1024
HEAD_DIM = 64
N_HEADS = 8
ATT_WIDTH = N_HEADS * HEAD_DIM
DIL_PATTERNS = ((128, 1), (512, 4), (2048, 16))
BLOCK = 128
SB_ROWS = 1024
SB_STEP = 4
ROPE_THETA = 500000.0
ROPE_DIM = HEAD_DIM // 4
N_HEADS_MEM = 4
MEM_HEAD_DIM = 128
MEM_WIDTH = N_HEADS_MEM * MEM_HEAD_DIM
D_FF = 2816
IN_COLS = 6 * ATT_WIDTH + 2 * D_MODEL
RMS_EPS = 1e-6
ADAM_LR = 0.001
ADAM_B1 = 0.9
ADAM_B2 = 0.999
ADAM_EPS = 1e-08
ADAM_WD = 0.01
ADAM_STEP = 10

N_CHIPS = 4
LANES = 128
FLAT_COLS = 1024
VMEM_LIMIT = 56 * 1024 * 1024

PAIRS = ATT_WIDTH // LANES
COL_QA, COL_KA, COL_VA, COL_QB, COL_KB, COL_VB = (i * PAIRS for i in range(6))

MM_CAP = 1408
TOK_CAP = 2048
NN = (((1,), (0,)), ((), ()))
NT = (((1,), (1,)), ((), ()))
TN = (((0,), (0,)), ((), ()))
BNN = (((2,), (1,)), ((0,), (0,)))
BNT = (((2,), (2,)), ((0,), (0,)))
BTN = (((1,), (1,)), ((0,), (0,)))
DIL_BATCH = 8
DIL_CHUNK = 512


def _tile(dim, cap, unit=LANES):
    if dim <= cap:
        return dim
    best = None
    for t in range(unit, cap + 1, unit):
        if dim % t == 0:
            best = t
    assert best is not None, (dim, cap)
    return best


def _row_cap(cols):
    return max(256, (1 << 18) // cols)


def _params(sem):
    return pltpu.CompilerParams(dimension_semantics=sem, vmem_limit_bytes=VMEM_LIMIT)


def _mm(a, b, *, name, ta=False, tb=False, add=None, out_dtype=F32,
        tm_cap=MM_CAP, tn_cap=MM_CAP, tk_cap=MM_CAP):
    if ta:
        k_dim, m_dim = a.shape
    else:
        m_dim, k_dim = a.shape
    if tb:
        n_dim, kb = b.shape
    else:
        kb, n_dim = b.shape
    assert kb == k_dim, (a.shape, b.shape, ta, tb)
    tm, tn, tk = _tile(m_dim, tm_cap), _tile(n_dim, tn_cap), _tile(k_dim, tk_cap)
    nk = k_dim // tk
    dims = (((0 if ta else 1,), (1 if tb else 0,)), ((), ()))
    has_add = add is not None

    def body(*refs):
        if has_add:
            a_ref, b_ref, add_ref, o_ref = refs[:4]
        else:
            a_ref, b_ref, o_ref = refs[:3]
        part = lax.dot_general(a_ref[...].astype(BF16), b_ref[...].astype(BF16), dims, preferred_element_type=F32)

        def finish(r):
            if has_add:
                r = add_ref[...] + r
            o_ref[...] = r.astype(out_dtype)

        if nk == 1:
            finish(part)
            return
        acc_ref = refs[-1]
        k = pl.program_id(2)

        @pl.when(k == 0)
        def _():
            acc_ref[...] = part

        @pl.when(k > 0)
        def _():
            acc_ref[...] += part

        @pl.when(k == nk - 1)
        def _():
            finish(acc_ref[...])

    a_spec = pl.BlockSpec((tk, tm), lambda i, j, k: (k, i)) if ta else pl.BlockSpec((tm, tk), lambda i, j, k: (i, k))
    b_spec = pl.BlockSpec((tn, tk), lambda i, j, k: (j, k)) if tb else pl.BlockSpec((tk, tn), lambda i, j, k: (k, j))
    o_spec = pl.BlockSpec((tm, tn), lambda i, j, k: (i, j))
    in_specs = [a_spec, b_spec] + ([o_spec] if has_add else [])
    args = (a, b) + ((add,) if has_add else ())
    return pl.pallas_call(
        body, name=name, grid=(m_dim // tm, n_dim // tn, nk),
        in_specs=in_specs, out_specs=o_spec,
        out_shape=jax.ShapeDtypeStruct((m_dim, n_dim), out_dtype),
        scratch_shapes=[pltpu.VMEM((tm, tn), F32)] if nk > 1 else [],
        compiler_params=_params(("parallel", "parallel", "arbitrary")),
    )(*args)


def _mm_core(name, a, b, a_spec, b_spec, o_spec, out_shape, grid, dims, *, add=None, out_dtype=F32, rider=None):
    nk = grid[2]
    has_add = add is not None
    n_in = 3 if has_add else 2
    acc_shape = tuple(d for d in o_spec.block_shape if d is not None)
    extra, extra_shapes, extra_sems = _rider_parts(rider)
    n_w = len(extra)

    def body(*refs):
        a_ref, b_ref = refs[:2]
        o_ref = refs[n_in + n_w]
        step = (pl.program_id(0) * grid[1] + pl.program_id(1)) * nk + pl.program_id(2)
        begin, end = _rider_hooks(rider, refs[n_in:n_in + n_w], refs[n_in + n_w + 1:n_in + 2 * n_w + 1], refs[-2:],
                                  step, grid[0] * grid[1] * nk)
        begin()
        part = lax.dot_general(a_ref[...].astype(BF16), b_ref[...].astype(BF16), dims, preferred_element_type=F32)

        def finish(r):
            if has_add:
                r = refs[2][...] + r
            o_ref[...] = r.astype(out_dtype)

        if nk == 1:
            finish(part)
        else:
            acc_ref = refs[n_in + 2 * n_w + 1]
            k = pl.program_id(2)

            @pl.when(k == 0)
            def _():
                acc_ref[...] = part

            @pl.when(k > 0)
            def _():
                acc_ref[...] += part

            @pl.when(k == nk - 1)
            def _():
                finish(acc_ref[...])
        end()

    in_specs = [a_spec, b_spec] + ([o_spec] if has_add else []) + [ANY] * n_w
    args = (a, b) + ((add,) if has_add else ()) + extra
    res = pl.pallas_call(
        body, name=name, grid=grid, in_specs=in_specs, out_specs=[o_spec] + [ANY] * n_w,
        out_shape=[jax.ShapeDtypeStruct(out_shape, out_dtype)] + extra_shapes,
        scratch_shapes=([pltpu.VMEM(acc_shape, F32)] if nk > 1 else []) + extra_sems,
        compiler_params=_params(("arbitrary",) * 3 if n_w else ("parallel", "parallel", "arbitrary")),
    )(*args)
    return (res[0], res[1:]) if n_w else res[0]


def _mm_cs(a, w3, *, name, add=None, out_dtype=F32):
    m_dim, k_dim = a.shape
    _, _, n4 = w3.shape
    tm, tn, tk = _tile(m_dim, MM_CAP if add is not None else TOK_CAP), _tile(n4, MM_CAP), _tile(k_dim, MM_CAP)
    npb = n4 // tn
    return _mm_core(name, a, w3,
                    pl.BlockSpec((tm, tk), lambda i, j, k: (i, k)),
                    pl.BlockSpec((None, tk, tn), lambda i, j, k: (j // npb, k, j % npb)),
                    pl.BlockSpec((tm, tn), lambda i, j, k: (i, j)),
                    (m_dim, N_CHIPS * n4), (m_dim // tm, N_CHIPS * npb, k_dim // tk), NN, add=add, out_dtype=out_dtype)


def _mm_cs_dx(dy, w3, *, name, out_dtype=F32, rider=None):
    m_dim, _ = dy.shape
    _, k_dim, n4 = w3.shape
    tm, tkw, tn = _tile(m_dim, MM_CAP), _tile(k_dim, MM_CAP), _tile(n4, MM_CAP)
    npb = n4 // tn
    return _mm_core(name, dy, w3,
                    pl.BlockSpec((tm, tn), lambda i, j, k: (i, k)),
                    pl.BlockSpec((None, tkw, tn), lambda i, j, k: (k // npb, j, k % npb)),
                    pl.BlockSpec((tm, tkw), lambda i, j, k: (i, j)),
                    (m_dim, k_dim), (m_dim // tm, k_dim // tkw, N_CHIPS * npb), NT, out_dtype=out_dtype, rider=rider)


def _mm_cs_dw(a, dy, *, name):
    m_dim, k_dim = a.shape
    n4 = dy.shape[1] // N_CHIPS
    tmk, tn, tk = _tile(k_dim, MM_CAP), _tile(n4, MM_CAP), _tile(m_dim, TOK_CAP)
    npb = n4 // tn
    return _mm_core(name, a, dy,
                    pl.BlockSpec((tk, tmk), lambda i, j, k: (k, i)),
                    pl.BlockSpec((tk, tn), lambda i, j, k: (k, j)),
                    pl.BlockSpec((None, tmk, tn), lambda i, j, k: (j // npb, i, j % npb)),
                    (N_CHIPS, k_dim, n4), (k_dim // tmk, N_CHIPS * npb, m_dim // tk), TN)


def _mm_ffn_up_dw(n, d3, *, name):
    t_dim, d = n.shape
    _, _, f4 = d3.shape
    tk = _tile(t_dim, TOK_CAP)
    return _mm_core(name, n, d3,
                    pl.BlockSpec((tk, d), lambda i, j, k: (k, 0)),
                    pl.BlockSpec((None, tk, f4), lambda i, j, k: (j, k, 0)),
                    pl.BlockSpec((None, d, f4), lambda i, j, k: (j, 0, 0)),
                    (N_CHIPS, d, f4), (1, N_CHIPS, t_dim // tk), TN)


def _mm_ffn_up_dx(d3, w3, *, name, add=None):
    _, t_dim, f4 = d3.shape
    _, d, _ = w3.shape
    tm = _tile(t_dim, MM_CAP)
    return _mm_core(name, d3, w3,
                    pl.BlockSpec((None, tm, f4), lambda i, j, k: (k, i, 0)),
                    pl.BlockSpec((None, d, f4), lambda i, j, k: (k, 0, 0)),
                    pl.BlockSpec((tm, d), lambda i, j, k: (i, 0)),
                    (t_dim, d), (t_dim // tm, 1, N_CHIPS), NT, add=add)


def _mm_ffn_down_dw(act3, dh, *, name):
    _, t_dim, f4 = act3.shape
    d = dh.shape[1]
    tk = _tile(t_dim, TOK_CAP)
    return _mm_core(name, act3, dh,
                    pl.BlockSpec((None, tk, f4), lambda i, j, k: (i, k, 0)),
                    pl.BlockSpec((tk, d), lambda i, j, k: (k, 0)),
                    pl.BlockSpec((None, f4, d), lambda i, j, k: (i, 0, 0)),
                    (N_CHIPS, f4, d), (N_CHIPS, 1, t_dim // tk), TN)


def _rms_fwd(x, g, *, name, tt=512, rider=None):
    t_dim, d = x.shape
    tt = _tile(t_dim, tt, 8)
    extra, extra_shapes, extra_sems = _rider_parts(rider)
    n_w = len(extra)

    def body(*refs):
        x_ref, g_ref, o_ref = refs[0], refs[1], refs[2 + n_w]
        begin, end = _rider_hooks(rider, refs[2:2 + n_w], refs[3 + n_w:3 + 2 * n_w], refs[-2:], pl.program_id(0),
                                  t_dim // tt)
        begin()
        xv = x_ref[...]
        r = lax.rsqrt(jnp.mean(xv * xv, axis=-1, keepdims=True) + RMS_EPS)
        o_ref[...] = ((xv * r) * g_ref[...]).astype(o_ref.dtype)
        end()

    res = pl.pallas_call(
        body, name=name, grid=(t_dim // tt,),
        in_specs=[pl.BlockSpec((tt, d), lambda i: (i, 0)), pl.BlockSpec((1, d), lambda i: (0, 0))] + [ANY] * n_w,
        out_specs=[pl.BlockSpec((tt, d), lambda i: (i, 0))] + [ANY] * n_w,
        out_shape=[jax.ShapeDtypeStruct((t_dim, d), BF16)] + extra_shapes,
        scratch_shapes=extra_sems,
        compiler_params=_params(("arbitrary",) if n_w else ("parallel",)),
    )(x, g, *extra)
    return (res[0], res[1:]) if n_w else res[0]


def _rms_bwd(x, g, dy, add, *, name, tt=1024, rider=None):
    t_dim, d = x.shape
    tt = _tile(t_dim, tt, 8)
    has_add = add is not None
    n_in = 4 if has_add else 3
    extra, extra_shapes, extra_sems = _rider_parts(rider)
    n_w = len(extra)

    def body(*refs):
        x_ref, g_ref, dy_ref = refs[:3]
        add_ref = refs[3] if has_add else None
        dx_ref, dg_ref = refs[n_in + n_w:n_in + n_w + 2]
        begin, end = _rider_hooks(rider, refs[n_in:n_in + n_w], refs[n_in + n_w + 2:n_in + 2 * n_w + 2], refs[-2:],
                                  pl.program_id(0), t_dim // tt)
        begin()
        xv = x_ref[...]
        dyv = dy_ref[...].astype(F32)
        r = lax.rsqrt(jnp.mean(xv * xv, axis=-1, keepdims=True) + RMS_EPS)
        xh = xv * r
        u = dyv * g_ref[...]
        dx = r * (u - xh * jnp.mean(u * xh, axis=-1, keepdims=True))
        if has_add:
            dx = add_ref[...] + dx
        dx_ref[...] = dx

        @pl.when(pl.program_id(0) == 0)
        def _():
            dg_ref[...] = jnp.zeros_like(dg_ref)

        dg_ref[...] += jnp.sum(dyv * xh, axis=0, keepdims=True)
        end()

    row = pl.BlockSpec((tt, d), lambda i: (i, 0))
    vec = pl.BlockSpec((1, d), lambda i: (0, 0))
    in_specs = [row, vec, row] + ([row] if has_add else []) + [ANY] * n_w
    args = (x, g, dy) + ((add,) if has_add else ()) + extra
    res = pl.pallas_call(
        body, name=name, grid=(t_dim // tt,),
        in_specs=in_specs, out_specs=[row, vec] + [ANY] * n_w,
        out_shape=[jax.ShapeDtypeStruct((t_dim, d), F32), jax.ShapeDtypeStruct((1, d), F32)] + extra_shapes,
        scratch_shapes=extra_sems,
        compiler_params=_params(("arbitrary",)),
    )(*args)
    return (res[0], res[1], res[2:]) if n_w else (res[0], res[1])


def _down_final(act3, wd3, h, g, target, *, tt=1024):
    n_s, t_dim, f4 = act3.shape
    d = h.shape[1]
    n_steps = t_dim // tt

    def body(a_ref, w_ref, h_ref, g_ref, t_ref, loss_ref, dh_ref, dg_ref, acc_ref, sq_ref):
        i, k = pl.program_id(0), pl.program_id(1)
        part = jnp.dot(a_ref[...], w_ref[...], preferred_element_type=F32)

        @pl.when(k == 0)
        def _():
            acc_ref[...] = part

        @pl.when(k > 0)
        def _():
            acc_ref[...] += part

        @pl.when(jnp.logical_and(i == 0, k == 0))
        def _():
            dg_ref[...] = jnp.zeros_like(dg_ref)
            sq_ref[...] = jnp.zeros_like(sq_ref)

        @pl.when(k == n_s - 1)
        def _():
            xv = h_ref[...] + acc_ref[...]
            gv = g_ref[...]
            r = lax.rsqrt(jnp.mean(xv * xv, axis=-1, keepdims=True) + RMS_EPS)
            xh = xv * r
            err = xh * gv - t_ref[...]
            dyv = err * (1.0 / d)
            u = dyv * gv
            dh_ref[...] = r * (u - xh * jnp.mean(u * xh, axis=-1, keepdims=True))
            dg_ref[...] += jnp.sum(dyv * xh, axis=0, keepdims=True)
            sq_ref[...] += jnp.sum(err * err, axis=0, keepdims=True)

        @pl.when(jnp.logical_and(i == n_steps - 1, k == n_s - 1))
        def _():
            total = jnp.sum(sq_ref[...], axis=-1, keepdims=True) * (0.5 / d)
            loss_ref[...] = jnp.broadcast_to(total, loss_ref.shape)

    row = pl.BlockSpec((tt, d), lambda i, k: (i, 0))
    vec = pl.BlockSpec((1, d), lambda i, k: (0, 0))
    return pl.pallas_call(
        body, name="down_final_loss", grid=(n_steps, n_s),
        in_specs=[pl.BlockSpec((None, tt, f4), lambda i, k: (k, i, 0)),
                  pl.BlockSpec((None, f4, d), lambda i, k: (k, 0, 0)), row, vec, row],
        out_specs=[pl.BlockSpec((1, LANES), lambda i, k: (0, 0)), row, vec],
        out_shape=[jax.ShapeDtypeStruct((1, LANES), F32), jax.ShapeDtypeStruct((t_dim, d), F32),
                   jax.ShapeDtypeStruct((1, d), F32)],
        scratch_shapes=[pltpu.VMEM((tt, d), F32), pltpu.VMEM((1, d), F32)],
        compiler_params=_params(("arbitrary", "arbitrary")),
    )(act3, wd3, h, g, target)


def _rope_table(pos, inv_lane, sel_a, sel_b, *, tt=512):
    t_dim = pos.shape[0]

    def body(p_ref, f_ref, a_ref, b_ref, c_ref, s_ref):
        ang = p_ref[...] * f_ref[...]
        on = (a_ref[...] + b_ref[...]) > 0.0
        c_ref[...] = jnp.where(on, jnp.cos(ang), 1.0)
        s_ref[...] = jnp.where(on, jnp.sin(ang), 0.0)

    vec = pl.BlockSpec((1, LANES), lambda i: (0, 0))
    row = pl.BlockSpec((tt, LANES), lambda i: (i, 0))
    shp = jax.ShapeDtypeStruct((t_dim, LANES), F32)
    return pl.pallas_call(
        body, name="rope_table", grid=(t_dim // tt,),
        in_specs=[pl.BlockSpec((tt, 1), lambda i: (i, 0)), vec, vec, vec],
        out_specs=[row, row], out_shape=[shp, shp],
        compiler_params=_params(("parallel",)),
    )(pos, inv_lane, sel_a, sel_b)


def _rotate(xv, cs, sn, sa, sb):
    half = ROPE_DIM // 2
    up = pltpu.roll(xv, LANES - half, 1)
    dn = pltpu.roll(xv, half, 1)
    return xv * cs + (dn * sb - up * sa) * sn


def _head_masks():
    h1 = lax.broadcasted_iota(jnp.int32, (1, LANES), 1) < HEAD_DIM
    return h1, jnp.logical_not(h1)


def _split_heads(xv, h1, h2):
    return jnp.where(h1, xv, 0.0).astype(BF16), jnp.where(h2, xv, 0.0).astype(BF16)


def _tri_masks():
    r = lax.broadcasted_iota(jnp.int32, (BLOCK, BLOCK), 0)
    c = lax.broadcasted_iota(jnp.int32, (BLOCK, BLOCK), 1)
    return c <= r, r <= c


def _stream_rows(start, dil):
    if dil == 1:
        return pl.ds(pl.multiple_of(start, BLOCK), BLOCK)
    return pl.ds(start, BLOCK, stride=dil)


def _dil_tile(idx, dil, nb):
    r = idx // nb
    n = idx % nb
    return (_stream_rows(r + dil * BLOCK * n, dil), _stream_rows(r + dil * BLOCK * jnp.maximum(n - 1, 0), dil),
            n > 0)


def _dil_specs(b_dim, s_dim):
    def col(c0):
        return pl.BlockSpec((None, s_dim, LANES),lambda b, h: (b, 0, c0 + h))
    tab = pl.BlockSpec((None, s_dim, LANES),lambda b, h: (b, 0, 0))
    vec = pl.BlockSpec((1, LANES), lambda b, h: (0, 0))
    return col, tab, vec


def _dil_fwd(proj3, cs3, sn3, sel_a, sel_b, rider=None):
    b_dim, s_dim, _ = proj3.shape
    scale = HEAD_DIM ** -0.5
    n_pat = len(DIL_PATTERNS)
    extra, extra_shapes, extra_sems = _rider_parts(rider)
    n_w = len(extra)
    n_steps = b_dim * PAIRS

    def body(*refs):
        q_ref, k_ref, v_ref, cs_ref, sn_ref, sa_ref, sb_ref = refs[:7]
        o16_ref, o32_ref, l_ref = refs[7 + n_w:10 + n_w]
        qr, kr = refs[10 + 2 * n_w:12 + 2 * n_w]
        per_pattern = refs[12 + 2 * n_w:12 + 2 * n_w + 2 * n_pat]
        og, lg = per_pattern[:n_pat], per_pattern[n_pat:]
        step = pl.program_id(0) * PAIRS + pl.program_id(1)
        begin, end = _rider_hooks(rider, refs[7:7 + n_w], refs[10 + n_w:10 + 2 * n_w], refs[-2:], step, n_steps)
        begin()
        h1, h2 = _head_masks()
        cur_ok, prev_ok = _tri_masks()
        sa, sb = sa_ref[...], sb_ref[...]

        def prep(j, _):
            rows = pl.ds(pl.multiple_of(j * DIL_CHUNK, DIL_CHUNK), DIL_CHUNK)
            cs, sn = cs_ref[rows, :], sn_ref[rows, :]
            qr[rows, :] = _rotate(q_ref[rows, :], cs, sn, sa, sb) * scale
            kr[rows, :] = _rotate(k_ref[rows, :], cs, sn, sa, sb)
            return 0

        lax.fori_loop(0, s_dim // DIL_CHUNK, prep, 0)

        for g, (_, dil) in enumerate(DIL_PATTERNS):
            nb = s_dim // dil // BLOCK

            def some(bi, _, g=g, dil=dil, nb=nb):
                tiles = [_dil_tile(bi * DIL_BATCH + t, dil, nb) for t in range(DIL_BATCH)]
                rows = [t[0] for t in tiles]
                q1, q2 = _split_heads(jnp.stack([qr[rw, :] for rw in rows]), h1, h2)
                kc = jnp.stack([kr[rw, :] for rw in rows]).astype(BF16)
                vc1, vc2 = _split_heads(jnp.stack([v_ref[rw, :] for rw in rows]), h1, h2)
                if nb > 1:
                    kp = jnp.stack([kr[t[1], :] for t in tiles]).astype(BF16)
                    vp1, vp2 = _split_heads(jnp.stack([v_ref[t[1], :] for t in tiles]), h1, h2)
                    p_ok = jnp.stack([jnp.logical_and(prev_ok, t[2]) for t in tiles])

                def head(qh, vch, vph):
                    sc = jnp.where(cur_ok, lax.dot_general(qh, kc, BNT, preferred_element_type=F32), -jnp.inf)
                    m = jnp.max(sc, axis=-1, keepdims=True)
                    if nb > 1:
                        sp = jnp.where(p_ok, lax.dot_general(qh, kp, BNT, preferred_element_type=F32), -jnp.inf)
                        m = jnp.maximum(m, jnp.max(sp, axis=-1, keepdims=True))
                    pc = jnp.exp(sc - m)
                    den = jnp.sum(pc, axis=-1, keepdims=True)
                    acc = lax.dot_general(pc.astype(BF16), vch, BNN, preferred_element_type=F32)
                    if nb > 1:
                        pp = jnp.exp(sp - m)
                        den = den + jnp.sum(pp, axis=-1, keepdims=True)
                        acc = acc + lax.dot_general(pp.astype(BF16), vph, BNN, preferred_element_type=F32)
                    return acc / den, m + jnp.log(den)

                o1, l1 = head(q1, vc1, vp1 if nb > 1 else None)
                o2, l2 = head(q2, vc2, vp2 if nb > 1 else None)
                o, l = o1 + o2, jnp.where(h1, l1, l2)
                for t, rw in enumerate(rows):
                    og[g][rw, :] = o[t]
                    lg[g][rw, :] = l[t]
                return 0

            lax.fori_loop(0, dil * nb // DIL_BATCH, some, 0)

        def comb(j, _):
            rows = pl.ds(pl.multiple_of(j * DIL_CHUNK, DIL_CHUNK), DIL_CHUNK)
            ls = [lg[g][rows, :] for g in range(n_pat)]
            m = jnp.maximum(jnp.maximum(ls[0], ls[1]), ls[2])
            es = [jnp.exp(l - m) for l in ls]
            den = es[0] + es[1] + es[2]
            o = (es[0] * og[0][rows, :] + es[1] * og[1][rows, :] + es[2] * og[2][rows, :]) / den
            o16_ref[rows, :] = o.astype(BF16)
            o32_ref[rows, :] = o
            l_ref[rows, :] = m + jnp.log(den)
            return 0

        lax.fori_loop(0, s_dim // DIL_CHUNK, comb, 0)
        end()

    col, tab, vec = _dil_specs(b_dim, s_dim)
    out = pl.BlockSpec((None, s_dim, LANES),lambda b, h: (b, 0, h))
    shp = (b_dim, s_dim, ATT_WIDTH)
    res = pl.pallas_call(
        body, name="dil_fwd", grid=(b_dim, PAIRS),
        in_specs=[col(COL_QA), col(COL_KA), col(COL_VA), tab, tab, vec, vec] + [ANY] * n_w,
        out_specs=[out, out, out] + [ANY] * n_w,
        out_shape=[jax.ShapeDtypeStruct(shp, BF16), jax.ShapeDtypeStruct(shp, F32), jax.ShapeDtypeStruct(shp, F32)]
        + extra_shapes,
        scratch_shapes=[pltpu.VMEM((s_dim, LANES), F32)] * (2 + 2 * n_pat) + extra_sems,
        compiler_params=_params(("arbitrary", "arbitrary")),
    )(proj3, proj3, proj3, cs3, sn3, sel_a, sel_b, *extra)
    return res[:3], res[3:]


def _dil_bwd(proj3, cs3, sn3, sel_a, sel_b, do3, o3, lse3, rider=None):
    b_dim, s_dim, _ = proj3.shape
    scale = HEAD_DIM ** -0.5
    extra, extra_shapes, extra_sems = _rider_parts(rider)
    n_w = len(extra)

    def body(*refs):
        q_ref, k_ref, v_ref, cs_ref, sn_ref, sa_ref, sb_ref, do_ref, o_ref, l_ref = refs[:10]
        dq_ref, dk_ref, dv_ref = refs[10 + n_w:13 + n_w]
        qr, kr, dqa, dka, dva = refs[13 + 2 * n_w:18 + 2 * n_w]
        step = pl.program_id(0) * PAIRS + pl.program_id(1)
        begin, end = _rider_hooks(rider, refs[10:10 + n_w], refs[13 + n_w:13 + 2 * n_w], refs[-2:], step,
                                  b_dim * PAIRS)
        begin()
        h1, h2 = _head_masks()
        cur_ok, prev_ok = _tri_masks()
        sa, sb = sa_ref[...], sb_ref[...]

        def prep(j, _):
            rows = pl.ds(pl.multiple_of(j * DIL_CHUNK, DIL_CHUNK), DIL_CHUNK)
            cs, sn = cs_ref[rows, :], sn_ref[rows, :]
            qr[rows, :] = _rotate(q_ref[rows, :], cs, sn, sa, sb) * scale
            kr[rows, :] = _rotate(k_ref[rows, :], cs, sn, sa, sb)
            zero = jnp.zeros((DIL_CHUNK, LANES), F32)
            dqa[rows, :] = zero
            dka[rows, :] = zero
            dva[rows, :] = zero
            return 0

        lax.fori_loop(0, s_dim // DIL_CHUNK, prep, 0)

        for _, dil in DIL_PATTERNS:
            nb = s_dim // dil // BLOCK

            def some(bi, _, dil=dil, nb=nb):
                tiles = [_dil_tile(bi * DIL_BATCH + t, dil, nb) for t in range(DIL_BATCH)]
                rows = [t[0] for t in tiles]
                q1, q2 = _split_heads(jnp.stack([qr[rw, :] for rw in rows]), h1, h2)
                dof = jnp.stack([do_ref[rw, :] for rw in rows])
                do1, do2 = _split_heads(dof, h1, h2)
                prod = dof * jnp.stack([o_ref[rw, :] for rw in rows])
                delta1 = jnp.sum(jnp.where(h1, prod, 0.0), axis=-1, keepdims=True)
                delta2 = jnp.sum(jnp.where(h2, prod, 0.0), axis=-1, keepdims=True)
                lt = jnp.stack([l_ref[rw, :] for rw in rows])
                lse1 = jnp.max(jnp.where(h1, lt, -jnp.inf), axis=-1, keepdims=True)
                lse2 = jnp.max(jnp.where(h2, lt, -jnp.inf), axis=-1, keepdims=True)

                def side(krows, ok):
                    kf = jnp.stack([kr[kw, :] for kw in krows])
                    k16 = kf.astype(BF16)
                    k1, k2 = _split_heads(kf, h1, h2)
                    v16 = jnp.stack([v_ref[kw, :] for kw in krows]).astype(BF16)

                    def head(qh, doh, lse, delta):
                        sc = lax.dot_general(qh, k16, BNT, preferred_element_type=F32)
                        p = jnp.where(ok, jnp.exp(sc - lse), 0.0)
                        dp = lax.dot_general(doh, v16, BNT, preferred_element_type=F32)
                        return p.astype(BF16), (p * (dp - delta)).astype(BF16)

                    p1, ds1 = head(q1, do1, lse1, delta1)
                    p2, ds2 = head(q2, do2, lse2, delta2)
                    dv = (lax.dot_general(p1, do1, BTN, preferred_element_type=F32)
                          + lax.dot_general(p2, do2, BTN, preferred_element_type=F32))
                    dk = (lax.dot_general(ds1, q1, BTN, preferred_element_type=F32)
                          + lax.dot_general(ds2, q2, BTN, preferred_element_type=F32))
                    for t, kw in enumerate(krows):
                        dva[kw, :] += dv[t]
                        dka[kw, :] += dk[t]
                    return (lax.dot_general(ds1, k1, BNN, preferred_element_type=F32)
                            + lax.dot_general(ds2, k2, BNN, preferred_element_type=F32))

                dq = side(rows, cur_ok)
                if nb > 1:
                    dq = dq + side([t[1] for t in tiles], jnp.stack([jnp.logical_and(prev_ok, t[2]) for t in tiles]))
                for t, rw in enumerate(rows):
                    dqa[rw, :] += dq[t] * scale
                return 0

            lax.fori_loop(0, dil * nb // DIL_BATCH, some, 0)

        def finish(j, _):
            rows = pl.ds(pl.multiple_of(j * DIL_CHUNK, DIL_CHUNK), DIL_CHUNK)
            cs, sn = cs_ref[rows, :], -sn_ref[rows, :]
            dq_ref[rows, :] = _rotate(dqa[rows, :], cs, sn, sa, sb).astype(BF16)
            dk_ref[rows, :] = _rotate(dka[rows, :], cs, sn, sa, sb).astype(BF16)
            dv_ref[rows, :] = dva[rows, :].astype(BF16)
            return 0

        lax.fori_loop(0, s_dim // DIL_CHUNK, finish, 0)
        end()

    col, tab, vec = _dil_specs(b_dim, s_dim)
    out = pl.BlockSpec((None, s_dim, LANES),lambda b, h: (b, 0, h))
    shp = jax.ShapeDtypeStruct((b_dim, s_dim, ATT_WIDTH), BF16)
    acc = pltpu.VMEM((s_dim, LANES), F32)
    res = pl.pallas_call(
        body, name="dil_bwd", grid=(b_dim, PAIRS),
        in_specs=[col(COL_QA), col(COL_KA), col(COL_VA), tab, tab, vec, vec, out, out, out] + [ANY] * n_w,
        out_specs=[out, out, out] + [ANY] * n_w, out_shape=[shp, shp, shp] + extra_shapes,
        scratch_shapes=[acc, acc, acc, acc, acc] + extra_sems,
        compiler_params=_params(("arbitrary", "arbitrary")),
    )(proj3, proj3, proj3, cs3, sn3, sel_a, sel_b, do3, o3, lse3, *extra)
    return res[:3], res[3:]


def _split_dot(x, tri):
    hi = x.astype(BF16)
    lo = (x - hi.astype(F32)).astype(BF16)
    return jnp.dot(hi, tri, preferred_element_type=F32) + jnp.dot(lo, tri, preferred_element_type=F32)


def _log_sigmoid(z):
    return jnp.minimum(z, 0.0) - jnp.log(1.0 + jnp.exp(-jnp.abs(z)))


def _sb_scores(qh, k16, valid):
    z = lax.dot_general(qh, k16, NT, preferred_element_type=F32)
    ls = _log_sigmoid(z)
    l1m = ls - z
    return ls, (l1m if valid is None else jnp.where(valid, l1m, 0.0))


def _sb_consts():
    r = lax.broadcasted_iota(jnp.int32, (BLOCK, BLOCK), 0)
    c = lax.broadcasted_iota(jnp.int32, (BLOCK, BLOCK), 1)
    after = (r > c).astype(BF16)
    before = (r < c).astype(BF16)
    qrow = lax.broadcasted_iota(jnp.int32, (SB_ROWS, BLOCK), 0)
    kcol = lax.broadcasted_iota(jnp.int32, (SB_ROWS, BLOCK), 1)
    return after, before, qrow, kcol


def _below(whole, lo, delta):
    if lo == 0:
        return whole + delta
    return whole + jnp.concatenate([jnp.zeros((lo,) + delta.shape[1:], delta.dtype), delta], axis=0)


def _pairs_loop(n_blocks, step, carry):
    def several(i, c):
        for j in range(SB_STEP):
            c = step(SB_STEP * i + j, c)
        return c

    return lax.fori_loop(0, n_blocks // SB_STEP, several, carry)


def _sb_fwd(proj3, rider=None):
    b_dim, s_dim, _ = proj3.shape
    scale = HEAD_DIM ** -0.5
    per = SB_ROWS // BLOCK
    extra, extra_shapes, extra_sems = _rider_parts(rider)
    n_w = len(extra)

    def body(*refs):
        q_ref, k_ref, v_ref = refs[:3]
        o_ref = refs[3 + n_w]
        step = pl.program_id(0) * PAIRS + pl.program_id(1)
        begin, end = _rider_hooks(rider, refs[3:3 + n_w], refs[4 + n_w:4 + 2 * n_w], refs[-2:], step, b_dim * PAIRS)
        begin()
        h1, h2 = _head_masks()
        after, _, qrow, kcol = _sb_consts()

        def qloop(qi, _):
            rows = pl.ds(pl.multiple_of(qi * SB_ROWS, SB_ROWS), SB_ROWS)
            q1, q2 = _split_heads(q_ref[rows, :] * scale, h1, h2)
            first = qi * per

            def block(kb, carry, lo):
                acc, run1, run2 = carry
                krows = pl.ds(pl.multiple_of(kb * BLOCK, BLOCK), BLOCK)
                k16 = k_ref[krows, :].astype(BF16)
                v1, v2 = _split_heads(v_ref[krows, :], h1, h2)
                valid = None if lo is None else kcol[:SB_ROWS - lo] < qrow[:SB_ROWS - lo]
                lo = lo or 0

                def head(qh, vh, run):
                    ls, l1m = _sb_scores(qh[lo:], k16, valid)
                    a = jnp.exp(ls + _split_dot(l1m, after) + run[lo:])
                    if valid is not None:
                        a = jnp.where(valid, a, 0.0)
                    return (jnp.dot(a.astype(BF16), vh, preferred_element_type=F32),
                            _below(run, lo, jnp.sum(l1m, axis=-1, keepdims=True)))

                o1, run1 = head(q1, v1, run1)
                o2, run2 = head(q2, v2, run2)
                return _below(acc, lo, o1 + o2), run1, run2

            zcol = jnp.zeros((SB_ROWS, 1), F32)
            carry = (jnp.zeros((SB_ROWS, LANES), F32), zcol, zcol)
            for kl in reversed(range(per)):
                carry = block(first + kl, carry, kl * BLOCK)
            acc, _, _ = _pairs_loop(first, lambda i, c: block(first - 1 - i, c, None), carry)
            o_ref[rows, :] = acc.astype(BF16)
            return 0

        lax.fori_loop(0, s_dim // SB_ROWS, qloop, 0)
        end()

    def col(c0):
        return pl.BlockSpec((None, s_dim, LANES),lambda b, h: (b, 0, c0 + h))

    res = pl.pallas_call(
        body, name="sb_fwd", grid=(b_dim, PAIRS),
        in_specs=[col(COL_QB), col(COL_KB), col(COL_VB)] + [ANY] * n_w, out_specs=[col(0)] + [ANY] * n_w,
        out_shape=[jax.ShapeDtypeStruct((b_dim, s_dim, ATT_WIDTH), BF16)] + extra_shapes,
        scratch_shapes=extra_sems,
        compiler_params=_params(("arbitrary", "arbitrary")),
    )(proj3, proj3, proj3, *extra)
    return res[0], res[1:]


def _sb_bwd(proj3, do3, rider=None):
    b_dim, s_dim, _ = proj3.shape
    scale = HEAD_DIM ** -0.5
    per = SB_ROWS // BLOCK
    nkb_max = s_dim // BLOCK
    extra, extra_shapes, extra_sems = _rider_parts(rider)
    n_w = len(extra)

    def body(*refs):
        q_ref, k_ref, v_ref, do_ref = refs[:4]
        dq_ref, dk_ref, dv_ref = refs[4 + n_w:7 + n_w]
        dka, dva, e_ref, sg_ref = refs[7 + 2 * n_w:11 + 2 * n_w]
        step = pl.program_id(0) * PAIRS + pl.program_id(1)
        begin, end = _rider_hooks(rider, refs[4:4 + n_w], refs[7 + n_w:7 + 2 * n_w], refs[-2:], step, b_dim * PAIRS)
        begin()
        h1, h2 = _head_masks()
        after, before, qrow, kcol = _sb_consts()
        dka[...] = jnp.zeros_like(dka)
        dva[...] = jnp.zeros_like(dva)

        def qloop(qi, _):
            rows = pl.ds(pl.multiple_of(qi * SB_ROWS, SB_ROWS), SB_ROWS)
            q1, q2 = _split_heads(q_ref[rows, :] * scale, h1, h2)
            do1, do2 = _split_heads(do_ref[rows, :].astype(F32), h1, h2)
            first = qi * per

            def pass1(kb, carry, lo):
                run1, run2 = carry
                krows = pl.ds(pl.multiple_of(kb * BLOCK, BLOCK), BLOCK)
                k16 = k_ref[krows, :].astype(BF16)
                v16 = v_ref[krows, :].astype(BF16)
                valid = None if lo is None else kcol[:SB_ROWS - lo] < qrow[:SB_ROWS - lo]
                lo = lo or 0
                part = pl.ds(lo, SB_ROWS - lo)

                def head(h, qh, doh, run):
                    ls, l1m = _sb_scores(qh[lo:], k16, valid)
                    a = jnp.exp(ls + _split_dot(l1m, after) + run[lo:])
                    if valid is not None:
                        a = jnp.where(valid, a, 0.0)
                    da = lax.dot_general(doh[lo:], v16, NT, preferred_element_type=F32)
                    e_ref[h, kb, part, :] = a * da
                    sg_ref[h, kb, part, :] = jnp.exp(ls)
                    return a.astype(BF16), _below(run, lo, jnp.sum(l1m, axis=-1, keepdims=True))

                a1, run1 = head(0, q1, do1, run1)
                a2, run2 = head(1, q2, do2, run2)
                dva[krows, :] += (lax.dot_general(a1, do1[lo:], TN, preferred_element_type=F32)
                                  + lax.dot_general(a2, do2[lo:], TN, preferred_element_type=F32))
                return run1, run2

            zcol = jnp.zeros((SB_ROWS, 1), F32)
            carry = (zcol, zcol)
            for kl in reversed(range(per)):
                carry = pass1(first + kl, carry, kl * BLOCK)
            _pairs_loop(first, lambda i, c: pass1(first - 1 - i, c, None), carry)

            def pass2(kb, carry, lo):
                dq, pre1, pre2 = carry
                krows = pl.ds(pl.multiple_of(kb * BLOCK, BLOCK), BLOCK)
                k1, k2 = _split_heads(k_ref[krows, :], h1, h2)
                valid = None if lo is None else kcol[:SB_ROWS - lo] < qrow[:SB_ROWS - lo]
                lo = lo or 0
                part = pl.ds(lo, SB_ROWS - lo)

                def head(h, pre):
                    ev = e_ref[h, kb, part, :]
                    sg = sg_ref[h, kb, part, :]
                    dz = ev * (1.0 - sg) - (_split_dot(ev, before) + pre[lo:]) * sg
                    if valid is not None:
                        dz = jnp.where(valid, dz, 0.0)
                    return dz.astype(BF16), _below(pre, lo, jnp.sum(ev, axis=-1, keepdims=True))

                dz1, pre1 = head(0, pre1)
                dz2, pre2 = head(1, pre2)
                dka[krows, :] += (lax.dot_general(dz1, q1[lo:], TN, preferred_element_type=F32)
                                  + lax.dot_general(dz2, q2[lo:], TN, preferred_element_type=F32))
                dq = _below(dq, lo, jnp.dot(dz1, k1, preferred_element_type=F32)
                            + jnp.dot(dz2, k2, preferred_element_type=F32))
                return dq, pre1, pre2

            carry = _pairs_loop(first, lambda i, c: pass2(i, c, None), (jnp.zeros((SB_ROWS, LANES), F32), zcol, zcol))
            for kl in range(per):
                carry = pass2(first + kl, carry, kl * BLOCK)
            dq = carry[0]
            dq_ref[rows, :] = (dq * scale).astype(BF16)
            return 0

        lax.fori_loop(0, s_dim // SB_ROWS, qloop, 0)
        dk_ref[...] = dka[...].astype(BF16)
        dv_ref[...] = dva[...].astype(BF16)
        end()

    def col(c0):
        return pl.BlockSpec((None, s_dim, LANES),lambda b, h: (b, 0, c0 + h))

    shp = jax.ShapeDtypeStruct((b_dim, s_dim, ATT_WIDTH), BF16)
    acc = pltpu.VMEM((s_dim, LANES), F32)
    strip = pltpu.VMEM((2, nkb_max, SB_ROWS, BLOCK), F32)
    res = pl.pallas_call(
        body, name="sb_bwd", grid=(b_dim, PAIRS),
        in_specs=[col(COL_QB), col(COL_KB), col(COL_VB), col(0)] + [ANY] * n_w,
        out_specs=[col(0), col(0), col(0)] + [ANY] * n_w,
        out_shape=[shp, shp, shp] + extra_shapes,
        scratch_shapes=[acc, acc, strip, strip] + extra_sems,
        compiler_params=_params(("arbitrary", "arbitrary")),
    )(proj3, proj3, proj3, do3, *extra)
    return res[:3], res[3:]


def _sigmoid(x):
    return 1.0 / (1.0 + jnp.exp(-x))


def _gate_out_norm(proj, ua, ub, w_out, x, g, *, tt=512):
    t_dim, d = ua.shape

    def body(ga_ref, gb_ref, ua_ref, ub_ref, w_ref, x_ref, g_ref, m_ref, h_ref, n_ref):
        mixed = (_sigmoid(ga_ref[...]) * ua_ref[...] + _sigmoid(gb_ref[...]) * ub_ref[...]).astype(BF16)
        m_ref[...] = mixed
        hv = x_ref[...] + jnp.dot(mixed, w_ref[...], preferred_element_type=F32)
        h_ref[...] = hv
        r = lax.rsqrt(jnp.mean(hv * hv, axis=-1, keepdims=True) + RMS_EPS)
        n_ref[...] = ((hv * r) * g_ref[...]).astype(BF16)

    row = pl.BlockSpec((tt, d), lambda i: (i, 0))
    return pl.pallas_call(
        body, name="gate_out_norm", grid=(t_dim // tt,),
        in_specs=[pl.BlockSpec((tt, d), lambda i: (i, 3)), pl.BlockSpec((tt, d), lambda i: (i, 4)), row, row,
                  pl.BlockSpec((d, d), lambda i: (0, 0)), row, pl.BlockSpec((1, d), lambda i: (0, 0))],
        out_specs=[row, row, row],
        out_shape=[jax.ShapeDtypeStruct((t_dim, d), BF16), jax.ShapeDtypeStruct((t_dim, d), F32),
                   jax.ShapeDtypeStruct((t_dim, d), BF16)],
        compiler_params=_params(("parallel",)),
    )(proj, proj, ua, ub, w_out, x, g)


def _out_dx_gate_bwd(dh, w_out, proj, ua, ub, *, tt=512):
    t_dim, d = ua.shape

    def body(dh_ref, w_ref, ga_ref, gb_ref, ua_ref, ub_ref, dua_ref, dub_ref, dg_ref):
        dm = lax.dot_general(dh_ref[...].astype(BF16), w_ref[...], NT, preferred_element_type=F32)
        sa = _sigmoid(ga_ref[...])
        sb = _sigmoid(gb_ref[...])
        dua_ref[...] = (dm * sa).astype(BF16)
        dub_ref[...] = (dm * sb).astype(BF16)
        dg_ref[:, :d] = (dm * ua_ref[...] * (sa * (1.0 - sa))).astype(BF16)
        dg_ref[:, d:] = (dm * ub_ref[...] * (sb * (1.0 - sb))).astype(BF16)

    row = pl.BlockSpec((tt, d), lambda i: (i, 0))
    wide = pl.BlockSpec((tt, 2 * d), lambda i: (i, 0))
    return pl.pallas_call(
        body, name="out_dx_gate_bwd", grid=(t_dim // tt,),
        in_specs=[row, pl.BlockSpec((d, d), lambda i: (0, 0)),
                  pl.BlockSpec((tt, d), lambda i: (i, 3)), pl.BlockSpec((tt, d), lambda i: (i, 4)), row, row],
        out_specs=[row, row, wide],
        out_shape=[jax.ShapeDtypeStruct((t_dim, d), BF16), jax.ShapeDtypeStruct((t_dim, d), BF16),
                   jax.ShapeDtypeStruct((t_dim, 2 * d), BF16)],
        compiler_params=_params(("parallel",)),
    )(dh, w_out, proj, proj, ua, ub)


def _ffn_up_swiglu(n, wg3, wu3, *, tt=1024):
    t_dim, d = n.shape
    n_s, _, f4 = wg3.shape

    def body(n_ref, wg_ref, wu_ref, g_ref, u_ref, a_ref):
        nv = n_ref[...]
        gv = jnp.dot(nv, wg_ref[...], preferred_element_type=F32)
        uv = jnp.dot(nv, wu_ref[...], preferred_element_type=F32)
        g_ref[...] = gv.astype(BF16)
        u_ref[...] = uv.astype(BF16)
        a_ref[...] = (gv * _sigmoid(gv) * uv).astype(BF16)

    wspec = pl.BlockSpec((None, d, f4), lambda i, s: (s, 0, 0))
    ospec = pl.BlockSpec((None, tt, f4), lambda i, s: (s, i, 0))
    shp = (n_s, t_dim, f4)
    return pl.pallas_call(
        body, name="ffn_up_swiglu", grid=(t_dim // tt, n_s),
        in_specs=[pl.BlockSpec((tt, d), lambda i, s: (i, 0)), wspec, wspec], out_specs=[ospec, ospec, ospec],
        out_shape=[jax.ShapeDtypeStruct(shp, BF16)] * 3,
        compiler_params=_params(("parallel", "parallel")),
    )(n, wg3, wu3)


def _ffn_down_dx_swiglu(dh, wd3, g3, u3, *, tt=1024):
    t_dim, d = dh.shape
    n_s, f4, _ = wd3.shape

    def body(dh_ref, w_ref, g_ref, u_ref, dg_ref, du_ref):
        da = lax.dot_general(dh_ref[...].astype(BF16), w_ref[...], NT, preferred_element_type=F32)
        gv = g_ref[...].astype(F32)
        sg = _sigmoid(gv)
        dg_ref[...] = (da * u_ref[...].astype(F32) * (sg + gv * sg * (1.0 - sg))).astype(BF16)
        du_ref[...] = (da * (gv * sg)).astype(BF16)

    spec = pl.BlockSpec((None, tt, f4), lambda i, s: (s, i, 0))
    shp = jax.ShapeDtypeStruct((n_s, t_dim, f4), BF16)
    return pl.pallas_call(
        body, name="ffn_down_dx_swiglu", grid=(t_dim // tt, n_s),
        in_specs=[pl.BlockSpec((tt, d), lambda i, s: (i, 0)), pl.BlockSpec((None, f4, d), lambda i, s: (s, 0, 0)),
                  spec, spec],
        out_specs=[spec, spec], out_shape=[shp, shp],
        compiler_params=_params(("parallel", "parallel")),
    )(dh, wd3, g3, u3)


def _mem_fwd(qm, kvm, *, tt=2048):
    b_dim, s_dim, _ = qm.shape
    n_mem = kvm.shape[1]
    scale = MEM_HEAD_DIM ** -0.5

    def body(q_ref, k_ref, v_ref, o_ref):
        sc = lax.dot_general(q_ref[0], k_ref[0], NT, preferred_element_type=F32) * scale
        p = jnp.exp(sc - jnp.max(sc, axis=-1, keepdims=True))
        p = p / jnp.sum(p, axis=-1, keepdims=True)
        o_ref[0] = jnp.dot(p.astype(BF16), v_ref[0], preferred_element_type=F32).astype(BF16)

    qs = pl.BlockSpec((1, tt, MEM_HEAD_DIM), lambda b, h, i: (b, i, h))
    return pl.pallas_call(
        body, name="mem_fwd", grid=(b_dim, N_HEADS_MEM, s_dim // tt),
        in_specs=[qs, pl.BlockSpec((1, n_mem, MEM_HEAD_DIM), lambda b, h, i: (b, 0, h)),
                  pl.BlockSpec((1, n_mem, MEM_HEAD_DIM), lambda b, h, i: (b, 0, N_HEADS_MEM + h))],
        out_specs=qs, out_shape=jax.ShapeDtypeStruct(qm.shape, BF16),
        compiler_params=_params(("parallel", "parallel", "parallel")),
    )(qm, kvm, kvm)


def _mem_bwd(qm, kvm, dom, *, tt=2048):
    b_dim, s_dim, _ = qm.shape
    n_mem = kvm.shape[1]
    scale = MEM_HEAD_DIM ** -0.5

    def body(q_ref, k_ref, v_ref, do_ref, dq_ref, dk_ref, dv_ref):
        qv, kv, vv, dov = q_ref[0], k_ref[0], v_ref[0], do_ref[0]
        sc = lax.dot_general(qv, kv, NT, preferred_element_type=F32) * scale
        p = jnp.exp(sc - jnp.max(sc, axis=-1, keepdims=True))
        p = p / jnp.sum(p, axis=-1, keepdims=True)
        dp = lax.dot_general(dov, vv, NT, preferred_element_type=F32)
        ds = (p * (dp - jnp.sum(p * dp, axis=-1, keepdims=True)) * scale).astype(BF16)
        dq_ref[0] = jnp.dot(ds, kv, preferred_element_type=F32).astype(BF16)

        @pl.when(pl.program_id(2) == 0)
        def _():
            dk_ref[...] = jnp.zeros_like(dk_ref)
            dv_ref[...] = jnp.zeros_like(dv_ref)

        dk_ref[0] += lax.dot_general(ds, qv, TN, preferred_element_type=F32)
        dv_ref[0] += lax.dot_general(p.astype(BF16), dov, TN, preferred_element_type=F32)

    qs = pl.BlockSpec((1, tt, MEM_HEAD_DIM), lambda b, h, i: (b, i, h))
    ks = pl.BlockSpec((1, n_mem, MEM_HEAD_DIM), lambda b, h, i: (b, 0, h))
    vs = pl.BlockSpec((1, n_mem, MEM_HEAD_DIM), lambda b, h, i: (b, 0, N_HEADS_MEM + h))
    return pl.pallas_call(
        body, name="mem_bwd", grid=(b_dim, N_HEADS_MEM, s_dim // tt),
        in_specs=[qs, ks, vs, qs], out_specs=[qs, ks, ks],
        out_shape=[jax.ShapeDtypeStruct(qm.shape, BF16), jax.ShapeDtypeStruct((b_dim, n_mem, MEM_WIDTH), F32),
                   jax.ShapeDtypeStruct((b_dim, n_mem, MEM_WIDTH), F32)],
        compiler_params=_params(("parallel", "parallel", "arbitrary")),
    )(qm, kvm, kvm, dom)


def _adamw_math(wv, gv, mv, vv):
    nm = ADAM_B1 * mv + (1.0 - ADAM_B1) * gv
    nv = ADAM_B2 * vv + (1.0 - ADAM_B2) * (gv * gv)
    m_hat = nm / (1.0 - ADAM_B1 ** ADAM_STEP)
    v_hat = nv / (1.0 - ADAM_B2 ** ADAM_STEP)
    return -ADAM_LR * (m_hat / (jnp.sqrt(v_hat) + ADAM_EPS) + ADAM_WD * wv), nm, nv


def _adamw(w, g, m, v, *, name):
    rows, cols = w.shape
    tr = _tile(rows, 256, 8)

    def body(w_ref, g_ref, m_ref, v_ref, d_ref, nm_ref, nv_ref):
        d_ref[...], nm_ref[...], nv_ref[...] = _adamw_math(w_ref[...], g_ref[...], m_ref[...], v_ref[...])

    spec = pl.BlockSpec((tr, cols), lambda i: (i, 0))
    shp = jax.ShapeDtypeStruct((rows, cols), F32)
    return pl.pallas_call(
        body, name=name, grid=(rows // tr,),
        in_specs=[spec] * 4, out_specs=[spec] * 3, out_shape=[shp] * 3,
        compiler_params=_params(("parallel",)),
    )(w, g, m, v)


def _prefetch_spec(grid, in_specs, out_specs):
    return pltpu.PrefetchScalarGridSpec(num_scalar_prefetch=1, grid=grid, in_specs=in_specs, out_specs=out_specs)


def _adamw_halves(w, mine, theirs, m, v, c_idx, *, name):
    rows, cols = w.shape
    half = rows // 2
    tr = _tile(half, _row_cap(cols), 8)
    nh = half // tr

    def body(c_ref, w_ref, mine_ref, theirs_ref, m_ref, v_ref, g_ref, d_ref, nm_ref, nv_ref):
        gv = jnp.where(pl.program_id(0) == c_ref[0], mine_ref[...], theirs_ref[...])
        g_ref[...] = gv
        d_ref[...], nm_ref[...], nv_ref[...] = _adamw_math(w_ref[...], gv, m_ref[...], v_ref[...])

    full = pl.BlockSpec((tr, cols), lambda h, i, c_ref: (h * nh + i, 0))
    part = pl.BlockSpec((tr, cols), lambda h, i, c_ref: (i, 0))
    shp = jax.ShapeDtypeStruct((rows, cols), F32)
    return pl.pallas_call(
        body, name=name, grid_spec=_prefetch_spec((2, nh), [full, part, part, full, full], [full] * 4),
        out_shape=[shp] * 4,
        compiler_params=_params(("parallel", "parallel")),
    )(c_idx, w, mine, theirs, m, v)


def _pair_sum(g3, theirs, c_idx, *, name):
    n, rows, cols = g3.shape
    half = rows // 2
    tr = _tile(half, _row_cap(cols), 16)

    def body(c_ref, g_ref, t_ref, o_ref):
        o_ref[...] = (g_ref[...] + t_ref[...]).astype(BF16)

    part = pl.BlockSpec((None, tr, cols), lambda s, i, c_ref: (s, i, 0))
    return pl.pallas_call(
        body, name=name,
        grid_spec=_prefetch_spec((n, half // tr),
                                 [pl.BlockSpec((None, None, tr, cols), lambda s, i, c_ref: (s, c_ref[0], i, 0)), part],
                                 part),
        out_shape=jax.ShapeDtypeStruct((n, half, cols), BF16),
        compiler_params=_params(("parallel", "parallel")),
    )(c_idx, g3.reshape(n, 2, half, cols), theirs)


def _chip_sum(pair, recv, s_idx, *, name):
    _, half, cols = pair.shape
    tr = _tile(half, _row_cap(cols), 16)

    def body(s_ref, p_ref, r_ref, o_ref):
        o_ref[...] = ((p_ref[...].astype(F32) + r_ref[0].astype(F32)) + r_ref[1].astype(F32)) + r_ref[2].astype(F32)

    return pl.pallas_call(
        body, name=name,
        grid_spec=_prefetch_spec((half // tr,),
                                 [pl.BlockSpec((None, tr, cols), lambda i, s_ref: (s_ref[0], i, 0)),
                                  pl.BlockSpec((N_CHIPS - 1, tr, cols), lambda i, s_ref: (0, i, 0))],
                                 pl.BlockSpec((tr, cols), lambda i, s_ref: (i, 0))),
        out_shape=jax.ShapeDtypeStruct((half, cols), F32),
        compiler_params=_params(("parallel",)),
    )(s_idx, pair, recv)


def _sum8(parts):
    n, rows, cols = parts.shape

    def body(p_ref, o_ref):
        acc = p_ref[0]
        for i in range(1, n):
            acc = acc + p_ref[i]
        o_ref[...] = acc

    return pl.pallas_call(
        body, name="small_sum", grid=(1,),
        in_specs=[pl.BlockSpec((n, rows, cols), lambda i: (0, 0, 0))],
        out_specs=pl.BlockSpec((rows, cols), lambda i: (0, 0)),
        out_shape=jax.ShapeDtypeStruct((rows, cols), parts.dtype),
        compiler_params=_params(("arbitrary",)),
    )(parts)


def _place():
    return lax.axis_index("x"), lax.axis_index("y"), lax.axis_index("c")


ANY = pl.BlockSpec(memory_space=pl.ANY)


def _rider_parts(rider):
    if rider is None:
        return (), [], []
    kind, arrays = rider
    n = len(arrays)
    shapes = {"gather": _gathered_shapes, "pair": _pair_shapes, "chip": _chip_shapes}[kind](arrays)
    sems = _gather_sems(n) if kind == "gather" else _exchange_sems(n if kind == "pair" else 3 * n)
    return tuple(arrays), shapes, sems


def _rider_hooks(rider, ins, outs, sems, step, n_steps):
    if rider is None:
        return (lambda: None), (lambda: None)
    if rider[0] == "gather":
        start, forward, finish = _gather_steps(ins, outs, *sems)
    else:
        start, finish = {"pair": _pair_steps, "chip": _chip_steps}[rider[0]](ins, outs, *sems)
        forward = None

    def begin():
        pl.when(step == 0)(start)

    def end():
        if forward is not None:
            pl.when(step == n_steps - 2)(forward)
        pl.when(step == n_steps - 1)(finish)

    return begin, end


def _gathered_shapes(shards):
    return [jax.ShapeDtypeStruct((N_CHIPS,) + s.shape, s.dtype) for s in shards]


def _gather_sems(n):
    return [pltpu.SemaphoreType.DMA((7 * n,)), pltpu.SemaphoreType.DMA((7 * n,))]


def _gather_steps(ins, outs, send_sems, recv_sems):
    n = len(ins)
    halves = [r.shape[0] // 2 for r in ins]
    x, y, c = _place()
    my_chip = 2 * x + y
    me, sibling = (x, y, c), (x, y, 1 - c)
    chips = [(1 - x, y), (x, 1 - y), (1 - x, 1 - y)]

    def half_of(w, chip, pc):
        return outs[w].at[chip, pl.ds(pc * halves[w], halves[w]), :]

    def copy(w, k, src, dst, to):
        return pltpu.make_async_remote_copy(
            src_ref=src, dst_ref=dst, send_sem=send_sems.at[7 * w + k], recv_sem=recv_sems.at[7 * w + k],
            device_id=to, device_id_type=MESH)

    def firsts():
        cps = []
        for w in range(n):
            cps.append(copy(w, 0, ins[w], outs[w].at[my_chip], sibling))
            mine = ins[w].at[pl.ds(c * halves[w], halves[w]), :]
            for j, (px, py) in enumerate(chips):
                cps.append(copy(w, 1 + j, mine, half_of(w, my_chip, c), (px, py, c)))
        return cps

    def passes():
        return [copy(w, 4 + j, half_of(w, 2 * px + py, c), half_of(w, 2 * px + py, c), sibling)
                for w in range(n) for j, (px, py) in enumerate(chips)]

    def start():
        for cp in firsts():
            cp.start()

    def forward():
        fws = passes()
        for w in range(n):
            for j, (px, py) in enumerate(chips):
                landed = half_of(w, 2 * px + py, c)
                copy(w, 1 + j, landed, landed, me).wait_recv()
                fws[3 * w + j].start()

    def finish():
        for w in range(n):
            copy(w, 0, ins[w], outs[w].at[my_chip], me).wait_recv()
            for j, (px, py) in enumerate(chips):
                landed = half_of(w, 2 * px + py, 1 - c)
                copy(w, 4 + j, landed, landed, me).wait_recv()
        for cp in firsts() + passes():
            cp.wait_send()

    return start, forward, finish


def _pair_shapes(grads):
    return [jax.ShapeDtypeStruct((g.shape[0], g.shape[1] // 2, g.shape[2]), g.dtype) for g in grads]


def _exchange_sems(n):
    return [pltpu.SemaphoreType.DMA((n,)), pltpu.SemaphoreType.DMA((n,))]


def _exchange_steps(copies):
    def start():
        for cp in copies():
            cp.start()

    def finish():
        for cp in copies():
            cp.wait()

    return start, finish


def _pair_steps(ins, outs, send_sems, recv_sems):
    x, y, c = _place()

    def copies():
        return [pltpu.make_async_remote_copy(
            src_ref=ins[w].at[:, pl.ds((1 - c) * (ins[w].shape[1] // 2), ins[w].shape[1] // 2), :], dst_ref=outs[w],
            send_sem=send_sems.at[w], recv_sem=recv_sems.at[w], device_id=(x, y, 1 - c), device_id_type=MESH)
            for w in range(len(ins))]

    return _exchange_steps(copies)


def _chip_shapes(pairs):
    return [jax.ShapeDtypeStruct((N_CHIPS - 1,) + p.shape[1:], p.dtype) for p in pairs]


def _chip_steps(ins, outs, send_sems, recv_sems):
    x, y, c = _place()
    others = [(1 - x, y), (x, 1 - y), (1 - x, 1 - y)]

    def copies():
        return [pltpu.make_async_remote_copy(
            src_ref=ins[w].at[2 * px + py], dst_ref=outs[w].at[j],
            send_sem=send_sems.at[3 * w + j], recv_sem=recv_sems.at[3 * w + j],
            device_id=(px, py, c), device_id_type=MESH)
            for w in range(len(ins)) for j, (px, py) in enumerate(others)]

    return _exchange_steps(copies)


def _swap_halves(mine):
    n = len(mine)

    def body(*refs):
        ins, outs, send_sems, recv_sems = refs[:n], refs[n:2 * n], refs[2 * n], refs[2 * n + 1]
        x, y, c = _place()
        copies = [pltpu.make_async_remote_copy(
            src_ref=ins[w], dst_ref=outs[w], send_sem=send_sems.at[w], recv_sem=recv_sems.at[w],
            device_id=(x, y, 1 - c), device_id_type=MESH) for w in range(n)]
        for cp in copies:
            cp.start()
        for cp in copies:
            cp.wait()

    return pl.pallas_call(
        body, name="grad_swap_halves",
        out_shape=[jax.ShapeDtypeStruct(h.shape, h.dtype) for h in mine],
        in_specs=[ANY] * n, out_specs=[ANY] * n,
        scratch_shapes=[pltpu.SemaphoreType.DMA((n,)), pltpu.SemaphoreType.DMA((n,))],
    )(*mine)


def _gather_small(small):
    srows, cols = small.shape

    def body(s_ref, all_ref, send_sems, recv_sems, local_sem):
        x, y, c = _place()
        me = 4 * x + 2 * y + c
        keep_small = pltpu.make_async_copy(s_ref, all_ref.at[me], local_sem)
        keep_small.start()
        sends = []
        for kk in range(1, 8):
            peer = (x ^ (kk >> 2), y ^ ((kk >> 1) & 1), c ^ (kk & 1))
            sends.append(pltpu.make_async_remote_copy(
                src_ref=s_ref, dst_ref=all_ref.at[me],
                send_sem=send_sems.at[kk], recv_sem=recv_sems.at[kk], device_id=peer, device_id_type=MESH))
        for cp in sends:
            cp.start()
        for kk in range(1, 8):
            px, py, pc = x ^ (kk >> 2), y ^ ((kk >> 1) & 1), c ^ (kk & 1)
            pltpu.make_async_remote_copy(
                src_ref=s_ref, dst_ref=all_ref.at[4 * px + 2 * py + pc],
                send_sem=send_sems.at[kk], recv_sem=recv_sems.at[kk], device_id=(px, py, pc),
                device_id_type=MESH).wait_recv()
        for cp in sends:
            cp.wait_send()
        keep_small.wait()

    return pl.pallas_call(
        body, name="gather_small",
        out_shape=jax.ShapeDtypeStruct((8, srows, cols), small.dtype),
        in_specs=[ANY], out_specs=ANY,
        scratch_shapes=[pltpu.SemaphoreType.DMA((8,)), pltpu.SemaphoreType.DMA((8,)), pltpu.SemaphoreType.DMA],
    )(small)


SHARDED = (("w_in", D_MODEL, IN_COLS, 1), ("w_up_a", ATT_WIDTH, D_MODEL, 1), ("w_up_b", ATT_WIDTH, D_MODEL, 1),
           ("w_out", D_MODEL, D_MODEL, 0), ("w_q_mem", D_MODEL, MEM_WIDTH, 0), ("w_kv_mem", D_MODEL, 2 * MEM_WIDTH, 0),
           ("w_o_mem", MEM_WIDTH, D_MODEL, 1), ("w_ffn_gate", D_MODEL, D_FF, 1), ("w_ffn_up", D_MODEL, D_FF, 1),
           ("w_ffn_down", D_FF, D_MODEL, 0))
NAMES = tuple(n for n, _, _, _ in SHARDED)
EARLY, LATE = NAMES[:1], NAMES[1:]
GAINS = ("g_mix", "g_mem_q", "g_mem_kv", "g_ffn", "g_final")


def _natural(w3):
    n, r, c = w3.shape
    return w3.reshape(n * r, c)


def _shard_major(g, axis):
    if axis == 1:
        return g
    r, c = g.shape
    return g.reshape(N_CHIPS, r // N_CHIPS, c)


def kernel(x, mem, positions, g_mix, w_in, w_up_a, w_up_b, w_out, g_mem_q, g_mem_kv, w_q_mem, w_kv_mem, w_o_mem, g_ffn, w_ffn_gate, w_ffn_up, w_ffn_down, g_final, loss_target, m_g_mix, m_w_in, m_w_up_a, m_w_up_b, m_w_out, m_g_mem_q, m_g_mem_kv, m_w_q_mem, m_w_kv_mem, m_w_o_mem, m_g_ffn, m_w_ffn_gate, m_w_ffn_up, m_w_ffn_down, m_g_final, v_g_mix, v_w_in, v_w_up_a, v_w_up_b, v_w_out, v_g_mem_q, v_g_mem_kv, v_w_q_mem, v_w_kv_mem, v_w_o_mem, v_g_ffn, v_w_ffn_gate, v_w_ffn_up, v_w_ffn_down, v_g_final):
    given = dict(locals())
    shards = {n: given[n][0] for n, _, _, _ in SHARDED}

    early_shards = [shards[n].astype(BF16) for n in EARLY]
    late_shards = [shards[n].astype(BF16) for n in LATE]
    c_idx = lax.axis_index("c").astype(jnp.int32).reshape(1)
    s_idx = (2 * lax.axis_index("x") + lax.axis_index("y")).astype(jnp.int32).reshape(1)

    loss_row, grad_x, mine, gain_grads = _local_step(x, mem, positions, loss_target, g_mix, g_mem_q, g_mem_kv,
                                                     g_ffn, g_final, {}, early_shards, late_shards, (c_idx, s_idx))
    return _reduce_and_update(given, shards, loss_row, grad_x, mine, gain_grads, c_idx)


def _reduce_halves(glist, names, c_idx, s_idx, pair_exchange, chip_exchange):
    theirs = pair_exchange(glist)
    pairs = [_pair_sum(g, t, c_idx, name="pair_sum_" + n) for n, g, t in zip(names, glist, theirs)]
    recv = chip_exchange(pairs)
    return [_chip_sum(p, r, s_idx, name="chip_sum_" + n) for n, p, r in zip(names, pairs, recv)]


def _local_step(x, mem, positions, loss_target, g_mix, g_mem_q, g_mem_kv, g_ffn, g_final, wf,
                early_shards=None, late_shards=None, place=None):
    b_dim, s_dim, d = x.shape
    t_dim = b_dim * s_dim
    n_mem = mem.shape[1]
    wf = dict(wf)

    xb = x.reshape(t_dim, d)
    tgt = loss_target.reshape(t_dim, d)
    memf = mem.reshape(b_dim * n_mem, d)
    gfin = g_final.reshape(1, d)
    pos = positions.reshape(t_dim, 1).astype(F32)

    lane = jnp.arange(LANES) % HEAD_DIM
    half = ROPE_DIM // 2
    inv_freq = ROPE_THETA ** (-jnp.arange(half, dtype=F32) / half)
    inv_lane = jnp.where(lane < ROPE_DIM, inv_freq[lane % half], 0.0).reshape(1, -1).astype(F32)
    sel_a = (lane < half).astype(F32).reshape(1, -1)
    sel_b = ((lane >= half) & (lane < ROPE_DIM)).astype(F32).reshape(1, -1)

    def rows3(t):
        return t.reshape(b_dim, s_dim, t.shape[-1])

    def rows2(t):
        return t.reshape(t_dim, t.shape[-1])

    if early_shards:
        n1, gathered = _rms_fwd(xb, g_mix, name="rms_mix", rider=("gather", early_shards))
        wf.update(zip(EARLY, gathered))
    else:
        n1 = _rms_fwd(xb, g_mix, name="rms_mix")
    proj = _mm_cs(n1, wf["w_in"], name="mm_in")
    proj3 = rows3(proj)
    cs, sn = _rope_table(pos, inv_lane, sel_a, sel_b)
    cs3, sn3 = rows3(cs), rows3(sn)
    (oa16, oa32, lse_a), _ = _dil_fwd(proj3, cs3, sn3, sel_a, sel_b)
    ob16, gathered = _sb_fwd(proj3, ("gather", late_shards) if late_shards else None)
    wf.update(zip(LATE, gathered))
    w_out, w_q, w_kv = _natural(wf["w_out"]), _natural(wf["w_q_mem"]), _natural(wf["w_kv_mem"])
    oa, ob = rows2(oa16), rows2(ob16)
    ua = _mm_cs(oa, wf["w_up_a"], name="mm_up_a")
    ub = _mm_cs(ob, wf["w_up_b"], name="mm_up_b")
    mixed, h1, hn = _gate_out_norm(proj, ua, ub, w_out, xb, g_mem_q)

    memn = _rms_fwd(memf, g_mem_kv, name="rms_mem_kv")
    qm = _mm(hn, w_q, name="mm_q_mem", out_dtype=BF16)
    kvm = _mm(memn, w_kv, name="mm_kv_mem", out_dtype=BF16)
    qm3, kvm3 = rows3(qm), kvm.reshape(b_dim, n_mem, 2 * MEM_WIDTH)
    om = rows2(_mem_fwd(qm3, kvm3))
    h2 = _mm_cs(om, wf["w_o_mem"], name="mm_o_mem", add=h1)

    n3 = _rms_fwd(h2, g_ffn, name="rms_ffn")
    gate3, up3, act3 = _ffn_up_swiglu(n3, wf["w_ffn_gate"], wf["w_ffn_up"])
    loss_row, dh3, dg_final = _down_final(act3, wf["w_ffn_down"], h2, gfin, tgt)

    grads = {}
    grads["w_ffn_down"] = _mm_ffn_down_dw(act3, dh3, name="mm_down_dw")
    dgate3, dup3 = _ffn_down_dx_swiglu(dh3, wf["w_ffn_down"], gate3, up3)
    grads["w_ffn_gate"] = _mm_ffn_up_dw(n3, dgate3, name="mm_gate_dw")
    grads["w_ffn_up"] = _mm_ffn_up_dw(n3, dup3, name="mm_up_dw")
    dn3 = _mm_ffn_up_dx(dgate3, wf["w_ffn_gate"], name="mm_gate_dx")
    dn3 = _mm_ffn_up_dx(dup3, wf["w_ffn_up"], name="mm_up_dx", add=dn3)
    dh2, dg_ffn = _rms_bwd(h2, g_ffn, dn3, dh3, name="rms_ffn_bwd")

    dom = _mm_cs_dx(dh2, wf["w_o_mem"], name="mm_o_mem_dx", out_dtype=BF16)
    grads["w_o_mem"] = _mm_cs_dw(om, dh2, name="mm_o_mem_dw")
    dqm, dkm, dvm = _mem_bwd(qm3, kvm3, rows3(dom))
    dqm = rows2(dqm)
    dkvm = jnp.concatenate([dkm, dvm], axis=-1).reshape(b_dim * n_mem, 2 * MEM_WIDTH).astype(BF16)
    grads["w_q_mem"] = _shard_major(_mm(hn, dqm, name="mm_q_mem_dw", ta=True), 0)
    dhn = _mm(dqm, w_q, name="mm_q_mem_dx", tb=True)
    grads["w_kv_mem"] = _shard_major(_mm(memn, dkvm, name="mm_kv_mem_dw", ta=True), 0)
    dmemn = _mm(dkvm, w_kv, name="mm_kv_mem_dx", tb=True)
    _, dg_mem_kv = _rms_bwd(memf, g_mem_kv, dmemn, None, name="rms_mem_kv_bwd")
    dh1, dg_mem_q = _rms_bwd(h1, g_mem_q, dhn, dh2, name="rms_mem_q_bwd")

    grads["w_out"] = _shard_major(_mm(mixed, dh1, name="mm_out_dw", ta=True), 0)
    dua, dub, dgates = _out_dx_gate_bwd(dh1, w_out, proj, ua, ub)
    doa = _mm_cs_dx(dua, wf["w_up_a"], name="mm_up_a_dx")
    grads["w_up_a"] = _mm_cs_dw(oa, dua, name="mm_up_a_dw")
    dob = _mm_cs_dx(dub, wf["w_up_b"], name="mm_up_b_dx", out_dtype=BF16)
    grads["w_up_b"] = _mm_cs_dw(ob, dub, name="mm_up_b_dw")

    att = {}

    def dil_with_pairs(glist):
        att["a"], theirs = _dil_bwd(proj3, cs3, sn3, sel_a, sel_b, rows3(doa), oa32, lse_a,
                                    ("pair", glist) if glist else None)
        return theirs

    def sb_with_chips(pairs):
        att["b"], recv = _sb_bwd(proj3, rows3(dob), ("chip", pairs) if pairs else None)
        return recv

    if place is None:
        dil_with_pairs(())
        sb_with_chips(())
    else:
        mine_late = _reduce_halves([grads[n] for n in LATE], LATE, *place, dil_with_pairs, sb_with_chips)
    dproj = jnp.concatenate([rows2(t) for t in att["a"] + att["b"]] + [dgates], axis=1)
    grads["w_in"] = _mm_cs_dw(n1, dproj, name="mm_in_dw")
    if place is None:
        dn1 = _mm_cs_dx(dproj, wf["w_in"], name="mm_in_dx")
        dx, dg_mix = _rms_bwd(xb, g_mix, dn1, dh1, name="rms_mix_bwd")
    else:
        tail = {}

        def dx_with_pairs(glist):
            tail["dn1"], theirs = _mm_cs_dx(dproj, wf["w_in"], name="mm_in_dx", rider=("pair", glist))
            return theirs

        def rms_with_chips(pairs):
            tail["dx"], tail["dg"], recv = _rms_bwd(xb, g_mix, tail["dn1"], dh1, name="rms_mix_bwd",
                                                    rider=("chip", pairs))
            return recv

        mine_early = _reduce_halves([grads[n] for n in EARLY], EARLY, *place, dx_with_pairs, rms_with_chips)
        dx, dg_mix = tail["dx"], tail["dg"]
    grad_x = dx.reshape(b_dim, s_dim, d)
    gains = (dg_mix, dg_mem_q, dg_mem_kv, dg_ffn, dg_final)
    if place is None:
        return loss_row, grad_x, grads, gains
    return loss_row, grad_x, mine_early + mine_late, gains


def _reduce_and_update(given, shards, loss_row, grad_x, mine, gain_grads, c_idx):
    d = D_MODEL
    dg_mix, dg_mem_q, dg_mem_kv, dg_ffn, dg_final = gain_grads
    small = jnp.concatenate([dg_mix, dg_mem_q, dg_mem_kv, dg_ffn, dg_final,
                             jnp.pad(loss_row, ((0, 0), (0, FLAT_COLS - LANES))), jnp.zeros((2, FLAT_COLS), F32)], axis=0)
    small_all = _gather_small(small)
    others = _swap_halves(mine)
    small_sum = _sum8(small_all)
    loss = small_sum[5, 0]

    out_g, out_d, out_m, out_v = {}, {}, {}, {}
    for n, mine_n, other_n in zip(NAMES, mine, others):
        g2, dl, nm, nv = _adamw_halves(shards[n], mine_n, other_n, given["m_" + n][0], given["v_" + n][0], c_idx,
                                       name="adamw_" + n)
        out_g[n], out_d[n], out_m[n], out_v[n] = g2[None], dl[None], nm[None], nv[None]
    gain_w = jnp.concatenate([given[n].reshape(1, d) for n in GAINS], axis=0)
    gain_m = jnp.concatenate([given["m_" + n].reshape(1, d) for n in GAINS], axis=0)
    gain_v = jnp.concatenate([given["v_" + n].reshape(1, d) for n in GAINS], axis=0)
    gain_g = small_sum[:len(GAINS)]
    gd, gm, gv = _adamw(gain_w, gain_g, gain_m, gain_v, name="adamw_gains")
    for i, n in enumerate(GAINS):
        shape = given[n].shape
        out_g[n], out_d[n] = gain_g[i].reshape(shape), gd[i].reshape(shape)
        out_m[n], out_v[n] = gm[i].reshape(shape), gv[i].reshape(shape)

    order = ["g_mix", "w_in", "w_up_a", "w_up_b", "w_out", "g_mem_q", "g_mem_kv", "w_q_mem", "w_kv_mem", "w_o_mem",
             "g_ffn", "w_ffn_gate", "w_ffn_up", "w_ffn_down", "g_final"]
    return (loss, grad_x, *[out_g[n] for n in order], *[out_d[n] for n in order],
            *[out_m[n] for n in order], *[out_v[n] for n in order])
```

```python
import jax
import jax.numpy as jnp
from jax import lax
from jax.experimental import pallas as pl
from jax.experimental.pallas import tpu as pltpu

F32 = jnp.float32
BF16 = jnp.bfloat16
MESH = pl.DeviceIdType.MESH

D_MODEL = 1024
HEAD_DIM = 64
N_HEADS = 8
ATT_WIDTH = N_HEADS * HEAD_DIM
DIL_PATTERNS = ((128, 1), (512, 4), (2048, 16))
BLOCK = 128
SB_ROWS = 1024
SB_STEP = 4
ROPE_THETA = 500000.0
ROPE_DIM = HEAD_DIM // 4
N_HEADS_MEM = 4
MEM_HEAD_DIM = 128
MEM_WIDTH = N_HEADS_MEM * MEM_HEAD_DIM
D_FF = 2816
IN_COLS = 6 * ATT_WIDTH + 2 * D_MODEL
RMS_EPS = 1e-6
ADAM_LR = 0.001
ADAM_B1 = 0.9
ADAM_B2 = 0.999
ADAM_EPS = 1e-08
ADAM_WD = 0.01
ADAM_STEP = 10

N_CHIPS = 4
LANES = 128
FLAT_COLS = 1024
VMEM_LIMIT = 56 * 1024 * 1024

PAIRS = ATT_WIDTH // LANES
COL_QA, COL_KA, COL_VA, COL_QB, COL_KB, COL_VB = (i * PAIRS for i in range(6))

MM_CAP = 1408
TOK_CAP = 2048
NN = (((1,), (0,)), ((), ()))
NT = (((1,), (1,)), ((), ()))
TN = (((0,), (0,)), ((), ()))
BNN = (((2,), (1,)), ((0,), (0,)))
BNT = (((2,), (2,)), ((0,), (0,)))
BTN = (((1,), (1,)), ((0,), (0,)))
DIL_BATCH = 8
DIL_CHUNK = 512


def _tile(dim, cap, unit=LANES):
    if dim <= cap:
        return dim
    best = None
    for t in range(unit, cap + 1, unit):
        if dim % t == 0:
            best = t
    assert best is not None, (dim, cap)
    return best


def _row_cap(cols):
    return max(256, (1 << 18) // cols)


def _params(sem):
    return pltpu.CompilerParams(dimension_semantics=sem, vmem_limit_bytes=VMEM_LIMIT)


def _mm(a, b, *, name, ta=False, tb=False, add=None, out_dtype=F32,
        tm_cap=MM_CAP, tn_cap=MM_CAP, tk_cap=MM_CAP):
    if ta:
        k_dim, m_dim = a.shape
    else:
        m_dim, k_dim = a.shape
    if tb:
        n_dim, kb = b.shape
    else:
        kb, n_dim = b.shape
    assert kb == k_dim, (a.shape, b.shape, ta, tb)
    tm, tn, tk = _tile(m_dim, tm_cap), _tile(n_dim, tn_cap), _tile(k_dim, tk_cap)
    nk = k_dim // tk
    dims = (((0 if ta else 1,), (1 if tb else 0,)), ((), ()))
    has_add = add is not None

    def body(*refs):
        if has_add:
            a_ref, b_ref, add_ref, o_ref = refs[:4]
        else:
            a_ref, b_ref, o_ref = refs[:3]
        part = lax.dot_general(a_ref[...].astype(BF16), b_ref[...].astype(BF16), dims, preferred_element_type=F32)

        def finish(r):
            if has_add:
                r = add_ref[...] + r
            o_ref[...] = r.astype(out_dtype)

        if nk == 1:
            finish(part)
            return
        acc_ref = refs[-1]
        k = pl.program_id(2)

        @pl.when(k == 0)
        def _():
            acc_ref[...] = part

        @pl.when(k > 0)
        def _():
            acc_ref[...] += part

        @pl.when(k == nk - 1)
        def _():
            finish(acc_ref[...])

    a_spec = pl.BlockSpec((tk, tm), lambda i, j, k: (k, i)) if ta else pl.BlockSpec((tm, tk), lambda i, j, k: (i, k))
    b_spec = pl.BlockSpec((tn, tk), lambda i, j, k: (j, k)) if tb else pl.BlockSpec((tk, tn), lambda i, j, k: (k, j))
    o_spec = pl.BlockSpec((tm, tn), lambda i, j, k: (i, j))
    in_specs = [a_spec, b_spec] + ([o_spec] if has_add else [])
    args = (a, b) + ((add,) if has_add else ())
    return pl.pallas_call(
        body, name=name, grid=(m_dim // tm, n_dim // tn, nk),
        in_specs=in_specs, out_specs=o_spec,
        out_shape=jax.ShapeDtypeStruct((m_dim, n_dim), out_dtype),
        scratch_shapes=[pltpu.VMEM((tm, tn), F32)] if nk > 1 else [],
        compiler_params=_params(("parallel", "parallel", "arbitrary")),
    )(*args)


def _mm_core(name, a, b, a_spec, b_spec, o_spec, out_shape, grid, dims, *, add=None, out_dtype=F32, rider=None):
    nk = grid[2]
    has_add = add is not None
    n_in = 3 if has_add else 2
    acc_shape = tuple(d for d in o_spec.block_shape if d is not None)
    extra, extra_shapes, extra_sems = _rider_parts(rider)
    n_w = len(extra)

    def body(*refs):
        a_ref, b_ref = refs[:2]
        o_ref = refs[n_in + n_w]
        step = (pl.program_id(0) * grid[1] + pl.program_id(1)) * nk + pl.program_id(2)
        begin, end = _rider_hooks(rider, refs[n_in:n_in + n_w], refs[n_in + n_w + 1:n_in + 2 * n_w + 1], refs[-2:],
                                  step, grid[0] * grid[1] * nk)
        begin()
        part = lax.dot_general(a_ref[...].astype(BF16), b_ref[...].astype(BF16), dims, preferred_element_type=F32)

        def finish(r):
            if has_add:
                r = refs[2][...] + r
            o_ref[...] = r.astype(out_dtype)

        if nk == 1:
            finish(part)
        else:
            acc_ref = refs[n_in + 2 * n_w + 1]
            k = pl.program_id(2)

            @pl.when(k == 0)
            def _():
                acc_ref[...] = part

            @pl.when(k > 0)
            def _():
                acc_ref[...] += part

            @pl.when(k == nk - 1)
            def _():
                finish(acc_ref[...])
        end()

    in_specs = [a_spec, b_spec] + ([o_spec] if has_add else []) + [ANY] * n_w
    args = (a, b) + ((add,) if has_add else ()) + extra
    res = pl.pallas_call(
        body, name=name, grid=grid, in_specs=in_specs, out_specs=[o_spec] + [ANY] * n_w,
        out_shape=[jax.ShapeDtypeStruct(out_shape, out_dtype)] + extra_shapes,
        scratch_shapes=([pltpu.VMEM(acc_shape, F32)] if nk > 1 else []) + extra_sems,
        compiler_params=_params(("arbitrary",) * 3 if n_w else ("parallel", "parallel", "arbitrary")),
    )(*args)
    return (res[0], res[1:]) if n_w else res[0]


def _mm_cs(a, w3, *, name, add=None, out_dtype=F32):
    m_dim, k_dim = a.shape
    _, _, n4 = w3.shape
    tm, tn, tk = _tile(m_dim, MM_CAP if add is not None else TOK_CAP), _tile(n4, MM_CAP), _tile(k_dim, MM_CAP)
    npb = n4 // tn
    return _mm_core(name, a, w3,
                    pl.BlockSpec((tm, tk), lambda i, j, k: (i, k)),
                    pl.BlockSpec((None, tk, tn), lambda i, j, k: (j // npb, k, j % npb)),
                    pl.BlockSpec((tm, tn), lambda i, j, k: (i, j)),
                    (m_dim, N_CHIPS * n4), (m_dim // tm, N_CHIPS * npb, k_dim // tk), NN, add=add, out_dtype=out_dtype)


def _mm_cs_dx(dy, w3, *, name, out_dtype=F32, rider=None):
    m_dim, _ = dy.shape
    _, k_dim, n4 = w3.shape
    tm, tkw, tn = _tile(m_dim, MM_CAP), _tile(k_dim, MM_CAP), _tile(n4, MM_CAP)
    npb = n4 // tn
    return _mm_core(name, dy, w3,
                    pl.BlockSpec((tm, tn), lambda i, j, k: (i, k)),
                    pl.BlockSpec((None, tkw, tn), lambda i, j, k: (k // npb, j, k % npb)),
                    pl.BlockSpec((tm, tkw), lambda i, j, k: (i, j)),
                    (m_dim, k_dim), (m_dim // tm, k_dim // tkw, N_CHIPS * npb), NT, out_dtype=out_dtype, rider=rider)


def _mm_cs_dw(a, dy, *, name):
    m_dim, k_dim = a.shape
    n4 = dy.shape[1] // N_CHIPS
    tmk, tn, tk = _tile(k_dim, MM_CAP), _tile(n4, MM_CAP), _tile(m_dim, TOK_CAP)
    npb = n4 // tn
    return _mm_core(name, a, dy,
                    pl.BlockSpec((tk, tmk), lambda i, j, k: (k, i)),
                    pl.BlockSpec((tk, tn), lambda i, j, k: (k, j)),
                    pl.BlockSpec((None, tmk, tn), lambda i, j, k: (j // npb, i, j % npb)),
                    (N_CHIPS, k_dim, n4), (k_dim // tmk, N_CHIPS * npb, m_dim // tk), TN)


def _mm_ffn_rows(a3, w3, *, name, add=None):
    _, t_dim, f4 = a3.shape
    _, _, d = w3.shape
    tm = _tile(t_dim, MM_CAP)
    return _mm_core(name, a3, w3,
                    pl.BlockSpec((None, tm, f4), lambda i, j, k: (k, i, 0)),
                    pl.BlockSpec((None, f4, d), lambda i, j, k: (k, 0, 0)),
                    pl.BlockSpec((tm, d), lambda i, j, k: (i, 0)),
                    (t_dim, d), (t_dim // tm, 1, N_CHIPS), NN, add=add)


def _mm_ffn_down_dw(act3, dh, *, name):
    _, t_dim, f4 = act3.shape
    d = dh.shape[1]
    tk = _tile(t_dim, TOK_CAP)
    return _mm_core(name, act3, dh,
                    pl.BlockSpec((None, tk, f4), lambda i, j, k: (i, k, 0)),
                    pl.BlockSpec((tk, d), lambda i, j, k: (k, 0)),
                    pl.BlockSpec((None, f4, d), lambda i, j, k: (i, 0, 0)),
                    (N_CHIPS, f4, d), (N_CHIPS, 1, t_dim // tk), TN)


def _rms_fwd(x, g, *, name, tt=512, rider=None):
    t_dim, d = x.shape
    tt = _tile(t_dim, tt, 8)
    extra, extra_shapes, extra_sems = _rider_parts(rider)
    n_w = len(extra)

    def body(*refs):
        x_ref, g_ref, o_ref = refs[0], refs[1], refs[2 + n_w]
        begin, end = _rider_hooks(rider, refs[2:2 + n_w], refs[3 + n_w:3 + 2 * n_w], refs[-2:], pl.program_id(0),
                                  t_dim // tt)
        begin()
        xv = x_ref[...]
        r = lax.rsqrt(jnp.mean(xv * xv, axis=-1, keepdims=True) + RMS_EPS)
        o_ref[...] = ((xv * r) * g_ref[...]).astype(o_ref.dtype)
        end()

    res = pl.pallas_call(
        body, name=name, grid=(t_dim // tt,),
        in_specs=[pl.BlockSpec((tt, d), lambda i: (i, 0)), pl.BlockSpec((1, d), lambda i: (0, 0))] + [ANY] * n_w,
        out_specs=[pl.BlockSpec((tt, d), lambda i: (i, 0))] + [ANY] * n_w,
        out_shape=[jax.ShapeDtypeStruct((t_dim, d), BF16)] + extra_shapes,
        scratch_shapes=extra_sems,
        compiler_params=_params(("arbitrary",) if n_w else ("parallel",)),
    )(x, g, *extra)
    return (res[0], res[1:]) if n_w else res[0]


def _rms_bwd(x, g, dy, add, *, name, tt=512, rider=None):
    t_dim, d = x.shape
    tt = _tile(t_dim, tt, 8)
    has_add = add is not None
    n_in = 4 if has_add else 3
    extra, extra_shapes, extra_sems = _rider_parts(rider)
    n_w = len(extra)

    def body(*refs):
        x_ref, g_ref, dy_ref = refs[:3]
        add_ref = refs[3] if has_add else None
        dx_ref, dg_ref = refs[n_in + n_w:n_in + n_w + 2]
        begin, end = _rider_hooks(rider, refs[n_in:n_in + n_w], refs[n_in + n_w + 2:n_in + 2 * n_w + 2], refs[-2:],
                                  pl.program_id(0), t_dim // tt)
        begin()
        xv = x_ref[...]
        dyv = dy_ref[...].astype(F32)
        r = lax.rsqrt(jnp.mean(xv * xv, axis=-1, keepdims=True) + RMS_EPS)
        xh = xv * r
        u = dyv * g_ref[...]
        dx = r * (u - xh * jnp.mean(u * xh, axis=-1, keepdims=True))
        if has_add:
            dx = add_ref[...] + dx
        dx_ref[...] = dx

        @pl.when(pl.program_id(0) == 0)
        def _():
            dg_ref[...] = jnp.zeros_like(dg_ref)

        dg_ref[...] += jnp.sum(dyv * xh, axis=0, keepdims=True)
        end()

    row = pl.BlockSpec((tt, d), lambda i: (i, 0))
    vec = pl.BlockSpec((1, d), lambda i: (0, 0))
    in_specs = [row, vec, row] + ([row] if has_add else []) + [ANY] * n_w
    args = (x, g, dy) + ((add,) if has_add else ()) + extra
    res = pl.pallas_call(
        body, name=name, grid=(t_dim // tt,),
        in_specs=in_specs, out_specs=[row, vec] + [ANY] * n_w,
        out_shape=[jax.ShapeDtypeStruct((t_dim, d), F32), jax.ShapeDtypeStruct((1, d), F32)] + extra_shapes,
        scratch_shapes=extra_sems,
        compiler_params=_params(("arbitrary",)),
    )(*args)
    return (res[0], res[1], res[2:]) if n_w else (res[0], res[1])


def _down_final(act3, wd3, h, g, target, *, tt=1024):
    n_s, t_dim, f4 = act3.shape
    d = h.shape[1]
    n_steps = t_dim // tt

    def body(a_ref, w_ref, h_ref, g_ref, t_ref, loss_ref, dh_ref, dg_ref, acc_ref, sq_ref):
        i, k = pl.program_id(0), pl.program_id(1)
        part = jnp.dot(a_ref[...], w_ref[...], preferred_element_type=F32)

        @pl.when(k == 0)
        def _():
            acc_ref[...] = part

        @pl.when(k > 0)
        def _():
            acc_ref[...] += part

        @pl.when(jnp.logical_and(i == 0, k == 0))
        def _():
            dg_ref[...] = jnp.zeros_like(dg_ref)
            sq_ref[...] = jnp.zeros_like(sq_ref)

        @pl.when(k == n_s - 1)
        def _():
            xv = h_ref[...] + acc_ref[...]
            gv = g_ref[...]
            r = lax.rsqrt(jnp.mean(xv * xv, axis=-1, keepdims=True) + RMS_EPS)
            xh = xv * r
            err = xh * gv - t_ref[...]
            dyv = err * (1.0 / d)
            u = dyv * gv
            dh_ref[...] = r * (u - xh * jnp.mean(u * xh, axis=-1, keepdims=True))
            dg_ref[...] += jnp.sum(dyv * xh, axis=0, keepdims=True)
            sq_ref[...] += jnp.sum(err * err, axis=0, keepdims=True)

        @pl.when(jnp.logical_and(i == n_steps - 1, k == n_s - 1))
        def _():
            total = jnp.sum(sq_ref[...], axis=-1, keepdims=True) * (0.5 / d)
            loss_ref[...] = jnp.broadcast_to(total, loss_ref.shape)

    row = pl.BlockSpec((tt, d), lambda i, k: (i, 0))
    vec = pl.BlockSpec((1, d), lambda i, k: (0, 0))
    return pl.pallas_call(
        body, name="down_final_loss", grid=(n_steps, n_s),
        in_specs=[pl.BlockSpec((None, tt, f4), lambda i, k: (k, i, 0)),
                  pl.BlockSpec((None, f4, d), lambda i, k: (k, 0, 0)), row, vec, row],
        out_specs=[pl.BlockSpec((1, LANES), lambda i, k: (0, 0)), row, vec],
        out_shape=[jax.ShapeDtypeStruct((1, LANES), F32), jax.ShapeDtypeStruct((t_dim, d), F32),
                   jax.ShapeDtypeStruct((1, d), F32)],
        scratch_shapes=[pltpu.VMEM((tt, d), F32), pltpu.VMEM((1, d), F32)],
        compiler_params=_params(("arbitrary", "arbitrary")),
    )(act3, wd3, h, g, target)


def _rope_table(pos, inv_lane, sel_a, sel_b, *, tt=512):
    t_dim = pos.shape[0]

    def body(p_ref, f_ref, a_ref, b_ref, c_ref, s_ref):
        ang = p_ref[...] * f_ref[...]
        on = (a_ref[...] + b_ref[...]) > 0.0
        c_ref[...] = jnp.where(on, jnp.cos(ang), 1.0)
        s_ref[...] = jnp.where(on, jnp.sin(ang), 0.0)

    vec = pl.BlockSpec((1, LANES), lambda i: (0, 0))
    row = pl.BlockSpec((tt, LANES), lambda i: (i, 0))
    shp = jax.ShapeDtypeStruct((t_dim, LANES), F32)
    return pl.pallas_call(
        body, name="rope_table", grid=(t_dim // tt,),
        in_specs=[pl.BlockSpec((tt, 1), lambda i: (i, 0)), vec, vec, vec],
        out_specs=[row, row], out_shape=[shp, shp],
        compiler_params=_params(("parallel",)),
    )(pos, inv_lane, sel_a, sel_b)


def _rotate(xv, cs, sn, sa, sb):
    half = ROPE_DIM // 2
    up = pltpu.roll(xv, LANES - half, 1)
    dn = pltpu.roll(xv, half, 1)
    return xv * cs + (dn * sb - up * sa) * sn


def _head_masks():
    h1 = lax.broadcasted_iota(jnp.int32, (1, LANES), 1) < HEAD_DIM
    return h1, jnp.logical_not(h1)


def _split_heads(xv, h1, h2):
    return jnp.where(h1, xv, 0.0).astype(BF16), jnp.where(h2, xv, 0.0).astype(BF16)


def _tri_masks():
    r = lax.broadcasted_iota(jnp.int32, (BLOCK, BLOCK), 0)
    c = lax.broadcasted_iota(jnp.int32, (BLOCK, BLOCK), 1)
    return c <= r, r <= c


def _stream_rows(start, dil):
    if dil == 1:
        return pl.ds(pl.multiple_of(start, BLOCK), BLOCK)
    return pl.ds(start, BLOCK, stride=dil)


def _dil_tile(idx, dil, nb):
    r = idx // nb
    n = idx % nb
    return (_stream_rows(r + dil * BLOCK * n, dil), _stream_rows(r + dil * BLOCK * jnp.maximum(n - 1, 0), dil),
            n > 0)


def _dil_specs(b_dim, s_dim):
    def col(c0):
        return pl.BlockSpec((None, s_dim, LANES),lambda b, h: (b, 0, c0 + h))
    tab = pl.BlockSpec((None, s_dim, LANES),lambda b, h: (b, 0, 0))
    vec = pl.BlockSpec((1, LANES), lambda b, h: (0, 0))
    return col, tab, vec


def _dil_fwd(proj3, cs3, sn3, sel_a, sel_b, rider=None):
    b_dim, s_dim, _ = proj3.shape
    scale = HEAD_DIM ** -0.5
    n_pat = len(DIL_PATTERNS)
    extra, extra_shapes, extra_sems = _rider_parts(rider)
    n_w = len(extra)
    n_steps = b_dim * PAIRS

    def body(*refs):
        q_ref, k_ref, v_ref, cs_ref, sn_ref, sa_ref, sb_ref = refs[:7]
        o16_ref, o32_ref, l_ref = refs[7 + n_w:10 + n_w]
        qr, kr = refs[10 + 2 * n_w:12 + 2 * n_w]
        per_pattern = refs[12 + 2 * n_w:12 + 2 * n_w + 2 * n_pat]
        og, lg = per_pattern[:n_pat], per_pattern[n_pat:]
        step = pl.program_id(0) * PAIRS + pl.program_id(1)
        begin, end = _rider_hooks(rider, refs[7:7 + n_w], refs[10 + n_w:10 + 2 * n_w], refs[-2:], step, n_steps)
        begin()
        h1, h2 = _head_masks()
        cur_ok, prev_ok = _tri_masks()
        sa, sb = sa_ref[...], sb_ref[...]

        def prep(j, _):
            rows = pl.ds(pl.multiple_of(j * DIL_CHUNK, DIL_CHUNK), DIL_CHUNK)
            cs, sn = cs_ref[rows, :], sn_ref[rows, :]
            qr[rows, :] = _rotate(q_ref[rows, :], cs, sn, sa, sb) * scale
            kr[rows, :] = _rotate(k_ref[rows, :], cs, sn, sa, sb)
            return 0

        lax.fori_loop(0, s_dim // DIL_CHUNK, prep, 0)

        for g, (_, dil) in enumerate(DIL_PATTERNS):
            nb = s_dim // dil // BLOCK

            def some(bi, _, g=g, dil=dil, nb=nb):
                tiles = [_dil_tile(bi * DIL_BATCH + t, dil, nb) for t in range(DIL_BATCH)]
                rows = [t[0] for t in tiles]
                q1, q2 = _split_heads(jnp.stack([qr[rw, :] for rw in rows]), h1, h2)
                kc = jnp.stack([kr[rw, :] for rw in rows]).astype(BF16)
                vc1, vc2 = _split_heads(jnp.stack([v_ref[rw, :] for rw in rows]), h1, h2)
                if nb > 1:
                    kp = jnp.stack([kr[t[1], :] for t in tiles]).astype(BF16)
                    vp1, vp2 = _split_heads(jnp.stack([v_ref[t[1], :] for t in tiles]), h1, h2)
                    p_ok = jnp.stack([jnp.logical_and(prev_ok, t[2]) for t in tiles])

                def head(qh, vch, vph):
                    sc = jnp.where(cur_ok, lax.dot_general(qh, kc, BNT, preferred_element_type=F32), -jnp.inf)
                    m = jnp.max(sc, axis=-1, keepdims=True)
                    if nb > 1:
                        sp = jnp.where(p_ok, lax.dot_general(qh, kp, BNT, preferred_element_type=F32), -jnp.inf)
                        m = jnp.maximum(m, jnp.max(sp, axis=-1, keepdims=True))
                    pc = jnp.exp(sc - m)
                    den = jnp.sum(pc, axis=-1, keepdims=True)
                    acc = lax.dot_general(pc.astype(BF16), vch, BNN, preferred_element_type=F32)
                    if nb > 1:
                        pp = jnp.exp(sp - m)
                        den = den + jnp.sum(pp, axis=-1, keepdims=True)
                        acc = acc + lax.dot_general(pp.astype(BF16), vph, BNN, preferred_element_type=F32)
                    return acc / den, m + jnp.log(den)

                o1, l1 = head(q1, vc1, vp1 if nb > 1 else None)
                o2, l2 = head(q2, vc2, vp2 if nb > 1 else None)
                o, l = o1 + o2, jnp.where(h1, l1, l2)
                for t, rw in enumerate(rows):
                    og[g][rw, :] = o[t]
                    lg[g][rw, :] = l[t]
                return 0

            lax.fori_loop(0, dil * nb // DIL_BATCH, some, 0)

        def comb(j, _):
            rows = pl.ds(pl.multiple_of(j * DIL_CHUNK, DIL_CHUNK), DIL_CHUNK)
            ls = [lg[g][rows, :] for g in range(n_pat)]
            m = jnp.maximum(jnp.maximum(ls[0], ls[1]), ls[2])
            es = [jnp.exp(l - m) for l in ls]
            den = es[0] + es[1] + es[2]
            o = (es[0] * og[0][rows, :] + es[1] * og[1][rows, :] + es[2] * og[2][rows, :]) / den
            o16_ref[rows, :] = o.astype(BF16)
            o32_ref[rows, :] = o
            l_ref[rows, :] = m + jnp.log(den)
            return 0

        lax.fori_loop(0, s_dim // DIL_CHUNK, comb, 0)
        end()

    col, tab, vec = _dil_specs(b_dim, s_dim)
    out = pl.BlockSpec((None, s_dim, LANES),lambda b, h: (b, 0, h))
    shp = (b_dim, s_dim, ATT_WIDTH)
    res = pl.pallas_call(
        body, name="dil_fwd", grid=(b_dim, PAIRS),
        in_specs=[col(COL_QA), col(COL_KA), col(COL_VA), tab, tab, vec, vec] + [ANY] * n_w,
        out_specs=[out, out, out] + [ANY] * n_w,
        out_shape=[jax.ShapeDtypeStruct(shp, BF16), jax.ShapeDtypeStruct(shp, F32), jax.ShapeDtypeStruct(shp, F32)]
        + extra_shapes,
        scratch_shapes=[pltpu.VMEM((s_dim, LANES), F32)] * (2 + 2 * n_pat) + extra_sems,
        compiler_params=_params(("arbitrary", "arbitrary")),
    )(proj3, proj3, proj3, cs3, sn3, sel_a, sel_b, *extra)
    return res[:3], res[3:]


def _dil_bwd(proj3, cs3, sn3, sel_a, sel_b, do3, o3, lse3, rider=None):
    b_dim, s_dim, _ = proj3.shape
    scale = HEAD_DIM ** -0.5
    extra, extra_shapes, extra_sems = _rider_parts(rider)
    n_w = len(extra)

    def body(*refs):
        q_ref, k_ref, v_ref, cs_ref, sn_ref, sa_ref, sb_ref, do_ref, o_ref, l_ref = refs[:10]
        dq_ref, dk_ref, dv_ref = refs[10 + n_w:13 + n_w]
        qr, kr, dqa, dka, dva = refs[13 + 2 * n_w:18 + 2 * n_w]
        step = pl.program_id(0) * PAIRS + pl.program_id(1)
        begin, end = _rider_hooks(rider, refs[10:10 + n_w], refs[13 + n_w:13 + 2 * n_w], refs[-2:], step,
                                  b_dim * PAIRS)
        begin()
        h1, h2 = _head_masks()
        cur_ok, prev_ok = _tri_masks()
        sa, sb = sa_ref[...], sb_ref[...]

        def prep(j, _):
            rows = pl.ds(pl.multiple_of(j * DIL_CHUNK, DIL_CHUNK), DIL_CHUNK)
            cs, sn = cs_ref[rows, :], sn_ref[rows, :]
            qr[rows, :] = _rotate(q_ref[rows, :], cs, sn, sa, sb) * scale
            kr[rows, :] = _rotate(k_ref[rows, :], cs, sn, sa, sb)
            zero = jnp.zeros((DIL_CHUNK, LANES), F32)
            dqa[rows, :] = zero
            dka[rows, :] = zero
            dva[rows, :] = zero
            return 0

        lax.fori_loop(0, s_dim // DIL_CHUNK, prep, 0)

        for _, dil in DIL_PATTERNS:
            nb = s_dim // dil // BLOCK

            def some(bi, _, dil=dil, nb=nb):
                tiles = [_dil_tile(bi * DIL_BATCH + t, dil, nb) for t in range(DIL_BATCH)]
                rows = [t[0] for t in tiles]
                q1, q2 = _split_heads(jnp.stack([qr[rw, :] for rw in rows]), h1, h2)
                dof = jnp.stack([do_ref[rw, :] for rw in rows])
                do1, do2 = _split_heads(dof, h1, h2)
                prod = dof * jnp.stack([o_ref[rw, :] for rw in rows])
                delta1 = jnp.sum(jnp.where(h1, prod, 0.0), axis=-1, keepdims=True)
                delta2 = jnp.sum(jnp.where(h2, prod, 0.0), axis=-1, keepdims=True)
                lt = jnp.stack([l_ref[rw, :] for rw in rows])
                lse1 = jnp.max(jnp.where(h1, lt, -jnp.inf), axis=-1, keepdims=True)
                lse2 = jnp.max(jnp.where(h2, lt, -jnp.inf), axis=-1, keepdims=True)

                def side(krows, ok):
                    kf = jnp.stack([kr[kw, :] for kw in krows])
                    k16 = kf.astype(BF16)
                    k1, k2 = _split_heads(kf, h1, h2)
                    v16 = jnp.stack([v_ref[kw, :] for kw in krows]).astype(BF16)

                    def head(qh, doh, lse, delta):
                        sc = lax.dot_general(qh, k16, BNT, preferred_element_type=F32)
                        p = jnp.where(ok, jnp.exp(sc - lse), 0.0)
                        dp = lax.dot_general(doh, v16, BNT, preferred_element_type=F32)
                        return p.astype(BF16), (p * (dp - delta)).astype(BF16)

                    p1, ds1 = head(q1, do1, lse1, delta1)
                    p2, ds2 = head(q2, do2, lse2, delta2)
                    dv = (lax.dot_general(p1, do1, BTN, preferred_element_type=F32)
                          + lax.dot_general(p2, do2, BTN, preferred_element_type=F32))
                    dk = (lax.dot_general(ds1, q1, BTN, preferred_element_type=F32)
                          + lax.dot_general(ds2, q2, BTN, preferred_element_type=F32))
                    for t, kw in enumerate(krows):
                        dva[kw, :] += dv[t]
                        dka[kw, :] += dk[t]
                    return (lax.dot_general(ds1, k1, BNN, preferred_element_type=F32)
                            + lax.dot_general(ds2, k2, BNN, preferred_element_type=F32))

                dq = side(rows, cur_ok)
                if nb > 1:
                    dq = dq + side([t[1] for t in tiles], jnp.stack([jnp.logical_and(prev_ok, t[2]) for t in tiles]))
                for t, rw in enumerate(rows):
                    dqa[rw, :] += dq[t] * scale
                return 0

            lax.fori_loop(0, dil * nb // DIL_BATCH, some, 0)

        def finish(j, _):
            rows = pl.ds(pl.multiple_of(j * DIL_CHUNK, DIL_CHUNK), DIL_CHUNK)
            cs, sn = cs_ref[rows, :], -sn_ref[rows, :]
            dq_ref[rows, :] = _rotate(dqa[rows, :], cs, sn, sa, sb).astype(BF16)
            dk_ref[rows, :] = _rotate(dka[rows, :], cs, sn, sa, sb).astype(BF16)
            dv_ref[rows, :] = dva[rows, :].astype(BF16)
            return 0

        lax.fori_loop(0, s_dim // DIL_CHUNK, finish, 0)
        end()

    col, tab, vec = _dil_specs(b_dim, s_dim)
    out = pl.BlockSpec((None, s_dim, LANES),lambda b, h: (b, 0, h))
    shp = jax.ShapeDtypeStruct((b_dim, s_dim, ATT_WIDTH), BF16)
    acc = pltpu.VMEM((s_dim, LANES), F32)
    res = pl.pallas_call(
        body, name="dil_bwd", grid=(b_dim, PAIRS),
        in_specs=[col(COL_QA), col(COL_KA), col(COL_VA), tab, tab, vec, vec, out, out, out] + [ANY] * n_w,
        out_specs=[out, out, out] + [ANY] * n_w, out_shape=[shp, shp, shp] + extra_shapes,
        scratch_shapes=[acc, acc, acc, acc, acc] + extra_sems,
        compiler_params=_params(("arbitrary", "arbitrary")),
    )(proj3, proj3, proj3, cs3, sn3, sel_a, sel_b, do3, o3, lse3, *extra)
    return res[:3], res[3:]


def _split_dot(x, tri):
    hi = x.astype(BF16)
    lo = (x - hi.astype(F32)).astype(BF16)
    return jnp.dot(hi, tri, preferred_element_type=F32) + jnp.dot(lo, tri, preferred_element_type=F32)


def _log_sigmoid(z):
    return jnp.minimum(z, 0.0) - jnp.log(1.0 + jnp.exp(-jnp.abs(z)))


def _sb_scores(qh, k16, valid):
    z = lax.dot_general(qh, k16, NT, preferred_element_type=F32)
    ls = _log_sigmoid(z)
    l1m = ls - z
    return ls, (l1m if valid is None else jnp.where(valid, l1m, 0.0))


def _sb_consts():
    r = lax.broadcasted_iota(jnp.int32, (BLOCK, BLOCK), 0)
    c = lax.broadcasted_iota(jnp.int32, (BLOCK, BLOCK), 1)
    after = (r > c).astype(BF16)
    before = (r < c).astype(BF16)
    qrow = lax.broadcasted_iota(jnp.int32, (SB_ROWS, BLOCK), 0)
    kcol = lax.broadcasted_iota(jnp.int32, (SB_ROWS, BLOCK), 1)
    return after, before, qrow, kcol


def _below(whole, lo, delta):
    if lo == 0:
        return whole + delta
    return whole + jnp.concatenate([jnp.zeros((lo,) + delta.shape[1:], delta.dtype), delta], axis=0)


def _pairs_loop(n_blocks, step, carry):
    def several(i, c):
        for j in range(SB_STEP):
            c = step(SB_STEP * i + j, c)
        return c

    return lax.fori_loop(0, n_blocks // SB_STEP, several, carry)


def _sb_fwd(proj3, rider=None):
    b_dim, s_dim, _ = proj3.shape
    scale = HEAD_DIM ** -0.5
    per = SB_ROWS // BLOCK
    extra, extra_shapes, extra_sems = _rider_parts(rider)
    n_w = len(extra)

    def body(*refs):
        q_ref, k_ref, v_ref = refs[:3]
        o_ref = refs[3 + n_w]
        step = pl.program_id(0) * PAIRS + pl.program_id(1)
        begin, end = _rider_hooks(rider, refs[3:3 + n_w], refs[4 + n_w:4 + 2 * n_w], refs[-2:], step, b_dim * PAIRS)
        begin()
        h1, h2 = _head_masks()
        after, _, qrow, kcol = _sb_consts()

        def qloop(qi, _):
            rows = pl.ds(pl.multiple_of(qi * SB_ROWS, SB_ROWS), SB_ROWS)
            q1, q2 = _split_heads(q_ref[rows, :] * scale, h1, h2)
            first = qi * per

            def block(kb, carry, lo):
                acc, run1, run2 = carry
                krows = pl.ds(pl.multiple_of(kb * BLOCK, BLOCK), BLOCK)
                k16 = k_ref[krows, :].astype(BF16)
                v1, v2 = _split_heads(v_ref[krows, :], h1, h2)
                valid = None if lo is None else kcol[:SB_ROWS - lo] < qrow[:SB_ROWS - lo]
                lo = lo or 0

                def head(qh, vh, run):
                    ls, l1m = _sb_scores(qh[lo:], k16, valid)
                    a = jnp.exp(ls + _split_dot(l1m, after) + run[lo:])
                    if valid is not None:
                        a = jnp.where(valid, a, 0.0)
                    return (jnp.dot(a.astype(BF16), vh, preferred_element_type=F32),
                            _below(run, lo, jnp.sum(l1m, axis=-1, keepdims=True)))

                o1, run1 = head(q1, v1, run1)
                o2, run2 = head(q2, v2, run2)
                return _below(acc, lo, o1 + o2), run1, run2

            zcol = jnp.zeros((SB_ROWS, 1), F32)
            carry = (jnp.zeros((SB_ROWS, LANES), F32), zcol, zcol)
            for kl in reversed(range(per)):
                carry = block(first + kl, carry, kl * BLOCK)
            acc, _, _ = _pairs_loop(first, lambda i, c: block(first - 1 - i, c, None), carry)
            o_ref[rows, :] = acc.astype(BF16)
            return 0

        lax.fori_loop(0, s_dim // SB_ROWS, qloop, 0)
        end()

    def col(c0):
        return pl.BlockSpec((None, s_dim, LANES),lambda b, h: (b, 0, c0 + h))

    res = pl.pallas_call(
        body, name="sb_fwd", grid=(b_dim, PAIRS),
        in_specs=[col(COL_QB), col(COL_KB), col(COL_VB)] + [ANY] * n_w, out_specs=[col(0)] + [ANY] * n_w,
        out_shape=[jax.ShapeDtypeStruct((b_dim, s_dim, ATT_WIDTH), BF16)] + extra_shapes,
        scratch_shapes=extra_sems,
        compiler_params=_params(("arbitrary", "arbitrary")),
    )(proj3, proj3, proj3, *extra)
    return res[0], res[1:]


def _sb_bwd(proj3, do3, rider=None):
    b_dim, s_dim, _ = proj3.shape
    scale = HEAD_DIM ** -0.5
    per = SB_ROWS // BLOCK
    nkb_max = s_dim // BLOCK
    extra, extra_shapes, extra_sems = _rider_parts(rider)
    n_w = len(extra)

    def body(*refs):
        q_ref, k_ref, v_ref, do_ref = refs[:4]
        dq_ref, dk_ref, dv_ref = refs[4 + n_w:7 + n_w]
        dka, dva, e_ref, sg_ref = refs[7 + 2 * n_w:11 + 2 * n_w]
        step = pl.program_id(0) * PAIRS + pl.program_id(1)
        begin, end = _rider_hooks(rider, refs[4:4 + n_w], refs[7 + n_w:7 + 2 * n_w], refs[-2:], step, b_dim * PAIRS)
        begin()
        h1, h2 = _head_masks()
        after, before, qrow, kcol = _sb_consts()
        dka[...] = jnp.zeros_like(dka)
        dva[...] = jnp.zeros_like(dva)

        def qloop(qi, _):
            rows = pl.ds(pl.multiple_of(qi * SB_ROWS, SB_ROWS), SB_ROWS)
            q1, q2 = _split_heads(q_ref[rows, :] * scale, h1, h2)
            do1, do2 = _split_heads(do_ref[rows, :].astype(F32), h1, h2)
            first = qi * per

            def pass1(kb, carry, lo):
                run1, run2 = carry
                krows = pl.ds(pl.multiple_of(kb * BLOCK, BLOCK), BLOCK)
                k16 = k_ref[krows, :].astype(BF16)
                v16 = v_ref[krows, :].astype(BF16)
                valid = None if lo is None else kcol[:SB_ROWS - lo] < qrow[:SB_ROWS - lo]
                lo = lo or 0
                part = pl.ds(lo, SB_ROWS - lo)

                def head(h, qh, doh, run):
                    ls, l1m = _sb_scores(qh[lo:], k16, valid)
                    a = jnp.exp(ls + _split_dot(l1m, after) + run[lo:])
                    if valid is not None:
                        a = jnp.where(valid, a, 0.0)
                    da = lax.dot_general(doh[lo:], v16, NT, preferred_element_type=F32)
                    e_ref[h, kb, part, :] = a * da
                    sg_ref[h, kb, part, :] = jnp.exp(ls)
                    return a.astype(BF16), _below(run, lo, jnp.sum(l1m, axis=-1, keepdims=True))

                a1, run1 = head(0, q1, do1, run1)
                a2, run2 = head(1, q2, do2, run2)
                dva[krows, :] += (lax.dot_general(a1, do1[lo:], TN, preferred_element_type=F32)
                                  + lax.dot_general(a2, do2[lo:], TN, preferred_element_type=F32))
                return run1, run2

            zcol = jnp.zeros((SB_ROWS, 1), F32)
            carry = (zcol, zcol)
            for kl in reversed(range(per)):
                carry = pass1(first + kl, carry, kl * BLOCK)
            _pairs_loop(first, lambda i, c: pass1(first - 1 - i, c, None), carry)

            def pass2(kb, carry, lo):
                dq, pre1, pre2 = carry
                krows = pl.ds(pl.multiple_of(kb * BLOCK, BLOCK), BLOCK)
                k1, k2 = _split_heads(k_ref[krows, :], h1, h2)
                valid = None if lo is None else kcol[:SB_ROWS - lo] < qrow[:SB_ROWS - lo]
                lo = lo or 0
                part = pl.ds(lo, SB_ROWS - lo)

                def head(h, pre):
                    ev = e_ref[h, kb, part, :]
                    sg = sg_ref[h, kb, part, :]
                    dz = ev * (1.0 - sg) - (_split_dot(ev, before) + pre[lo:]) * sg
                    if valid is not None:
                        dz = jnp.where(valid, dz, 0.0)
                    return dz.astype(BF16), _below(pre, lo, jnp.sum(ev, axis=-1, keepdims=True))

                dz1, pre1 = head(0, pre1)
                dz2, pre2 = head(1, pre2)
                dka[krows, :] += (lax.dot_general(dz1, q1[lo:], TN, preferred_element_type=F32)
                                  + lax.dot_general(dz2, q2[lo:], TN, preferred_element_type=F32))
                dq = _below(dq, lo, jnp.dot(dz1, k1, preferred_element_type=F32)
                            + jnp.dot(dz2, k2, preferred_element_type=F32))
                return dq, pre1, pre2

            carry = _pairs_loop(first, lambda i, c: pass2(i, c, None), (jnp.zeros((SB_ROWS, LANES), F32), zcol, zcol))
            for kl in range(per):
                carry = pass2(first + kl, carry, kl * BLOCK)
            dq = carry[0]
            dq_ref[rows, :] = (dq * scale).astype(BF16)
            return 0

        lax.fori_loop(0, s_dim // SB_ROWS, qloop, 0)
        dk_ref[...] = dka[...].astype(BF16)
        dv_ref[...] = dva[...].astype(BF16)
        end()

    def col(c0):
        return pl.BlockSpec((None, s_dim, LANES),lambda b, h: (b, 0, c0 + h))

    shp = jax.ShapeDtypeStruct((b_dim, s_dim, ATT_WIDTH), BF16)
    acc = pltpu.VMEM((s_dim, LANES), F32)
    strip = pltpu.VMEM((2, nkb_max, SB_ROWS, BLOCK), F32)
    res = pl.pallas_call(
        body, name="sb_bwd", grid=(b_dim, PAIRS),
        in_specs=[col(COL_QB), col(COL_KB), col(COL_VB), col(0)] + [ANY] * n_w,
        out_specs=[col(0), col(0), col(0)] + [ANY] * n_w,
        out_shape=[shp, shp, shp] + extra_shapes,
        scratch_shapes=[acc, acc, strip, strip] + extra_sems,
        compiler_params=_params(("arbitrary", "arbitrary")),
    )(proj3, proj3, proj3, do3, *extra)
    return res[:3], res[3:]


def _sigmoid(x):
    return 1.0 / (1.0 + jnp.exp(-x))


def _gate_out_norm(proj, ua, ub, w_out, x, g, *, tt=512):
    t_dim, d = ua.shape

    def body(ga_ref, gb_ref, ua_ref, ub_ref, w_ref, x_ref, g_ref, m_ref, h_ref, n_ref):
        mixed = (_sigmoid(ga_ref[...]) * ua_ref[...] + _sigmoid(gb_ref[...]) * ub_ref[...]).astype(BF16)
        m_ref[...] = mixed
        hv = x_ref[...] + jnp.dot(mixed, w_ref[...], preferred_element_type=F32)
        h_ref[...] = hv
        r = lax.rsqrt(jnp.mean(hv * hv, axis=-1, keepdims=True) + RMS_EPS)
        n_ref[...] = ((hv * r) * g_ref[...]).astype(BF16)

    row = pl.BlockSpec((tt, d), lambda i: (i, 0))
    return pl.pallas_call(
        body, name="gate_out_norm", grid=(t_dim // tt,),
        in_specs=[pl.BlockSpec((tt, d), lambda i: (i, 3)), pl.BlockSpec((tt, d), lambda i: (i, 4)), row, row,
                  pl.BlockSpec((d, d), lambda i: (0, 0)), row, pl.BlockSpec((1, d), lambda i: (0, 0))],
        out_specs=[row, row, row],
        out_shape=[jax.ShapeDtypeStruct((t_dim, d), BF16), jax.ShapeDtypeStruct((t_dim, d), F32),
                   jax.ShapeDtypeStruct((t_dim, d), BF16)],
        compiler_params=_params(("parallel",)),
    )(proj, proj, ua, ub, w_out, x, g)


def _out_dx_gate_bwd(dh, w_out, proj, ua, ub, *, tt=512):
    t_dim, d = ua.shape

    def body(dh_ref, w_ref, ga_ref, gb_ref, ua_ref, ub_ref, dua_ref, dub_ref, dg_ref):
        dm = lax.dot_general(dh_ref[...].astype(BF16), w_ref[...], NT, preferred_element_type=F32)
        sa = _sigmoid(ga_ref[...])
        sb = _sigmoid(gb_ref[...])
        dua_ref[...] = (dm * sa).astype(BF16)
        dub_ref[...] = (dm * sb).astype(BF16)
        dg_ref[:, :d] = (dm * ua_ref[...] * (sa * (1.0 - sa))).astype(BF16)
        dg_ref[:, d:] = (dm * ub_ref[...] * (sb * (1.0 - sb))).astype(BF16)

    row = pl.BlockSpec((tt, d), lambda i: (i, 0))
    wide = pl.BlockSpec((tt, 2 * d), lambda i: (i, 0))
    return pl.pallas_call(
        body, name="out_dx_gate_bwd", grid=(t_dim // tt,),
        in_specs=[row, pl.BlockSpec((d, d), lambda i: (0, 0)),
                  pl.BlockSpec((tt, d), lambda i: (i, 3)), pl.BlockSpec((tt, d), lambda i: (i, 4)), row, row],
        out_specs=[row, row, wide],
        out_shape=[jax.ShapeDtypeStruct((t_dim, d), BF16), jax.ShapeDtypeStruct((t_dim, d), BF16),
                   jax.ShapeDtypeStruct((t_dim, 2 * d), BF16)],
        compiler_params=_params(("parallel",)),
    )(dh, w_out, proj, proj, ua, ub)


def _ffn_up_swiglu(n, wg3, wu3, *, tt=1024):
    t_dim, d = n.shape
    n_s, f4, _ = wg3.shape

    def body(n_ref, wg_ref, wu_ref, g_ref, u_ref, a_ref):
        nv = n_ref[...]
        gv = lax.dot_general(nv, wg_ref[...], NT, preferred_element_type=F32)
        uv = lax.dot_general(nv, wu_ref[...], NT, preferred_element_type=F32)
        g_ref[...] = gv.astype(BF16)
        u_ref[...] = uv.astype(BF16)
        a_ref[...] = (gv * _sigmoid(gv) * uv).astype(BF16)

    wspec = pl.BlockSpec((None, f4, d), lambda i, s: (s, 0, 0))
    ospec = pl.BlockSpec((None, tt, f4), lambda i, s: (s, i, 0))
    shp = (n_s, t_dim, f4)
    return pl.pallas_call(
        body, name="ffn_up_swiglu", grid=(t_dim // tt, n_s),
        in_specs=[pl.BlockSpec((tt, d), lambda i, s: (i, 0)), wspec, wspec], out_specs=[ospec, ospec, ospec],
        out_shape=[jax.ShapeDtypeStruct(shp, BF16)] * 3,
        compiler_params=_params(("parallel", "parallel")),
    )(n, wg3, wu3)


def _ffn_down_dx_swiglu(dh, wd3, g3, u3, *, tt=1024):
    t_dim, d = dh.shape
    n_s, f4, _ = wd3.shape

    def body(dh_ref, w_ref, g_ref, u_ref, dg_ref, du_ref):
        da = lax.dot_general(dh_ref[...].astype(BF16), w_ref[...], NT, preferred_element_type=F32)
        gv = g_ref[...].astype(F32)
        sg = _sigmoid(gv)
        dg_ref[...] = (da * u_ref[...].astype(F32) * (sg + gv * sg * (1.0 - sg))).astype(BF16)
        du_ref[...] = (da * (gv * sg)).astype(BF16)

    spec = pl.BlockSpec((None, tt, f4), lambda i, s: (s, i, 0))
    shp = jax.ShapeDtypeStruct((n_s, t_dim, f4), BF16)
    return pl.pallas_call(
        body, name="ffn_down_dx_swiglu", grid=(t_dim // tt, n_s),
        in_specs=[pl.BlockSpec((tt, d), lambda i, s: (i, 0)), pl.BlockSpec((None, f4, d), lambda i, s: (s, 0, 0)),
                  spec, spec],
        out_specs=[spec, spec], out_shape=[shp, shp],
        compiler_params=_params(("parallel", "parallel")),
    )(dh, wd3, g3, u3)


def _mem_fwd(qm, kvm, *, tt=2048):
    b_dim, s_dim, _ = qm.shape
    n_mem = kvm.shape[1]
    scale = MEM_HEAD_DIM ** -0.5

    def body(q_ref, k_ref, v_ref, o_ref):
        sc = lax.dot_general(q_ref[0], k_ref[0], NT, preferred_element_type=F32) * scale
        p = jnp.exp(sc - jnp.max(sc, axis=-1, keepdims=True))
        p = p / jnp.sum(p, axis=-1, keepdims=True)
        o_ref[0] = jnp.dot(p.astype(BF16), v_ref[0], preferred_element_type=F32).astype(BF16)

    qs = pl.BlockSpec((1, tt, MEM_HEAD_DIM), lambda b, h, i: (b, i, h))
    return pl.pallas_call(
        body, name="mem_fwd", grid=(b_dim, N_HEADS_MEM, s_dim // tt),
        in_specs=[qs, pl.BlockSpec((1, n_mem, MEM_HEAD_DIM), lambda b, h, i: (b, 0, h)),
                  pl.BlockSpec((1, n_mem, MEM_HEAD_DIM), lambda b, h, i: (b, 0, N_HEADS_MEM + h))],
        out_specs=qs, out_shape=jax.ShapeDtypeStruct(qm.shape, BF16),
        compiler_params=_params(("parallel", "parallel", "parallel")),
    )(qm, kvm, kvm)


def _mem_bwd(qm, kvm, dom, *, tt=2048):
    b_dim, s_dim, _ = qm.shape
    n_mem = kvm.shape[1]
    scale = MEM_HEAD_DIM ** -0.5

    def body(q_ref, k_ref, v_ref, do_ref, dq_ref, dk_ref, dv_ref):
        qv, kv, vv, dov = q_ref[0], k_ref[0], v_ref[0], do_ref[0]
        sc = lax.dot_general(qv, kv, NT, preferred_element_type=F32) * scale
        p = jnp.exp(sc - jnp.max(sc, axis=-1, keepdims=True))
        p = p / jnp.sum(p, axis=-1, keepdims=True)
        dp = lax.dot_general(dov, vv, NT, preferred_element_type=F32)
        ds = (p * (dp - jnp.sum(p * dp, axis=-1, keepdims=True)) * scale).astype(BF16)
        dq_ref[0] = jnp.dot(ds, kv, preferred_element_type=F32).astype(BF16)

        @pl.when(pl.program_id(2) == 0)
        def _():
            dk_ref[...] = jnp.zeros_like(dk_ref)
            dv_ref[...] = jnp.zeros_like(dv_ref)

        dk_ref[0] += lax.dot_general(ds, qv, TN, preferred_element_type=F32)
        dv_ref[0] += lax.dot_general(p.astype(BF16), dov, TN, preferred_element_type=F32)

    qs = pl.BlockSpec((1, tt, MEM_HEAD_DIM), lambda b, h, i: (b, i, h))
    ks = pl.BlockSpec((1, n_mem, MEM_HEAD_DIM), lambda b, h, i: (b, 0, h))
    vs = pl.BlockSpec((1, n_mem, MEM_HEAD_DIM), lambda b, h, i: (b, 0, N_HEADS_MEM + h))
    return pl.pallas_call(
        body, name="mem_bwd", grid=(b_dim, N_HEADS_MEM, s_dim // tt),
        in_specs=[qs, ks, vs, qs], out_specs=[qs, ks, ks],
        out_shape=[jax.ShapeDtypeStruct(qm.shape, BF16), jax.ShapeDtypeStruct((b_dim, n_mem, MEM_WIDTH), F32),
                   jax.ShapeDtypeStruct((b_dim, n_mem, MEM_WIDTH), F32)],
        compiler_params=_params(("parallel", "parallel", "arbitrary")),
    )(qm, kvm, kvm, dom)


def _adamw_math(wv, gv, mv, vv):
    nm = ADAM_B1 * mv + (1.0 - ADAM_B1) * gv
    nv = ADAM_B2 * vv + (1.0 - ADAM_B2) * (gv * gv)
    m_hat = nm / (1.0 - ADAM_B1 ** ADAM_STEP)
    v_hat = nv / (1.0 - ADAM_B2 ** ADAM_STEP)
    return -ADAM_LR * (m_hat / (jnp.sqrt(v_hat) + ADAM_EPS) + ADAM_WD * wv), nm, nv


def _adamw(w, g, m, v, *, name):
    rows, cols = w.shape
    tr = _tile(rows, 256, 8)

    def body(w_ref, g_ref, m_ref, v_ref, d_ref, nm_ref, nv_ref):
        d_ref[...], nm_ref[...], nv_ref[...] = _adamw_math(w_ref[...], g_ref[...], m_ref[...], v_ref[...])

    spec = pl.BlockSpec((tr, cols), lambda i: (i, 0))
    shp = jax.ShapeDtypeStruct((rows, cols), F32)
    return pl.pallas_call(
        body, name=name, grid=(rows // tr,),
        in_specs=[spec] * 4, out_specs=[spec] * 3, out_shape=[shp] * 3,
        compiler_params=_params(("parallel",)),
    )(w, g, m, v)


def _prefetch_spec(grid, in_specs, out_specs):
    return pltpu.PrefetchScalarGridSpec(num_scalar_prefetch=1, grid=grid, in_specs=in_specs, out_specs=out_specs)


def _adamw_halves(w, mine, theirs, m, v, c_idx, *, name):
    rows, cols = w.shape
    half = rows // 2
    tr = _tile(half, _row_cap(cols), 8)
    nh = half // tr

    def body(c_ref, w_ref, mine_ref, theirs_ref, m_ref, v_ref, g_ref, d_ref, nm_ref, nv_ref):
        gv = jnp.where(pl.program_id(0) == c_ref[0], mine_ref[...], theirs_ref[...])
        g_ref[...] = gv
        d_ref[...], nm_ref[...], nv_ref[...] = _adamw_math(w_ref[...], gv, m_ref[...], v_ref[...])

    full = pl.BlockSpec((tr, cols), lambda h, i, c_ref: (h * nh + i, 0))
    part = pl.BlockSpec((tr, cols), lambda h, i, c_ref: (i, 0))
    shp = jax.ShapeDtypeStruct((rows, cols), F32)
    return pl.pallas_call(
        body, name=name, grid_spec=_prefetch_spec((2, nh), [full, part, part, full, full], [full] * 4),
        out_shape=[shp] * 4,
        compiler_params=_params(("parallel", "parallel")),
    )(c_idx, w, mine, theirs, m, v)


def _pair_sum(g3, theirs, c_idx, *, name):
    n, rows, cols = g3.shape
    half = rows // 2
    tr = _tile(half, _row_cap(cols), 16)

    def body(c_ref, g_ref, t_ref, o_ref):
        o_ref[...] = (g_ref[...] + t_ref[...]).astype(BF16)

    part = pl.BlockSpec((None, tr, cols), lambda s, i, c_ref: (s, i, 0))
    return pl.pallas_call(
        body, name=name,
        grid_spec=_prefetch_spec((n, half // tr),
                                 [pl.BlockSpec((None, None, tr, cols), lambda s, i, c_ref: (s, c_ref[0], i, 0)), part],
                                 part),
        out_shape=jax.ShapeDtypeStruct((n, half, cols), BF16),
        compiler_params=_params(("parallel", "parallel")),
    )(c_idx, g3.reshape(n, 2, half, cols), theirs)


def _chip_sum(pair, recv, s_idx, *, name):
    _, half, cols = pair.shape
    tr = _tile(half, _row_cap(cols), 16)

    def body(s_ref, p_ref, r_ref, o_ref):
        o_ref[...] = ((p_ref[...].astype(F32) + r_ref[0].astype(F32)) + r_ref[1].astype(F32)) + r_ref[2].astype(F32)

    return pl.pallas_call(
        body, name=name,
        grid_spec=_prefetch_spec((half // tr,),
                                 [pl.BlockSpec((None, tr, cols), lambda i, s_ref: (s_ref[0], i, 0)),
                                  pl.BlockSpec((N_CHIPS - 1, tr, cols), lambda i, s_ref: (0, i, 0))],
                                 pl.BlockSpec((tr, cols), lambda i, s_ref: (i, 0))),
        out_shape=jax.ShapeDtypeStruct((half, cols), F32),
        compiler_params=_params(("parallel",)),
    )(s_idx, pair, recv)


def _sum8(parts):
    n, rows, cols = parts.shape

    def body(p_ref, o_ref):
        acc = p_ref[0]
        for i in range(1, n):
            acc = acc + p_ref[i]
        o_ref[...] = acc

    return pl.pallas_call(
        body, name="small_sum", grid=(1,),
        in_specs=[pl.BlockSpec((n, rows, cols), lambda i: (0, 0, 0))],
        out_specs=pl.BlockSpec((rows, cols), lambda i: (0, 0)),
        out_shape=jax.ShapeDtypeStruct((rows, cols), parts.dtype),
        compiler_params=_params(("arbitrary",)),
    )(parts)


def _place():
    return lax.axis_index("x"), lax.axis_index("y"), lax.axis_index("c")


ANY = pl.BlockSpec(memory_space=pl.ANY)


def _rider_parts(rider):
    if rider is None:
        return (), [], []
    kind, arrays = rider
    n = len(arrays)
    shapes = {"gather": _gathered_shapes, "pair": _pair_shapes, "chip": _chip_shapes}[kind](arrays)
    sems = _gather_sems(n) if kind == "gather" else _exchange_sems(n if kind == "pair" else 3 * n)
    return tuple(arrays), shapes, sems


def _rider_hooks(rider, ins, outs, sems, step, n_steps):
    if rider is None:
        return (lambda: None), (lambda: None)
    if rider[0] == "gather":
        start, forward, finish = _gather_steps(ins, outs, *sems)
    else:
        start, finish = {"pair": _pair_steps, "chip": _chip_steps}[rider[0]](ins, outs, *sems)
        forward = None

    def begin():
        pl.when(step == 0)(start)

    def end():
        if forward is not None:
            pl.when(step == n_steps - 2)(forward)
        pl.when(step == n_steps - 1)(finish)

    return begin, end


def _gathered_shapes(shards):
    return [jax.ShapeDtypeStruct((N_CHIPS,) + s.shape, s.dtype) for s in shards]


def _gather_sems(n):
    return [pltpu.SemaphoreType.DMA((7 * n,)), pltpu.SemaphoreType.DMA((7 * n,))]


def _gather_steps(ins, outs, send_sems, recv_sems):
    n = len(ins)
    halves = [r.shape[0] // 2 for r in ins]
    x, y, c = _place()
    my_chip = 2 * x + y
    me, sibling = (x, y, c), (x, y, 1 - c)
    chips = [(1 - x, y), (x, 1 - y), (1 - x, 1 - y)]

    def half_of(w, chip, pc):
        return outs[w].at[chip, pl.ds(pc * halves[w], halves[w]), :]

    def copy(w, k, src, dst, to):
        return pltpu.make_async_remote_copy(
            src_ref=src, dst_ref=dst, send_sem=send_sems.at[7 * w + k], recv_sem=recv_sems.at[7 * w + k],
            device_id=to, device_id_type=MESH)

    def firsts():
        cps = []
        for w in range(n):
            cps.append(copy(w, 0, ins[w], outs[w].at[my_chip], sibling))
            mine = ins[w].at[pl.ds(c * halves[w], halves[w]), :]
            for j, (px, py) in enumerate(chips):
                cps.append(copy(w, 1 + j, mine, half_of(w, my_chip, c), (px, py, c)))
        return cps

    def passes():
        return [copy(w, 4 + j, half_of(w, 2 * px + py, c), half_of(w, 2 * px + py, c), sibling)
                for w in range(n) for j, (px, py) in enumerate(chips)]

    def start():
        for cp in firsts():
            cp.start()

    def forward():
        fws = passes()
        for w in range(n):
            for j, (px, py) in enumerate(chips):
                landed = half_of(w, 2 * px + py, c)
                copy(w, 1 + j, landed, landed, me).wait_recv()
                fws[3 * w + j].start()

    def finish():
        for w in range(n):
            copy(w, 0, ins[w], outs[w].at[my_chip], me).wait_recv()
            for j, (px, py) in enumerate(chips):
                landed = half_of(w, 2 * px + py, 1 - c)
                copy(w, 4 + j, landed, landed, me).wait_recv()
        for cp in firsts() + passes():
            cp.wait_send()

    return start, forward, finish


def _pair_shapes(grads):
    return [jax.ShapeDtypeStruct((g.shape[0], g.shape[1] // 2, g.shape[2]), g.dtype) for g in grads]


def _exchange_sems(n):
    return [pltpu.SemaphoreType.DMA((n,)), pltpu.SemaphoreType.DMA((n,))]


def _exchange_steps(copies):
    def start():
        for cp in copies():
            cp.start()

    def finish():
        for cp in copies():
            cp.wait()

    return start, finish


def _pair_steps(ins, outs, send_sems, recv_sems):
    x, y, c = _place()

    def copies():
        return [pltpu.make_async_remote_copy(
            src_ref=ins[w].at[:, pl.ds((1 - c) * (ins[w].shape[1] // 2), ins[w].shape[1] // 2), :], dst_ref=outs[w],
            send_sem=send_sems.at[w], recv_sem=recv_sems.at[w], device_id=(x, y, 1 - c), device_id_type=MESH)
            for w in range(len(ins))]

    return _exchange_steps(copies)


def _chip_shapes(pairs):
    return [jax.ShapeDtypeStruct((N_CHIPS - 1,) + p.shape[1:], p.dtype) for p in pairs]


def _chip_steps(ins, outs, send_sems, recv_sems):
    x, y, c = _place()
    others = [(1 - x, y), (x, 1 - y), (1 - x, 1 - y)]

    def copies():
        return [pltpu.make_async_remote_copy(
            src_ref=ins[w].at[2 * px + py], dst_ref=outs[w].at[j],
            send_sem=send_sems.at[3 * w + j], recv_sem=recv_sems.at[3 * w + j],
            device_id=(px, py, c), device_id_type=MESH)
            for w in range(len(ins)) for j, (px, py) in enumerate(others)]

    return _exchange_steps(copies)


def _swap_halves(mine):
    n = len(mine)

    def body(*refs):
        ins, outs, send_sems, recv_sems = refs[:n], refs[n:2 * n], refs[2 * n], refs[2 * n + 1]
        x, y, c = _place()
        copies = [pltpu.make_async_remote_copy(
            src_ref=ins[w], dst_ref=outs[w], send_sem=send_sems.at[w], recv_sem=recv_sems.at[w],
            device_id=(x, y, 1 - c), device_id_type=MESH) for w in range(n)]
        for cp in copies:
            cp.start()
        for cp in copies:
            cp.wait()

    return pl.pallas_call(
        body, name="grad_swap_halves",
        out_shape=[jax.ShapeDtypeStruct(h.shape, h.dtype) for h in mine],
        in_specs=[ANY] * n, out_specs=[ANY] * n,
        scratch_shapes=[pltpu.SemaphoreType.DMA((n,)), pltpu.SemaphoreType.DMA((n,))],
    )(*mine)


def _gather_small(small):
    srows, cols = small.shape

    def body(s_ref, all_ref, send_sems, recv_sems, local_sem):
        x, y, c = _place()
        me = 4 * x + 2 * y + c
        keep_small = pltpu.make_async_copy(s_ref, all_ref.at[me], local_sem)
        keep_small.start()
        sends = []
        for kk in range(1, 8):
            peer = (x ^ (kk >> 2), y ^ ((kk >> 1) & 1), c ^ (kk & 1))
            sends.append(pltpu.make_async_remote_copy(
                src_ref=s_ref, dst_ref=all_ref.at[me],
                send_sem=send_sems.at[kk], recv_sem=recv_sems.at[kk], device_id=peer, device_id_type=MESH))
        for cp in sends:
            cp.start()
        for kk in range(1, 8):
            px, py, pc = x ^ (kk >> 2), y ^ ((kk >> 1) & 1), c ^ (kk & 1)
            pltpu.make_async_remote_copy(
                src_ref=s_ref, dst_ref=all_ref.at[4 * px + 2 * py + pc],
                send_sem=send_sems.at[kk], recv_sem=recv_sems.at[kk], device_id=(px, py, pc),
                device_id_type=MESH).wait_recv()
        for cp in sends:
            cp.wait_send()
        keep_small.wait()

    return pl.pallas_call(
        body, name="gather_small",
        out_shape=jax.ShapeDtypeStruct((8, srows, cols), small.dtype),
        in_specs=[ANY], out_specs=ANY,
        scratch_shapes=[pltpu.SemaphoreType.DMA((8,)), pltpu.SemaphoreType.DMA((8,)), pltpu.SemaphoreType.DMA],
    )(small)


SHARDED = (("w_in", D_MODEL, IN_COLS, 1), ("w_up_a", ATT_WIDTH, D_MODEL, 1), ("w_up_b", ATT_WIDTH, D_MODEL, 1),
           ("w_out", D_MODEL, D_MODEL, 0), ("w_q_mem", D_MODEL, MEM_WIDTH, 0), ("w_kv_mem", D_MODEL, 2 * MEM_WIDTH, 0),
           ("w_o_mem", MEM_WIDTH, D_MODEL, 1), ("w_ffn_gate", D_FF, D_MODEL, 0), ("w_ffn_up", D_FF, D_MODEL, 0),
           ("w_ffn_down", D_FF, D_MODEL, 0))
TRANSPOSED = ("w_ffn_gate", "w_ffn_up")
NAMES = tuple(n for n, _, _, _ in SHARDED)


def _held(name, shard):
    return shard.T if name in TRANSPOSED else shard
EARLY, LATE = NAMES[:1], NAMES[1:]
GAINS = ("g_mix", "g_mem_q", "g_mem_kv", "g_ffn", "g_final")


def _natural(w3):
    n, r, c = w3.shape
    return w3.reshape(n * r, c)


def _shard_major(g, axis):
    if axis == 1:
        return g
    r, c = g.shape
    return g.reshape(N_CHIPS, r // N_CHIPS, c)


def kernel(x, mem, positions, g_mix, w_in, w_up_a, w_up_b, w_out, g_mem_q, g_mem_kv, w_q_mem, w_kv_mem, w_o_mem, g_ffn, w_ffn_gate, w_ffn_up, w_ffn_down, g_final, loss_target, m_g_mix, m_w_in, m_w_up_a, m_w_up_b, m_w_out, m_g_mem_q, m_g_mem_kv, m_w_q_mem, m_w_kv_mem, m_w_o_mem, m_g_ffn, m_w_ffn_gate, m_w_ffn_up, m_w_ffn_down, m_g_final, v_g_mix, v_w_in, v_w_up_a, v_w_up_b, v_w_out, v_g_mem_q, v_g_mem_kv, v_w_q_mem, v_w_kv_mem, v_w_o_mem, v_g_ffn, v_w_ffn_gate, v_w_ffn_up, v_w_ffn_down, v_g_final):
    given = dict(locals())
    shards = {n: _held(n, given[n][0]) for n in NAMES}

    early_shards = [shards[n].astype(BF16) for n in EARLY]
    late_shards = [shards[n].astype(BF16) for n in LATE]
    c_idx = lax.axis_index("c").astype(jnp.int32).reshape(1)
    s_idx = (2 * lax.axis_index("x") + lax.axis_index("y")).astype(jnp.int32).reshape(1)

    loss_row, grad_x, mine, gain_grads = _local_step(x, mem, positions, loss_target, g_mix, g_mem_q, g_mem_kv,
                                                     g_ffn, g_final, {}, early_shards, late_shards, (c_idx, s_idx))
    return _reduce_and_update(given, shards, loss_row, grad_x, mine, gain_grads, c_idx)


def _reduce_halves(glist, names, c_idx, s_idx, pair_exchange, chip_exchange):
    theirs = pair_exchange(glist)
    pairs = [_pair_sum(g, t, c_idx, name="pair_sum_" + n) for n, g, t in zip(names, glist, theirs)]
    recv = chip_exchange(pairs)
    return [_chip_sum(p, r, s_idx, name="chip_sum_" + n) for n, p, r in zip(names, pairs, recv)]


def _local_step(x, mem, positions, loss_target, g_mix, g_mem_q, g_mem_kv, g_ffn, g_final, wf,
                early_shards=None, late_shards=None, place=None):
    b_dim, s_dim, d = x.shape
    t_dim = b_dim * s_dim
    n_mem = mem.shape[1]
    wf = dict(wf)

    xb = x.reshape(t_dim, d)
    tgt = loss_target.reshape(t_dim, d)
    memf = mem.reshape(b_dim * n_mem, d)
    gfin = g_final.reshape(1, d)
    pos = positions.reshape(t_dim, 1).astype(F32)

    lane = jnp.arange(LANES) % HEAD_DIM
    half = ROPE_DIM // 2
    inv_freq = ROPE_THETA ** (-jnp.arange(half, dtype=F32) / half)
    inv_lane = jnp.where(lane < ROPE_DIM, inv_freq[lane % half], 0.0).reshape(1, -1).astype(F32)
    sel_a = (lane < half).astype(F32).reshape(1, -1)
    sel_b = ((lane >= half) & (lane < ROPE_DIM)).astype(F32).reshape(1, -1)

    def rows3(t):
        return t.reshape(b_dim, s_dim, t.shape[-1])

    def rows2(t):
        return t.reshape(t_dim, t.shape[-1])

    if early_shards:
        n1, gathered = _rms_fwd(xb, g_mix, name="rms_mix", rider=("gather", early_shards))
        wf.update(zip(EARLY, gathered))
    else:
        n1 = _rms_fwd(xb, g_mix, name="rms_mix")
    proj = _mm_cs(n1, wf["w_in"], name="mm_in")
    proj3 = rows3(proj)
    cs, sn = _rope_table(pos, inv_lane, sel_a, sel_b)
    cs3, sn3 = rows3(cs), rows3(sn)
    (oa16, oa32, lse_a), _ = _dil_fwd(proj3, cs3, sn3, sel_a, sel_b)
    ob16, gathered = _sb_fwd(proj3, ("gather", late_shards) if late_shards else None)
    wf.update(zip(LATE, gathered))
    w_out, w_q, w_kv = _natural(wf["w_out"]), _natural(wf["w_q_mem"]), _natural(wf["w_kv_mem"])
    oa, ob = rows2(oa16), rows2(ob16)
    ua = _mm_cs(oa, wf["w_up_a"], name="mm_up_a")
    ub = _mm_cs(ob, wf["w_up_b"], name="mm_up_b")
    mixed, h1, hn = _gate_out_norm(proj, ua, ub, w_out, xb, g_mem_q)

    memn = _rms_fwd(memf, g_mem_kv, name="rms_mem_kv")
    qm = _mm(hn, w_q, name="mm_q_mem", out_dtype=BF16)
    kvm = _mm(memn, w_kv, name="mm_kv_mem", out_dtype=BF16)
    qm3, kvm3 = rows3(qm), kvm.reshape(b_dim, n_mem, 2 * MEM_WIDTH)
    om = rows2(_mem_fwd(qm3, kvm3))
    h2 = _mm_cs(om, wf["w_o_mem"], name="mm_o_mem", add=h1)

    n3 = _rms_fwd(h2, g_ffn, name="rms_ffn")
    gate3, up3, act3 = _ffn_up_swiglu(n3, wf["w_ffn_gate"], wf["w_ffn_up"])
    loss_row, dh3, dg_final = _down_final(act3, wf["w_ffn_down"], h2, gfin, tgt)

    grads = {}
    grads["w_ffn_down"] = _mm_ffn_down_dw(act3, dh3, name="mm_down_dw")
    dgate3, dup3 = _ffn_down_dx_swiglu(dh3, wf["w_ffn_down"], gate3, up3)
    grads["w_ffn_gate"] = _mm_ffn_down_dw(dgate3, n3, name="mm_gate_dw")
    grads["w_ffn_up"] = _mm_ffn_down_dw(dup3, n3, name="mm_up_dw")
    dn3 = _mm_ffn_rows(dgate3, wf["w_ffn_gate"], name="mm_gate_dx")
    dn3 = _mm_ffn_rows(dup3, wf["w_ffn_up"], name="mm_up_dx", add=dn3)
    dh2, dg_ffn = _rms_bwd(h2, g_ffn, dn3, dh3, name="rms_ffn_bwd")

    dom = _mm_cs_dx(dh2, wf["w_o_mem"], name="mm_o_mem_dx", out_dtype=BF16)
    grads["w_o_mem"] = _mm_cs_dw(om, dh2, name="mm_o_mem_dw")
    dqm, dkm, dvm = _mem_bwd(qm3, kvm3, rows3(dom))
    dqm = rows2(dqm)
    dkvm = jnp.concatenate([dkm, dvm], axis=-1).reshape(b_dim * n_mem, 2 * MEM_WIDTH).astype(BF16)
    grads["w_q_mem"] = _shard_major(_mm(hn, dqm, name="mm_q_mem_dw", ta=True), 0)
    dhn = _mm(dqm, w_q, name="mm_q_mem_dx", tb=True)
    grads["w_kv_mem"] = _shard_major(_mm(memn, dkvm, name="mm_kv_mem_dw", ta=True), 0)
    dmemn = _mm(dkvm, w_kv, name="mm_kv_mem_dx", tb=True)
    _, dg_mem_kv = _rms_bwd(memf, g_mem_kv, dmemn, None, name="rms_mem_kv_bwd")
    dh1, dg_mem_q = _rms_bwd(h1, g_mem_q, dhn, dh2, name="rms_mem_q_bwd")

    grads["w_out"] = _shard_major(_mm(mixed, dh1, name="mm_out_dw", ta=True), 0)
    dua, dub, dgates = _out_dx_gate_bwd(dh1, w_out, proj, ua, ub)
    doa = _mm_cs_dx(dua, wf["w_up_a"], name="mm_up_a_dx")
    grads["w_up_a"] = _mm_cs_dw(oa, dua, name="mm_up_a_dw")
    dob = _mm_cs_dx(dub, wf["w_up_b"], name="mm_up_b_dx", out_dtype=BF16)
    grads["w_up_b"] = _mm_cs_dw(ob, dub, name="mm_up_b_dw")

    att = {}

    def dil_with_pairs(glist):
        att["a"], theirs = _dil_bwd(proj3, cs3, sn3, sel_a, sel_b, rows3(doa), oa32, lse_a,
                                    ("pair", glist) if glist else None)
        return theirs

    def sb_with_chips(pairs):
        att["b"], recv = _sb_bwd(proj3, rows3(dob), ("chip", pairs) if pairs else None)
        return recv

    if place is None:
        dil_with_pairs(())
        sb_with_chips(())
    else:
        mine_late = _reduce_halves([grads[n] for n in LATE], LATE, *place, dil_with_pairs, sb_with_chips)
    dproj = jnp.concatenate([rows2(t) for t in att["a"] + att["b"]] + [dgates], axis=1)
    grads["w_in"] = _mm_cs_dw(n1, dproj, name="mm_in_dw")
    if place is None:
        dn1 = _mm_cs_dx(dproj, wf["w_in"], name="mm_in_dx")
        dx, dg_mix = _rms_bwd(xb, g_mix, dn1, dh1, name="rms_mix_bwd")
    else:
        tail = {}

        def dx_with_pairs(glist):
            tail["dn1"], theirs = _mm_cs_dx(dproj, wf["w_in"], name="mm_in_dx", rider=("pair", glist))
            return theirs

        def rms_with_chips(pairs):
            tail["dx"], tail["dg"], recv = _rms_bwd(xb, g_mix, tail["dn1"], dh1, name="rms_mix_bwd",
                                                    rider=("chip", pairs))
            return recv

        mine_early = _reduce_halves([grads[n] for n in EARLY], EARLY, *place, dx_with_pairs, rms_with_chips)
        dx, dg_mix = tail["dx"], tail["dg"]
    grad_x = dx.reshape(b_dim, s_dim, d)
    gains = (dg_mix, dg_mem_q, dg_mem_kv, dg_ffn, dg_final)
    if place is None:
        return loss_row, grad_x, grads, gains
    return loss_row, grad_x, mine_early + mine_late, gains


def _reduce_and_update(given, shards, loss_row, grad_x, mine, gain_grads, c_idx):
    d = D_MODEL
    dg_mix, dg_mem_q, dg_mem_kv, dg_ffn, dg_final = gain_grads
    small = jnp.concatenate([dg_mix, dg_mem_q, dg_mem_kv, dg_ffn, dg_final,
                             jnp.pad(loss_row, ((0, 0), (0, FLAT_COLS - LANES))), jnp.zeros((2, FLAT_COLS), F32)], axis=0)
    small_all = _gather_small(small)
    others = _swap_halves(mine)
    small_sum = _sum8(small_all)
    loss = small_sum[5, 0]

    out_g, out_d, out_m, out_v = {}, {}, {}, {}
    for n, mine_n, other_n in zip(NAMES, mine, others):
        res = _adamw_halves(shards[n], mine_n, other_n, _held(n, given["m_" + n][0]), _held(n, given["v_" + n][0]),
                            c_idx, name="adamw_" + n)
        out_g[n], out_d[n], out_m[n], out_v[n] = [_held(n, r)[None] for r in res]
    gain_w = jnp.concatenate([given[n].reshape(1, d) for n in GAINS], axis=0)
    gain_m = jnp.concatenate([given["m_" + n].reshape(1, d) for n in GAINS], axis=0)
    gain_v = jnp.concatenate([given["v_" + n].reshape(1, d) for n in GAINS], axis=0)
    gain_g = small_sum[:len(GAINS)]
    gd, gm, gv = _adamw(gain_w, gain_g, gain_m, gain_v, name="adamw_gains")
    for i, n in enumerate(GAINS):
        shape = given[n].shape
        out_g[n], out_d[n] = gain_g[i].reshape(shape), gd[i].reshape(shape)
        out_m[n], out_v[n] = gm[i].reshape(shape), gv[i].reshape(shape)

    order = ["g_mix", "w_in", "w_up_a", "w_up_b", "w_out", "g_mem_q", "g_mem_kv", "w_q_mem", "w_kv_mem", "w_o_mem",
             "g_ffn", "w_ffn_gate", "w_ffn_up", "w_ffn_down", "g_final"]
    return (loss, grad_x, *[out_g[n] for n in order], *[out_d[n] for n in order],
            *[out_m[n] for n in order], *[out_v[n] for n in order])
```

```python
import jax
import jax.numpy as jnp
from jax import lax
from jax.experimental import pallas as pl
from jax.experimental.pallas import tpu as pltpu

F32 = jnp.float32
BF16 = jnp.bfloat16
MESH = pl.DeviceIdType.MESH

D_MODEL = 1024
HEAD_DIM = 64
N_HEADS = 8
ATT_WIDTH = N_HEADS * HEAD_DIM
DIL_PATTERNS = ((128, 1), (512, 4), (2048, 16))
BLOCK = 128
SB_ROWS = 1024
SB_STEP = 4
ROPE_THETA = 500000.0
ROPE_DIM = HEAD_DIM // 4
N_HEADS_MEM = 4
MEM_HEAD_DIM = 128
MEM_WIDTH = N_HEADS_MEM * MEM_HEAD_DIM
D_FF = 2816
IN_COLS = 6 * ATT_WIDTH + 2 * D_MODEL
RMS_EPS = 1e-6
ADAM_LR = 0.001
ADAM_B1 = 0.9
ADAM_B2 = 0.999
ADAM_EPS = 1e-08
ADAM_WD = 0.01
ADAM_STEP = 10

N_CHIPS = 4
LANES = 128
FLAT_COLS = 1024
VMEM_LIMIT = 56 * 1024 * 1024

PAIRS = ATT_WIDTH // LANES
COL_QA, COL_KA, COL_VA, COL_QB, COL_KB, COL_VB = (i * PAIRS for i in range(6))

MM_CAP = 1408
TOK_CAP = 2048
NN = (((1,), (0,)), ((), ()))
NT = (((1,), (1,)), ((), ()))
TN = (((0,), (0,)), ((), ()))
BNN = (((2,), (1,)), ((0,), (0,)))
BNT = (((2,), (2,)), ((0,), (0,)))
BTN = (((1,), (1,)), ((0,), (0,)))
DIL_BATCH = 16
DIL_CHUNK = 512


def _tile(dim, cap, unit=LANES):
    if dim <= cap:
        return dim
    best = None
    for t in range(unit, cap + 1, unit):
        if dim % t == 0:
            best = t
    assert best is not None, (dim, cap)
    return best


def _row_cap(cols):
    return max(256, (1 << 18) // cols)


def _params(sem):
    return pltpu.CompilerParams(dimension_semantics=sem, vmem_limit_bytes=VMEM_LIMIT)


def _mm(a, b, *, name, ta=False, tb=False, add=None, out_dtype=F32,
        tm_cap=MM_CAP, tn_cap=MM_CAP, tk_cap=MM_CAP):
    if ta:
        k_dim, m_dim = a.shape
    else:
        m_dim, k_dim = a.shape
    if tb:
        n_dim, kb = b.shape
    else:
        kb, n_dim = b.shape
    assert kb == k_dim, (a.shape, b.shape, ta, tb)
    tm, tn, tk = _tile(m_dim, tm_cap), _tile(n_dim, tn_cap), _tile(k_dim, tk_cap)
    nk = k_dim // tk
    dims = (((0 if ta else 1,), (1 if tb else 0,)), ((), ()))
    has_add = add is not None

    def body(*refs):
        if has_add:
            a_ref, b_ref, add_ref, o_ref = refs[:4]
        else:
            a_ref, b_ref, o_ref = refs[:3]
        part = lax.dot_general(a_ref[...].astype(BF16), b_ref[...].astype(BF16), dims, preferred_element_type=F32)

        def finish(r):
            if has_add:
                r = add_ref[...] + r
            o_ref[...] = r.astype(out_dtype)

        if nk == 1:
            finish(part)
            return
        acc_ref = refs[-1]
        k = pl.program_id(2)

        @pl.when(k == 0)
        def _():
            acc_ref[...] = part

        @pl.when(k > 0)
        def _():
            acc_ref[...] += part

        @pl.when(k == nk - 1)
        def _():
            finish(acc_ref[...])

    a_spec = pl.BlockSpec((tk, tm), lambda i, j, k: (k, i)) if ta else pl.BlockSpec((tm, tk), lambda i, j, k: (i, k))
    b_spec = pl.BlockSpec((tn, tk), lambda i, j, k: (j, k)) if tb else pl.BlockSpec((tk, tn), lambda i, j, k: (k, j))
    o_spec = pl.BlockSpec((tm, tn), lambda i, j, k: (i, j))
    in_specs = [a_spec, b_spec] + ([o_spec] if has_add else [])
    args = (a, b) + ((add,) if has_add else ())
    return pl.pallas_call(
        body, name=name, grid=(m_dim // tm, n_dim // tn, nk),
        in_specs=in_specs, out_specs=o_spec,
        out_shape=jax.ShapeDtypeStruct((m_dim, n_dim), out_dtype),
        scratch_shapes=[pltpu.VMEM((tm, tn), F32)] if nk > 1 else [],
        compiler_params=_params(("parallel", "parallel", "arbitrary")),
    )(*args)


def _mm_core(name, a, b, a_spec, b_spec, o_spec, out_shape, grid, dims, *, add=None, out_dtype=F32, rider=None):
    nk = grid[2]
    has_add = add is not None
    n_in = 3 if has_add else 2
    acc_shape = tuple(d for d in o_spec.block_shape if d is not None)
    extra, extra_shapes, extra_sems = _rider_parts(rider)
    n_w = len(extra)

    def body(*refs):
        a_ref, b_ref = refs[:2]
        o_ref = refs[n_in + n_w]
        step = (pl.program_id(0) * grid[1] + pl.program_id(1)) * nk + pl.program_id(2)
        begin, end = _rider_hooks(rider, refs[n_in:n_in + n_w], refs[n_in + n_w + 1:n_in + 2 * n_w + 1], refs[-2:],
                                  step, grid[0] * grid[1] * nk)
        begin()
        part = lax.dot_general(a_ref[...].astype(BF16), b_ref[...].astype(BF16), dims, preferred_element_type=F32)

        def finish(r):
            if has_add:
                r = refs[2][...] + r
            o_ref[...] = r.astype(out_dtype)

        if nk == 1:
            finish(part)
        else:
            acc_ref = refs[n_in + 2 * n_w + 1]
            k = pl.program_id(2)

            @pl.when(k == 0)
            def _():
                acc_ref[...] = part

            @pl.when(k > 0)
            def _():
                acc_ref[...] += part

            @pl.when(k == nk - 1)
            def _():
                finish(acc_ref[...])
        end()

    in_specs = [a_spec, b_spec] + ([o_spec] if has_add else []) + [ANY] * n_w
    args = (a, b) + ((add,) if has_add else ()) + extra
    res = pl.pallas_call(
        body, name=name, grid=grid, in_specs=in_specs, out_specs=[o_spec] + [ANY] * n_w,
        out_shape=[jax.ShapeDtypeStruct(out_shape, out_dtype)] + extra_shapes,
        scratch_shapes=([pltpu.VMEM(acc_shape, F32)] if nk > 1 else []) + extra_sems,
        compiler_params=_params(("arbitrary",) * 3 if n_w else ("parallel", "parallel", "arbitrary")),
    )(*args)
    return (res[0], res[1:]) if n_w else res[0]


def _mm_cs(a, w3, *, name, add=None, out_dtype=F32):
    m_dim, k_dim = a.shape
    _, _, n4 = w3.shape
    tm, tn, tk = _tile(m_dim, MM_CAP if add is not None else TOK_CAP), _tile(n4, MM_CAP), _tile(k_dim, MM_CAP)
    npb = n4 // tn
    return _mm_core(name, a, w3,
                    pl.BlockSpec((tm, tk), lambda i, j, k: (i, k)),
                    pl.BlockSpec((None, tk, tn), lambda i, j, k: (j // npb, k, j % npb)),
                    pl.BlockSpec((tm, tn), lambda i, j, k: (i, j)),
                    (m_dim, N_CHIPS * n4), (m_dim // tm, N_CHIPS * npb, k_dim // tk), NN, add=add, out_dtype=out_dtype)


def _mm_cs_dx(dy, w3, *, name, out_dtype=F32, rider=None):
    m_dim, _ = dy.shape
    _, k_dim, n4 = w3.shape
    tm, tkw, tn = _tile(m_dim, MM_CAP), _tile(k_dim, MM_CAP), _tile(n4, MM_CAP)
    npb = n4 // tn
    return _mm_core(name, dy, w3,
                    pl.BlockSpec((tm, tn), lambda i, j, k: (i, k)),
                    pl.BlockSpec((None, tkw, tn), lambda i, j, k: (k // npb, j, k % npb)),
                    pl.BlockSpec((tm, tkw), lambda i, j, k: (i, j)),
                    (m_dim, k_dim), (m_dim // tm, k_dim // tkw, N_CHIPS * npb), NT, out_dtype=out_dtype, rider=rider)


def _mm_cs_dw(a, dy, *, name):
    m_dim, k_dim = a.shape
    n4 = dy.shape[1] // N_CHIPS
    tmk, tn, tk = _tile(k_dim, MM_CAP), _tile(n4, MM_CAP), _tile(m_dim, TOK_CAP)
    npb = n4 // tn
    return _mm_core(name, a, dy,
                    pl.BlockSpec((tk, tmk), lambda i, j, k: (k, i)),
                    pl.BlockSpec((tk, tn), lambda i, j, k: (k, j)),
                    pl.BlockSpec((None, tmk, tn), lambda i, j, k: (j // npb, i, j % npb)),
                    (N_CHIPS, k_dim, n4), (k_dim // tmk, N_CHIPS * npb, m_dim // tk), TN)


def _mm_ffn_rows(a3, w3, *, name, add=None):
    _, t_dim, f4 = a3.shape
    _, _, d = w3.shape
    tm = _tile(t_dim, MM_CAP)
    return _mm_core(name, a3, w3,
                    pl.BlockSpec((None, tm, f4), lambda i, j, k: (k, i, 0)),
                    pl.BlockSpec((None, f4, d), lambda i, j, k: (k, 0, 0)),
                    pl.BlockSpec((tm, d), lambda i, j, k: (i, 0)),
                    (t_dim, d), (t_dim // tm, 1, N_CHIPS), NN, add=add)


def _mm_ffn_down_dw(act3, dh, *, name):
    _, t_dim, f4 = act3.shape
    d = dh.shape[1]
    tk = _tile(t_dim, TOK_CAP)
    return _mm_core(name, act3, dh,
                    pl.BlockSpec((None, tk, f4), lambda i, j, k: (i, k, 0)),
                    pl.BlockSpec((tk, d), lambda i, j, k: (k, 0)),
                    pl.BlockSpec((None, f4, d), lambda i, j, k: (i, 0, 0)),
                    (N_CHIPS, f4, d), (N_CHIPS, 1, t_dim // tk), TN)


def _rms_fwd(x, g, *, name, tt=512, rider=None):
    t_dim, d = x.shape
    tt = _tile(t_dim, tt, 8)
    extra, extra_shapes, extra_sems = _rider_parts(rider)
    n_w = len(extra)

    def body(*refs):
        x_ref, g_ref, o_ref = refs[0], refs[1], refs[2 + n_w]
        begin, end = _rider_hooks(rider, refs[2:2 + n_w], refs[3 + n_w:3 + 2 * n_w], refs[-2:], pl.program_id(0),
                                  t_dim // tt)
        begin()
        xv = x_ref[...]
        r = lax.rsqrt(jnp.mean(xv * xv, axis=-1, keepdims=True) + RMS_EPS)
        o_ref[...] = ((xv * r) * g_ref[...]).astype(o_ref.dtype)
        end()

    res = pl.pallas_call(
        body, name=name, grid=(t_dim // tt,),
        in_specs=[pl.BlockSpec((tt, d), lambda i: (i, 0)), pl.BlockSpec((1, d), lambda i: (0, 0))] + [ANY] * n_w,
        out_specs=[pl.BlockSpec((tt, d), lambda i: (i, 0))] + [ANY] * n_w,
        out_shape=[jax.ShapeDtypeStruct((t_dim, d), BF16)] + extra_shapes,
        scratch_shapes=extra_sems,
        compiler_params=_params(("arbitrary",) if n_w else ("parallel",)),
    )(x, g, *extra)
    return (res[0], res[1:]) if n_w else res[0]


def _rms_bwd(x, g, dy, add, *, name, tt=512, rider=None):
    t_dim, d = x.shape
    tt = _tile(t_dim, tt, 8)
    has_add = add is not None
    n_in = 4 if has_add else 3
    extra, extra_shapes, extra_sems = _rider_parts(rider)
    n_w = len(extra)

    def body(*refs):
        x_ref, g_ref, dy_ref = refs[:3]
        add_ref = refs[3] if has_add else None
        dx_ref, dg_ref = refs[n_in + n_w:n_in + n_w + 2]
        begin, end = _rider_hooks(rider, refs[n_in:n_in + n_w], refs[n_in + n_w + 2:n_in + 2 * n_w + 2], refs[-2:],
                                  pl.program_id(0), t_dim // tt)
        begin()
        xv = x_ref[...]
        dyv = dy_ref[...].astype(F32)
        r = lax.rsqrt(jnp.mean(xv * xv, axis=-1, keepdims=True) + RMS_EPS)
        xh = xv * r
        u = dyv * g_ref[...]
        dx = r * (u - xh * jnp.mean(u * xh, axis=-1, keepdims=True))
        if has_add:
            dx = add_ref[...] + dx
        dx_ref[...] = dx

        @pl.when(pl.program_id(0) == 0)
        def _():
            dg_ref[...] = jnp.zeros_like(dg_ref)

        dg_ref[...] += jnp.sum(dyv * xh, axis=0, keepdims=True)
        end()

    row = pl.BlockSpec((tt, d), lambda i: (i, 0))
    vec = pl.BlockSpec((1, d), lambda i: (0, 0))
    in_specs = [row, vec, row] + ([row] if has_add else []) + [ANY] * n_w
    args = (x, g, dy) + ((add,) if has_add else ()) + extra
    res = pl.pallas_call(
        body, name=name, grid=(t_dim // tt,),
        in_specs=in_specs, out_specs=[row, vec] + [ANY] * n_w,
        out_shape=[jax.ShapeDtypeStruct((t_dim, d), F32), jax.ShapeDtypeStruct((1, d), F32)] + extra_shapes,
        scratch_shapes=extra_sems,
        compiler_params=_params(("arbitrary",)),
    )(*args)
    return (res[0], res[1], res[2:]) if n_w else (res[0], res[1])


def _down_final(act3, wd3, h, g, target, *, tt=1024):
    n_s, t_dim, f4 = act3.shape
    d = h.shape[1]
    n_steps = t_dim // tt

    def body(a_ref, w_ref, h_ref, g_ref, t_ref, loss_ref, dh_ref, dg_ref, acc_ref, sq_ref):
        i, k = pl.program_id(0), pl.program_id(1)
        part = jnp.dot(a_ref[...], w_ref[...], preferred_element_type=F32)

        @pl.when(k == 0)
        def _():
            acc_ref[...] = part

        @pl.when(k > 0)
        def _():
            acc_ref[...] += part

        @pl.when(jnp.logical_and(i == 0, k == 0))
        def _():
            dg_ref[...] = jnp.zeros_like(dg_ref)
            sq_ref[...] = jnp.zeros_like(sq_ref)

        @pl.when(k == n_s - 1)
        def _():
            xv = h_ref[...] + acc_ref[...]
            gv = g_ref[...]
            r = lax.rsqrt(jnp.mean(xv * xv, axis=-1, keepdims=True) + RMS_EPS)
            xh = xv * r
            err = xh * gv - t_ref[...]
            dyv = err * (1.0 / d)
            u = dyv * gv
            dh_ref[...] = r * (u - xh * jnp.mean(u * xh, axis=-1, keepdims=True))
            dg_ref[...] += jnp.sum(dyv * xh, axis=0, keepdims=True)
            sq_ref[...] += jnp.sum(err * err, axis=0, keepdims=True)

        @pl.when(jnp.logical_and(i == n_steps - 1, k == n_s - 1))
        def _():
            total = jnp.sum(sq_ref[...], axis=-1, keepdims=True) * (0.5 / d)
            loss_ref[...] = jnp.broadcast_to(total, loss_ref.shape)

    row = pl.BlockSpec((tt, d), lambda i, k: (i, 0))
    vec = pl.BlockSpec((1, d), lambda i, k: (0, 0))
    return pl.pallas_call(
        body, name="down_final_loss", grid=(n_steps, n_s),
        in_specs=[pl.BlockSpec((None, tt, f4), lambda i, k: (k, i, 0)),
                  pl.BlockSpec((None, f4, d), lambda i, k: (k, 0, 0)), row, vec, row],
        out_specs=[pl.BlockSpec((1, LANES), lambda i, k: (0, 0)), row, vec],
        out_shape=[jax.ShapeDtypeStruct((1, LANES), F32), jax.ShapeDtypeStruct((t_dim, d), F32),
                   jax.ShapeDtypeStruct((1, d), F32)],
        scratch_shapes=[pltpu.VMEM((tt, d), F32), pltpu.VMEM((1, d), F32)],
        compiler_params=_params(("arbitrary", "arbitrary")),
    )(act3, wd3, h, g, target)


def _rope_table(pos, inv_lane, sel_a, sel_b, *, tt=512):
    t_dim = pos.shape[0]

    def body(p_ref, f_ref, a_ref, b_ref, c_ref, s_ref):
        ang = p_ref[...] * f_ref[...]
        on = (a_ref[...] + b_ref[...]) > 0.0
        c_ref[...] = jnp.where(on, jnp.cos(ang), 1.0)
        s_ref[...] = jnp.where(on, jnp.sin(ang), 0.0)

    vec = pl.BlockSpec((1, LANES), lambda i: (0, 0))
    row = pl.BlockSpec((tt, LANES), lambda i: (i, 0))
    shp = jax.ShapeDtypeStruct((t_dim, LANES), F32)
    return pl.pallas_call(
        body, name="rope_table", grid=(t_dim // tt,),
        in_specs=[pl.BlockSpec((tt, 1), lambda i: (i, 0)), vec, vec, vec],
        out_specs=[row, row], out_shape=[shp, shp],
        compiler_params=_params(("parallel",)),
    )(pos, inv_lane, sel_a, sel_b)


def _rotate(xv, cs, sn, sa, sb):
    half = ROPE_DIM // 2
    up = pltpu.roll(xv, LANES - half, 1)
    dn = pltpu.roll(xv, half, 1)
    return xv * cs + (dn * sb - up * sa) * sn


def _head_masks():
    h1 = lax.broadcasted_iota(jnp.int32, (1, LANES), 1) < HEAD_DIM
    return h1, jnp.logical_not(h1)


def _split_heads(xv, h1, h2):
    return jnp.where(h1, xv, 0.0).astype(BF16), jnp.where(h2, xv, 0.0).astype(BF16)


def _tri_masks():
    r = lax.broadcasted_iota(jnp.int32, (BLOCK, BLOCK), 0)
    c = lax.broadcasted_iota(jnp.int32, (BLOCK, BLOCK), 1)
    return c <= r, r <= c


def _stream_rows(start, dil):
    if dil == 1:
        return pl.ds(pl.multiple_of(start, BLOCK), BLOCK)
    return pl.ds(start, BLOCK, stride=dil)


def _dil_tile(idx, dil, nb):
    r = idx // nb
    n = idx % nb
    return (_stream_rows(r + dil * BLOCK * n, dil), _stream_rows(r + dil * BLOCK * jnp.maximum(n - 1, 0), dil),
            n > 0)


def _dil_specs(b_dim, s_dim):
    def col(c0):
        return pl.BlockSpec((None, s_dim, LANES),lambda b, h: (b, 0, c0 + h))
    tab = pl.BlockSpec((None, s_dim, LANES),lambda b, h: (b, 0, 0))
    vec = pl.BlockSpec((1, LANES), lambda b, h: (0, 0))
    return col, tab, vec


def _dil_fwd(proj3, cs3, sn3, sel_a, sel_b, rider=None):
    b_dim, s_dim, _ = proj3.shape
    scale = HEAD_DIM ** -0.5
    n_pat = len(DIL_PATTERNS)
    extra, extra_shapes, extra_sems = _rider_parts(rider)
    n_w = len(extra)
    n_steps = b_dim * PAIRS

    def body(*refs):
        q_ref, k_ref, v_ref, cs_ref, sn_ref, sa_ref, sb_ref = refs[:7]
        o16_ref, o32_ref, l_ref = refs[7 + n_w:10 + n_w]
        qr, kr = refs[10 + 2 * n_w:12 + 2 * n_w]
        per_pattern = refs[12 + 2 * n_w:12 + 2 * n_w + 2 * n_pat]
        og, lg = per_pattern[:n_pat], per_pattern[n_pat:]
        step = pl.program_id(0) * PAIRS + pl.program_id(1)
        begin, end = _rider_hooks(rider, refs[7:7 + n_w], refs[10 + n_w:10 + 2 * n_w], refs[-2:], step, n_steps)
        begin()
        h1, h2 = _head_masks()
        cur_ok, prev_ok = _tri_masks()
        sa, sb = sa_ref[...], sb_ref[...]

        def prep(j, _):
            rows = pl.ds(pl.multiple_of(j * DIL_CHUNK, DIL_CHUNK), DIL_CHUNK)
            cs, sn = cs_ref[rows, :], sn_ref[rows, :]
            qr[rows, :] = _rotate(q_ref[rows, :], cs, sn, sa, sb) * scale
            kr[rows, :] = _rotate(k_ref[rows, :], cs, sn, sa, sb)
            return 0

        lax.fori_loop(0, s_dim // DIL_CHUNK, prep, 0)

        for g, (_, dil) in enumerate(DIL_PATTERNS):
            nb = s_dim // dil // BLOCK

            def some(bi, _, g=g, dil=dil, nb=nb):
                tiles = [_dil_tile(bi * DIL_BATCH + t, dil, nb) for t in range(DIL_BATCH)]
                rows = [t[0] for t in tiles]
                q1, q2 = _split_heads(jnp.stack([qr[rw, :] for rw in rows]), h1, h2)
                kc = jnp.stack([kr[rw, :] for rw in rows]).astype(BF16)
                vc1, vc2 = _split_heads(jnp.stack([v_ref[rw, :] for rw in rows]), h1, h2)
                if nb > 1:
                    kp = jnp.stack([kr[t[1], :] for t in tiles]).astype(BF16)
                    vp1, vp2 = _split_heads(jnp.stack([v_ref[t[1], :] for t in tiles]), h1, h2)
                    p_ok = jnp.stack([jnp.logical_and(prev_ok, t[2]) for t in tiles])

                def head(qh, vch, vph):
                    sc = jnp.where(cur_ok, lax.dot_general(qh, kc, BNT, preferred_element_type=F32), -jnp.inf)
                    m = jnp.max(sc, axis=-1, keepdims=True)
                    if nb > 1:
                        sp = jnp.where(p_ok, lax.dot_general(qh, kp, BNT, preferred_element_type=F32), -jnp.inf)
                        m = jnp.maximum(m, jnp.max(sp, axis=-1, keepdims=True))
                    pc = jnp.exp(sc - m)
                    den = jnp.sum(pc, axis=-1, keepdims=True)
                    acc = lax.dot_general(pc.astype(BF16), vch, BNN, preferred_element_type=F32)
                    if nb > 1:
                        pp = jnp.exp(sp - m)
                        den = den + jnp.sum(pp, axis=-1, keepdims=True)
                        acc = acc + lax.dot_general(pp.astype(BF16), vph, BNN, preferred_element_type=F32)
                    return acc / den, m + jnp.log(den)

                o1, l1 = head(q1, vc1, vp1 if nb > 1 else None)
                o2, l2 = head(q2, vc2, vp2 if nb > 1 else None)
                o, l = o1 + o2, jnp.where(h1, l1, l2)
                for t, rw in enumerate(rows):
                    og[g][rw, :] = o[t]
                    lg[g][rw, :] = l[t]
                return 0

            lax.fori_loop(0, dil * nb // DIL_BATCH, some, 0)

        def comb(j, _):
            rows = pl.ds(pl.multiple_of(j * DIL_CHUNK, DIL_CHUNK), DIL_CHUNK)
            ls = [lg[g][rows, :] for g in range(n_pat)]
            m = jnp.maximum(jnp.maximum(ls[0], ls[1]), ls[2])
            es = [jnp.exp(l - m) for l in ls]
            den = es[0] + es[1] + es[2]
            o = (es[0] * og[0][rows, :] + es[1] * og[1][rows, :] + es[2] * og[2][rows, :]) / den
            o16_ref[rows, :] = o.astype(BF16)
            o32_ref[rows, :] = o
            l_ref[rows, :] = m + jnp.log(den)
            return 0

        lax.fori_loop(0, s_dim // DIL_CHUNK, comb, 0)
        end()

    col, tab, vec = _dil_specs(b_dim, s_dim)
    out = pl.BlockSpec((None, s_dim, LANES),lambda b, h: (b, 0, h))
    shp = (b_dim, s_dim, ATT_WIDTH)
    res = pl.pallas_call(
        body, name="dil_fwd", grid=(b_dim, PAIRS),
        in_specs=[col(COL_QA), col(COL_KA), col(COL_VA), tab, tab, vec, vec] + [ANY] * n_w,
        out_specs=[out, out, out] + [ANY] * n_w,
        out_shape=[jax.ShapeDtypeStruct(shp, BF16), jax.ShapeDtypeStruct(shp, F32), jax.ShapeDtypeStruct(shp, F32)]
        + extra_shapes,
        scratch_shapes=[pltpu.VMEM((s_dim, LANES), F32)] * (2 + 2 * n_pat) + extra_sems,
        compiler_params=_params(("arbitrary", "arbitrary")),
    )(proj3, proj3, proj3, cs3, sn3, sel_a, sel_b, *extra)
    return res[:3], res[3:]


def _dil_bwd(proj3, cs3, sn3, sel_a, sel_b, do3, o3, lse3, rider=None):
    b_dim, s_dim, _ = proj3.shape
    scale = HEAD_DIM ** -0.5
    extra, extra_shapes, extra_sems = _rider_parts(rider)
    n_w = len(extra)

    def body(*refs):
        q_ref, k_ref, v_ref, cs_ref, sn_ref, sa_ref, sb_ref, do_ref, o_ref, l_ref = refs[:10]
        dq_ref, dk_ref, dv_ref = refs[10 + n_w:13 + n_w]
        qr, kr, dqa, dka, dva = refs[13 + 2 * n_w:18 + 2 * n_w]
        step = pl.program_id(0) * PAIRS + pl.program_id(1)
        begin, end = _rider_hooks(rider, refs[10:10 + n_w], refs[13 + n_w:13 + 2 * n_w], refs[-2:], step,
                                  b_dim * PAIRS)
        begin()
        h1, h2 = _head_masks()
        cur_ok, prev_ok = _tri_masks()
        sa, sb = sa_ref[...], sb_ref[...]

        def prep(j, _):
            rows = pl.ds(pl.multiple_of(j * DIL_CHUNK, DIL_CHUNK), DIL_CHUNK)
            cs, sn = cs_ref[rows, :], sn_ref[rows, :]
            qr[rows, :] = _rotate(q_ref[rows, :], cs, sn, sa, sb) * scale
            kr[rows, :] = _rotate(k_ref[rows, :], cs, sn, sa, sb)
            zero = jnp.zeros((DIL_CHUNK, LANES), F32)
            dqa[rows, :] = zero
            dka[rows, :] = zero
            dva[rows, :] = zero
            return 0

        lax.fori_loop(0, s_dim // DIL_CHUNK, prep, 0)

        for _, dil in DIL_PATTERNS:
            nb = s_dim // dil // BLOCK

            def some(bi, _, dil=dil, nb=nb):
                tiles = [_dil_tile(bi * DIL_BATCH + t, dil, nb) for t in range(DIL_BATCH)]
                rows = [t[0] for t in tiles]
                q1, q2 = _split_heads(jnp.stack([qr[rw, :] for rw in rows]), h1, h2)
                dof = jnp.stack([do_ref[rw, :] for rw in rows])
                do1, do2 = _split_heads(dof, h1, h2)
                prod = dof * jnp.stack([o_ref[rw, :] for rw in rows])
                delta1 = jnp.sum(jnp.where(h1, prod, 0.0), axis=-1, keepdims=True)
                delta2 = jnp.sum(jnp.where(h2, prod, 0.0), axis=-1, keepdims=True)
                lt = jnp.stack([l_ref[rw, :] for rw in rows])
                lse1 = jnp.max(jnp.where(h1, lt, -jnp.inf), axis=-1, keepdims=True)
                lse2 = jnp.max(jnp.where(h2, lt, -jnp.inf), axis=-1, keepdims=True)

                def side(krows, ok):
                    kf = jnp.stack([kr[kw, :] for kw in krows])
                    k16 = kf.astype(BF16)
                    k1, k2 = _split_heads(kf, h1, h2)
                    v16 = jnp.stack([v_ref[kw, :] for kw in krows]).astype(BF16)

                    def head(qh, doh, lse, delta):
                        sc = lax.dot_general(qh, k16, BNT, preferred_element_type=F32)
                        p = jnp.where(ok, jnp.exp(sc - lse), 0.0)
                        dp = lax.dot_general(doh, v16, BNT, preferred_element_type=F32)
                        return p.astype(BF16), (p * (dp - delta)).astype(BF16)

                    p1, ds1 = head(q1, do1, lse1, delta1)
                    p2, ds2 = head(q2, do2, lse2, delta2)
                    dv = (lax.dot_general(p1, do1, BTN, preferred_element_type=F32)
                          + lax.dot_general(p2, do2, BTN, preferred_element_type=F32))
                    dk = (lax.dot_general(ds1, q1, BTN, preferred_element_type=F32)
                          + lax.dot_general(ds2, q2, BTN, preferred_element_type=F32))
                    for t, kw in enumerate(krows):
                        dva[kw, :] += dv[t]
                        dka[kw, :] += dk[t]
                    return (lax.dot_general(ds1, k1, BNN, preferred_element_type=F32)
                            + lax.dot_general(ds2, k2, BNN, preferred_element_type=F32))

                dq = side(rows, cur_ok)
                if nb > 1:
                    dq = dq + side([t[1] for t in tiles], jnp.stack([jnp.logical_and(prev_ok, t[2]) for t in tiles]))
                for t, rw in enumerate(rows):
                    dqa[rw, :] += dq[t] * scale
                return 0

            lax.fori_loop(0, dil * nb // DIL_BATCH, some, 0)

        def finish(j, _):
            rows = pl.ds(pl.multiple_of(j * DIL_CHUNK, DIL_CHUNK), DIL_CHUNK)
            cs, sn = cs_ref[rows, :], -sn_ref[rows, :]
            dq_ref[rows, :] = _rotate(dqa[rows, :], cs, sn, sa, sb).astype(BF16)
            dk_ref[rows, :] = _rotate(dka[rows, :], cs, sn, sa, sb).astype(BF16)
            dv_ref[rows, :] = dva[rows, :].astype(BF16)
            return 0

        lax.fori_loop(0, s_dim // DIL_CHUNK, finish, 0)
        end()

    col, tab, vec = _dil_specs(b_dim, s_dim)
    out = pl.BlockSpec((None, s_dim, LANES),lambda b, h: (b, 0, h))
    shp = jax.ShapeDtypeStruct((b_dim, s_dim, ATT_WIDTH), BF16)
    acc = pltpu.VMEM((s_dim, LANES), F32)
    res = pl.pallas_call(
        body, name="dil_bwd", grid=(b_dim, PAIRS),
        in_specs=[col(COL_QA), col(COL_KA), col(COL_VA), tab, tab, vec, vec, out, out, out] + [ANY] * n_w,
        out_specs=[out, out, out] + [ANY] * n_w, out_shape=[shp, shp, shp] + extra_shapes,
        scratch_shapes=[acc, acc, acc, acc, acc] + extra_sems,
        compiler_params=_params(("arbitrary", "arbitrary")),
    )(proj3, proj3, proj3, cs3, sn3, sel_a, sel_b, do3, o3, lse3, *extra)
    return res[:3], res[3:]


def _split_dot(x, tri):
    hi = x.astype(BF16)
    lo = (x - hi.astype(F32)).astype(BF16)
    return jnp.dot(hi, tri, preferred_element_type=F32) + jnp.dot(lo, tri, preferred_element_type=F32)


def _log_sigmoid(z):
    return jnp.minimum(z, 0.0) - jnp.log(1.0 + jnp.exp(-jnp.abs(z)))


def _sb_scores(qh, k16, valid):
    z = lax.dot_general(qh, k16, NT, preferred_element_type=F32)
    ls = _log_sigmoid(z)
    l1m = ls - z
    return ls, (l1m if valid is None else jnp.where(valid, l1m, 0.0))


def _sb_consts():
    r = lax.broadcasted_iota(jnp.int32, (BLOCK, BLOCK), 0)
    c = lax.broadcasted_iota(jnp.int32, (BLOCK, BLOCK), 1)
    after = (r > c).astype(BF16)
    before = (r < c).astype(BF16)
    qrow = lax.broadcasted_iota(jnp.int32, (SB_ROWS, BLOCK), 0)
    kcol = lax.broadcasted_iota(jnp.int32, (SB_ROWS, BLOCK), 1)
    return after, before, qrow, kcol


def _below(whole, lo, delta):
    if lo == 0:
        return whole + delta
    return whole + jnp.concatenate([jnp.zeros((lo,) + delta.shape[1:], delta.dtype), delta], axis=0)


def _pairs_loop(n_blocks, step, carry):
    def several(i, c):
        for j in range(SB_STEP):
            c = step(SB_STEP * i + j, c)
        return c

    return lax.fori_loop(0, n_blocks // SB_STEP, several, carry)


def _sb_fwd(proj3, rider=None):
    b_dim, s_dim, _ = proj3.shape
    scale = HEAD_DIM ** -0.5
    per = SB_ROWS // BLOCK
    extra, extra_shapes, extra_sems = _rider_parts(rider)
    n_w = len(extra)

    def body(*refs):
        q_ref, k_ref, v_ref = refs[:3]
        o_ref = refs[3 + n_w]
        step = pl.program_id(0) * PAIRS + pl.program_id(1)
        begin, end = _rider_hooks(rider, refs[3:3 + n_w], refs[4 + n_w:4 + 2 * n_w], refs[-2:], step, b_dim * PAIRS)
        begin()
        h1, h2 = _head_masks()
        after, _, qrow, kcol = _sb_consts()

        def qloop(qi, _):
            rows = pl.ds(pl.multiple_of(qi * SB_ROWS, SB_ROWS), SB_ROWS)
            q1, q2 = _split_heads(q_ref[rows, :] * scale, h1, h2)
            first = qi * per

            def block(kb, carry, lo):
                acc, run1, run2 = carry
                krows = pl.ds(pl.multiple_of(kb * BLOCK, BLOCK), BLOCK)
                k16 = k_ref[krows, :].astype(BF16)
                v1, v2 = _split_heads(v_ref[krows, :], h1, h2)
                valid = None if lo is None else kcol[:SB_ROWS - lo] < qrow[:SB_ROWS - lo]
                lo = lo or 0

                def head(qh, vh, run):
                    ls, l1m = _sb_scores(qh[lo:], k16, valid)
                    a = jnp.exp(ls + _split_dot(l1m, after) + run[lo:])
                    if valid is not None:
                        a = jnp.where(valid, a, 0.0)
                    return (jnp.dot(a.astype(BF16), vh, preferred_element_type=F32),
                            _below(run, lo, jnp.sum(l1m, axis=-1, keepdims=True)))

                o1, run1 = head(q1, v1, run1)
                o2, run2 = head(q2, v2, run2)
                return _below(acc, lo, o1 + o2), run1, run2

            zcol = jnp.zeros((SB_ROWS, 1), F32)
            carry = (jnp.zeros((SB_ROWS, LANES), F32), zcol, zcol)
            for kl in reversed(range(per)):
                carry = block(first + kl, carry, kl * BLOCK)
            acc, _, _ = _pairs_loop(first, lambda i, c: block(first - 1 - i, c, None), carry)
            o_ref[rows, :] = acc.astype(BF16)
            return 0

        lax.fori_loop(0, s_dim // SB_ROWS, qloop, 0)
        end()

    def col(c0):
        return pl.BlockSpec((None, s_dim, LANES),lambda b, h: (b, 0, c0 + h))

    res = pl.pallas_call(
        body, name="sb_fwd", grid=(b_dim, PAIRS),
        in_specs=[col(COL_QB), col(COL_KB), col(COL_VB)] + [ANY] * n_w, out_specs=[col(0)] + [ANY] * n_w,
        out_shape=[jax.ShapeDtypeStruct((b_dim, s_dim, ATT_WIDTH), BF16)] + extra_shapes,
        scratch_shapes=extra_sems,
        compiler_params=_params(("arbitrary", "arbitrary")),
    )(proj3, proj3, proj3, *extra)
    return res[0], res[1:]


def _sb_bwd(proj3, do3, rider=None):
    b_dim, s_dim, _ = proj3.shape
    scale = HEAD_DIM ** -0.5
    per = SB_ROWS // BLOCK
    nkb_max = s_dim // BLOCK
    extra, extra_shapes, extra_sems = _rider_parts(rider)
    n_w = len(extra)

    def body(*refs):
        q_ref, k_ref, v_ref, do_ref = refs[:4]
        dq_ref, dk_ref, dv_ref = refs[4 + n_w:7 + n_w]
        dka, dva, e_ref, sg_ref = refs[7 + 2 * n_w:11 + 2 * n_w]
        step = pl.program_id(0) * PAIRS + pl.program_id(1)
        begin, end = _rider_hooks(rider, refs[4:4 + n_w], refs[7 + n_w:7 + 2 * n_w], refs[-2:], step, b_dim * PAIRS)
        begin()
        h1, h2 = _head_masks()
        after, before, qrow, kcol = _sb_consts()
        dka[...] = jnp.zeros_like(dka)
        dva[...] = jnp.zeros_like(dva)

        def qloop(qi, _):
            rows = pl.ds(pl.multiple_of(qi * SB_ROWS, SB_ROWS), SB_ROWS)
            q1, q2 = _split_heads(q_ref[rows, :] * scale, h1, h2)
            do1, do2 = _split_heads(do_ref[rows, :].astype(F32), h1, h2)
            first = qi * per

            def pass1(kb, carry, lo):
                run1, run2 = carry
                krows = pl.ds(pl.multiple_of(kb * BLOCK, BLOCK), BLOCK)
                k16 = k_ref[krows, :].astype(BF16)
                v16 = v_ref[krows, :].astype(BF16)
                valid = None if lo is None else kcol[:SB_ROWS - lo] < qrow[:SB_ROWS - lo]
                lo = lo or 0
                part = pl.ds(lo, SB_ROWS - lo)

                def head(h, qh, doh, run):
                    ls, l1m = _sb_scores(qh[lo:], k16, valid)
                    a = jnp.exp(ls + _split_dot(l1m, after) + run[lo:])
                    if valid is not None:
                        a = jnp.where(valid, a, 0.0)
                    da = lax.dot_general(doh[lo:], v16, NT, preferred_element_type=F32)
                    e_ref[h, kb, part, :] = a * da
                    sg_ref[h, kb, part, :] = jnp.exp(ls)
                    return a.astype(BF16), _below(run, lo, jnp.sum(l1m, axis=-1, keepdims=True))

                a1, run1 = head(0, q1, do1, run1)
                a2, run2 = head(1, q2, do2, run2)
                dva[krows, :] += (lax.dot_general(a1, do1[lo:], TN, preferred_element_type=F32)
                                  + lax.dot_general(a2, do2[lo:], TN, preferred_element_type=F32))
                return run1, run2

            zcol = jnp.zeros((SB_ROWS, 1), F32)
            carry = (zcol, zcol)
            for kl in reversed(range(per)):
                carry = pass1(first + kl, carry, kl * BLOCK)
            _pairs_loop(first, lambda i, c: pass1(first - 1 - i, c, None), carry)

            def pass2(kb, carry, lo):
                dq, pre1, pre2 = carry
                krows = pl.ds(pl.multiple_of(kb * BLOCK, BLOCK), BLOCK)
                k1, k2 = _split_heads(k_ref[krows, :], h1, h2)
                valid = None if lo is None else kcol[:SB_ROWS - lo] < qrow[:SB_ROWS - lo]
                lo = lo or 0
                part = pl.ds(lo, SB_ROWS - lo)

                def head(h, pre):
                    ev = e_ref[h, kb, part, :]
                    sg = sg_ref[h, kb, part, :]
                    dz = ev * (1.0 - sg) - (_split_dot(ev, before) + pre[lo:]) * sg
                    if valid is not None:
                        dz = jnp.where(valid, dz, 0.0)
                    return dz.astype(BF16), _below(pre, lo, jnp.sum(ev, axis=-1, keepdims=True))

                dz1, pre1 = head(0, pre1)
                dz2, pre2 = head(1, pre2)
                dka[krows, :] += (lax.dot_general(dz1, q1[lo:], TN, preferred_element_type=F32)
                                  + lax.dot_general(dz2, q2[lo:], TN, preferred_element_type=F32))
                dq = _below(dq, lo, jnp.dot(dz1, k1, preferred_element_type=F32)
                            + jnp.dot(dz2, k2, preferred_element_type=F32))
                return dq, pre1, pre2

            carry = _pairs_loop(first, lambda i, c: pass2(i, c, None), (jnp.zeros((SB_ROWS, LANES), F32), zcol, zcol))
            for kl in range(per):
                carry = pass2(first + kl, carry, kl * BLOCK)
            dq = carry[0]
            dq_ref[rows, :] = (dq * scale).astype(BF16)
            return 0

        lax.fori_loop(0, s_dim // SB_ROWS, qloop, 0)
        dk_ref[...] = dka[...].astype(BF16)
        dv_ref[...] = dva[...].astype(BF16)
        end()

    def col(c0):
        return pl.BlockSpec((None, s_dim, LANES),lambda b, h: (b, 0, c0 + h))

    shp = jax.ShapeDtypeStruct((b_dim, s_dim, ATT_WIDTH), BF16)
    acc = pltpu.VMEM((s_dim, LANES), F32)
    strip = pltpu.VMEM((2, nkb_max, SB_ROWS, BLOCK), F32)
    res = pl.pallas_call(
        body, name="sb_bwd", grid=(b_dim, PAIRS),
        in_specs=[col(COL_QB), col(COL_KB), col(COL_VB), col(0)] + [ANY] * n_w,
        out_specs=[col(0), col(0), col(0)] + [ANY] * n_w,
        out_shape=[shp, shp, shp] + extra_shapes,
        scratch_shapes=[acc, acc, strip, strip] + extra_sems,
        compiler_params=_params(("arbitrary", "arbitrary")),
    )(proj3, proj3, proj3, do3, *extra)
    return res[:3], res[3:]


def _sigmoid(x):
    return 1.0 / (1.0 + jnp.exp(-x))


def _gate_out_norm(proj, ua, ub, w_out, x, g, *, tt=512):
    t_dim, d = ua.shape

    def body(ga_ref, gb_ref, ua_ref, ub_ref, w_ref, x_ref, g_ref, m_ref, h_ref, n_ref):
        mixed = (_sigmoid(ga_ref[...]) * ua_ref[...] + _sigmoid(gb_ref[...]) * ub_ref[...]).astype(BF16)
        m_ref[...] = mixed
        hv = x_ref[...] + jnp.dot(mixed, w_ref[...], preferred_element_type=F32)
        h_ref[...] = hv
        r = lax.rsqrt(jnp.mean(hv * hv, axis=-1, keepdims=True) + RMS_EPS)
        n_ref[...] = ((hv * r) * g_ref[...]).astype(BF16)

    row = pl.BlockSpec((tt, d), lambda i: (i, 0))
    return pl.pallas_call(
        body, name="gate_out_norm", grid=(t_dim // tt,),
        in_specs=[pl.BlockSpec((tt, d), lambda i: (i, 3)), pl.BlockSpec((tt, d), lambda i: (i, 4)), row, row,
                  pl.BlockSpec((d, d), lambda i: (0, 0)), row, pl.BlockSpec((1, d), lambda i: (0, 0))],
        out_specs=[row, row, row],
        out_shape=[jax.ShapeDtypeStruct((t_dim, d), BF16), jax.ShapeDtypeStruct((t_dim, d), F32),
                   jax.ShapeDtypeStruct((t_dim, d), BF16)],
        compiler_params=_params(("parallel",)),
    )(proj, proj, ua, ub, w_out, x, g)


def _out_dx_gate_bwd(dh, w_out, proj, ua, ub, *, tt=512):
    t_dim, d = ua.shape

    def body(dh_ref, w_ref, ga_ref, gb_ref, ua_ref, ub_ref, dua_ref, dub_ref, dg_ref):
        dm = lax.dot_general(dh_ref[...].astype(BF16), w_ref[...], NT, preferred_element_type=F32)
        sa = _sigmoid(ga_ref[...])
        sb = _sigmoid(gb_ref[...])
        dua_ref[...] = (dm * sa).astype(BF16)
        dub_ref[...] = (dm * sb).astype(BF16)
        dg_ref[:, :d] = (dm * ua_ref[...] * (sa * (1.0 - sa))).astype(BF16)
        dg_ref[:, d:] = (dm * ub_ref[...] * (sb * (1.0 - sb))).astype(BF16)

    row = pl.BlockSpec((tt, d), lambda i: (i, 0))
    wide = pl.BlockSpec((tt, 2 * d), lambda i: (i, 0))
    return pl.pallas_call(
        body, name="out_dx_gate_bwd", grid=(t_dim // tt,),
        in_specs=[row, pl.BlockSpec((d, d), lambda i: (0, 0)),
                  pl.BlockSpec((tt, d), lambda i: (i, 3)), pl.BlockSpec((tt, d), lambda i: (i, 4)), row, row],
        out_specs=[row, row, wide],
        out_shape=[jax.ShapeDtypeStruct((t_dim, d), BF16), jax.ShapeDtypeStruct((t_dim, d), BF16),
                   jax.ShapeDtypeStruct((t_dim, 2 * d), BF16)],
        compiler_params=_params(("parallel",)),
    )(dh, w_out, proj, proj, ua, ub)


def _ffn_up_swiglu(n, wg3, wu3, *, tt=1024):
    t_dim, d = n.shape
    n_s, f4, _ = wg3.shape

    def body(n_ref, wg_ref, wu_ref, g_ref, u_ref, a_ref):
        nv = n_ref[...]
        gv = lax.dot_general(nv, wg_ref[...], NT, preferred_element_type=F32)
        uv = lax.dot_general(nv, wu_ref[...], NT, preferred_element_type=F32)
        g_ref[...] = gv.astype(BF16)
        u_ref[...] = uv.astype(BF16)
        a_ref[...] = (gv * _sigmoid(gv) * uv).astype(BF16)

    wspec = pl.BlockSpec((None, f4, d), lambda i, s: (s, 0, 0))
    ospec = pl.BlockSpec((None, tt, f4), lambda i, s: (s, i, 0))
    shp = (n_s, t_dim, f4)
    return pl.pallas_call(
        body, name="ffn_up_swiglu", grid=(t_dim // tt, n_s),
        in_specs=[pl.BlockSpec((tt, d), lambda i, s: (i, 0)), wspec, wspec], out_specs=[ospec, ospec, ospec],
        out_shape=[jax.ShapeDtypeStruct(shp, BF16)] * 3,
        compiler_params=_params(("parallel", "parallel")),
    )(n, wg3, wu3)


def _ffn_down_dx_swiglu(dh, wd3, g3, u3, *, tt=1024):
    t_dim, d = dh.shape
    n_s, f4, _ = wd3.shape

    def body(dh_ref, w_ref, g_ref, u_ref, dg_ref, du_ref):
        da = lax.dot_general(dh_ref[...].astype(BF16), w_ref[...], NT, preferred_element_type=F32)
        gv = g_ref[...].astype(F32)
        sg = _sigmoid(gv)
        dg_ref[...] = (da * u_ref[...].astype(F32) * (sg + gv * sg * (1.0 - sg))).astype(BF16)
        du_ref[...] = (da * (gv * sg)).astype(BF16)

    spec = pl.BlockSpec((None, tt, f4), lambda i, s: (s, i, 0))
    shp = jax.ShapeDtypeStruct((n_s, t_dim, f4), BF16)
    return pl.pallas_call(
        body, name="ffn_down_dx_swiglu", grid=(t_dim // tt, n_s),
        in_specs=[pl.BlockSpec((tt, d), lambda i, s: (i, 0)), pl.BlockSpec((None, f4, d), lambda i, s: (s, 0, 0)),
                  spec, spec],
        out_specs=[spec, spec], out_shape=[shp, shp],
        compiler_params=_params(("parallel", "parallel")),
    )(dh, wd3, g3, u3)


def _mem_fwd(qm, kvm, *, tt=2048):
    b_dim, s_dim, _ = qm.shape
    n_mem = kvm.shape[1]
    scale = MEM_HEAD_DIM ** -0.5

    def body(q_ref, k_ref, v_ref, o_ref):
        sc = lax.dot_general(q_ref[0], k_ref[0], NT, preferred_element_type=F32) * scale
        p = jnp.exp(sc - jnp.max(sc, axis=-1, keepdims=True))
        p = p / jnp.sum(p, axis=-1, keepdims=True)
        o_ref[0] = jnp.dot(p.astype(BF16), v_ref[0], preferred_element_type=F32).astype(BF16)

    qs = pl.BlockSpec((1, tt, MEM_HEAD_DIM), lambda b, h, i: (b, i, h))
    return pl.pallas_call(
        body, name="mem_fwd", grid=(b_dim, N_HEADS_MEM, s_dim // tt),
        in_specs=[qs, pl.BlockSpec((1, n_mem, MEM_HEAD_DIM), lambda b, h, i: (b, 0, h)),
                  pl.BlockSpec((1, n_mem, MEM_HEAD_DIM), lambda b, h, i: (b, 0, N_HEADS_MEM + h))],
        out_specs=qs, out_shape=jax.ShapeDtypeStruct(qm.shape, BF16),
        compiler_params=_params(("parallel", "parallel", "parallel")),
    )(qm, kvm, kvm)


def _mem_bwd(qm, kvm, dom, *, tt=2048):
    b_dim, s_dim, _ = qm.shape
    n_mem = kvm.shape[1]
    scale = MEM_HEAD_DIM ** -0.5

    def body(q_ref, k_ref, v_ref, do_ref, dq_ref, dk_ref, dv_ref):
        qv, kv, vv, dov = q_ref[0], k_ref[0], v_ref[0], do_ref[0]
        sc = lax.dot_general(qv, kv, NT, preferred_element_type=F32) * scale
        p = jnp.exp(sc - jnp.max(sc, axis=-1, keepdims=True))
        p = p / jnp.sum(p, axis=-1, keepdims=True)
        dp = lax.dot_general(dov, vv, NT, preferred_element_type=F32)
        ds = (p * (dp - jnp.sum(p * dp, axis=-1, keepdims=True)) * scale).astype(BF16)
        dq_ref[0] = jnp.dot(ds, kv, preferred_element_type=F32).astype(BF16)

        @pl.when(pl.program_id(2) == 0)
        def _():
            dk_ref[...] = jnp.zeros_like(dk_ref)
            dv_ref[...] = jnp.zeros_like(dv_ref)

        dk_ref[0] += lax.dot_general(ds, qv, TN, preferred_element_type=F32)
        dv_ref[0] += lax.dot_general(p.astype(BF16), dov, TN, preferred_element_type=F32)

    qs = pl.BlockSpec((1, tt, MEM_HEAD_DIM), lambda b, h, i: (b, i, h))
    ks = pl.BlockSpec((1, n_mem, MEM_HEAD_DIM), lambda b, h, i: (b, 0, h))
    vs = pl.BlockSpec((1, n_mem, MEM_HEAD_DIM), lambda b, h, i: (b, 0, N_HEADS_MEM + h))
    return pl.pallas_call(
        body, name="mem_bwd", grid=(b_dim, N_HEADS_MEM, s_dim // tt),
        in_specs=[qs, ks, vs, qs], out_specs=[qs, ks, ks],
        out_shape=[jax.ShapeDtypeStruct(qm.shape, BF16), jax.ShapeDtypeStruct((b_dim, n_mem, MEM_WIDTH), F32),
                   jax.ShapeDtypeStruct((b_dim, n_mem, MEM_WIDTH), F32)],
        compiler_params=_params(("parallel", "parallel", "arbitrary")),
    )(qm, kvm, kvm, dom)


def _adamw_math(wv, gv, mv, vv):
    nm = ADAM_B1 * mv + (1.0 - ADAM_B1) * gv
    nv = ADAM_B2 * vv + (1.0 - ADAM_B2) * (gv * gv)
    m_hat = nm / (1.0 - ADAM_B1 ** ADAM_STEP)
    v_hat = nv / (1.0 - ADAM_B2 ** ADAM_STEP)
    return -ADAM_LR * (m_hat / (jnp.sqrt(v_hat) + ADAM_EPS) + ADAM_WD * wv), nm, nv


def _adamw(w, g, m, v, *, name):
    rows, cols = w.shape
    tr = _tile(rows, 256, 8)

    def body(w_ref, g_ref, m_ref, v_ref, d_ref, nm_ref, nv_ref):
        d_ref[...], nm_ref[...], nv_ref[...] = _adamw_math(w_ref[...], g_ref[...], m_ref[...], v_ref[...])

    spec = pl.BlockSpec((tr, cols), lambda i: (i, 0))
    shp = jax.ShapeDtypeStruct((rows, cols), F32)
    return pl.pallas_call(
        body, name=name, grid=(rows // tr,),
        in_specs=[spec] * 4, out_specs=[spec] * 3, out_shape=[shp] * 3,
        compiler_params=_params(("parallel",)),
    )(w, g, m, v)


def _prefetch_spec(grid, in_specs, out_specs):
    return pltpu.PrefetchScalarGridSpec(num_scalar_prefetch=1, grid=grid, in_specs=in_specs, out_specs=out_specs)


def _adamw_halves(w, mine, theirs, m, v, c_idx, *, name):
    rows, cols = w.shape
    half = rows // 2
    tr = _tile(half, _row_cap(cols), 8)
    nh = half // tr

    def body(c_ref, w_ref, mine_ref, theirs_ref, m_ref, v_ref, g_ref, d_ref, nm_ref, nv_ref):
        gv = jnp.where(pl.program_id(0) == c_ref[0], mine_ref[...], theirs_ref[...])
        g_ref[...] = gv
        d_ref[...], nm_ref[...], nv_ref[...] = _adamw_math(w_ref[...], gv, m_ref[...], v_ref[...])

    full = pl.BlockSpec((tr, cols), lambda h, i, c_ref: (h * nh + i, 0))
    part = pl.BlockSpec((tr, cols), lambda h, i, c_ref: (i, 0))
    shp = jax.ShapeDtypeStruct((rows, cols), F32)
    return pl.pallas_call(
        body, name=name, grid_spec=_prefetch_spec((2, nh), [full, part, part, full, full], [full] * 4),
        out_shape=[shp] * 4,
        compiler_params=_params(("parallel", "parallel")),
    )(c_idx, w, mine, theirs, m, v)


def _pair_sum(g3, theirs, c_idx, *, name):
    n, rows, cols = g3.shape
    half = rows // 2
    tr = _tile(half, _row_cap(cols), 16)

    def body(c_ref, g_ref, t_ref, o_ref):
        o_ref[...] = (g_ref[...] + t_ref[...]).astype(BF16)

    part = pl.BlockSpec((None, tr, cols), lambda s, i, c_ref: (s, i, 0))
    return pl.pallas_call(
        body, name=name,
        grid_spec=_prefetch_spec((n, half // tr),
                                 [pl.BlockSpec((None, None, tr, cols), lambda s, i, c_ref: (s, c_ref[0], i, 0)), part],
                                 part),
        out_shape=jax.ShapeDtypeStruct((n, half, cols), BF16),
        compiler_params=_params(("parallel", "parallel")),
    )(c_idx, g3.reshape(n, 2, half, cols), theirs)


def _chip_sum(pair, recv, s_idx, *, name):
    _, half, cols = pair.shape
    tr = _tile(half, _row_cap(cols), 16)

    def body(s_ref, p_ref, r_ref, o_ref):
        o_ref[...] = ((p_ref[...].astype(F32) + r_ref[0].astype(F32)) + r_ref[1].astype(F32)) + r_ref[2].astype(F32)

    return pl.pallas_call(
        body, name=name,
        grid_spec=_prefetch_spec((half // tr,),
                                 [pl.BlockSpec((None, tr, cols), lambda i, s_ref: (s_ref[0], i, 0)),
                                  pl.BlockSpec((N_CHIPS - 1, tr, cols), lambda i, s_ref: (0, i, 0))],
                                 pl.BlockSpec((tr, cols), lambda i, s_ref: (i, 0))),
        out_shape=jax.ShapeDtypeStruct((half, cols), F32),
        compiler_params=_params(("parallel",)),
    )(s_idx, pair, recv)


def _sum8(parts):
    n, rows, cols = parts.shape

    def body(p_ref, o_ref):
        acc = p_ref[0]
        for i in range(1, n):
            acc = acc + p_ref[i]
        o_ref[...] = acc

    return pl.pallas_call(
        body, name="small_sum", grid=(1,),
        in_specs=[pl.BlockSpec((n, rows, cols), lambda i: (0, 0, 0))],
        out_specs=pl.BlockSpec((rows, cols), lambda i: (0, 0)),
        out_shape=jax.ShapeDtypeStruct((rows, cols), parts.dtype),
        compiler_params=_params(("arbitrary",)),
    )(parts)


def _place():
    return lax.axis_index("x"), lax.axis_index("y"), lax.axis_index("c")


ANY = pl.BlockSpec(memory_space=pl.ANY)


def _rider_parts(rider):
    if rider is None:
        return (), [], []
    kind, arrays = rider
    n = len(arrays)
    shapes = {"gather": _gathered_shapes, "pair": _pair_shapes, "chip": _chip_shapes}[kind](arrays)
    sems = _gather_sems(n) if kind == "gather" else _exchange_sems(n if kind == "pair" else 3 * n)
    return tuple(arrays), shapes, sems


def _rider_hooks(rider, ins, outs, sems, step, n_steps):
    if rider is None:
        return (lambda: None), (lambda: None)
    if rider[0] == "gather":
        start, forward, finish = _gather_steps(ins, outs, *sems)
    else:
        start, finish = {"pair": _pair_steps, "chip": _chip_steps}[rider[0]](ins, outs, *sems)
        forward = None

    def begin():
        pl.when(step == 0)(start)

    def end():
        if forward is not None:
            pl.when(step == n_steps - 2)(forward)
        pl.when(step == n_steps - 1)(finish)

    return begin, end


def _gathered_shapes(shards):
    return [jax.ShapeDtypeStruct((N_CHIPS,) + s.shape, s.dtype) for s in shards]


def _gather_sems(n):
    return [pltpu.SemaphoreType.DMA((7 * n,)), pltpu.SemaphoreType.DMA((7 * n,))]


def _gather_steps(ins, outs, send_sems, recv_sems):
    n = len(ins)
    halves = [r.shape[0] // 2 for r in ins]
    x, y, c = _place()
    my_chip = 2 * x + y
    me, sibling = (x, y, c), (x, y, 1 - c)
    chips = [(1 - x, y), (x, 1 - y), (1 - x, 1 - y)]

    def half_of(w, chip, pc):
        return outs[w].at[chip, pl.ds(pc * halves[w], halves[w]), :]

    def copy(w, k, src, dst, to):
        return pltpu.make_async_remote_copy(
            src_ref=src, dst_ref=dst, send_sem=send_sems.at[7 * w + k], recv_sem=recv_sems.at[7 * w + k],
            device_id=to, device_id_type=MESH)

    def firsts():
        cps = []
        for w in range(n):
            cps.append(copy(w, 0, ins[w], outs[w].at[my_chip], sibling))
            mine = ins[w].at[pl.ds(c * halves[w], halves[w]), :]
            for j, (px, py) in enumerate(chips):
                cps.append(copy(w, 1 + j, mine, half_of(w, my_chip, c), (px, py, c)))
        return cps

    def passes():
        return [copy(w, 4 + j, half_of(w, 2 * px + py, c), half_of(w, 2 * px + py, c), sibling)
                for w in range(n) for j, (px, py) in enumerate(chips)]

    def start():
        for cp in firsts():
            cp.start()

    def forward():
        fws = passes()
        for w in range(n):
            for j, (px, py) in enumerate(chips):
                landed = half_of(w, 2 * px + py, c)
                copy(w, 1 + j, landed, landed, me).wait_recv()
                fws[3 * w + j].start()

    def finish():
        for w in range(n):
            copy(w, 0, ins[w], outs[w].at[my_chip], me).wait_recv()
            for j, (px, py) in enumerate(chips):
                landed = half_of(w, 2 * px + py, 1 - c)
                copy(w, 4 + j, landed, landed, me).wait_recv()
        for cp in firsts() + passes():
            cp.wait_send()

    return start, forward, finish


def _pair_shapes(grads):
    return [jax.ShapeDtypeStruct((g.shape[0], g.shape[1] // 2, g.shape[2]), g.dtype) for g in grads]


def _exchange_sems(n):
    return [pltpu.SemaphoreType.DMA((n,)), pltpu.SemaphoreType.DMA((n,))]


def _exchange_steps(copies):
    def start():
        for cp in copies():
            cp.start()

    def finish():
        for cp in copies():
            cp.wait()

    return start, finish


def _pair_steps(ins, outs, send_sems, recv_sems):
    x, y, c = _place()

    def copies():
        return [pltpu.make_async_remote_copy(
            src_ref=ins[w].at[:, pl.ds((1 - c) * (ins[w].shape[1] // 2), ins[w].shape[1] // 2), :], dst_ref=outs[w],
            send_sem=send_sems.at[w], recv_sem=recv_sems.at[w], device_id=(x, y, 1 - c), device_id_type=MESH)
            for w in range(len(ins))]

    return _exchange_steps(copies)


def _chip_shapes(pairs):
    return [jax.ShapeDtypeStruct((N_CHIPS - 1,) + p.shape[1:], p.dtype) for p in pairs]


def _chip_steps(ins, outs, send_sems, recv_sems):
    x, y, c = _place()
    others = [(1 - x, y), (x, 1 - y), (1 - x, 1 - y)]

    def copies():
        return [pltpu.make_async_remote_copy(
            src_ref=ins[w].at[2 * px + py], dst_ref=outs[w].at[j],
            send_sem=send_sems.at[3 * w + j], recv_sem=recv_sems.at[3 * w + j],
            device_id=(px, py, c), device_id_type=MESH)
            for w in range(len(ins)) for j, (px, py) in enumerate(others)]

    return _exchange_steps(copies)


def _swap_halves(mine):
    n = len(mine)

    def body(*refs):
        ins, outs, send_sems, recv_sems = refs[:n], refs[n:2 * n], refs[2 * n], refs[2 * n + 1]
        x, y, c = _place()
        copies = [pltpu.make_async_remote_copy(
            src_ref=ins[w], dst_ref=outs[w], send_sem=send_sems.at[w], recv_sem=recv_sems.at[w],
            device_id=(x, y, 1 - c), device_id_type=MESH) for w in range(n)]
        for cp in copies:
            cp.start()
        for cp in copies:
            cp.wait()

    return pl.pallas_call(
        body, name="grad_swap_halves",
        out_shape=[jax.ShapeDtypeStruct(h.shape, h.dtype) for h in mine],
        in_specs=[ANY] * n, out_specs=[ANY] * n,
        scratch_shapes=[pltpu.SemaphoreType.DMA((n,)), pltpu.SemaphoreType.DMA((n,))],
    )(*mine)


def _gather_small(small):
    srows, cols = small.shape

    def body(s_ref, all_ref, send_sems, recv_sems, local_sem):
        x, y, c = _place()
        me = 4 * x + 2 * y + c
        keep_small = pltpu.make_async_copy(s_ref, all_ref.at[me], local_sem)
        keep_small.start()
        sends = []
        for kk in range(1, 8):
            peer = (x ^ (kk >> 2), y ^ ((kk >> 1) & 1), c ^ (kk & 1))
            sends.append(pltpu.make_async_remote_copy(
                src_ref=s_ref, dst_ref=all_ref.at[me],
                send_sem=send_sems.at[kk], recv_sem=recv_sems.at[kk], device_id=peer, device_id_type=MESH))
        for cp in sends:
            cp.start()
        for kk in range(1, 8):
            px, py, pc = x ^ (kk >> 2), y ^ ((kk >> 1) & 1), c ^ (kk & 1)
            pltpu.make_async_remote_copy(
                src_ref=s_ref, dst_ref=all_ref.at[4 * px + 2 * py + pc],
                send_sem=send_sems.at[kk], recv_sem=recv_sems.at[kk], device_id=(px, py, pc),
                device_id_type=MESH).wait_recv()
        for cp in sends:
            cp.wait_send()
        keep_small.wait()

    return pl.pallas_call(
        body, name="gather_small",
        out_shape=jax.ShapeDtypeStruct((8, srows, cols), small.dtype),
        in_specs=[ANY], out_specs=ANY,
        scratch_shapes=[pltpu.SemaphoreType.DMA((8,)), pltpu.SemaphoreType.DMA((8,)), pltpu.SemaphoreType.DMA],
    )(small)


SHARDED = (("w_in", D_MODEL, IN_COLS, 1), ("w_up_a", ATT_WIDTH, D_MODEL, 1), ("w_up_b", ATT_WIDTH, D_MODEL, 1),
           ("w_out", D_MODEL, D_MODEL, 0), ("w_q_mem", D_MODEL, MEM_WIDTH, 0), ("w_kv_mem", D_MODEL, 2 * MEM_WIDTH, 0),
           ("w_o_mem", MEM_WIDTH, D_MODEL, 1), ("w_ffn_gate", D_FF, D_MODEL, 0), ("w_ffn_up", D_FF, D_MODEL, 0),
           ("w_ffn_down", D_FF, D_MODEL, 0))
TRANSPOSED = ("w_ffn_gate", "w_ffn_up")
NAMES = tuple(n for n, _, _, _ in SHARDED)


def _held(name, shard):
    return shard.T if name in TRANSPOSED else shard
EARLY, LATE = NAMES[:1], NAMES[1:]
GAINS = ("g_mix", "g_mem_q", "g_mem_kv", "g_ffn", "g_final")


def _natural(w3):
    n, r, c = w3.shape
    return w3.reshape(n * r, c)


def _shard_major(g, axis):
    if axis == 1:
        return g
    r, c = g.shape
    return g.reshape(N_CHIPS, r // N_CHIPS, c)


def kernel(x, mem, positions, g_mix, w_in, w_up_a, w_up_b, w_out, g_mem_q, g_mem_kv, w_q_mem, w_kv_mem, w_o_mem, g_ffn, w_ffn_gate, w_ffn_up, w_ffn_down, g_final, loss_target, m_g_mix, m_w_in, m_w_up_a, m_w_up_b, m_w_out, m_g_mem_q, m_g_mem_kv, m_w_q_mem, m_w_kv_mem, m_w_o_mem, m_g_ffn, m_w_ffn_gate, m_w_ffn_up, m_w_ffn_down, m_g_final, v_g_mix, v_w_in, v_w_up_a, v_w_up_b, v_w_out, v_g_mem_q, v_g_mem_kv, v_w_q_mem, v_w_kv_mem, v_w_o_mem, v_g_ffn, v_w_ffn_gate, v_w_ffn_up, v_w_ffn_down, v_g_final):
    given = dict(locals())
    shards = {n: _held(n, given[n][0]) for n in NAMES}

    early_shards = [shards[n].astype(BF16) for n in EARLY]
    late_shards = [shards[n].astype(BF16) for n in LATE]
    c_idx = lax.axis_index("c").astype(jnp.int32).reshape(1)
    s_idx = (2 * lax.axis_index("x") + lax.axis_index("y")).astype(jnp.int32).reshape(1)

    loss_row, grad_x, mine, gain_grads = _local_step(x, mem, positions, loss_target, g_mix, g_mem_q, g_mem_kv,
                                                     g_ffn, g_final, {}, early_shards, late_shards, (c_idx, s_idx))
    return _reduce_and_update(given, shards, loss_row, grad_x, mine, gain_grads, c_idx)


def _reduce_halves(glist, names, c_idx, s_idx, pair_exchange, chip_exchange):
    theirs = pair_exchange(glist)
    pairs = [_pair_sum(g, t, c_idx, name="pair_sum_" + n) for n, g, t in zip(names, glist, theirs)]
    recv = chip_exchange(pairs)
    return [_chip_sum(p, r, s_idx, name="chip_sum_" + n) for n, p, r in zip(names, pairs, recv)]


def _local_step(x, mem, positions, loss_target, g_mix, g_mem_q, g_mem_kv, g_ffn, g_final, wf,
                early_shards=None, late_shards=None, place=None):
    b_dim, s_dim, d = x.shape
    t_dim = b_dim * s_dim
    n_mem = mem.shape[1]
    wf = dict(wf)

    xb = x.reshape(t_dim, d)
    tgt = loss_target.reshape(t_dim, d)
    memf = mem.reshape(b_dim * n_mem, d)
    gfin = g_final.reshape(1, d)
    pos = positions.reshape(t_dim, 1).astype(F32)

    lane = jnp.arange(LANES) % HEAD_DIM
    half = ROPE_DIM // 2
    inv_freq = ROPE_THETA ** (-jnp.arange(half, dtype=F32) / half)
    inv_lane = jnp.where(lane < ROPE_DIM, inv_freq[lane % half], 0.0).reshape(1, -1).astype(F32)
    sel_a = (lane < half).astype(F32).reshape(1, -1)
    sel_b = ((lane >= half) & (lane < ROPE_DIM)).astype(F32).reshape(1, -1)

    def rows3(t):
        return t.reshape(b_dim, s_dim, t.shape[-1])

    def rows2(t):
        return t.reshape(t_dim, t.shape[-1])

    if early_shards:
        n1, gathered = _rms_fwd(xb, g_mix, name="rms_mix", rider=("gather", early_shards))
        wf.update(zip(EARLY, gathered))
    else:
        n1 = _rms_fwd(xb, g_mix, name="rms_mix")
    proj = _mm_cs(n1, wf["w_in"], name="mm_in")
    proj3 = rows3(proj)
    cs, sn = _rope_table(pos, inv_lane, sel_a, sel_b)
    cs3, sn3 = rows3(cs), rows3(sn)
    (oa16, oa32, lse_a), _ = _dil_fwd(proj3, cs3, sn3, sel_a, sel_b)
    ob16, gathered = _sb_fwd(proj3, ("gather", late_shards) if late_shards else None)
    wf.update(zip(LATE, gathered))
    w_out, w_q, w_kv = _natural(wf["w_out"]), _natural(wf["w_q_mem"]), _natural(wf["w_kv_mem"])
    oa, ob = rows2(oa16), rows2(ob16)
    ua = _mm_cs(oa, wf["w_up_a"], name="mm_up_a")
    ub = _mm_cs(ob, wf["w_up_b"], name="mm_up_b")
    mixed, h1, hn = _gate_out_norm(proj, ua, ub, w_out, xb, g_mem_q)

    memn = _rms_fwd(memf, g_mem_kv, name="rms_mem_kv")
    qm = _mm(hn, w_q, name="mm_q_mem", out_dtype=BF16)
    kvm = _mm(memn, w_kv, name="mm_kv_mem", out_dtype=BF16)
    qm3, kvm3 = rows3(qm), kvm.reshape(b_dim, n_mem, 2 * MEM_WIDTH)
    om = rows2(_mem_fwd(qm3, kvm3))
    h2 = _mm_cs(om, wf["w_o_mem"], name="mm_o_mem", add=h1)

    n3 = _rms_fwd(h2, g_ffn, name="rms_ffn")
    gate3, up3, act3 = _ffn_up_swiglu(n3, wf["w_ffn_gate"], wf["w_ffn_up"])
    loss_row, dh3, dg_final = _down_final(act3, wf["w_ffn_down"], h2, gfin, tgt)

    grads = {}
    grads["w_ffn_down"] = _mm_ffn_down_dw(act3, dh3, name="mm_down_dw")
    dgate3, dup3 = _ffn_down_dx_swiglu(dh3, wf["w_ffn_down"], gate3, up3)
    grads["w_ffn_gate"] = _mm_ffn_down_dw(dgate3, n3, name="mm_gate_dw")
    grads["w_ffn_up"] = _mm_ffn_down_dw(dup3, n3, name="mm_up_dw")
    dn3 = _mm_ffn_rows(dgate3, wf["w_ffn_gate"], name="mm_gate_dx")
    dn3 = _mm_ffn_rows(dup3, wf["w_ffn_up"], name="mm_up_dx", add=dn3)
    dh2, dg_ffn = _rms_bwd(h2, g_ffn, dn3, dh3, name="rms_ffn_bwd")

    dom = _mm_cs_dx(dh2, wf["w_o_mem"], name="mm_o_mem_dx", out_dtype=BF16)
    grads["w_o_mem"] = _mm_cs_dw(om, dh2, name="mm_o_mem_dw")
    dqm, dkm, dvm = _mem_bwd(qm3, kvm3, rows3(dom))
    dqm = rows2(dqm)
    dkvm = jnp.concatenate([dkm, dvm], axis=-1).reshape(b_dim * n_mem, 2 * MEM_WIDTH).astype(BF16)
    grads["w_q_mem"] = _shard_major(_mm(hn, dqm, name="mm_q_mem_dw", ta=True), 0)
    dhn = _mm(dqm, w_q, name="mm_q_mem_dx", tb=True)
    grads["w_kv_mem"] = _shard_major(_mm(memn, dkvm, name="mm_kv_mem_dw", ta=True), 0)
    dmemn = _mm(dkvm, w_kv, name="mm_kv_mem_dx", tb=True)
    _, dg_mem_kv = _rms_bwd(memf, g_mem_kv, dmemn, None, name="rms_mem_kv_bwd")
    dh1, dg_mem_q = _rms_bwd(h1, g_mem_q, dhn, dh2, name="rms_mem_q_bwd")

    grads["w_out"] = _shard_major(_mm(mixed, dh1, name="mm_out_dw", ta=True), 0)
    dua, dub, dgates = _out_dx_gate_bwd(dh1, w_out, proj, ua, ub)
    doa = _mm_cs_dx(dua, wf["w_up_a"], name="mm_up_a_dx")
    grads["w_up_a"] = _mm_cs_dw(oa, dua, name="mm_up_a_dw")
    dob = _mm_cs_dx(dub, wf["w_up_b"], name="mm_up_b_dx", out_dtype=BF16)
    grads["w_up_b"] = _mm_cs_dw(ob, dub, name="mm_up_b_dw")

    att = {}

    def dil_with_pairs(glist):
        att["a"], theirs = _dil_bwd(proj3, cs3, sn3, sel_a, sel_b, rows3(doa), oa32, lse_a,
                                    ("pair", glist) if glist else None)
        return theirs

    def sb_with_chips(pairs):
        att["b"], recv = _sb_bwd(proj3, rows3(dob), ("chip", pairs) if pairs else None)
        return recv

    if place is None:
        dil_with_pairs(())
        sb_with_chips(())
    else:
        mine_late = _reduce_halves([grads[n] for n in LATE], LATE, *place, dil_with_pairs, sb_with_chips)
    dproj = jnp.concatenate([rows2(t) for t in att["a"] + att["b"]] + [dgates], axis=1)
    grads["w_in"] = _mm_cs_dw(n1, dproj, name="mm_in_dw")
    if place is None:
        dn1 = _mm_cs_dx(dproj, wf["w_in"], name="mm_in_dx")
        dx, dg_mix = _rms_bwd(xb, g_mix, dn1, dh1, name="rms_mix_bwd")
    else:
        tail = {}

        def dx_with_pairs(glist):
            tail["dn1"], theirs = _mm_cs_dx(dproj, wf["w_in"], name="mm_in_dx", rider=("pair", glist))
            return theirs

        def rms_with_chips(pairs):
            tail["dx"], tail["dg"], recv = _rms_bwd(xb, g_mix, tail["dn1"], dh1, name="rms_mix_bwd",
                                                    rider=("chip", pairs))
            return recv

        mine_early = _reduce_halves([grads[n] for n in EARLY], EARLY, *place, dx_with_pairs, rms_with_chips)
        dx, dg_mix = tail["dx"], tail["dg"]
    grad_x = dx.reshape(b_dim, s_dim, d)
    gains = (dg_mix, dg_mem_q, dg_mem_kv, dg_ffn, dg_final)
    if place is None:
        return loss_row, grad_x, grads, gains
    return loss_row, grad_x, mine_early + mine_late, gains


def _reduce_and_update(given, shards, loss_row, grad_x, mine, gain_grads, c_idx):
    d = D_MODEL
    dg_mix, dg_mem_q, dg_mem_kv, dg_ffn, dg_final = gain_grads
    small = jnp.concatenate([dg_mix, dg_mem_q, dg_mem_kv, dg_ffn, dg_final,
                             jnp.pad(loss_row, ((0, 0), (0, FLAT_COLS - LANES))), jnp.zeros((2, FLAT_COLS), F32)], axis=0)
    small_all = _gather_small(small)
    others = _swap_halves(mine)
    small_sum = _sum8(small_all)
    loss = small_sum[5, 0]

    out_g, out_d, out_m, out_v = {}, {}, {}, {}
    for n, mine_n, other_n in zip(NAMES, mine, others):
        res = _adamw_halves(shards[n], mine_n, other_n, _held(n, given["m_" + n][0]), _held(n, given["v_" + n][0]),
                            c_idx, name="adamw_" + n)
        out_g[n], out_d[n], out_m[n], out_v[n] = [_held(n, r)[None] for r in res]
    gain_w = jnp.concatenate([given[n].reshape(1, d) for n in GAINS], axis=0)
    gain_m = jnp.concatenate([given["m_" + n].reshape(1, d) for n in GAINS], axis=0)
    gain_v = jnp.concatenate([given["v_" + n].reshape(1, d) for n in GAINS], axis=0)
    gain_g = small_sum[:len(GAINS)]
    gd, gm, gv = _adamw(gain_w, gain_g, gain_m, gain_v, name="adamw_gains")
    for i, n in enumerate(GAINS):
        shape = given[n].shape
        out_g[n], out_d[n] = gain_g[i].reshape(shape), gd[i].reshape(shape)
        out_m[n], out_v[n] = gm[i].reshape(shape), gv[i].reshape(shape)

    order = ["g_mix", "w_in", "w_up_a", "w_up_b", "w_out", "g_mem_q", "g_mem_kv", "w_q_mem", "w_kv_mem", "w_o_mem",
             "g_ffn", "w_ffn_gate", "w_ffn_up", "w_ffn_down", "g_final"]
    return (loss, grad_x, *[out_g[n] for n in order], *[out_d[n] for n in order],
            *[out_m[n] for n in order], *[out_v[n] for n in order])
```

```python
import jax
import jax.numpy as jnp
from jax import lax
from jax.experimental import pallas as pl
from jax.experimental.pallas import tpu as pltpu

F32 = jnp.float32
BF16 = jnp.bfloat16
MESH = pl.DeviceIdType.MESH

D_MODEL = 1024
HEAD_DIM = 64
N_HEADS = 8
ATT_WIDTH = N_HEADS * HEAD_DIM
DIL_PATTERNS = ((128, 1), (512, 4), (2048, 16))
BLOCK = 128
SB_ROWS = 1024
SB_STEP = 4
ROPE_THETA = 500000.0
ROPE_DIM = HEAD_DIM // 4
N_HEADS_MEM = 4
MEM_HEAD_DIM = 128
MEM_WIDTH = N_HEADS_MEM * MEM_HEAD_DIM
D_FF = 2816
IN_COLS = 6 * ATT_WIDTH + 2 * D_MODEL
RMS_EPS = 1e-6
ADAM_LR = 0.001
ADAM_B1 = 0.9
ADAM_B2 = 0.999
ADAM_EPS = 1e-08
ADAM_WD = 0.01
ADAM_STEP = 10

N_CHIPS = 4
LANES = 128
FLAT_COLS = 1024
VMEM_LIMIT = 56 * 1024 * 1024

PAIRS = ATT_WIDTH // LANES
COL_QA, COL_KA, COL_VA, COL_QB, COL_KB, COL_VB = (i * PAIRS for i in range(6))

MM_CAP = 1408
TOK_CAP = 2048
NN = (((1,), (0,)), ((), ()))
NT = (((1,), (1,)), ((), ()))
TN = (((0,), (0,)), ((), ()))
BNN = (((2,), (1,)), ((0,), (0,)))
BNT = (((2,), (2,)), ((0,), (0,)))
BTN = (((1,), (1,)), ((0,), (0,)))
DIL_BATCH = 16
DIL_CHUNK = 512


def _tile(dim, cap, unit=LANES):
    if dim <= cap:
        return dim
    best = None
    for t in range(unit, cap + 1, unit):
        if dim % t == 0:
            best = t
    assert best is not None, (dim, cap)
    return best


def _row_cap(cols):
    return max(256, (1 << 18) // cols)


def _params(sem):
    return pltpu.CompilerParams(dimension_semantics=sem, vmem_limit_bytes=VMEM_LIMIT)


def _mm(a, b, *, name, ta=False, tb=False, add=None, out_dtype=F32,
        tm_cap=MM_CAP, tn_cap=MM_CAP, tk_cap=MM_CAP):
    if ta:
        k_dim, m_dim = a.shape
    else:
        m_dim, k_dim = a.shape
    if tb:
        n_dim, kb = b.shape
    else:
        kb, n_dim = b.shape
    assert kb == k_dim, (a.shape, b.shape, ta, tb)
    tm, tn, tk = _tile(m_dim, tm_cap), _tile(n_dim, tn_cap), _tile(k_dim, tk_cap)
    nk = k_dim // tk
    dims = (((0 if ta else 1,), (1 if tb else 0,)), ((), ()))
    has_add = add is not None

    def body(*refs):
        if has_add:
            a_ref, b_ref, add_ref, o_ref = refs[:4]
        else:
            a_ref, b_ref, o_ref = refs[:3]
        part = lax.dot_general(a_ref[...].astype(BF16), b_ref[...].astype(BF16), dims, preferred_element_type=F32)

        def finish(r):
            if has_add:
                r = add_ref[...] + r
            o_ref[...] = r.astype(out_dtype)

        if nk == 1:
            finish(part)
            return
        acc_ref = refs[-1]
        k = pl.program_id(2)

        @pl.when(k == 0)
        def _():
            acc_ref[...] = part

        @pl.when(k > 0)
        def _():
            acc_ref[...] += part

        @pl.when(k == nk - 1)
        def _():
            finish(acc_ref[...])

    a_spec = pl.BlockSpec((tk, tm), lambda i, j, k: (k, i)) if ta else pl.BlockSpec((tm, tk), lambda i, j, k: (i, k))
    b_spec = pl.BlockSpec((tn, tk), lambda i, j, k: (j, k)) if tb else pl.BlockSpec((tk, tn), lambda i, j, k: (k, j))
    o_spec = pl.BlockSpec((tm, tn), lambda i, j, k: (i, j))
    in_specs = [a_spec, b_spec] + ([o_spec] if has_add else [])
    args = (a, b) + ((add,) if has_add else ())
    return pl.pallas_call(
        body, name=name, grid=(m_dim // tm, n_dim // tn, nk),
        in_specs=in_specs, out_specs=o_spec,
        out_shape=jax.ShapeDtypeStruct((m_dim, n_dim), out_dtype),
        scratch_shapes=[pltpu.VMEM((tm, tn), F32)] if nk > 1 else [],
        compiler_params=_params(("parallel", "parallel", "arbitrary")),
    )(*args)


def _mm_core(name, a, b, a_spec, b_spec, o_spec, out_shape, grid, dims, *, add=None, out_dtype=F32, rider=None):
    nk = grid[2]
    has_add = add is not None
    n_in = 3 if has_add else 2
    acc_shape = tuple(d for d in o_spec.block_shape if d is not None)
    extra, extra_shapes, extra_sems = _rider_parts(rider)
    n_w = len(extra)

    def body(*refs):
        a_ref, b_ref = refs[:2]
        o_ref = refs[n_in + n_w]
        step = (pl.program_id(0) * grid[1] + pl.program_id(1)) * nk + pl.program_id(2)
        begin, end = _rider_hooks(rider, refs[n_in:n_in + n_w], refs[n_in + n_w + 1:n_in + 2 * n_w + 1], refs[-2:],
                                  step, grid[0] * grid[1] * nk)
        begin()
        part = lax.dot_general(a_ref[...].astype(BF16), b_ref[...].astype(BF16), dims, preferred_element_type=F32)

        def finish(r):
            if has_add:
                r = refs[2][...] + r
            o_ref[...] = r.astype(out_dtype)

        if nk == 1:
            finish(part)
        else:
            acc_ref = refs[n_in + 2 * n_w + 1]
            k = pl.program_id(2)

            @pl.when(k == 0)
            def _():
                acc_ref[...] = part

            @pl.when(k > 0)
            def _():
                acc_ref[...] += part

            @pl.when(k == nk - 1)
            def _():
                finish(acc_ref[...])
        end()

    in_specs = [a_spec, b_spec] + ([o_spec] if has_add else []) + [ANY] * n_w
    args = (a, b) + ((add,) if has_add else ()) + extra
    res = pl.pallas_call(
        body, name=name, grid=grid, in_specs=in_specs, out_specs=[o_spec] + [ANY] * n_w,
        out_shape=[jax.ShapeDtypeStruct(out_shape, out_dtype)] + extra_shapes,
        scratch_shapes=([pltpu.VMEM(acc_shape, F32)] if nk > 1 else []) + extra_sems,
        compiler_params=_params(("arbitrary",) * 3 if n_w else ("parallel", "parallel", "arbitrary")),
    )(*args)
    return (res[0], res[1:]) if n_w else res[0]


def _mm_cs(a, w3, *, name, add=None, out_dtype=F32):
    m_dim, k_dim = a.shape
    _, _, n4 = w3.shape
    tm, tn, tk = _tile(m_dim, MM_CAP if add is not None else TOK_CAP), _tile(n4, MM_CAP), _tile(k_dim, MM_CAP)
    npb = n4 // tn
    return _mm_core(name, a, w3,
                    pl.BlockSpec((tm, tk), lambda i, j, k: (i, k)),
                    pl.BlockSpec((None, tk, tn), lambda i, j, k: (j // npb, k, j % npb)),
                    pl.BlockSpec((tm, tn), lambda i, j, k: (i, j)),
                    (m_dim, N_CHIPS * n4), (m_dim // tm, N_CHIPS * npb, k_dim // tk), NN, add=add, out_dtype=out_dtype)


def _mm_cs_dx(dy, w3, *, name, out_dtype=F32, rider=None):
    m_dim, _ = dy.shape
    _, k_dim, n4 = w3.shape
    tm, tkw, tn = _tile(m_dim, MM_CAP), _tile(k_dim, MM_CAP), _tile(n4, MM_CAP)
    npb = n4 // tn
    return _mm_core(name, dy, w3,
                    pl.BlockSpec((tm, tn), lambda i, j, k: (i, k)),
                    pl.BlockSpec((None, tkw, tn), lambda i, j, k: (k // npb, j, k % npb)),
                    pl.BlockSpec((tm, tkw), lambda i, j, k: (i, j)),
                    (m_dim, k_dim), (m_dim // tm, k_dim // tkw, N_CHIPS * npb), NT, out_dtype=out_dtype, rider=rider)


def _mm_cs_dw(a, dy, *, name):
    m_dim, k_dim = a.shape
    n4 = dy.shape[1] // N_CHIPS
    tmk, tn, tk = _tile(k_dim, MM_CAP), _tile(n4, MM_CAP), _tile(m_dim, TOK_CAP)
    npb = n4 // tn
    return _mm_core(name, a, dy,
                    pl.BlockSpec((tk, tmk), lambda i, j, k: (k, i)),
                    pl.BlockSpec((tk, tn), lambda i, j, k: (k, j)),
                    pl.BlockSpec((None, tmk, tn), lambda i, j, k: (j // npb, i, j % npb)),
                    (N_CHIPS, k_dim, n4), (k_dim // tmk, N_CHIPS * npb, m_dim // tk), TN)


def _mm_sm(a, w3, *, name, add=None, out_dtype=F32, tt=1024):
    t_dim, k_dim = a.shape
    n_s, _, n4 = w3.shape
    has_add = add is not None

    def body(*refs):
        a_ref, w_ref, o_ref = refs[0], refs[1], refs[-1]
        av = a_ref[...].astype(BF16)
        for s in range(n_s):
            cols = slice(s * n4, (s + 1) * n4)
            r = jnp.dot(av, w_ref[s], preferred_element_type=F32)
            if has_add:
                r = refs[2][:, cols] + r
            o_ref[:, cols] = r.astype(out_dtype)

    row = pl.BlockSpec((tt, n_s * n4), lambda i: (i, 0))
    return pl.pallas_call(
        body, name=name, grid=(t_dim // tt,),
        in_specs=[pl.BlockSpec((tt, k_dim), lambda i: (i, 0)), pl.BlockSpec(w3.shape, lambda i: (0, 0, 0))]
        + ([row] if has_add else []),
        out_specs=row, out_shape=jax.ShapeDtypeStruct((t_dim, n_s * n4), out_dtype),
        compiler_params=_params(("parallel",)),
    )(*((a, w3) + ((add,) if has_add else ())))


def _mm_sm_dx(dy, w3, *, name, out_dtype=F32, tt=1024):
    t_dim, _ = dy.shape
    n_s, k_dim, n4 = w3.shape

    def body(dy_ref, w_ref, o_ref):
        dyv = dy_ref[...].astype(BF16)
        acc = lax.dot_general(dyv[:, :n4], w_ref[0], NT, preferred_element_type=F32)
        for s in range(1, n_s):
            acc = acc + lax.dot_general(dyv[:, s * n4:(s + 1) * n4], w_ref[s], NT, preferred_element_type=F32)
        o_ref[...] = acc.astype(out_dtype)

    return pl.pallas_call(
        body, name=name, grid=(t_dim // tt,),
        in_specs=[pl.BlockSpec((tt, n_s * n4), lambda i: (i, 0)), pl.BlockSpec(w3.shape, lambda i: (0, 0, 0))],
        out_specs=pl.BlockSpec((tt, k_dim), lambda i: (i, 0)),
        out_shape=jax.ShapeDtypeStruct((t_dim, k_dim), out_dtype),
        compiler_params=_params(("parallel",)),
    )(dy, w3)


def _mm_sm_dw(a, dy, *, name, tk=1024):
    t_dim, k_dim = a.shape
    n4 = dy.shape[1] // N_CHIPS

    def body(a_ref, dy_ref, o_ref):
        av = a_ref[...].astype(BF16)
        dyv = dy_ref[...].astype(BF16)

        @pl.when(pl.program_id(0) == 0)
        def _():
            o_ref[...] = jnp.zeros_like(o_ref)

        for s in range(N_CHIPS):
            o_ref[s] += lax.dot_general(av, dyv[:, s * n4:(s + 1) * n4], TN, preferred_element_type=F32)

    return pl.pallas_call(
        body, name=name, grid=(t_dim // tk,),
        in_specs=[pl.BlockSpec((tk, k_dim), lambda i: (i, 0)), pl.BlockSpec((tk, N_CHIPS * n4), lambda i: (i, 0))],
        out_specs=pl.BlockSpec((N_CHIPS, k_dim, n4), lambda i: (0, 0, 0)),
        out_shape=jax.ShapeDtypeStruct((N_CHIPS, k_dim, n4), F32),
        compiler_params=_params(("arbitrary",)),
    )(a, dy)


def _mm_ffn_rows(a3, w3, *, name, add=None):
    _, t_dim, f4 = a3.shape
    _, _, d = w3.shape
    tm = _tile(t_dim, MM_CAP)
    return _mm_core(name, a3, w3,
                    pl.BlockSpec((None, tm, f4), lambda i, j, k: (k, i, 0)),
                    pl.BlockSpec((None, f4, d), lambda i, j, k: (k, 0, 0)),
                    pl.BlockSpec((tm, d), lambda i, j, k: (i, 0)),
                    (t_dim, d), (t_dim // tm, 1, N_CHIPS), NN, add=add)


def _mm_ffn_down_dw(act3, dh, *, name):
    _, t_dim, f4 = act3.shape
    d = dh.shape[1]
    tk = _tile(t_dim, TOK_CAP)
    return _mm_core(name, act3, dh,
                    pl.BlockSpec((None, tk, f4), lambda i, j, k: (i, k, 0)),
                    pl.BlockSpec((tk, d), lambda i, j, k: (k, 0)),
                    pl.BlockSpec((None, f4, d), lambda i, j, k: (i, 0, 0)),
                    (N_CHIPS, f4, d), (N_CHIPS, 1, t_dim // tk), TN)


def _rms_fwd(x, g, *, name, tt=512, rider=None):
    t_dim, d = x.shape
    tt = _tile(t_dim, tt, 8)
    extra, extra_shapes, extra_sems = _rider_parts(rider)
    n_w = len(extra)

    def body(*refs):
        x_ref, g_ref, o_ref = refs[0], refs[1], refs[2 + n_w]
        begin, end = _rider_hooks(rider, refs[2:2 + n_w], refs[3 + n_w:3 + 2 * n_w], refs[-2:], pl.program_id(0),
                                  t_dim // tt)
        begin()
        xv = x_ref[...]
        r = lax.rsqrt(jnp.mean(xv * xv, axis=-1, keepdims=True) + RMS_EPS)
        o_ref[...] = ((xv * r) * g_ref[...]).astype(o_ref.dtype)
        end()

    res = pl.pallas_call(
        body, name=name, grid=(t_dim // tt,),
        in_specs=[pl.BlockSpec((tt, d), lambda i: (i, 0)), pl.BlockSpec((1, d), lambda i: (0, 0))] + [ANY] * n_w,
        out_specs=[pl.BlockSpec((tt, d), lambda i: (i, 0))] + [ANY] * n_w,
        out_shape=[jax.ShapeDtypeStruct((t_dim, d), BF16)] + extra_shapes,
        scratch_shapes=extra_sems,
        compiler_params=_params(("arbitrary",) if n_w else ("parallel",)),
    )(x, g, *extra)
    return (res[0], res[1:]) if n_w else res[0]


def _rms_bwd(x, g, dy, add, *, name, tt=512, rider=None):
    t_dim, d = x.shape
    tt = _tile(t_dim, tt, 8)
    has_add = add is not None
    n_in = 4 if has_add else 3
    extra, extra_shapes, extra_sems = _rider_parts(rider)
    n_w = len(extra)

    def body(*refs):
        x_ref, g_ref, dy_ref = refs[:3]
        add_ref = refs[3] if has_add else None
        dx_ref, dg_ref = refs[n_in + n_w:n_in + n_w + 2]
        begin, end = _rider_hooks(rider, refs[n_in:n_in + n_w], refs[n_in + n_w + 2:n_in + 2 * n_w + 2], refs[-2:],
                                  pl.program_id(0), t_dim // tt)
        begin()
        xv = x_ref[...]
        dyv = dy_ref[...].astype(F32)
        r = lax.rsqrt(jnp.mean(xv * xv, axis=-1, keepdims=True) + RMS_EPS)
        xh = xv * r
        u = dyv * g_ref[...]
        dx = r * (u - xh * jnp.mean(u * xh, axis=-1, keepdims=True))
        if has_add:
            dx = add_ref[...] + dx
        dx_ref[...] = dx

        @pl.when(pl.program_id(0) == 0)
        def _():
            dg_ref[...] = jnp.zeros_like(dg_ref)

        dg_ref[...] += jnp.sum(dyv * xh, axis=0, keepdims=True)
        end()

    row = pl.BlockSpec((tt, d), lambda i: (i, 0))
    vec = pl.BlockSpec((1, d), lambda i: (0, 0))
    in_specs = [row, vec, row] + ([row] if has_add else []) + [ANY] * n_w
    args = (x, g, dy) + ((add,) if has_add else ()) + extra
    res = pl.pallas_call(
        body, name=name, grid=(t_dim // tt,),
        in_specs=in_specs, out_specs=[row, vec] + [ANY] * n_w,
        out_shape=[jax.ShapeDtypeStruct((t_dim, d), F32), jax.ShapeDtypeStruct((1, d), F32)] + extra_shapes,
        scratch_shapes=extra_sems,
        compiler_params=_params(("arbitrary",)),
    )(*args)
    return (res[0], res[1], res[2:]) if n_w else (res[0], res[1])


def _down_final(act3, wd3, h, g, target, *, tt=1024):
    n_s, t_dim, f4 = act3.shape
    d = h.shape[1]
    n_steps = t_dim // tt

    def body(a_ref, w_ref, h_ref, g_ref, t_ref, loss_ref, dh_ref, dg_ref, acc_ref, sq_ref):
        i, k = pl.program_id(0), pl.program_id(1)
        part = jnp.dot(a_ref[...], w_ref[...], preferred_element_type=F32)

        @pl.when(k == 0)
        def _():
            acc_ref[...] = part

        @pl.when(k > 0)
        def _():
            acc_ref[...] += part

        @pl.when(jnp.logical_and(i == 0, k == 0))
        def _():
            dg_ref[...] = jnp.zeros_like(dg_ref)
            sq_ref[...] = jnp.zeros_like(sq_ref)

        @pl.when(k == n_s - 1)
        def _():
            xv = h_ref[...] + acc_ref[...]
            gv = g_ref[...]
            r = lax.rsqrt(jnp.mean(xv * xv, axis=-1, keepdims=True) + RMS_EPS)
            xh = xv * r
            err = xh * gv - t_ref[...]
            dyv = err * (1.0 / d)
            u = dyv * gv
            dh_ref[...] = r * (u - xh * jnp.mean(u * xh, axis=-1, keepdims=True))
            dg_ref[...] += jnp.sum(dyv * xh, axis=0, keepdims=True)
            sq_ref[...] += jnp.sum(err * err, axis=0, keepdims=True)

        @pl.when(jnp.logical_and(i == n_steps - 1, k == n_s - 1))
        def _():
            total = jnp.sum(sq_ref[...], axis=-1, keepdims=True) * (0.5 / d)
            loss_ref[...] = jnp.broadcast_to(total, loss_ref.shape)

    row = pl.BlockSpec((tt, d), lambda i, k: (i, 0))
    vec = pl.BlockSpec((1, d), lambda i, k: (0, 0))
    return pl.pallas_call(
        body, name="down_final_loss", grid=(n_steps, n_s),
        in_specs=[pl.BlockSpec((None, tt, f4), lambda i, k: (k, i, 0)),
                  pl.BlockSpec((None, f4, d), lambda i, k: (k, 0, 0)), row, vec, row],
        out_specs=[pl.BlockSpec((1, LANES), lambda i, k: (0, 0)), row, vec],
        out_shape=[jax.ShapeDtypeStruct((1, LANES), F32), jax.ShapeDtypeStruct((t_dim, d), F32),
                   jax.ShapeDtypeStruct((1, d), F32)],
        scratch_shapes=[pltpu.VMEM((tt, d), F32), pltpu.VMEM((1, d), F32)],
        compiler_params=_params(("arbitrary", "arbitrary")),
    )(act3, wd3, h, g, target)


def _rope_table(pos, inv_lane, sel_a, sel_b, *, tt=512):
    t_dim = pos.shape[0]

    def body(p_ref, f_ref, a_ref, b_ref, c_ref, s_ref):
        ang = p_ref[...] * f_ref[...]
        on = (a_ref[...] + b_ref[...]) > 0.0
        c_ref[...] = jnp.where(on, jnp.cos(ang), 1.0)
        s_ref[...] = jnp.where(on, jnp.sin(ang), 0.0)

    vec = pl.BlockSpec((1, LANES), lambda i: (0, 0))
    row = pl.BlockSpec((tt, LANES), lambda i: (i, 0))
    shp = jax.ShapeDtypeStruct((t_dim, LANES), F32)
    return pl.pallas_call(
        body, name="rope_table", grid=(t_dim // tt,),
        in_specs=[pl.BlockSpec((tt, 1), lambda i: (i, 0)), vec, vec, vec],
        out_specs=[row, row], out_shape=[shp, shp],
        compiler_params=_params(("parallel",)),
    )(pos, inv_lane, sel_a, sel_b)


def _rotate(xv, cs, sn, sa, sb):
    half = ROPE_DIM // 2
    up = pltpu.roll(xv, LANES - half, 1)
    dn = pltpu.roll(xv, half, 1)
    return xv * cs + (dn * sb - up * sa) * sn


def _head_masks():
    h1 = lax.broadcasted_iota(jnp.int32, (1, LANES), 1) < HEAD_DIM
    return h1, jnp.logical_not(h1)


def _split_heads(xv, h1, h2):
    return jnp.where(h1, xv, 0.0).astype(BF16), jnp.where(h2, xv, 0.0).astype(BF16)


def _tri_masks():
    r = lax.broadcasted_iota(jnp.int32, (BLOCK, BLOCK), 0)
    c = lax.broadcasted_iota(jnp.int32, (BLOCK, BLOCK), 1)
    return c <= r, r <= c


def _stream_rows(start, dil):
    if dil == 1:
        return pl.ds(pl.multiple_of(start, BLOCK), BLOCK)
    return pl.ds(start, BLOCK, stride=dil)


def _dil_tile(idx, dil, nb):
    r = idx // nb
    n = idx % nb
    return (_stream_rows(r + dil * BLOCK * n, dil), _stream_rows(r + dil * BLOCK * jnp.maximum(n - 1, 0), dil),
            n > 0)


def _dil_specs(b_dim, s_dim):
    def col(c0):
        return pl.BlockSpec((None, s_dim, LANES),lambda b, h: (b, 0, c0 + h))
    tab = pl.BlockSpec((None, s_dim, LANES),lambda b, h: (b, 0, 0))
    vec = pl.BlockSpec((1, LANES), lambda b, h: (0, 0))
    return col, tab, vec


def _dil_fwd(proj3, cs3, sn3, sel_a, sel_b, rider=None):
    b_dim, s_dim, _ = proj3.shape
    scale = HEAD_DIM ** -0.5
    n_pat = len(DIL_PATTERNS)
    extra, extra_shapes, extra_sems = _rider_parts(rider)
    n_w = len(extra)
    n_steps = b_dim * PAIRS

    def body(*refs):
        q_ref, k_ref, v_ref, cs_ref, sn_ref, sa_ref, sb_ref = refs[:7]
        o16_ref, o32_ref, l_ref = refs[7 + n_w:10 + n_w]
        qr, kr = refs[10 + 2 * n_w:12 + 2 * n_w]
        per_pattern = refs[12 + 2 * n_w:12 + 2 * n_w + 2 * n_pat]
        og, lg = per_pattern[:n_pat], per_pattern[n_pat:]
        step = pl.program_id(0) * PAIRS + pl.program_id(1)
        begin, end = _rider_hooks(rider, refs[7:7 + n_w], refs[10 + n_w:10 + 2 * n_w], refs[-2:], step, n_steps)
        begin()
        h1, h2 = _head_masks()
        cur_ok, prev_ok = _tri_masks()
        sa, sb = sa_ref[...], sb_ref[...]

        def prep(j, _):
            rows = pl.ds(pl.multiple_of(j * DIL_CHUNK, DIL_CHUNK), DIL_CHUNK)
            cs, sn = cs_ref[rows, :], sn_ref[rows, :]
            qr[rows, :] = _rotate(q_ref[rows, :], cs, sn, sa, sb) * scale
            kr[rows, :] = _rotate(k_ref[rows, :], cs, sn, sa, sb)
            return 0

        lax.fori_loop(0, s_dim // DIL_CHUNK, prep, 0)

        for g, (_, dil) in enumerate(DIL_PATTERNS):
            nb = s_dim // dil // BLOCK

            def some(bi, _, g=g, dil=dil, nb=nb):
                tiles = [_dil_tile(bi * DIL_BATCH + t, dil, nb) for t in range(DIL_BATCH)]
                rows = [t[0] for t in tiles]
                q1, q2 = _split_heads(jnp.stack([qr[rw, :] for rw in rows]), h1, h2)
                kc = jnp.stack([kr[rw, :] for rw in rows]).astype(BF16)
                vc1, vc2 = _split_heads(jnp.stack([v_ref[rw, :] for rw in rows]), h1, h2)
                if nb > 1:
                    kp = jnp.stack([kr[t[1], :] for t in tiles]).astype(BF16)
                    vp1, vp2 = _split_heads(jnp.stack([v_ref[t[1], :] for t in tiles]), h1, h2)
                    p_ok = jnp.stack([jnp.logical_and(prev_ok, t[2]) for t in tiles])

                def head(qh, vch, vph):
                    sc = jnp.where(cur_ok, lax.dot_general(qh, kc, BNT, preferred_element_type=F32), -jnp.inf)
                    m = jnp.max(sc, axis=-1, keepdims=True)
                    if nb > 1:
                        sp = jnp.where(p_ok, lax.dot_general(qh, kp, BNT, preferred_element_type=F32), -jnp.inf)
                        m = jnp.maximum(m, jnp.max(sp, axis=-1, keepdims=True))
                    pc = jnp.exp(sc - m)
                    den = jnp.sum(pc, axis=-1, keepdims=True)
                    acc = lax.dot_general(pc.astype(BF16), vch, BNN, preferred_element_type=F32)
                    if nb > 1:
                        pp = jnp.exp(sp - m)
                        den = den + jnp.sum(pp, axis=-1, keepdims=True)
                        acc = acc + lax.dot_general(pp.astype(BF16), vph, BNN, preferred_element_type=F32)
                    return acc / den, m + jnp.log(den)

                o1, l1 = head(q1, vc1, vp1 if nb > 1 else None)
                o2, l2 = head(q2, vc2, vp2 if nb > 1 else None)
                o, l = o1 + o2, jnp.where(h1, l1, l2)
                for t, rw in enumerate(rows):
                    og[g][rw, :] = o[t]
                    lg[g][rw, :] = l[t]
                return 0

            lax.fori_loop(0, dil * nb // DIL_BATCH, some, 0)

        def comb(j, _):
            rows = pl.ds(pl.multiple_of(j * DIL_CHUNK, DIL_CHUNK), DIL_CHUNK)
            ls = [lg[g][rows, :] for g in range(n_pat)]
            m = jnp.maximum(jnp.maximum(ls[0], ls[1]), ls[2])
            es = [jnp.exp(l - m) for l in ls]
            den = es[0] + es[1] + es[2]
            o = (es[0] * og[0][rows, :] + es[1] * og[1][rows, :] + es[2] * og[2][rows, :]) / den
            o16_ref[rows, :] = o.astype(BF16)
            o32_ref[rows, :] = o
            l_ref[rows, :] = m + jnp.log(den)
            return 0

        lax.fori_loop(0, s_dim // DIL_CHUNK, comb, 0)
        end()

    col, tab, vec = _dil_specs(b_dim, s_dim)
    out = pl.BlockSpec((None, s_dim, LANES),lambda b, h: (b, 0, h))
    shp = (b_dim, s_dim, ATT_WIDTH)
    res = pl.pallas_call(
        body, name="dil_fwd", grid=(b_dim, PAIRS),
        in_specs=[col(COL_QA), col(COL_KA), col(COL_VA), tab, tab, vec, vec] + [ANY] * n_w,
        out_specs=[out, out, out] + [ANY] * n_w,
        out_shape=[jax.ShapeDtypeStruct(shp, BF16), jax.ShapeDtypeStruct(shp, F32), jax.ShapeDtypeStruct(shp, F32)]
        + extra_shapes,
        scratch_shapes=[pltpu.VMEM((s_dim, LANES), F32)] * (2 + 2 * n_pat) + extra_sems,
        compiler_params=_params(("arbitrary", "arbitrary")),
    )(proj3, proj3, proj3, cs3, sn3, sel_a, sel_b, *extra)
    return res[:3], res[3:]


def _dil_bwd(proj3, cs3, sn3, sel_a, sel_b, do3, o3, lse3, rider=None):
    b_dim, s_dim, _ = proj3.shape
    scale = HEAD_DIM ** -0.5
    extra, extra_shapes, extra_sems = _rider_parts(rider)
    n_w = len(extra)

    def body(*refs):
        q_ref, k_ref, v_ref, cs_ref, sn_ref, sa_ref, sb_ref, do_ref, o_ref, l_ref = refs[:10]
        dq_ref, dk_ref, dv_ref = refs[10 + n_w:13 + n_w]
        qr, kr, dqa, dka, dva = refs[13 + 2 * n_w:18 + 2 * n_w]
        step = pl.program_id(0) * PAIRS + pl.program_id(1)
        begin, end = _rider_hooks(rider, refs[10:10 + n_w], refs[13 + n_w:13 + 2 * n_w], refs[-2:], step,
                                  b_dim * PAIRS)
        begin()
        h1, h2 = _head_masks()
        cur_ok, prev_ok = _tri_masks()
        sa, sb = sa_ref[...], sb_ref[...]

        def prep(j, _):
            rows = pl.ds(pl.multiple_of(j * DIL_CHUNK, DIL_CHUNK), DIL_CHUNK)
            cs, sn = cs_ref[rows, :], sn_ref[rows, :]
            qr[rows, :] = _rotate(q_ref[rows, :], cs, sn, sa, sb) * scale
            kr[rows, :] = _rotate(k_ref[rows, :], cs, sn, sa, sb)
            zero = jnp.zeros((DIL_CHUNK, LANES), F32)
            dqa[rows, :] = zero
            dka[rows, :] = zero
            dva[rows, :] = zero
            return 0

        lax.fori_loop(0, s_dim // DIL_CHUNK, prep, 0)

        for _, dil in DIL_PATTERNS:
            nb = s_dim // dil // BLOCK

            def some(bi, _, dil=dil, nb=nb):
                tiles = [_dil_tile(bi * DIL_BATCH + t, dil, nb) for t in range(DIL_BATCH)]
                rows = [t[0] for t in tiles]
                q1, q2 = _split_heads(jnp.stack([qr[rw, :] for rw in rows]), h1, h2)
                dof = jnp.stack([do_ref[rw, :] for rw in rows])
                do1, do2 = _split_heads(dof, h1, h2)
                prod = dof * jnp.stack([o_ref[rw, :] for rw in rows])
                delta1 = jnp.sum(jnp.where(h1, prod, 0.0), axis=-1, keepdims=True)
                delta2 = jnp.sum(jnp.where(h2, prod, 0.0), axis=-1, keepdims=True)
                lt = jnp.stack([l_ref[rw, :] for rw in rows])
                lse1 = jnp.max(jnp.where(h1, lt, -jnp.inf), axis=-1, keepdims=True)
                lse2 = jnp.max(jnp.where(h2, lt, -jnp.inf), axis=-1, keepdims=True)

                def side(krows, ok):
                    kf = jnp.stack([kr[kw, :] for kw in krows])
                    k16 = kf.astype(BF16)
                    k1, k2 = _split_heads(kf, h1, h2)
                    v16 = jnp.stack([v_ref[kw, :] for kw in krows]).astype(BF16)

                    def head(qh, doh, lse, delta):
                        sc = lax.dot_general(qh, k16, BNT, preferred_element_type=F32)
                        p = jnp.where(ok, jnp.exp(sc - lse), 0.0)
                        dp = lax.dot_general(doh, v16, BNT, preferred_element_type=F32)
                        return p.astype(BF16), (p * (dp - delta)).astype(BF16)

                    p1, ds1 = head(q1, do1, lse1, delta1)
                    p2, ds2 = head(q2, do2, lse2, delta2)
                    dv = (lax.dot_general(p1, do1, BTN, preferred_element_type=F32)
                          + lax.dot_general(p2, do2, BTN, preferred_element_type=F32))
                    dk = (lax.dot_general(ds1, q1, BTN, preferred_element_type=F32)
                          + lax.dot_general(ds2, q2, BTN, preferred_element_type=F32))
                    for t, kw in enumerate(krows):
                        dva[kw, :] += dv[t]
                        dka[kw, :] += dk[t]
                    return (lax.dot_general(ds1, k1, BNN, preferred_element_type=F32)
                            + lax.dot_general(ds2, k2, BNN, preferred_element_type=F32))

                dq = side(rows, cur_ok)
                if nb > 1:
                    dq = dq + side([t[1] for t in tiles], jnp.stack([jnp.logical_and(prev_ok, t[2]) for t in tiles]))
                for t, rw in enumerate(rows):
                    dqa[rw, :] += dq[t] * scale
                return 0

            lax.fori_loop(0, dil * nb // DIL_BATCH, some, 0)

        def finish(j, _):
            rows = pl.ds(pl.multiple_of(j * DIL_CHUNK, DIL_CHUNK), DIL_CHUNK)
            cs, sn = cs_ref[rows, :], -sn_ref[rows, :]
            dq_ref[rows, :] = _rotate(dqa[rows, :], cs, sn, sa, sb).astype(BF16)
            dk_ref[rows, :] = _rotate(dka[rows, :], cs, sn, sa, sb).astype(BF16)
            dv_ref[rows, :] = dva[rows, :].astype(BF16)
            return 0

        lax.fori_loop(0, s_dim // DIL_CHUNK, finish, 0)
        end()

    col, tab, vec = _dil_specs(b_dim, s_dim)
    out = pl.BlockSpec((None, s_dim, LANES),lambda b, h: (b, 0, h))
    shp = jax.ShapeDtypeStruct((b_dim, s_dim, ATT_WIDTH), BF16)
    acc = pltpu.VMEM((s_dim, LANES), F32)
    res = pl.pallas_call(
        body, name="dil_bwd", grid=(b_dim, PAIRS),
        in_specs=[col(COL_QA), col(COL_KA), col(COL_VA), tab, tab, vec, vec, out, out, out] + [ANY] * n_w,
        out_specs=[out, out, out] + [ANY] * n_w, out_shape=[shp, shp, shp] + extra_shapes,
        scratch_shapes=[acc, acc, acc, acc, acc] + extra_sems,
        compiler_params=_params(("arbitrary", "arbitrary")),
    )(proj3, proj3, proj3, cs3, sn3, sel_a, sel_b, do3, o3, lse3, *extra)
    return res[:3], res[3:]


def _split_dot(x, tri):
    hi = x.astype(BF16)
    lo = (x - hi.astype(F32)).astype(BF16)
    return jnp.dot(hi, tri, preferred_element_type=F32) + jnp.dot(lo, tri, preferred_element_type=F32)


def _log_sigmoid(z):
    return jnp.minimum(z, 0.0) - jnp.log(1.0 + jnp.exp(-jnp.abs(z)))


def _sb_scores(qh, k16, valid):
    z = lax.dot_general(qh, k16, NT, preferred_element_type=F32)
    ls = _log_sigmoid(z)
    l1m = ls - z
    return ls, (l1m if valid is None else jnp.where(valid, l1m, 0.0))


def _sb_consts():
    r = lax.broadcasted_iota(jnp.int32, (BLOCK, BLOCK), 0)
    c = lax.broadcasted_iota(jnp.int32, (BLOCK, BLOCK), 1)
    after = (r > c).astype(BF16)
    before = (r < c).astype(BF16)
    qrow = lax.broadcasted_iota(jnp.int32, (SB_ROWS, BLOCK), 0)
    kcol = lax.broadcasted_iota(jnp.int32, (SB_ROWS, BLOCK), 1)
    return after, before, qrow, kcol


def _below(whole, lo, delta):
    if lo == 0:
        return whole + delta
    return whole + jnp.concatenate([jnp.zeros((lo,) + delta.shape[1:], delta.dtype), delta], axis=0)


def _pairs_loop(n_blocks, step, carry):
    def several(i, c):
        for j in range(SB_STEP):
            c = step(SB_STEP * i + j, c)
        return c

    return lax.fori_loop(0, n_blocks // SB_STEP, several, carry)


def _sb_fwd(proj3, rider=None):
    b_dim, s_dim, _ = proj3.shape
    scale = HEAD_DIM ** -0.5
    per = SB_ROWS // BLOCK
    extra, extra_shapes, extra_sems = _rider_parts(rider)
    n_w = len(extra)

    def body(*refs):
        q_ref, k_ref, v_ref = refs[:3]
        o_ref = refs[3 + n_w]
        step = pl.program_id(0) * PAIRS + pl.program_id(1)
        begin, end = _rider_hooks(rider, refs[3:3 + n_w], refs[4 + n_w:4 + 2 * n_w], refs[-2:], step, b_dim * PAIRS)
        begin()
        h1, h2 = _head_masks()
        after, _, qrow, kcol = _sb_consts()

        def qloop(qi, _):
            rows = pl.ds(pl.multiple_of(qi * SB_ROWS, SB_ROWS), SB_ROWS)
            q1, q2 = _split_heads(q_ref[rows, :] * scale, h1, h2)
            first = qi * per

            def block(kb, carry, lo):
                acc, run1, run2 = carry
                krows = pl.ds(pl.multiple_of(kb * BLOCK, BLOCK), BLOCK)
                k16 = k_ref[krows, :].astype(BF16)
                v1, v2 = _split_heads(v_ref[krows, :], h1, h2)
                valid = None if lo is None else kcol[:SB_ROWS - lo] < qrow[:SB_ROWS - lo]
                lo = lo or 0

                def head(qh, vh, run):
                    ls, l1m = _sb_scores(qh[lo:], k16, valid)
                    a = jnp.exp(ls + _split_dot(l1m, after) + run[lo:])
                    if valid is not None:
                        a = jnp.where(valid, a, 0.0)
                    return (jnp.dot(a.astype(BF16), vh, preferred_element_type=F32),
                            _below(run, lo, jnp.sum(l1m, axis=-1, keepdims=True)))

                o1, run1 = head(q1, v1, run1)
                o2, run2 = head(q2, v2, run2)
                return _below(acc, lo, o1 + o2), run1, run2

            zcol = jnp.zeros((SB_ROWS, 1), F32)
            carry = (jnp.zeros((SB_ROWS, LANES), F32), zcol, zcol)
            for kl in reversed(range(per)):
                carry = block(first + kl, carry, kl * BLOCK)
            acc, _, _ = _pairs_loop(first, lambda i, c: block(first - 1 - i, c, None), carry)
            o_ref[rows, :] = acc.astype(BF16)
            return 0

        lax.fori_loop(0, s_dim // SB_ROWS, qloop, 0)
        end()

    def col(c0):
        return pl.BlockSpec((None, s_dim, LANES),lambda b, h: (b, 0, c0 + h))

    res = pl.pallas_call(
        body, name="sb_fwd", grid=(b_dim, PAIRS),
        in_specs=[col(COL_QB), col(COL_KB), col(COL_VB)] + [ANY] * n_w, out_specs=[col(0)] + [ANY] * n_w,
        out_shape=[jax.ShapeDtypeStruct((b_dim, s_dim, ATT_WIDTH), BF16)] + extra_shapes,
        scratch_shapes=extra_sems,
        compiler_params=_params(("arbitrary", "arbitrary")),
    )(proj3, proj3, proj3, *extra)
    return res[0], res[1:]


def _sb_bwd(proj3, do3, rider=None):
    b_dim, s_dim, _ = proj3.shape
    scale = HEAD_DIM ** -0.5
    per = SB_ROWS // BLOCK
    nkb_max = s_dim // BLOCK
    extra, extra_shapes, extra_sems = _rider_parts(rider)
    n_w = len(extra)

    def body(*refs):
        q_ref, k_ref, v_ref, do_ref = refs[:4]
        dq_ref, dk_ref, dv_ref = refs[4 + n_w:7 + n_w]
        dka, dva, e_ref, sg_ref = refs[7 + 2 * n_w:11 + 2 * n_w]
        step = pl.program_id(0) * PAIRS + pl.program_id(1)
        begin, end = _rider_hooks(rider, refs[4:4 + n_w], refs[7 + n_w:7 + 2 * n_w], refs[-2:], step, b_dim * PAIRS)
        begin()
        h1, h2 = _head_masks()
        after, before, qrow, kcol = _sb_consts()
        dka[...] = jnp.zeros_like(dka)
        dva[...] = jnp.zeros_like(dva)

        def qloop(qi, _):
            rows = pl.ds(pl.multiple_of(qi * SB_ROWS, SB_ROWS), SB_ROWS)
            q1, q2 = _split_heads(q_ref[rows, :] * scale, h1, h2)
            do1, do2 = _split_heads(do_ref[rows, :].astype(F32), h1, h2)
            first = qi * per

            def pass1(kb, carry, lo):
                run1, run2 = carry
                krows = pl.ds(pl.multiple_of(kb * BLOCK, BLOCK), BLOCK)
                k16 = k_ref[krows, :].astype(BF16)
                v16 = v_ref[krows, :].astype(BF16)
                valid = None if lo is None else kcol[:SB_ROWS - lo] < qrow[:SB_ROWS - lo]
                lo = lo or 0
                part = pl.ds(lo, SB_ROWS - lo)

                def head(h, qh, doh, run):
                    ls, l1m = _sb_scores(qh[lo:], k16, valid)
                    a = jnp.exp(ls + _split_dot(l1m, after) + run[lo:])
                    if valid is not None:
                        a = jnp.where(valid, a, 0.0)
                    da = lax.dot_general(doh[lo:], v16, NT, preferred_element_type=F32)
                    e_ref[h, kb, part, :] = a * da
                    sg_ref[h, kb, part, :] = jnp.exp(ls)
                    return a.astype(BF16), _below(run, lo, jnp.sum(l1m, axis=-1, keepdims=True))

                a1, run1 = head(0, q1, do1, run1)
                a2, run2 = head(1, q2, do2, run2)
                dva[krows, :] += (lax.dot_general(a1, do1[lo:], TN, preferred_element_type=F32)
                                  + lax.dot_general(a2, do2[lo:], TN, preferred_element_type=F32))
                return run1, run2

            zcol = jnp.zeros((SB_ROWS, 1), F32)
            carry = (zcol, zcol)
            for kl in reversed(range(per)):
                carry = pass1(first + kl, carry, kl * BLOCK)
            _pairs_loop(first, lambda i, c: pass1(first - 1 - i, c, None), carry)

            def pass2(kb, carry, lo):
                dq, pre1, pre2 = carry
                krows = pl.ds(pl.multiple_of(kb * BLOCK, BLOCK), BLOCK)
                k1, k2 = _split_heads(k_ref[krows, :], h1, h2)
                valid = None if lo is None else kcol[:SB_ROWS - lo] < qrow[:SB_ROWS - lo]
                lo = lo or 0
                part = pl.ds(lo, SB_ROWS - lo)

                def head(h, pre):
                    ev = e_ref[h, kb, part, :]
                    sg = sg_ref[h, kb, part, :]
                    dz = ev * (1.0 - sg) - (_split_dot(ev, before) + pre[lo:]) * sg
                    if valid is not None:
                        dz = jnp.where(valid, dz, 0.0)
                    return dz.astype(BF16), _below(pre, lo, jnp.sum(ev, axis=-1, keepdims=True))

                dz1, pre1 = head(0, pre1)
                dz2, pre2 = head(1, pre2)
                dka[krows, :] += (lax.dot_general(dz1, q1[lo:], TN, preferred_element_type=F32)
                                  + lax.dot_general(dz2, q2[lo:], TN, preferred_element_type=F32))
                dq = _below(dq, lo, jnp.dot(dz1, k1, preferred_element_type=F32)
                            + jnp.dot(dz2, k2, preferred_element_type=F32))
                return dq, pre1, pre2

            carry = _pairs_loop(first, lambda i, c: pass2(i, c, None), (jnp.zeros((SB_ROWS, LANES), F32), zcol, zcol))
            for kl in range(per):
                carry = pass2(first + kl, carry, kl * BLOCK)
            dq = carry[0]
            dq_ref[rows, :] = (dq * scale).astype(BF16)
            return 0

        lax.fori_loop(0, s_dim // SB_ROWS, qloop, 0)
        dk_ref[...] = dka[...].astype(BF16)
        dv_ref[...] = dva[...].astype(BF16)
        end()

    def col(c0):
        return pl.BlockSpec((None, s_dim, LANES),lambda b, h: (b, 0, c0 + h))

    shp = jax.ShapeDtypeStruct((b_dim, s_dim, ATT_WIDTH), BF16)
    acc = pltpu.VMEM((s_dim, LANES), F32)
    strip = pltpu.VMEM((2, nkb_max, SB_ROWS, BLOCK), F32)
    res = pl.pallas_call(
        body, name="sb_bwd", grid=(b_dim, PAIRS),
        in_specs=[col(COL_QB), col(COL_KB), col(COL_VB), col(0)] + [ANY] * n_w,
        out_specs=[col(0), col(0), col(0)] + [ANY] * n_w,
        out_shape=[shp, shp, shp] + extra_shapes,
        scratch_shapes=[acc, acc, strip, strip] + extra_sems,
        compiler_params=_params(("arbitrary", "arbitrary")),
    )(proj3, proj3, proj3, do3, *extra)
    return res[:3], res[3:]


def _sigmoid(x):
    return 1.0 / (1.0 + jnp.exp(-x))


def _gate_out_norm(proj, ua, ub, w_out, x, g, *, tt=512):
    t_dim, d = ua.shape

    def body(ga_ref, gb_ref, ua_ref, ub_ref, w_ref, x_ref, g_ref, m_ref, h_ref, n_ref):
        mixed = (_sigmoid(ga_ref[...]) * ua_ref[...] + _sigmoid(gb_ref[...]) * ub_ref[...]).astype(BF16)
        m_ref[...] = mixed
        hv = x_ref[...] + jnp.dot(mixed, w_ref[...], preferred_element_type=F32)
        h_ref[...] = hv
        r = lax.rsqrt(jnp.mean(hv * hv, axis=-1, keepdims=True) + RMS_EPS)
        n_ref[...] = ((hv * r) * g_ref[...]).astype(BF16)

    row = pl.BlockSpec((tt, d), lambda i: (i, 0))
    return pl.pallas_call(
        body, name="gate_out_norm", grid=(t_dim // tt,),
        in_specs=[pl.BlockSpec((tt, d), lambda i: (i, 3)), pl.BlockSpec((tt, d), lambda i: (i, 4)), row, row,
                  pl.BlockSpec((d, d), lambda i: (0, 0)), row, pl.BlockSpec((1, d), lambda i: (0, 0))],
        out_specs=[row, row, row],
        out_shape=[jax.ShapeDtypeStruct((t_dim, d), BF16), jax.ShapeDtypeStruct((t_dim, d), F32),
                   jax.ShapeDtypeStruct((t_dim, d), BF16)],
        compiler_params=_params(("parallel",)),
    )(proj, proj, ua, ub, w_out, x, g)


def _out_dx_gate_bwd(dh, w_out, proj, ua, ub, *, tt=512):
    t_dim, d = ua.shape

    def body(dh_ref, w_ref, ga_ref, gb_ref, ua_ref, ub_ref, dua_ref, dub_ref, dg_ref):
        dm = lax.dot_general(dh_ref[...].astype(BF16), w_ref[...], NT, preferred_element_type=F32)
        sa = _sigmoid(ga_ref[...])
        sb = _sigmoid(gb_ref[...])
        dua_ref[...] = (dm * sa).astype(BF16)
        dub_ref[...] = (dm * sb).astype(BF16)
        dg_ref[:, :d] = (dm * ua_ref[...] * (sa * (1.0 - sa))).astype(BF16)
        dg_ref[:, d:] = (dm * ub_ref[...] * (sb * (1.0 - sb))).astype(BF16)

    row = pl.BlockSpec((tt, d), lambda i: (i, 0))
    wide = pl.BlockSpec((tt, 2 * d), lambda i: (i, 0))
    return pl.pallas_call(
        body, name="out_dx_gate_bwd", grid=(t_dim // tt,),
        in_specs=[row, pl.BlockSpec((d, d), lambda i: (0, 0)),
                  pl.BlockSpec((tt, d), lambda i: (i, 3)), pl.BlockSpec((tt, d), lambda i: (i, 4)), row, row],
        out_specs=[row, row, wide],
        out_shape=[jax.ShapeDtypeStruct((t_dim, d), BF16), jax.ShapeDtypeStruct((t_dim, d), BF16),
                   jax.ShapeDtypeStruct((t_dim, 2 * d), BF16)],
        compiler_params=_params(("parallel",)),
    )(dh, w_out, proj, proj, ua, ub)


def _ffn_up_swiglu(n, wg3, wu3, *, tt=1024):
    t_dim, d = n.shape
    n_s, f4, _ = wg3.shape

    def body(n_ref, wg_ref, wu_ref, g_ref, u_ref, a_ref):
        nv = n_ref[...]
        gv = lax.dot_general(nv, wg_ref[...], NT, preferred_element_type=F32)
        uv = lax.dot_general(nv, wu_ref[...], NT, preferred_element_type=F32)
        g_ref[...] = gv.astype(BF16)
        u_ref[...] = uv.astype(BF16)
        a_ref[...] = (gv * _sigmoid(gv) * uv).astype(BF16)

    wspec = pl.BlockSpec((None, f4, d), lambda i, s: (s, 0, 0))
    ospec = pl.BlockSpec((None, tt, f4), lambda i, s: (s, i, 0))
    shp = (n_s, t_dim, f4)
    return pl.pallas_call(
        body, name="ffn_up_swiglu", grid=(t_dim // tt, n_s),
        in_specs=[pl.BlockSpec((tt, d), lambda i, s: (i, 0)), wspec, wspec], out_specs=[ospec, ospec, ospec],
        out_shape=[jax.ShapeDtypeStruct(shp, BF16)] * 3,
        compiler_params=_params(("parallel", "parallel")),
    )(n, wg3, wu3)


def _ffn_down_dx_swiglu(dh, wd3, g3, u3, *, tt=1024):
    t_dim, d = dh.shape
    n_s, f4, _ = wd3.shape

    def body(dh_ref, w_ref, g_ref, u_ref, dg_ref, du_ref):
        da = lax.dot_general(dh_ref[...].astype(BF16), w_ref[...], NT, preferred_element_type=F32)
        gv = g_ref[...].astype(F32)
        sg = _sigmoid(gv)
        dg_ref[...] = (da * u_ref[...].astype(F32) * (sg + gv * sg * (1.0 - sg))).astype(BF16)
        du_ref[...] = (da * (gv * sg)).astype(BF16)

    spec = pl.BlockSpec((None, tt, f4), lambda i, s: (s, i, 0))
    shp = jax.ShapeDtypeStruct((n_s, t_dim, f4), BF16)
    return pl.pallas_call(
        body, name="ffn_down_dx_swiglu", grid=(t_dim // tt, n_s),
        in_specs=[pl.BlockSpec((tt, d), lambda i, s: (i, 0)), pl.BlockSpec((None, f4, d), lambda i, s: (s, 0, 0)),
                  spec, spec],
        out_specs=[spec, spec], out_shape=[shp, shp],
        compiler_params=_params(("parallel", "parallel")),
    )(dh, wd3, g3, u3)


def _mem_fwd(qm, kvm, *, tt=2048):
    b_dim, s_dim, _ = qm.shape
    n_mem = kvm.shape[1]
    scale = MEM_HEAD_DIM ** -0.5

    def body(q_ref, k_ref, v_ref, o_ref):
        sc = lax.dot_general(q_ref[0], k_ref[0], NT, preferred_element_type=F32) * scale
        p = jnp.exp(sc - jnp.max(sc, axis=-1, keepdims=True))
        p = p / jnp.sum(p, axis=-1, keepdims=True)
        o_ref[0] = jnp.dot(p.astype(BF16), v_ref[0], preferred_element_type=F32).astype(BF16)

    qs = pl.BlockSpec((1, tt, MEM_HEAD_DIM), lambda b, h, i: (b, i, h))
    return pl.pallas_call(
        body, name="mem_fwd", grid=(b_dim, N_HEADS_MEM, s_dim // tt),
        in_specs=[qs, pl.BlockSpec((1, n_mem, MEM_HEAD_DIM), lambda b, h, i: (b, 0, h)),
                  pl.BlockSpec((1, n_mem, MEM_HEAD_DIM), lambda b, h, i: (b, 0, N_HEADS_MEM + h))],
        out_specs=qs, out_shape=jax.ShapeDtypeStruct(qm.shape, BF16),
        compiler_params=_params(("parallel", "parallel", "parallel")),
    )(qm, kvm, kvm)


def _mem_bwd(qm, kvm, dom, *, tt=2048):
    b_dim, s_dim, _ = qm.shape
    n_mem = kvm.shape[1]
    scale = MEM_HEAD_DIM ** -0.5

    def body(q_ref, k_ref, v_ref, do_ref, dq_ref, dk_ref, dv_ref):
        qv, kv, vv, dov = q_ref[0], k_ref[0], v_ref[0], do_ref[0]
        sc = lax.dot_general(qv, kv, NT, preferred_element_type=F32) * scale
        p = jnp.exp(sc - jnp.max(sc, axis=-1, keepdims=True))
        p = p / jnp.sum(p, axis=-1, keepdims=True)
        dp = lax.dot_general(dov, vv, NT, preferred_element_type=F32)
        ds = (p * (dp - jnp.sum(p * dp, axis=-1, keepdims=True)) * scale).astype(BF16)
        dq_ref[0] = jnp.dot(ds, kv, preferred_element_type=F32).astype(BF16)

        @pl.when(pl.program_id(2) == 0)
        def _():
            dk_ref[...] = jnp.zeros_like(dk_ref)
            dv_ref[...] = jnp.zeros_like(dv_ref)

        dk_ref[0] += lax.dot_general(ds, qv, TN, preferred_element_type=F32)
        dv_ref[0] += lax.dot_general(p.astype(BF16), dov, TN, preferred_element_type=F32)

    qs = pl.BlockSpec((1, tt, MEM_HEAD_DIM), lambda b, h, i: (b, i, h))
    ks = pl.BlockSpec((1, n_mem, MEM_HEAD_DIM), lambda b, h, i: (b, 0, h))
    vs = pl.BlockSpec((1, n_mem, MEM_HEAD_DIM), lambda b, h, i: (b, 0, N_HEADS_MEM + h))
    return pl.pallas_call(
        body, name="mem_bwd", grid=(b_dim, N_HEADS_MEM, s_dim // tt),
        in_specs=[qs, ks, vs, qs], out_specs=[qs, ks, ks],
        out_shape=[jax.ShapeDtypeStruct(qm.shape, BF16), jax.ShapeDtypeStruct((b_dim, n_mem, MEM_WIDTH), F32),
                   jax.ShapeDtypeStruct((b_dim, n_mem, MEM_WIDTH), F32)],
        compiler_params=_params(("parallel", "parallel", "arbitrary")),
    )(qm, kvm, kvm, dom)


def _adamw_math(wv, gv, mv, vv):
    nm = ADAM_B1 * mv + (1.0 - ADAM_B1) * gv
    nv = ADAM_B2 * vv + (1.0 - ADAM_B2) * (gv * gv)
    m_hat = nm / (1.0 - ADAM_B1 ** ADAM_STEP)
    v_hat = nv / (1.0 - ADAM_B2 ** ADAM_STEP)
    return -ADAM_LR * (m_hat / (jnp.sqrt(v_hat) + ADAM_EPS) + ADAM_WD * wv), nm, nv


def _adamw(w, g, m, v, *, name):
    rows, cols = w.shape
    tr = _tile(rows, 256, 8)

    def body(w_ref, g_ref, m_ref, v_ref, d_ref, nm_ref, nv_ref):
        d_ref[...], nm_ref[...], nv_ref[...] = _adamw_math(w_ref[...], g_ref[...], m_ref[...], v_ref[...])

    spec = pl.BlockSpec((tr, cols), lambda i: (i, 0))
    shp = jax.ShapeDtypeStruct((rows, cols), F32)
    return pl.pallas_call(
        body, name=name, grid=(rows // tr,),
        in_specs=[spec] * 4, out_specs=[spec] * 3, out_shape=[shp] * 3,
        compiler_params=_params(("parallel",)),
    )(w, g, m, v)


def _prefetch_spec(grid, in_specs, out_specs):
    return pltpu.PrefetchScalarGridSpec(num_scalar_prefetch=1, grid=grid, in_specs=in_specs, out_specs=out_specs)


def _adamw_halves(w, mine, theirs, m, v, c_idx, *, name):
    rows, cols = w.shape
    half = rows // 2
    tr = _tile(half, _row_cap(cols), 8)
    nh = half // tr

    def body(c_ref, w_ref, mine_ref, theirs_ref, m_ref, v_ref, g_ref, d_ref, nm_ref, nv_ref):
        gv = jnp.where(pl.program_id(0) == c_ref[0], mine_ref[...], theirs_ref[...])
        g_ref[...] = gv
        d_ref[...], nm_ref[...], nv_ref[...] = _adamw_math(w_ref[...], gv, m_ref[...], v_ref[...])

    full = pl.BlockSpec((tr, cols), lambda h, i, c_ref: (h * nh + i, 0))
    part = pl.BlockSpec((tr, cols), lambda h, i, c_ref: (i, 0))
    shp = jax.ShapeDtypeStruct((rows, cols), F32)
    return pl.pallas_call(
        body, name=name, grid_spec=_prefetch_spec((2, nh), [full, part, part, full, full], [full] * 4),
        out_shape=[shp] * 4,
        compiler_params=_params(("parallel", "parallel")),
    )(c_idx, w, mine, theirs, m, v)


def _pair_sum(g3, theirs, c_idx, *, name):
    n, rows, cols = g3.shape
    half = rows // 2
    tr = _tile(half, _row_cap(cols), 16)

    def body(c_ref, g_ref, t_ref, o_ref):
        o_ref[...] = (g_ref[...] + t_ref[...]).astype(BF16)

    part = pl.BlockSpec((None, tr, cols), lambda s, i, c_ref: (s, i, 0))
    return pl.pallas_call(
        body, name=name,
        grid_spec=_prefetch_spec((n, half // tr),
                                 [pl.BlockSpec((None, None, tr, cols), lambda s, i, c_ref: (s, c_ref[0], i, 0)), part],
                                 part),
        out_shape=jax.ShapeDtypeStruct((n, half, cols), BF16),
        compiler_params=_params(("parallel", "parallel")),
    )(c_idx, g3.reshape(n, 2, half, cols), theirs)


def _chip_sum(pair, recv, s_idx, *, name):
    _, half, cols = pair.shape
    tr = _tile(half, _row_cap(cols), 16)

    def body(s_ref, p_ref, r_ref, o_ref):
        o_ref[...] = ((p_ref[...].astype(F32) + r_ref[0].astype(F32)) + r_ref[1].astype(F32)) + r_ref[2].astype(F32)

    return pl.pallas_call(
        body, name=name,
        grid_spec=_prefetch_spec((half // tr,),
                                 [pl.BlockSpec((None, tr, cols), lambda i, s_ref: (s_ref[0], i, 0)),
                                  pl.BlockSpec((N_CHIPS - 1, tr, cols), lambda i, s_ref: (0, i, 0))],
                                 pl.BlockSpec((tr, cols), lambda i, s_ref: (i, 0))),
        out_shape=jax.ShapeDtypeStruct((half, cols), F32),
        compiler_params=_params(("parallel",)),
    )(s_idx, pair, recv)


def _sum8(parts):
    n, rows, cols = parts.shape

    def body(p_ref, o_ref):
        acc = p_ref[0]
        for i in range(1, n):
            acc = acc + p_ref[i]
        o_ref[...] = acc

    return pl.pallas_call(
        body, name="small_sum", grid=(1,),
        in_specs=[pl.BlockSpec((n, rows, cols), lambda i: (0, 0, 0))],
        out_specs=pl.BlockSpec((rows, cols), lambda i: (0, 0)),
        out_shape=jax.ShapeDtypeStruct((rows, cols), parts.dtype),
        compiler_params=_params(("arbitrary",)),
    )(parts)


def _place():
    return lax.axis_index("x"), lax.axis_index("y"), lax.axis_index("c")


ANY = pl.BlockSpec(memory_space=pl.ANY)


def _rider_parts(rider):
    if rider is None:
        return (), [], []
    kind, arrays = rider
    n = len(arrays)
    shapes = {"gather": _gathered_shapes, "pair": _pair_shapes, "chip": _chip_shapes}[kind](arrays)
    sems = _gather_sems(n) if kind == "gather" else _exchange_sems(n if kind == "pair" else 3 * n)
    return tuple(arrays), shapes, sems


def _rider_hooks(rider, ins, outs, sems, step, n_steps):
    if rider is None:
        return (lambda: None), (lambda: None)
    if rider[0] == "gather":
        start, forward, finish = _gather_steps(ins, outs, *sems)
    else:
        start, finish = {"pair": _pair_steps, "chip": _chip_steps}[rider[0]](ins, outs, *sems)
        forward = None

    def begin():
        pl.when(step == 0)(start)

    def end():
        if forward is not None:
            pl.when(step == n_steps - 2)(forward)
        pl.when(step == n_steps - 1)(finish)

    return begin, end


def _gathered_shapes(shards):
    return [jax.ShapeDtypeStruct((N_CHIPS,) + s.shape, s.dtype) for s in shards]


def _gather_sems(n):
    return [pltpu.SemaphoreType.DMA((7 * n,)), pltpu.SemaphoreType.DMA((7 * n,))]


def _gather_steps(ins, outs, send_sems, recv_sems):
    n = len(ins)
    halves = [r.shape[0] // 2 for r in ins]
    x, y, c = _place()
    my_chip = 2 * x + y
    me, sibling = (x, y, c), (x, y, 1 - c)
    chips = [(1 - x, y), (x, 1 - y), (1 - x, 1 - y)]

    def half_of(w, chip, pc):
        return outs[w].at[chip, pl.ds(pc * halves[w], halves[w]), :]

    def copy(w, k, src, dst, to):
        return pltpu.make_async_remote_copy(
            src_ref=src, dst_ref=dst, send_sem=send_sems.at[7 * w + k], recv_sem=recv_sems.at[7 * w + k],
            device_id=to, device_id_type=MESH)

    def firsts():
        cps = []
        for w in range(n):
            cps.append(copy(w, 0, ins[w], outs[w].at[my_chip], sibling))
            mine = ins[w].at[pl.ds(c * halves[w], halves[w]), :]
            for j, (px, py) in enumerate(chips):
                cps.append(copy(w, 1 + j, mine, half_of(w, my_chip, c), (px, py, c)))
        return cps

    def passes():
        return [copy(w, 4 + j, half_of(w, 2 * px + py, c), half_of(w, 2 * px + py, c), sibling)
                for w in range(n) for j, (px, py) in enumerate(chips)]

    def start():
        for cp in firsts():
            cp.start()

    def forward():
        fws = passes()
        for w in range(n):
            for j, (px, py) in enumerate(chips):
                landed = half_of(w, 2 * px + py, c)
                copy(w, 1 + j, landed, landed, me).wait_recv()
                fws[3 * w + j].start()

    def finish():
        for w in range(n):
            copy(w, 0, ins[w], outs[w].at[my_chip], me).wait_recv()
            for j, (px, py) in enumerate(chips):
                landed = half_of(w, 2 * px + py, 1 - c)
                copy(w, 4 + j, landed, landed, me).wait_recv()
        for cp in firsts() + passes():
            cp.wait_send()

    return start, forward, finish


def _pair_shapes(grads):
    return [jax.ShapeDtypeStruct((g.shape[0], g.shape[1] // 2, g.shape[2]), g.dtype) for g in grads]


def _exchange_sems(n):
    return [pltpu.SemaphoreType.DMA((n,)), pltpu.SemaphoreType.DMA((n,))]


def _exchange_steps(copies):
    def start():
        for cp in copies():
            cp.start()

    def finish():
        for cp in copies():
            cp.wait()

    return start, finish


def _pair_steps(ins, outs, send_sems, recv_sems):
    x, y, c = _place()

    def copies():
        return [pltpu.make_async_remote_copy(
            src_ref=ins[w].at[:, pl.ds((1 - c) * (ins[w].shape[1] // 2), ins[w].shape[1] // 2), :], dst_ref=outs[w],
            send_sem=send_sems.at[w], recv_sem=recv_sems.at[w], device_id=(x, y, 1 - c), device_id_type=MESH)
            for w in range(len(ins))]

    return _exchange_steps(copies)


def _chip_shapes(pairs):
    return [jax.ShapeDtypeStruct((N_CHIPS - 1,) + p.shape[1:], p.dtype) for p in pairs]


def _chip_steps(ins, outs, send_sems, recv_sems):
    x, y, c = _place()
    others = [(1 - x, y), (x, 1 - y), (1 - x, 1 - y)]

    def copies():
        return [pltpu.make_async_remote_copy(
            src_ref=ins[w].at[2 * px + py], dst_ref=outs[w].at[j],
            send_sem=send_sems.at[3 * w + j], recv_sem=recv_sems.at[3 * w + j],
            device_id=(px, py, c), device_id_type=MESH)
            for w in range(len(ins)) for j, (px, py) in enumerate(others)]

    return _exchange_steps(copies)


def _swap_halves(mine):
    n = len(mine)

    def body(*refs):
        ins, outs, send_sems, recv_sems = refs[:n], refs[n:2 * n], refs[2 * n], refs[2 * n + 1]
        x, y, c = _place()
        copies = [pltpu.make_async_remote_copy(
            src_ref=ins[w], dst_ref=outs[w], send_sem=send_sems.at[w], recv_sem=recv_sems.at[w],
            device_id=(x, y, 1 - c), device_id_type=MESH) for w in range(n)]
        for cp in copies:
            cp.start()
        for cp in copies:
            cp.wait()

    return pl.pallas_call(
        body, name="grad_swap_halves",
        out_shape=[jax.ShapeDtypeStruct(h.shape, h.dtype) for h in mine],
        in_specs=[ANY] * n, out_specs=[ANY] * n,
        scratch_shapes=[pltpu.SemaphoreType.DMA((n,)), pltpu.SemaphoreType.DMA((n,))],
    )(*mine)


def _gather_small(small):
    srows, cols = small.shape

    def body(s_ref, all_ref, send_sems, recv_sems, local_sem):
        x, y, c = _place()
        me = 4 * x + 2 * y + c
        keep_small = pltpu.make_async_copy(s_ref, all_ref.at[me], local_sem)
        keep_small.start()
        sends = []
        for kk in range(1, 8):
            peer = (x ^ (kk >> 2), y ^ ((kk >> 1) & 1), c ^ (kk & 1))
            sends.append(pltpu.make_async_remote_copy(
                src_ref=s_ref, dst_ref=all_ref.at[me],
                send_sem=send_sems.at[kk], recv_sem=recv_sems.at[kk], device_id=peer, device_id_type=MESH))
        for cp in sends:
            cp.start()
        for kk in range(1, 8):
            px, py, pc = x ^ (kk >> 2), y ^ ((kk >> 1) & 1), c ^ (kk & 1)
            pltpu.make_async_remote_copy(
                src_ref=s_ref, dst_ref=all_ref.at[4 * px + 2 * py + pc],
                send_sem=send_sems.at[kk], recv_sem=recv_sems.at[kk], device_id=(px, py, pc),
                device_id_type=MESH).wait_recv()
        for cp in sends:
            cp.wait_send()
        keep_small.wait()

    return pl.pallas_call(
        body, name="gather_small",
        out_shape=jax.ShapeDtypeStruct((8, srows, cols), small.dtype),
        in_specs=[ANY], out_specs=ANY,
        scratch_shapes=[pltpu.SemaphoreType.DMA((8,)), pltpu.SemaphoreType.DMA((8,)), pltpu.SemaphoreType.DMA],
    )(small)


SHARDED = (("w_in", D_MODEL, IN_COLS, 1), ("w_up_a", ATT_WIDTH, D_MODEL, 1), ("w_up_b", ATT_WIDTH, D_MODEL, 1),
           ("w_out", D_MODEL, D_MODEL, 0), ("w_q_mem", D_MODEL, MEM_WIDTH, 0), ("w_kv_mem", D_MODEL, 2 * MEM_WIDTH, 0),
           ("w_o_mem", MEM_WIDTH, D_MODEL, 1), ("w_ffn_gate", D_FF, D_MODEL, 0), ("w_ffn_up", D_FF, D_MODEL, 0),
           ("w_ffn_down", D_FF, D_MODEL, 0))
TRANSPOSED = ("w_ffn_gate", "w_ffn_up")
NAMES = tuple(n for n, _, _, _ in SHARDED)


def _held(name, shard):
    return shard.T if name in TRANSPOSED else shard
EARLY, LATE = NAMES[:1], NAMES[1:]
GAINS = ("g_mix", "g_mem_q", "g_mem_kv", "g_ffn", "g_final")


def _natural(w3):
    n, r, c = w3.shape
    return w3.reshape(n * r, c)


def _shard_major(g, axis):
    if axis == 1:
        return g
    r, c = g.shape
    return g.reshape(N_CHIPS, r // N_CHIPS, c)


def kernel(x, mem, positions, g_mix, w_in, w_up_a, w_up_b, w_out, g_mem_q, g_mem_kv, w_q_mem, w_kv_mem, w_o_mem, g_ffn, w_ffn_gate, w_ffn_up, w_ffn_down, g_final, loss_target, m_g_mix, m_w_in, m_w_up_a, m_w_up_b, m_w_out, m_g_mem_q, m_g_mem_kv, m_w_q_mem, m_w_kv_mem, m_w_o_mem, m_g_ffn, m_w_ffn_gate, m_w_ffn_up, m_w_ffn_down, m_g_final, v_g_mix, v_w_in, v_w_up_a, v_w_up_b, v_w_out, v_g_mem_q, v_g_mem_kv, v_w_q_mem, v_w_kv_mem, v_w_o_mem, v_g_ffn, v_w_ffn_gate, v_w_ffn_up, v_w_ffn_down, v_g_final):
    given = dict(locals())
    shards = {n: _held(n, given[n][0]) for n in NAMES}

    early_shards = [shards[n].astype(BF16) for n in EARLY]
    late_shards = [shards[n].astype(BF16) for n in LATE]
    c_idx = lax.axis_index("c").astype(jnp.int32).reshape(1)
    s_idx = (2 * lax.axis_index("x") + lax.axis_index("y")).astype(jnp.int32).reshape(1)

    loss_row, grad_x, mine, gain_grads = _local_step(x, mem, positions, loss_target, g_mix, g_mem_q, g_mem_kv,
                                                     g_ffn, g_final, {}, early_shards, late_shards, (c_idx, s_idx))
    return _reduce_and_update(given, shards, loss_row, grad_x, mine, gain_grads, c_idx)


def _reduce_halves(glist, names, c_idx, s_idx, pair_exchange, chip_exchange):
    theirs = pair_exchange(glist)
    pairs = [_pair_sum(g, t, c_idx, name="pair_sum_" + n) for n, g, t in zip(names, glist, theirs)]
    recv = chip_exchange(pairs)
    return [_chip_sum(p, r, s_idx, name="chip_sum_" + n) for n, p, r in zip(names, pairs, recv)]


def _local_step(x, mem, positions, loss_target, g_mix, g_mem_q, g_mem_kv, g_ffn, g_final, wf,
                early_shards=None, late_shards=None, place=None):
    b_dim, s_dim, d = x.shape
    t_dim = b_dim * s_dim
    n_mem = mem.shape[1]
    wf = dict(wf)

    xb = x.reshape(t_dim, d)
    tgt = loss_target.reshape(t_dim, d)
    memf = mem.reshape(b_dim * n_mem, d)
    gfin = g_final.reshape(1, d)
    pos = positions.reshape(t_dim, 1).astype(F32)

    lane = jnp.arange(LANES) % HEAD_DIM
    half = ROPE_DIM // 2
    inv_freq = ROPE_THETA ** (-jnp.arange(half, dtype=F32) / half)
    inv_lane = jnp.where(lane < ROPE_DIM, inv_freq[lane % half], 0.0).reshape(1, -1).astype(F32)
    sel_a = (lane < half).astype(F32).reshape(1, -1)
    sel_b = ((lane >= half) & (lane < ROPE_DIM)).astype(F32).reshape(1, -1)

    def rows3(t):
        return t.reshape(b_dim, s_dim, t.shape[-1])

    def rows2(t):
        return t.reshape(t_dim, t.shape[-1])

    if early_shards:
        n1, gathered = _rms_fwd(xb, g_mix, name="rms_mix", rider=("gather", early_shards))
        wf.update(zip(EARLY, gathered))
    else:
        n1 = _rms_fwd(xb, g_mix, name="rms_mix")
    proj = _mm_cs(n1, wf["w_in"], name="mm_in")
    proj3 = rows3(proj)
    cs, sn = _rope_table(pos, inv_lane, sel_a, sel_b)
    cs3, sn3 = rows3(cs), rows3(sn)
    (oa16, oa32, lse_a), _ = _dil_fwd(proj3, cs3, sn3, sel_a, sel_b)
    ob16, gathered = _sb_fwd(proj3, ("gather", late_shards) if late_shards else None)
    wf.update(zip(LATE, gathered))
    w_out, w_q, w_kv = _natural(wf["w_out"]), _natural(wf["w_q_mem"]), _natural(wf["w_kv_mem"])
    oa, ob = rows2(oa16), rows2(ob16)
    ua = _mm_sm(oa, wf["w_up_a"], name="mm_up_a")
    ub = _mm_sm(ob, wf["w_up_b"], name="mm_up_b")
    mixed, h1, hn = _gate_out_norm(proj, ua, ub, w_out, xb, g_mem_q)

    memn = _rms_fwd(memf, g_mem_kv, name="rms_mem_kv")
    qm = _mm(hn, w_q, name="mm_q_mem", out_dtype=BF16)
    kvm = _mm(memn, w_kv, name="mm_kv_mem", out_dtype=BF16)
    qm3, kvm3 = rows3(qm), kvm.reshape(b_dim, n_mem, 2 * MEM_WIDTH)
    om = rows2(_mem_fwd(qm3, kvm3))
    h2 = _mm_sm(om, wf["w_o_mem"], name="mm_o_mem", add=h1)

    n3 = _rms_fwd(h2, g_ffn, name="rms_ffn")
    gate3, up3, act3 = _ffn_up_swiglu(n3, wf["w_ffn_gate"], wf["w_ffn_up"])
    loss_row, dh3, dg_final = _down_final(act3, wf["w_ffn_down"], h2, gfin, tgt)

    grads = {}
    grads["w_ffn_down"] = _mm_ffn_down_dw(act3, dh3, name="mm_down_dw")
    dgate3, dup3 = _ffn_down_dx_swiglu(dh3, wf["w_ffn_down"], gate3, up3)
    grads["w_ffn_gate"] = _mm_ffn_down_dw(dgate3, n3, name="mm_gate_dw")
    grads["w_ffn_up"] = _mm_ffn_down_dw(dup3, n3, name="mm_up_dw")
    dn3 = _mm_ffn_rows(dgate3, wf["w_ffn_gate"], name="mm_gate_dx")
    dn3 = _mm_ffn_rows(dup3, wf["w_ffn_up"], name="mm_up_dx", add=dn3)
    dh2, dg_ffn = _rms_bwd(h2, g_ffn, dn3, dh3, name="rms_ffn_bwd")

    dom = _mm_sm_dx(dh2, wf["w_o_mem"], name="mm_o_mem_dx", out_dtype=BF16)
    grads["w_o_mem"] = _mm_sm_dw(om, dh2, name="mm_o_mem_dw")
    dqm, dkm, dvm = _mem_bwd(qm3, kvm3, rows3(dom))
    dqm = rows2(dqm)
    dkvm = jnp.concatenate([dkm, dvm], axis=-1).reshape(b_dim * n_mem, 2 * MEM_WIDTH).astype(BF16)
    grads["w_q_mem"] = _shard_major(_mm(hn, dqm, name="mm_q_mem_dw", ta=True), 0)
    dhn = _mm(dqm, w_q, name="mm_q_mem_dx", tb=True)
    grads["w_kv_mem"] = _shard_major(_mm(memn, dkvm, name="mm_kv_mem_dw", ta=True), 0)
    dmemn = _mm(dkvm, w_kv, name="mm_kv_mem_dx", tb=True)
    _, dg_mem_kv = _rms_bwd(memf, g_mem_kv, dmemn, None, name="rms_mem_kv_bwd")
    dh1, dg_mem_q = _rms_bwd(h1, g_mem_q, dhn, dh2, name="rms_mem_q_bwd")

    grads["w_out"] = _shard_major(_mm(mixed, dh1, name="mm_out_dw", ta=True), 0)
    dua, dub, dgates = _out_dx_gate_bwd(dh1, w_out, proj, ua, ub)
    doa = _mm_sm_dx(dua, wf["w_up_a"], name="mm_up_a_dx")
    grads["w_up_a"] = _mm_sm_dw(oa, dua, name="mm_up_a_dw")
    dob = _mm_sm_dx(dub, wf["w_up_b"], name="mm_up_b_dx", out_dtype=BF16)
    grads["w_up_b"] = _mm_sm_dw(ob, dub, name="mm_up_b_dw")

    att = {}

    def dil_with_pairs(glist):
        att["a"], theirs = _dil_bwd(proj3, cs3, sn3, sel_a, sel_b, rows3(doa), oa32, lse_a,
                                    ("pair", glist) if glist else None)
        return theirs

    def sb_with_chips(pairs):
        att["b"], recv = _sb_bwd(proj3, rows3(dob), ("chip", pairs) if pairs else None)
        return recv

    if place is None:
        dil_with_pairs(())
        sb_with_chips(())
    else:
        mine_late = _reduce_halves([grads[n] for n in LATE], LATE, *place, dil_with_pairs, sb_with_chips)
    dproj = jnp.concatenate([rows2(t) for t in att["a"] + att["b"]] + [dgates], axis=1)
    grads["w_in"] = _mm_cs_dw(n1, dproj, name="mm_in_dw")
    if place is None:
        dn1 = _mm_cs_dx(dproj, wf["w_in"], name="mm_in_dx")
        dx, dg_mix = _rms_bwd(xb, g_mix, dn1, dh1, name="rms_mix_bwd")
    else:
        tail = {}

        def dx_with_pairs(glist):
            tail["dn1"], theirs = _mm_cs_dx(dproj, wf["w_in"], name="mm_in_dx", rider=("pair", glist))
            return theirs

        def rms_with_chips(pairs):
            tail["dx"], tail["dg"], recv = _rms_bwd(xb, g_mix, tail["dn1"], dh1, name="rms_mix_bwd",
                                                    rider=("chip", pairs))
            return recv

        mine_early = _reduce_halves([grads[n] for n in EARLY], EARLY, *place, dx_with_pairs, rms_with_chips)
        dx, dg_mix = tail["dx"], tail["dg"]
    grad_x = dx.reshape(b_dim, s_dim, d)
    gains = (dg_mix, dg_mem_q, dg_mem_kv, dg_ffn, dg_final)
    if place is None:
        return loss_row, grad_x, grads, gains
    return loss_row, grad_x, mine_early + mine_late, gains


def _reduce_and_update(given, shards, loss_row, grad_x, mine, gain_grads, c_idx):
    d = D_MODEL
    dg_mix, dg_mem_q, dg_mem_kv, dg_ffn, dg_final = gain_grads
    small = jnp.concatenate([dg_mix, dg_mem_q, dg_mem_kv, dg_ffn, dg_final,
                             jnp.pad(loss_row, ((0, 0), (0, FLAT_COLS - LANES))), jnp.zeros((2, FLAT_COLS), F32)], axis=0)
    small_all = _gather_small(small)
    others = _swap_halves(mine)
    small_sum = _sum8(small_all)
    loss = small_sum[5, 0]

    out_g, out_d, out_m, out_v = {}, {}, {}, {}
    for n, mine_n, other_n in zip(NAMES, mine, others):
        res = _adamw_halves(shards[n], mine_n, other_n, _held(n, given["m_" + n][0]), _held(n, given["v_" + n][0]),
                            c_idx, name="adamw_" + n)
        out_g[n], out_d[n], out_m[n], out_v[n] = [_held(n, r)[None] for r in res]
    gain_w = jnp.concatenate([given[n].reshape(1, d) for n in GAINS], axis=0)
    gain_m = jnp.concatenate([given["m_" + n].reshape(1, d) for n in GAINS], axis=0)
    gain_v = jnp.concatenate([given["v_" + n].reshape(1, d) for n in GAINS], axis=0)
    gain_g = small_sum[:len(GAINS)]
    gd, gm, gv = _adamw(gain_w, gain_g, gain_m, gain_v, name="adamw_gains")
    for i, n in enumerate(GAINS):
        shape = given[n].shape
        out_g[n], out_d[n] = gain_g[i].reshape(shape), gd[i].reshape(shape)
        out_m[n], out_v[n] = gm[i].reshape(shape), gv[i].reshape(shape)

    order = ["g_mix", "w_in", "w_up_a", "w_up_b", "w_out", "g_mem_q", "g_mem_kv", "w_q_mem", "w_kv_mem", "w_o_mem",
             "g_ffn", "w_ffn_gate", "w_ffn_up", "w_ffn_down", "g_final"]
    return (loss, grad_x, *[out_g[n] for n in order], *[out_d[n] for n in order],
            *[out_m[n] for n in order], *[out_v[n] for n in order])
```

```python
import jax
import jax.numpy as jnp
from jax import lax
from jax.experimental import pallas as pl
from jax.experimental.pallas import tpu as pltpu

F32 = jnp.float32
BF16 = jnp.bfloat16
MESH = pl.DeviceIdType.MESH

D_MODEL = 1024
HEAD_DIM = 64
N_HEADS = 8
ATT_WIDTH = N_HEADS * HEAD_DIM
DIL_PATTERNS = ((128, 1), (512, 4), (2048, 16))
BLOCK = 128
SB_ROWS = 1024
SB_STEP = 4
ROPE_THETA = 500000.0
ROPE_DIM = HEAD_DIM // 4
N_HEADS_MEM = 4
MEM_HEAD_DIM = 128
MEM_WIDTH = N_HEADS_MEM * MEM_HEAD_DIM
D_FF = 2816
IN_COLS = 6 * ATT_WIDTH + 2 * D_MODEL
RMS_EPS = 1e-6
ADAM_LR = 0.001
ADAM_B1 = 0.9
ADAM_B2 = 0.999
ADAM_EPS = 1e-08
ADAM_WD = 0.01
ADAM_STEP = 10

N_CHIPS = 4
LANES = 128
FLAT_COLS = 1024
VMEM_LIMIT = 56 * 1024 * 1024

PAIRS = ATT_WIDTH // LANES
COL_QA, COL_KA, COL_VA, COL_QB, COL_KB, COL_VB = (i * PAIRS for i in range(6))

MM_CAP = 1408
TOK_CAP = 2048
NN = (((1,), (0,)), ((), ()))
NT = (((1,), (1,)), ((), ()))
TN = (((0,), (0,)), ((), ()))
BNN = (((2,), (1,)), ((0,), (0,)))
BNT = (((2,), (2,)), ((0,), (0,)))
BTN = (((1,), (1,)), ((0,), (0,)))
DIL_BATCH = 16
DIL_CHUNK = 512


def _tile(dim, cap, unit=LANES):
    if dim <= cap:
        return dim
    best = None
    for t in range(unit, cap + 1, unit):
        if dim % t == 0:
            best = t
    assert best is not None, (dim, cap)
    return best


def _row_cap(cols):
    return max(256, (1 << 18) // cols)


def _params(sem):
    return pltpu.CompilerParams(dimension_semantics=sem, vmem_limit_bytes=VMEM_LIMIT)


def _mm(a, b, *, name, ta=False, tb=False, add=None, out_dtype=F32,
        tm_cap=MM_CAP, tn_cap=MM_CAP, tk_cap=MM_CAP):
    if ta:
        k_dim, m_dim = a.shape
    else:
        m_dim, k_dim = a.shape
    if tb:
        n_dim, kb = b.shape
    else:
        kb, n_dim = b.shape
    assert kb == k_dim, (a.shape, b.shape, ta, tb)
    tm, tn, tk = _tile(m_dim, tm_cap), _tile(n_dim, tn_cap), _tile(k_dim, tk_cap)
    nk = k_dim // tk
    dims = (((0 if ta else 1,), (1 if tb else 0,)), ((), ()))
    has_add = add is not None

    def body(*refs):
        if has_add:
            a_ref, b_ref, add_ref, o_ref = refs[:4]
        else:
            a_ref, b_ref, o_ref = refs[:3]
        part = lax.dot_general(a_ref[...].astype(BF16), b_ref[...].astype(BF16), dims, preferred_element_type=F32)

        def finish(r):
            if has_add:
                r = add_ref[...] + r
            o_ref[...] = r.astype(out_dtype)

        if nk == 1:
            finish(part)
            return
        acc_ref = refs[-1]
        k = pl.program_id(2)

        @pl.when(k == 0)
        def _():
            acc_ref[...] = part

        @pl.when(k > 0)
        def _():
            acc_ref[...] += part

        @pl.when(k == nk - 1)
        def _():
            finish(acc_ref[...])

    a_spec = pl.BlockSpec((tk, tm), lambda i, j, k: (k, i)) if ta else pl.BlockSpec((tm, tk), lambda i, j, k: (i, k))
    b_spec = pl.BlockSpec((tn, tk), lambda i, j, k: (j, k)) if tb else pl.BlockSpec((tk, tn), lambda i, j, k: (k, j))
    o_spec = pl.BlockSpec((tm, tn), lambda i, j, k: (i, j))
    in_specs = [a_spec, b_spec] + ([o_spec] if has_add else [])
    args = (a, b) + ((add,) if has_add else ())
    return pl.pallas_call(
        body, name=name, grid=(m_dim // tm, n_dim // tn, nk),
        in_specs=in_specs, out_specs=o_spec,
        out_shape=jax.ShapeDtypeStruct((m_dim, n_dim), out_dtype),
        scratch_shapes=[pltpu.VMEM((tm, tn), F32)] if nk > 1 else [],
        compiler_params=_params(("parallel", "parallel", "arbitrary")),
    )(*args)


def _mm_core(name, a, b, a_spec, b_spec, o_spec, out_shape, grid, dims, *, add=None, out_dtype=F32, rider=None):
    nk = grid[2]
    has_add = add is not None
    n_in = 3 if has_add else 2
    acc_shape = tuple(d for d in o_spec.block_shape if d is not None)
    extra, extra_shapes, extra_sems = _rider_parts(rider)
    n_w = len(extra)

    def body(*refs):
        a_ref, b_ref = refs[:2]
        o_ref = refs[n_in + n_w]
        step = (pl.program_id(0) * grid[1] + pl.program_id(1)) * nk + pl.program_id(2)
        begin, end = _rider_hooks(rider, refs[n_in:n_in + n_w], refs[n_in + n_w + 1:n_in + 2 * n_w + 1], refs[-2:],
                                  step, grid[0] * grid[1] * nk)
        begin()
        part = lax.dot_general(a_ref[...].astype(BF16), b_ref[...].astype(BF16), dims, preferred_element_type=F32)

        def finish(r):
            if has_add:
                r = refs[2][...] + r
            o_ref[...] = r.astype(out_dtype)

        if nk == 1:
            finish(part)
        else:
            acc_ref = refs[n_in + 2 * n_w + 1]
            k = pl.program_id(2)

            @pl.when(k == 0)
            def _():
                acc_ref[...] = part

            @pl.when(k > 0)
            def _():
                acc_ref[...] += part

            @pl.when(k == nk - 1)
            def _():
                finish(acc_ref[...])
        end()

    in_specs = [a_spec, b_spec] + ([o_spec] if has_add else []) + [ANY] * n_w
    args = (a, b) + ((add,) if has_add else ()) + extra
    res = pl.pallas_call(
        body, name=name, grid=grid, in_specs=in_specs, out_specs=[o_spec] + [ANY] * n_w,
        out_shape=[jax.ShapeDtypeStruct(out_shape, out_dtype)] + extra_shapes,
        scratch_shapes=([pltpu.VMEM(acc_shape, F32)] if nk > 1 else []) + extra_sems,
        compiler_params=_params(("arbitrary",) * 3 if n_w else ("parallel", "parallel", "arbitrary")),
    )(*args)
    return (res[0], res[1:]) if n_w else res[0]


def _mm_cs(a, w3, *, name, add=None, out_dtype=F32):
    m_dim, k_dim = a.shape
    _, _, n4 = w3.shape
    tm, tn, tk = _tile(m_dim, MM_CAP if add is not None else TOK_CAP), _tile(n4, MM_CAP), _tile(k_dim, MM_CAP)
    npb = n4 // tn
    return _mm_core(name, a, w3,
                    pl.BlockSpec((tm, tk), lambda i, j, k: (i, k)),
                    pl.BlockSpec((None, tk, tn), lambda i, j, k: (j // npb, k, j % npb)),
                    pl.BlockSpec((tm, tn), lambda i, j, k: (i, j)),
                    (m_dim, N_CHIPS * n4), (m_dim // tm, N_CHIPS * npb, k_dim // tk), NN, add=add, out_dtype=out_dtype)


def _mm_cs_dx(dy, w3, *, name, out_dtype=F32, rider=None):
    m_dim, _ = dy.shape
    _, k_dim, n4 = w3.shape
    tm, tkw, tn = _tile(m_dim, MM_CAP), _tile(k_dim, MM_CAP), _tile(n4, MM_CAP)
    npb = n4 // tn
    return _mm_core(name, dy, w3,
                    pl.BlockSpec((tm, tn), lambda i, j, k: (i, k)),
                    pl.BlockSpec((None, tkw, tn), lambda i, j, k: (k // npb, j, k % npb)),
                    pl.BlockSpec((tm, tkw), lambda i, j, k: (i, j)),
                    (m_dim, k_dim), (m_dim // tm, k_dim // tkw, N_CHIPS * npb), NT, out_dtype=out_dtype, rider=rider)


def _mm_cs_dw(a, dy, *, name):
    m_dim, k_dim = a.shape
    n4 = dy.shape[1] // N_CHIPS
    tmk, tn, tk = _tile(k_dim, MM_CAP), _tile(n4, MM_CAP), _tile(m_dim, TOK_CAP)
    npb = n4 // tn
    return _mm_core(name, a, dy,
                    pl.BlockSpec((tk, tmk), lambda i, j, k: (k, i)),
                    pl.BlockSpec((tk, tn), lambda i, j, k: (k, j)),
                    pl.BlockSpec((None, tmk, tn), lambda i, j, k: (j // npb, i, j % npb)),
                    (N_CHIPS, k_dim, n4), (k_dim // tmk, N_CHIPS * npb, m_dim // tk), TN)


def _mm_sm(a, w3, *, name, add=None, out_dtype=F32, tt=1024):
    t_dim, k_dim = a.shape
    n_s, _, n4 = w3.shape
    has_add = add is not None

    def body(*refs):
        a_ref, w_ref, o_ref = refs[0], refs[1], refs[-1]
        av = a_ref[...].astype(BF16)
        for s in range(n_s):
            cols = slice(s * n4, (s + 1) * n4)
            r = jnp.dot(av, w_ref[s], preferred_element_type=F32)
            if has_add:
                r = refs[2][:, cols] + r
            o_ref[:, cols] = r.astype(out_dtype)

    row = pl.BlockSpec((tt, n_s * n4), lambda i: (i, 0))
    return pl.pallas_call(
        body, name=name, grid=(t_dim // tt,),
        in_specs=[pl.BlockSpec((tt, k_dim), lambda i: (i, 0)), pl.BlockSpec(w3.shape, lambda i: (0, 0, 0))]
        + ([row] if has_add else []),
        out_specs=row, out_shape=jax.ShapeDtypeStruct((t_dim, n_s * n4), out_dtype),
        compiler_params=_params(("parallel",)),
    )(*((a, w3) + ((add,) if has_add else ())))


def _mm_sm_dx(dy, w3, *, name, out_dtype=F32, tt=1024):
    t_dim, _ = dy.shape
    n_s, k_dim, n4 = w3.shape

    def body(dy_ref, w_ref, o_ref):
        dyv = dy_ref[...].astype(BF16)
        acc = lax.dot_general(dyv[:, :n4], w_ref[0], NT, preferred_element_type=F32)
        for s in range(1, n_s):
            acc = acc + lax.dot_general(dyv[:, s * n4:(s + 1) * n4], w_ref[s], NT, preferred_element_type=F32)
        o_ref[...] = acc.astype(out_dtype)

    return pl.pallas_call(
        body, name=name, grid=(t_dim // tt,),
        in_specs=[pl.BlockSpec((tt, n_s * n4), lambda i: (i, 0)), pl.BlockSpec(w3.shape, lambda i: (0, 0, 0))],
        out_specs=pl.BlockSpec((tt, k_dim), lambda i: (i, 0)),
        out_shape=jax.ShapeDtypeStruct((t_dim, k_dim), out_dtype),
        compiler_params=_params(("parallel",)),
    )(dy, w3)


def _mm_sm_dw(a, dy, *, name, tk=1024):
    t_dim, k_dim = a.shape
    n4 = dy.shape[1] // N_CHIPS

    def body(a_ref, dy_ref, o_ref):
        av = a_ref[...].astype(BF16)
        dyv = dy_ref[...].astype(BF16)

        @pl.when(pl.program_id(0) == 0)
        def _():
            o_ref[...] = jnp.zeros_like(o_ref)

        for s in range(N_CHIPS):
            o_ref[s] += lax.dot_general(av, dyv[:, s * n4:(s + 1) * n4], TN, preferred_element_type=F32)

    return pl.pallas_call(
        body, name=name, grid=(t_dim // tk,),
        in_specs=[pl.BlockSpec((tk, k_dim), lambda i: (i, 0)), pl.BlockSpec((tk, N_CHIPS * n4), lambda i: (i, 0))],
        out_specs=pl.BlockSpec((N_CHIPS, k_dim, n4), lambda i: (0, 0, 0)),
        out_shape=jax.ShapeDtypeStruct((N_CHIPS, k_dim, n4), F32),
        compiler_params=_params(("arbitrary",)),
    )(a, dy)


def _ffn_up_dx(dg3, du3, wg3, wu3, *, tt=512):
    n_s, t_dim, f4 = dg3.shape
    d = wg3.shape[2]

    def body(dg_ref, du_ref, wg_ref, wu_ref, o_ref):
        acc = None
        for s in range(n_s):
            part = (jnp.dot(dg_ref[s], wg_ref[s], preferred_element_type=F32)
                    + jnp.dot(du_ref[s], wu_ref[s], preferred_element_type=F32))
            acc = part if acc is None else acc + part
        o_ref[...] = acc

    a_spec = pl.BlockSpec((n_s, tt, f4), lambda i: (0, i, 0))
    w_spec = pl.BlockSpec(wg3.shape, lambda i: (0, 0, 0))
    return pl.pallas_call(
        body, name="ffn_up_dx", grid=(t_dim // tt,),
        in_specs=[a_spec, a_spec, w_spec, w_spec], out_specs=pl.BlockSpec((tt, d), lambda i: (i, 0)),
        out_shape=jax.ShapeDtypeStruct((t_dim, d), F32),
        compiler_params=_params(("parallel",)),
    )(dg3, du3, wg3, wu3)


def _mm_ffn_down_dw(act3, dh, *, name):
    _, t_dim, f4 = act3.shape
    d = dh.shape[1]
    tk = _tile(t_dim, TOK_CAP)
    return _mm_core(name, act3, dh,
                    pl.BlockSpec((None, tk, f4), lambda i, j, k: (i, k, 0)),
                    pl.BlockSpec((tk, d), lambda i, j, k: (k, 0)),
                    pl.BlockSpec((None, f4, d), lambda i, j, k: (i, 0, 0)),
                    (N_CHIPS, f4, d), (N_CHIPS, 1, t_dim // tk), TN)


def _rms_fwd(x, g, *, name, tt=512, rider=None):
    t_dim, d = x.shape
    tt = _tile(t_dim, tt, 8)
    extra, extra_shapes, extra_sems = _rider_parts(rider)
    n_w = len(extra)

    def body(*refs):
        x_ref, g_ref, o_ref = refs[0], refs[1], refs[2 + n_w]
        begin, end = _rider_hooks(rider, refs[2:2 + n_w], refs[3 + n_w:3 + 2 * n_w], refs[-2:], pl.program_id(0),
                                  t_dim // tt)
        begin()
        xv = x_ref[...]
        r = lax.rsqrt(jnp.mean(xv * xv, axis=-1, keepdims=True) + RMS_EPS)
        o_ref[...] = ((xv * r) * g_ref[...]).astype(o_ref.dtype)
        end()

    res = pl.pallas_call(
        body, name=name, grid=(t_dim // tt,),
        in_specs=[pl.BlockSpec((tt, d), lambda i: (i, 0)), pl.BlockSpec((1, d), lambda i: (0, 0))] + [ANY] * n_w,
        out_specs=[pl.BlockSpec((tt, d), lambda i: (i, 0))] + [ANY] * n_w,
        out_shape=[jax.ShapeDtypeStruct((t_dim, d), BF16)] + extra_shapes,
        scratch_shapes=extra_sems,
        compiler_params=_params(("arbitrary",) if n_w else ("parallel",)),
    )(x, g, *extra)
    return (res[0], res[1:]) if n_w else res[0]


def _rms_bwd(x, g, dy, add, *, name, tt=512, rider=None):
    t_dim, d = x.shape
    tt = _tile(t_dim, tt, 8)
    has_add = add is not None
    n_in = 4 if has_add else 3
    extra, extra_shapes, extra_sems = _rider_parts(rider)
    n_w = len(extra)

    def body(*refs):
        x_ref, g_ref, dy_ref = refs[:3]
        add_ref = refs[3] if has_add else None
        dx_ref, dg_ref = refs[n_in + n_w:n_in + n_w + 2]
        begin, end = _rider_hooks(rider, refs[n_in:n_in + n_w], refs[n_in + n_w + 2:n_in + 2 * n_w + 2], refs[-2:],
                                  pl.program_id(0), t_dim // tt)
        begin()
        xv = x_ref[...]
        dyv = dy_ref[...].astype(F32)
        r = lax.rsqrt(jnp.mean(xv * xv, axis=-1, keepdims=True) + RMS_EPS)
        xh = xv * r
        u = dyv * g_ref[...]
        dx = r * (u - xh * jnp.mean(u * xh, axis=-1, keepdims=True))
        if has_add:
            dx = add_ref[...] + dx
        dx_ref[...] = dx

        @pl.when(pl.program_id(0) == 0)
        def _():
            dg_ref[...] = jnp.zeros_like(dg_ref)

        dg_ref[...] += jnp.sum(dyv * xh, axis=0, keepdims=True)
        end()

    row = pl.BlockSpec((tt, d), lambda i: (i, 0))
    vec = pl.BlockSpec((1, d), lambda i: (0, 0))
    in_specs = [row, vec, row] + ([row] if has_add else []) + [ANY] * n_w
    args = (x, g, dy) + ((add,) if has_add else ()) + extra
    res = pl.pallas_call(
        body, name=name, grid=(t_dim // tt,),
        in_specs=in_specs, out_specs=[row, vec] + [ANY] * n_w,
        out_shape=[jax.ShapeDtypeStruct((t_dim, d), F32), jax.ShapeDtypeStruct((1, d), F32)] + extra_shapes,
        scratch_shapes=extra_sems,
        compiler_params=_params(("arbitrary",)),
    )(*args)
    return (res[0], res[1], res[2:]) if n_w else (res[0], res[1])


def _down_final(act3, wd3, h, g, target, *, tt=1024):
    n_s, t_dim, f4 = act3.shape
    d = h.shape[1]
    n_steps = t_dim // tt

    def body(a_ref, w_ref, h_ref, g_ref, t_ref, loss_ref, dh_ref, dg_ref, acc_ref, sq_ref):
        i, k = pl.program_id(0), pl.program_id(1)
        part = jnp.dot(a_ref[...], w_ref[...], preferred_element_type=F32)

        @pl.when(k == 0)
        def _():
            acc_ref[...] = part

        @pl.when(k > 0)
        def _():
            acc_ref[...] += part

        @pl.when(jnp.logical_and(i == 0, k == 0))
        def _():
            dg_ref[...] = jnp.zeros_like(dg_ref)
            sq_ref[...] = jnp.zeros_like(sq_ref)

        @pl.when(k == n_s - 1)
        def _():
            xv = h_ref[...] + acc_ref[...]
            gv = g_ref[...]
            r = lax.rsqrt(jnp.mean(xv * xv, axis=-1, keepdims=True) + RMS_EPS)
            xh = xv * r
            err = xh * gv - t_ref[...]
            dyv = err * (1.0 / d)
            u = dyv * gv
            dh_ref[...] = r * (u - xh * jnp.mean(u * xh, axis=-1, keepdims=True))
            dg_ref[...] += jnp.sum(dyv * xh, axis=0, keepdims=True)
            sq_ref[...] += jnp.sum(err * err, axis=0, keepdims=True)

        @pl.when(jnp.logical_and(i == n_steps - 1, k == n_s - 1))
        def _():
            total = jnp.sum(sq_ref[...], axis=-1, keepdims=True) * (0.5 / d)
            loss_ref[...] = jnp.broadcast_to(total, loss_ref.shape)

    row = pl.BlockSpec((tt, d), lambda i, k: (i, 0))
    vec = pl.BlockSpec((1, d), lambda i, k: (0, 0))
    return pl.pallas_call(
        body, name="down_final_loss", grid=(n_steps, n_s),
        in_specs=[pl.BlockSpec((None, tt, f4), lambda i, k: (k, i, 0)),
                  pl.BlockSpec((None, f4, d), lambda i, k: (k, 0, 0)), row, vec, row],
        out_specs=[pl.BlockSpec((1, LANES), lambda i, k: (0, 0)), row, vec],
        out_shape=[jax.ShapeDtypeStruct((1, LANES), F32), jax.ShapeDtypeStruct((t_dim, d), F32),
                   jax.ShapeDtypeStruct((1, d), F32)],
        scratch_shapes=[pltpu.VMEM((tt, d), F32), pltpu.VMEM((1, d), F32)],
        compiler_params=_params(("arbitrary", "arbitrary")),
    )(act3, wd3, h, g, target)


def _rope_table(pos, inv_lane, sel_a, sel_b, *, tt=512):
    t_dim = pos.shape[0]

    def body(p_ref, f_ref, a_ref, b_ref, c_ref, s_ref):
        ang = p_ref[...] * f_ref[...]
        on = (a_ref[...] + b_ref[...]) > 0.0
        c_ref[...] = jnp.where(on, jnp.cos(ang), 1.0)
        s_ref[...] = jnp.where(on, jnp.sin(ang), 0.0)

    vec = pl.BlockSpec((1, LANES), lambda i: (0, 0))
    row = pl.BlockSpec((tt, LANES), lambda i: (i, 0))
    shp = jax.ShapeDtypeStruct((t_dim, LANES), F32)
    return pl.pallas_call(
        body, name="rope_table", grid=(t_dim // tt,),
        in_specs=[pl.BlockSpec((tt, 1), lambda i: (i, 0)), vec, vec, vec],
        out_specs=[row, row], out_shape=[shp, shp],
        compiler_params=_params(("parallel",)),
    )(pos, inv_lane, sel_a, sel_b)


def _rotate(xv, cs, sn, sa, sb):
    half = ROPE_DIM // 2
    up = pltpu.roll(xv, LANES - half, 1)
    dn = pltpu.roll(xv, half, 1)
    return xv * cs + (dn * sb - up * sa) * sn


def _head_masks():
    h1 = lax.broadcasted_iota(jnp.int32, (1, LANES), 1) < HEAD_DIM
    return h1, jnp.logical_not(h1)


def _split_heads(xv, h1, h2):
    return jnp.where(h1, xv, 0.0).astype(BF16), jnp.where(h2, xv, 0.0).astype(BF16)


def _tri_masks():
    r = lax.broadcasted_iota(jnp.int32, (BLOCK, BLOCK), 0)
    c = lax.broadcasted_iota(jnp.int32, (BLOCK, BLOCK), 1)
    return c <= r, r <= c


def _stream_rows(start, dil):
    if dil == 1:
        return pl.ds(pl.multiple_of(start, BLOCK), BLOCK)
    return pl.ds(start, BLOCK, stride=dil)


def _dil_tile(idx, dil, nb):
    r = idx // nb
    n = idx % nb
    return (_stream_rows(r + dil * BLOCK * n, dil), _stream_rows(r + dil * BLOCK * jnp.maximum(n - 1, 0), dil),
            n > 0)


def _dil_specs(b_dim, s_dim):
    def col(c0):
        return pl.BlockSpec((None, s_dim, LANES),lambda b, h: (b, 0, c0 + h))
    tab = pl.BlockSpec((None, s_dim, LANES),lambda b, h: (b, 0, 0))
    vec = pl.BlockSpec((1, LANES), lambda b, h: (0, 0))
    return col, tab, vec


def _dil_fwd(proj3, cs3, sn3, sel_a, sel_b, rider=None):
    b_dim, s_dim, _ = proj3.shape
    scale = HEAD_DIM ** -0.5
    n_pat = len(DIL_PATTERNS)
    extra, extra_shapes, extra_sems = _rider_parts(rider)
    n_w = len(extra)
    n_steps = b_dim * PAIRS

    def body(*refs):
        q_ref, k_ref, v_ref, cs_ref, sn_ref, sa_ref, sb_ref = refs[:7]
        o16_ref, o32_ref, l_ref = refs[7 + n_w:10 + n_w]
        qr, kr = refs[10 + 2 * n_w:12 + 2 * n_w]
        per_pattern = refs[12 + 2 * n_w:12 + 2 * n_w + 2 * n_pat]
        og, lg = per_pattern[:n_pat], per_pattern[n_pat:]
        step = pl.program_id(0) * PAIRS + pl.program_id(1)
        begin, end = _rider_hooks(rider, refs[7:7 + n_w], refs[10 + n_w:10 + 2 * n_w], refs[-2:], step, n_steps)
        begin()
        h1, h2 = _head_masks()
        cur_ok, prev_ok = _tri_masks()
        sa, sb = sa_ref[...], sb_ref[...]

        def prep(j, _):
            rows = pl.ds(pl.multiple_of(j * DIL_CHUNK, DIL_CHUNK), DIL_CHUNK)
            cs, sn = cs_ref[rows, :], sn_ref[rows, :]
            qr[rows, :] = _rotate(q_ref[rows, :], cs, sn, sa, sb) * scale
            kr[rows, :] = _rotate(k_ref[rows, :], cs, sn, sa, sb)
            return 0

        lax.fori_loop(0, s_dim // DIL_CHUNK, prep, 0)

        for g, (_, dil) in enumerate(DIL_PATTERNS):
            nb = s_dim // dil // BLOCK

            def some(bi, _, g=g, dil=dil, nb=nb):
                tiles = [_dil_tile(bi * DIL_BATCH + t, dil, nb) for t in range(DIL_BATCH)]
                rows = [t[0] for t in tiles]
                q1, q2 = _split_heads(jnp.stack([qr[rw, :] for rw in rows]), h1, h2)
                kc = jnp.stack([kr[rw, :] for rw in rows]).astype(BF16)
                vc1, vc2 = _split_heads(jnp.stack([v_ref[rw, :] for rw in rows]), h1, h2)
                if nb > 1:
                    kp = jnp.stack([kr[t[1], :] for t in tiles]).astype(BF16)
                    vp1, vp2 = _split_heads(jnp.stack([v_ref[t[1], :] for t in tiles]), h1, h2)
                    p_ok = jnp.stack([jnp.logical_and(prev_ok, t[2]) for t in tiles])

                def head(qh, vch, vph):
                    sc = jnp.where(cur_ok, lax.dot_general(qh, kc, BNT, preferred_element_type=F32), -jnp.inf)
                    m = jnp.max(sc, axis=-1, keepdims=True)
                    if nb > 1:
                        sp = jnp.where(p_ok, lax.dot_general(qh, kp, BNT, preferred_element_type=F32), -jnp.inf)
                        m = jnp.maximum(m, jnp.max(sp, axis=-1, keepdims=True))
                    pc = jnp.exp(sc - m)
                    den = jnp.sum(pc, axis=-1, keepdims=True)
                    acc = lax.dot_general(pc.astype(BF16), vch, BNN, preferred_element_type=F32)
                    if nb > 1:
                        pp = jnp.exp(sp - m)
                        den = den + jnp.sum(pp, axis=-1, keepdims=True)
                        acc = acc + lax.dot_general(pp.astype(BF16), vph, BNN, preferred_element_type=F32)
                    return acc / den, m + jnp.log(den)

                o1, l1 = head(q1, vc1, vp1 if nb > 1 else None)
                o2, l2 = head(q2, vc2, vp2 if nb > 1 else None)
                o, l = o1 + o2, jnp.where(h1, l1, l2)
                for t, rw in enumerate(rows):
                    og[g][rw, :] = o[t]
                    lg[g][rw, :] = l[t]
                return 0

            lax.fori_loop(0, dil * nb // DIL_BATCH, some, 0)

        def comb(j, _):
            rows = pl.ds(pl.multiple_of(j * DIL_CHUNK, DIL_CHUNK), DIL_CHUNK)
            ls = [lg[g][rows, :] for g in range(n_pat)]
            m = jnp.maximum(jnp.maximum(ls[0], ls[1]), ls[2])
            es = [jnp.exp(l - m) for l in ls]
            den = es[0] + es[1] + es[2]
            o = (es[0] * og[0][rows, :] + es[1] * og[1][rows, :] + es[2] * og[2][rows, :]) / den
            o16_ref[rows, :] = o.astype(BF16)
            o32_ref[rows, :] = o
            l_ref[rows, :] = m + jnp.log(den)
            return 0

        lax.fori_loop(0, s_dim // DIL_CHUNK, comb, 0)
        end()

    col, tab, vec = _dil_specs(b_dim, s_dim)
    out = pl.BlockSpec((None, s_dim, LANES),lambda b, h: (b, 0, h))
    shp = (b_dim, s_dim, ATT_WIDTH)
    res = pl.pallas_call(
        body, name="dil_fwd", grid=(b_dim, PAIRS),
        in_specs=[col(COL_QA), col(COL_KA), col(COL_VA), tab, tab, vec, vec] + [ANY] * n_w,
        out_specs=[out, out, out] + [ANY] * n_w,
        out_shape=[jax.ShapeDtypeStruct(shp, BF16), jax.ShapeDtypeStruct(shp, F32), jax.ShapeDtypeStruct(shp, F32)]
        + extra_shapes,
        scratch_shapes=[pltpu.VMEM((s_dim, LANES), F32)] * (2 + 2 * n_pat) + extra_sems,
        compiler_params=_params(("arbitrary", "arbitrary")),
    )(proj3, proj3, proj3, cs3, sn3, sel_a, sel_b, *extra)
    return res[:3], res[3:]


def _dil_bwd(proj3, cs3, sn3, sel_a, sel_b, do3, o3, lse3, rider=None):
    b_dim, s_dim, _ = proj3.shape
    scale = HEAD_DIM ** -0.5
    extra, extra_shapes, extra_sems = _rider_parts(rider)
    n_w = len(extra)

    def body(*refs):
        q_ref, k_ref, v_ref, cs_ref, sn_ref, sa_ref, sb_ref, do_ref, o_ref, l_ref = refs[:10]
        dq_ref, dk_ref, dv_ref = refs[10 + n_w:13 + n_w]
        qr, kr, dqa, dka, dva = refs[13 + 2 * n_w:18 + 2 * n_w]
        step = pl.program_id(0) * PAIRS + pl.program_id(1)
        begin, end = _rider_hooks(rider, refs[10:10 + n_w], refs[13 + n_w:13 + 2 * n_w], refs[-2:], step,
                                  b_dim * PAIRS)
        begin()
        h1, h2 = _head_masks()
        cur_ok, prev_ok = _tri_masks()
        sa, sb = sa_ref[...], sb_ref[...]

        def prep(j, _):
            rows = pl.ds(pl.multiple_of(j * DIL_CHUNK, DIL_CHUNK), DIL_CHUNK)
            cs, sn = cs_ref[rows, :], sn_ref[rows, :]
            qr[rows, :] = _rotate(q_ref[rows, :], cs, sn, sa, sb) * scale
            kr[rows, :] = _rotate(k_ref[rows, :], cs, sn, sa, sb)
            zero = jnp.zeros((DIL_CHUNK, LANES), F32)
            dqa[rows, :] = zero
            dka[rows, :] = zero
            dva[rows, :] = zero
            return 0

        lax.fori_loop(0, s_dim // DIL_CHUNK, prep, 0)

        for _, dil in DIL_PATTERNS:
            nb = s_dim // dil // BLOCK

            def some(bi, _, dil=dil, nb=nb):
                tiles = [_dil_tile(bi * DIL_BATCH + t, dil, nb) for t in range(DIL_BATCH)]
                rows = [t[0] for t in tiles]
                q1, q2 = _split_heads(jnp.stack([qr[rw, :] for rw in rows]), h1, h2)
                dof = jnp.stack([do_ref[rw, :] for rw in rows])
                do1, do2 = _split_heads(dof, h1, h2)
                prod = dof * jnp.stack([o_ref[rw, :] for rw in rows])
                delta1 = jnp.sum(jnp.where(h1, prod, 0.0), axis=-1, keepdims=True)
                delta2 = jnp.sum(jnp.where(h2, prod, 0.0), axis=-1, keepdims=True)
                lt = jnp.stack([l_ref[rw, :] for rw in rows])
                lse1 = jnp.max(jnp.where(h1, lt, -jnp.inf), axis=-1, keepdims=True)
                lse2 = jnp.max(jnp.where(h2, lt, -jnp.inf), axis=-1, keepdims=True)

                def side(krows, ok):
                    kf = jnp.stack([kr[kw, :] for kw in krows])
                    k16 = kf.astype(BF16)
                    k1, k2 = _split_heads(kf, h1, h2)
                    v16 = jnp.stack([v_ref[kw, :] for kw in krows]).astype(BF16)

                    def head(qh, doh, lse, delta):
                        sc = lax.dot_general(qh, k16, BNT, preferred_element_type=F32)
                        p = jnp.where(ok, jnp.exp(sc - lse), 0.0)
                        dp = lax.dot_general(doh, v16, BNT, preferred_element_type=F32)
                        return p.astype(BF16), (p * (dp - delta)).astype(BF16)

                    p1, ds1 = head(q1, do1, lse1, delta1)
                    p2, ds2 = head(q2, do2, lse2, delta2)
                    dv = (lax.dot_general(p1, do1, BTN, preferred_element_type=F32)
                          + lax.dot_general(p2, do2, BTN, preferred_element_type=F32))
                    dk = (lax.dot_general(ds1, q1, BTN, preferred_element_type=F32)
                          + lax.dot_general(ds2, q2, BTN, preferred_element_type=F32))
                    for t, kw in enumerate(krows):
                        dva[kw, :] += dv[t]
                        dka[kw, :] += dk[t]
                    return (lax.dot_general(ds1, k1, BNN, preferred_element_type=F32)
                            + lax.dot_general(ds2, k2, BNN, preferred_element_type=F32))

                dq = side(rows, cur_ok)
                if nb > 1:
                    dq = dq + side([t[1] for t in tiles], jnp.stack([jnp.logical_and(prev_ok, t[2]) for t in tiles]))
                for t, rw in enumerate(rows):
                    dqa[rw, :] += dq[t] * scale
                return 0

            lax.fori_loop(0, dil * nb // DIL_BATCH, some, 0)

        def finish(j, _):
            rows = pl.ds(pl.multiple_of(j * DIL_CHUNK, DIL_CHUNK), DIL_CHUNK)
            cs, sn = cs_ref[rows, :], -sn_ref[rows, :]
            dq_ref[rows, :] = _rotate(dqa[rows, :], cs, sn, sa, sb).astype(BF16)
            dk_ref[rows, :] = _rotate(dka[rows, :], cs, sn, sa, sb).astype(BF16)
            dv_ref[rows, :] = dva[rows, :].astype(BF16)
            return 0

        lax.fori_loop(0, s_dim // DIL_CHUNK, finish, 0)
        end()

    col, tab, vec = _dil_specs(b_dim, s_dim)
    out = pl.BlockSpec((None, s_dim, LANES),lambda b, h: (b, 0, h))
    shp = jax.ShapeDtypeStruct((b_dim, s_dim, ATT_WIDTH), BF16)
    acc = pltpu.VMEM((s_dim, LANES), F32)
    res = pl.pallas_call(
        body, name="dil_bwd", grid=(b_dim, PAIRS),
        in_specs=[col(COL_QA), col(COL_KA), col(COL_VA), tab, tab, vec, vec, out, out, out] + [ANY] * n_w,
        out_specs=[out, out, out] + [ANY] * n_w, out_shape=[shp, shp, shp] + extra_shapes,
        scratch_shapes=[acc, acc, acc, acc, acc] + extra_sems,
        compiler_params=_params(("arbitrary", "arbitrary")),
    )(proj3, proj3, proj3, cs3, sn3, sel_a, sel_b, do3, o3, lse3, *extra)
    return res[:3], res[3:]


def _split_dot(x, tri):
    hi = x.astype(BF16)
    lo = (x - hi.astype(F32)).astype(BF16)
    return jnp.dot(hi, tri, preferred_element_type=F32) + jnp.dot(lo, tri, preferred_element_type=F32)


def _log_sigmoid(z):
    return jnp.minimum(z, 0.0) - jnp.log(1.0 + jnp.exp(-jnp.abs(z)))


def _sb_scores(qh, k16, valid):
    z = lax.dot_general(qh, k16, NT, preferred_element_type=F32)
    ls = _log_sigmoid(z)
    l1m = ls - z
    return ls, (l1m if valid is None else jnp.where(valid, l1m, 0.0))


def _sb_consts():
    r = lax.broadcasted_iota(jnp.int32, (BLOCK, BLOCK), 0)
    c = lax.broadcasted_iota(jnp.int32, (BLOCK, BLOCK), 1)
    after = (r > c).astype(BF16)
    before = (r < c).astype(BF16)
    qrow = lax.broadcasted_iota(jnp.int32, (SB_ROWS, BLOCK), 0)
    kcol = lax.broadcasted_iota(jnp.int32, (SB_ROWS, BLOCK), 1)
    return after, before, qrow, kcol


def _below(whole, lo, delta):
    if lo == 0:
        return whole + delta
    return whole + jnp.concatenate([jnp.zeros((lo,) + delta.shape[1:], delta.dtype), delta], axis=0)


def _pairs_loop(n_blocks, step, carry):
    def several(i, c):
        for j in range(SB_STEP):
            c = step(SB_STEP * i + j, c)
        return c

    return lax.fori_loop(0, n_blocks // SB_STEP, several, carry)


def _sb_fwd(proj3, rider=None):
    b_dim, s_dim, _ = proj3.shape
    scale = HEAD_DIM ** -0.5
    per = SB_ROWS // BLOCK
    extra, extra_shapes, extra_sems = _rider_parts(rider)
    n_w = len(extra)

    def body(*refs):
        q_ref, k_ref, v_ref = refs[:3]
        o_ref = refs[3 + n_w]
        step = pl.program_id(0) * PAIRS + pl.program_id(1)
        begin, end = _rider_hooks(rider, refs[3:3 + n_w], refs[4 + n_w:4 + 2 * n_w], refs[-2:], step, b_dim * PAIRS)
        begin()
        h1, h2 = _head_masks()
        after, _, qrow, kcol = _sb_consts()

        def qloop(qi, _):
            rows = pl.ds(pl.multiple_of(qi * SB_ROWS, SB_ROWS), SB_ROWS)
            q1, q2 = _split_heads(q_ref[rows, :] * scale, h1, h2)
            first = qi * per

            def block(kb, carry, lo):
                acc, run1, run2 = carry
                krows = pl.ds(pl.multiple_of(kb * BLOCK, BLOCK), BLOCK)
                k16 = k_ref[krows, :].astype(BF16)
                v1, v2 = _split_heads(v_ref[krows, :], h1, h2)
                valid = None if lo is None else kcol[:SB_ROWS - lo] < qrow[:SB_ROWS - lo]
                lo = lo or 0

                def head(qh, vh, run):
                    ls, l1m = _sb_scores(qh[lo:], k16, valid)
                    a = jnp.exp(ls + _split_dot(l1m, after) + run[lo:])
                    if valid is not None:
                        a = jnp.where(valid, a, 0.0)
                    return (jnp.dot(a.astype(BF16), vh, preferred_element_type=F32),
                            _below(run, lo, jnp.sum(l1m, axis=-1, keepdims=True)))

                o1, run1 = head(q1, v1, run1)
                o2, run2 = head(q2, v2, run2)
                return _below(acc, lo, o1 + o2), run1, run2

            zcol = jnp.zeros((SB_ROWS, 1), F32)
            carry = (jnp.zeros((SB_ROWS, LANES), F32), zcol, zcol)
            for kl in reversed(range(per)):
                carry = block(first + kl, carry, kl * BLOCK)
            acc, _, _ = _pairs_loop(first, lambda i, c: block(first - 1 - i, c, None), carry)
            o_ref[rows, :] = acc.astype(BF16)
            return 0

        lax.fori_loop(0, s_dim // SB_ROWS, qloop, 0)
        end()

    def col(c0):
        return pl.BlockSpec((None, s_dim, LANES),lambda b, h: (b, 0, c0 + h))

    res = pl.pallas_call(
        body, name="sb_fwd", grid=(b_dim, PAIRS),
        in_specs=[col(COL_QB), col(COL_KB), col(COL_VB)] + [ANY] * n_w, out_specs=[col(0)] + [ANY] * n_w,
        out_shape=[jax.ShapeDtypeStruct((b_dim, s_dim, ATT_WIDTH), BF16)] + extra_shapes,
        scratch_shapes=extra_sems,
        compiler_params=_params(("arbitrary", "arbitrary")),
    )(proj3, proj3, proj3, *extra)
    return res[0], res[1:]


def _sb_bwd(proj3, do3, rider=None):
    b_dim, s_dim, _ = proj3.shape
    scale = HEAD_DIM ** -0.5
    per = SB_ROWS // BLOCK
    nkb_max = s_dim // BLOCK
    extra, extra_shapes, extra_sems = _rider_parts(rider)
    n_w = len(extra)

    def body(*refs):
        q_ref, k_ref, v_ref, do_ref = refs[:4]
        dq_ref, dk_ref, dv_ref = refs[4 + n_w:7 + n_w]
        dka, dva, e_ref, sg_ref = refs[7 + 2 * n_w:11 + 2 * n_w]
        step = pl.program_id(0) * PAIRS + pl.program_id(1)
        begin, end = _rider_hooks(rider, refs[4:4 + n_w], refs[7 + n_w:7 + 2 * n_w], refs[-2:], step, b_dim * PAIRS)
        begin()
        h1, h2 = _head_masks()
        after, before, qrow, kcol = _sb_consts()
        dka[...] = jnp.zeros_like(dka)
        dva[...] = jnp.zeros_like(dva)

        def qloop(qi, _):
            rows = pl.ds(pl.multiple_of(qi * SB_ROWS, SB_ROWS), SB_ROWS)
            q1, q2 = _split_heads(q_ref[rows, :] * scale, h1, h2)
            do1, do2 = _split_heads(do_ref[rows, :].astype(F32), h1, h2)
            first = qi * per

            def pass1(kb, carry, lo):
                run1, run2 = carry
                krows = pl.ds(pl.multiple_of(kb * BLOCK, BLOCK), BLOCK)
                k16 = k_ref[krows, :].astype(BF16)
                v16 = v_ref[krows, :].astype(BF16)
                valid = None if lo is None else kcol[:SB_ROWS - lo] < qrow[:SB_ROWS - lo]
                lo = lo or 0
                part = pl.ds(lo, SB_ROWS - lo)

                def head(h, qh, doh, run):
                    ls, l1m = _sb_scores(qh[lo:], k16, valid)
                    a = jnp.exp(ls + _split_dot(l1m, after) + run[lo:])
                    if valid is not None:
                        a = jnp.where(valid, a, 0.0)
                    da = lax.dot_general(doh[lo:], v16, NT, preferred_element_type=F32)
                    e_ref[h, kb, part, :] = a * da
                    sg_ref[h, kb, part, :] = jnp.exp(ls)
                    return a.astype(BF16), _below(run, lo, jnp.sum(l1m, axis=-1, keepdims=True))

                a1, run1 = head(0, q1, do1, run1)
                a2, run2 = head(1, q2, do2, run2)
                dva[krows, :] += (lax.dot_general(a1, do1[lo:], TN, preferred_element_type=F32)
                                  + lax.dot_general(a2, do2[lo:], TN, preferred_element_type=F32))
                return run1, run2

            zcol = jnp.zeros((SB_ROWS, 1), F32)
            carry = (zcol, zcol)
            for kl in reversed(range(per)):
                carry = pass1(first + kl, carry, kl * BLOCK)
            _pairs_loop(first, lambda i, c: pass1(first - 1 - i, c, None), carry)

            def pass2(kb, carry, lo):
                dq, pre1, pre2 = carry
                krows = pl.ds(pl.multiple_of(kb * BLOCK, BLOCK), BLOCK)
                k1, k2 = _split_heads(k_ref[krows, :], h1, h2)
                valid = None if lo is None else kcol[:SB_ROWS - lo] < qrow[:SB_ROWS - lo]
                lo = lo or 0
                part = pl.ds(lo, SB_ROWS - lo)

                def head(h, pre):
                    ev = e_ref[h, kb, part, :]
                    sg = sg_ref[h, kb, part, :]
                    dz = ev * (1.0 - sg) - (_split_dot(ev, before) + pre[lo:]) * sg
                    if valid is not None:
                        dz = jnp.where(valid, dz, 0.0)
                    return dz.astype(BF16), _below(pre, lo, jnp.sum(ev, axis=-1, keepdims=True))

                dz1, pre1 = head(0, pre1)
                dz2, pre2 = head(1, pre2)
                dka[krows, :] += (lax.dot_general(dz1, q1[lo:], TN, preferred_element_type=F32)
                                  + lax.dot_general(dz2, q2[lo:], TN, preferred_element_type=F32))
                dq = _below(dq, lo, jnp.dot(dz1, k1, preferred_element_type=F32)
                            + jnp.dot(dz2, k2, preferred_element_type=F32))
                return dq, pre1, pre2

            carry = _pairs_loop(first, lambda i, c: pass2(i, c, None), (jnp.zeros((SB_ROWS, LANES), F32), zcol, zcol))
            for kl in range(per):
                carry = pass2(first + kl, carry, kl * BLOCK)
            dq = carry[0]
            dq_ref[rows, :] = (dq * scale).astype(BF16)
            return 0

        lax.fori_loop(0, s_dim // SB_ROWS, qloop, 0)
        dk_ref[...] = dka[...].astype(BF16)
        dv_ref[...] = dva[...].astype(BF16)
        end()

    def col(c0):
        return pl.BlockSpec((None, s_dim, LANES),lambda b, h: (b, 0, c0 + h))

    shp = jax.ShapeDtypeStruct((b_dim, s_dim, ATT_WIDTH), BF16)
    acc = pltpu.VMEM((s_dim, LANES), F32)
    strip = pltpu.VMEM((2, nkb_max, SB_ROWS, BLOCK), F32)
    res = pl.pallas_call(
        body, name="sb_bwd", grid=(b_dim, PAIRS),
        in_specs=[col(COL_QB), col(COL_KB), col(COL_VB), col(0)] + [ANY] * n_w,
        out_specs=[col(0), col(0), col(0)] + [ANY] * n_w,
        out_shape=[shp, shp, shp] + extra_shapes,
        scratch_shapes=[acc, acc, strip, strip] + extra_sems,
        compiler_params=_params(("arbitrary", "arbitrary")),
    )(proj3, proj3, proj3, do3, *extra)
    return res[:3], res[3:]


def _sigmoid(x):
    return 1.0 / (1.0 + jnp.exp(-x))


def _gate_out_norm(proj, ua, ub, w_out, x, g, *, tt=512):
    t_dim, d = ua.shape

    def body(ga_ref, gb_ref, ua_ref, ub_ref, w_ref, x_ref, g_ref, m_ref, h_ref, n_ref):
        mixed = (_sigmoid(ga_ref[...]) * ua_ref[...] + _sigmoid(gb_ref[...]) * ub_ref[...]).astype(BF16)
        m_ref[...] = mixed
        hv = x_ref[...] + jnp.dot(mixed, w_ref[...], preferred_element_type=F32)
        h_ref[...] = hv
        r = lax.rsqrt(jnp.mean(hv * hv, axis=-1, keepdims=True) + RMS_EPS)
        n_ref[...] = ((hv * r) * g_ref[...]).astype(BF16)

    row = pl.BlockSpec((tt, d), lambda i: (i, 0))
    return pl.pallas_call(
        body, name="gate_out_norm", grid=(t_dim // tt,),
        in_specs=[pl.BlockSpec((tt, d), lambda i: (i, 3)), pl.BlockSpec((tt, d), lambda i: (i, 4)), row, row,
                  pl.BlockSpec((d, d), lambda i: (0, 0)), row, pl.BlockSpec((1, d), lambda i: (0, 0))],
        out_specs=[row, row, row],
        out_shape=[jax.ShapeDtypeStruct((t_dim, d), BF16), jax.ShapeDtypeStruct((t_dim, d), F32),
                   jax.ShapeDtypeStruct((t_dim, d), BF16)],
        compiler_params=_params(("parallel",)),
    )(proj, proj, ua, ub, w_out, x, g)


def _out_dx_gate_bwd(dh, w_out, proj, ua, ub, *, tt=512):
    t_dim, d = ua.shape

    def body(dh_ref, w_ref, ga_ref, gb_ref, ua_ref, ub_ref, dua_ref, dub_ref, dg_ref):
        dm = lax.dot_general(dh_ref[...].astype(BF16), w_ref[...], NT, preferred_element_type=F32)
        sa = _sigmoid(ga_ref[...])
        sb = _sigmoid(gb_ref[...])
        dua_ref[...] = (dm * sa).astype(BF16)
        dub_ref[...] = (dm * sb).astype(BF16)
        dg_ref[:, :d] = (dm * ua_ref[...] * (sa * (1.0 - sa))).astype(BF16)
        dg_ref[:, d:] = (dm * ub_ref[...] * (sb * (1.0 - sb))).astype(BF16)

    row = pl.BlockSpec((tt, d), lambda i: (i, 0))
    wide = pl.BlockSpec((tt, 2 * d), lambda i: (i, 0))
    return pl.pallas_call(
        body, name="out_dx_gate_bwd", grid=(t_dim // tt,),
        in_specs=[row, pl.BlockSpec((d, d), lambda i: (0, 0)),
                  pl.BlockSpec((tt, d), lambda i: (i, 3)), pl.BlockSpec((tt, d), lambda i: (i, 4)), row, row],
        out_specs=[row, row, wide],
        out_shape=[jax.ShapeDtypeStruct((t_dim, d), BF16), jax.ShapeDtypeStruct((t_dim, d), BF16),
                   jax.ShapeDtypeStruct((t_dim, 2 * d), BF16)],
        compiler_params=_params(("parallel",)),
    )(dh, w_out, proj, proj, ua, ub)


def _ffn_up_swiglu(n, wg3, wu3, *, tt=1024):
    t_dim, d = n.shape
    n_s, f4, _ = wg3.shape

    def body(n_ref, wg_ref, wu_ref, g_ref, u_ref, a_ref):
        nv = n_ref[...]
        gv = lax.dot_general(nv, wg_ref[...], NT, preferred_element_type=F32)
        uv = lax.dot_general(nv, wu_ref[...], NT, preferred_element_type=F32)
        g_ref[...] = gv.astype(BF16)
        u_ref[...] = uv.astype(BF16)
        a_ref[...] = (gv * _sigmoid(gv) * uv).astype(BF16)

    wspec = pl.BlockSpec((None, f4, d), lambda i, s: (s, 0, 0))
    ospec = pl.BlockSpec((None, tt, f4), lambda i, s: (s, i, 0))
    shp = (n_s, t_dim, f4)
    return pl.pallas_call(
        body, name="ffn_up_swiglu", grid=(t_dim // tt, n_s),
        in_specs=[pl.BlockSpec((tt, d), lambda i, s: (i, 0)), wspec, wspec], out_specs=[ospec, ospec, ospec],
        out_shape=[jax.ShapeDtypeStruct(shp, BF16)] * 3,
        compiler_params=_params(("parallel", "parallel")),
    )(n, wg3, wu3)


def _ffn_down_dx_swiglu(dh, wd3, g3, u3, *, tt=1024):
    t_dim, d = dh.shape
    n_s, f4, _ = wd3.shape

    def body(dh_ref, w_ref, g_ref, u_ref, dg_ref, du_ref):
        da = lax.dot_general(dh_ref[...].astype(BF16), w_ref[...], NT, preferred_element_type=F32)
        gv = g_ref[...].astype(F32)
        sg = _sigmoid(gv)
        dg_ref[...] = (da * u_ref[...].astype(F32) * (sg + gv * sg * (1.0 - sg))).astype(BF16)
        du_ref[...] = (da * (gv * sg)).astype(BF16)

    spec = pl.BlockSpec((None, tt, f4), lambda i, s: (s, i, 0))
    shp = jax.ShapeDtypeStruct((n_s, t_dim, f4), BF16)
    return pl.pallas_call(
        body, name="ffn_down_dx_swiglu", grid=(t_dim // tt, n_s),
        in_specs=[pl.BlockSpec((tt, d), lambda i, s: (i, 0)), pl.BlockSpec((None, f4, d), lambda i, s: (s, 0, 0)),
                  spec, spec],
        out_specs=[spec, spec], out_shape=[shp, shp],
        compiler_params=_params(("parallel", "parallel")),
    )(dh, wd3, g3, u3)


def _mem_fwd(qm, kvm, *, tt=2048):
    b_dim, s_dim, _ = qm.shape
    n_mem = kvm.shape[1]
    scale = MEM_HEAD_DIM ** -0.5

    def body(q_ref, k_ref, v_ref, o_ref):
        sc = lax.dot_general(q_ref[0], k_ref[0], NT, preferred_element_type=F32) * scale
        p = jnp.exp(sc - jnp.max(sc, axis=-1, keepdims=True))
        p = p / jnp.sum(p, axis=-1, keepdims=True)
        o_ref[0] = jnp.dot(p.astype(BF16), v_ref[0], preferred_element_type=F32).astype(BF16)

    qs = pl.BlockSpec((1, tt, MEM_HEAD_DIM), lambda b, h, i: (b, i, h))
    return pl.pallas_call(
        body, name="mem_fwd", grid=(b_dim, N_HEADS_MEM, s_dim // tt),
        in_specs=[qs, pl.BlockSpec((1, n_mem, MEM_HEAD_DIM), lambda b, h, i: (b, 0, h)),
                  pl.BlockSpec((1, n_mem, MEM_HEAD_DIM), lambda b, h, i: (b, 0, N_HEADS_MEM + h))],
        out_specs=qs, out_shape=jax.ShapeDtypeStruct(qm.shape, BF16),
        compiler_params=_params(("parallel", "parallel", "parallel")),
    )(qm, kvm, kvm)


def _mem_bwd(qm, kvm, dom, *, tt=2048):
    b_dim, s_dim, _ = qm.shape
    n_mem = kvm.shape[1]
    scale = MEM_HEAD_DIM ** -0.5

    def body(q_ref, k_ref, v_ref, do_ref, dq_ref, dk_ref, dv_ref):
        qv, kv, vv, dov = q_ref[0], k_ref[0], v_ref[0], do_ref[0]
        sc = lax.dot_general(qv, kv, NT, preferred_element_type=F32) * scale
        p = jnp.exp(sc - jnp.max(sc, axis=-1, keepdims=True))
        p = p / jnp.sum(p, axis=-1, keepdims=True)
        dp = lax.dot_general(dov, vv, NT, preferred_element_type=F32)
        ds = (p * (dp - jnp.sum(p * dp, axis=-1, keepdims=True)) * scale).astype(BF16)
        dq_ref[0] = jnp.dot(ds, kv, preferred_element_type=F32).astype(BF16)

        @pl.when(pl.program_id(2) == 0)
        def _():
            dk_ref[...] = jnp.zeros_like(dk_ref)
            dv_ref[...] = jnp.zeros_like(dv_ref)

        dk_ref[0] += lax.dot_general(ds, qv, TN, preferred_element_type=F32)
        dv_ref[0] += lax.dot_general(p.astype(BF16), dov, TN, preferred_element_type=F32)

    qs = pl.BlockSpec((1, tt, MEM_HEAD_DIM), lambda b, h, i: (b, i, h))
    ks = pl.BlockSpec((1, n_mem, MEM_HEAD_DIM), lambda b, h, i: (b, 0, h))
    vs = pl.BlockSpec((1, n_mem, MEM_HEAD_DIM), lambda b, h, i: (b, 0, N_HEADS_MEM + h))
    return pl.pallas_call(
        body, name="mem_bwd", grid=(b_dim, N_HEADS_MEM, s_dim // tt),
        in_specs=[qs, ks, vs, qs], out_specs=[qs, ks, ks],
        out_shape=[jax.ShapeDtypeStruct(qm.shape, BF16), jax.ShapeDtypeStruct((b_dim, n_mem, MEM_WIDTH), F32),
                   jax.ShapeDtypeStruct((b_dim, n_mem, MEM_WIDTH), F32)],
        compiler_params=_params(("parallel", "parallel", "arbitrary")),
    )(qm, kvm, kvm, dom)


def _adamw_math(wv, gv, mv, vv):
    nm = ADAM_B1 * mv + (1.0 - ADAM_B1) * gv
    nv = ADAM_B2 * vv + (1.0 - ADAM_B2) * (gv * gv)
    m_hat = nm / (1.0 - ADAM_B1 ** ADAM_STEP)
    v_hat = nv / (1.0 - ADAM_B2 ** ADAM_STEP)
    return -ADAM_LR * (m_hat / (jnp.sqrt(v_hat) + ADAM_EPS) + ADAM_WD * wv), nm, nv


def _adamw(w, g, m, v, *, name):
    rows, cols = w.shape
    tr = _tile(rows, 256, 8)

    def body(w_ref, g_ref, m_ref, v_ref, d_ref, nm_ref, nv_ref):
        d_ref[...], nm_ref[...], nv_ref[...] = _adamw_math(w_ref[...], g_ref[...], m_ref[...], v_ref[...])

    spec = pl.BlockSpec((tr, cols), lambda i: (i, 0))
    shp = jax.ShapeDtypeStruct((rows, cols), F32)
    return pl.pallas_call(
        body, name=name, grid=(rows // tr,),
        in_specs=[spec] * 4, out_specs=[spec] * 3, out_shape=[shp] * 3,
        compiler_params=_params(("parallel",)),
    )(w, g, m, v)


def _prefetch_spec(grid, in_specs, out_specs):
    return pltpu.PrefetchScalarGridSpec(num_scalar_prefetch=1, grid=grid, in_specs=in_specs, out_specs=out_specs)


def _adamw_halves(w, mine, theirs, m, v, c_idx, *, name):
    rows, cols = w.shape
    half = rows // 2
    tr = _tile(half, _row_cap(cols), 8)
    nh = half // tr

    def body(c_ref, w_ref, mine_ref, theirs_ref, m_ref, v_ref, g_ref, d_ref, nm_ref, nv_ref):
        gv = jnp.where(pl.program_id(0) == c_ref[0], mine_ref[...], theirs_ref[...])
        g_ref[...] = gv
        d_ref[...], nm_ref[...], nv_ref[...] = _adamw_math(w_ref[...], gv, m_ref[...], v_ref[...])

    full = pl.BlockSpec((tr, cols), lambda h, i, c_ref: (h * nh + i, 0))
    part = pl.BlockSpec((tr, cols), lambda h, i, c_ref: (i, 0))
    shp = jax.ShapeDtypeStruct((rows, cols), F32)
    return pl.pallas_call(
        body, name=name, grid_spec=_prefetch_spec((2, nh), [full, part, part, full, full], [full] * 4),
        out_shape=[shp] * 4,
        compiler_params=_params(("parallel", "parallel")),
    )(c_idx, w, mine, theirs, m, v)


def _pair_sum(g3, theirs, c_idx, *, name):
    n, rows, cols = g3.shape
    half = rows // 2
    tr = _tile(half, _row_cap(cols), 16)

    def body(c_ref, g_ref, t_ref, o_ref):
        o_ref[...] = (g_ref[...] + t_ref[...]).astype(BF16)

    part = pl.BlockSpec((None, tr, cols), lambda s, i, c_ref: (s, i, 0))
    return pl.pallas_call(
        body, name=name,
        grid_spec=_prefetch_spec((n, half // tr),
                                 [pl.BlockSpec((None, None, tr, cols), lambda s, i, c_ref: (s, c_ref[0], i, 0)), part],
                                 part),
        out_shape=jax.ShapeDtypeStruct((n, half, cols), BF16),
        compiler_params=_params(("parallel", "parallel")),
    )(c_idx, g3.reshape(n, 2, half, cols), theirs)


def _chip_sum(pair, recv, s_idx, *, name):
    _, half, cols = pair.shape
    tr = _tile(half, _row_cap(cols), 16)

    def body(s_ref, p_ref, r_ref, o_ref):
        o_ref[...] = ((p_ref[...].astype(F32) + r_ref[0].astype(F32)) + r_ref[1].astype(F32)) + r_ref[2].astype(F32)

    return pl.pallas_call(
        body, name=name,
        grid_spec=_prefetch_spec((half // tr,),
                                 [pl.BlockSpec((None, tr, cols), lambda i, s_ref: (s_ref[0], i, 0)),
                                  pl.BlockSpec((N_CHIPS - 1, tr, cols), lambda i, s_ref: (0, i, 0))],
                                 pl.BlockSpec((tr, cols), lambda i, s_ref: (i, 0))),
        out_shape=jax.ShapeDtypeStruct((half, cols), F32),
        compiler_params=_params(("parallel",)),
    )(s_idx, pair, recv)


def _sum8(parts):
    n, rows, cols = parts.shape

    def body(p_ref, o_ref):
        acc = p_ref[0]
        for i in range(1, n):
            acc = acc + p_ref[i]
        o_ref[...] = acc

    return pl.pallas_call(
        body, name="small_sum", grid=(1,),
        in_specs=[pl.BlockSpec((n, rows, cols), lambda i: (0, 0, 0))],
        out_specs=pl.BlockSpec((rows, cols), lambda i: (0, 0)),
        out_shape=jax.ShapeDtypeStruct((rows, cols), parts.dtype),
        compiler_params=_params(("arbitrary",)),
    )(parts)


def _place():
    return lax.axis_index("x"), lax.axis_index("y"), lax.axis_index("c")


ANY = pl.BlockSpec(memory_space=pl.ANY)


def _rider_parts(rider):
    if rider is None:
        return (), [], []
    kind, arrays = rider
    n = len(arrays)
    shapes = {"gather": _gathered_shapes, "pair": _pair_shapes, "chip": _chip_shapes}[kind](arrays)
    sems = _gather_sems(n) if kind == "gather" else _exchange_sems(n if kind == "pair" else 3 * n)
    return tuple(arrays), shapes, sems


def _rider_hooks(rider, ins, outs, sems, step, n_steps):
    if rider is None:
        return (lambda: None), (lambda: None)
    if rider[0] == "gather":
        start, forward, finish = _gather_steps(ins, outs, *sems)
    else:
        start, finish = {"pair": _pair_steps, "chip": _chip_steps}[rider[0]](ins, outs, *sems)
        forward = None

    def begin():
        pl.when(step == 0)(start)

    def end():
        if forward is not None:
            pl.when(step == n_steps - 2)(forward)
        pl.when(step == n_steps - 1)(finish)

    return begin, end


def _gathered_shapes(shards):
    return [jax.ShapeDtypeStruct((N_CHIPS,) + s.shape, s.dtype) for s in shards]


def _gather_sems(n):
    return [pltpu.SemaphoreType.DMA((7 * n,)), pltpu.SemaphoreType.DMA((7 * n,))]


def _gather_steps(ins, outs, send_sems, recv_sems):
    n = len(ins)
    halves = [r.shape[0] // 2 for r in ins]
    x, y, c = _place()
    my_chip = 2 * x + y
    me, sibling = (x, y, c), (x, y, 1 - c)
    chips = [(1 - x, y), (x, 1 - y), (1 - x, 1 - y)]

    def half_of(w, chip, pc):
        return outs[w].at[chip, pl.ds(pc * halves[w], halves[w]), :]

    def copy(w, k, src, dst, to):
        return pltpu.make_async_remote_copy(
            src_ref=src, dst_ref=dst, send_sem=send_sems.at[7 * w + k], recv_sem=recv_sems.at[7 * w + k],
            device_id=to, device_id_type=MESH)

    def firsts():
        cps = []
        for w in range(n):
            cps.append(copy(w, 0, ins[w], outs[w].at[my_chip], sibling))
            mine = ins[w].at[pl.ds(c * halves[w], halves[w]), :]
            for j, (px, py) in enumerate(chips):
                cps.append(copy(w, 1 + j, mine, half_of(w, my_chip, c), (px, py, c)))
        return cps

    def passes():
        return [copy(w, 4 + j, half_of(w, 2 * px + py, c), half_of(w, 2 * px + py, c), sibling)
                for w in range(n) for j, (px, py) in enumerate(chips)]

    def start():
        for cp in firsts():
            cp.start()

    def forward():
        fws = passes()
        for w in range(n):
            for j, (px, py) in enumerate(chips):
                landed = half_of(w, 2 * px + py, c)
                copy(w, 1 + j, landed, landed, me).wait_recv()
                fws[3 * w + j].start()

    def finish():
        for w in range(n):
            copy(w, 0, ins[w], outs[w].at[my_chip], me).wait_recv()
            for j, (px, py) in enumerate(chips):
                landed = half_of(w, 2 * px + py, 1 - c)
                copy(w, 4 + j, landed, landed, me).wait_recv()
        for cp in firsts() + passes():
            cp.wait_send()

    return start, forward, finish


def _pair_shapes(grads):
    return [jax.ShapeDtypeStruct((g.shape[0], g.shape[1] // 2, g.shape[2]), g.dtype) for g in grads]


def _exchange_sems(n):
    return [pltpu.SemaphoreType.DMA((n,)), pltpu.SemaphoreType.DMA((n,))]


def _exchange_steps(copies):
    def start():
        for cp in copies():
            cp.start()

    def finish():
        for cp in copies():
            cp.wait()

    return start, finish


def _pair_steps(ins, outs, send_sems, recv_sems):
    x, y, c = _place()

    def copies():
        return [pltpu.make_async_remote_copy(
            src_ref=ins[w].at[:, pl.ds((1 - c) * (ins[w].shape[1] // 2), ins[w].shape[1] // 2), :], dst_ref=outs[w],
            send_sem=send_sems.at[w], recv_sem=recv_sems.at[w], device_id=(x, y, 1 - c), device_id_type=MESH)
            for w in range(len(ins))]

    return _exchange_steps(copies)


def _chip_shapes(pairs):
    return [jax.ShapeDtypeStruct((N_CHIPS - 1,) + p.shape[1:], p.dtype) for p in pairs]


def _chip_steps(ins, outs, send_sems, recv_sems):
    x, y, c = _place()
    others = [(1 - x, y), (x, 1 - y), (1 - x, 1 - y)]

    def copies():
        return [pltpu.make_async_remote_copy(
            src_ref=ins[w].at[2 * px + py], dst_ref=outs[w].at[j],
            send_sem=send_sems.at[3 * w + j], recv_sem=recv_sems.at[3 * w + j],
            device_id=(px, py, c), device_id_type=MESH)
            for w in range(len(ins)) for j, (px, py) in enumerate(others)]

    return _exchange_steps(copies)


def _swap_halves(mine):
    n = len(mine)

    def body(*refs):
        ins, outs, send_sems, recv_sems = refs[:n], refs[n:2 * n], refs[2 * n], refs[2 * n + 1]
        x, y, c = _place()
        copies = [pltpu.make_async_remote_copy(
            src_ref=ins[w], dst_ref=outs[w], send_sem=send_sems.at[w], recv_sem=recv_sems.at[w],
            device_id=(x, y, 1 - c), device_id_type=MESH) for w in range(n)]
        for cp in copies:
            cp.start()
        for cp in copies:
            cp.wait()

    return pl.pallas_call(
        body, name="grad_swap_halves",
        out_shape=[jax.ShapeDtypeStruct(h.shape, h.dtype) for h in mine],
        in_specs=[ANY] * n, out_specs=[ANY] * n,
        scratch_shapes=[pltpu.SemaphoreType.DMA((n,)), pltpu.SemaphoreType.DMA((n,))],
    )(*mine)


def _gather_small(small):
    srows, cols = small.shape

    def body(s_ref, all_ref, send_sems, recv_sems, local_sem):
        x, y, c = _place()
        me = 4 * x + 2 * y + c
        keep_small = pltpu.make_async_copy(s_ref, all_ref.at[me], local_sem)
        keep_small.start()
        sends = []
        for kk in range(1, 8):
            peer = (x ^ (kk >> 2), y ^ ((kk >> 1) & 1), c ^ (kk & 1))
            sends.append(pltpu.make_async_remote_copy(
                src_ref=s_ref, dst_ref=all_ref.at[me],
                send_sem=send_sems.at[kk], recv_sem=recv_sems.at[kk], device_id=peer, device_id_type=MESH))
        for cp in sends:
            cp.start()
        for kk in range(1, 8):
            px, py, pc = x ^ (kk >> 2), y ^ ((kk >> 1) & 1), c ^ (kk & 1)
            pltpu.make_async_remote_copy(
                src_ref=s_ref, dst_ref=all_ref.at[4 * px + 2 * py + pc],
                send_sem=send_sems.at[kk], recv_sem=recv_sems.at[kk], device_id=(px, py, pc),
                device_id_type=MESH).wait_recv()
        for cp in sends:
            cp.wait_send()
        keep_small.wait()

    return pl.pallas_call(
        body, name="gather_small",
        out_shape=jax.ShapeDtypeStruct((8, srows, cols), small.dtype),
        in_specs=[ANY], out_specs=ANY,
        scratch_shapes=[pltpu.SemaphoreType.DMA((8,)), pltpu.SemaphoreType.DMA((8,)), pltpu.SemaphoreType.DMA],
    )(small)


SHARDED = (("w_in", D_MODEL, IN_COLS, 1), ("w_up_a", ATT_WIDTH, D_MODEL, 1), ("w_up_b", ATT_WIDTH, D_MODEL, 1),
           ("w_out", D_MODEL, D_MODEL, 0), ("w_q_mem", D_MODEL, MEM_WIDTH, 0), ("w_kv_mem", D_MODEL, 2 * MEM_WIDTH, 0),
           ("w_o_mem", MEM_WIDTH, D_MODEL, 1), ("w_ffn_gate", D_FF, D_MODEL, 0), ("w_ffn_up", D_FF, D_MODEL, 0),
           ("w_ffn_down", D_FF, D_MODEL, 0))
TRANSPOSED = ("w_ffn_gate", "w_ffn_up")
NAMES = tuple(n for n, _, _, _ in SHARDED)


def _held(name, shard):
    return shard.T if name in TRANSPOSED else shard
EARLY, LATE = NAMES[:1], NAMES[1:]
GAINS = ("g_mix", "g_mem_q", "g_mem_kv", "g_ffn", "g_final")


def _natural(w3):
    n, r, c = w3.shape
    return w3.reshape(n * r, c)


def _shard_major(g, axis):
    if axis == 1:
        return g
    r, c = g.shape
    return g.reshape(N_CHIPS, r // N_CHIPS, c)


def kernel(x, mem, positions, g_mix, w_in, w_up_a, w_up_b, w_out, g_mem_q, g_mem_kv, w_q_mem, w_kv_mem, w_o_mem, g_ffn, w_ffn_gate, w_ffn_up, w_ffn_down, g_final, loss_target, m_g_mix, m_w_in, m_w_up_a, m_w_up_b, m_w_out, m_g_mem_q, m_g_mem_kv, m_w_q_mem, m_w_kv_mem, m_w_o_mem, m_g_ffn, m_w_ffn_gate, m_w_ffn_up, m_w_ffn_down, m_g_final, v_g_mix, v_w_in, v_w_up_a, v_w_up_b, v_w_out, v_g_mem_q, v_g_mem_kv, v_w_q_mem, v_w_kv_mem, v_w_o_mem, v_g_ffn, v_w_ffn_gate, v_w_ffn_up, v_w_ffn_down, v_g_final):
    given = dict(locals())
    shards = {n: _held(n, given[n][0]) for n in NAMES}

    early_shards = [shards[n].astype(BF16) for n in EARLY]
    late_shards = [shards[n].astype(BF16) for n in LATE]
    c_idx = lax.axis_index("c").astype(jnp.int32).reshape(1)
    s_idx = (2 * lax.axis_index("x") + lax.axis_index("y")).astype(jnp.int32).reshape(1)

    loss_row, grad_x, mine, gain_grads = _local_step(x, mem, positions, loss_target, g_mix, g_mem_q, g_mem_kv,
                                                     g_ffn, g_final, {}, early_shards, late_shards, (c_idx, s_idx))
    return _reduce_and_update(given, shards, loss_row, grad_x, mine, gain_grads, c_idx)


def _reduce_halves(glist, names, c_idx, s_idx, pair_exchange, chip_exchange):
    theirs = pair_exchange(glist)
    pairs = [_pair_sum(g, t, c_idx, name="pair_sum_" + n) for n, g, t in zip(names, glist, theirs)]
    recv = chip_exchange(pairs)
    return [_chip_sum(p, r, s_idx, name="chip_sum_" + n) for n, p, r in zip(names, pairs, recv)]


def _local_step(x, mem, positions, loss_target, g_mix, g_mem_q, g_mem_kv, g_ffn, g_final, wf,
                early_shards=None, late_shards=None, place=None):
    b_dim, s_dim, d = x.shape
    t_dim = b_dim * s_dim
    n_mem = mem.shape[1]
    wf = dict(wf)

    xb = x.reshape(t_dim, d)
    tgt = loss_target.reshape(t_dim, d)
    memf = mem.reshape(b_dim * n_mem, d)
    gfin = g_final.reshape(1, d)
    pos = positions.reshape(t_dim, 1).astype(F32)

    lane = jnp.arange(LANES) % HEAD_DIM
    half = ROPE_DIM // 2
    inv_freq = ROPE_THETA ** (-jnp.arange(half, dtype=F32) / half)
    inv_lane = jnp.where(lane < ROPE_DIM, inv_freq[lane % half], 0.0).reshape(1, -1).astype(F32)
    sel_a = (lane < half).astype(F32).reshape(1, -1)
    sel_b = ((lane >= half) & (lane < ROPE_DIM)).astype(F32).reshape(1, -1)

    def rows3(t):
        return t.reshape(b_dim, s_dim, t.shape[-1])

    def rows2(t):
        return t.reshape(t_dim, t.shape[-1])

    if early_shards:
        n1, gathered = _rms_fwd(xb, g_mix, name="rms_mix", rider=("gather", early_shards))
        wf.update(zip(EARLY, gathered))
    else:
        n1 = _rms_fwd(xb, g_mix, name="rms_mix")
    proj = _mm_cs(n1, wf["w_in"], name="mm_in")
    proj3 = rows3(proj)
    cs, sn = _rope_table(pos, inv_lane, sel_a, sel_b)
    cs3, sn3 = rows3(cs), rows3(sn)
    (oa16, oa32, lse_a), _ = _dil_fwd(proj3, cs3, sn3, sel_a, sel_b)
    ob16, gathered = _sb_fwd(proj3, ("gather", late_shards) if late_shards else None)
    wf.update(zip(LATE, gathered))
    w_out, w_q, w_kv = _natural(wf["w_out"]), _natural(wf["w_q_mem"]), _natural(wf["w_kv_mem"])
    oa, ob = rows2(oa16), rows2(ob16)
    ua = _mm_sm(oa, wf["w_up_a"], name="mm_up_a")
    ub = _mm_sm(ob, wf["w_up_b"], name="mm_up_b")
    mixed, h1, hn = _gate_out_norm(proj, ua, ub, w_out, xb, g_mem_q)

    memn = _rms_fwd(memf, g_mem_kv, name="rms_mem_kv")
    qm = _mm(hn, w_q, name="mm_q_mem", out_dtype=BF16)
    kvm = _mm(memn, w_kv, name="mm_kv_mem", out_dtype=BF16)
    qm3, kvm3 = rows3(qm), kvm.reshape(b_dim, n_mem, 2 * MEM_WIDTH)
    om = rows2(_mem_fwd(qm3, kvm3))
    h2 = _mm_sm(om, wf["w_o_mem"], name="mm_o_mem", add=h1)

    n3 = _rms_fwd(h2, g_ffn, name="rms_ffn")
    gate3, up3, act3 = _ffn_up_swiglu(n3, wf["w_ffn_gate"], wf["w_ffn_up"])
    loss_row, dh3, dg_final = _down_final(act3, wf["w_ffn_down"], h2, gfin, tgt)

    grads = {}
    grads["w_ffn_down"] = _mm_ffn_down_dw(act3, dh3, name="mm_down_dw")
    dgate3, dup3 = _ffn_down_dx_swiglu(dh3, wf["w_ffn_down"], gate3, up3)
    grads["w_ffn_gate"] = _mm_ffn_down_dw(dgate3, n3, name="mm_gate_dw")
    grads["w_ffn_up"] = _mm_ffn_down_dw(dup3, n3, name="mm_up_dw")
    dn3 = _ffn_up_dx(dgate3, dup3, wf["w_ffn_gate"], wf["w_ffn_up"])
    dh2, dg_ffn = _rms_bwd(h2, g_ffn, dn3, dh3, name="rms_ffn_bwd")

    dom = _mm_sm_dx(dh2, wf["w_o_mem"], name="mm_o_mem_dx", out_dtype=BF16)
    grads["w_o_mem"] = _mm_sm_dw(om, dh2, name="mm_o_mem_dw")
    dqm, dkm, dvm = _mem_bwd(qm3, kvm3, rows3(dom))
    dqm = rows2(dqm)
    dkvm = jnp.concatenate([dkm, dvm], axis=-1).reshape(b_dim * n_mem, 2 * MEM_WIDTH).astype(BF16)
    grads["w_q_mem"] = _shard_major(_mm(hn, dqm, name="mm_q_mem_dw", ta=True), 0)
    dhn = _mm(dqm, w_q, name="mm_q_mem_dx", tb=True)
    grads["w_kv_mem"] = _shard_major(_mm(memn, dkvm, name="mm_kv_mem_dw", ta=True), 0)
    dmemn = _mm(dkvm, w_kv, name="mm_kv_mem_dx", tb=True)
    _, dg_mem_kv = _rms_bwd(memf, g_mem_kv, dmemn, None, name="rms_mem_kv_bwd")
    dh1, dg_mem_q = _rms_bwd(h1, g_mem_q, dhn, dh2, name="rms_mem_q_bwd")

    grads["w_out"] = _shard_major(_mm(mixed, dh1, name="mm_out_dw", ta=True), 0)
    dua, dub, dgates = _out_dx_gate_bwd(dh1, w_out, proj, ua, ub)
    doa = _mm_sm_dx(dua, wf["w_up_a"], name="mm_up_a_dx")
    grads["w_up_a"] = _mm_sm_dw(oa, dua, name="mm_up_a_dw")
    dob = _mm_sm_dx(dub, wf["w_up_b"], name="mm_up_b_dx", out_dtype=BF16)
    grads["w_up_b"] = _mm_sm_dw(ob, dub, name="mm_up_b_dw")

    att = {}

    def dil_with_pairs(glist):
        att["a"], theirs = _dil_bwd(proj3, cs3, sn3, sel_a, sel_b, rows3(doa), oa32, lse_a,
                                    ("pair", glist) if glist else None)
        return theirs

    def sb_with_chips(pairs):
        att["b"], recv = _sb_bwd(proj3, rows3(dob), ("chip", pairs) if pairs else None)
        return recv

    if place is None:
        dil_with_pairs(())
        sb_with_chips(())
    else:
        mine_late = _reduce_halves([grads[n] for n in LATE], LATE, *place, dil_with_pairs, sb_with_chips)
    dproj = jnp.concatenate([rows2(t) for t in att["a"] + att["b"]] + [dgates], axis=1)
    grads["w_in"] = _mm_cs_dw(n1, dproj, name="mm_in_dw")
    if place is None:
        dn1 = _mm_cs_dx(dproj, wf["w_in"], name="mm_in_dx")
        dx, dg_mix = _rms_bwd(xb, g_mix, dn1, dh1, name="rms_mix_bwd")
    else:
        tail = {}

        def dx_with_pairs(glist):
            tail["dn1"], theirs = _mm_cs_dx(dproj, wf["w_in"], name="mm_in_dx", rider=("pair", glist))
            return theirs

        def rms_with_chips(pairs):
            tail["dx"], tail["dg"], recv = _rms_bwd(xb, g_mix, tail["dn1"], dh1, name="rms_mix_bwd",
                                                    rider=("chip", pairs))
            return recv

        mine_early = _reduce_halves([grads[n] for n in EARLY], EARLY, *place, dx_with_pairs, rms_with_chips)
        dx, dg_mix = tail["dx"], tail["dg"]
    grad_x = dx.reshape(b_dim, s_dim, d)
    gains = (dg_mix, dg_mem_q, dg_mem_kv, dg_ffn, dg_final)
    if place is None:
        return loss_row, grad_x, grads, gains
    return loss_row, grad_x, mine_early + mine_late, gains


def _reduce_and_update(given, shards, loss_row, grad_x, mine, gain_grads, c_idx):
    d = D_MODEL
    dg_mix, dg_mem_q, dg_mem_kv, dg_ffn, dg_final = gain_grads
    small = jnp.concatenate([dg_mix, dg_mem_q, dg_mem_kv, dg_ffn, dg_final,
                             jnp.pad(loss_row, ((0, 0), (0, FLAT_COLS - LANES))), jnp.zeros((2, FLAT_COLS), F32)], axis=0)
    small_all = _gather_small(small)
    others = _swap_halves(mine)
    small_sum = _sum8(small_all)
    loss = small_sum[5, 0]

    out_g, out_d, out_m, out_v = {}, {}, {}, {}
    for n, mine_n, other_n in zip(NAMES, mine, others):
        res = _adamw_halves(shards[n], mine_n, other_n, _held(n, given["m_" + n][0]), _held(n, given["v_" + n][0]),
                            c_idx, name="adamw_" + n)
        out_g[n], out_d[n], out_m[n], out_v[n] = [_held(n, r)[None] for r in res]
    gain_w = jnp.concatenate([given[n].reshape(1, d) for n in GAINS], axis=0)
    gain_m = jnp.concatenate([given["m_" + n].reshape(1, d) for n in GAINS], axis=0)
    gain_v = jnp.concatenate([given["v_" + n].reshape(1, d) for n in GAINS], axis=0)
    gain_g = small_sum[:len(GAINS)]
    gd, gm, gv = _adamw(gain_w, gain_g, gain_m, gain_v, name="adamw_gains")
    for i, n in enumerate(GAINS):
        shape = given[n].shape
        out_g[n], out_d[n] = gain_g[i].reshape(shape), gd[i].reshape(shape)
        out_m[n], out_v[n] = gm[i].reshape(shape), gv[i].reshape(shape)

    order = ["g_mix", "w_in", "w_up_a", "w_up_b", "w_out", "g_mem_q", "g_mem_kv", "w_q_mem", "w_kv_mem", "w_o_mem",
             "g_ffn", "w_ffn_gate", "w_ffn_up", "w_ffn_down", "g_final"]
    return (loss, grad_x, *[out_g[n] for n in order], *[out_d[n] for n in order],
            *[out_m[n] for n in order], *[out_v[n] for n in order])
```

```python
import jax
import jax.numpy as jnp
from jax import lax
from jax.experimental import pallas as pl
from jax.experimental.pallas import tpu as pltpu

F32 = jnp.float32
BF16 = jnp.bfloat16
MESH = pl.DeviceIdType.MESH

D_MODEL = 1024
HEAD_DIM = 64
N_HEADS = 8
ATT_WIDTH = N_HEADS * HEAD_DIM
DIL_PATTERNS = ((128, 1), (512, 4), (2048, 16))
BLOCK = 128
SB_ROWS = 1024
SB_STEP = 4
ROPE_THETA = 500000.0
ROPE_DIM = HEAD_DIM // 4
N_HEADS_MEM = 4
MEM_HEAD_DIM = 128
MEM_WIDTH = N_HEADS_MEM * MEM_HEAD_DIM
D_FF = 2816
IN_COLS = 6 * ATT_WIDTH + 2 * D_MODEL
RMS_EPS = 1e-6
ADAM_LR = 0.001
ADAM_B1 = 0.9
ADAM_B2 = 0.999
ADAM_EPS = 1e-08
ADAM_WD = 0.01
ADAM_STEP = 10

N_CHIPS = 4
LANES = 128
FLAT_COLS = 1024
VMEM_LIMIT = 56 * 1024 * 1024

PAIRS = ATT_WIDTH // LANES
COL_QA, COL_KA, COL_VA, COL_QB, COL_KB, COL_VB = (i * PAIRS for i in range(6))

MM_CAP = 1408
TOK_CAP = 2048
NN = (((1,), (0,)), ((), ()))
NT = (((1,), (1,)), ((), ()))
TN = (((0,), (0,)), ((), ()))
BNN = (((2,), (1,)), ((0,), (0,)))
BNT = (((2,), (2,)), ((0,), (0,)))
BTN = (((1,), (1,)), ((0,), (0,)))
DIL_BATCH = 16
DIL_CHUNK = 512


def _tile(dim, cap, unit=LANES):
    if dim <= cap:
        return dim
    best = None
    for t in range(unit, cap + 1, unit):
        if dim % t == 0:
            best = t
    assert best is not None, (dim, cap)
    return best


def _row_cap(cols):
    return max(256, (1 << 18) // cols)


def _params(sem):
    return pltpu.CompilerParams(dimension_semantics=sem, vmem_limit_bytes=VMEM_LIMIT)


def _mm(a, b, *, name, ta=False, tb=False, add=None, out_dtype=F32,
        tm_cap=MM_CAP, tn_cap=MM_CAP, tk_cap=MM_CAP):
    if ta:
        k_dim, m_dim = a.shape
    else:
        m_dim, k_dim = a.shape
    if tb:
        n_dim, kb = b.shape
    else:
        kb, n_dim = b.shape
    assert kb == k_dim, (a.shape, b.shape, ta, tb)
    tm, tn, tk = _tile(m_dim, tm_cap), _tile(n_dim, tn_cap), _tile(k_dim, tk_cap)
    nk = k_dim // tk
    dims = (((0 if ta else 1,), (1 if tb else 0,)), ((), ()))
    has_add = add is not None

    def body(*refs):
        if has_add:
            a_ref, b_ref, add_ref, o_ref = refs[:4]
        else:
            a_ref, b_ref, o_ref = refs[:3]
        part = lax.dot_general(a_ref[...].astype(BF16), b_ref[...].astype(BF16), dims, preferred_element_type=F32)

        def finish(r):
            if has_add:
                r = add_ref[...] + r
            o_ref[...] = r.astype(out_dtype)

        if nk == 1:
            finish(part)
            return
        acc_ref = refs[-1]
        k = pl.program_id(2)

        @pl.when(k == 0)
        def _():
            acc_ref[...] = part

        @pl.when(k > 0)
        def _():
            acc_ref[...] += part

        @pl.when(k == nk - 1)
        def _():
            finish(acc_ref[...])

    a_spec = pl.BlockSpec((tk, tm), lambda i, j, k: (k, i)) if ta else pl.BlockSpec((tm, tk), lambda i, j, k: (i, k))
    b_spec = pl.BlockSpec((tn, tk), lambda i, j, k: (j, k)) if tb else pl.BlockSpec((tk, tn), lambda i, j, k: (k, j))
    o_spec = pl.BlockSpec((tm, tn), lambda i, j, k: (i, j))
    in_specs = [a_spec, b_spec] + ([o_spec] if has_add else [])
    args = (a, b) + ((add,) if has_add else ())
    return pl.pallas_call(
        body, name=name, grid=(m_dim // tm, n_dim // tn, nk),
        in_specs=in_specs, out_specs=o_spec,
        out_shape=jax.ShapeDtypeStruct((m_dim, n_dim), out_dtype),
        scratch_shapes=[pltpu.VMEM((tm, tn), F32)] if nk > 1 else [],
        compiler_params=_params(("parallel", "parallel", "arbitrary")),
    )(*args)


def _mm_core(name, a, b, a_spec, b_spec, o_spec, out_shape, grid, dims, *, add=None, out_dtype=F32, rider=None):
    nk = grid[2]
    has_add = add is not None
    n_in = 3 if has_add else 2
    acc_shape = tuple(d for d in o_spec.block_shape if d is not None)
    extra, extra_shapes, extra_sems = _rider_parts(rider)
    n_w = len(extra)

    def body(*refs):
        a_ref, b_ref = refs[:2]
        o_ref = refs[n_in + n_w]
        step = (pl.program_id(0) * grid[1] + pl.program_id(1)) * nk + pl.program_id(2)
        begin, end = _rider_hooks(rider, refs[n_in:n_in + n_w], refs[n_in + n_w + 1:n_in + 2 * n_w + 1], refs[-2:],
                                  step, grid[0] * grid[1] * nk)
        begin()
        part = lax.dot_general(a_ref[...].astype(BF16), b_ref[...].astype(BF16), dims, preferred_element_type=F32)

        def finish(r):
            if has_add:
                r = refs[2][...] + r
            o_ref[...] = r.astype(out_dtype)

        if nk == 1:
            finish(part)
        else:
            acc_ref = refs[n_in + 2 * n_w + 1]
            k = pl.program_id(2)

            @pl.when(k == 0)
            def _():
                acc_ref[...] = part

            @pl.when(k > 0)
            def _():
                acc_ref[...] += part

            @pl.when(k == nk - 1)
            def _():
                finish(acc_ref[...])
        end()

    in_specs = [a_spec, b_spec] + ([o_spec] if has_add else []) + [ANY] * n_w
    args = (a, b) + ((add,) if has_add else ()) + extra
    res = pl.pallas_call(
        body, name=name, grid=grid, in_specs=in_specs, out_specs=[o_spec] + [ANY] * n_w,
        out_shape=[jax.ShapeDtypeStruct(out_shape, out_dtype)] + extra_shapes,
        scratch_shapes=([pltpu.VMEM(acc_shape, F32)] if nk > 1 else []) + extra_sems,
        compiler_params=_params(("arbitrary",) * 3 if n_w else ("parallel", "parallel", "arbitrary")),
    )(*args)
    return (res[0], res[1:]) if n_w else res[0]


def _mm_cs(a, w3, *, name, add=None, out_dtype=F32):
    m_dim, k_dim = a.shape
    _, _, n4 = w3.shape
    tm, tn, tk = _tile(m_dim, MM_CAP if add is not None else TOK_CAP), _tile(n4, MM_CAP), _tile(k_dim, MM_CAP)
    npb = n4 // tn
    return _mm_core(name, a, w3,
                    pl.BlockSpec((tm, tk), lambda i, j, k: (i, k)),
                    pl.BlockSpec((None, tk, tn), lambda i, j, k: (j // npb, k, j % npb)),
                    pl.BlockSpec((tm, tn), lambda i, j, k: (i, j)),
                    (m_dim, N_CHIPS * n4), (m_dim // tm, N_CHIPS * npb, k_dim // tk), NN, add=add, out_dtype=out_dtype)


def _mm_cs_dx(dy, w3, *, name, out_dtype=F32, rider=None):
    m_dim, _ = dy.shape
    _, k_dim, n4 = w3.shape
    tm, tkw, tn = _tile(m_dim, MM_CAP), _tile(k_dim, MM_CAP), _tile(n4, MM_CAP)
    npb = n4 // tn
    return _mm_core(name, dy, w3,
                    pl.BlockSpec((tm, tn), lambda i, j, k: (i, k)),
                    pl.BlockSpec((None, tkw, tn), lambda i, j, k: (k // npb, j, k % npb)),
                    pl.BlockSpec((tm, tkw), lambda i, j, k: (i, j)),
                    (m_dim, k_dim), (m_dim // tm, k_dim // tkw, N_CHIPS * npb), NT, out_dtype=out_dtype, rider=rider)


def _mm_cs_dw(a, dy, *, name):
    m_dim, k_dim = a.shape
    n4 = dy.shape[1] // N_CHIPS
    tmk, tn, tk = _tile(k_dim, MM_CAP), _tile(n4, MM_CAP), _tile(m_dim, TOK_CAP)
    npb = n4 // tn
    return _mm_core(name, a, dy,
                    pl.BlockSpec((tk, tmk), lambda i, j, k: (k, i)),
                    pl.BlockSpec((tk, tn), lambda i, j, k: (k, j)),
                    pl.BlockSpec((None, tmk, tn), lambda i, j, k: (j // npb, i, j % npb)),
                    (N_CHIPS, k_dim, n4), (k_dim // tmk, N_CHIPS * npb, m_dim // tk), TN)


def _mm_sm(a, w3, *, name, add=None, out_dtype=F32, tt=1024):
    t_dim, k_dim = a.shape
    n_s, _, n4 = w3.shape
    has_add = add is not None

    def body(*refs):
        a_ref, w_ref, o_ref = refs[0], refs[1], refs[-1]
        av = a_ref[...].astype(BF16)
        for s in range(n_s):
            cols = slice(s * n4, (s + 1) * n4)
            r = jnp.dot(av, w_ref[s], preferred_element_type=F32)
            if has_add:
                r = refs[2][:, cols] + r
            o_ref[:, cols] = r.astype(out_dtype)

    row = pl.BlockSpec((tt, n_s * n4), lambda i: (i, 0))
    return pl.pallas_call(
        body, name=name, grid=(t_dim // tt,),
        in_specs=[pl.BlockSpec((tt, k_dim), lambda i: (i, 0)), pl.BlockSpec(w3.shape, lambda i: (0, 0, 0))]
        + ([row] if has_add else []),
        out_specs=row, out_shape=jax.ShapeDtypeStruct((t_dim, n_s * n4), out_dtype),
        compiler_params=_params(("parallel",)),
    )(*((a, w3) + ((add,) if has_add else ())))


def _mm_sm_dx(dy, w3, *, name, out_dtype=F32, tt=1024):
    t_dim, _ = dy.shape
    n_s, k_dim, n4 = w3.shape

    def body(dy_ref, w_ref, o_ref):
        dyv = dy_ref[...].astype(BF16)
        acc = lax.dot_general(dyv[:, :n4], w_ref[0], NT, preferred_element_type=F32)
        for s in range(1, n_s):
            acc = acc + lax.dot_general(dyv[:, s * n4:(s + 1) * n4], w_ref[s], NT, preferred_element_type=F32)
        o_ref[...] = acc.astype(out_dtype)

    return pl.pallas_call(
        body, name=name, grid=(t_dim // tt,),
        in_specs=[pl.BlockSpec((tt, n_s * n4), lambda i: (i, 0)), pl.BlockSpec(w3.shape, lambda i: (0, 0, 0))],
        out_specs=pl.BlockSpec((tt, k_dim), lambda i: (i, 0)),
        out_shape=jax.ShapeDtypeStruct((t_dim, k_dim), out_dtype),
        compiler_params=_params(("parallel",)),
    )(dy, w3)


def _mm_sm_dw(a, dy, *, name, tk=1024):
    t_dim, k_dim = a.shape
    n4 = dy.shape[1] // N_CHIPS

    def body(a_ref, dy_ref, o_ref):
        av = a_ref[...].astype(BF16)
        dyv = dy_ref[...].astype(BF16)

        @pl.when(pl.program_id(0) == 0)
        def _():
            o_ref[...] = jnp.zeros_like(o_ref)

        for s in range(N_CHIPS):
            o_ref[s] += lax.dot_general(av, dyv[:, s * n4:(s + 1) * n4], TN, preferred_element_type=F32)

    return pl.pallas_call(
        body, name=name, grid=(t_dim // tk,),
        in_specs=[pl.BlockSpec((tk, k_dim), lambda i: (i, 0)), pl.BlockSpec((tk, N_CHIPS * n4), lambda i: (i, 0))],
        out_specs=pl.BlockSpec((N_CHIPS, k_dim, n4), lambda i: (0, 0, 0)),
        out_shape=jax.ShapeDtypeStruct((N_CHIPS, k_dim, n4), F32),
        compiler_params=_params(("arbitrary",)),
    )(a, dy)


def _ffn_up_dx(dg3, du3, wg3, wu3, *, tt=512):
    n_s, t_dim, f4 = dg3.shape
    d = wg3.shape[2]

    def body(dg_ref, du_ref, wg_ref, wu_ref, o_ref):
        acc = None
        for s in range(n_s):
            part = (jnp.dot(dg_ref[s], wg_ref[s], preferred_element_type=F32)
                    + jnp.dot(du_ref[s], wu_ref[s], preferred_element_type=F32))
            acc = part if acc is None else acc + part
        o_ref[...] = acc

    a_spec = pl.BlockSpec((n_s, tt, f4), lambda i: (0, i, 0))
    w_spec = pl.BlockSpec(wg3.shape, lambda i: (0, 0, 0))
    return pl.pallas_call(
        body, name="ffn_up_dx", grid=(t_dim // tt,),
        in_specs=[a_spec, a_spec, w_spec, w_spec], out_specs=pl.BlockSpec((tt, d), lambda i: (i, 0)),
        out_shape=jax.ShapeDtypeStruct((t_dim, d), F32),
        compiler_params=_params(("parallel",)),
    )(dg3, du3, wg3, wu3)


def _mm_ffn_down_dw(act3, dh, *, name, tk=1024):
    n_s, t_dim, f4 = act3.shape
    d = dh.shape[1]

    def body(a_ref, b_ref, o_ref):
        bv = b_ref[...].astype(BF16)

        @pl.when(pl.program_id(0) == 0)
        def _():
            o_ref[...] = jnp.zeros_like(o_ref)

        for s in range(n_s):
            o_ref[s] += lax.dot_general(a_ref[s], bv, TN, preferred_element_type=F32)

    return pl.pallas_call(
        body, name=name, grid=(t_dim // tk,),
        in_specs=[pl.BlockSpec((n_s, tk, f4), lambda i: (0, i, 0)), pl.BlockSpec((tk, d), lambda i: (i, 0))],
        out_specs=pl.BlockSpec((n_s, f4, d), lambda i: (0, 0, 0)),
        out_shape=jax.ShapeDtypeStruct((n_s, f4, d), F32),
        compiler_params=_params(("arbitrary",)),
    )(act3, dh)


def _rms_fwd(x, g, *, name, tt=512, rider=None):
    t_dim, d = x.shape
    tt = _tile(t_dim, tt, 8)
    extra, extra_shapes, extra_sems = _rider_parts(rider)
    n_w = len(extra)

    def body(*refs):
        x_ref, g_ref, o_ref = refs[0], refs[1], refs[2 + n_w]
        begin, end = _rider_hooks(rider, refs[2:2 + n_w], refs[3 + n_w:3 + 2 * n_w], refs[-2:], pl.program_id(0),
                                  t_dim // tt)
        begin()
        xv = x_ref[...]
        r = lax.rsqrt(jnp.mean(xv * xv, axis=-1, keepdims=True) + RMS_EPS)
        o_ref[...] = ((xv * r) * g_ref[...]).astype(o_ref.dtype)
        end()

    res = pl.pallas_call(
        body, name=name, grid=(t_dim // tt,),
        in_specs=[pl.BlockSpec((tt, d), lambda i: (i, 0)), pl.BlockSpec((1, d), lambda i: (0, 0))] + [ANY] * n_w,
        out_specs=[pl.BlockSpec((tt, d), lambda i: (i, 0))] + [ANY] * n_w,
        out_shape=[jax.ShapeDtypeStruct((t_dim, d), BF16)] + extra_shapes,
        scratch_shapes=extra_sems,
        compiler_params=_params(("arbitrary",) if n_w else ("parallel",)),
    )(x, g, *extra)
    return (res[0], res[1:]) if n_w else res[0]


def _rms_bwd(x, g, dy, add, *, name, tt=512, rider=None):
    t_dim, d = x.shape
    tt = _tile(t_dim, tt, 8)
    has_add = add is not None
    n_in = 4 if has_add else 3
    extra, extra_shapes, extra_sems = _rider_parts(rider)
    n_w = len(extra)

    def body(*refs):
        x_ref, g_ref, dy_ref = refs[:3]
        add_ref = refs[3] if has_add else None
        dx_ref, dg_ref = refs[n_in + n_w:n_in + n_w + 2]
        begin, end = _rider_hooks(rider, refs[n_in:n_in + n_w], refs[n_in + n_w + 2:n_in + 2 * n_w + 2], refs[-2:],
                                  pl.program_id(0), t_dim // tt)
        begin()
        xv = x_ref[...]
        dyv = dy_ref[...].astype(F32)
        r = lax.rsqrt(jnp.mean(xv * xv, axis=-1, keepdims=True) + RMS_EPS)
        xh = xv * r
        u = dyv * g_ref[...]
        dx = r * (u - xh * jnp.mean(u * xh, axis=-1, keepdims=True))
        if has_add:
            dx = add_ref[...] + dx
        dx_ref[...] = dx

        @pl.when(pl.program_id(0) == 0)
        def _():
            dg_ref[...] = jnp.zeros_like(dg_ref)

        dg_ref[...] += jnp.sum(dyv * xh, axis=0, keepdims=True)
        end()

    row = pl.BlockSpec((tt, d), lambda i: (i, 0))
    vec = pl.BlockSpec((1, d), lambda i: (0, 0))
    in_specs = [row, vec, row] + ([row] if has_add else []) + [ANY] * n_w
    args = (x, g, dy) + ((add,) if has_add else ()) + extra
    res = pl.pallas_call(
        body, name=name, grid=(t_dim // tt,),
        in_specs=in_specs, out_specs=[row, vec] + [ANY] * n_w,
        out_shape=[jax.ShapeDtypeStruct((t_dim, d), F32), jax.ShapeDtypeStruct((1, d), F32)] + extra_shapes,
        scratch_shapes=extra_sems,
        compiler_params=_params(("arbitrary",)),
    )(*args)
    return (res[0], res[1], res[2:]) if n_w else (res[0], res[1])


def _down_final(act3, wd3, h, g, target, *, tt=1024):
    n_s, t_dim, f4 = act3.shape
    d = h.shape[1]
    n_steps = t_dim // tt

    def body(a_ref, w_ref, h_ref, g_ref, t_ref, loss_ref, dh_ref, dg_ref, acc_ref, sq_ref):
        i, k = pl.program_id(0), pl.program_id(1)
        part = jnp.dot(a_ref[...], w_ref[...], preferred_element_type=F32)

        @pl.when(k == 0)
        def _():
            acc_ref[...] = part

        @pl.when(k > 0)
        def _():
            acc_ref[...] += part

        @pl.when(jnp.logical_and(i == 0, k == 0))
        def _():
            dg_ref[...] = jnp.zeros_like(dg_ref)
            sq_ref[...] = jnp.zeros_like(sq_ref)

        @pl.when(k == n_s - 1)
        def _():
            xv = h_ref[...] + acc_ref[...]
            gv = g_ref[...]
            r = lax.rsqrt(jnp.mean(xv * xv, axis=-1, keepdims=True) + RMS_EPS)
            xh = xv * r
            err = xh * gv - t_ref[...]
            dyv = err * (1.0 / d)
            u = dyv * gv
            dh_ref[...] = r * (u - xh * jnp.mean(u * xh, axis=-1, keepdims=True))
            dg_ref[...] += jnp.sum(dyv * xh, axis=0, keepdims=True)
            sq_ref[...] += jnp.sum(err * err, axis=0, keepdims=True)

        @pl.when(jnp.logical_and(i == n_steps - 1, k == n_s - 1))
        def _():
            total = jnp.sum(sq_ref[...], axis=-1, keepdims=True) * (0.5 / d)
            loss_ref[...] = jnp.broadcast_to(total, loss_ref.shape)

    row = pl.BlockSpec((tt, d), lambda i, k: (i, 0))
    vec = pl.BlockSpec((1, d), lambda i, k: (0, 0))
    return pl.pallas_call(
        body, name="down_final_loss", grid=(n_steps, n_s),
        in_specs=[pl.BlockSpec((None, tt, f4), lambda i, k: (k, i, 0)),
                  pl.BlockSpec((None, f4, d), lambda i, k: (k, 0, 0)), row, vec, row],
        out_specs=[pl.BlockSpec((1, LANES), lambda i, k: (0, 0)), row, vec],
        out_shape=[jax.ShapeDtypeStruct((1, LANES), F32), jax.ShapeDtypeStruct((t_dim, d), F32),
                   jax.ShapeDtypeStruct((1, d), F32)],
        scratch_shapes=[pltpu.VMEM((tt, d), F32), pltpu.VMEM((1, d), F32)],
        compiler_params=_params(("arbitrary", "arbitrary")),
    )(act3, wd3, h, g, target)


def _rope_table(pos, inv_lane, sel_a, sel_b, *, tt=512):
    t_dim = pos.shape[0]

    def body(p_ref, f_ref, a_ref, b_ref, c_ref, s_ref):
        ang = p_ref[...] * f_ref[...]
        on = (a_ref[...] + b_ref[...]) > 0.0
        c_ref[...] = jnp.where(on, jnp.cos(ang), 1.0)
        s_ref[...] = jnp.where(on, jnp.sin(ang), 0.0)

    vec = pl.BlockSpec((1, LANES), lambda i: (0, 0))
    row = pl.BlockSpec((tt, LANES), lambda i: (i, 0))
    shp = jax.ShapeDtypeStruct((t_dim, LANES), F32)
    return pl.pallas_call(
        body, name="rope_table", grid=(t_dim // tt,),
        in_specs=[pl.BlockSpec((tt, 1), lambda i: (i, 0)), vec, vec, vec],
        out_specs=[row, row], out_shape=[shp, shp],
        compiler_params=_params(("parallel",)),
    )(pos, inv_lane, sel_a, sel_b)


def _rotate(xv, cs, sn, sa, sb):
    half = ROPE_DIM // 2
    up = pltpu.roll(xv, LANES - half, 1)
    dn = pltpu.roll(xv, half, 1)
    return xv * cs + (dn * sb - up * sa) * sn


def _head_masks():
    h1 = lax.broadcasted_iota(jnp.int32, (1, LANES), 1) < HEAD_DIM
    return h1, jnp.logical_not(h1)


def _split_heads(xv, h1, h2):
    return jnp.where(h1, xv, 0.0).astype(BF16), jnp.where(h2, xv, 0.0).astype(BF16)


def _tri_masks():
    r = lax.broadcasted_iota(jnp.int32, (BLOCK, BLOCK), 0)
    c = lax.broadcasted_iota(jnp.int32, (BLOCK, BLOCK), 1)
    return c <= r, r <= c


def _stream_rows(start, dil):
    if dil == 1:
        return pl.ds(pl.multiple_of(start, BLOCK), BLOCK)
    return pl.ds(start, BLOCK, stride=dil)


def _dil_tile(idx, dil, nb):
    r = idx // nb
    n = idx % nb
    return (_stream_rows(r + dil * BLOCK * n, dil), _stream_rows(r + dil * BLOCK * jnp.maximum(n - 1, 0), dil),
            n > 0)


def _dil_specs(b_dim, s_dim):
    def col(c0):
        return pl.BlockSpec((None, s_dim, LANES),lambda b, h: (b, 0, c0 + h))
    tab = pl.BlockSpec((None, s_dim, LANES),lambda b, h: (b, 0, 0))
    vec = pl.BlockSpec((1, LANES), lambda b, h: (0, 0))
    return col, tab, vec


def _dil_fwd(proj3, cs3, sn3, sel_a, sel_b, rider=None):
    b_dim, s_dim, _ = proj3.shape
    scale = HEAD_DIM ** -0.5
    n_pat = len(DIL_PATTERNS)
    extra, extra_shapes, extra_sems = _rider_parts(rider)
    n_w = len(extra)
    n_steps = b_dim * PAIRS

    def body(*refs):
        q_ref, k_ref, v_ref, cs_ref, sn_ref, sa_ref, sb_ref = refs[:7]
        o16_ref, o32_ref, l_ref = refs[7 + n_w:10 + n_w]
        qr, kr = refs[10 + 2 * n_w:12 + 2 * n_w]
        per_pattern = refs[12 + 2 * n_w:12 + 2 * n_w + 2 * n_pat]
        og, lg = per_pattern[:n_pat], per_pattern[n_pat:]
        step = pl.program_id(0) * PAIRS + pl.program_id(1)
        begin, end = _rider_hooks(rider, refs[7:7 + n_w], refs[10 + n_w:10 + 2 * n_w], refs[-2:], step, n_steps)
        begin()
        h1, h2 = _head_masks()
        cur_ok, prev_ok = _tri_masks()
        sa, sb = sa_ref[...], sb_ref[...]

        def prep(j, _):
            rows = pl.ds(pl.multiple_of(j * DIL_CHUNK, DIL_CHUNK), DIL_CHUNK)
            cs, sn = cs_ref[rows, :], sn_ref[rows, :]
            qr[rows, :] = _rotate(q_ref[rows, :], cs, sn, sa, sb) * scale
            kr[rows, :] = _rotate(k_ref[rows, :], cs, sn, sa, sb)
            return 0

        lax.fori_loop(0, s_dim // DIL_CHUNK, prep, 0)

        for g, (_, dil) in enumerate(DIL_PATTERNS):
            nb = s_dim // dil // BLOCK

            def some(bi, _, g=g, dil=dil, nb=nb):
                tiles = [_dil_tile(bi * DIL_BATCH + t, dil, nb) for t in range(DIL_BATCH)]
                rows = [t[0] for t in tiles]
                q1, q2 = _split_heads(jnp.stack([qr[rw, :] for rw in rows]), h1, h2)
                kc = jnp.stack([kr[rw, :] for rw in rows]).astype(BF16)
                vc1, vc2 = _split_heads(jnp.stack([v_ref[rw, :] for rw in rows]), h1, h2)
                if nb > 1:
                    kp = jnp.stack([kr[t[1], :] for t in tiles]).astype(BF16)
                    vp1, vp2 = _split_heads(jnp.stack([v_ref[t[1], :] for t in tiles]), h1, h2)
                    p_ok = jnp.stack([jnp.logical_and(prev_ok, t[2]) for t in tiles])

                def head(qh, vch, vph):
                    sc = jnp.where(cur_ok, lax.dot_general(qh, kc, BNT, preferred_element_type=F32), -jnp.inf)
                    m = jnp.max(sc, axis=-1, keepdims=True)
                    if nb > 1:
                        sp = jnp.where(p_ok, lax.dot_general(qh, kp, BNT, preferred_element_type=F32), -jnp.inf)
                        m = jnp.maximum(m, jnp.max(sp, axis=-1, keepdims=True))
                    pc = jnp.exp(sc - m)
                    den = jnp.sum(pc, axis=-1, keepdims=True)
                    acc = lax.dot_general(pc.astype(BF16), vch, BNN, preferred_element_type=F32)
                    if nb > 1:
                        pp = jnp.exp(sp - m)
                        den = den + jnp.sum(pp, axis=-1, keepdims=True)
                        acc = acc + lax.dot_general(pp.astype(BF16), vph, BNN, preferred_element_type=F32)
                    return acc / den, m + jnp.log(den)

                o1, l1 = head(q1, vc1, vp1 if nb > 1 else None)
                o2, l2 = head(q2, vc2, vp2 if nb > 1 else None)
                o, l = o1 + o2, jnp.where(h1, l1, l2)
                for t, rw in enumerate(rows):
                    og[g][rw, :] = o[t]
                    lg[g][rw, :] = l[t]
                return 0

            lax.fori_loop(0, dil * nb // DIL_BATCH, some, 0)

        def comb(j, _):
            rows = pl.ds(pl.multiple_of(j * DIL_CHUNK, DIL_CHUNK), DIL_CHUNK)
            ls = [lg[g][rows, :] for g in range(n_pat)]
            m = jnp.maximum(jnp.maximum(ls[0], ls[1]), ls[2])
            es = [jnp.exp(l - m) for l in ls]
            den = es[0] + es[1] + es[2]
            o = (es[0] * og[0][rows, :] + es[1] * og[1][rows, :] + es[2] * og[2][rows, :]) / den
            o16_ref[rows, :] = o.astype(BF16)
            o32_ref[rows, :] = o
            l_ref[rows, :] = m + jnp.log(den)
            return 0

        lax.fori_loop(0, s_dim // DIL_CHUNK, comb, 0)
        end()

    col, tab, vec = _dil_specs(b_dim, s_dim)
    out = pl.BlockSpec((None, s_dim, LANES),lambda b, h: (b, 0, h))
    shp = (b_dim, s_dim, ATT_WIDTH)
    res = pl.pallas_call(
        body, name="dil_fwd", grid=(b_dim, PAIRS),
        in_specs=[col(COL_QA), col(COL_KA), col(COL_VA), tab, tab, vec, vec] + [ANY] * n_w,
        out_specs=[out, out, out] + [ANY] * n_w,
        out_shape=[jax.ShapeDtypeStruct(shp, BF16), jax.ShapeDtypeStruct(shp, F32), jax.ShapeDtypeStruct(shp, F32)]
        + extra_shapes,
        scratch_shapes=[pltpu.VMEM((s_dim, LANES), F32)] * (2 + 2 * n_pat) + extra_sems,
        compiler_params=_params(("arbitrary", "arbitrary")),
    )(proj3, proj3, proj3, cs3, sn3, sel_a, sel_b, *extra)
    return res[:3], res[3:]


def _dil_bwd(proj3, cs3, sn3, sel_a, sel_b, do3, o3, lse3, rider=None):
    b_dim, s_dim, _ = proj3.shape
    scale = HEAD_DIM ** -0.5
    extra, extra_shapes, extra_sems = _rider_parts(rider)
    n_w = len(extra)

    def body(*refs):
        q_ref, k_ref, v_ref, cs_ref, sn_ref, sa_ref, sb_ref, do_ref, o_ref, l_ref = refs[:10]
        dq_ref, dk_ref, dv_ref = refs[10 + n_w:13 + n_w]
        qr, kr, dqa, dka, dva = refs[13 + 2 * n_w:18 + 2 * n_w]
        step = pl.program_id(0) * PAIRS + pl.program_id(1)
        begin, end = _rider_hooks(rider, refs[10:10 + n_w], refs[13 + n_w:13 + 2 * n_w], refs[-2:], step,
                                  b_dim * PAIRS)
        begin()
        h1, h2 = _head_masks()
        cur_ok, prev_ok = _tri_masks()
        sa, sb = sa_ref[...], sb_ref[...]

        def prep(j, _):
            rows = pl.ds(pl.multiple_of(j * DIL_CHUNK, DIL_CHUNK), DIL_CHUNK)
            cs, sn = cs_ref[rows, :], sn_ref[rows, :]
            qr[rows, :] = _rotate(q_ref[rows, :], cs, sn, sa, sb) * scale
            kr[rows, :] = _rotate(k_ref[rows, :], cs, sn, sa, sb)
            zero = jnp.zeros((DIL_CHUNK, LANES), F32)
            dqa[rows, :] = zero
            dka[rows, :] = zero
            dva[rows, :] = zero
            return 0

        lax.fori_loop(0, s_dim // DIL_CHUNK, prep, 0)

        for _, dil in DIL_PATTERNS:
            nb = s_dim // dil // BLOCK

            def some(bi, _, dil=dil, nb=nb):
                tiles = [_dil_tile(bi * DIL_BATCH + t, dil, nb) for t in range(DIL_BATCH)]
                rows = [t[0] for t in tiles]
                q1, q2 = _split_heads(jnp.stack([qr[rw, :] for rw in rows]), h1, h2)
                dof = jnp.stack([do_ref[rw, :] for rw in rows])
                do1, do2 = _split_heads(dof, h1, h2)
                prod = dof * jnp.stack([o_ref[rw, :] for rw in rows])
                delta1 = jnp.sum(jnp.where(h1, prod, 0.0), axis=-1, keepdims=True)
                delta2 = jnp.sum(jnp.where(h2, prod, 0.0), axis=-1, keepdims=True)
                lt = jnp.stack([l_ref[rw, :] for rw in rows])
                lse1 = jnp.max(jnp.where(h1, lt, -jnp.inf), axis=-1, keepdims=True)
                lse2 = jnp.max(jnp.where(h2, lt, -jnp.inf), axis=-1, keepdims=True)

                def side(krows, ok):
                    kf = jnp.stack([kr[kw, :] for kw in krows])
                    k16 = kf.astype(BF16)
                    k1, k2 = _split_heads(kf, h1, h2)
                    v16 = jnp.stack([v_ref[kw, :] for kw in krows]).astype(BF16)

                    def head(qh, doh, lse, delta):
                        sc = lax.dot_general(qh, k16, BNT, preferred_element_type=F32)
                        p = jnp.where(ok, jnp.exp(sc - lse), 0.0)
                        dp = lax.dot_general(doh, v16, BNT, preferred_element_type=F32)
                        return p.astype(BF16), (p * (dp - delta)).astype(BF16)

                    p1, ds1 = head(q1, do1, lse1, delta1)
                    p2, ds2 = head(q2, do2, lse2, delta2)
                    dv = (lax.dot_general(p1, do1, BTN, preferred_element_type=F32)
                          + lax.dot_general(p2, do2, BTN, preferred_element_type=F32))
                    dk = (lax.dot_general(ds1, q1, BTN, preferred_element_type=F32)
                          + lax.dot_general(ds2, q2, BTN, preferred_element_type=F32))
                    for t, kw in enumerate(krows):
                        dva[kw, :] += dv[t]
                        dka[kw, :] += dk[t]
                    return (lax.dot_general(ds1, k1, BNN, preferred_element_type=F32)
                            + lax.dot_general(ds2, k2, BNN, preferred_element_type=F32))

                dq = side(rows, cur_ok)
                if nb > 1:
                    dq = dq + side([t[1] for t in tiles], jnp.stack([jnp.logical_and(prev_ok, t[2]) for t in tiles]))
                for t, rw in enumerate(rows):
                    dqa[rw, :] += dq[t] * scale
                return 0

            lax.fori_loop(0, dil * nb // DIL_BATCH, some, 0)

        def finish(j, _):
            rows = pl.ds(pl.multiple_of(j * DIL_CHUNK, DIL_CHUNK), DIL_CHUNK)
            cs, sn = cs_ref[rows, :], -sn_ref[rows, :]
            dq_ref[rows, :] = _rotate(dqa[rows, :], cs, sn, sa, sb).astype(BF16)
            dk_ref[rows, :] = _rotate(dka[rows, :], cs, sn, sa, sb).astype(BF16)
            dv_ref[rows, :] = dva[rows, :].astype(BF16)
            return 0

        lax.fori_loop(0, s_dim // DIL_CHUNK, finish, 0)
        end()

    col, tab, vec = _dil_specs(b_dim, s_dim)
    out = pl.BlockSpec((None, s_dim, LANES),lambda b, h: (b, 0, h))
    shp = jax.ShapeDtypeStruct((b_dim, s_dim, ATT_WIDTH), BF16)
    acc = pltpu.VMEM((s_dim, LANES), F32)
    res = pl.pallas_call(
        body, name="dil_bwd", grid=(b_dim, PAIRS),
        in_specs=[col(COL_QA), col(COL_KA), col(COL_VA), tab, tab, vec, vec, out, out, out] + [ANY] * n_w,
        out_specs=[out, out, out] + [ANY] * n_w, out_shape=[shp, shp, shp] + extra_shapes,
        scratch_shapes=[acc, acc, acc, acc, acc] + extra_sems,
        compiler_params=_params(("arbitrary", "arbitrary")),
    )(proj3, proj3, proj3, cs3, sn3, sel_a, sel_b, do3, o3, lse3, *extra)
    return res[:3], res[3:]


def _split_dot(x, tri):
    hi = x.astype(BF16)
    lo = (x - hi.astype(F32)).astype(BF16)
    return jnp.dot(hi, tri, preferred_element_type=F32) + jnp.dot(lo, tri, preferred_element_type=F32)


def _log_sigmoid(z):
    return jnp.minimum(z, 0.0) - jnp.log(1.0 + jnp.exp(-jnp.abs(z)))


def _sb_scores(qh, k16, valid):
    z = lax.dot_general(qh, k16, NT, preferred_element_type=F32)
    ls = _log_sigmoid(z)
    l1m = ls - z
    return ls, (l1m if valid is None else jnp.where(valid, l1m, 0.0))


def _sb_consts():
    r = lax.broadcasted_iota(jnp.int32, (BLOCK, BLOCK), 0)
    c = lax.broadcasted_iota(jnp.int32, (BLOCK, BLOCK), 1)
    after = (r > c).astype(BF16)
    before = (r < c).astype(BF16)
    qrow = lax.broadcasted_iota(jnp.int32, (SB_ROWS, BLOCK), 0)
    kcol = lax.broadcasted_iota(jnp.int32, (SB_ROWS, BLOCK), 1)
    return after, before, qrow, kcol


def _below(whole, lo, delta):
    if lo == 0:
        return whole + delta
    return whole + jnp.concatenate([jnp.zeros((lo,) + delta.shape[1:], delta.dtype), delta], axis=0)


def _pairs_loop(n_blocks, step, carry):
    def several(i, c):
        for j in range(SB_STEP):
            c = step(SB_STEP * i + j, c)
        return c

    return lax.fori_loop(0, n_blocks // SB_STEP, several, carry)


def _sb_fwd(proj3, rider=None):
    b_dim, s_dim, _ = proj3.shape
    scale = HEAD_DIM ** -0.5
    per = SB_ROWS // BLOCK
    extra, extra_shapes, extra_sems = _rider_parts(rider)
    n_w = len(extra)

    def body(*refs):
        q_ref, k_ref, v_ref = refs[:3]
        o_ref = refs[3 + n_w]
        step = pl.program_id(0) * PAIRS + pl.program_id(1)
        begin, end = _rider_hooks(rider, refs[3:3 + n_w], refs[4 + n_w:4 + 2 * n_w], refs[-2:], step, b_dim * PAIRS)
        begin()
        h1, h2 = _head_masks()
        after, _, qrow, kcol = _sb_consts()

        def qloop(qi, _):
            rows = pl.ds(pl.multiple_of(qi * SB_ROWS, SB_ROWS), SB_ROWS)
            q1, q2 = _split_heads(q_ref[rows, :] * scale, h1, h2)
            first = qi * per

            def block(kb, carry, lo):
                acc, run1, run2 = carry
                krows = pl.ds(pl.multiple_of(kb * BLOCK, BLOCK), BLOCK)
                k16 = k_ref[krows, :].astype(BF16)
                v1, v2 = _split_heads(v_ref[krows, :], h1, h2)
                valid = None if lo is None else kcol[:SB_ROWS - lo] < qrow[:SB_ROWS - lo]
                lo = lo or 0

                def head(qh, vh, run):
                    ls, l1m = _sb_scores(qh[lo:], k16, valid)
                    a = jnp.exp(ls + _split_dot(l1m, after) + run[lo:])
                    if valid is not None:
                        a = jnp.where(valid, a, 0.0)
                    return (jnp.dot(a.astype(BF16), vh, preferred_element_type=F32),
                            _below(run, lo, jnp.sum(l1m, axis=-1, keepdims=True)))

                o1, run1 = head(q1, v1, run1)
                o2, run2 = head(q2, v2, run2)
                return _below(acc, lo, o1 + o2), run1, run2

            zcol = jnp.zeros((SB_ROWS, 1), F32)
            carry = (jnp.zeros((SB_ROWS, LANES), F32), zcol, zcol)
            for kl in reversed(range(per)):
                carry = block(first + kl, carry, kl * BLOCK)
            acc, _, _ = _pairs_loop(first, lambda i, c: block(first - 1 - i, c, None), carry)
            o_ref[rows, :] = acc.astype(BF16)
            return 0

        lax.fori_loop(0, s_dim // SB_ROWS, qloop, 0)
        end()

    def col(c0):
        return pl.BlockSpec((None, s_dim, LANES),lambda b, h: (b, 0, c0 + h))

    res = pl.pallas_call(
        body, name="sb_fwd", grid=(b_dim, PAIRS),
        in_specs=[col(COL_QB), col(COL_KB), col(COL_VB)] + [ANY] * n_w, out_specs=[col(0)] + [ANY] * n_w,
        out_shape=[jax.ShapeDtypeStruct((b_dim, s_dim, ATT_WIDTH), BF16)] + extra_shapes,
        scratch_shapes=extra_sems,
        compiler_params=_params(("arbitrary", "arbitrary")),
    )(proj3, proj3, proj3, *extra)
    return res[0], res[1:]


def _sb_bwd(proj3, do3, rider=None):
    b_dim, s_dim, _ = proj3.shape
    scale = HEAD_DIM ** -0.5
    per = SB_ROWS // BLOCK
    nkb_max = s_dim // BLOCK
    extra, extra_shapes, extra_sems = _rider_parts(rider)
    n_w = len(extra)

    def body(*refs):
        q_ref, k_ref, v_ref, do_ref = refs[:4]
        dq_ref, dk_ref, dv_ref = refs[4 + n_w:7 + n_w]
        dka, dva, e_ref, sg_ref = refs[7 + 2 * n_w:11 + 2 * n_w]
        step = pl.program_id(0) * PAIRS + pl.program_id(1)
        begin, end = _rider_hooks(rider, refs[4:4 + n_w], refs[7 + n_w:7 + 2 * n_w], refs[-2:], step, b_dim * PAIRS)
        begin()
        h1, h2 = _head_masks()
        after, before, qrow, kcol = _sb_consts()
        dka[...] = jnp.zeros_like(dka)
        dva[...] = jnp.zeros_like(dva)

        def qloop(qi, _):
            rows = pl.ds(pl.multiple_of(qi * SB_ROWS, SB_ROWS), SB_ROWS)
            q1, q2 = _split_heads(q_ref[rows, :] * scale, h1, h2)
            do1, do2 = _split_heads(do_ref[rows, :].astype(F32), h1, h2)
            first = qi * per

            def pass1(kb, carry, lo):
                run1, run2 = carry
                krows = pl.ds(pl.multiple_of(kb * BLOCK, BLOCK), BLOCK)
                k16 = k_ref[krows, :].astype(BF16)
                v16 = v_ref[krows, :].astype(BF16)
                valid = None if lo is None else kcol[:SB_ROWS - lo] < qrow[:SB_ROWS - lo]
                lo = lo or 0
                part = pl.ds(lo, SB_ROWS - lo)

                def head(h, qh, doh, run):
                    ls, l1m = _sb_scores(qh[lo:], k16, valid)
                    a = jnp.exp(ls + _split_dot(l1m, after) + run[lo:])
                    if valid is not None:
                        a = jnp.where(valid, a, 0.0)
                    da = lax.dot_general(doh[lo:], v16, NT, preferred_element_type=F32)
                    e_ref[h, kb, part, :] = a * da
                    sg_ref[h, kb, part, :] = jnp.exp(ls)
                    return a.astype(BF16), _below(run, lo, jnp.sum(l1m, axis=-1, keepdims=True))

                a1, run1 = head(0, q1, do1, run1)
                a2, run2 = head(1, q2, do2, run2)
                dva[krows, :] += (lax.dot_general(a1, do1[lo:], TN, preferred_element_type=F32)
                                  + lax.dot_general(a2, do2[lo:], TN, preferred_element_type=F32))
                return run1, run2

            zcol = jnp.zeros((SB_ROWS, 1), F32)
            carry = (zcol, zcol)
            for kl in reversed(range(per)):
                carry = pass1(first + kl, carry, kl * BLOCK)
            _pairs_loop(first, lambda i, c: pass1(first - 1 - i, c, None), carry)

            def pass2(kb, carry, lo):
                dq, pre1, pre2 = carry
                krows = pl.ds(pl.multiple_of(kb * BLOCK, BLOCK), BLOCK)
                k1, k2 = _split_heads(k_ref[krows, :], h1, h2)
                valid = None if lo is None else kcol[:SB_ROWS - lo] < qrow[:SB_ROWS - lo]
                lo = lo or 0
                part = pl.ds(lo, SB_ROWS - lo)

                def head(h, pre):
                    ev = e_ref[h, kb, part, :]
                    sg = sg_ref[h, kb, part, :]
                    dz = ev * (1.0 - sg) - (_split_dot(ev, before) + pre[lo:]) * sg
                    if valid is not None:
                        dz = jnp.where(valid, dz, 0.0)
                    return dz.astype(BF16), _below(pre, lo, jnp.sum(ev, axis=-1, keepdims=True))

                dz1, pre1 = head(0, pre1)
                dz2, pre2 = head(1, pre2)
                dka[krows, :] += (lax.dot_general(dz1, q1[lo:], TN, preferred_element_type=F32)
                                  + lax.dot_general(dz2, q2[lo:], TN, preferred_element_type=F32))
                dq = _below(dq, lo, jnp.dot(dz1, k1, preferred_element_type=F32)
                            + jnp.dot(dz2, k2, preferred_element_type=F32))
                return dq, pre1, pre2

            carry = _pairs_loop(first, lambda i, c: pass2(i, c, None), (jnp.zeros((SB_ROWS, LANES), F32), zcol, zcol))
            for kl in range(per):
                carry = pass2(first + kl, carry, kl * BLOCK)
            dq = carry[0]
            dq_ref[rows, :] = (dq * scale).astype(BF16)
            return 0

        lax.fori_loop(0, s_dim // SB_ROWS, qloop, 0)
        dk_ref[...] = dka[...].astype(BF16)
        dv_ref[...] = dva[...].astype(BF16)
        end()

    def col(c0):
        return pl.BlockSpec((None, s_dim, LANES),lambda b, h: (b, 0, c0 + h))

    shp = jax.ShapeDtypeStruct((b_dim, s_dim, ATT_WIDTH), BF16)
    acc = pltpu.VMEM((s_dim, LANES), F32)
    strip = pltpu.VMEM((2, nkb_max, SB_ROWS, BLOCK), F32)
    res = pl.pallas_call(
        body, name="sb_bwd", grid=(b_dim, PAIRS),
        in_specs=[col(COL_QB), col(COL_KB), col(COL_VB), col(0)] + [ANY] * n_w,
        out_specs=[col(0), col(0), col(0)] + [ANY] * n_w,
        out_shape=[shp, shp, shp] + extra_shapes,
        scratch_shapes=[acc, acc, strip, strip] + extra_sems,
        compiler_params=_params(("arbitrary", "arbitrary")),
    )(proj3, proj3, proj3, do3, *extra)
    return res[:3], res[3:]


def _sigmoid(x):
    return 1.0 / (1.0 + jnp.exp(-x))


def _gate_out_norm(proj, ua, ub, w_out, x, g, *, tt=512):
    t_dim, d = ua.shape

    def body(ga_ref, gb_ref, ua_ref, ub_ref, w_ref, x_ref, g_ref, m_ref, h_ref, n_ref):
        mixed = (_sigmoid(ga_ref[...]) * ua_ref[...] + _sigmoid(gb_ref[...]) * ub_ref[...]).astype(BF16)
        m_ref[...] = mixed
        hv = x_ref[...] + jnp.dot(mixed, w_ref[...], preferred_element_type=F32)
        h_ref[...] = hv
        r = lax.rsqrt(jnp.mean(hv * hv, axis=-1, keepdims=True) + RMS_EPS)
        n_ref[...] = ((hv * r) * g_ref[...]).astype(BF16)

    row = pl.BlockSpec((tt, d), lambda i: (i, 0))
    return pl.pallas_call(
        body, name="gate_out_norm", grid=(t_dim // tt,),
        in_specs=[pl.BlockSpec((tt, d), lambda i: (i, 3)), pl.BlockSpec((tt, d), lambda i: (i, 4)), row, row,
                  pl.BlockSpec((d, d), lambda i: (0, 0)), row, pl.BlockSpec((1, d), lambda i: (0, 0))],
        out_specs=[row, row, row],
        out_shape=[jax.ShapeDtypeStruct((t_dim, d), BF16), jax.ShapeDtypeStruct((t_dim, d), F32),
                   jax.ShapeDtypeStruct((t_dim, d), BF16)],
        compiler_params=_params(("parallel",)),
    )(proj, proj, ua, ub, w_out, x, g)


def _out_dx_gate_bwd(dh, w_out, proj, ua, ub, *, tt=512):
    t_dim, d = ua.shape

    def body(dh_ref, w_ref, ga_ref, gb_ref, ua_ref, ub_ref, dua_ref, dub_ref, dg_ref):
        dm = lax.dot_general(dh_ref[...].astype(BF16), w_ref[...], NT, preferred_element_type=F32)
        sa = _sigmoid(ga_ref[...])
        sb = _sigmoid(gb_ref[...])
        dua_ref[...] = (dm * sa).astype(BF16)
        dub_ref[...] = (dm * sb).astype(BF16)
        dg_ref[:, :d] = (dm * ua_ref[...] * (sa * (1.0 - sa))).astype(BF16)
        dg_ref[:, d:] = (dm * ub_ref[...] * (sb * (1.0 - sb))).astype(BF16)

    row = pl.BlockSpec((tt, d), lambda i: (i, 0))
    wide = pl.BlockSpec((tt, 2 * d), lambda i: (i, 0))
    return pl.pallas_call(
        body, name="out_dx_gate_bwd", grid=(t_dim // tt,),
        in_specs=[row, pl.BlockSpec((d, d), lambda i: (0, 0)),
                  pl.BlockSpec((tt, d), lambda i: (i, 3)), pl.BlockSpec((tt, d), lambda i: (i, 4)), row, row],
        out_specs=[row, row, wide],
        out_shape=[jax.ShapeDtypeStruct((t_dim, d), BF16), jax.ShapeDtypeStruct((t_dim, d), BF16),
                   jax.ShapeDtypeStruct((t_dim, 2 * d), BF16)],
        compiler_params=_params(("parallel",)),
    )(dh, w_out, proj, proj, ua, ub)


def _ffn_up_swiglu(n, wg3, wu3, *, tt=1024):
    t_dim, d = n.shape
    n_s, f4, _ = wg3.shape

    def body(n_ref, wg_ref, wu_ref, g_ref, u_ref, a_ref):
        nv = n_ref[...]
        gv = lax.dot_general(nv, wg_ref[...], NT, preferred_element_type=F32)
        uv = lax.dot_general(nv, wu_ref[...], NT, preferred_element_type=F32)
        g_ref[...] = gv.astype(BF16)
        u_ref[...] = uv.astype(BF16)
        a_ref[...] = (gv * _sigmoid(gv) * uv).astype(BF16)

    wspec = pl.BlockSpec((None, f4, d), lambda i, s: (s, 0, 0))
    ospec = pl.BlockSpec((None, tt, f4), lambda i, s: (s, i, 0))
    shp = (n_s, t_dim, f4)
    return pl.pallas_call(
        body, name="ffn_up_swiglu", grid=(t_dim // tt, n_s),
        in_specs=[pl.BlockSpec((tt, d), lambda i, s: (i, 0)), wspec, wspec], out_specs=[ospec, ospec, ospec],
        out_shape=[jax.ShapeDtypeStruct(shp, BF16)] * 3,
        compiler_params=_params(("parallel", "parallel")),
    )(n, wg3, wu3)


def _ffn_down_dx_swiglu(dh, wd3, g3, u3, *, tt=1024):
    t_dim, d = dh.shape
    n_s, f4, _ = wd3.shape

    def body(dh_ref, w_ref, g_ref, u_ref, dg_ref, du_ref):
        da = lax.dot_general(dh_ref[...].astype(BF16), w_ref[...], NT, preferred_element_type=F32)
        gv = g_ref[...].astype(F32)
        sg = _sigmoid(gv)
        dg_ref[...] = (da * u_ref[...].astype(F32) * (sg + gv * sg * (1.0 - sg))).astype(BF16)
        du_ref[...] = (da * (gv * sg)).astype(BF16)

    spec = pl.BlockSpec((None, tt, f4), lambda i, s: (s, i, 0))
    shp = jax.ShapeDtypeStruct((n_s, t_dim, f4), BF16)
    return pl.pallas_call(
        body, name="ffn_down_dx_swiglu", grid=(t_dim // tt, n_s),
        in_specs=[pl.BlockSpec((tt, d), lambda i, s: (i, 0)), pl.BlockSpec((None, f4, d), lambda i, s: (s, 0, 0)),
                  spec, spec],
        out_specs=[spec, spec], out_shape=[shp, shp],
        compiler_params=_params(("parallel", "parallel")),
    )(dh, wd3, g3, u3)


def _mem_fwd(qm, kvm, *, tt=2048):
    b_dim, s_dim, _ = qm.shape
    n_mem = kvm.shape[1]
    scale = MEM_HEAD_DIM ** -0.5

    def body(q_ref, k_ref, v_ref, o_ref):
        sc = lax.dot_general(q_ref[0], k_ref[0], NT, preferred_element_type=F32) * scale
        p = jnp.exp(sc - jnp.max(sc, axis=-1, keepdims=True))
        p = p / jnp.sum(p, axis=-1, keepdims=True)
        o_ref[0] = jnp.dot(p.astype(BF16), v_ref[0], preferred_element_type=F32).astype(BF16)

    qs = pl.BlockSpec((1, tt, MEM_HEAD_DIM), lambda b, h, i: (b, i, h))
    return pl.pallas_call(
        body, name="mem_fwd", grid=(b_dim, N_HEADS_MEM, s_dim // tt),
        in_specs=[qs, pl.BlockSpec((1, n_mem, MEM_HEAD_DIM), lambda b, h, i: (b, 0, h)),
                  pl.BlockSpec((1, n_mem, MEM_HEAD_DIM), lambda b, h, i: (b, 0, N_HEADS_MEM + h))],
        out_specs=qs, out_shape=jax.ShapeDtypeStruct(qm.shape, BF16),
        compiler_params=_params(("parallel", "parallel", "parallel")),
    )(qm, kvm, kvm)


def _mem_bwd(qm, kvm, dom, *, tt=2048):
    b_dim, s_dim, _ = qm.shape
    n_mem = kvm.shape[1]
    scale = MEM_HEAD_DIM ** -0.5

    def body(q_ref, k_ref, v_ref, do_ref, dq_ref, dk_ref, dv_ref):
        qv, kv, vv, dov = q_ref[0], k_ref[0], v_ref[0], do_ref[0]
        sc = lax.dot_general(qv, kv, NT, preferred_element_type=F32) * scale
        p = jnp.exp(sc - jnp.max(sc, axis=-1, keepdims=True))
        p = p / jnp.sum(p, axis=-1, keepdims=True)
        dp = lax.dot_general(dov, vv, NT, preferred_element_type=F32)
        ds = (p * (dp - jnp.sum(p * dp, axis=-1, keepdims=True)) * scale).astype(BF16)
        dq_ref[0] = jnp.dot(ds, kv, preferred_element_type=F32).astype(BF16)

        @pl.when(pl.program_id(2) == 0)
        def _():
            dk_ref[...] = jnp.zeros_like(dk_ref)
            dv_ref[...] = jnp.zeros_like(dv_ref)

        dk_ref[0] += lax.dot_general(ds, qv, TN, preferred_element_type=F32)
        dv_ref[0] += lax.dot_general(p.astype(BF16), dov, TN, preferred_element_type=F32)

    qs = pl.BlockSpec((1, tt, MEM_HEAD_DIM), lambda b, h, i: (b, i, h))
    ks = pl.BlockSpec((1, n_mem, MEM_HEAD_DIM), lambda b, h, i: (b, 0, h))
    vs = pl.BlockSpec((1, n_mem, MEM_HEAD_DIM), lambda b, h, i: (b, 0, N_HEADS_MEM + h))
    return pl.pallas_call(
        body, name="mem_bwd", grid=(b_dim, N_HEADS_MEM, s_dim // tt),
        in_specs=[qs, ks, vs, qs], out_specs=[qs, ks, ks],
        out_shape=[jax.ShapeDtypeStruct(qm.shape, BF16), jax.ShapeDtypeStruct((b_dim, n_mem, MEM_WIDTH), F32),
                   jax.ShapeDtypeStruct((b_dim, n_mem, MEM_WIDTH), F32)],
        compiler_params=_params(("parallel", "parallel", "arbitrary")),
    )(qm, kvm, kvm, dom)


def _adamw_math(wv, gv, mv, vv):
    nm = ADAM_B1 * mv + (1.0 - ADAM_B1) * gv
    nv = ADAM_B2 * vv + (1.0 - ADAM_B2) * (gv * gv)
    m_hat = nm / (1.0 - ADAM_B1 ** ADAM_STEP)
    v_hat = nv / (1.0 - ADAM_B2 ** ADAM_STEP)
    return -ADAM_LR * (m_hat / (jnp.sqrt(v_hat) + ADAM_EPS) + ADAM_WD * wv), nm, nv


def _adamw(w, g, m, v, *, name):
    rows, cols = w.shape
    tr = _tile(rows, 256, 8)

    def body(w_ref, g_ref, m_ref, v_ref, d_ref, nm_ref, nv_ref):
        d_ref[...], nm_ref[...], nv_ref[...] = _adamw_math(w_ref[...], g_ref[...], m_ref[...], v_ref[...])

    spec = pl.BlockSpec((tr, cols), lambda i: (i, 0))
    shp = jax.ShapeDtypeStruct((rows, cols), F32)
    return pl.pallas_call(
        body, name=name, grid=(rows // tr,),
        in_specs=[spec] * 4, out_specs=[spec] * 3, out_shape=[shp] * 3,
        compiler_params=_params(("parallel",)),
    )(w, g, m, v)


def _prefetch_spec(grid, in_specs, out_specs):
    return pltpu.PrefetchScalarGridSpec(num_scalar_prefetch=1, grid=grid, in_specs=in_specs, out_specs=out_specs)


def _adamw_halves(w, mine, theirs, m, v, c_idx, *, name):
    rows, cols = w.shape
    half = rows // 2
    tr = _tile(half, _row_cap(cols), 8)
    nh = half // tr

    def body(c_ref, w_ref, mine_ref, theirs_ref, m_ref, v_ref, g_ref, d_ref, nm_ref, nv_ref):
        gv = jnp.where(pl.program_id(0) == c_ref[0], mine_ref[...], theirs_ref[...])
        g_ref[...] = gv
        d_ref[...], nm_ref[...], nv_ref[...] = _adamw_math(w_ref[...], gv, m_ref[...], v_ref[...])

    full = pl.BlockSpec((tr, cols), lambda h, i, c_ref: (h * nh + i, 0))
    part = pl.BlockSpec((tr, cols), lambda h, i, c_ref: (i, 0))
    shp = jax.ShapeDtypeStruct((rows, cols), F32)
    return pl.pallas_call(
        body, name=name, grid_spec=_prefetch_spec((2, nh), [full, part, part, full, full], [full] * 4),
        out_shape=[shp] * 4,
        compiler_params=_params(("parallel", "parallel")),
    )(c_idx, w, mine, theirs, m, v)


def _pair_sum(g3, theirs, c_idx, *, name):
    n, rows, cols = g3.shape
    half = rows // 2
    tr = _tile(half, _row_cap(cols), 16)

    def body(c_ref, g_ref, t_ref, o_ref):
        o_ref[...] = (g_ref[...] + t_ref[...]).astype(BF16)

    part = pl.BlockSpec((None, tr, cols), lambda s, i, c_ref: (s, i, 0))
    return pl.pallas_call(
        body, name=name,
        grid_spec=_prefetch_spec((n, half // tr),
                                 [pl.BlockSpec((None, None, tr, cols), lambda s, i, c_ref: (s, c_ref[0], i, 0)), part],
                                 part),
        out_shape=jax.ShapeDtypeStruct((n, half, cols), BF16),
        compiler_params=_params(("parallel", "parallel")),
    )(c_idx, g3.reshape(n, 2, half, cols), theirs)


def _chip_sum(pair, recv, s_idx, *, name):
    _, half, cols = pair.shape
    tr = _tile(half, _row_cap(cols), 16)

    def body(s_ref, p_ref, r_ref, o_ref):
        o_ref[...] = ((p_ref[...].astype(F32) + r_ref[0].astype(F32)) + r_ref[1].astype(F32)) + r_ref[2].astype(F32)

    return pl.pallas_call(
        body, name=name,
        grid_spec=_prefetch_spec((half // tr,),
                                 [pl.BlockSpec((None, tr, cols), lambda i, s_ref: (s_ref[0], i, 0)),
                                  pl.BlockSpec((N_CHIPS - 1, tr, cols), lambda i, s_ref: (0, i, 0))],
                                 pl.BlockSpec((tr, cols), lambda i, s_ref: (i, 0))),
        out_shape=jax.ShapeDtypeStruct((half, cols), F32),
        compiler_params=_params(("parallel",)),
    )(s_idx, pair, recv)


def _sum8(parts):
    n, rows, cols = parts.shape

    def body(p_ref, o_ref):
        acc = p_ref[0]
        for i in range(1, n):
            acc = acc + p_ref[i]
        o_ref[...] = acc

    return pl.pallas_call(
        body, name="small_sum", grid=(1,),
        in_specs=[pl.BlockSpec((n, rows, cols), lambda i: (0, 0, 0))],
        out_specs=pl.BlockSpec((rows, cols), lambda i: (0, 0)),
        out_shape=jax.ShapeDtypeStruct((rows, cols), parts.dtype),
        compiler_params=_params(("arbitrary",)),
    )(parts)


def _place():
    return lax.axis_index("x"), lax.axis_index("y"), lax.axis_index("c")


ANY = pl.BlockSpec(memory_space=pl.ANY)


def _rider_parts(rider):
    if rider is None:
        return (), [], []
    kind, arrays = rider
    n = len(arrays)
    shapes = {"gather": _gathered_shapes, "pair": _pair_shapes, "chip": _chip_shapes}[kind](arrays)
    sems = _gather_sems(n) if kind == "gather" else _exchange_sems(n if kind == "pair" else 3 * n)
    return tuple(arrays), shapes, sems


def _rider_hooks(rider, ins, outs, sems, step, n_steps):
    if rider is None:
        return (lambda: None), (lambda: None)
    if rider[0] == "gather":
        start, forward, finish = _gather_steps(ins, outs, *sems)
    else:
        start, finish = {"pair": _pair_steps, "chip": _chip_steps}[rider[0]](ins, outs, *sems)
        forward = None

    def begin():
        pl.when(step == 0)(start)

    def end():
        if forward is not None:
            pl.when(step == n_steps - 2)(forward)
        pl.when(step == n_steps - 1)(finish)

    return begin, end


def _gathered_shapes(shards):
    return [jax.ShapeDtypeStruct((N_CHIPS,) + s.shape, s.dtype) for s in shards]


def _gather_sems(n):
    return [pltpu.SemaphoreType.DMA((7 * n,)), pltpu.SemaphoreType.DMA((7 * n,))]


def _gather_steps(ins, outs, send_sems, recv_sems):
    n = len(ins)
    halves = [r.shape[0] // 2 for r in ins]
    x, y, c = _place()
    my_chip = 2 * x + y
    me, sibling = (x, y, c), (x, y, 1 - c)
    chips = [(1 - x, y), (x, 1 - y), (1 - x, 1 - y)]

    def half_of(w, chip, pc):
        return outs[w].at[chip, pl.ds(pc * halves[w], halves[w]), :]

    def copy(w, k, src, dst, to):
        return pltpu.make_async_remote_copy(
            src_ref=src, dst_ref=dst, send_sem=send_sems.at[7 * w + k], recv_sem=recv_sems.at[7 * w + k],
            device_id=to, device_id_type=MESH)

    def firsts():
        cps = []
        for w in range(n):
            cps.append(copy(w, 0, ins[w], outs[w].at[my_chip], sibling))
            mine = ins[w].at[pl.ds(c * halves[w], halves[w]), :]
            for j, (px, py) in enumerate(chips):
                cps.append(copy(w, 1 + j, mine, half_of(w, my_chip, c), (px, py, c)))
        return cps

    def passes():
        return [copy(w, 4 + j, half_of(w, 2 * px + py, c), half_of(w, 2 * px + py, c), sibling)
                for w in range(n) for j, (px, py) in enumerate(chips)]

    def start():
        for cp in firsts():
            cp.start()

    def forward():
        fws = passes()
        for w in range(n):
            for j, (px, py) in enumerate(chips):
                landed = half_of(w, 2 * px + py, c)
                copy(w, 1 + j, landed, landed, me).wait_recv()
                fws[3 * w + j].start()

    def finish():
        for w in range(n):
            copy(w, 0, ins[w], outs[w].at[my_chip], me).wait_recv()
            for j, (px, py) in enumerate(chips):
                landed = half_of(w, 2 * px + py, 1 - c)
                copy(w, 4 + j, landed, landed, me).wait_recv()
        for cp in firsts() + passes():
            cp.wait_send()

    return start, forward, finish


def _pair_shapes(grads):
    return [jax.ShapeDtypeStruct((g.shape[0], g.shape[1] // 2, g.shape[2]), g.dtype) for g in grads]


def _exchange_sems(n):
    return [pltpu.SemaphoreType.DMA((n,)), pltpu.SemaphoreType.DMA((n,))]


def _exchange_steps(copies):
    def start():
        for cp in copies():
            cp.start()

    def finish():
        for cp in copies():
            cp.wait()

    return start, finish


def _pair_steps(ins, outs, send_sems, recv_sems):
    x, y, c = _place()

    def copies():
        return [pltpu.make_async_remote_copy(
            src_ref=ins[w].at[:, pl.ds((1 - c) * (ins[w].shape[1] // 2), ins[w].shape[1] // 2), :], dst_ref=outs[w],
            send_sem=send_sems.at[w], recv_sem=recv_sems.at[w], device_id=(x, y, 1 - c), device_id_type=MESH)
            for w in range(len(ins))]

    return _exchange_steps(copies)


def _chip_shapes(pairs):
    return [jax.ShapeDtypeStruct((N_CHIPS - 1,) + p.shape[1:], p.dtype) for p in pairs]


def _chip_steps(ins, outs, send_sems, recv_sems):
    x, y, c = _place()
    others = [(1 - x, y), (x, 1 - y), (1 - x, 1 - y)]

    def copies():
        return [pltpu.make_async_remote_copy(
            src_ref=ins[w].at[2 * px + py], dst_ref=outs[w].at[j],
            send_sem=send_sems.at[3 * w + j], recv_sem=recv_sems.at[3 * w + j],
            device_id=(px, py, c), device_id_type=MESH)
            for w in range(len(ins)) for j, (px, py) in enumerate(others)]

    return _exchange_steps(copies)


def _swap_halves(mine):
    n = len(mine)

    def body(*refs):
        ins, outs, send_sems, recv_sems = refs[:n], refs[n:2 * n], refs[2 * n], refs[2 * n + 1]
        x, y, c = _place()
        copies = [pltpu.make_async_remote_copy(
            src_ref=ins[w], dst_ref=outs[w], send_sem=send_sems.at[w], recv_sem=recv_sems.at[w],
            device_id=(x, y, 1 - c), device_id_type=MESH) for w in range(n)]
        for cp in copies:
            cp.start()
        for cp in copies:
            cp.wait()

    return pl.pallas_call(
        body, name="grad_swap_halves",
        out_shape=[jax.ShapeDtypeStruct(h.shape, h.dtype) for h in mine],
        in_specs=[ANY] * n, out_specs=[ANY] * n,
        scratch_shapes=[pltpu.SemaphoreType.DMA((n,)), pltpu.SemaphoreType.DMA((n,))],
    )(*mine)


def _gather_small(small):
    srows, cols = small.shape

    def body(s_ref, all_ref, send_sems, recv_sems, local_sem):
        x, y, c = _place()
        me = 4 * x + 2 * y + c
        keep_small = pltpu.make_async_copy(s_ref, all_ref.at[me], local_sem)
        keep_small.start()
        sends = []
        for kk in range(1, 8):
            peer = (x ^ (kk >> 2), y ^ ((kk >> 1) & 1), c ^ (kk & 1))
            sends.append(pltpu.make_async_remote_copy(
                src_ref=s_ref, dst_ref=all_ref.at[me],
                send_sem=send_sems.at[kk], recv_sem=recv_sems.at[kk], device_id=peer, device_id_type=MESH))
        for cp in sends:
            cp.start()
        for kk in range(1, 8):
            px, py, pc = x ^ (kk >> 2), y ^ ((kk >> 1) & 1), c ^ (kk & 1)
            pltpu.make_async_remote_copy(
                src_ref=s_ref, dst_ref=all_ref.at[4 * px + 2 * py + pc],
                send_sem=send_sems.at[kk], recv_sem=recv_sems.at[kk], device_id=(px, py, pc),
                device_id_type=MESH).wait_recv()
        for cp in sends:
            cp.wait_send()
        keep_small.wait()

    return pl.pallas_call(
        body, name="gather_small",
        out_shape=jax.ShapeDtypeStruct((8, srows, cols), small.dtype),
        in_specs=[ANY], out_specs=ANY,
        scratch_shapes=[pltpu.SemaphoreType.DMA((8,)), pltpu.SemaphoreType.DMA((8,)), pltpu.SemaphoreType.DMA],
    )(small)


SHARDED = (("w_in", D_MODEL, IN_COLS, 1), ("w_up_a", ATT_WIDTH, D_MODEL, 1), ("w_up_b", ATT_WIDTH, D_MODEL, 1),
           ("w_out", D_MODEL, D_MODEL, 0), ("w_q_mem", D_MODEL, MEM_WIDTH, 0), ("w_kv_mem", D_MODEL, 2 * MEM_WIDTH, 0),
           ("w_o_mem", MEM_WIDTH, D_MODEL, 1), ("w_ffn_gate", D_FF, D_MODEL, 0), ("w_ffn_up", D_FF, D_MODEL, 0),
           ("w_ffn_down", D_FF, D_MODEL, 0))
TRANSPOSED = ("w_ffn_gate", "w_ffn_up")
NAMES = tuple(n for n, _, _, _ in SHARDED)


def _held(name, shard):
    return shard.T if name in TRANSPOSED else shard
EARLY, LATE = NAMES[:1], NAMES[1:]
GAINS = ("g_mix", "g_mem_q", "g_mem_kv", "g_ffn", "g_final")


def _natural(w3):
    n, r, c = w3.shape
    return w3.reshape(n * r, c)


def _shard_major(g, axis):
    if axis == 1:
        return g
    r, c = g.shape
    return g.reshape(N_CHIPS, r // N_CHIPS, c)


def kernel(x, mem, positions, g_mix, w_in, w_up_a, w_up_b, w_out, g_mem_q, g_mem_kv, w_q_mem, w_kv_mem, w_o_mem, g_ffn, w_ffn_gate, w_ffn_up, w_ffn_down, g_final, loss_target, m_g_mix, m_w_in, m_w_up_a, m_w_up_b, m_w_out, m_g_mem_q, m_g_mem_kv, m_w_q_mem, m_w_kv_mem, m_w_o_mem, m_g_ffn, m_w_ffn_gate, m_w_ffn_up, m_w_ffn_down, m_g_final, v_g_mix, v_w_in, v_w_up_a, v_w_up_b, v_w_out, v_g_mem_q, v_g_mem_kv, v_w_q_mem, v_w_kv_mem, v_w_o_mem, v_g_ffn, v_w_ffn_gate, v_w_ffn_up, v_w_ffn_down, v_g_final):
    given = dict(locals())
    shards = {n: _held(n, given[n][0]) for n in NAMES}

    early_shards = [shards[n].astype(BF16) for n in EARLY]
    late_shards = [shards[n].astype(BF16) for n in LATE]
    c_idx = lax.axis_index("c").astype(jnp.int32).reshape(1)
    s_idx = (2 * lax.axis_index("x") + lax.axis_index("y")).astype(jnp.int32).reshape(1)

    loss_row, grad_x, mine, gain_grads = _local_step(x, mem, positions, loss_target, g_mix, g_mem_q, g_mem_kv,
                                                     g_ffn, g_final, {}, early_shards, late_shards, (c_idx, s_idx))
    return _reduce_and_update(given, shards, loss_row, grad_x, mine, gain_grads, c_idx)


def _reduce_halves(glist, names, c_idx, s_idx, pair_exchange, chip_exchange):
    theirs = pair_exchange(glist)
    pairs = [_pair_sum(g, t, c_idx, name="pair_sum_" + n) for n, g, t in zip(names, glist, theirs)]
    recv = chip_exchange(pairs)
    return [_chip_sum(p, r, s_idx, name="chip_sum_" + n) for n, p, r in zip(names, pairs, recv)]


def _local_step(x, mem, positions, loss_target, g_mix, g_mem_q, g_mem_kv, g_ffn, g_final, wf,
                early_shards=None, late_shards=None, place=None):
    b_dim, s_dim, d = x.shape
    t_dim = b_dim * s_dim
    n_mem = mem.shape[1]
    wf = dict(wf)

    xb = x.reshape(t_dim, d)
    tgt = loss_target.reshape(t_dim, d)
    memf = mem.reshape(b_dim * n_mem, d)
    gfin = g_final.reshape(1, d)
    pos = positions.reshape(t_dim, 1).astype(F32)

    lane = jnp.arange(LANES) % HEAD_DIM
    half = ROPE_DIM // 2
    inv_freq = ROPE_THETA ** (-jnp.arange(half, dtype=F32) / half)
    inv_lane = jnp.where(lane < ROPE_DIM, inv_freq[lane % half], 0.0).reshape(1, -1).astype(F32)
    sel_a = (lane < half).astype(F32).reshape(1, -1)
    sel_b = ((lane >= half) & (lane < ROPE_DIM)).astype(F32).reshape(1, -1)

    def rows3(t):
        return t.reshape(b_dim, s_dim, t.shape[-1])

    def rows2(t):
        return t.reshape(t_dim, t.shape[-1])

    if early_shards:
        n1, gathered = _rms_fwd(xb, g_mix, name="rms_mix", rider=("gather", early_shards))
        wf.update(zip(EARLY, gathered))
    else:
        n1 = _rms_fwd(xb, g_mix, name="rms_mix")
    proj = _mm_cs(n1, wf["w_in"], name="mm_in")
    proj3 = rows3(proj)
    cs, sn = _rope_table(pos, inv_lane, sel_a, sel_b)
    cs3, sn3 = rows3(cs), rows3(sn)
    (oa16, oa32, lse_a), _ = _dil_fwd(proj3, cs3, sn3, sel_a, sel_b)
    ob16, gathered = _sb_fwd(proj3, ("gather", late_shards) if late_shards else None)
    wf.update(zip(LATE, gathered))
    w_out, w_q, w_kv = _natural(wf["w_out"]), _natural(wf["w_q_mem"]), _natural(wf["w_kv_mem"])
    oa, ob = rows2(oa16), rows2(ob16)
    ua = _mm_sm(oa, wf["w_up_a"], name="mm_up_a")
    ub = _mm_sm(ob, wf["w_up_b"], name="mm_up_b")
    mixed, h1, hn = _gate_out_norm(proj, ua, ub, w_out, xb, g_mem_q)

    memn = _rms_fwd(memf, g_mem_kv, name="rms_mem_kv")
    qm = _mm(hn, w_q, name="mm_q_mem", out_dtype=BF16)
    kvm = _mm(memn, w_kv, name="mm_kv_mem", out_dtype=BF16)
    qm3, kvm3 = rows3(qm), kvm.reshape(b_dim, n_mem, 2 * MEM_WIDTH)
    om = rows2(_mem_fwd(qm3, kvm3))
    h2 = _mm_sm(om, wf["w_o_mem"], name="mm_o_mem", add=h1)

    n3 = _rms_fwd(h2, g_ffn, name="rms_ffn")
    gate3, up3, act3 = _ffn_up_swiglu(n3, wf["w_ffn_gate"], wf["w_ffn_up"])
    loss_row, dh3, dg_final = _down_final(act3, wf["w_ffn_down"], h2, gfin, tgt)

    grads = {}
    grads["w_ffn_down"] = _mm_ffn_down_dw(act3, dh3, name="mm_down_dw")
    dgate3, dup3 = _ffn_down_dx_swiglu(dh3, wf["w_ffn_down"], gate3, up3)
    grads["w_ffn_gate"] = _mm_ffn_down_dw(dgate3, n3, name="mm_gate_dw")
    grads["w_ffn_up"] = _mm_ffn_down_dw(dup3, n3, name="mm_up_dw")
    dn3 = _ffn_up_dx(dgate3, dup3, wf["w_ffn_gate"], wf["w_ffn_up"])
    dh2, dg_ffn = _rms_bwd(h2, g_ffn, dn3, dh3, name="rms_ffn_bwd")

    dom = _mm_sm_dx(dh2, wf["w_o_mem"], name="mm_o_mem_dx", out_dtype=BF16)
    grads["w_o_mem"] = _mm_sm_dw(om, dh2, name="mm_o_mem_dw")
    dqm, dkm, dvm = _mem_bwd(qm3, kvm3, rows3(dom))
    dqm = rows2(dqm)
    dkvm = jnp.concatenate([dkm, dvm], axis=-1).reshape(b_dim * n_mem, 2 * MEM_WIDTH).astype(BF16)
    grads["w_q_mem"] = _shard_major(_mm(hn, dqm, name="mm_q_mem_dw", ta=True), 0)
    dhn = _mm(dqm, w_q, name="mm_q_mem_dx", tb=True)
    grads["w_kv_mem"] = _shard_major(_mm(memn, dkvm, name="mm_kv_mem_dw", ta=True), 0)
    dmemn = _mm(dkvm, w_kv, name="mm_kv_mem_dx", tb=True)
    _, dg_mem_kv = _rms_bwd(memf, g_mem_kv, dmemn, None, name="rms_mem_kv_bwd")
    dh1, dg_mem_q = _rms_bwd(h1, g_mem_q, dhn, dh2, name="rms_mem_q_bwd")

    grads["w_out"] = _shard_major(_mm(mixed, dh1, name="mm_out_dw", ta=True), 0)
    dua, dub, dgates = _out_dx_gate_bwd(dh1, w_out, proj, ua, ub)
    doa = _mm_sm_dx(dua, wf["w_up_a"], name="mm_up_a_dx")
    grads["w_up_a"] = _mm_sm_dw(oa, dua, name="mm_up_a_dw")
    dob = _mm_sm_dx(dub, wf["w_up_b"], name="mm_up_b_dx", out_dtype=BF16)
    grads["w_up_b"] = _mm_sm_dw(ob, dub, name="mm_up_b_dw")

    att = {}

    def dil_with_pairs(glist):
        att["a"], theirs = _dil_bwd(proj3, cs3, sn3, sel_a, sel_b, rows3(doa), oa32, lse_a,
                                    ("pair", glist) if glist else None)
        return theirs

    def sb_with_chips(pairs):
        att["b"], recv = _sb_bwd(proj3, rows3(dob), ("chip", pairs) if pairs else None)
        return recv

    if place is None:
        dil_with_pairs(())
        sb_with_chips(())
    else:
        mine_late = _reduce_halves([grads[n] for n in LATE], LATE, *place, dil_with_pairs, sb_with_chips)
    dproj = jnp.concatenate([rows2(t) for t in att["a"] + att["b"]] + [dgates], axis=1)
    grads["w_in"] = _mm_cs_dw(n1, dproj, name="mm_in_dw")
    if place is None:
        dn1 = _mm_cs_dx(dproj, wf["w_in"], name="mm_in_dx")
        dx, dg_mix = _rms_bwd(xb, g_mix, dn1, dh1, name="rms_mix_bwd")
    else:
        tail = {}

        def dx_with_pairs(glist):
            tail["dn1"], theirs = _mm_cs_dx(dproj, wf["w_in"], name="mm_in_dx", rider=("pair", glist))
            return theirs

        def rms_with_chips(pairs):
            tail["dx"], tail["dg"], recv = _rms_bwd(xb, g_mix, tail["dn1"], dh1, name="rms_mix_bwd",
                                                    rider=("chip", pairs))
            return recv

        mine_early = _reduce_halves([grads[n] for n in EARLY], EARLY, *place, dx_with_pairs, rms_with_chips)
        dx, dg_mix = tail["dx"], tail["dg"]
    grad_x = dx.reshape(b_dim, s_dim, d)
    gains = (dg_mix, dg_mem_q, dg_mem_kv, dg_ffn, dg_final)
    if place is None:
        return loss_row, grad_x, grads, gains
    return loss_row, grad_x, mine_early + mine_late, gains


def _reduce_and_update(given, shards, loss_row, grad_x, mine, gain_grads, c_idx):
    d = D_MODEL
    dg_mix, dg_mem_q, dg_mem_kv, dg_ffn, dg_final = gain_grads
    small = jnp.concatenate([dg_mix, dg_mem_q, dg_mem_kv, dg_ffn, dg_final,
                             jnp.pad(loss_row, ((0, 0), (0, FLAT_COLS - LANES))), jnp.zeros((2, FLAT_COLS), F32)], axis=0)
    small_all = _gather_small(small)
    others = _swap_halves(mine)
    small_sum = _sum8(small_all)
    loss = small_sum[5, 0]

    out_g, out_d, out_m, out_v = {}, {}, {}, {}
    for n, mine_n, other_n in zip(NAMES, mine, others):
        res = _adamw_halves(shards[n], mine_n, other_n, _held(n, given["m_" + n][0]), _held(n, given["v_" + n][0]),
                            c_idx, name="adamw_" + n)
        out_g[n], out_d[n], out_m[n], out_v[n] = [_held(n, r)[None] for r in res]
    gain_w = jnp.concatenate([given[n].reshape(1, d) for n in GAINS], axis=0)
    gain_m = jnp.concatenate([given["m_" + n].reshape(1, d) for n in GAINS], axis=0)
    gain_v = jnp.concatenate([given["v_" + n].reshape(1, d) for n in GAINS], axis=0)
    gain_g = small_sum[:len(GAINS)]
    gd, gm, gv = _adamw(gain_w, gain_g, gain_m, gain_v, name="adamw_gains")
    for i, n in enumerate(GAINS):
        shape = given[n].shape
        out_g[n], out_d[n] = gain_g[i].reshape(shape), gd[i].reshape(shape)
        out_m[n], out_v[n] = gm[i].reshape(shape), gv[i].reshape(shape)

    order = ["g_mix", "w_in", "w_up_a", "w_up_b", "w_out", "g_mem_q", "g_mem_kv", "w_q_mem", "w_kv_mem", "w_o_mem",
             "g_ffn", "w_ffn_gate", "w_ffn_up", "w_ffn_down", "g_final"]
    return (loss, grad_x, *[out_g[n] for n in order], *[out_d[n] for n in order],
            *[out_m[n] for n in order], *[out_v[n] for n in order])
```

```python
import jax
import jax.numpy as jnp
from jax import lax
from jax.experimental import pallas as pl
from jax.experimental.pallas import tpu as pltpu

F32 = jnp.float32
BF16 = jnp.bfloat16
MESH = pl.DeviceIdType.MESH

D_MODEL = 1024
HEAD_DIM = 64
N_HEADS = 8
ATT_WIDTH = N_HEADS * HEAD_DIM
DIL_PATTERNS = ((128, 1), (512, 4), (2048, 16))
BLOCK = 128
SB_ROWS = 1024
SB_STEP = 4
ROPE_THETA = 500000.0
ROPE_DIM = HEAD_DIM // 4
N_HEADS_MEM = 4
MEM_HEAD_DIM = 128
MEM_WIDTH = N_HEADS_MEM * MEM_HEAD_DIM
D_FF = 2816
IN_COLS = 6 * ATT_WIDTH + 2 * D_MODEL
RMS_EPS = 1e-6
ADAM_LR = 0.001
ADAM_B1 = 0.9
ADAM_B2 = 0.999
ADAM_EPS = 1e-08
ADAM_WD = 0.01
ADAM_STEP = 10

N_CHIPS = 4
LANES = 128
FLAT_COLS = 1024
VMEM_LIMIT = 56 * 1024 * 1024

PAIRS = ATT_WIDTH // LANES
COL_QA, COL_KA, COL_VA, COL_QB, COL_KB, COL_VB = (i * PAIRS for i in range(6))

MM_CAP = 1408
TOK_CAP = 2048
NN = (((1,), (0,)), ((), ()))
NT = (((1,), (1,)), ((), ()))
TN = (((0,), (0,)), ((), ()))
BNN = (((2,), (1,)), ((0,), (0,)))
BNT = (((2,), (2,)), ((0,), (0,)))
BTN = (((1,), (1,)), ((0,), (0,)))
DIL_BATCH = 16
DIL_CHUNK = 512


def _tile(dim, cap, unit=LANES):
    if dim <= cap:
        return dim
    best = None
    for t in range(unit, cap + 1, unit):
        if dim % t == 0:
            best = t
    assert best is not None, (dim, cap)
    return best


def _row_cap(cols):
    return max(256, (1 << 18) // cols)


def _params(sem):
    return pltpu.CompilerParams(dimension_semantics=sem, vmem_limit_bytes=VMEM_LIMIT)


def _mm(a, b, *, name, ta=False, tb=False, add=None, out_dtype=F32,
        tm_cap=MM_CAP, tn_cap=MM_CAP, tk_cap=MM_CAP):
    if ta:
        k_dim, m_dim = a.shape
    else:
        m_dim, k_dim = a.shape
    if tb:
        n_dim, kb = b.shape
    else:
        kb, n_dim = b.shape
    assert kb == k_dim, (a.shape, b.shape, ta, tb)
    tm, tn, tk = _tile(m_dim, tm_cap), _tile(n_dim, tn_cap), _tile(k_dim, tk_cap)
    nk = k_dim // tk
    dims = (((0 if ta else 1,), (1 if tb else 0,)), ((), ()))
    has_add = add is not None

    def body(*refs):
        if has_add:
            a_ref, b_ref, add_ref, o_ref = refs[:4]
        else:
            a_ref, b_ref, o_ref = refs[:3]
        part = lax.dot_general(a_ref[...].astype(BF16), b_ref[...].astype(BF16), dims, preferred_element_type=F32)

        def finish(r):
            if has_add:
                r = add_ref[...] + r
            o_ref[...] = r.astype(out_dtype)

        if nk == 1:
            finish(part)
            return
        acc_ref = refs[-1]
        k = pl.program_id(2)

        @pl.when(k == 0)
        def _():
            acc_ref[...] = part

        @pl.when(k > 0)
        def _():
            acc_ref[...] += part

        @pl.when(k == nk - 1)
        def _():
            finish(acc_ref[...])

    a_spec = pl.BlockSpec((tk, tm), lambda i, j, k: (k, i)) if ta else pl.BlockSpec((tm, tk), lambda i, j, k: (i, k))
    b_spec = pl.BlockSpec((tn, tk), lambda i, j, k: (j, k)) if tb else pl.BlockSpec((tk, tn), lambda i, j, k: (k, j))
    o_spec = pl.BlockSpec((tm, tn), lambda i, j, k: (i, j))
    in_specs = [a_spec, b_spec] + ([o_spec] if has_add else [])
    args = (a, b) + ((add,) if has_add else ())
    return pl.pallas_call(
        body, name=name, grid=(m_dim // tm, n_dim // tn, nk),
        in_specs=in_specs, out_specs=o_spec,
        out_shape=jax.ShapeDtypeStruct((m_dim, n_dim), out_dtype),
        scratch_shapes=[pltpu.VMEM((tm, tn), F32)] if nk > 1 else [],
        compiler_params=_params(("parallel", "parallel", "arbitrary")),
    )(*args)


def _mm_core(name, a, b, a_spec, b_spec, o_spec, out_shape, grid, dims, *, add=None, out_dtype=F32, rider=None):
    nk = grid[2]
    has_add = add is not None
    n_in = 3 if has_add else 2
    acc_shape = tuple(d for d in o_spec.block_shape if d is not None)
    extra, extra_shapes, extra_sems = _rider_parts(rider)
    n_w = len(extra)

    def body(*refs):
        a_ref, b_ref = refs[:2]
        o_ref = refs[n_in + n_w]
        step = (pl.program_id(0) * grid[1] + pl.program_id(1)) * nk + pl.program_id(2)
        begin, end = _rider_hooks(rider, refs[n_in:n_in + n_w], refs[n_in + n_w + 1:n_in + 2 * n_w + 1], refs[-2:],
                                  step, grid[0] * grid[1] * nk)
        begin()
        part = lax.dot_general(a_ref[...].astype(BF16), b_ref[...].astype(BF16), dims, preferred_element_type=F32)

        def finish(r):
            if has_add:
                r = refs[2][...] + r
            o_ref[...] = r.astype(out_dtype)

        if nk == 1:
            finish(part)
        else:
            acc_ref = refs[n_in + 2 * n_w + 1]
            k = pl.program_id(2)

            @pl.when(k == 0)
            def _():
                acc_ref[...] = part

            @pl.when(k > 0)
            def _():
                acc_ref[...] += part

            @pl.when(k == nk - 1)
            def _():
                finish(acc_ref[...])
        end()

    in_specs = [a_spec, b_spec] + ([o_spec] if has_add else []) + [ANY] * n_w
    args = (a, b) + ((add,) if has_add else ()) + extra
    res = pl.pallas_call(
        body, name=name, grid=grid, in_specs=in_specs, out_specs=[o_spec] + [ANY] * n_w,
        out_shape=[jax.ShapeDtypeStruct(out_shape, out_dtype)] + extra_shapes,
        scratch_shapes=([pltpu.VMEM(acc_shape, F32)] if nk > 1 else []) + extra_sems,
        compiler_params=_params(("arbitrary",) * 3 if n_w else ("parallel", "parallel", "arbitrary")),
    )(*args)
    return (res[0], res[1:]) if n_w else res[0]


def _mm_cs(a, w3, *, name):
    m_dim, k_dim = a.shape
    _, _, n4 = w3.shape
    tm, tn, tk = _tile(m_dim, TOK_CAP), _tile(n4, MM_CAP), _tile(k_dim, MM_CAP)
    npb = n4 // tn
    return _mm_core(name, a, w3,
                    pl.BlockSpec((tm, tk), lambda i, j, k: (i, k)),
                    pl.BlockSpec((None, tk, tn), lambda i, j, k: (j // npb, k, j % npb)),
                    pl.BlockSpec((tm, tn), lambda i, j, k: (i, j)),
                    (m_dim, N_CHIPS * n4), (m_dim // tm, N_CHIPS * npb, k_dim // tk), NN)


def _mm_cs_dx(dy, w3, *, name, out_dtype=F32, rider=None):
    m_dim, _ = dy.shape
    _, k_dim, n4 = w3.shape
    tm, tkw, tn = _tile(m_dim, TOK_CAP), _tile(k_dim, MM_CAP), _tile(n4, MM_CAP)
    npb = n4 // tn
    return _mm_core(name, dy, w3,
                    pl.BlockSpec((tm, tn), lambda i, j, k: (i, k)),
                    pl.BlockSpec((None, tkw, tn), lambda i, j, k: (k // npb, j, k % npb)),
                    pl.BlockSpec((tm, tkw), lambda i, j, k: (i, j)),
                    (m_dim, k_dim), (m_dim // tm, k_dim // tkw, N_CHIPS * npb), NT, out_dtype=out_dtype, rider=rider)


def _mm_cs_dw(a, dy, *, name):
    m_dim, k_dim = a.shape
    n4 = dy.shape[1] // N_CHIPS
    tmk, tn, tk = _tile(k_dim, MM_CAP), _tile(n4, MM_CAP), _tile(m_dim, TOK_CAP)
    npb = n4 // tn
    return _mm_core(name, a, dy,
                    pl.BlockSpec((tk, tmk), lambda i, j, k: (k, i)),
                    pl.BlockSpec((tk, tn), lambda i, j, k: (k, j)),
                    pl.BlockSpec((None, tmk, tn), lambda i, j, k: (j // npb, i, j % npb)),
                    (N_CHIPS, k_dim, n4), (k_dim // tmk, N_CHIPS * npb, m_dim // tk), TN)


def _mm_sm(a, w3, *, name, add=None, out_dtype=F32, tt=1024):
    t_dim, k_dim = a.shape
    n_s, _, n4 = w3.shape
    has_add = add is not None

    def body(*refs):
        a_ref, w_ref, o_ref = refs[0], refs[1], refs[-1]
        av = a_ref[...].astype(BF16)
        for s in range(n_s):
            cols = slice(s * n4, (s + 1) * n4)
            r = jnp.dot(av, w_ref[s], preferred_element_type=F32)
            if has_add:
                r = refs[2][:, cols] + r
            o_ref[:, cols] = r.astype(out_dtype)

    row = pl.BlockSpec((tt, n_s * n4), lambda i: (i, 0))
    return pl.pallas_call(
        body, name=name, grid=(t_dim // tt,),
        in_specs=[pl.BlockSpec((tt, k_dim), lambda i: (i, 0)), pl.BlockSpec(w3.shape, lambda i: (0, 0, 0))]
        + ([row] if has_add else []),
        out_specs=row, out_shape=jax.ShapeDtypeStruct((t_dim, n_s * n4), out_dtype),
        compiler_params=_params(("parallel",)),
    )(*((a, w3) + ((add,) if has_add else ())))


def _mm_sm_dx(dy, w3, *, name, out_dtype=F32, tt=1024):
    t_dim, _ = dy.shape
    n_s, k_dim, n4 = w3.shape

    def body(dy_ref, w_ref, o_ref):
        dyv = dy_ref[...].astype(BF16)
        acc = lax.dot_general(dyv[:, :n4], w_ref[0], NT, preferred_element_type=F32)
        for s in range(1, n_s):
            acc = acc + lax.dot_general(dyv[:, s * n4:(s + 1) * n4], w_ref[s], NT, preferred_element_type=F32)
        o_ref[...] = acc.astype(out_dtype)

    return pl.pallas_call(
        body, name=name, grid=(t_dim // tt,),
        in_specs=[pl.BlockSpec((tt, n_s * n4), lambda i: (i, 0)), pl.BlockSpec(w3.shape, lambda i: (0, 0, 0))],
        out_specs=pl.BlockSpec((tt, k_dim), lambda i: (i, 0)),
        out_shape=jax.ShapeDtypeStruct((t_dim, k_dim), out_dtype),
        compiler_params=_params(("parallel",)),
    )(dy, w3)


def _mm_sm_dw(a, dy, *, name, tk=1024):
    t_dim, k_dim = a.shape
    n4 = dy.shape[1] // N_CHIPS

    def body(a_ref, dy_ref, o_ref):
        av = a_ref[...].astype(BF16)
        dyv = dy_ref[...].astype(BF16)

        @pl.when(pl.program_id(0) == 0)
        def _():
            o_ref[...] = jnp.zeros_like(o_ref)

        for s in range(N_CHIPS):
            o_ref[s] += lax.dot_general(av, dyv[:, s * n4:(s + 1) * n4], TN, preferred_element_type=F32)

    return pl.pallas_call(
        body, name=name, grid=(t_dim // tk,),
        in_specs=[pl.BlockSpec((tk, k_dim), lambda i: (i, 0)), pl.BlockSpec((tk, N_CHIPS * n4), lambda i: (i, 0))],
        out_specs=pl.BlockSpec((N_CHIPS, k_dim, n4), lambda i: (0, 0, 0)),
        out_shape=jax.ShapeDtypeStruct((N_CHIPS, k_dim, n4), F32),
        compiler_params=_params(("arbitrary",)),
    )(a, dy)


def _ffn_up_dx(dg3, du3, wg3, wu3, *, tt=512):
    n_s, t_dim, f4 = dg3.shape
    d = wg3.shape[2]

    def body(dg_ref, du_ref, wg_ref, wu_ref, o_ref):
        acc = None
        for s in range(n_s):
            part = (jnp.dot(dg_ref[s], wg_ref[s], preferred_element_type=F32)
                    + jnp.dot(du_ref[s], wu_ref[s], preferred_element_type=F32))
            acc = part if acc is None else acc + part
        o_ref[...] = acc

    a_spec = pl.BlockSpec((n_s, tt, f4), lambda i: (0, i, 0))
    w_spec = pl.BlockSpec(wg3.shape, lambda i: (0, 0, 0))
    return pl.pallas_call(
        body, name="ffn_up_dx", grid=(t_dim // tt,),
        in_specs=[a_spec, a_spec, w_spec, w_spec], out_specs=pl.BlockSpec((tt, d), lambda i: (i, 0)),
        out_shape=jax.ShapeDtypeStruct((t_dim, d), F32),
        compiler_params=_params(("parallel",)),
    )(dg3, du3, wg3, wu3)


def _mm_ffn_down_dw(act3, dh, *, name, tk=1024):
    n_s, t_dim, f4 = act3.shape
    d = dh.shape[1]

    def body(a_ref, b_ref, o_ref):
        bv = b_ref[...].astype(BF16)

        @pl.when(pl.program_id(0) == 0)
        def _():
            o_ref[...] = jnp.zeros_like(o_ref)

        for s in range(n_s):
            o_ref[s] += lax.dot_general(a_ref[s], bv, TN, preferred_element_type=F32)

    return pl.pallas_call(
        body, name=name, grid=(t_dim // tk,),
        in_specs=[pl.BlockSpec((n_s, tk, f4), lambda i: (0, i, 0)), pl.BlockSpec((tk, d), lambda i: (i, 0))],
        out_specs=pl.BlockSpec((n_s, f4, d), lambda i: (0, 0, 0)),
        out_shape=jax.ShapeDtypeStruct((n_s, f4, d), F32),
        compiler_params=_params(("arbitrary",)),
    )(act3, dh)


def _rms_fwd(x, g, *, name, tt=512, rider=None):
    t_dim, d = x.shape
    tt = _tile(t_dim, tt, 8)
    extra, extra_shapes, extra_sems = _rider_parts(rider)
    n_w = len(extra)

    def body(*refs):
        x_ref, g_ref, o_ref = refs[0], refs[1], refs[2 + n_w]
        begin, end = _rider_hooks(rider, refs[2:2 + n_w], refs[3 + n_w:3 + 2 * n_w], refs[-2:], pl.program_id(0),
                                  t_dim // tt)
        begin()
        xv = x_ref[...]
        r = lax.rsqrt(jnp.mean(xv * xv, axis=-1, keepdims=True) + RMS_EPS)
        o_ref[...] = ((xv * r) * g_ref[...]).astype(o_ref.dtype)
        end()

    res = pl.pallas_call(
        body, name=name, grid=(t_dim // tt,),
        in_specs=[pl.BlockSpec((tt, d), lambda i: (i, 0)), pl.BlockSpec((1, d), lambda i: (0, 0))] + [ANY] * n_w,
        out_specs=[pl.BlockSpec((tt, d), lambda i: (i, 0))] + [ANY] * n_w,
        out_shape=[jax.ShapeDtypeStruct((t_dim, d), BF16)] + extra_shapes,
        scratch_shapes=extra_sems,
        compiler_params=_params(("arbitrary",) if n_w else ("parallel",)),
    )(x, g, *extra)
    return (res[0], res[1:]) if n_w else res[0]


def _rms_bwd(x, g, dy, add, *, name, tt=512, rider=None):
    t_dim, d = x.shape
    tt = _tile(t_dim, tt, 8)
    has_add = add is not None
    n_in = 4 if has_add else 3
    extra, extra_shapes, extra_sems = _rider_parts(rider)
    n_w = len(extra)

    def body(*refs):
        x_ref, g_ref, dy_ref = refs[:3]
        add_ref = refs[3] if has_add else None
        dx_ref, dg_ref = refs[n_in + n_w:n_in + n_w + 2]
        begin, end = _rider_hooks(rider, refs[n_in:n_in + n_w], refs[n_in + n_w + 2:n_in + 2 * n_w + 2], refs[-2:],
                                  pl.program_id(0), t_dim // tt)
        begin()
        xv = x_ref[...]
        dyv = dy_ref[...].astype(F32)
        r = lax.rsqrt(jnp.mean(xv * xv, axis=-1, keepdims=True) + RMS_EPS)
        xh = xv * r
        u = dyv * g_ref[...]
        dx = r * (u - xh * jnp.mean(u * xh, axis=-1, keepdims=True))
        if has_add:
            dx = add_ref[...] + dx
        dx_ref[...] = dx

        @pl.when(pl.program_id(0) == 0)
        def _():
            dg_ref[...] = jnp.zeros_like(dg_ref)

        dg_ref[...] += jnp.sum(dyv * xh, axis=0, keepdims=True)
        end()

    row = pl.BlockSpec((tt, d), lambda i: (i, 0))
    vec = pl.BlockSpec((1, d), lambda i: (0, 0))
    in_specs = [row, vec, row] + ([row] if has_add else []) + [ANY] * n_w
    args = (x, g, dy) + ((add,) if has_add else ()) + extra
    res = pl.pallas_call(
        body, name=name, grid=(t_dim // tt,),
        in_specs=in_specs, out_specs=[row, vec] + [ANY] * n_w,
        out_shape=[jax.ShapeDtypeStruct((t_dim, d), F32), jax.ShapeDtypeStruct((1, d), F32)] + extra_shapes,
        scratch_shapes=extra_sems,
        compiler_params=_params(("arbitrary",)),
    )(*args)
    return (res[0], res[1], res[2:]) if n_w else (res[0], res[1])


def _down_final(act3, wd3, h, g, target, *, tt=1024):
    n_s, t_dim, f4 = act3.shape
    d = h.shape[1]
    n_steps = t_dim // tt

    def body(a_ref, w_ref, h_ref, g_ref, t_ref, loss_ref, dh_ref, dg_ref, acc_ref, sq_ref):
        i, k = pl.program_id(0), pl.program_id(1)
        part = jnp.dot(a_ref[...], w_ref[...], preferred_element_type=F32)

        @pl.when(k == 0)
        def _():
            acc_ref[...] = part

        @pl.when(k > 0)
        def _():
            acc_ref[...] += part

        @pl.when(jnp.logical_and(i == 0, k == 0))
        def _():
            dg_ref[...] = jnp.zeros_like(dg_ref)
            sq_ref[...] = jnp.zeros_like(sq_ref)

        @pl.when(k == n_s - 1)
        def _():
            xv = h_ref[...] + acc_ref[...]
            gv = g_ref[...]
            r = lax.rsqrt(jnp.mean(xv * xv, axis=-1, keepdims=True) + RMS_EPS)
            xh = xv * r
            err = xh * gv - t_ref[...]
            dyv = err * (1.0 / d)
            u = dyv * gv
            dh_ref[...] = r * (u - xh * jnp.mean(u * xh, axis=-1, keepdims=True))
            dg_ref[...] += jnp.sum(dyv * xh, axis=0, keepdims=True)
            sq_ref[...] += jnp.sum(err * err, axis=0, keepdims=True)

        @pl.when(jnp.logical_and(i == n_steps - 1, k == n_s - 1))
        def _():
            total = jnp.sum(sq_ref[...], axis=-1, keepdims=True) * (0.5 / d)
            loss_ref[...] = jnp.broadcast_to(total, loss_ref.shape)

    row = pl.BlockSpec((tt, d), lambda i, k: (i, 0))
    vec = pl.BlockSpec((1, d), lambda i, k: (0, 0))
    return pl.pallas_call(
        body, name="down_final_loss", grid=(n_steps, n_s),
        in_specs=[pl.BlockSpec((None, tt, f4), lambda i, k: (k, i, 0)),
                  pl.BlockSpec((None, f4, d), lambda i, k: (k, 0, 0)), row, vec, row],
        out_specs=[pl.BlockSpec((1, LANES), lambda i, k: (0, 0)), row, vec],
        out_shape=[jax.ShapeDtypeStruct((1, LANES), F32), jax.ShapeDtypeStruct((t_dim, d), F32),
                   jax.ShapeDtypeStruct((1, d), F32)],
        scratch_shapes=[pltpu.VMEM((tt, d), F32), pltpu.VMEM((1, d), F32)],
        compiler_params=_params(("arbitrary", "arbitrary")),
    )(act3, wd3, h, g, target)


def _rope_table(pos, inv_lane, sel_a, sel_b, *, tt=512):
    t_dim = pos.shape[0]

    def body(p_ref, f_ref, a_ref, b_ref, c_ref, s_ref):
        ang = p_ref[...] * f_ref[...]
        on = (a_ref[...] + b_ref[...]) > 0.0
        c_ref[...] = jnp.where(on, jnp.cos(ang), 1.0)
        s_ref[...] = jnp.where(on, jnp.sin(ang), 0.0)

    vec = pl.BlockSpec((1, LANES), lambda i: (0, 0))
    row = pl.BlockSpec((tt, LANES), lambda i: (i, 0))
    shp = jax.ShapeDtypeStruct((t_dim, LANES), F32)
    return pl.pallas_call(
        body, name="rope_table", grid=(t_dim // tt,),
        in_specs=[pl.BlockSpec((tt, 1), lambda i: (i, 0)), vec, vec, vec],
        out_specs=[row, row], out_shape=[shp, shp],
        compiler_params=_params(("parallel",)),
    )(pos, inv_lane, sel_a, sel_b)


def _rotate(xv, cs, sn, sa, sb):
    half = ROPE_DIM // 2
    up = pltpu.roll(xv, LANES - half, 1)
    dn = pltpu.roll(xv, half, 1)
    return xv * cs + (dn * sb - up * sa) * sn


def _head_masks():
    h1 = lax.broadcasted_iota(jnp.int32, (1, LANES), 1) < HEAD_DIM
    return h1, jnp.logical_not(h1)


def _split_heads(xv, h1, h2):
    return jnp.where(h1, xv, 0.0).astype(BF16), jnp.where(h2, xv, 0.0).astype(BF16)


def _tri_masks():
    r = lax.broadcasted_iota(jnp.int32, (BLOCK, BLOCK), 0)
    c = lax.broadcasted_iota(jnp.int32, (BLOCK, BLOCK), 1)
    return c <= r, r <= c


def _stream_rows(start, dil):
    if dil == 1:
        return pl.ds(pl.multiple_of(start, BLOCK), BLOCK)
    return pl.ds(start, BLOCK, stride=dil)


def _dil_tile(idx, dil, nb):
    r = idx // nb
    n = idx % nb
    return (_stream_rows(r + dil * BLOCK * n, dil), _stream_rows(r + dil * BLOCK * jnp.maximum(n - 1, 0), dil),
            n > 0)


def _dil_specs(b_dim, s_dim):
    def col(c0):
        return pl.BlockSpec((None, s_dim, LANES),lambda b, h: (b, 0, c0 + h))
    tab = pl.BlockSpec((None, s_dim, LANES),lambda b, h: (b, 0, 0))
    vec = pl.BlockSpec((1, LANES), lambda b, h: (0, 0))
    return col, tab, vec


def _dil_fwd(proj3, cs3, sn3, sel_a, sel_b, rider=None):
    b_dim, s_dim, _ = proj3.shape
    scale = HEAD_DIM ** -0.5
    n_pat = len(DIL_PATTERNS)
    extra, extra_shapes, extra_sems = _rider_parts(rider)
    n_w = len(extra)
    n_steps = b_dim * PAIRS

    def body(*refs):
        q_ref, k_ref, v_ref, cs_ref, sn_ref, sa_ref, sb_ref = refs[:7]
        o16_ref, o32_ref, l_ref = refs[7 + n_w:10 + n_w]
        qr, kr = refs[10 + 2 * n_w:12 + 2 * n_w]
        per_pattern = refs[12 + 2 * n_w:12 + 2 * n_w + 2 * n_pat]
        og, lg = per_pattern[:n_pat], per_pattern[n_pat:]
        step = pl.program_id(0) * PAIRS + pl.program_id(1)
        begin, end = _rider_hooks(rider, refs[7:7 + n_w], refs[10 + n_w:10 + 2 * n_w], refs[-2:], step, n_steps)
        begin()
        h1, h2 = _head_masks()
        cur_ok, prev_ok = _tri_masks()
        sa, sb = sa_ref[...], sb_ref[...]

        def prep(j, _):
            rows = pl.ds(pl.multiple_of(j * DIL_CHUNK, DIL_CHUNK), DIL_CHUNK)
            cs, sn = cs_ref[rows, :], sn_ref[rows, :]
            qr[rows, :] = _rotate(q_ref[rows, :], cs, sn, sa, sb) * scale
            kr[rows, :] = _rotate(k_ref[rows, :], cs, sn, sa, sb)
            return 0

        lax.fori_loop(0, s_dim // DIL_CHUNK, prep, 0)

        for g, (_, dil) in enumerate(DIL_PATTERNS):
            nb = s_dim // dil // BLOCK

            def some(bi, _, g=g, dil=dil, nb=nb):
                tiles = [_dil_tile(bi * DIL_BATCH + t, dil, nb) for t in range(DIL_BATCH)]
                rows = [t[0] for t in tiles]
                q1, q2 = _split_heads(jnp.stack([qr[rw, :] for rw in rows]), h1, h2)
                kc = jnp.stack([kr[rw, :] for rw in rows]).astype(BF16)
                vc1, vc2 = _split_heads(jnp.stack([v_ref[rw, :] for rw in rows]), h1, h2)
                if nb > 1:
                    kp = jnp.stack([kr[t[1], :] for t in tiles]).astype(BF16)
                    vp1, vp2 = _split_heads(jnp.stack([v_ref[t[1], :] for t in tiles]), h1, h2)
                    p_ok = jnp.stack([jnp.logical_and(prev_ok, t[2]) for t in tiles])

                def head(qh, vch, vph):
                    sc = jnp.where(cur_ok, lax.dot_general(qh, kc, BNT, preferred_element_type=F32), -jnp.inf)
                    m = jnp.max(sc, axis=-1, keepdims=True)
                    if nb > 1:
                        sp = jnp.where(p_ok, lax.dot_general(qh, kp, BNT, preferred_element_type=F32), -jnp.inf)
                        m = jnp.maximum(m, jnp.max(sp, axis=-1, keepdims=True))
                    pc = jnp.exp(sc - m)
                    den = jnp.sum(pc, axis=-1, keepdims=True)
                    acc = lax.dot_general(pc.astype(BF16), vch, BNN, preferred_element_type=F32)
                    if nb > 1:
                        pp = jnp.exp(sp - m)
                        den = den + jnp.sum(pp, axis=-1, keepdims=True)
                        acc = acc + lax.dot_general(pp.astype(BF16), vph, BNN, preferred_element_type=F32)
                    return acc / den, m + jnp.log(den)

                o1, l1 = head(q1, vc1, vp1 if nb > 1 else None)
                o2, l2 = head(q2, vc2, vp2 if nb > 1 else None)
                o, l = o1 + o2, jnp.where(h1, l1, l2)
                for t, rw in enumerate(rows):
                    og[g][rw, :] = o[t]
                    lg[g][rw, :] = l[t]
                return 0

            lax.fori_loop(0, dil * nb // DIL_BATCH, some, 0)

        def comb(j, _):
            rows = pl.ds(pl.multiple_of(j * DIL_CHUNK, DIL_CHUNK), DIL_CHUNK)
            ls = [lg[g][rows, :] for g in range(n_pat)]
            m = jnp.maximum(jnp.maximum(ls[0], ls[1]), ls[2])
            es = [jnp.exp(l - m) for l in ls]
            den = es[0] + es[1] + es[2]
            o = (es[0] * og[0][rows, :] + es[1] * og[1][rows, :] + es[2] * og[2][rows, :]) / den
            o16_ref[rows, :] = o.astype(BF16)
            o32_ref[rows, :] = o
            l_ref[rows, :] = m + jnp.log(den)
            return 0

        lax.fori_loop(0, s_dim // DIL_CHUNK, comb, 0)
        end()

    col, tab, vec = _dil_specs(b_dim, s_dim)
    out = pl.BlockSpec((None, s_dim, LANES),lambda b, h: (b, 0, h))
    shp = (b_dim, s_dim, ATT_WIDTH)
    res = pl.pallas_call(
        body, name="dil_fwd", grid=(b_dim, PAIRS),
        in_specs=[col(COL_QA), col(COL_KA), col(COL_VA), tab, tab, vec, vec] + [ANY] * n_w,
        out_specs=[out, out, out] + [ANY] * n_w,
        out_shape=[jax.ShapeDtypeStruct(shp, BF16), jax.ShapeDtypeStruct(shp, F32), jax.ShapeDtypeStruct(shp, F32)]
        + extra_shapes,
        scratch_shapes=[pltpu.VMEM((s_dim, LANES), F32)] * (2 + 2 * n_pat) + extra_sems,
        compiler_params=_params(("arbitrary", "arbitrary")),
    )(proj3, proj3, proj3, cs3, sn3, sel_a, sel_b, *extra)
    return res[:3], res[3:]


def _dil_bwd(proj3, cs3, sn3, sel_a, sel_b, do3, o3, lse3, rider=None):
    b_dim, s_dim, _ = proj3.shape
    scale = HEAD_DIM ** -0.5
    extra, extra_shapes, extra_sems = _rider_parts(rider)
    n_w = len(extra)

    def body(*refs):
        q_ref, k_ref, v_ref, cs_ref, sn_ref, sa_ref, sb_ref, do_ref, o_ref, l_ref = refs[:10]
        dq_ref, dk_ref, dv_ref = refs[10 + n_w:13 + n_w]
        qr, kr, dqa, dka, dva = refs[13 + 2 * n_w:18 + 2 * n_w]
        step = pl.program_id(0) * PAIRS + pl.program_id(1)
        begin, end = _rider_hooks(rider, refs[10:10 + n_w], refs[13 + n_w:13 + 2 * n_w], refs[-2:], step,
                                  b_dim * PAIRS)
        begin()
        h1, h2 = _head_masks()
        cur_ok, prev_ok = _tri_masks()
        sa, sb = sa_ref[...], sb_ref[...]

        def prep(j, _):
            rows = pl.ds(pl.multiple_of(j * DIL_CHUNK, DIL_CHUNK), DIL_CHUNK)
            cs, sn = cs_ref[rows, :], sn_ref[rows, :]
            qr[rows, :] = _rotate(q_ref[rows, :], cs, sn, sa, sb) * scale
            kr[rows, :] = _rotate(k_ref[rows, :], cs, sn, sa, sb)
            zero = jnp.zeros((DIL_CHUNK, LANES), F32)
            dqa[rows, :] = zero
            dka[rows, :] = zero
            dva[rows, :] = zero
            return 0

        lax.fori_loop(0, s_dim // DIL_CHUNK, prep, 0)

        for _, dil in DIL_PATTERNS:
            nb = s_dim // dil // BLOCK

            def some(bi, _, dil=dil, nb=nb):
                tiles = [_dil_tile(bi * DIL_BATCH + t, dil, nb) for t in range(DIL_BATCH)]
                rows = [t[0] for t in tiles]
                q1, q2 = _split_heads(jnp.stack([qr[rw, :] for rw in rows]), h1, h2)
                dof = jnp.stack([do_ref[rw, :] for rw in rows])
                do1, do2 = _split_heads(dof, h1, h2)
                prod = dof * jnp.stack([o_ref[rw, :] for rw in rows])
                delta1 = jnp.sum(jnp.where(h1, prod, 0.0), axis=-1, keepdims=True)
                delta2 = jnp.sum(jnp.where(h2, prod, 0.0), axis=-1, keepdims=True)
                lt = jnp.stack([l_ref[rw, :] for rw in rows])
                lse1 = jnp.max(jnp.where(h1, lt, -jnp.inf), axis=-1, keepdims=True)
                lse2 = jnp.max(jnp.where(h2, lt, -jnp.inf), axis=-1, keepdims=True)

                def side(krows, ok):
                    kf = jnp.stack([kr[kw, :] for kw in krows])
                    k16 = kf.astype(BF16)
                    k1, k2 = _split_heads(kf, h1, h2)
                    v16 = jnp.stack([v_ref[kw, :] for kw in krows]).astype(BF16)

                    def head(qh, doh, lse, delta):
                        sc = lax.dot_general(qh, k16, BNT, preferred_element_type=F32)
                        p = jnp.where(ok, jnp.exp(sc - lse), 0.0)
                        dp = lax.dot_general(doh, v16, BNT, preferred_element_type=F32)
                        return p.astype(BF16), (p * (dp - delta)).astype(BF16)

                    p1, ds1 = head(q1, do1, lse1, delta1)
                    p2, ds2 = head(q2, do2, lse2, delta2)
                    dv = (lax.dot_general(p1, do1, BTN, preferred_element_type=F32)
                          + lax.dot_general(p2, do2, BTN, preferred_element_type=F32))
                    dk = (lax.dot_general(ds1, q1, BTN, preferred_element_type=F32)
                          + lax.dot_general(ds2, q2, BTN, preferred_element_type=F32))
                    for t, kw in enumerate(krows):
                        dva[kw, :] += dv[t]
                        dka[kw, :] += dk[t]
                    return (lax.dot_general(ds1, k1, BNN, preferred_element_type=F32)
                            + lax.dot_general(ds2, k2, BNN, preferred_element_type=F32))

                dq = side(rows, cur_ok)
                if nb > 1:
                    dq = dq + side([t[1] for t in tiles], jnp.stack([jnp.logical_and(prev_ok, t[2]) for t in tiles]))
                for t, rw in enumerate(rows):
                    dqa[rw, :] += dq[t] * scale
                return 0

            lax.fori_loop(0, dil * nb // DIL_BATCH, some, 0)

        def finish(j, _):
            rows = pl.ds(pl.multiple_of(j * DIL_CHUNK, DIL_CHUNK), DIL_CHUNK)
            cs, sn = cs_ref[rows, :], -sn_ref[rows, :]
            dq_ref[rows, :] = _rotate(dqa[rows, :], cs, sn, sa, sb).astype(BF16)
            dk_ref[rows, :] = _rotate(dka[rows, :], cs, sn, sa, sb).astype(BF16)
            dv_ref[rows, :] = dva[rows, :].astype(BF16)
            return 0

        lax.fori_loop(0, s_dim // DIL_CHUNK, finish, 0)
        end()

    col, tab, vec = _dil_specs(b_dim, s_dim)
    out = pl.BlockSpec((None, s_dim, LANES),lambda b, h: (b, 0, h))
    shp = jax.ShapeDtypeStruct((b_dim, s_dim, ATT_WIDTH), BF16)
    acc = pltpu.VMEM((s_dim, LANES), F32)
    res = pl.pallas_call(
        body, name="dil_bwd", grid=(b_dim, PAIRS),
        in_specs=[col(COL_QA), col(COL_KA), col(COL_VA), tab, tab, vec, vec, out, out, out] + [ANY] * n_w,
        out_specs=[out, out, out] + [ANY] * n_w, out_shape=[shp, shp, shp] + extra_shapes,
        scratch_shapes=[acc, acc, acc, acc, acc] + extra_sems,
        compiler_params=_params(("arbitrary", "arbitrary")),
    )(proj3, proj3, proj3, cs3, sn3, sel_a, sel_b, do3, o3, lse3, *extra)
    return res[:3], res[3:]


def _split_dot(x, tri):
    hi = x.astype(BF16)
    lo = (x - hi.astype(F32)).astype(BF16)
    return jnp.dot(hi, tri, preferred_element_type=F32) + jnp.dot(lo, tri, preferred_element_type=F32)


def _log_sigmoid(z):
    return jnp.minimum(z, 0.0) - jnp.log(1.0 + jnp.exp(-jnp.abs(z)))


def _sb_scores(qh, k16, valid):
    z = lax.dot_general(qh, k16, NT, preferred_element_type=F32)
    ls = _log_sigmoid(z)
    l1m = ls - z
    return ls, (l1m if valid is None else jnp.where(valid, l1m, 0.0))


def _sb_consts():
    r = lax.broadcasted_iota(jnp.int32, (BLOCK, BLOCK), 0)
    c = lax.broadcasted_iota(jnp.int32, (BLOCK, BLOCK), 1)
    after = (r > c).astype(BF16)
    before = (r < c).astype(BF16)
    qrow = lax.broadcasted_iota(jnp.int32, (SB_ROWS, BLOCK), 0)
    kcol = lax.broadcasted_iota(jnp.int32, (SB_ROWS, BLOCK), 1)
    return after, before, qrow, kcol


def _below(whole, lo, delta):
    if lo == 0:
        return whole + delta
    return whole + jnp.concatenate([jnp.zeros((lo,) + delta.shape[1:], delta.dtype), delta], axis=0)


def _pairs_loop(n_blocks, step, carry):
    def several(i, c):
        for j in range(SB_STEP):
            c = step(SB_STEP * i + j, c)
        return c

    return lax.fori_loop(0, n_blocks // SB_STEP, several, carry)


def _sb_fwd(proj3, rider=None):
    b_dim, s_dim, _ = proj3.shape
    scale = HEAD_DIM ** -0.5
    per = SB_ROWS // BLOCK
    extra, extra_shapes, extra_sems = _rider_parts(rider)
    n_w = len(extra)

    def body(*refs):
        q_ref, k_ref, v_ref = refs[:3]
        o_ref = refs[3 + n_w]
        step = pl.program_id(0) * PAIRS + pl.program_id(1)
        begin, end = _rider_hooks(rider, refs[3:3 + n_w], refs[4 + n_w:4 + 2 * n_w], refs[-2:], step, b_dim * PAIRS)
        begin()
        h1, h2 = _head_masks()
        after, _, qrow, kcol = _sb_consts()

        def qloop(qi, _):
            rows = pl.ds(pl.multiple_of(qi * SB_ROWS, SB_ROWS), SB_ROWS)
            q1, q2 = _split_heads(q_ref[rows, :] * scale, h1, h2)
            first = qi * per

            def block(kb, carry, lo):
                acc, run1, run2 = carry
                krows = pl.ds(pl.multiple_of(kb * BLOCK, BLOCK), BLOCK)
                k16 = k_ref[krows, :].astype(BF16)
                v1, v2 = _split_heads(v_ref[krows, :], h1, h2)
                valid = None if lo is None else kcol[:SB_ROWS - lo] < qrow[:SB_ROWS - lo]
                lo = lo or 0

                def head(qh, vh, run):
                    ls, l1m = _sb_scores(qh[lo:], k16, valid)
                    a = jnp.exp(ls + _split_dot(l1m, after) + run[lo:])
                    if valid is not None:
                        a = jnp.where(valid, a, 0.0)
                    return (jnp.dot(a.astype(BF16), vh, preferred_element_type=F32),
                            _below(run, lo, jnp.sum(l1m, axis=-1, keepdims=True)))

                o1, run1 = head(q1, v1, run1)
                o2, run2 = head(q2, v2, run2)
                return _below(acc, lo, o1 + o2), run1, run2

            zcol = jnp.zeros((SB_ROWS, 1), F32)
            carry = (jnp.zeros((SB_ROWS, LANES), F32), zcol, zcol)
            for kl in reversed(range(per)):
                carry = block(first + kl, carry, kl * BLOCK)
            acc, _, _ = _pairs_loop(first, lambda i, c: block(first - 1 - i, c, None), carry)
            o_ref[rows, :] = acc.astype(BF16)
            return 0

        lax.fori_loop(0, s_dim // SB_ROWS, qloop, 0)
        end()

    def col(c0):
        return pl.BlockSpec((None, s_dim, LANES),lambda b, h: (b, 0, c0 + h))

    res = pl.pallas_call(
        body, name="sb_fwd", grid=(b_dim, PAIRS),
        in_specs=[col(COL_QB), col(COL_KB), col(COL_VB)] + [ANY] * n_w, out_specs=[col(0)] + [ANY] * n_w,
        out_shape=[jax.ShapeDtypeStruct((b_dim, s_dim, ATT_WIDTH), BF16)] + extra_shapes,
        scratch_shapes=extra_sems,
        compiler_params=_params(("arbitrary", "arbitrary")),
    )(proj3, proj3, proj3, *extra)
    return res[0], res[1:]


def _sb_bwd(proj3, do3, rider=None):
    b_dim, s_dim, _ = proj3.shape
    scale = HEAD_DIM ** -0.5
    per = SB_ROWS // BLOCK
    nkb_max = s_dim // BLOCK
    extra, extra_shapes, extra_sems = _rider_parts(rider)
    n_w = len(extra)

    def body(*refs):
        q_ref, k_ref, v_ref, do_ref = refs[:4]
        dq_ref, dk_ref, dv_ref = refs[4 + n_w:7 + n_w]
        dka, dva, e_ref, sg_ref = refs[7 + 2 * n_w:11 + 2 * n_w]
        step = pl.program_id(0) * PAIRS + pl.program_id(1)
        begin, end = _rider_hooks(rider, refs[4:4 + n_w], refs[7 + n_w:7 + 2 * n_w], refs[-2:], step, b_dim * PAIRS)
        begin()
        h1, h2 = _head_masks()
        after, before, qrow, kcol = _sb_consts()
        dka[...] = jnp.zeros_like(dka)
        dva[...] = jnp.zeros_like(dva)

        def qloop(qi, _):
            rows = pl.ds(pl.multiple_of(qi * SB_ROWS, SB_ROWS), SB_ROWS)
            q1, q2 = _split_heads(q_ref[rows, :] * scale, h1, h2)
            do1, do2 = _split_heads(do_ref[rows, :].astype(F32), h1, h2)
            first = qi * per

            def pass1(kb, carry, lo):
                run1, run2 = carry
                krows = pl.ds(pl.multiple_of(kb * BLOCK, BLOCK), BLOCK)
                k16 = k_ref[krows, :].astype(BF16)
                v16 = v_ref[krows, :].astype(BF16)
                valid = None if lo is None else kcol[:SB_ROWS - lo] < qrow[:SB_ROWS - lo]
                lo = lo or 0
                part = pl.ds(lo, SB_ROWS - lo)

                def head(h, qh, doh, run):
                    ls, l1m = _sb_scores(qh[lo:], k16, valid)
                    a = jnp.exp(ls + _split_dot(l1m, after) + run[lo:])
                    if valid is not None:
                        a = jnp.where(valid, a, 0.0)
                    da = lax.dot_general(doh[lo:], v16, NT, preferred_element_type=F32)
                    e_ref[h, kb, part, :] = a * da
                    sg_ref[h, kb, part, :] = jnp.exp(ls)
                    return a.astype(BF16), _below(run, lo, jnp.sum(l1m, axis=-1, keepdims=True))

                a1, run1 = head(0, q1, do1, run1)
                a2, run2 = head(1, q2, do2, run2)
                dva[krows, :] += (lax.dot_general(a1, do1[lo:], TN, preferred_element_type=F32)
                                  + lax.dot_general(a2, do2[lo:], TN, preferred_element_type=F32))
                return run1, run2

            zcol = jnp.zeros((SB_ROWS, 1), F32)
            carry = (zcol, zcol)
            for kl in reversed(range(per)):
                carry = pass1(first + kl, carry, kl * BLOCK)
            _pairs_loop(first, lambda i, c: pass1(first - 1 - i, c, None), carry)

            def pass2(kb, carry, lo):
                dq, pre1, pre2 = carry
                krows = pl.ds(pl.multiple_of(kb * BLOCK, BLOCK), BLOCK)
                k1, k2 = _split_heads(k_ref[krows, :], h1, h2)
                valid = None if lo is None else kcol[:SB_ROWS - lo] < qrow[:SB_ROWS - lo]
                lo = lo or 0
                part = pl.ds(lo, SB_ROWS - lo)

                def head(h, pre):
                    ev = e_ref[h, kb, part, :]
                    sg = sg_ref[h, kb, part, :]
                    dz = ev * (1.0 - sg) - (_split_dot(ev, before) + pre[lo:]) * sg
                    if valid is not None:
                        dz = jnp.where(valid, dz, 0.0)
                    return dz.astype(BF16), _below(pre, lo, jnp.sum(ev, axis=-1, keepdims=True))

                dz1, pre1 = head(0, pre1)
                dz2, pre2 = head(1, pre2)
                dka[krows, :] += (lax.dot_general(dz1, q1[lo:], TN, preferred_element_type=F32)
                                  + lax.dot_general(dz2, q2[lo:], TN, preferred_element_type=F32))
                dq = _below(dq, lo, jnp.dot(dz1, k1, preferred_element_type=F32)
                            + jnp.dot(dz2, k2, preferred_element_type=F32))
                return dq, pre1, pre2

            carry = _pairs_loop(first, lambda i, c: pass2(i, c, None), (jnp.zeros((SB_ROWS, LANES), F32), zcol, zcol))
            for kl in range(per):
                carry = pass2(first + kl, carry, kl * BLOCK)
            dq = carry[0]
            dq_ref[rows, :] = (dq * scale).astype(BF16)
            return 0

        lax.fori_loop(0, s_dim // SB_ROWS, qloop, 0)
        dk_ref[...] = dka[...].astype(BF16)
        dv_ref[...] = dva[...].astype(BF16)
        end()

    def col(c0):
        return pl.BlockSpec((None, s_dim, LANES),lambda b, h: (b, 0, c0 + h))

    shp = jax.ShapeDtypeStruct((b_dim, s_dim, ATT_WIDTH), BF16)
    acc = pltpu.VMEM((s_dim, LANES), F32)
    strip = pltpu.VMEM((2, nkb_max, SB_ROWS, BLOCK), F32)
    res = pl.pallas_call(
        body, name="sb_bwd", grid=(b_dim, PAIRS),
        in_specs=[col(COL_QB), col(COL_KB), col(COL_VB), col(0)] + [ANY] * n_w,
        out_specs=[col(0), col(0), col(0)] + [ANY] * n_w,
        out_shape=[shp, shp, shp] + extra_shapes,
        scratch_shapes=[acc, acc, strip, strip] + extra_sems,
        compiler_params=_params(("arbitrary", "arbitrary")),
    )(proj3, proj3, proj3, do3, *extra)
    return res[:3], res[3:]


def _sigmoid(x):
    return 1.0 / (1.0 + jnp.exp(-x))


def _gate_out_norm(proj, ua, ub, w_out, x, g, *, tt=512):
    t_dim, d = ua.shape

    def body(ga_ref, gb_ref, ua_ref, ub_ref, w_ref, x_ref, g_ref, m_ref, h_ref, n_ref):
        mixed = (_sigmoid(ga_ref[...]) * ua_ref[...] + _sigmoid(gb_ref[...]) * ub_ref[...]).astype(BF16)
        m_ref[...] = mixed
        hv = x_ref[...] + jnp.dot(mixed, w_ref[...], preferred_element_type=F32)
        h_ref[...] = hv
        r = lax.rsqrt(jnp.mean(hv * hv, axis=-1, keepdims=True) + RMS_EPS)
        n_ref[...] = ((hv * r) * g_ref[...]).astype(BF16)

    row = pl.BlockSpec((tt, d), lambda i: (i, 0))
    return pl.pallas_call(
        body, name="gate_out_norm", grid=(t_dim // tt,),
        in_specs=[pl.BlockSpec((tt, d), lambda i: (i, 3)), pl.BlockSpec((tt, d), lambda i: (i, 4)), row, row,
                  pl.BlockSpec((d, d), lambda i: (0, 0)), row, pl.BlockSpec((1, d), lambda i: (0, 0))],
        out_specs=[row, row, row],
        out_shape=[jax.ShapeDtypeStruct((t_dim, d), BF16), jax.ShapeDtypeStruct((t_dim, d), F32),
                   jax.ShapeDtypeStruct((t_dim, d), BF16)],
        compiler_params=_params(("parallel",)),
    )(proj, proj, ua, ub, w_out, x, g)


def _out_dx_gate_bwd(dh, w_out, proj, ua, ub, *, tt=512):
    t_dim, d = ua.shape

    def body(dh_ref, w_ref, ga_ref, gb_ref, ua_ref, ub_ref, dua_ref, dub_ref, dg_ref):
        dm = lax.dot_general(dh_ref[...].astype(BF16), w_ref[...], NT, preferred_element_type=F32)
        sa = _sigmoid(ga_ref[...])
        sb = _sigmoid(gb_ref[...])
        dua_ref[...] = (dm * sa).astype(BF16)
        dub_ref[...] = (dm * sb).astype(BF16)
        dg_ref[:, :d] = (dm * ua_ref[...] * (sa * (1.0 - sa))).astype(BF16)
        dg_ref[:, d:] = (dm * ub_ref[...] * (sb * (1.0 - sb))).astype(BF16)

    row = pl.BlockSpec((tt, d), lambda i: (i, 0))
    wide = pl.BlockSpec((tt, 2 * d), lambda i: (i, 0))
    return pl.pallas_call(
        body, name="out_dx_gate_bwd", grid=(t_dim // tt,),
        in_specs=[row, pl.BlockSpec((d, d), lambda i: (0, 0)),
                  pl.BlockSpec((tt, d), lambda i: (i, 3)), pl.BlockSpec((tt, d), lambda i: (i, 4)), row, row],
        out_specs=[row, row, wide],
        out_shape=[jax.ShapeDtypeStruct((t_dim, d), BF16), jax.ShapeDtypeStruct((t_dim, d), BF16),
                   jax.ShapeDtypeStruct((t_dim, 2 * d), BF16)],
        compiler_params=_params(("parallel",)),
    )(dh, w_out, proj, proj, ua, ub)


def _ffn_up_swiglu(n, wg3, wu3, *, tt=1024):
    t_dim, d = n.shape
    n_s, f4, _ = wg3.shape

    def body(n_ref, wg_ref, wu_ref, g_ref, u_ref, a_ref):
        nv = n_ref[...]
        gv = lax.dot_general(nv, wg_ref[...], NT, preferred_element_type=F32)
        uv = lax.dot_general(nv, wu_ref[...], NT, preferred_element_type=F32)
        g_ref[...] = gv.astype(BF16)
        u_ref[...] = uv.astype(BF16)
        a_ref[...] = (gv * _sigmoid(gv) * uv).astype(BF16)

    wspec = pl.BlockSpec((None, f4, d), lambda i, s: (s, 0, 0))
    ospec = pl.BlockSpec((None, tt, f4), lambda i, s: (s, i, 0))
    shp = (n_s, t_dim, f4)
    return pl.pallas_call(
        body, name="ffn_up_swiglu", grid=(t_dim // tt, n_s),
        in_specs=[pl.BlockSpec((tt, d), lambda i, s: (i, 0)), wspec, wspec], out_specs=[ospec, ospec, ospec],
        out_shape=[jax.ShapeDtypeStruct(shp, BF16)] * 3,
        compiler_params=_params(("parallel", "parallel")),
    )(n, wg3, wu3)


def _ffn_down_dx_swiglu(dh, wd3, g3, u3, *, tt=1024):
    t_dim, d = dh.shape
    n_s, f4, _ = wd3.shape

    def body(dh_ref, w_ref, g_ref, u_ref, dg_ref, du_ref):
        da = lax.dot_general(dh_ref[...].astype(BF16), w_ref[...], NT, preferred_element_type=F32)
        gv = g_ref[...].astype(F32)
        sg = _sigmoid(gv)
        dg_ref[...] = (da * u_ref[...].astype(F32) * (sg + gv * sg * (1.0 - sg))).astype(BF16)
        du_ref[...] = (da * (gv * sg)).astype(BF16)

    spec = pl.BlockSpec((None, tt, f4), lambda i, s: (s, i, 0))
    shp = jax.ShapeDtypeStruct((n_s, t_dim, f4), BF16)
    return pl.pallas_call(
        body, name="ffn_down_dx_swiglu", grid=(t_dim // tt, n_s),
        in_specs=[pl.BlockSpec((tt, d), lambda i, s: (i, 0)), pl.BlockSpec((None, f4, d), lambda i, s: (s, 0, 0)),
                  spec, spec],
        out_specs=[spec, spec], out_shape=[shp, shp],
        compiler_params=_params(("parallel", "parallel")),
    )(dh, wd3, g3, u3)


def _mem_fwd(qm, kvm, *, tt=2048):
    b_dim, s_dim, _ = qm.shape
    n_mem = kvm.shape[1]
    scale = MEM_HEAD_DIM ** -0.5

    def body(q_ref, k_ref, v_ref, o_ref):
        sc = lax.dot_general(q_ref[0], k_ref[0], NT, preferred_element_type=F32) * scale
        p = jnp.exp(sc - jnp.max(sc, axis=-1, keepdims=True))
        p = p / jnp.sum(p, axis=-1, keepdims=True)
        o_ref[0] = jnp.dot(p.astype(BF16), v_ref[0], preferred_element_type=F32).astype(BF16)

    qs = pl.BlockSpec((1, tt, MEM_HEAD_DIM), lambda b, h, i: (b, i, h))
    return pl.pallas_call(
        body, name="mem_fwd", grid=(b_dim, N_HEADS_MEM, s_dim // tt),
        in_specs=[qs, pl.BlockSpec((1, n_mem, MEM_HEAD_DIM), lambda b, h, i: (b, 0, h)),
                  pl.BlockSpec((1, n_mem, MEM_HEAD_DIM), lambda b, h, i: (b, 0, N_HEADS_MEM + h))],
        out_specs=qs, out_shape=jax.ShapeDtypeStruct(qm.shape, BF16),
        compiler_params=_params(("parallel", "parallel", "parallel")),
    )(qm, kvm, kvm)


def _mem_bwd(qm, kvm, dom, *, tt=2048):
    b_dim, s_dim, _ = qm.shape
    n_mem = kvm.shape[1]
    scale = MEM_HEAD_DIM ** -0.5

    def body(q_ref, k_ref, v_ref, do_ref, dq_ref, dk_ref, dv_ref):
        qv, kv, vv, dov = q_ref[0], k_ref[0], v_ref[0], do_ref[0]
        sc = lax.dot_general(qv, kv, NT, preferred_element_type=F32) * scale
        p = jnp.exp(sc - jnp.max(sc, axis=-1, keepdims=True))
        p = p / jnp.sum(p, axis=-1, keepdims=True)
        dp = lax.dot_general(dov, vv, NT, preferred_element_type=F32)
        ds = (p * (dp - jnp.sum(p * dp, axis=-1, keepdims=True)) * scale).astype(BF16)
        dq_ref[0] = jnp.dot(ds, kv, preferred_element_type=F32).astype(BF16)

        @pl.when(pl.program_id(2) == 0)
        def _():
            dk_ref[...] = jnp.zeros_like(dk_ref)
            dv_ref[...] = jnp.zeros_like(dv_ref)

        dk_ref[0] += lax.dot_general(ds, qv, TN, preferred_element_type=F32)
        dv_ref[0] += lax.dot_general(p.astype(BF16), dov, TN, preferred_element_type=F32)

    qs = pl.BlockSpec((1, tt, MEM_HEAD_DIM), lambda b, h, i: (b, i, h))
    ks = pl.BlockSpec((1, n_mem, MEM_HEAD_DIM), lambda b, h, i: (b, 0, h))
    vs = pl.BlockSpec((1, n_mem, MEM_HEAD_DIM), lambda b, h, i: (b, 0, N_HEADS_MEM + h))
    return pl.pallas_call(
        body, name="mem_bwd", grid=(b_dim, N_HEADS_MEM, s_dim // tt),
        in_specs=[qs, ks, vs, qs], out_specs=[qs, ks, ks],
        out_shape=[jax.ShapeDtypeStruct(qm.shape, BF16), jax.ShapeDtypeStruct((b_dim, n_mem, MEM_WIDTH), F32),
                   jax.ShapeDtypeStruct((b_dim, n_mem, MEM_WIDTH), F32)],
        compiler_params=_params(("parallel", "parallel", "arbitrary")),
    )(qm, kvm, kvm, dom)


def _adamw_math(wv, gv, mv, vv):
    nm = ADAM_B1 * mv + (1.0 - ADAM_B1) * gv
    nv = ADAM_B2 * vv + (1.0 - ADAM_B2) * (gv * gv)
    m_hat = nm / (1.0 - ADAM_B1 ** ADAM_STEP)
    v_hat = nv / (1.0 - ADAM_B2 ** ADAM_STEP)
    return -ADAM_LR * (m_hat / (jnp.sqrt(v_hat) + ADAM_EPS) + ADAM_WD * wv), nm, nv


def _adamw(w, g, m, v, *, name):
    rows, cols = w.shape
    tr = _tile(rows, 256, 8)

    def body(w_ref, g_ref, m_ref, v_ref, d_ref, nm_ref, nv_ref):
        d_ref[...], nm_ref[...], nv_ref[...] = _adamw_math(w_ref[...], g_ref[...], m_ref[...], v_ref[...])

    spec = pl.BlockSpec((tr, cols), lambda i: (i, 0))
    shp = jax.ShapeDtypeStruct((rows, cols), F32)
    return pl.pallas_call(
        body, name=name, grid=(rows // tr,),
        in_specs=[spec] * 4, out_specs=[spec] * 3, out_shape=[shp] * 3,
        compiler_params=_params(("parallel",)),
    )(w, g, m, v)


def _prefetch_spec(grid, in_specs, out_specs):
    return pltpu.PrefetchScalarGridSpec(num_scalar_prefetch=1, grid=grid, in_specs=in_specs, out_specs=out_specs)


def _adamw_halves(w, mine, theirs, m, v, c_idx, *, name):
    rows, cols = w.shape
    half = rows // 2
    tr = _tile(half, _row_cap(cols), 8)
    nh = half // tr

    def body(c_ref, w_ref, mine_ref, theirs_ref, m_ref, v_ref, g_ref, d_ref, nm_ref, nv_ref):
        gv = jnp.where(pl.program_id(0) == c_ref[0], mine_ref[...], theirs_ref[...])
        g_ref[...] = gv
        d_ref[...], nm_ref[...], nv_ref[...] = _adamw_math(w_ref[...], gv, m_ref[...], v_ref[...])

    full = pl.BlockSpec((tr, cols), lambda h, i, c_ref: (h * nh + i, 0))
    part = pl.BlockSpec((tr, cols), lambda h, i, c_ref: (i, 0))
    shp = jax.ShapeDtypeStruct((rows, cols), F32)
    return pl.pallas_call(
        body, name=name, grid_spec=_prefetch_spec((2, nh), [full, part, part, full, full], [full] * 4),
        out_shape=[shp] * 4,
        compiler_params=_params(("parallel", "parallel")),
    )(c_idx, w, mine, theirs, m, v)


def _pair_sum(g3, theirs, c_idx, *, name):
    n, rows, cols = g3.shape
    half = rows // 2
    tr = _tile(half, _row_cap(cols), 16)

    def body(c_ref, g_ref, t_ref, o_ref):
        o_ref[...] = (g_ref[...] + t_ref[...]).astype(BF16)

    part = pl.BlockSpec((None, tr, cols), lambda s, i, c_ref: (s, i, 0))
    return pl.pallas_call(
        body, name=name,
        grid_spec=_prefetch_spec((n, half // tr),
                                 [pl.BlockSpec((None, None, tr, cols), lambda s, i, c_ref: (s, c_ref[0], i, 0)), part],
                                 part),
        out_shape=jax.ShapeDtypeStruct((n, half, cols), BF16),
        compiler_params=_params(("parallel", "parallel")),
    )(c_idx, g3.reshape(n, 2, half, cols), theirs)


def _chip_sum(pair, recv, s_idx, *, name):
    _, half, cols = pair.shape
    tr = _tile(half, _row_cap(cols), 16)

    def body(s_ref, p_ref, r_ref, o_ref):
        o_ref[...] = ((p_ref[...].astype(F32) + r_ref[0].astype(F32)) + r_ref[1].astype(F32)) + r_ref[2].astype(F32)

    return pl.pallas_call(
        body, name=name,
        grid_spec=_prefetch_spec((half // tr,),
                                 [pl.BlockSpec((None, tr, cols), lambda i, s_ref: (s_ref[0], i, 0)),
                                  pl.BlockSpec((N_CHIPS - 1, tr, cols), lambda i, s_ref: (0, i, 0))],
                                 pl.BlockSpec((tr, cols), lambda i, s_ref: (i, 0))),
        out_shape=jax.ShapeDtypeStruct((half, cols), F32),
        compiler_params=_params(("parallel",)),
    )(s_idx, pair, recv)


def _sum8(parts):
    n, rows, cols = parts.shape

    def body(p_ref, o_ref):
        acc = p_ref[0]
        for i in range(1, n):
            acc = acc + p_ref[i]
        o_ref[...] = acc

    return pl.pallas_call(
        body, name="small_sum", grid=(1,),
        in_specs=[pl.BlockSpec((n, rows, cols), lambda i: (0, 0, 0))],
        out_specs=pl.BlockSpec((rows, cols), lambda i: (0, 0)),
        out_shape=jax.ShapeDtypeStruct((rows, cols), parts.dtype),
        compiler_params=_params(("arbitrary",)),
    )(parts)


def _place():
    return lax.axis_index("x"), lax.axis_index("y"), lax.axis_index("c")


ANY = pl.BlockSpec(memory_space=pl.ANY)


def _rider_parts(rider):
    if rider is None:
        return (), [], []
    kind, arrays = rider
    n = len(arrays)
    shapes = {"gather": _gathered_shapes, "pair": _pair_shapes, "chip": _chip_shapes}[kind](arrays)
    sems = _gather_sems(n) if kind == "gather" else _exchange_sems(n if kind == "pair" else 3 * n)
    return tuple(arrays), shapes, sems


def _rider_hooks(rider, ins, outs, sems, step, n_steps):
    if rider is None:
        return (lambda: None), (lambda: None)
    if rider[0] == "gather":
        start, forward, finish = _gather_steps(ins, outs, *sems)
    else:
        start, finish = {"pair": _pair_steps, "chip": _chip_steps}[rider[0]](ins, outs, *sems)
        forward = None

    def begin():
        pl.when(step == 0)(start)

    def end():
        if forward is not None:
            pl.when(step == n_steps - 2)(forward)
        pl.when(step == n_steps - 1)(finish)

    return begin, end


def _gathered_shapes(shards):
    return [jax.ShapeDtypeStruct((N_CHIPS,) + s.shape, s.dtype) for s in shards]


def _gather_sems(n):
    return [pltpu.SemaphoreType.DMA((7 * n,)), pltpu.SemaphoreType.DMA((7 * n,))]


def _gather_steps(ins, outs, send_sems, recv_sems):
    n = len(ins)
    halves = [r.shape[0] // 2 for r in ins]
    x, y, c = _place()
    my_chip = 2 * x + y
    me, sibling = (x, y, c), (x, y, 1 - c)
    chips = [(1 - x, y), (x, 1 - y), (1 - x, 1 - y)]

    def half_of(w, chip, pc):
        return outs[w].at[chip, pl.ds(pc * halves[w], halves[w]), :]

    def copy(w, k, src, dst, to):
        return pltpu.make_async_remote_copy(
            src_ref=src, dst_ref=dst, send_sem=send_sems.at[7 * w + k], recv_sem=recv_sems.at[7 * w + k],
            device_id=to, device_id_type=MESH)

    def firsts():
        cps = []
        for w in range(n):
            cps.append(copy(w, 0, ins[w], outs[w].at[my_chip], sibling))
            mine = ins[w].at[pl.ds(c * halves[w], halves[w]), :]
            for j, (px, py) in enumerate(chips):
                cps.append(copy(w, 1 + j, mine, half_of(w, my_chip, c), (px, py, c)))
        return cps

    def passes():
        return [copy(w, 4 + j, half_of(w, 2 * px + py, c), half_of(w, 2 * px + py, c), sibling)
                for w in range(n) for j, (px, py) in enumerate(chips)]

    def start():
        for cp in firsts():
            cp.start()

    def forward():
        fws = passes()
        for w in range(n):
            for j, (px, py) in enumerate(chips):
                landed = half_of(w, 2 * px + py, c)
                copy(w, 1 + j, landed, landed, me).wait_recv()
                fws[3 * w + j].start()

    def finish():
        for w in range(n):
            copy(w, 0, ins[w], outs[w].at[my_chip], me).wait_recv()
            for j, (px, py) in enumerate(chips):
                landed = half_of(w, 2 * px + py, 1 - c)
                copy(w, 4 + j, landed, landed, me).wait_recv()
        for cp in firsts() + passes():
            cp.wait_send()

    return start, forward, finish


def _pair_shapes(grads):
    return [jax.ShapeDtypeStruct((g.shape[0], g.shape[1] // 2, g.shape[2]), g.dtype) for g in grads]


def _exchange_sems(n):
    return [pltpu.SemaphoreType.DMA((n,)), pltpu.SemaphoreType.DMA((n,))]


def _exchange_steps(copies):
    def start():
        for cp in copies():
            cp.start()

    def finish():
        for cp in copies():
            cp.wait()

    return start, finish


def _pair_steps(ins, outs, send_sems, recv_sems):
    x, y, c = _place()

    def copies():
        return [pltpu.make_async_remote_copy(
            src_ref=ins[w].at[:, pl.ds((1 - c) * (ins[w].shape[1] // 2), ins[w].shape[1] // 2), :], dst_ref=outs[w],
            send_sem=send_sems.at[w], recv_sem=recv_sems.at[w], device_id=(x, y, 1 - c), device_id_type=MESH)
            for w in range(len(ins))]

    return _exchange_steps(copies)


def _chip_shapes(pairs):
    return [jax.ShapeDtypeStruct((N_CHIPS - 1,) + p.shape[1:], p.dtype) for p in pairs]


def _chip_steps(ins, outs, send_sems, recv_sems):
    x, y, c = _place()
    others = [(1 - x, y), (x, 1 - y), (1 - x, 1 - y)]

    def copies():
        return [pltpu.make_async_remote_copy(
            src_ref=ins[w].at[2 * px + py], dst_ref=outs[w].at[j],
            send_sem=send_sems.at[3 * w + j], recv_sem=recv_sems.at[3 * w + j],
            device_id=(px, py, c), device_id_type=MESH)
            for w in range(len(ins)) for j, (px, py) in enumerate(others)]

    return _exchange_steps(copies)


def _swap_halves(mine):
    n = len(mine)

    def body(*refs):
        ins, outs, send_sems, recv_sems = refs[:n], refs[n:2 * n], refs[2 * n], refs[2 * n + 1]
        x, y, c = _place()
        copies = [pltpu.make_async_remote_copy(
            src_ref=ins[w], dst_ref=outs[w], send_sem=send_sems.at[w], recv_sem=recv_sems.at[w],
            device_id=(x, y, 1 - c), device_id_type=MESH) for w in range(n)]
        for cp in copies:
            cp.start()
        for cp in copies:
            cp.wait()

    return pl.pallas_call(
        body, name="grad_swap_halves",
        out_shape=[jax.ShapeDtypeStruct(h.shape, h.dtype) for h in mine],
        in_specs=[ANY] * n, out_specs=[ANY] * n,
        scratch_shapes=[pltpu.SemaphoreType.DMA((n,)), pltpu.SemaphoreType.DMA((n,))],
    )(*mine)


def _gather_small(small):
    srows, cols = small.shape

    def body(s_ref, all_ref, send_sems, recv_sems, local_sem):
        x, y, c = _place()
        me = 4 * x + 2 * y + c
        keep_small = pltpu.make_async_copy(s_ref, all_ref.at[me], local_sem)
        keep_small.start()
        sends = []
        for kk in range(1, 8):
            peer = (x ^ (kk >> 2), y ^ ((kk >> 1) & 1), c ^ (kk & 1))
            sends.append(pltpu.make_async_remote_copy(
                src_ref=s_ref, dst_ref=all_ref.at[me],
                send_sem=send_sems.at[kk], recv_sem=recv_sems.at[kk], device_id=peer, device_id_type=MESH))
        for cp in sends:
            cp.start()
        for kk in range(1, 8):
            px, py, pc = x ^ (kk >> 2), y ^ ((kk >> 1) & 1), c ^ (kk & 1)
            pltpu.make_async_remote_copy(
                src_ref=s_ref, dst_ref=all_ref.at[4 * px + 2 * py + pc],
                send_sem=send_sems.at[kk], recv_sem=recv_sems.at[kk], device_id=(px, py, pc),
                device_id_type=MESH).wait_recv()
        for cp in sends:
            cp.wait_send()
        keep_small.wait()

    return pl.pallas_call(
        body, name="gather_small",
        out_shape=jax.ShapeDtypeStruct((8, srows, cols), small.dtype),
        in_specs=[ANY], out_specs=ANY,
        scratch_shapes=[pltpu.SemaphoreType.DMA((8,)), pltpu.SemaphoreType.DMA((8,)), pltpu.SemaphoreType.DMA],
    )(small)


SHARDED = (("w_in", D_MODEL, IN_COLS, 1), ("w_up_a", ATT_WIDTH, D_MODEL, 1), ("w_up_b", ATT_WIDTH, D_MODEL, 1),
           ("w_out", D_MODEL, D_MODEL, 0), ("w_q_mem", D_MODEL, MEM_WIDTH, 0), ("w_kv_mem", D_MODEL, 2 * MEM_WIDTH, 0),
           ("w_o_mem", MEM_WIDTH, D_MODEL, 1), ("w_ffn_gate", D_FF, D_MODEL, 0), ("w_ffn_up", D_FF, D_MODEL, 0),
           ("w_ffn_down", D_FF, D_MODEL, 0))
TRANSPOSED = ("w_ffn_gate", "w_ffn_up")
NAMES = tuple(n for n, _, _, _ in SHARDED)


def _held(name, shard):
    return shard.T if name in TRANSPOSED else shard
EARLY, LATE = NAMES[:1], NAMES[1:]
GAINS = ("g_mix", "g_mem_q", "g_mem_kv", "g_ffn", "g_final")


def _natural(w3):
    n, r, c = w3.shape
    return w3.reshape(n * r, c)


def _shard_major(g, axis):
    if axis == 1:
        return g
    r, c = g.shape
    return g.reshape(N_CHIPS, r // N_CHIPS, c)


def kernel(x, mem, positions, g_mix, w_in, w_up_a, w_up_b, w_out, g_mem_q, g_mem_kv, w_q_mem, w_kv_mem, w_o_mem, g_ffn, w_ffn_gate, w_ffn_up, w_ffn_down, g_final, loss_target, m_g_mix, m_w_in, m_w_up_a, m_w_up_b, m_w_out, m_g_mem_q, m_g_mem_kv, m_w_q_mem, m_w_kv_mem, m_w_o_mem, m_g_ffn, m_w_ffn_gate, m_w_ffn_up, m_w_ffn_down, m_g_final, v_g_mix, v_w_in, v_w_up_a, v_w_up_b, v_w_out, v_g_mem_q, v_g_mem_kv, v_w_q_mem, v_w_kv_mem, v_w_o_mem, v_g_ffn, v_w_ffn_gate, v_w_ffn_up, v_w_ffn_down, v_g_final):
    given = dict(locals())
    shards = {n: _held(n, given[n][0]) for n in NAMES}

    early_shards = [shards[n].astype(BF16) for n in EARLY]
    late_shards = [shards[n].astype(BF16) for n in LATE]
    c_idx = lax.axis_index("c").astype(jnp.int32).reshape(1)
    s_idx = (2 * lax.axis_index("x") + lax.axis_index("y")).astype(jnp.int32).reshape(1)

    loss_row, grad_x, mine, gain_grads = _local_step(x, mem, positions, loss_target, g_mix, g_mem_q, g_mem_kv,
                                                     g_ffn, g_final, {}, early_shards, late_shards, (c_idx, s_idx))
    return _reduce_and_update(given, shards, loss_row, grad_x, mine, gain_grads, c_idx)


def _reduce_halves(glist, names, c_idx, s_idx, pair_exchange, chip_exchange):
    theirs = pair_exchange(glist)
    pairs = [_pair_sum(g, t, c_idx, name="pair_sum_" + n) for n, g, t in zip(names, glist, theirs)]
    recv = chip_exchange(pairs)
    return [_chip_sum(p, r, s_idx, name="chip_sum_" + n) for n, p, r in zip(names, pairs, recv)]


def _local_step(x, mem, positions, loss_target, g_mix, g_mem_q, g_mem_kv, g_ffn, g_final, wf,
                early_shards=None, late_shards=None, place=None):
    b_dim, s_dim, d = x.shape
    t_dim = b_dim * s_dim
    n_mem = mem.shape[1]
    wf = dict(wf)

    xb = x.reshape(t_dim, d)
    tgt = loss_target.reshape(t_dim, d)
    memf = mem.reshape(b_dim * n_mem, d)
    gfin = g_final.reshape(1, d)
    pos = positions.reshape(t_dim, 1).astype(F32)

    lane = jnp.arange(LANES) % HEAD_DIM
    half = ROPE_DIM // 2
    inv_freq = ROPE_THETA ** (-jnp.arange(half, dtype=F32) / half)
    inv_lane = jnp.where(lane < ROPE_DIM, inv_freq[lane % half], 0.0).reshape(1, -1).astype(F32)
    sel_a = (lane < half).astype(F32).reshape(1, -1)
    sel_b = ((lane >= half) & (lane < ROPE_DIM)).astype(F32).reshape(1, -1)

    def rows3(t):
        return t.reshape(b_dim, s_dim, t.shape[-1])

    def rows2(t):
        return t.reshape(t_dim, t.shape[-1])

    if early_shards:
        n1, gathered = _rms_fwd(xb, g_mix, name="rms_mix", rider=("gather", early_shards))
        wf.update(zip(EARLY, gathered))
    else:
        n1 = _rms_fwd(xb, g_mix, name="rms_mix")
    proj = _mm_cs(n1, wf["w_in"], name="mm_in")
    proj3 = rows3(proj)
    cs, sn = _rope_table(pos, inv_lane, sel_a, sel_b)
    cs3, sn3 = rows3(cs), rows3(sn)
    (oa16, oa32, lse_a), _ = _dil_fwd(proj3, cs3, sn3, sel_a, sel_b)
    ob16, gathered = _sb_fwd(proj3, ("gather", late_shards) if late_shards else None)
    wf.update(zip(LATE, gathered))
    w_out, w_q, w_kv = _natural(wf["w_out"]), _natural(wf["w_q_mem"]), _natural(wf["w_kv_mem"])
    oa, ob = rows2(oa16), rows2(ob16)
    ua = _mm_sm(oa, wf["w_up_a"], name="mm_up_a")
    ub = _mm_sm(ob, wf["w_up_b"], name="mm_up_b")
    mixed, h1, hn = _gate_out_norm(proj, ua, ub, w_out, xb, g_mem_q)

    memn = _rms_fwd(memf, g_mem_kv, name="rms_mem_kv")
    qm = _mm(hn, w_q, name="mm_q_mem", out_dtype=BF16)
    kvm = _mm(memn, w_kv, name="mm_kv_mem", out_dtype=BF16)
    qm3, kvm3 = rows3(qm), kvm.reshape(b_dim, n_mem, 2 * MEM_WIDTH)
    om = rows2(_mem_fwd(qm3, kvm3))
    h2 = _mm_sm(om, wf["w_o_mem"], name="mm_o_mem", add=h1)

    n3 = _rms_fwd(h2, g_ffn, name="rms_ffn")
    gate3, up3, act3 = _ffn_up_swiglu(n3, wf["w_ffn_gate"], wf["w_ffn_up"])
    loss_row, dh3, dg_final = _down_final(act3, wf["w_ffn_down"], h2, gfin, tgt)

    grads = {}
    grads["w_ffn_down"] = _mm_ffn_down_dw(act3, dh3, name="mm_down_dw")
    dgate3, dup3 = _ffn_down_dx_swiglu(dh3, wf["w_ffn_down"], gate3, up3)
    grads["w_ffn_gate"] = _mm_ffn_down_dw(dgate3, n3, name="mm_gate_dw")
    grads["w_ffn_up"] = _mm_ffn_down_dw(dup3, n3, name="mm_up_dw")
    dn3 = _ffn_up_dx(dgate3, dup3, wf["w_ffn_gate"], wf["w_ffn_up"])
    dh2, dg_ffn = _rms_bwd(h2, g_ffn, dn3, dh3, name="rms_ffn_bwd")

    dom = _mm_sm_dx(dh2, wf["w_o_mem"], name="mm_o_mem_dx", out_dtype=BF16)
    grads["w_o_mem"] = _mm_sm_dw(om, dh2, name="mm_o_mem_dw")
    dqm, dkm, dvm = _mem_bwd(qm3, kvm3, rows3(dom))
    dqm = rows2(dqm)
    dkvm = jnp.concatenate([dkm, dvm], axis=-1).reshape(b_dim * n_mem, 2 * MEM_WIDTH).astype(BF16)
    grads["w_q_mem"] = _shard_major(_mm(hn, dqm, name="mm_q_mem_dw", ta=True), 0)
    dhn = _mm(dqm, w_q, name="mm_q_mem_dx", tb=True)
    grads["w_kv_mem"] = _shard_major(_mm(memn, dkvm, name="mm_kv_mem_dw", ta=True), 0)
    dmemn = _mm(dkvm, w_kv, name="mm_kv_mem_dx", tb=True)
    _, dg_mem_kv = _rms_bwd(memf, g_mem_kv, dmemn, None, name="rms_mem_kv_bwd")
    dh1, dg_mem_q = _rms_bwd(h1, g_mem_q, dhn, dh2, name="rms_mem_q_bwd")

    grads["w_out"] = _shard_major(_mm(mixed, dh1, name="mm_out_dw", ta=True), 0)
    dua, dub, dgates = _out_dx_gate_bwd(dh1, w_out, proj, ua, ub)
    doa = _mm_sm_dx(dua, wf["w_up_a"], name="mm_up_a_dx")
    grads["w_up_a"] = _mm_sm_dw(oa, dua, name="mm_up_a_dw")
    dob = _mm_sm_dx(dub, wf["w_up_b"], name="mm_up_b_dx", out_dtype=BF16)
    grads["w_up_b"] = _mm_sm_dw(ob, dub, name="mm_up_b_dw")

    att = {}

    def dil_with_pairs(glist):
        att["a"], theirs = _dil_bwd(proj3, cs3, sn3, sel_a, sel_b, rows3(doa), oa32, lse_a,
                                    ("pair", glist) if glist else None)
        return theirs

    def sb_with_chips(pairs):
        att["b"], recv = _sb_bwd(proj3, rows3(dob), ("chip", pairs) if pairs else None)
        return recv

    if place is None:
        dil_with_pairs(())
        sb_with_chips(())
    else:
        mine_late = _reduce_halves([grads[n] for n in LATE], LATE, *place, dil_with_pairs, sb_with_chips)
    dproj = jnp.concatenate([rows2(t) for t in att["a"] + att["b"]] + [dgates], axis=1)
    grads["w_in"] = _mm_cs_dw(n1, dproj, name="mm_in_dw")
    if place is None:
        dn1 = _mm_cs_dx(dproj, wf["w_in"], name="mm_in_dx")
        dx, dg_mix = _rms_bwd(xb, g_mix, dn1, dh1, name="rms_mix_bwd")
    else:
        tail = {}

        def dx_with_pairs(glist):
            tail["dn1"], theirs = _mm_cs_dx(dproj, wf["w_in"], name="mm_in_dx", rider=("pair", glist))
            return theirs

        def rms_with_chips(pairs):
            tail["dx"], tail["dg"], recv = _rms_bwd(xb, g_mix, tail["dn1"], dh1, name="rms_mix_bwd",
                                                    rider=("chip", pairs))
            return recv

        mine_early = _reduce_halves([grads[n] for n in EARLY], EARLY, *place, dx_with_pairs, rms_with_chips)
        dx, dg_mix = tail["dx"], tail["dg"]
    grad_x = dx.reshape(b_dim, s_dim, d)
    gains = (dg_mix, dg_mem_q, dg_mem_kv, dg_ffn, dg_final)
    if place is None:
        return loss_row, grad_x, grads, gains
    return loss_row, grad_x, mine_early + mine_late, gains


def _reduce_and_update(given, shards, loss_row, grad_x, mine, gain_grads, c_idx):
    d = D_MODEL
    dg_mix, dg_mem_q, dg_mem_kv, dg_ffn, dg_final = gain_grads
    small = jnp.concatenate([dg_mix, dg_mem_q, dg_mem_kv, dg_ffn, dg_final,
                             jnp.pad(loss_row, ((0, 0), (0, FLAT_COLS - LANES))), jnp.zeros((2, FLAT_COLS), F32)], axis=0)
    small_all = _gather_small(small)
    others = _swap_halves(mine)
    small_sum = _sum8(small_all)
    loss = small_sum[5, 0]

    out_g, out_d, out_m, out_v = {}, {}, {}, {}
    for n, mine_n, other_n in zip(NAMES, mine, others):
        res = _adamw_halves(shards[n], mine_n, other_n, _held(n, given["m_" + n][0]), _held(n, given["v_" + n][0]),
                            c_idx, name="adamw_" + n)
        out_g[n], out_d[n], out_m[n], out_v[n] = [_held(n, r)[None] for r in res]
    gain_w = jnp.concatenate([given[n].reshape(1, d) for n in GAINS], axis=0)
    gain_m = jnp.concatenate([given["m_" + n].reshape(1, d) for n in GAINS], axis=0)
    gain_v = jnp.concatenate([given["v_" + n].reshape(1, d) for n in GAINS], axis=0)
    gain_g = small_sum[:len(GAINS)]
    gd, gm, gv = _adamw(gain_w, gain_g, gain_m, gain_v, name="adamw_gains")
    for i, n in enumerate(GAINS):
        shape = given[n].shape
        out_g[n], out_d[n] = gain_g[i].reshape(shape), gd[i].reshape(shape)
        out_m[n], out_v[n] = gm[i].reshape(shape), gv[i].reshape(shape)

    order = ["g_mix", "w_in", "w_up_a", "w_up_b", "w_out", "g_mem_q", "g_mem_kv", "w_q_mem", "w_kv_mem", "w_o_mem",
             "g_ffn", "w_ffn_gate", "w_ffn_up", "w_ffn_down", "g_final"]
    return (loss, grad_x, *[out_g[n] for n in order], *[out_d[n] for n in order],
            *[out_m[n] for n in order], *[out_v[n] for n in order])
```

```python
import jax
import jax.numpy as jnp
from jax import lax
from jax.experimental import pallas as pl
from jax.experimental.pallas import tpu as pltpu

F32 = jnp.float32
BF16 = jnp.bfloat16
MESH = pl.DeviceIdType.MESH

D_MODEL = 1024
HEAD_DIM = 64
N_HEADS = 8
ATT_WIDTH = N_HEADS * HEAD_DIM
DIL_PATTERNS = ((128, 1), (512, 4), (2048, 16))
BLOCK = 128
SB_ROWS = 1024
SB_STEP = 8
ROPE_THETA = 500000.0
ROPE_DIM = HEAD_DIM // 4
N_HEADS_MEM = 4
MEM_HEAD_DIM = 128
MEM_WIDTH = N_HEADS_MEM * MEM_HEAD_DIM
D_FF = 2816
IN_COLS = 6 * ATT_WIDTH + 2 * D_MODEL
RMS_EPS = 1e-6
ADAM_LR = 0.001
ADAM_B1 = 0.9
ADAM_B2 = 0.999
ADAM_EPS = 1e-08
ADAM_WD = 0.01
ADAM_STEP = 10

N_CHIPS = 4
LANES = 128
FLAT_COLS = 1024
VMEM_LIMIT = 56 * 1024 * 1024

PAIRS = ATT_WIDTH // LANES
COL_QA, COL_KA, COL_VA, COL_QB, COL_KB, COL_VB = (i * PAIRS for i in range(6))

MM_CAP = 1408
TOK_CAP = 2048
NN = (((1,), (0,)), ((), ()))
NT = (((1,), (1,)), ((), ()))
TN = (((0,), (0,)), ((), ()))
BNN = (((2,), (1,)), ((0,), (0,)))
BNT = (((2,), (2,)), ((0,), (0,)))
BTN = (((1,), (1,)), ((0,), (0,)))
DIL_BATCH = 16
DIL_CHUNK = 512


def _tile(dim, cap, unit=LANES):
    if dim <= cap:
        return dim
    best = None
    for t in range(unit, cap + 1, unit):
        if dim % t == 0:
            best = t
    assert best is not None, (dim, cap)
    return best


def _row_cap(cols):
    return max(256, (1 << 18) // cols)


def _params(sem):
    return pltpu.CompilerParams(dimension_semantics=sem, vmem_limit_bytes=VMEM_LIMIT)


def _mm(a, b, *, name, ta=False, tb=False, add=None, out_dtype=F32,
        tm_cap=MM_CAP, tn_cap=MM_CAP, tk_cap=MM_CAP):
    if ta:
        k_dim, m_dim = a.shape
    else:
        m_dim, k_dim = a.shape
    if tb:
        n_dim, kb = b.shape
    else:
        kb, n_dim = b.shape
    assert kb == k_dim, (a.shape, b.shape, ta, tb)
    tm, tn, tk = _tile(m_dim, tm_cap), _tile(n_dim, tn_cap), _tile(k_dim, tk_cap)
    nk = k_dim // tk
    dims = (((0 if ta else 1,), (1 if tb else 0,)), ((), ()))
    has_add = add is not None

    def body(*refs):
        if has_add:
            a_ref, b_ref, add_ref, o_ref = refs[:4]
        else:
            a_ref, b_ref, o_ref = refs[:3]
        part = lax.dot_general(a_ref[...].astype(BF16), b_ref[...].astype(BF16), dims, preferred_element_type=F32)

        def finish(r):
            if has_add:
                r = add_ref[...] + r
            o_ref[...] = r.astype(out_dtype)

        if nk == 1:
            finish(part)
            return
        acc_ref = refs[-1]
        k = pl.program_id(2)

        @pl.when(k == 0)
        def _():
            acc_ref[...] = part

        @pl.when(k > 0)
        def _():
            acc_ref[...] += part

        @pl.when(k == nk - 1)
        def _():
            finish(acc_ref[...])

    a_spec = pl.BlockSpec((tk, tm), lambda i, j, k: (k, i)) if ta else pl.BlockSpec((tm, tk), lambda i, j, k: (i, k))
    b_spec = pl.BlockSpec((tn, tk), lambda i, j, k: (j, k)) if tb else pl.BlockSpec((tk, tn), lambda i, j, k: (k, j))
    o_spec = pl.BlockSpec((tm, tn), lambda i, j, k: (i, j))
    in_specs = [a_spec, b_spec] + ([o_spec] if has_add else [])
    args = (a, b) + ((add,) if has_add else ())
    return pl.pallas_call(
        body, name=name, grid=(m_dim // tm, n_dim // tn, nk),
        in_specs=in_specs, out_specs=o_spec,
        out_shape=jax.ShapeDtypeStruct((m_dim, n_dim), out_dtype),
        scratch_shapes=[pltpu.VMEM((tm, tn), F32)] if nk > 1 else [],
        compiler_params=_params(("parallel", "parallel", "arbitrary")),
    )(*args)


def _mm_core(name, a, b, a_spec, b_spec, o_spec, out_shape, grid, dims, *, add=None, out_dtype=F32, rider=None):
    nk = grid[2]
    has_add = add is not None
    n_in = 3 if has_add else 2
    acc_shape = tuple(d for d in o_spec.block_shape if d is not None)
    extra, extra_shapes, extra_sems = _rider_parts(rider)
    n_w = len(extra)

    def body(*refs):
        a_ref, b_ref = refs[:2]
        o_ref = refs[n_in + n_w]
        step = (pl.program_id(0) * grid[1] + pl.program_id(1)) * nk + pl.program_id(2)
        begin, end = _rider_hooks(rider, refs[n_in:n_in + n_w], refs[n_in + n_w + 1:n_in + 2 * n_w + 1], refs[-2:],
                                  step, grid[0] * grid[1] * nk)
        begin()
        part = lax.dot_general(a_ref[...].astype(BF16), b_ref[...].astype(BF16), dims, preferred_element_type=F32)

        def finish(r):
            if has_add:
                r = refs[2][...] + r
            o_ref[...] = r.astype(out_dtype)

        if nk == 1:
            finish(part)
        else:
            acc_ref = refs[n_in + 2 * n_w + 1]
            k = pl.program_id(2)

            @pl.when(k == 0)
            def _():
                acc_ref[...] = part

            @pl.when(k > 0)
            def _():
                acc_ref[...] += part

            @pl.when(k == nk - 1)
            def _():
                finish(acc_ref[...])
        end()

    in_specs = [a_spec, b_spec] + ([o_spec] if has_add else []) + [ANY] * n_w
    args = (a, b) + ((add,) if has_add else ()) + extra
    res = pl.pallas_call(
        body, name=name, grid=grid, in_specs=in_specs, out_specs=[o_spec] + [ANY] * n_w,
        out_shape=[jax.ShapeDtypeStruct(out_shape, out_dtype)] + extra_shapes,
        scratch_shapes=([pltpu.VMEM(acc_shape, F32)] if nk > 1 else []) + extra_sems,
        compiler_params=_params(("arbitrary",) * 3 if n_w else ("parallel", "parallel", "arbitrary")),
    )(*args)
    return (res[0], res[1:]) if n_w else res[0]


def _mm_cs(a, w3, *, name):
    m_dim, k_dim = a.shape
    _, _, n4 = w3.shape
    tm, tn, tk = _tile(m_dim, TOK_CAP), _tile(n4, MM_CAP), _tile(k_dim, MM_CAP)
    npb = n4 // tn
    return _mm_core(name, a, w3,
                    pl.BlockSpec((tm, tk), lambda i, j, k: (i, k)),
                    pl.BlockSpec((None, tk, tn), lambda i, j, k: (j // npb, k, j % npb)),
                    pl.BlockSpec((tm, tn), lambda i, j, k: (i, j)),
                    (m_dim, N_CHIPS * n4), (m_dim // tm, N_CHIPS * npb, k_dim // tk), NN)


def _mm_cs_dx(dy, w3, *, name, out_dtype=F32, rider=None):
    m_dim, _ = dy.shape
    _, k_dim, n4 = w3.shape
    tm, tkw, tn = _tile(m_dim, TOK_CAP), _tile(k_dim, MM_CAP), _tile(n4, MM_CAP)
    npb = n4 // tn
    return _mm_core(name, dy, w3,
                    pl.BlockSpec((tm, tn), lambda i, j, k: (i, k)),
                    pl.BlockSpec((None, tkw, tn), lambda i, j, k: (k // npb, j, k % npb)),
                    pl.BlockSpec((tm, tkw), lambda i, j, k: (i, j)),
                    (m_dim, k_dim), (m_dim // tm, k_dim // tkw, N_CHIPS * npb), NT, out_dtype=out_dtype, rider=rider)


def _mm_cs_dw(a, dy, *, name):
    m_dim, k_dim = a.shape
    n4 = dy.shape[1] // N_CHIPS
    tmk, tn, tk = _tile(k_dim, MM_CAP), _tile(n4, MM_CAP), _tile(m_dim, TOK_CAP)
    npb = n4 // tn
    return _mm_core(name, a, dy,
                    pl.BlockSpec((tk, tmk), lambda i, j, k: (k, i)),
                    pl.BlockSpec((tk, tn), lambda i, j, k: (k, j)),
                    pl.BlockSpec((None, tmk, tn), lambda i, j, k: (j // npb, i, j % npb)),
                    (N_CHIPS, k_dim, n4), (k_dim // tmk, N_CHIPS * npb, m_dim // tk), TN)


def _mm_sm(a, w3, *, name, add=None, out_dtype=F32, tt=1024):
    t_dim, k_dim = a.shape
    n_s, _, n4 = w3.shape
    has_add = add is not None

    def body(*refs):
        a_ref, w_ref, o_ref = refs[0], refs[1], refs[-1]
        av = a_ref[...].astype(BF16)
        for s in range(n_s):
            cols = slice(s * n4, (s + 1) * n4)
            r = jnp.dot(av, w_ref[s], preferred_element_type=F32)
            if has_add:
                r = refs[2][:, cols] + r
            o_ref[:, cols] = r.astype(out_dtype)

    row = pl.BlockSpec((tt, n_s * n4), lambda i: (i, 0))
    return pl.pallas_call(
        body, name=name, grid=(t_dim // tt,),
        in_specs=[pl.BlockSpec((tt, k_dim), lambda i: (i, 0)), pl.BlockSpec(w3.shape, lambda i: (0, 0, 0))]
        + ([row] if has_add else []),
        out_specs=row, out_shape=jax.ShapeDtypeStruct((t_dim, n_s * n4), out_dtype),
        compiler_params=_params(("parallel",)),
    )(*((a, w3) + ((add,) if has_add else ())))


def _mm_sm_dx(dy, w3, *, name, out_dtype=F32, tt=1024):
    t_dim, _ = dy.shape
    n_s, k_dim, n4 = w3.shape

    def body(dy_ref, w_ref, o_ref):
        dyv = dy_ref[...].astype(BF16)
        acc = lax.dot_general(dyv[:, :n4], w_ref[0], NT, preferred_element_type=F32)
        for s in range(1, n_s):
            acc = acc + lax.dot_general(dyv[:, s * n4:(s + 1) * n4], w_ref[s], NT, preferred_element_type=F32)
        o_ref[...] = acc.astype(out_dtype)

    return pl.pallas_call(
        body, name=name, grid=(t_dim // tt,),
        in_specs=[pl.BlockSpec((tt, n_s * n4), lambda i: (i, 0)), pl.BlockSpec(w3.shape, lambda i: (0, 0, 0))],
        out_specs=pl.BlockSpec((tt, k_dim), lambda i: (i, 0)),
        out_shape=jax.ShapeDtypeStruct((t_dim, k_dim), out_dtype),
        compiler_params=_params(("parallel",)),
    )(dy, w3)


def _mm_sm_dw(a, dy, *, name, tk=1024):
    t_dim, k_dim = a.shape
    n4 = dy.shape[1] // N_CHIPS

    def body(a_ref, dy_ref, o_ref):
        av = a_ref[...].astype(BF16)
        dyv = dy_ref[...].astype(BF16)

        @pl.when(pl.program_id(0) == 0)
        def _():
            o_ref[...] = jnp.zeros_like(o_ref)

        for s in range(N_CHIPS):
            o_ref[s] += lax.dot_general(av, dyv[:, s * n4:(s + 1) * n4], TN, preferred_element_type=F32)

    return pl.pallas_call(
        body, name=name, grid=(t_dim // tk,),
        in_specs=[pl.BlockSpec((tk, k_dim), lambda i: (i, 0)), pl.BlockSpec((tk, N_CHIPS * n4), lambda i: (i, 0))],
        out_specs=pl.BlockSpec((N_CHIPS, k_dim, n4), lambda i: (0, 0, 0)),
        out_shape=jax.ShapeDtypeStruct((N_CHIPS, k_dim, n4), F32),
        compiler_params=_params(("arbitrary",)),
    )(a, dy)


def _ffn_up_dx(dg3, du3, wg3, wu3, *, tt=512):
    n_s, t_dim, f4 = dg3.shape
    d = wg3.shape[2]

    def body(dg_ref, du_ref, wg_ref, wu_ref, o_ref):
        acc = None
        for s in range(n_s):
            part = (jnp.dot(dg_ref[s], wg_ref[s], preferred_element_type=F32)
                    + jnp.dot(du_ref[s], wu_ref[s], preferred_element_type=F32))
            acc = part if acc is None else acc + part
        o_ref[...] = acc

    a_spec = pl.BlockSpec((n_s, tt, f4), lambda i: (0, i, 0))
    w_spec = pl.BlockSpec(wg3.shape, lambda i: (0, 0, 0))
    return pl.pallas_call(
        body, name="ffn_up_dx", grid=(t_dim // tt,),
        in_specs=[a_spec, a_spec, w_spec, w_spec], out_specs=pl.BlockSpec((tt, d), lambda i: (i, 0)),
        out_shape=jax.ShapeDtypeStruct((t_dim, d), F32),
        compiler_params=_params(("parallel",)),
    )(dg3, du3, wg3, wu3)


def _mm_ffn_down_dw(act3, dh, *, name, tk=1024):
    n_s, t_dim, f4 = act3.shape
    d = dh.shape[1]

    def body(a_ref, b_ref, o_ref):
        bv = b_ref[...].astype(BF16)

        @pl.when(pl.program_id(0) == 0)
        def _():
            o_ref[...] = jnp.zeros_like(o_ref)

        for s in range(n_s):
            o_ref[s] += lax.dot_general(a_ref[s], bv, TN, preferred_element_type=F32)

    return pl.pallas_call(
        body, name=name, grid=(t_dim // tk,),
        in_specs=[pl.BlockSpec((n_s, tk, f4), lambda i: (0, i, 0)), pl.BlockSpec((tk, d), lambda i: (i, 0))],
        out_specs=pl.BlockSpec((n_s, f4, d), lambda i: (0, 0, 0)),
        out_shape=jax.ShapeDtypeStruct((n_s, f4, d), F32),
        compiler_params=_params(("arbitrary",)),
    )(act3, dh)


def _rms_fwd(x, g, *, name, tt=512, rider=None):
    t_dim, d = x.shape
    tt = _tile(t_dim, tt, 8)
    extra, extra_shapes, extra_sems = _rider_parts(rider)
    n_w = len(extra)

    def body(*refs):
        x_ref, g_ref, o_ref = refs[0], refs[1], refs[2 + n_w]
        begin, end = _rider_hooks(rider, refs[2:2 + n_w], refs[3 + n_w:3 + 2 * n_w], refs[-2:], pl.program_id(0),
                                  t_dim // tt)
        begin()
        xv = x_ref[...]
        r = lax.rsqrt(jnp.mean(xv * xv, axis=-1, keepdims=True) + RMS_EPS)
        o_ref[...] = ((xv * r) * g_ref[...]).astype(o_ref.dtype)
        end()

    res = pl.pallas_call(
        body, name=name, grid=(t_dim // tt,),
        in_specs=[pl.BlockSpec((tt, d), lambda i: (i, 0)), pl.BlockSpec((1, d), lambda i: (0, 0))] + [ANY] * n_w,
        out_specs=[pl.BlockSpec((tt, d), lambda i: (i, 0))] + [ANY] * n_w,
        out_shape=[jax.ShapeDtypeStruct((t_dim, d), BF16)] + extra_shapes,
        scratch_shapes=extra_sems,
        compiler_params=_params(("arbitrary",) if n_w else ("parallel",)),
    )(x, g, *extra)
    return (res[0], res[1:]) if n_w else res[0]


def _rms_bwd(x, g, dy, add, *, name, tt=512, rider=None):
    t_dim, d = x.shape
    tt = _tile(t_dim, tt, 8)
    has_add = add is not None
    n_in = 4 if has_add else 3
    extra, extra_shapes, extra_sems = _rider_parts(rider)
    n_w = len(extra)

    def body(*refs):
        x_ref, g_ref, dy_ref = refs[:3]
        add_ref = refs[3] if has_add else None
        dx_ref, dg_ref = refs[n_in + n_w:n_in + n_w + 2]
        begin, end = _rider_hooks(rider, refs[n_in:n_in + n_w], refs[n_in + n_w + 2:n_in + 2 * n_w + 2], refs[-2:],
                                  pl.program_id(0), t_dim // tt)
        begin()
        xv = x_ref[...]
        dyv = dy_ref[...].astype(F32)
        r = lax.rsqrt(jnp.mean(xv * xv, axis=-1, keepdims=True) + RMS_EPS)
        xh = xv * r
        u = dyv * g_ref[...]
        dx = r * (u - xh * jnp.mean(u * xh, axis=-1, keepdims=True))
        if has_add:
            dx = add_ref[...] + dx
        dx_ref[...] = dx

        @pl.when(pl.program_id(0) == 0)
        def _():
            dg_ref[...] = jnp.zeros_like(dg_ref)

        dg_ref[...] += jnp.sum(dyv * xh, axis=0, keepdims=True)
        end()

    row = pl.BlockSpec((tt, d), lambda i: (i, 0))
    vec = pl.BlockSpec((1, d), lambda i: (0, 0))
    in_specs = [row, vec, row] + ([row] if has_add else []) + [ANY] * n_w
    args = (x, g, dy) + ((add,) if has_add else ()) + extra
    res = pl.pallas_call(
        body, name=name, grid=(t_dim // tt,),
        in_specs=in_specs, out_specs=[row, vec] + [ANY] * n_w,
        out_shape=[jax.ShapeDtypeStruct((t_dim, d), F32), jax.ShapeDtypeStruct((1, d), F32)] + extra_shapes,
        scratch_shapes=extra_sems,
        compiler_params=_params(("arbitrary",)),
    )(*args)
    return (res[0], res[1], res[2:]) if n_w else (res[0], res[1])


def _down_final(act3, wd3, h, g, target, *, tt=512):
    n_s, t_dim, f4 = act3.shape
    d = h.shape[1]
    n_steps = t_dim // tt

    def body(a_ref, w_ref, h_ref, g_ref, t_ref, loss_ref, dh_ref, dg_ref, sq_ref):
        i = pl.program_id(0)
        xv = h_ref[...]
        for s in range(n_s):
            xv = xv + jnp.dot(a_ref[s], w_ref[s], preferred_element_type=F32)
        gv = g_ref[...]
        r = lax.rsqrt(jnp.mean(xv * xv, axis=-1, keepdims=True) + RMS_EPS)
        xh = xv * r
        err = xh * gv - t_ref[...]
        dyv = err * (1.0 / d)
        u = dyv * gv
        dh_ref[...] = r * (u - xh * jnp.mean(u * xh, axis=-1, keepdims=True))

        @pl.when(i == 0)
        def _():
            dg_ref[...] = jnp.zeros_like(dg_ref)
            sq_ref[...] = jnp.zeros_like(sq_ref)

        dg_ref[...] += jnp.sum(dyv * xh, axis=0, keepdims=True)
        sq_ref[...] += jnp.sum(err * err, axis=0, keepdims=True)

        @pl.when(i == n_steps - 1)
        def _():
            total = jnp.sum(sq_ref[...], axis=-1, keepdims=True) * (0.5 / d)
            loss_ref[...] = jnp.broadcast_to(total, loss_ref.shape)

    row = pl.BlockSpec((tt, d), lambda i: (i, 0))
    vec = pl.BlockSpec((1, d), lambda i: (0, 0))
    return pl.pallas_call(
        body, name="down_final_loss", grid=(n_steps,),
        in_specs=[pl.BlockSpec((n_s, tt, f4), lambda i: (0, i, 0)), pl.BlockSpec(wd3.shape, lambda i: (0, 0, 0)),
                  row, vec, row],
        out_specs=[pl.BlockSpec((1, LANES), lambda i: (0, 0)), row, vec],
        out_shape=[jax.ShapeDtypeStruct((1, LANES), F32), jax.ShapeDtypeStruct((t_dim, d), F32),
                   jax.ShapeDtypeStruct((1, d), F32)],
        scratch_shapes=[pltpu.VMEM((1, d), F32)],
        compiler_params=_params(("arbitrary",)),
    )(act3, wd3, h, g, target)


def _rope_table(pos, inv_lane, sel_a, sel_b, *, tt=512):
    t_dim = pos.shape[0]

    def body(p_ref, f_ref, a_ref, b_ref, c_ref, s_ref):
        ang = p_ref[...] * f_ref[...]
        on = (a_ref[...] + b_ref[...]) > 0.0
        c_ref[...] = jnp.where(on, jnp.cos(ang), 1.0)
        s_ref[...] = jnp.where(on, jnp.sin(ang), 0.0)

    vec = pl.BlockSpec((1, LANES), lambda i: (0, 0))
    row = pl.BlockSpec((tt, LANES), lambda i: (i, 0))
    shp = jax.ShapeDtypeStruct((t_dim, LANES), F32)
    return pl.pallas_call(
        body, name="rope_table", grid=(t_dim // tt,),
        in_specs=[pl.BlockSpec((tt, 1), lambda i: (i, 0)), vec, vec, vec],
        out_specs=[row, row], out_shape=[shp, shp],
        compiler_params=_params(("parallel",)),
    )(pos, inv_lane, sel_a, sel_b)


def _rotate(xv, cs, sn, sa, sb):
    half = ROPE_DIM // 2
    up = pltpu.roll(xv, LANES - half, 1)
    dn = pltpu.roll(xv, half, 1)
    return xv * cs + (dn * sb - up * sa) * sn


def _head_masks():
    h1 = lax.broadcasted_iota(jnp.int32, (1, LANES), 1) < HEAD_DIM
    return h1, jnp.logical_not(h1)


def _split_heads(xv, h1, h2):
    return jnp.where(h1, xv, 0.0).astype(BF16), jnp.where(h2, xv, 0.0).astype(BF16)


def _tri_masks():
    r = lax.broadcasted_iota(jnp.int32, (BLOCK, BLOCK), 0)
    c = lax.broadcasted_iota(jnp.int32, (BLOCK, BLOCK), 1)
    return c <= r, r <= c


def _stream_rows(start, dil):
    if dil == 1:
        return pl.ds(pl.multiple_of(start, BLOCK), BLOCK)
    return pl.ds(start, BLOCK, stride=dil)


def _dil_tile(idx, dil, nb):
    r = idx // nb
    n = idx % nb
    return (_stream_rows(r + dil * BLOCK * n, dil), _stream_rows(r + dil * BLOCK * jnp.maximum(n - 1, 0), dil),
            n > 0)


def _dil_specs(b_dim, s_dim):
    def col(c0):
        return pl.BlockSpec((None, s_dim, LANES),lambda b, h: (b, 0, c0 + h))
    tab = pl.BlockSpec((None, s_dim, LANES),lambda b, h: (b, 0, 0))
    vec = pl.BlockSpec((1, LANES), lambda b, h: (0, 0))
    return col, tab, vec


def _dil_fwd(proj3, cs3, sn3, sel_a, sel_b, rider=None):
    b_dim, s_dim, _ = proj3.shape
    scale = HEAD_DIM ** -0.5
    n_pat = len(DIL_PATTERNS)
    extra, extra_shapes, extra_sems = _rider_parts(rider)
    n_w = len(extra)
    n_steps = b_dim * PAIRS

    def body(*refs):
        q_ref, k_ref, v_ref, cs_ref, sn_ref, sa_ref, sb_ref = refs[:7]
        o16_ref, o32_ref, l_ref = refs[7 + n_w:10 + n_w]
        qr, kr = refs[10 + 2 * n_w:12 + 2 * n_w]
        per_pattern = refs[12 + 2 * n_w:12 + 2 * n_w + 2 * n_pat]
        og, lg = per_pattern[:n_pat], per_pattern[n_pat:]
        step = pl.program_id(0) * PAIRS + pl.program_id(1)
        begin, end = _rider_hooks(rider, refs[7:7 + n_w], refs[10 + n_w:10 + 2 * n_w], refs[-2:], step, n_steps)
        begin()
        h1, h2 = _head_masks()
        cur_ok, prev_ok = _tri_masks()
        sa, sb = sa_ref[...], sb_ref[...]

        def prep(j, _):
            rows = pl.ds(pl.multiple_of(j * DIL_CHUNK, DIL_CHUNK), DIL_CHUNK)
            cs, sn = cs_ref[rows, :], sn_ref[rows, :]
            qr[rows, :] = _rotate(q_ref[rows, :], cs, sn, sa, sb) * scale
            kr[rows, :] = _rotate(k_ref[rows, :], cs, sn, sa, sb)
            return 0

        lax.fori_loop(0, s_dim // DIL_CHUNK, prep, 0)

        for g, (_, dil) in enumerate(DIL_PATTERNS):
            nb = s_dim // dil // BLOCK

            def some(bi, _, g=g, dil=dil, nb=nb):
                tiles = [_dil_tile(bi * DIL_BATCH + t, dil, nb) for t in range(DIL_BATCH)]
                rows = [t[0] for t in tiles]
                q1, q2 = _split_heads(jnp.stack([qr[rw, :] for rw in rows]), h1, h2)
                kc = jnp.stack([kr[rw, :] for rw in rows]).astype(BF16)
                vc1, vc2 = _split_heads(jnp.stack([v_ref[rw, :] for rw in rows]), h1, h2)
                if nb > 1:
                    kp = jnp.stack([kr[t[1], :] for t in tiles]).astype(BF16)
                    vp1, vp2 = _split_heads(jnp.stack([v_ref[t[1], :] for t in tiles]), h1, h2)
                    p_ok = jnp.stack([jnp.logical_and(prev_ok, t[2]) for t in tiles])

                def head(qh, vch, vph):
                    sc = jnp.where(cur_ok, lax.dot_general(qh, kc, BNT, preferred_element_type=F32), -jnp.inf)
                    m = jnp.max(sc, axis=-1, keepdims=True)
                    if nb > 1:
                        sp = jnp.where(p_ok, lax.dot_general(qh, kp, BNT, preferred_element_type=F32), -jnp.inf)
                        m = jnp.maximum(m, jnp.max(sp, axis=-1, keepdims=True))
                    pc = jnp.exp(sc - m)
                    den = jnp.sum(pc, axis=-1, keepdims=True)
                    acc = lax.dot_general(pc.astype(BF16), vch, BNN, preferred_element_type=F32)
                    if nb > 1:
                        pp = jnp.exp(sp - m)
                        den = den + jnp.sum(pp, axis=-1, keepdims=True)
                        acc = acc + lax.dot_general(pp.astype(BF16), vph, BNN, preferred_element_type=F32)
                    return acc / den, m + jnp.log(den)

                o1, l1 = head(q1, vc1, vp1 if nb > 1 else None)
                o2, l2 = head(q2, vc2, vp2 if nb > 1 else None)
                o, l = o1 + o2, jnp.where(h1, l1, l2)
                for t, rw in enumerate(rows):
                    og[g][rw, :] = o[t]
                    lg[g][rw, :] = l[t]
                return 0

            lax.fori_loop(0, dil * nb // DIL_BATCH, some, 0)

        def comb(j, _):
            rows = pl.ds(pl.multiple_of(j * DIL_CHUNK, DIL_CHUNK), DIL_CHUNK)
            ls = [lg[g][rows, :] for g in range(n_pat)]
            m = jnp.maximum(jnp.maximum(ls[0], ls[1]), ls[2])
            es = [jnp.exp(l - m) for l in ls]
            den = es[0] + es[1] + es[2]
            o = (es[0] * og[0][rows, :] + es[1] * og[1][rows, :] + es[2] * og[2][rows, :]) / den
            o16_ref[rows, :] = o.astype(BF16)
            o32_ref[rows, :] = o
            l_ref[rows, :] = m + jnp.log(den)
            return 0

        lax.fori_loop(0, s_dim // DIL_CHUNK, comb, 0)
        end()

    col, tab, vec = _dil_specs(b_dim, s_dim)
    out = pl.BlockSpec((None, s_dim, LANES),lambda b, h: (b, 0, h))
    shp = (b_dim, s_dim, ATT_WIDTH)
    res = pl.pallas_call(
        body, name="dil_fwd", grid=(b_dim, PAIRS),
        in_specs=[col(COL_QA), col(COL_KA), col(COL_VA), tab, tab, vec, vec] + [ANY] * n_w,
        out_specs=[out, out, out] + [ANY] * n_w,
        out_shape=[jax.ShapeDtypeStruct(shp, BF16), jax.ShapeDtypeStruct(shp, F32), jax.ShapeDtypeStruct(shp, F32)]
        + extra_shapes,
        scratch_shapes=[pltpu.VMEM((s_dim, LANES), F32)] * (2 + 2 * n_pat) + extra_sems,
        compiler_params=_params(("arbitrary", "arbitrary")),
    )(proj3, proj3, proj3, cs3, sn3, sel_a, sel_b, *extra)
    return res[:3], res[3:]


def _dil_bwd(proj3, cs3, sn3, sel_a, sel_b, do3, o3, lse3, rider=None):
    b_dim, s_dim, _ = proj3.shape
    scale = HEAD_DIM ** -0.5
    extra, extra_shapes, extra_sems = _rider_parts(rider)
    n_w = len(extra)

    def body(*refs):
        q_ref, k_ref, v_ref, cs_ref, sn_ref, sa_ref, sb_ref, do_ref, o_ref, l_ref = refs[:10]
        dq_ref, dk_ref, dv_ref = refs[10 + n_w:13 + n_w]
        qr, kr, dqa, dka, dva = refs[13 + 2 * n_w:18 + 2 * n_w]
        step = pl.program_id(0) * PAIRS + pl.program_id(1)
        begin, end = _rider_hooks(rider, refs[10:10 + n_w], refs[13 + n_w:13 + 2 * n_w], refs[-2:], step,
                                  b_dim * PAIRS)
        begin()
        h1, h2 = _head_masks()
        cur_ok, prev_ok = _tri_masks()
        sa, sb = sa_ref[...], sb_ref[...]

        def prep(j, _):
            rows = pl.ds(pl.multiple_of(j * DIL_CHUNK, DIL_CHUNK), DIL_CHUNK)
            cs, sn = cs_ref[rows, :], sn_ref[rows, :]
            qr[rows, :] = _rotate(q_ref[rows, :], cs, sn, sa, sb) * scale
            kr[rows, :] = _rotate(k_ref[rows, :], cs, sn, sa, sb)
            zero = jnp.zeros((DIL_CHUNK, LANES), F32)
            dqa[rows, :] = zero
            dka[rows, :] = zero
            dva[rows, :] = zero
            return 0

        lax.fori_loop(0, s_dim // DIL_CHUNK, prep, 0)

        for _, dil in DIL_PATTERNS:
            nb = s_dim // dil // BLOCK

            def some(bi, _, dil=dil, nb=nb):
                tiles = [_dil_tile(bi * DIL_BATCH + t, dil, nb) for t in range(DIL_BATCH)]
                rows = [t[0] for t in tiles]
                q1, q2 = _split_heads(jnp.stack([qr[rw, :] for rw in rows]), h1, h2)
                dof = jnp.stack([do_ref[rw, :] for rw in rows])
                do1, do2 = _split_heads(dof, h1, h2)
                prod = dof * jnp.stack([o_ref[rw, :] for rw in rows])
                delta1 = jnp.sum(jnp.where(h1, prod, 0.0), axis=-1, keepdims=True)
                delta2 = jnp.sum(jnp.where(h2, prod, 0.0), axis=-1, keepdims=True)
                lt = jnp.stack([l_ref[rw, :] for rw in rows])
                lse1 = jnp.max(jnp.where(h1, lt, -jnp.inf), axis=-1, keepdims=True)
                lse2 = jnp.max(jnp.where(h2, lt, -jnp.inf), axis=-1, keepdims=True)

                def side(krows, ok):
                    kf = jnp.stack([kr[kw, :] for kw in krows])
                    k16 = kf.astype(BF16)
                    k1, k2 = _split_heads(kf, h1, h2)
                    v16 = jnp.stack([v_ref[kw, :] for kw in krows]).astype(BF16)

                    def head(qh, doh, lse, delta):
                        sc = lax.dot_general(qh, k16, BNT, preferred_element_type=F32)
                        p = jnp.where(ok, jnp.exp(sc - lse), 0.0)
                        dp = lax.dot_general(doh, v16, BNT, preferred_element_type=F32)
                        return p.astype(BF16), (p * (dp - delta)).astype(BF16)

                    p1, ds1 = head(q1, do1, lse1, delta1)
                    p2, ds2 = head(q2, do2, lse2, delta2)
                    dv = (lax.dot_general(p1, do1, BTN, preferred_element_type=F32)
                          + lax.dot_general(p2, do2, BTN, preferred_element_type=F32))
                    dk = (lax.dot_general(ds1, q1, BTN, preferred_element_type=F32)
                          + lax.dot_general(ds2, q2, BTN, preferred_element_type=F32))
                    for t, kw in enumerate(krows):
                        dva[kw, :] += dv[t]
                        dka[kw, :] += dk[t]
                    return (lax.dot_general(ds1, k1, BNN, preferred_element_type=F32)
                            + lax.dot_general(ds2, k2, BNN, preferred_element_type=F32))

                dq = side(rows, cur_ok)
                if nb > 1:
                    dq = dq + side([t[1] for t in tiles], jnp.stack([jnp.logical_and(prev_ok, t[2]) for t in tiles]))
                for t, rw in enumerate(rows):
                    dqa[rw, :] += dq[t] * scale
                return 0

            lax.fori_loop(0, dil * nb // DIL_BATCH, some, 0)

        def finish(j, _):
            rows = pl.ds(pl.multiple_of(j * DIL_CHUNK, DIL_CHUNK), DIL_CHUNK)
            cs, sn = cs_ref[rows, :], -sn_ref[rows, :]
            dq_ref[rows, :] = _rotate(dqa[rows, :], cs, sn, sa, sb).astype(BF16)
            dk_ref[rows, :] = _rotate(dka[rows, :], cs, sn, sa, sb).astype(BF16)
            dv_ref[rows, :] = dva[rows, :].astype(BF16)
            return 0

        lax.fori_loop(0, s_dim // DIL_CHUNK, finish, 0)
        end()

    col, tab, vec = _dil_specs(b_dim, s_dim)
    out = pl.BlockSpec((None, s_dim, LANES),lambda b, h: (b, 0, h))
    shp = jax.ShapeDtypeStruct((b_dim, s_dim, ATT_WIDTH), BF16)
    acc = pltpu.VMEM((s_dim, LANES), F32)
    res = pl.pallas_call(
        body, name="dil_bwd", grid=(b_dim, PAIRS),
        in_specs=[col(COL_QA), col(COL_KA), col(COL_VA), tab, tab, vec, vec, out, out, out] + [ANY] * n_w,
        out_specs=[out, out, out] + [ANY] * n_w, out_shape=[shp, shp, shp] + extra_shapes,
        scratch_shapes=[acc, acc, acc, acc, acc] + extra_sems,
        compiler_params=_params(("arbitrary", "arbitrary")),
    )(proj3, proj3, proj3, cs3, sn3, sel_a, sel_b, do3, o3, lse3, *extra)
    return res[:3], res[3:]


def _split_dot(x, tri):
    hi = x.astype(BF16)
    lo = (x - hi.astype(F32)).astype(BF16)
    return jnp.dot(hi, tri, preferred_element_type=F32) + jnp.dot(lo, tri, preferred_element_type=F32)


def _log_sigmoid(z):
    return jnp.minimum(z, 0.0) - jnp.log(1.0 + jnp.exp(-jnp.abs(z)))


def _sb_scores(qh, k16, valid):
    z = lax.dot_general(qh, k16, NT, preferred_element_type=F32)
    ls = _log_sigmoid(z)
    l1m = ls - z
    return ls, (l1m if valid is None else jnp.where(valid, l1m, 0.0))


def _sb_consts():
    r = lax.broadcasted_iota(jnp.int32, (BLOCK, BLOCK), 0)
    c = lax.broadcasted_iota(jnp.int32, (BLOCK, BLOCK), 1)
    after = (r > c).astype(BF16)
    before = (r < c).astype(BF16)
    qrow = lax.broadcasted_iota(jnp.int32, (SB_ROWS, BLOCK), 0)
    kcol = lax.broadcasted_iota(jnp.int32, (SB_ROWS, BLOCK), 1)
    return after, before, qrow, kcol


def _below(whole, lo, delta):
    if lo == 0:
        return whole + delta
    return whole + jnp.concatenate([jnp.zeros((lo,) + delta.shape[1:], delta.dtype), delta], axis=0)


def _pairs_loop(n_blocks, step, carry):
    def several(i, c):
        for j in range(SB_STEP):
            c = step(SB_STEP * i + j, c)
        return c

    return lax.fori_loop(0, n_blocks // SB_STEP, several, carry)


def _sb_fwd(proj3, rider=None):
    b_dim, s_dim, _ = proj3.shape
    scale = HEAD_DIM ** -0.5
    per = SB_ROWS // BLOCK
    extra, extra_shapes, extra_sems = _rider_parts(rider)
    n_w = len(extra)

    def body(*refs):
        q_ref, k_ref, v_ref = refs[:3]
        o_ref = refs[3 + n_w]
        step = pl.program_id(0) * PAIRS + pl.program_id(1)
        begin, end = _rider_hooks(rider, refs[3:3 + n_w], refs[4 + n_w:4 + 2 * n_w], refs[-2:], step, b_dim * PAIRS)
        begin()
        h1, h2 = _head_masks()
        after, _, qrow, kcol = _sb_consts()

        def qloop(qi, _):
            rows = pl.ds(pl.multiple_of(qi * SB_ROWS, SB_ROWS), SB_ROWS)
            q1, q2 = _split_heads(q_ref[rows, :] * scale, h1, h2)
            first = qi * per

            def block(kb, carry, lo):
                acc, run1, run2 = carry
                krows = pl.ds(pl.multiple_of(kb * BLOCK, BLOCK), BLOCK)
                k16 = k_ref[krows, :].astype(BF16)
                v1, v2 = _split_heads(v_ref[krows, :], h1, h2)
                valid = None if lo is None else kcol[:SB_ROWS - lo] < qrow[:SB_ROWS - lo]
                lo = lo or 0

                def head(qh, vh, run):
                    ls, l1m = _sb_scores(qh[lo:], k16, valid)
                    a = jnp.exp(ls + _split_dot(l1m, after) + run[lo:])
                    if valid is not None:
                        a = jnp.where(valid, a, 0.0)
                    return (jnp.dot(a.astype(BF16), vh, preferred_element_type=F32),
                            _below(run, lo, jnp.sum(l1m, axis=-1, keepdims=True)))

                o1, run1 = head(q1, v1, run1)
                o2, run2 = head(q2, v2, run2)
                return _below(acc, lo, o1 + o2), run1, run2

            zcol = jnp.zeros((SB_ROWS, 1), F32)
            carry = (jnp.zeros((SB_ROWS, LANES), F32), zcol, zcol)
            for kl in reversed(range(per)):
                carry = block(first + kl, carry, kl * BLOCK)
            acc, _, _ = _pairs_loop(first, lambda i, c: block(first - 1 - i, c, None), carry)
            o_ref[rows, :] = acc.astype(BF16)
            return 0

        lax.fori_loop(0, s_dim // SB_ROWS, qloop, 0)
        end()

    def col(c0):
        return pl.BlockSpec((None, s_dim, LANES),lambda b, h: (b, 0, c0 + h))

    res = pl.pallas_call(
        body, name="sb_fwd", grid=(b_dim, PAIRS),
        in_specs=[col(COL_QB), col(COL_KB), col(COL_VB)] + [ANY] * n_w, out_specs=[col(0)] + [ANY] * n_w,
        out_shape=[jax.ShapeDtypeStruct((b_dim, s_dim, ATT_WIDTH), BF16)] + extra_shapes,
        scratch_shapes=extra_sems,
        compiler_params=_params(("arbitrary", "arbitrary")),
    )(proj3, proj3, proj3, *extra)
    return res[0], res[1:]


def _sb_bwd(proj3, do3, rider=None):
    b_dim, s_dim, _ = proj3.shape
    scale = HEAD_DIM ** -0.5
    per = SB_ROWS // BLOCK
    nkb_max = s_dim // BLOCK
    extra, extra_shapes, extra_sems = _rider_parts(rider)
    n_w = len(extra)

    def body(*refs):
        q_ref, k_ref, v_ref, do_ref = refs[:4]
        dq_ref, dk_ref, dv_ref = refs[4 + n_w:7 + n_w]
        dka, dva, e_ref, sg_ref = refs[7 + 2 * n_w:11 + 2 * n_w]
        step = pl.program_id(0) * PAIRS + pl.program_id(1)
        begin, end = _rider_hooks(rider, refs[4:4 + n_w], refs[7 + n_w:7 + 2 * n_w], refs[-2:], step, b_dim * PAIRS)
        begin()
        h1, h2 = _head_masks()
        after, before, qrow, kcol = _sb_consts()
        dka[...] = jnp.zeros_like(dka)
        dva[...] = jnp.zeros_like(dva)

        def qloop(qi, _):
            rows = pl.ds(pl.multiple_of(qi * SB_ROWS, SB_ROWS), SB_ROWS)
            q1, q2 = _split_heads(q_ref[rows, :] * scale, h1, h2)
            do1, do2 = _split_heads(do_ref[rows, :].astype(F32), h1, h2)
            first = qi * per

            def pass1(kb, carry, lo):
                run1, run2 = carry
                krows = pl.ds(pl.multiple_of(kb * BLOCK, BLOCK), BLOCK)
                k16 = k_ref[krows, :].astype(BF16)
                v16 = v_ref[krows, :].astype(BF16)
                valid = None if lo is None else kcol[:SB_ROWS - lo] < qrow[:SB_ROWS - lo]
                lo = lo or 0
                part = pl.ds(lo, SB_ROWS - lo)

                def head(h, qh, doh, run):
                    ls, l1m = _sb_scores(qh[lo:], k16, valid)
                    a = jnp.exp(ls + _split_dot(l1m, after) + run[lo:])
                    if valid is not None:
                        a = jnp.where(valid, a, 0.0)
                    da = lax.dot_general(doh[lo:], v16, NT, preferred_element_type=F32)
                    e_ref[h, kb, part, :] = a * da
                    sg_ref[h, kb, part, :] = jnp.exp(ls)
                    return a.astype(BF16), _below(run, lo, jnp.sum(l1m, axis=-1, keepdims=True))

                a1, run1 = head(0, q1, do1, run1)
                a2, run2 = head(1, q2, do2, run2)
                dva[krows, :] += (lax.dot_general(a1, do1[lo:], TN, preferred_element_type=F32)
                                  + lax.dot_general(a2, do2[lo:], TN, preferred_element_type=F32))
                return run1, run2

            zcol = jnp.zeros((SB_ROWS, 1), F32)
            carry = (zcol, zcol)
            for kl in reversed(range(per)):
                carry = pass1(first + kl, carry, kl * BLOCK)
            _pairs_loop(first, lambda i, c: pass1(first - 1 - i, c, None), carry)

            def pass2(kb, carry, lo):
                dq, pre1, pre2 = carry
                krows = pl.ds(pl.multiple_of(kb * BLOCK, BLOCK), BLOCK)
                k1, k2 = _split_heads(k_ref[krows, :], h1, h2)
                valid = None if lo is None else kcol[:SB_ROWS - lo] < qrow[:SB_ROWS - lo]
                lo = lo or 0
                part = pl.ds(lo, SB_ROWS - lo)

                def head(h, pre):
                    ev = e_ref[h, kb, part, :]
                    sg = sg_ref[h, kb, part, :]
                    dz = ev * (1.0 - sg) - (_split_dot(ev, before) + pre[lo:]) * sg
                    if valid is not None:
                        dz = jnp.where(valid, dz, 0.0)
                    return dz.astype(BF16), _below(pre, lo, jnp.sum(ev, axis=-1, keepdims=True))

                dz1, pre1 = head(0, pre1)
                dz2, pre2 = head(1, pre2)
                dka[krows, :] += (lax.dot_general(dz1, q1[lo:], TN, preferred_element_type=F32)
                                  + lax.dot_general(dz2, q2[lo:], TN, preferred_element_type=F32))
                dq = _below(dq, lo, jnp.dot(dz1, k1, preferred_element_type=F32)
                            + jnp.dot(dz2, k2, preferred_element_type=F32))
                return dq, pre1, pre2

            carry = _pairs_loop(first, lambda i, c: pass2(i, c, None), (jnp.zeros((SB_ROWS, LANES), F32), zcol, zcol))
            for kl in range(per):
                carry = pass2(first + kl, carry, kl * BLOCK)
            dq = carry[0]
            dq_ref[rows, :] = (dq * scale).astype(BF16)
            return 0

        lax.fori_loop(0, s_dim // SB_ROWS, qloop, 0)
        dk_ref[...] = dka[...].astype(BF16)
        dv_ref[...] = dva[...].astype(BF16)
        end()

    def col(c0):
        return pl.BlockSpec((None, s_dim, LANES),lambda b, h: (b, 0, c0 + h))

    shp = jax.ShapeDtypeStruct((b_dim, s_dim, ATT_WIDTH), BF16)
    acc = pltpu.VMEM((s_dim, LANES), F32)
    strip = pltpu.VMEM((2, nkb_max, SB_ROWS, BLOCK), F32)
    res = pl.pallas_call(
        body, name="sb_bwd", grid=(b_dim, PAIRS),
        in_specs=[col(COL_QB), col(COL_KB), col(COL_VB), col(0)] + [ANY] * n_w,
        out_specs=[col(0), col(0), col(0)] + [ANY] * n_w,
        out_shape=[shp, shp, shp] + extra_shapes,
        scratch_shapes=[acc, acc, strip, strip] + extra_sems,
        compiler_params=_params(("arbitrary", "arbitrary")),
    )(proj3, proj3, proj3, do3, *extra)
    return res[:3], res[3:]


def _sigmoid(x):
    return 1.0 / (1.0 + jnp.exp(-x))


def _gate_out_norm(proj, ua, ub, w_out, x, g, *, tt=512):
    t_dim, d = ua.shape

    def body(ga_ref, gb_ref, ua_ref, ub_ref, w_ref, x_ref, g_ref, m_ref, h_ref, n_ref):
        mixed = (_sigmoid(ga_ref[...]) * ua_ref[...] + _sigmoid(gb_ref[...]) * ub_ref[...]).astype(BF16)
        m_ref[...] = mixed
        hv = x_ref[...] + jnp.dot(mixed, w_ref[...], preferred_element_type=F32)
        h_ref[...] = hv
        r = lax.rsqrt(jnp.mean(hv * hv, axis=-1, keepdims=True) + RMS_EPS)
        n_ref[...] = ((hv * r) * g_ref[...]).astype(BF16)

    row = pl.BlockSpec((tt, d), lambda i: (i, 0))
    return pl.pallas_call(
        body, name="gate_out_norm", grid=(t_dim // tt,),
        in_specs=[pl.BlockSpec((tt, d), lambda i: (i, 3)), pl.BlockSpec((tt, d), lambda i: (i, 4)), row, row,
                  pl.BlockSpec((d, d), lambda i: (0, 0)), row, pl.BlockSpec((1, d), lambda i: (0, 0))],
        out_specs=[row, row, row],
        out_shape=[jax.ShapeDtypeStruct((t_dim, d), BF16), jax.ShapeDtypeStruct((t_dim, d), F32),
                   jax.ShapeDtypeStruct((t_dim, d), BF16)],
        compiler_params=_params(("parallel",)),
    )(proj, proj, ua, ub, w_out, x, g)


def _out_dx_gate_bwd(dh, w_out, proj, ua, ub, *, tt=512):
    t_dim, d = ua.shape

    def body(dh_ref, w_ref, ga_ref, gb_ref, ua_ref, ub_ref, dua_ref, dub_ref, dg_ref):
        dm = lax.dot_general(dh_ref[...].astype(BF16), w_ref[...], NT, preferred_element_type=F32)
        sa = _sigmoid(ga_ref[...])
        sb = _sigmoid(gb_ref[...])
        dua_ref[...] = (dm * sa).astype(BF16)
        dub_ref[...] = (dm * sb).astype(BF16)
        dg_ref[:, :d] = (dm * ua_ref[...] * (sa * (1.0 - sa))).astype(BF16)
        dg_ref[:, d:] = (dm * ub_ref[...] * (sb * (1.0 - sb))).astype(BF16)

    row = pl.BlockSpec((tt, d), lambda i: (i, 0))
    wide = pl.BlockSpec((tt, 2 * d), lambda i: (i, 0))
    return pl.pallas_call(
        body, name="out_dx_gate_bwd", grid=(t_dim // tt,),
        in_specs=[row, pl.BlockSpec((d, d), lambda i: (0, 0)),
                  pl.BlockSpec((tt, d), lambda i: (i, 3)), pl.BlockSpec((tt, d), lambda i: (i, 4)), row, row],
        out_specs=[row, row, wide],
        out_shape=[jax.ShapeDtypeStruct((t_dim, d), BF16), jax.ShapeDtypeStruct((t_dim, d), BF16),
                   jax.ShapeDtypeStruct((t_dim, 2 * d), BF16)],
        compiler_params=_params(("parallel",)),
    )(dh, w_out, proj, proj, ua, ub)


def _ffn_up_swiglu(n, wg3, wu3, *, tt=1024):
    t_dim, d = n.shape
    n_s, f4, _ = wg3.shape

    def body(n_ref, wg_ref, wu_ref, g_ref, u_ref, a_ref):
        nv = n_ref[...]
        gv = lax.dot_general(nv, wg_ref[...], NT, preferred_element_type=F32)
        uv = lax.dot_general(nv, wu_ref[...], NT, preferred_element_type=F32)
        g_ref[...] = gv.astype(BF16)
        u_ref[...] = uv.astype(BF16)
        a_ref[...] = (gv * _sigmoid(gv) * uv).astype(BF16)

    wspec = pl.BlockSpec((None, f4, d), lambda i, s: (s, 0, 0))
    ospec = pl.BlockSpec((None, tt, f4), lambda i, s: (s, i, 0))
    shp = (n_s, t_dim, f4)
    return pl.pallas_call(
        body, name="ffn_up_swiglu", grid=(t_dim // tt, n_s),
        in_specs=[pl.BlockSpec((tt, d), lambda i, s: (i, 0)), wspec, wspec], out_specs=[ospec, ospec, ospec],
        out_shape=[jax.ShapeDtypeStruct(shp, BF16)] * 3,
        compiler_params=_params(("parallel", "parallel")),
    )(n, wg3, wu3)


def _ffn_down_dx_swiglu(dh, wd3, g3, u3, *, tt=1024):
    t_dim, d = dh.shape
    n_s, f4, _ = wd3.shape

    def body(dh_ref, w_ref, g_ref, u_ref, dg_ref, du_ref):
        da = lax.dot_general(dh_ref[...].astype(BF16), w_ref[...], NT, preferred_element_type=F32)
        gv = g_ref[...].astype(F32)
        sg = _sigmoid(gv)
        dg_ref[...] = (da * u_ref[...].astype(F32) * (sg + gv * sg * (1.0 - sg))).astype(BF16)
        du_ref[...] = (da * (gv * sg)).astype(BF16)

    spec = pl.BlockSpec((None, tt, f4), lambda i, s: (s, i, 0))
    shp = jax.ShapeDtypeStruct((n_s, t_dim, f4), BF16)
    return pl.pallas_call(
        body, name="ffn_down_dx_swiglu", grid=(t_dim // tt, n_s),
        in_specs=[pl.BlockSpec((tt, d), lambda i, s: (i, 0)), pl.BlockSpec((None, f4, d), lambda i, s: (s, 0, 0)),
                  spec, spec],
        out_specs=[spec, spec], out_shape=[shp, shp],
        compiler_params=_params(("parallel", "parallel")),
    )(dh, wd3, g3, u3)


def _mem_fwd(qm, kvm, *, tt=2048):
    b_dim, s_dim, _ = qm.shape
    n_mem = kvm.shape[1]
    scale = MEM_HEAD_DIM ** -0.5

    def body(q_ref, k_ref, v_ref, o_ref):
        sc = lax.dot_general(q_ref[0], k_ref[0], NT, preferred_element_type=F32) * scale
        p = jnp.exp(sc - jnp.max(sc, axis=-1, keepdims=True))
        p = p / jnp.sum(p, axis=-1, keepdims=True)
        o_ref[0] = jnp.dot(p.astype(BF16), v_ref[0], preferred_element_type=F32).astype(BF16)

    qs = pl.BlockSpec((1, tt, MEM_HEAD_DIM), lambda b, h, i: (b, i, h))
    return pl.pallas_call(
        body, name="mem_fwd", grid=(b_dim, N_HEADS_MEM, s_dim // tt),
        in_specs=[qs, pl.BlockSpec((1, n_mem, MEM_HEAD_DIM), lambda b, h, i: (b, 0, h)),
                  pl.BlockSpec((1, n_mem, MEM_HEAD_DIM), lambda b, h, i: (b, 0, N_HEADS_MEM + h))],
        out_specs=qs, out_shape=jax.ShapeDtypeStruct(qm.shape, BF16),
        compiler_params=_params(("parallel", "parallel", "parallel")),
    )(qm, kvm, kvm)


def _mem_bwd(qm, kvm, dom, *, tt=2048):
    b_dim, s_dim, _ = qm.shape
    n_mem = kvm.shape[1]
    scale = MEM_HEAD_DIM ** -0.5

    def body(q_ref, k_ref, v_ref, do_ref, dq_ref, dk_ref, dv_ref):
        qv, kv, vv, dov = q_ref[0], k_ref[0], v_ref[0], do_ref[0]
        sc = lax.dot_general(qv, kv, NT, preferred_element_type=F32) * scale
        p = jnp.exp(sc - jnp.max(sc, axis=-1, keepdims=True))
        p = p / jnp.sum(p, axis=-1, keepdims=True)
        dp = lax.dot_general(dov, vv, NT, preferred_element_type=F32)
        ds = (p * (dp - jnp.sum(p * dp, axis=-1, keepdims=True)) * scale).astype(BF16)
        dq_ref[0] = jnp.dot(ds, kv, preferred_element_type=F32).astype(BF16)

        @pl.when(pl.program_id(2) == 0)
        def _():
            dk_ref[...] = jnp.zeros_like(dk_ref)
            dv_ref[...] = jnp.zeros_like(dv_ref)

        dk_ref[0] += lax.dot_general(ds, qv, TN, preferred_element_type=F32)
        dv_ref[0] += lax.dot_general(p.astype(BF16), dov, TN, preferred_element_type=F32)

    qs = pl.BlockSpec((1, tt, MEM_HEAD_DIM), lambda b, h, i: (b, i, h))
    ks = pl.BlockSpec((1, n_mem, MEM_HEAD_DIM), lambda b, h, i: (b, 0, h))
    vs = pl.BlockSpec((1, n_mem, MEM_HEAD_DIM), lambda b, h, i: (b, 0, N_HEADS_MEM + h))
    return pl.pallas_call(
        body, name="mem_bwd", grid=(b_dim, N_HEADS_MEM, s_dim // tt),
        in_specs=[qs, ks, vs, qs], out_specs=[qs, ks, ks],
        out_shape=[jax.ShapeDtypeStruct(qm.shape, BF16), jax.ShapeDtypeStruct((b_dim, n_mem, MEM_WIDTH), F32),
                   jax.ShapeDtypeStruct((b_dim, n_mem, MEM_WIDTH), F32)],
        compiler_params=_params(("parallel", "parallel", "arbitrary")),
    )(qm, kvm, kvm, dom)


def _adamw_math(wv, gv, mv, vv):
    nm = ADAM_B1 * mv + (1.0 - ADAM_B1) * gv
    nv = ADAM_B2 * vv + (1.0 - ADAM_B2) * (gv * gv)
    m_hat = nm / (1.0 - ADAM_B1 ** ADAM_STEP)
    v_hat = nv / (1.0 - ADAM_B2 ** ADAM_STEP)
    return -ADAM_LR * (m_hat / (jnp.sqrt(v_hat) + ADAM_EPS) + ADAM_WD * wv), nm, nv


def _adamw(w, g, m, v, *, name):
    rows, cols = w.shape
    tr = _tile(rows, 256, 8)

    def body(w_ref, g_ref, m_ref, v_ref, d_ref, nm_ref, nv_ref):
        d_ref[...], nm_ref[...], nv_ref[...] = _adamw_math(w_ref[...], g_ref[...], m_ref[...], v_ref[...])

    spec = pl.BlockSpec((tr, cols), lambda i: (i, 0))
    shp = jax.ShapeDtypeStruct((rows, cols), F32)
    return pl.pallas_call(
        body, name=name, grid=(rows // tr,),
        in_specs=[spec] * 4, out_specs=[spec] * 3, out_shape=[shp] * 3,
        compiler_params=_params(("parallel",)),
    )(w, g, m, v)


def _prefetch_spec(grid, in_specs, out_specs):
    return pltpu.PrefetchScalarGridSpec(num_scalar_prefetch=1, grid=grid, in_specs=in_specs, out_specs=out_specs)


def _adamw_halves(w, mine, theirs, m, v, c_idx, *, name):
    rows, cols = w.shape
    half = rows // 2
    tr = _tile(half, _row_cap(cols), 8)
    nh = half // tr

    def body(c_ref, w_ref, mine_ref, theirs_ref, m_ref, v_ref, g_ref, d_ref, nm_ref, nv_ref):
        gv = jnp.where(pl.program_id(0) == c_ref[0], mine_ref[...], theirs_ref[...])
        g_ref[...] = gv
        d_ref[...], nm_ref[...], nv_ref[...] = _adamw_math(w_ref[...], gv, m_ref[...], v_ref[...])

    full = pl.BlockSpec((tr, cols), lambda h, i, c_ref: (h * nh + i, 0))
    part = pl.BlockSpec((tr, cols), lambda h, i, c_ref: (i, 0))
    shp = jax.ShapeDtypeStruct((rows, cols), F32)
    return pl.pallas_call(
        body, name=name, grid_spec=_prefetch_spec((2, nh), [full, part, part, full, full], [full] * 4),
        out_shape=[shp] * 4,
        compiler_params=_params(("parallel", "parallel")),
    )(c_idx, w, mine, theirs, m, v)


def _pair_sum(g3, theirs, c_idx, *, name):
    n, rows, cols = g3.shape
    half = rows // 2
    tr = _tile(half, _row_cap(cols), 16)

    def body(c_ref, g_ref, t_ref, o_ref):
        o_ref[...] = (g_ref[...] + t_ref[...]).astype(BF16)

    part = pl.BlockSpec((None, tr, cols), lambda s, i, c_ref: (s, i, 0))
    return pl.pallas_call(
        body, name=name,
        grid_spec=_prefetch_spec((n, half // tr),
                                 [pl.BlockSpec((None, None, tr, cols), lambda s, i, c_ref: (s, c_ref[0], i, 0)), part],
                                 part),
        out_shape=jax.ShapeDtypeStruct((n, half, cols), BF16),
        compiler_params=_params(("parallel", "parallel")),
    )(c_idx, g3.reshape(n, 2, half, cols), theirs)


def _chip_sum(pair, recv, s_idx, *, name):
    _, half, cols = pair.shape
    tr = _tile(half, _row_cap(cols), 16)

    def body(s_ref, p_ref, r_ref, o_ref):
        o_ref[...] = ((p_ref[...].astype(F32) + r_ref[0].astype(F32)) + r_ref[1].astype(F32)) + r_ref[2].astype(F32)

    return pl.pallas_call(
        body, name=name,
        grid_spec=_prefetch_spec((half // tr,),
                                 [pl.BlockSpec((None, tr, cols), lambda i, s_ref: (s_ref[0], i, 0)),
                                  pl.BlockSpec((N_CHIPS - 1, tr, cols), lambda i, s_ref: (0, i, 0))],
                                 pl.BlockSpec((tr, cols), lambda i, s_ref: (i, 0))),
        out_shape=jax.ShapeDtypeStruct((half, cols), F32),
        compiler_params=_params(("parallel",)),
    )(s_idx, pair, recv)


def _sum8(parts):
    n, rows, cols = parts.shape

    def body(p_ref, o_ref):
        acc = p_ref[0]
        for i in range(1, n):
            acc = acc + p_ref[i]
        o_ref[...] = acc

    return pl.pallas_call(
        body, name="small_sum", grid=(1,),
        in_specs=[pl.BlockSpec((n, rows, cols), lambda i: (0, 0, 0))],
        out_specs=pl.BlockSpec((rows, cols), lambda i: (0, 0)),
        out_shape=jax.ShapeDtypeStruct((rows, cols), parts.dtype),
        compiler_params=_params(("arbitrary",)),
    )(parts)


def _place():
    return lax.axis_index("x"), lax.axis_index("y"), lax.axis_index("c")


ANY = pl.BlockSpec(memory_space=pl.ANY)


def _rider_parts(rider):
    if rider is None:
        return (), [], []
    kind, arrays = rider
    n = len(arrays)
    shapes = {"gather": _gathered_shapes, "pair": _pair_shapes, "chip": _chip_shapes}[kind](arrays)
    sems = _gather_sems(n) if kind == "gather" else _exchange_sems(n if kind == "pair" else 3 * n)
    return tuple(arrays), shapes, sems


def _rider_hooks(rider, ins, outs, sems, step, n_steps):
    if rider is None:
        return (lambda: None), (lambda: None)
    if rider[0] == "gather":
        start, forward, finish = _gather_steps(ins, outs, *sems)
    else:
        start, finish = {"pair": _pair_steps, "chip": _chip_steps}[rider[0]](ins, outs, *sems)
        forward = None

    def begin():
        pl.when(step == 0)(start)

    def end():
        if forward is not None:
            pl.when(step == n_steps - 2)(forward)
        pl.when(step == n_steps - 1)(finish)

    return begin, end


def _gathered_shapes(shards):
    return [jax.ShapeDtypeStruct((N_CHIPS,) + s.shape, s.dtype) for s in shards]


def _gather_sems(n):
    return [pltpu.SemaphoreType.DMA((7 * n,)), pltpu.SemaphoreType.DMA((7 * n,))]


def _gather_steps(ins, outs, send_sems, recv_sems):
    n = len(ins)
    halves = [r.shape[0] // 2 for r in ins]
    x, y, c = _place()
    my_chip = 2 * x + y
    me, sibling = (x, y, c), (x, y, 1 - c)
    chips = [(1 - x, y), (x, 1 - y), (1 - x, 1 - y)]

    def half_of(w, chip, pc):
        return outs[w].at[chip, pl.ds(pc * halves[w], halves[w]), :]

    def copy(w, k, src, dst, to):
        return pltpu.make_async_remote_copy(
            src_ref=src, dst_ref=dst, send_sem=send_sems.at[7 * w + k], recv_sem=recv_sems.at[7 * w + k],
            device_id=to, device_id_type=MESH)

    def firsts():
        cps = []
        for w in range(n):
            cps.append(copy(w, 0, ins[w], outs[w].at[my_chip], sibling))
            mine = ins[w].at[pl.ds(c * halves[w], halves[w]), :]
            for j, (px, py) in enumerate(chips):
                cps.append(copy(w, 1 + j, mine, half_of(w, my_chip, c), (px, py, c)))
        return cps

    def passes():
        return [copy(w, 4 + j, half_of(w, 2 * px + py, c), half_of(w, 2 * px + py, c), sibling)
                for w in range(n) for j, (px, py) in enumerate(chips)]

    def start():
        for cp in firsts():
            cp.start()

    def forward():
        fws = passes()
        for w in range(n):
            for j, (px, py) in enumerate(chips):
                landed = half_of(w, 2 * px + py, c)
                copy(w, 1 + j, landed, landed, me).wait_recv()
                fws[3 * w + j].start()

    def finish():
        for w in range(n):
            copy(w, 0, ins[w], outs[w].at[my_chip], me).wait_recv()
            for j, (px, py) in enumerate(chips):
                landed = half_of(w, 2 * px + py, 1 - c)
                copy(w, 4 + j, landed, landed, me).wait_recv()
        for cp in firsts() + passes():
            cp.wait_send()

    return start, forward, finish


def _pair_shapes(grads):
    return [jax.ShapeDtypeStruct((g.shape[0], g.shape[1] // 2, g.shape[2]), g.dtype) for g in grads]


def _exchange_sems(n):
    return [pltpu.SemaphoreType.DMA((n,)), pltpu.SemaphoreType.DMA((n,))]


def _exchange_steps(copies):
    def start():
        for cp in copies():
            cp.start()

    def finish():
        for cp in copies():
            cp.wait()

    return start, finish


def _pair_steps(ins, outs, send_sems, recv_sems):
    x, y, c = _place()

    def copies():
        return [pltpu.make_async_remote_copy(
            src_ref=ins[w].at[:, pl.ds((1 - c) * (ins[w].shape[1] // 2), ins[w].shape[1] // 2), :], dst_ref=outs[w],
            send_sem=send_sems.at[w], recv_sem=recv_sems.at[w], device_id=(x, y, 1 - c), device_id_type=MESH)
            for w in range(len(ins))]

    return _exchange_steps(copies)


def _chip_shapes(pairs):
    return [jax.ShapeDtypeStruct((N_CHIPS - 1,) + p.shape[1:], p.dtype) for p in pairs]


def _chip_steps(ins, outs, send_sems, recv_sems):
    x, y, c = _place()
    others = [(1 - x, y), (x, 1 - y), (1 - x, 1 - y)]

    def copies():
        return [pltpu.make_async_remote_copy(
            src_ref=ins[w].at[2 * px + py], dst_ref=outs[w].at[j],
            send_sem=send_sems.at[3 * w + j], recv_sem=recv_sems.at[3 * w + j],
            device_id=(px, py, c), device_id_type=MESH)
            for w in range(len(ins)) for j, (px, py) in enumerate(others)]

    return _exchange_steps(copies)


def _swap_halves(mine):
    n = len(mine)

    def body(*refs):
        ins, outs, send_sems, recv_sems = refs[:n], refs[n:2 * n], refs[2 * n], refs[2 * n + 1]
        x, y, c = _place()
        copies = [pltpu.make_async_remote_copy(
            src_ref=ins[w], dst_ref=outs[w], send_sem=send_sems.at[w], recv_sem=recv_sems.at[w],
            device_id=(x, y, 1 - c), device_id_type=MESH) for w in range(n)]
        for cp in copies:
            cp.start()
        for cp in copies:
            cp.wait()

    return pl.pallas_call(
        body, name="grad_swap_halves",
        out_shape=[jax.ShapeDtypeStruct(h.shape, h.dtype) for h in mine],
        in_specs=[ANY] * n, out_specs=[ANY] * n,
        scratch_shapes=[pltpu.SemaphoreType.DMA((n,)), pltpu.SemaphoreType.DMA((n,))],
    )(*mine)


def _gather_small(small):
    srows, cols = small.shape

    def body(s_ref, all_ref, send_sems, recv_sems, local_sem):
        x, y, c = _place()
        me = 4 * x + 2 * y + c
        keep_small = pltpu.make_async_copy(s_ref, all_ref.at[me], local_sem)
        keep_small.start()
        sends = []
        for kk in range(1, 8):
            peer = (x ^ (kk >> 2), y ^ ((kk >> 1) & 1), c ^ (kk & 1))
            sends.append(pltpu.make_async_remote_copy(
                src_ref=s_ref, dst_ref=all_ref.at[me],
                send_sem=send_sems.at[kk], recv_sem=recv_sems.at[kk], device_id=peer, device_id_type=MESH))
        for cp in sends:
            cp.start()
        for kk in range(1, 8):
            px, py, pc = x ^ (kk >> 2), y ^ ((kk >> 1) & 1), c ^ (kk & 1)
            pltpu.make_async_remote_copy(
                src_ref=s_ref, dst_ref=all_ref.at[4 * px + 2 * py + pc],
                send_sem=send_sems.at[kk], recv_sem=recv_sems.at[kk], device_id=(px, py, pc),
                device_id_type=MESH).wait_recv()
        for cp in sends:
            cp.wait_send()
        keep_small.wait()

    return pl.pallas_call(
        body, name="gather_small",
        out_shape=jax.ShapeDtypeStruct((8, srows, cols), small.dtype),
        in_specs=[ANY], out_specs=ANY,
        scratch_shapes=[pltpu.SemaphoreType.DMA((8,)), pltpu.SemaphoreType.DMA((8,)), pltpu.SemaphoreType.DMA],
    )(small)


SHARDED = (("w_in", D_MODEL, IN_COLS, 1), ("w_up_a", ATT_WIDTH, D_MODEL, 1), ("w_up_b", ATT_WIDTH, D_MODEL, 1),
           ("w_out", D_MODEL, D_MODEL, 0), ("w_q_mem", D_MODEL, MEM_WIDTH, 0), ("w_kv_mem", D_MODEL, 2 * MEM_WIDTH, 0),
           ("w_o_mem", MEM_WIDTH, D_MODEL, 1), ("w_ffn_gate", D_FF, D_MODEL, 0), ("w_ffn_up", D_FF, D_MODEL, 0),
           ("w_ffn_down", D_FF, D_MODEL, 0))
TRANSPOSED = ("w_ffn_gate", "w_ffn_up")
NAMES = tuple(n for n, _, _, _ in SHARDED)


def _held(name, shard):
    return shard.T if name in TRANSPOSED else shard
EARLY, LATE = NAMES[:1], NAMES[1:]
GAINS = ("g_mix", "g_mem_q", "g_mem_kv", "g_ffn", "g_final")


def _natural(w3):
    n, r, c = w3.shape
    return w3.reshape(n * r, c)


def _shard_major(g, axis):
    if axis == 1:
        return g
    r, c = g.shape
    return g.reshape(N_CHIPS, r // N_CHIPS, c)


def kernel(x, mem, positions, g_mix, w_in, w_up_a, w_up_b, w_out, g_mem_q, g_mem_kv, w_q_mem, w_kv_mem, w_o_mem, g_ffn, w_ffn_gate, w_ffn_up, w_ffn_down, g_final, loss_target, m_g_mix, m_w_in, m_w_up_a, m_w_up_b, m_w_out, m_g_mem_q, m_g_mem_kv, m_w_q_mem, m_w_kv_mem, m_w_o_mem, m_g_ffn, m_w_ffn_gate, m_w_ffn_up, m_w_ffn_down, m_g_final, v_g_mix, v_w_in, v_w_up_a, v_w_up_b, v_w_out, v_g_mem_q, v_g_mem_kv, v_w_q_mem, v_w_kv_mem, v_w_o_mem, v_g_ffn, v_w_ffn_gate, v_w_ffn_up, v_w_ffn_down, v_g_final):
    given = dict(locals())
    shards = {n: _held(n, given[n][0]) for n in NAMES}

    early_shards = [shards[n].astype(BF16) for n in EARLY]
    late_shards = [shards[n].astype(BF16) for n in LATE]
    c_idx = lax.axis_index("c").astype(jnp.int32).reshape(1)
    s_idx = (2 * lax.axis_index("x") + lax.axis_index("y")).astype(jnp.int32).reshape(1)

    loss_row, grad_x, mine, gain_grads = _local_step(x, mem, positions, loss_target, g_mix, g_mem_q, g_mem_kv,
                                                     g_ffn, g_final, {}, early_shards, late_shards, (c_idx, s_idx))
    return _reduce_and_update(given, shards, loss_row, grad_x, mine, gain_grads, c_idx)


def _reduce_halves(glist, names, c_idx, s_idx, pair_exchange, chip_exchange):
    theirs = pair_exchange(glist)
    pairs = [_pair_sum(g, t, c_idx, name="pair_sum_" + n) for n, g, t in zip(names, glist, theirs)]
    recv = chip_exchange(pairs)
    return [_chip_sum(p, r, s_idx, name="chip_sum_" + n) for n, p, r in zip(names, pairs, recv)]


def _local_step(x, mem, positions, loss_target, g_mix, g_mem_q, g_mem_kv, g_ffn, g_final, wf,
                early_shards=None, late_shards=None, place=None):
    b_dim, s_dim, d = x.shape
    t_dim = b_dim * s_dim
    n_mem = mem.shape[1]
    wf = dict(wf)

    xb = x.reshape(t_dim, d)
    tgt = loss_target.reshape(t_dim, d)
    memf = mem.reshape(b_dim * n_mem, d)
    gfin = g_final.reshape(1, d)
    pos = positions.reshape(t_dim, 1).astype(F32)

    lane = jnp.arange(LANES) % HEAD_DIM
    half = ROPE_DIM // 2
    inv_freq = ROPE_THETA ** (-jnp.arange(half, dtype=F32) / half)
    inv_lane = jnp.where(lane < ROPE_DIM, inv_freq[lane % half], 0.0).reshape(1, -1).astype(F32)
    sel_a = (lane < half).astype(F32).reshape(1, -1)
    sel_b = ((lane >= half) & (lane < ROPE_DIM)).astype(F32).reshape(1, -1)

    def rows3(t):
        return t.reshape(b_dim, s_dim, t.shape[-1])

    def rows2(t):
        return t.reshape(t_dim, t.shape[-1])

    if early_shards:
        n1, gathered = _rms_fwd(xb, g_mix, name="rms_mix", rider=("gather", early_shards))
        wf.update(zip(EARLY, gathered))
    else:
        n1 = _rms_fwd(xb, g_mix, name="rms_mix")
    proj = _mm_cs(n1, wf["w_in"], name="mm_in")
    proj3 = rows3(proj)
    cs, sn = _rope_table(pos, inv_lane, sel_a, sel_b)
    cs3, sn3 = rows3(cs), rows3(sn)
    (oa16, oa32, lse_a), _ = _dil_fwd(proj3, cs3, sn3, sel_a, sel_b)
    ob16, gathered = _sb_fwd(proj3, ("gather", late_shards) if late_shards else None)
    wf.update(zip(LATE, gathered))
    w_out, w_q, w_kv = _natural(wf["w_out"]), _natural(wf["w_q_mem"]), _natural(wf["w_kv_mem"])
    oa, ob = rows2(oa16), rows2(ob16)
    ua = _mm_sm(oa, wf["w_up_a"], name="mm_up_a")
    ub = _mm_sm(ob, wf["w_up_b"], name="mm_up_b")
    mixed, h1, hn = _gate_out_norm(proj, ua, ub, w_out, xb, g_mem_q)

    memn = _rms_fwd(memf, g_mem_kv, name="rms_mem_kv")
    qm = _mm(hn, w_q, name="mm_q_mem", out_dtype=BF16)
    kvm = _mm(memn, w_kv, name="mm_kv_mem", out_dtype=BF16)
    qm3, kvm3 = rows3(qm), kvm.reshape(b_dim, n_mem, 2 * MEM_WIDTH)
    om = rows2(_mem_fwd(qm3, kvm3))
    h2 = _mm_sm(om, wf["w_o_mem"], name="mm_o_mem", add=h1)

    n3 = _rms_fwd(h2, g_ffn, name="rms_ffn")
    gate3, up3, act3 = _ffn_up_swiglu(n3, wf["w_ffn_gate"], wf["w_ffn_up"])
    loss_row, dh3, dg_final = _down_final(act3, wf["w_ffn_down"], h2, gfin, tgt)

    grads = {}
    grads["w_ffn_down"] = _mm_ffn_down_dw(act3, dh3, name="mm_down_dw")
    dgate3, dup3 = _ffn_down_dx_swiglu(dh3, wf["w_ffn_down"], gate3, up3)
    grads["w_ffn_gate"] = _mm_ffn_down_dw(dgate3, n3, name="mm_gate_dw")
    grads["w_ffn_up"] = _mm_ffn_down_dw(dup3, n3, name="mm_up_dw")
    dn3 = _ffn_up_dx(dgate3, dup3, wf["w_ffn_gate"], wf["w_ffn_up"])
    dh2, dg_ffn = _rms_bwd(h2, g_ffn, dn3, dh3, name="rms_ffn_bwd")

    dom = _mm_sm_dx(dh2, wf["w_o_mem"], name="mm_o_mem_dx", out_dtype=BF16)
    grads["w_o_mem"] = _mm_sm_dw(om, dh2, name="mm_o_mem_dw")
    dqm, dkm, dvm = _mem_bwd(qm3, kvm3, rows3(dom))
    dqm = rows2(dqm)
    dkvm = jnp.concatenate([dkm, dvm], axis=-1).reshape(b_dim * n_mem, 2 * MEM_WIDTH).astype(BF16)
    grads["w_q_mem"] = _shard_major(_mm(hn, dqm, name="mm_q_mem_dw", ta=True), 0)
    dhn = _mm(dqm, w_q, name="mm_q_mem_dx", tb=True)
    grads["w_kv_mem"] = _shard_major(_mm(memn, dkvm, name="mm_kv_mem_dw", ta=True), 0)
    dmemn = _mm(dkvm, w_kv, name="mm_kv_mem_dx", tb=True)
    _, dg_mem_kv = _rms_bwd(memf, g_mem_kv, dmemn, None, name="rms_mem_kv_bwd")
    dh1, dg_mem_q = _rms_bwd(h1, g_mem_q, dhn, dh2, name="rms_mem_q_bwd")

    grads["w_out"] = _shard_major(_mm(mixed, dh1, name="mm_out_dw", ta=True), 0)
    dua, dub, dgates = _out_dx_gate_bwd(dh1, w_out, proj, ua, ub)
    doa = _mm_sm_dx(dua, wf["w_up_a"], name="mm_up_a_dx")
    grads["w_up_a"] = _mm_sm_dw(oa, dua, name="mm_up_a_dw")
    dob = _mm_sm_dx(dub, wf["w_up_b"], name="mm_up_b_dx", out_dtype=BF16)
    grads["w_up_b"] = _mm_sm_dw(ob, dub, name="mm_up_b_dw")

    att = {}

    def dil_with_pairs(glist):
        att["a"], theirs = _dil_bwd(proj3, cs3, sn3, sel_a, sel_b, rows3(doa), oa32, lse_a,
                                    ("pair", glist) if glist else None)
        return theirs

    def sb_with_chips(pairs):
        att["b"], recv = _sb_bwd(proj3, rows3(dob), ("chip", pairs) if pairs else None)
        return recv

    if place is None:
        dil_with_pairs(())
        sb_with_chips(())
    else:
        mine_late = _reduce_halves([grads[n] for n in LATE], LATE, *place, dil_with_pairs, sb_with_chips)
    dproj = jnp.concatenate([rows2(t) for t in att["a"] + att["b"]] + [dgates], axis=1)
    grads["w_in"] = _mm_cs_dw(n1, dproj, name="mm_in_dw")
    if place is None:
        dn1 = _mm_cs_dx(dproj, wf["w_in"], name="mm_in_dx")
        dx, dg_mix = _rms_bwd(xb, g_mix, dn1, dh1, name="rms_mix_bwd")
    else:
        tail = {}

        def dx_with_pairs(glist):
            tail["dn1"], theirs = _mm_cs_dx(dproj, wf["w_in"], name="mm_in_dx", rider=("pair", glist))
            return theirs

        def rms_with_chips(pairs):
            tail["dx"], tail["dg"], recv = _rms_bwd(xb, g_mix, tail["dn1"], dh1, name="rms_mix_bwd",
                                                    rider=("chip", pairs))
            return recv

        mine_early = _reduce_halves([grads[n] for n in EARLY], EARLY, *place, dx_with_pairs, rms_with_chips)
        dx, dg_mix = tail["dx"], tail["dg"]
    grad_x = dx.reshape(b_dim, s_dim, d)
    gains = (dg_mix, dg_mem_q, dg_mem_kv, dg_ffn, dg_final)
    if place is None:
        return loss_row, grad_x, grads, gains
    return loss_row, grad_x, mine_early + mine_late, gains


def _reduce_and_update(given, shards, loss_row, grad_x, mine, gain_grads, c_idx):
    d = D_MODEL
    dg_mix, dg_mem_q, dg_mem_kv, dg_ffn, dg_final = gain_grads
    small = jnp.concatenate([dg_mix, dg_mem_q, dg_mem_kv, dg_ffn, dg_final,
                             jnp.pad(loss_row, ((0, 0), (0, FLAT_COLS - LANES))), jnp.zeros((2, FLAT_COLS), F32)], axis=0)
    small_all = _gather_small(small)
    others = _swap_halves(mine)
    small_sum = _sum8(small_all)
    loss = small_sum[5, 0]

    out_g, out_d, out_m, out_v = {}, {}, {}, {}
    for n, mine_n, other_n in zip(NAMES, mine, others):
        res = _adamw_halves(shards[n], mine_n, other_n, _held(n, given["m_" + n][0]), _held(n, given["v_" + n][0]),
                            c_idx, name="adamw_" + n)
        out_g[n], out_d[n], out_m[n], out_v[n] = [_held(n, r)[None] for r in res]
    gain_w = jnp.concatenate([given[n].reshape(1, d) for n in GAINS], axis=0)
    gain_m = jnp.concatenate([given["m_" + n].reshape(1, d) for n in GAINS], axis=0)
    gain_v = jnp.concatenate([given["v_" + n].reshape(1, d) for n in GAINS], axis=0)
    gain_g = small_sum[:len(GAINS)]
    gd, gm, gv = _adamw(gain_w, gain_g, gain_m, gain_v, name="adamw_gains")
    for i, n in enumerate(GAINS):
        shape = given[n].shape
        out_g[n], out_d[n] = gain_g[i].reshape(shape), gd[i].reshape(shape)
        out_m[n], out_v[n] = gm[i].reshape(shape), gv[i].reshape(shape)

    order = ["g_mix", "w_in", "w_up_a", "w_up_b", "w_out", "g_mem_q", "g_mem_kv", "w_q_mem", "w_kv_mem", "w_o_mem",
             "g_ffn", "w_ffn_gate", "w_ffn_up", "w_ffn_down", "g_final"]
    return (loss, grad_x, *[out_g[n] for n in order], *[out_d[n] for n in order],
            *[out_m[n] for n in order], *[out_v[n] for n in order])
```

```python
import jax
import jax.numpy as jnp
from jax import lax
from jax.experimental import pallas as pl
from jax.experimental.pallas import tpu as pltpu

F32 = jnp.float32
BF16 = jnp.bfloat16
MESH = pl.DeviceIdType.MESH

D_MODEL = 1024
HEAD_DIM = 64
N_HEADS = 8
ATT_WIDTH = N_HEADS * HEAD_DIM
DIL_PATTERNS = ((128, 1), (512, 4), (2048, 16))
BLOCK = 128
SB_ROWS = 1024
SB_STEP_FWD, SB_STEP_BWD = 8, 4
ROPE_THETA = 500000.0
ROPE_DIM = HEAD_DIM // 4
N_HEADS_MEM = 4
MEM_HEAD_DIM = 128
MEM_WIDTH = N_HEADS_MEM * MEM_HEAD_DIM
D_FF = 2816
IN_COLS = 6 * ATT_WIDTH + 2 * D_MODEL
RMS_EPS = 1e-6
ADAM_LR = 0.001
ADAM_B1 = 0.9
ADAM_B2 = 0.999
ADAM_EPS = 1e-08
ADAM_WD = 0.01
ADAM_STEP = 10

N_CHIPS = 4
LANES = 128
FLAT_COLS = 1024
VMEM_LIMIT = 56 * 1024 * 1024

PAIRS = ATT_WIDTH // LANES
COL_QA, COL_KA, COL_VA, COL_QB, COL_KB, COL_VB = (i * PAIRS for i in range(6))

MM_CAP = 1408
TOK_CAP = 2048
NN = (((1,), (0,)), ((), ()))
NT = (((1,), (1,)), ((), ()))
TN = (((0,), (0,)), ((), ()))
BNN = (((2,), (1,)), ((0,), (0,)))
BNT = (((2,), (2,)), ((0,), (0,)))
BTN = (((1,), (1,)), ((0,), (0,)))
DIL_BATCH = 16
DIL_CHUNK = 512


def _tile(dim, cap, unit=LANES):
    if dim <= cap:
        return dim
    best = None
    for t in range(unit, cap + 1, unit):
        if dim % t == 0:
            best = t
    assert best is not None, (dim, cap)
    return best


def _row_cap(cols):
    return max(256, (1 << 18) // cols)


def _params(sem):
    return pltpu.CompilerParams(dimension_semantics=sem, vmem_limit_bytes=VMEM_LIMIT)


def _mm(a, b, *, name, ta=False, tb=False, add=None, out_dtype=F32,
        tm_cap=MM_CAP, tn_cap=MM_CAP, tk_cap=MM_CAP):
    if ta:
        k_dim, m_dim = a.shape
    else:
        m_dim, k_dim = a.shape
    if tb:
        n_dim, kb = b.shape
    else:
        kb, n_dim = b.shape
    assert kb == k_dim, (a.shape, b.shape, ta, tb)
    tm, tn, tk = _tile(m_dim, tm_cap), _tile(n_dim, tn_cap), _tile(k_dim, tk_cap)
    nk = k_dim // tk
    dims = (((0 if ta else 1,), (1 if tb else 0,)), ((), ()))
    has_add = add is not None

    def body(*refs):
        if has_add:
            a_ref, b_ref, add_ref, o_ref = refs[:4]
        else:
            a_ref, b_ref, o_ref = refs[:3]
        part = lax.dot_general(a_ref[...].astype(BF16), b_ref[...].astype(BF16), dims, preferred_element_type=F32)

        def finish(r):
            if has_add:
                r = add_ref[...] + r
            o_ref[...] = r.astype(out_dtype)

        if nk == 1:
            finish(part)
            return
        acc_ref = refs[-1]
        k = pl.program_id(2)

        @pl.when(k == 0)
        def _():
            acc_ref[...] = part

        @pl.when(k > 0)
        def _():
            acc_ref[...] += part

        @pl.when(k == nk - 1)
        def _():
            finish(acc_ref[...])

    a_spec = pl.BlockSpec((tk, tm), lambda i, j, k: (k, i)) if ta else pl.BlockSpec((tm, tk), lambda i, j, k: (i, k))
    b_spec = pl.BlockSpec((tn, tk), lambda i, j, k: (j, k)) if tb else pl.BlockSpec((tk, tn), lambda i, j, k: (k, j))
    o_spec = pl.BlockSpec((tm, tn), lambda i, j, k: (i, j))
    in_specs = [a_spec, b_spec] + ([o_spec] if has_add else [])
    args = (a, b) + ((add,) if has_add else ())
    return pl.pallas_call(
        body, name=name, grid=(m_dim // tm, n_dim // tn, nk),
        in_specs=in_specs, out_specs=o_spec,
        out_shape=jax.ShapeDtypeStruct((m_dim, n_dim), out_dtype),
        scratch_shapes=[pltpu.VMEM((tm, tn), F32)] if nk > 1 else [],
        compiler_params=_params(("parallel", "parallel", "arbitrary")),
    )(*args)


def _mm_core(name, a, b, a_spec, b_spec, o_spec, out_shape, grid, dims, *, add=None, out_dtype=F32, rider=None):
    nk = grid[2]
    has_add = add is not None
    n_in = 3 if has_add else 2
    acc_shape = tuple(d for d in o_spec.block_shape if d is not None)
    extra, extra_shapes, extra_sems = _rider_parts(rider)
    n_w = len(extra)

    def body(*refs):
        a_ref, b_ref = refs[:2]
        o_ref = refs[n_in + n_w]
        step = (pl.program_id(0) * grid[1] + pl.program_id(1)) * nk + pl.program_id(2)
        begin, end = _rider_hooks(rider, refs[n_in:n_in + n_w], refs[n_in + n_w + 1:n_in + 2 * n_w + 1], refs[-2:],
                                  step, grid[0] * grid[1] * nk)
        begin()
        part = lax.dot_general(a_ref[...].astype(BF16), b_ref[...].astype(BF16), dims, preferred_element_type=F32)

        def finish(r):
            if has_add:
                r = refs[2][...] + r
            o_ref[...] = r.astype(out_dtype)

        if nk == 1:
            finish(part)
        else:
            acc_ref = refs[n_in + 2 * n_w + 1]
            k = pl.program_id(2)

            @pl.when(k == 0)
            def _():
                acc_ref[...] = part

            @pl.when(k > 0)
            def _():
                acc_ref[...] += part

            @pl.when(k == nk - 1)
            def _():
                finish(acc_ref[...])
        end()

    in_specs = [a_spec, b_spec] + ([o_spec] if has_add else []) + [ANY] * n_w
    args = (a, b) + ((add,) if has_add else ()) + extra
    res = pl.pallas_call(
        body, name=name, grid=grid, in_specs=in_specs, out_specs=[o_spec] + [ANY] * n_w,
        out_shape=[jax.ShapeDtypeStruct(out_shape, out_dtype)] + extra_shapes,
        scratch_shapes=([pltpu.VMEM(acc_shape, F32)] if nk > 1 else []) + extra_sems,
        compiler_params=_params(("arbitrary",) * 3 if n_w else ("parallel", "parallel", "arbitrary")),
    )(*args)
    return (res[0], res[1:]) if n_w else res[0]


def _mm_cs(a, w3, *, name):
    m_dim, k_dim = a.shape
    _, _, n4 = w3.shape
    tm, tn, tk = _tile(m_dim, TOK_CAP), _tile(n4, MM_CAP), _tile(k_dim, MM_CAP)
    npb = n4 // tn
    return _mm_core(name, a, w3,
                    pl.BlockSpec((tm, tk), lambda i, j, k: (i, k)),
                    pl.BlockSpec((None, tk, tn), lambda i, j, k: (j // npb, k, j % npb)),
                    pl.BlockSpec((tm, tn), lambda i, j, k: (i, j)),
                    (m_dim, N_CHIPS * n4), (m_dim // tm, N_CHIPS * npb, k_dim // tk), NN)


def _mm_cs_dx(dy, w3, *, name, out_dtype=F32, rider=None):
    m_dim, _ = dy.shape
    _, k_dim, n4 = w3.shape
    tm, tkw, tn = _tile(m_dim, TOK_CAP), _tile(k_dim, MM_CAP), _tile(n4, MM_CAP)
    npb = n4 // tn
    return _mm_core(name, dy, w3,
                    pl.BlockSpec((tm, tn), lambda i, j, k: (i, k)),
                    pl.BlockSpec((None, tkw, tn), lambda i, j, k: (k // npb, j, k % npb)),
                    pl.BlockSpec((tm, tkw), lambda i, j, k: (i, j)),
                    (m_dim, k_dim), (m_dim // tm, k_dim // tkw, N_CHIPS * npb), NT, out_dtype=out_dtype, rider=rider)


def _mm_cs_dw(a, dy, *, name):
    m_dim, k_dim = a.shape
    n4 = dy.shape[1] // N_CHIPS
    tmk, tn, tk = _tile(k_dim, MM_CAP), _tile(n4, MM_CAP), _tile(m_dim, TOK_CAP)
    npb = n4 // tn
    return _mm_core(name, a, dy,
                    pl.BlockSpec((tk, tmk), lambda i, j, k: (k, i)),
                    pl.BlockSpec((tk, tn), lambda i, j, k: (k, j)),
                    pl.BlockSpec((None, tmk, tn), lambda i, j, k: (j // npb, i, j % npb)),
                    (N_CHIPS, k_dim, n4), (k_dim // tmk, N_CHIPS * npb, m_dim // tk), TN)


def _mm_sm(a, w3, *, name, add=None, out_dtype=F32, tt=1024):
    t_dim, k_dim = a.shape
    n_s, _, n4 = w3.shape
    has_add = add is not None

    def body(*refs):
        a_ref, w_ref, o_ref = refs[0], refs[1], refs[-1]
        av = a_ref[...].astype(BF16)
        for s in range(n_s):
            cols = slice(s * n4, (s + 1) * n4)
            r = jnp.dot(av, w_ref[s], preferred_element_type=F32)
            if has_add:
                r = refs[2][:, cols] + r
            o_ref[:, cols] = r.astype(out_dtype)

    row = pl.BlockSpec((tt, n_s * n4), lambda i: (i, 0))
    return pl.pallas_call(
        body, name=name, grid=(t_dim // tt,),
        in_specs=[pl.BlockSpec((tt, k_dim), lambda i: (i, 0)), pl.BlockSpec(w3.shape, lambda i: (0, 0, 0))]
        + ([row] if has_add else []),
        out_specs=row, out_shape=jax.ShapeDtypeStruct((t_dim, n_s * n4), out_dtype),
        compiler_params=_params(("parallel",)),
    )(*((a, w3) + ((add,) if has_add else ())))


def _mm_sm_dx(dy, w3, *, name, out_dtype=F32, tt=1024):
    t_dim, _ = dy.shape
    n_s, k_dim, n4 = w3.shape

    def body(dy_ref, w_ref, o_ref):
        dyv = dy_ref[...].astype(BF16)
        acc = lax.dot_general(dyv[:, :n4], w_ref[0], NT, preferred_element_type=F32)
        for s in range(1, n_s):
            acc = acc + lax.dot_general(dyv[:, s * n4:(s + 1) * n4], w_ref[s], NT, preferred_element_type=F32)
        o_ref[...] = acc.astype(out_dtype)

    return pl.pallas_call(
        body, name=name, grid=(t_dim // tt,),
        in_specs=[pl.BlockSpec((tt, n_s * n4), lambda i: (i, 0)), pl.BlockSpec(w3.shape, lambda i: (0, 0, 0))],
        out_specs=pl.BlockSpec((tt, k_dim), lambda i: (i, 0)),
        out_shape=jax.ShapeDtypeStruct((t_dim, k_dim), out_dtype),
        compiler_params=_params(("parallel",)),
    )(dy, w3)


def _mm_sm_dw(a, dy, *, name, tk=1024):
    t_dim, k_dim = a.shape
    n4 = dy.shape[1] // N_CHIPS

    def body(a_ref, dy_ref, o_ref):
        av = a_ref[...].astype(BF16)
        dyv = dy_ref[...].astype(BF16)

        @pl.when(pl.program_id(0) == 0)
        def _():
            o_ref[...] = jnp.zeros_like(o_ref)

        for s in range(N_CHIPS):
            o_ref[s] += lax.dot_general(av, dyv[:, s * n4:(s + 1) * n4], TN, preferred_element_type=F32)

    return pl.pallas_call(
        body, name=name, grid=(t_dim // tk,),
        in_specs=[pl.BlockSpec((tk, k_dim), lambda i: (i, 0)), pl.BlockSpec((tk, N_CHIPS * n4), lambda i: (i, 0))],
        out_specs=pl.BlockSpec((N_CHIPS, k_dim, n4), lambda i: (0, 0, 0)),
        out_shape=jax.ShapeDtypeStruct((N_CHIPS, k_dim, n4), F32),
        compiler_params=_params(("arbitrary",)),
    )(a, dy)


def _ffn_up_dx(dg3, du3, wg3, wu3, *, tt=512):
    n_s, t_dim, f4 = dg3.shape
    d = wg3.shape[2]

    def body(dg_ref, du_ref, wg_ref, wu_ref, o_ref):
        acc = None
        for s in range(n_s):
            part = (jnp.dot(dg_ref[s], wg_ref[s], preferred_element_type=F32)
                    + jnp.dot(du_ref[s], wu_ref[s], preferred_element_type=F32))
            acc = part if acc is None else acc + part
        o_ref[...] = acc

    a_spec = pl.BlockSpec((n_s, tt, f4), lambda i: (0, i, 0))
    w_spec = pl.BlockSpec(wg3.shape, lambda i: (0, 0, 0))
    return pl.pallas_call(
        body, name="ffn_up_dx", grid=(t_dim // tt,),
        in_specs=[a_spec, a_spec, w_spec, w_spec], out_specs=pl.BlockSpec((tt, d), lambda i: (i, 0)),
        out_shape=jax.ShapeDtypeStruct((t_dim, d), F32),
        compiler_params=_params(("parallel",)),
    )(dg3, du3, wg3, wu3)


def _mm_ffn_down_dw(act3, dh, *, name, tk=1024):
    n_s, t_dim, f4 = act3.shape
    d = dh.shape[1]

    def body(a_ref, b_ref, o_ref):
        bv = b_ref[...].astype(BF16)

        @pl.when(pl.program_id(0) == 0)
        def _():
            o_ref[...] = jnp.zeros_like(o_ref)

        for s in range(n_s):
            o_ref[s] += lax.dot_general(a_ref[s], bv, TN, preferred_element_type=F32)

    return pl.pallas_call(
        body, name=name, grid=(t_dim // tk,),
        in_specs=[pl.BlockSpec((n_s, tk, f4), lambda i: (0, i, 0)), pl.BlockSpec((tk, d), lambda i: (i, 0))],
        out_specs=pl.BlockSpec((n_s, f4, d), lambda i: (0, 0, 0)),
        out_shape=jax.ShapeDtypeStruct((n_s, f4, d), F32),
        compiler_params=_params(("arbitrary",)),
    )(act3, dh)


def _rms_fwd(x, g, *, name, tt=512, rider=None):
    t_dim, d = x.shape
    tt = _tile(t_dim, tt, 8)
    extra, extra_shapes, extra_sems = _rider_parts(rider)
    n_w = len(extra)

    def body(*refs):
        x_ref, g_ref, o_ref = refs[0], refs[1], refs[2 + n_w]
        begin, end = _rider_hooks(rider, refs[2:2 + n_w], refs[3 + n_w:3 + 2 * n_w], refs[-2:], pl.program_id(0),
                                  t_dim // tt)
        begin()
        xv = x_ref[...]
        r = lax.rsqrt(jnp.mean(xv * xv, axis=-1, keepdims=True) + RMS_EPS)
        o_ref[...] = ((xv * r) * g_ref[...]).astype(o_ref.dtype)
        end()

    res = pl.pallas_call(
        body, name=name, grid=(t_dim // tt,),
        in_specs=[pl.BlockSpec((tt, d), lambda i: (i, 0)), pl.BlockSpec((1, d), lambda i: (0, 0))] + [ANY] * n_w,
        out_specs=[pl.BlockSpec((tt, d), lambda i: (i, 0))] + [ANY] * n_w,
        out_shape=[jax.ShapeDtypeStruct((t_dim, d), BF16)] + extra_shapes,
        scratch_shapes=extra_sems,
        compiler_params=_params(("arbitrary",) if n_w else ("parallel",)),
    )(x, g, *extra)
    return (res[0], res[1:]) if n_w else res[0]


def _rms_bwd(x, g, dy, add, *, name, tt=512, rider=None):
    t_dim, d = x.shape
    tt = _tile(t_dim, tt, 8)
    has_add = add is not None
    n_in = 4 if has_add else 3
    extra, extra_shapes, extra_sems = _rider_parts(rider)
    n_w = len(extra)

    def body(*refs):
        x_ref, g_ref, dy_ref = refs[:3]
        add_ref = refs[3] if has_add else None
        dx_ref, dg_ref = refs[n_in + n_w:n_in + n_w + 2]
        begin, end = _rider_hooks(rider, refs[n_in:n_in + n_w], refs[n_in + n_w + 2:n_in + 2 * n_w + 2], refs[-2:],
                                  pl.program_id(0), t_dim // tt)
        begin()
        xv = x_ref[...]
        dyv = dy_ref[...].astype(F32)
        r = lax.rsqrt(jnp.mean(xv * xv, axis=-1, keepdims=True) + RMS_EPS)
        xh = xv * r
        u = dyv * g_ref[...]
        dx = r * (u - xh * jnp.mean(u * xh, axis=-1, keepdims=True))
        if has_add:
            dx = add_ref[...] + dx
        dx_ref[...] = dx

        @pl.when(pl.program_id(0) == 0)
        def _():
            dg_ref[...] = jnp.zeros_like(dg_ref)

        dg_ref[...] += jnp.sum(dyv * xh, axis=0, keepdims=True)
        end()

    row = pl.BlockSpec((tt, d), lambda i: (i, 0))
    vec = pl.BlockSpec((1, d), lambda i: (0, 0))
    in_specs = [row, vec, row] + ([row] if has_add else []) + [ANY] * n_w
    args = (x, g, dy) + ((add,) if has_add else ()) + extra
    res = pl.pallas_call(
        body, name=name, grid=(t_dim // tt,),
        in_specs=in_specs, out_specs=[row, vec] + [ANY] * n_w,
        out_shape=[jax.ShapeDtypeStruct((t_dim, d), F32), jax.ShapeDtypeStruct((1, d), F32)] + extra_shapes,
        scratch_shapes=extra_sems,
        compiler_params=_params(("arbitrary",)),
    )(*args)
    return (res[0], res[1], res[2:]) if n_w else (res[0], res[1])


def _down_final(act3, wd3, h, g, target, *, tt=512):
    n_s, t_dim, f4 = act3.shape
    d = h.shape[1]
    n_steps = t_dim // tt

    def body(a_ref, w_ref, h_ref, g_ref, t_ref, loss_ref, dh_ref, dg_ref, sq_ref):
        i = pl.program_id(0)
        xv = h_ref[...]
        for s in range(n_s):
            xv = xv + jnp.dot(a_ref[s], w_ref[s], preferred_element_type=F32)
        gv = g_ref[...]
        r = lax.rsqrt(jnp.mean(xv * xv, axis=-1, keepdims=True) + RMS_EPS)
        xh = xv * r
        err = xh * gv - t_ref[...]
        dyv = err * (1.0 / d)
        u = dyv * gv
        dh_ref[...] = r * (u - xh * jnp.mean(u * xh, axis=-1, keepdims=True))

        @pl.when(i == 0)
        def _():
            dg_ref[...] = jnp.zeros_like(dg_ref)
            sq_ref[...] = jnp.zeros_like(sq_ref)

        dg_ref[...] += jnp.sum(dyv * xh, axis=0, keepdims=True)
        sq_ref[...] += jnp.sum(err * err, axis=0, keepdims=True)

        @pl.when(i == n_steps - 1)
        def _():
            total = jnp.sum(sq_ref[...], axis=-1, keepdims=True) * (0.5 / d)
            loss_ref[...] = jnp.broadcast_to(total, loss_ref.shape)

    row = pl.BlockSpec((tt, d), lambda i: (i, 0))
    vec = pl.BlockSpec((1, d), lambda i: (0, 0))
    return pl.pallas_call(
        body, name="down_final_loss", grid=(n_steps,),
        in_specs=[pl.BlockSpec((n_s, tt, f4), lambda i: (0, i, 0)), pl.BlockSpec(wd3.shape, lambda i: (0, 0, 0)),
                  row, vec, row],
        out_specs=[pl.BlockSpec((1, LANES), lambda i: (0, 0)), row, vec],
        out_shape=[jax.ShapeDtypeStruct((1, LANES), F32), jax.ShapeDtypeStruct((t_dim, d), F32),
                   jax.ShapeDtypeStruct((1, d), F32)],
        scratch_shapes=[pltpu.VMEM((1, d), F32)],
        compiler_params=_params(("arbitrary",)),
    )(act3, wd3, h, g, target)


def _rope_table(pos, inv_lane, sel_a, sel_b, *, tt=512):
    t_dim = pos.shape[0]

    def body(p_ref, f_ref, a_ref, b_ref, c_ref, s_ref):
        ang = p_ref[...] * f_ref[...]
        on = (a_ref[...] + b_ref[...]) > 0.0
        c_ref[...] = jnp.where(on, jnp.cos(ang), 1.0)
        s_ref[...] = jnp.where(on, jnp.sin(ang), 0.0)

    vec = pl.BlockSpec((1, LANES), lambda i: (0, 0))
    row = pl.BlockSpec((tt, LANES), lambda i: (i, 0))
    shp = jax.ShapeDtypeStruct((t_dim, LANES), F32)
    return pl.pallas_call(
        body, name="rope_table", grid=(t_dim // tt,),
        in_specs=[pl.BlockSpec((tt, 1), lambda i: (i, 0)), vec, vec, vec],
        out_specs=[row, row], out_shape=[shp, shp],
        compiler_params=_params(("parallel",)),
    )(pos, inv_lane, sel_a, sel_b)


def _rotate(xv, cs, sn, sa, sb):
    half = ROPE_DIM // 2
    up = pltpu.roll(xv, LANES - half, 1)
    dn = pltpu.roll(xv, half, 1)
    return xv * cs + (dn * sb - up * sa) * sn


def _head_masks():
    h1 = lax.broadcasted_iota(jnp.int32, (1, LANES), 1) < HEAD_DIM
    return h1, jnp.logical_not(h1)


def _split_heads(xv, h1, h2):
    return jnp.where(h1, xv, 0.0).astype(BF16), jnp.where(h2, xv, 0.0).astype(BF16)


def _tri_masks():
    r = lax.broadcasted_iota(jnp.int32, (BLOCK, BLOCK), 0)
    c = lax.broadcasted_iota(jnp.int32, (BLOCK, BLOCK), 1)
    return c <= r, r <= c


def _stream_rows(start, dil):
    if dil == 1:
        return pl.ds(pl.multiple_of(start, BLOCK), BLOCK)
    return pl.ds(start, BLOCK, stride=dil)


def _dil_tile(idx, dil, nb):
    r = idx // nb
    n = idx % nb
    return (_stream_rows(r + dil * BLOCK * n, dil), _stream_rows(r + dil * BLOCK * jnp.maximum(n - 1, 0), dil),
            n > 0)


def _dil_specs(b_dim, s_dim):
    def col(c0):
        return pl.BlockSpec((None, s_dim, LANES),lambda b, h: (b, 0, c0 + h))
    tab = pl.BlockSpec((None, s_dim, LANES),lambda b, h: (b, 0, 0))
    vec = pl.BlockSpec((1, LANES), lambda b, h: (0, 0))
    return col, tab, vec


def _dil_fwd(proj3, cs3, sn3, sel_a, sel_b, rider=None):
    b_dim, s_dim, _ = proj3.shape
    scale = HEAD_DIM ** -0.5
    n_pat = len(DIL_PATTERNS)
    extra, extra_shapes, extra_sems = _rider_parts(rider)
    n_w = len(extra)
    n_steps = b_dim * PAIRS

    def body(*refs):
        q_ref, k_ref, v_ref, cs_ref, sn_ref, sa_ref, sb_ref = refs[:7]
        o16_ref, o32_ref, l_ref = refs[7 + n_w:10 + n_w]
        qr, kr = refs[10 + 2 * n_w:12 + 2 * n_w]
        per_pattern = refs[12 + 2 * n_w:12 + 2 * n_w + 2 * n_pat]
        og, lg = per_pattern[:n_pat], per_pattern[n_pat:]
        step = pl.program_id(0) * PAIRS + pl.program_id(1)
        begin, end = _rider_hooks(rider, refs[7:7 + n_w], refs[10 + n_w:10 + 2 * n_w], refs[-2:], step, n_steps)
        begin()
        h1, h2 = _head_masks()
        cur_ok, prev_ok = _tri_masks()
        sa, sb = sa_ref[...], sb_ref[...]

        def prep(j, _):
            rows = pl.ds(pl.multiple_of(j * DIL_CHUNK, DIL_CHUNK), DIL_CHUNK)
            cs, sn = cs_ref[rows, :], sn_ref[rows, :]
            qr[rows, :] = _rotate(q_ref[rows, :], cs, sn, sa, sb) * scale
            kr[rows, :] = _rotate(k_ref[rows, :], cs, sn, sa, sb)
            return 0

        lax.fori_loop(0, s_dim // DIL_CHUNK, prep, 0)

        for g, (_, dil) in enumerate(DIL_PATTERNS):
            nb = s_dim // dil // BLOCK

            def some(bi, _, g=g, dil=dil, nb=nb):
                tiles = [_dil_tile(bi * DIL_BATCH + t, dil, nb) for t in range(DIL_BATCH)]
                rows = [t[0] for t in tiles]
                q1, q2 = _split_heads(jnp.stack([qr[rw, :] for rw in rows]), h1, h2)
                kc = jnp.stack([kr[rw, :] for rw in rows]).astype(BF16)
                vc1, vc2 = _split_heads(jnp.stack([v_ref[rw, :] for rw in rows]), h1, h2)
                if nb > 1:
                    kp = jnp.stack([kr[t[1], :] for t in tiles]).astype(BF16)
                    vp1, vp2 = _split_heads(jnp.stack([v_ref[t[1], :] for t in tiles]), h1, h2)
                    p_ok = jnp.stack([jnp.logical_and(prev_ok, t[2]) for t in tiles])

                def head(qh, vch, vph):
                    sc = jnp.where(cur_ok, lax.dot_general(qh, kc, BNT, preferred_element_type=F32), -jnp.inf)
                    m = jnp.max(sc, axis=-1, keepdims=True)
                    if nb > 1:
                        sp = jnp.where(p_ok, lax.dot_general(qh, kp, BNT, preferred_element_type=F32), -jnp.inf)
                        m = jnp.maximum(m, jnp.max(sp, axis=-1, keepdims=True))
                    pc = jnp.exp(sc - m)
                    den = jnp.sum(pc, axis=-1, keepdims=True)
                    acc = lax.dot_general(pc.astype(BF16), vch, BNN, preferred_element_type=F32)
                    if nb > 1:
                        pp = jnp.exp(sp - m)
                        den = den + jnp.sum(pp, axis=-1, keepdims=True)
                        acc = acc + lax.dot_general(pp.astype(BF16), vph, BNN, preferred_element_type=F32)
                    return acc / den, m + jnp.log(den)

                o1, l1 = head(q1, vc1, vp1 if nb > 1 else None)
                o2, l2 = head(q2, vc2, vp2 if nb > 1 else None)
                o, l = o1 + o2, jnp.where(h1, l1, l2)
                for t, rw in enumerate(rows):
                    og[g][rw, :] = o[t]
                    lg[g][rw, :] = l[t]
                return 0

            lax.fori_loop(0, dil * nb // DIL_BATCH, some, 0)

        def comb(j, _):
            rows = pl.ds(pl.multiple_of(j * DIL_CHUNK, DIL_CHUNK), DIL_CHUNK)
            ls = [lg[g][rows, :] for g in range(n_pat)]
            m = jnp.maximum(jnp.maximum(ls[0], ls[1]), ls[2])
            es = [jnp.exp(l - m) for l in ls]
            den = es[0] + es[1] + es[2]
            o = (es[0] * og[0][rows, :] + es[1] * og[1][rows, :] + es[2] * og[2][rows, :]) / den
            o16_ref[rows, :] = o.astype(BF16)
            o32_ref[rows, :] = o
            l_ref[rows, :] = m + jnp.log(den)
            return 0

        lax.fori_loop(0, s_dim // DIL_CHUNK, comb, 0)
        end()

    col, tab, vec = _dil_specs(b_dim, s_dim)
    out = pl.BlockSpec((None, s_dim, LANES),lambda b, h: (b, 0, h))
    shp = (b_dim, s_dim, ATT_WIDTH)
    res = pl.pallas_call(
        body, name="dil_fwd", grid=(b_dim, PAIRS),
        in_specs=[col(COL_QA), col(COL_KA), col(COL_VA), tab, tab, vec, vec] + [ANY] * n_w,
        out_specs=[out, out, out] + [ANY] * n_w,
        out_shape=[jax.ShapeDtypeStruct(shp, BF16), jax.ShapeDtypeStruct(shp, F32), jax.ShapeDtypeStruct(shp, F32)]
        + extra_shapes,
        scratch_shapes=[pltpu.VMEM((s_dim, LANES), F32)] * (2 + 2 * n_pat) + extra_sems,
        compiler_params=_params(("arbitrary", "arbitrary")),
    )(proj3, proj3, proj3, cs3, sn3, sel_a, sel_b, *extra)
    return res[:3], res[3:]


def _dil_bwd(proj3, cs3, sn3, sel_a, sel_b, do3, o3, lse3, rider=None):
    b_dim, s_dim, _ = proj3.shape
    scale = HEAD_DIM ** -0.5
    extra, extra_shapes, extra_sems = _rider_parts(rider)
    n_w = len(extra)

    def body(*refs):
        q_ref, k_ref, v_ref, cs_ref, sn_ref, sa_ref, sb_ref, do_ref, o_ref, l_ref = refs[:10]
        dq_ref, dk_ref, dv_ref = refs[10 + n_w:13 + n_w]
        qr, kr, dqa, dka, dva = refs[13 + 2 * n_w:18 + 2 * n_w]
        step = pl.program_id(0) * PAIRS + pl.program_id(1)
        begin, end = _rider_hooks(rider, refs[10:10 + n_w], refs[13 + n_w:13 + 2 * n_w], refs[-2:], step,
                                  b_dim * PAIRS)
        begin()
        h1, h2 = _head_masks()
        cur_ok, prev_ok = _tri_masks()
        sa, sb = sa_ref[...], sb_ref[...]

        def prep(j, _):
            rows = pl.ds(pl.multiple_of(j * DIL_CHUNK, DIL_CHUNK), DIL_CHUNK)
            cs, sn = cs_ref[rows, :], sn_ref[rows, :]
            qr[rows, :] = _rotate(q_ref[rows, :], cs, sn, sa, sb) * scale
            kr[rows, :] = _rotate(k_ref[rows, :], cs, sn, sa, sb)
            zero = jnp.zeros((DIL_CHUNK, LANES), F32)
            dqa[rows, :] = zero
            dka[rows, :] = zero
            dva[rows, :] = zero
            return 0

        lax.fori_loop(0, s_dim // DIL_CHUNK, prep, 0)

        for _, dil in DIL_PATTERNS:
            nb = s_dim // dil // BLOCK

            def some(bi, _, dil=dil, nb=nb):
                tiles = [_dil_tile(bi * DIL_BATCH + t, dil, nb) for t in range(DIL_BATCH)]
                rows = [t[0] for t in tiles]
                q1, q2 = _split_heads(jnp.stack([qr[rw, :] for rw in rows]), h1, h2)
                dof = jnp.stack([do_ref[rw, :] for rw in rows])
                do1, do2 = _split_heads(dof, h1, h2)
                prod = dof * jnp.stack([o_ref[rw, :] for rw in rows])
                delta1 = jnp.sum(jnp.where(h1, prod, 0.0), axis=-1, keepdims=True)
                delta2 = jnp.sum(jnp.where(h2, prod, 0.0), axis=-1, keepdims=True)
                lt = jnp.stack([l_ref[rw, :] for rw in rows])
                lse1 = jnp.max(jnp.where(h1, lt, -jnp.inf), axis=-1, keepdims=True)
                lse2 = jnp.max(jnp.where(h2, lt, -jnp.inf), axis=-1, keepdims=True)

                def side(krows, ok):
                    kf = jnp.stack([kr[kw, :] for kw in krows])
                    k16 = kf.astype(BF16)
                    k1, k2 = _split_heads(kf, h1, h2)
                    v16 = jnp.stack([v_ref[kw, :] for kw in krows]).astype(BF16)

                    def head(qh, doh, lse, delta):
                        sc = lax.dot_general(qh, k16, BNT, preferred_element_type=F32)
                        p = jnp.where(ok, jnp.exp(sc - lse), 0.0)
                        dp = lax.dot_general(doh, v16, BNT, preferred_element_type=F32)
                        return p.astype(BF16), (p * (dp - delta)).astype(BF16)

                    p1, ds1 = head(q1, do1, lse1, delta1)
                    p2, ds2 = head(q2, do2, lse2, delta2)
                    dv = (lax.dot_general(p1, do1, BTN, preferred_element_type=F32)
                          + lax.dot_general(p2, do2, BTN, preferred_element_type=F32))
                    dk = (lax.dot_general(ds1, q1, BTN, preferred_element_type=F32)
                          + lax.dot_general(ds2, q2, BTN, preferred_element_type=F32))
                    for t, kw in enumerate(krows):
                        dva[kw, :] += dv[t]
                        dka[kw, :] += dk[t]
                    return (lax.dot_general(ds1, k1, BNN, preferred_element_type=F32)
                            + lax.dot_general(ds2, k2, BNN, preferred_element_type=F32))

                dq = side(rows, cur_ok)
                if nb > 1:
                    dq = dq + side([t[1] for t in tiles], jnp.stack([jnp.logical_and(prev_ok, t[2]) for t in tiles]))
                for t, rw in enumerate(rows):
                    dqa[rw, :] += dq[t] * scale
                return 0

            lax.fori_loop(0, dil * nb // DIL_BATCH, some, 0)

        def finish(j, _):
            rows = pl.ds(pl.multiple_of(j * DIL_CHUNK, DIL_CHUNK), DIL_CHUNK)
            cs, sn = cs_ref[rows, :], -sn_ref[rows, :]
            dq_ref[rows, :] = _rotate(dqa[rows, :], cs, sn, sa, sb).astype(BF16)
            dk_ref[rows, :] = _rotate(dka[rows, :], cs, sn, sa, sb).astype(BF16)
            dv_ref[rows, :] = dva[rows, :].astype(BF16)
            return 0

        lax.fori_loop(0, s_dim // DIL_CHUNK, finish, 0)
        end()

    col, tab, vec = _dil_specs(b_dim, s_dim)
    out = pl.BlockSpec((None, s_dim, LANES),lambda b, h: (b, 0, h))
    shp = jax.ShapeDtypeStruct((b_dim, s_dim, ATT_WIDTH), BF16)
    acc = pltpu.VMEM((s_dim, LANES), F32)
    res = pl.pallas_call(
        body, name="dil_bwd", grid=(b_dim, PAIRS),
        in_specs=[col(COL_QA), col(COL_KA), col(COL_VA), tab, tab, vec, vec, out, out, out] + [ANY] * n_w,
        out_specs=[out, out, out] + [ANY] * n_w, out_shape=[shp, shp, shp] + extra_shapes,
        scratch_shapes=[acc, acc, acc, acc, acc] + extra_sems,
        compiler_params=_params(("arbitrary", "arbitrary")),
    )(proj3, proj3, proj3, cs3, sn3, sel_a, sel_b, do3, o3, lse3, *extra)
    return res[:3], res[3:]


def _split_dot(x, tri):
    hi = x.astype(BF16)
    lo = (x - hi.astype(F32)).astype(BF16)
    return jnp.dot(hi, tri, preferred_element_type=F32) + jnp.dot(lo, tri, preferred_element_type=F32)


def _log_sigmoid(z):
    return jnp.minimum(z, 0.0) - jnp.log(1.0 + jnp.exp(-jnp.abs(z)))


def _sb_scores(qh, k16, valid):
    z = lax.dot_general(qh, k16, NT, preferred_element_type=F32)
    ls = _log_sigmoid(z)
    l1m = ls - z
    return ls, (l1m if valid is None else jnp.where(valid, l1m, 0.0))


def _sb_consts():
    r = lax.broadcasted_iota(jnp.int32, (BLOCK, BLOCK), 0)
    c = lax.broadcasted_iota(jnp.int32, (BLOCK, BLOCK), 1)
    after = (r > c).astype(BF16)
    before = (r < c).astype(BF16)
    qrow = lax.broadcasted_iota(jnp.int32, (SB_ROWS, BLOCK), 0)
    kcol = lax.broadcasted_iota(jnp.int32, (SB_ROWS, BLOCK), 1)
    return after, before, qrow, kcol


def _below(whole, lo, delta):
    if lo == 0:
        return whole + delta
    return whole + jnp.concatenate([jnp.zeros((lo,) + delta.shape[1:], delta.dtype), delta], axis=0)


def _pairs_loop(n_blocks, step, carry, per_iter):
    def several(i, c):
        for j in range(per_iter):
            c = step(per_iter * i + j, c)
        return c

    return lax.fori_loop(0, n_blocks // per_iter, several, carry)


def _sb_fwd(proj3, rider=None):
    b_dim, s_dim, _ = proj3.shape
    scale = HEAD_DIM ** -0.5
    per = SB_ROWS // BLOCK
    extra, extra_shapes, extra_sems = _rider_parts(rider)
    n_w = len(extra)

    def body(*refs):
        q_ref, k_ref, v_ref = refs[:3]
        o_ref = refs[3 + n_w]
        step = pl.program_id(0) * PAIRS + pl.program_id(1)
        begin, end = _rider_hooks(rider, refs[3:3 + n_w], refs[4 + n_w:4 + 2 * n_w], refs[-2:], step, b_dim * PAIRS)
        begin()
        h1, h2 = _head_masks()
        after, _, qrow, kcol = _sb_consts()

        def qloop(qi, _):
            rows = pl.ds(pl.multiple_of(qi * SB_ROWS, SB_ROWS), SB_ROWS)
            q1, q2 = _split_heads(q_ref[rows, :] * scale, h1, h2)
            first = qi * per

            def block(kb, carry, lo):
                acc, run1, run2 = carry
                krows = pl.ds(pl.multiple_of(kb * BLOCK, BLOCK), BLOCK)
                k16 = k_ref[krows, :].astype(BF16)
                v1, v2 = _split_heads(v_ref[krows, :], h1, h2)
                valid = None if lo is None else kcol[:SB_ROWS - lo] < qrow[:SB_ROWS - lo]
                lo = lo or 0

                def head(qh, vh, run):
                    ls, l1m = _sb_scores(qh[lo:], k16, valid)
                    a = jnp.exp(ls + _split_dot(l1m, after) + run[lo:])
                    if valid is not None:
                        a = jnp.where(valid, a, 0.0)
                    return (jnp.dot(a.astype(BF16), vh, preferred_element_type=F32),
                            _below(run, lo, jnp.sum(l1m, axis=-1, keepdims=True)))

                o1, run1 = head(q1, v1, run1)
                o2, run2 = head(q2, v2, run2)
                return _below(acc, lo, o1 + o2), run1, run2

            zcol = jnp.zeros((SB_ROWS, 1), F32)
            carry = (jnp.zeros((SB_ROWS, LANES), F32), zcol, zcol)
            for kl in reversed(range(per)):
                carry = block(first + kl, carry, kl * BLOCK)
            acc, _, _ = _pairs_loop(first, lambda i, c: block(first - 1 - i, c, None), carry, SB_STEP_FWD)
            o_ref[rows, :] = acc.astype(BF16)
            return 0

        lax.fori_loop(0, s_dim // SB_ROWS, qloop, 0)
        end()

    def col(c0):
        return pl.BlockSpec((None, s_dim, LANES),lambda b, h: (b, 0, c0 + h))

    res = pl.pallas_call(
        body, name="sb_fwd", grid=(b_dim, PAIRS),
        in_specs=[col(COL_QB), col(COL_KB), col(COL_VB)] + [ANY] * n_w, out_specs=[col(0)] + [ANY] * n_w,
        out_shape=[jax.ShapeDtypeStruct((b_dim, s_dim, ATT_WIDTH), BF16)] + extra_shapes,
        scratch_shapes=extra_sems,
        compiler_params=_params(("arbitrary", "arbitrary")),
    )(proj3, proj3, proj3, *extra)
    return res[0], res[1:]


def _sb_bwd(proj3, do3, rider=None):
    b_dim, s_dim, _ = proj3.shape
    scale = HEAD_DIM ** -0.5
    per = SB_ROWS // BLOCK
    nkb_max = s_dim // BLOCK
    extra, extra_shapes, extra_sems = _rider_parts(rider)
    n_w = len(extra)

    def body(*refs):
        q_ref, k_ref, v_ref, do_ref = refs[:4]
        dq_ref, dk_ref, dv_ref = refs[4 + n_w:7 + n_w]
        dka, dva, e_ref, sg_ref = refs[7 + 2 * n_w:11 + 2 * n_w]
        step = pl.program_id(0) * PAIRS + pl.program_id(1)
        begin, end = _rider_hooks(rider, refs[4:4 + n_w], refs[7 + n_w:7 + 2 * n_w], refs[-2:], step, b_dim * PAIRS)
        begin()
        h1, h2 = _head_masks()
        after, before, qrow, kcol = _sb_consts()
        dka[...] = jnp.zeros_like(dka)
        dva[...] = jnp.zeros_like(dva)

        def qloop(qi, _):
            rows = pl.ds(pl.multiple_of(qi * SB_ROWS, SB_ROWS), SB_ROWS)
            q1, q2 = _split_heads(q_ref[rows, :] * scale, h1, h2)
            do1, do2 = _split_heads(do_ref[rows, :].astype(F32), h1, h2)
            first = qi * per

            def pass1(kb, carry, lo):
                run1, run2 = carry
                krows = pl.ds(pl.multiple_of(kb * BLOCK, BLOCK), BLOCK)
                k16 = k_ref[krows, :].astype(BF16)
                v16 = v_ref[krows, :].astype(BF16)
                valid = None if lo is None else kcol[:SB_ROWS - lo] < qrow[:SB_ROWS - lo]
                lo = lo or 0
                part = pl.ds(lo, SB_ROWS - lo)

                def head(h, qh, doh, run):
                    ls, l1m = _sb_scores(qh[lo:], k16, valid)
                    a = jnp.exp(ls + _split_dot(l1m, after) + run[lo:])
                    if valid is not None:
                        a = jnp.where(valid, a, 0.0)
                    da = lax.dot_general(doh[lo:], v16, NT, preferred_element_type=F32)
                    e_ref[h, kb, part, :] = a * da
                    sg_ref[h, kb, part, :] = jnp.exp(ls)
                    return a.astype(BF16), _below(run, lo, jnp.sum(l1m, axis=-1, keepdims=True))

                a1, run1 = head(0, q1, do1, run1)
                a2, run2 = head(1, q2, do2, run2)
                dva[krows, :] += (lax.dot_general(a1, do1[lo:], TN, preferred_element_type=F32)
                                  + lax.dot_general(a2, do2[lo:], TN, preferred_element_type=F32))
                return run1, run2

            zcol = jnp.zeros((SB_ROWS, 1), F32)
            carry = (zcol, zcol)
            for kl in reversed(range(per)):
                carry = pass1(first + kl, carry, kl * BLOCK)
            _pairs_loop(first, lambda i, c: pass1(first - 1 - i, c, None), carry, SB_STEP_BWD)

            def pass2(kb, carry, lo):
                dq, pre1, pre2 = carry
                krows = pl.ds(pl.multiple_of(kb * BLOCK, BLOCK), BLOCK)
                k1, k2 = _split_heads(k_ref[krows, :], h1, h2)
                valid = None if lo is None else kcol[:SB_ROWS - lo] < qrow[:SB_ROWS - lo]
                lo = lo or 0
                part = pl.ds(lo, SB_ROWS - lo)

                def head(h, pre):
                    ev = e_ref[h, kb, part, :]
                    sg = sg_ref[h, kb, part, :]
                    dz = ev * (1.0 - sg) - (_split_dot(ev, before) + pre[lo:]) * sg
                    if valid is not None:
                        dz = jnp.where(valid, dz, 0.0)
                    return dz.astype(BF16), _below(pre, lo, jnp.sum(ev, axis=-1, keepdims=True))

                dz1, pre1 = head(0, pre1)
                dz2, pre2 = head(1, pre2)
                dka[krows, :] += (lax.dot_general(dz1, q1[lo:], TN, preferred_element_type=F32)
                                  + lax.dot_general(dz2, q2[lo:], TN, preferred_element_type=F32))
                dq = _below(dq, lo, jnp.dot(dz1, k1, preferred_element_type=F32)
                            + jnp.dot(dz2, k2, preferred_element_type=F32))
                return dq, pre1, pre2

            carry = _pairs_loop(first, lambda i, c: pass2(i, c, None), (jnp.zeros((SB_ROWS, LANES), F32), zcol, zcol),
                                SB_STEP_BWD)
            for kl in range(per):
                carry = pass2(first + kl, carry, kl * BLOCK)
            dq = carry[0]
            dq_ref[rows, :] = (dq * scale).astype(BF16)
            return 0

        lax.fori_loop(0, s_dim // SB_ROWS, qloop, 0)
        dk_ref[...] = dka[...].astype(BF16)
        dv_ref[...] = dva[...].astype(BF16)
        end()

    def col(c0):
        return pl.BlockSpec((None, s_dim, LANES),lambda b, h: (b, 0, c0 + h))

    shp = jax.ShapeDtypeStruct((b_dim, s_dim, ATT_WIDTH), BF16)
    acc = pltpu.VMEM((s_dim, LANES), F32)
    strip = pltpu.VMEM((2, nkb_max, SB_ROWS, BLOCK), F32)
    res = pl.pallas_call(
        body, name="sb_bwd", grid=(b_dim, PAIRS),
        in_specs=[col(COL_QB), col(COL_KB), col(COL_VB), col(0)] + [ANY] * n_w,
        out_specs=[col(0), col(0), col(0)] + [ANY] * n_w,
        out_shape=[shp, shp, shp] + extra_shapes,
        scratch_shapes=[acc, acc, strip, strip] + extra_sems,
        compiler_params=_params(("arbitrary", "arbitrary")),
    )(proj3, proj3, proj3, do3, *extra)
    return res[:3], res[3:]


def _sigmoid(x):
    return 1.0 / (1.0 + jnp.exp(-x))


def _gate_out_norm(proj, ua, ub, w_out, x, g, *, tt=512):
    t_dim, d = ua.shape

    def body(ga_ref, gb_ref, ua_ref, ub_ref, w_ref, x_ref, g_ref, m_ref, h_ref, n_ref):
        mixed = (_sigmoid(ga_ref[...]) * ua_ref[...] + _sigmoid(gb_ref[...]) * ub_ref[...]).astype(BF16)
        m_ref[...] = mixed
        hv = x_ref[...] + jnp.dot(mixed, w_ref[...], preferred_element_type=F32)
        h_ref[...] = hv
        r = lax.rsqrt(jnp.mean(hv * hv, axis=-1, keepdims=True) + RMS_EPS)
        n_ref[...] = ((hv * r) * g_ref[...]).astype(BF16)

    row = pl.BlockSpec((tt, d), lambda i: (i, 0))
    return pl.pallas_call(
        body, name="gate_out_norm", grid=(t_dim // tt,),
        in_specs=[pl.BlockSpec((tt, d), lambda i: (i, 3)), pl.BlockSpec((tt, d), lambda i: (i, 4)), row, row,
                  pl.BlockSpec((d, d), lambda i: (0, 0)), row, pl.BlockSpec((1, d), lambda i: (0, 0))],
        out_specs=[row, row, row],
        out_shape=[jax.ShapeDtypeStruct((t_dim, d), BF16), jax.ShapeDtypeStruct((t_dim, d), F32),
                   jax.ShapeDtypeStruct((t_dim, d), BF16)],
        compiler_params=_params(("parallel",)),
    )(proj, proj, ua, ub, w_out, x, g)


def _out_dx_gate_bwd(dh, w_out, proj, ua, ub, *, tt=512):
    t_dim, d = ua.shape

    def body(dh_ref, w_ref, ga_ref, gb_ref, ua_ref, ub_ref, dua_ref, dub_ref, dg_ref):
        dm = lax.dot_general(dh_ref[...].astype(BF16), w_ref[...], NT, preferred_element_type=F32)
        sa = _sigmoid(ga_ref[...])
        sb = _sigmoid(gb_ref[...])
        dua_ref[...] = (dm * sa).astype(BF16)
        dub_ref[...] = (dm * sb).astype(BF16)
        dg_ref[:, :d] = (dm * ua_ref[...] * (sa * (1.0 - sa))).astype(BF16)
        dg_ref[:, d:] = (dm * ub_ref[...] * (sb * (1.0 - sb))).astype(BF16)

    row = pl.BlockSpec((tt, d), lambda i: (i, 0))
    wide = pl.BlockSpec((tt, 2 * d), lambda i: (i, 0))
    return pl.pallas_call(
        body, name="out_dx_gate_bwd", grid=(t_dim // tt,),
        in_specs=[row, pl.BlockSpec((d, d), lambda i: (0, 0)),
                  pl.BlockSpec((tt, d), lambda i: (i, 3)), pl.BlockSpec((tt, d), lambda i: (i, 4)), row, row],
        out_specs=[row, row, wide],
        out_shape=[jax.ShapeDtypeStruct((t_dim, d), BF16), jax.ShapeDtypeStruct((t_dim, d), BF16),
                   jax.ShapeDtypeStruct((t_dim, 2 * d), BF16)],
        compiler_params=_params(("parallel",)),
    )(dh, w_out, proj, proj, ua, ub)


def _ffn_up_swiglu(n, wg3, wu3, *, tt=1024):
    t_dim, d = n.shape
    n_s, f4, _ = wg3.shape

    def body(n_ref, wg_ref, wu_ref, g_ref, u_ref, a_ref):
        nv = n_ref[...]
        gv = lax.dot_general(nv, wg_ref[...], NT, preferred_element_type=F32)
        uv = lax.dot_general(nv, wu_ref[...], NT, preferred_element_type=F32)
        g_ref[...] = gv.astype(BF16)
        u_ref[...] = uv.astype(BF16)
        a_ref[...] = (gv * _sigmoid(gv) * uv).astype(BF16)

    wspec = pl.BlockSpec((None, f4, d), lambda i, s: (s, 0, 0))
    ospec = pl.BlockSpec((None, tt, f4), lambda i, s: (s, i, 0))
    shp = (n_s, t_dim, f4)
    return pl.pallas_call(
        body, name="ffn_up_swiglu", grid=(t_dim // tt, n_s),
        in_specs=[pl.BlockSpec((tt, d), lambda i, s: (i, 0)), wspec, wspec], out_specs=[ospec, ospec, ospec],
        out_shape=[jax.ShapeDtypeStruct(shp, BF16)] * 3,
        compiler_params=_params(("parallel", "parallel")),
    )(n, wg3, wu3)


def _ffn_down_dx_swiglu(dh, wd3, g3, u3, *, tt=1024):
    t_dim, d = dh.shape
    n_s, f4, _ = wd3.shape

    def body(dh_ref, w_ref, g_ref, u_ref, dg_ref, du_ref):
        da = lax.dot_general(dh_ref[...].astype(BF16), w_ref[...], NT, preferred_element_type=F32)
        gv = g_ref[...].astype(F32)
        sg = _sigmoid(gv)
        dg_ref[...] = (da * u_ref[...].astype(F32) * (sg + gv * sg * (1.0 - sg))).astype(BF16)
        du_ref[...] = (da * (gv * sg)).astype(BF16)

    spec = pl.BlockSpec((None, tt, f4), lambda i, s: (s, i, 0))
    shp = jax.ShapeDtypeStruct((n_s, t_dim, f4), BF16)
    return pl.pallas_call(
        body, name="ffn_down_dx_swiglu", grid=(t_dim // tt, n_s),
        in_specs=[pl.BlockSpec((tt, d), lambda i, s: (i, 0)), pl.BlockSpec((None, f4, d), lambda i, s: (s, 0, 0)),
                  spec, spec],
        out_specs=[spec, spec], out_shape=[shp, shp],
        compiler_params=_params(("parallel", "parallel")),
    )(dh, wd3, g3, u3)


def _mem_fwd(qm, kvm, *, tt=2048):
    b_dim, s_dim, _ = qm.shape
    n_mem = kvm.shape[1]
    scale = MEM_HEAD_DIM ** -0.5

    def body(q_ref, k_ref, v_ref, o_ref):
        sc = lax.dot_general(q_ref[0], k_ref[0], NT, preferred_element_type=F32) * scale
        p = jnp.exp(sc - jnp.max(sc, axis=-1, keepdims=True))
        p = p / jnp.sum(p, axis=-1, keepdims=True)
        o_ref[0] = jnp.dot(p.astype(BF16), v_ref[0], preferred_element_type=F32).astype(BF16)

    qs = pl.BlockSpec((1, tt, MEM_HEAD_DIM), lambda b, h, i: (b, i, h))
    return pl.pallas_call(
        body, name="mem_fwd", grid=(b_dim, N_HEADS_MEM, s_dim // tt),
        in_specs=[qs, pl.BlockSpec((1, n_mem, MEM_HEAD_DIM), lambda b, h, i: (b, 0, h)),
                  pl.BlockSpec((1, n_mem, MEM_HEAD_DIM), lambda b, h, i: (b, 0, N_HEADS_MEM + h))],
        out_specs=qs, out_shape=jax.ShapeDtypeStruct(qm.shape, BF16),
        compiler_params=_params(("parallel", "parallel", "parallel")),
    )(qm, kvm, kvm)


def _mem_bwd(qm, kvm, dom, *, tt=2048):
    b_dim, s_dim, _ = qm.shape
    n_mem = kvm.shape[1]
    scale = MEM_HEAD_DIM ** -0.5

    def body(q_ref, k_ref, v_ref, do_ref, dq_ref, dk_ref, dv_ref):
        qv, kv, vv, dov = q_ref[0], k_ref[0], v_ref[0], do_ref[0]
        sc = lax.dot_general(qv, kv, NT, preferred_element_type=F32) * scale
        p = jnp.exp(sc - jnp.max(sc, axis=-1, keepdims=True))
        p = p / jnp.sum(p, axis=-1, keepdims=True)
        dp = lax.dot_general(dov, vv, NT, preferred_element_type=F32)
        ds = (p * (dp - jnp.sum(p * dp, axis=-1, keepdims=True)) * scale).astype(BF16)
        dq_ref[0] = jnp.dot(ds, kv, preferred_element_type=F32).astype(BF16)

        @pl.when(pl.program_id(2) == 0)
        def _():
            dk_ref[...] = jnp.zeros_like(dk_ref)
            dv_ref[...] = jnp.zeros_like(dv_ref)

        dk_ref[0] += lax.dot_general(ds, qv, TN, preferred_element_type=F32)
        dv_ref[0] += lax.dot_general(p.astype(BF16), dov, TN, preferred_element_type=F32)

    qs = pl.BlockSpec((1, tt, MEM_HEAD_DIM), lambda b, h, i: (b, i, h))
    ks = pl.BlockSpec((1, n_mem, MEM_HEAD_DIM), lambda b, h, i: (b, 0, h))
    vs = pl.BlockSpec((1, n_mem, MEM_HEAD_DIM), lambda b, h, i: (b, 0, N_HEADS_MEM + h))
    return pl.pallas_call(
        body, name="mem_bwd", grid=(b_dim, N_HEADS_MEM, s_dim // tt),
        in_specs=[qs, ks, vs, qs], out_specs=[qs, ks, ks],
        out_shape=[jax.ShapeDtypeStruct(qm.shape, BF16), jax.ShapeDtypeStruct((b_dim, n_mem, MEM_WIDTH), F32),
                   jax.ShapeDtypeStruct((b_dim, n_mem, MEM_WIDTH), F32)],
        compiler_params=_params(("parallel", "parallel", "arbitrary")),
    )(qm, kvm, kvm, dom)


def _adamw_math(wv, gv, mv, vv):
    nm = ADAM_B1 * mv + (1.0 - ADAM_B1) * gv
    nv = ADAM_B2 * vv + (1.0 - ADAM_B2) * (gv * gv)
    m_hat = nm / (1.0 - ADAM_B1 ** ADAM_STEP)
    v_hat = nv / (1.0 - ADAM_B2 ** ADAM_STEP)
    return -ADAM_LR * (m_hat / (jnp.sqrt(v_hat) + ADAM_EPS) + ADAM_WD * wv), nm, nv


def _adamw(w, g, m, v, *, name):
    rows, cols = w.shape
    tr = _tile(rows, 256, 8)

    def body(w_ref, g_ref, m_ref, v_ref, d_ref, nm_ref, nv_ref):
        d_ref[...], nm_ref[...], nv_ref[...] = _adamw_math(w_ref[...], g_ref[...], m_ref[...], v_ref[...])

    spec = pl.BlockSpec((tr, cols), lambda i: (i, 0))
    shp = jax.ShapeDtypeStruct((rows, cols), F32)
    return pl.pallas_call(
        body, name=name, grid=(rows // tr,),
        in_specs=[spec] * 4, out_specs=[spec] * 3, out_shape=[shp] * 3,
        compiler_params=_params(("parallel",)),
    )(w, g, m, v)


def _prefetch_spec(grid, in_specs, out_specs):
    return pltpu.PrefetchScalarGridSpec(num_scalar_prefetch=1, grid=grid, in_specs=in_specs, out_specs=out_specs)


def _adamw_halves(w, mine, theirs, m, v, c_idx, *, name):
    rows, cols = w.shape
    half = rows // 2
    tr = _tile(half, _row_cap(cols), 8)
    nh = half // tr

    def body(c_ref, w_ref, mine_ref, theirs_ref, m_ref, v_ref, g_ref, d_ref, nm_ref, nv_ref):
        gv = jnp.where(pl.program_id(0) == c_ref[0], mine_ref[...], theirs_ref[...])
        g_ref[...] = gv
        d_ref[...], nm_ref[...], nv_ref[...] = _adamw_math(w_ref[...], gv, m_ref[...], v_ref[...])

    full = pl.BlockSpec((tr, cols), lambda h, i, c_ref: (h * nh + i, 0))
    part = pl.BlockSpec((tr, cols), lambda h, i, c_ref: (i, 0))
    shp = jax.ShapeDtypeStruct((rows, cols), F32)
    return pl.pallas_call(
        body, name=name, grid_spec=_prefetch_spec((2, nh), [full, part, part, full, full], [full] * 4),
        out_shape=[shp] * 4,
        compiler_params=_params(("parallel", "parallel")),
    )(c_idx, w, mine, theirs, m, v)


def _pair_sum(g3, theirs, c_idx, *, name):
    n, rows, cols = g3.shape
    half = rows // 2
    tr = _tile(half, _row_cap(cols), 16)

    def body(c_ref, g_ref, t_ref, o_ref):
        o_ref[...] = (g_ref[...] + t_ref[...]).astype(BF16)

    part = pl.BlockSpec((None, tr, cols), lambda s, i, c_ref: (s, i, 0))
    return pl.pallas_call(
        body, name=name,
        grid_spec=_prefetch_spec((n, half // tr),
                                 [pl.BlockSpec((None, None, tr, cols), lambda s, i, c_ref: (s, c_ref[0], i, 0)), part],
                                 part),
        out_shape=jax.ShapeDtypeStruct((n, half, cols), BF16),
        compiler_params=_params(("parallel", "parallel")),
    )(c_idx, g3.reshape(n, 2, half, cols), theirs)


def _chip_sum(pair, recv, s_idx, *, name):
    _, half, cols = pair.shape
    tr = _tile(half, _row_cap(cols), 16)

    def body(s_ref, p_ref, r_ref, o_ref):
        o_ref[...] = ((p_ref[...].astype(F32) + r_ref[0].astype(F32)) + r_ref[1].astype(F32)) + r_ref[2].astype(F32)

    return pl.pallas_call(
        body, name=name,
        grid_spec=_prefetch_spec((half // tr,),
                                 [pl.BlockSpec((None, tr, cols), lambda i, s_ref: (s_ref[0], i, 0)),
                                  pl.BlockSpec((N_CHIPS - 1, tr, cols), lambda i, s_ref: (0, i, 0))],
                                 pl.BlockSpec((tr, cols), lambda i, s_ref: (i, 0))),
        out_shape=jax.ShapeDtypeStruct((half, cols), F32),
        compiler_params=_params(("parallel",)),
    )(s_idx, pair, recv)


def _sum8(parts):
    n, rows, cols = parts.shape

    def body(p_ref, o_ref):
        acc = p_ref[0]
        for i in range(1, n):
            acc = acc + p_ref[i]
        o_ref[...] = acc

    return pl.pallas_call(
        body, name="small_sum", grid=(1,),
        in_specs=[pl.BlockSpec((n, rows, cols), lambda i: (0, 0, 0))],
        out_specs=pl.BlockSpec((rows, cols), lambda i: (0, 0)),
        out_shape=jax.ShapeDtypeStruct((rows, cols), parts.dtype),
        compiler_params=_params(("arbitrary",)),
    )(parts)


def _place():
    return lax.axis_index("x"), lax.axis_index("y"), lax.axis_index("c")


ANY = pl.BlockSpec(memory_space=pl.ANY)


def _rider_parts(rider):
    if rider is None:
        return (), [], []
    kind, arrays = rider
    n = len(arrays)
    shapes = {"gather": _gathered_shapes, "pair": _pair_shapes, "chip": _chip_shapes}[kind](arrays)
    sems = _gather_sems(n) if kind == "gather" else _exchange_sems(n if kind == "pair" else 3 * n)
    return tuple(arrays), shapes, sems


def _rider_hooks(rider, ins, outs, sems, step, n_steps):
    if rider is None:
        return (lambda: None), (lambda: None)
    if rider[0] == "gather":
        start, forward, finish = _gather_steps(ins, outs, *sems)
    else:
        start, finish = {"pair": _pair_steps, "chip": _chip_steps}[rider[0]](ins, outs, *sems)
        forward = None

    def begin():
        pl.when(step == 0)(start)

    def end():
        if forward is not None:
            pl.when(step == n_steps - 2)(forward)
        pl.when(step == n_steps - 1)(finish)

    return begin, end


def _gathered_shapes(shards):
    return [jax.ShapeDtypeStruct((N_CHIPS,) + s.shape, s.dtype) for s in shards]


def _gather_sems(n):
    return [pltpu.SemaphoreType.DMA((7 * n,)), pltpu.SemaphoreType.DMA((7 * n,))]


def _gather_steps(ins, outs, send_sems, recv_sems):
    n = len(ins)
    halves = [r.shape[0] // 2 for r in ins]
    x, y, c = _place()
    my_chip = 2 * x + y
    me, sibling = (x, y, c), (x, y, 1 - c)
    chips = [(1 - x, y), (x, 1 - y), (1 - x, 1 - y)]

    def half_of(w, chip, pc):
        return outs[w].at[chip, pl.ds(pc * halves[w], halves[w]), :]

    def copy(w, k, src, dst, to):
        return pltpu.make_async_remote_copy(
            src_ref=src, dst_ref=dst, send_sem=send_sems.at[7 * w + k], recv_sem=recv_sems.at[7 * w + k],
            device_id=to, device_id_type=MESH)

    def firsts():
        cps = []
        for w in range(n):
            cps.append(copy(w, 0, ins[w], outs[w].at[my_chip], sibling))
            mine = ins[w].at[pl.ds(c * halves[w], halves[w]), :]
            for j, (px, py) in enumerate(chips):
                cps.append(copy(w, 1 + j, mine, half_of(w, my_chip, c), (px, py, c)))
        return cps

    def passes():
        return [copy(w, 4 + j, half_of(w, 2 * px + py, c), half_of(w, 2 * px + py, c), sibling)
                for w in range(n) for j, (px, py) in enumerate(chips)]

    def start():
        for cp in firsts():
            cp.start()

    def forward():
        fws = passes()
        for w in range(n):
            for j, (px, py) in enumerate(chips):
                landed = half_of(w, 2 * px + py, c)
                copy(w, 1 + j, landed, landed, me).wait_recv()
                fws[3 * w + j].start()

    def finish():
        for w in range(n):
            copy(w, 0, ins[w], outs[w].at[my_chip], me).wait_recv()
            for j, (px, py) in enumerate(chips):
                landed = half_of(w, 2 * px + py, 1 - c)
                copy(w, 4 + j, landed, landed, me).wait_recv()
        for cp in firsts() + passes():
            cp.wait_send()

    return start, forward, finish


def _pair_shapes(grads):
    return [jax.ShapeDtypeStruct((g.shape[0], g.shape[1] // 2, g.shape[2]), g.dtype) for g in grads]


def _exchange_sems(n):
    return [pltpu.SemaphoreType.DMA((n,)), pltpu.SemaphoreType.DMA((n,))]


def _exchange_steps(copies):
    def start():
        for cp in copies():
            cp.start()

    def finish():
        for cp in copies():
            cp.wait()

    return start, finish


def _pair_steps(ins, outs, send_sems, recv_sems):
    x, y, c = _place()

    def copies():
        return [pltpu.make_async_remote_copy(
            src_ref=ins[w].at[:, pl.ds((1 - c) * (ins[w].shape[1] // 2), ins[w].shape[1] // 2), :], dst_ref=outs[w],
            send_sem=send_sems.at[w], recv_sem=recv_sems.at[w], device_id=(x, y, 1 - c), device_id_type=MESH)
            for w in range(len(ins))]

    return _exchange_steps(copies)


def _chip_shapes(pairs):
    return [jax.ShapeDtypeStruct((N_CHIPS - 1,) + p.shape[1:], p.dtype) for p in pairs]


def _chip_steps(ins, outs, send_sems, recv_sems):
    x, y, c = _place()
    others = [(1 - x, y), (x, 1 - y), (1 - x, 1 - y)]

    def copies():
        return [pltpu.make_async_remote_copy(
            src_ref=ins[w].at[2 * px + py], dst_ref=outs[w].at[j],
            send_sem=send_sems.at[3 * w + j], recv_sem=recv_sems.at[3 * w + j],
            device_id=(px, py, c), device_id_type=MESH)
            for w in range(len(ins)) for j, (px, py) in enumerate(others)]

    return _exchange_steps(copies)


def _swap_halves(mine):
    n = len(mine)

    def body(*refs):
        ins, outs, send_sems, recv_sems = refs[:n], refs[n:2 * n], refs[2 * n], refs[2 * n + 1]
        x, y, c = _place()
        copies = [pltpu.make_async_remote_copy(
            src_ref=ins[w], dst_ref=outs[w], send_sem=send_sems.at[w], recv_sem=recv_sems.at[w],
            device_id=(x, y, 1 - c), device_id_type=MESH) for w in range(n)]
        for cp in copies:
            cp.start()
        for cp in copies:
            cp.wait()

    return pl.pallas_call(
        body, name="grad_swap_halves",
        out_shape=[jax.ShapeDtypeStruct(h.shape, h.dtype) for h in mine],
        in_specs=[ANY] * n, out_specs=[ANY] * n,
        scratch_shapes=[pltpu.SemaphoreType.DMA((n,)), pltpu.SemaphoreType.DMA((n,))],
    )(*mine)


def _gather_small(small):
    srows, cols = small.shape

    def body(s_ref, all_ref, send_sems, recv_sems, local_sem):
        x, y, c = _place()
        me = 4 * x + 2 * y + c
        keep_small = pltpu.make_async_copy(s_ref, all_ref.at[me], local_sem)
        keep_small.start()
        sends = []
        for kk in range(1, 8):
            peer = (x ^ (kk >> 2), y ^ ((kk >> 1) & 1), c ^ (kk & 1))
            sends.append(pltpu.make_async_remote_copy(
                src_ref=s_ref, dst_ref=all_ref.at[me],
                send_sem=send_sems.at[kk], recv_sem=recv_sems.at[kk], device_id=peer, device_id_type=MESH))
        for cp in sends:
            cp.start()
        for kk in range(1, 8):
            px, py, pc = x ^ (kk >> 2), y ^ ((kk >> 1) & 1), c ^ (kk & 1)
            pltpu.make_async_remote_copy(
                src_ref=s_ref, dst_ref=all_ref.at[4 * px + 2 * py + pc],
                send_sem=send_sems.at[kk], recv_sem=recv_sems.at[kk], device_id=(px, py, pc),
                device_id_type=MESH).wait_recv()
        for cp in sends:
            cp.wait_send()
        keep_small.wait()

    return pl.pallas_call(
        body, name="gather_small",
        out_shape=jax.ShapeDtypeStruct((8, srows, cols), small.dtype),
        in_specs=[ANY], out_specs=ANY,
        scratch_shapes=[pltpu.SemaphoreType.DMA((8,)), pltpu.SemaphoreType.DMA((8,)), pltpu.SemaphoreType.DMA],
    )(small)


SHARDED = (("w_in", D_MODEL, IN_COLS, 1), ("w_up_a", ATT_WIDTH, D_MODEL, 1), ("w_up_b", ATT_WIDTH, D_MODEL, 1),
           ("w_out", D_MODEL, D_MODEL, 0), ("w_q_mem", D_MODEL, MEM_WIDTH, 0), ("w_kv_mem", D_MODEL, 2 * MEM_WIDTH, 0),
           ("w_o_mem", MEM_WIDTH, D_MODEL, 1), ("w_ffn_gate", D_FF, D_MODEL, 0), ("w_ffn_up", D_FF, D_MODEL, 0),
           ("w_ffn_down", D_FF, D_MODEL, 0))
TRANSPOSED = ("w_ffn_gate", "w_ffn_up")
NAMES = tuple(n for n, _, _, _ in SHARDED)


def _held(name, shard):
    return shard.T if name in TRANSPOSED else shard
EARLY, LATE = NAMES[:1], NAMES[1:]
GAINS = ("g_mix", "g_mem_q", "g_mem_kv", "g_ffn", "g_final")


def _natural(w3):
    n, r, c = w3.shape
    return w3.reshape(n * r, c)


def _shard_major(g, axis):
    if axis == 1:
        return g
    r, c = g.shape
    return g.reshape(N_CHIPS, r // N_CHIPS, c)


def kernel(x, mem, positions, g_mix, w_in, w_up_a, w_up_b, w_out, g_mem_q, g_mem_kv, w_q_mem, w_kv_mem, w_o_mem, g_ffn, w_ffn_gate, w_ffn_up, w_ffn_down, g_final, loss_target, m_g_mix, m_w_in, m_w_up_a, m_w_up_b, m_w_out, m_g_mem_q, m_g_mem_kv, m_w_q_mem, m_w_kv_mem, m_w_o_mem, m_g_ffn, m_w_ffn_gate, m_w_ffn_up, m_w_ffn_down, m_g_final, v_g_mix, v_w_in, v_w_up_a, v_w_up_b, v_w_out, v_g_mem_q, v_g_mem_kv, v_w_q_mem, v_w_kv_mem, v_w_o_mem, v_g_ffn, v_w_ffn_gate, v_w_ffn_up, v_w_ffn_down, v_g_final):
    given = dict(locals())
    shards = {n: _held(n, given[n][0]) for n in NAMES}

    early_shards = [shards[n].astype(BF16) for n in EARLY]
    late_shards = [shards[n].astype(BF16) for n in LATE]
    c_idx = lax.axis_index("c").astype(jnp.int32).reshape(1)
    s_idx = (2 * lax.axis_index("x") + lax.axis_index("y")).astype(jnp.int32).reshape(1)

    loss_row, grad_x, mine, gain_grads = _local_step(x, mem, positions, loss_target, g_mix, g_mem_q, g_mem_kv,
                                                     g_ffn, g_final, {}, early_shards, late_shards, (c_idx, s_idx))
    return _reduce_and_update(given, shards, loss_row, grad_x, mine, gain_grads, c_idx)


def _reduce_halves(glist, names, c_idx, s_idx, pair_exchange, chip_exchange):
    theirs = pair_exchange(glist)
    pairs = [_pair_sum(g, t, c_idx, name="pair_sum_" + n) for n, g, t in zip(names, glist, theirs)]
    recv = chip_exchange(pairs)
    return [_chip_sum(p, r, s_idx, name="chip_sum_" + n) for n, p, r in zip(names, pairs, recv)]


def _local_step(x, mem, positions, loss_target, g_mix, g_mem_q, g_mem_kv, g_ffn, g_final, wf,
                early_shards=None, late_shards=None, place=None):
    b_dim, s_dim, d = x.shape
    t_dim = b_dim * s_dim
    n_mem = mem.shape[1]
    wf = dict(wf)

    xb = x.reshape(t_dim, d)
    tgt = loss_target.reshape(t_dim, d)
    memf = mem.reshape(b_dim * n_mem, d)
    gfin = g_final.reshape(1, d)
    pos = positions.reshape(t_dim, 1).astype(F32)

    lane = jnp.arange(LANES) % HEAD_DIM
    half = ROPE_DIM // 2
    inv_freq = ROPE_THETA ** (-jnp.arange(half, dtype=F32) / half)
    inv_lane = jnp.where(lane < ROPE_DIM, inv_freq[lane % half], 0.0).reshape(1, -1).astype(F32)
    sel_a = (lane < half).astype(F32).reshape(1, -1)
    sel_b = ((lane >= half) & (lane < ROPE_DIM)).astype(F32).reshape(1, -1)

    def rows3(t):
        return t.reshape(b_dim, s_dim, t.shape[-1])

    def rows2(t):
        return t.reshape(t_dim, t.shape[-1])

    if early_shards:
        n1, gathered = _rms_fwd(xb, g_mix, name="rms_mix", rider=("gather", early_shards))
        wf.update(zip(EARLY, gathered))
    else:
        n1 = _rms_fwd(xb, g_mix, name="rms_mix")
    proj = _mm_cs(n1, wf["w_in"], name="mm_in")
    proj3 = rows3(proj)
    cs, sn = _rope_table(pos, inv_lane, sel_a, sel_b)
    cs3, sn3 = rows3(cs), rows3(sn)
    (oa16, oa32, lse_a), _ = _dil_fwd(proj3, cs3, sn3, sel_a, sel_b)
    ob16, gathered = _sb_fwd(proj3, ("gather", late_shards) if late_shards else None)
    wf.update(zip(LATE, gathered))
    w_out, w_q, w_kv = _natural(wf["w_out"]), _natural(wf["w_q_mem"]), _natural(wf["w_kv_mem"])
    oa, ob = rows2(oa16), rows2(ob16)
    ua = _mm_sm(oa, wf["w_up_a"], name="mm_up_a")
    ub = _mm_sm(ob, wf["w_up_b"], name="mm_up_b")
    mixed, h1, hn = _gate_out_norm(proj, ua, ub, w_out, xb, g_mem_q)

    memn = _rms_fwd(memf, g_mem_kv, name="rms_mem_kv")
    qm = _mm(hn, w_q, name="mm_q_mem", out_dtype=BF16)
    kvm = _mm(memn, w_kv, name="mm_kv_mem", out_dtype=BF16)
    qm3, kvm3 = rows3(qm), kvm.reshape(b_dim, n_mem, 2 * MEM_WIDTH)
    om = rows2(_mem_fwd(qm3, kvm3))
    h2 = _mm_sm(om, wf["w_o_mem"], name="mm_o_mem", add=h1)

    n3 = _rms_fwd(h2, g_ffn, name="rms_ffn")
    gate3, up3, act3 = _ffn_up_swiglu(n3, wf["w_ffn_gate"], wf["w_ffn_up"])
    loss_row, dh3, dg_final = _down_final(act3, wf["w_ffn_down"], h2, gfin, tgt)

    grads = {}
    grads["w_ffn_down"] = _mm_ffn_down_dw(act3, dh3, name="mm_down_dw")
    dgate3, dup3 = _ffn_down_dx_swiglu(dh3, wf["w_ffn_down"], gate3, up3)
    grads["w_ffn_gate"] = _mm_ffn_down_dw(dgate3, n3, name="mm_gate_dw")
    grads["w_ffn_up"] = _mm_ffn_down_dw(dup3, n3, name="mm_up_dw")
    dn3 = _ffn_up_dx(dgate3, dup3, wf["w_ffn_gate"], wf["w_ffn_up"])
    dh2, dg_ffn = _rms_bwd(h2, g_ffn, dn3, dh3, name="rms_ffn_bwd")

    dom = _mm_sm_dx(dh2, wf["w_o_mem"], name="mm_o_mem_dx", out_dtype=BF16)
    grads["w_o_mem"] = _mm_sm_dw(om, dh2, name="mm_o_mem_dw")
    dqm, dkm, dvm = _mem_bwd(qm3, kvm3, rows3(dom))
    dqm = rows2(dqm)
    dkvm = jnp.concatenate([dkm, dvm], axis=-1).reshape(b_dim * n_mem, 2 * MEM_WIDTH).astype(BF16)
    grads["w_q_mem"] = _shard_major(_mm(hn, dqm, name="mm_q_mem_dw", ta=True), 0)
    dhn = _mm(dqm, w_q, name="mm_q_mem_dx", tb=True)
    grads["w_kv_mem"] = _shard_major(_mm(memn, dkvm, name="mm_kv_mem_dw", ta=True), 0)
    dmemn = _mm(dkvm, w_kv, name="mm_kv_mem_dx", tb=True)
    _, dg_mem_kv = _rms_bwd(memf, g_mem_kv, dmemn, None, name="rms_mem_kv_bwd")
    dh1, dg_mem_q = _rms_bwd(h1, g_mem_q, dhn, dh2, name="rms_mem_q_bwd")

    grads["w_out"] = _shard_major(_mm(mixed, dh1, name="mm_out_dw", ta=True), 0)
    dua, dub, dgates = _out_dx_gate_bwd(dh1, w_out, proj, ua, ub)
    doa = _mm_sm_dx(dua, wf["w_up_a"], name="mm_up_a_dx")
    grads["w_up_a"] = _mm_sm_dw(oa, dua, name="mm_up_a_dw")
    dob = _mm_sm_dx(dub, wf["w_up_b"], name="mm_up_b_dx", out_dtype=BF16)
    grads["w_up_b"] = _mm_sm_dw(ob, dub, name="mm_up_b_dw")

    att = {}

    def dil_with_pairs(glist):
        att["a"], theirs = _dil_bwd(proj3, cs3, sn3, sel_a, sel_b, rows3(doa), oa32, lse_a,
                                    ("pair", glist) if glist else None)
        return theirs

    def sb_with_chips(pairs):
        att["b"], recv = _sb_bwd(proj3, rows3(dob), ("chip", pairs) if pairs else None)
        return recv

    if place is None:
        dil_with_pairs(())
        sb_with_chips(())
    else:
        mine_late = _reduce_halves([grads[n] for n in LATE], LATE, *place, dil_with_pairs, sb_with_chips)
    dproj = jnp.concatenate([rows2(t) for t in att["a"] + att["b"]] + [dgates], axis=1)
    grads["w_in"] = _mm_cs_dw(n1, dproj, name="mm_in_dw")
    if place is None:
        dn1 = _mm_cs_dx(dproj, wf["w_in"], name="mm_in_dx")
        dx, dg_mix = _rms_bwd(xb, g_mix, dn1, dh1, name="rms_mix_bwd")
    else:
        tail = {}

        def dx_with_pairs(glist):
            tail["dn1"], theirs = _mm_cs_dx(dproj, wf["w_in"], name="mm_in_dx", rider=("pair", glist))
            return theirs

        def rms_with_chips(pairs):
            tail["dx"], tail["dg"], recv = _rms_bwd(xb, g_mix, tail["dn1"], dh1, name="rms_mix_bwd",
                                                    rider=("chip", pairs))
            return recv

        mine_early = _reduce_halves([grads[n] for n in EARLY], EARLY, *place, dx_with_pairs, rms_with_chips)
        dx, dg_mix = tail["dx"], tail["dg"]
    grad_x = dx.reshape(b_dim, s_dim, d)
    gains = (dg_mix, dg_mem_q, dg_mem_kv, dg_ffn, dg_final)
    if place is None:
        return loss_row, grad_x, grads, gains
    return loss_row, grad_x, mine_early + mine_late, gains


def _reduce_and_update(given, shards, loss_row, grad_x, mine, gain_grads, c_idx):
    d = D_MODEL
    dg_mix, dg_mem_q, dg_mem_kv, dg_ffn, dg_final = gain_grads
    small = jnp.concatenate([dg_mix, dg_mem_q, dg_mem_kv, dg_ffn, dg_final,
                             jnp.pad(loss_row, ((0, 0), (0, FLAT_COLS - LANES))), jnp.zeros((2, FLAT_COLS), F32)], axis=0)
    small_all = _gather_small(small)
    others = _swap_halves(mine)
    small_sum = _sum8(small_all)
    loss = small_sum[5, 0]

    out_g, out_d, out_m, out_v = {}, {}, {}, {}
    for n, mine_n, other_n in zip(NAMES, mine, others):
        res = _adamw_halves(shards[n], mine_n, other_n, _held(n, given["m_" + n][0]), _held(n, given["v_" + n][0]),
                            c_idx, name="adamw_" + n)
        out_g[n], out_d[n], out_m[n], out_v[n] = [_held(n, r)[None] for r in res]
    gain_w = jnp.concatenate([given[n].reshape(1, d) for n in GAINS], axis=0)
    gain_m = jnp.concatenate([given["m_" + n].reshape(1, d) for n in GAINS], axis=0)
    gain_v = jnp.concatenate([given["v_" + n].reshape(1, d) for n in GAINS], axis=0)
    gain_g = small_sum[:len(GAINS)]
    gd, gm, gv = _adamw(gain_w, gain_g, gain_m, gain_v, name="adamw_gains")
    for i, n in enumerate(GAINS):
        shape = given[n].shape
        out_g[n], out_d[n] = gain_g[i].reshape(shape), gd[i].reshape(shape)
        out_m[n], out_v[n] = gm[i].reshape(shape), gv[i].reshape(shape)

    order = ["g_mix", "w_in", "w_up_a", "w_up_b", "w_out", "g_mem_q", "g_mem_kv", "w_q_mem", "w_kv_mem", "w_o_mem",
             "g_ffn", "w_ffn_gate", "w_ffn_up", "w_ffn_down", "g_final"]
    return (loss, grad_x, *[out_g[n] for n in order], *[out_d[n] for n in order],
            *[out_m[n] for n in order], *[out_v[n] for n in order])
```

```python
import jax
import jax.numpy as jnp
from jax import lax
from jax.experimental import pallas as pl
from jax.experimental.pallas import tpu as pltpu

F32 = jnp.float32
BF16 = jnp.bfloat16
MESH = pl.DeviceIdType.MESH

D_MODEL = 1024
HEAD_DIM = 64
N_HEADS = 8
ATT_WIDTH = N_HEADS * HEAD_DIM
DIL_PATTERNS = ((128, 1), (512, 4), (2048, 16))
BLOCK = 128
SB_ROWS = 1024
SB_STEP_FWD, SB_STEP_BWD = 8, 4
ROPE_THETA = 500000.0
ROPE_DIM = HEAD_DIM // 4
N_HEADS_MEM = 4
MEM_HEAD_DIM = 128
MEM_WIDTH = N_HEADS_MEM * MEM_HEAD_DIM
D_FF = 2816
IN_COLS = 6 * ATT_WIDTH + 2 * D_MODEL
RMS_EPS = 1e-6
ADAM_LR = 0.001
ADAM_B1 = 0.9
ADAM_B2 = 0.999
ADAM_EPS = 1e-08
ADAM_WD = 0.01
ADAM_STEP = 10

N_CHIPS = 4
LANES = 128
FLAT_COLS = 1024
VMEM_LIMIT = 56 * 1024 * 1024

PAIRS = ATT_WIDTH // LANES
COL_QA, COL_KA, COL_VA, COL_QB, COL_KB, COL_VB = (i * PAIRS for i in range(6))

MM_CAP = 1408
TOK_CAP = 2048
NN = (((1,), (0,)), ((), ()))
NT = (((1,), (1,)), ((), ()))
TN = (((0,), (0,)), ((), ()))
BNN = (((2,), (1,)), ((0,), (0,)))
BNT = (((2,), (2,)), ((0,), (0,)))
BTN = (((1,), (1,)), ((0,), (0,)))
DIL_BATCH = 16
DIL_CHUNK = 512


def _tile(dim, cap, unit=LANES):
    if dim <= cap:
        return dim
    best = None
    for t in range(unit, cap + 1, unit):
        if dim % t == 0:
            best = t
    assert best is not None, (dim, cap)
    return best


def _row_cap(cols):
    return max(256, (1 << 18) // cols)


def _params(sem):
    return pltpu.CompilerParams(dimension_semantics=sem, vmem_limit_bytes=VMEM_LIMIT)


def _mm(a, b, *, name, ta=False, tb=False, add=None, out_dtype=F32,
        tm_cap=MM_CAP, tn_cap=MM_CAP, tk_cap=MM_CAP):
    if ta:
        k_dim, m_dim = a.shape
    else:
        m_dim, k_dim = a.shape
    if tb:
        n_dim, kb = b.shape
    else:
        kb, n_dim = b.shape
    assert kb == k_dim, (a.shape, b.shape, ta, tb)
    tm, tn, tk = _tile(m_dim, tm_cap), _tile(n_dim, tn_cap), _tile(k_dim, tk_cap)
    nk = k_dim // tk
    dims = (((0 if ta else 1,), (1 if tb else 0,)), ((), ()))
    has_add = add is not None

    def body(*refs):
        if has_add:
            a_ref, b_ref, add_ref, o_ref = refs[:4]
        else:
            a_ref, b_ref, o_ref = refs[:3]
        part = lax.dot_general(a_ref[...].astype(BF16), b_ref[...].astype(BF16), dims, preferred_element_type=F32)

        def finish(r):
            if has_add:
                r = add_ref[...] + r
            o_ref[...] = r.astype(out_dtype)

        if nk == 1:
            finish(part)
            return
        acc_ref = refs[-1]
        k = pl.program_id(2)

        @pl.when(k == 0)
        def _():
            acc_ref[...] = part

        @pl.when(k > 0)
        def _():
            acc_ref[...] += part

        @pl.when(k == nk - 1)
        def _():
            finish(acc_ref[...])

    a_spec = pl.BlockSpec((tk, tm), lambda i, j, k: (k, i)) if ta else pl.BlockSpec((tm, tk), lambda i, j, k: (i, k))
    b_spec = pl.BlockSpec((tn, tk), lambda i, j, k: (j, k)) if tb else pl.BlockSpec((tk, tn), lambda i, j, k: (k, j))
    o_spec = pl.BlockSpec((tm, tn), lambda i, j, k: (i, j))
    in_specs = [a_spec, b_spec] + ([o_spec] if has_add else [])
    args = (a, b) + ((add,) if has_add else ())
    return pl.pallas_call(
        body, name=name, grid=(m_dim // tm, n_dim // tn, nk),
        in_specs=in_specs, out_specs=o_spec,
        out_shape=jax.ShapeDtypeStruct((m_dim, n_dim), out_dtype),
        scratch_shapes=[pltpu.VMEM((tm, tn), F32)] if nk > 1 else [],
        compiler_params=_params(("parallel", "parallel", "arbitrary")),
    )(*args)


def _mm_core(name, a, b, a_spec, b_spec, o_spec, out_shape, grid, dims, *, add=None, out_dtype=F32, rider=None):
    nk = grid[2]
    has_add = add is not None
    n_in = 3 if has_add else 2
    acc_shape = tuple(d for d in o_spec.block_shape if d is not None)
    extra, extra_shapes, extra_sems = _rider_parts(rider)
    n_w = len(extra)

    def body(*refs):
        a_ref, b_ref = refs[:2]
        o_ref = refs[n_in + n_w]
        step = (pl.program_id(0) * grid[1] + pl.program_id(1)) * nk + pl.program_id(2)
        begin, end = _rider_hooks(rider, refs[n_in:n_in + n_w], refs[n_in + n_w + 1:n_in + 2 * n_w + 1], refs[-2:],
                                  step, grid[0] * grid[1] * nk)
        begin()
        part = lax.dot_general(a_ref[...].astype(BF16), b_ref[...].astype(BF16), dims, preferred_element_type=F32)

        def finish(r):
            if has_add:
                r = refs[2][...] + r
            o_ref[...] = r.astype(out_dtype)

        if nk == 1:
            finish(part)
        else:
            acc_ref = refs[n_in + 2 * n_w + 1]
            k = pl.program_id(2)

            @pl.when(k == 0)
            def _():
                acc_ref[...] = part

            @pl.when(k > 0)
            def _():
                acc_ref[...] += part

            @pl.when(k == nk - 1)
            def _():
                finish(acc_ref[...])
        end()

    in_specs = [a_spec, b_spec] + ([o_spec] if has_add else []) + [ANY] * n_w
    args = (a, b) + ((add,) if has_add else ()) + extra
    res = pl.pallas_call(
        body, name=name, grid=grid, in_specs=in_specs, out_specs=[o_spec] + [ANY] * n_w,
        out_shape=[jax.ShapeDtypeStruct(out_shape, out_dtype)] + extra_shapes,
        scratch_shapes=([pltpu.VMEM(acc_shape, F32)] if nk > 1 else []) + extra_sems,
        compiler_params=_params(("arbitrary",) * 3 if n_w else ("parallel", "parallel", "arbitrary")),
    )(*args)
    return (res[0], res[1:]) if n_w else res[0]


def _mm_cs(a, w3, *, name):
    m_dim, k_dim = a.shape
    _, _, n4 = w3.shape
    tm, tn, tk = _tile(m_dim, TOK_CAP), _tile(n4, MM_CAP), _tile(k_dim, MM_CAP)
    npb = n4 // tn
    return _mm_core(name, a, w3,
                    pl.BlockSpec((tm, tk), lambda i, j, k: (i, k)),
                    pl.BlockSpec((None, tk, tn), lambda i, j, k: (j // npb, k, j % npb)),
                    pl.BlockSpec((tm, tn), lambda i, j, k: (i, j)),
                    (m_dim, N_CHIPS * n4), (m_dim // tm, N_CHIPS * npb, k_dim // tk), NN)


def _mm_cs_dx(dy, w3, *, name, out_dtype=F32, rider=None):
    m_dim, _ = dy.shape
    _, k_dim, n4 = w3.shape
    tm, tkw, tn = _tile(m_dim, TOK_CAP), _tile(k_dim, MM_CAP), _tile(n4, MM_CAP)
    npb = n4 // tn
    return _mm_core(name, dy, w3,
                    pl.BlockSpec((tm, tn), lambda i, j, k: (i, k)),
                    pl.BlockSpec((None, tkw, tn), lambda i, j, k: (k // npb, j, k % npb)),
                    pl.BlockSpec((tm, tkw), lambda i, j, k: (i, j)),
                    (m_dim, k_dim), (m_dim // tm, k_dim // tkw, N_CHIPS * npb), NT, out_dtype=out_dtype, rider=rider)


def _mm_cs_dw(a, dy, *, name):
    m_dim, k_dim = a.shape
    n4 = dy.shape[1] // N_CHIPS
    tmk, tn, tk = _tile(k_dim, MM_CAP), _tile(n4, MM_CAP), _tile(m_dim, TOK_CAP)
    npb = n4 // tn
    return _mm_core(name, a, dy,
                    pl.BlockSpec((tk, tmk), lambda i, j, k: (k, i)),
                    pl.BlockSpec((tk, tn), lambda i, j, k: (k, j)),
                    pl.BlockSpec((None, tmk, tn), lambda i, j, k: (j // npb, i, j % npb)),
                    (N_CHIPS, k_dim, n4), (k_dim // tmk, N_CHIPS * npb, m_dim // tk), TN)


def _mm_sm(a, w3, *, name, add=None, out_dtype=F32, tt=1024):
    t_dim, k_dim = a.shape
    n_s, _, n4 = w3.shape
    has_add = add is not None

    def body(*refs):
        a_ref, w_ref, o_ref = refs[0], refs[1], refs[-1]
        av = a_ref[...].astype(BF16)
        for s in range(n_s):
            cols = slice(s * n4, (s + 1) * n4)
            r = jnp.dot(av, w_ref[s], preferred_element_type=F32)
            if has_add:
                r = refs[2][:, cols] + r
            o_ref[:, cols] = r.astype(out_dtype)

    row = pl.BlockSpec((tt, n_s * n4), lambda i: (i, 0))
    return pl.pallas_call(
        body, name=name, grid=(t_dim // tt,),
        in_specs=[pl.BlockSpec((tt, k_dim), lambda i: (i, 0)), pl.BlockSpec(w3.shape, lambda i: (0, 0, 0))]
        + ([row] if has_add else []),
        out_specs=row, out_shape=jax.ShapeDtypeStruct((t_dim, n_s * n4), out_dtype),
        compiler_params=_params(("parallel",)),
    )(*((a, w3) + ((add,) if has_add else ())))


def _mm_sm_dx(dy, w3, *, name, out_dtype=F32, tt=1024):
    t_dim, _ = dy.shape
    n_s, k_dim, n4 = w3.shape

    def body(dy_ref, w_ref, o_ref):
        dyv = dy_ref[...].astype(BF16)
        acc = lax.dot_general(dyv[:, :n4], w_ref[0], NT, preferred_element_type=F32)
        for s in range(1, n_s):
            acc = acc + lax.dot_general(dyv[:, s * n4:(s + 1) * n4], w_ref[s], NT, preferred_element_type=F32)
        o_ref[...] = acc.astype(out_dtype)

    return pl.pallas_call(
        body, name=name, grid=(t_dim // tt,),
        in_specs=[pl.BlockSpec((tt, n_s * n4), lambda i: (i, 0)), pl.BlockSpec(w3.shape, lambda i: (0, 0, 0))],
        out_specs=pl.BlockSpec((tt, k_dim), lambda i: (i, 0)),
        out_shape=jax.ShapeDtypeStruct((t_dim, k_dim), out_dtype),
        compiler_params=_params(("parallel",)),
    )(dy, w3)


def _mm_sm_dw(a, dy, *, name, tk=1024):
    t_dim, k_dim = a.shape
    n4 = dy.shape[1] // N_CHIPS

    def body(a_ref, dy_ref, o_ref):
        av = a_ref[...].astype(BF16)
        dyv = dy_ref[...].astype(BF16)

        @pl.when(pl.program_id(0) == 0)
        def _():
            o_ref[...] = jnp.zeros_like(o_ref)

        for s in range(N_CHIPS):
            o_ref[s] += lax.dot_general(av, dyv[:, s * n4:(s + 1) * n4], TN, preferred_element_type=F32)

    return pl.pallas_call(
        body, name=name, grid=(t_dim // tk,),
        in_specs=[pl.BlockSpec((tk, k_dim), lambda i: (i, 0)), pl.BlockSpec((tk, N_CHIPS * n4), lambda i: (i, 0))],
        out_specs=pl.BlockSpec((N_CHIPS, k_dim, n4), lambda i: (0, 0, 0)),
        out_shape=jax.ShapeDtypeStruct((N_CHIPS, k_dim, n4), F32),
        compiler_params=_params(("arbitrary",)),
    )(a, dy)


def _ffn_up_dx(dg3, du3, wg3, wu3, *, tt=512):
    n_s, t_dim, f4 = dg3.shape
    d = wg3.shape[2]

    def body(dg_ref, du_ref, wg_ref, wu_ref, o_ref):
        acc = None
        for s in range(n_s):
            part = (jnp.dot(dg_ref[s], wg_ref[s], preferred_element_type=F32)
                    + jnp.dot(du_ref[s], wu_ref[s], preferred_element_type=F32))
            acc = part if acc is None else acc + part
        o_ref[...] = acc

    a_spec = pl.BlockSpec((n_s, tt, f4), lambda i: (0, i, 0))
    w_spec = pl.BlockSpec(wg3.shape, lambda i: (0, 0, 0))
    return pl.pallas_call(
        body, name="ffn_up_dx", grid=(t_dim // tt,),
        in_specs=[a_spec, a_spec, w_spec, w_spec], out_specs=pl.BlockSpec((tt, d), lambda i: (i, 0)),
        out_shape=jax.ShapeDtypeStruct((t_dim, d), F32),
        compiler_params=_params(("parallel",)),
    )(dg3, du3, wg3, wu3)


def _mm_ffn_down_dw(act3, dh, *, name, tk=1024):
    n_s, t_dim, f4 = act3.shape
    d = dh.shape[1]

    def body(a_ref, b_ref, o_ref):
        bv = b_ref[...].astype(BF16)

        @pl.when(pl.program_id(0) == 0)
        def _():
            o_ref[...] = jnp.zeros_like(o_ref)

        for s in range(n_s):
            o_ref[s] += lax.dot_general(a_ref[s], bv, TN, preferred_element_type=F32)

    return pl.pallas_call(
        body, name=name, grid=(t_dim // tk,),
        in_specs=[pl.BlockSpec((n_s, tk, f4), lambda i: (0, i, 0)), pl.BlockSpec((tk, d), lambda i: (i, 0))],
        out_specs=pl.BlockSpec((n_s, f4, d), lambda i: (0, 0, 0)),
        out_shape=jax.ShapeDtypeStruct((n_s, f4, d), F32),
        compiler_params=_params(("arbitrary",)),
    )(act3, dh)


def _rms_fwd(x, g, *, name, tt=512, rider=None):
    t_dim, d = x.shape
    tt = _tile(t_dim, tt, 8)
    extra, extra_shapes, extra_sems = _rider_parts(rider)
    n_w = len(extra)

    def body(*refs):
        x_ref, g_ref, o_ref = refs[0], refs[1], refs[2 + n_w]
        begin, end = _rider_hooks(rider, refs[2:2 + n_w], refs[3 + n_w:3 + 2 * n_w], refs[-2:], pl.program_id(0),
                                  t_dim // tt)
        begin()
        xv = x_ref[...]
        r = lax.rsqrt(jnp.mean(xv * xv, axis=-1, keepdims=True) + RMS_EPS)
        o_ref[...] = ((xv * r) * g_ref[...]).astype(o_ref.dtype)
        end()

    res = pl.pallas_call(
        body, name=name, grid=(t_dim // tt,),
        in_specs=[pl.BlockSpec((tt, d), lambda i: (i, 0)), pl.BlockSpec((1, d), lambda i: (0, 0))] + [ANY] * n_w,
        out_specs=[pl.BlockSpec((tt, d), lambda i: (i, 0))] + [ANY] * n_w,
        out_shape=[jax.ShapeDtypeStruct((t_dim, d), BF16)] + extra_shapes,
        scratch_shapes=extra_sems,
        compiler_params=_params(("arbitrary",) if n_w else ("parallel",)),
    )(x, g, *extra)
    return (res[0], res[1:]) if n_w else res[0]


def _rms_bwd(x, g, dy, add, *, name, tt=512, rider=None):
    t_dim, d = x.shape
    tt = _tile(t_dim, tt, 8)
    has_add = add is not None
    n_in = 4 if has_add else 3
    extra, extra_shapes, extra_sems = _rider_parts(rider)
    n_w = len(extra)

    def body(*refs):
        x_ref, g_ref, dy_ref = refs[:3]
        add_ref = refs[3] if has_add else None
        dx_ref, dg_ref = refs[n_in + n_w:n_in + n_w + 2]
        begin, end = _rider_hooks(rider, refs[n_in:n_in + n_w], refs[n_in + n_w + 2:n_in + 2 * n_w + 2], refs[-2:],
                                  pl.program_id(0), t_dim // tt)
        begin()
        xv = x_ref[...]
        dyv = dy_ref[...].astype(F32)
        r = lax.rsqrt(jnp.mean(xv * xv, axis=-1, keepdims=True) + RMS_EPS)
        xh = xv * r
        u = dyv * g_ref[...]
        dx = r * (u - xh * jnp.mean(u * xh, axis=-1, keepdims=True))
        if has_add:
            dx = add_ref[...] + dx
        dx_ref[...] = dx

        @pl.when(pl.program_id(0) == 0)
        def _():
            dg_ref[...] = jnp.zeros_like(dg_ref)

        dg_ref[...] += jnp.sum(dyv * xh, axis=0, keepdims=True)
        end()

    row = pl.BlockSpec((tt, d), lambda i: (i, 0))
    vec = pl.BlockSpec((1, d), lambda i: (0, 0))
    in_specs = [row, vec, row] + ([row] if has_add else []) + [ANY] * n_w
    args = (x, g, dy) + ((add,) if has_add else ()) + extra
    res = pl.pallas_call(
        body, name=name, grid=(t_dim // tt,),
        in_specs=in_specs, out_specs=[row, vec] + [ANY] * n_w,
        out_shape=[jax.ShapeDtypeStruct((t_dim, d), F32), jax.ShapeDtypeStruct((1, d), F32)] + extra_shapes,
        scratch_shapes=extra_sems,
        compiler_params=_params(("arbitrary",)),
    )(*args)
    return (res[0], res[1], res[2:]) if n_w else (res[0], res[1])


def _down_final(act3, wd3, h, g, target, *, tt=512):
    n_s, t_dim, f4 = act3.shape
    d = h.shape[1]
    n_steps = t_dim // tt

    def body(a_ref, w_ref, h_ref, g_ref, t_ref, loss_ref, dh_ref, dg_ref, sq_ref):
        i = pl.program_id(0)
        xv = h_ref[...]
        for s in range(n_s):
            xv = xv + jnp.dot(a_ref[s], w_ref[s], preferred_element_type=F32)
        gv = g_ref[...]
        r = lax.rsqrt(jnp.mean(xv * xv, axis=-1, keepdims=True) + RMS_EPS)
        xh = xv * r
        err = xh * gv - t_ref[...]
        dyv = err * (1.0 / d)
        u = dyv * gv
        dh_ref[...] = r * (u - xh * jnp.mean(u * xh, axis=-1, keepdims=True))

        @pl.when(i == 0)
        def _():
            dg_ref[...] = jnp.zeros_like(dg_ref)
            sq_ref[...] = jnp.zeros_like(sq_ref)

        dg_ref[...] += jnp.sum(dyv * xh, axis=0, keepdims=True)
        sq_ref[...] += jnp.sum(err * err, axis=0, keepdims=True)

        @pl.when(i == n_steps - 1)
        def _():
            total = jnp.sum(sq_ref[...], axis=-1, keepdims=True) * (0.5 / d)
            loss_ref[...] = jnp.broadcast_to(total, loss_ref.shape)

    row = pl.BlockSpec((tt, d), lambda i: (i, 0))
    vec = pl.BlockSpec((1, d), lambda i: (0, 0))
    return pl.pallas_call(
        body, name="down_final_loss", grid=(n_steps,),
        in_specs=[pl.BlockSpec((n_s, tt, f4), lambda i: (0, i, 0)), pl.BlockSpec(wd3.shape, lambda i: (0, 0, 0)),
                  row, vec, row],
        out_specs=[pl.BlockSpec((1, LANES), lambda i: (0, 0)), row, vec],
        out_shape=[jax.ShapeDtypeStruct((1, LANES), F32), jax.ShapeDtypeStruct((t_dim, d), F32),
                   jax.ShapeDtypeStruct((1, d), F32)],
        scratch_shapes=[pltpu.VMEM((1, d), F32)],
        compiler_params=_params(("arbitrary",)),
    )(act3, wd3, h, g, target)


def _rope_table(pos, inv_lane, sel_a, sel_b, *, tt=512):
    t_dim = pos.shape[0]

    def body(p_ref, f_ref, a_ref, b_ref, c_ref, s_ref):
        ang = p_ref[...] * f_ref[...]
        on = (a_ref[...] + b_ref[...]) > 0.0
        c_ref[...] = jnp.where(on, jnp.cos(ang), 1.0)
        s_ref[...] = jnp.where(on, jnp.sin(ang), 0.0)

    vec = pl.BlockSpec((1, LANES), lambda i: (0, 0))
    row = pl.BlockSpec((tt, LANES), lambda i: (i, 0))
    shp = jax.ShapeDtypeStruct((t_dim, LANES), F32)
    return pl.pallas_call(
        body, name="rope_table", grid=(t_dim // tt,),
        in_specs=[pl.BlockSpec((tt, 1), lambda i: (i, 0)), vec, vec, vec],
        out_specs=[row, row], out_shape=[shp, shp],
        compiler_params=_params(("parallel",)),
    )(pos, inv_lane, sel_a, sel_b)


def _rotate(xv, cs, sn, sa, sb):
    half = ROPE_DIM // 2
    up = pltpu.roll(xv, LANES - half, 1)
    dn = pltpu.roll(xv, half, 1)
    return xv * cs + (dn * sb - up * sa) * sn


def _head_masks():
    h1 = lax.broadcasted_iota(jnp.int32, (1, LANES), 1) < HEAD_DIM
    return h1, jnp.logical_not(h1)


def _split_heads(xv, h1, h2):
    return jnp.where(h1, xv, 0.0).astype(BF16), jnp.where(h2, xv, 0.0).astype(BF16)


def _tri_masks():
    r = lax.broadcasted_iota(jnp.int32, (BLOCK, BLOCK), 0)
    c = lax.broadcasted_iota(jnp.int32, (BLOCK, BLOCK), 1)
    return c <= r, r <= c


def _stream_rows(start, dil):
    if dil == 1:
        return pl.ds(pl.multiple_of(start, BLOCK), BLOCK)
    return pl.ds(start, BLOCK, stride=dil)


def _dil_tile(idx, dil, nb):
    r = idx // nb
    n = idx % nb
    return (_stream_rows(r + dil * BLOCK * n, dil), _stream_rows(r + dil * BLOCK * jnp.maximum(n - 1, 0), dil),
            n > 0)


def _dil_specs(b_dim, s_dim):
    def col(c0):
        return pl.BlockSpec((None, s_dim, LANES),lambda b, h: (b, 0, c0 + h))
    tab = pl.BlockSpec((None, s_dim, LANES),lambda b, h: (b, 0, 0))
    vec = pl.BlockSpec((1, LANES), lambda b, h: (0, 0))
    return col, tab, vec


def _dil_fwd(proj3, cs3, sn3, sel_a, sel_b, rider=None):
    b_dim, s_dim, _ = proj3.shape
    scale = HEAD_DIM ** -0.5
    n_pat = len(DIL_PATTERNS)
    extra, extra_shapes, extra_sems = _rider_parts(rider)
    n_w = len(extra)
    n_steps = b_dim * PAIRS

    def body(*refs):
        q_ref, k_ref, v_ref, cs_ref, sn_ref, sa_ref, sb_ref = refs[:7]
        o16_ref, o32_ref, l_ref = refs[7 + n_w:10 + n_w]
        qr, kr = refs[10 + 2 * n_w:12 + 2 * n_w]
        per_pattern = refs[12 + 2 * n_w:12 + 2 * n_w + 2 * n_pat]
        og, lg = per_pattern[:n_pat], per_pattern[n_pat:]
        step = pl.program_id(0) * PAIRS + pl.program_id(1)
        begin, end = _rider_hooks(rider, refs[7:7 + n_w], refs[10 + n_w:10 + 2 * n_w], refs[-2:], step, n_steps)
        begin()
        h1, h2 = _head_masks()
        cur_ok, prev_ok = _tri_masks()
        sa, sb = sa_ref[...], sb_ref[...]

        def prep(j, _):
            rows = pl.ds(pl.multiple_of(j * DIL_CHUNK, DIL_CHUNK), DIL_CHUNK)
            cs, sn = cs_ref[rows, :], sn_ref[rows, :]
            qr[rows, :] = _rotate(q_ref[rows, :], cs, sn, sa, sb) * scale
            kr[rows, :] = _rotate(k_ref[rows, :], cs, sn, sa, sb)
            return 0

        lax.fori_loop(0, s_dim // DIL_CHUNK, prep, 0)

        for g, (_, dil) in enumerate(DIL_PATTERNS):
            nb = s_dim // dil // BLOCK

            def some(bi, _, g=g, dil=dil, nb=nb):
                tiles = [_dil_tile(bi * DIL_BATCH + t, dil, nb) for t in range(DIL_BATCH)]
                rows = [t[0] for t in tiles]
                q1, q2 = _split_heads(jnp.stack([qr[rw, :] for rw in rows]), h1, h2)
                kc = jnp.stack([kr[rw, :] for rw in rows]).astype(BF16)
                vc1, vc2 = _split_heads(jnp.stack([v_ref[rw, :] for rw in rows]), h1, h2)
                if nb > 1:
                    kp = jnp.stack([kr[t[1], :] for t in tiles]).astype(BF16)
                    vp1, vp2 = _split_heads(jnp.stack([v_ref[t[1], :] for t in tiles]), h1, h2)
                    p_ok = jnp.stack([jnp.logical_and(prev_ok, t[2]) for t in tiles])

                def head(qh, vch, vph):
                    sc = jnp.where(cur_ok, lax.dot_general(qh, kc, BNT, preferred_element_type=F32), -jnp.inf)
                    m = jnp.max(sc, axis=-1, keepdims=True)
                    if nb > 1:
                        sp = jnp.where(p_ok, lax.dot_general(qh, kp, BNT, preferred_element_type=F32), -jnp.inf)
                        m = jnp.maximum(m, jnp.max(sp, axis=-1, keepdims=True))
                    pc = jnp.exp(sc - m)
                    den = jnp.sum(pc, axis=-1, keepdims=True)
                    acc = lax.dot_general(pc.astype(BF16), vch, BNN, preferred_element_type=F32)
                    if nb > 1:
                        pp = jnp.exp(sp - m)
                        den = den + jnp.sum(pp, axis=-1, keepdims=True)
                        acc = acc + lax.dot_general(pp.astype(BF16), vph, BNN, preferred_element_type=F32)
                    return acc / den, m + jnp.log(den)

                o1, l1 = head(q1, vc1, vp1 if nb > 1 else None)
                o2, l2 = head(q2, vc2, vp2 if nb > 1 else None)
                o, l = o1 + o2, jnp.where(h1, l1, l2)
                for t, rw in enumerate(rows):
                    og[g][rw, :] = o[t]
                    lg[g][rw, :] = l[t]
                return 0

            lax.fori_loop(0, dil * nb // DIL_BATCH, some, 0)

        def comb(j, _):
            rows = pl.ds(pl.multiple_of(j * DIL_CHUNK, DIL_CHUNK), DIL_CHUNK)
            ls = [lg[g][rows, :] for g in range(n_pat)]
            m = jnp.maximum(jnp.maximum(ls[0], ls[1]), ls[2])
            es = [jnp.exp(l - m) for l in ls]
            den = es[0] + es[1] + es[2]
            o = (es[0] * og[0][rows, :] + es[1] * og[1][rows, :] + es[2] * og[2][rows, :]) / den
            o16_ref[rows, :] = o.astype(BF16)
            o32_ref[rows, :] = o
            l_ref[rows, :] = m + jnp.log(den)
            return 0

        lax.fori_loop(0, s_dim // DIL_CHUNK, comb, 0)
        end()

    col, tab, vec = _dil_specs(b_dim, s_dim)
    out = pl.BlockSpec((None, s_dim, LANES),lambda b, h: (b, 0, h))
    shp = (b_dim, s_dim, ATT_WIDTH)
    res = pl.pallas_call(
        body, name="dil_fwd", grid=(b_dim, PAIRS),
        in_specs=[col(COL_QA), col(COL_KA), col(COL_VA), tab, tab, vec, vec] + [ANY] * n_w,
        out_specs=[out, out, out] + [ANY] * n_w,
        out_shape=[jax.ShapeDtypeStruct(shp, BF16), jax.ShapeDtypeStruct(shp, F32), jax.ShapeDtypeStruct(shp, F32)]
        + extra_shapes,
        scratch_shapes=[pltpu.VMEM((s_dim, LANES), F32)] * (2 + 2 * n_pat) + extra_sems,
        compiler_params=_params(("arbitrary", "arbitrary")),
    )(proj3, proj3, proj3, cs3, sn3, sel_a, sel_b, *extra)
    return res[:3], res[3:]


def _dil_bwd(proj3, cs3, sn3, sel_a, sel_b, do3, o3, lse3, rider=None):
    b_dim, s_dim, _ = proj3.shape
    scale = HEAD_DIM ** -0.5
    extra, extra_shapes, extra_sems = _rider_parts(rider)
    n_w = len(extra)

    def body(*refs):
        q_ref, k_ref, v_ref, cs_ref, sn_ref, sa_ref, sb_ref, do_ref, o_ref, l_ref = refs[:10]
        dq_ref, dk_ref, dv_ref = refs[10 + n_w:13 + n_w]
        qr, kr, dqa, dka, dva = refs[13 + 2 * n_w:18 + 2 * n_w]
        step = pl.program_id(0) * PAIRS + pl.program_id(1)
        begin, end = _rider_hooks(rider, refs[10:10 + n_w], refs[13 + n_w:13 + 2 * n_w], refs[-2:], step,
                                  b_dim * PAIRS)
        begin()
        h1, h2 = _head_masks()
        cur_ok, prev_ok = _tri_masks()
        sa, sb = sa_ref[...], sb_ref[...]

        def prep(j, _):
            rows = pl.ds(pl.multiple_of(j * DIL_CHUNK, DIL_CHUNK), DIL_CHUNK)
            cs, sn = cs_ref[rows, :], sn_ref[rows, :]
            qr[rows, :] = _rotate(q_ref[rows, :], cs, sn, sa, sb) * scale
            kr[rows, :] = _rotate(k_ref[rows, :], cs, sn, sa, sb)
            zero = jnp.zeros((DIL_CHUNK, LANES), F32)
            dqa[rows, :] = zero
            dka[rows, :] = zero
            dva[rows, :] = zero
            return 0

        lax.fori_loop(0, s_dim // DIL_CHUNK, prep, 0)

        for _, dil in DIL_PATTERNS:
            nb = s_dim // dil // BLOCK

            def some(bi, _, dil=dil, nb=nb):
                tiles = [_dil_tile(bi * DIL_BATCH + t, dil, nb) for t in range(DIL_BATCH)]
                rows = [t[0] for t in tiles]
                q1, q2 = _split_heads(jnp.stack([qr[rw, :] for rw in rows]), h1, h2)
                dof = jnp.stack([do_ref[rw, :] for rw in rows])
                do1, do2 = _split_heads(dof, h1, h2)
                prod = dof * jnp.stack([o_ref[rw, :] for rw in rows])
                delta1 = jnp.sum(jnp.where(h1, prod, 0.0), axis=-1, keepdims=True)
                delta2 = jnp.sum(jnp.where(h2, prod, 0.0), axis=-1, keepdims=True)
                lt = jnp.stack([l_ref[rw, :] for rw in rows])
                lse1 = jnp.max(jnp.where(h1, lt, -jnp.inf), axis=-1, keepdims=True)
                lse2 = jnp.max(jnp.where(h2, lt, -jnp.inf), axis=-1, keepdims=True)

                def side(krows, ok):
                    kf = jnp.stack([kr[kw, :] for kw in krows])
                    k16 = kf.astype(BF16)
                    k1, k2 = _split_heads(kf, h1, h2)
                    v16 = jnp.stack([v_ref[kw, :] for kw in krows]).astype(BF16)

                    def head(qh, doh, lse, delta):
                        sc = lax.dot_general(qh, k16, BNT, preferred_element_type=F32)
                        p = jnp.where(ok, jnp.exp(sc - lse), 0.0)
                        dp = lax.dot_general(doh, v16, BNT, preferred_element_type=F32)
                        return p.astype(BF16), (p * (dp - delta)).astype(BF16)

                    p1, ds1 = head(q1, do1, lse1, delta1)
                    p2, ds2 = head(q2, do2, lse2, delta2)
                    dv = (lax.dot_general(p1, do1, BTN, preferred_element_type=F32)
                          + lax.dot_general(p2, do2, BTN, preferred_element_type=F32))
                    dk = (lax.dot_general(ds1, q1, BTN, preferred_element_type=F32)
                          + lax.dot_general(ds2, q2, BTN, preferred_element_type=F32))
                    for t, kw in enumerate(krows):
                        dva[kw, :] += dv[t]
                        dka[kw, :] += dk[t]
                    return (lax.dot_general(ds1, k1, BNN, preferred_element_type=F32)
                            + lax.dot_general(ds2, k2, BNN, preferred_element_type=F32))

                dq = side(rows, cur_ok)
                if nb > 1:
                    dq = dq + side([t[1] for t in tiles], jnp.stack([jnp.logical_and(prev_ok, t[2]) for t in tiles]))
                for t, rw in enumerate(rows):
                    dqa[rw, :] += dq[t] * scale
                return 0

            lax.fori_loop(0, dil * nb // DIL_BATCH, some, 0)

        def finish(j, _):
            rows = pl.ds(pl.multiple_of(j * DIL_CHUNK, DIL_CHUNK), DIL_CHUNK)
            cs, sn = cs_ref[rows, :], -sn_ref[rows, :]
            dq_ref[rows, :] = _rotate(dqa[rows, :], cs, sn, sa, sb).astype(BF16)
            dk_ref[rows, :] = _rotate(dka[rows, :], cs, sn, sa, sb).astype(BF16)
            dv_ref[rows, :] = dva[rows, :].astype(BF16)
            return 0

        lax.fori_loop(0, s_dim // DIL_CHUNK, finish, 0)
        end()

    col, tab, vec = _dil_specs(b_dim, s_dim)
    out = pl.BlockSpec((None, s_dim, LANES),lambda b, h: (b, 0, h))
    shp = jax.ShapeDtypeStruct((b_dim, s_dim, ATT_WIDTH), BF16)
    acc = pltpu.VMEM((s_dim, LANES), F32)
    res = pl.pallas_call(
        body, name="dil_bwd", grid=(b_dim, PAIRS),
        in_specs=[col(COL_QA), col(COL_KA), col(COL_VA), tab, tab, vec, vec, out, out, out] + [ANY] * n_w,
        out_specs=[out, out, out] + [ANY] * n_w, out_shape=[shp, shp, shp] + extra_shapes,
        scratch_shapes=[acc, acc, acc, acc, acc] + extra_sems,
        compiler_params=_params(("arbitrary", "arbitrary")),
    )(proj3, proj3, proj3, cs3, sn3, sel_a, sel_b, do3, o3, lse3, *extra)
    return res[:3], res[3:]


def _split_dot(x, tri):
    hi = x.astype(BF16)
    lo = (x - hi.astype(F32)).astype(BF16)
    return jnp.dot(hi, tri, preferred_element_type=F32) + jnp.dot(lo, tri, preferred_element_type=F32)


def _log_sigmoid(z):
    return jnp.minimum(z, 0.0) - jnp.log(1.0 + jnp.exp(-jnp.abs(z)))


def _sb_scores(qh, k16, valid):
    z = lax.dot_general(qh, k16, NT, preferred_element_type=F32)
    ls = _log_sigmoid(z)
    l1m = ls - z
    return ls, (l1m if valid is None else jnp.where(valid, l1m, 0.0))


def _sb_consts():
    r = lax.broadcasted_iota(jnp.int32, (BLOCK, BLOCK), 0)
    c = lax.broadcasted_iota(jnp.int32, (BLOCK, BLOCK), 1)
    after = (r > c).astype(BF16)
    before = (r < c).astype(BF16)
    qrow = lax.broadcasted_iota(jnp.int32, (SB_ROWS, BLOCK), 0)
    kcol = lax.broadcasted_iota(jnp.int32, (SB_ROWS, BLOCK), 1)
    return after, before, qrow, kcol


def _below(whole, lo, delta):
    if lo == 0:
        return whole + delta
    return whole + jnp.concatenate([jnp.zeros((lo,) + delta.shape[1:], delta.dtype), delta], axis=0)


def _pairs_loop(n_blocks, step, carry, per_iter):
    def several(i, c):
        for j in range(per_iter):
            c = step(per_iter * i + j, c)
        return c

    return lax.fori_loop(0, n_blocks // per_iter, several, carry)


def _sb_fwd(proj3, rider=None):
    b_dim, s_dim, _ = proj3.shape
    scale = HEAD_DIM ** -0.5
    per = SB_ROWS // BLOCK
    extra, extra_shapes, extra_sems = _rider_parts(rider)
    n_w = len(extra)

    def body(*refs):
        q_ref, k_ref, v_ref = refs[:3]
        o_ref = refs[3 + n_w]
        step = pl.program_id(0) * PAIRS + pl.program_id(1)
        begin, end = _rider_hooks(rider, refs[3:3 + n_w], refs[4 + n_w:4 + 2 * n_w], refs[-2:], step, b_dim * PAIRS)
        begin()
        h1, h2 = _head_masks()
        after, _, qrow, kcol = _sb_consts()

        def qloop(qi, _):
            rows = pl.ds(pl.multiple_of(qi * SB_ROWS, SB_ROWS), SB_ROWS)
            q1, q2 = _split_heads(q_ref[rows, :] * scale, h1, h2)
            first = qi * per

            def block(kb, carry, lo):
                acc, run1, run2 = carry
                krows = pl.ds(pl.multiple_of(kb * BLOCK, BLOCK), BLOCK)
                k16 = k_ref[krows, :].astype(BF16)
                v1, v2 = _split_heads(v_ref[krows, :], h1, h2)
                valid = None if lo is None else kcol[:SB_ROWS - lo] < qrow[:SB_ROWS - lo]
                lo = lo or 0

                def head(qh, vh, run):
                    ls, l1m = _sb_scores(qh[lo:], k16, valid)
                    a = jnp.exp(ls + _split_dot(l1m, after) + run[lo:])
                    if valid is not None:
                        a = jnp.where(valid, a, 0.0)
                    return (jnp.dot(a.astype(BF16), vh, preferred_element_type=F32),
                            _below(run, lo, jnp.sum(l1m, axis=-1, keepdims=True)))

                o1, run1 = head(q1, v1, run1)
                o2, run2 = head(q2, v2, run2)
                return _below(acc, lo, o1 + o2), run1, run2

            zcol = jnp.zeros((SB_ROWS, 1), F32)
            carry = (jnp.zeros((SB_ROWS, LANES), F32), zcol, zcol)
            for kl in reversed(range(per)):
                carry = block(first + kl, carry, kl * BLOCK)
            acc, _, _ = _pairs_loop(first, lambda i, c: block(first - 1 - i, c, None), carry, SB_STEP_FWD)
            o_ref[rows, :] = acc.astype(BF16)
            return 0

        lax.fori_loop(0, s_dim // SB_ROWS, qloop, 0)
        end()

    def col(c0):
        return pl.BlockSpec((None, s_dim, LANES),lambda b, h: (b, 0, c0 + h))

    res = pl.pallas_call(
        body, name="sb_fwd", grid=(b_dim, PAIRS),
        in_specs=[col(COL_QB), col(COL_KB), col(COL_VB)] + [ANY] * n_w, out_specs=[col(0)] + [ANY] * n_w,
        out_shape=[jax.ShapeDtypeStruct((b_dim, s_dim, ATT_WIDTH), BF16)] + extra_shapes,
        scratch_shapes=extra_sems,
        compiler_params=_params(("arbitrary", "arbitrary")),
    )(proj3, proj3, proj3, *extra)
    return res[0], res[1:]


def _sb_bwd(proj3, do3, rider=None):
    b_dim, s_dim, _ = proj3.shape
    scale = HEAD_DIM ** -0.5
    per = SB_ROWS // BLOCK
    nkb_max = s_dim // BLOCK
    extra, extra_shapes, extra_sems = _rider_parts(rider)
    n_w = len(extra)

    def body(*refs):
        q_ref, k_ref, v_ref, do_ref = refs[:4]
        dq_ref, dk_ref, dv_ref = refs[4 + n_w:7 + n_w]
        dka, dva, e_ref, sg_ref = refs[7 + 2 * n_w:11 + 2 * n_w]
        step = pl.program_id(0) * PAIRS + pl.program_id(1)
        begin, end = _rider_hooks(rider, refs[4:4 + n_w], refs[7 + n_w:7 + 2 * n_w], refs[-2:], step, b_dim * PAIRS)
        begin()
        h1, h2 = _head_masks()
        after, before, qrow, kcol = _sb_consts()
        dka[...] = jnp.zeros_like(dka)
        dva[...] = jnp.zeros_like(dva)

        def qloop(qi, _):
            rows = pl.ds(pl.multiple_of(qi * SB_ROWS, SB_ROWS), SB_ROWS)
            q1, q2 = _split_heads(q_ref[rows, :] * scale, h1, h2)
            do1, do2 = _split_heads(do_ref[rows, :].astype(F32), h1, h2)
            first = qi * per

            def pass1(kb, carry, lo):
                run1, run2 = carry
                krows = pl.ds(pl.multiple_of(kb * BLOCK, BLOCK), BLOCK)
                k16 = k_ref[krows, :].astype(BF16)
                v16 = v_ref[krows, :].astype(BF16)
                valid = None if lo is None else kcol[:SB_ROWS - lo] < qrow[:SB_ROWS - lo]
                lo = lo or 0
                part = pl.ds(lo, SB_ROWS - lo)

                def head(h, qh, doh, run):
                    ls, l1m = _sb_scores(qh[lo:], k16, valid)
                    a = jnp.exp(ls + _split_dot(l1m, after) + run[lo:])
                    if valid is not None:
                        a = jnp.where(valid, a, 0.0)
                    da = lax.dot_general(doh[lo:], v16, NT, preferred_element_type=F32)
                    e_ref[h, kb, part, :] = a * da
                    sg_ref[h, kb, part, :] = jnp.exp(ls)
                    return a.astype(BF16), _below(run, lo, jnp.sum(l1m, axis=-1, keepdims=True))

                a1, run1 = head(0, q1, do1, run1)
                a2, run2 = head(1, q2, do2, run2)
                dva[krows, :] += (lax.dot_general(a1, do1[lo:], TN, preferred_element_type=F32)
                                  + lax.dot_general(a2, do2[lo:], TN, preferred_element_type=F32))
                return run1, run2

            zcol = jnp.zeros((SB_ROWS, 1), F32)
            carry = (zcol, zcol)
            for kl in reversed(range(per)):
                carry = pass1(first + kl, carry, kl * BLOCK)
            _pairs_loop(first, lambda i, c: pass1(first - 1 - i, c, None), carry, SB_STEP_BWD)

            def pass2(kb, carry, lo):
                dq, pre1, pre2 = carry
                krows = pl.ds(pl.multiple_of(kb * BLOCK, BLOCK), BLOCK)
                k1, k2 = _split_heads(k_ref[krows, :], h1, h2)
                valid = None if lo is None else kcol[:SB_ROWS - lo] < qrow[:SB_ROWS - lo]
                lo = lo or 0
                part = pl.ds(lo, SB_ROWS - lo)

                def head(h, pre):
                    ev = e_ref[h, kb, part, :]
                    sg = sg_ref[h, kb, part, :]
                    dz = ev * (1.0 - sg) - (_split_dot(ev, before) + pre[lo:]) * sg
                    if valid is not None:
                        dz = jnp.where(valid, dz, 0.0)
                    return dz.astype(BF16), _below(pre, lo, jnp.sum(ev, axis=-1, keepdims=True))

                dz1, pre1 = head(0, pre1)
                dz2, pre2 = head(1, pre2)
                dka[krows, :] += (lax.dot_general(dz1, q1[lo:], TN, preferred_element_type=F32)
                                  + lax.dot_general(dz2, q2[lo:], TN, preferred_element_type=F32))
                dq = _below(dq, lo, jnp.dot(dz1, k1, preferred_element_type=F32)
                            + jnp.dot(dz2, k2, preferred_element_type=F32))
                return dq, pre1, pre2

            carry = _pairs_loop(first, lambda i, c: pass2(i, c, None), (jnp.zeros((SB_ROWS, LANES), F32), zcol, zcol),
                                SB_STEP_BWD)
            for kl in range(per):
                carry = pass2(first + kl, carry, kl * BLOCK)
            dq = carry[0]
            dq_ref[rows, :] = (dq * scale).astype(BF16)
            return 0

        lax.fori_loop(0, s_dim // SB_ROWS, qloop, 0)
        dk_ref[...] = dka[...].astype(BF16)
        dv_ref[...] = dva[...].astype(BF16)
        end()

    def col(c0):
        return pl.BlockSpec((None, s_dim, LANES),lambda b, h: (b, 0, c0 + h))

    shp = jax.ShapeDtypeStruct((b_dim, s_dim, ATT_WIDTH), BF16)
    acc = pltpu.VMEM((s_dim, LANES), F32)
    strip = pltpu.VMEM((2, nkb_max, SB_ROWS, BLOCK), F32)
    res = pl.pallas_call(
        body, name="sb_bwd", grid=(b_dim, PAIRS),
        in_specs=[col(COL_QB), col(COL_KB), col(COL_VB), col(0)] + [ANY] * n_w,
        out_specs=[col(0), col(0), col(0)] + [ANY] * n_w,
        out_shape=[shp, shp, shp] + extra_shapes,
        scratch_shapes=[acc, acc, strip, strip] + extra_sems,
        compiler_params=_params(("arbitrary", "arbitrary")),
    )(proj3, proj3, proj3, do3, *extra)
    return res[:3], res[3:]


def _sigmoid(x):
    return 1.0 / (1.0 + jnp.exp(-x))


def _gate_out_norm(proj, ua, ub, w_out, x, g, *, tt=512):
    t_dim, d = ua.shape

    def body(ga_ref, gb_ref, ua_ref, ub_ref, w_ref, x_ref, g_ref, m_ref, h_ref, n_ref):
        mixed = (_sigmoid(ga_ref[...]) * ua_ref[...] + _sigmoid(gb_ref[...]) * ub_ref[...]).astype(BF16)
        m_ref[...] = mixed
        hv = x_ref[...] + jnp.dot(mixed, w_ref[...], preferred_element_type=F32)
        h_ref[...] = hv
        r = lax.rsqrt(jnp.mean(hv * hv, axis=-1, keepdims=True) + RMS_EPS)
        n_ref[...] = ((hv * r) * g_ref[...]).astype(BF16)

    row = pl.BlockSpec((tt, d), lambda i: (i, 0))
    return pl.pallas_call(
        body, name="gate_out_norm", grid=(t_dim // tt,),
        in_specs=[pl.BlockSpec((tt, d), lambda i: (i, 3)), pl.BlockSpec((tt, d), lambda i: (i, 4)), row, row,
                  pl.BlockSpec((d, d), lambda i: (0, 0)), row, pl.BlockSpec((1, d), lambda i: (0, 0))],
        out_specs=[row, row, row],
        out_shape=[jax.ShapeDtypeStruct((t_dim, d), BF16), jax.ShapeDtypeStruct((t_dim, d), F32),
                   jax.ShapeDtypeStruct((t_dim, d), BF16)],
        compiler_params=_params(("parallel",)),
    )(proj, proj, ua, ub, w_out, x, g)


def _out_dx_gate_bwd(dh, w_out, proj, ua, ub, *, tt=512):
    t_dim, d = ua.shape

    def body(dh_ref, w_ref, ga_ref, gb_ref, ua_ref, ub_ref, dua_ref, dub_ref, dg_ref):
        dm = lax.dot_general(dh_ref[...].astype(BF16), w_ref[...], NT, preferred_element_type=F32)
        sa = _sigmoid(ga_ref[...])
        sb = _sigmoid(gb_ref[...])
        dua_ref[...] = (dm * sa).astype(BF16)
        dub_ref[...] = (dm * sb).astype(BF16)
        dg_ref[:, :d] = (dm * ua_ref[...] * (sa * (1.0 - sa))).astype(BF16)
        dg_ref[:, d:] = (dm * ub_ref[...] * (sb * (1.0 - sb))).astype(BF16)

    row = pl.BlockSpec((tt, d), lambda i: (i, 0))
    wide = pl.BlockSpec((tt, 2 * d), lambda i: (i, 0))
    return pl.pallas_call(
        body, name="out_dx_gate_bwd", grid=(t_dim // tt,),
        in_specs=[row, pl.BlockSpec((d, d), lambda i: (0, 0)),
                  pl.BlockSpec((tt, d), lambda i: (i, 3)), pl.BlockSpec((tt, d), lambda i: (i, 4)), row, row],
        out_specs=[row, row, wide],
        out_shape=[jax.ShapeDtypeStruct((t_dim, d), BF16), jax.ShapeDtypeStruct((t_dim, d), BF16),
                   jax.ShapeDtypeStruct((t_dim, 2 * d), BF16)],
        compiler_params=_params(("parallel",)),
    )(dh, w_out, proj, proj, ua, ub)


def _ffn_up_swiglu(n, wg3, wu3, *, tt=1024):
    t_dim, d = n.shape
    n_s, f4, _ = wg3.shape

    def body(n_ref, wg_ref, wu_ref, g_ref, u_ref, a_ref):
        nv = n_ref[...]
        gv = lax.dot_general(nv, wg_ref[...], NT, preferred_element_type=F32)
        uv = lax.dot_general(nv, wu_ref[...], NT, preferred_element_type=F32)
        g_ref[...] = gv.astype(BF16)
        u_ref[...] = uv.astype(BF16)
        a_ref[...] = (gv * _sigmoid(gv) * uv).astype(BF16)

    wspec = pl.BlockSpec((None, f4, d), lambda i, s: (s, 0, 0))
    ospec = pl.BlockSpec((None, tt, f4), lambda i, s: (s, i, 0))
    shp = (n_s, t_dim, f4)
    return pl.pallas_call(
        body, name="ffn_up_swiglu", grid=(t_dim // tt, n_s),
        in_specs=[pl.BlockSpec((tt, d), lambda i, s: (i, 0)), wspec, wspec], out_specs=[ospec, ospec, ospec],
        out_shape=[jax.ShapeDtypeStruct(shp, BF16)] * 3,
        compiler_params=_params(("parallel", "parallel")),
    )(n, wg3, wu3)


def _ffn_down_dx_swiglu(dh, wd3, g3, u3, *, tt=1024):
    t_dim, d = dh.shape
    n_s, f4, _ = wd3.shape

    def body(dh_ref, w_ref, g_ref, u_ref, dg_ref, du_ref):
        da = lax.dot_general(dh_ref[...].astype(BF16), w_ref[...], NT, preferred_element_type=F32)
        gv = g_ref[...].astype(F32)
        sg = _sigmoid(gv)
        dg_ref[...] = (da * u_ref[...].astype(F32) * (sg + gv * sg * (1.0 - sg))).astype(BF16)
        du_ref[...] = (da * (gv * sg)).astype(BF16)

    spec = pl.BlockSpec((None, tt, f4), lambda i, s: (s, i, 0))
    shp = jax.ShapeDtypeStruct((n_s, t_dim, f4), BF16)
    return pl.pallas_call(
        body, name="ffn_down_dx_swiglu", grid=(t_dim // tt, n_s),
        in_specs=[pl.BlockSpec((tt, d), lambda i, s: (i, 0)), pl.BlockSpec((None, f4, d), lambda i, s: (s, 0, 0)),
                  spec, spec],
        out_specs=[spec, spec], out_shape=[shp, shp],
        compiler_params=_params(("parallel", "parallel")),
    )(dh, wd3, g3, u3)


def _mem_fwd(qm, kvm, *, tt=2048):
    b_dim, s_dim, _ = qm.shape
    n_mem = kvm.shape[1]
    scale = MEM_HEAD_DIM ** -0.5

    def body(q_ref, k_ref, v_ref, o_ref):
        sc = lax.dot_general(q_ref[0], k_ref[0], NT, preferred_element_type=F32) * scale
        p = jnp.exp(sc - jnp.max(sc, axis=-1, keepdims=True))
        p = p / jnp.sum(p, axis=-1, keepdims=True)
        o_ref[0] = jnp.dot(p.astype(BF16), v_ref[0], preferred_element_type=F32).astype(BF16)

    qs = pl.BlockSpec((1, tt, MEM_HEAD_DIM), lambda b, h, i: (b, i, h))
    return pl.pallas_call(
        body, name="mem_fwd", grid=(b_dim, N_HEADS_MEM, s_dim // tt),
        in_specs=[qs, pl.BlockSpec((1, n_mem, MEM_HEAD_DIM), lambda b, h, i: (b, 0, h)),
                  pl.BlockSpec((1, n_mem, MEM_HEAD_DIM), lambda b, h, i: (b, 0, N_HEADS_MEM + h))],
        out_specs=qs, out_shape=jax.ShapeDtypeStruct(qm.shape, BF16),
        compiler_params=_params(("parallel", "parallel", "parallel")),
    )(qm, kvm, kvm)


def _mem_bwd(qm, kvm, dom, *, tt=2048):
    b_dim, s_dim, _ = qm.shape
    n_mem = kvm.shape[1]
    scale = MEM_HEAD_DIM ** -0.5

    def body(q_ref, k_ref, v_ref, do_ref, dq_ref, dk_ref, dv_ref):
        qv, kv, vv, dov = q_ref[0], k_ref[0], v_ref[0], do_ref[0]
        sc = lax.dot_general(qv, kv, NT, preferred_element_type=F32) * scale
        p = jnp.exp(sc - jnp.max(sc, axis=-1, keepdims=True))
        p = p / jnp.sum(p, axis=-1, keepdims=True)
        dp = lax.dot_general(dov, vv, NT, preferred_element_type=F32)
        ds = (p * (dp - jnp.sum(p * dp, axis=-1, keepdims=True)) * scale).astype(BF16)
        dq_ref[0] = jnp.dot(ds, kv, preferred_element_type=F32).astype(BF16)

        @pl.when(pl.program_id(2) == 0)
        def _():
            dk_ref[...] = jnp.zeros_like(dk_ref)
            dv_ref[...] = jnp.zeros_like(dv_ref)

        dk_ref[0] += lax.dot_general(ds, qv, TN, preferred_element_type=F32)
        dv_ref[0] += lax.dot_general(p.astype(BF16), dov, TN, preferred_element_type=F32)

    qs = pl.BlockSpec((1, tt, MEM_HEAD_DIM), lambda b, h, i: (b, i, h))
    ks = pl.BlockSpec((1, n_mem, MEM_HEAD_DIM), lambda b, h, i: (b, 0, h))
    vs = pl.BlockSpec((1, n_mem, MEM_HEAD_DIM), lambda b, h, i: (b, 0, N_HEADS_MEM + h))
    return pl.pallas_call(
        body, name="mem_bwd", grid=(b_dim, N_HEADS_MEM, s_dim // tt),
        in_specs=[qs, ks, vs, qs], out_specs=[qs, ks, ks],
        out_shape=[jax.ShapeDtypeStruct(qm.shape, BF16), jax.ShapeDtypeStruct((b_dim, n_mem, MEM_WIDTH), F32),
                   jax.ShapeDtypeStruct((b_dim, n_mem, MEM_WIDTH), F32)],
        compiler_params=_params(("parallel", "parallel", "arbitrary")),
    )(qm, kvm, kvm, dom)


def _adamw_math(wv, gv, mv, vv):
    nm = ADAM_B1 * mv + (1.0 - ADAM_B1) * gv
    nv = ADAM_B2 * vv + (1.0 - ADAM_B2) * (gv * gv)
    m_hat = nm / (1.0 - ADAM_B1 ** ADAM_STEP)
    v_hat = nv / (1.0 - ADAM_B2 ** ADAM_STEP)
    return -ADAM_LR * (m_hat / (jnp.sqrt(v_hat) + ADAM_EPS) + ADAM_WD * wv), nm, nv


def _adamw(w, g, m, v, *, name):
    rows, cols = w.shape
    tr = _tile(rows, 256, 8)

    def body(w_ref, g_ref, m_ref, v_ref, d_ref, nm_ref, nv_ref):
        d_ref[...], nm_ref[...], nv_ref[...] = _adamw_math(w_ref[...], g_ref[...], m_ref[...], v_ref[...])

    spec = pl.BlockSpec((tr, cols), lambda i: (i, 0))
    shp = jax.ShapeDtypeStruct((rows, cols), F32)
    return pl.pallas_call(
        body, name=name, grid=(rows // tr,),
        in_specs=[spec] * 4, out_specs=[spec] * 3, out_shape=[shp] * 3,
        compiler_params=_params(("parallel",)),
    )(w, g, m, v)


def _prefetch_spec(grid, in_specs, out_specs):
    return pltpu.PrefetchScalarGridSpec(num_scalar_prefetch=1, grid=grid, in_specs=in_specs, out_specs=out_specs)


def _adamw_halves(w, mine, theirs, m, v, c_idx, *, name):
    rows, cols = w.shape
    half = rows // 2
    tr = _tile(half, _row_cap(cols), 8)
    nh = half // tr

    def body(c_ref, w_ref, mine_ref, theirs_ref, m_ref, v_ref, g_ref, d_ref, nm_ref, nv_ref):
        gv = jnp.where(pl.program_id(0) == c_ref[0], mine_ref[...], theirs_ref[...])
        g_ref[...] = gv
        d_ref[...], nm_ref[...], nv_ref[...] = _adamw_math(w_ref[...], gv, m_ref[...], v_ref[...])

    full = pl.BlockSpec((tr, cols), lambda h, i, c_ref: (h * nh + i, 0))
    part = pl.BlockSpec((tr, cols), lambda h, i, c_ref: (i, 0))
    shp = jax.ShapeDtypeStruct((rows, cols), F32)
    return pl.pallas_call(
        body, name=name, grid_spec=_prefetch_spec((2, nh), [full, part, part, full, full], [full] * 4),
        out_shape=[shp] * 4,
        compiler_params=_params(("parallel", "parallel")),
    )(c_idx, w, mine, theirs, m, v)


def _pair_sum(g3, theirs, c_idx, *, name):
    n, rows, cols = g3.shape
    half = rows // 2
    tr = _tile(half, _row_cap(cols), 16)

    def body(c_ref, g_ref, t_ref, o_ref):
        o_ref[...] = (g_ref[...] + t_ref[...]).astype(BF16)

    part = pl.BlockSpec((None, tr, cols), lambda s, i, c_ref: (s, i, 0))
    return pl.pallas_call(
        body, name=name,
        grid_spec=_prefetch_spec((n, half // tr),
                                 [pl.BlockSpec((None, None, tr, cols), lambda s, i, c_ref: (s, c_ref[0], i, 0)), part],
                                 part),
        out_shape=jax.ShapeDtypeStruct((n, half, cols), BF16),
        compiler_params=_params(("parallel", "parallel")),
    )(c_idx, g3.reshape(n, 2, half, cols), theirs)


def _chip_sum(pair, recv, s_idx, *, name):
    _, half, cols = pair.shape
    tr = _tile(half, _row_cap(cols), 16)

    def body(s_ref, p_ref, r_ref, o_ref):
        o_ref[...] = ((p_ref[...].astype(F32) + r_ref[0].astype(F32)) + r_ref[1].astype(F32)) + r_ref[2].astype(F32)

    return pl.pallas_call(
        body, name=name,
        grid_spec=_prefetch_spec((half // tr,),
                                 [pl.BlockSpec((None, tr, cols), lambda i, s_ref: (s_ref[0], i, 0)),
                                  pl.BlockSpec((N_CHIPS - 1, tr, cols), lambda i, s_ref: (0, i, 0))],
                                 pl.BlockSpec((tr, cols), lambda i, s_ref: (i, 0))),
        out_shape=jax.ShapeDtypeStruct((half, cols), F32),
        compiler_params=_params(("parallel",)),
    )(s_idx, pair, recv)


def _sum8(parts):
    n, rows, cols = parts.shape

    def body(p_ref, o_ref):
        acc = p_ref[0]
        for i in range(1, n):
            acc = acc + p_ref[i]
        o_ref[...] = acc

    return pl.pallas_call(
        body, name="small_sum", grid=(1,),
        in_specs=[pl.BlockSpec((n, rows, cols), lambda i: (0, 0, 0))],
        out_specs=pl.BlockSpec((rows, cols), lambda i: (0, 0)),
        out_shape=jax.ShapeDtypeStruct((rows, cols), parts.dtype),
        compiler_params=_params(("arbitrary",)),
    )(parts)


def _place():
    return lax.axis_index("x"), lax.axis_index("y"), lax.axis_index("c")


ANY = pl.BlockSpec(memory_space=pl.ANY)


def _rider_parts(rider):
    if rider is None:
        return (), [], []
    kind, arrays = rider
    n = len(arrays)
    shapes = {"gather": _gathered_shapes, "pair": _pair_shapes, "chip": _chip_shapes}[kind](arrays)
    sems = _gather_sems(n) if kind == "gather" else _exchange_sems(n if kind == "pair" else 3 * n)
    return tuple(arrays), shapes, sems


def _rider_hooks(rider, ins, outs, sems, step, n_steps):
    if rider is None:
        return (lambda: None), (lambda: None)
    if rider[0] == "gather":
        start, forward, finish = _gather_steps(ins, outs, *sems)
    else:
        start, finish = {"pair": _pair_steps, "chip": _chip_steps}[rider[0]](ins, outs, *sems)
        forward = None

    def begin():
        pl.when(step == 0)(start)

    def end():
        if forward is not None:
            pl.when(step == n_steps - 2)(forward)
        pl.when(step == n_steps - 1)(finish)

    return begin, end


def _exchange_alone(rider, *, name):
    extra, shapes, sems = _rider_parts(rider)
    n = len(extra)

    def body(*refs):
        begin, end = _rider_hooks(rider, refs[:n], refs[n:2 * n], refs[-2:], jnp.int32(0), 1)
        begin()
        end()

    return pl.pallas_call(
        body, name=name, out_shape=shapes, in_specs=[ANY] * n, out_specs=[ANY] * n, scratch_shapes=sems,
    )(*extra)


def _gathered_shapes(shards):
    return [jax.ShapeDtypeStruct((N_CHIPS,) + s.shape, s.dtype) for s in shards]


def _gather_sems(n):
    return [pltpu.SemaphoreType.DMA((7 * n,)), pltpu.SemaphoreType.DMA((7 * n,))]


def _gather_steps(ins, outs, send_sems, recv_sems):
    n = len(ins)
    halves = [r.shape[0] // 2 for r in ins]
    x, y, c = _place()
    my_chip = 2 * x + y
    me, sibling = (x, y, c), (x, y, 1 - c)
    chips = [(1 - x, y), (x, 1 - y), (1 - x, 1 - y)]

    def half_of(w, chip, pc):
        return outs[w].at[chip, pl.ds(pc * halves[w], halves[w]), :]

    def copy(w, k, src, dst, to):
        return pltpu.make_async_remote_copy(
            src_ref=src, dst_ref=dst, send_sem=send_sems.at[7 * w + k], recv_sem=recv_sems.at[7 * w + k],
            device_id=to, device_id_type=MESH)

    def firsts():
        cps = []
        for w in range(n):
            cps.append(copy(w, 0, ins[w], outs[w].at[my_chip], sibling))
            mine = ins[w].at[pl.ds(c * halves[w], halves[w]), :]
            for j, (px, py) in enumerate(chips):
                cps.append(copy(w, 1 + j, mine, half_of(w, my_chip, c), (px, py, c)))
        return cps

    def passes():
        return [copy(w, 4 + j, half_of(w, 2 * px + py, c), half_of(w, 2 * px + py, c), sibling)
                for w in range(n) for j, (px, py) in enumerate(chips)]

    def start():
        for cp in firsts():
            cp.start()

    def forward():
        fws = passes()
        for w in range(n):
            for j, (px, py) in enumerate(chips):
                landed = half_of(w, 2 * px + py, c)
                copy(w, 1 + j, landed, landed, me).wait_recv()
                fws[3 * w + j].start()

    def finish():
        for w in range(n):
            copy(w, 0, ins[w], outs[w].at[my_chip], me).wait_recv()
            for j, (px, py) in enumerate(chips):
                landed = half_of(w, 2 * px + py, 1 - c)
                copy(w, 4 + j, landed, landed, me).wait_recv()
        for cp in firsts() + passes():
            cp.wait_send()

    return start, forward, finish


def _pair_shapes(grads):
    return [jax.ShapeDtypeStruct((g.shape[0], g.shape[1] // 2, g.shape[2]), g.dtype) for g in grads]


def _exchange_sems(n):
    return [pltpu.SemaphoreType.DMA((n,)), pltpu.SemaphoreType.DMA((n,))]


def _exchange_steps(copies):
    def start():
        for cp in copies():
            cp.start()

    def finish():
        for cp in copies():
            cp.wait()

    return start, finish


def _pair_steps(ins, outs, send_sems, recv_sems):
    x, y, c = _place()

    def copies():
        return [pltpu.make_async_remote_copy(
            src_ref=ins[w].at[:, pl.ds((1 - c) * (ins[w].shape[1] // 2), ins[w].shape[1] // 2), :], dst_ref=outs[w],
            send_sem=send_sems.at[w], recv_sem=recv_sems.at[w], device_id=(x, y, 1 - c), device_id_type=MESH)
            for w in range(len(ins))]

    return _exchange_steps(copies)


def _chip_shapes(pairs):
    return [jax.ShapeDtypeStruct((N_CHIPS - 1,) + p.shape[1:], p.dtype) for p in pairs]


def _chip_steps(ins, outs, send_sems, recv_sems):
    x, y, c = _place()
    others = [(1 - x, y), (x, 1 - y), (1 - x, 1 - y)]

    def copies():
        return [pltpu.make_async_remote_copy(
            src_ref=ins[w].at[2 * px + py], dst_ref=outs[w].at[j],
            send_sem=send_sems.at[3 * w + j], recv_sem=recv_sems.at[3 * w + j],
            device_id=(px, py, c), device_id_type=MESH)
            for w in range(len(ins)) for j, (px, py) in enumerate(others)]

    return _exchange_steps(copies)


def _swap_halves(mine):
    n = len(mine)

    def body(*refs):
        ins, outs, send_sems, recv_sems = refs[:n], refs[n:2 * n], refs[2 * n], refs[2 * n + 1]
        x, y, c = _place()
        copies = [pltpu.make_async_remote_copy(
            src_ref=ins[w], dst_ref=outs[w], send_sem=send_sems.at[w], recv_sem=recv_sems.at[w],
            device_id=(x, y, 1 - c), device_id_type=MESH) for w in range(n)]
        for cp in copies:
            cp.start()
        for cp in copies:
            cp.wait()

    return pl.pallas_call(
        body, name="grad_swap_halves",
        out_shape=[jax.ShapeDtypeStruct(h.shape, h.dtype) for h in mine],
        in_specs=[ANY] * n, out_specs=[ANY] * n,
        scratch_shapes=[pltpu.SemaphoreType.DMA((n,)), pltpu.SemaphoreType.DMA((n,))],
    )(*mine)


def _gather_small(small):
    srows, cols = small.shape

    def body(s_ref, all_ref, send_sems, recv_sems, local_sem):
        x, y, c = _place()
        me = 4 * x + 2 * y + c
        keep_small = pltpu.make_async_copy(s_ref, all_ref.at[me], local_sem)
        keep_small.start()
        sends = []
        for kk in range(1, 8):
            peer = (x ^ (kk >> 2), y ^ ((kk >> 1) & 1), c ^ (kk & 1))
            sends.append(pltpu.make_async_remote_copy(
                src_ref=s_ref, dst_ref=all_ref.at[me],
                send_sem=send_sems.at[kk], recv_sem=recv_sems.at[kk], device_id=peer, device_id_type=MESH))
        for cp in sends:
            cp.start()
        for kk in range(1, 8):
            px, py, pc = x ^ (kk >> 2), y ^ ((kk >> 1) & 1), c ^ (kk & 1)
            pltpu.make_async_remote_copy(
                src_ref=s_ref, dst_ref=all_ref.at[4 * px + 2 * py + pc],
                send_sem=send_sems.at[kk], recv_sem=recv_sems.at[kk], device_id=(px, py, pc),
                device_id_type=MESH).wait_recv()
        for cp in sends:
            cp.wait_send()
        keep_small.wait()

    return pl.pallas_call(
        body, name="gather_small",
        out_shape=jax.ShapeDtypeStruct((8, srows, cols), small.dtype),
        in_specs=[ANY], out_specs=ANY,
        scratch_shapes=[pltpu.SemaphoreType.DMA((8,)), pltpu.SemaphoreType.DMA((8,)), pltpu.SemaphoreType.DMA],
    )(small)


SHARDED = (("w_in", D_MODEL, IN_COLS, 1), ("w_up_a", ATT_WIDTH, D_MODEL, 1), ("w_up_b", ATT_WIDTH, D_MODEL, 1),
           ("w_out", D_MODEL, D_MODEL, 0), ("w_q_mem", D_MODEL, MEM_WIDTH, 0), ("w_kv_mem", D_MODEL, 2 * MEM_WIDTH, 0),
           ("w_o_mem", MEM_WIDTH, D_MODEL, 1), ("w_ffn_gate", D_FF, D_MODEL, 0), ("w_ffn_up", D_FF, D_MODEL, 0),
           ("w_ffn_down", D_FF, D_MODEL, 0))
TRANSPOSED = ("w_ffn_gate", "w_ffn_up")
NAMES = tuple(n for n, _, _, _ in SHARDED)


def _held(name, shard):
    return shard.T if name in TRANSPOSED else shard
EARLY, LATE = NAMES[:1], NAMES[1:]
GAINS = ("g_mix", "g_mem_q", "g_mem_kv", "g_ffn", "g_final")


def _natural(w3):
    n, r, c = w3.shape
    return w3.reshape(n * r, c)


def _shard_major(g, axis):
    if axis == 1:
        return g
    r, c = g.shape
    return g.reshape(N_CHIPS, r // N_CHIPS, c)


def kernel(x, mem, positions, g_mix, w_in, w_up_a, w_up_b, w_out, g_mem_q, g_mem_kv, w_q_mem, w_kv_mem, w_o_mem, g_ffn, w_ffn_gate, w_ffn_up, w_ffn_down, g_final, loss_target, m_g_mix, m_w_in, m_w_up_a, m_w_up_b, m_w_out, m_g_mem_q, m_g_mem_kv, m_w_q_mem, m_w_kv_mem, m_w_o_mem, m_g_ffn, m_w_ffn_gate, m_w_ffn_up, m_w_ffn_down, m_g_final, v_g_mix, v_w_in, v_w_up_a, v_w_up_b, v_w_out, v_g_mem_q, v_g_mem_kv, v_w_q_mem, v_w_kv_mem, v_w_o_mem, v_g_ffn, v_w_ffn_gate, v_w_ffn_up, v_w_ffn_down, v_g_final):
    given = dict(locals())
    shards = {n: _held(n, given[n][0]) for n in NAMES}

    early_shards = [shards[n].astype(BF16) for n in EARLY]
    late_shards = [shards[n].astype(BF16) for n in LATE]
    c_idx = lax.axis_index("c").astype(jnp.int32).reshape(1)
    s_idx = (2 * lax.axis_index("x") + lax.axis_index("y")).astype(jnp.int32).reshape(1)

    loss_row, grad_x, mine, gain_grads = _local_step(x, mem, positions, loss_target, g_mix, g_mem_q, g_mem_kv,
                                                     g_ffn, g_final, {}, early_shards, late_shards, (c_idx, s_idx))
    return _reduce_and_update(given, shards, loss_row, grad_x, mine, gain_grads, c_idx)


def _reduce_halves(glist, names, c_idx, s_idx, pair_exchange, chip_exchange):
    theirs = pair_exchange(glist)
    pairs = [_pair_sum(g, t, c_idx, name="pair_sum_" + n) for n, g, t in zip(names, glist, theirs)]
    recv = chip_exchange(pairs)
    return [_chip_sum(p, r, s_idx, name="chip_sum_" + n) for n, p, r in zip(names, pairs, recv)]


def _local_step(x, mem, positions, loss_target, g_mix, g_mem_q, g_mem_kv, g_ffn, g_final, wf,
                early_shards=None, late_shards=None, place=None):
    b_dim, s_dim, d = x.shape
    t_dim = b_dim * s_dim
    n_mem = mem.shape[1]
    wf = dict(wf)

    xb = x.reshape(t_dim, d)
    tgt = loss_target.reshape(t_dim, d)
    memf = mem.reshape(b_dim * n_mem, d)
    gfin = g_final.reshape(1, d)
    pos = positions.reshape(t_dim, 1).astype(F32)

    lane = jnp.arange(LANES) % HEAD_DIM
    half = ROPE_DIM // 2
    inv_freq = ROPE_THETA ** (-jnp.arange(half, dtype=F32) / half)
    inv_lane = jnp.where(lane < ROPE_DIM, inv_freq[lane % half], 0.0).reshape(1, -1).astype(F32)
    sel_a = (lane < half).astype(F32).reshape(1, -1)
    sel_b = ((lane >= half) & (lane < ROPE_DIM)).astype(F32).reshape(1, -1)

    def rows3(t):
        return t.reshape(b_dim, s_dim, t.shape[-1])

    def rows2(t):
        return t.reshape(t_dim, t.shape[-1])

    if early_shards:
        n1, gathered = _rms_fwd(xb, g_mix, name="rms_mix", rider=("gather", early_shards))
        wf.update(zip(EARLY, gathered))
    else:
        n1 = _rms_fwd(xb, g_mix, name="rms_mix")
    proj = _mm_cs(n1, wf["w_in"], name="mm_in")
    proj3 = rows3(proj)
    cs, sn = _rope_table(pos, inv_lane, sel_a, sel_b)
    cs3, sn3 = rows3(cs), rows3(sn)
    (oa16, oa32, lse_a), _ = _dil_fwd(proj3, cs3, sn3, sel_a, sel_b)
    ob16, gathered = _sb_fwd(proj3, ("gather", late_shards) if late_shards else None)
    wf.update(zip(LATE, gathered))
    w_out, w_q, w_kv = _natural(wf["w_out"]), _natural(wf["w_q_mem"]), _natural(wf["w_kv_mem"])
    oa, ob = rows2(oa16), rows2(ob16)
    ua = _mm_sm(oa, wf["w_up_a"], name="mm_up_a")
    ub = _mm_sm(ob, wf["w_up_b"], name="mm_up_b")
    mixed, h1, hn = _gate_out_norm(proj, ua, ub, w_out, xb, g_mem_q)

    memn = _rms_fwd(memf, g_mem_kv, name="rms_mem_kv")
    qm = _mm(hn, w_q, name="mm_q_mem", out_dtype=BF16)
    kvm = _mm(memn, w_kv, name="mm_kv_mem", out_dtype=BF16)
    qm3, kvm3 = rows3(qm), kvm.reshape(b_dim, n_mem, 2 * MEM_WIDTH)
    om = rows2(_mem_fwd(qm3, kvm3))
    h2 = _mm_sm(om, wf["w_o_mem"], name="mm_o_mem", add=h1)

    n3 = _rms_fwd(h2, g_ffn, name="rms_ffn")
    gate3, up3, act3 = _ffn_up_swiglu(n3, wf["w_ffn_gate"], wf["w_ffn_up"])
    loss_row, dh3, dg_final = _down_final(act3, wf["w_ffn_down"], h2, gfin, tgt)

    grads = {}
    grads["w_ffn_down"] = _mm_ffn_down_dw(act3, dh3, name="mm_down_dw")
    dgate3, dup3 = _ffn_down_dx_swiglu(dh3, wf["w_ffn_down"], gate3, up3)
    grads["w_ffn_gate"] = _mm_ffn_down_dw(dgate3, n3, name="mm_gate_dw")
    grads["w_ffn_up"] = _mm_ffn_down_dw(dup3, n3, name="mm_up_dw")
    dn3 = _ffn_up_dx(dgate3, dup3, wf["w_ffn_gate"], wf["w_ffn_up"])
    dh2, dg_ffn = _rms_bwd(h2, g_ffn, dn3, dh3, name="rms_ffn_bwd")

    dom = _mm_sm_dx(dh2, wf["w_o_mem"], name="mm_o_mem_dx", out_dtype=BF16)
    grads["w_o_mem"] = _mm_sm_dw(om, dh2, name="mm_o_mem_dw")
    dqm, dkm, dvm = _mem_bwd(qm3, kvm3, rows3(dom))
    dqm = rows2(dqm)
    dkvm = jnp.concatenate([dkm, dvm], axis=-1).reshape(b_dim * n_mem, 2 * MEM_WIDTH).astype(BF16)
    grads["w_q_mem"] = _shard_major(_mm(hn, dqm, name="mm_q_mem_dw", ta=True), 0)
    dhn = _mm(dqm, w_q, name="mm_q_mem_dx", tb=True)
    grads["w_kv_mem"] = _shard_major(_mm(memn, dkvm, name="mm_kv_mem_dw", ta=True), 0)
    dmemn = _mm(dkvm, w_kv, name="mm_kv_mem_dx", tb=True)
    _, dg_mem_kv = _rms_bwd(memf, g_mem_kv, dmemn, None, name="rms_mem_kv_bwd")
    dh1, dg_mem_q = _rms_bwd(h1, g_mem_q, dhn, dh2, name="rms_mem_q_bwd")

    grads["w_out"] = _shard_major(_mm(mixed, dh1, name="mm_out_dw", ta=True), 0)
    dua, dub, dgates = _out_dx_gate_bwd(dh1, w_out, proj, ua, ub)
    doa = _mm_sm_dx(dua, wf["w_up_a"], name="mm_up_a_dx")
    grads["w_up_a"] = _mm_sm_dw(oa, dua, name="mm_up_a_dw")
    dob = _mm_sm_dx(dub, wf["w_up_b"], name="mm_up_b_dx", out_dtype=BF16)
    grads["w_up_b"] = _mm_sm_dw(ob, dub, name="mm_up_b_dw")

    att = {}

    def dil_with_pairs(glist):
        att["a"], theirs = _dil_bwd(proj3, cs3, sn3, sel_a, sel_b, rows3(doa), oa32, lse_a,
                                    ("pair", glist) if glist else None)
        return theirs

    def sb_with_chips(pairs):
        att["b"], recv = _sb_bwd(proj3, rows3(dob), ("chip", pairs) if pairs else None)
        return recv

    if place is None:
        dil_with_pairs(())
        sb_with_chips(())
    else:
        mine_late = _reduce_halves([grads[n] for n in LATE], LATE, *place, dil_with_pairs, sb_with_chips)
    dproj = jnp.concatenate([rows2(t) for t in att["a"] + att["b"]] + [dgates], axis=1)
    grads["w_in"] = _mm_cs_dw(n1, dproj, name="mm_in_dw")
    if place is None:
        dn1 = _mm_cs_dx(dproj, wf["w_in"], name="mm_in_dx")
        dx, dg_mix = _rms_bwd(xb, g_mix, dn1, dh1, name="rms_mix_bwd")
    else:
        tail = {}

        def pair_alone(glist):
            return _exchange_alone(("pair", glist), name="grad_pair_exchange")

        def dx_with_chips(pairs):
            tail["dn1"], recv = _mm_cs_dx(dproj, wf["w_in"], name="mm_in_dx", rider=("chip", pairs))
            return recv

        mine_early = _reduce_halves([grads[n] for n in EARLY], EARLY, *place, pair_alone, dx_with_chips)
        dx, dg_mix = _rms_bwd(xb, g_mix, tail["dn1"], dh1, name="rms_mix_bwd")
    grad_x = dx.reshape(b_dim, s_dim, d)
    gains = (dg_mix, dg_mem_q, dg_mem_kv, dg_ffn, dg_final)
    if place is None:
        return loss_row, grad_x, grads, gains
    return loss_row, grad_x, mine_early + mine_late, gains


def _reduce_and_update(given, shards, loss_row, grad_x, mine, gain_grads, c_idx):
    d = D_MODEL
    dg_mix, dg_mem_q, dg_mem_kv, dg_ffn, dg_final = gain_grads
    small = jnp.concatenate([dg_mix, dg_mem_q, dg_mem_kv, dg_ffn, dg_final,
                             jnp.pad(loss_row, ((0, 0), (0, FLAT_COLS - LANES))), jnp.zeros((2, FLAT_COLS), F32)], axis=0)
    small_all = _gather_small(small)
    others = _swap_halves(mine)
    small_sum = _sum8(small_all)
    loss = small_sum[5, 0]

    out_g, out_d, out_m, out_v = {}, {}, {}, {}
    for n, mine_n, other_n in zip(NAMES, mine, others):
        res = _adamw_halves(shards[n], mine_n, other_n, _held(n, given["m_" + n][0]), _held(n, given["v_" + n][0]),
                            c_idx, name="adamw_" + n)
        out_g[n], out_d[n], out_m[n], out_v[n] = [_held(n, r)[None] for r in res]
    gain_w = jnp.concatenate([given[n].reshape(1, d) for n in GAINS], axis=0)
    gain_m = jnp.concatenate([given["m_" + n].reshape(1, d) for n in GAINS], axis=0)
    gain_v = jnp.concatenate([given["v_" + n].reshape(1, d) for n in GAINS], axis=0)
    gain_g = small_sum[:len(GAINS)]
    gd, gm, gv = _adamw(gain_w, gain_g, gain_m, gain_v, name="adamw_gains")
    for i, n in enumerate(GAINS):
        shape = given[n].shape
        out_g[n], out_d[n] = gain_g[i].reshape(shape), gd[i].reshape(shape)
        out_m[n], out_v[n] = gm[i].reshape(shape), gv[i].reshape(shape)

    order = ["g_mix", "w_in", "w_up_a", "w_up_b", "w_out", "g_mem_q", "g_mem_kv", "w_q_mem", "w_kv_mem", "w_o_mem",
             "g_ffn", "w_ffn_gate", "w_ffn_up", "w_ffn_down", "g_final"]
    return (loss, grad_x, *[out_g[n] for n in order], *[out_d[n] for n in order],
            *[out_m[n] for n in order], *[out_v[n] for n in order])
```

```python
import jax
import jax.numpy as jnp
from jax import lax
from jax.experimental import pallas as pl
from jax.experimental.pallas import tpu as pltpu

F32 = jnp.float32
BF16 = jnp.bfloat16
MESH = pl.DeviceIdType.MESH

D_MODEL = 1024
HEAD_DIM = 64
N_HEADS = 8
ATT_WIDTH = N_HEADS * HEAD_DIM
DIL_PATTERNS = ((128, 1), (512, 4), (2048, 16))
BLOCK = 128
SB_ROWS = 1024
SB_STEP_FWD, SB_STEP_BWD = 8, 4
ROPE_THETA = 500000.0
ROPE_DIM = HEAD_DIM // 4
N_HEADS_MEM = 4
MEM_HEAD_DIM = 128
MEM_WIDTH = N_HEADS_MEM * MEM_HEAD_DIM
D_FF = 2816
IN_COLS = 6 * ATT_WIDTH + 2 * D_MODEL
RMS_EPS = 1e-6
ADAM_LR = 0.001
ADAM_B1 = 0.9
ADAM_B2 = 0.999
ADAM_EPS = 1e-08
ADAM_WD = 0.01
ADAM_STEP = 10

N_CHIPS = 4
LANES = 128
FLAT_COLS = 1024
VMEM_LIMIT = 56 * 1024 * 1024

PAIRS = ATT_WIDTH // LANES
COL_QA, COL_KA, COL_VA, COL_QB, COL_KB, COL_VB = (i * PAIRS for i in range(6))

MM_CAP = 1408
TOK_CAP = 2048
NN = (((1,), (0,)), ((), ()))
NT = (((1,), (1,)), ((), ()))
TN = (((0,), (0,)), ((), ()))
BNN = (((2,), (1,)), ((0,), (0,)))
BNT = (((2,), (2,)), ((0,), (0,)))
BTN = (((1,), (1,)), ((0,), (0,)))
DIL_BATCH = 16
DIL_CHUNK = 512


def _tile(dim, cap, unit=LANES):
    if dim <= cap:
        return dim
    best = None
    for t in range(unit, cap + 1, unit):
        if dim % t == 0:
            best = t
    assert best is not None, (dim, cap)
    return best


def _row_cap(cols):
    return max(256, (1 << 18) // cols)


def _params(sem):
    return pltpu.CompilerParams(dimension_semantics=sem, vmem_limit_bytes=VMEM_LIMIT)


def _mm(a, b, *, name, ta=False, tb=False, add=None, out_dtype=F32,
        tm_cap=MM_CAP, tn_cap=MM_CAP, tk_cap=MM_CAP):
    if ta:
        k_dim, m_dim = a.shape
    else:
        m_dim, k_dim = a.shape
    if tb:
        n_dim, kb = b.shape
    else:
        kb, n_dim = b.shape
    assert kb == k_dim, (a.shape, b.shape, ta, tb)
    tm, tn, tk = _tile(m_dim, tm_cap), _tile(n_dim, tn_cap), _tile(k_dim, tk_cap)
    nk = k_dim // tk
    dims = (((0 if ta else 1,), (1 if tb else 0,)), ((), ()))
    has_add = add is not None

    def body(*refs):
        if has_add:
            a_ref, b_ref, add_ref, o_ref = refs[:4]
        else:
            a_ref, b_ref, o_ref = refs[:3]
        part = lax.dot_general(a_ref[...].astype(BF16), b_ref[...].astype(BF16), dims, preferred_element_type=F32)

        def finish(r):
            if has_add:
                r = add_ref[...] + r
            o_ref[...] = r.astype(out_dtype)

        if nk == 1:
            finish(part)
            return
        acc_ref = refs[-1]
        k = pl.program_id(2)

        @pl.when(k == 0)
        def _():
            acc_ref[...] = part

        @pl.when(k > 0)
        def _():
            acc_ref[...] += part

        @pl.when(k == nk - 1)
        def _():
            finish(acc_ref[...])

    a_spec = pl.BlockSpec((tk, tm), lambda i, j, k: (k, i)) if ta else pl.BlockSpec((tm, tk), lambda i, j, k: (i, k))
    b_spec = pl.BlockSpec((tn, tk), lambda i, j, k: (j, k)) if tb else pl.BlockSpec((tk, tn), lambda i, j, k: (k, j))
    o_spec = pl.BlockSpec((tm, tn), lambda i, j, k: (i, j))
    in_specs = [a_spec, b_spec] + ([o_spec] if has_add else [])
    args = (a, b) + ((add,) if has_add else ())
    return pl.pallas_call(
        body, name=name, grid=(m_dim // tm, n_dim // tn, nk),
        in_specs=in_specs, out_specs=o_spec,
        out_shape=jax.ShapeDtypeStruct((m_dim, n_dim), out_dtype),
        scratch_shapes=[pltpu.VMEM((tm, tn), F32)] if nk > 1 else [],
        compiler_params=_params(("parallel", "parallel", "arbitrary")),
    )(*args)


def _mm_core(name, a, b, a_spec, b_spec, o_spec, out_shape, grid, dims, *, add=None, out_dtype=F32, rider=None):
    nk = grid[2]
    has_add = add is not None
    n_in = 3 if has_add else 2
    acc_shape = tuple(d for d in o_spec.block_shape if d is not None)
    extra, extra_shapes, extra_sems = _rider_parts(rider)
    n_w = len(extra)

    def body(*refs):
        a_ref, b_ref = refs[:2]
        o_ref = refs[n_in + n_w]
        step = (pl.program_id(0) * grid[1] + pl.program_id(1)) * nk + pl.program_id(2)
        begin, end = _rider_hooks(rider, refs[n_in:n_in + n_w], refs[n_in + n_w + 1:n_in + 2 * n_w + 1], refs[-2:],
                                  step, grid[0] * grid[1] * nk)
        begin()
        part = lax.dot_general(a_ref[...].astype(BF16), b_ref[...].astype(BF16), dims, preferred_element_type=F32)

        def finish(r):
            if has_add:
                r = refs[2][...] + r
            o_ref[...] = r.astype(out_dtype)

        if nk == 1:
            finish(part)
        else:
            acc_ref = refs[n_in + 2 * n_w + 1]
            k = pl.program_id(2)

            @pl.when(k == 0)
            def _():
                acc_ref[...] = part

            @pl.when(k > 0)
            def _():
                acc_ref[...] += part

            @pl.when(k == nk - 1)
            def _():
                finish(acc_ref[...])
        end()

    in_specs = [a_spec, b_spec] + ([o_spec] if has_add else []) + [ANY] * n_w
    args = (a, b) + ((add,) if has_add else ()) + extra
    res = pl.pallas_call(
        body, name=name, grid=grid, in_specs=in_specs, out_specs=[o_spec] + [ANY] * n_w,
        out_shape=[jax.ShapeDtypeStruct(out_shape, out_dtype)] + extra_shapes,
        scratch_shapes=([pltpu.VMEM(acc_shape, F32)] if nk > 1 else []) + extra_sems,
        compiler_params=_params(("arbitrary",) * 3 if n_w else ("parallel", "parallel", "arbitrary")),
    )(*args)
    return (res[0], res[1:]) if n_w else res[0]


def _mm_cs(a, w3, *, name):
    m_dim, k_dim = a.shape
    _, _, n4 = w3.shape
    tm, tn, tk = _tile(m_dim, TOK_CAP), _tile(n4, MM_CAP), _tile(k_dim, MM_CAP)
    npb = n4 // tn
    return _mm_core(name, a, w3,
                    pl.BlockSpec((tm, tk), lambda i, j, k: (i, k)),
                    pl.BlockSpec((None, tk, tn), lambda i, j, k: (j // npb, k, j % npb)),
                    pl.BlockSpec((tm, tn), lambda i, j, k: (i, j)),
                    (m_dim, N_CHIPS * n4), (m_dim // tm, N_CHIPS * npb, k_dim // tk), NN)


def _mm_cs_dx(dy, w3, *, name, out_dtype=F32, rider=None):
    m_dim, _ = dy.shape
    _, k_dim, n4 = w3.shape
    tm, tkw, tn = _tile(m_dim, TOK_CAP), _tile(k_dim, MM_CAP), _tile(n4, MM_CAP)
    npb = n4 // tn
    return _mm_core(name, dy, w3,
                    pl.BlockSpec((tm, tn), lambda i, j, k: (i, k)),
                    pl.BlockSpec((None, tkw, tn), lambda i, j, k: (k // npb, j, k % npb)),
                    pl.BlockSpec((tm, tkw), lambda i, j, k: (i, j)),
                    (m_dim, k_dim), (m_dim // tm, k_dim // tkw, N_CHIPS * npb), NT, out_dtype=out_dtype, rider=rider)


def _mm_cs_dw(a, dy, *, name):
    m_dim, k_dim = a.shape
    n4 = dy.shape[1] // N_CHIPS
    tmk, tn, tk = _tile(k_dim, MM_CAP), _tile(n4, MM_CAP), _tile(m_dim, TOK_CAP)
    npb = n4 // tn
    return _mm_core(name, a, dy,
                    pl.BlockSpec((tk, tmk), lambda i, j, k: (k, i)),
                    pl.BlockSpec((tk, tn), lambda i, j, k: (k, j)),
                    pl.BlockSpec((None, tmk, tn), lambda i, j, k: (j // npb, i, j % npb)),
                    (N_CHIPS, k_dim, n4), (k_dim // tmk, N_CHIPS * npb, m_dim // tk), TN)


def _mm_sm(a, w3, *, name, add=None, out_dtype=F32, tt=1024):
    t_dim, k_dim = a.shape
    n_s, _, n4 = w3.shape
    has_add = add is not None

    def body(*refs):
        a_ref, w_ref, o_ref = refs[0], refs[1], refs[-1]
        av = a_ref[...].astype(BF16)
        for s in range(n_s):
            cols = slice(s * n4, (s + 1) * n4)
            r = jnp.dot(av, w_ref[s], preferred_element_type=F32)
            if has_add:
                r = refs[2][:, cols] + r
            o_ref[:, cols] = r.astype(out_dtype)

    row = pl.BlockSpec((tt, n_s * n4), lambda i: (i, 0))
    return pl.pallas_call(
        body, name=name, grid=(t_dim // tt,),
        in_specs=[pl.BlockSpec((tt, k_dim), lambda i: (i, 0)), pl.BlockSpec(w3.shape, lambda i: (0, 0, 0))]
        + ([row] if has_add else []),
        out_specs=row, out_shape=jax.ShapeDtypeStruct((t_dim, n_s * n4), out_dtype),
        compiler_params=_params(("parallel",)),
    )(*((a, w3) + ((add,) if has_add else ())))


def _mm_sm_dx(dy, w3, *, name, out_dtype=F32, tt=1024):
    t_dim, _ = dy.shape
    n_s, k_dim, n4 = w3.shape

    def body(dy_ref, w_ref, o_ref):
        dyv = dy_ref[...].astype(BF16)
        acc = lax.dot_general(dyv[:, :n4], w_ref[0], NT, preferred_element_type=F32)
        for s in range(1, n_s):
            acc = acc + lax.dot_general(dyv[:, s * n4:(s + 1) * n4], w_ref[s], NT, preferred_element_type=F32)
        o_ref[...] = acc.astype(out_dtype)

    return pl.pallas_call(
        body, name=name, grid=(t_dim // tt,),
        in_specs=[pl.BlockSpec((tt, n_s * n4), lambda i: (i, 0)), pl.BlockSpec(w3.shape, lambda i: (0, 0, 0))],
        out_specs=pl.BlockSpec((tt, k_dim), lambda i: (i, 0)),
        out_shape=jax.ShapeDtypeStruct((t_dim, k_dim), out_dtype),
        compiler_params=_params(("parallel",)),
    )(dy, w3)


def _mm_sm_dw(a, dy, *, name, tk=1024):
    t_dim, k_dim = a.shape
    n4 = dy.shape[1] // N_CHIPS

    def body(a_ref, dy_ref, o_ref):
        av = a_ref[...].astype(BF16)
        dyv = dy_ref[...].astype(BF16)

        @pl.when(pl.program_id(0) == 0)
        def _():
            o_ref[...] = jnp.zeros_like(o_ref)

        for s in range(N_CHIPS):
            o_ref[s] += lax.dot_general(av, dyv[:, s * n4:(s + 1) * n4], TN, preferred_element_type=F32)

    return pl.pallas_call(
        body, name=name, grid=(t_dim // tk,),
        in_specs=[pl.BlockSpec((tk, k_dim), lambda i: (i, 0)), pl.BlockSpec((tk, N_CHIPS * n4), lambda i: (i, 0))],
        out_specs=pl.BlockSpec((N_CHIPS, k_dim, n4), lambda i: (0, 0, 0)),
        out_shape=jax.ShapeDtypeStruct((N_CHIPS, k_dim, n4), F32),
        compiler_params=_params(("arbitrary",)),
    )(a, dy)


def _ffn_up_dx(dg3, du3, wg3, wu3, *, tt=512):
    n_s, t_dim, f4 = dg3.shape
    d = wg3.shape[2]

    def body(dg_ref, du_ref, wg_ref, wu_ref, o_ref):
        acc = None
        for s in range(n_s):
            part = (jnp.dot(dg_ref[s], wg_ref[s], preferred_element_type=F32)
                    + jnp.dot(du_ref[s], wu_ref[s], preferred_element_type=F32))
            acc = part if acc is None else acc + part
        o_ref[...] = acc

    a_spec = pl.BlockSpec((n_s, tt, f4), lambda i: (0, i, 0))
    w_spec = pl.BlockSpec(wg3.shape, lambda i: (0, 0, 0))
    return pl.pallas_call(
        body, name="ffn_up_dx", grid=(t_dim // tt,),
        in_specs=[a_spec, a_spec, w_spec, w_spec], out_specs=pl.BlockSpec((tt, d), lambda i: (i, 0)),
        out_shape=jax.ShapeDtypeStruct((t_dim, d), F32),
        compiler_params=_params(("parallel",)),
    )(dg3, du3, wg3, wu3)


def _mm_ffn_down_dw(act3, dh, *, name, tk=1024):
    n_s, t_dim, f4 = act3.shape
    d = dh.shape[1]

    def body(a_ref, b_ref, o_ref):
        bv = b_ref[...].astype(BF16)

        @pl.when(pl.program_id(0) == 0)
        def _():
            o_ref[...] = jnp.zeros_like(o_ref)

        for s in range(n_s):
            o_ref[s] += lax.dot_general(a_ref[s], bv, TN, preferred_element_type=F32)

    return pl.pallas_call(
        body, name=name, grid=(t_dim // tk,),
        in_specs=[pl.BlockSpec((n_s, tk, f4), lambda i: (0, i, 0)), pl.BlockSpec((tk, d), lambda i: (i, 0))],
        out_specs=pl.BlockSpec((n_s, f4, d), lambda i: (0, 0, 0)),
        out_shape=jax.ShapeDtypeStruct((n_s, f4, d), F32),
        compiler_params=_params(("arbitrary",)),
    )(act3, dh)


def _rms_fwd(x, g, *, name, tt=512, rider=None):
    t_dim, d = x.shape
    tt = _tile(t_dim, tt, 8)
    extra, extra_shapes, extra_sems = _rider_parts(rider)
    n_w = len(extra)

    def body(*refs):
        x_ref, g_ref, o_ref = refs[0], refs[1], refs[2 + n_w]
        begin, end = _rider_hooks(rider, refs[2:2 + n_w], refs[3 + n_w:3 + 2 * n_w], refs[-2:], pl.program_id(0),
                                  t_dim // tt)
        begin()
        xv = x_ref[...]
        r = lax.rsqrt(jnp.mean(xv * xv, axis=-1, keepdims=True) + RMS_EPS)
        o_ref[...] = ((xv * r) * g_ref[...]).astype(o_ref.dtype)
        end()

    res = pl.pallas_call(
        body, name=name, grid=(t_dim // tt,),
        in_specs=[pl.BlockSpec((tt, d), lambda i: (i, 0)), pl.BlockSpec((1, d), lambda i: (0, 0))] + [ANY] * n_w,
        out_specs=[pl.BlockSpec((tt, d), lambda i: (i, 0))] + [ANY] * n_w,
        out_shape=[jax.ShapeDtypeStruct((t_dim, d), BF16)] + extra_shapes,
        scratch_shapes=extra_sems,
        compiler_params=_params(("arbitrary",) if n_w else ("parallel",)),
    )(x, g, *extra)
    return (res[0], res[1:]) if n_w else res[0]


def _rms_bwd(x, g, dy, add, *, name, tt=512):
    t_dim, d = x.shape
    tt = _tile(t_dim, tt, 8)
    has_add = add is not None

    def body(*refs):
        x_ref, g_ref, dy_ref = refs[:3]
        add_ref = refs[3] if has_add else None
        dx_ref, dg_ref = refs[-2:]
        xv = x_ref[...]
        dyv = dy_ref[...].astype(F32)
        r = lax.rsqrt(jnp.mean(xv * xv, axis=-1, keepdims=True) + RMS_EPS)
        xh = xv * r
        u = dyv * g_ref[...]
        dx = r * (u - xh * jnp.mean(u * xh, axis=-1, keepdims=True))
        if has_add:
            dx = add_ref[...] + dx
        dx_ref[...] = dx

        @pl.when(pl.program_id(0) == 0)
        def _():
            dg_ref[...] = jnp.zeros_like(dg_ref)

        dg_ref[...] += jnp.sum(dyv * xh, axis=0, keepdims=True)

    row = pl.BlockSpec((tt, d), lambda i: (i, 0))
    vec = pl.BlockSpec((1, d), lambda i: (0, 0))
    in_specs = [row, vec, row] + ([row] if has_add else [])
    args = (x, g, dy) + ((add,) if has_add else ())
    return pl.pallas_call(
        body, name=name, grid=(t_dim // tt,),
        in_specs=in_specs, out_specs=[row, vec],
        out_shape=[jax.ShapeDtypeStruct((t_dim, d), F32), jax.ShapeDtypeStruct((1, d), F32)],
        compiler_params=_params(("arbitrary",)),
    )(*args)


def _down_final(act3, wd3, h, g, target, *, tt=512):
    n_s, t_dim, f4 = act3.shape
    d = h.shape[1]
    n_steps = t_dim // tt

    def body(a_ref, w_ref, h_ref, g_ref, t_ref, loss_ref, dh_ref, dg_ref, sq_ref):
        i = pl.program_id(0)
        xv = h_ref[...]
        for s in range(n_s):
            xv = xv + jnp.dot(a_ref[s], w_ref[s], preferred_element_type=F32)
        gv = g_ref[...]
        r = lax.rsqrt(jnp.mean(xv * xv, axis=-1, keepdims=True) + RMS_EPS)
        xh = xv * r
        err = xh * gv - t_ref[...]
        dyv = err * (1.0 / d)
        u = dyv * gv
        dh_ref[...] = r * (u - xh * jnp.mean(u * xh, axis=-1, keepdims=True))

        @pl.when(i == 0)
        def _():
            dg_ref[...] = jnp.zeros_like(dg_ref)
            sq_ref[...] = jnp.zeros_like(sq_ref)

        dg_ref[...] += jnp.sum(dyv * xh, axis=0, keepdims=True)
        sq_ref[...] += jnp.sum(err * err, axis=0, keepdims=True)

        @pl.when(i == n_steps - 1)
        def _():
            total = jnp.sum(sq_ref[...], axis=-1, keepdims=True) * (0.5 / d)
            loss_ref[...] = jnp.broadcast_to(total, loss_ref.shape)

    row = pl.BlockSpec((tt, d), lambda i: (i, 0))
    vec = pl.BlockSpec((1, d), lambda i: (0, 0))
    return pl.pallas_call(
        body, name="down_final_loss", grid=(n_steps,),
        in_specs=[pl.BlockSpec((n_s, tt, f4), lambda i: (0, i, 0)), pl.BlockSpec(wd3.shape, lambda i: (0, 0, 0)),
                  row, vec, row],
        out_specs=[pl.BlockSpec((1, LANES), lambda i: (0, 0)), row, vec],
        out_shape=[jax.ShapeDtypeStruct((1, LANES), F32), jax.ShapeDtypeStruct((t_dim, d), F32),
                   jax.ShapeDtypeStruct((1, d), F32)],
        scratch_shapes=[pltpu.VMEM((1, d), F32)],
        compiler_params=_params(("arbitrary",)),
    )(act3, wd3, h, g, target)


def _rope_table(pos, inv_lane, sel_a, sel_b, *, tt=512):
    t_dim = pos.shape[0]

    def body(p_ref, f_ref, a_ref, b_ref, c_ref, s_ref):
        ang = p_ref[...] * f_ref[...]
        on = (a_ref[...] + b_ref[...]) > 0.0
        c_ref[...] = jnp.where(on, jnp.cos(ang), 1.0)
        s_ref[...] = jnp.where(on, jnp.sin(ang), 0.0)

    vec = pl.BlockSpec((1, LANES), lambda i: (0, 0))
    row = pl.BlockSpec((tt, LANES), lambda i: (i, 0))
    shp = jax.ShapeDtypeStruct((t_dim, LANES), F32)
    return pl.pallas_call(
        body, name="rope_table", grid=(t_dim // tt,),
        in_specs=[pl.BlockSpec((tt, 1), lambda i: (i, 0)), vec, vec, vec],
        out_specs=[row, row], out_shape=[shp, shp],
        compiler_params=_params(("parallel",)),
    )(pos, inv_lane, sel_a, sel_b)


def _rotate(xv, cs, sn, sa, sb):
    half = ROPE_DIM // 2
    up = pltpu.roll(xv, LANES - half, 1)
    dn = pltpu.roll(xv, half, 1)
    return xv * cs + (dn * sb - up * sa) * sn


def _head_masks():
    h1 = lax.broadcasted_iota(jnp.int32, (1, LANES), 1) < HEAD_DIM
    return h1, jnp.logical_not(h1)


def _split_heads(xv, h1, h2):
    return jnp.where(h1, xv, 0.0).astype(BF16), jnp.where(h2, xv, 0.0).astype(BF16)


def _tri_masks():
    r = lax.broadcasted_iota(jnp.int32, (BLOCK, BLOCK), 0)
    c = lax.broadcasted_iota(jnp.int32, (BLOCK, BLOCK), 1)
    return c <= r, r <= c


def _stream_rows(start, dil):
    if dil == 1:
        return pl.ds(pl.multiple_of(start, BLOCK), BLOCK)
    return pl.ds(start, BLOCK, stride=dil)


def _dil_tile(idx, dil, nb):
    r = idx // nb
    n = idx % nb
    return (_stream_rows(r + dil * BLOCK * n, dil), _stream_rows(r + dil * BLOCK * jnp.maximum(n - 1, 0), dil),
            n > 0)


def _dil_specs(b_dim, s_dim):
    def col(c0):
        return pl.BlockSpec((None, s_dim, LANES),lambda b, h: (b, 0, c0 + h))
    tab = pl.BlockSpec((None, s_dim, LANES),lambda b, h: (b, 0, 0))
    vec = pl.BlockSpec((1, LANES), lambda b, h: (0, 0))
    return col, tab, vec


def _dil_fwd(proj3, cs3, sn3, sel_a, sel_b, rider=None):
    b_dim, s_dim, _ = proj3.shape
    scale = HEAD_DIM ** -0.5
    n_pat = len(DIL_PATTERNS)
    extra, extra_shapes, extra_sems = _rider_parts(rider)
    n_w = len(extra)
    n_steps = b_dim * PAIRS

    def body(*refs):
        q_ref, k_ref, v_ref, cs_ref, sn_ref, sa_ref, sb_ref = refs[:7]
        o16_ref, o32_ref, l_ref = refs[7 + n_w:10 + n_w]
        qr, kr = refs[10 + 2 * n_w:12 + 2 * n_w]
        per_pattern = refs[12 + 2 * n_w:12 + 2 * n_w + 2 * n_pat]
        og, lg = per_pattern[:n_pat], per_pattern[n_pat:]
        step = pl.program_id(0) * PAIRS + pl.program_id(1)
        begin, end = _rider_hooks(rider, refs[7:7 + n_w], refs[10 + n_w:10 + 2 * n_w], refs[-2:], step, n_steps)
        begin()
        h1, h2 = _head_masks()
        cur_ok, prev_ok = _tri_masks()
        sa, sb = sa_ref[...], sb_ref[...]

        def prep(j, _):
            rows = pl.ds(pl.multiple_of(j * DIL_CHUNK, DIL_CHUNK), DIL_CHUNK)
            cs, sn = cs_ref[rows, :], sn_ref[rows, :]
            qr[rows, :] = _rotate(q_ref[rows, :], cs, sn, sa, sb) * scale
            kr[rows, :] = _rotate(k_ref[rows, :], cs, sn, sa, sb)
            return 0

        lax.fori_loop(0, s_dim // DIL_CHUNK, prep, 0)

        for g, (_, dil) in enumerate(DIL_PATTERNS):
            nb = s_dim // dil // BLOCK

            def some(bi, _, g=g, dil=dil, nb=nb):
                tiles = [_dil_tile(bi * DIL_BATCH + t, dil, nb) for t in range(DIL_BATCH)]
                rows = [t[0] for t in tiles]
                q1, q2 = _split_heads(jnp.stack([qr[rw, :] for rw in rows]), h1, h2)
                kc = jnp.stack([kr[rw, :] for rw in rows]).astype(BF16)
                vc1, vc2 = _split_heads(jnp.stack([v_ref[rw, :] for rw in rows]), h1, h2)
                if nb > 1:
                    kp = jnp.stack([kr[t[1], :] for t in tiles]).astype(BF16)
                    vp1, vp2 = _split_heads(jnp.stack([v_ref[t[1], :] for t in tiles]), h1, h2)
                    p_ok = jnp.stack([jnp.logical_and(prev_ok, t[2]) for t in tiles])

                def head(qh, vch, vph):
                    sc = jnp.where(cur_ok, lax.dot_general(qh, kc, BNT, preferred_element_type=F32), -jnp.inf)
                    m = jnp.max(sc, axis=-1, keepdims=True)
                    if nb > 1:
                        sp = jnp.where(p_ok, lax.dot_general(qh, kp, BNT, preferred_element_type=F32), -jnp.inf)
                        m = jnp.maximum(m, jnp.max(sp, axis=-1, keepdims=True))
                    pc = jnp.exp(sc - m)
                    den = jnp.sum(pc, axis=-1, keepdims=True)
                    acc = lax.dot_general(pc.astype(BF16), vch, BNN, preferred_element_type=F32)
                    if nb > 1:
                        pp = jnp.exp(sp - m)
                        den = den + jnp.sum(pp, axis=-1, keepdims=True)
                        acc = acc + lax.dot_general(pp.astype(BF16), vph, BNN, preferred_element_type=F32)
                    return acc / den, m + jnp.log(den)

                o1, l1 = head(q1, vc1, vp1 if nb > 1 else None)
                o2, l2 = head(q2, vc2, vp2 if nb > 1 else None)
                o, l = o1 + o2, jnp.where(h1, l1, l2)
                for t, rw in enumerate(rows):
                    og[g][rw, :] = o[t]
                    lg[g][rw, :] = l[t]
                return 0

            lax.fori_loop(0, dil * nb // DIL_BATCH, some, 0)

        def comb(j, _):
            rows = pl.ds(pl.multiple_of(j * DIL_CHUNK, DIL_CHUNK), DIL_CHUNK)
            ls = [lg[g][rows, :] for g in range(n_pat)]
            m = jnp.maximum(jnp.maximum(ls[0], ls[1]), ls[2])
            es = [jnp.exp(l - m) for l in ls]
            den = es[0] + es[1] + es[2]
            o = (es[0] * og[0][rows, :] + es[1] * og[1][rows, :] + es[2] * og[2][rows, :]) / den
            o16_ref[rows, :] = o.astype(BF16)
            o32_ref[rows, :] = o
            l_ref[rows, :] = m + jnp.log(den)
            return 0

        lax.fori_loop(0, s_dim // DIL_CHUNK, comb, 0)
        end()

    col, tab, vec = _dil_specs(b_dim, s_dim)
    out = pl.BlockSpec((None, s_dim, LANES),lambda b, h: (b, 0, h))
    shp = (b_dim, s_dim, ATT_WIDTH)
    res = pl.pallas_call(
        body, name="dil_fwd", grid=(b_dim, PAIRS),
        in_specs=[col(COL_QA), col(COL_KA), col(COL_VA), tab, tab, vec, vec] + [ANY] * n_w,
        out_specs=[out, out, out] + [ANY] * n_w,
        out_shape=[jax.ShapeDtypeStruct(shp, BF16), jax.ShapeDtypeStruct(shp, F32), jax.ShapeDtypeStruct(shp, F32)]
        + extra_shapes,
        scratch_shapes=[pltpu.VMEM((s_dim, LANES), F32)] * (2 + 2 * n_pat) + extra_sems,
        compiler_params=_params(("arbitrary", "arbitrary")),
    )(proj3, proj3, proj3, cs3, sn3, sel_a, sel_b, *extra)
    return res[:3], res[3:]


def _dil_bwd(proj3, cs3, sn3, sel_a, sel_b, do3, o3, lse3, rider=None):
    b_dim, s_dim, _ = proj3.shape
    scale = HEAD_DIM ** -0.5
    extra, extra_shapes, extra_sems = _rider_parts(rider)
    n_w = len(extra)

    def body(*refs):
        q_ref, k_ref, v_ref, cs_ref, sn_ref, sa_ref, sb_ref, do_ref, o_ref, l_ref = refs[:10]
        dq_ref, dk_ref, dv_ref = refs[10 + n_w:13 + n_w]
        qr, kr, dqa, dka, dva = refs[13 + 2 * n_w:18 + 2 * n_w]
        step = pl.program_id(0) * PAIRS + pl.program_id(1)
        begin, end = _rider_hooks(rider, refs[10:10 + n_w], refs[13 + n_w:13 + 2 * n_w], refs[-2:], step,
                                  b_dim * PAIRS)
        begin()
        h1, h2 = _head_masks()
        cur_ok, prev_ok = _tri_masks()
        sa, sb = sa_ref[...], sb_ref[...]

        def prep(j, _):
            rows = pl.ds(pl.multiple_of(j * DIL_CHUNK, DIL_CHUNK), DIL_CHUNK)
            cs, sn = cs_ref[rows, :], sn_ref[rows, :]
            qr[rows, :] = _rotate(q_ref[rows, :], cs, sn, sa, sb) * scale
            kr[rows, :] = _rotate(k_ref[rows, :], cs, sn, sa, sb)
            zero = jnp.zeros((DIL_CHUNK, LANES), F32)
            dqa[rows, :] = zero
            dka[rows, :] = zero
            dva[rows, :] = zero
            return 0

        lax.fori_loop(0, s_dim // DIL_CHUNK, prep, 0)

        for _, dil in DIL_PATTERNS:
            nb = s_dim // dil // BLOCK

            def some(bi, _, dil=dil, nb=nb):
                tiles = [_dil_tile(bi * DIL_BATCH + t, dil, nb) for t in range(DIL_BATCH)]
                rows = [t[0] for t in tiles]
                q1, q2 = _split_heads(jnp.stack([qr[rw, :] for rw in rows]), h1, h2)
                dof = jnp.stack([do_ref[rw, :] for rw in rows])
                do1, do2 = _split_heads(dof, h1, h2)
                prod = dof * jnp.stack([o_ref[rw, :] for rw in rows])
                delta1 = jnp.sum(jnp.where(h1, prod, 0.0), axis=-1, keepdims=True)
                delta2 = jnp.sum(jnp.where(h2, prod, 0.0), axis=-1, keepdims=True)
                lt = jnp.stack([l_ref[rw, :] for rw in rows])
                lse1 = jnp.max(jnp.where(h1, lt, -jnp.inf), axis=-1, keepdims=True)
                lse2 = jnp.max(jnp.where(h2, lt, -jnp.inf), axis=-1, keepdims=True)

                def side(krows, ok):
                    kf = jnp.stack([kr[kw, :] for kw in krows])
                    k16 = kf.astype(BF16)
                    k1, k2 = _split_heads(kf, h1, h2)
                    v16 = jnp.stack([v_ref[kw, :] for kw in krows]).astype(BF16)

                    def head(qh, doh, lse, delta):
                        sc = lax.dot_general(qh, k16, BNT, preferred_element_type=F32)
                        p = jnp.where(ok, jnp.exp(sc - lse), 0.0)
                        dp = lax.dot_general(doh, v16, BNT, preferred_element_type=F32)
                        return p.astype(BF16), (p * (dp - delta)).astype(BF16)

                    p1, ds1 = head(q1, do1, lse1, delta1)
                    p2, ds2 = head(q2, do2, lse2, delta2)
                    dv = (lax.dot_general(p1, do1, BTN, preferred_element_type=F32)
                          + lax.dot_general(p2, do2, BTN, preferred_element_type=F32))
                    dk = (lax.dot_general(ds1, q1, BTN, preferred_element_type=F32)
                          + lax.dot_general(ds2, q2, BTN, preferred_element_type=F32))
                    for t, kw in enumerate(krows):
                        dva[kw, :] += dv[t]
                        dka[kw, :] += dk[t]
                    return (lax.dot_general(ds1, k1, BNN, preferred_element_type=F32)
                            + lax.dot_general(ds2, k2, BNN, preferred_element_type=F32))

                dq = side(rows, cur_ok)
                if nb > 1:
                    dq = dq + side([t[1] for t in tiles], jnp.stack([jnp.logical_and(prev_ok, t[2]) for t in tiles]))
                for t, rw in enumerate(rows):
                    dqa[rw, :] += dq[t] * scale
                return 0

            lax.fori_loop(0, dil * nb // DIL_BATCH, some, 0)

        def finish(j, _):
            rows = pl.ds(pl.multiple_of(j * DIL_CHUNK, DIL_CHUNK), DIL_CHUNK)
            cs, sn = cs_ref[rows, :], -sn_ref[rows, :]
            dq_ref[rows, :] = _rotate(dqa[rows, :], cs, sn, sa, sb).astype(BF16)
            dk_ref[rows, :] = _rotate(dka[rows, :], cs, sn, sa, sb).astype(BF16)
            dv_ref[rows, :] = dva[rows, :].astype(BF16)
            return 0

        lax.fori_loop(0, s_dim // DIL_CHUNK, finish, 0)
        end()

    col, tab, vec = _dil_specs(b_dim, s_dim)
    out = pl.BlockSpec((None, s_dim, LANES),lambda b, h: (b, 0, h))
    shp = jax.ShapeDtypeStruct((b_dim, s_dim, ATT_WIDTH), BF16)
    acc = pltpu.VMEM((s_dim, LANES), F32)
    res = pl.pallas_call(
        body, name="dil_bwd", grid=(b_dim, PAIRS),
        in_specs=[col(COL_QA), col(COL_KA), col(COL_VA), tab, tab, vec, vec, out, out, out] + [ANY] * n_w,
        out_specs=[out, out, out] + [ANY] * n_w, out_shape=[shp, shp, shp] + extra_shapes,
        scratch_shapes=[acc, acc, acc, acc, acc] + extra_sems,
        compiler_params=_params(("arbitrary", "arbitrary")),
    )(proj3, proj3, proj3, cs3, sn3, sel_a, sel_b, do3, o3, lse3, *extra)
    return res[:3], res[3:]


def _split_dot(x, tri):
    hi = x.astype(BF16)
    lo = (x - hi.astype(F32)).astype(BF16)
    return jnp.dot(hi, tri, preferred_element_type=F32) + jnp.dot(lo, tri, preferred_element_type=F32)


def _log_sigmoid(z):
    return jnp.minimum(z, 0.0) - jnp.log(1.0 + jnp.exp(-jnp.abs(z)))


def _sb_scores(qh, k16, valid):
    z = lax.dot_general(qh, k16, NT, preferred_element_type=F32)
    ls = _log_sigmoid(z)
    l1m = ls - z
    return ls, (l1m if valid is None else jnp.where(valid, l1m, 0.0))


def _sb_consts():
    r = lax.broadcasted_iota(jnp.int32, (BLOCK, BLOCK), 0)
    c = lax.broadcasted_iota(jnp.int32, (BLOCK, BLOCK), 1)
    after = (r > c).astype(BF16)
    before = (r < c).astype(BF16)
    qrow = lax.broadcasted_iota(jnp.int32, (SB_ROWS, BLOCK), 0)
    kcol = lax.broadcasted_iota(jnp.int32, (SB_ROWS, BLOCK), 1)
    return after, before, qrow, kcol


def _below(whole, lo, delta):
    if lo == 0:
        return whole + delta
    return whole + jnp.concatenate([jnp.zeros((lo,) + delta.shape[1:], delta.dtype), delta], axis=0)


def _pairs_loop(n_blocks, step, carry, per_iter):
    def several(i, c):
        for j in range(per_iter):
            c = step(per_iter * i + j, c)
        return c

    return lax.fori_loop(0, n_blocks // per_iter, several, carry)


def _sb_fwd(proj3, rider=None):
    b_dim, s_dim, _ = proj3.shape
    scale = HEAD_DIM ** -0.5
    per = SB_ROWS // BLOCK
    extra, extra_shapes, extra_sems = _rider_parts(rider)
    n_w = len(extra)

    def body(*refs):
        q_ref, k_ref, v_ref = refs[:3]
        o_ref = refs[3 + n_w]
        step = pl.program_id(0) * PAIRS + pl.program_id(1)
        begin, end = _rider_hooks(rider, refs[3:3 + n_w], refs[4 + n_w:4 + 2 * n_w], refs[-2:], step, b_dim * PAIRS)
        begin()
        h1, h2 = _head_masks()
        after, _, qrow, kcol = _sb_consts()

        def qloop(qi, _):
            rows = pl.ds(pl.multiple_of(qi * SB_ROWS, SB_ROWS), SB_ROWS)
            q1, q2 = _split_heads(q_ref[rows, :] * scale, h1, h2)
            first = qi * per

            def block(kb, carry, lo):
                acc, run1, run2 = carry
                krows = pl.ds(pl.multiple_of(kb * BLOCK, BLOCK), BLOCK)
                k16 = k_ref[krows, :].astype(BF16)
                v1, v2 = _split_heads(v_ref[krows, :], h1, h2)
                valid = None if lo is None else kcol[:SB_ROWS - lo] < qrow[:SB_ROWS - lo]
                lo = lo or 0

                def head(qh, vh, run):
                    ls, l1m = _sb_scores(qh[lo:], k16, valid)
                    a = jnp.exp(ls + _split_dot(l1m, after) + run[lo:])
                    if valid is not None:
                        a = jnp.where(valid, a, 0.0)
                    return (jnp.dot(a.astype(BF16), vh, preferred_element_type=F32),
                            _below(run, lo, jnp.sum(l1m, axis=-1, keepdims=True)))

                o1, run1 = head(q1, v1, run1)
                o2, run2 = head(q2, v2, run2)
                return _below(acc, lo, o1 + o2), run1, run2

            zcol = jnp.zeros((SB_ROWS, 1), F32)
            carry = (jnp.zeros((SB_ROWS, LANES), F32), zcol, zcol)
            for kl in reversed(range(per)):
                carry = block(first + kl, carry, kl * BLOCK)
            acc, _, _ = _pairs_loop(first, lambda i, c: block(first - 1 - i, c, None), carry, SB_STEP_FWD)
            o_ref[rows, :] = acc.astype(BF16)
            return 0

        lax.fori_loop(0, s_dim // SB_ROWS, qloop, 0)
        end()

    def col(c0):
        return pl.BlockSpec((None, s_dim, LANES),lambda b, h: (b, 0, c0 + h))

    res = pl.pallas_call(
        body, name="sb_fwd", grid=(b_dim, PAIRS),
        in_specs=[col(COL_QB), col(COL_KB), col(COL_VB)] + [ANY] * n_w, out_specs=[col(0)] + [ANY] * n_w,
        out_shape=[jax.ShapeDtypeStruct((b_dim, s_dim, ATT_WIDTH), BF16)] + extra_shapes,
        scratch_shapes=extra_sems,
        compiler_params=_params(("arbitrary", "arbitrary")),
    )(proj3, proj3, proj3, *extra)
    return res[0], res[1:]


def _sb_bwd(proj3, do3, rider=None):
    b_dim, s_dim, _ = proj3.shape
    scale = HEAD_DIM ** -0.5
    per = SB_ROWS // BLOCK
    nkb_max = s_dim // BLOCK
    extra, extra_shapes, extra_sems = _rider_parts(rider)
    n_w = len(extra)

    def body(*refs):
        q_ref, k_ref, v_ref, do_ref = refs[:4]
        dq_ref, dk_ref, dv_ref = refs[4 + n_w:7 + n_w]
        dka, dva, e_ref, sg_ref = refs[7 + 2 * n_w:11 + 2 * n_w]
        step = pl.program_id(0) * PAIRS + pl.program_id(1)
        begin, end = _rider_hooks(rider, refs[4:4 + n_w], refs[7 + n_w:7 + 2 * n_w], refs[-2:], step, b_dim * PAIRS)
        begin()
        h1, h2 = _head_masks()
        after, before, qrow, kcol = _sb_consts()
        dka[...] = jnp.zeros_like(dka)
        dva[...] = jnp.zeros_like(dva)

        def qloop(qi, _):
            rows = pl.ds(pl.multiple_of(qi * SB_ROWS, SB_ROWS), SB_ROWS)
            q1, q2 = _split_heads(q_ref[rows, :] * scale, h1, h2)
            do1, do2 = _split_heads(do_ref[rows, :].astype(F32), h1, h2)
            first = qi * per

            def pass1(kb, carry, lo):
                run1, run2 = carry
                krows = pl.ds(pl.multiple_of(kb * BLOCK, BLOCK), BLOCK)
                k16 = k_ref[krows, :].astype(BF16)
                v16 = v_ref[krows, :].astype(BF16)
                valid = None if lo is None else kcol[:SB_ROWS - lo] < qrow[:SB_ROWS - lo]
                lo = lo or 0
                part = pl.ds(lo, SB_ROWS - lo)

                def head(h, qh, doh, run):
                    ls, l1m = _sb_scores(qh[lo:], k16, valid)
                    a = jnp.exp(ls + _split_dot(l1m, after) + run[lo:])
                    if valid is not None:
                        a = jnp.where(valid, a, 0.0)
                    da = lax.dot_general(doh[lo:], v16, NT, preferred_element_type=F32)
                    e_ref[h, kb, part, :] = a * da
                    sg_ref[h, kb, part, :] = jnp.exp(ls)
                    return a.astype(BF16), _below(run, lo, jnp.sum(l1m, axis=-1, keepdims=True))

                a1, run1 = head(0, q1, do1, run1)
                a2, run2 = head(1, q2, do2, run2)
                dva[krows, :] += (lax.dot_general(a1, do1[lo:], TN, preferred_element_type=F32)
                                  + lax.dot_general(a2, do2[lo:], TN, preferred_element_type=F32))
                return run1, run2

            zcol = jnp.zeros((SB_ROWS, 1), F32)
            carry = (zcol, zcol)
            for kl in reversed(range(per)):
                carry = pass1(first + kl, carry, kl * BLOCK)
            _pairs_loop(first, lambda i, c: pass1(first - 1 - i, c, None), carry, SB_STEP_BWD)

            def pass2(kb, carry, lo):
                dq, pre1, pre2 = carry
                krows = pl.ds(pl.multiple_of(kb * BLOCK, BLOCK), BLOCK)
                k1, k2 = _split_heads(k_ref[krows, :], h1, h2)
                valid = None if lo is None else kcol[:SB_ROWS - lo] < qrow[:SB_ROWS - lo]
                lo = lo or 0
                part = pl.ds(lo, SB_ROWS - lo)

                def head(h, pre):
                    ev = e_ref[h, kb, part, :]
                    sg = sg_ref[h, kb, part, :]
                    dz = ev * (1.0 - sg) - (_split_dot(ev, before) + pre[lo:]) * sg
                    if valid is not None:
                        dz = jnp.where(valid, dz, 0.0)
                    return dz.astype(BF16), _below(pre, lo, jnp.sum(ev, axis=-1, keepdims=True))

                dz1, pre1 = head(0, pre1)
                dz2, pre2 = head(1, pre2)
                dka[krows, :] += (lax.dot_general(dz1, q1[lo:], TN, preferred_element_type=F32)
                                  + lax.dot_general(dz2, q2[lo:], TN, preferred_element_type=F32))
                dq = _below(dq, lo, jnp.dot(dz1, k1, preferred_element_type=F32)
                            + jnp.dot(dz2, k2, preferred_element_type=F32))
                return dq, pre1, pre2

            carry = _pairs_loop(first, lambda i, c: pass2(i, c, None), (jnp.zeros((SB_ROWS, LANES), F32), zcol, zcol),
                                SB_STEP_BWD)
            for kl in range(per):
                carry = pass2(first + kl, carry, kl * BLOCK)
            dq = carry[0]
            dq_ref[rows, :] = (dq * scale).astype(BF16)
            return 0

        lax.fori_loop(0, s_dim // SB_ROWS, qloop, 0)
        dk_ref[...] = dka[...].astype(BF16)
        dv_ref[...] = dva[...].astype(BF16)
        end()

    def col(c0):
        return pl.BlockSpec((None, s_dim, LANES),lambda b, h: (b, 0, c0 + h))

    shp = jax.ShapeDtypeStruct((b_dim, s_dim, ATT_WIDTH), BF16)
    acc = pltpu.VMEM((s_dim, LANES), F32)
    strip = pltpu.VMEM((2, nkb_max, SB_ROWS, BLOCK), F32)
    res = pl.pallas_call(
        body, name="sb_bwd", grid=(b_dim, PAIRS),
        in_specs=[col(COL_QB), col(COL_KB), col(COL_VB), col(0)] + [ANY] * n_w,
        out_specs=[col(0), col(0), col(0)] + [ANY] * n_w,
        out_shape=[shp, shp, shp] + extra_shapes,
        scratch_shapes=[acc, acc, strip, strip] + extra_sems,
        compiler_params=_params(("arbitrary", "arbitrary")),
    )(proj3, proj3, proj3, do3, *extra)
    return res[:3], res[3:]


def _sigmoid(x):
    return 1.0 / (1.0 + jnp.exp(-x))


def _gate_out_norm(proj, ua, ub, w_out, x, g, *, tt=512):
    t_dim, d = ua.shape

    def body(ga_ref, gb_ref, ua_ref, ub_ref, w_ref, x_ref, g_ref, m_ref, h_ref, n_ref):
        mixed = (_sigmoid(ga_ref[...]) * ua_ref[...] + _sigmoid(gb_ref[...]) * ub_ref[...]).astype(BF16)
        m_ref[...] = mixed
        hv = x_ref[...] + jnp.dot(mixed, w_ref[...], preferred_element_type=F32)
        h_ref[...] = hv
        r = lax.rsqrt(jnp.mean(hv * hv, axis=-1, keepdims=True) + RMS_EPS)
        n_ref[...] = ((hv * r) * g_ref[...]).astype(BF16)

    row = pl.BlockSpec((tt, d), lambda i: (i, 0))
    return pl.pallas_call(
        body, name="gate_out_norm", grid=(t_dim // tt,),
        in_specs=[pl.BlockSpec((tt, d), lambda i: (i, 3)), pl.BlockSpec((tt, d), lambda i: (i, 4)), row, row,
                  pl.BlockSpec((d, d), lambda i: (0, 0)), row, pl.BlockSpec((1, d), lambda i: (0, 0))],
        out_specs=[row, row, row],
        out_shape=[jax.ShapeDtypeStruct((t_dim, d), BF16), jax.ShapeDtypeStruct((t_dim, d), F32),
                   jax.ShapeDtypeStruct((t_dim, d), BF16)],
        compiler_params=_params(("parallel",)),
    )(proj, proj, ua, ub, w_out, x, g)


def _out_dx_gate_bwd(dh, w_out, proj, ua, ub, *, tt=512):
    t_dim, d = ua.shape

    def body(dh_ref, w_ref, ga_ref, gb_ref, ua_ref, ub_ref, dua_ref, dub_ref, dg_ref):
        dm = lax.dot_general(dh_ref[...].astype(BF16), w_ref[...], NT, preferred_element_type=F32)
        sa = _sigmoid(ga_ref[...])
        sb = _sigmoid(gb_ref[...])
        dua_ref[...] = (dm * sa).astype(BF16)
        dub_ref[...] = (dm * sb).astype(BF16)
        dg_ref[:, :d] = (dm * ua_ref[...] * (sa * (1.0 - sa))).astype(BF16)
        dg_ref[:, d:] = (dm * ub_ref[...] * (sb * (1.0 - sb))).astype(BF16)

    row = pl.BlockSpec((tt, d), lambda i: (i, 0))
    wide = pl.BlockSpec((tt, 2 * d), lambda i: (i, 0))
    return pl.pallas_call(
        body, name="out_dx_gate_bwd", grid=(t_dim // tt,),
        in_specs=[row, pl.BlockSpec((d, d), lambda i: (0, 0)),
                  pl.BlockSpec((tt, d), lambda i: (i, 3)), pl.BlockSpec((tt, d), lambda i: (i, 4)), row, row],
        out_specs=[row, row, wide],
        out_shape=[jax.ShapeDtypeStruct((t_dim, d), BF16), jax.ShapeDtypeStruct((t_dim, d), BF16),
                   jax.ShapeDtypeStruct((t_dim, 2 * d), BF16)],
        compiler_params=_params(("parallel",)),
    )(dh, w_out, proj, proj, ua, ub)


def _ffn_up_swiglu(n, wg3, wu3, *, tt=512):
    t_dim, d = n.shape
    n_s, f4, _ = wg3.shape

    def body(n_ref, wg_ref, wu_ref, g_ref, u_ref, a_ref):
        nv = n_ref[...]
        for s in range(n_s):
            gv = lax.dot_general(nv, wg_ref[s], NT, preferred_element_type=F32)
            uv = lax.dot_general(nv, wu_ref[s], NT, preferred_element_type=F32)
            g_ref[s] = gv.astype(BF16)
            u_ref[s] = uv.astype(BF16)
            a_ref[s] = (gv * _sigmoid(gv) * uv).astype(BF16)

    wspec = pl.BlockSpec(wg3.shape, lambda i: (0, 0, 0))
    ospec = pl.BlockSpec((n_s, tt, f4), lambda i: (0, i, 0))
    shp = (n_s, t_dim, f4)
    return pl.pallas_call(
        body, name="ffn_up_swiglu", grid=(t_dim // tt,),
        in_specs=[pl.BlockSpec((tt, d), lambda i: (i, 0)), wspec, wspec], out_specs=[ospec, ospec, ospec],
        out_shape=[jax.ShapeDtypeStruct(shp, BF16)] * 3,
        compiler_params=_params(("parallel",)),
    )(n, wg3, wu3)


def _ffn_down_dx_swiglu(dh, wd3, g3, u3, *, tt=512):
    t_dim, d = dh.shape
    n_s, f4, _ = wd3.shape

    def body(dh_ref, w_ref, g_ref, u_ref, dg_ref, du_ref):
        dhv = dh_ref[...].astype(BF16)
        for s in range(n_s):
            da = lax.dot_general(dhv, w_ref[s], NT, preferred_element_type=F32)
            gv = g_ref[s].astype(F32)
            sg = _sigmoid(gv)
            dg_ref[s] = (da * u_ref[s].astype(F32) * (sg + gv * sg * (1.0 - sg))).astype(BF16)
            du_ref[s] = (da * (gv * sg)).astype(BF16)

    spec = pl.BlockSpec((n_s, tt, f4), lambda i: (0, i, 0))
    shp = jax.ShapeDtypeStruct((n_s, t_dim, f4), BF16)
    return pl.pallas_call(
        body, name="ffn_down_dx_swiglu", grid=(t_dim // tt,),
        in_specs=[pl.BlockSpec((tt, d), lambda i: (i, 0)), pl.BlockSpec(wd3.shape, lambda i: (0, 0, 0)), spec, spec],
        out_specs=[spec, spec], out_shape=[shp, shp],
        compiler_params=_params(("parallel",)),
    )(dh, wd3, g3, u3)


def _mem_fwd(qm, kvm, *, tt=2048):
    b_dim, s_dim, _ = qm.shape
    n_mem = kvm.shape[1]
    scale = MEM_HEAD_DIM ** -0.5

    def body(q_ref, k_ref, v_ref, o_ref):
        sc = lax.dot_general(q_ref[0], k_ref[0], NT, preferred_element_type=F32) * scale
        p = jnp.exp(sc - jnp.max(sc, axis=-1, keepdims=True))
        p = p / jnp.sum(p, axis=-1, keepdims=True)
        o_ref[0] = jnp.dot(p.astype(BF16), v_ref[0], preferred_element_type=F32).astype(BF16)

    qs = pl.BlockSpec((1, tt, MEM_HEAD_DIM), lambda b, h, i: (b, i, h))
    return pl.pallas_call(
        body, name="mem_fwd", grid=(b_dim, N_HEADS_MEM, s_dim // tt),
        in_specs=[qs, pl.BlockSpec((1, n_mem, MEM_HEAD_DIM), lambda b, h, i: (b, 0, h)),
                  pl.BlockSpec((1, n_mem, MEM_HEAD_DIM), lambda b, h, i: (b, 0, N_HEADS_MEM + h))],
        out_specs=qs, out_shape=jax.ShapeDtypeStruct(qm.shape, BF16),
        compiler_params=_params(("parallel", "parallel", "parallel")),
    )(qm, kvm, kvm)


def _mem_bwd(qm, kvm, dom, *, tt=2048):
    b_dim, s_dim, _ = qm.shape
    n_mem = kvm.shape[1]
    scale = MEM_HEAD_DIM ** -0.5

    def body(q_ref, k_ref, v_ref, do_ref, dq_ref, dk_ref, dv_ref):
        qv, kv, vv, dov = q_ref[0], k_ref[0], v_ref[0], do_ref[0]
        sc = lax.dot_general(qv, kv, NT, preferred_element_type=F32) * scale
        p = jnp.exp(sc - jnp.max(sc, axis=-1, keepdims=True))
        p = p / jnp.sum(p, axis=-1, keepdims=True)
        dp = lax.dot_general(dov, vv, NT, preferred_element_type=F32)
        ds = (p * (dp - jnp.sum(p * dp, axis=-1, keepdims=True)) * scale).astype(BF16)
        dq_ref[0] = jnp.dot(ds, kv, preferred_element_type=F32).astype(BF16)

        @pl.when(pl.program_id(2) == 0)
        def _():
            dk_ref[...] = jnp.zeros_like(dk_ref)
            dv_ref[...] = jnp.zeros_like(dv_ref)

        dk_ref[0] += lax.dot_general(ds, qv, TN, preferred_element_type=F32)
        dv_ref[0] += lax.dot_general(p.astype(BF16), dov, TN, preferred_element_type=F32)

    qs = pl.BlockSpec((1, tt, MEM_HEAD_DIM), lambda b, h, i: (b, i, h))
    ks = pl.BlockSpec((1, n_mem, MEM_HEAD_DIM), lambda b, h, i: (b, 0, h))
    vs = pl.BlockSpec((1, n_mem, MEM_HEAD_DIM), lambda b, h, i: (b, 0, N_HEADS_MEM + h))
    return pl.pallas_call(
        body, name="mem_bwd", grid=(b_dim, N_HEADS_MEM, s_dim // tt),
        in_specs=[qs, ks, vs, qs], out_specs=[qs, ks, ks],
        out_shape=[jax.ShapeDtypeStruct(qm.shape, BF16), jax.ShapeDtypeStruct((b_dim, n_mem, MEM_WIDTH), F32),
                   jax.ShapeDtypeStruct((b_dim, n_mem, MEM_WIDTH), F32)],
        compiler_params=_params(("parallel", "parallel", "arbitrary")),
    )(qm, kvm, kvm, dom)


def _adamw_math(wv, gv, mv, vv):
    nm = ADAM_B1 * mv + (1.0 - ADAM_B1) * gv
    nv = ADAM_B2 * vv + (1.0 - ADAM_B2) * (gv * gv)
    m_hat = nm / (1.0 - ADAM_B1 ** ADAM_STEP)
    v_hat = nv / (1.0 - ADAM_B2 ** ADAM_STEP)
    return -ADAM_LR * (m_hat / (jnp.sqrt(v_hat) + ADAM_EPS) + ADAM_WD * wv), nm, nv


def _adamw(w, g, m, v, *, name):
    rows, cols = w.shape
    tr = _tile(rows, 256, 8)

    def body(w_ref, g_ref, m_ref, v_ref, d_ref, nm_ref, nv_ref):
        d_ref[...], nm_ref[...], nv_ref[...] = _adamw_math(w_ref[...], g_ref[...], m_ref[...], v_ref[...])

    spec = pl.BlockSpec((tr, cols), lambda i: (i, 0))
    shp = jax.ShapeDtypeStruct((rows, cols), F32)
    return pl.pallas_call(
        body, name=name, grid=(rows // tr,),
        in_specs=[spec] * 4, out_specs=[spec] * 3, out_shape=[shp] * 3,
        compiler_params=_params(("parallel",)),
    )(w, g, m, v)


def _prefetch_spec(grid, in_specs, out_specs):
    return pltpu.PrefetchScalarGridSpec(num_scalar_prefetch=1, grid=grid, in_specs=in_specs, out_specs=out_specs)


def _adamw_halves(w, mine, theirs, m, v, c_idx, *, name):
    rows, cols = w.shape
    half = rows // 2
    tr = _tile(half, _row_cap(cols), 8)
    nh = half // tr

    def body(c_ref, w_ref, mine_ref, theirs_ref, m_ref, v_ref, g_ref, d_ref, nm_ref, nv_ref):
        gv = jnp.where(pl.program_id(0) == c_ref[0], mine_ref[...], theirs_ref[...])
        g_ref[...] = gv
        d_ref[...], nm_ref[...], nv_ref[...] = _adamw_math(w_ref[...], gv, m_ref[...], v_ref[...])

    full = pl.BlockSpec((tr, cols), lambda h, i, c_ref: (h * nh + i, 0))
    part = pl.BlockSpec((tr, cols), lambda h, i, c_ref: (i, 0))
    shp = jax.ShapeDtypeStruct((rows, cols), F32)
    return pl.pallas_call(
        body, name=name, grid_spec=_prefetch_spec((2, nh), [full, part, part, full, full], [full] * 4),
        out_shape=[shp] * 4,
        compiler_params=_params(("parallel", "parallel")),
    )(c_idx, w, mine, theirs, m, v)


def _pair_sum(g3, theirs, c_idx, *, name):
    n, rows, cols = g3.shape
    half = rows // 2
    tr = _tile(half, _row_cap(cols), 16)

    def body(c_ref, g_ref, t_ref, o_ref):
        o_ref[...] = (g_ref[...] + t_ref[...]).astype(BF16)

    part = pl.BlockSpec((None, tr, cols), lambda s, i, c_ref: (s, i, 0))
    return pl.pallas_call(
        body, name=name,
        grid_spec=_prefetch_spec((n, half // tr),
                                 [pl.BlockSpec((None, None, tr, cols), lambda s, i, c_ref: (s, c_ref[0], i, 0)), part],
                                 part),
        out_shape=jax.ShapeDtypeStruct((n, half, cols), BF16),
        compiler_params=_params(("parallel", "parallel")),
    )(c_idx, g3.reshape(n, 2, half, cols), theirs)


def _chip_sum(pair, recv, s_idx, *, name):
    _, half, cols = pair.shape
    tr = _tile(half, _row_cap(cols), 16)

    def body(s_ref, p_ref, r_ref, o_ref):
        o_ref[...] = ((p_ref[...].astype(F32) + r_ref[0].astype(F32)) + r_ref[1].astype(F32)) + r_ref[2].astype(F32)

    return pl.pallas_call(
        body, name=name,
        grid_spec=_prefetch_spec((half // tr,),
                                 [pl.BlockSpec((None, tr, cols), lambda i, s_ref: (s_ref[0], i, 0)),
                                  pl.BlockSpec((N_CHIPS - 1, tr, cols), lambda i, s_ref: (0, i, 0))],
                                 pl.BlockSpec((tr, cols), lambda i, s_ref: (i, 0))),
        out_shape=jax.ShapeDtypeStruct((half, cols), F32),
        compiler_params=_params(("parallel",)),
    )(s_idx, pair, recv)


def _sum8(parts):
    n, rows, cols = parts.shape

    def body(p_ref, o_ref):
        acc = p_ref[0]
        for i in range(1, n):
            acc = acc + p_ref[i]
        o_ref[...] = acc

    return pl.pallas_call(
        body, name="small_sum", grid=(1,),
        in_specs=[pl.BlockSpec((n, rows, cols), lambda i: (0, 0, 0))],
        out_specs=pl.BlockSpec((rows, cols), lambda i: (0, 0)),
        out_shape=jax.ShapeDtypeStruct((rows, cols), parts.dtype),
        compiler_params=_params(("arbitrary",)),
    )(parts)


def _place():
    return lax.axis_index("x"), lax.axis_index("y"), lax.axis_index("c")


ANY = pl.BlockSpec(memory_space=pl.ANY)


def _rider_parts(rider):
    if rider is None:
        return (), [], []
    kind, arrays = rider
    n = len(arrays)
    shapes = {"gather": _gathered_shapes, "pair": _pair_shapes, "chip": _chip_shapes}[kind](arrays)
    sems = _gather_sems(n) if kind == "gather" else _exchange_sems(n if kind == "pair" else 3 * n)
    return tuple(arrays), shapes, sems


def _rider_hooks(rider, ins, outs, sems, step, n_steps):
    if rider is None:
        return (lambda: None), (lambda: None)
    if rider[0] == "gather":
        start, forward, finish = _gather_steps(ins, outs, *sems)
    else:
        start, finish = {"pair": _pair_steps, "chip": _chip_steps}[rider[0]](ins, outs, *sems)
        forward = None

    def begin():
        pl.when(step == 0)(start)

    def end():
        if forward is not None:
            pl.when(step == n_steps - 2)(forward)
        pl.when(step == n_steps - 1)(finish)

    return begin, end


def _exchange_alone(rider, *, name):
    extra, shapes, sems = _rider_parts(rider)
    n = len(extra)

    def body(*refs):
        begin, end = _rider_hooks(rider, refs[:n], refs[n:2 * n], refs[-2:], jnp.int32(0), 1)
        begin()
        end()

    return pl.pallas_call(
        body, name=name, out_shape=shapes, in_specs=[ANY] * n, out_specs=[ANY] * n, scratch_shapes=sems,
    )(*extra)


def _gathered_shapes(shards):
    return [jax.ShapeDtypeStruct((N_CHIPS,) + s.shape, s.dtype) for s in shards]


def _gather_sems(n):
    return [pltpu.SemaphoreType.DMA((7 * n,)), pltpu.SemaphoreType.DMA((7 * n,))]


def _gather_steps(ins, outs, send_sems, recv_sems):
    n = len(ins)
    halves = [r.shape[0] // 2 for r in ins]
    x, y, c = _place()
    my_chip = 2 * x + y
    me, sibling = (x, y, c), (x, y, 1 - c)
    chips = [(1 - x, y), (x, 1 - y), (1 - x, 1 - y)]

    def half_of(w, chip, pc):
        return outs[w].at[chip, pl.ds(pc * halves[w], halves[w]), :]

    def copy(w, k, src, dst, to):
        return pltpu.make_async_remote_copy(
            src_ref=src, dst_ref=dst, send_sem=send_sems.at[7 * w + k], recv_sem=recv_sems.at[7 * w + k],
            device_id=to, device_id_type=MESH)

    def firsts():
        cps = []
        for w in range(n):
            cps.append(copy(w, 0, ins[w], outs[w].at[my_chip], sibling))
            mine = ins[w].at[pl.ds(c * halves[w], halves[w]), :]
            for j, (px, py) in enumerate(chips):
                cps.append(copy(w, 1 + j, mine, half_of(w, my_chip, c), (px, py, c)))
        return cps

    def passes():
        return [copy(w, 4 + j, half_of(w, 2 * px + py, c), half_of(w, 2 * px + py, c), sibling)
                for w in range(n) for j, (px, py) in enumerate(chips)]

    def start():
        for cp in firsts():
            cp.start()

    def forward():
        fws = passes()
        for w in range(n):
            for j, (px, py) in enumerate(chips):
                landed = half_of(w, 2 * px + py, c)
                copy(w, 1 + j, landed, landed, me).wait_recv()
                fws[3 * w + j].start()

    def finish():
        for w in range(n):
            copy(w, 0, ins[w], outs[w].at[my_chip], me).wait_recv()
            for j, (px, py) in enumerate(chips):
                landed = half_of(w, 2 * px + py, 1 - c)
                copy(w, 4 + j, landed, landed, me).wait_recv()
        for cp in firsts() + passes():
            cp.wait_send()

    return start, forward, finish


def _pair_shapes(grads):
    return [jax.ShapeDtypeStruct((g.shape[0], g.shape[1] // 2, g.shape[2]), g.dtype) for g in grads]


def _exchange_sems(n):
    return [pltpu.SemaphoreType.DMA((n,)), pltpu.SemaphoreType.DMA((n,))]


def _exchange_steps(copies):
    def start():
        for cp in copies():
            cp.start()

    def finish():
        for cp in copies():
            cp.wait()

    return start, finish


def _pair_steps(ins, outs, send_sems, recv_sems):
    x, y, c = _place()

    def copies():
        return [pltpu.make_async_remote_copy(
            src_ref=ins[w].at[:, pl.ds((1 - c) * (ins[w].shape[1] // 2), ins[w].shape[1] // 2), :], dst_ref=outs[w],
            send_sem=send_sems.at[w], recv_sem=recv_sems.at[w], device_id=(x, y, 1 - c), device_id_type=MESH)
            for w in range(len(ins))]

    return _exchange_steps(copies)


def _chip_shapes(pairs):
    return [jax.ShapeDtypeStruct((N_CHIPS - 1,) + p.shape[1:], p.dtype) for p in pairs]


def _chip_steps(ins, outs, send_sems, recv_sems):
    x, y, c = _place()
    others = [(1 - x, y), (x, 1 - y), (1 - x, 1 - y)]

    def copies():
        return [pltpu.make_async_remote_copy(
            src_ref=ins[w].at[2 * px + py], dst_ref=outs[w].at[j],
            send_sem=send_sems.at[3 * w + j], recv_sem=recv_sems.at[3 * w + j],
            device_id=(px, py, c), device_id_type=MESH)
            for w in range(len(ins)) for j, (px, py) in enumerate(others)]

    return _exchange_steps(copies)


def _swap_halves(mine):
    n = len(mine)

    def body(*refs):
        ins, outs, send_sems, recv_sems = refs[:n], refs[n:2 * n], refs[2 * n], refs[2 * n + 1]
        x, y, c = _place()
        copies = [pltpu.make_async_remote_copy(
            src_ref=ins[w], dst_ref=outs[w], send_sem=send_sems.at[w], recv_sem=recv_sems.at[w],
            device_id=(x, y, 1 - c), device_id_type=MESH) for w in range(n)]
        for cp in copies:
            cp.start()
        for cp in copies:
            cp.wait()

    return pl.pallas_call(
        body, name="grad_swap_halves",
        out_shape=[jax.ShapeDtypeStruct(h.shape, h.dtype) for h in mine],
        in_specs=[ANY] * n, out_specs=[ANY] * n,
        scratch_shapes=[pltpu.SemaphoreType.DMA((n,)), pltpu.SemaphoreType.DMA((n,))],
    )(*mine)


def _gather_small(small):
    srows, cols = small.shape

    def body(s_ref, all_ref, send_sems, recv_sems, local_sem):
        x, y, c = _place()
        me = 4 * x + 2 * y + c
        keep_small = pltpu.make_async_copy(s_ref, all_ref.at[me], local_sem)
        keep_small.start()
        sends = []
        for kk in range(1, 8):
            peer = (x ^ (kk >> 2), y ^ ((kk >> 1) & 1), c ^ (kk & 1))
            sends.append(pltpu.make_async_remote_copy(
                src_ref=s_ref, dst_ref=all_ref.at[me],
                send_sem=send_sems.at[kk], recv_sem=recv_sems.at[kk], device_id=peer, device_id_type=MESH))
        for cp in sends:
            cp.start()
        for kk in range(1, 8):
            px, py, pc = x ^ (kk >> 2), y ^ ((kk >> 1) & 1), c ^ (kk & 1)
            pltpu.make_async_remote_copy(
                src_ref=s_ref, dst_ref=all_ref.at[4 * px + 2 * py + pc],
                send_sem=send_sems.at[kk], recv_sem=recv_sems.at[kk], device_id=(px, py, pc),
                device_id_type=MESH).wait_recv()
        for cp in sends:
            cp.wait_send()
        keep_small.wait()

    return pl.pallas_call(
        body, name="gather_small",
        out_shape=jax.ShapeDtypeStruct((8, srows, cols), small.dtype),
        in_specs=[ANY], out_specs=ANY,
        scratch_shapes=[pltpu.SemaphoreType.DMA((8,)), pltpu.SemaphoreType.DMA((8,)), pltpu.SemaphoreType.DMA],
    )(small)


SHARDED = (("w_in", D_MODEL, IN_COLS, 1), ("w_up_a", ATT_WIDTH, D_MODEL, 1), ("w_up_b", ATT_WIDTH, D_MODEL, 1),
           ("w_out", D_MODEL, D_MODEL, 0), ("w_q_mem", D_MODEL, MEM_WIDTH, 0), ("w_kv_mem", D_MODEL, 2 * MEM_WIDTH, 0),
           ("w_o_mem", MEM_WIDTH, D_MODEL, 1), ("w_ffn_gate", D_FF, D_MODEL, 0), ("w_ffn_up", D_FF, D_MODEL, 0),
           ("w_ffn_down", D_FF, D_MODEL, 0))
TRANSPOSED = ("w_ffn_gate", "w_ffn_up")
NAMES = tuple(n for n, _, _, _ in SHARDED)


def _held(name, shard):
    return shard.T if name in TRANSPOSED else shard
EARLY, LATE = NAMES[:1], NAMES[1:]
GAINS = ("g_mix", "g_mem_q", "g_mem_kv", "g_ffn", "g_final")


def _natural(w3):
    n, r, c = w3.shape
    return w3.reshape(n * r, c)


def _shard_major(g, axis):
    if axis == 1:
        return g
    r, c = g.shape
    return g.reshape(N_CHIPS, r // N_CHIPS, c)


def kernel(x, mem, positions, g_mix, w_in, w_up_a, w_up_b, w_out, g_mem_q, g_mem_kv, w_q_mem, w_kv_mem, w_o_mem, g_ffn, w_ffn_gate, w_ffn_up, w_ffn_down, g_final, loss_target, m_g_mix, m_w_in, m_w_up_a, m_w_up_b, m_w_out, m_g_mem_q, m_g_mem_kv, m_w_q_mem, m_w_kv_mem, m_w_o_mem, m_g_ffn, m_w_ffn_gate, m_w_ffn_up, m_w_ffn_down, m_g_final, v_g_mix, v_w_in, v_w_up_a, v_w_up_b, v_w_out, v_g_mem_q, v_g_mem_kv, v_w_q_mem, v_w_kv_mem, v_w_o_mem, v_g_ffn, v_w_ffn_gate, v_w_ffn_up, v_w_ffn_down, v_g_final):
    given = dict(locals())
    shards = {n: _held(n, given[n][0]) for n in NAMES}

    early_shards = [shards[n].astype(BF16) for n in EARLY]
    late_shards = [shards[n].astype(BF16) for n in LATE]
    c_idx = lax.axis_index("c").astype(jnp.int32).reshape(1)
    s_idx = (2 * lax.axis_index("x") + lax.axis_index("y")).astype(jnp.int32).reshape(1)

    loss_row, grad_x, mine, gain_grads = _local_step(x, mem, positions, loss_target, g_mix, g_mem_q, g_mem_kv,
                                                     g_ffn, g_final, {}, early_shards, late_shards, (c_idx, s_idx))
    return _reduce_and_update(given, shards, loss_row, grad_x, mine, gain_grads, c_idx)


def _reduce_halves(glist, names, c_idx, s_idx, pair_exchange, chip_exchange):
    theirs = pair_exchange(glist)
    pairs = [_pair_sum(g, t, c_idx, name="pair_sum_" + n) for n, g, t in zip(names, glist, theirs)]
    recv = chip_exchange(pairs)
    return [_chip_sum(p, r, s_idx, name="chip_sum_" + n) for n, p, r in zip(names, pairs, recv)]


def _local_step(x, mem, positions, loss_target, g_mix, g_mem_q, g_mem_kv, g_ffn, g_final, wf,
                early_shards=None, late_shards=None, place=None):
    b_dim, s_dim, d = x.shape
    t_dim = b_dim * s_dim
    n_mem = mem.shape[1]
    wf = dict(wf)

    xb = x.reshape(t_dim, d)
    tgt = loss_target.reshape(t_dim, d)
    memf = mem.reshape(b_dim * n_mem, d)
    gfin = g_final.reshape(1, d)
    pos = positions.reshape(t_dim, 1).astype(F32)

    lane = jnp.arange(LANES) % HEAD_DIM
    half = ROPE_DIM // 2
    inv_freq = ROPE_THETA ** (-jnp.arange(half, dtype=F32) / half)
    inv_lane = jnp.where(lane < ROPE_DIM, inv_freq[lane % half], 0.0).reshape(1, -1).astype(F32)
    sel_a = (lane < half).astype(F32).reshape(1, -1)
    sel_b = ((lane >= half) & (lane < ROPE_DIM)).astype(F32).reshape(1, -1)

    def rows3(t):
        return t.reshape(b_dim, s_dim, t.shape[-1])

    def rows2(t):
        return t.reshape(t_dim, t.shape[-1])

    if early_shards:
        n1, gathered = _rms_fwd(xb, g_mix, name="rms_mix", rider=("gather", early_shards))
        wf.update(zip(EARLY, gathered))
    else:
        n1 = _rms_fwd(xb, g_mix, name="rms_mix")
    proj = _mm_cs(n1, wf["w_in"], name="mm_in")
    proj3 = rows3(proj)
    cs, sn = _rope_table(pos, inv_lane, sel_a, sel_b)
    cs3, sn3 = rows3(cs), rows3(sn)
    (oa16, oa32, lse_a), _ = _dil_fwd(proj3, cs3, sn3, sel_a, sel_b)
    ob16, gathered = _sb_fwd(proj3, ("gather", late_shards) if late_shards else None)
    wf.update(zip(LATE, gathered))
    w_out, w_q, w_kv = _natural(wf["w_out"]), _natural(wf["w_q_mem"]), _natural(wf["w_kv_mem"])
    oa, ob = rows2(oa16), rows2(ob16)
    ua = _mm_sm(oa, wf["w_up_a"], name="mm_up_a")
    ub = _mm_sm(ob, wf["w_up_b"], name="mm_up_b")
    mixed, h1, hn = _gate_out_norm(proj, ua, ub, w_out, xb, g_mem_q)

    memn = _rms_fwd(memf, g_mem_kv, name="rms_mem_kv")
    qm = _mm(hn, w_q, name="mm_q_mem", out_dtype=BF16)
    kvm = _mm(memn, w_kv, name="mm_kv_mem", out_dtype=BF16)
    qm3, kvm3 = rows3(qm), kvm.reshape(b_dim, n_mem, 2 * MEM_WIDTH)
    om = rows2(_mem_fwd(qm3, kvm3))
    h2 = _mm_sm(om, wf["w_o_mem"], name="mm_o_mem", add=h1)

    n3 = _rms_fwd(h2, g_ffn, name="rms_ffn")
    gate3, up3, act3 = _ffn_up_swiglu(n3, wf["w_ffn_gate"], wf["w_ffn_up"])
    loss_row, dh3, dg_final = _down_final(act3, wf["w_ffn_down"], h2, gfin, tgt)

    grads = {}
    grads["w_ffn_down"] = _mm_ffn_down_dw(act3, dh3, name="mm_down_dw")
    dgate3, dup3 = _ffn_down_dx_swiglu(dh3, wf["w_ffn_down"], gate3, up3)
    grads["w_ffn_gate"] = _mm_ffn_down_dw(dgate3, n3, name="mm_gate_dw")
    grads["w_ffn_up"] = _mm_ffn_down_dw(dup3, n3, name="mm_up_dw")
    dn3 = _ffn_up_dx(dgate3, dup3, wf["w_ffn_gate"], wf["w_ffn_up"])
    dh2, dg_ffn = _rms_bwd(h2, g_ffn, dn3, dh3, name="rms_ffn_bwd")

    dom = _mm_sm_dx(dh2, wf["w_o_mem"], name="mm_o_mem_dx", out_dtype=BF16)
    grads["w_o_mem"] = _mm_sm_dw(om, dh2, name="mm_o_mem_dw")
    dqm, dkm, dvm = _mem_bwd(qm3, kvm3, rows3(dom))
    dqm = rows2(dqm)
    dkvm = jnp.concatenate([dkm, dvm], axis=-1).reshape(b_dim * n_mem, 2 * MEM_WIDTH).astype(BF16)
    grads["w_q_mem"] = _shard_major(_mm(hn, dqm, name="mm_q_mem_dw", ta=True), 0)
    dhn = _mm(dqm, w_q, name="mm_q_mem_dx", tb=True)
    grads["w_kv_mem"] = _shard_major(_mm(memn, dkvm, name="mm_kv_mem_dw", ta=True), 0)
    dmemn = _mm(dkvm, w_kv, name="mm_kv_mem_dx", tb=True)
    _, dg_mem_kv = _rms_bwd(memf, g_mem_kv, dmemn, None, name="rms_mem_kv_bwd")
    dh1, dg_mem_q = _rms_bwd(h1, g_mem_q, dhn, dh2, name="rms_mem_q_bwd")

    grads["w_out"] = _shard_major(_mm(mixed, dh1, name="mm_out_dw", ta=True), 0)
    dua, dub, dgates = _out_dx_gate_bwd(dh1, w_out, proj, ua, ub)
    doa = _mm_sm_dx(dua, wf["w_up_a"], name="mm_up_a_dx")
    grads["w_up_a"] = _mm_sm_dw(oa, dua, name="mm_up_a_dw")
    dob = _mm_sm_dx(dub, wf["w_up_b"], name="mm_up_b_dx", out_dtype=BF16)
    grads["w_up_b"] = _mm_sm_dw(ob, dub, name="mm_up_b_dw")

    att = {}

    def dil_with_pairs(glist):
        att["a"], theirs = _dil_bwd(proj3, cs3, sn3, sel_a, sel_b, rows3(doa), oa32, lse_a,
                                    ("pair", glist) if glist else None)
        return theirs

    def sb_with_chips(pairs):
        att["b"], recv = _sb_bwd(proj3, rows3(dob), ("chip", pairs) if pairs else None)
        return recv

    if place is None:
        dil_with_pairs(())
        sb_with_chips(())
    else:
        mine_late = _reduce_halves([grads[n] for n in LATE], LATE, *place, dil_with_pairs, sb_with_chips)
    dproj = jnp.concatenate([rows2(t) for t in att["a"] + att["b"]] + [dgates], axis=1)
    grads["w_in"] = _mm_cs_dw(n1, dproj, name="mm_in_dw")
    if place is None:
        dn1 = _mm_cs_dx(dproj, wf["w_in"], name="mm_in_dx")
        dx, dg_mix = _rms_bwd(xb, g_mix, dn1, dh1, name="rms_mix_bwd")
    else:
        tail = {}

        def pair_alone(glist):
            return _exchange_alone(("pair", glist), name="grad_pair_exchange")

        def dx_with_chips(pairs):
            tail["dn1"], recv = _mm_cs_dx(dproj, wf["w_in"], name="mm_in_dx", rider=("chip", pairs))
            return recv

        mine_early = _reduce_halves([grads[n] for n in EARLY], EARLY, *place, pair_alone, dx_with_chips)
        dx, dg_mix = _rms_bwd(xb, g_mix, tail["dn1"], dh1, name="rms_mix_bwd")
    grad_x = dx.reshape(b_dim, s_dim, d)
    gains = (dg_mix, dg_mem_q, dg_mem_kv, dg_ffn, dg_final)
    if place is None:
        return loss_row, grad_x, grads, gains
    return loss_row, grad_x, mine_early + mine_late, gains


def _reduce_and_update(given, shards, loss_row, grad_x, mine, gain_grads, c_idx):
    d = D_MODEL
    dg_mix, dg_mem_q, dg_mem_kv, dg_ffn, dg_final = gain_grads
    small = jnp.concatenate([dg_mix, dg_mem_q, dg_mem_kv, dg_ffn, dg_final,
                             jnp.pad(loss_row, ((0, 0), (0, FLAT_COLS - LANES))), jnp.zeros((2, FLAT_COLS), F32)], axis=0)
    small_all = _gather_small(small)
    others = _swap_halves(mine)
    small_sum = _sum8(small_all)
    loss = small_sum[5, 0]

    out_g, out_d, out_m, out_v = {}, {}, {}, {}
    for n, mine_n, other_n in zip(NAMES, mine, others):
        res = _adamw_halves(shards[n], mine_n, other_n, _held(n, given["m_" + n][0]), _held(n, given["v_" + n][0]),
                            c_idx, name="adamw_" + n)
        out_g[n], out_d[n], out_m[n], out_v[n] = [_held(n, r)[None] for r in res]
    gain_w = jnp.concatenate([given[n].reshape(1, d) for n in GAINS], axis=0)
    gain_m = jnp.concatenate([given["m_" + n].reshape(1, d) for n in GAINS], axis=0)
    gain_v = jnp.concatenate([given["v_" + n].reshape(1, d) for n in GAINS], axis=0)
    gain_g = small_sum[:len(GAINS)]
    gd, gm, gv = _adamw(gain_w, gain_g, gain_m, gain_v, name="adamw_gains")
    for i, n in enumerate(GAINS):
        shape = given[n].shape
        out_g[n], out_d[n] = gain_g[i].reshape(shape), gd[i].reshape(shape)
        out_m[n], out_v[n] = gm[i].reshape(shape), gv[i].reshape(shape)

    order = ["g_mix", "w_in", "w_up_a", "w_up_b", "w_out", "g_mem_q", "g_mem_kv", "w_q_mem", "w_kv_mem", "w_o_mem",
             "g_ffn", "w_ffn_gate", "w_ffn_up", "w_ffn_down", "g_final"]
    return (loss, grad_x, *[out_g[n] for n in order], *[out_d[n] for n in order],
            *[out_m[n] for n in order], *[out_v[n] for n in order])
```

```python
import jax
import jax.numpy as jnp
from jax import lax
from jax.experimental import pallas as pl
from jax.experimental.pallas import tpu as pltpu

F32 = jnp.float32
BF16 = jnp.bfloat16
MESH = pl.DeviceIdType.MESH

D_MODEL = 1024
HEAD_DIM = 64
N_HEADS = 8
ATT_WIDTH = N_HEADS * HEAD_DIM
DIL_PATTERNS = ((128, 1), (512, 4), (2048, 16))
BLOCK = 128
SB_ROWS = 1024
SB_STEP_FWD, SB_STEP_BWD = 8, 4
ROPE_THETA = 500000.0
ROPE_DIM = HEAD_DIM // 4
N_HEADS_MEM = 4
MEM_HEAD_DIM = 128
MEM_WIDTH = N_HEADS_MEM * MEM_HEAD_DIM
D_FF = 2816
IN_COLS = 6 * ATT_WIDTH + 2 * D_MODEL
RMS_EPS = 1e-6
ADAM_LR = 0.001
ADAM_B1 = 0.9
ADAM_B2 = 0.999
ADAM_EPS = 1e-08
ADAM_WD = 0.01
ADAM_STEP = 10

N_CHIPS = 4
LANES = 128
FLAT_COLS = 1024
VMEM_LIMIT = 56 * 1024 * 1024

PAIRS = ATT_WIDTH // LANES
COL_QA, COL_KA, COL_VA, COL_QB, COL_KB, COL_VB = (i * PAIRS for i in range(6))

MM_CAP = 1408
TOK_CAP = 2048
NN = (((1,), (0,)), ((), ()))
NT = (((1,), (1,)), ((), ()))
TN = (((0,), (0,)), ((), ()))
BNN = (((2,), (1,)), ((0,), (0,)))
BNT = (((2,), (2,)), ((0,), (0,)))
BTN = (((1,), (1,)), ((0,), (0,)))
DIL_BATCH = 16
DIL_CHUNK = 512


def _tile(dim, cap, unit=LANES):
    if dim <= cap:
        return dim
    best = None
    for t in range(unit, cap + 1, unit):
        if dim % t == 0:
            best = t
    assert best is not None, (dim, cap)
    return best


def _row_cap(cols):
    return max(256, (1 << 18) // cols)


def _params(sem):
    return pltpu.CompilerParams(dimension_semantics=sem, vmem_limit_bytes=VMEM_LIMIT)


def _mm(a, b, *, name, ta=False, tb=False, add=None, out_dtype=F32,
        tm_cap=MM_CAP, tn_cap=MM_CAP, tk_cap=MM_CAP):
    if ta:
        k_dim, m_dim = a.shape
    else:
        m_dim, k_dim = a.shape
    if tb:
        n_dim, kb = b.shape
    else:
        kb, n_dim = b.shape
    assert kb == k_dim, (a.shape, b.shape, ta, tb)
    tm, tn, tk = _tile(m_dim, tm_cap), _tile(n_dim, tn_cap), _tile(k_dim, tk_cap)
    nk = k_dim // tk
    dims = (((0 if ta else 1,), (1 if tb else 0,)), ((), ()))
    has_add = add is not None

    def body(*refs):
        if has_add:
            a_ref, b_ref, add_ref, o_ref = refs[:4]
        else:
            a_ref, b_ref, o_ref = refs[:3]
        part = lax.dot_general(a_ref[...].astype(BF16), b_ref[...].astype(BF16), dims, preferred_element_type=F32)

        def finish(r):
            if has_add:
                r = add_ref[...] + r
            o_ref[...] = r.astype(out_dtype)

        if nk == 1:
            finish(part)
            return
        acc_ref = refs[-1]
        k = pl.program_id(2)

        @pl.when(k == 0)
        def _():
            acc_ref[...] = part

        @pl.when(k > 0)
        def _():
            acc_ref[...] += part

        @pl.when(k == nk - 1)
        def _():
            finish(acc_ref[...])

    a_spec = pl.BlockSpec((tk, tm), lambda i, j, k: (k, i)) if ta else pl.BlockSpec((tm, tk), lambda i, j, k: (i, k))
    b_spec = pl.BlockSpec((tn, tk), lambda i, j, k: (j, k)) if tb else pl.BlockSpec((tk, tn), lambda i, j, k: (k, j))
    o_spec = pl.BlockSpec((tm, tn), lambda i, j, k: (i, j))
    in_specs = [a_spec, b_spec] + ([o_spec] if has_add else [])
    args = (a, b) + ((add,) if has_add else ())
    return pl.pallas_call(
        body, name=name, grid=(m_dim // tm, n_dim // tn, nk),
        in_specs=in_specs, out_specs=o_spec,
        out_shape=jax.ShapeDtypeStruct((m_dim, n_dim), out_dtype),
        scratch_shapes=[pltpu.VMEM((tm, tn), F32)] if nk > 1 else [],
        compiler_params=_params(("parallel", "parallel", "arbitrary")),
    )(*args)


def _mm_core(name, a, b, a_spec, b_spec, o_spec, out_shape, grid, dims, *, add=None, out_dtype=F32, rider=None):
    nk = grid[2]
    has_add = add is not None
    n_in = 3 if has_add else 2
    acc_shape = tuple(d for d in o_spec.block_shape if d is not None)
    extra, extra_shapes, extra_sems = _rider_parts(rider)
    n_w = len(extra)

    def body(*refs):
        a_ref, b_ref = refs[:2]
        o_ref = refs[n_in + n_w]
        step = (pl.program_id(0) * grid[1] + pl.program_id(1)) * nk + pl.program_id(2)
        begin, end = _rider_hooks(rider, refs[n_in:n_in + n_w], refs[n_in + n_w + 1:n_in + 2 * n_w + 1], refs[-2:],
                                  step, grid[0] * grid[1] * nk)
        begin()
        part = lax.dot_general(a_ref[...].astype(BF16), b_ref[...].astype(BF16), dims, preferred_element_type=F32)

        def finish(r):
            if has_add:
                r = refs[2][...] + r
            o_ref[...] = r.astype(out_dtype)

        if nk == 1:
            finish(part)
        else:
            acc_ref = refs[n_in + 2 * n_w + 1]
            k = pl.program_id(2)

            @pl.when(k == 0)
            def _():
                acc_ref[...] = part

            @pl.when(k > 0)
            def _():
                acc_ref[...] += part

            @pl.when(k == nk - 1)
            def _():
                finish(acc_ref[...])
        end()

    in_specs = [a_spec, b_spec] + ([o_spec] if has_add else []) + [ANY] * n_w
    args = (a, b) + ((add,) if has_add else ()) + extra
    res = pl.pallas_call(
        body, name=name, grid=grid, in_specs=in_specs, out_specs=[o_spec] + [ANY] * n_w,
        out_shape=[jax.ShapeDtypeStruct(out_shape, out_dtype)] + extra_shapes,
        scratch_shapes=([pltpu.VMEM(acc_shape, F32)] if nk > 1 else []) + extra_sems,
        compiler_params=_params(("arbitrary",) * 3 if n_w else ("parallel", "parallel", "arbitrary")),
    )(*args)
    return (res[0], res[1:]) if n_w else res[0]


def _mm_cs(a, w3, *, name):
    m_dim, k_dim = a.shape
    _, _, n4 = w3.shape
    tm, tn, tk = _tile(m_dim, TOK_CAP), _tile(n4, MM_CAP), _tile(k_dim, MM_CAP)
    npb = n4 // tn
    return _mm_core(name, a, w3,
                    pl.BlockSpec((tm, tk), lambda i, j, k: (i, k)),
                    pl.BlockSpec((None, tk, tn), lambda i, j, k: (j // npb, k, j % npb)),
                    pl.BlockSpec((tm, tn), lambda i, j, k: (i, j)),
                    (m_dim, N_CHIPS * n4), (m_dim // tm, N_CHIPS * npb, k_dim // tk), NN)


def _mm_cs_dx(dy, w3, *, name, out_dtype=F32, rider=None):
    m_dim, _ = dy.shape
    _, k_dim, n4 = w3.shape
    tm, tkw, tn = _tile(m_dim, TOK_CAP), _tile(k_dim, MM_CAP), _tile(n4, MM_CAP)
    npb = n4 // tn
    return _mm_core(name, dy, w3,
                    pl.BlockSpec((tm, tn), lambda i, j, k: (i, k)),
                    pl.BlockSpec((None, tkw, tn), lambda i, j, k: (k // npb, j, k % npb)),
                    pl.BlockSpec((tm, tkw), lambda i, j, k: (i, j)),
                    (m_dim, k_dim), (m_dim // tm, k_dim // tkw, N_CHIPS * npb), NT, out_dtype=out_dtype, rider=rider)


def _mm_cs_dw(a, dy, *, name):
    m_dim, k_dim = a.shape
    n4 = dy.shape[1] // N_CHIPS
    tmk, tn, tk = _tile(k_dim, MM_CAP), _tile(n4, MM_CAP), _tile(m_dim, TOK_CAP)
    npb = n4 // tn
    return _mm_core(name, a, dy,
                    pl.BlockSpec((tk, tmk), lambda i, j, k: (k, i)),
                    pl.BlockSpec((tk, tn), lambda i, j, k: (k, j)),
                    pl.BlockSpec((None, tmk, tn), lambda i, j, k: (j // npb, i, j % npb)),
                    (N_CHIPS, k_dim, n4), (k_dim // tmk, N_CHIPS * npb, m_dim // tk), TN)


def _mm_sm(a, w3, *, name, add=None, out_dtype=F32, tt=1024):
    t_dim, k_dim = a.shape
    n_s, _, n4 = w3.shape
    has_add = add is not None

    def body(*refs):
        a_ref, w_ref, o_ref = refs[0], refs[1], refs[-1]
        av = a_ref[...].astype(BF16)
        for s in range(n_s):
            cols = slice(s * n4, (s + 1) * n4)
            r = jnp.dot(av, w_ref[s], preferred_element_type=F32)
            if has_add:
                r = refs[2][:, cols] + r
            o_ref[:, cols] = r.astype(out_dtype)

    row = pl.BlockSpec((tt, n_s * n4), lambda i: (i, 0))
    return pl.pallas_call(
        body, name=name, grid=(t_dim // tt,),
        in_specs=[pl.BlockSpec((tt, k_dim), lambda i: (i, 0)), pl.BlockSpec(w3.shape, lambda i: (0, 0, 0))]
        + ([row] if has_add else []),
        out_specs=row, out_shape=jax.ShapeDtypeStruct((t_dim, n_s * n4), out_dtype),
        compiler_params=_params(("parallel",)),
    )(*((a, w3) + ((add,) if has_add else ())))


def _mm_sm_dx(dy, w3, *, name, out_dtype=F32, tt=1024):
    t_dim, _ = dy.shape
    n_s, k_dim, n4 = w3.shape

    def body(dy_ref, w_ref, o_ref):
        dyv = dy_ref[...].astype(BF16)
        acc = lax.dot_general(dyv[:, :n4], w_ref[0], NT, preferred_element_type=F32)
        for s in range(1, n_s):
            acc = acc + lax.dot_general(dyv[:, s * n4:(s + 1) * n4], w_ref[s], NT, preferred_element_type=F32)
        o_ref[...] = acc.astype(out_dtype)

    return pl.pallas_call(
        body, name=name, grid=(t_dim // tt,),
        in_specs=[pl.BlockSpec((tt, n_s * n4), lambda i: (i, 0)), pl.BlockSpec(w3.shape, lambda i: (0, 0, 0))],
        out_specs=pl.BlockSpec((tt, k_dim), lambda i: (i, 0)),
        out_shape=jax.ShapeDtypeStruct((t_dim, k_dim), out_dtype),
        compiler_params=_params(("parallel",)),
    )(dy, w3)


def _mm_sm_dw(a, dy, *, name, tk=1024):
    t_dim, k_dim = a.shape
    n4 = dy.shape[1] // N_CHIPS

    def body(a_ref, dy_ref, o_ref):
        av = a_ref[...].astype(BF16)
        dyv = dy_ref[...].astype(BF16)

        @pl.when(pl.program_id(0) == 0)
        def _():
            o_ref[...] = jnp.zeros_like(o_ref)

        for s in range(N_CHIPS):
            o_ref[s] += lax.dot_general(av, dyv[:, s * n4:(s + 1) * n4], TN, preferred_element_type=F32)

    return pl.pallas_call(
        body, name=name, grid=(t_dim // tk,),
        in_specs=[pl.BlockSpec((tk, k_dim), lambda i: (i, 0)), pl.BlockSpec((tk, N_CHIPS * n4), lambda i: (i, 0))],
        out_specs=pl.BlockSpec((N_CHIPS, k_dim, n4), lambda i: (0, 0, 0)),
        out_shape=jax.ShapeDtypeStruct((N_CHIPS, k_dim, n4), F32),
        compiler_params=_params(("arbitrary",)),
    )(a, dy)


def _ffn_up_dx(dg3, du3, wg3, wu3, *, tt=512):
    n_s, t_dim, f4 = dg3.shape
    d = wg3.shape[2]

    def body(dg_ref, du_ref, wg_ref, wu_ref, o_ref):
        acc = None
        for s in range(n_s):
            part = (jnp.dot(dg_ref[s], wg_ref[s], preferred_element_type=F32)
                    + jnp.dot(du_ref[s], wu_ref[s], preferred_element_type=F32))
            acc = part if acc is None else acc + part
        o_ref[...] = acc

    a_spec = pl.BlockSpec((n_s, tt, f4), lambda i: (0, i, 0))
    w_spec = pl.BlockSpec(wg3.shape, lambda i: (0, 0, 0))
    return pl.pallas_call(
        body, name="ffn_up_dx", grid=(t_dim // tt,),
        in_specs=[a_spec, a_spec, w_spec, w_spec], out_specs=pl.BlockSpec((tt, d), lambda i: (i, 0)),
        out_shape=jax.ShapeDtypeStruct((t_dim, d), F32),
        compiler_params=_params(("parallel",)),
    )(dg3, du3, wg3, wu3)


def _mm_ffn_down_dw(act3, dh, *, name, tk=1024):
    n_s, t_dim, f4 = act3.shape
    d = dh.shape[1]

    def body(a_ref, b_ref, o_ref):
        bv = b_ref[...].astype(BF16)

        @pl.when(pl.program_id(0) == 0)
        def _():
            o_ref[...] = jnp.zeros_like(o_ref)

        for s in range(n_s):
            o_ref[s] += lax.dot_general(a_ref[s], bv, TN, preferred_element_type=F32)

    return pl.pallas_call(
        body, name=name, grid=(t_dim // tk,),
        in_specs=[pl.BlockSpec((n_s, tk, f4), lambda i: (0, i, 0)), pl.BlockSpec((tk, d), lambda i: (i, 0))],
        out_specs=pl.BlockSpec((n_s, f4, d), lambda i: (0, 0, 0)),
        out_shape=jax.ShapeDtypeStruct((n_s, f4, d), F32),
        compiler_params=_params(("arbitrary",)),
    )(act3, dh)


def _rms_fwd(x, g, *, name, tt=512, rider=None):
    t_dim, d = x.shape
    tt = _tile(t_dim, tt, 8)
    extra, extra_shapes, extra_sems = _rider_parts(rider)
    n_w = len(extra)

    def body(*refs):
        x_ref, g_ref, o_ref = refs[0], refs[1], refs[2 + n_w]
        begin, end = _rider_hooks(rider, refs[2:2 + n_w], refs[3 + n_w:3 + 2 * n_w], refs[-2:], pl.program_id(0),
                                  t_dim // tt)
        begin()
        xv = x_ref[...]
        r = lax.rsqrt(jnp.mean(xv * xv, axis=-1, keepdims=True) + RMS_EPS)
        o_ref[...] = ((xv * r) * g_ref[...]).astype(o_ref.dtype)
        end()

    res = pl.pallas_call(
        body, name=name, grid=(t_dim // tt,),
        in_specs=[pl.BlockSpec((tt, d), lambda i: (i, 0)), pl.BlockSpec((1, d), lambda i: (0, 0))] + [ANY] * n_w,
        out_specs=[pl.BlockSpec((tt, d), lambda i: (i, 0))] + [ANY] * n_w,
        out_shape=[jax.ShapeDtypeStruct((t_dim, d), BF16)] + extra_shapes,
        scratch_shapes=extra_sems,
        compiler_params=_params(("arbitrary",) if n_w else ("parallel",)),
    )(x, g, *extra)
    return (res[0], res[1:]) if n_w else res[0]


def _rms_bwd(x, g, dy, add, *, name, tt=512):
    t_dim, d = x.shape
    tt = _tile(t_dim, tt, 8)
    has_add = add is not None

    def body(*refs):
        x_ref, g_ref, dy_ref = refs[:3]
        add_ref = refs[3] if has_add else None
        dx_ref, dg_ref = refs[-2:]
        xv = x_ref[...]
        dyv = dy_ref[...].astype(F32)
        r = lax.rsqrt(jnp.mean(xv * xv, axis=-1, keepdims=True) + RMS_EPS)
        xh = xv * r
        u = dyv * g_ref[...]
        dx = r * (u - xh * jnp.mean(u * xh, axis=-1, keepdims=True))
        if has_add:
            dx = add_ref[...] + dx
        dx_ref[...] = dx

        @pl.when(pl.program_id(0) == 0)
        def _():
            dg_ref[...] = jnp.zeros_like(dg_ref)

        dg_ref[...] += jnp.sum(dyv * xh, axis=0, keepdims=True)

    row = pl.BlockSpec((tt, d), lambda i: (i, 0))
    vec = pl.BlockSpec((1, d), lambda i: (0, 0))
    in_specs = [row, vec, row] + ([row] if has_add else [])
    args = (x, g, dy) + ((add,) if has_add else ())
    return pl.pallas_call(
        body, name=name, grid=(t_dim // tt,),
        in_specs=in_specs, out_specs=[row, vec],
        out_shape=[jax.ShapeDtypeStruct((t_dim, d), F32), jax.ShapeDtypeStruct((1, d), F32)],
        compiler_params=_params(("arbitrary",)),
    )(*args)


def _down_final(act3, wd3, h, g, target, *, tt=512):
    n_s, t_dim, f4 = act3.shape
    d = h.shape[1]
    n_steps = t_dim // tt

    def body(a_ref, w_ref, h_ref, g_ref, t_ref, loss_ref, dh_ref, dg_ref, sq_ref):
        i = pl.program_id(0)
        xv = h_ref[...]
        for s in range(n_s):
            xv = xv + jnp.dot(a_ref[s], w_ref[s], preferred_element_type=F32)
        gv = g_ref[...]
        r = lax.rsqrt(jnp.mean(xv * xv, axis=-1, keepdims=True) + RMS_EPS)
        xh = xv * r
        err = xh * gv - t_ref[...]
        dyv = err * (1.0 / d)
        u = dyv * gv
        dh_ref[...] = r * (u - xh * jnp.mean(u * xh, axis=-1, keepdims=True))

        @pl.when(i == 0)
        def _():
            dg_ref[...] = jnp.zeros_like(dg_ref)
            sq_ref[...] = jnp.zeros_like(sq_ref)

        dg_ref[...] += jnp.sum(dyv * xh, axis=0, keepdims=True)
        sq_ref[...] += jnp.sum(err * err, axis=0, keepdims=True)

        @pl.when(i == n_steps - 1)
        def _():
            total = jnp.sum(sq_ref[...], axis=-1, keepdims=True) * (0.5 / d)
            loss_ref[...] = jnp.broadcast_to(total, loss_ref.shape)

    row = pl.BlockSpec((tt, d), lambda i: (i, 0))
    vec = pl.BlockSpec((1, d), lambda i: (0, 0))
    return pl.pallas_call(
        body, name="down_final_loss", grid=(n_steps,),
        in_specs=[pl.BlockSpec((n_s, tt, f4), lambda i: (0, i, 0)), pl.BlockSpec(wd3.shape, lambda i: (0, 0, 0)),
                  row, vec, row],
        out_specs=[pl.BlockSpec((1, LANES), lambda i: (0, 0)), row, vec],
        out_shape=[jax.ShapeDtypeStruct((1, LANES), F32), jax.ShapeDtypeStruct((t_dim, d), F32),
                   jax.ShapeDtypeStruct((1, d), F32)],
        scratch_shapes=[pltpu.VMEM((1, d), F32)],
        compiler_params=_params(("arbitrary",)),
    )(act3, wd3, h, g, target)


def _rope_table(pos, inv_lane, sel_a, sel_b, *, tt=512):
    t_dim = pos.shape[0]

    def body(p_ref, f_ref, a_ref, b_ref, c_ref, s_ref):
        ang = p_ref[...] * f_ref[...]
        on = (a_ref[...] + b_ref[...]) > 0.0
        c_ref[...] = jnp.where(on, jnp.cos(ang), 1.0)
        s_ref[...] = jnp.where(on, jnp.sin(ang), 0.0)

    vec = pl.BlockSpec((1, LANES), lambda i: (0, 0))
    row = pl.BlockSpec((tt, LANES), lambda i: (i, 0))
    shp = jax.ShapeDtypeStruct((t_dim, LANES), F32)
    return pl.pallas_call(
        body, name="rope_table", grid=(t_dim // tt,),
        in_specs=[pl.BlockSpec((tt, 1), lambda i: (i, 0)), vec, vec, vec],
        out_specs=[row, row], out_shape=[shp, shp],
        compiler_params=_params(("parallel",)),
    )(pos, inv_lane, sel_a, sel_b)


def _rotate(xv, cs, sn, sa, sb):
    half = ROPE_DIM // 2
    up = pltpu.roll(xv, LANES - half, 1)
    dn = pltpu.roll(xv, half, 1)
    return xv * cs + (dn * sb - up * sa) * sn


def _head_masks():
    h1 = lax.broadcasted_iota(jnp.int32, (1, LANES), 1) < HEAD_DIM
    return h1, jnp.logical_not(h1)


def _split_heads(xv, h1, h2):
    return jnp.where(h1, xv, 0.0).astype(BF16), jnp.where(h2, xv, 0.0).astype(BF16)


def _tri_masks():
    r = lax.broadcasted_iota(jnp.int32, (BLOCK, BLOCK), 0)
    c = lax.broadcasted_iota(jnp.int32, (BLOCK, BLOCK), 1)
    return c <= r, r <= c


def _stream_rows(start, dil):
    if dil == 1:
        return pl.ds(pl.multiple_of(start, BLOCK), BLOCK)
    return pl.ds(start, BLOCK, stride=dil)


def _dil_tile(idx, dil, nb):
    r = idx // nb
    n = idx % nb
    return (_stream_rows(r + dil * BLOCK * n, dil), _stream_rows(r + dil * BLOCK * jnp.maximum(n - 1, 0), dil),
            n > 0)


def _dil_specs(b_dim, s_dim):
    def col(c0):
        return pl.BlockSpec((None, s_dim, LANES),lambda b, h: (b, 0, c0 + h))
    tab = pl.BlockSpec((None, s_dim, LANES),lambda b, h: (b, 0, 0))
    vec = pl.BlockSpec((1, LANES), lambda b, h: (0, 0))
    return col, tab, vec


def _dil_fwd(proj3, cs3, sn3, sel_a, sel_b, rider=None):
    b_dim, s_dim, _ = proj3.shape
    scale = HEAD_DIM ** -0.5
    n_pat = len(DIL_PATTERNS)
    extra, extra_shapes, extra_sems = _rider_parts(rider)
    n_w = len(extra)
    n_steps = b_dim * PAIRS

    def body(*refs):
        q_ref, k_ref, v_ref, cs_ref, sn_ref, sa_ref, sb_ref = refs[:7]
        o16_ref, o32_ref, l_ref = refs[7 + n_w:10 + n_w]
        qr, kr = refs[10 + 2 * n_w:12 + 2 * n_w]
        per_pattern = refs[12 + 2 * n_w:12 + 2 * n_w + 2 * n_pat]
        og, lg = per_pattern[:n_pat], per_pattern[n_pat:]
        step = pl.program_id(0) * PAIRS + pl.program_id(1)
        begin, end = _rider_hooks(rider, refs[7:7 + n_w], refs[10 + n_w:10 + 2 * n_w], refs[-2:], step, n_steps)
        begin()
        h1, h2 = _head_masks()
        cur_ok, prev_ok = _tri_masks()
        sa, sb = sa_ref[...], sb_ref[...]

        def prep(j, _):
            rows = pl.ds(pl.multiple_of(j * DIL_CHUNK, DIL_CHUNK), DIL_CHUNK)
            cs, sn = cs_ref[rows, :], sn_ref[rows, :]
            qr[rows, :] = _rotate(q_ref[rows, :], cs, sn, sa, sb) * scale
            kr[rows, :] = _rotate(k_ref[rows, :], cs, sn, sa, sb)
            return 0

        lax.fori_loop(0, s_dim // DIL_CHUNK, prep, 0)

        for g, (_, dil) in enumerate(DIL_PATTERNS):
            nb = s_dim // dil // BLOCK

            def some(bi, _, g=g, dil=dil, nb=nb):
                tiles = [_dil_tile(bi * DIL_BATCH + t, dil, nb) for t in range(DIL_BATCH)]
                rows = [t[0] for t in tiles]
                q1, q2 = _split_heads(jnp.stack([qr[rw, :] for rw in rows]), h1, h2)
                kc = jnp.stack([kr[rw, :] for rw in rows]).astype(BF16)
                vc1, vc2 = _split_heads(jnp.stack([v_ref[rw, :] for rw in rows]), h1, h2)
                if nb > 1:
                    kp = jnp.stack([kr[t[1], :] for t in tiles]).astype(BF16)
                    vp1, vp2 = _split_heads(jnp.stack([v_ref[t[1], :] for t in tiles]), h1, h2)
                    p_ok = jnp.stack([jnp.logical_and(prev_ok, t[2]) for t in tiles])

                def head(qh, vch, vph):
                    sc = jnp.where(cur_ok, lax.dot_general(qh, kc, BNT, preferred_element_type=F32), -jnp.inf)
                    m = jnp.max(sc, axis=-1, keepdims=True)
                    if nb > 1:
                        sp = jnp.where(p_ok, lax.dot_general(qh, kp, BNT, preferred_element_type=F32), -jnp.inf)
                        m = jnp.maximum(m, jnp.max(sp, axis=-1, keepdims=True))
                    pc = jnp.exp(sc - m)
                    den = jnp.sum(pc, axis=-1, keepdims=True)
                    acc = lax.dot_general(pc.astype(BF16), vch, BNN, preferred_element_type=F32)
                    if nb > 1:
                        pp = jnp.exp(sp - m)
                        den = den + jnp.sum(pp, axis=-1, keepdims=True)
                        acc = acc + lax.dot_general(pp.astype(BF16), vph, BNN, preferred_element_type=F32)
                    return acc / den, m + jnp.log(den)

                o1, l1 = head(q1, vc1, vp1 if nb > 1 else None)
                o2, l2 = head(q2, vc2, vp2 if nb > 1 else None)
                o, l = o1 + o2, jnp.where(h1, l1, l2)
                for t, rw in enumerate(rows):
                    og[g][rw, :] = o[t]
                    lg[g][rw, :] = l[t]
                return 0

            lax.fori_loop(0, dil * nb // DIL_BATCH, some, 0)

        def comb(j, _):
            rows = pl.ds(pl.multiple_of(j * DIL_CHUNK, DIL_CHUNK), DIL_CHUNK)
            ls = [lg[g][rows, :] for g in range(n_pat)]
            m = jnp.maximum(jnp.maximum(ls[0], ls[1]), ls[2])
            es = [jnp.exp(l - m) for l in ls]
            den = es[0] + es[1] + es[2]
            o = (es[0] * og[0][rows, :] + es[1] * og[1][rows, :] + es[2] * og[2][rows, :]) / den
            o16_ref[rows, :] = o.astype(BF16)
            o32_ref[rows, :] = o
            l_ref[rows, :] = m + jnp.log(den)
            return 0

        lax.fori_loop(0, s_dim // DIL_CHUNK, comb, 0)
        end()

    col, tab, vec = _dil_specs(b_dim, s_dim)
    out = pl.BlockSpec((None, s_dim, LANES),lambda b, h: (b, 0, h))
    shp = (b_dim, s_dim, ATT_WIDTH)
    res = pl.pallas_call(
        body, name="dil_fwd", grid=(b_dim, PAIRS),
        in_specs=[col(COL_QA), col(COL_KA), col(COL_VA), tab, tab, vec, vec] + [ANY] * n_w,
        out_specs=[out, out, out] + [ANY] * n_w,
        out_shape=[jax.ShapeDtypeStruct(shp, BF16), jax.ShapeDtypeStruct(shp, F32), jax.ShapeDtypeStruct(shp, F32)]
        + extra_shapes,
        scratch_shapes=[pltpu.VMEM((s_dim, LANES), F32)] * (2 + 2 * n_pat) + extra_sems,
        compiler_params=_params(("arbitrary", "arbitrary")),
    )(proj3, proj3, proj3, cs3, sn3, sel_a, sel_b, *extra)
    return res[:3], res[3:]


def _dil_bwd(proj3, cs3, sn3, sel_a, sel_b, do3, o3, lse3, rider=None):
    b_dim, s_dim, _ = proj3.shape
    scale = HEAD_DIM ** -0.5
    extra, extra_shapes, extra_sems = _rider_parts(rider)
    n_w = len(extra)

    def body(*refs):
        q_ref, k_ref, v_ref, cs_ref, sn_ref, sa_ref, sb_ref, do_ref, o_ref, l_ref = refs[:10]
        dq_ref, dk_ref, dv_ref = refs[10 + n_w:13 + n_w]
        qr, kr, dqa, dka, dva = refs[13 + 2 * n_w:18 + 2 * n_w]
        step = pl.program_id(0) * PAIRS + pl.program_id(1)
        begin, end = _rider_hooks(rider, refs[10:10 + n_w], refs[13 + n_w:13 + 2 * n_w], refs[-2:], step,
                                  b_dim * PAIRS)
        begin()
        h1, h2 = _head_masks()
        cur_ok, prev_ok = _tri_masks()
        sa, sb = sa_ref[...], sb_ref[...]

        def prep(j, _):
            rows = pl.ds(pl.multiple_of(j * DIL_CHUNK, DIL_CHUNK), DIL_CHUNK)
            cs, sn = cs_ref[rows, :], sn_ref[rows, :]
            qr[rows, :] = _rotate(q_ref[rows, :], cs, sn, sa, sb) * scale
            kr[rows, :] = _rotate(k_ref[rows, :], cs, sn, sa, sb)
            zero = jnp.zeros((DIL_CHUNK, LANES), F32)
            dqa[rows, :] = zero
            dka[rows, :] = zero
            dva[rows, :] = zero
            return 0

        lax.fori_loop(0, s_dim // DIL_CHUNK, prep, 0)

        for _, dil in DIL_PATTERNS:
            nb = s_dim // dil // BLOCK

            def some(bi, _, dil=dil, nb=nb):
                tiles = [_dil_tile(bi * DIL_BATCH + t, dil, nb) for t in range(DIL_BATCH)]
                rows = [t[0] for t in tiles]
                q1, q2 = _split_heads(jnp.stack([qr[rw, :] for rw in rows]), h1, h2)
                dof = jnp.stack([do_ref[rw, :] for rw in rows])
                do1, do2 = _split_heads(dof, h1, h2)
                prod = dof * jnp.stack([o_ref[rw, :] for rw in rows])
                delta1 = jnp.sum(jnp.where(h1, prod, 0.0), axis=-1, keepdims=True)
                delta2 = jnp.sum(jnp.where(h2, prod, 0.0), axis=-1, keepdims=True)
                lt = jnp.stack([l_ref[rw, :] for rw in rows])
                lse1 = jnp.max(jnp.where(h1, lt, -jnp.inf), axis=-1, keepdims=True)
                lse2 = jnp.max(jnp.where(h2, lt, -jnp.inf), axis=-1, keepdims=True)

                def side(krows, ok):
                    kf = jnp.stack([kr[kw, :] for kw in krows])
                    k16 = kf.astype(BF16)
                    k1, k2 = _split_heads(kf, h1, h2)
                    v16 = jnp.stack([v_ref[kw, :] for kw in krows]).astype(BF16)

                    def head(qh, doh, lse, delta):
                        sc = lax.dot_general(qh, k16, BNT, preferred_element_type=F32)
                        p = jnp.where(ok, jnp.exp(sc - lse), 0.0)
                        dp = lax.dot_general(doh, v16, BNT, preferred_element_type=F32)
                        return p.astype(BF16), (p * (dp - delta)).astype(BF16)

                    p1, ds1 = head(q1, do1, lse1, delta1)
                    p2, ds2 = head(q2, do2, lse2, delta2)
                    dv = (lax.dot_general(p1, do1, BTN, preferred_element_type=F32)
                          + lax.dot_general(p2, do2, BTN, preferred_element_type=F32))
                    dk = (lax.dot_general(ds1, q1, BTN, preferred_element_type=F32)
                          + lax.dot_general(ds2, q2, BTN, preferred_element_type=F32))
                    for t, kw in enumerate(krows):
                        dva[kw, :] += dv[t]
                        dka[kw, :] += dk[t]
                    return (lax.dot_general(ds1, k1, BNN, preferred_element_type=F32)
                            + lax.dot_general(ds2, k2, BNN, preferred_element_type=F32))

                dq = side(rows, cur_ok)
                if nb > 1:
                    dq = dq + side([t[1] for t in tiles], jnp.stack([jnp.logical_and(prev_ok, t[2]) for t in tiles]))
                for t, rw in enumerate(rows):
                    dqa[rw, :] += dq[t] * scale
                return 0

            lax.fori_loop(0, dil * nb // DIL_BATCH, some, 0)

        def finish(j, _):
            rows = pl.ds(pl.multiple_of(j * DIL_CHUNK, DIL_CHUNK), DIL_CHUNK)
            cs, sn = cs_ref[rows, :], -sn_ref[rows, :]
            dq_ref[rows, :] = _rotate(dqa[rows, :], cs, sn, sa, sb).astype(BF16)
            dk_ref[rows, :] = _rotate(dka[rows, :], cs, sn, sa, sb).astype(BF16)
            dv_ref[rows, :] = dva[rows, :].astype(BF16)
            return 0

        lax.fori_loop(0, s_dim // DIL_CHUNK, finish, 0)
        end()

    col, tab, vec = _dil_specs(b_dim, s_dim)
    out = pl.BlockSpec((None, s_dim, LANES),lambda b, h: (b, 0, h))
    shp = jax.ShapeDtypeStruct((b_dim, s_dim, ATT_WIDTH), BF16)
    acc = pltpu.VMEM((s_dim, LANES), F32)
    res = pl.pallas_call(
        body, name="dil_bwd", grid=(b_dim, PAIRS),
        in_specs=[col(COL_QA), col(COL_KA), col(COL_VA), tab, tab, vec, vec, out, out, out] + [ANY] * n_w,
        out_specs=[out, out, out] + [ANY] * n_w, out_shape=[shp, shp, shp] + extra_shapes,
        scratch_shapes=[acc, acc, acc, acc, acc] + extra_sems,
        compiler_params=_params(("arbitrary", "arbitrary")),
    )(proj3, proj3, proj3, cs3, sn3, sel_a, sel_b, do3, o3, lse3, *extra)
    return res[:3], res[3:]


def _split_dot(x, tri):
    hi = x.astype(BF16)
    lo = (x - hi.astype(F32)).astype(BF16)
    return jnp.dot(hi, tri, preferred_element_type=F32) + jnp.dot(lo, tri, preferred_element_type=F32)


def _log_sigmoid(z):
    return jnp.minimum(z, 0.0) - jnp.log(1.0 + jnp.exp(-jnp.abs(z)))


def _sb_scores(qh, k16, valid):
    z = lax.dot_general(qh, k16, NT, preferred_element_type=F32)
    ls = _log_sigmoid(z)
    l1m = ls - z
    return ls, (l1m if valid is None else jnp.where(valid, l1m, 0.0))


def _sb_consts():
    r = lax.broadcasted_iota(jnp.int32, (BLOCK, BLOCK), 0)
    c = lax.broadcasted_iota(jnp.int32, (BLOCK, BLOCK), 1)
    after = (r > c).astype(BF16)
    before = (r < c).astype(BF16)
    qrow = lax.broadcasted_iota(jnp.int32, (SB_ROWS, BLOCK), 0)
    kcol = lax.broadcasted_iota(jnp.int32, (SB_ROWS, BLOCK), 1)
    return after, before, qrow, kcol


def _below(whole, lo, delta):
    if lo == 0:
        return whole + delta
    return whole + jnp.concatenate([jnp.zeros((lo,) + delta.shape[1:], delta.dtype), delta], axis=0)


def _pairs_loop(n_blocks, step, carry, per_iter):
    def several(i, c):
        for j in range(per_iter):
            c = step(per_iter * i + j, c)
        return c

    return lax.fori_loop(0, n_blocks // per_iter, several, carry)


def _sb_fwd(proj3, rider=None):
    b_dim, s_dim, _ = proj3.shape
    scale = HEAD_DIM ** -0.5
    per = SB_ROWS // BLOCK
    extra, extra_shapes, extra_sems = _rider_parts(rider)
    n_w = len(extra)

    def body(*refs):
        q_ref, k_ref, v_ref = refs[:3]
        o_ref = refs[3 + n_w]
        step = pl.program_id(0) * PAIRS + pl.program_id(1)
        begin, end = _rider_hooks(rider, refs[3:3 + n_w], refs[4 + n_w:4 + 2 * n_w], refs[-2:], step, b_dim * PAIRS)
        begin()
        h1, h2 = _head_masks()
        after, _, qrow, kcol = _sb_consts()

        def qloop(qi, _):
            rows = pl.ds(pl.multiple_of(qi * SB_ROWS, SB_ROWS), SB_ROWS)
            q1, q2 = _split_heads(q_ref[rows, :] * scale, h1, h2)
            first = qi * per

            def block(kb, carry, lo):
                acc, run1, run2 = carry
                krows = pl.ds(pl.multiple_of(kb * BLOCK, BLOCK), BLOCK)
                k16 = k_ref[krows, :].astype(BF16)
                v1, v2 = _split_heads(v_ref[krows, :], h1, h2)
                valid = None if lo is None else kcol[:SB_ROWS - lo] < qrow[:SB_ROWS - lo]
                lo = lo or 0

                def head(qh, vh, run):
                    ls, l1m = _sb_scores(qh[lo:], k16, valid)
                    a = jnp.exp(ls + _split_dot(l1m, after) + run[lo:])
                    if valid is not None:
                        a = jnp.where(valid, a, 0.0)
                    return (jnp.dot(a.astype(BF16), vh, preferred_element_type=F32),
                            _below(run, lo, jnp.sum(l1m, axis=-1, keepdims=True)))

                o1, run1 = head(q1, v1, run1)
                o2, run2 = head(q2, v2, run2)
                return _below(acc, lo, o1 + o2), run1, run2

            zcol = jnp.zeros((SB_ROWS, 1), F32)
            carry = (jnp.zeros((SB_ROWS, LANES), F32), zcol, zcol)
            for kl in reversed(range(per)):
                carry = block(first + kl, carry, kl * BLOCK)
            acc, _, _ = _pairs_loop(first, lambda i, c: block(first - 1 - i, c, None), carry, SB_STEP_FWD)
            o_ref[rows, :] = acc.astype(BF16)
            return 0

        lax.fori_loop(0, s_dim // SB_ROWS, qloop, 0)
        end()

    def col(c0):
        return pl.BlockSpec((None, s_dim, LANES),lambda b, h: (b, 0, c0 + h))

    res = pl.pallas_call(
        body, name="sb_fwd", grid=(b_dim, PAIRS),
        in_specs=[col(COL_QB), col(COL_KB), col(COL_VB)] + [ANY] * n_w, out_specs=[col(0)] + [ANY] * n_w,
        out_shape=[jax.ShapeDtypeStruct((b_dim, s_dim, ATT_WIDTH), BF16)] + extra_shapes,
        scratch_shapes=extra_sems,
        compiler_params=_params(("arbitrary", "arbitrary")),
    )(proj3, proj3, proj3, *extra)
    return res[0], res[1:]


def _sb_bwd(proj3, do3, rider=None):
    b_dim, s_dim, _ = proj3.shape
    scale = HEAD_DIM ** -0.5
    per = SB_ROWS // BLOCK
    nkb_max = s_dim // BLOCK
    extra, extra_shapes, extra_sems = _rider_parts(rider)
    n_w = len(extra)

    def body(*refs):
        q_ref, k_ref, v_ref, do_ref = refs[:4]
        dq_ref, dk_ref, dv_ref = refs[4 + n_w:7 + n_w]
        dka, dva, e_ref, sg_ref = refs[7 + 2 * n_w:11 + 2 * n_w]
        step = pl.program_id(0) * PAIRS + pl.program_id(1)
        begin, end = _rider_hooks(rider, refs[4:4 + n_w], refs[7 + n_w:7 + 2 * n_w], refs[-2:], step, b_dim * PAIRS)
        begin()
        h1, h2 = _head_masks()
        after, before, qrow, kcol = _sb_consts()
        dka[...] = jnp.zeros_like(dka)
        dva[...] = jnp.zeros_like(dva)

        def qloop(qi, _):
            rows = pl.ds(pl.multiple_of(qi * SB_ROWS, SB_ROWS), SB_ROWS)
            q1, q2 = _split_heads(q_ref[rows, :] * scale, h1, h2)
            do1, do2 = _split_heads(do_ref[rows, :].astype(F32), h1, h2)
            first = qi * per

            def pass1(kb, carry, lo):
                run1, run2 = carry
                krows = pl.ds(pl.multiple_of(kb * BLOCK, BLOCK), BLOCK)
                k16 = k_ref[krows, :].astype(BF16)
                v16 = v_ref[krows, :].astype(BF16)
                valid = None if lo is None else kcol[:SB_ROWS - lo] < qrow[:SB_ROWS - lo]
                lo = lo or 0
                part = pl.ds(lo, SB_ROWS - lo)

                def head(h, qh, doh, run):
                    ls, l1m = _sb_scores(qh[lo:], k16, valid)
                    a = jnp.exp(ls + _split_dot(l1m, after) + run[lo:])
                    if valid is not None:
                        a = jnp.where(valid, a, 0.0)
                    da = lax.dot_general(doh[lo:], v16, NT, preferred_element_type=F32)
                    e_ref[h, kb, part, :] = a * da
                    sg_ref[h, kb, part, :] = jnp.exp(ls)
                    return a.astype(BF16), _below(run, lo, jnp.sum(l1m, axis=-1, keepdims=True))

                a1, run1 = head(0, q1, do1, run1)
                a2, run2 = head(1, q2, do2, run2)
                dva[krows, :] += (lax.dot_general(a1, do1[lo:], TN, preferred_element_type=F32)
                                  + lax.dot_general(a2, do2[lo:], TN, preferred_element_type=F32))
                return run1, run2

            zcol = jnp.zeros((SB_ROWS, 1), F32)
            carry = (zcol, zcol)
            for kl in reversed(range(per)):
                carry = pass1(first + kl, carry, kl * BLOCK)
            _pairs_loop(first, lambda i, c: pass1(first - 1 - i, c, None), carry, SB_STEP_BWD)

            def pass2(kb, carry, lo):
                dq, pre1, pre2 = carry
                krows = pl.ds(pl.multiple_of(kb * BLOCK, BLOCK), BLOCK)
                k1, k2 = _split_heads(k_ref[krows, :], h1, h2)
                valid = None if lo is None else kcol[:SB_ROWS - lo] < qrow[:SB_ROWS - lo]
                lo = lo or 0
                part = pl.ds(lo, SB_ROWS - lo)

                def head(h, pre):
                    ev = e_ref[h, kb, part, :]
                    sg = sg_ref[h, kb, part, :]
                    dz = ev * (1.0 - sg) - (_split_dot(ev, before) + pre[lo:]) * sg
                    if valid is not None:
                        dz = jnp.where(valid, dz, 0.0)
                    return dz.astype(BF16), _below(pre, lo, jnp.sum(ev, axis=-1, keepdims=True))

                dz1, pre1 = head(0, pre1)
                dz2, pre2 = head(1, pre2)
                dka[krows, :] += (lax.dot_general(dz1, q1[lo:], TN, preferred_element_type=F32)
                                  + lax.dot_general(dz2, q2[lo:], TN, preferred_element_type=F32))
                dq = _below(dq, lo, jnp.dot(dz1, k1, preferred_element_type=F32)
                            + jnp.dot(dz2, k2, preferred_element_type=F32))
                return dq, pre1, pre2

            carry = _pairs_loop(first, lambda i, c: pass2(i, c, None), (jnp.zeros((SB_ROWS, LANES), F32), zcol, zcol),
                                SB_STEP_BWD)
            for kl in range(per):
                carry = pass2(first + kl, carry, kl * BLOCK)
            dq = carry[0]
            dq_ref[rows, :] = (dq * scale).astype(BF16)
            return 0

        lax.fori_loop(0, s_dim // SB_ROWS, qloop, 0)
        dk_ref[...] = dka[...].astype(BF16)
        dv_ref[...] = dva[...].astype(BF16)
        end()

    def col(c0):
        return pl.BlockSpec((None, s_dim, LANES),lambda b, h: (b, 0, c0 + h))

    shp = jax.ShapeDtypeStruct((b_dim, s_dim, ATT_WIDTH), BF16)
    acc = pltpu.VMEM((s_dim, LANES), F32)
    strip = pltpu.VMEM((2, nkb_max, SB_ROWS, BLOCK), F32)
    res = pl.pallas_call(
        body, name="sb_bwd", grid=(b_dim, PAIRS),
        in_specs=[col(COL_QB), col(COL_KB), col(COL_VB), col(0)] + [ANY] * n_w,
        out_specs=[col(0), col(0), col(0)] + [ANY] * n_w,
        out_shape=[shp, shp, shp] + extra_shapes,
        scratch_shapes=[acc, acc, strip, strip] + extra_sems,
        compiler_params=_params(("arbitrary", "arbitrary")),
    )(proj3, proj3, proj3, do3, *extra)
    return res[:3], res[3:]


def _sigmoid(x):
    return 1.0 / (1.0 + jnp.exp(-x))


def _gate_out_norm(proj, ua, ub, w_out, x, g, *, tt=512):
    t_dim, d = ua.shape

    def body(ga_ref, gb_ref, ua_ref, ub_ref, w_ref, x_ref, g_ref, m_ref, h_ref, n_ref):
        mixed = (_sigmoid(ga_ref[...]) * ua_ref[...] + _sigmoid(gb_ref[...]) * ub_ref[...]).astype(BF16)
        m_ref[...] = mixed
        hv = x_ref[...] + jnp.dot(mixed, w_ref[...], preferred_element_type=F32)
        h_ref[...] = hv
        r = lax.rsqrt(jnp.mean(hv * hv, axis=-1, keepdims=True) + RMS_EPS)
        n_ref[...] = ((hv * r) * g_ref[...]).astype(BF16)

    row = pl.BlockSpec((tt, d), lambda i: (i, 0))
    return pl.pallas_call(
        body, name="gate_out_norm", grid=(t_dim // tt,),
        in_specs=[pl.BlockSpec((tt, d), lambda i: (i, 3)), pl.BlockSpec((tt, d), lambda i: (i, 4)), row, row,
                  pl.BlockSpec((d, d), lambda i: (0, 0)), row, pl.BlockSpec((1, d), lambda i: (0, 0))],
        out_specs=[row, row, row],
        out_shape=[jax.ShapeDtypeStruct((t_dim, d), BF16), jax.ShapeDtypeStruct((t_dim, d), F32),
                   jax.ShapeDtypeStruct((t_dim, d), BF16)],
        compiler_params=_params(("parallel",)),
    )(proj, proj, ua, ub, w_out, x, g)


def _out_dx_gate_bwd(dh, w_out, proj, ua, ub, *, tt=512):
    t_dim, d = ua.shape

    def body(dh_ref, w_ref, ga_ref, gb_ref, ua_ref, ub_ref, dua_ref, dub_ref, dg_ref):
        dm = lax.dot_general(dh_ref[...].astype(BF16), w_ref[...], NT, preferred_element_type=F32)
        sa = _sigmoid(ga_ref[...])
        sb = _sigmoid(gb_ref[...])
        dua_ref[...] = (dm * sa).astype(BF16)
        dub_ref[...] = (dm * sb).astype(BF16)
        dg_ref[:, :d] = (dm * ua_ref[...] * (sa * (1.0 - sa))).astype(BF16)
        dg_ref[:, d:] = (dm * ub_ref[...] * (sb * (1.0 - sb))).astype(BF16)

    row = pl.BlockSpec((tt, d), lambda i: (i, 0))
    wide = pl.BlockSpec((tt, 2 * d), lambda i: (i, 0))
    return pl.pallas_call(
        body, name="out_dx_gate_bwd", grid=(t_dim // tt,),
        in_specs=[row, pl.BlockSpec((d, d), lambda i: (0, 0)),
                  pl.BlockSpec((tt, d), lambda i: (i, 3)), pl.BlockSpec((tt, d), lambda i: (i, 4)), row, row],
        out_specs=[row, row, wide],
        out_shape=[jax.ShapeDtypeStruct((t_dim, d), BF16), jax.ShapeDtypeStruct((t_dim, d), BF16),
                   jax.ShapeDtypeStruct((t_dim, 2 * d), BF16)],
        compiler_params=_params(("parallel",)),
    )(dh, w_out, proj, proj, ua, ub)


def _ffn_up_swiglu(n, wg3, wu3, *, tt=512):
    t_dim, d = n.shape
    n_s, f4, _ = wg3.shape

    def body(n_ref, wg_ref, wu_ref, g_ref, u_ref, a_ref):
        nv = n_ref[...]
        for s in range(n_s):
            gv = lax.dot_general(nv, wg_ref[s], NT, preferred_element_type=F32)
            uv = lax.dot_general(nv, wu_ref[s], NT, preferred_element_type=F32)
            g_ref[s] = gv.astype(BF16)
            u_ref[s] = uv.astype(BF16)
            a_ref[s] = (gv * _sigmoid(gv) * uv).astype(BF16)

    wspec = pl.BlockSpec(wg3.shape, lambda i: (0, 0, 0))
    ospec = pl.BlockSpec((n_s, tt, f4), lambda i: (0, i, 0))
    shp = (n_s, t_dim, f4)
    return pl.pallas_call(
        body, name="ffn_up_swiglu", grid=(t_dim // tt,),
        in_specs=[pl.BlockSpec((tt, d), lambda i: (i, 0)), wspec, wspec], out_specs=[ospec, ospec, ospec],
        out_shape=[jax.ShapeDtypeStruct(shp, BF16)] * 3,
        compiler_params=_params(("parallel",)),
    )(n, wg3, wu3)


def _ffn_down_dx_swiglu(dh, wd3, g3, u3, *, tt=512):
    t_dim, d = dh.shape
    n_s, f4, _ = wd3.shape

    def body(dh_ref, w_ref, g_ref, u_ref, dg_ref, du_ref):
        dhv = dh_ref[...].astype(BF16)
        for s in range(n_s):
            da = lax.dot_general(dhv, w_ref[s], NT, preferred_element_type=F32)
            gv = g_ref[s].astype(F32)
            sg = _sigmoid(gv)
            dg_ref[s] = (da * u_ref[s].astype(F32) * (sg + gv * sg * (1.0 - sg))).astype(BF16)
            du_ref[s] = (da * (gv * sg)).astype(BF16)

    spec = pl.BlockSpec((n_s, tt, f4), lambda i: (0, i, 0))
    shp = jax.ShapeDtypeStruct((n_s, t_dim, f4), BF16)
    return pl.pallas_call(
        body, name="ffn_down_dx_swiglu", grid=(t_dim // tt,),
        in_specs=[pl.BlockSpec((tt, d), lambda i: (i, 0)), pl.BlockSpec(wd3.shape, lambda i: (0, 0, 0)), spec, spec],
        out_specs=[spec, spec], out_shape=[shp, shp],
        compiler_params=_params(("parallel",)),
    )(dh, wd3, g3, u3)


def _mem_fwd(qm, kvm, *, tt=2048):
    b_dim, s_dim, _ = qm.shape
    n_mem = kvm.shape[1]
    scale = MEM_HEAD_DIM ** -0.5

    def body(q_ref, k_ref, v_ref, o_ref):
        sc = lax.dot_general(q_ref[0], k_ref[0], NT, preferred_element_type=F32) * scale
        p = jnp.exp(sc - jnp.max(sc, axis=-1, keepdims=True))
        p = p / jnp.sum(p, axis=-1, keepdims=True)
        o_ref[0] = jnp.dot(p.astype(BF16), v_ref[0], preferred_element_type=F32).astype(BF16)

    qs = pl.BlockSpec((1, tt, MEM_HEAD_DIM), lambda b, h, i: (b, i, h))
    return pl.pallas_call(
        body, name="mem_fwd", grid=(b_dim, N_HEADS_MEM, s_dim // tt),
        in_specs=[qs, pl.BlockSpec((1, n_mem, MEM_HEAD_DIM), lambda b, h, i: (b, 0, h)),
                  pl.BlockSpec((1, n_mem, MEM_HEAD_DIM), lambda b, h, i: (b, 0, N_HEADS_MEM + h))],
        out_specs=qs, out_shape=jax.ShapeDtypeStruct(qm.shape, BF16),
        compiler_params=_params(("parallel", "parallel", "parallel")),
    )(qm, kvm, kvm)


def _mem_bwd(qm, kvm, dom, *, tt=2048):
    b_dim, s_dim, _ = qm.shape
    n_mem = kvm.shape[1]
    scale = MEM_HEAD_DIM ** -0.5

    def body(q_ref, k_ref, v_ref, do_ref, dq_ref, dk_ref, dv_ref):
        qv, kv, vv, dov = q_ref[0], k_ref[0], v_ref[0], do_ref[0]
        sc = lax.dot_general(qv, kv, NT, preferred_element_type=F32) * scale
        p = jnp.exp(sc - jnp.max(sc, axis=-1, keepdims=True))
        p = p / jnp.sum(p, axis=-1, keepdims=True)
        dp = lax.dot_general(dov, vv, NT, preferred_element_type=F32)
        ds = (p * (dp - jnp.sum(p * dp, axis=-1, keepdims=True)) * scale).astype(BF16)
        dq_ref[0] = jnp.dot(ds, kv, preferred_element_type=F32).astype(BF16)

        @pl.when(pl.program_id(2) == 0)
        def _():
            dk_ref[...] = jnp.zeros_like(dk_ref)
            dv_ref[...] = jnp.zeros_like(dv_ref)

        dk_ref[0] += lax.dot_general(ds, qv, TN, preferred_element_type=F32)
        dv_ref[0] += lax.dot_general(p.astype(BF16), dov, TN, preferred_element_type=F32)

    qs = pl.BlockSpec((1, tt, MEM_HEAD_DIM), lambda b, h, i: (b, i, h))
    ks = pl.BlockSpec((1, n_mem, MEM_HEAD_DIM), lambda b, h, i: (b, 0, h))
    vs = pl.BlockSpec((1, n_mem, MEM_HEAD_DIM), lambda b, h, i: (b, 0, N_HEADS_MEM + h))
    return pl.pallas_call(
        body, name="mem_bwd", grid=(b_dim, N_HEADS_MEM, s_dim // tt),
        in_specs=[qs, ks, vs, qs], out_specs=[qs, ks, ks],
        out_shape=[jax.ShapeDtypeStruct(qm.shape, BF16), jax.ShapeDtypeStruct((b_dim, n_mem, MEM_WIDTH), F32),
                   jax.ShapeDtypeStruct((b_dim, n_mem, MEM_WIDTH), F32)],
        compiler_params=_params(("parallel", "parallel", "arbitrary")),
    )(qm, kvm, kvm, dom)


def _adamw_math(wv, gv, mv, vv):
    nm = ADAM_B1 * mv + (1.0 - ADAM_B1) * gv
    nv = ADAM_B2 * vv + (1.0 - ADAM_B2) * (gv * gv)
    m_hat = nm / (1.0 - ADAM_B1 ** ADAM_STEP)
    v_hat = nv / (1.0 - ADAM_B2 ** ADAM_STEP)
    return -ADAM_LR * (m_hat / (jnp.sqrt(v_hat) + ADAM_EPS) + ADAM_WD * wv), nm, nv


def _adamw(w, g, m, v, *, name):
    rows, cols = w.shape
    tr = _tile(rows, 256, 8)

    def body(w_ref, g_ref, m_ref, v_ref, d_ref, nm_ref, nv_ref):
        d_ref[...], nm_ref[...], nv_ref[...] = _adamw_math(w_ref[...], g_ref[...], m_ref[...], v_ref[...])

    spec = pl.BlockSpec((tr, cols), lambda i: (i, 0))
    shp = jax.ShapeDtypeStruct((rows, cols), F32)
    return pl.pallas_call(
        body, name=name, grid=(rows // tr,),
        in_specs=[spec] * 4, out_specs=[spec] * 3, out_shape=[shp] * 3,
        compiler_params=_params(("parallel",)),
    )(w, g, m, v)


def _prefetch_spec(grid, in_specs, out_specs):
    return pltpu.PrefetchScalarGridSpec(num_scalar_prefetch=1, grid=grid, in_specs=in_specs, out_specs=out_specs)


def _adamw_halves(w, mine, theirs, m, v, c_idx, *, name):
    rows, cols = w.shape
    half = rows // 2
    tr = _tile(half, _row_cap(cols), 8)
    nh = half // tr

    def body(c_ref, w_ref, mine_ref, theirs_ref, m_ref, v_ref, g_ref, d_ref, nm_ref, nv_ref):
        gv = jnp.where(pl.program_id(0) == c_ref[0], mine_ref[...], theirs_ref[...])
        g_ref[...] = gv
        d_ref[...], nm_ref[...], nv_ref[...] = _adamw_math(w_ref[...], gv, m_ref[...], v_ref[...])

    full = pl.BlockSpec((tr, cols), lambda h, i, c_ref: (h * nh + i, 0))
    part = pl.BlockSpec((tr, cols), lambda h, i, c_ref: (i, 0))
    shp = jax.ShapeDtypeStruct((rows, cols), F32)
    return pl.pallas_call(
        body, name=name, grid_spec=_prefetch_spec((2, nh), [full, part, part, full, full], [full] * 4),
        out_shape=[shp] * 4,
        compiler_params=_params(("parallel", "parallel")),
    )(c_idx, w, mine, theirs, m, v)


def _pair_sum(g3, theirs, c_idx, *, name):
    n, rows, cols = g3.shape
    half = rows // 2
    tr = _tile(half, _row_cap(cols), 16)

    def body(c_ref, g_ref, t_ref, o_ref):
        o_ref[...] = (g_ref[...] + t_ref[...]).astype(BF16)

    part = pl.BlockSpec((None, tr, cols), lambda s, i, c_ref: (s, i, 0))
    return pl.pallas_call(
        body, name=name,
        grid_spec=_prefetch_spec((n, half // tr),
                                 [pl.BlockSpec((None, None, tr, cols), lambda s, i, c_ref: (s, c_ref[0], i, 0)), part],
                                 part),
        out_shape=jax.ShapeDtypeStruct((n, half, cols), BF16),
        compiler_params=_params(("parallel", "parallel")),
    )(c_idx, g3.reshape(n, 2, half, cols), theirs)


def _chip_sum(pair, recv, s_idx, *, name):
    _, half, cols = pair.shape
    tr = _tile(half, _row_cap(cols), 16)

    def body(s_ref, p_ref, r_ref, o_ref):
        o_ref[...] = ((p_ref[...].astype(F32) + r_ref[0].astype(F32)) + r_ref[1].astype(F32)) + r_ref[2].astype(F32)

    return pl.pallas_call(
        body, name=name,
        grid_spec=_prefetch_spec((half // tr,),
                                 [pl.BlockSpec((None, tr, cols), lambda i, s_ref: (s_ref[0], i, 0)),
                                  pl.BlockSpec((N_CHIPS - 1, tr, cols), lambda i, s_ref: (0, i, 0))],
                                 pl.BlockSpec((tr, cols), lambda i, s_ref: (i, 0))),
        out_shape=jax.ShapeDtypeStruct((half, cols), F32),
        compiler_params=_params(("parallel",)),
    )(s_idx, pair, recv)


def _sum8(parts):
    n, rows, cols = parts.shape

    def body(p_ref, o_ref):
        acc = p_ref[0]
        for i in range(1, n):
            acc = acc + p_ref[i]
        o_ref[...] = acc

    return pl.pallas_call(
        body, name="small_sum", grid=(1,),
        in_specs=[pl.BlockSpec((n, rows, cols), lambda i: (0, 0, 0))],
        out_specs=pl.BlockSpec((rows, cols), lambda i: (0, 0)),
        out_shape=jax.ShapeDtypeStruct((rows, cols), parts.dtype),
        compiler_params=_params(("arbitrary",)),
    )(parts)


def _place():
    return lax.axis_index("x"), lax.axis_index("y"), lax.axis_index("c")


ANY = pl.BlockSpec(memory_space=pl.ANY)


def _rider_parts(rider):
    if rider is None:
        return (), [], []
    kind, arrays = rider
    n = len(arrays)
    shapes = {"gather": _gathered_shapes, "pair": _pair_shapes, "chip": _chip_shapes}[kind](arrays)
    sems = _gather_sems(n) if kind == "gather" else _exchange_sems(n if kind == "pair" else 3 * n)
    return tuple(arrays), shapes, sems


def _rider_hooks(rider, ins, outs, sems, step, n_steps):
    if rider is None:
        return (lambda: None), (lambda: None)
    if rider[0] == "gather":
        start, forward, finish = _gather_steps(ins, outs, *sems)
    else:
        start, finish = {"pair": _pair_steps, "chip": _chip_steps}[rider[0]](ins, outs, *sems)
        forward = None

    def begin():
        pl.when(step == 0)(start)

    def end():
        if forward is not None:
            pl.when(step == n_steps - 2)(forward)
        pl.when(step == n_steps - 1)(finish)

    return begin, end


def _exchange_alone(rider, *, name):
    extra, shapes, sems = _rider_parts(rider)
    n = len(extra)

    def body(*refs):
        begin, end = _rider_hooks(rider, refs[:n], refs[n:2 * n], refs[-2:], jnp.int32(0), 1)
        begin()
        end()

    return pl.pallas_call(
        body, name=name, out_shape=shapes, in_specs=[ANY] * n, out_specs=[ANY] * n, scratch_shapes=sems,
    )(*extra)


def _gathered_shapes(shards):
    return [jax.ShapeDtypeStruct((N_CHIPS,) + s.shape, s.dtype) for s in shards]


def _gather_sems(n):
    return [pltpu.SemaphoreType.DMA((7 * n,)), pltpu.SemaphoreType.DMA((7 * n,))]


def _gather_steps(ins, outs, send_sems, recv_sems):
    n = len(ins)
    halves = [r.shape[0] // 2 for r in ins]
    x, y, c = _place()
    my_chip = 2 * x + y
    me, sibling = (x, y, c), (x, y, 1 - c)
    chips = [(1 - x, y), (x, 1 - y), (1 - x, 1 - y)]

    def half_of(w, chip, pc):
        return outs[w].at[chip, pl.ds(pc * halves[w], halves[w]), :]

    def copy(w, k, src, dst, to):
        return pltpu.make_async_remote_copy(
            src_ref=src, dst_ref=dst, send_sem=send_sems.at[7 * w + k], recv_sem=recv_sems.at[7 * w + k],
            device_id=to, device_id_type=MESH)

    def firsts():
        cps = []
        for w in range(n):
            cps.append(copy(w, 0, ins[w], outs[w].at[my_chip], sibling))
            mine = ins[w].at[pl.ds(c * halves[w], halves[w]), :]
            for j, (px, py) in enumerate(chips):
                cps.append(copy(w, 1 + j, mine, half_of(w, my_chip, c), (px, py, c)))
        return cps

    def passes():
        return [copy(w, 4 + j, half_of(w, 2 * px + py, c), half_of(w, 2 * px + py, c), sibling)
                for w in range(n) for j, (px, py) in enumerate(chips)]

    def start():
        for cp in firsts():
            cp.start()

    def forward():
        fws = passes()
        for w in range(n):
            for j, (px, py) in enumerate(chips):
                landed = half_of(w, 2 * px + py, c)
                copy(w, 1 + j, landed, landed, me).wait_recv()
                fws[3 * w + j].start()

    def finish():
        for w in range(n):
            copy(w, 0, ins[w], outs[w].at[my_chip], me).wait_recv()
            for j, (px, py) in enumerate(chips):
                landed = half_of(w, 2 * px + py, 1 - c)
                copy(w, 4 + j, landed, landed, me).wait_recv()
        for cp in firsts() + passes():
            cp.wait_send()

    return start, forward, finish


def _pair_shapes(grads):
    return [jax.ShapeDtypeStruct((g.shape[0], g.shape[1] // 2, g.shape[2]), g.dtype) for g in grads]


def _exchange_sems(n):
    return [pltpu.SemaphoreType.DMA((n,)), pltpu.SemaphoreType.DMA((n,))]


def _exchange_steps(copies):
    def start():
        for cp in copies():
            cp.start()

    def finish():
        for cp in copies():
            cp.wait()

    return start, finish


def _pair_steps(ins, outs, send_sems, recv_sems):
    x, y, c = _place()

    def copies():
        return [pltpu.make_async_remote_copy(
            src_ref=ins[w].at[:, pl.ds((1 - c) * (ins[w].shape[1] // 2), ins[w].shape[1] // 2), :], dst_ref=outs[w],
            send_sem=send_sems.at[w], recv_sem=recv_sems.at[w], device_id=(x, y, 1 - c), device_id_type=MESH)
            for w in range(len(ins))]

    return _exchange_steps(copies)


def _chip_shapes(pairs):
    return [jax.ShapeDtypeStruct((N_CHIPS - 1,) + p.shape[1:], p.dtype) for p in pairs]


def _chip_steps(ins, outs, send_sems, recv_sems):
    x, y, c = _place()
    others = [(1 - x, y), (x, 1 - y), (1 - x, 1 - y)]

    def copies():
        return [pltpu.make_async_remote_copy(
            src_ref=ins[w].at[2 * px + py], dst_ref=outs[w].at[j],
            send_sem=send_sems.at[3 * w + j], recv_sem=recv_sems.at[3 * w + j],
            device_id=(px, py, c), device_id_type=MESH)
            for w in range(len(ins)) for j, (px, py) in enumerate(others)]

    return _exchange_steps(copies)


def _swap_halves(mine):
    n = len(mine)

    def body(*refs):
        ins, outs, send_sems, recv_sems = refs[:n], refs[n:2 * n], refs[2 * n], refs[2 * n + 1]
        x, y, c = _place()
        copies = [pltpu.make_async_remote_copy(
            src_ref=ins[w], dst_ref=outs[w], send_sem=send_sems.at[w], recv_sem=recv_sems.at[w],
            device_id=(x, y, 1 - c), device_id_type=MESH) for w in range(n)]
        for cp in copies:
            cp.start()
        for cp in copies:
            cp.wait()

    return pl.pallas_call(
        body, name="grad_swap_halves",
        out_shape=[jax.ShapeDtypeStruct(h.shape, h.dtype) for h in mine],
        in_specs=[ANY] * n, out_specs=[ANY] * n,
        scratch_shapes=[pltpu.SemaphoreType.DMA((n,)), pltpu.SemaphoreType.DMA((n,))],
    )(*mine)


def _gather_small(small):
    srows, cols = small.shape

    def body(s_ref, all_ref, send_sems, recv_sems, local_sem):
        x, y, c = _place()
        me = 4 * x + 2 * y + c
        keep_small = pltpu.make_async_copy(s_ref, all_ref.at[me], local_sem)
        keep_small.start()
        sends = []
        for kk in range(1, 8):
            peer = (x ^ (kk >> 2), y ^ ((kk >> 1) & 1), c ^ (kk & 1))
            sends.append(pltpu.make_async_remote_copy(
                src_ref=s_ref, dst_ref=all_ref.at[me],
                send_sem=send_sems.at[kk], recv_sem=recv_sems.at[kk], device_id=peer, device_id_type=MESH))
        for cp in sends:
            cp.start()
        for kk in range(1, 8):
            px, py, pc = x ^ (kk >> 2), y ^ ((kk >> 1) & 1), c ^ (kk & 1)
            pltpu.make_async_remote_copy(
                src_ref=s_ref, dst_ref=all_ref.at[4 * px + 2 * py + pc],
                send_sem=send_sems.at[kk], recv_sem=recv_sems.at[kk], device_id=(px, py, pc),
                device_id_type=MESH).wait_recv()
        for cp in sends:
            cp.wait_send()
        keep_small.wait()

    return pl.pallas_call(
        body, name="gather_small",
        out_shape=jax.ShapeDtypeStruct((8, srows, cols), small.dtype),
        in_specs=[ANY], out_specs=ANY,
        scratch_shapes=[pltpu.SemaphoreType.DMA((8,)), pltpu.SemaphoreType.DMA((8,)), pltpu.SemaphoreType.DMA],
    )(small)


SHARDED = (("w_in", D_MODEL, IN_COLS, 1), ("w_up_a", ATT_WIDTH, D_MODEL, 1), ("w_up_b", ATT_WIDTH, D_MODEL, 1),
           ("w_out", D_MODEL, D_MODEL, 0), ("w_q_mem", D_MODEL, MEM_WIDTH, 0), ("w_kv_mem", D_MODEL, 2 * MEM_WIDTH, 0),
           ("w_o_mem", MEM_WIDTH, D_MODEL, 1), ("w_ffn_gate", D_FF, D_MODEL, 0), ("w_ffn_up", D_FF, D_MODEL, 0),
           ("w_ffn_down", D_FF, D_MODEL, 0))
TRANSPOSED = ("w_ffn_gate", "w_ffn_up")
NAMES = tuple(n for n, _, _, _ in SHARDED)


def _held(name, shard):
    return shard.T if name in TRANSPOSED else shard
EARLY, LATE = NAMES[:1], NAMES[1:]
GAINS = ("g_mix", "g_mem_q", "g_mem_kv", "g_ffn", "g_final")


def _natural(w3):
    n, r, c = w3.shape
    return w3.reshape(n * r, c)


def _shard_major(g, axis):
    if axis == 1:
        return g
    r, c = g.shape
    return g.reshape(N_CHIPS, r // N_CHIPS, c)


def kernel(x, mem, positions, g_mix, w_in, w_up_a, w_up_b, w_out, g_mem_q, g_mem_kv, w_q_mem, w_kv_mem, w_o_mem, g_ffn, w_ffn_gate, w_ffn_up, w_ffn_down, g_final, loss_target, m_g_mix, m_w_in, m_w_up_a, m_w_up_b, m_w_out, m_g_mem_q, m_g_mem_kv, m_w_q_mem, m_w_kv_mem, m_w_o_mem, m_g_ffn, m_w_ffn_gate, m_w_ffn_up, m_w_ffn_down, m_g_final, v_g_mix, v_w_in, v_w_up_a, v_w_up_b, v_w_out, v_g_mem_q, v_g_mem_kv, v_w_q_mem, v_w_kv_mem, v_w_o_mem, v_g_ffn, v_w_ffn_gate, v_w_ffn_up, v_w_ffn_down, v_g_final):
    given = dict(locals())
    shards = {n: _held(n, given[n][0]) for n in NAMES}

    early_shards = [shards[n].astype(BF16) for n in EARLY]
    late_shards = [shards[n].astype(BF16) for n in LATE]
    c_idx = lax.axis_index("c").astype(jnp.int32).reshape(1)
    s_idx = (2 * lax.axis_index("x") + lax.axis_index("y")).astype(jnp.int32).reshape(1)

    loss_row, grad_x, mine, gain_grads = _local_step(x, mem, positions, loss_target, g_mix, g_mem_q, g_mem_kv,
                                                     g_ffn, g_final, {}, early_shards, late_shards, (c_idx, s_idx))
    return _reduce_and_update(given, shards, loss_row, grad_x, mine, gain_grads, c_idx)


def _reduce_halves(glist, names, c_idx, s_idx, pair_exchange, chip_exchange):
    theirs = pair_exchange(glist)
    pairs = [_pair_sum(g, t, c_idx, name="pair_sum_" + n) for n, g, t in zip(names, glist, theirs)]
    recv = chip_exchange(pairs)
    return [_chip_sum(p, r, s_idx, name="chip_sum_" + n) for n, p, r in zip(names, pairs, recv)]


def _local_step(x, mem, positions, loss_target, g_mix, g_mem_q, g_mem_kv, g_ffn, g_final, wf,
                early_shards=None, late_shards=None, place=None):
    b_dim, s_dim, d = x.shape
    t_dim = b_dim * s_dim
    n_mem = mem.shape[1]
    wf = dict(wf)

    xb = x.reshape(t_dim, d)
    tgt = loss_target.reshape(t_dim, d)
    memf = mem.reshape(b_dim * n_mem, d)
    gfin = g_final.reshape(1, d)
    pos = positions.reshape(t_dim, 1).astype(F32)

    lane = jnp.arange(LANES) % HEAD_DIM
    half = ROPE_DIM // 2
    inv_freq = ROPE_THETA ** (-jnp.arange(half, dtype=F32) / half)
    inv_lane = jnp.where(lane < ROPE_DIM, inv_freq[lane % half], 0.0).reshape(1, -1).astype(F32)
    sel_a = (lane < half).astype(F32).reshape(1, -1)
    sel_b = ((lane >= half) & (lane < ROPE_DIM)).astype(F32).reshape(1, -1)

    def rows3(t):
        return t.reshape(b_dim, s_dim, t.shape[-1])

    def rows2(t):
        return t.reshape(t_dim, t.shape[-1])

    if early_shards:
        n1, gathered = _rms_fwd(xb, g_mix, name="rms_mix", rider=("gather", early_shards))
        wf.update(zip(EARLY, gathered))
    else:
        n1 = _rms_fwd(xb, g_mix, name="rms_mix")
    proj = _mm_cs(n1, wf["w_in"], name="mm_in")
    proj3 = rows3(proj)
    cs, sn = _rope_table(pos, inv_lane, sel_a, sel_b)
    cs3, sn3 = rows3(cs), rows3(sn)
    (oa16, oa32, lse_a), _ = _dil_fwd(proj3, cs3, sn3, sel_a, sel_b)
    ob16, gathered = _sb_fwd(proj3, ("gather", late_shards) if late_shards else None)
    wf.update(zip(LATE, gathered))
    w_out, w_q, w_kv = _natural(wf["w_out"]), _natural(wf["w_q_mem"]), _natural(wf["w_kv_mem"])
    oa, ob = rows2(oa16), rows2(ob16)
    ua = _mm_sm(oa, wf["w_up_a"], name="mm_up_a", out_dtype=BF16)
    ub = _mm_sm(ob, wf["w_up_b"], name="mm_up_b", out_dtype=BF16)
    mixed, h1, hn = _gate_out_norm(proj, ua, ub, w_out, xb, g_mem_q)

    memn = _rms_fwd(memf, g_mem_kv, name="rms_mem_kv")
    qm = _mm(hn, w_q, name="mm_q_mem", out_dtype=BF16)
    kvm = _mm(memn, w_kv, name="mm_kv_mem", out_dtype=BF16)
    qm3, kvm3 = rows3(qm), kvm.reshape(b_dim, n_mem, 2 * MEM_WIDTH)
    om = rows2(_mem_fwd(qm3, kvm3))
    h2 = _mm_sm(om, wf["w_o_mem"], name="mm_o_mem", add=h1)

    n3 = _rms_fwd(h2, g_ffn, name="rms_ffn")
    gate3, up3, act3 = _ffn_up_swiglu(n3, wf["w_ffn_gate"], wf["w_ffn_up"])
    loss_row, dh3, dg_final = _down_final(act3, wf["w_ffn_down"], h2, gfin, tgt)

    grads = {}
    grads["w_ffn_down"] = _mm_ffn_down_dw(act3, dh3, name="mm_down_dw")
    dgate3, dup3 = _ffn_down_dx_swiglu(dh3, wf["w_ffn_down"], gate3, up3)
    grads["w_ffn_gate"] = _mm_ffn_down_dw(dgate3, n3, name="mm_gate_dw")
    grads["w_ffn_up"] = _mm_ffn_down_dw(dup3, n3, name="mm_up_dw")
    dn3 = _ffn_up_dx(dgate3, dup3, wf["w_ffn_gate"], wf["w_ffn_up"])
    dh2, dg_ffn = _rms_bwd(h2, g_ffn, dn3, dh3, name="rms_ffn_bwd")

    dom = _mm_sm_dx(dh2, wf["w_o_mem"], name="mm_o_mem_dx", out_dtype=BF16)
    grads["w_o_mem"] = _mm_sm_dw(om, dh2, name="mm_o_mem_dw")
    dqm, dkm, dvm = _mem_bwd(qm3, kvm3, rows3(dom))
    dqm = rows2(dqm)
    dkvm = jnp.concatenate([dkm, dvm], axis=-1).reshape(b_dim * n_mem, 2 * MEM_WIDTH).astype(BF16)
    grads["w_q_mem"] = _shard_major(_mm(hn, dqm, name="mm_q_mem_dw", ta=True), 0)
    dhn = _mm(dqm, w_q, name="mm_q_mem_dx", tb=True)
    grads["w_kv_mem"] = _shard_major(_mm(memn, dkvm, name="mm_kv_mem_dw", ta=True), 0)
    dmemn = _mm(dkvm, w_kv, name="mm_kv_mem_dx", tb=True)
    _, dg_mem_kv = _rms_bwd(memf, g_mem_kv, dmemn, None, name="rms_mem_kv_bwd")
    dh1, dg_mem_q = _rms_bwd(h1, g_mem_q, dhn, dh2, name="rms_mem_q_bwd")

    grads["w_out"] = _shard_major(_mm(mixed, dh1, name="mm_out_dw", ta=True), 0)
    dua, dub, dgates = _out_dx_gate_bwd(dh1, w_out, proj, ua, ub)
    doa = _mm_sm_dx(dua, wf["w_up_a"], name="mm_up_a_dx")
    grads["w_up_a"] = _mm_sm_dw(oa, dua, name="mm_up_a_dw")
    dob = _mm_sm_dx(dub, wf["w_up_b"], name="mm_up_b_dx", out_dtype=BF16)
    grads["w_up_b"] = _mm_sm_dw(ob, dub, name="mm_up_b_dw")

    att = {}

    def dil_with_pairs(glist):
        att["a"], theirs = _dil_bwd(proj3, cs3, sn3, sel_a, sel_b, rows3(doa), oa32, lse_a,
                                    ("pair", glist) if glist else None)
        return theirs

    def sb_with_chips(pairs):
        att["b"], recv = _sb_bwd(proj3, rows3(dob), ("chip", pairs) if pairs else None)
        return recv

    if place is None:
        dil_with_pairs(())
        sb_with_chips(())
    else:
        mine_late = _reduce_halves([grads[n] for n in LATE], LATE, *place, dil_with_pairs, sb_with_chips)
    dproj = jnp.concatenate([rows2(t) for t in att["a"] + att["b"]] + [dgates], axis=1)
    grads["w_in"] = _mm_cs_dw(n1, dproj, name="mm_in_dw")
    if place is None:
        dn1 = _mm_cs_dx(dproj, wf["w_in"], name="mm_in_dx")
        dx, dg_mix = _rms_bwd(xb, g_mix, dn1, dh1, name="rms_mix_bwd")
    else:
        tail = {}

        def pair_alone(glist):
            return _exchange_alone(("pair", glist), name="grad_pair_exchange")

        def dx_with_chips(pairs):
            tail["dn1"], recv = _mm_cs_dx(dproj, wf["w_in"], name="mm_in_dx", rider=("chip", pairs))
            return recv

        mine_early = _reduce_halves([grads[n] for n in EARLY], EARLY, *place, pair_alone, dx_with_chips)
        dx, dg_mix = _rms_bwd(xb, g_mix, tail["dn1"], dh1, name="rms_mix_bwd")
    grad_x = dx.reshape(b_dim, s_dim, d)
    gains = (dg_mix, dg_mem_q, dg_mem_kv, dg_ffn, dg_final)
    if place is None:
        return loss_row, grad_x, grads, gains
    return loss_row, grad_x, mine_early + mine_late, gains


def _reduce_and_update(given, shards, loss_row, grad_x, mine, gain_grads, c_idx):
    d = D_MODEL
    dg_mix, dg_mem_q, dg_mem_kv, dg_ffn, dg_final = gain_grads
    small = jnp.concatenate([dg_mix, dg_mem_q, dg_mem_kv, dg_ffn, dg_final,
                             jnp.pad(loss_row, ((0, 0), (0, FLAT_COLS - LANES))), jnp.zeros((2, FLAT_COLS), F32)], axis=0)
    small_all = _gather_small(small)
    others = _swap_halves(mine)
    small_sum = _sum8(small_all)
    loss = small_sum[5, 0]

    out_g, out_d, out_m, out_v = {}, {}, {}, {}
    for n, mine_n, other_n in zip(NAMES, mine, others):
        res = _adamw_halves(shards[n], mine_n, other_n, _held(n, given["m_" + n][0]), _held(n, given["v_" + n][0]),
                            c_idx, name="adamw_" + n)
        out_g[n], out_d[n], out_m[n], out_v[n] = [_held(n, r)[None] for r in res]
    gain_w = jnp.concatenate([given[n].reshape(1, d) for n in GAINS], axis=0)
    gain_m = jnp.concatenate([given["m_" + n].reshape(1, d) for n in GAINS], axis=0)
    gain_v = jnp.concatenate([given["v_" + n].reshape(1, d) for n in GAINS], axis=0)
    gain_g = small_sum[:len(GAINS)]
    gd, gm, gv = _adamw(gain_w, gain_g, gain_m, gain_v, name="adamw_gains")
    for i, n in enumerate(GAINS):
        shape = given[n].shape
        out_g[n], out_d[n] = gain_g[i].reshape(shape), gd[i].reshape(shape)
        out_m[n], out_v[n] = gm[i].reshape(shape), gv[i].reshape(shape)

    order = ["g_mix", "w_in", "w_up_a", "w_up_b", "w_out", "g_mem_q", "g_mem_kv", "w_q_mem", "w_kv_mem", "w_o_mem",
             "g_ffn", "w_ffn_gate", "w_ffn_up", "w_ffn_down", "g_final"]
    return (loss, grad_x, *[out_g[n] for n in order], *[out_d[n] for n in order],
            *[out_m[n] for n in order], *[out_v[n] for n in order])
```

```python
import jax
import jax.numpy as jnp
from jax import lax
from jax.experimental import pallas as pl
from jax.experimental.pallas import tpu as pltpu

F32 = jnp.float32
BF16 = jnp.bfloat16
MESH = pl.DeviceIdType.MESH

D_MODEL = 1024
HEAD_DIM = 64
N_HEADS = 8
ATT_WIDTH = N_HEADS * HEAD_DIM
DIL_PATTERNS = ((128, 1), (512, 4), (2048, 16))
BLOCK = 128
SB_ROWS = 1024
SB_STEP_FWD, SB_STEP_BWD = 8, 4
ROPE_THETA = 500000.0
ROPE_DIM = HEAD_DIM // 4
N_HEADS_MEM = 4
MEM_HEAD_DIM = 128
MEM_WIDTH = N_HEADS_MEM * MEM_HEAD_DIM
D_FF = 2816
IN_COLS = 6 * ATT_WIDTH + 2 * D_MODEL
RMS_EPS = 1e-6
ADAM_LR = 0.001
ADAM_B1 = 0.9
ADAM_B2 = 0.999
ADAM_EPS = 1e-08
ADAM_WD = 0.01
ADAM_STEP = 10

N_CHIPS = 4
LANES = 128
FLAT_COLS = 1024
VMEM_LIMIT = 56 * 1024 * 1024

PAIRS = ATT_WIDTH // LANES
COL_QA, COL_KA, COL_VA, COL_QB, COL_KB, COL_VB = (i * PAIRS for i in range(6))

MM_CAP = 1408
TOK_CAP = 2048
NN = (((1,), (0,)), ((), ()))
NT = (((1,), (1,)), ((), ()))
TN = (((0,), (0,)), ((), ()))
BNN = (((2,), (1,)), ((0,), (0,)))
BNT = (((2,), (2,)), ((0,), (0,)))
BTN = (((1,), (1,)), ((0,), (0,)))
DIL_BATCH = 16
DIL_CHUNK = 512


def _tile(dim, cap, unit=LANES):
    if dim <= cap:
        return dim
    best = None
    for t in range(unit, cap + 1, unit):
        if dim % t == 0:
            best = t
    assert best is not None, (dim, cap)
    return best


def _row_cap(cols):
    return max(256, (1 << 18) // cols)


def _params(sem):
    return pltpu.CompilerParams(dimension_semantics=sem, vmem_limit_bytes=VMEM_LIMIT)


def _mm(a, b, *, name, ta=False, tb=False, add=None, out_dtype=F32,
        tm_cap=MM_CAP, tn_cap=MM_CAP, tk_cap=MM_CAP):
    if ta:
        k_dim, m_dim = a.shape
    else:
        m_dim, k_dim = a.shape
    if tb:
        n_dim, kb = b.shape
    else:
        kb, n_dim = b.shape
    assert kb == k_dim, (a.shape, b.shape, ta, tb)
    tm, tn, tk = _tile(m_dim, tm_cap), _tile(n_dim, tn_cap), _tile(k_dim, tk_cap)
    nk = k_dim // tk
    dims = (((0 if ta else 1,), (1 if tb else 0,)), ((), ()))
    has_add = add is not None

    def body(*refs):
        if has_add:
            a_ref, b_ref, add_ref, o_ref = refs[:4]
        else:
            a_ref, b_ref, o_ref = refs[:3]
        part = lax.dot_general(a_ref[...].astype(BF16), b_ref[...].astype(BF16), dims, preferred_element_type=F32)

        def finish(r):
            if has_add:
                r = add_ref[...] + r
            o_ref[...] = r.astype(out_dtype)

        if nk == 1:
            finish(part)
            return
        acc_ref = refs[-1]
        k = pl.program_id(2)

        @pl.when(k == 0)
        def _():
            acc_ref[...] = part

        @pl.when(k > 0)
        def _():
            acc_ref[...] += part

        @pl.when(k == nk - 1)
        def _():
            finish(acc_ref[...])

    a_spec = pl.BlockSpec((tk, tm), lambda i, j, k: (k, i)) if ta else pl.BlockSpec((tm, tk), lambda i, j, k: (i, k))
    b_spec = pl.BlockSpec((tn, tk), lambda i, j, k: (j, k)) if tb else pl.BlockSpec((tk, tn), lambda i, j, k: (k, j))
    o_spec = pl.BlockSpec((tm, tn), lambda i, j, k: (i, j))
    in_specs = [a_spec, b_spec] + ([o_spec] if has_add else [])
    args = (a, b) + ((add,) if has_add else ())
    return pl.pallas_call(
        body, name=name, grid=(m_dim // tm, n_dim // tn, nk),
        in_specs=in_specs, out_specs=o_spec,
        out_shape=jax.ShapeDtypeStruct((m_dim, n_dim), out_dtype),
        scratch_shapes=[pltpu.VMEM((tm, tn), F32)] if nk > 1 else [],
        compiler_params=_params(("parallel", "parallel", "arbitrary")),
    )(*args)


def _mm_core(name, a, b, a_spec, b_spec, o_spec, out_shape, grid, dims, *, add=None, out_dtype=F32, rider=None):
    nk = grid[2]
    has_add = add is not None
    n_in = 3 if has_add else 2
    acc_shape = tuple(d for d in o_spec.block_shape if d is not None)
    extra, extra_shapes, extra_sems = _rider_parts(rider)
    n_w = len(extra)

    def body(*refs):
        a_ref, b_ref = refs[:2]
        o_ref = refs[n_in + n_w]
        step = (pl.program_id(0) * grid[1] + pl.program_id(1)) * nk + pl.program_id(2)
        begin, end = _rider_hooks(rider, refs[n_in:n_in + n_w], refs[n_in + n_w + 1:n_in + 2 * n_w + 1], refs[-2:],
                                  step, grid[0] * grid[1] * nk)
        begin()
        part = lax.dot_general(a_ref[...].astype(BF16), b_ref[...].astype(BF16), dims, preferred_element_type=F32)

        def finish(r):
            if has_add:
                r = refs[2][...] + r
            o_ref[...] = r.astype(out_dtype)

        if nk == 1:
            finish(part)
        else:
            acc_ref = refs[n_in + 2 * n_w + 1]
            k = pl.program_id(2)

            @pl.when(k == 0)
            def _():
                acc_ref[...] = part

            @pl.when(k > 0)
            def _():
                acc_ref[...] += part

            @pl.when(k == nk - 1)
            def _():
                finish(acc_ref[...])
        end()

    in_specs = [a_spec, b_spec] + ([o_spec] if has_add else []) + [ANY] * n_w
    args = (a, b) + ((add,) if has_add else ()) + extra
    res = pl.pallas_call(
        body, name=name, grid=grid, in_specs=in_specs, out_specs=[o_spec] + [ANY] * n_w,
        out_shape=[jax.ShapeDtypeStruct(out_shape, out_dtype)] + extra_shapes,
        scratch_shapes=([pltpu.VMEM(acc_shape, F32)] if nk > 1 else []) + extra_sems,
        compiler_params=_params(("arbitrary",) * 3 if n_w else ("parallel", "parallel", "arbitrary")),
    )(*args)
    return (res[0], res[1:]) if n_w else res[0]


def _mm_cs(a, w3, *, name):
    m_dim, k_dim = a.shape
    _, _, n4 = w3.shape
    tm, tn, tk = _tile(m_dim, TOK_CAP), _tile(n4, MM_CAP), _tile(k_dim, MM_CAP)
    npb = n4 // tn
    return _mm_core(name, a, w3,
                    pl.BlockSpec((tm, tk), lambda i, j, k: (i, k)),
                    pl.BlockSpec((None, tk, tn), lambda i, j, k: (j // npb, k, j % npb)),
                    pl.BlockSpec((tm, tn), lambda i, j, k: (i, j)),
                    (m_dim, N_CHIPS * n4), (m_dim // tm, N_CHIPS * npb, k_dim // tk), NN)


def _mm_cs_dx(dy, w3, *, name, out_dtype=F32, rider=None):
    m_dim, _ = dy.shape
    _, k_dim, n4 = w3.shape
    tm, tkw, tn = _tile(m_dim, TOK_CAP), _tile(k_dim, MM_CAP), _tile(n4, MM_CAP)
    npb = n4 // tn
    return _mm_core(name, dy, w3,
                    pl.BlockSpec((tm, tn), lambda i, j, k: (i, k)),
                    pl.BlockSpec((None, tkw, tn), lambda i, j, k: (k // npb, j, k % npb)),
                    pl.BlockSpec((tm, tkw), lambda i, j, k: (i, j)),
                    (m_dim, k_dim), (m_dim // tm, k_dim // tkw, N_CHIPS * npb), NT, out_dtype=out_dtype, rider=rider)


def _mm_cs_dw(a, dy, *, name):
    m_dim, k_dim = a.shape
    n4 = dy.shape[1] // N_CHIPS
    tmk, tn, tk = _tile(k_dim, MM_CAP), _tile(n4, MM_CAP), _tile(m_dim, TOK_CAP)
    npb = n4 // tn
    return _mm_core(name, a, dy,
                    pl.BlockSpec((tk, tmk), lambda i, j, k: (k, i)),
                    pl.BlockSpec((tk, tn), lambda i, j, k: (k, j)),
                    pl.BlockSpec((None, tmk, tn), lambda i, j, k: (j // npb, i, j % npb)),
                    (N_CHIPS, k_dim, n4), (k_dim // tmk, N_CHIPS * npb, m_dim // tk), TN)


def _mm_sm(a, w3, *, name, add=None, out_dtype=F32, tt=1024):
    t_dim, k_dim = a.shape
    n_s, _, n4 = w3.shape
    has_add = add is not None

    def body(*refs):
        a_ref, w_ref, o_ref = refs[0], refs[1], refs[-1]
        av = a_ref[...].astype(BF16)
        for s in range(n_s):
            cols = slice(s * n4, (s + 1) * n4)
            r = jnp.dot(av, w_ref[s], preferred_element_type=F32)
            if has_add:
                r = refs[2][:, cols] + r
            o_ref[:, cols] = r.astype(out_dtype)

    row = pl.BlockSpec((tt, n_s * n4), lambda i: (i, 0))
    return pl.pallas_call(
        body, name=name, grid=(t_dim // tt,),
        in_specs=[pl.BlockSpec((tt, k_dim), lambda i: (i, 0)), pl.BlockSpec(w3.shape, lambda i: (0, 0, 0))]
        + ([row] if has_add else []),
        out_specs=row, out_shape=jax.ShapeDtypeStruct((t_dim, n_s * n4), out_dtype),
        compiler_params=_params(("parallel",)),
    )(*((a, w3) + ((add,) if has_add else ())))


def _mm_sm_dx(dy, w3, *, name, out_dtype=F32, tt=1024):
    t_dim, _ = dy.shape
    n_s, k_dim, n4 = w3.shape

    def body(dy_ref, w_ref, o_ref):
        dyv = dy_ref[...].astype(BF16)
        acc = lax.dot_general(dyv[:, :n4], w_ref[0], NT, preferred_element_type=F32)
        for s in range(1, n_s):
            acc = acc + lax.dot_general(dyv[:, s * n4:(s + 1) * n4], w_ref[s], NT, preferred_element_type=F32)
        o_ref[...] = acc.astype(out_dtype)

    return pl.pallas_call(
        body, name=name, grid=(t_dim // tt,),
        in_specs=[pl.BlockSpec((tt, n_s * n4), lambda i: (i, 0)), pl.BlockSpec(w3.shape, lambda i: (0, 0, 0))],
        out_specs=pl.BlockSpec((tt, k_dim), lambda i: (i, 0)),
        out_shape=jax.ShapeDtypeStruct((t_dim, k_dim), out_dtype),
        compiler_params=_params(("parallel",)),
    )(dy, w3)


def _mm_sm_dw(a, dy, *, name, tk=1024):
    t_dim, k_dim = a.shape
    n4 = dy.shape[1] // N_CHIPS

    def body(a_ref, dy_ref, o_ref):
        av = a_ref[...].astype(BF16)
        dyv = dy_ref[...].astype(BF16)

        @pl.when(pl.program_id(0) == 0)
        def _():
            o_ref[...] = jnp.zeros_like(o_ref)

        for s in range(N_CHIPS):
            o_ref[s] += lax.dot_general(av, dyv[:, s * n4:(s + 1) * n4], TN, preferred_element_type=F32)

    return pl.pallas_call(
        body, name=name, grid=(t_dim // tk,),
        in_specs=[pl.BlockSpec((tk, k_dim), lambda i: (i, 0)), pl.BlockSpec((tk, N_CHIPS * n4), lambda i: (i, 0))],
        out_specs=pl.BlockSpec((N_CHIPS, k_dim, n4), lambda i: (0, 0, 0)),
        out_shape=jax.ShapeDtypeStruct((N_CHIPS, k_dim, n4), F32),
        compiler_params=_params(("arbitrary",)),
    )(a, dy)


def _ffn_up_dx(dg3, du3, wg3, wu3, *, tt=512):
    n_s, t_dim, f4 = dg3.shape
    d = wg3.shape[2]

    def body(dg_ref, du_ref, wg_ref, wu_ref, o_ref):
        acc = None
        for s in range(n_s):
            part = (jnp.dot(dg_ref[s], wg_ref[s], preferred_element_type=F32)
                    + jnp.dot(du_ref[s], wu_ref[s], preferred_element_type=F32))
            acc = part if acc is None else acc + part
        o_ref[...] = acc.astype(BF16)

    a_spec = pl.BlockSpec((n_s, tt, f4), lambda i: (0, i, 0))
    w_spec = pl.BlockSpec(wg3.shape, lambda i: (0, 0, 0))
    return pl.pallas_call(
        body, name="ffn_up_dx", grid=(t_dim // tt,),
        in_specs=[a_spec, a_spec, w_spec, w_spec], out_specs=pl.BlockSpec((tt, d), lambda i: (i, 0)),
        out_shape=jax.ShapeDtypeStruct((t_dim, d), BF16),
        compiler_params=_params(("parallel",)),
    )(dg3, du3, wg3, wu3)


def _mm_ffn_down_dw(act3, dh, *, name, tk=1024):
    n_s, t_dim, f4 = act3.shape
    d = dh.shape[1]

    def body(a_ref, b_ref, o_ref):
        bv = b_ref[...].astype(BF16)

        @pl.when(pl.program_id(0) == 0)
        def _():
            o_ref[...] = jnp.zeros_like(o_ref)

        for s in range(n_s):
            o_ref[s] += lax.dot_general(a_ref[s], bv, TN, preferred_element_type=F32)

    return pl.pallas_call(
        body, name=name, grid=(t_dim // tk,),
        in_specs=[pl.BlockSpec((n_s, tk, f4), lambda i: (0, i, 0)), pl.BlockSpec((tk, d), lambda i: (i, 0))],
        out_specs=pl.BlockSpec((n_s, f4, d), lambda i: (0, 0, 0)),
        out_shape=jax.ShapeDtypeStruct((n_s, f4, d), F32),
        compiler_params=_params(("arbitrary",)),
    )(act3, dh)


def _rms_fwd(x, g, *, name, tt=512, rider=None):
    t_dim, d = x.shape
    tt = _tile(t_dim, tt, 8)
    extra, extra_shapes, extra_sems = _rider_parts(rider)
    n_w = len(extra)

    def body(*refs):
        x_ref, g_ref, o_ref = refs[0], refs[1], refs[2 + n_w]
        begin, end = _rider_hooks(rider, refs[2:2 + n_w], refs[3 + n_w:3 + 2 * n_w], refs[-2:], pl.program_id(0),
                                  t_dim // tt)
        begin()
        xv = x_ref[...]
        r = lax.rsqrt(jnp.mean(xv * xv, axis=-1, keepdims=True) + RMS_EPS)
        o_ref[...] = ((xv * r) * g_ref[...]).astype(o_ref.dtype)
        end()

    res = pl.pallas_call(
        body, name=name, grid=(t_dim // tt,),
        in_specs=[pl.BlockSpec((tt, d), lambda i: (i, 0)), pl.BlockSpec((1, d), lambda i: (0, 0))] + [ANY] * n_w,
        out_specs=[pl.BlockSpec((tt, d), lambda i: (i, 0))] + [ANY] * n_w,
        out_shape=[jax.ShapeDtypeStruct((t_dim, d), BF16)] + extra_shapes,
        scratch_shapes=extra_sems,
        compiler_params=_params(("arbitrary",) if n_w else ("parallel",)),
    )(x, g, *extra)
    return (res[0], res[1:]) if n_w else res[0]


def _rms_bwd(x, g, dy, add, *, name, tt=512):
    t_dim, d = x.shape
    tt = _tile(t_dim, tt, 8)
    has_add = add is not None

    def body(*refs):
        x_ref, g_ref, dy_ref = refs[:3]
        add_ref = refs[3] if has_add else None
        dx_ref, dg_ref = refs[-2:]
        xv = x_ref[...]
        dyv = dy_ref[...].astype(F32)
        r = lax.rsqrt(jnp.mean(xv * xv, axis=-1, keepdims=True) + RMS_EPS)
        xh = xv * r
        u = dyv * g_ref[...]
        dx = r * (u - xh * jnp.mean(u * xh, axis=-1, keepdims=True))
        if has_add:
            dx = add_ref[...] + dx
        dx_ref[...] = dx

        @pl.when(pl.program_id(0) == 0)
        def _():
            dg_ref[...] = jnp.zeros_like(dg_ref)

        dg_ref[...] += jnp.sum(dyv * xh, axis=0, keepdims=True)

    row = pl.BlockSpec((tt, d), lambda i: (i, 0))
    vec = pl.BlockSpec((1, d), lambda i: (0, 0))
    in_specs = [row, vec, row] + ([row] if has_add else [])
    args = (x, g, dy) + ((add,) if has_add else ())
    return pl.pallas_call(
        body, name=name, grid=(t_dim // tt,),
        in_specs=in_specs, out_specs=[row, vec],
        out_shape=[jax.ShapeDtypeStruct((t_dim, d), F32), jax.ShapeDtypeStruct((1, d), F32)],
        compiler_params=_params(("arbitrary",)),
    )(*args)


def _down_final(act3, wd3, h, g, target, *, tt=512):
    n_s, t_dim, f4 = act3.shape
    d = h.shape[1]
    n_steps = t_dim // tt

    def body(a_ref, w_ref, h_ref, g_ref, t_ref, loss_ref, dh_ref, dg_ref, sq_ref):
        i = pl.program_id(0)
        xv = h_ref[...]
        for s in range(n_s):
            xv = xv + jnp.dot(a_ref[s], w_ref[s], preferred_element_type=F32)
        gv = g_ref[...]
        r = lax.rsqrt(jnp.mean(xv * xv, axis=-1, keepdims=True) + RMS_EPS)
        xh = xv * r
        err = xh * gv - t_ref[...]
        dyv = err * (1.0 / d)
        u = dyv * gv
        dh_ref[...] = r * (u - xh * jnp.mean(u * xh, axis=-1, keepdims=True))

        @pl.when(i == 0)
        def _():
            dg_ref[...] = jnp.zeros_like(dg_ref)
            sq_ref[...] = jnp.zeros_like(sq_ref)

        dg_ref[...] += jnp.sum(dyv * xh, axis=0, keepdims=True)
        sq_ref[...] += jnp.sum(err * err, axis=0, keepdims=True)

        @pl.when(i == n_steps - 1)
        def _():
            total = jnp.sum(sq_ref[...], axis=-1, keepdims=True) * (0.5 / d)
            loss_ref[...] = jnp.broadcast_to(total, loss_ref.shape)

    row = pl.BlockSpec((tt, d), lambda i: (i, 0))
    vec = pl.BlockSpec((1, d), lambda i: (0, 0))
    return pl.pallas_call(
        body, name="down_final_loss", grid=(n_steps,),
        in_specs=[pl.BlockSpec((n_s, tt, f4), lambda i: (0, i, 0)), pl.BlockSpec(wd3.shape, lambda i: (0, 0, 0)),
                  row, vec, row],
        out_specs=[pl.BlockSpec((1, LANES), lambda i: (0, 0)), row, vec],
        out_shape=[jax.ShapeDtypeStruct((1, LANES), F32), jax.ShapeDtypeStruct((t_dim, d), F32),
                   jax.ShapeDtypeStruct((1, d), F32)],
        scratch_shapes=[pltpu.VMEM((1, d), F32)],
        compiler_params=_params(("arbitrary",)),
    )(act3, wd3, h, g, target)


def _rope_table(pos, inv_lane, sel_a, sel_b, *, tt=512):
    t_dim = pos.shape[0]

    def body(p_ref, f_ref, a_ref, b_ref, c_ref, s_ref):
        ang = p_ref[...] * f_ref[...]
        on = (a_ref[...] + b_ref[...]) > 0.0
        c_ref[...] = jnp.where(on, jnp.cos(ang), 1.0)
        s_ref[...] = jnp.where(on, jnp.sin(ang), 0.0)

    vec = pl.BlockSpec((1, LANES), lambda i: (0, 0))
    row = pl.BlockSpec((tt, LANES), lambda i: (i, 0))
    shp = jax.ShapeDtypeStruct((t_dim, LANES), F32)
    return pl.pallas_call(
        body, name="rope_table", grid=(t_dim // tt,),
        in_specs=[pl.BlockSpec((tt, 1), lambda i: (i, 0)), vec, vec, vec],
        out_specs=[row, row], out_shape=[shp, shp],
        compiler_params=_params(("parallel",)),
    )(pos, inv_lane, sel_a, sel_b)


def _rotate(xv, cs, sn, sa, sb):
    half = ROPE_DIM // 2
    up = pltpu.roll(xv, LANES - half, 1)
    dn = pltpu.roll(xv, half, 1)
    return xv * cs + (dn * sb - up * sa) * sn


def _head_masks():
    h1 = lax.broadcasted_iota(jnp.int32, (1, LANES), 1) < HEAD_DIM
    return h1, jnp.logical_not(h1)


def _split_heads(xv, h1, h2):
    return jnp.where(h1, xv, 0.0).astype(BF16), jnp.where(h2, xv, 0.0).astype(BF16)


def _tri_masks():
    r = lax.broadcasted_iota(jnp.int32, (BLOCK, BLOCK), 0)
    c = lax.broadcasted_iota(jnp.int32, (BLOCK, BLOCK), 1)
    return c <= r, r <= c


def _stream_rows(start, dil):
    if dil == 1:
        return pl.ds(pl.multiple_of(start, BLOCK), BLOCK)
    return pl.ds(start, BLOCK, stride=dil)


def _dil_tile(idx, dil, nb):
    r = idx // nb
    n = idx % nb
    return (_stream_rows(r + dil * BLOCK * n, dil), _stream_rows(r + dil * BLOCK * jnp.maximum(n - 1, 0), dil),
            n > 0)


def _dil_specs(b_dim, s_dim):
    def col(c0):
        return pl.BlockSpec((None, s_dim, LANES),lambda b, h: (b, 0, c0 + h))
    tab = pl.BlockSpec((None, s_dim, LANES),lambda b, h: (b, 0, 0))
    vec = pl.BlockSpec((1, LANES), lambda b, h: (0, 0))
    return col, tab, vec


def _dil_fwd(proj3, cs3, sn3, sel_a, sel_b, rider=None):
    b_dim, s_dim, _ = proj3.shape
    scale = HEAD_DIM ** -0.5
    n_pat = len(DIL_PATTERNS)
    extra, extra_shapes, extra_sems = _rider_parts(rider)
    n_w = len(extra)
    n_steps = b_dim * PAIRS

    def body(*refs):
        q_ref, k_ref, v_ref, cs_ref, sn_ref, sa_ref, sb_ref = refs[:7]
        o16_ref, o32_ref, l_ref = refs[7 + n_w:10 + n_w]
        qr, kr = refs[10 + 2 * n_w:12 + 2 * n_w]
        per_pattern = refs[12 + 2 * n_w:12 + 2 * n_w + 2 * n_pat]
        og, lg = per_pattern[:n_pat], per_pattern[n_pat:]
        step = pl.program_id(0) * PAIRS + pl.program_id(1)
        begin, end = _rider_hooks(rider, refs[7:7 + n_w], refs[10 + n_w:10 + 2 * n_w], refs[-2:], step, n_steps)
        begin()
        h1, h2 = _head_masks()
        cur_ok, prev_ok = _tri_masks()
        sa, sb = sa_ref[...], sb_ref[...]

        def prep(j, _):
            rows = pl.ds(pl.multiple_of(j * DIL_CHUNK, DIL_CHUNK), DIL_CHUNK)
            cs, sn = cs_ref[rows, :], sn_ref[rows, :]
            qr[rows, :] = _rotate(q_ref[rows, :], cs, sn, sa, sb) * scale
            kr[rows, :] = _rotate(k_ref[rows, :], cs, sn, sa, sb)
            return 0

        lax.fori_loop(0, s_dim // DIL_CHUNK, prep, 0)

        for g, (_, dil) in enumerate(DIL_PATTERNS):
            nb = s_dim // dil // BLOCK

            def some(bi, _, g=g, dil=dil, nb=nb):
                tiles = [_dil_tile(bi * DIL_BATCH + t, dil, nb) for t in range(DIL_BATCH)]
                rows = [t[0] for t in tiles]
                q1, q2 = _split_heads(jnp.stack([qr[rw, :] for rw in rows]), h1, h2)
                kc = jnp.stack([kr[rw, :] for rw in rows]).astype(BF16)
                vc1, vc2 = _split_heads(jnp.stack([v_ref[rw, :] for rw in rows]), h1, h2)
                if nb > 1:
                    kp = jnp.stack([kr[t[1], :] for t in tiles]).astype(BF16)
                    vp1, vp2 = _split_heads(jnp.stack([v_ref[t[1], :] for t in tiles]), h1, h2)
                    p_ok = jnp.stack([jnp.logical_and(prev_ok, t[2]) for t in tiles])

                def head(qh, vch, vph):
                    sc = jnp.where(cur_ok, lax.dot_general(qh, kc, BNT, preferred_element_type=F32), -jnp.inf)
                    m = jnp.max(sc, axis=-1, keepdims=True)
                    if nb > 1:
                        sp = jnp.where(p_ok, lax.dot_general(qh, kp, BNT, preferred_element_type=F32), -jnp.inf)
                        m = jnp.maximum(m, jnp.max(sp, axis=-1, keepdims=True))
                    pc = jnp.exp(sc - m)
                    den = jnp.sum(pc, axis=-1, keepdims=True)
                    acc = lax.dot_general(pc.astype(BF16), vch, BNN, preferred_element_type=F32)
                    if nb > 1:
                        pp = jnp.exp(sp - m)
                        den = den + jnp.sum(pp, axis=-1, keepdims=True)
                        acc = acc + lax.dot_general(pp.astype(BF16), vph, BNN, preferred_element_type=F32)
                    return acc / den, m + jnp.log(den)

                o1, l1 = head(q1, vc1, vp1 if nb > 1 else None)
                o2, l2 = head(q2, vc2, vp2 if nb > 1 else None)
                o, l = o1 + o2, jnp.where(h1, l1, l2)
                for t, rw in enumerate(rows):
                    og[g][rw, :] = o[t]
                    lg[g][rw, :] = l[t]
                return 0

            lax.fori_loop(0, dil * nb // DIL_BATCH, some, 0)

        def comb(j, _):
            rows = pl.ds(pl.multiple_of(j * DIL_CHUNK, DIL_CHUNK), DIL_CHUNK)
            ls = [lg[g][rows, :] for g in range(n_pat)]
            m = jnp.maximum(jnp.maximum(ls[0], ls[1]), ls[2])
            es = [jnp.exp(l - m) for l in ls]
            den = es[0] + es[1] + es[2]
            o = (es[0] * og[0][rows, :] + es[1] * og[1][rows, :] + es[2] * og[2][rows, :]) / den
            o16_ref[rows, :] = o.astype(BF16)
            o32_ref[rows, :] = o
            l_ref[rows, :] = m + jnp.log(den)
            return 0

        lax.fori_loop(0, s_dim // DIL_CHUNK, comb, 0)
        end()

    col, tab, vec = _dil_specs(b_dim, s_dim)
    out = pl.BlockSpec((None, s_dim, LANES),lambda b, h: (b, 0, h))
    shp = (b_dim, s_dim, ATT_WIDTH)
    res = pl.pallas_call(
        body, name="dil_fwd", grid=(b_dim, PAIRS),
        in_specs=[col(COL_QA), col(COL_KA), col(COL_VA), tab, tab, vec, vec] + [ANY] * n_w,
        out_specs=[out, out, out] + [ANY] * n_w,
        out_shape=[jax.ShapeDtypeStruct(shp, BF16), jax.ShapeDtypeStruct(shp, F32), jax.ShapeDtypeStruct(shp, F32)]
        + extra_shapes,
        scratch_shapes=[pltpu.VMEM((s_dim, LANES), F32)] * (2 + 2 * n_pat) + extra_sems,
        compiler_params=_params(("arbitrary", "arbitrary")),
    )(proj3, proj3, proj3, cs3, sn3, sel_a, sel_b, *extra)
    return res[:3], res[3:]


def _dil_bwd(proj3, cs3, sn3, sel_a, sel_b, do3, o3, lse3, rider=None):
    b_dim, s_dim, _ = proj3.shape
    scale = HEAD_DIM ** -0.5
    extra, extra_shapes, extra_sems = _rider_parts(rider)
    n_w = len(extra)

    def body(*refs):
        q_ref, k_ref, v_ref, cs_ref, sn_ref, sa_ref, sb_ref, do_ref, o_ref, l_ref = refs[:10]
        dq_ref, dk_ref, dv_ref = refs[10 + n_w:13 + n_w]
        qr, kr, dqa, dka, dva = refs[13 + 2 * n_w:18 + 2 * n_w]
        step = pl.program_id(0) * PAIRS + pl.program_id(1)
        begin, end = _rider_hooks(rider, refs[10:10 + n_w], refs[13 + n_w:13 + 2 * n_w], refs[-2:], step,
                                  b_dim * PAIRS)
        begin()
        h1, h2 = _head_masks()
        cur_ok, prev_ok = _tri_masks()
        sa, sb = sa_ref[...], sb_ref[...]

        def prep(j, _):
            rows = pl.ds(pl.multiple_of(j * DIL_CHUNK, DIL_CHUNK), DIL_CHUNK)
            cs, sn = cs_ref[rows, :], sn_ref[rows, :]
            qr[rows, :] = _rotate(q_ref[rows, :], cs, sn, sa, sb) * scale
            kr[rows, :] = _rotate(k_ref[rows, :], cs, sn, sa, sb)
            zero = jnp.zeros((DIL_CHUNK, LANES), F32)
            dqa[rows, :] = zero
            dka[rows, :] = zero
            dva[rows, :] = zero
            return 0

        lax.fori_loop(0, s_dim // DIL_CHUNK, prep, 0)

        for _, dil in DIL_PATTERNS:
            nb = s_dim // dil // BLOCK

            def some(bi, _, dil=dil, nb=nb):
                tiles = [_dil_tile(bi * DIL_BATCH + t, dil, nb) for t in range(DIL_BATCH)]
                rows = [t[0] for t in tiles]
                q1, q2 = _split_heads(jnp.stack([qr[rw, :] for rw in rows]), h1, h2)
                dof = jnp.stack([do_ref[rw, :] for rw in rows])
                do1, do2 = _split_heads(dof, h1, h2)
                prod = dof * jnp.stack([o_ref[rw, :] for rw in rows])
                delta1 = jnp.sum(jnp.where(h1, prod, 0.0), axis=-1, keepdims=True)
                delta2 = jnp.sum(jnp.where(h2, prod, 0.0), axis=-1, keepdims=True)
                lt = jnp.stack([l_ref[rw, :] for rw in rows])
                lse1 = jnp.max(jnp.where(h1, lt, -jnp.inf), axis=-1, keepdims=True)
                lse2 = jnp.max(jnp.where(h2, lt, -jnp.inf), axis=-1, keepdims=True)

                def side(krows, ok):
                    kf = jnp.stack([kr[kw, :] for kw in krows])
                    k16 = kf.astype(BF16)
                    k1, k2 = _split_heads(kf, h1, h2)
                    v16 = jnp.stack([v_ref[kw, :] for kw in krows]).astype(BF16)

                    def head(qh, doh, lse, delta):
                        sc = lax.dot_general(qh, k16, BNT, preferred_element_type=F32)
                        p = jnp.where(ok, jnp.exp(sc - lse), 0.0)
                        dp = lax.dot_general(doh, v16, BNT, preferred_element_type=F32)
                        return p.astype(BF16), (p * (dp - delta)).astype(BF16)

                    p1, ds1 = head(q1, do1, lse1, delta1)
                    p2, ds2 = head(q2, do2, lse2, delta2)
                    dv = (lax.dot_general(p1, do1, BTN, preferred_element_type=F32)
                          + lax.dot_general(p2, do2, BTN, preferred_element_type=F32))
                    dk = (lax.dot_general(ds1, q1, BTN, preferred_element_type=F32)
                          + lax.dot_general(ds2, q2, BTN, preferred_element_type=F32))
                    for t, kw in enumerate(krows):
                        dva[kw, :] += dv[t]
                        dka[kw, :] += dk[t]
                    return (lax.dot_general(ds1, k1, BNN, preferred_element_type=F32)
                            + lax.dot_general(ds2, k2, BNN, preferred_element_type=F32))

                dq = side(rows, cur_ok)
                if nb > 1:
                    dq = dq + side([t[1] for t in tiles], jnp.stack([jnp.logical_and(prev_ok, t[2]) for t in tiles]))
                for t, rw in enumerate(rows):
                    dqa[rw, :] += dq[t] * scale
                return 0

            lax.fori_loop(0, dil * nb // DIL_BATCH, some, 0)

        def finish(j, _):
            rows = pl.ds(pl.multiple_of(j * DIL_CHUNK, DIL_CHUNK), DIL_CHUNK)
            cs, sn = cs_ref[rows, :], -sn_ref[rows, :]
            dq_ref[rows, :] = _rotate(dqa[rows, :], cs, sn, sa, sb).astype(BF16)
            dk_ref[rows, :] = _rotate(dka[rows, :], cs, sn, sa, sb).astype(BF16)
            dv_ref[rows, :] = dva[rows, :].astype(BF16)
            return 0

        lax.fori_loop(0, s_dim // DIL_CHUNK, finish, 0)
        end()

    col, tab, vec = _dil_specs(b_dim, s_dim)
    out = pl.BlockSpec((None, s_dim, LANES),lambda b, h: (b, 0, h))
    shp = jax.ShapeDtypeStruct((b_dim, s_dim, ATT_WIDTH), BF16)
    acc = pltpu.VMEM((s_dim, LANES), F32)
    res = pl.pallas_call(
        body, name="dil_bwd", grid=(b_dim, PAIRS),
        in_specs=[col(COL_QA), col(COL_KA), col(COL_VA), tab, tab, vec, vec, out, out, out] + [ANY] * n_w,
        out_specs=[out, out, out] + [ANY] * n_w, out_shape=[shp, shp, shp] + extra_shapes,
        scratch_shapes=[acc, acc, acc, acc, acc] + extra_sems,
        compiler_params=_params(("arbitrary", "arbitrary")),
    )(proj3, proj3, proj3, cs3, sn3, sel_a, sel_b, do3, o3, lse3, *extra)
    return res[:3], res[3:]


def _split_dot(x, tri):
    hi = x.astype(BF16)
    lo = (x - hi.astype(F32)).astype(BF16)
    return jnp.dot(hi, tri, preferred_element_type=F32) + jnp.dot(lo, tri, preferred_element_type=F32)


def _log_sigmoid(z):
    return jnp.minimum(z, 0.0) - jnp.log(1.0 + jnp.exp(-jnp.abs(z)))


def _sb_scores(qh, k16, valid):
    z = lax.dot_general(qh, k16, NT, preferred_element_type=F32)
    ls = _log_sigmoid(z)
    l1m = ls - z
    return ls, (l1m if valid is None else jnp.where(valid, l1m, 0.0))


def _sb_consts():
    r = lax.broadcasted_iota(jnp.int32, (BLOCK, BLOCK), 0)
    c = lax.broadcasted_iota(jnp.int32, (BLOCK, BLOCK), 1)
    after = (r > c).astype(BF16)
    before = (r < c).astype(BF16)
    qrow = lax.broadcasted_iota(jnp.int32, (SB_ROWS, BLOCK), 0)
    kcol = lax.broadcasted_iota(jnp.int32, (SB_ROWS, BLOCK), 1)
    return after, before, qrow, kcol


def _below(whole, lo, delta):
    if lo == 0:
        return whole + delta
    return whole + jnp.concatenate([jnp.zeros((lo,) + delta.shape[1:], delta.dtype), delta], axis=0)


def _pairs_loop(n_blocks, step, carry, per_iter):
    def several(i, c):
        for j in range(per_iter):
            c = step(per_iter * i + j, c)
        return c

    return lax.fori_loop(0, n_blocks // per_iter, several, carry)


def _sb_fwd(proj3, rider=None):
    b_dim, s_dim, _ = proj3.shape
    scale = HEAD_DIM ** -0.5
    per = SB_ROWS // BLOCK
    extra, extra_shapes, extra_sems = _rider_parts(rider)
    n_w = len(extra)

    def body(*refs):
        q_ref, k_ref, v_ref = refs[:3]
        o_ref = refs[3 + n_w]
        step = pl.program_id(0) * PAIRS + pl.program_id(1)
        begin, end = _rider_hooks(rider, refs[3:3 + n_w], refs[4 + n_w:4 + 2 * n_w], refs[-2:], step, b_dim * PAIRS)
        begin()
        h1, h2 = _head_masks()
        after, _, qrow, kcol = _sb_consts()

        def qloop(qi, _):
            rows = pl.ds(pl.multiple_of(qi * SB_ROWS, SB_ROWS), SB_ROWS)
            q1, q2 = _split_heads(q_ref[rows, :] * scale, h1, h2)
            first = qi * per

            def block(kb, carry, lo):
                acc, run1, run2 = carry
                krows = pl.ds(pl.multiple_of(kb * BLOCK, BLOCK), BLOCK)
                k16 = k_ref[krows, :].astype(BF16)
                v1, v2 = _split_heads(v_ref[krows, :], h1, h2)
                valid = None if lo is None else kcol[:SB_ROWS - lo] < qrow[:SB_ROWS - lo]
                lo = lo or 0

                def head(qh, vh, run):
                    ls, l1m = _sb_scores(qh[lo:], k16, valid)
                    a = jnp.exp(ls + _split_dot(l1m, after) + run[lo:])
                    if valid is not None:
                        a = jnp.where(valid, a, 0.0)
                    return (jnp.dot(a.astype(BF16), vh, preferred_element_type=F32),
                            _below(run, lo, jnp.sum(l1m, axis=-1, keepdims=True)))

                o1, run1 = head(q1, v1, run1)
                o2, run2 = head(q2, v2, run2)
                return _below(acc, lo, o1 + o2), run1, run2

            zcol = jnp.zeros((SB_ROWS, 1), F32)
            carry = (jnp.zeros((SB_ROWS, LANES), F32), zcol, zcol)
            for kl in reversed(range(per)):
                carry = block(first + kl, carry, kl * BLOCK)
            acc, _, _ = _pairs_loop(first, lambda i, c: block(first - 1 - i, c, None), carry, SB_STEP_FWD)
            o_ref[rows, :] = acc.astype(BF16)
            return 0

        lax.fori_loop(0, s_dim // SB_ROWS, qloop, 0)
        end()

    def col(c0):
        return pl.BlockSpec((None, s_dim, LANES),lambda b, h: (b, 0, c0 + h))

    res = pl.pallas_call(
        body, name="sb_fwd", grid=(b_dim, PAIRS),
        in_specs=[col(COL_QB), col(COL_KB), col(COL_VB)] + [ANY] * n_w, out_specs=[col(0)] + [ANY] * n_w,
        out_shape=[jax.ShapeDtypeStruct((b_dim, s_dim, ATT_WIDTH), BF16)] + extra_shapes,
        scratch_shapes=extra_sems,
        compiler_params=_params(("arbitrary", "arbitrary")),
    )(proj3, proj3, proj3, *extra)
    return res[0], res[1:]


def _sb_bwd(proj3, do3, rider=None):
    b_dim, s_dim, _ = proj3.shape
    scale = HEAD_DIM ** -0.5
    per = SB_ROWS // BLOCK
    nkb_max = s_dim // BLOCK
    extra, extra_shapes, extra_sems = _rider_parts(rider)
    n_w = len(extra)

    def body(*refs):
        q_ref, k_ref, v_ref, do_ref = refs[:4]
        dq_ref, dk_ref, dv_ref = refs[4 + n_w:7 + n_w]
        dka, dva, e_ref, sg_ref = refs[7 + 2 * n_w:11 + 2 * n_w]
        step = pl.program_id(0) * PAIRS + pl.program_id(1)
        begin, end = _rider_hooks(rider, refs[4:4 + n_w], refs[7 + n_w:7 + 2 * n_w], refs[-2:], step, b_dim * PAIRS)
        begin()
        h1, h2 = _head_masks()
        after, before, qrow, kcol = _sb_consts()
        dka[...] = jnp.zeros_like(dka)
        dva[...] = jnp.zeros_like(dva)

        def qloop(qi, _):
            rows = pl.ds(pl.multiple_of(qi * SB_ROWS, SB_ROWS), SB_ROWS)
            q1, q2 = _split_heads(q_ref[rows, :] * scale, h1, h2)
            do1, do2 = _split_heads(do_ref[rows, :].astype(F32), h1, h2)
            first = qi * per

            def pass1(kb, carry, lo):
                run1, run2 = carry
                krows = pl.ds(pl.multiple_of(kb * BLOCK, BLOCK), BLOCK)
                k16 = k_ref[krows, :].astype(BF16)
                v16 = v_ref[krows, :].astype(BF16)
                valid = None if lo is None else kcol[:SB_ROWS - lo] < qrow[:SB_ROWS - lo]
                lo = lo or 0
                part = pl.ds(lo, SB_ROWS - lo)

                def head(h, qh, doh, run):
                    ls, l1m = _sb_scores(qh[lo:], k16, valid)
                    a = jnp.exp(ls + _split_dot(l1m, after) + run[lo:])
                    if valid is not None:
                        a = jnp.where(valid, a, 0.0)
                    da = lax.dot_general(doh[lo:], v16, NT, preferred_element_type=F32)
                    e_ref[h, kb, part, :] = a * da
                    sg_ref[h, kb, part, :] = jnp.exp(ls)
                    return a.astype(BF16), _below(run, lo, jnp.sum(l1m, axis=-1, keepdims=True))

                a1, run1 = head(0, q1, do1, run1)
                a2, run2 = head(1, q2, do2, run2)
                dva[krows, :] += (lax.dot_general(a1, do1[lo:], TN, preferred_element_type=F32)
                                  + lax.dot_general(a2, do2[lo:], TN, preferred_element_type=F32))
                return run1, run2

            zcol = jnp.zeros((SB_ROWS, 1), F32)
            carry = (zcol, zcol)
            for kl in reversed(range(per)):
                carry = pass1(first + kl, carry, kl * BLOCK)
            _pairs_loop(first, lambda i, c: pass1(first - 1 - i, c, None), carry, SB_STEP_BWD)

            def pass2(kb, carry, lo):
                dq, pre1, pre2 = carry
                krows = pl.ds(pl.multiple_of(kb * BLOCK, BLOCK), BLOCK)
                k1, k2 = _split_heads(k_ref[krows, :], h1, h2)
                valid = None if lo is None else kcol[:SB_ROWS - lo] < qrow[:SB_ROWS - lo]
                lo = lo or 0
                part = pl.ds(lo, SB_ROWS - lo)

                def head(h, pre):
                    ev = e_ref[h, kb, part, :]
                    sg = sg_ref[h, kb, part, :]
                    dz = ev * (1.0 - sg) - (_split_dot(ev, before) + pre[lo:]) * sg
                    if valid is not None:
                        dz = jnp.where(valid, dz, 0.0)
                    return dz.astype(BF16), _below(pre, lo, jnp.sum(ev, axis=-1, keepdims=True))

                dz1, pre1 = head(0, pre1)
                dz2, pre2 = head(1, pre2)
                dka[krows, :] += (lax.dot_general(dz1, q1[lo:], TN, preferred_element_type=F32)
                                  + lax.dot_general(dz2, q2[lo:], TN, preferred_element_type=F32))
                dq = _below(dq, lo, jnp.dot(dz1, k1, preferred_element_type=F32)
                            + jnp.dot(dz2, k2, preferred_element_type=F32))
                return dq, pre1, pre2

            carry = _pairs_loop(first, lambda i, c: pass2(i, c, None), (jnp.zeros((SB_ROWS, LANES), F32), zcol, zcol),
                                SB_STEP_BWD)
            for kl in range(per):
                carry = pass2(first + kl, carry, kl * BLOCK)
            dq = carry[0]
            dq_ref[rows, :] = (dq * scale).astype(BF16)
            return 0

        lax.fori_loop(0, s_dim // SB_ROWS, qloop, 0)
        dk_ref[...] = dka[...].astype(BF16)
        dv_ref[...] = dva[...].astype(BF16)
        end()

    def col(c0):
        return pl.BlockSpec((None, s_dim, LANES),lambda b, h: (b, 0, c0 + h))

    shp = jax.ShapeDtypeStruct((b_dim, s_dim, ATT_WIDTH), BF16)
    acc = pltpu.VMEM((s_dim, LANES), F32)
    strip = pltpu.VMEM((2, nkb_max, SB_ROWS, BLOCK), F32)
    res = pl.pallas_call(
        body, name="sb_bwd", grid=(b_dim, PAIRS),
        in_specs=[col(COL_QB), col(COL_KB), col(COL_VB), col(0)] + [ANY] * n_w,
        out_specs=[col(0), col(0), col(0)] + [ANY] * n_w,
        out_shape=[shp, shp, shp] + extra_shapes,
        scratch_shapes=[acc, acc, strip, strip] + extra_sems,
        compiler_params=_params(("arbitrary", "arbitrary")),
    )(proj3, proj3, proj3, do3, *extra)
    return res[:3], res[3:]


def _sigmoid(x):
    return 1.0 / (1.0 + jnp.exp(-x))


def _gate_out_norm(proj, ua, ub, w_out, x, g, *, tt=512):
    t_dim, d = ua.shape

    def body(ga_ref, gb_ref, ua_ref, ub_ref, w_ref, x_ref, g_ref, m_ref, h_ref, n_ref):
        mixed = (_sigmoid(ga_ref[...]) * ua_ref[...] + _sigmoid(gb_ref[...]) * ub_ref[...]).astype(BF16)
        m_ref[...] = mixed
        hv = x_ref[...] + jnp.dot(mixed, w_ref[...], preferred_element_type=F32)
        h_ref[...] = hv
        r = lax.rsqrt(jnp.mean(hv * hv, axis=-1, keepdims=True) + RMS_EPS)
        n_ref[...] = ((hv * r) * g_ref[...]).astype(BF16)

    row = pl.BlockSpec((tt, d), lambda i: (i, 0))
    return pl.pallas_call(
        body, name="gate_out_norm", grid=(t_dim // tt,),
        in_specs=[pl.BlockSpec((tt, d), lambda i: (i, 3)), pl.BlockSpec((tt, d), lambda i: (i, 4)), row, row,
                  pl.BlockSpec((d, d), lambda i: (0, 0)), row, pl.BlockSpec((1, d), lambda i: (0, 0))],
        out_specs=[row, row, row],
        out_shape=[jax.ShapeDtypeStruct((t_dim, d), BF16), jax.ShapeDtypeStruct((t_dim, d), F32),
                   jax.ShapeDtypeStruct((t_dim, d), BF16)],
        compiler_params=_params(("parallel",)),
    )(proj, proj, ua, ub, w_out, x, g)


def _out_dx_gate_bwd(dh, w_out, proj, ua, ub, *, tt=512):
    t_dim, d = ua.shape

    def body(dh_ref, w_ref, ga_ref, gb_ref, ua_ref, ub_ref, dua_ref, dub_ref, dg_ref):
        dm = lax.dot_general(dh_ref[...].astype(BF16), w_ref[...], NT, preferred_element_type=F32)
        sa = _sigmoid(ga_ref[...])
        sb = _sigmoid(gb_ref[...])
        dua_ref[...] = (dm * sa).astype(BF16)
        dub_ref[...] = (dm * sb).astype(BF16)
        dg_ref[:, :d] = (dm * ua_ref[...] * (sa * (1.0 - sa))).astype(BF16)
        dg_ref[:, d:] = (dm * ub_ref[...] * (sb * (1.0 - sb))).astype(BF16)

    row = pl.BlockSpec((tt, d), lambda i: (i, 0))
    wide = pl.BlockSpec((tt, 2 * d), lambda i: (i, 0))
    return pl.pallas_call(
        body, name="out_dx_gate_bwd", grid=(t_dim // tt,),
        in_specs=[row, pl.BlockSpec((d, d), lambda i: (0, 0)),
                  pl.BlockSpec((tt, d), lambda i: (i, 3)), pl.BlockSpec((tt, d), lambda i: (i, 4)), row, row],
        out_specs=[row, row, wide],
        out_shape=[jax.ShapeDtypeStruct((t_dim, d), BF16), jax.ShapeDtypeStruct((t_dim, d), BF16),
                   jax.ShapeDtypeStruct((t_dim, 2 * d), BF16)],
        compiler_params=_params(("parallel",)),
    )(dh, w_out, proj, proj, ua, ub)


def _ffn_up_swiglu(n, wg3, wu3, *, tt=512):
    t_dim, d = n.shape
    n_s, f4, _ = wg3.shape

    def body(n_ref, wg_ref, wu_ref, g_ref, u_ref, a_ref):
        nv = n_ref[...]
        for s in range(n_s):
            gv = lax.dot_general(nv, wg_ref[s], NT, preferred_element_type=F32)
            uv = lax.dot_general(nv, wu_ref[s], NT, preferred_element_type=F32)
            g_ref[s] = gv.astype(BF16)
            u_ref[s] = uv.astype(BF16)
            a_ref[s] = (gv * _sigmoid(gv) * uv).astype(BF16)

    wspec = pl.BlockSpec(wg3.shape, lambda i: (0, 0, 0))
    ospec = pl.BlockSpec((n_s, tt, f4), lambda i: (0, i, 0))
    shp = (n_s, t_dim, f4)
    return pl.pallas_call(
        body, name="ffn_up_swiglu", grid=(t_dim // tt,),
        in_specs=[pl.BlockSpec((tt, d), lambda i: (i, 0)), wspec, wspec], out_specs=[ospec, ospec, ospec],
        out_shape=[jax.ShapeDtypeStruct(shp, BF16)] * 3,
        compiler_params=_params(("parallel",)),
    )(n, wg3, wu3)


def _ffn_down_dx_swiglu(dh, wd3, g3, u3, *, tt=512):
    t_dim, d = dh.shape
    n_s, f4, _ = wd3.shape

    def body(dh_ref, w_ref, g_ref, u_ref, dg_ref, du_ref):
        dhv = dh_ref[...].astype(BF16)
        for s in range(n_s):
            da = lax.dot_general(dhv, w_ref[s], NT, preferred_element_type=F32)
            gv = g_ref[s].astype(F32)
            sg = _sigmoid(gv)
            dg_ref[s] = (da * u_ref[s].astype(F32) * (sg + gv * sg * (1.0 - sg))).astype(BF16)
            du_ref[s] = (da * (gv * sg)).astype(BF16)

    spec = pl.BlockSpec((n_s, tt, f4), lambda i: (0, i, 0))
    shp = jax.ShapeDtypeStruct((n_s, t_dim, f4), BF16)
    return pl.pallas_call(
        body, name="ffn_down_dx_swiglu", grid=(t_dim // tt,),
        in_specs=[pl.BlockSpec((tt, d), lambda i: (i, 0)), pl.BlockSpec(wd3.shape, lambda i: (0, 0, 0)), spec, spec],
        out_specs=[spec, spec], out_shape=[shp, shp],
        compiler_params=_params(("parallel",)),
    )(dh, wd3, g3, u3)


def _mem_fwd(qm, kvm, *, tt=2048):
    b_dim, s_dim, _ = qm.shape
    n_mem = kvm.shape[1]
    scale = MEM_HEAD_DIM ** -0.5

    def body(q_ref, k_ref, v_ref, o_ref):
        sc = lax.dot_general(q_ref[0], k_ref[0], NT, preferred_element_type=F32) * scale
        p = jnp.exp(sc - jnp.max(sc, axis=-1, keepdims=True))
        p = p / jnp.sum(p, axis=-1, keepdims=True)
        o_ref[0] = jnp.dot(p.astype(BF16), v_ref[0], preferred_element_type=F32).astype(BF16)

    qs = pl.BlockSpec((1, tt, MEM_HEAD_DIM), lambda b, h, i: (b, i, h))
    return pl.pallas_call(
        body, name="mem_fwd", grid=(b_dim, N_HEADS_MEM, s_dim // tt),
        in_specs=[qs, pl.BlockSpec((1, n_mem, MEM_HEAD_DIM), lambda b, h, i: (b, 0, h)),
                  pl.BlockSpec((1, n_mem, MEM_HEAD_DIM), lambda b, h, i: (b, 0, N_HEADS_MEM + h))],
        out_specs=qs, out_shape=jax.ShapeDtypeStruct(qm.shape, BF16),
        compiler_params=_params(("parallel", "parallel", "parallel")),
    )(qm, kvm, kvm)


def _mem_bwd(qm, kvm, dom, *, tt=2048):
    b_dim, s_dim, _ = qm.shape
    n_mem = kvm.shape[1]
    scale = MEM_HEAD_DIM ** -0.5

    def body(q_ref, k_ref, v_ref, do_ref, dq_ref, dk_ref, dv_ref):
        qv, kv, vv, dov = q_ref[0], k_ref[0], v_ref[0], do_ref[0]
        sc = lax.dot_general(qv, kv, NT, preferred_element_type=F32) * scale
        p = jnp.exp(sc - jnp.max(sc, axis=-1, keepdims=True))
        p = p / jnp.sum(p, axis=-1, keepdims=True)
        dp = lax.dot_general(dov, vv, NT, preferred_element_type=F32)
        ds = (p * (dp - jnp.sum(p * dp, axis=-1, keepdims=True)) * scale).astype(BF16)
        dq_ref[0] = jnp.dot(ds, kv, preferred_element_type=F32).astype(BF16)

        @pl.when(pl.program_id(2) == 0)
        def _():
            dk_ref[...] = jnp.zeros_like(dk_ref)
            dv_ref[...] = jnp.zeros_like(dv_ref)

        dk_ref[0] += lax.dot_general(ds, qv, TN, preferred_element_type=F32)
        dv_ref[0] += lax.dot_general(p.astype(BF16), dov, TN, preferred_element_type=F32)

    qs = pl.BlockSpec((1, tt, MEM_HEAD_DIM), lambda b, h, i: (b, i, h))
    ks = pl.BlockSpec((1, n_mem, MEM_HEAD_DIM), lambda b, h, i: (b, 0, h))
    vs = pl.BlockSpec((1, n_mem, MEM_HEAD_DIM), lambda b, h, i: (b, 0, N_HEADS_MEM + h))
    return pl.pallas_call(
        body, name="mem_bwd", grid=(b_dim, N_HEADS_MEM, s_dim // tt),
        in_specs=[qs, ks, vs, qs], out_specs=[qs, ks, ks],
        out_shape=[jax.ShapeDtypeStruct(qm.shape, BF16), jax.ShapeDtypeStruct((b_dim, n_mem, MEM_WIDTH), F32),
                   jax.ShapeDtypeStruct((b_dim, n_mem, MEM_WIDTH), F32)],
        compiler_params=_params(("parallel", "parallel", "arbitrary")),
    )(qm, kvm, kvm, dom)


def _adamw_math(wv, gv, mv, vv):
    nm = ADAM_B1 * mv + (1.0 - ADAM_B1) * gv
    nv = ADAM_B2 * vv + (1.0 - ADAM_B2) * (gv * gv)
    m_hat = nm / (1.0 - ADAM_B1 ** ADAM_STEP)
    v_hat = nv / (1.0 - ADAM_B2 ** ADAM_STEP)
    return -ADAM_LR * (m_hat / (jnp.sqrt(v_hat) + ADAM_EPS) + ADAM_WD * wv), nm, nv


def _adamw(w, g, m, v, *, name):
    rows, cols = w.shape
    tr = _tile(rows, 256, 8)

    def body(w_ref, g_ref, m_ref, v_ref, d_ref, nm_ref, nv_ref):
        d_ref[...], nm_ref[...], nv_ref[...] = _adamw_math(w_ref[...], g_ref[...], m_ref[...], v_ref[...])

    spec = pl.BlockSpec((tr, cols), lambda i: (i, 0))
    shp = jax.ShapeDtypeStruct((rows, cols), F32)
    return pl.pallas_call(
        body, name=name, grid=(rows // tr,),
        in_specs=[spec] * 4, out_specs=[spec] * 3, out_shape=[shp] * 3,
        compiler_params=_params(("parallel",)),
    )(w, g, m, v)


def _prefetch_spec(grid, in_specs, out_specs):
    return pltpu.PrefetchScalarGridSpec(num_scalar_prefetch=1, grid=grid, in_specs=in_specs, out_specs=out_specs)


def _adamw_halves(w, mine, theirs, m, v, c_idx, *, name):
    rows, cols = w.shape
    half = rows // 2
    tr = _tile(half, _row_cap(cols), 8)
    nh = half // tr

    def body(c_ref, w_ref, mine_ref, theirs_ref, m_ref, v_ref, g_ref, d_ref, nm_ref, nv_ref):
        gv = jnp.where(pl.program_id(0) == c_ref[0], mine_ref[...], theirs_ref[...])
        g_ref[...] = gv
        d_ref[...], nm_ref[...], nv_ref[...] = _adamw_math(w_ref[...], gv, m_ref[...], v_ref[...])

    full = pl.BlockSpec((tr, cols), lambda h, i, c_ref: (h * nh + i, 0))
    part = pl.BlockSpec((tr, cols), lambda h, i, c_ref: (i, 0))
    shp = jax.ShapeDtypeStruct((rows, cols), F32)
    return pl.pallas_call(
        body, name=name, grid_spec=_prefetch_spec((2, nh), [full, part, part, full, full], [full] * 4),
        out_shape=[shp] * 4,
        compiler_params=_params(("parallel", "parallel")),
    )(c_idx, w, mine, theirs, m, v)


def _pair_sum(g3, theirs, c_idx, *, name):
    n, rows, cols = g3.shape
    half = rows // 2
    tr = _tile(half, _row_cap(cols), 16)

    def body(c_ref, g_ref, t_ref, o_ref):
        o_ref[...] = (g_ref[...] + t_ref[...]).astype(BF16)

    part = pl.BlockSpec((None, tr, cols), lambda s, i, c_ref: (s, i, 0))
    return pl.pallas_call(
        body, name=name,
        grid_spec=_prefetch_spec((n, half // tr),
                                 [pl.BlockSpec((None, None, tr, cols), lambda s, i, c_ref: (s, c_ref[0], i, 0)), part],
                                 part),
        out_shape=jax.ShapeDtypeStruct((n, half, cols), BF16),
        compiler_params=_params(("parallel", "parallel")),
    )(c_idx, g3.reshape(n, 2, half, cols), theirs)


def _chip_sum(pair, recv, s_idx, *, name):
    _, half, cols = pair.shape
    tr = _tile(half, _row_cap(cols), 16)

    def body(s_ref, p_ref, r_ref, o_ref):
        o_ref[...] = ((p_ref[...].astype(F32) + r_ref[0].astype(F32)) + r_ref[1].astype(F32)) + r_ref[2].astype(F32)

    return pl.pallas_call(
        body, name=name,
        grid_spec=_prefetch_spec((half // tr,),
                                 [pl.BlockSpec((None, tr, cols), lambda i, s_ref: (s_ref[0], i, 0)),
                                  pl.BlockSpec((N_CHIPS - 1, tr, cols), lambda i, s_ref: (0, i, 0))],
                                 pl.BlockSpec((tr, cols), lambda i, s_ref: (i, 0))),
        out_shape=jax.ShapeDtypeStruct((half, cols), F32),
        compiler_params=_params(("parallel",)),
    )(s_idx, pair, recv)


def _sum8(parts):
    n, rows, cols = parts.shape

    def body(p_ref, o_ref):
        acc = p_ref[0]
        for i in range(1, n):
            acc = acc + p_ref[i]
        o_ref[...] = acc

    return pl.pallas_call(
        body, name="small_sum", grid=(1,),
        in_specs=[pl.BlockSpec((n, rows, cols), lambda i: (0, 0, 0))],
        out_specs=pl.BlockSpec((rows, cols), lambda i: (0, 0)),
        out_shape=jax.ShapeDtypeStruct((rows, cols), parts.dtype),
        compiler_params=_params(("arbitrary",)),
    )(parts)


def _place():
    return lax.axis_index("x"), lax.axis_index("y"), lax.axis_index("c")


ANY = pl.BlockSpec(memory_space=pl.ANY)


def _rider_parts(rider):
    if rider is None:
        return (), [], []
    kind, arrays = rider
    n = len(arrays)
    shapes = {"gather": _gathered_shapes, "pair": _pair_shapes, "chip": _chip_shapes}[kind](arrays)
    sems = _gather_sems(n) if kind == "gather" else _exchange_sems(n if kind == "pair" else 3 * n)
    return tuple(arrays), shapes, sems


def _rider_hooks(rider, ins, outs, sems, step, n_steps):
    if rider is None:
        return (lambda: None), (lambda: None)
    if rider[0] == "gather":
        start, forward, finish = _gather_steps(ins, outs, *sems)
    else:
        start, finish = {"pair": _pair_steps, "chip": _chip_steps}[rider[0]](ins, outs, *sems)
        forward = None

    def begin():
        pl.when(step == 0)(start)

    def end():
        if forward is not None:
            pl.when(step == n_steps - 2)(forward)
        pl.when(step == n_steps - 1)(finish)

    return begin, end


def _exchange_alone(rider, *, name):
    extra, shapes, sems = _rider_parts(rider)
    n = len(extra)

    def body(*refs):
        begin, end = _rider_hooks(rider, refs[:n], refs[n:2 * n], refs[-2:], jnp.int32(0), 1)
        begin()
        end()

    return pl.pallas_call(
        body, name=name, out_shape=shapes, in_specs=[ANY] * n, out_specs=[ANY] * n, scratch_shapes=sems,
    )(*extra)


def _gathered_shapes(shards):
    return [jax.ShapeDtypeStruct((N_CHIPS,) + s.shape, s.dtype) for s in shards]


def _gather_sems(n):
    return [pltpu.SemaphoreType.DMA((7 * n,)), pltpu.SemaphoreType.DMA((7 * n,))]


def _gather_steps(ins, outs, send_sems, recv_sems):
    n = len(ins)
    halves = [r.shape[0] // 2 for r in ins]
    x, y, c = _place()
    my_chip = 2 * x + y
    me, sibling = (x, y, c), (x, y, 1 - c)
    chips = [(1 - x, y), (x, 1 - y), (1 - x, 1 - y)]

    def half_of(w, chip, pc):
        return outs[w].at[chip, pl.ds(pc * halves[w], halves[w]), :]

    def copy(w, k, src, dst, to):
        return pltpu.make_async_remote_copy(
            src_ref=src, dst_ref=dst, send_sem=send_sems.at[7 * w + k], recv_sem=recv_sems.at[7 * w + k],
            device_id=to, device_id_type=MESH)

    def firsts():
        cps = []
        for w in range(n):
            cps.append(copy(w, 0, ins[w], outs[w].at[my_chip], sibling))
            mine = ins[w].at[pl.ds(c * halves[w], halves[w]), :]
            for j, (px, py) in enumerate(chips):
                cps.append(copy(w, 1 + j, mine, half_of(w, my_chip, c), (px, py, c)))
        return cps

    def passes():
        return [copy(w, 4 + j, half_of(w, 2 * px + py, c), half_of(w, 2 * px + py, c), sibling)
                for w in range(n) for j, (px, py) in enumerate(chips)]

    def start():
        for cp in firsts():
            cp.start()

    def forward():
        fws = passes()
        for w in range(n):
            for j, (px, py) in enumerate(chips):
                landed = half_of(w, 2 * px + py, c)
                copy(w, 1 + j, landed, landed, me).wait_recv()
                fws[3 * w + j].start()

    def finish():
        for w in range(n):
            copy(w, 0, ins[w], outs[w].at[my_chip], me).wait_recv()
            for j, (px, py) in enumerate(chips):
                landed = half_of(w, 2 * px + py, 1 - c)
                copy(w, 4 + j, landed, landed, me).wait_recv()
        for cp in firsts() + passes():
            cp.wait_send()

    return start, forward, finish


def _pair_shapes(grads):
    return [jax.ShapeDtypeStruct((g.shape[0], g.shape[1] // 2, g.shape[2]), g.dtype) for g in grads]


def _exchange_sems(n):
    return [pltpu.SemaphoreType.DMA((n,)), pltpu.SemaphoreType.DMA((n,))]


def _exchange_steps(copies):
    def start():
        for cp in copies():
            cp.start()

    def finish():
        for cp in copies():
            cp.wait()

    return start, finish


def _pair_steps(ins, outs, send_sems, recv_sems):
    x, y, c = _place()

    def copies():
        return [pltpu.make_async_remote_copy(
            src_ref=ins[w].at[:, pl.ds((1 - c) * (ins[w].shape[1] // 2), ins[w].shape[1] // 2), :], dst_ref=outs[w],
            send_sem=send_sems.at[w], recv_sem=recv_sems.at[w], device_id=(x, y, 1 - c), device_id_type=MESH)
            for w in range(len(ins))]

    return _exchange_steps(copies)


def _chip_shapes(pairs):
    return [jax.ShapeDtypeStruct((N_CHIPS - 1,) + p.shape[1:], p.dtype) for p in pairs]


def _chip_steps(ins, outs, send_sems, recv_sems):
    x, y, c = _place()
    others = [(1 - x, y), (x, 1 - y), (1 - x, 1 - y)]

    def copies():
        return [pltpu.make_async_remote_copy(
            src_ref=ins[w].at[2 * px + py], dst_ref=outs[w].at[j],
            send_sem=send_sems.at[3 * w + j], recv_sem=recv_sems.at[3 * w + j],
            device_id=(px, py, c), device_id_type=MESH)
            for w in range(len(ins)) for j, (px, py) in enumerate(others)]

    return _exchange_steps(copies)


def _swap_halves(mine):
    n = len(mine)

    def body(*refs):
        ins, outs, send_sems, recv_sems = refs[:n], refs[n:2 * n], refs[2 * n], refs[2 * n + 1]
        x, y, c = _place()
        copies = [pltpu.make_async_remote_copy(
            src_ref=ins[w], dst_ref=outs[w], send_sem=send_sems.at[w], recv_sem=recv_sems.at[w],
            device_id=(x, y, 1 - c), device_id_type=MESH) for w in range(n)]
        for cp in copies:
            cp.start()
        for cp in copies:
            cp.wait()

    return pl.pallas_call(
        body, name="grad_swap_halves",
        out_shape=[jax.ShapeDtypeStruct(h.shape, h.dtype) for h in mine],
        in_specs=[ANY] * n, out_specs=[ANY] * n,
        scratch_shapes=[pltpu.SemaphoreType.DMA((n,)), pltpu.SemaphoreType.DMA((n,))],
    )(*mine)


def _gather_small(small):
    srows, cols = small.shape

    def body(s_ref, all_ref, send_sems, recv_sems, local_sem):
        x, y, c = _place()
        me = 4 * x + 2 * y + c
        keep_small = pltpu.make_async_copy(s_ref, all_ref.at[me], local_sem)
        keep_small.start()
        sends = []
        for kk in range(1, 8):
            peer = (x ^ (kk >> 2), y ^ ((kk >> 1) & 1), c ^ (kk & 1))
            sends.append(pltpu.make_async_remote_copy(
                src_ref=s_ref, dst_ref=all_ref.at[me],
                send_sem=send_sems.at[kk], recv_sem=recv_sems.at[kk], device_id=peer, device_id_type=MESH))
        for cp in sends:
            cp.start()
        for kk in range(1, 8):
            px, py, pc = x ^ (kk >> 2), y ^ ((kk >> 1) & 1), c ^ (kk & 1)
            pltpu.make_async_remote_copy(
                src_ref=s_ref, dst_ref=all_ref.at[4 * px + 2 * py + pc],
                send_sem=send_sems.at[kk], recv_sem=recv_sems.at[kk], device_id=(px, py, pc),
                device_id_type=MESH).wait_recv()
        for cp in sends:
            cp.wait_send()
        keep_small.wait()

    return pl.pallas_call(
        body, name="gather_small",
        out_shape=jax.ShapeDtypeStruct((8, srows, cols), small.dtype),
        in_specs=[ANY], out_specs=ANY,
        scratch_shapes=[pltpu.SemaphoreType.DMA((8,)), pltpu.SemaphoreType.DMA((8,)), pltpu.SemaphoreType.DMA],
    )(small)


SHARDED = (("w_in", D_MODEL, IN_COLS, 1), ("w_up_a", ATT_WIDTH, D_MODEL, 1), ("w_up_b", ATT_WIDTH, D_MODEL, 1),
           ("w_out", D_MODEL, D_MODEL, 0), ("w_q_mem", D_MODEL, MEM_WIDTH, 0), ("w_kv_mem", D_MODEL, 2 * MEM_WIDTH, 0),
           ("w_o_mem", MEM_WIDTH, D_MODEL, 1), ("w_ffn_gate", D_FF, D_MODEL, 0), ("w_ffn_up", D_FF, D_MODEL, 0),
           ("w_ffn_down", D_FF, D_MODEL, 0))
TRANSPOSED = ("w_ffn_gate", "w_ffn_up")
NAMES = tuple(n for n, _, _, _ in SHARDED)


def _held(name, shard):
    return shard.T if name in TRANSPOSED else shard
EARLY, LATE = NAMES[:1], NAMES[1:]
GAINS = ("g_mix", "g_mem_q", "g_mem_kv", "g_ffn", "g_final")


def _natural(w3):
    n, r, c = w3.shape
    return w3.reshape(n * r, c)


def _shard_major(g, axis):
    if axis == 1:
        return g
    r, c = g.shape
    return g.reshape(N_CHIPS, r // N_CHIPS, c)


def kernel(x, mem, positions, g_mix, w_in, w_up_a, w_up_b, w_out, g_mem_q, g_mem_kv, w_q_mem, w_kv_mem, w_o_mem, g_ffn, w_ffn_gate, w_ffn_up, w_ffn_down, g_final, loss_target, m_g_mix, m_w_in, m_w_up_a, m_w_up_b, m_w_out, m_g_mem_q, m_g_mem_kv, m_w_q_mem, m_w_kv_mem, m_w_o_mem, m_g_ffn, m_w_ffn_gate, m_w_ffn_up, m_w_ffn_down, m_g_final, v_g_mix, v_w_in, v_w_up_a, v_w_up_b, v_w_out, v_g_mem_q, v_g_mem_kv, v_w_q_mem, v_w_kv_mem, v_w_o_mem, v_g_ffn, v_w_ffn_gate, v_w_ffn_up, v_w_ffn_down, v_g_final):
    given = dict(locals())
    shards = {n: _held(n, given[n][0]) for n in NAMES}

    early_shards = [shards[n].astype(BF16) for n in EARLY]
    late_shards = [shards[n].astype(BF16) for n in LATE]
    c_idx = lax.axis_index("c").astype(jnp.int32).reshape(1)
    s_idx = (2 * lax.axis_index("x") + lax.axis_index("y")).astype(jnp.int32).reshape(1)

    loss_row, grad_x, mine, gain_grads = _local_step(x, mem, positions, loss_target, g_mix, g_mem_q, g_mem_kv,
                                                     g_ffn, g_final, {}, early_shards, late_shards, (c_idx, s_idx))
    return _reduce_and_update(given, shards, loss_row, grad_x, mine, gain_grads, c_idx)


def _reduce_halves(glist, names, c_idx, s_idx, pair_exchange, chip_exchange):
    theirs = pair_exchange(glist)
    pairs = [_pair_sum(g, t, c_idx, name="pair_sum_" + n) for n, g, t in zip(names, glist, theirs)]
    recv = chip_exchange(pairs)
    return [_chip_sum(p, r, s_idx, name="chip_sum_" + n) for n, p, r in zip(names, pairs, recv)]


def _local_step(x, mem, positions, loss_target, g_mix, g_mem_q, g_mem_kv, g_ffn, g_final, wf,
                early_shards=None, late_shards=None, place=None):
    b_dim, s_dim, d = x.shape
    t_dim = b_dim * s_dim
    n_mem = mem.shape[1]
    wf = dict(wf)

    xb = x.reshape(t_dim, d)
    tgt = loss_target.reshape(t_dim, d)
    memf = mem.reshape(b_dim * n_mem, d)
    gfin = g_final.reshape(1, d)
    pos = positions.reshape(t_dim, 1).astype(F32)

    lane = jnp.arange(LANES) % HEAD_DIM
    half = ROPE_DIM // 2
    inv_freq = ROPE_THETA ** (-jnp.arange(half, dtype=F32) / half)
    inv_lane = jnp.where(lane < ROPE_DIM, inv_freq[lane % half], 0.0).reshape(1, -1).astype(F32)
    sel_a = (lane < half).astype(F32).reshape(1, -1)
    sel_b = ((lane >= half) & (lane < ROPE_DIM)).astype(F32).reshape(1, -1)

    def rows3(t):
        return t.reshape(b_dim, s_dim, t.shape[-1])

    def rows2(t):
        return t.reshape(t_dim, t.shape[-1])

    if early_shards:
        n1, gathered = _rms_fwd(xb, g_mix, name="rms_mix", rider=("gather", early_shards))
        wf.update(zip(EARLY, gathered))
    else:
        n1 = _rms_fwd(xb, g_mix, name="rms_mix")
    proj = _mm_cs(n1, wf["w_in"], name="mm_in")
    proj3 = rows3(proj)
    cs, sn = _rope_table(pos, inv_lane, sel_a, sel_b)
    cs3, sn3 = rows3(cs), rows3(sn)
    (oa16, oa32, lse_a), _ = _dil_fwd(proj3, cs3, sn3, sel_a, sel_b)
    ob16, gathered = _sb_fwd(proj3, ("gather", late_shards) if late_shards else None)
    wf.update(zip(LATE, gathered))
    w_out, w_q, w_kv = _natural(wf["w_out"]), _natural(wf["w_q_mem"]), _natural(wf["w_kv_mem"])
    oa, ob = rows2(oa16), rows2(ob16)
    ua = _mm_sm(oa, wf["w_up_a"], name="mm_up_a", out_dtype=BF16)
    ub = _mm_sm(ob, wf["w_up_b"], name="mm_up_b", out_dtype=BF16)
    mixed, h1, hn = _gate_out_norm(proj, ua, ub, w_out, xb, g_mem_q)

    memn = _rms_fwd(memf, g_mem_kv, name="rms_mem_kv")
    qm = _mm(hn, w_q, name="mm_q_mem", out_dtype=BF16)
    kvm = _mm(memn, w_kv, name="mm_kv_mem", out_dtype=BF16)
    qm3, kvm3 = rows3(qm), kvm.reshape(b_dim, n_mem, 2 * MEM_WIDTH)
    om = rows2(_mem_fwd(qm3, kvm3))
    h2 = _mm_sm(om, wf["w_o_mem"], name="mm_o_mem", add=h1)

    n3 = _rms_fwd(h2, g_ffn, name="rms_ffn")
    gate3, up3, act3 = _ffn_up_swiglu(n3, wf["w_ffn_gate"], wf["w_ffn_up"])
    loss_row, dh3, dg_final = _down_final(act3, wf["w_ffn_down"], h2, gfin, tgt)

    grads = {}
    grads["w_ffn_down"] = _mm_ffn_down_dw(act3, dh3, name="mm_down_dw")
    dgate3, dup3 = _ffn_down_dx_swiglu(dh3, wf["w_ffn_down"], gate3, up3)
    grads["w_ffn_gate"] = _mm_ffn_down_dw(dgate3, n3, name="mm_gate_dw")
    grads["w_ffn_up"] = _mm_ffn_down_dw(dup3, n3, name="mm_up_dw")
    dn3 = _ffn_up_dx(dgate3, dup3, wf["w_ffn_gate"], wf["w_ffn_up"])
    dh2, dg_ffn = _rms_bwd(h2, g_ffn, dn3, dh3, name="rms_ffn_bwd")

    dom = _mm_sm_dx(dh2, wf["w_o_mem"], name="mm_o_mem_dx", out_dtype=BF16)
    grads["w_o_mem"] = _mm_sm_dw(om, dh2, name="mm_o_mem_dw")
    dqm, dkm, dvm = _mem_bwd(qm3, kvm3, rows3(dom))
    dqm = rows2(dqm)
    dkvm = jnp.concatenate([dkm, dvm], axis=-1).reshape(b_dim * n_mem, 2 * MEM_WIDTH).astype(BF16)
    grads["w_q_mem"] = _shard_major(_mm(hn, dqm, name="mm_q_mem_dw", ta=True), 0)
    dhn = _mm(dqm, w_q, name="mm_q_mem_dx", tb=True, out_dtype=BF16)
    grads["w_kv_mem"] = _shard_major(_mm(memn, dkvm, name="mm_kv_mem_dw", ta=True), 0)
    dmemn = _mm(dkvm, w_kv, name="mm_kv_mem_dx", tb=True)
    _, dg_mem_kv = _rms_bwd(memf, g_mem_kv, dmemn, None, name="rms_mem_kv_bwd")
    dh1, dg_mem_q = _rms_bwd(h1, g_mem_q, dhn, dh2, name="rms_mem_q_bwd")

    grads["w_out"] = _shard_major(_mm(mixed, dh1, name="mm_out_dw", ta=True), 0)
    dua, dub, dgates = _out_dx_gate_bwd(dh1, w_out, proj, ua, ub)
    doa = _mm_sm_dx(dua, wf["w_up_a"], name="mm_up_a_dx")
    grads["w_up_a"] = _mm_sm_dw(oa, dua, name="mm_up_a_dw")
    dob = _mm_sm_dx(dub, wf["w_up_b"], name="mm_up_b_dx", out_dtype=BF16)
    grads["w_up_b"] = _mm_sm_dw(ob, dub, name="mm_up_b_dw")

    att = {}

    def dil_with_pairs(glist):
        att["a"], theirs = _dil_bwd(proj3, cs3, sn3, sel_a, sel_b, rows3(doa), oa32, lse_a,
                                    ("pair", glist) if glist else None)
        return theirs

    def sb_with_chips(pairs):
        att["b"], recv = _sb_bwd(proj3, rows3(dob), ("chip", pairs) if pairs else None)
        return recv

    if place is None:
        dil_with_pairs(())
        sb_with_chips(())
    else:
        mine_late = _reduce_halves([grads[n] for n in LATE], LATE, *place, dil_with_pairs, sb_with_chips)
    dproj = jnp.concatenate([rows2(t) for t in att["a"] + att["b"]] + [dgates], axis=1)
    grads["w_in"] = _mm_cs_dw(n1, dproj, name="mm_in_dw")
    if place is None:
        dn1 = _mm_cs_dx(dproj, wf["w_in"], name="mm_in_dx", out_dtype=BF16)
        dx, dg_mix = _rms_bwd(xb, g_mix, dn1, dh1, name="rms_mix_bwd")
    else:
        tail = {}

        def pair_alone(glist):
            return _exchange_alone(("pair", glist), name="grad_pair_exchange")

        def dx_with_chips(pairs):
            tail["dn1"], recv = _mm_cs_dx(dproj, wf["w_in"], name="mm_in_dx", out_dtype=BF16, rider=("chip", pairs))
            return recv

        mine_early = _reduce_halves([grads[n] for n in EARLY], EARLY, *place, pair_alone, dx_with_chips)
        dx, dg_mix = _rms_bwd(xb, g_mix, tail["dn1"], dh1, name="rms_mix_bwd")
    grad_x = dx.reshape(b_dim, s_dim, d)
    gains = (dg_mix, dg_mem_q, dg_mem_kv, dg_ffn, dg_final)
    if place is None:
        return loss_row, grad_x, grads, gains
    return loss_row, grad_x, mine_early + mine_late, gains


def _reduce_and_update(given, shards, loss_row, grad_x, mine, gain_grads, c_idx):
    d = D_MODEL
    dg_mix, dg_mem_q, dg_mem_kv, dg_ffn, dg_final = gain_grads
    small = jnp.concatenate([dg_mix, dg_mem_q, dg_mem_kv, dg_ffn, dg_final,
                             jnp.pad(loss_row, ((0, 0), (0, FLAT_COLS - LANES))), jnp.zeros((2, FLAT_COLS), F32)], axis=0)
    small_all = _gather_small(small)
    others = _swap_halves(mine)
    small_sum = _sum8(small_all)
    loss = small_sum[5, 0]

    out_g, out_d, out_m, out_v = {}, {}, {}, {}
    for n, mine_n, other_n in zip(NAMES, mine, others):
        res = _adamw_halves(shards[n], mine_n, other_n, _held(n, given["m_" + n][0]), _held(n, given["v_" + n][0]),
                            c_idx, name="adamw_" + n)
        out_g[n], out_d[n], out_m[n], out_v[n] = [_held(n, r)[None] for r in res]
    gain_w = jnp.concatenate([given[n].reshape(1, d) for n in GAINS], axis=0)
    gain_m = jnp.concatenate([given["m_" + n].reshape(1, d) for n in GAINS], axis=0)
    gain_v = jnp.concatenate([given["v_" + n].reshape(1, d) for n in GAINS], axis=0)
    gain_g = small_sum[:len(GAINS)]
    gd, gm, gv = _adamw(gain_w, gain_g, gain_m, gain_v, name="adamw_gains")
    for i, n in enumerate(GAINS):
        shape = given[n].shape
        out_g[n], out_d[n] = gain_g[i].reshape(shape), gd[i].reshape(shape)
        out_m[n], out_v[n] = gm[i].reshape(shape), gv[i].reshape(shape)

    order = ["g_mix", "w_in", "w_up_a", "w_up_b", "w_out", "g_mem_q", "g_mem_kv", "w_q_mem", "w_kv_mem", "w_o_mem",
             "g_ffn", "w_ffn_gate", "w_ffn_up", "w_ffn_down", "g_final"]
    return (loss, grad_x, *[out_g[n] for n in order], *[out_d[n] for n in order],
            *[out_m[n] for n in order], *[out_v[n] for n in order])
```

```python
import jax
import jax.numpy as jnp
from jax import lax
from jax.experimental import pallas as pl
from jax.experimental.pallas import tpu as pltpu

F32 = jnp.float32
BF16 = jnp.bfloat16
MESH = pl.DeviceIdType.MESH

D_MODEL = 1024
HEAD_DIM = 64
N_HEADS = 8
ATT_WIDTH = N_HEADS * HEAD_DIM
DIL_PATTERNS = ((128, 1), (512, 4), (2048, 16))
BLOCK = 128
SB_ROWS = 1024
SB_STEP_FWD, SB_STEP_BWD = 8, 4
ROPE_THETA = 500000.0
ROPE_DIM = HEAD_DIM // 4
N_HEADS_MEM = 4
MEM_HEAD_DIM = 128
MEM_WIDTH = N_HEADS_MEM * MEM_HEAD_DIM
D_FF = 2816
IN_COLS = 6 * ATT_WIDTH + 2 * D_MODEL
RMS_EPS = 1e-6
ADAM_LR = 0.001
ADAM_B1 = 0.9
ADAM_B2 = 0.999
ADAM_EPS = 1e-08
ADAM_WD = 0.01
ADAM_STEP = 10

N_CHIPS = 4
LANES = 128
FLAT_COLS = 1024
VMEM_LIMIT = 56 * 1024 * 1024

PAIRS = ATT_WIDTH // LANES
COL_QA, COL_KA, COL_VA, COL_QB, COL_KB, COL_VB = (i * PAIRS for i in range(6))

MM_CAP = 1408
TOK_CAP = 2048
NN = (((1,), (0,)), ((), ()))
NT = (((1,), (1,)), ((), ()))
TN = (((0,), (0,)), ((), ()))
BNN = (((2,), (1,)), ((0,), (0,)))
BNT = (((2,), (2,)), ((0,), (0,)))
BTN = (((1,), (1,)), ((0,), (0,)))
DIL_BATCH = 16
DIL_CHUNK = 512


def _tile(dim, cap, unit=LANES):
    if dim <= cap:
        return dim
    best = None
    for t in range(unit, cap + 1, unit):
        if dim % t == 0:
            best = t
    assert best is not None, (dim, cap)
    return best


def _row_cap(cols):
    return max(256, (1 << 18) // cols)


def _params(sem):
    return pltpu.CompilerParams(dimension_semantics=sem, vmem_limit_bytes=VMEM_LIMIT)


def _mm(a, b, *, name, ta=False, tb=False, add=None, out_dtype=F32,
        tm_cap=MM_CAP, tn_cap=MM_CAP, tk_cap=MM_CAP):
    if ta:
        k_dim, m_dim = a.shape
    else:
        m_dim, k_dim = a.shape
    if tb:
        n_dim, kb = b.shape
    else:
        kb, n_dim = b.shape
    assert kb == k_dim, (a.shape, b.shape, ta, tb)
    tm, tn, tk = _tile(m_dim, tm_cap), _tile(n_dim, tn_cap), _tile(k_dim, tk_cap)
    nk = k_dim // tk
    dims = (((0 if ta else 1,), (1 if tb else 0,)), ((), ()))
    has_add = add is not None

    def body(*refs):
        if has_add:
            a_ref, b_ref, add_ref, o_ref = refs[:4]
        else:
            a_ref, b_ref, o_ref = refs[:3]
        part = lax.dot_general(a_ref[...].astype(BF16), b_ref[...].astype(BF16), dims, preferred_element_type=F32)

        def finish(r):
            if has_add:
                r = add_ref[...] + r
            o_ref[...] = r.astype(out_dtype)

        if nk == 1:
            finish(part)
            return
        acc_ref = refs[-1]
        k = pl.program_id(2)

        @pl.when(k == 0)
        def _():
            acc_ref[...] = part

        @pl.when(k > 0)
        def _():
            acc_ref[...] += part

        @pl.when(k == nk - 1)
        def _():
            finish(acc_ref[...])

    a_spec = pl.BlockSpec((tk, tm), lambda i, j, k: (k, i)) if ta else pl.BlockSpec((tm, tk), lambda i, j, k: (i, k))
    b_spec = pl.BlockSpec((tn, tk), lambda i, j, k: (j, k)) if tb else pl.BlockSpec((tk, tn), lambda i, j, k: (k, j))
    o_spec = pl.BlockSpec((tm, tn), lambda i, j, k: (i, j))
    in_specs = [a_spec, b_spec] + ([o_spec] if has_add else [])
    args = (a, b) + ((add,) if has_add else ())
    return pl.pallas_call(
        body, name=name, grid=(m_dim // tm, n_dim // tn, nk),
        in_specs=in_specs, out_specs=o_spec,
        out_shape=jax.ShapeDtypeStruct((m_dim, n_dim), out_dtype),
        scratch_shapes=[pltpu.VMEM((tm, tn), F32)] if nk > 1 else [],
        compiler_params=_params(("parallel", "parallel", "arbitrary")),
    )(*args)


def _mm_core(name, a, b, a_spec, b_spec, o_spec, out_shape, grid, dims, *, add=None, out_dtype=F32, rider=None):
    nk = grid[2]
    has_add = add is not None
    n_in = 3 if has_add else 2
    acc_shape = tuple(d for d in o_spec.block_shape if d is not None)
    extra, extra_shapes, extra_sems = _rider_parts(rider)
    n_w = len(extra)

    def body(*refs):
        a_ref, b_ref = refs[:2]
        o_ref = refs[n_in + n_w]
        step = (pl.program_id(0) * grid[1] + pl.program_id(1)) * nk + pl.program_id(2)
        begin, end = _rider_hooks(rider, refs[n_in:n_in + n_w], refs[n_in + n_w + 1:n_in + 2 * n_w + 1], refs[-2:],
                                  step, grid[0] * grid[1] * nk)
        begin()
        part = lax.dot_general(a_ref[...].astype(BF16), b_ref[...].astype(BF16), dims, preferred_element_type=F32)

        def finish(r):
            if has_add:
                r = refs[2][...] + r
            o_ref[...] = r.astype(out_dtype)

        if nk == 1:
            finish(part)
        else:
            acc_ref = refs[n_in + 2 * n_w + 1]
            k = pl.program_id(2)

            @pl.when(k == 0)
            def _():
                acc_ref[...] = part

            @pl.when(k > 0)
            def _():
                acc_ref[...] += part

            @pl.when(k == nk - 1)
            def _():
                finish(acc_ref[...])
        end()

    in_specs = [a_spec, b_spec] + ([o_spec] if has_add else []) + [ANY] * n_w
    args = (a, b) + ((add,) if has_add else ()) + extra
    res = pl.pallas_call(
        body, name=name, grid=grid, in_specs=in_specs, out_specs=[o_spec] + [ANY] * n_w,
        out_shape=[jax.ShapeDtypeStruct(out_shape, out_dtype)] + extra_shapes,
        scratch_shapes=([pltpu.VMEM(acc_shape, F32)] if nk > 1 else []) + extra_sems,
        compiler_params=_params(("arbitrary",) * 3 if n_w else ("parallel", "parallel", "arbitrary")),
    )(*args)
    return (res[0], res[1:]) if n_w else res[0]


def _mm_cs(a, w3, *, name):
    m_dim, k_dim = a.shape
    _, _, n4 = w3.shape
    tm, tn, tk = _tile(m_dim, TOK_CAP), _tile(n4, MM_CAP), _tile(k_dim, MM_CAP)
    npb = n4 // tn
    return _mm_core(name, a, w3,
                    pl.BlockSpec((tm, tk), lambda i, j, k: (i, k)),
                    pl.BlockSpec((None, tk, tn), lambda i, j, k: (j // npb, k, j % npb)),
                    pl.BlockSpec((tm, tn), lambda i, j, k: (i, j)),
                    (m_dim, N_CHIPS * n4), (m_dim // tm, N_CHIPS * npb, k_dim // tk), NN)


def _mm_cs_dx(dy, w3, *, name, out_dtype=F32, rider=None):
    m_dim, _ = dy.shape
    _, k_dim, n4 = w3.shape
    tm, tkw, tn = _tile(m_dim, TOK_CAP), _tile(k_dim, MM_CAP), _tile(n4, MM_CAP)
    npb = n4 // tn
    return _mm_core(name, dy, w3,
                    pl.BlockSpec((tm, tn), lambda i, j, k: (i, k)),
                    pl.BlockSpec((None, tkw, tn), lambda i, j, k: (k // npb, j, k % npb)),
                    pl.BlockSpec((tm, tkw), lambda i, j, k: (i, j)),
                    (m_dim, k_dim), (m_dim // tm, k_dim // tkw, N_CHIPS * npb), NT, out_dtype=out_dtype, rider=rider)


def _mm_cs_dw(a, dy, *, name):
    m_dim, k_dim = a.shape
    n4 = dy.shape[1] // N_CHIPS
    tmk, tn, tk = _tile(k_dim, MM_CAP), _tile(n4, MM_CAP), _tile(m_dim, TOK_CAP)
    npb = n4 // tn
    return _mm_core(name, a, dy,
                    pl.BlockSpec((tk, tmk), lambda i, j, k: (k, i)),
                    pl.BlockSpec((tk, tn), lambda i, j, k: (k, j)),
                    pl.BlockSpec((None, tmk, tn), lambda i, j, k: (j // npb, i, j % npb)),
                    (N_CHIPS, k_dim, n4), (k_dim // tmk, N_CHIPS * npb, m_dim // tk), TN)


def _mm_sm(a, w3, *, name, add=None, out_dtype=F32, tt=1024):
    t_dim, k_dim = a.shape
    n_s, _, n4 = w3.shape
    has_add = add is not None

    def body(*refs):
        a_ref, w_ref, o_ref = refs[0], refs[1], refs[-1]
        av = a_ref[...].astype(BF16)
        for s in range(n_s):
            cols = slice(s * n4, (s + 1) * n4)
            r = jnp.dot(av, w_ref[s], preferred_element_type=F32)
            if has_add:
                r = refs[2][:, cols] + r
            o_ref[:, cols] = r.astype(out_dtype)

    row = pl.BlockSpec((tt, n_s * n4), lambda i: (i, 0))
    return pl.pallas_call(
        body, name=name, grid=(t_dim // tt,),
        in_specs=[pl.BlockSpec((tt, k_dim), lambda i: (i, 0)), pl.BlockSpec(w3.shape, lambda i: (0, 0, 0))]
        + ([row] if has_add else []),
        out_specs=row, out_shape=jax.ShapeDtypeStruct((t_dim, n_s * n4), out_dtype),
        compiler_params=_params(("parallel",)),
    )(*((a, w3) + ((add,) if has_add else ())))


def _mm_sm_dx(dy, w3, *, name, out_dtype=F32, tt=1024):
    t_dim, _ = dy.shape
    n_s, k_dim, n4 = w3.shape

    def body(dy_ref, w_ref, o_ref):
        dyv = dy_ref[...].astype(BF16)
        acc = lax.dot_general(dyv[:, :n4], w_ref[0], NT, preferred_element_type=F32)
        for s in range(1, n_s):
            acc = acc + lax.dot_general(dyv[:, s * n4:(s + 1) * n4], w_ref[s], NT, preferred_element_type=F32)
        o_ref[...] = acc.astype(out_dtype)

    return pl.pallas_call(
        body, name=name, grid=(t_dim // tt,),
        in_specs=[pl.BlockSpec((tt, n_s * n4), lambda i: (i, 0)), pl.BlockSpec(w3.shape, lambda i: (0, 0, 0))],
        out_specs=pl.BlockSpec((tt, k_dim), lambda i: (i, 0)),
        out_shape=jax.ShapeDtypeStruct((t_dim, k_dim), out_dtype),
        compiler_params=_params(("parallel",)),
    )(dy, w3)


def _mm_sm_dw(a, dy, *, name, tk=1024):
    t_dim, k_dim = a.shape
    n4 = dy.shape[1] // N_CHIPS

    def body(a_ref, dy_ref, o_ref):
        av = a_ref[...].astype(BF16)
        dyv = dy_ref[...].astype(BF16)

        @pl.when(pl.program_id(0) == 0)
        def _():
            o_ref[...] = jnp.zeros_like(o_ref)

        for s in range(N_CHIPS):
            o_ref[s] += lax.dot_general(av, dyv[:, s * n4:(s + 1) * n4], TN, preferred_element_type=F32)

    return pl.pallas_call(
        body, name=name, grid=(t_dim // tk,),
        in_specs=[pl.BlockSpec((tk, k_dim), lambda i: (i, 0)), pl.BlockSpec((tk, N_CHIPS * n4), lambda i: (i, 0))],
        out_specs=pl.BlockSpec((N_CHIPS, k_dim, n4), lambda i: (0, 0, 0)),
        out_shape=jax.ShapeDtypeStruct((N_CHIPS, k_dim, n4), F32),
        compiler_params=_params(("arbitrary",)),
    )(a, dy)


def _ffn_up_dx(dg3, du3, wg3, wu3, *, tt=512):
    n_s, t_dim, f4 = dg3.shape
    d = wg3.shape[2]

    def body(dg_ref, du_ref, wg_ref, wu_ref, o_ref):
        acc = None
        for s in range(n_s):
            part = (jnp.dot(dg_ref[s], wg_ref[s], preferred_element_type=F32)
                    + jnp.dot(du_ref[s], wu_ref[s], preferred_element_type=F32))
            acc = part if acc is None else acc + part
        o_ref[...] = acc.astype(BF16)

    a_spec = pl.BlockSpec((n_s, tt, f4), lambda i: (0, i, 0))
    w_spec = pl.BlockSpec(wg3.shape, lambda i: (0, 0, 0))
    return pl.pallas_call(
        body, name="ffn_up_dx", grid=(t_dim // tt,),
        in_specs=[a_spec, a_spec, w_spec, w_spec], out_specs=pl.BlockSpec((tt, d), lambda i: (i, 0)),
        out_shape=jax.ShapeDtypeStruct((t_dim, d), BF16),
        compiler_params=_params(("parallel",)),
    )(dg3, du3, wg3, wu3)


def _mm_ffn_down_dw(act3, dh, *, name, tk=1024):
    n_s, t_dim, f4 = act3.shape
    d = dh.shape[1]

    def body(a_ref, b_ref, o_ref):
        bv = b_ref[...].astype(BF16)

        @pl.when(pl.program_id(0) == 0)
        def _():
            o_ref[...] = jnp.zeros_like(o_ref)

        for s in range(n_s):
            o_ref[s] += lax.dot_general(a_ref[s], bv, TN, preferred_element_type=F32)

    return pl.pallas_call(
        body, name=name, grid=(t_dim // tk,),
        in_specs=[pl.BlockSpec((n_s, tk, f4), lambda i: (0, i, 0)), pl.BlockSpec((tk, d), lambda i: (i, 0))],
        out_specs=pl.BlockSpec((n_s, f4, d), lambda i: (0, 0, 0)),
        out_shape=jax.ShapeDtypeStruct((n_s, f4, d), F32),
        compiler_params=_params(("arbitrary",)),
    )(act3, dh)


def _rms_fwd(x, g, *, name, tt=512, rider=None):
    t_dim, d = x.shape
    tt = _tile(t_dim, tt, 8)
    extra, extra_shapes, extra_sems = _rider_parts(rider)
    n_w = len(extra)

    def body(*refs):
        x_ref, g_ref, o_ref = refs[0], refs[1], refs[2 + n_w]
        begin, end = _rider_hooks(rider, refs[2:2 + n_w], refs[3 + n_w:3 + 2 * n_w], refs[-2:], pl.program_id(0),
                                  t_dim // tt)
        begin()
        xv = x_ref[...]
        r = lax.rsqrt(jnp.mean(xv * xv, axis=-1, keepdims=True) + RMS_EPS)
        o_ref[...] = ((xv * r) * g_ref[...]).astype(o_ref.dtype)
        end()

    res = pl.pallas_call(
        body, name=name, grid=(t_dim // tt,),
        in_specs=[pl.BlockSpec((tt, d), lambda i: (i, 0)), pl.BlockSpec((1, d), lambda i: (0, 0))] + [ANY] * n_w,
        out_specs=[pl.BlockSpec((tt, d), lambda i: (i, 0))] + [ANY] * n_w,
        out_shape=[jax.ShapeDtypeStruct((t_dim, d), BF16)] + extra_shapes,
        scratch_shapes=extra_sems,
        compiler_params=_params(("arbitrary",) if n_w else ("parallel",)),
    )(x, g, *extra)
    return (res[0], res[1:]) if n_w else res[0]


def _rms_bwd(x, g, dy, add, *, name, tt=512):
    t_dim, d = x.shape
    tt = _tile(t_dim, tt, 8)
    has_add = add is not None

    def body(*refs):
        x_ref, g_ref, dy_ref = refs[:3]
        add_ref = refs[3] if has_add else None
        dx_ref, dg_ref = refs[-2:]
        xv = x_ref[...]
        dyv = dy_ref[...].astype(F32)
        r = lax.rsqrt(jnp.mean(xv * xv, axis=-1, keepdims=True) + RMS_EPS)
        xh = xv * r
        u = dyv * g_ref[...]
        dx = r * (u - xh * jnp.mean(u * xh, axis=-1, keepdims=True))
        if has_add:
            dx = add_ref[...] + dx
        dx_ref[...] = dx

        @pl.when(pl.program_id(0) == 0)
        def _():
            dg_ref[...] = jnp.zeros_like(dg_ref)

        dg_ref[...] += jnp.sum(dyv * xh, axis=0, keepdims=True)

    row = pl.BlockSpec((tt, d), lambda i: (i, 0))
    vec = pl.BlockSpec((1, d), lambda i: (0, 0))
    in_specs = [row, vec, row] + ([row] if has_add else [])
    args = (x, g, dy) + ((add,) if has_add else ())
    return pl.pallas_call(
        body, name=name, grid=(t_dim // tt,),
        in_specs=in_specs, out_specs=[row, vec],
        out_shape=[jax.ShapeDtypeStruct((t_dim, d), F32), jax.ShapeDtypeStruct((1, d), F32)],
        compiler_params=_params(("arbitrary",)),
    )(*args)


def _down_final(act3, wd3, h, g, target, *, tt=512):
    n_s, t_dim, f4 = act3.shape
    d = h.shape[1]
    n_steps = t_dim // tt

    def body(a_ref, w_ref, h_ref, g_ref, t_ref, loss_ref, dh_ref, dg_ref, sq_ref):
        i = pl.program_id(0)
        xv = h_ref[...]
        for s in range(n_s):
            xv = xv + jnp.dot(a_ref[s], w_ref[s], preferred_element_type=F32)
        gv = g_ref[...]
        r = lax.rsqrt(jnp.mean(xv * xv, axis=-1, keepdims=True) + RMS_EPS)
        xh = xv * r
        err = xh * gv - t_ref[...]
        dyv = err * (1.0 / d)
        u = dyv * gv
        dh_ref[...] = r * (u - xh * jnp.mean(u * xh, axis=-1, keepdims=True))

        @pl.when(i == 0)
        def _():
            dg_ref[...] = jnp.zeros_like(dg_ref)
            sq_ref[...] = jnp.zeros_like(sq_ref)

        dg_ref[...] += jnp.sum(dyv * xh, axis=0, keepdims=True)
        sq_ref[...] += jnp.sum(err * err, axis=0, keepdims=True)

        @pl.when(i == n_steps - 1)
        def _():
            total = jnp.sum(sq_ref[...], axis=-1, keepdims=True) * (0.5 / d)
            loss_ref[...] = jnp.broadcast_to(total, loss_ref.shape)

    row = pl.BlockSpec((tt, d), lambda i: (i, 0))
    vec = pl.BlockSpec((1, d), lambda i: (0, 0))
    return pl.pallas_call(
        body, name="down_final_loss", grid=(n_steps,),
        in_specs=[pl.BlockSpec((n_s, tt, f4), lambda i: (0, i, 0)), pl.BlockSpec(wd3.shape, lambda i: (0, 0, 0)),
                  row, vec, row],
        out_specs=[pl.BlockSpec((1, LANES), lambda i: (0, 0)), row, vec],
        out_shape=[jax.ShapeDtypeStruct((1, LANES), F32), jax.ShapeDtypeStruct((t_dim, d), F32),
                   jax.ShapeDtypeStruct((1, d), F32)],
        scratch_shapes=[pltpu.VMEM((1, d), F32)],
        compiler_params=_params(("arbitrary",)),
    )(act3, wd3, h, g, target)


def _rope_table(pos, inv_lane, sel_a, sel_b, *, tt=512):
    t_dim = pos.shape[0]

    def body(p_ref, f_ref, a_ref, b_ref, c_ref, s_ref):
        ang = p_ref[...] * f_ref[...]
        on = (a_ref[...] + b_ref[...]) > 0.0
        c_ref[...] = jnp.where(on, jnp.cos(ang), 1.0)
        s_ref[...] = jnp.where(on, jnp.sin(ang), 0.0)

    vec = pl.BlockSpec((1, LANES), lambda i: (0, 0))
    row = pl.BlockSpec((tt, LANES), lambda i: (i, 0))
    shp = jax.ShapeDtypeStruct((t_dim, LANES), F32)
    return pl.pallas_call(
        body, name="rope_table", grid=(t_dim // tt,),
        in_specs=[pl.BlockSpec((tt, 1), lambda i: (i, 0)), vec, vec, vec],
        out_specs=[row, row], out_shape=[shp, shp],
        compiler_params=_params(("parallel",)),
    )(pos, inv_lane, sel_a, sel_b)


def _rotate(xv, cs, sn, sa, sb):
    half = ROPE_DIM // 2
    up = pltpu.roll(xv, LANES - half, 1)
    dn = pltpu.roll(xv, half, 1)
    return xv * cs + (dn * sb - up * sa) * sn


def _head_masks():
    h1 = lax.broadcasted_iota(jnp.int32, (1, LANES), 1) < HEAD_DIM
    return h1, jnp.logical_not(h1)


def _split_heads(xv, h1, h2):
    return jnp.where(h1, xv, 0.0).astype(BF16), jnp.where(h2, xv, 0.0).astype(BF16)


def _tri_masks():
    r = lax.broadcasted_iota(jnp.int32, (BLOCK, BLOCK), 0)
    c = lax.broadcasted_iota(jnp.int32, (BLOCK, BLOCK), 1)
    return c <= r, r <= c


def _stream_rows(start, dil):
    if dil == 1:
        return pl.ds(pl.multiple_of(start, BLOCK), BLOCK)
    return pl.ds(start, BLOCK, stride=dil)


def _dil_tile(idx, dil, nb):
    r = idx // nb
    n = idx % nb
    return (_stream_rows(r + dil * BLOCK * n, dil), _stream_rows(r + dil * BLOCK * jnp.maximum(n - 1, 0), dil),
            n > 0)


def _dil_specs(b_dim, s_dim):
    def col(c0):
        return pl.BlockSpec((None, s_dim, LANES),lambda b, h: (b, 0, c0 + h))
    tab = pl.BlockSpec((None, s_dim, LANES),lambda b, h: (b, 0, 0))
    vec = pl.BlockSpec((1, LANES), lambda b, h: (0, 0))
    return col, tab, vec


def _dil_fwd(proj3, cs3, sn3, sel_a, sel_b, rider=None):
    b_dim, s_dim, _ = proj3.shape
    scale = HEAD_DIM ** -0.5
    n_pat = len(DIL_PATTERNS)
    extra, extra_shapes, extra_sems = _rider_parts(rider)
    n_w = len(extra)
    n_steps = b_dim * PAIRS

    def body(*refs):
        q_ref, k_ref, v_ref, cs_ref, sn_ref, sa_ref, sb_ref = refs[:7]
        o16_ref, o32_ref, l_ref = refs[7 + n_w:10 + n_w]
        qr, kr = refs[10 + 2 * n_w:12 + 2 * n_w]
        per_pattern = refs[12 + 2 * n_w:12 + 2 * n_w + 2 * n_pat]
        og, lg = per_pattern[:n_pat], per_pattern[n_pat:]
        step = pl.program_id(0) * PAIRS + pl.program_id(1)
        begin, end = _rider_hooks(rider, refs[7:7 + n_w], refs[10 + n_w:10 + 2 * n_w], refs[-2:], step, n_steps)
        begin()
        h1, h2 = _head_masks()
        cur_ok, prev_ok = _tri_masks()
        sa, sb = sa_ref[...], sb_ref[...]

        def prep(j, _):
            rows = pl.ds(pl.multiple_of(j * DIL_CHUNK, DIL_CHUNK), DIL_CHUNK)
            cs, sn = cs_ref[rows, :], sn_ref[rows, :]
            qr[rows, :] = _rotate(q_ref[rows, :], cs, sn, sa, sb) * scale
            kr[rows, :] = _rotate(k_ref[rows, :], cs, sn, sa, sb)
            return 0

        lax.fori_loop(0, s_dim // DIL_CHUNK, prep, 0)

        for g, (_, dil) in enumerate(DIL_PATTERNS):
            nb = s_dim // dil // BLOCK

            def some(bi, _, g=g, dil=dil, nb=nb):
                tiles = [_dil_tile(bi * DIL_BATCH + t, dil, nb) for t in range(DIL_BATCH)]
                rows = [t[0] for t in tiles]
                q1, q2 = _split_heads(jnp.stack([qr[rw, :] for rw in rows]), h1, h2)
                kc = jnp.stack([kr[rw, :] for rw in rows]).astype(BF16)
                vc1, vc2 = _split_heads(jnp.stack([v_ref[rw, :] for rw in rows]), h1, h2)
                if nb > 1:
                    kp = jnp.stack([kr[t[1], :] for t in tiles]).astype(BF16)
                    vp1, vp2 = _split_heads(jnp.stack([v_ref[t[1], :] for t in tiles]), h1, h2)
                    p_ok = jnp.stack([jnp.logical_and(prev_ok, t[2]) for t in tiles])

                def head(qh, vch, vph):
                    sc = jnp.where(cur_ok, lax.dot_general(qh, kc, BNT, preferred_element_type=F32), -jnp.inf)
                    m = jnp.max(sc, axis=-1, keepdims=True)
                    if nb > 1:
                        sp = jnp.where(p_ok, lax.dot_general(qh, kp, BNT, preferred_element_type=F32), -jnp.inf)
                        m = jnp.maximum(m, jnp.max(sp, axis=-1, keepdims=True))
                    pc = jnp.exp(sc - m)
                    den = jnp.sum(pc, axis=-1, keepdims=True)
                    acc = lax.dot_general(pc.astype(BF16), vch, BNN, preferred_element_type=F32)
                    if nb > 1:
                        pp = jnp.exp(sp - m)
                        den = den + jnp.sum(pp, axis=-1, keepdims=True)
                        acc = acc + lax.dot_general(pp.astype(BF16), vph, BNN, preferred_element_type=F32)
                    return acc / den, m + jnp.log(den)

                o1, l1 = head(q1, vc1, vp1 if nb > 1 else None)
                o2, l2 = head(q2, vc2, vp2 if nb > 1 else None)
                o, l = o1 + o2, jnp.where(h1, l1, l2)
                for t, rw in enumerate(rows):
                    og[g][rw, :] = o[t]
                    lg[g][rw, :] = l[t]
                return 0

            lax.fori_loop(0, dil * nb // DIL_BATCH, some, 0)

        def comb(j, _):
            rows = pl.ds(pl.multiple_of(j * DIL_CHUNK, DIL_CHUNK), DIL_CHUNK)
            ls = [lg[g][rows, :] for g in range(n_pat)]
            m = jnp.maximum(jnp.maximum(ls[0], ls[1]), ls[2])
            es = [jnp.exp(l - m) for l in ls]
            den = es[0] + es[1] + es[2]
            o = (es[0] * og[0][rows, :] + es[1] * og[1][rows, :] + es[2] * og[2][rows, :]) / den
            o16_ref[rows, :] = o.astype(BF16)
            o32_ref[rows, :] = o
            l_ref[rows, :] = m + jnp.log(den)
            return 0

        lax.fori_loop(0, s_dim // DIL_CHUNK, comb, 0)
        end()

    col, tab, vec = _dil_specs(b_dim, s_dim)
    out = pl.BlockSpec((None, s_dim, LANES),lambda b, h: (b, 0, h))
    shp = (b_dim, s_dim, ATT_WIDTH)
    res = pl.pallas_call(
        body, name="dil_fwd", grid=(b_dim, PAIRS),
        in_specs=[col(COL_QA), col(COL_KA), col(COL_VA), tab, tab, vec, vec] + [ANY] * n_w,
        out_specs=[out, out, out] + [ANY] * n_w,
        out_shape=[jax.ShapeDtypeStruct(shp, BF16), jax.ShapeDtypeStruct(shp, F32), jax.ShapeDtypeStruct(shp, F32)]
        + extra_shapes,
        scratch_shapes=[pltpu.VMEM((s_dim, LANES), F32)] * (2 + 2 * n_pat) + extra_sems,
        compiler_params=_params(("arbitrary", "arbitrary")),
    )(proj3, proj3, proj3, cs3, sn3, sel_a, sel_b, *extra)
    return res[:3], res[3:]


def _dil_bwd(proj3, cs3, sn3, sel_a, sel_b, do3, o3, lse3, rider=None):
    b_dim, s_dim, _ = proj3.shape
    scale = HEAD_DIM ** -0.5
    extra, extra_shapes, extra_sems = _rider_parts(rider)
    n_w = len(extra)

    def body(*refs):
        q_ref, k_ref, v_ref, cs_ref, sn_ref, sa_ref, sb_ref, do_ref, o_ref, l_ref = refs[:10]
        dq_ref, dk_ref, dv_ref = refs[10 + n_w:13 + n_w]
        qr, kr, dqa, dka, dva = refs[13 + 2 * n_w:18 + 2 * n_w]
        step = pl.program_id(0) * PAIRS + pl.program_id(1)
        begin, end = _rider_hooks(rider, refs[10:10 + n_w], refs[13 + n_w:13 + 2 * n_w], refs[-2:], step,
                                  b_dim * PAIRS)
        begin()
        h1, h2 = _head_masks()
        cur_ok, prev_ok = _tri_masks()
        sa, sb = sa_ref[...], sb_ref[...]

        def prep(j, _):
            rows = pl.ds(pl.multiple_of(j * DIL_CHUNK, DIL_CHUNK), DIL_CHUNK)
            cs, sn = cs_ref[rows, :], sn_ref[rows, :]
            qr[rows, :] = _rotate(q_ref[rows, :], cs, sn, sa, sb) * scale
            kr[rows, :] = _rotate(k_ref[rows, :], cs, sn, sa, sb)
            zero = jnp.zeros((DIL_CHUNK, LANES), F32)
            dqa[rows, :] = zero
            dka[rows, :] = zero
            dva[rows, :] = zero
            return 0

        lax.fori_loop(0, s_dim // DIL_CHUNK, prep, 0)

        for _, dil in DIL_PATTERNS:
            nb = s_dim // dil // BLOCK

            def some(bi, _, dil=dil, nb=nb):
                tiles = [_dil_tile(bi * DIL_BATCH + t, dil, nb) for t in range(DIL_BATCH)]
                rows = [t[0] for t in tiles]
                q1, q2 = _split_heads(jnp.stack([qr[rw, :] for rw in rows]), h1, h2)
                dof = jnp.stack([do_ref[rw, :] for rw in rows])
                do1, do2 = _split_heads(dof, h1, h2)
                prod = dof * jnp.stack([o_ref[rw, :] for rw in rows])
                delta1 = jnp.sum(jnp.where(h1, prod, 0.0), axis=-1, keepdims=True)
                delta2 = jnp.sum(jnp.where(h2, prod, 0.0), axis=-1, keepdims=True)
                lt = jnp.stack([l_ref[rw, :] for rw in rows])
                lse1 = jnp.max(jnp.where(h1, lt, -jnp.inf), axis=-1, keepdims=True)
                lse2 = jnp.max(jnp.where(h2, lt, -jnp.inf), axis=-1, keepdims=True)

                def side(krows, ok):
                    kf = jnp.stack([kr[kw, :] for kw in krows])
                    k16 = kf.astype(BF16)
                    k1, k2 = _split_heads(kf, h1, h2)
                    v16 = jnp.stack([v_ref[kw, :] for kw in krows]).astype(BF16)

                    def head(qh, doh, lse, delta):
                        sc = lax.dot_general(qh, k16, BNT, preferred_element_type=F32)
                        p = jnp.where(ok, jnp.exp(sc - lse), 0.0)
                        dp = lax.dot_general(doh, v16, BNT, preferred_element_type=F32)
                        return p.astype(BF16), (p * (dp - delta)).astype(BF16)

                    p1, ds1 = head(q1, do1, lse1, delta1)
                    p2, ds2 = head(q2, do2, lse2, delta2)
                    dv = (lax.dot_general(p1, do1, BTN, preferred_element_type=F32)
                          + lax.dot_general(p2, do2, BTN, preferred_element_type=F32))
                    dk = (lax.dot_general(ds1, q1, BTN, preferred_element_type=F32)
                          + lax.dot_general(ds2, q2, BTN, preferred_element_type=F32))
                    for t, kw in enumerate(krows):
                        dva[kw, :] += dv[t]
                        dka[kw, :] += dk[t]
                    return (lax.dot_general(ds1, k1, BNN, preferred_element_type=F32)
                            + lax.dot_general(ds2, k2, BNN, preferred_element_type=F32))

                dq = side(rows, cur_ok)
                if nb > 1:
                    dq = dq + side([t[1] for t in tiles], jnp.stack([jnp.logical_and(prev_ok, t[2]) for t in tiles]))
                for t, rw in enumerate(rows):
                    dqa[rw, :] += dq[t] * scale
                return 0

            lax.fori_loop(0, dil * nb // DIL_BATCH, some, 0)

        def finish(j, _):
            rows = pl.ds(pl.multiple_of(j * DIL_CHUNK, DIL_CHUNK), DIL_CHUNK)
            cs, sn = cs_ref[rows, :], -sn_ref[rows, :]
            dq_ref[rows, :] = _rotate(dqa[rows, :], cs, sn, sa, sb).astype(BF16)
            dk_ref[rows, :] = _rotate(dka[rows, :], cs, sn, sa, sb).astype(BF16)
            dv_ref[rows, :] = dva[rows, :].astype(BF16)
            return 0

        lax.fori_loop(0, s_dim // DIL_CHUNK, finish, 0)
        end()

    col, tab, vec = _dil_specs(b_dim, s_dim)
    out = pl.BlockSpec((None, s_dim, LANES),lambda b, h: (b, 0, h))
    shp = jax.ShapeDtypeStruct((b_dim, s_dim, ATT_WIDTH), BF16)
    acc = pltpu.VMEM((s_dim, LANES), F32)
    res = pl.pallas_call(
        body, name="dil_bwd", grid=(b_dim, PAIRS),
        in_specs=[col(COL_QA), col(COL_KA), col(COL_VA), tab, tab, vec, vec, out, out, out] + [ANY] * n_w,
        out_specs=[out, out, out] + [ANY] * n_w, out_shape=[shp, shp, shp] + extra_shapes,
        scratch_shapes=[acc, acc, acc, acc, acc] + extra_sems,
        compiler_params=_params(("arbitrary", "arbitrary")),
    )(proj3, proj3, proj3, cs3, sn3, sel_a, sel_b, do3, o3, lse3, *extra)
    return res[:3], res[3:]


def _split_dot(x, tri):
    hi = x.astype(BF16)
    lo = (x - hi.astype(F32)).astype(BF16)
    return jnp.dot(hi, tri, preferred_element_type=F32) + jnp.dot(lo, tri, preferred_element_type=F32)


LOG2E = 1.4426950408889634
LN2 = 0.6931471805599453


def _log2_sigmoid(z):
    return jnp.minimum(z, 0.0) - jnp.log2(1.0 + jnp.exp2(-jnp.abs(z)))


def _sb_scores(qh, k16, valid):
    z = lax.dot_general(qh, k16, NT, preferred_element_type=F32)
    ls = _log2_sigmoid(z)
    l1m = ls - z
    return ls, (l1m if valid is None else jnp.where(valid, l1m, 0.0))


def _sb_consts():
    r = lax.broadcasted_iota(jnp.int32, (BLOCK, BLOCK), 0)
    c = lax.broadcasted_iota(jnp.int32, (BLOCK, BLOCK), 1)
    after = (r > c).astype(BF16)
    before = (r < c).astype(BF16)
    qrow = lax.broadcasted_iota(jnp.int32, (SB_ROWS, BLOCK), 0)
    kcol = lax.broadcasted_iota(jnp.int32, (SB_ROWS, BLOCK), 1)
    return after, before, qrow, kcol


def _below(whole, lo, delta):
    if lo == 0:
        return whole + delta
    return whole + jnp.concatenate([jnp.zeros((lo,) + delta.shape[1:], delta.dtype), delta], axis=0)


def _pairs_loop(n_blocks, step, carry, per_iter):
    def several(i, c):
        for j in range(per_iter):
            c = step(per_iter * i + j, c)
        return c

    return lax.fori_loop(0, n_blocks // per_iter, several, carry)


def _sb_fwd(proj3, rider=None):
    b_dim, s_dim, _ = proj3.shape
    scale = HEAD_DIM ** -0.5
    per = SB_ROWS // BLOCK
    extra, extra_shapes, extra_sems = _rider_parts(rider)
    n_w = len(extra)

    def body(*refs):
        q_ref, k_ref, v_ref = refs[:3]
        o_ref = refs[3 + n_w]
        step = pl.program_id(0) * PAIRS + pl.program_id(1)
        begin, end = _rider_hooks(rider, refs[3:3 + n_w], refs[4 + n_w:4 + 2 * n_w], refs[-2:], step, b_dim * PAIRS)
        begin()
        h1, h2 = _head_masks()
        after, _, qrow, kcol = _sb_consts()

        def qloop(qi, _):
            rows = pl.ds(pl.multiple_of(qi * SB_ROWS, SB_ROWS), SB_ROWS)
            q1, q2 = _split_heads(q_ref[rows, :] * (scale * LOG2E), h1, h2)
            first = qi * per

            def block(kb, carry, lo):
                acc, run1, run2 = carry
                krows = pl.ds(pl.multiple_of(kb * BLOCK, BLOCK), BLOCK)
                k16 = k_ref[krows, :].astype(BF16)
                v1, v2 = _split_heads(v_ref[krows, :], h1, h2)
                valid = None if lo is None else kcol[:SB_ROWS - lo] < qrow[:SB_ROWS - lo]
                lo = lo or 0

                def head(qh, vh, run):
                    ls, l1m = _sb_scores(qh[lo:], k16, valid)
                    a = jnp.exp2(ls + _split_dot(l1m, after) + run[lo:])
                    if valid is not None:
                        a = jnp.where(valid, a, 0.0)
                    return (jnp.dot(a.astype(BF16), vh, preferred_element_type=F32),
                            _below(run, lo, jnp.sum(l1m, axis=-1, keepdims=True)))

                o1, run1 = head(q1, v1, run1)
                o2, run2 = head(q2, v2, run2)
                return _below(acc, lo, o1 + o2), run1, run2

            zcol = jnp.zeros((SB_ROWS, 1), F32)
            carry = (jnp.zeros((SB_ROWS, LANES), F32), zcol, zcol)
            for kl in reversed(range(per)):
                carry = block(first + kl, carry, kl * BLOCK)
            acc, _, _ = _pairs_loop(first, lambda i, c: block(first - 1 - i, c, None), carry, SB_STEP_FWD)
            o_ref[rows, :] = acc.astype(BF16)
            return 0

        lax.fori_loop(0, s_dim // SB_ROWS, qloop, 0)
        end()

    def col(c0):
        return pl.BlockSpec((None, s_dim, LANES),lambda b, h: (b, 0, c0 + h))

    res = pl.pallas_call(
        body, name="sb_fwd", grid=(b_dim, PAIRS),
        in_specs=[col(COL_QB), col(COL_KB), col(COL_VB)] + [ANY] * n_w, out_specs=[col(0)] + [ANY] * n_w,
        out_shape=[jax.ShapeDtypeStruct((b_dim, s_dim, ATT_WIDTH), BF16)] + extra_shapes,
        scratch_shapes=extra_sems,
        compiler_params=_params(("arbitrary", "arbitrary")),
    )(proj3, proj3, proj3, *extra)
    return res[0], res[1:]


def _sb_bwd(proj3, do3, rider=None):
    b_dim, s_dim, _ = proj3.shape
    scale = HEAD_DIM ** -0.5
    per = SB_ROWS // BLOCK
    nkb_max = s_dim // BLOCK
    extra, extra_shapes, extra_sems = _rider_parts(rider)
    n_w = len(extra)

    def body(*refs):
        q_ref, k_ref, v_ref, do_ref = refs[:4]
        dq_ref, dk_ref, dv_ref = refs[4 + n_w:7 + n_w]
        dka, dva, e_ref, sg_ref = refs[7 + 2 * n_w:11 + 2 * n_w]
        step = pl.program_id(0) * PAIRS + pl.program_id(1)
        begin, end = _rider_hooks(rider, refs[4:4 + n_w], refs[7 + n_w:7 + 2 * n_w], refs[-2:], step, b_dim * PAIRS)
        begin()
        h1, h2 = _head_masks()
        after, before, qrow, kcol = _sb_consts()
        dka[...] = jnp.zeros_like(dka)
        dva[...] = jnp.zeros_like(dva)

        def qloop(qi, _):
            rows = pl.ds(pl.multiple_of(qi * SB_ROWS, SB_ROWS), SB_ROWS)
            q1, q2 = _split_heads(q_ref[rows, :] * (scale * LOG2E), h1, h2)
            do1, do2 = _split_heads(do_ref[rows, :].astype(F32), h1, h2)
            first = qi * per

            def pass1(kb, carry, lo):
                run1, run2 = carry
                krows = pl.ds(pl.multiple_of(kb * BLOCK, BLOCK), BLOCK)
                k16 = k_ref[krows, :].astype(BF16)
                v16 = v_ref[krows, :].astype(BF16)
                valid = None if lo is None else kcol[:SB_ROWS - lo] < qrow[:SB_ROWS - lo]
                lo = lo or 0
                part = pl.ds(lo, SB_ROWS - lo)

                def head(h, qh, doh, run):
                    ls, l1m = _sb_scores(qh[lo:], k16, valid)
                    a = jnp.exp2(ls + _split_dot(l1m, after) + run[lo:])
                    if valid is not None:
                        a = jnp.where(valid, a, 0.0)
                    da = lax.dot_general(doh[lo:], v16, NT, preferred_element_type=F32)
                    e_ref[h, kb, part, :] = a * da
                    sg_ref[h, kb, part, :] = jnp.exp2(ls)
                    return a.astype(BF16), _below(run, lo, jnp.sum(l1m, axis=-1, keepdims=True))

                a1, run1 = head(0, q1, do1, run1)
                a2, run2 = head(1, q2, do2, run2)
                dva[krows, :] += (lax.dot_general(a1, do1[lo:], TN, preferred_element_type=F32)
                                  + lax.dot_general(a2, do2[lo:], TN, preferred_element_type=F32))
                return run1, run2

            zcol = jnp.zeros((SB_ROWS, 1), F32)
            carry = (zcol, zcol)
            for kl in reversed(range(per)):
                carry = pass1(first + kl, carry, kl * BLOCK)
            _pairs_loop(first, lambda i, c: pass1(first - 1 - i, c, None), carry, SB_STEP_BWD)

            def pass2(kb, carry, lo):
                dq, pre1, pre2 = carry
                krows = pl.ds(pl.multiple_of(kb * BLOCK, BLOCK), BLOCK)
                k1, k2 = _split_heads(k_ref[krows, :], h1, h2)
                valid = None if lo is None else kcol[:SB_ROWS - lo] < qrow[:SB_ROWS - lo]
                lo = lo or 0
                part = pl.ds(lo, SB_ROWS - lo)

                def head(h, pre):
                    ev = e_ref[h, kb, part, :]
                    sg = sg_ref[h, kb, part, :]
                    dz = ev * (1.0 - sg) - (_split_dot(ev, before) + pre[lo:]) * sg
                    if valid is not None:
                        dz = jnp.where(valid, dz, 0.0)
                    return dz.astype(BF16), _below(pre, lo, jnp.sum(ev, axis=-1, keepdims=True))

                dz1, pre1 = head(0, pre1)
                dz2, pre2 = head(1, pre2)
                dka[krows, :] += (lax.dot_general(dz1, q1[lo:], TN, preferred_element_type=F32)
                                  + lax.dot_general(dz2, q2[lo:], TN, preferred_element_type=F32))
                dq = _below(dq, lo, jnp.dot(dz1, k1, preferred_element_type=F32)
                            + jnp.dot(dz2, k2, preferred_element_type=F32))
                return dq, pre1, pre2

            carry = _pairs_loop(first, lambda i, c: pass2(i, c, None), (jnp.zeros((SB_ROWS, LANES), F32), zcol, zcol),
                                SB_STEP_BWD)
            for kl in range(per):
                carry = pass2(first + kl, carry, kl * BLOCK)
            dq = carry[0]
            dq_ref[rows, :] = (dq * scale).astype(BF16)
            return 0

        lax.fori_loop(0, s_dim // SB_ROWS, qloop, 0)
        dk_ref[...] = (dka[...] * LN2).astype(BF16)
        dv_ref[...] = dva[...].astype(BF16)
        end()

    def col(c0):
        return pl.BlockSpec((None, s_dim, LANES),lambda b, h: (b, 0, c0 + h))

    shp = jax.ShapeDtypeStruct((b_dim, s_dim, ATT_WIDTH), BF16)
    acc = pltpu.VMEM((s_dim, LANES), F32)
    strip = pltpu.VMEM((2, nkb_max, SB_ROWS, BLOCK), F32)
    res = pl.pallas_call(
        body, name="sb_bwd", grid=(b_dim, PAIRS),
        in_specs=[col(COL_QB), col(COL_KB), col(COL_VB), col(0)] + [ANY] * n_w,
        out_specs=[col(0), col(0), col(0)] + [ANY] * n_w,
        out_shape=[shp, shp, shp] + extra_shapes,
        scratch_shapes=[acc, acc, strip, strip] + extra_sems,
        compiler_params=_params(("arbitrary", "arbitrary")),
    )(proj3, proj3, proj3, do3, *extra)
    return res[:3], res[3:]


def _sigmoid(x):
    return 1.0 / (1.0 + jnp.exp(-x))


def _gate_out_norm(proj, ua, ub, w_out, x, g, *, tt=512):
    t_dim, d = ua.shape

    def body(ga_ref, gb_ref, ua_ref, ub_ref, w_ref, x_ref, g_ref, m_ref, h_ref, n_ref):
        mixed = (_sigmoid(ga_ref[...]) * ua_ref[...] + _sigmoid(gb_ref[...]) * ub_ref[...]).astype(BF16)
        m_ref[...] = mixed
        hv = x_ref[...] + jnp.dot(mixed, w_ref[...], preferred_element_type=F32)
        h_ref[...] = hv
        r = lax.rsqrt(jnp.mean(hv * hv, axis=-1, keepdims=True) + RMS_EPS)
        n_ref[...] = ((hv * r) * g_ref[...]).astype(BF16)

    row = pl.BlockSpec((tt, d), lambda i: (i, 0))
    return pl.pallas_call(
        body, name="gate_out_norm", grid=(t_dim // tt,),
        in_specs=[pl.BlockSpec((tt, d), lambda i: (i, 3)), pl.BlockSpec((tt, d), lambda i: (i, 4)), row, row,
                  pl.BlockSpec((d, d), lambda i: (0, 0)), row, pl.BlockSpec((1, d), lambda i: (0, 0))],
        out_specs=[row, row, row],
        out_shape=[jax.ShapeDtypeStruct((t_dim, d), BF16), jax.ShapeDtypeStruct((t_dim, d), F32),
                   jax.ShapeDtypeStruct((t_dim, d), BF16)],
        compiler_params=_params(("parallel",)),
    )(proj, proj, ua, ub, w_out, x, g)


def _out_dx_gate_bwd(dh, w_out, proj, ua, ub, *, tt=512):
    t_dim, d = ua.shape

    def body(dh_ref, w_ref, ga_ref, gb_ref, ua_ref, ub_ref, dua_ref, dub_ref, dg_ref):
        dm = lax.dot_general(dh_ref[...].astype(BF16), w_ref[...], NT, preferred_element_type=F32)
        sa = _sigmoid(ga_ref[...])
        sb = _sigmoid(gb_ref[...])
        dua_ref[...] = (dm * sa).astype(BF16)
        dub_ref[...] = (dm * sb).astype(BF16)
        dg_ref[:, :d] = (dm * ua_ref[...] * (sa * (1.0 - sa))).astype(BF16)
        dg_ref[:, d:] = (dm * ub_ref[...] * (sb * (1.0 - sb))).astype(BF16)

    row = pl.BlockSpec((tt, d), lambda i: (i, 0))
    wide = pl.BlockSpec((tt, 2 * d), lambda i: (i, 0))
    return pl.pallas_call(
        body, name="out_dx_gate_bwd", grid=(t_dim // tt,),
        in_specs=[row, pl.BlockSpec((d, d), lambda i: (0, 0)),
                  pl.BlockSpec((tt, d), lambda i: (i, 3)), pl.BlockSpec((tt, d), lambda i: (i, 4)), row, row],
        out_specs=[row, row, wide],
        out_shape=[jax.ShapeDtypeStruct((t_dim, d), BF16), jax.ShapeDtypeStruct((t_dim, d), BF16),
                   jax.ShapeDtypeStruct((t_dim, 2 * d), BF16)],
        compiler_params=_params(("parallel",)),
    )(dh, w_out, proj, proj, ua, ub)


def _ffn_up_swiglu(n, wg3, wu3, *, tt=512):
    t_dim, d = n.shape
    n_s, f4, _ = wg3.shape

    def body(n_ref, wg_ref, wu_ref, g_ref, u_ref, a_ref):
        nv = n_ref[...]
        for s in range(n_s):
            gv = lax.dot_general(nv, wg_ref[s], NT, preferred_element_type=F32)
            uv = lax.dot_general(nv, wu_ref[s], NT, preferred_element_type=F32)
            g_ref[s] = gv.astype(BF16)
            u_ref[s] = uv.astype(BF16)
            a_ref[s] = (gv * _sigmoid(gv) * uv).astype(BF16)

    wspec = pl.BlockSpec(wg3.shape, lambda i: (0, 0, 0))
    ospec = pl.BlockSpec((n_s, tt, f4), lambda i: (0, i, 0))
    shp = (n_s, t_dim, f4)
    return pl.pallas_call(
        body, name="ffn_up_swiglu", grid=(t_dim // tt,),
        in_specs=[pl.BlockSpec((tt, d), lambda i: (i, 0)), wspec, wspec], out_specs=[ospec, ospec, ospec],
        out_shape=[jax.ShapeDtypeStruct(shp, BF16)] * 3,
        compiler_params=_params(("parallel",)),
    )(n, wg3, wu3)


def _ffn_down_dx_swiglu(dh, wd3, g3, u3, *, tt=512):
    t_dim, d = dh.shape
    n_s, f4, _ = wd3.shape

    def body(dh_ref, w_ref, g_ref, u_ref, dg_ref, du_ref):
        dhv = dh_ref[...].astype(BF16)
        for s in range(n_s):
            da = lax.dot_general(dhv, w_ref[s], NT, preferred_element_type=F32)
            gv = g_ref[s].astype(F32)
            sg = _sigmoid(gv)
            dg_ref[s] = (da * u_ref[s].astype(F32) * (sg + gv * sg * (1.0 - sg))).astype(BF16)
            du_ref[s] = (da * (gv * sg)).astype(BF16)

    spec = pl.BlockSpec((n_s, tt, f4), lambda i: (0, i, 0))
    shp = jax.ShapeDtypeStruct((n_s, t_dim, f4), BF16)
    return pl.pallas_call(
        body, name="ffn_down_dx_swiglu", grid=(t_dim // tt,),
        in_specs=[pl.BlockSpec((tt, d), lambda i: (i, 0)), pl.BlockSpec(wd3.shape, lambda i: (0, 0, 0)), spec, spec],
        out_specs=[spec, spec], out_shape=[shp, shp],
        compiler_params=_params(("parallel",)),
    )(dh, wd3, g3, u3)


def _mem_fwd(qm, kvm, *, tt=2048):
    b_dim, s_dim, _ = qm.shape
    n_mem = kvm.shape[1]
    scale = MEM_HEAD_DIM ** -0.5

    def body(q_ref, k_ref, v_ref, o_ref):
        sc = lax.dot_general(q_ref[0], k_ref[0], NT, preferred_element_type=F32) * scale
        p = jnp.exp(sc - jnp.max(sc, axis=-1, keepdims=True))
        p = p / jnp.sum(p, axis=-1, keepdims=True)
        o_ref[0] = jnp.dot(p.astype(BF16), v_ref[0], preferred_element_type=F32).astype(BF16)

    qs = pl.BlockSpec((1, tt, MEM_HEAD_DIM), lambda b, h, i: (b, i, h))
    return pl.pallas_call(
        body, name="mem_fwd", grid=(b_dim, N_HEADS_MEM, s_dim // tt),
        in_specs=[qs, pl.BlockSpec((1, n_mem, MEM_HEAD_DIM), lambda b, h, i: (b, 0, h)),
                  pl.BlockSpec((1, n_mem, MEM_HEAD_DIM), lambda b, h, i: (b, 0, N_HEADS_MEM + h))],
        out_specs=qs, out_shape=jax.ShapeDtypeStruct(qm.shape, BF16),
        compiler_params=_params(("parallel", "parallel", "parallel")),
    )(qm, kvm, kvm)


def _mem_bwd(qm, kvm, dom, *, tt=2048):
    b_dim, s_dim, _ = qm.shape
    n_mem = kvm.shape[1]
    scale = MEM_HEAD_DIM ** -0.5

    def body(q_ref, k_ref, v_ref, do_ref, dq_ref, dk_ref, dv_ref):
        qv, kv, vv, dov = q_ref[0], k_ref[0], v_ref[0], do_ref[0]
        sc = lax.dot_general(qv, kv, NT, preferred_element_type=F32) * scale
        p = jnp.exp(sc - jnp.max(sc, axis=-1, keepdims=True))
        p = p / jnp.sum(p, axis=-1, keepdims=True)
        dp = lax.dot_general(dov, vv, NT, preferred_element_type=F32)
        ds = (p * (dp - jnp.sum(p * dp, axis=-1, keepdims=True)) * scale).astype(BF16)
        dq_ref[0] = jnp.dot(ds, kv, preferred_element_type=F32).astype(BF16)

        @pl.when(pl.program_id(2) == 0)
        def _():
            dk_ref[...] = jnp.zeros_like(dk_ref)
            dv_ref[...] = jnp.zeros_like(dv_ref)

        dk_ref[0] += lax.dot_general(ds, qv, TN, preferred_element_type=F32)
        dv_ref[0] += lax.dot_general(p.astype(BF16), dov, TN, preferred_element_type=F32)

    qs = pl.BlockSpec((1, tt, MEM_HEAD_DIM), lambda b, h, i: (b, i, h))
    ks = pl.BlockSpec((1, n_mem, MEM_HEAD_DIM), lambda b, h, i: (b, 0, h))
    vs = pl.BlockSpec((1, n_mem, MEM_HEAD_DIM), lambda b, h, i: (b, 0, N_HEADS_MEM + h))
    return pl.pallas_call(
        body, name="mem_bwd", grid=(b_dim, N_HEADS_MEM, s_dim // tt),
        in_specs=[qs, ks, vs, qs], out_specs=[qs, ks, ks],
        out_shape=[jax.ShapeDtypeStruct(qm.shape, BF16), jax.ShapeDtypeStruct((b_dim, n_mem, MEM_WIDTH), F32),
                   jax.ShapeDtypeStruct((b_dim, n_mem, MEM_WIDTH), F32)],
        compiler_params=_params(("parallel", "parallel", "arbitrary")),
    )(qm, kvm, kvm, dom)


def _adamw_math(wv, gv, mv, vv):
    nm = ADAM_B1 * mv + (1.0 - ADAM_B1) * gv
    nv = ADAM_B2 * vv + (1.0 - ADAM_B2) * (gv * gv)
    m_hat = nm / (1.0 - ADAM_B1 ** ADAM_STEP)
    v_hat = nv / (1.0 - ADAM_B2 ** ADAM_STEP)
    return -ADAM_LR * (m_hat / (jnp.sqrt(v_hat) + ADAM_EPS) + ADAM_WD * wv), nm, nv


def _adamw(w, g, m, v, *, name):
    rows, cols = w.shape
    tr = _tile(rows, 256, 8)

    def body(w_ref, g_ref, m_ref, v_ref, d_ref, nm_ref, nv_ref):
        d_ref[...], nm_ref[...], nv_ref[...] = _adamw_math(w_ref[...], g_ref[...], m_ref[...], v_ref[...])

    spec = pl.BlockSpec((tr, cols), lambda i: (i, 0))
    shp = jax.ShapeDtypeStruct((rows, cols), F32)
    return pl.pallas_call(
        body, name=name, grid=(rows // tr,),
        in_specs=[spec] * 4, out_specs=[spec] * 3, out_shape=[shp] * 3,
        compiler_params=_params(("parallel",)),
    )(w, g, m, v)


def _prefetch_spec(grid, in_specs, out_specs):
    return pltpu.PrefetchScalarGridSpec(num_scalar_prefetch=1, grid=grid, in_specs=in_specs, out_specs=out_specs)


def _adamw_halves(w, mine, theirs, m, v, c_idx, *, name):
    rows, cols = w.shape
    half = rows // 2
    tr = _tile(half, _row_cap(cols), 8)
    nh = half // tr

    def body(c_ref, w_ref, mine_ref, theirs_ref, m_ref, v_ref, g_ref, d_ref, nm_ref, nv_ref):
        gv = jnp.where(pl.program_id(0) == c_ref[0], mine_ref[...], theirs_ref[...])
        g_ref[...] = gv
        d_ref[...], nm_ref[...], nv_ref[...] = _adamw_math(w_ref[...], gv, m_ref[...], v_ref[...])

    full = pl.BlockSpec((tr, cols), lambda h, i, c_ref: (h * nh + i, 0))
    part = pl.BlockSpec((tr, cols), lambda h, i, c_ref: (i, 0))
    shp = jax.ShapeDtypeStruct((rows, cols), F32)
    return pl.pallas_call(
        body, name=name, grid_spec=_prefetch_spec((2, nh), [full, part, part, full, full], [full] * 4),
        out_shape=[shp] * 4,
        compiler_params=_params(("parallel", "parallel")),
    )(c_idx, w, mine, theirs, m, v)


def _pair_sum(g3, theirs, c_idx, *, name):
    n, rows, cols = g3.shape
    half = rows // 2
    tr = _tile(half, _row_cap(cols), 16)

    def body(c_ref, g_ref, t_ref, o_ref):
        o_ref[...] = (g_ref[...] + t_ref[...]).astype(BF16)

    part = pl.BlockSpec((None, tr, cols), lambda s, i, c_ref: (s, i, 0))
    return pl.pallas_call(
        body, name=name,
        grid_spec=_prefetch_spec((n, half // tr),
                                 [pl.BlockSpec((None, None, tr, cols), lambda s, i, c_ref: (s, c_ref[0], i, 0)), part],
                                 part),
        out_shape=jax.ShapeDtypeStruct((n, half, cols), BF16),
        compiler_params=_params(("parallel", "parallel")),
    )(c_idx, g3.reshape(n, 2, half, cols), theirs)


def _chip_sum(pair, recv, s_idx, *, name):
    _, half, cols = pair.shape
    tr = _tile(half, _row_cap(cols), 16)

    def body(s_ref, p_ref, r_ref, o_ref):
        o_ref[...] = ((p_ref[...].astype(F32) + r_ref[0].astype(F32)) + r_ref[1].astype(F32)) + r_ref[2].astype(F32)

    return pl.pallas_call(
        body, name=name,
        grid_spec=_prefetch_spec((half // tr,),
                                 [pl.BlockSpec((None, tr, cols), lambda i, s_ref: (s_ref[0], i, 0)),
                                  pl.BlockSpec((N_CHIPS - 1, tr, cols), lambda i, s_ref: (0, i, 0))],
                                 pl.BlockSpec((tr, cols), lambda i, s_ref: (i, 0))),
        out_shape=jax.ShapeDtypeStruct((half, cols), F32),
        compiler_params=_params(("parallel",)),
    )(s_idx, pair, recv)


def _sum8(parts):
    n, rows, cols = parts.shape

    def body(p_ref, o_ref):
        acc = p_ref[0]
        for i in range(1, n):
            acc = acc + p_ref[i]
        o_ref[...] = acc

    return pl.pallas_call(
        body, name="small_sum", grid=(1,),
        in_specs=[pl.BlockSpec((n, rows, cols), lambda i: (0, 0, 0))],
        out_specs=pl.BlockSpec((rows, cols), lambda i: (0, 0)),
        out_shape=jax.ShapeDtypeStruct((rows, cols), parts.dtype),
        compiler_params=_params(("arbitrary",)),
    )(parts)


def _place():
    return lax.axis_index("x"), lax.axis_index("y"), lax.axis_index("c")


ANY = pl.BlockSpec(memory_space=pl.ANY)


def _rider_parts(rider):
    if rider is None:
        return (), [], []
    kind, arrays = rider
    n = len(arrays)
    shapes = {"gather": _gathered_shapes, "pair": _pair_shapes, "chip": _chip_shapes}[kind](arrays)
    sems = _gather_sems(n) if kind == "gather" else _exchange_sems(n if kind == "pair" else 3 * n)
    return tuple(arrays), shapes, sems


def _rider_hooks(rider, ins, outs, sems, step, n_steps):
    if rider is None:
        return (lambda: None), (lambda: None)
    if rider[0] == "gather":
        start, forward, finish = _gather_steps(ins, outs, *sems)
    else:
        start, finish = {"pair": _pair_steps, "chip": _chip_steps}[rider[0]](ins, outs, *sems)
        forward = None

    def begin():
        pl.when(step == 0)(start)

    def end():
        if forward is not None:
            pl.when(step == n_steps - 2)(forward)
        pl.when(step == n_steps - 1)(finish)

    return begin, end


def _exchange_alone(rider, *, name):
    extra, shapes, sems = _rider_parts(rider)
    n = len(extra)

    def body(*refs):
        begin, end = _rider_hooks(rider, refs[:n], refs[n:2 * n], refs[-2:], jnp.int32(0), 1)
        begin()
        end()

    return pl.pallas_call(
        body, name=name, out_shape=shapes, in_specs=[ANY] * n, out_specs=[ANY] * n, scratch_shapes=sems,
    )(*extra)


def _gathered_shapes(shards):
    return [jax.ShapeDtypeStruct((N_CHIPS,) + s.shape, s.dtype) for s in shards]


def _gather_sems(n):
    return [pltpu.SemaphoreType.DMA((7 * n,)), pltpu.SemaphoreType.DMA((7 * n,))]


def _gather_steps(ins, outs, send_sems, recv_sems):
    n = len(ins)
    halves = [r.shape[0] // 2 for r in ins]
    x, y, c = _place()
    my_chip = 2 * x + y
    me, sibling = (x, y, c), (x, y, 1 - c)
    chips = [(1 - x, y), (x, 1 - y), (1 - x, 1 - y)]

    def half_of(w, chip, pc):
        return outs[w].at[chip, pl.ds(pc * halves[w], halves[w]), :]

    def copy(w, k, src, dst, to):
        return pltpu.make_async_remote_copy(
            src_ref=src, dst_ref=dst, send_sem=send_sems.at[7 * w + k], recv_sem=recv_sems.at[7 * w + k],
            device_id=to, device_id_type=MESH)

    def firsts():
        cps = []
        for w in range(n):
            cps.append(copy(w, 0, ins[w], outs[w].at[my_chip], sibling))
            mine = ins[w].at[pl.ds(c * halves[w], halves[w]), :]
            for j, (px, py) in enumerate(chips):
                cps.append(copy(w, 1 + j, mine, half_of(w, my_chip, c), (px, py, c)))
        return cps

    def passes():
        return [copy(w, 4 + j, half_of(w, 2 * px + py, c), half_of(w, 2 * px + py, c), sibling)
                for w in range(n) for j, (px, py) in enumerate(chips)]

    def start():
        for cp in firsts():
            cp.start()

    def forward():
        fws = passes()
        for w in range(n):
            for j, (px, py) in enumerate(chips):
                landed = half_of(w, 2 * px + py, c)
                copy(w, 1 + j, landed, landed, me).wait_recv()
                fws[3 * w + j].start()

    def finish():
        for w in range(n):
            copy(w, 0, ins[w], outs[w].at[my_chip], me).wait_recv()
            for j, (px, py) in enumerate(chips):
                landed = half_of(w, 2 * px + py, 1 - c)
                copy(w, 4 + j, landed, landed, me).wait_recv()
        for cp in firsts() + passes():
            cp.wait_send()

    return start, forward, finish


def _pair_shapes(grads):
    return [jax.ShapeDtypeStruct((g.shape[0], g.shape[1] // 2, g.shape[2]), g.dtype) for g in grads]


def _exchange_sems(n):
    return [pltpu.SemaphoreType.DMA((n,)), pltpu.SemaphoreType.DMA((n,))]


def _exchange_steps(copies):
    def start():
        for cp in copies():
            cp.start()

    def finish():
        for cp in copies():
            cp.wait()

    return start, finish


def _pair_steps(ins, outs, send_sems, recv_sems):
    x, y, c = _place()

    def copies():
        return [pltpu.make_async_remote_copy(
            src_ref=ins[w].at[:, pl.ds((1 - c) * (ins[w].shape[1] // 2), ins[w].shape[1] // 2), :], dst_ref=outs[w],
            send_sem=send_sems.at[w], recv_sem=recv_sems.at[w], device_id=(x, y, 1 - c), device_id_type=MESH)
            for w in range(len(ins))]

    return _exchange_steps(copies)


def _chip_shapes(pairs):
    return [jax.ShapeDtypeStruct((N_CHIPS - 1,) + p.shape[1:], p.dtype) for p in pairs]


def _chip_steps(ins, outs, send_sems, recv_sems):
    x, y, c = _place()
    others = [(1 - x, y), (x, 1 - y), (1 - x, 1 - y)]

    def copies():
        return [pltpu.make_async_remote_copy(
            src_ref=ins[w].at[2 * px + py], dst_ref=outs[w].at[j],
            send_sem=send_sems.at[3 * w + j], recv_sem=recv_sems.at[3 * w + j],
            device_id=(px, py, c), device_id_type=MESH)
            for w in range(len(ins)) for j, (px, py) in enumerate(others)]

    return _exchange_steps(copies)


def _swap_halves(mine):
    n = len(mine)

    def body(*refs):
        ins, outs, send_sems, recv_sems = refs[:n], refs[n:2 * n], refs[2 * n], refs[2 * n + 1]
        x, y, c = _place()
        copies = [pltpu.make_async_remote_copy(
            src_ref=ins[w], dst_ref=outs[w], send_sem=send_sems.at[w], recv_sem=recv_sems.at[w],
            device_id=(x, y, 1 - c), device_id_type=MESH) for w in range(n)]
        for cp in copies:
            cp.start()
        for cp in copies:
            cp.wait()

    return pl.pallas_call(
        body, name="grad_swap_halves",
        out_shape=[jax.ShapeDtypeStruct(h.shape, h.dtype) for h in mine],
        in_specs=[ANY] * n, out_specs=[ANY] * n,
        scratch_shapes=[pltpu.SemaphoreType.DMA((n,)), pltpu.SemaphoreType.DMA((n,))],
    )(*mine)


def _gather_small(small):
    srows, cols = small.shape

    def body(s_ref, all_ref, send_sems, recv_sems, local_sem):
        x, y, c = _place()
        me = 4 * x + 2 * y + c
        keep_small = pltpu.make_async_copy(s_ref, all_ref.at[me], local_sem)
        keep_small.start()
        sends = []
        for kk in range(1, 8):
            peer = (x ^ (kk >> 2), y ^ ((kk >> 1) & 1), c ^ (kk & 1))
            sends.append(pltpu.make_async_remote_copy(
                src_ref=s_ref, dst_ref=all_ref.at[me],
                send_sem=send_sems.at[kk], recv_sem=recv_sems.at[kk], device_id=peer, device_id_type=MESH))
        for cp in sends:
            cp.start()
        for kk in range(1, 8):
            px, py, pc = x ^ (kk >> 2), y ^ ((kk >> 1) & 1), c ^ (kk & 1)
            pltpu.make_async_remote_copy(
                src_ref=s_ref, dst_ref=all_ref.at[4 * px + 2 * py + pc],
                send_sem=send_sems.at[kk], recv_sem=recv_sems.at[kk], device_id=(px, py, pc),
                device_id_type=MESH).wait_recv()
        for cp in sends:
            cp.wait_send()
        keep_small.wait()

    return pl.pallas_call(
        body, name="gather_small",
        out_shape=jax.ShapeDtypeStruct((8, srows, cols), small.dtype),
        in_specs=[ANY], out_specs=ANY,
        scratch_shapes=[pltpu.SemaphoreType.DMA((8,)), pltpu.SemaphoreType.DMA((8,)), pltpu.SemaphoreType.DMA],
    )(small)


SHARDED = (("w_in", D_MODEL, IN_COLS, 1), ("w_up_a", ATT_WIDTH, D_MODEL, 1), ("w_up_b", ATT_WIDTH, D_MODEL, 1),
           ("w_out", D_MODEL, D_MODEL, 0), ("w_q_mem", D_MODEL, MEM_WIDTH, 0), ("w_kv_mem", D_MODEL, 2 * MEM_WIDTH, 0),
           ("w_o_mem", MEM_WIDTH, D_MODEL, 1), ("w_ffn_gate", D_FF, D_MODEL, 0), ("w_ffn_up", D_FF, D_MODEL, 0),
           ("w_ffn_down", D_FF, D_MODEL, 0))
TRANSPOSED = ("w_ffn_gate", "w_ffn_up")
NAMES = tuple(n for n, _, _, _ in SHARDED)


def _held(name, shard):
    return shard.T if name in TRANSPOSED else shard
EARLY, LATE = NAMES[:1], NAMES[1:]
GAINS = ("g_mix", "g_mem_q", "g_mem_kv", "g_ffn", "g_final")


def _natural(w3):
    n, r, c = w3.shape
    return w3.reshape(n * r, c)


def _shard_major(g, axis):
    if axis == 1:
        return g
    r, c = g.shape
    return g.reshape(N_CHIPS, r // N_CHIPS, c)


def kernel(x, mem, positions, g_mix, w_in, w_up_a, w_up_b, w_out, g_mem_q, g_mem_kv, w_q_mem, w_kv_mem, w_o_mem, g_ffn, w_ffn_gate, w_ffn_up, w_ffn_down, g_final, loss_target, m_g_mix, m_w_in, m_w_up_a, m_w_up_b, m_w_out, m_g_mem_q, m_g_mem_kv, m_w_q_mem, m_w_kv_mem, m_w_o_mem, m_g_ffn, m_w_ffn_gate, m_w_ffn_up, m_w_ffn_down, m_g_final, v_g_mix, v_w_in, v_w_up_a, v_w_up_b, v_w_out, v_g_mem_q, v_g_mem_kv, v_w_q_mem, v_w_kv_mem, v_w_o_mem, v_g_ffn, v_w_ffn_gate, v_w_ffn_up, v_w_ffn_down, v_g_final):
    given = dict(locals())
    shards = {n: _held(n, given[n][0]) for n in NAMES}

    early_shards = [shards[n].astype(BF16) for n in EARLY]
    late_shards = [shards[n].astype(BF16) for n in LATE]
    c_idx = lax.axis_index("c").astype(jnp.int32).reshape(1)
    s_idx = (2 * lax.axis_index("x") + lax.axis_index("y")).astype(jnp.int32).reshape(1)

    loss_row, grad_x, mine, gain_grads = _local_step(x, mem, positions, loss_target, g_mix, g_mem_q, g_mem_kv,
                                                     g_ffn, g_final, {}, early_shards, late_shards, (c_idx, s_idx))
    return _reduce_and_update(given, shards, loss_row, grad_x, mine, gain_grads, c_idx)


def _reduce_halves(glist, names, c_idx, s_idx, pair_exchange, chip_exchange):
    theirs = pair_exchange(glist)
    pairs = [_pair_sum(g, t, c_idx, name="pair_sum_" + n) for n, g, t in zip(names, glist, theirs)]
    recv = chip_exchange(pairs)
    return [_chip_sum(p, r, s_idx, name="chip_sum_" + n) for n, p, r in zip(names, pairs, recv)]


def _local_step(x, mem, positions, loss_target, g_mix, g_mem_q, g_mem_kv, g_ffn, g_final, wf,
                early_shards=None, late_shards=None, place=None):
    b_dim, s_dim, d = x.shape
    t_dim = b_dim * s_dim
    n_mem = mem.shape[1]
    wf = dict(wf)

    xb = x.reshape(t_dim, d)
    tgt = loss_target.reshape(t_dim, d)
    memf = mem.reshape(b_dim * n_mem, d)
    gfin = g_final.reshape(1, d)
    pos = positions.reshape(t_dim, 1).astype(F32)

    lane = jnp.arange(LANES) % HEAD_DIM
    half = ROPE_DIM // 2
    inv_freq = ROPE_THETA ** (-jnp.arange(half, dtype=F32) / half)
    inv_lane = jnp.where(lane < ROPE_DIM, inv_freq[lane % half], 0.0).reshape(1, -1).astype(F32)
    sel_a = (lane < half).astype(F32).reshape(1, -1)
    sel_b = ((lane >= half) & (lane < ROPE_DIM)).astype(F32).reshape(1, -1)

    def rows3(t):
        return t.reshape(b_dim, s_dim, t.shape[-1])

    def rows2(t):
        return t.reshape(t_dim, t.shape[-1])

    if early_shards:
        n1, gathered = _rms_fwd(xb, g_mix, name="rms_mix", rider=("gather", early_shards))
        wf.update(zip(EARLY, gathered))
    else:
        n1 = _rms_fwd(xb, g_mix, name="rms_mix")
    proj = _mm_cs(n1, wf["w_in"], name="mm_in")
    proj3 = rows3(proj)
    cs, sn = _rope_table(pos, inv_lane, sel_a, sel_b)
    cs3, sn3 = rows3(cs), rows3(sn)
    (oa16, oa32, lse_a), _ = _dil_fwd(proj3, cs3, sn3, sel_a, sel_b)
    ob16, gathered = _sb_fwd(proj3, ("gather", late_shards) if late_shards else None)
    wf.update(zip(LATE, gathered))
    w_out, w_q, w_kv = _natural(wf["w_out"]), _natural(wf["w_q_mem"]), _natural(wf["w_kv_mem"])
    oa, ob = rows2(oa16), rows2(ob16)
    ua = _mm_sm(oa, wf["w_up_a"], name="mm_up_a", out_dtype=BF16)
    ub = _mm_sm(ob, wf["w_up_b"], name="mm_up_b", out_dtype=BF16)
    mixed, h1, hn = _gate_out_norm(proj, ua, ub, w_out, xb, g_mem_q)

    memn = _rms_fwd(memf, g_mem_kv, name="rms_mem_kv")
    qm = _mm(hn, w_q, name="mm_q_mem", out_dtype=BF16)
    kvm = _mm(memn, w_kv, name="mm_kv_mem", out_dtype=BF16)
    qm3, kvm3 = rows3(qm), kvm.reshape(b_dim, n_mem, 2 * MEM_WIDTH)
    om = rows2(_mem_fwd(qm3, kvm3))
    h2 = _mm_sm(om, wf["w_o_mem"], name="mm_o_mem", add=h1)

    n3 = _rms_fwd(h2, g_ffn, name="rms_ffn")
    gate3, up3, act3 = _ffn_up_swiglu(n3, wf["w_ffn_gate"], wf["w_ffn_up"])
    loss_row, dh3, dg_final = _down_final(act3, wf["w_ffn_down"], h2, gfin, tgt)

    grads = {}
    grads["w_ffn_down"] = _mm_ffn_down_dw(act3, dh3, name="mm_down_dw")
    dgate3, dup3 = _ffn_down_dx_swiglu(dh3, wf["w_ffn_down"], gate3, up3)
    grads["w_ffn_gate"] = _mm_ffn_down_dw(dgate3, n3, name="mm_gate_dw")
    grads["w_ffn_up"] = _mm_ffn_down_dw(dup3, n3, name="mm_up_dw")
    dn3 = _ffn_up_dx(dgate3, dup3, wf["w_ffn_gate"], wf["w_ffn_up"])
    dh2, dg_ffn = _rms_bwd(h2, g_ffn, dn3, dh3, name="rms_ffn_bwd")

    dom = _mm_sm_dx(dh2, wf["w_o_mem"], name="mm_o_mem_dx", out_dtype=BF16)
    grads["w_o_mem"] = _mm_sm_dw(om, dh2, name="mm_o_mem_dw")
    dqm, dkm, dvm = _mem_bwd(qm3, kvm3, rows3(dom))
    dqm = rows2(dqm)
    dkvm = jnp.concatenate([dkm, dvm], axis=-1).reshape(b_dim * n_mem, 2 * MEM_WIDTH).astype(BF16)
    grads["w_q_mem"] = _shard_major(_mm(hn, dqm, name="mm_q_mem_dw", ta=True), 0)
    dhn = _mm(dqm, w_q, name="mm_q_mem_dx", tb=True, out_dtype=BF16)
    grads["w_kv_mem"] = _shard_major(_mm(memn, dkvm, name="mm_kv_mem_dw", ta=True), 0)
    dmemn = _mm(dkvm, w_kv, name="mm_kv_mem_dx", tb=True)
    _, dg_mem_kv = _rms_bwd(memf, g_mem_kv, dmemn, None, name="rms_mem_kv_bwd")
    dh1, dg_mem_q = _rms_bwd(h1, g_mem_q, dhn, dh2, name="rms_mem_q_bwd")

    grads["w_out"] = _shard_major(_mm(mixed, dh1, name="mm_out_dw", ta=True), 0)
    dua, dub, dgates = _out_dx_gate_bwd(dh1, w_out, proj, ua, ub)
    doa = _mm_sm_dx(dua, wf["w_up_a"], name="mm_up_a_dx")
    grads["w_up_a"] = _mm_sm_dw(oa, dua, name="mm_up_a_dw")
    dob = _mm_sm_dx(dub, wf["w_up_b"], name="mm_up_b_dx", out_dtype=BF16)
    grads["w_up_b"] = _mm_sm_dw(ob, dub, name="mm_up_b_dw")

    att = {}

    def dil_with_pairs(glist):
        att["a"], theirs = _dil_bwd(proj3, cs3, sn3, sel_a, sel_b, rows3(doa), oa32, lse_a,
                                    ("pair", glist) if glist else None)
        return theirs

    def sb_with_chips(pairs):
        att["b"], recv = _sb_bwd(proj3, rows3(dob), ("chip", pairs) if pairs else None)
        return recv

    if place is None:
        dil_with_pairs(())
        sb_with_chips(())
    else:
        mine_late = _reduce_halves([grads[n] for n in LATE], LATE, *place, dil_with_pairs, sb_with_chips)
    dproj = jnp.concatenate([rows2(t) for t in att["a"] + att["b"]] + [dgates], axis=1)
    grads["w_in"] = _mm_cs_dw(n1, dproj, name="mm_in_dw")
    if place is None:
        dn1 = _mm_cs_dx(dproj, wf["w_in"], name="mm_in_dx", out_dtype=BF16)
        dx, dg_mix = _rms_bwd(xb, g_mix, dn1, dh1, name="rms_mix_bwd")
    else:
        tail = {}

        def pair_alone(glist):
            return _exchange_alone(("pair", glist), name="grad_pair_exchange")

        def dx_with_chips(pairs):
            tail["dn1"], recv = _mm_cs_dx(dproj, wf["w_in"], name="mm_in_dx", out_dtype=BF16, rider=("chip", pairs))
            return recv

        mine_early = _reduce_halves([grads[n] for n in EARLY], EARLY, *place, pair_alone, dx_with_chips)
        dx, dg_mix = _rms_bwd(xb, g_mix, tail["dn1"], dh1, name="rms_mix_bwd")
    grad_x = dx.reshape(b_dim, s_dim, d)
    gains = (dg_mix, dg_mem_q, dg_mem_kv, dg_ffn, dg_final)
    if place is None:
        return loss_row, grad_x, grads, gains
    return loss_row, grad_x, mine_early + mine_late, gains


def _reduce_and_update(given, shards, loss_row, grad_x, mine, gain_grads, c_idx):
    d = D_MODEL
    dg_mix, dg_mem_q, dg_mem_kv, dg_ffn, dg_final = gain_grads
    small = jnp.concatenate([dg_mix, dg_mem_q, dg_mem_kv, dg_ffn, dg_final,
                             jnp.pad(loss_row, ((0, 0), (0, FLAT_COLS - LANES))), jnp.zeros((2, FLAT_COLS), F32)], axis=0)
    small_all = _gather_small(small)
    others = _swap_halves(mine)
    small_sum = _sum8(small_all)
    loss = small_sum[5, 0]

    out_g, out_d, out_m, out_v = {}, {}, {}, {}
    for n, mine_n, other_n in zip(NAMES, mine, others):
        res = _adamw_halves(shards[n], mine_n, other_n, _held(n, given["m_" + n][0]), _held(n, given["v_" + n][0]),
                            c_idx, name="adamw_" + n)
        out_g[n], out_d[n], out_m[n], out_v[n] = [_held(n, r)[None] for r in res]
    gain_w = jnp.concatenate([given[n].reshape(1, d) for n in GAINS], axis=0)
    gain_m = jnp.concatenate([given["m_" + n].reshape(1, d) for n in GAINS], axis=0)
    gain_v = jnp.concatenate([given["v_" + n].reshape(1, d) for n in GAINS], axis=0)
    gain_g = small_sum[:len(GAINS)]
    gd, gm, gv = _adamw(gain_w, gain_g, gain_m, gain_v, name="adamw_gains")
    for i, n in enumerate(GAINS):
        shape = given[n].shape
        out_g[n], out_d[n] = gain_g[i].reshape(shape), gd[i].reshape(shape)
        out_m[n], out_v[n] = gm[i].reshape(shape), gv[i].reshape(shape)

    order = ["g_mix", "w_in", "w_up_a", "w_up_b", "w_out", "g_mem_q", "g_mem_kv", "w_q_mem", "w_kv_mem", "w_o_mem",
             "g_ffn", "w_ffn_gate", "w_ffn_up", "w_ffn_down", "g_final"]
    return (loss, grad_x, *[out_g[n] for n in order], *[out_d[n] for n in order],
            *[out_m[n] for n in order], *[out_v[n] for n in order])
```

```python
import jax
import jax.numpy as jnp
from jax import lax
from jax.experimental import pallas as pl
from jax.experimental.pallas import tpu as pltpu

F32 = jnp.float32
BF16 = jnp.bfloat16
MESH = pl.DeviceIdType.MESH

D_MODEL = 1024
HEAD_DIM = 64
N_HEADS = 8
ATT_WIDTH = N_HEADS * HEAD_DIM
DIL_PATTERNS = ((128, 1), (512, 4), (2048, 16))
BLOCK = 128
SB_ROWS = 1024
SB_STEP_FWD, SB_STEP_BWD = 8, 4
ROPE_THETA = 500000.0
ROPE_DIM = HEAD_DIM // 4
N_HEADS_MEM = 4
MEM_HEAD_DIM = 128
MEM_WIDTH = N_HEADS_MEM * MEM_HEAD_DIM
D_FF = 2816
IN_COLS = 6 * ATT_WIDTH + 2 * D_MODEL
RMS_EPS = 1e-6
ADAM_LR = 0.001
ADAM_B1 = 0.9
ADAM_B2 = 0.999
ADAM_EPS = 1e-08
ADAM_WD = 0.01
ADAM_STEP = 10

N_CHIPS = 4
LANES = 128
FLAT_COLS = 1024
VMEM_LIMIT = 56 * 1024 * 1024

PAIRS = ATT_WIDTH // LANES
COL_QA, COL_KA, COL_VA, COL_QB, COL_KB, COL_VB = (i * PAIRS for i in range(6))

MM_CAP = 1408
TOK_CAP = 2048
NN = (((1,), (0,)), ((), ()))
NT = (((1,), (1,)), ((), ()))
TN = (((0,), (0,)), ((), ()))
BNN = (((2,), (1,)), ((0,), (0,)))
BNT = (((2,), (2,)), ((0,), (0,)))
BTN = (((1,), (1,)), ((0,), (0,)))
DIL_BATCH = 16
DIL_CHUNK = 512


def _tile(dim, cap, unit=LANES):
    if dim <= cap:
        return dim
    best = None
    for t in range(unit, cap + 1, unit):
        if dim % t == 0:
            best = t
    assert best is not None, (dim, cap)
    return best


def _row_cap(cols):
    return max(256, (1 << 18) // cols)


def _params(sem):
    return pltpu.CompilerParams(dimension_semantics=sem, vmem_limit_bytes=VMEM_LIMIT)


def _mm(a, b, *, name, ta=False, tb=False, add=None, out_dtype=F32,
        tm_cap=MM_CAP, tn_cap=MM_CAP, tk_cap=MM_CAP):
    if ta:
        k_dim, m_dim = a.shape
    else:
        m_dim, k_dim = a.shape
    if tb:
        n_dim, kb = b.shape
    else:
        kb, n_dim = b.shape
    assert kb == k_dim, (a.shape, b.shape, ta, tb)
    tm, tn, tk = _tile(m_dim, tm_cap), _tile(n_dim, tn_cap), _tile(k_dim, tk_cap)
    nk = k_dim // tk
    dims = (((0 if ta else 1,), (1 if tb else 0,)), ((), ()))
    has_add = add is not None

    def body(*refs):
        if has_add:
            a_ref, b_ref, add_ref, o_ref = refs[:4]
        else:
            a_ref, b_ref, o_ref = refs[:3]
        part = lax.dot_general(a_ref[...].astype(BF16), b_ref[...].astype(BF16), dims, preferred_element_type=F32)

        def finish(r):
            if has_add:
                r = add_ref[...] + r
            o_ref[...] = r.astype(out_dtype)

        if nk == 1:
            finish(part)
            return
        acc_ref = refs[-1]
        k = pl.program_id(2)

        @pl.when(k == 0)
        def _():
            acc_ref[...] = part

        @pl.when(k > 0)
        def _():
            acc_ref[...] += part

        @pl.when(k == nk - 1)
        def _():
            finish(acc_ref[...])

    a_spec = pl.BlockSpec((tk, tm), lambda i, j, k: (k, i)) if ta else pl.BlockSpec((tm, tk), lambda i, j, k: (i, k))
    b_spec = pl.BlockSpec((tn, tk), lambda i, j, k: (j, k)) if tb else pl.BlockSpec((tk, tn), lambda i, j, k: (k, j))
    o_spec = pl.BlockSpec((tm, tn), lambda i, j, k: (i, j))
    in_specs = [a_spec, b_spec] + ([o_spec] if has_add else [])
    args = (a, b) + ((add,) if has_add else ())
    return pl.pallas_call(
        body, name=name, grid=(m_dim // tm, n_dim // tn, nk),
        in_specs=in_specs, out_specs=o_spec,
        out_shape=jax.ShapeDtypeStruct((m_dim, n_dim), out_dtype),
        scratch_shapes=[pltpu.VMEM((tm, tn), F32)] if nk > 1 else [],
        compiler_params=_params(("parallel", "parallel", "arbitrary")),
    )(*args)


def _mm_core(name, a, b, a_spec, b_spec, o_spec, out_shape, grid, dims, *, add=None, out_dtype=F32, rider=None):
    nk = grid[2]
    has_add = add is not None
    n_in = 3 if has_add else 2
    acc_shape = tuple(d for d in o_spec.block_shape if d is not None)
    extra, extra_shapes, extra_sems = _rider_parts(rider)
    n_w = len(extra)

    def body(*refs):
        a_ref, b_ref = refs[:2]
        o_ref = refs[n_in + n_w]
        step = (pl.program_id(0) * grid[1] + pl.program_id(1)) * nk + pl.program_id(2)
        begin, end = _rider_hooks(rider, refs[n_in:n_in + n_w], refs[n_in + n_w + 1:n_in + 2 * n_w + 1], refs[-2:],
                                  step, grid[0] * grid[1] * nk)
        begin()
        part = lax.dot_general(a_ref[...].astype(BF16), b_ref[...].astype(BF16), dims, preferred_element_type=F32)

        def finish(r):
            if has_add:
                r = refs[2][...] + r
            o_ref[...] = r.astype(out_dtype)

        if nk == 1:
            finish(part)
        else:
            acc_ref = refs[n_in + 2 * n_w + 1]
            k = pl.program_id(2)

            @pl.when(k == 0)
            def _():
                acc_ref[...] = part

            @pl.when(k > 0)
            def _():
                acc_ref[...] += part

            @pl.when(k == nk - 1)
            def _():
                finish(acc_ref[...])
        end()

    in_specs = [a_spec, b_spec] + ([o_spec] if has_add else []) + [ANY] * n_w
    args = (a, b) + ((add,) if has_add else ()) + extra
    res = pl.pallas_call(
        body, name=name, grid=grid, in_specs=in_specs, out_specs=[o_spec] + [ANY] * n_w,
        out_shape=[jax.ShapeDtypeStruct(out_shape, out_dtype)] + extra_shapes,
        scratch_shapes=([pltpu.VMEM(acc_shape, F32)] if nk > 1 else []) + extra_sems,
        compiler_params=_params(("arbitrary",) * 3 if n_w else ("parallel", "parallel", "arbitrary")),
    )(*args)
    return (res[0], res[1:]) if n_w else res[0]


def _mm_cs(a, w3, *, name):
    m_dim, k_dim = a.shape
    _, _, n4 = w3.shape
    tm, tn, tk = _tile(m_dim, TOK_CAP), _tile(n4, MM_CAP), _tile(k_dim, MM_CAP)
    npb = n4 // tn
    return _mm_core(name, a, w3,
                    pl.BlockSpec((tm, tk), lambda i, j, k: (i, k)),
                    pl.BlockSpec((None, tk, tn), lambda i, j, k: (j // npb, k, j % npb)),
                    pl.BlockSpec((tm, tn), lambda i, j, k: (i, j)),
                    (m_dim, N_CHIPS * n4), (m_dim // tm, N_CHIPS * npb, k_dim // tk), NN)


def _mm_cs_dx(dy, w3, *, name, out_dtype=F32, rider=None):
    m_dim, _ = dy.shape
    _, k_dim, n4 = w3.shape
    tm, tkw, tn = _tile(m_dim, TOK_CAP), _tile(k_dim, MM_CAP), _tile(n4, MM_CAP)
    npb = n4 // tn
    return _mm_core(name, dy, w3,
                    pl.BlockSpec((tm, tn), lambda i, j, k: (i, k)),
                    pl.BlockSpec((None, tkw, tn), lambda i, j, k: (k // npb, j, k % npb)),
                    pl.BlockSpec((tm, tkw), lambda i, j, k: (i, j)),
                    (m_dim, k_dim), (m_dim // tm, k_dim // tkw, N_CHIPS * npb), NT, out_dtype=out_dtype, rider=rider)


def _mm_cs_dw(a, dy, *, name):
    m_dim, k_dim = a.shape
    n4 = dy.shape[1] // N_CHIPS
    tmk, tn, tk = _tile(k_dim, MM_CAP), _tile(n4, MM_CAP), _tile(m_dim, TOK_CAP)
    npb = n4 // tn
    return _mm_core(name, a, dy,
                    pl.BlockSpec((tk, tmk), lambda i, j, k: (k, i)),
                    pl.BlockSpec((tk, tn), lambda i, j, k: (k, j)),
                    pl.BlockSpec((None, tmk, tn), lambda i, j, k: (j // npb, i, j % npb)),
                    (N_CHIPS, k_dim, n4), (k_dim // tmk, N_CHIPS * npb, m_dim // tk), TN)


def _mm_sm(a, w3, *, name, add=None, out_dtype=F32, tt=1024):
    t_dim, k_dim = a.shape
    n_s, _, n4 = w3.shape
    has_add = add is not None

    def body(*refs):
        a_ref, w_ref, o_ref = refs[0], refs[1], refs[-1]
        av = a_ref[...].astype(BF16)
        for s in range(n_s):
            cols = slice(s * n4, (s + 1) * n4)
            r = jnp.dot(av, w_ref[s], preferred_element_type=F32)
            if has_add:
                r = refs[2][:, cols] + r
            o_ref[:, cols] = r.astype(out_dtype)

    row = pl.BlockSpec((tt, n_s * n4), lambda i: (i, 0))
    return pl.pallas_call(
        body, name=name, grid=(t_dim // tt,),
        in_specs=[pl.BlockSpec((tt, k_dim), lambda i: (i, 0)), pl.BlockSpec(w3.shape, lambda i: (0, 0, 0))]
        + ([row] if has_add else []),
        out_specs=row, out_shape=jax.ShapeDtypeStruct((t_dim, n_s * n4), out_dtype),
        compiler_params=_params(("parallel",)),
    )(*((a, w3) + ((add,) if has_add else ())))


def _mm_sm_dx(dy, w3, *, name, out_dtype=F32, tt=1024):
    t_dim, _ = dy.shape
    n_s, k_dim, n4 = w3.shape

    def body(dy_ref, w_ref, o_ref):
        dyv = dy_ref[...].astype(BF16)
        acc = lax.dot_general(dyv[:, :n4], w_ref[0], NT, preferred_element_type=F32)
        for s in range(1, n_s):
            acc = acc + lax.dot_general(dyv[:, s * n4:(s + 1) * n4], w_ref[s], NT, preferred_element_type=F32)
        o_ref[...] = acc.astype(out_dtype)

    return pl.pallas_call(
        body, name=name, grid=(t_dim // tt,),
        in_specs=[pl.BlockSpec((tt, n_s * n4), lambda i: (i, 0)), pl.BlockSpec(w3.shape, lambda i: (0, 0, 0))],
        out_specs=pl.BlockSpec((tt, k_dim), lambda i: (i, 0)),
        out_shape=jax.ShapeDtypeStruct((t_dim, k_dim), out_dtype),
        compiler_params=_params(("parallel",)),
    )(dy, w3)


def _mm_sm_dw(a, dy, *, name, tk=1024):
    t_dim, k_dim = a.shape
    n4 = dy.shape[1] // N_CHIPS

    def body(a_ref, dy_ref, o_ref):
        av = a_ref[...].astype(BF16)
        dyv = dy_ref[...].astype(BF16)

        @pl.when(pl.program_id(0) == 0)
        def _():
            o_ref[...] = jnp.zeros_like(o_ref)

        for s in range(N_CHIPS):
            o_ref[s] += lax.dot_general(av, dyv[:, s * n4:(s + 1) * n4], TN, preferred_element_type=F32)

    return pl.pallas_call(
        body, name=name, grid=(t_dim // tk,),
        in_specs=[pl.BlockSpec((tk, k_dim), lambda i: (i, 0)), pl.BlockSpec((tk, N_CHIPS * n4), lambda i: (i, 0))],
        out_specs=pl.BlockSpec((N_CHIPS, k_dim, n4), lambda i: (0, 0, 0)),
        out_shape=jax.ShapeDtypeStruct((N_CHIPS, k_dim, n4), F32),
        compiler_params=_params(("arbitrary",)),
    )(a, dy)


def _ffn_up_dx(dg3, du3, wg3, wu3, *, tt=512):
    n_s, t_dim, f4 = dg3.shape
    d = wg3.shape[2]

    def body(dg_ref, du_ref, wg_ref, wu_ref, o_ref):
        acc = None
        for s in range(n_s):
            part = (jnp.dot(dg_ref[s], wg_ref[s], preferred_element_type=F32)
                    + jnp.dot(du_ref[s], wu_ref[s], preferred_element_type=F32))
            acc = part if acc is None else acc + part
        o_ref[...] = acc.astype(BF16)

    a_spec = pl.BlockSpec((n_s, tt, f4), lambda i: (0, i, 0))
    w_spec = pl.BlockSpec(wg3.shape, lambda i: (0, 0, 0))
    return pl.pallas_call(
        body, name="ffn_up_dx", grid=(t_dim // tt,),
        in_specs=[a_spec, a_spec, w_spec, w_spec], out_specs=pl.BlockSpec((tt, d), lambda i: (i, 0)),
        out_shape=jax.ShapeDtypeStruct((t_dim, d), BF16),
        compiler_params=_params(("parallel",)),
    )(dg3, du3, wg3, wu3)


def _mm_ffn_down_dw(act3, dh, *, name, tk=1024):
    n_s, t_dim, f4 = act3.shape
    d = dh.shape[1]

    def body(a_ref, b_ref, o_ref):
        bv = b_ref[...].astype(BF16)

        @pl.when(pl.program_id(0) == 0)
        def _():
            o_ref[...] = jnp.zeros_like(o_ref)

        for s in range(n_s):
            o_ref[s] += lax.dot_general(a_ref[s], bv, TN, preferred_element_type=F32)

    return pl.pallas_call(
        body, name=name, grid=(t_dim // tk,),
        in_specs=[pl.BlockSpec((n_s, tk, f4), lambda i: (0, i, 0)), pl.BlockSpec((tk, d), lambda i: (i, 0))],
        out_specs=pl.BlockSpec((n_s, f4, d), lambda i: (0, 0, 0)),
        out_shape=jax.ShapeDtypeStruct((n_s, f4, d), F32),
        compiler_params=_params(("arbitrary",)),
    )(act3, dh)


def _rms_fwd(x, g, *, name, tt=512, rider=None):
    t_dim, d = x.shape
    tt = _tile(t_dim, tt, 8)
    extra, extra_shapes, extra_sems = _rider_parts(rider)
    n_w = len(extra)

    def body(*refs):
        x_ref, g_ref, o_ref = refs[0], refs[1], refs[2 + n_w]
        begin, end = _rider_hooks(rider, refs[2:2 + n_w], refs[3 + n_w:3 + 2 * n_w], refs[-2:], pl.program_id(0),
                                  t_dim // tt)
        begin()
        xv = x_ref[...]
        r = lax.rsqrt(jnp.mean(xv * xv, axis=-1, keepdims=True) + RMS_EPS)
        o_ref[...] = ((xv * r) * g_ref[...]).astype(o_ref.dtype)
        end()

    res = pl.pallas_call(
        body, name=name, grid=(t_dim // tt,),
        in_specs=[pl.BlockSpec((tt, d), lambda i: (i, 0)), pl.BlockSpec((1, d), lambda i: (0, 0))] + [ANY] * n_w,
        out_specs=[pl.BlockSpec((tt, d), lambda i: (i, 0))] + [ANY] * n_w,
        out_shape=[jax.ShapeDtypeStruct((t_dim, d), BF16)] + extra_shapes,
        scratch_shapes=extra_sems,
        compiler_params=_params(("arbitrary",) if n_w else ("parallel",)),
    )(x, g, *extra)
    return (res[0], res[1:]) if n_w else res[0]


def _rms_bwd(x, g, dy, add, *, name, tt=512):
    t_dim, d = x.shape
    tt = _tile(t_dim, tt, 8)
    has_add = add is not None

    def body(*refs):
        x_ref, g_ref, dy_ref = refs[:3]
        add_ref = refs[3] if has_add else None
        dx_ref, dg_ref = refs[-2:]
        xv = x_ref[...]
        dyv = dy_ref[...].astype(F32)
        r = lax.rsqrt(jnp.mean(xv * xv, axis=-1, keepdims=True) + RMS_EPS)
        xh = xv * r
        u = dyv * g_ref[...]
        dx = r * (u - xh * jnp.mean(u * xh, axis=-1, keepdims=True))
        if has_add:
            dx = add_ref[...] + dx
        dx_ref[...] = dx

        @pl.when(pl.program_id(0) == 0)
        def _():
            dg_ref[...] = jnp.zeros_like(dg_ref)

        dg_ref[...] += jnp.sum(dyv * xh, axis=0, keepdims=True)

    row = pl.BlockSpec((tt, d), lambda i: (i, 0))
    vec = pl.BlockSpec((1, d), lambda i: (0, 0))
    in_specs = [row, vec, row] + ([row] if has_add else [])
    args = (x, g, dy) + ((add,) if has_add else ())
    return pl.pallas_call(
        body, name=name, grid=(t_dim // tt,),
        in_specs=in_specs, out_specs=[row, vec],
        out_shape=[jax.ShapeDtypeStruct((t_dim, d), F32), jax.ShapeDtypeStruct((1, d), F32)],
        compiler_params=_params(("arbitrary",)),
    )(*args)


def _down_final(act3, wd3, h, g, target, *, tt=512):
    n_s, t_dim, f4 = act3.shape
    d = h.shape[1]
    n_steps = t_dim // tt

    def body(a_ref, w_ref, h_ref, g_ref, t_ref, loss_ref, dh_ref, dg_ref, sq_ref):
        i = pl.program_id(0)
        xv = h_ref[...]
        for s in range(n_s):
            xv = xv + jnp.dot(a_ref[s], w_ref[s], preferred_element_type=F32)
        gv = g_ref[...]
        r = lax.rsqrt(jnp.mean(xv * xv, axis=-1, keepdims=True) + RMS_EPS)
        xh = xv * r
        err = xh * gv - t_ref[...]
        dyv = err * (1.0 / d)
        u = dyv * gv
        dh_ref[...] = r * (u - xh * jnp.mean(u * xh, axis=-1, keepdims=True))

        @pl.when(i == 0)
        def _():
            dg_ref[...] = jnp.zeros_like(dg_ref)
            sq_ref[...] = jnp.zeros_like(sq_ref)

        dg_ref[...] += jnp.sum(dyv * xh, axis=0, keepdims=True)
        sq_ref[...] += jnp.sum(err * err, axis=0, keepdims=True)

        @pl.when(i == n_steps - 1)
        def _():
            total = jnp.sum(sq_ref[...], axis=-1, keepdims=True) * (0.5 / d)
            loss_ref[...] = jnp.broadcast_to(total, loss_ref.shape)

    row = pl.BlockSpec((tt, d), lambda i: (i, 0))
    vec = pl.BlockSpec((1, d), lambda i: (0, 0))
    return pl.pallas_call(
        body, name="down_final_loss", grid=(n_steps,),
        in_specs=[pl.BlockSpec((n_s, tt, f4), lambda i: (0, i, 0)), pl.BlockSpec(wd3.shape, lambda i: (0, 0, 0)),
                  row, vec, row],
        out_specs=[pl.BlockSpec((1, LANES), lambda i: (0, 0)), row, vec],
        out_shape=[jax.ShapeDtypeStruct((1, LANES), F32), jax.ShapeDtypeStruct((t_dim, d), F32),
                   jax.ShapeDtypeStruct((1, d), F32)],
        scratch_shapes=[pltpu.VMEM((1, d), F32)],
        compiler_params=_params(("arbitrary",)),
    )(act3, wd3, h, g, target)


def _rope_table(pos, inv_lane, sel_a, sel_b, *, tt=512):
    t_dim = pos.shape[0]

    def body(p_ref, f_ref, a_ref, b_ref, c_ref, s_ref):
        ang = p_ref[...] * f_ref[...]
        on = (a_ref[...] + b_ref[...]) > 0.0
        c_ref[...] = jnp.where(on, jnp.cos(ang), 1.0)
        s_ref[...] = jnp.where(on, jnp.sin(ang), 0.0)

    vec = pl.BlockSpec((1, LANES), lambda i: (0, 0))
    row = pl.BlockSpec((tt, LANES), lambda i: (i, 0))
    shp = jax.ShapeDtypeStruct((t_dim, LANES), F32)
    return pl.pallas_call(
        body, name="rope_table", grid=(t_dim // tt,),
        in_specs=[pl.BlockSpec((tt, 1), lambda i: (i, 0)), vec, vec, vec],
        out_specs=[row, row], out_shape=[shp, shp],
        compiler_params=_params(("parallel",)),
    )(pos, inv_lane, sel_a, sel_b)


def _rotate(xv, cs, sn, sa, sb):
    half = ROPE_DIM // 2
    up = pltpu.roll(xv, LANES - half, 1)
    dn = pltpu.roll(xv, half, 1)
    return xv * cs + (dn * sb - up * sa) * sn


def _head_masks():
    h1 = lax.broadcasted_iota(jnp.int32, (1, LANES), 1) < HEAD_DIM
    return h1, jnp.logical_not(h1)


def _split_heads(xv, h1, h2):
    return jnp.where(h1, xv, 0.0).astype(BF16), jnp.where(h2, xv, 0.0).astype(BF16)


def _tri_masks():
    r = lax.broadcasted_iota(jnp.int32, (BLOCK, BLOCK), 0)
    c = lax.broadcasted_iota(jnp.int32, (BLOCK, BLOCK), 1)
    return c <= r, r <= c


def _stream_rows(start, dil):
    if dil == 1:
        return pl.ds(pl.multiple_of(start, BLOCK), BLOCK)
    return pl.ds(start, BLOCK, stride=dil)


def _dil_tile(idx, dil, nb):
    r = idx // nb
    n = idx % nb
    return (_stream_rows(r + dil * BLOCK * n, dil), _stream_rows(r + dil * BLOCK * jnp.maximum(n - 1, 0), dil),
            n > 0)


def _dil_specs(b_dim, s_dim):
    def col(c0):
        return pl.BlockSpec((None, s_dim, LANES),lambda b, h: (b, 0, c0 + h))
    tab = pl.BlockSpec((None, s_dim, LANES),lambda b, h: (b, 0, 0))
    vec = pl.BlockSpec((1, LANES), lambda b, h: (0, 0))
    return col, tab, vec


def _dil_fwd(proj3, cs3, sn3, sel_a, sel_b, rider=None):
    b_dim, s_dim, _ = proj3.shape
    scale = HEAD_DIM ** -0.5
    n_pat = len(DIL_PATTERNS)
    extra, extra_shapes, extra_sems = _rider_parts(rider)
    n_w = len(extra)
    n_steps = b_dim * PAIRS

    def body(*refs):
        q_ref, k_ref, v_ref, cs_ref, sn_ref, sa_ref, sb_ref = refs[:7]
        o16_ref, o32_ref, l_ref = refs[7 + n_w:10 + n_w]
        qr, kr = refs[10 + 2 * n_w:12 + 2 * n_w]
        per_pattern = refs[12 + 2 * n_w:12 + 2 * n_w + 2 * n_pat]
        og, lg = per_pattern[:n_pat], per_pattern[n_pat:]
        step = pl.program_id(0) * PAIRS + pl.program_id(1)
        begin, end = _rider_hooks(rider, refs[7:7 + n_w], refs[10 + n_w:10 + 2 * n_w], refs[-2:], step, n_steps)
        begin()
        h1, h2 = _head_masks()
        cur_ok, prev_ok = _tri_masks()
        sa, sb = sa_ref[...], sb_ref[...]

        def prep(j, _):
            rows = pl.ds(pl.multiple_of(j * DIL_CHUNK, DIL_CHUNK), DIL_CHUNK)
            cs, sn = cs_ref[rows, :], sn_ref[rows, :]
            qr[rows, :] = _rotate(q_ref[rows, :], cs, sn, sa, sb) * scale
            kr[rows, :] = _rotate(k_ref[rows, :], cs, sn, sa, sb)
            return 0

        lax.fori_loop(0, s_dim // DIL_CHUNK, prep, 0)

        for g, (_, dil) in enumerate(DIL_PATTERNS):
            nb = s_dim // dil // BLOCK

            def some(bi, _, g=g, dil=dil, nb=nb):
                tiles = [_dil_tile(bi * DIL_BATCH + t, dil, nb) for t in range(DIL_BATCH)]
                rows = [t[0] for t in tiles]
                q1, q2 = _split_heads(jnp.stack([qr[rw, :] for rw in rows]), h1, h2)
                kc = jnp.stack([kr[rw, :] for rw in rows]).astype(BF16)
                vc1, vc2 = _split_heads(jnp.stack([v_ref[rw, :] for rw in rows]), h1, h2)
                if nb > 1:
                    kp = jnp.stack([kr[t[1], :] for t in tiles]).astype(BF16)
                    vp1, vp2 = _split_heads(jnp.stack([v_ref[t[1], :] for t in tiles]), h1, h2)
                    p_ok = jnp.stack([jnp.logical_and(prev_ok, t[2]) for t in tiles])

                def head(qh, vch, vph):
                    sc = jnp.where(cur_ok, lax.dot_general(qh, kc, BNT, preferred_element_type=F32), -jnp.inf)
                    m = jnp.max(sc, axis=-1, keepdims=True)
                    if nb > 1:
                        sp = jnp.where(p_ok, lax.dot_general(qh, kp, BNT, preferred_element_type=F32), -jnp.inf)
                        m = jnp.maximum(m, jnp.max(sp, axis=-1, keepdims=True))
                    pc = jnp.exp(sc - m)
                    den = jnp.sum(pc, axis=-1, keepdims=True)
                    acc = lax.dot_general(pc.astype(BF16), vch, BNN, preferred_element_type=F32)
                    if nb > 1:
                        pp = jnp.exp(sp - m)
                        den = den + jnp.sum(pp, axis=-1, keepdims=True)
                        acc = acc + lax.dot_general(pp.astype(BF16), vph, BNN, preferred_element_type=F32)
                    return acc / den, m + jnp.log(den)

                o1, l1 = head(q1, vc1, vp1 if nb > 1 else None)
                o2, l2 = head(q2, vc2, vp2 if nb > 1 else None)
                o, l = o1 + o2, jnp.where(h1, l1, l2)
                for t, rw in enumerate(rows):
                    og[g][rw, :] = o[t]
                    lg[g][rw, :] = l[t]
                return 0

            lax.fori_loop(0, dil * nb // DIL_BATCH, some, 0)

        def comb(j, _):
            rows = pl.ds(pl.multiple_of(j * DIL_CHUNK, DIL_CHUNK), DIL_CHUNK)
            ls = [lg[g][rows, :] for g in range(n_pat)]
            m = jnp.maximum(jnp.maximum(ls[0], ls[1]), ls[2])
            es = [jnp.exp(l - m) for l in ls]
            den = es[0] + es[1] + es[2]
            o = (es[0] * og[0][rows, :] + es[1] * og[1][rows, :] + es[2] * og[2][rows, :]) / den
            o16_ref[rows, :] = o.astype(BF16)
            o32_ref[rows, :] = o
            l_ref[rows, :] = m + jnp.log(den)
            return 0

        lax.fori_loop(0, s_dim // DIL_CHUNK, comb, 0)
        end()

    col, tab, vec = _dil_specs(b_dim, s_dim)
    out = pl.BlockSpec((None, s_dim, LANES),lambda b, h: (b, 0, h))
    shp = (b_dim, s_dim, ATT_WIDTH)
    res = pl.pallas_call(
        body, name="dil_fwd", grid=(b_dim, PAIRS),
        in_specs=[col(COL_QA), col(COL_KA), col(COL_VA), tab, tab, vec, vec] + [ANY] * n_w,
        out_specs=[out, out, out] + [ANY] * n_w,
        out_shape=[jax.ShapeDtypeStruct(shp, BF16), jax.ShapeDtypeStruct(shp, F32), jax.ShapeDtypeStruct(shp, F32)]
        + extra_shapes,
        scratch_shapes=[pltpu.VMEM((s_dim, LANES), F32)] * (2 + 2 * n_pat) + extra_sems,
        compiler_params=_params(("arbitrary", "arbitrary")),
    )(proj3, proj3, proj3, cs3, sn3, sel_a, sel_b, *extra)
    return res[:3], res[3:]


def _dil_bwd(proj3, cs3, sn3, sel_a, sel_b, do3, o3, lse3, rider=None):
    b_dim, s_dim, _ = proj3.shape
    scale = HEAD_DIM ** -0.5
    extra, extra_shapes, extra_sems = _rider_parts(rider)
    n_w = len(extra)

    def body(*refs):
        q_ref, k_ref, v_ref, cs_ref, sn_ref, sa_ref, sb_ref, do_ref, o_ref, l_ref = refs[:10]
        dq_ref, dk_ref, dv_ref = refs[10 + n_w:13 + n_w]
        qr, kr, dqa, dka, dva = refs[13 + 2 * n_w:18 + 2 * n_w]
        step = pl.program_id(0) * PAIRS + pl.program_id(1)
        begin, end = _rider_hooks(rider, refs[10:10 + n_w], refs[13 + n_w:13 + 2 * n_w], refs[-2:], step,
                                  b_dim * PAIRS)
        begin()
        h1, h2 = _head_masks()
        cur_ok, prev_ok = _tri_masks()
        sa, sb = sa_ref[...], sb_ref[...]

        def prep(j, _):
            rows = pl.ds(pl.multiple_of(j * DIL_CHUNK, DIL_CHUNK), DIL_CHUNK)
            cs, sn = cs_ref[rows, :], sn_ref[rows, :]
            qr[rows, :] = _rotate(q_ref[rows, :], cs, sn, sa, sb) * scale
            kr[rows, :] = _rotate(k_ref[rows, :], cs, sn, sa, sb)
            zero = jnp.zeros((DIL_CHUNK, LANES), F32)
            dqa[rows, :] = zero
            dka[rows, :] = zero
            dva[rows, :] = zero
            return 0

        lax.fori_loop(0, s_dim // DIL_CHUNK, prep, 0)

        for _, dil in DIL_PATTERNS:
            nb = s_dim // dil // BLOCK

            def some(bi, _, dil=dil, nb=nb):
                tiles = [_dil_tile(bi * DIL_BATCH + t, dil, nb) for t in range(DIL_BATCH)]
                rows = [t[0] for t in tiles]
                q1, q2 = _split_heads(jnp.stack([qr[rw, :] for rw in rows]), h1, h2)
                dof = jnp.stack([do_ref[rw, :] for rw in rows])
                do1, do2 = _split_heads(dof, h1, h2)
                prod = dof * jnp.stack([o_ref[rw, :] for rw in rows])
                delta1 = jnp.sum(jnp.where(h1, prod, 0.0), axis=-1, keepdims=True)
                delta2 = jnp.sum(jnp.where(h2, prod, 0.0), axis=-1, keepdims=True)
                lt = jnp.stack([l_ref[rw, :] for rw in rows])
                lse1 = jnp.max(jnp.where(h1, lt, -jnp.inf), axis=-1, keepdims=True)
                lse2 = jnp.max(jnp.where(h2, lt, -jnp.inf), axis=-1, keepdims=True)

                def side(krows, ok):
                    kf = jnp.stack([kr[kw, :] for kw in krows])
                    k16 = kf.astype(BF16)
                    k1, k2 = _split_heads(kf, h1, h2)
                    v16 = jnp.stack([v_ref[kw, :] for kw in krows]).astype(BF16)

                    def head(qh, doh, lse, delta):
                        sc = lax.dot_general(qh, k16, BNT, preferred_element_type=F32)
                        p = jnp.where(ok, jnp.exp(sc - lse), 0.0)
                        dp = lax.dot_general(doh, v16, BNT, preferred_element_type=F32)
                        return p.astype(BF16), (p * (dp - delta)).astype(BF16)

                    p1, ds1 = head(q1, do1, lse1, delta1)
                    p2, ds2 = head(q2, do2, lse2, delta2)
                    dv = (lax.dot_general(p1, do1, BTN, preferred_element_type=F32)
                          + lax.dot_general(p2, do2, BTN, preferred_element_type=F32))
                    dk = (lax.dot_general(ds1, q1, BTN, preferred_element_type=F32)
                          + lax.dot_general(ds2, q2, BTN, preferred_element_type=F32))
                    for t, kw in enumerate(krows):
                        dva[kw, :] += dv[t]
                        dka[kw, :] += dk[t]
                    return (lax.dot_general(ds1, k1, BNN, preferred_element_type=F32)
                            + lax.dot_general(ds2, k2, BNN, preferred_element_type=F32))

                dq = side(rows, cur_ok)
                if nb > 1:
                    dq = dq + side([t[1] for t in tiles], jnp.stack([jnp.logical_and(prev_ok, t[2]) for t in tiles]))
                for t, rw in enumerate(rows):
                    dqa[rw, :] += dq[t] * scale
                return 0

            lax.fori_loop(0, dil * nb // DIL_BATCH, some, 0)

        def finish(j, _):
            rows = pl.ds(pl.multiple_of(j * DIL_CHUNK, DIL_CHUNK), DIL_CHUNK)
            cs, sn = cs_ref[rows, :], -sn_ref[rows, :]
            dq_ref[rows, :] = _rotate(dqa[rows, :], cs, sn, sa, sb).astype(BF16)
            dk_ref[rows, :] = _rotate(dka[rows, :], cs, sn, sa, sb).astype(BF16)
            dv_ref[rows, :] = dva[rows, :].astype(BF16)
            return 0

        lax.fori_loop(0, s_dim // DIL_CHUNK, finish, 0)
        end()

    col, tab, vec = _dil_specs(b_dim, s_dim)
    out = pl.BlockSpec((None, s_dim, LANES),lambda b, h: (b, 0, h))
    shp = jax.ShapeDtypeStruct((b_dim, s_dim, ATT_WIDTH), BF16)
    acc = pltpu.VMEM((s_dim, LANES), F32)
    res = pl.pallas_call(
        body, name="dil_bwd", grid=(b_dim, PAIRS),
        in_specs=[col(COL_QA), col(COL_KA), col(COL_VA), tab, tab, vec, vec, out, out, out] + [ANY] * n_w,
        out_specs=[out, out, out] + [ANY] * n_w, out_shape=[shp, shp, shp] + extra_shapes,
        scratch_shapes=[acc, acc, acc, acc, acc] + extra_sems,
        compiler_params=_params(("arbitrary", "arbitrary")),
    )(proj3, proj3, proj3, cs3, sn3, sel_a, sel_b, do3, o3, lse3, *extra)
    return res[:3], res[3:]


def _split_dot(x, tri):
    hi = x.astype(BF16)
    lo = (x - hi.astype(F32)).astype(BF16)
    return jnp.dot(hi, tri, preferred_element_type=F32) + jnp.dot(lo, tri, preferred_element_type=F32)


LOG2E = 1.4426950408889634
LN2 = 0.6931471805599453


def _log2_sigmoid(z):
    return jnp.minimum(z, 0.0) - jnp.log2(1.0 + jnp.exp2(-jnp.abs(z)))


def _sb_scores(qh, k16, valid):
    z = lax.dot_general(qh, k16, NT, preferred_element_type=F32)
    ls = _log2_sigmoid(z)
    l1m = ls - z
    return ls, (l1m if valid is None else jnp.where(valid, l1m, 0.0))


def _sb_consts():
    r = lax.broadcasted_iota(jnp.int32, (BLOCK, BLOCK), 0)
    c = lax.broadcasted_iota(jnp.int32, (BLOCK, BLOCK), 1)
    after = (r > c).astype(BF16)
    before = (r < c).astype(BF16)
    qrow = lax.broadcasted_iota(jnp.int32, (SB_ROWS, BLOCK), 0)
    kcol = lax.broadcasted_iota(jnp.int32, (SB_ROWS, BLOCK), 1)
    return after, before, qrow, kcol


def _below(whole, lo, delta):
    if lo == 0:
        return whole + delta
    return whole + jnp.concatenate([jnp.zeros((lo,) + delta.shape[1:], delta.dtype), delta], axis=0)


def _pairs_loop(n_blocks, step, carry, per_iter):
    def several(i, c):
        for j in range(per_iter):
            c = step(per_iter * i + j, c)
        return c

    return lax.fori_loop(0, n_blocks // per_iter, several, carry)


def _sb_fwd(proj3, rider=None):
    b_dim, s_dim, _ = proj3.shape
    scale = HEAD_DIM ** -0.5
    per = SB_ROWS // BLOCK
    extra, extra_shapes, extra_sems = _rider_parts(rider)
    n_w = len(extra)

    def body(*refs):
        q_ref, k_ref, v_ref = refs[:3]
        o_ref = refs[3 + n_w]
        step = pl.program_id(0) * PAIRS + pl.program_id(1)
        begin, end = _rider_hooks(rider, refs[3:3 + n_w], refs[4 + n_w:4 + 2 * n_w], refs[-2:], step, b_dim * PAIRS)
        begin()
        h1, h2 = _head_masks()
        after, _, qrow, kcol = _sb_consts()

        def qloop(qi, _):
            rows = pl.ds(pl.multiple_of(qi * SB_ROWS, SB_ROWS), SB_ROWS)
            q1, q2 = _split_heads(q_ref[rows, :] * (scale * LOG2E), h1, h2)
            first = qi * per

            def block(kb, carry, lo):
                acc, run1, run2 = carry
                krows = pl.ds(pl.multiple_of(kb * BLOCK, BLOCK), BLOCK)
                k16 = k_ref[krows, :].astype(BF16)
                v1, v2 = _split_heads(v_ref[krows, :], h1, h2)
                valid = None if lo is None else kcol[:SB_ROWS - lo] < qrow[:SB_ROWS - lo]
                lo = lo or 0

                def head(qh, vh, run):
                    ls, l1m = _sb_scores(qh[lo:], k16, valid)
                    a = jnp.exp2(ls + _split_dot(l1m, after) + run[lo:])
                    if valid is not None:
                        a = jnp.where(valid, a, 0.0)
                    return (jnp.dot(a.astype(BF16), vh, preferred_element_type=F32),
                            _below(run, lo, jnp.sum(l1m, axis=-1, keepdims=True)))

                o1, run1 = head(q1, v1, run1)
                o2, run2 = head(q2, v2, run2)
                return _below(acc, lo, o1 + o2), run1, run2

            zcol = jnp.zeros((SB_ROWS, 1), F32)
            carry = (jnp.zeros((SB_ROWS, LANES), F32), zcol, zcol)
            for kl in reversed(range(per)):
                carry = block(first + kl, carry, kl * BLOCK)
            acc, _, _ = _pairs_loop(first, lambda i, c: block(first - 1 - i, c, None), carry, SB_STEP_FWD)
            o_ref[rows, :] = acc.astype(BF16)
            return 0

        lax.fori_loop(0, s_dim // SB_ROWS, qloop, 0)
        end()

    def col(c0):
        return pl.BlockSpec((None, s_dim, LANES),lambda b, h: (b, 0, c0 + h))

    res = pl.pallas_call(
        body, name="sb_fwd", grid=(b_dim, PAIRS),
        in_specs=[col(COL_QB), col(COL_KB), col(COL_VB)] + [ANY] * n_w, out_specs=[col(0)] + [ANY] * n_w,
        out_shape=[jax.ShapeDtypeStruct((b_dim, s_dim, ATT_WIDTH), BF16)] + extra_shapes,
        scratch_shapes=extra_sems,
        compiler_params=_params(("arbitrary", "arbitrary")),
    )(proj3, proj3, proj3, *extra)
    return res[0], res[1:]


def _sb_bwd(proj3, do3, rider=None):
    b_dim, s_dim, _ = proj3.shape
    scale = HEAD_DIM ** -0.5
    per = SB_ROWS // BLOCK
    nkb_max = s_dim // BLOCK
    extra, extra_shapes, extra_sems = _rider_parts(rider)
    n_w = len(extra)

    def body(*refs):
        q_ref, k_ref, v_ref, do_ref = refs[:4]
        dq_ref, dk_ref, dv_ref = refs[4 + n_w:7 + n_w]
        dka, dva, e_ref, sg_ref = refs[7 + 2 * n_w:11 + 2 * n_w]
        step = pl.program_id(0) * PAIRS + pl.program_id(1)
        begin, end = _rider_hooks(rider, refs[4:4 + n_w], refs[7 + n_w:7 + 2 * n_w], refs[-2:], step, b_dim * PAIRS)
        begin()
        h1, h2 = _head_masks()
        after, before, qrow, kcol = _sb_consts()
        dka[...] = jnp.zeros_like(dka)
        dva[...] = jnp.zeros_like(dva)

        def qloop(qi, _):
            rows = pl.ds(pl.multiple_of(qi * SB_ROWS, SB_ROWS), SB_ROWS)
            q1, q2 = _split_heads(q_ref[rows, :] * (scale * LOG2E), h1, h2)
            do1, do2 = _split_heads(do_ref[rows, :].astype(F32), h1, h2)
            first = qi * per

            def pass1(kb, carry, lo):
                run1, run2 = carry
                krows = pl.ds(pl.multiple_of(kb * BLOCK, BLOCK), BLOCK)
                k16 = k_ref[krows, :].astype(BF16)
                v16 = v_ref[krows, :].astype(BF16)
                valid = None if lo is None else kcol[:SB_ROWS - lo] < qrow[:SB_ROWS - lo]
                lo = lo or 0
                part = pl.ds(lo, SB_ROWS - lo)

                def head(h, qh, doh, run):
                    ls, l1m = _sb_scores(qh[lo:], k16, valid)
                    a = jnp.exp2(ls + _split_dot(l1m, after) + run[lo:])
                    if valid is not None:
                        a = jnp.where(valid, a, 0.0)
                    da = lax.dot_general(doh[lo:], v16, NT, preferred_element_type=F32)
                    e_ref[h, kb, part, :] = a * da
                    sg_ref[h, kb, part, :] = jnp.exp2(ls)
                    return a.astype(BF16), _below(run, lo, jnp.sum(l1m, axis=-1, keepdims=True))

                a1, run1 = head(0, q1, do1, run1)
                a2, run2 = head(1, q2, do2, run2)
                dva[krows, :] += (lax.dot_general(a1, do1[lo:], TN, preferred_element_type=F32)
                                  + lax.dot_general(a2, do2[lo:], TN, preferred_element_type=F32))
                return run1, run2

            zcol = jnp.zeros((SB_ROWS, 1), F32)
            carry = (zcol, zcol)
            for kl in reversed(range(per)):
                carry = pass1(first + kl, carry, kl * BLOCK)
            _pairs_loop(first, lambda i, c: pass1(first - 1 - i, c, None), carry, SB_STEP_BWD)

            def pass2(kb, carry, lo):
                dq, pre1, pre2 = carry
                krows = pl.ds(pl.multiple_of(kb * BLOCK, BLOCK), BLOCK)
                k1, k2 = _split_heads(k_ref[krows, :], h1, h2)
                valid = None if lo is None else kcol[:SB_ROWS - lo] < qrow[:SB_ROWS - lo]
                lo = lo or 0
                part = pl.ds(lo, SB_ROWS - lo)

                def head(h, pre):
                    ev = e_ref[h, kb, part, :]
                    sg = sg_ref[h, kb, part, :]
                    dz = ev - (ev + _split_dot(ev, before) + pre[lo:]) * sg
                    if valid is not None:
                        dz = jnp.where(valid, dz, 0.0)
                    return dz.astype(BF16), _below(pre, lo, jnp.sum(ev, axis=-1, keepdims=True))

                dz1, pre1 = head(0, pre1)
                dz2, pre2 = head(1, pre2)
                dka[krows, :] += (lax.dot_general(dz1, q1[lo:], TN, preferred_element_type=F32)
                                  + lax.dot_general(dz2, q2[lo:], TN, preferred_element_type=F32))
                dq = _below(dq, lo, jnp.dot(dz1, k1, preferred_element_type=F32)
                            + jnp.dot(dz2, k2, preferred_element_type=F32))
                return dq, pre1, pre2

            carry = _pairs_loop(first, lambda i, c: pass2(i, c, None), (jnp.zeros((SB_ROWS, LANES), F32), zcol, zcol),
                                SB_STEP_BWD)
            for kl in range(per):
                carry = pass2(first + kl, carry, kl * BLOCK)
            dq = carry[0]
            dq_ref[rows, :] = (dq * scale).astype(BF16)
            return 0

        lax.fori_loop(0, s_dim // SB_ROWS, qloop, 0)
        dk_ref[...] = (dka[...] * LN2).astype(BF16)
        dv_ref[...] = dva[...].astype(BF16)
        end()

    def col(c0):
        return pl.BlockSpec((None, s_dim, LANES),lambda b, h: (b, 0, c0 + h))

    shp = jax.ShapeDtypeStruct((b_dim, s_dim, ATT_WIDTH), BF16)
    acc = pltpu.VMEM((s_dim, LANES), F32)
    strip = pltpu.VMEM((2, nkb_max, SB_ROWS, BLOCK), F32)
    res = pl.pallas_call(
        body, name="sb_bwd", grid=(b_dim, PAIRS),
        in_specs=[col(COL_QB), col(COL_KB), col(COL_VB), col(0)] + [ANY] * n_w,
        out_specs=[col(0), col(0), col(0)] + [ANY] * n_w,
        out_shape=[shp, shp, shp] + extra_shapes,
        scratch_shapes=[acc, acc, strip, strip] + extra_sems,
        compiler_params=_params(("arbitrary", "arbitrary")),
    )(proj3, proj3, proj3, do3, *extra)
    return res[:3], res[3:]


def _sigmoid(x):
    return 1.0 / (1.0 + jnp.exp(-x))


def _gate_out_norm(proj, ua, ub, w_out, x, g, *, tt=512):
    t_dim, d = ua.shape

    def body(ga_ref, gb_ref, ua_ref, ub_ref, w_ref, x_ref, g_ref, m_ref, h_ref, n_ref):
        mixed = (_sigmoid(ga_ref[...]) * ua_ref[...] + _sigmoid(gb_ref[...]) * ub_ref[...]).astype(BF16)
        m_ref[...] = mixed
        hv = x_ref[...] + jnp.dot(mixed, w_ref[...], preferred_element_type=F32)
        h_ref[...] = hv
        r = lax.rsqrt(jnp.mean(hv * hv, axis=-1, keepdims=True) + RMS_EPS)
        n_ref[...] = ((hv * r) * g_ref[...]).astype(BF16)

    row = pl.BlockSpec((tt, d), lambda i: (i, 0))
    return pl.pallas_call(
        body, name="gate_out_norm", grid=(t_dim // tt,),
        in_specs=[pl.BlockSpec((tt, d), lambda i: (i, 3)), pl.BlockSpec((tt, d), lambda i: (i, 4)), row, row,
                  pl.BlockSpec((d, d), lambda i: (0, 0)), row, pl.BlockSpec((1, d), lambda i: (0, 0))],
        out_specs=[row, row, row],
        out_shape=[jax.ShapeDtypeStruct((t_dim, d), BF16), jax.ShapeDtypeStruct((t_dim, d), F32),
                   jax.ShapeDtypeStruct((t_dim, d), BF16)],
        compiler_params=_params(("parallel",)),
    )(proj, proj, ua, ub, w_out, x, g)


def _out_dx_gate_bwd(dh, w_out, proj, ua, ub, *, tt=512):
    t_dim, d = ua.shape

    def body(dh_ref, w_ref, ga_ref, gb_ref, ua_ref, ub_ref, dua_ref, dub_ref, dg_ref):
        dm = lax.dot_general(dh_ref[...].astype(BF16), w_ref[...], NT, preferred_element_type=F32)
        sa = _sigmoid(ga_ref[...])
        sb = _sigmoid(gb_ref[...])
        dua_ref[...] = (dm * sa).astype(BF16)
        dub_ref[...] = (dm * sb).astype(BF16)
        dg_ref[:, :d] = (dm * ua_ref[...] * (sa * (1.0 - sa))).astype(BF16)
        dg_ref[:, d:] = (dm * ub_ref[...] * (sb * (1.0 - sb))).astype(BF16)

    row = pl.BlockSpec((tt, d), lambda i: (i, 0))
    wide = pl.BlockSpec((tt, 2 * d), lambda i: (i, 0))
    return pl.pallas_call(
        body, name="out_dx_gate_bwd", grid=(t_dim // tt,),
        in_specs=[row, pl.BlockSpec((d, d), lambda i: (0, 0)),
                  pl.BlockSpec((tt, d), lambda i: (i, 3)), pl.BlockSpec((tt, d), lambda i: (i, 4)), row, row],
        out_specs=[row, row, wide],
        out_shape=[jax.ShapeDtypeStruct((t_dim, d), BF16), jax.ShapeDtypeStruct((t_dim, d), BF16),
                   jax.ShapeDtypeStruct((t_dim, 2 * d), BF16)],
        compiler_params=_params(("parallel",)),
    )(dh, w_out, proj, proj, ua, ub)


def _ffn_up_swiglu(n, wg3, wu3, *, tt=512):
    t_dim, d = n.shape
    n_s, f4, _ = wg3.shape

    def body(n_ref, wg_ref, wu_ref, g_ref, u_ref, a_ref):
        nv = n_ref[...]
        for s in range(n_s):
            gv = lax.dot_general(nv, wg_ref[s], NT, preferred_element_type=F32)
            uv = lax.dot_general(nv, wu_ref[s], NT, preferred_element_type=F32)
            g_ref[s] = gv.astype(BF16)
            u_ref[s] = uv.astype(BF16)
            a_ref[s] = (gv * _sigmoid(gv) * uv).astype(BF16)

    wspec = pl.BlockSpec(wg3.shape, lambda i: (0, 0, 0))
    ospec = pl.BlockSpec((n_s, tt, f4), lambda i: (0, i, 0))
    shp = (n_s, t_dim, f4)
    return pl.pallas_call(
        body, name="ffn_up_swiglu", grid=(t_dim // tt,),
        in_specs=[pl.BlockSpec((tt, d), lambda i: (i, 0)), wspec, wspec], out_specs=[ospec, ospec, ospec],
        out_shape=[jax.ShapeDtypeStruct(shp, BF16)] * 3,
        compiler_params=_params(("parallel",)),
    )(n, wg3, wu3)


def _ffn_down_dx_swiglu(dh, wd3, g3, u3, *, tt=512):
    t_dim, d = dh.shape
    n_s, f4, _ = wd3.shape

    def body(dh_ref, w_ref, g_ref, u_ref, dg_ref, du_ref):
        dhv = dh_ref[...].astype(BF16)
        for s in range(n_s):
            da = lax.dot_general(dhv, w_ref[s], NT, preferred_element_type=F32)
            gv = g_ref[s].astype(F32)
            sg = _sigmoid(gv)
            dg_ref[s] = (da * u_ref[s].astype(F32) * (sg + gv * sg * (1.0 - sg))).astype(BF16)
            du_ref[s] = (da * (gv * sg)).astype(BF16)

    spec = pl.BlockSpec((n_s, tt, f4), lambda i: (0, i, 0))
    shp = jax.ShapeDtypeStruct((n_s, t_dim, f4), BF16)
    return pl.pallas_call(
        body, name="ffn_down_dx_swiglu", grid=(t_dim // tt,),
        in_specs=[pl.BlockSpec((tt, d), lambda i: (i, 0)), pl.BlockSpec(wd3.shape, lambda i: (0, 0, 0)), spec, spec],
        out_specs=[spec, spec], out_shape=[shp, shp],
        compiler_params=_params(("parallel",)),
    )(dh, wd3, g3, u3)


def _mem_fwd(qm, kvm, *, tt=2048):
    b_dim, s_dim, _ = qm.shape
    n_mem = kvm.shape[1]
    scale = MEM_HEAD_DIM ** -0.5

    def body(q_ref, k_ref, v_ref, o_ref):
        sc = lax.dot_general(q_ref[0], k_ref[0], NT, preferred_element_type=F32) * scale
        p = jnp.exp(sc - jnp.max(sc, axis=-1, keepdims=True))
        p = p / jnp.sum(p, axis=-1, keepdims=True)
        o_ref[0] = jnp.dot(p.astype(BF16), v_ref[0], preferred_element_type=F32).astype(BF16)

    qs = pl.BlockSpec((1, tt, MEM_HEAD_DIM), lambda b, h, i: (b, i, h))
    return pl.pallas_call(
        body, name="mem_fwd", grid=(b_dim, N_HEADS_MEM, s_dim // tt),
        in_specs=[qs, pl.BlockSpec((1, n_mem, MEM_HEAD_DIM), lambda b, h, i: (b, 0, h)),
                  pl.BlockSpec((1, n_mem, MEM_HEAD_DIM), lambda b, h, i: (b, 0, N_HEADS_MEM + h))],
        out_specs=qs, out_shape=jax.ShapeDtypeStruct(qm.shape, BF16),
        compiler_params=_params(("parallel", "parallel", "parallel")),
    )(qm, kvm, kvm)


def _mem_bwd(qm, kvm, dom, *, tt=2048):
    b_dim, s_dim, _ = qm.shape
    n_mem = kvm.shape[1]
    scale = MEM_HEAD_DIM ** -0.5

    def body(q_ref, k_ref, v_ref, do_ref, dq_ref, dk_ref, dv_ref):
        qv, kv, vv, dov = q_ref[0], k_ref[0], v_ref[0], do_ref[0]
        sc = lax.dot_general(qv, kv, NT, preferred_element_type=F32) * scale
        p = jnp.exp(sc - jnp.max(sc, axis=-1, keepdims=True))
        p = p / jnp.sum(p, axis=-1, keepdims=True)
        dp = lax.dot_general(dov, vv, NT, preferred_element_type=F32)
        ds = (p * (dp - jnp.sum(p * dp, axis=-1, keepdims=True)) * scale).astype(BF16)
        dq_ref[0] = jnp.dot(ds, kv, preferred_element_type=F32).astype(BF16)

        @pl.when(pl.program_id(2) == 0)
        def _():
            dk_ref[...] = jnp.zeros_like(dk_ref)
            dv_ref[...] = jnp.zeros_like(dv_ref)

        dk_ref[0] += lax.dot_general(ds, qv, TN, preferred_element_type=F32)
        dv_ref[0] += lax.dot_general(p.astype(BF16), dov, TN, preferred_element_type=F32)

    qs = pl.BlockSpec((1, tt, MEM_HEAD_DIM), lambda b, h, i: (b, i, h))
    ks = pl.BlockSpec((1, n_mem, MEM_HEAD_DIM), lambda b, h, i: (b, 0, h))
    vs = pl.BlockSpec((1, n_mem, MEM_HEAD_DIM), lambda b, h, i: (b, 0, N_HEADS_MEM + h))
    return pl.pallas_call(
        body, name="mem_bwd", grid=(b_dim, N_HEADS_MEM, s_dim // tt),
        in_specs=[qs, ks, vs, qs], out_specs=[qs, ks, ks],
        out_shape=[jax.ShapeDtypeStruct(qm.shape, BF16), jax.ShapeDtypeStruct((b_dim, n_mem, MEM_WIDTH), F32),
                   jax.ShapeDtypeStruct((b_dim, n_mem, MEM_WIDTH), F32)],
        compiler_params=_params(("parallel", "parallel", "arbitrary")),
    )(qm, kvm, kvm, dom)


def _adamw_math(wv, gv, mv, vv):
    nm = ADAM_B1 * mv + (1.0 - ADAM_B1) * gv
    nv = ADAM_B2 * vv + (1.0 - ADAM_B2) * (gv * gv)
    m_hat = nm / (1.0 - ADAM_B1 ** ADAM_STEP)
    v_hat = nv / (1.0 - ADAM_B2 ** ADAM_STEP)
    return -ADAM_LR * (m_hat / (jnp.sqrt(v_hat) + ADAM_EPS) + ADAM_WD * wv), nm, nv


def _adamw(w, g, m, v, *, name):
    rows, cols = w.shape
    tr = _tile(rows, 256, 8)

    def body(w_ref, g_ref, m_ref, v_ref, d_ref, nm_ref, nv_ref):
        d_ref[...], nm_ref[...], nv_ref[...] = _adamw_math(w_ref[...], g_ref[...], m_ref[...], v_ref[...])

    spec = pl.BlockSpec((tr, cols), lambda i: (i, 0))
    shp = jax.ShapeDtypeStruct((rows, cols), F32)
    return pl.pallas_call(
        body, name=name, grid=(rows // tr,),
        in_specs=[spec] * 4, out_specs=[spec] * 3, out_shape=[shp] * 3,
        compiler_params=_params(("parallel",)),
    )(w, g, m, v)


def _prefetch_spec(grid, in_specs, out_specs):
    return pltpu.PrefetchScalarGridSpec(num_scalar_prefetch=1, grid=grid, in_specs=in_specs, out_specs=out_specs)


def _adamw_halves(w, mine, theirs, m, v, c_idx, *, name):
    rows, cols = w.shape
    half = rows // 2
    tr = _tile(half, _row_cap(cols), 8)
    nh = half // tr

    def body(c_ref, w_ref, mine_ref, theirs_ref, m_ref, v_ref, g_ref, d_ref, nm_ref, nv_ref):
        gv = jnp.where(pl.program_id(0) == c_ref[0], mine_ref[...], theirs_ref[...])
        g_ref[...] = gv
        d_ref[...], nm_ref[...], nv_ref[...] = _adamw_math(w_ref[...], gv, m_ref[...], v_ref[...])

    full = pl.BlockSpec((tr, cols), lambda h, i, c_ref: (h * nh + i, 0))
    part = pl.BlockSpec((tr, cols), lambda h, i, c_ref: (i, 0))
    shp = jax.ShapeDtypeStruct((rows, cols), F32)
    return pl.pallas_call(
        body, name=name, grid_spec=_prefetch_spec((2, nh), [full, part, part, full, full], [full] * 4),
        out_shape=[shp] * 4,
        compiler_params=_params(("parallel", "parallel")),
    )(c_idx, w, mine, theirs, m, v)


def _pair_sum(g3, theirs, c_idx, *, name):
    n, rows, cols = g3.shape
    half = rows // 2
    tr = _tile(half, _row_cap(cols), 16)

    def body(c_ref, g_ref, t_ref, o_ref):
        o_ref[...] = (g_ref[...] + t_ref[...]).astype(BF16)

    part = pl.BlockSpec((None, tr, cols), lambda s, i, c_ref: (s, i, 0))
    return pl.pallas_call(
        body, name=name,
        grid_spec=_prefetch_spec((n, half // tr),
                                 [pl.BlockSpec((None, None, tr, cols), lambda s, i, c_ref: (s, c_ref[0], i, 0)), part],
                                 part),
        out_shape=jax.ShapeDtypeStruct((n, half, cols), BF16),
        compiler_params=_params(("parallel", "parallel")),
    )(c_idx, g3.reshape(n, 2, half, cols), theirs)


def _chip_sum(pair, recv, s_idx, *, name):
    _, half, cols = pair.shape
    tr = _tile(half, _row_cap(cols), 16)

    def body(s_ref, p_ref, r_ref, o_ref):
        o_ref[...] = ((p_ref[...].astype(F32) + r_ref[0].astype(F32)) + r_ref[1].astype(F32)) + r_ref[2].astype(F32)

    return pl.pallas_call(
        body, name=name,
        grid_spec=_prefetch_spec((half // tr,),
                                 [pl.BlockSpec((None, tr, cols), lambda i, s_ref: (s_ref[0], i, 0)),
                                  pl.BlockSpec((N_CHIPS - 1, tr, cols), lambda i, s_ref: (0, i, 0))],
                                 pl.BlockSpec((tr, cols), lambda i, s_ref: (i, 0))),
        out_shape=jax.ShapeDtypeStruct((half, cols), F32),
        compiler_params=_params(("parallel",)),
    )(s_idx, pair, recv)


def _sum8(parts):
    n, rows, cols = parts.shape

    def body(p_ref, o_ref):
        acc = p_ref[0]
        for i in range(1, n):
            acc = acc + p_ref[i]
        o_ref[...] = acc

    return pl.pallas_call(
        body, name="small_sum", grid=(1,),
        in_specs=[pl.BlockSpec((n, rows, cols), lambda i: (0, 0, 0))],
        out_specs=pl.BlockSpec((rows, cols), lambda i: (0, 0)),
        out_shape=jax.ShapeDtypeStruct((rows, cols), parts.dtype),
        compiler_params=_params(("arbitrary",)),
    )(parts)


def _place():
    return lax.axis_index("x"), lax.axis_index("y"), lax.axis_index("c")


ANY = pl.BlockSpec(memory_space=pl.ANY)


def _rider_parts(rider):
    if rider is None:
        return (), [], []
    kind, arrays = rider
    n = len(arrays)
    shapes = {"gather": _gathered_shapes, "pair": _pair_shapes, "chip": _chip_shapes}[kind](arrays)
    sems = _gather_sems(n) if kind == "gather" else _exchange_sems(n if kind == "pair" else 3 * n)
    return tuple(arrays), shapes, sems


def _rider_hooks(rider, ins, outs, sems, step, n_steps):
    if rider is None:
        return (lambda: None), (lambda: None)
    if rider[0] == "gather":
        start, forward, finish = _gather_steps(ins, outs, *sems)
    else:
        start, finish = {"pair": _pair_steps, "chip": _chip_steps}[rider[0]](ins, outs, *sems)
        forward = None

    def begin():
        pl.when(step == 0)(start)

    def end():
        if forward is not None:
            pl.when(step == n_steps - 2)(forward)
        pl.when(step == n_steps - 1)(finish)

    return begin, end


def _exchange_alone(rider, *, name):
    extra, shapes, sems = _rider_parts(rider)
    n = len(extra)

    def body(*refs):
        begin, end = _rider_hooks(rider, refs[:n], refs[n:2 * n], refs[-2:], jnp.int32(0), 1)
        begin()
        end()

    return pl.pallas_call(
        body, name=name, out_shape=shapes, in_specs=[ANY] * n, out_specs=[ANY] * n, scratch_shapes=sems,
    )(*extra)


def _gathered_shapes(shards):
    return [jax.ShapeDtypeStruct((N_CHIPS,) + s.shape, s.dtype) for s in shards]


def _gather_sems(n):
    return [pltpu.SemaphoreType.DMA((7 * n,)), pltpu.SemaphoreType.DMA((7 * n,))]


def _gather_steps(ins, outs, send_sems, recv_sems):
    n = len(ins)
    halves = [r.shape[0] // 2 for r in ins]
    x, y, c = _place()
    my_chip = 2 * x + y
    me, sibling = (x, y, c), (x, y, 1 - c)
    chips = [(1 - x, y), (x, 1 - y), (1 - x, 1 - y)]

    def half_of(w, chip, pc):
        return outs[w].at[chip, pl.ds(pc * halves[w], halves[w]), :]

    def copy(w, k, src, dst, to):
        return pltpu.make_async_remote_copy(
            src_ref=src, dst_ref=dst, send_sem=send_sems.at[7 * w + k], recv_sem=recv_sems.at[7 * w + k],
            device_id=to, device_id_type=MESH)

    def firsts():
        cps = []
        for w in range(n):
            cps.append(copy(w, 0, ins[w], outs[w].at[my_chip], sibling))
            mine = ins[w].at[pl.ds(c * halves[w], halves[w]), :]
            for j, (px, py) in enumerate(chips):
                cps.append(copy(w, 1 + j, mine, half_of(w, my_chip, c), (px, py, c)))
        return cps

    def passes():
        return [copy(w, 4 + j, half_of(w, 2 * px + py, c), half_of(w, 2 * px + py, c), sibling)
                for w in range(n) for j, (px, py) in enumerate(chips)]

    def start():
        for cp in firsts():
            cp.start()

    def forward():
        fws = passes()
        for w in range(n):
            for j, (px, py) in enumerate(chips):
                landed = half_of(w, 2 * px + py, c)
                copy(w, 1 + j, landed, landed, me).wait_recv()
                fws[3 * w + j].start()

    def finish():
        for w in range(n):
            copy(w, 0, ins[w], outs[w].at[my_chip], me).wait_recv()
            for j, (px, py) in enumerate(chips):
                landed = half_of(w, 2 * px + py, 1 - c)
                copy(w, 4 + j, landed, landed, me).wait_recv()
        for cp in firsts() + passes():
            cp.wait_send()

    return start, forward, finish


def _pair_shapes(grads):
    return [jax.ShapeDtypeStruct((g.shape[0], g.shape[1] // 2, g.shape[2]), g.dtype) for g in grads]


def _exchange_sems(n):
    return [pltpu.SemaphoreType.DMA((n,)), pltpu.SemaphoreType.DMA((n,))]


def _exchange_steps(copies):
    def start():
        for cp in copies():
            cp.start()

    def finish():
        for cp in copies():
            cp.wait()

    return start, finish


def _pair_steps(ins, outs, send_sems, recv_sems):
    x, y, c = _place()

    def copies():
        return [pltpu.make_async_remote_copy(
            src_ref=ins[w].at[:, pl.ds((1 - c) * (ins[w].shape[1] // 2), ins[w].shape[1] // 2), :], dst_ref=outs[w],
            send_sem=send_sems.at[w], recv_sem=recv_sems.at[w], device_id=(x, y, 1 - c), device_id_type=MESH)
            for w in range(len(ins))]

    return _exchange_steps(copies)


def _chip_shapes(pairs):
    return [jax.ShapeDtypeStruct((N_CHIPS - 1,) + p.shape[1:], p.dtype) for p in pairs]


def _chip_steps(ins, outs, send_sems, recv_sems):
    x, y, c = _place()
    others = [(1 - x, y), (x, 1 - y), (1 - x, 1 - y)]

    def copies():
        return [pltpu.make_async_remote_copy(
            src_ref=ins[w].at[2 * px + py], dst_ref=outs[w].at[j],
            send_sem=send_sems.at[3 * w + j], recv_sem=recv_sems.at[3 * w + j],
            device_id=(px, py, c), device_id_type=MESH)
            for w in range(len(ins)) for j, (px, py) in enumerate(others)]

    return _exchange_steps(copies)


def _swap_halves(mine):
    n = len(mine)

    def body(*refs):
        ins, outs, send_sems, recv_sems = refs[:n], refs[n:2 * n], refs[2 * n], refs[2 * n + 1]
        x, y, c = _place()
        copies = [pltpu.make_async_remote_copy(
            src_ref=ins[w], dst_ref=outs[w], send_sem=send_sems.at[w], recv_sem=recv_sems.at[w],
            device_id=(x, y, 1 - c), device_id_type=MESH) for w in range(n)]
        for cp in copies:
            cp.start()
        for cp in copies:
            cp.wait()

    return pl.pallas_call(
        body, name="grad_swap_halves",
        out_shape=[jax.ShapeDtypeStruct(h.shape, h.dtype) for h in mine],
        in_specs=[ANY] * n, out_specs=[ANY] * n,
        scratch_shapes=[pltpu.SemaphoreType.DMA((n,)), pltpu.SemaphoreType.DMA((n,))],
    )(*mine)


def _gather_small(small):
    srows, cols = small.shape

    def body(s_ref, all_ref, send_sems, recv_sems, local_sem):
        x, y, c = _place()
        me = 4 * x + 2 * y + c
        keep_small = pltpu.make_async_copy(s_ref, all_ref.at[me], local_sem)
        keep_small.start()
        sends = []
        for kk in range(1, 8):
            peer = (x ^ (kk >> 2), y ^ ((kk >> 1) & 1), c ^ (kk & 1))
            sends.append(pltpu.make_async_remote_copy(
                src_ref=s_ref, dst_ref=all_ref.at[me],
                send_sem=send_sems.at[kk], recv_sem=recv_sems.at[kk], device_id=peer, device_id_type=MESH))
        for cp in sends:
            cp.start()
        for kk in range(1, 8):
            px, py, pc = x ^ (kk >> 2), y ^ ((kk >> 1) & 1), c ^ (kk & 1)
            pltpu.make_async_remote_copy(
                src_ref=s_ref, dst_ref=all_ref.at[4 * px + 2 * py + pc],
                send_sem=send_sems.at[kk], recv_sem=recv_sems.at[kk], device_id=(px, py, pc),
                device_id_type=MESH).wait_recv()
        for cp in sends:
            cp.wait_send()
        keep_small.wait()

    return pl.pallas_call(
        body, name="gather_small",
        out_shape=jax.ShapeDtypeStruct((8, srows, cols), small.dtype),
        in_specs=[ANY], out_specs=ANY,
        scratch_shapes=[pltpu.SemaphoreType.DMA((8,)), pltpu.SemaphoreType.DMA((8,)), pltpu.SemaphoreType.DMA],
    )(small)


SHARDED = (("w_in", D_MODEL, IN_COLS, 1), ("w_up_a", ATT_WIDTH, D_MODEL, 1), ("w_up_b", ATT_WIDTH, D_MODEL, 1),
           ("w_out", D_MODEL, D_MODEL, 0), ("w_q_mem", D_MODEL, MEM_WIDTH, 0), ("w_kv_mem", D_MODEL, 2 * MEM_WIDTH, 0),
           ("w_o_mem", MEM_WIDTH, D_MODEL, 1), ("w_ffn_gate", D_FF, D_MODEL, 0), ("w_ffn_up", D_FF, D_MODEL, 0),
           ("w_ffn_down", D_FF, D_MODEL, 0))
TRANSPOSED = ("w_ffn_gate", "w_ffn_up")
NAMES = tuple(n for n, _, _, _ in SHARDED)


def _held(name, shard):
    return shard.T if name in TRANSPOSED else shard
EARLY, LATE = NAMES[:1], NAMES[1:]
GAINS = ("g_mix", "g_mem_q", "g_mem_kv", "g_ffn", "g_final")


def _natural(w3):
    n, r, c = w3.shape
    return w3.reshape(n * r, c)


def _shard_major(g, axis):
    if axis == 1:
        return g
    r, c = g.shape
    return g.reshape(N_CHIPS, r // N_CHIPS, c)


def kernel(x, mem, positions, g_mix, w_in, w_up_a, w_up_b, w_out, g_mem_q, g_mem_kv, w_q_mem, w_kv_mem, w_o_mem, g_ffn, w_ffn_gate, w_ffn_up, w_ffn_down, g_final, loss_target, m_g_mix, m_w_in, m_w_up_a, m_w_up_b, m_w_out, m_g_mem_q, m_g_mem_kv, m_w_q_mem, m_w_kv_mem, m_w_o_mem, m_g_ffn, m_w_ffn_gate, m_w_ffn_up, m_w_ffn_down, m_g_final, v_g_mix, v_w_in, v_w_up_a, v_w_up_b, v_w_out, v_g_mem_q, v_g_mem_kv, v_w_q_mem, v_w_kv_mem, v_w_o_mem, v_g_ffn, v_w_ffn_gate, v_w_ffn_up, v_w_ffn_down, v_g_final):
    given = dict(locals())
    shards = {n: _held(n, given[n][0]) for n in NAMES}

    early_shards = [shards[n].astype(BF16) for n in EARLY]
    late_shards = [shards[n].astype(BF16) for n in LATE]
    c_idx = lax.axis_index("c").astype(jnp.int32).reshape(1)
    s_idx = (2 * lax.axis_index("x") + lax.axis_index("y")).astype(jnp.int32).reshape(1)

    loss_row, grad_x, mine, gain_grads = _local_step(x, mem, positions, loss_target, g_mix, g_mem_q, g_mem_kv,
                                                     g_ffn, g_final, {}, early_shards, late_shards, (c_idx, s_idx))
    return _reduce_and_update(given, shards, loss_row, grad_x, mine, gain_grads, c_idx)


def _reduce_halves(glist, names, c_idx, s_idx, pair_exchange, chip_exchange):
    theirs = pair_exchange(glist)
    pairs = [_pair_sum(g, t, c_idx, name="pair_sum_" + n) for n, g, t in zip(names, glist, theirs)]
    recv = chip_exchange(pairs)
    return [_chip_sum(p, r, s_idx, name="chip_sum_" + n) for n, p, r in zip(names, pairs, recv)]


def _local_step(x, mem, positions, loss_target, g_mix, g_mem_q, g_mem_kv, g_ffn, g_final, wf,
                early_shards=None, late_shards=None, place=None):
    b_dim, s_dim, d = x.shape
    t_dim = b_dim * s_dim
    n_mem = mem.shape[1]
    wf = dict(wf)

    xb = x.reshape(t_dim, d)
    tgt = loss_target.reshape(t_dim, d)
    memf = mem.reshape(b_dim * n_mem, d)
    gfin = g_final.reshape(1, d)
    pos = positions.reshape(t_dim, 1).astype(F32)

    lane = jnp.arange(LANES) % HEAD_DIM
    half = ROPE_DIM // 2
    inv_freq = ROPE_THETA ** (-jnp.arange(half, dtype=F32) / half)
    inv_lane = jnp.where(lane < ROPE_DIM, inv_freq[lane % half], 0.0).reshape(1, -1).astype(F32)
    sel_a = (lane < half).astype(F32).reshape(1, -1)
    sel_b = ((lane >= half) & (lane < ROPE_DIM)).astype(F32).reshape(1, -1)

    def rows3(t):
        return t.reshape(b_dim, s_dim, t.shape[-1])

    def rows2(t):
        return t.reshape(t_dim, t.shape[-1])

    if early_shards:
        n1, gathered = _rms_fwd(xb, g_mix, name="rms_mix", rider=("gather", early_shards))
        wf.update(zip(EARLY, gathered))
    else:
        n1 = _rms_fwd(xb, g_mix, name="rms_mix")
    proj = _mm_cs(n1, wf["w_in"], name="mm_in")
    proj3 = rows3(proj)
    cs, sn = _rope_table(pos, inv_lane, sel_a, sel_b)
    cs3, sn3 = rows3(cs), rows3(sn)
    (oa16, oa32, lse_a), _ = _dil_fwd(proj3, cs3, sn3, sel_a, sel_b)
    ob16, gathered = _sb_fwd(proj3, ("gather", late_shards) if late_shards else None)
    wf.update(zip(LATE, gathered))
    w_out, w_q, w_kv = _natural(wf["w_out"]), _natural(wf["w_q_mem"]), _natural(wf["w_kv_mem"])
    oa, ob = rows2(oa16), rows2(ob16)
    ua = _mm_sm(oa, wf["w_up_a"], name="mm_up_a", out_dtype=BF16)
    ub = _mm_sm(ob, wf["w_up_b"], name="mm_up_b", out_dtype=BF16)
    mixed, h1, hn = _gate_out_norm(proj, ua, ub, w_out, xb, g_mem_q)

    memn = _rms_fwd(memf, g_mem_kv, name="rms_mem_kv")
    qm = _mm(hn, w_q, name="mm_q_mem", out_dtype=BF16)
    kvm = _mm(memn, w_kv, name="mm_kv_mem", out_dtype=BF16)
    qm3, kvm3 = rows3(qm), kvm.reshape(b_dim, n_mem, 2 * MEM_WIDTH)
    om = rows2(_mem_fwd(qm3, kvm3))
    h2 = _mm_sm(om, wf["w_o_mem"], name="mm_o_mem", add=h1)

    n3 = _rms_fwd(h2, g_ffn, name="rms_ffn")
    gate3, up3, act3 = _ffn_up_swiglu(n3, wf["w_ffn_gate"], wf["w_ffn_up"])
    loss_row, dh3, dg_final = _down_final(act3, wf["w_ffn_down"], h2, gfin, tgt)

    grads = {}
    grads["w_ffn_down"] = _mm_ffn_down_dw(act3, dh3, name="mm_down_dw")
    dgate3, dup3 = _ffn_down_dx_swiglu(dh3, wf["w_ffn_down"], gate3, up3)
    grads["w_ffn_gate"] = _mm_ffn_down_dw(dgate3, n3, name="mm_gate_dw")
    grads["w_ffn_up"] = _mm_ffn_down_dw(dup3, n3, name="mm_up_dw")
    dn3 = _ffn_up_dx(dgate3, dup3, wf["w_ffn_gate"], wf["w_ffn_up"])
    dh2, dg_ffn = _rms_bwd(h2, g_ffn, dn3, dh3, name="rms_ffn_bwd")

    dom = _mm_sm_dx(dh2, wf["w_o_mem"], name="mm_o_mem_dx", out_dtype=BF16)
    grads["w_o_mem"] = _mm_sm_dw(om, dh2, name="mm_o_mem_dw")
    dqm, dkm, dvm = _mem_bwd(qm3, kvm3, rows3(dom))
    dqm = rows2(dqm)
    dkvm = jnp.concatenate([dkm, dvm], axis=-1).reshape(b_dim * n_mem, 2 * MEM_WIDTH).astype(BF16)
    grads["w_q_mem"] = _shard_major(_mm(hn, dqm, name="mm_q_mem_dw", ta=True), 0)
    dhn = _mm(dqm, w_q, name="mm_q_mem_dx", tb=True, out_dtype=BF16)
    grads["w_kv_mem"] = _shard_major(_mm(memn, dkvm, name="mm_kv_mem_dw", ta=True), 0)
    dmemn = _mm(dkvm, w_kv, name="mm_kv_mem_dx", tb=True)
    _, dg_mem_kv = _rms_bwd(memf, g_mem_kv, dmemn, None, name="rms_mem_kv_bwd")
    dh1, dg_mem_q = _rms_bwd(h1, g_mem_q, dhn, dh2, name="rms_mem_q_bwd")

    grads["w_out"] = _shard_major(_mm(mixed, dh1, name="mm_out_dw", ta=True), 0)
    dua, dub, dgates = _out_dx_gate_bwd(dh1, w_out, proj, ua, ub)
    doa = _mm_sm_dx(dua, wf["w_up_a"], name="mm_up_a_dx")
    grads["w_up_a"] = _mm_sm_dw(oa, dua, name="mm_up_a_dw")
    dob = _mm_sm_dx(dub, wf["w_up_b"], name="mm_up_b_dx", out_dtype=BF16)
    grads["w_up_b"] = _mm_sm_dw(ob, dub, name="mm_up_b_dw")

    att = {}

    def dil_with_pairs(glist):
        att["a"], theirs = _dil_bwd(proj3, cs3, sn3, sel_a, sel_b, rows3(doa), oa32, lse_a,
                                    ("pair", glist) if glist else None)
        return theirs

    def sb_with_chips(pairs):
        att["b"], recv = _sb_bwd(proj3, rows3(dob), ("chip", pairs) if pairs else None)
        return recv

    if place is None:
        dil_with_pairs(())
        sb_with_chips(())
    else:
        mine_late = _reduce_halves([grads[n] for n in LATE], LATE, *place, dil_with_pairs, sb_with_chips)
    dproj = jnp.concatenate([rows2(t) for t in att["a"] + att["b"]] + [dgates], axis=1)
    grads["w_in"] = _mm_cs_dw(n1, dproj, name="mm_in_dw")
    if place is None:
        dn1 = _mm_cs_dx(dproj, wf["w_in"], name="mm_in_dx", out_dtype=BF16)
        dx, dg_mix = _rms_bwd(xb, g_mix, dn1, dh1, name="rms_mix_bwd")
    else:
        tail = {}

        def pair_alone(glist):
            return _exchange_alone(("pair", glist), name="grad_pair_exchange")

        def dx_with_chips(pairs):
            tail["dn1"], recv = _mm_cs_dx(dproj, wf["w_in"], name="mm_in_dx", out_dtype=BF16, rider=("chip", pairs))
            return recv

        mine_early = _reduce_halves([grads[n] for n in EARLY], EARLY, *place, pair_alone, dx_with_chips)
        dx, dg_mix = _rms_bwd(xb, g_mix, tail["dn1"], dh1, name="rms_mix_bwd")
    grad_x = dx.reshape(b_dim, s_dim, d)
    gains = (dg_mix, dg_mem_q, dg_mem_kv, dg_ffn, dg_final)
    if place is None:
        return loss_row, grad_x, grads, gains
    return loss_row, grad_x, mine_early + mine_late, gains


def _reduce_and_update(given, shards, loss_row, grad_x, mine, gain_grads, c_idx):
    d = D_MODEL
    dg_mix, dg_mem_q, dg_mem_kv, dg_ffn, dg_final = gain_grads
    small = jnp.concatenate([dg_mix, dg_mem_q, dg_mem_kv, dg_ffn, dg_final,
                             jnp.pad(loss_row, ((0, 0), (0, FLAT_COLS - LANES))), jnp.zeros((2, FLAT_COLS), F32)], axis=0)
    small_all = _gather_small(small)
    others = _swap_halves(mine)
    small_sum = _sum8(small_all)
    loss = small_sum[5, 0]

    out_g, out_d, out_m, out_v = {}, {}, {}, {}
    for n, mine_n, other_n in zip(NAMES, mine, others):
        res = _adamw_halves(shards[n], mine_n, other_n, _held(n, given["m_" + n][0]), _held(n, given["v_" + n][0]),
                            c_idx, name="adamw_" + n)
        out_g[n], out_d[n], out_m[n], out_v[n] = [_held(n, r)[None] for r in res]
    gain_w = jnp.concatenate([given[n].reshape(1, d) for n in GAINS], axis=0)
    gain_m = jnp.concatenate([given["m_" + n].reshape(1, d) for n in GAINS], axis=0)
    gain_v = jnp.concatenate([given["v_" + n].reshape(1, d) for n in GAINS], axis=0)
    gain_g = small_sum[:len(GAINS)]
    gd, gm, gv = _adamw(gain_w, gain_g, gain_m, gain_v, name="adamw_gains")
    for i, n in enumerate(GAINS):
        shape = given[n].shape
        out_g[n], out_d[n] = gain_g[i].reshape(shape), gd[i].reshape(shape)
        out_m[n], out_v[n] = gm[i].reshape(shape), gv[i].reshape(shape)

    order = ["g_mix", "w_in", "w_up_a", "w_up_b", "w_out", "g_mem_q", "g_mem_kv", "w_q_mem", "w_kv_mem", "w_o_mem",
             "g_ffn", "w_ffn_gate", "w_ffn_up", "w_ffn_down", "g_final"]
    return (loss, grad_x, *[out_g[n] for n in order], *[out_d[n] for n in order],
            *[out_m[n] for n in order], *[out_v[n] for n in order])
```
